```python
import jax, jax.numpy as jnp
from jax import lax
import numpy as np

D_MODEL = 1024
BATCH = 8
SEQ = 4096
DEPTH = 1

CHUNK = 64
D_MIX = D_MODEL
DN_HEAD_DIM = 128
DN_WIDTH = D_MIX // 2
DN_HEADS = DN_WIDTH // DN_HEAD_DIM
CONV_K = 4
SG_WIDTH = D_MIX - DN_WIDTH
SG_GROUPS = 4
SG_DIM = SG_WIDTH // SG_GROUPS
SG_BLOCK = 128
D_FF = 2816
FFN_CONV = 3
EPS = 1e-6
PROJ_COLS = 4 * DN_WIDTH + 2 * SG_WIDTH + 2 * DN_HEADS

kernel_name = "hybrid_gdn_gmlp_convffn_block"


def rmsnorm(x, g):
    xf = x.astype(jnp.float32)
    y = xf * lax.rsqrt(jnp.mean(xf * xf, axis=-1, keepdims=True) + EPS)
    return (y * g.astype(jnp.float32)).astype(x.dtype)


def l2norm(x):
    return x * lax.rsqrt(jnp.sum(x * x, axis=-1, keepdims=True) + EPS)


def causal_dwconv(x, w):
    K = w.shape[0]
    T = x.shape[1]
    xp = jnp.pad(x, ((0, 0), (K - 1, 0), (0, 0)))
    out = xp[:, 0:T] * w[0]
    for k in range(1, K):
        out = out + xp[:, k:k + T] * w[k]
    return out


def gated_delta_chunked(q, k, v, g, beta):
    B, T, H, D = q.shape
    N = T // CHUNK
    C = CHUNK
    ch = lambda a: a.reshape(B, N, C, H, D).transpose(0, 3, 1, 2, 4)
    q, k, v = ch(q), ch(k), ch(v)
    g = g.reshape(B, N, C, H).transpose(0, 3, 1, 2)
    beta = beta.reshape(B, N, C, H).transpose(0, 3, 1, 2)
    Gc = jnp.cumsum(g, axis=-1)
    incl = jnp.tril(jnp.ones((C, C), dtype=bool))
    strict = jnp.tril(jnp.ones((C, C), dtype=bool), -1)
    decay = jnp.exp(jnp.where(incl, Gc[..., :, None] - Gc[..., None, :], -jnp.inf))
    k_beta = k * beta[..., None]
    v_beta = v * beta[..., None]
    L = jnp.where(strict, jnp.einsum('bhnid,bhnjd->bhnij', k_beta, k) * decay, 0.0)
    eye = jnp.eye(C, dtype=jnp.float32)
    rhs = jnp.concatenate([v_beta, k_beta * jnp.exp(Gc)[..., None]], axis=-1)
    sol = lax.linalg.triangular_solve(eye + L, rhs, left_side=True, lower=True,
                                      transpose_a=False, conjugate_a=False, unit_diagonal=True)
    value, k_cumdecay = sol[..., :D], sol[..., D:]
    attn_intra = jnp.einsum('bhnid,bhnjd->bhnij', q, k) * decay
    q_decay = q * jnp.exp(Gc)[..., None]
    k_tail = k * jnp.exp(Gc[..., -1:] - Gc)[..., None]
    chunk_decay = jnp.exp(Gc[..., -1])

    def step(S, inp):
        a_i, val_i, kcd_i, qd_i, kt_i, cd_i = inp
        v_new = val_i - jnp.einsum('bhcd,bhde->bhce', kcd_i, S)
        o = jnp.einsum('bhcd,bhde->bhce', qd_i, S) + jnp.einsum('bhij,bhje->bhie', a_i, v_new)
        S = S * cd_i[..., None, None] + jnp.einsum('bhcd,bhce->bhde', kt_i, v_new)
        return S, o

    mv = lambda a: jnp.moveaxis(a, 2, 0)
    S0 = jnp.zeros((B, H, D, D), jnp.float32)
    _, o = lax.scan(step, S0, (mv(attn_intra), mv(value), mv(k_cumdecay), mv(q_decay),
                               mv(k_tail), mv(chunk_decay)))
    return o.transpose(1, 0, 3, 2, 4).reshape(B, T, H, D)


def deltanet_group(q_raw, k_raw, v_raw, gate, b_raw, a_raw, conv_w, a_log, dt_bias, norm_g):
    B, T, _ = q_raw.shape
    qkv = jax.nn.silu(causal_dwconv(jnp.concatenate([q_raw, k_raw, v_raw], axis=-1), conv_w))
    qkv = qkv.astype(jnp.float32).reshape(B, T, 3, DN_HEADS, DN_HEAD_DIM)
    q = l2norm(qkv[:, :, 0]) * (DN_HEAD_DIM ** -0.5)
    k = l2norm(qkv[:, :, 1])
    v = qkv[:, :, 2]
    beta = jax.nn.sigmoid(b_raw.astype(jnp.float32))
    g = -jnp.exp(a_log.astype(jnp.float32)) * jax.nn.softplus(a_raw.astype(jnp.float32) + dt_bias.astype(jnp.float32))
    o = gated_delta_chunked(q, k, v, g, beta)
    o = rmsnorm(o, norm_g).reshape(B, T, DN_WIDTH)
    return (o * jax.nn.silu(gate.astype(jnp.float32))).astype(q_raw.dtype)


def gmlp_group(u_raw, v_raw, norm_g, w_s, b_s):
    B, T, _ = u_raw.shape
    u = jax.nn.gelu(u_raw)
    v = jax.nn.gelu(v_raw).reshape(B, T, SG_GROUPS, SG_DIM)
    v = rmsnorm(v, norm_g.reshape(SG_GROUPS, SG_DIM))
    v = v.reshape(B, T // SG_BLOCK, SG_BLOCK, SG_GROUPS, SG_DIM)
    pos_chunk = jnp.arange(SG_BLOCK) // CHUNK
    mask = pos_chunk[None, :] <= pos_chunk[:, None]
    w_m = jnp.where(mask, w_s, 0.0).astype(v.dtype)
    s = jnp.einsum('gij,bnjgd->bnigd', w_m, v) + b_s.T[None, None, :, :, None]
    return u * s.reshape(B, T, SG_WIDTH)


def _fwd_setup_inputs(seed: int = 0) -> dict:
    key = jax.random.key(seed)
    ks = jax.random.split(key, 20)
    nrm = lambda k, shape, s: jax.random.normal(k, shape, jnp.float32) * s
    dt = jnp.exp(jax.random.uniform(ks[5], (DEPTH, DN_HEADS), jnp.float32,
                                    np.log(1e-3), np.log(1e-1)))
    return {
        "x": nrm(ks[0], (BATCH, SEQ, D_MODEL), 1.0),
        "attn_norm_g": 1.0 + nrm(ks[1], (DEPTH, D_MODEL), 0.02),
        "w_in": nrm(ks[2], (DEPTH, D_MODEL, PROJ_COLS), D_MODEL ** -0.5),
        "dn_conv_w": nrm(ks[3], (DEPTH, CONV_K, 3 * DN_WIDTH), CONV_K ** -0.5),
        "dn_a_log": jnp.log(jax.random.uniform(ks[4], (DEPTH, DN_HEADS), jnp.float32, 1.0, 16.0)),
        "dn_dt_bias": dt + jnp.log(-jnp.expm1(-dt)),
        "dn_out_norm_g": 1.0 + nrm(ks[6], (DEPTH, DN_HEAD_DIM), 0.02),
        "sg_norm_g": 1.0 + nrm(ks[7], (DEPTH, SG_WIDTH), 0.02),
        "sg_w": nrm(ks[8], (DEPTH, SG_GROUPS, SG_BLOCK, SG_BLOCK), SG_BLOCK ** -0.5),
        "sg_b": 1.0 + nrm(ks[9], (DEPTH, SG_GROUPS, SG_BLOCK), 0.01),
        "w_out": nrm(ks[10], (DEPTH, D_MIX, D_MODEL), D_MIX ** -0.5),
        "ffn_norm_g": 1.0 + nrm(ks[11], (DEPTH, D_MODEL), 0.02),
        "w_up": nrm(ks[12], (DEPTH, D_MODEL, 2 * D_FF), D_MODEL ** -0.5),
        "ffn_conv_w": nrm(ks[13], (DEPTH, FFN_CONV, 2 * D_FF), FFN_CONV ** -0.5),
        "ffn_conv_b": nrm(ks[14], (DEPTH, 2 * D_FF), 0.01),
        "w_down": nrm(ks[15], (DEPTH, D_FF, D_MODEL), D_FF ** -0.5),
        "final_norm_g": 1.0 + nrm(ks[16], (D_MODEL,), 0.02),
    }


def _fwd_reference(x, attn_norm_g, w_in, dn_conv_w, dn_a_log, dn_dt_bias, dn_out_norm_g,
              sg_norm_g, sg_w, sg_b, w_out, ffn_norm_g, w_up, ffn_conv_w, ffn_conv_b,
              w_down, final_norm_g):
    split_at = np.cumsum([DN_WIDTH] * 4 + [SG_WIDTH] * 2 + [DN_HEADS]).tolist()
    for l in range(DEPTH):
        h = rmsnorm(x, attn_norm_g[l])
        p = jnp.einsum('btd,dc->btc', h, w_in[l])
        q_raw, k_raw, v_raw, gate, u_raw, vg_raw, b_raw, a_raw = jnp.split(p, split_at, axis=-1)
        o_dn = deltanet_group(q_raw, k_raw, v_raw, gate, b_raw, a_raw, dn_conv_w[l],
                              dn_a_log[l], dn_dt_bias[l], dn_out_norm_g[l])
        o_sg = gmlp_group(u_raw, vg_raw, sg_norm_g[l], sg_w[l], sg_b[l])
        mix = jnp.concatenate([o_dn, o_sg], axis=-1)
        x = x + jnp.einsum('btc,cd->btd', mix, w_out[l])
        h = rmsnorm(x, ffn_norm_g[l])
        up = jnp.einsum('btd,df->btf', h, w_up[l])
        up = causal_dwconv(up, ffn_conv_w[l]) + ffn_conv_b[l]
        g_ff, v_ff = jnp.split(up, 2, axis=-1)
        x = x + jnp.einsum('btf,fd->btd', jax.nn.silu(g_ff) * v_ff, w_down[l])
    return rmsnorm(x, final_norm_g)


import jax as _jax
import jax.numpy as _jnp

TWIN_FORMAT = 'train_step'
FWD_PARAMS = ['x', 'attn_norm_g', 'w_in', 'dn_conv_w', 'dn_a_log', 'dn_dt_bias', 'dn_out_norm_g', 'sg_norm_g', 'sg_w', 'sg_b', 'w_out', 'ffn_norm_g', 'w_up', 'ffn_conv_w', 'ffn_conv_b', 'w_down', 'final_norm_g']
TWIN_WEIGHTS = ['attn_norm_g', 'w_in', 'dn_conv_w', 'dn_a_log', 'dn_dt_bias', 'dn_out_norm_g', 'sg_norm_g', 'sg_w', 'sg_b', 'w_out', 'ffn_norm_g', 'w_up', 'ffn_conv_w', 'ffn_conv_b', 'w_down', 'final_norm_g']
TWIN_DIFF_INPUT = 'x'
TWIN_INPUTS = ['x', 'attn_norm_g', 'w_in', 'dn_conv_w', 'dn_a_log', 'dn_dt_bias', 'dn_out_norm_g', 'sg_norm_g', 'sg_w', 'sg_b', 'w_out', 'ffn_norm_g', 'w_up', 'ffn_conv_w', 'ffn_conv_b', 'w_down', 'final_norm_g', 'loss_target', 'm_attn_norm_g', 'm_w_in', 'm_dn_conv_w', 'm_dn_a_log', 'm_dn_dt_bias', 'm_dn_out_norm_g', 'm_sg_norm_g', 'm_sg_w', 'm_sg_b', 'm_w_out', 'm_ffn_norm_g', 'm_w_up', 'm_ffn_conv_w', 'm_ffn_conv_b', 'm_w_down', 'm_final_norm_g', 'v_attn_norm_g', 'v_w_in', 'v_dn_conv_w', 'v_dn_a_log', 'v_dn_dt_bias', 'v_dn_out_norm_g', 'v_sg_norm_g', 'v_sg_w', 'v_sg_b', 'v_w_out', 'v_ffn_norm_g', 'v_w_up', 'v_ffn_conv_w', 'v_ffn_conv_b', 'v_w_down', 'v_final_norm_g']
TWIN_OUTPUTS = ['loss', 'grad_x', 'grad_attn_norm_g', 'grad_w_in', 'grad_dn_conv_w', 'grad_dn_a_log', 'grad_dn_dt_bias', 'grad_dn_out_norm_g', 'grad_sg_norm_g', 'grad_sg_w', 'grad_sg_b', 'grad_w_out', 'grad_ffn_norm_g', 'grad_w_up', 'grad_ffn_conv_w', 'grad_ffn_conv_b', 'grad_w_down', 'grad_final_norm_g', 'delta_attn_norm_g', 'delta_w_in', 'delta_dn_conv_w', 'delta_dn_a_log', 'delta_dn_dt_bias', 'delta_dn_out_norm_g', 'delta_sg_norm_g', 'delta_sg_w', 'delta_sg_b', 'delta_w_out', 'delta_ffn_norm_g', 'delta_w_up', 'delta_ffn_conv_w', 'delta_ffn_conv_b', 'delta_w_down', 'delta_final_norm_g', 'new_m_attn_norm_g', 'new_m_w_in', 'new_m_dn_conv_w', 'new_m_dn_a_log', 'new_m_dn_dt_bias', 'new_m_dn_out_norm_g', 'new_m_sg_norm_g', 'new_m_sg_w', 'new_m_sg_b', 'new_m_w_out', 'new_m_ffn_norm_g', 'new_m_w_up', 'new_m_ffn_conv_w', 'new_m_ffn_conv_b', 'new_m_w_down', 'new_m_final_norm_g', 'new_v_attn_norm_g', 'new_v_w_in', 'new_v_dn_conv_w', 'new_v_dn_a_log', 'new_v_dn_dt_bias', 'new_v_dn_out_norm_g', 'new_v_sg_norm_g', 'new_v_sg_w', 'new_v_sg_b', 'new_v_w_out', 'new_v_ffn_norm_g', 'new_v_w_up', 'new_v_ffn_conv_w', 'new_v_ffn_conv_b', 'new_v_w_down', 'new_v_final_norm_g']
TWIN_LEAF_KINDS = {'loss': 'loss', 'grad_x': 'grad_x', 'grad_attn_norm_g': 'grad_w', 'grad_w_in': 'grad_w', 'grad_dn_conv_w': 'grad_w', 'grad_dn_a_log': 'grad_w', 'grad_dn_dt_bias': 'grad_w', 'grad_dn_out_norm_g': 'grad_w', 'grad_sg_norm_g': 'grad_w', 'grad_sg_w': 'grad_w', 'grad_sg_b': 'grad_w', 'grad_w_out': 'grad_w', 'grad_ffn_norm_g': 'grad_w', 'grad_w_up': 'grad_w', 'grad_ffn_conv_w': 'grad_w', 'grad_ffn_conv_b': 'grad_w', 'grad_w_down': 'grad_w', 'grad_final_norm_g': 'grad_w', 'delta_attn_norm_g': 'delta_w', 'delta_w_in': 'delta_w', 'delta_dn_conv_w': 'delta_w', 'delta_dn_a_log': 'delta_w', 'delta_dn_dt_bias': 'delta_w', 'delta_dn_out_norm_g': 'delta_w', 'delta_sg_norm_g': 'delta_w', 'delta_sg_w': 'delta_w', 'delta_sg_b': 'delta_w', 'delta_w_out': 'delta_w', 'delta_ffn_norm_g': 'delta_w', 'delta_w_up': 'delta_w', 'delta_ffn_conv_w': 'delta_w', 'delta_ffn_conv_b': 'delta_w', 'delta_w_down': 'delta_w', 'delta_final_norm_g': 'delta_w', 'new_m_attn_norm_g': 'new_m', 'new_m_w_in': 'new_m', 'new_m_dn_conv_w': 'new_m', 'new_m_dn_a_log': 'new_m', 'new_m_dn_dt_bias': 'new_m', 'new_m_dn_out_norm_g': 'new_m', 'new_m_sg_norm_g': 'new_m', 'new_m_sg_w': 'new_m', 'new_m_sg_b': 'new_m', 'new_m_w_out': 'new_m', 'new_m_ffn_norm_g': 'new_m', 'new_m_w_up': 'new_m', 'new_m_ffn_conv_w': 'new_m', 'new_m_ffn_conv_b': 'new_m', 'new_m_w_down': 'new_m', 'new_m_final_norm_g': 'new_m', 'new_v_attn_norm_g': 'new_v', 'new_v_w_in': 'new_v', 'new_v_dn_conv_w': 'new_v', 'new_v_dn_a_log': 'new_v', 'new_v_dn_dt_bias': 'new_v', 'new_v_dn_out_norm_g': 'new_v', 'new_v_sg_norm_g': 'new_v', 'new_v_sg_w': 'new_v', 'new_v_sg_b': 'new_v', 'new_v_w_out': 'new_v', 'new_v_ffn_norm_g': 'new_v', 'new_v_w_up': 'new_v', 'new_v_ffn_conv_w': 'new_v', 'new_v_ffn_conv_b': 'new_v', 'new_v_w_down': 'new_v', 'new_v_final_norm_g': 'new_v'}


def _forward(args):
    return _fwd_reference(*[args[k] for k in FWD_PARAMS])


def _output_shape():
    def fwd():
        inp = _fwd_setup_inputs(0)
        return _fwd_reference(*[inp[k] for k in FWD_PARAMS])
    out = _jax.eval_shape(fwd)
    return out.shape, out.dtype

N_MICROBATCH = 1
ADAM_LR = 0.001
ADAM_B1 = 0.9
ADAM_B2 = 0.999
ADAM_EPS = 1e-08
ADAM_WD = 0.01
ADAM_STEP = 10
PER_EXAMPLE_BATCH_AXIS = {'x': 0, 'loss_target': 0}
SHARED_INPUTS = []
_WEIGHT_DTYPES = {'attn_norm_g': _jnp.float32, 'w_in': _jnp.float32, 'dn_conv_w': _jnp.float32, 'dn_a_log': _jnp.float32, 'dn_dt_bias': _jnp.float32, 'dn_out_norm_g': _jnp.float32, 'sg_norm_g': _jnp.float32, 'sg_w': _jnp.float32, 'sg_b': _jnp.float32, 'w_out': _jnp.float32, 'ffn_norm_g': _jnp.float32, 'w_up': _jnp.float32, 'ffn_conv_w': _jnp.float32, 'ffn_conv_b': _jnp.float32, 'w_down': _jnp.float32, 'final_norm_g': _jnp.float32}
MOMENT_SCALE = {'attn_norm_g': 1.652387e-01, 'w_in': 9.344056e-02, 'dn_conv_w': 6.937990e-02, 'dn_a_log': 5.916385e-01, 'dn_dt_bias': 5.671997e-01, 'dn_out_norm_g': 1.882234e-01, 'sg_norm_g': 9.428583e-02, 'sg_w': 8.824608e-02, 'sg_b': 1.007074e-01, 'w_out': 1.181142e-01, 'ffn_norm_g': 1.186214e-01, 'w_up': 4.916990e-02, 'ffn_conv_w': 5.033130e-02, 'ffn_conv_b': 5.101650e-02, 'w_down': 8.011413e-02, 'final_norm_g': 3.203276e+01}


def _to_microbatches(a, axis):
    t = _jnp.moveaxis(a, axis, 0)
    t = t.reshape((N_MICROBATCH, t.shape[0] // N_MICROBATCH) + t.shape[1:])
    return _jnp.moveaxis(t, 1, axis + 1)


def setup_inputs(seed: int = 0) -> dict:
    inp = _fwd_setup_inputs(seed)
    key = _jax.random.fold_in(_jax.random.key(seed), 7919)
    shape, _ = _output_shape()
    out = dict(inp)
    out["loss_target"] = _jax.random.normal(_jax.random.fold_in(key, 0), shape, _jnp.float32)
    for i, name in enumerate(TWIN_WEIGHTS):
        w = inp[name].astype(_jnp.float32)
        if MOMENT_SCALE is None:
            s = _jnp.sqrt(_jnp.mean(_jnp.square(w)) + 1e-30)
        else:
            s = MOMENT_SCALE[name]
        km, kv = _jax.random.split(_jax.random.fold_in(key, i + 1))
        out[name] = w
        out["m_" + name] = s * _jax.random.normal(km, w.shape, _jnp.float32)
        out["v_" + name] = (s * s) * _jax.random.uniform(kv, w.shape, _jnp.float32, 0.5, 1.5)
    if N_MICROBATCH > 1:
        for name, axis in PER_EXAMPLE_BATCH_AXIS.items():
            out[name] = _to_microbatches(out[name], axis)
    return {'x': out['x'], 'attn_norm_g': out['attn_norm_g'], 'w_in': out['w_in'], 'dn_conv_w': out['dn_conv_w'], 'dn_a_log': out['dn_a_log'], 'dn_dt_bias': out['dn_dt_bias'], 'dn_out_norm_g': out['dn_out_norm_g'], 'sg_norm_g': out['sg_norm_g'], 'sg_w': out['sg_w'], 'sg_b': out['sg_b'], 'w_out': out['w_out'], 'ffn_norm_g': out['ffn_norm_g'], 'w_up': out['w_up'], 'ffn_conv_w': out['ffn_conv_w'], 'ffn_conv_b': out['ffn_conv_b'], 'w_down': out['w_down'], 'final_norm_g': out['final_norm_g'], 'loss_target': out['loss_target'], 'm_attn_norm_g': out['m_attn_norm_g'], 'm_w_in': out['m_w_in'], 'm_dn_conv_w': out['m_dn_conv_w'], 'm_dn_a_log': out['m_dn_a_log'], 'm_dn_dt_bias': out['m_dn_dt_bias'], 'm_dn_out_norm_g': out['m_dn_out_norm_g'], 'm_sg_norm_g': out['m_sg_norm_g'], 'm_sg_w': out['m_sg_w'], 'm_sg_b': out['m_sg_b'], 'm_w_out': out['m_w_out'], 'm_ffn_norm_g': out['m_ffn_norm_g'], 'm_w_up': out['m_w_up'], 'm_ffn_conv_w': out['m_ffn_conv_w'], 'm_ffn_conv_b': out['m_ffn_conv_b'], 'm_w_down': out['m_w_down'], 'm_final_norm_g': out['m_final_norm_g'], 'v_attn_norm_g': out['v_attn_norm_g'], 'v_w_in': out['v_w_in'], 'v_dn_conv_w': out['v_dn_conv_w'], 'v_dn_a_log': out['v_dn_a_log'], 'v_dn_dt_bias': out['v_dn_dt_bias'], 'v_dn_out_norm_g': out['v_dn_out_norm_g'], 'v_sg_norm_g': out['v_sg_norm_g'], 'v_sg_w': out['v_sg_w'], 'v_sg_b': out['v_sg_b'], 'v_w_out': out['v_w_out'], 'v_ffn_norm_g': out['v_ffn_norm_g'], 'v_w_up': out['v_w_up'], 'v_ffn_conv_w': out['v_ffn_conv_w'], 'v_ffn_conv_b': out['v_ffn_conv_b'], 'v_w_down': out['v_w_down'], 'v_final_norm_g': out['v_final_norm_g']}


def _loss(weights, diff, rest, loss_target):
    with _jax.named_scope("forward"):
        args = {**rest, TWIN_DIFF_INPUT: diff, **{k: w.astype(_WEIGHT_DTYPES[k]) for k, w in weights.items()}}
        y = _forward(args)
    with _jax.named_scope("loss_head"):
        err = _jnp.square(y.astype(_jnp.float32) - loss_target)
        return 0.5 * _jnp.sum(_jnp.mean(err, axis=-1)) if err.ndim else 0.5 * err


def _adamw(w, g, m, v):
    m = ADAM_B1 * m + (1.0 - ADAM_B1) * g
    v = ADAM_B2 * v + (1.0 - ADAM_B2) * _jnp.square(g)
    m_hat = m / (1.0 - ADAM_B1 ** ADAM_STEP)
    v_hat = v / (1.0 - ADAM_B2 ** ADAM_STEP)
    delta = -ADAM_LR * (m_hat / (_jnp.sqrt(v_hat) + ADAM_EPS) + ADAM_WD * w)
    return delta, m, v


def reference(x, attn_norm_g, w_in, dn_conv_w, dn_a_log, dn_dt_bias, dn_out_norm_g, sg_norm_g, sg_w, sg_b, w_out, ffn_norm_g, w_up, ffn_conv_w, ffn_conv_b, w_down, final_norm_g, loss_target, m_attn_norm_g, m_w_in, m_dn_conv_w, m_dn_a_log, m_dn_dt_bias, m_dn_out_norm_g, m_sg_norm_g, m_sg_w, m_sg_b, m_w_out, m_ffn_norm_g, m_w_up, m_ffn_conv_w, m_ffn_conv_b, m_w_down, m_final_norm_g, v_attn_norm_g, v_w_in, v_dn_conv_w, v_dn_a_log, v_dn_dt_bias, v_dn_out_norm_g, v_sg_norm_g, v_sg_w, v_sg_b, v_w_out, v_ffn_norm_g, v_w_up, v_ffn_conv_w, v_ffn_conv_b, v_w_down, v_final_norm_g):
    given = dict(x=x, attn_norm_g=attn_norm_g, w_in=w_in, dn_conv_w=dn_conv_w, dn_a_log=dn_a_log, dn_dt_bias=dn_dt_bias, dn_out_norm_g=dn_out_norm_g, sg_norm_g=sg_norm_g, sg_w=sg_w, sg_b=sg_b, w_out=w_out, ffn_norm_g=ffn_norm_g, w_up=w_up, ffn_conv_w=ffn_conv_w, ffn_conv_b=ffn_conv_b, w_down=w_down, final_norm_g=final_norm_g, loss_target=loss_target, m_attn_norm_g=m_attn_norm_g, m_w_in=m_w_in, m_dn_conv_w=m_dn_conv_w, m_dn_a_log=m_dn_a_log, m_dn_dt_bias=m_dn_dt_bias, m_dn_out_norm_g=m_dn_out_norm_g, m_sg_norm_g=m_sg_norm_g, m_sg_w=m_sg_w, m_sg_b=m_sg_b, m_w_out=m_w_out, m_ffn_norm_g=m_ffn_norm_g, m_w_up=m_w_up, m_ffn_conv_w=m_ffn_conv_w, m_ffn_conv_b=m_ffn_conv_b, m_w_down=m_w_down, m_final_norm_g=m_final_norm_g, v_attn_norm_g=v_attn_norm_g, v_w_in=v_w_in, v_dn_conv_w=v_dn_conv_w, v_dn_a_log=v_dn_a_log, v_dn_dt_bias=v_dn_dt_bias, v_dn_out_norm_g=v_dn_out_norm_g, v_sg_norm_g=v_sg_norm_g, v_sg_w=v_sg_w, v_sg_b=v_sg_b, v_w_out=v_w_out, v_ffn_norm_g=v_ffn_norm_g, v_w_up=v_w_up, v_ffn_conv_w=v_ffn_conv_w, v_ffn_conv_b=v_ffn_conv_b, v_w_down=v_w_down, v_final_norm_g=v_final_norm_g)
    weights = {n: given[n] for n in TWIN_WEIGHTS}
    shared = {n: given[n] for n in SHARED_INPUTS}
    per_example = {n: given[n] for n in ['x']}
    grad_fn = _jax.value_and_grad(_loss, argnums=(0, 1))

    def one_microbatch(ex, loss_target):
        ex = dict(ex)
        diff = ex.pop(TWIN_DIFF_INPUT)
        return grad_fn(weights, diff, {**shared, **ex}, loss_target)

    if N_MICROBATCH == 1:
        loss, (grad_w, grad_x) = one_microbatch(per_example, given["loss_target"])
    else:
        def body(carry, xs):
            loss_sum, grad_sum = carry
            l_k, (gw_k, gx_k) = one_microbatch(xs[0], xs[1])
            with _jax.named_scope("update"):
                return (loss_sum + l_k, _jax.tree.map(_jnp.add, grad_sum, gw_k)), gx_k

        init = (_jnp.zeros((), _jnp.float32), _jax.tree.map(_jnp.zeros_like, weights))
        (loss, grad_w), grad_x = _jax.lax.scan(body, init, (per_example, given["loss_target"]))
    with _jax.named_scope("update"):
        delta_w, new_m, new_v = {}, {}, {}
        for n in TWIN_WEIGHTS:
            delta_w[n], new_m[n], new_v[n] = _adamw(weights[n], grad_w[n], given["m_" + n], given["v_" + n])
    return (loss, grad_x, *[grad_w[n] for n in TWIN_WEIGHTS], *[delta_w[n] for n in TWIN_WEIGHTS],
            *[new_m[n] for n in TWIN_WEIGHTS], *[new_v[n] for n in TWIN_WEIGHTS])
```

```python
import functools
import math

import jax
import jax.numpy as jnp
from jax import lax
from jax.experimental import pallas as pl
from jax.experimental.pallas import tpu as pltpu

F32 = jnp.float32
BF16 = jnp.bfloat16
HI = lax.Precision.HIGHEST

D_MODEL = 1024
DN_WIDTH = 512
HEAD_DIM = 128
N_HEADS = 4
SG_WIDTH = 512
SG_GROUPS = 4
SG_DIM = 128
SG_BLOCK = 128
D_FF = 2816
CHUNK = 64
CONV_K = 4
FFN_CONV = 3
EPS = 1e-6
PROJ_COLS = 3080
PROJ_MAIN = 3072
PROJ_PAD = 3328
GELU_C = math.sqrt(2.0 / math.pi)
N_DEV = 8
LANES = 128
SUBLANES = 8
HALO = SUBLANES
VMEM_LIMIT = 48 * 1024 * 1024

ADAM_LR = 0.001
ADAM_B1 = 0.9
ADAM_B2 = 0.999
ADAM_EPS = 1e-08
ADAM_WD = 0.01
ADAM_STEP = 10

MESH_AXES = ("x", "y", "c")
MESH_ID = pl.DeviceIdType.MESH


def _pcall(body, **kw):
    return pl.pallas_call(body, **kw)


def _params(*sem):
    return pltpu.CompilerParams(dimension_semantics=sem, vmem_limit_bytes=VMEM_LIMIT)


def _pick(n, cap):
    best = None
    for t in range(LANES, cap + 1, LANES):
        if n % t == 0:
            best = t
    return best if best else n


def dot_f32(a, b, dims):
    return lax.dot_general(a, b, dims, precision=HI, preferred_element_type=F32)


def dot_nn(a, b):
    return dot_f32(a, b, (((1,), (0,)), ((), ())))


def dot_nt(a, b):
    return dot_f32(a, b, (((1,), (1,)), ((), ())))


def dot_tn(a, b):
    return dot_f32(a, b, (((0,), (0,)), ((), ())))


def sigmoid(x):
    return 1.0 / (1.0 + jnp.exp(-x))


def silu(x):
    return x * sigmoid(x)


def silu_grad(x):
    s = sigmoid(x)
    return s * (1.0 + x * (1.0 - s))


def gelu(x):
    return 0.5 * x * (1.0 + jnp.tanh(GELU_C * (x + 0.044715 * x * x * x)))


def gelu_grad(x):
    t = jnp.tanh(GELU_C * (x + 0.044715 * x * x * x))
    return 0.5 * (1.0 + t) + 0.5 * x * (1.0 - t * t) * GELU_C * (1.0 + 3.0 * 0.044715 * x * x)


def softplus(z):
    return jnp.maximum(z, 0.0) + jnp.log(1.0 + jnp.exp(-jnp.abs(z)))


def rms_fwd(x, g):
    r = lax.rsqrt(jnp.mean(x * x, axis=-1, keepdims=True) + EPS)
    return x * r * g, r


def rms_bwd(x, r, g, dy):
    dyg = dy * g
    xr = x * r
    dx = r * (dyg - xr * jnp.mean(dyg * xr, axis=-1, keepdims=True))
    return dx, dy * xr


def l2_fwd(x):
    r = lax.rsqrt(jnp.sum(x * x, axis=-1, keepdims=True) + EPS)
    return x * r, r


def l2_bwd(x, r, dy):
    xr = x * r
    return r * (dy - xr * jnp.sum(dy * xr, axis=-1, keepdims=True))


def _tri_masks(n):
    row = lax.broadcasted_iota(jnp.int32, (n, n), 0)
    col = lax.broadcasted_iota(jnp.int32, (n, n), 1)
    return row >= col, row > col


def chunk_cumsum(g4):
    incl, _ = _tri_masks(g4.shape[0])
    gc = dot_nn(incl.astype(F32), g4)
    return gc, gc.T


def _unit_lower_inverse(l_strict):
    c = l_strict.shape[0]
    row = lax.broadcasted_iota(jnp.int32, (c, c), 0)
    col = lax.broadcasted_iota(jnp.int32, (c, c), 1)
    n = -l_strict
    a = (row == col).astype(F32) + n
    p = n
    for _ in range(int(math.log2(c)) - 1):
        p = dot_nn(p, p)
        a = a + dot_nn(a, p)
    return a


def dn_chunk_fwd(q, k, v, beta, gc_col, gc_row, s):
    c = q.shape[0]
    incl, strict = _tri_masks(c)
    decay = jnp.where(incl, jnp.exp(jnp.minimum(gc_col - gc_row, 0.0)), 0.0)
    gamma = jnp.exp(gc_col)
    gc_last = gc_col[c - 1:c, :]
    tau = jnp.exp(gc_last - gc_col)
    cd = jnp.exp(gc_last)
    kb = k * beta
    vb = v * beta
    l_mat = jnp.where(strict, dot_nt(kb, k) * decay, 0.0)
    a_inv = _unit_lower_inverse(l_mat)
    value = dot_nn(a_inv, vb)
    kcd = dot_nn(a_inv, kb * gamma)
    attn = dot_nt(q, k) * decay
    v_new = value - dot_nn(kcd, s)
    qd = q * gamma
    kt = k * tau
    o = dot_nn(qd, s) + dot_nn(attn, v_new)
    s_new = s * cd + dot_tn(kt, v_new)
    loc = dict(decay=decay, gamma=gamma, tau=tau, cd=cd, kb=kb, l_mat=l_mat, a_inv=a_inv, value=value,
               kcd=kcd, attn=attn, v_new=v_new, qd=qd, kt=kt, incl=incl, strict=strict)
    return o, s_new, loc


def dn_chunk_bwd(loc, q, k, v, beta, s, do, ds_new):
    c = q.shape[0]
    decay, gamma, tau, cd = loc["decay"], loc["gamma"], loc["tau"], loc["cd"]
    a_inv, attn, l_mat = loc["a_inv"], loc["attn"], loc["l_mat"]
    v_new, qd, kt, kcd, value, kb = loc["v_new"], loc["qd"], loc["kt"], loc["kcd"], loc["value"], loc["kb"]

    dv_new = dot_tn(attn, do) + dot_nn(kt, ds_new)
    dattn = jnp.where(loc["incl"], dot_nt(do, v_new), 0.0)
    dqd = dot_nt(do, s)
    ds = dot_tn(qd, do) + ds_new * cd - dot_tn(kcd, dv_new)
    dcd = jnp.sum(jnp.sum(ds_new * s, axis=1, keepdims=True), axis=0, keepdims=True)
    dkt = dot_nt(v_new, ds_new)
    dkcd = -dot_nt(dv_new, s)
    dvb = dot_tn(a_inv, dv_new)
    dkbg = dot_tn(a_inv, dkcd)
    dl = jnp.where(loc["strict"], -(dot_nt(dvb, value) + dot_nt(dkbg, kcd)), 0.0)
    dkk = dl * decay
    dqk = dattn * decay
    e = dl * l_mat + dattn * attn
    dgc = jnp.sum(e, axis=1, keepdims=True) - jnp.sum(e, axis=0, keepdims=True).T
    dkb = dot_nn(dkk, k) + dkbg * gamma
    dk = dot_tn(dkk, kb) + dot_tn(dqk, q) + dkt * tau
    dq = dot_nn(dqk, k) + dqd * gamma
    dgamma = jnp.sum(dkbg * kb, axis=1, keepdims=True) + jnp.sum(dqd * q, axis=1, keepdims=True)
    dtau_tau = jnp.sum(dkt * k, axis=1, keepdims=True) * tau
    dgc = dgc + dgamma * gamma - dtau_tau
    last = (lax.broadcasted_iota(jnp.int32, (c, 1), 0) == c - 1).astype(F32)
    dgc = dgc + last * (jnp.sum(dtau_tau, axis=0, keepdims=True) + dcd * cd)
    dk = dk + dkb * beta
    dbeta = jnp.sum(dkb * k, axis=1, keepdims=True) + jnp.sum(dvb * v, axis=1, keepdims=True)
    dv = dvb * beta
    return dq, dk, dv, dbeta, dgc, ds


def _token_tile(t):
    return _pick(t, 256)


def _rmsnorm_fwd(x, g):
    t, d = x.shape
    tm = _token_tile(t)

    def body(x_ref, g_ref, h_ref, r_ref):
        y, r = rms_fwd(x_ref[...], g_ref[...])
        h_ref[...] = y.astype(BF16)
        r_ref[...] = r

    return _pcall(
        body, grid=(t // tm,),
        in_specs=[pl.BlockSpec((tm, d), lambda i: (i, 0)), pl.BlockSpec((1, d), lambda i: (0, 0))],
        out_specs=[pl.BlockSpec((tm, d), lambda i: (i, 0)), pl.BlockSpec((tm, 1), lambda i: (i, 0))],
        out_shape=[jax.ShapeDtypeStruct((t, d), BF16), jax.ShapeDtypeStruct((t, 1), F32)],
        compiler_params=_params("parallel"), name="rmsnorm_fwd")(x, g)


def _rmsnorm_bwd(x, r, g, dh, dres):
    t, d = x.shape
    tm = _token_tile(t)

    def body(x_ref, r_ref, g_ref, dh_ref, dres_ref, dx_ref, dxb_ref, dg_ref):
        dx, dg_rows = rms_bwd(x_ref[...], r_ref[...], g_ref[...], dh_ref[...])
        dx = dx + dres_ref[...]
        dx_ref[...] = dx
        dxb_ref[...] = dx.astype(BF16)

        @pl.when(pl.program_id(0) == 0)
        def _():
            dg_ref[...] = jnp.zeros_like(dg_ref)

        dg_ref[...] += jnp.sum(dg_rows, axis=0, keepdims=True)

    tile = pl.BlockSpec((tm, d), lambda i: (i, 0))
    row = pl.BlockSpec((1, d), lambda i: (0, 0))
    return _pcall(
        body, grid=(t // tm,),
        in_specs=[tile, pl.BlockSpec((tm, 1), lambda i: (i, 0)), row, tile, tile],
        out_specs=[tile, tile, row],
        out_shape=[jax.ShapeDtypeStruct((t, d), F32), jax.ShapeDtypeStruct((t, d), BF16), jax.ShapeDtypeStruct((1, d), F32)],
        compiler_params=_params("arbitrary"), name="rmsnorm_bwd")(x, r, g, dh, dres)


def _matmul(a, b, mode, name, add=None, out_dtype=F32):
    if mode == "nn":
        (m, k), n = a.shape, b.shape[1]
    elif mode == "nt":
        (m, k), n = a.shape, b.shape[0]
    else:
        (k, m), n = a.shape, b.shape[1]
    tm, tn, tk = _pick(m, 512), _pick(n, 512), _pick(k, 1024)
    nk = k // tk
    dims = {"nn": (((1,), (0,)), ((), ())), "nt": (((1,), (1,)), ((), ())), "tn": (((0,), (0,)), ((), ()))}[mode]

    def body(*refs):
        if add is None:
            a_ref, b_ref, o_ref, acc_ref = refs
        else:
            a_ref, b_ref, add_ref, o_ref, acc_ref = refs
        kk = pl.program_id(2)
        part = lax.dot_general(a_ref[...], b_ref[...], dims, preferred_element_type=F32)

        @pl.when(kk == 0)
        def _():
            acc_ref[...] = part

        @pl.when(kk > 0)
        def _():
            acc_ref[...] += part

        @pl.when(kk == nk - 1)
        def _():
            res = acc_ref[...]
            if add is not None:
                res = res + add_ref[...]
            o_ref[...] = res.astype(o_ref.dtype)

    a_spec = pl.BlockSpec((tk, tm), lambda i, j, kk: (kk, i)) if mode == "tn" else pl.BlockSpec((tm, tk), lambda i, j, kk: (i, kk))
    b_spec = pl.BlockSpec((tn, tk), lambda i, j, kk: (j, kk)) if mode == "nt" else pl.BlockSpec((tk, tn), lambda i, j, kk: (kk, j))
    o_spec = pl.BlockSpec((tm, tn), lambda i, j, kk: (i, j))
    in_specs = [a_spec, b_spec] + ([o_spec] if add is not None else [])
    args = (a, b) + ((add,) if add is not None else ())
    return _pcall(
        body, grid=(m // tm, n // tn, nk), in_specs=in_specs, out_specs=o_spec,
        out_shape=jax.ShapeDtypeStruct((m, n), out_dtype), scratch_shapes=[pltpu.VMEM((tm, tn), F32)],
        compiler_params=_params("parallel", "parallel", "arbitrary"), name=name)(*args)


def _prev_halo_spec(tm, width, col_block):
    return pl.BlockSpec((HALO, width), lambda i: (jnp.maximum(i * (tm // HALO) - 1, 0), col_block))


def _fill_with_prev(xp_ref, tile, halo, first):
    xp_ref[0:HALO, :] = jnp.where(first, 0.0, halo)
    xp_ref[HALO:, :] = tile


def _causal_conv(xp_ref, w, taps, tm):
    out = None
    for k in range(taps):
        term = xp_ref[pl.ds(HALO - (taps - 1 - k), tm), :] * w[k:k + 1, :]
        out = term if out is None else out + term
    return out


def _dn_prep(p, conv_w, a_log4, dt_bias4):
    t = p.shape[0]
    tm = _token_tile(t)
    w3 = 3 * DN_WIDTH

    def body(x_ref, halo_ref, pb_ref, pa_ref, w_ref, alog_ref, dtb_ref, q_ref, k_ref, v_ref, beta_ref, g_ref, xp_ref):
        _fill_with_prev(xp_ref, x_ref[...], halo_ref[...], pl.program_id(0) == 0)
        y = silu(_causal_conv(xp_ref, w_ref[...], CONV_K, tm))
        for h in range(N_HEADS):
            sl = slice(h * HEAD_DIM, (h + 1) * HEAD_DIM)
            qn, _ = l2_fwd(y[:, sl])
            q_ref[:, sl] = qn * (HEAD_DIM ** -0.5)
            kn, _ = l2_fwd(y[:, DN_WIDTH + h * HEAD_DIM:DN_WIDTH + (h + 1) * HEAD_DIM])
            k_ref[:, sl] = kn
        v_ref[...] = y[:, 2 * DN_WIDTH:]
        lane = lax.broadcasted_iota(jnp.int32, (tm, LANES), 1)
        head = lane < N_HEADS
        beta_ref[...] = jnp.where(head, sigmoid(pb_ref[...]), 0.0)
        g_ref[...] = jnp.where(head, -jnp.exp(alog_ref[...]) * softplus(pa_ref[...] + dtb_ref[...]), 0.0)

    tok = lambda w, cb: pl.BlockSpec((tm, w), lambda i: (i, cb))
    full = lambda a: pl.BlockSpec(a.shape, lambda i: (0, 0))
    return _pcall(
        body, grid=(t // tm,),
        in_specs=[tok(w3, 0), _prev_halo_spec(tm, w3, 0), tok(LANES, PROJ_MAIN // LANES), tok(LANES, PROJ_MAIN // LANES + 1),
                  full(conv_w), full(a_log4), full(dt_bias4)],
        out_specs=[tok(DN_WIDTH, 0)] * 3 + [tok(LANES, 0)] * 2,
        out_shape=[jax.ShapeDtypeStruct((t, DN_WIDTH), F32)] * 3 + [jax.ShapeDtypeStruct((t, LANES), F32)] * 2,
        scratch_shapes=[pltpu.VMEM((HALO + tm, w3), F32)],
        compiler_params=_params("parallel"), name="dn_prep")(p, p, p, p, conv_w, a_log4, dt_bias4)


def _dn_prep_bwd(p, conv_w, a_log4, dt_bias4, dq, dk, dv, dbeta4, dg4):
    t = p.shape[0]
    tm = _token_tile(t)
    w3 = 3 * DN_WIDTH

    def body(x_ref, halo_ref, pb_ref, pa_ref, w_ref, alog_ref, dtb_ref, dq_ref, dk_ref, dv_ref, dbeta_ref, dg_ref,
             dc_ref, dw_ref, dpb_ref, dpa_ref, dalog_ref, ddtb_ref, xp_ref):
        first = pl.program_id(0) == 0
        _fill_with_prev(xp_ref, x_ref[...], halo_ref[...], first)
        c = _causal_conv(xp_ref, w_ref[...], CONV_K, tm)
        y = silu(c)
        for h in range(N_HEADS):
            sl = slice(h * HEAD_DIM, (h + 1) * HEAD_DIM)
            slk = slice(DN_WIDTH + h * HEAD_DIM, DN_WIDTH + (h + 1) * HEAD_DIM)
            _, rq = l2_fwd(y[:, sl])
            dc_ref[:, sl] = l2_bwd(y[:, sl], rq, dq_ref[:, sl] * (HEAD_DIM ** -0.5)) * silu_grad(c[:, sl])
            _, rk = l2_fwd(y[:, slk])
            dc_ref[:, slk] = l2_bwd(y[:, slk], rk, dk_ref[:, sl]) * silu_grad(c[:, slk])
        dc_ref[:, 2 * DN_WIDTH:] = dv_ref[...] * silu_grad(c[:, 2 * DN_WIDTH:])

        @pl.when(first)
        def _():
            dw_ref[...] = jnp.zeros_like(dw_ref)
            dalog_ref[...] = jnp.zeros_like(dalog_ref)
            ddtb_ref[...] = jnp.zeros_like(ddtb_ref)

        dc = dc_ref[...]
        for k in range(CONV_K):
            shifted = xp_ref[pl.ds(HALO - (CONV_K - 1 - k), tm), :]
            dw_ref[k:k + 1, :] += jnp.sum(dc * shifted, axis=0, keepdims=True)

        lane = lax.broadcasted_iota(jnp.int32, (tm, LANES), 1)
        head = lane < N_HEADS
        beta = sigmoid(pb_ref[...])
        dpb_ref[...] = jnp.where(head, dbeta_ref[...] * beta * (1.0 - beta), 0.0).astype(BF16)
        z = pa_ref[...] + dtb_ref[...]
        neg_rate = -jnp.exp(alog_ref[...])
        dg = dg_ref[...]
        dpa = jnp.where(head, dg * neg_rate * sigmoid(z), 0.0)
        dpa_ref[...] = dpa.astype(BF16)
        g = jnp.where(head, neg_rate * softplus(z), 0.0)
        dalog_ref[...] += jnp.sum(dg * g, axis=0, keepdims=True)
        ddtb_ref[...] += jnp.sum(dpa, axis=0, keepdims=True)

    tok = lambda w, cb: pl.BlockSpec((tm, w), lambda i: (i, cb))
    full = lambda shape: pl.BlockSpec(shape, lambda i: (0, 0))
    return _pcall(
        body, grid=(t // tm,),
        in_specs=[tok(w3, 0), _prev_halo_spec(tm, w3, 0), tok(LANES, PROJ_MAIN // LANES), tok(LANES, PROJ_MAIN // LANES + 1),
                  full(conv_w.shape), full(a_log4.shape), full(dt_bias4.shape)] + [tok(DN_WIDTH, 0)] * 3 + [tok(LANES, 0)] * 2,
        out_specs=[tok(w3, 0), full((CONV_K, w3)), tok(LANES, 0), tok(LANES, 0), full((1, LANES)), full((1, LANES))],
        out_shape=[jax.ShapeDtypeStruct((t, w3), F32), jax.ShapeDtypeStruct((CONV_K, w3), F32),
                   jax.ShapeDtypeStruct((t, LANES), BF16), jax.ShapeDtypeStruct((t, LANES), BF16),
                   jax.ShapeDtypeStruct((1, LANES), F32), jax.ShapeDtypeStruct((1, LANES), F32)],
        scratch_shapes=[pltpu.VMEM((HALO + tm, w3), F32)],
        compiler_params=_params("arbitrary"), name="dn_prep_bwd")(p, p, p, p, conv_w, a_log4, dt_bias4, dq, dk, dv, dbeta4, dg4)


def _conv_bwd_input(dc, w, name):
    t, c = dc.shape
    taps = w.shape[0]
    tm = _token_tile(t)
    ct = _pick(c, 1536)
    n_tok = t // tm

    def body(dc_ref, next_ref, w_ref, dx_ref, buf_ref):
        buf_ref[0:tm, :] = dc_ref[...]
        buf_ref[tm:, :] = jnp.where(pl.program_id(0) == n_tok - 1, 0.0, next_ref[...])
        wv = w_ref[...]
        out = None
        for k in range(taps):
            term = buf_ref[pl.ds(taps - 1 - k, tm), :] * wv[k:k + 1, :]
            out = term if out is None else out + term
        dx_ref[...] = out.astype(BF16)

    return _pcall(
        body, grid=(n_tok, c // ct),
        in_specs=[pl.BlockSpec((tm, ct), lambda i, j: (i, j)),
                  pl.BlockSpec((HALO, ct), lambda i, j: (jnp.minimum((i + 1) * (tm // HALO), t // HALO - 1), j)),
                  pl.BlockSpec((taps, ct), lambda i, j: (0, j))],
        out_specs=pl.BlockSpec((tm, ct), lambda i, j: (i, j)),
        out_shape=jax.ShapeDtypeStruct((t, c), BF16),
        scratch_shapes=[pltpu.VMEM((tm + HALO, ct), F32)],
        compiler_params=_params("parallel", "parallel"), name=name)(dc, dc, w)


def _dn_forward(q, k, v, beta4, g4, p, norm_g):
    t = q.shape[0]
    n = t // CHUNK

    def body(q_ref, k_ref, v_ref, b_ref, g_ref, gate_ref, ng_ref, mix_ref, s_all_ref, s_ref):
        @pl.when(pl.program_id(0) == 0)
        def _():
            s_ref[...] = jnp.zeros_like(s_ref)

        gc, gct = chunk_cumsum(g_ref[...])
        for h in range(N_HEADS):
            sl = slice(h * HEAD_DIM, (h + 1) * HEAD_DIM)
            s = s_ref[h]
            s_all_ref[0, h] = s
            o, s_new, _ = dn_chunk_fwd(q_ref[:, sl], k_ref[:, sl], v_ref[:, sl], b_ref[:, h:h + 1],
                                       gc[:, h:h + 1], gct[h:h + 1, :], s)
            s_ref[h] = s_new
            o_n, _ = rms_fwd(o, ng_ref[...])
            mix_ref[:, sl] = (o_n * silu(gate_ref[:, sl])).astype(BF16)

    ch = lambda w, cb: pl.BlockSpec((CHUNK, w), lambda i: (i, cb))
    return _pcall(
        body, grid=(n,),
        in_specs=[ch(DN_WIDTH, 0)] * 3 + [ch(LANES, 0)] * 2 + [ch(DN_WIDTH, 3), pl.BlockSpec((1, HEAD_DIM), lambda i: (0, 0))],
        out_specs=[ch(DN_WIDTH, 0), pl.BlockSpec((1, N_HEADS, HEAD_DIM, HEAD_DIM), lambda i: (i, 0, 0, 0))],
        out_shape=[jax.ShapeDtypeStruct((t, DN_WIDTH), BF16), jax.ShapeDtypeStruct((n, N_HEADS, HEAD_DIM, HEAD_DIM), F32)],
        scratch_shapes=[pltpu.VMEM((N_HEADS, HEAD_DIM, HEAD_DIM), F32)],
        compiler_params=_params("arbitrary"), name="dn_forward")(q, k, v, beta4, g4, p, norm_g)


def _dn_backward(q, k, v, beta4, g4, p, norm_g, s_all, dmix):
    t = q.shape[0]
    n = t // CHUNK

    def body(q_ref, k_ref, v_ref, b_ref, g_ref, gate_ref, ng_ref, s_in_ref, dmix_ref,
             dq_ref, dk_ref, dv_ref, db_ref, dg_ref, dgate_ref, dng_ref, ds_ref):
        @pl.when(pl.program_id(0) == 0)
        def _():
            ds_ref[...] = jnp.zeros_like(ds_ref)
            dng_ref[...] = jnp.zeros_like(dng_ref)

        gc, gct = chunk_cumsum(g_ref[...])
        lane = lax.broadcasted_iota(jnp.int32, (CHUNK, LANES), 1)
        db4 = jnp.zeros((CHUNK, LANES), F32)
        dgc4 = jnp.zeros((CHUNK, LANES), F32)
        for h in range(N_HEADS):
            sl = slice(h * HEAD_DIM, (h + 1) * HEAD_DIM)
            qh, kh, vh, bh, s = q_ref[:, sl], k_ref[:, sl], v_ref[:, sl], b_ref[:, h:h + 1], s_in_ref[0, h]
            o, _, loc = dn_chunk_fwd(qh, kh, vh, bh, gc[:, h:h + 1], gct[h:h + 1, :], s)
            o_n, r = rms_fwd(o, ng_ref[...])
            gate = gate_ref[:, sl]
            dmx = dmix_ref[:, sl]
            dgate_ref[:, sl] = (dmx * o_n * silu_grad(gate)).astype(BF16)
            do, dng_rows = rms_bwd(o, r, ng_ref[...], dmx * silu(gate))
            dng_ref[...] += jnp.sum(dng_rows, axis=0, keepdims=True)
            dq, dk, dv, dbeta, dgc, ds = dn_chunk_bwd(loc, qh, kh, vh, bh, s, do, ds_ref[h])
            dq_ref[:, sl] = dq
            dk_ref[:, sl] = dk
            dv_ref[:, sl] = dv
            ds_ref[h] = ds
            db4 = jnp.where(lane == h, dbeta, db4)
            dgc4 = jnp.where(lane == h, dgc, dgc4)
        _, strict = _tri_masks(CHUNK)
        db_ref[...] = db4
        dg_ref[...] = dot_nn(jnp.logical_not(strict).astype(F32), dgc4)

    rev = lambda w, cb: pl.BlockSpec((CHUNK, w), lambda i: (n - 1 - i, cb))
    return _pcall(
        body, grid=(n,),
        in_specs=[rev(DN_WIDTH, 0)] * 3 + [rev(LANES, 0)] * 2 + [rev(DN_WIDTH, 3), pl.BlockSpec((1, HEAD_DIM), lambda i: (0, 0)),
                  pl.BlockSpec((1, N_HEADS, HEAD_DIM, HEAD_DIM), lambda i: (n - 1 - i, 0, 0, 0)), rev(DN_WIDTH, 0)],
        out_specs=[rev(DN_WIDTH, 0)] * 3 + [rev(LANES, 0)] * 2 + [rev(DN_WIDTH, 0), pl.BlockSpec((1, HEAD_DIM), lambda i: (0, 0))],
        out_shape=[jax.ShapeDtypeStruct((t, DN_WIDTH), F32)] * 3 + [jax.ShapeDtypeStruct((t, LANES), F32)] * 2
        + [jax.ShapeDtypeStruct((t, DN_WIDTH), BF16), jax.ShapeDtypeStruct((1, HEAD_DIM), F32)],
        scratch_shapes=[pltpu.VMEM((N_HEADS, HEAD_DIM, HEAD_DIM), F32)],
        compiler_params=_params("arbitrary"), name="dn_backward")(q, k, v, beta4, g4, p, norm_g, s_all, dmix)


def _sg_mask():
    row = lax.broadcasted_iota(jnp.int32, (SG_BLOCK, SG_BLOCK), 0)
    col = lax.broadcasted_iota(jnp.int32, (SG_BLOCK, SG_BLOCK), 1)
    return (col // CHUNK) <= (row // CHUNK)


def _sg_forward(p, norm_g, w_s, b_t):
    t = p.shape[0]

    def body(u_ref, v_ref, ng_ref, w_ref, b_ref, o_ref):
        mask = _sg_mask()
        for g in range(SG_GROUPS):
            sl = slice(g * SG_DIM, (g + 1) * SG_DIM)
            vn, _ = rms_fwd(gelu(v_ref[:, sl]), ng_ref[:, sl])
            s = dot_nn(jnp.where(mask, w_ref[g], 0.0), vn) + b_ref[:, g:g + 1]
            o_ref[:, sl] = (gelu(u_ref[:, sl]) * s).astype(BF16)

    blk = lambda cb: pl.BlockSpec((SG_BLOCK, SG_WIDTH), lambda i: (i, cb))
    return _pcall(
        body, grid=(t // SG_BLOCK,),
        in_specs=[blk(4), blk(5), pl.BlockSpec((1, SG_WIDTH), lambda i: (0, 0)),
                  pl.BlockSpec((SG_GROUPS, SG_BLOCK, SG_BLOCK), lambda i: (0, 0, 0)), pl.BlockSpec((SG_BLOCK, SG_GROUPS), lambda i: (0, 0))],
        out_specs=blk(0), out_shape=jax.ShapeDtypeStruct((t, SG_WIDTH), BF16),
        compiler_params=_params("parallel"), name="sg_forward")(p, p, norm_g, w_s, b_t)


def _sg_backward(p, norm_g, w_s, b_t, dmix):
    t = p.shape[0]

    def body(u_ref, v_ref, ng_ref, w_ref, b_ref, do_ref, du_ref, dv_ref, dng_ref, dw_ref, db_ref):
        @pl.when(pl.program_id(0) == 0)
        def _():
            dng_ref[...] = jnp.zeros_like(dng_ref)
            dw_ref[...] = jnp.zeros_like(dw_ref)
            db_ref[...] = jnp.zeros_like(db_ref)

        mask = _sg_mask()
        lane = lax.broadcasted_iota(jnp.int32, (SG_BLOCK, LANES), 1)
        db = jnp.zeros((SG_BLOCK, LANES), F32)
        for g in range(SG_GROUPS):
            sl = slice(g * SG_DIM, (g + 1) * SG_DIM)
            u_raw, v_raw, do = u_ref[:, sl], v_ref[:, sl], do_ref[:, sl]
            vg = gelu(v_raw)
            vn, r = rms_fwd(vg, ng_ref[:, sl])
            w_m = jnp.where(mask, w_ref[g], 0.0)
            s = dot_nn(w_m, vn) + b_ref[:, g:g + 1]
            du_ref[:, sl] = (do * s * gelu_grad(u_raw)).astype(BF16)
            ds = do * gelu(u_raw)
            db = jnp.where(lane == g, jnp.sum(ds, axis=1, keepdims=True), db)
            dw_ref[g] += jnp.where(mask, dot_nt(ds, vn), 0.0)
            dvg, dng_rows = rms_bwd(vg, r, ng_ref[:, sl], dot_tn(w_m, ds))
            dng_ref[:, sl] += jnp.sum(dng_rows, axis=0, keepdims=True)
            dv_ref[:, sl] = (dvg * gelu_grad(v_raw)).astype(BF16)
        db_ref[...] += db

    blk = lambda cb: pl.BlockSpec((SG_BLOCK, SG_WIDTH), lambda i: (i, cb))
    const2 = lambda shape: pl.BlockSpec(shape, lambda i: (0, 0))
    w_spec = pl.BlockSpec((SG_GROUPS, SG_BLOCK, SG_BLOCK), lambda i: (0, 0, 0))
    return _pcall(
        body, grid=(t // SG_BLOCK,),
        in_specs=[blk(4), blk(5), const2((1, SG_WIDTH)), w_spec, const2((SG_BLOCK, SG_GROUPS)), blk(1)],
        out_specs=[blk(0), blk(0), const2((1, SG_WIDTH)), w_spec, const2((SG_BLOCK, LANES))],
        out_shape=[jax.ShapeDtypeStruct((t, SG_WIDTH), BF16)] * 2 + [jax.ShapeDtypeStruct((1, SG_WIDTH), F32),
                   jax.ShapeDtypeStruct((SG_GROUPS, SG_BLOCK, SG_BLOCK), F32), jax.ShapeDtypeStruct((SG_BLOCK, LANES), F32)],
        compiler_params=_params("arbitrary"), name="sg_backward")(p, p, norm_g, w_s, b_t, dmix)


FFN_CT = 256


def _ffn_act(up, conv_w, conv_b):
    t = up.shape[0]
    tm = _token_tile(t)
    nj = D_FF // FFN_CT

    def body(ug_ref, uv_ref, hg_ref, hv_ref, wg_ref, wv_ref, bg_ref, bv_ref, act_ref, xg_ref, xv_ref):
        first = pl.program_id(0) == 0
        _fill_with_prev(xg_ref, ug_ref[...], hg_ref[...], first)
        _fill_with_prev(xv_ref, uv_ref[...], hv_ref[...], first)
        cg = _causal_conv(xg_ref, wg_ref[...], FFN_CONV, tm) + bg_ref[...]
        cv = _causal_conv(xv_ref, wv_ref[...], FFN_CONV, tm) + bv_ref[...]
        act_ref[...] = (silu(cg) * cv).astype(BF16)

    tok = lambda off: pl.BlockSpec((tm, FFN_CT), lambda i, j: (i, j + off))
    halo = lambda off: pl.BlockSpec((HALO, FFN_CT), lambda i, j: (jnp.maximum(i * (tm // HALO) - 1, 0), j + off))
    par = lambda rows, off: pl.BlockSpec((rows, FFN_CT), lambda i, j: (0, j + off))
    return _pcall(
        body, grid=(t // tm, nj),
        in_specs=[tok(0), tok(nj), halo(0), halo(nj), par(FFN_CONV, 0), par(FFN_CONV, nj), par(1, 0), par(1, nj)],
        out_specs=pl.BlockSpec((tm, FFN_CT), lambda i, j: (i, j)),
        out_shape=jax.ShapeDtypeStruct((t, D_FF), BF16),
        scratch_shapes=[pltpu.VMEM((HALO + tm, FFN_CT), F32)] * 2,
        compiler_params=_params("parallel", "parallel"), name="ffn_act")(up, up, up, up, conv_w, conv_w, conv_b, conv_b)


def _ffn_act_bwd(up, conv_w, conv_b, dact):
    t = up.shape[0]
    tm = _token_tile(t)
    nj = D_FF // FFN_CT

    def body(ug_ref, uv_ref, hg_ref, hv_ref, wg_ref, wv_ref, bg_ref, bv_ref, da_ref,
             dcg_ref, dcv_ref, dwg_ref, dwv_ref, dbg_ref, dbv_ref, xg_ref, xv_ref):
        first = pl.program_id(1) == 0
        _fill_with_prev(xg_ref, ug_ref[...], hg_ref[...], first)
        _fill_with_prev(xv_ref, uv_ref[...], hv_ref[...], first)
        cg = _causal_conv(xg_ref, wg_ref[...], FFN_CONV, tm) + bg_ref[...]
        cv = _causal_conv(xv_ref, wv_ref[...], FFN_CONV, tm) + bv_ref[...]
        da = da_ref[...]
        dcg = da * cv * silu_grad(cg)
        dcv = da * silu(cg)
        dcg_ref[...] = dcg
        dcv_ref[...] = dcv

        @pl.when(first)
        def _():
            dwg_ref[...] = jnp.zeros_like(dwg_ref)
            dwv_ref[...] = jnp.zeros_like(dwv_ref)
            dbg_ref[...] = jnp.zeros_like(dbg_ref)
            dbv_ref[...] = jnp.zeros_like(dbv_ref)

        dbg_ref[...] += jnp.sum(dcg, axis=0, keepdims=True)
        dbv_ref[...] += jnp.sum(dcv, axis=0, keepdims=True)
        for k in range(FFN_CONV):
            sh = pl.ds(HALO - (FFN_CONV - 1 - k), tm)
            dwg_ref[k:k + 1, :] += jnp.sum(dcg * xg_ref[sh, :], axis=0, keepdims=True)
            dwv_ref[k:k + 1, :] += jnp.sum(dcv * xv_ref[sh, :], axis=0, keepdims=True)

    tok = lambda off: pl.BlockSpec((tm, FFN_CT), lambda j, i: (i, j + off))
    halo = lambda off: pl.BlockSpec((HALO, FFN_CT), lambda j, i: (jnp.maximum(i * (tm // HALO) - 1, 0), j + off))
    par = lambda rows, off: pl.BlockSpec((rows, FFN_CT), lambda j, i: (0, j + off))
    return _pcall(
        body, grid=(nj, t // tm),
        in_specs=[tok(0), tok(nj), halo(0), halo(nj), par(FFN_CONV, 0), par(FFN_CONV, nj), par(1, 0), par(1, nj), tok(0)],
        out_specs=[tok(0), tok(0), par(FFN_CONV, 0), par(FFN_CONV, 0), par(1, 0), par(1, 0)],
        out_shape=[jax.ShapeDtypeStruct((t, D_FF), F32)] * 2 + [jax.ShapeDtypeStruct((FFN_CONV, D_FF), F32)] * 2
        + [jax.ShapeDtypeStruct((1, D_FF), F32)] * 2,
        scratch_shapes=[pltpu.VMEM((HALO + tm, FFN_CT), F32)] * 2,
        compiler_params=_params("parallel", "arbitrary"), name="ffn_act_bwd")(up, up, up, up, conv_w, conv_w, conv_b, conv_b, dact)


def _final_loss(x3, target, g):
    t, d = x3.shape
    tm = _token_tile(t)

    def body(x_ref, t_ref, g_ref, loss_ref, dx_ref, dxb_ref, dg_ref):
        @pl.when(pl.program_id(0) == 0)
        def _():
            loss_ref[...] = jnp.zeros_like(loss_ref)
            dg_ref[...] = jnp.zeros_like(dg_ref)

        x = x_ref[...]
        y, r = rms_fwd(x, g_ref[...])
        err = y - t_ref[...]
        per_tok = jnp.mean(err * err, axis=-1, keepdims=True)
        loss_ref[...] += 0.5 * jnp.sum(per_tok, axis=0, keepdims=True)
        dx, dg_rows = rms_bwd(x, r, g_ref[...], err * (1.0 / d))
        dx_ref[...] = dx
        dxb_ref[...] = dx.astype(BF16)
        dg_ref[...] += jnp.sum(dg_rows, axis=0, keepdims=True)

    tile = pl.BlockSpec((tm, d), lambda i: (i, 0))
    row = pl.BlockSpec((1, d), lambda i: (0, 0))
    return _pcall(
        body, grid=(t // tm,), in_specs=[tile, tile, row],
        out_specs=[pl.BlockSpec((1, LANES), lambda i: (0, 0)), tile, tile, row],
        out_shape=[jax.ShapeDtypeStruct((1, LANES), F32), jax.ShapeDtypeStruct((t, d), F32), jax.ShapeDtypeStruct((t, d), BF16),
                   jax.ShapeDtypeStruct((1, d), F32)],
        compiler_params=_params("arbitrary"), name="final_loss")(x3, target, g)


def _my_position():
    return lax.axis_index("x"), lax.axis_index("y"), lax.axis_index("c")


def _all_gather(slab):
    r, c = slab.shape

    def body(x_ref, out_ref, send_sems, recv_sems, local_sem):
        x, y, cc = _my_position()
        me, sibling = (x, y, cc), (x, y, 1 - cc)
        chips = [(1 - x, y), (x, 1 - y), (1 - x, 1 - y)]

        def block(px, py, pc):
            return out_ref.at[4 * px + 2 * py + pc]

        def copy(k, blk, to, src=None):
            return pltpu.make_async_remote_copy(
                src_ref=block(*blk) if src is None else src, dst_ref=block(*blk),
                send_sem=send_sems.at[k], recv_sem=recv_sems.at[k], device_id=to, device_id_type=MESH_ID)

        mine = pltpu.make_async_copy(x_ref, block(*me), local_sem)
        mine.start()
        first = [copy(0, me, sibling, src=x_ref)]
        first += [copy(1 + j, me, (*chip, cc), src=x_ref) for j, chip in enumerate(chips)]
        for cp in first:
            cp.start()
        passed = [copy(4 + j, (*chip, cc), sibling) for j, chip in enumerate(chips)]
        for j, chip in enumerate(chips):
            copy(1 + j, (*chip, cc), me).wait_recv()
            passed[j].start()
        copy(0, sibling, me).wait_recv()
        for j, chip in enumerate(chips):
            copy(4 + j, (*chip, 1 - cc), me).wait_recv()
        for cp in first + passed:
            cp.wait_send()
        mine.wait()

    return _pcall(
        body, out_shape=jax.ShapeDtypeStruct((N_DEV, r, c), slab.dtype),
        in_specs=[pl.BlockSpec(memory_space=pl.ANY)], out_specs=pl.BlockSpec(memory_space=pl.ANY),
        scratch_shapes=[pltpu.SemaphoreType.DMA((7,)), pltpu.SemaphoreType.DMA((7,)), pltpu.SemaphoreType.DMA],
        name="all_gather")(slab)


def _all_to_all(send):
    _, r, c = send.shape

    def body(send_ref, recv_ref, send_sems, recv_sems, local_sem):
        x, y, cc = _my_position()
        me = 4 * x + 2 * y + cc
        mine = pltpu.make_async_copy(send_ref.at[me], recv_ref.at[me], local_sem)
        mine.start()
        copies = []
        for rel in range(1, N_DEV):
            px, py, pc = x ^ (rel >> 2), y ^ ((rel >> 1) & 1), cc ^ (rel & 1)
            copies.append(pltpu.make_async_remote_copy(
                src_ref=send_ref.at[4 * px + 2 * py + pc], dst_ref=recv_ref.at[me],
                send_sem=send_sems.at[rel - 1], recv_sem=recv_sems.at[rel - 1],
                device_id=(px, py, pc), device_id_type=MESH_ID))
        for cp in copies:
            cp.start()
        for cp in copies:
            cp.wait()
        mine.wait()

    return _pcall(
        body, out_shape=jax.ShapeDtypeStruct(send.shape, send.dtype),
        in_specs=[pl.BlockSpec(memory_space=pl.ANY)], out_specs=pl.BlockSpec(memory_space=pl.ANY),
        scratch_shapes=[pltpu.SemaphoreType.DMA((7,)), pltpu.SemaphoreType.DMA((7,)), pltpu.SemaphoreType.DMA],
        name="all_to_all")(send)


def _sum_and_adamw(recv, w, m, v):
    _, r, c = recv.shape
    tr = SLAB_ROW_TILE
    bc1 = 1.0 - ADAM_B1 ** ADAM_STEP
    bc2 = 1.0 - ADAM_B2 ** ADAM_STEP

    def body(recv_ref, w_ref, m_ref, v_ref, g_ref, d_ref, nm_ref, nv_ref):
        g = recv_ref[0]
        for s in range(1, N_DEV):
            g = g + recv_ref[s]
        m_new = ADAM_B1 * m_ref[...] + (1.0 - ADAM_B1) * g
        v_new = ADAM_B2 * v_ref[...] + (1.0 - ADAM_B2) * (g * g)
        m_hat = m_new / bc1
        v_hat = v_new / bc2
        g_ref[...] = g
        d_ref[...] = -ADAM_LR * (m_hat / (jnp.sqrt(v_hat) + ADAM_EPS) + ADAM_WD * w_ref[...])
        nm_ref[...] = m_new
        nv_ref[...] = v_new

    tile = pl.BlockSpec((tr, c), lambda i: (i, 0))
    return _pcall(
        body, grid=(r // tr,),
        in_specs=[pl.BlockSpec((N_DEV, tr, c), lambda i: (0, i, 0)), tile, tile, tile],
        out_specs=[tile] * 4, out_shape=[jax.ShapeDtypeStruct((r, c), F32)] * 4,
        compiler_params=_params("parallel"), name="sum_and_adamw")(recv, w, m, v)


SHARDED = ("w_in", "w_up", "w_out", "w_down", "dn_conv_w", "ffn_conv_w")
REPLICATED = ("attn_norm_g", "dn_a_log", "dn_dt_bias", "dn_out_norm_g", "sg_norm_g", "sg_w", "sg_b", "ffn_norm_g",
              "ffn_conv_b", "final_norm_g")
WEIGHT_ORDER = ("attn_norm_g", "w_in", "dn_conv_w", "dn_a_log", "dn_dt_bias", "dn_out_norm_g", "sg_norm_g", "sg_w", "sg_b",
                "w_out", "ffn_norm_g", "w_up", "ffn_conv_w", "ffn_conv_b", "w_down", "final_norm_g")
SLAB_COLS = 1024
SLAB_ROW_TILE = 128


def _pad_to(flat, multiple):
    pad = (-flat.shape[-1]) % multiple
    if pad == 0:
        return flat
    return jnp.pad(flat, [(0, 0)] * (flat.ndim - 1) + [(0, pad)])


def _pack_local(named):
    flat = jnp.concatenate([named[n].reshape(-1) for n in SHARDED + REPLICATED])
    return _pad_to(flat, SLAB_ROW_TILE * SLAB_COLS).reshape(-1, SLAB_COLS)


def _unpack_local(slab, like):
    flat = slab.reshape(-1)
    out, off = {}, 0
    for n in SHARDED + REPLICATED:
        size = like[n].size
        out[n] = flat[off:off + size].reshape(like[n].shape)
        off += size
    return out


def _split_columns(full, n_local):
    r = full.shape[0]
    return full.reshape(r, N_DEV, n_local).transpose(1, 0, 2).reshape(N_DEV, r * n_local)


def _join_columns(blocks, r, n_local):
    return blocks.reshape(N_DEV, r, n_local).transpose(1, 0, 2).reshape(r, N_DEV * n_local)


def _lanes4(a):
    return jnp.pad(a.reshape(1, N_HEADS), ((0, 0), (0, LANES - N_HEADS)))


def kernel(x, attn_norm_g, w_in, dn_conv_w, dn_a_log, dn_dt_bias, dn_out_norm_g, sg_norm_g, sg_w, sg_b, w_out, ffn_norm_g, w_up, ffn_conv_w, ffn_conv_b, w_down, final_norm_g, loss_target, m_attn_norm_g, m_w_in, m_dn_conv_w, m_dn_a_log, m_dn_dt_bias, m_dn_out_norm_g, m_sg_norm_g, m_sg_w, m_sg_b, m_w_out, m_ffn_norm_g, m_w_up, m_ffn_conv_w, m_ffn_conv_b, m_w_down, m_final_norm_g, v_attn_norm_g, v_w_in, v_dn_conv_w, v_dn_a_log, v_dn_dt_bias, v_dn_out_norm_g, v_sg_norm_g, v_sg_w, v_sg_b, v_w_out, v_ffn_norm_g, v_w_up, v_ffn_conv_w, v_ffn_conv_b, v_w_down, v_final_norm_g):
    weights = dict(attn_norm_g=attn_norm_g, w_in=w_in, dn_conv_w=dn_conv_w, dn_a_log=dn_a_log, dn_dt_bias=dn_dt_bias,
                   dn_out_norm_g=dn_out_norm_g, sg_norm_g=sg_norm_g, sg_w=sg_w, sg_b=sg_b, w_out=w_out, ffn_norm_g=ffn_norm_g,
                   w_up=w_up, ffn_conv_w=ffn_conv_w, ffn_conv_b=ffn_conv_b, w_down=w_down, final_norm_g=final_norm_g)
    m_in = dict(attn_norm_g=m_attn_norm_g, w_in=m_w_in, dn_conv_w=m_dn_conv_w, dn_a_log=m_dn_a_log, dn_dt_bias=m_dn_dt_bias,
                dn_out_norm_g=m_dn_out_norm_g, sg_norm_g=m_sg_norm_g, sg_w=m_sg_w, sg_b=m_sg_b, w_out=m_w_out,
                ffn_norm_g=m_ffn_norm_g, w_up=m_w_up, ffn_conv_w=m_ffn_conv_w, ffn_conv_b=m_ffn_conv_b, w_down=m_w_down,
                final_norm_g=m_final_norm_g)
    v_in = dict(attn_norm_g=v_attn_norm_g, w_in=v_w_in, dn_conv_w=v_dn_conv_w, dn_a_log=v_dn_a_log, dn_dt_bias=v_dn_dt_bias,
                dn_out_norm_g=v_dn_out_norm_g, sg_norm_g=v_sg_norm_g, sg_w=v_sg_w, sg_b=v_sg_b, w_out=v_w_out,
                ffn_norm_g=v_ffn_norm_g, w_up=v_w_up, ffn_conv_w=v_ffn_conv_w, ffn_conv_b=v_ffn_conv_b, w_down=v_w_down,
                final_norm_g=v_final_norm_g)

    n_in, n_up = w_in.shape[2], w_up.shape[2]
    r_out, r_down = w_out.shape[1], w_down.shape[1]
    n_dnc, n_ffc = dn_conv_w.shape[2], ffn_conv_w.shape[2]
    taps = jnp.concatenate([dn_conv_w.reshape(-1), ffn_conv_w.reshape(-1)])
    flat = jnp.concatenate([w_in.reshape(-1).astype(BF16), w_up.reshape(-1).astype(BF16), w_out.reshape(-1).astype(BF16),
                            w_down.reshape(-1).astype(BF16), lax.bitcast_convert_type(taps, BF16).reshape(-1)])
    gathered = _all_gather(_pad_to(flat, 2 * SUBLANES * SLAB_COLS).reshape(-1, SLAB_COLS))
    gathered = gathered.reshape(N_DEV, -1)
    off = 0

    def take(size):
        nonlocal off
        part = gathered[:, off:off + size]
        off += size
        return part

    w_in_full = _join_columns(take(D_MODEL * n_in), D_MODEL, n_in)
    w_up_full = _join_columns(take(D_MODEL * n_up), D_MODEL, n_up)
    w_out_full = take(r_out * D_MODEL).reshape(N_DEV * r_out, D_MODEL)
    w_down_full = take(r_down * D_MODEL).reshape(N_DEV * r_down, D_MODEL)
    taps_all = lax.bitcast_convert_type(take(2 * (CONV_K * n_dnc + FFN_CONV * n_ffc)).reshape(N_DEV, -1, 2), F32)
    dn_conv_full = _join_columns(taps_all[:, :CONV_K * n_dnc], CONV_K, n_dnc)
    ffn_conv_full = _join_columns(taps_all[:, CONV_K * n_dnc:], FFN_CONV, n_ffc)
    pad_block = lambda cols: jnp.pad(cols, ((0, 0), (0, LANES - cols.shape[1])))
    w_in_p = jnp.concatenate([w_in_full[:, :PROJ_MAIN], pad_block(w_in_full[:, PROJ_MAIN:PROJ_MAIN + N_HEADS]),
                              pad_block(w_in_full[:, PROJ_MAIN + N_HEADS:])], axis=1)

    loss_lanes, grad_x, g = _local_step(
        x[0], loss_target[0], w_in_p, w_up_full, w_out_full, w_down_full, dn_conv_full, ffn_conv_full, attn_norm_g, dn_a_log,
        dn_dt_bias, dn_out_norm_g, sg_norm_g, sg_w, sg_b, ffn_norm_g, ffn_conv_b, final_norm_g)

    small = jnp.concatenate([g[n].reshape(-1) for n in REPLICATED])
    send = jnp.concatenate([
        _split_columns(g["w_in"], n_in), _split_columns(g["w_up"], n_up), g["w_out"].reshape(N_DEV, -1),
        g["w_down"].reshape(N_DEV, -1), _split_columns(g["dn_conv_w"], n_dnc), _split_columns(g["ffn_conv_w"], n_ffc),
        jnp.broadcast_to(small[None, :], (N_DEV, small.shape[0]))], axis=1)
    send = _pad_to(send, SLAB_ROW_TILE * SLAB_COLS).reshape(N_DEV, -1, SLAB_COLS)
    recv = _all_to_all(send)

    grad_s, delta_s, new_m_s, new_v_s = _sum_and_adamw(recv, _pack_local(weights), _pack_local(m_in), _pack_local(v_in))
    grads, deltas = _unpack_local(grad_s, weights), _unpack_local(delta_s, weights)
    new_m, new_v = _unpack_local(new_m_s, weights), _unpack_local(new_v_s, weights)

    loss = lax.psum(loss_lanes[0, 0], MESH_AXES)
    return (loss, grad_x[None], *[grads[n] for n in WEIGHT_ORDER], *[deltas[n] for n in WEIGHT_ORDER],
            *[new_m[n] for n in WEIGHT_ORDER], *[new_v[n] for n in WEIGHT_ORDER])


def _local_step(x2d, tgt, w_in_p, w_up_full, w_out_full, w_down_full, dn_conv_full, ffn_conv_full, attn_norm_g, dn_a_log,
                dn_dt_bias, dn_out_norm_g, sg_norm_g, sg_w, sg_b, ffn_norm_g, ffn_conv_b, final_norm_g):
    g1, g2, gf = attn_norm_g, ffn_norm_g, final_norm_g.reshape(1, D_MODEL)
    a_log4, dt_bias4 = _lanes4(dn_a_log), _lanes4(dn_dt_bias)
    sg_w3 = sg_w[0]
    sg_b_t = sg_b[0].T
    conv_b = ffn_conv_b

    h1, rstd1 = _rmsnorm_fwd(x2d, g1)
    p = _matmul(h1, w_in_p, "nn", "in_proj")
    q, k, v, beta4, g4 = _dn_prep(p, dn_conv_full, a_log4, dt_bias4)
    mix_dn, s_all = _dn_forward(q, k, v, beta4, g4, p, dn_out_norm_g)
    o_sg = _sg_forward(p, sg_norm_g, sg_w3, sg_b_t)
    mix = jnp.concatenate([mix_dn, o_sg], axis=1)
    x2 = _matmul(mix, w_out_full, "nn", "out_proj", add=x2d)
    h2, rstd2 = _rmsnorm_fwd(x2, g2)
    up = _matmul(h2, w_up_full, "nn", "up_proj")
    act = _ffn_act(up, ffn_conv_full, conv_b)
    x3 = _matmul(act, w_down_full, "nn", "down_proj", add=x2)
    loss_lanes, dx3, dx3b, d_gf = _final_loss(x3, tgt, gf)

    dact = _matmul(dx3b, w_down_full, "nt", "down_proj_dx")
    d_w_down = _matmul(act, dx3b, "tn", "down_proj_dw")
    dc_g, dc_v, dcw_g, dcw_v, dcb_g, dcb_v = _ffn_act_bwd(up, ffn_conv_full, conv_b, dact)
    dup = jnp.concatenate([_conv_bwd_input(dc_g, ffn_conv_full[:, :D_FF], "ffn_conv_dx_gate"),
                           _conv_bwd_input(dc_v, ffn_conv_full[:, D_FF:], "ffn_conv_dx_value")], axis=1)
    dh2 = _matmul(dup, w_up_full, "nt", "up_proj_dx")
    d_w_up = _matmul(h2, dup, "tn", "up_proj_dw")
    dx2, dx2b, d_g2 = _rmsnorm_bwd(x2, rstd2, g2, dh2, dx3)
    dmix = _matmul(dx2b, w_out_full, "nt", "out_proj_dx")
    d_w_out = _matmul(mix, dx2b, "tn", "out_proj_dw")
    dp_u, dp_v, d_sg_norm, d_sg_w, d_sg_b_t = _sg_backward(p, sg_norm_g, sg_w3, sg_b_t, dmix)
    dq, dk, dv, dbeta4, dg4, dp_gate, d_dn_norm = _dn_backward(q, k, v, beta4, g4, p, dn_out_norm_g, s_all, dmix)
    dc_dn, d_dn_conv, dp_b, dp_a, d_a_log4, d_dt_bias4 = _dn_prep_bwd(p, dn_conv_full, a_log4, dt_bias4, dq, dk, dv, dbeta4, dg4)
    dp_qkv = _conv_bwd_input(dc_dn, dn_conv_full, "dn_conv_dx")
    dp = jnp.concatenate([dp_qkv, dp_gate, dp_u, dp_v, dp_b, dp_a], axis=1)
    dh1 = _matmul(dp, w_in_p, "nt", "in_proj_dx")
    d_w_in_p = _matmul(h1, dp, "tn", "in_proj_dw")
    grad_x, _, d_g1 = _rmsnorm_bwd(x2d, rstd1, g1, dh1, dx2)
    d_w_in = jnp.concatenate([d_w_in_p[:, :PROJ_MAIN], d_w_in_p[:, PROJ_MAIN:PROJ_MAIN + N_HEADS],
                              d_w_in_p[:, PROJ_MAIN + LANES:PROJ_MAIN + LANES + N_HEADS]], axis=1)

    grads = dict(
        attn_norm_g=d_g1, w_in=d_w_in, dn_conv_w=d_dn_conv, dn_a_log=d_a_log4[:, :N_HEADS], dn_dt_bias=d_dt_bias4[:, :N_HEADS],
        dn_out_norm_g=d_dn_norm, sg_norm_g=d_sg_norm, sg_w=d_sg_w, sg_b=d_sg_b_t[:, :SG_GROUPS].T, w_out=d_w_out,
        ffn_norm_g=d_g2, w_up=d_w_up, ffn_conv_w=jnp.concatenate([dcw_g, dcw_v], axis=1),
        ffn_conv_b=jnp.concatenate([dcb_g, dcb_v], axis=1), w_down=d_w_down, final_norm_g=d_gf)
    return loss_lanes, grad_x, grads
```

```python
import functools
import math

import jax
import jax.numpy as jnp
from jax import lax
from jax.experimental import pallas as pl
from jax.experimental.pallas import tpu as pltpu

F32 = jnp.float32
BF16 = jnp.bfloat16
HI = lax.Precision.HIGHEST

D_MODEL = 1024
DN_WIDTH = 512
HEAD_DIM = 128
N_HEADS = 4
SG_WIDTH = 512
SG_GROUPS = 4
SG_DIM = 128
SG_BLOCK = 128
D_FF = 2816
CHUNK = 64
CONV_K = 4
FFN_CONV = 3
EPS = 1e-6
PROJ_COLS = 3080
PROJ_MAIN = 3072
PROJ_PAD = 3328
GELU_C = math.sqrt(2.0 / math.pi)
N_DEV = 8
LANES = 128
SUBLANES = 8
HALO = SUBLANES
VMEM_LIMIT = 48 * 1024 * 1024

ADAM_LR = 0.001
ADAM_B1 = 0.9
ADAM_B2 = 0.999
ADAM_EPS = 1e-08
ADAM_WD = 0.01
ADAM_STEP = 10

MESH_AXES = ("x", "y", "c")
MESH_ID = pl.DeviceIdType.MESH


def _pcall(body, **kw):
    return pl.pallas_call(body, **kw)


def _params(*sem):
    return pltpu.CompilerParams(dimension_semantics=sem, vmem_limit_bytes=VMEM_LIMIT)


def _pick(n, cap):
    best = None
    for t in range(LANES, cap + 1, LANES):
        if n % t == 0:
            best = t
    return best if best else n


FAST, MID, EXACT = "bf16 operands, one pass", "three bf16 passes", "six bf16 passes"


def dot_f32(a, b, dims, tier):
    if tier == FAST:
        return lax.dot_general(a.astype(BF16), b.astype(BF16), dims, preferred_element_type=F32)
    prec = lax.Precision.HIGH if tier == MID else HI
    return lax.dot_general(a, b, dims, precision=prec, preferred_element_type=F32)


def dot_nn(a, b, tier=EXACT):
    return dot_f32(a, b, (((1,), (0,)), ((), ())), tier)


def dot_nt(a, b, tier=EXACT):
    return dot_f32(a, b, (((1,), (1,)), ((), ())), tier)


def dot_tn(a, b, tier=EXACT):
    return dot_f32(a, b, (((0,), (0,)), ((), ())), tier)


def sigmoid(x):
    return 1.0 / (1.0 + jnp.exp(-x))


def silu(x):
    return x * sigmoid(x)


def silu_grad(x):
    s = sigmoid(x)
    return s * (1.0 + x * (1.0 - s))


def gelu(x):
    return 0.5 * x * (1.0 + jnp.tanh(GELU_C * (x + 0.044715 * x * x * x)))


def gelu_grad(x):
    t = jnp.tanh(GELU_C * (x + 0.044715 * x * x * x))
    return 0.5 * (1.0 + t) + 0.5 * x * (1.0 - t * t) * GELU_C * (1.0 + 3.0 * 0.044715 * x * x)


def softplus(z):
    return jnp.maximum(z, 0.0) + jnp.log(1.0 + jnp.exp(-jnp.abs(z)))


def rms_fwd(x, g):
    r = lax.rsqrt(jnp.mean(x * x, axis=-1, keepdims=True) + EPS)
    return x * r * g, r


def rms_bwd(x, r, g, dy):
    dyg = dy * g
    xr = x * r
    dx = r * (dyg - xr * jnp.mean(dyg * xr, axis=-1, keepdims=True))
    return dx, dy * xr


def l2_fwd(x):
    r = lax.rsqrt(jnp.sum(x * x, axis=-1, keepdims=True) + EPS)
    return x * r, r


def l2_bwd(x, r, dy):
    xr = x * r
    return r * (dy - xr * jnp.sum(dy * xr, axis=-1, keepdims=True))


def _tri_masks(n):
    row = lax.broadcasted_iota(jnp.int32, (n, n), 0)
    col = lax.broadcasted_iota(jnp.int32, (n, n), 1)
    return row >= col, row > col


def chunk_cumsum(g4):
    incl, _ = _tri_masks(g4.shape[0])
    return dot_nn(incl.astype(F32), g4)


STACK = N_HEADS * CHUNK


def _head_rows(h):
    return slice(h * CHUNK, (h + 1) * CHUNK)


def _stack_heads(x):
    return jnp.concatenate([x[:, h * HEAD_DIM:(h + 1) * HEAD_DIM] for h in range(N_HEADS)], axis=0)


def _stack_lanes(x4):
    return jnp.concatenate([x4[:, h:h + 1] for h in range(N_HEADS)], axis=0)


def _per_head(fn):
    return jnp.concatenate([fn(h) for h in range(N_HEADS)], axis=0)


def _unit_lower_inverse(l_strict, order):
    c = l_strict.shape[0]
    row = lax.broadcasted_iota(jnp.int32, (c, c), 0)
    col = lax.broadcasted_iota(jnp.int32, (c, c), 1)
    n = -l_strict
    a = (row == col).astype(F32) + n
    p = n
    for _ in range(int(math.log2(order)) - 1):
        p = dot_nn(p, p, FAST)
        a = a + dot_nn(a, p, FAST)
    return a


def dn_chunk_fwd(q, k, v, beta, gc4, s_of):
    row = lax.broadcasted_iota(jnp.int32, (STACK, STACK), 0)
    col = lax.broadcasted_iota(jnp.int32, (STACK, STACK), 1)
    same = (row // CHUNK) == (col // CHUNK)
    incl = jnp.logical_and(same, row >= col)
    strict = jnp.logical_and(same, row > col)
    gc_col = _stack_lanes(gc4)
    gc_row = jnp.sum(jnp.where(row == col, gc_col, 0.0), axis=0, keepdims=True)
    decay = jnp.where(incl, jnp.exp(jnp.minimum(gc_col - gc_row, 0.0)), 0.0)
    gamma = jnp.exp(gc_col)
    gc_last = jnp.concatenate([jnp.broadcast_to(gc4[CHUNK - 1:CHUNK, h:h + 1], (CHUNK, 1)) for h in range(N_HEADS)], axis=0)
    tau = jnp.exp(gc_last - gc_col)
    cd = jnp.exp(gc_last)
    kb = k * beta
    l_mat = jnp.where(strict, dot_nt(kb, k, FAST) * decay, 0.0)
    a_inv = _unit_lower_inverse(l_mat, CHUNK)
    sol = dot_nn(a_inv, jnp.concatenate([v * beta, kb * gamma], axis=1), MID)
    value, kcd = sol[:, :HEAD_DIM], sol[:, HEAD_DIM:]
    attn = jnp.where(incl, dot_nt(q, k, FAST) * decay, 0.0)
    qd = q * gamma
    kt = k * tau
    v_new = value - _per_head(lambda h: dot_nn(kcd[_head_rows(h)], s_of(h), FAST))
    o = _per_head(lambda h: dot_nn(qd[_head_rows(h)], s_of(h), FAST)) + dot_nn(attn, v_new, FAST)
    s_new = [s_of(h) * cd[h * CHUNK:h * CHUNK + 1, :] + dot_tn(kt[_head_rows(h)], v_new[_head_rows(h)], FAST)
             for h in range(N_HEADS)]
    loc = dict(decay=decay, gamma=gamma, tau=tau, cd=cd, kb=kb, l_mat=l_mat, a_inv=a_inv, sol=sol, kcd=kcd, attn=attn,
               v_new=v_new, qd=qd, kt=kt, incl=incl, strict=strict)
    return o, s_new, loc


def dn_chunk_bwd(loc, q, k, v, beta, s_of, do, ds_new_of):
    decay, gamma, tau, cd = loc["decay"], loc["gamma"], loc["tau"], loc["cd"]
    a_inv, attn, l_mat, sol = loc["a_inv"], loc["attn"], loc["l_mat"], loc["sol"]
    v_new, qd, kt, kcd, kb = loc["v_new"], loc["qd"], loc["kt"], loc["kcd"], loc["kb"]
    hr = _head_rows

    dv_new = dot_tn(attn, do, MID) + _per_head(lambda h: dot_nn(kt[hr(h)], ds_new_of(h), MID))
    dattn = jnp.where(loc["incl"], dot_nt(do, v_new, MID), 0.0)
    dqd = _per_head(lambda h: dot_nt(do[hr(h)], s_of(h), MID))
    ds = [dot_tn(qd[hr(h)], do[hr(h)], MID) + ds_new_of(h) * cd[h * CHUNK:h * CHUNK + 1, :]
          - dot_tn(kcd[hr(h)], dv_new[hr(h)], MID) for h in range(N_HEADS)]
    dkt = _per_head(lambda h: dot_nt(v_new[hr(h)], ds_new_of(h), MID))
    dkcd = -_per_head(lambda h: dot_nt(dv_new[hr(h)], s_of(h), MID))
    drhs = dot_tn(a_inv, jnp.concatenate([dv_new, dkcd], axis=1), MID)
    dvb, dkbg = drhs[:, :HEAD_DIM], drhs[:, HEAD_DIM:]
    dl = jnp.where(loc["strict"], -dot_nt(drhs, sol, MID), 0.0)
    dkk = dl * decay
    dqk = dattn * decay
    e = dl * l_mat + dattn * attn
    dgc = jnp.sum(e, axis=1, keepdims=True) - jnp.sum(e, axis=0, keepdims=True).T
    dkb = dot_nn(dkk, k, MID) + dkbg * gamma
    dk = dot_tn(dkk, kb, MID) + dot_tn(dqk, q, MID) + dkt * tau
    dq = dot_nn(dqk, k, MID) + dqd * gamma
    dgamma = jnp.sum(dkbg * kb, axis=1, keepdims=True) + jnp.sum(dqd * q, axis=1, keepdims=True)
    dtau_tau = jnp.sum(dkt * k, axis=1, keepdims=True) * tau
    dgc = dgc + dgamma * gamma - dtau_tau
    is_last = (lax.broadcasted_iota(jnp.int32, (STACK, 1), 0) % CHUNK) == CHUNK - 1

    def last_term(h):
        s, ds_new = s_of(h), ds_new_of(h)
        dcd = jnp.sum(jnp.sum(ds_new * s, axis=1, keepdims=True), axis=0, keepdims=True)
        total = jnp.sum(dtau_tau[hr(h)], axis=0, keepdims=True) + dcd * cd[h * CHUNK:h * CHUNK + 1, :]
        return jnp.broadcast_to(total, (CHUNK, 1))

    dgc = dgc + jnp.where(is_last, _per_head(last_term), 0.0)
    dk = dk + dkb * beta
    dbeta = jnp.sum(dkb * k, axis=1, keepdims=True) + jnp.sum(dvb * v, axis=1, keepdims=True)
    dv = dvb * beta
    return dq, dk, dv, dbeta, dgc, ds


def _token_tile(t):
    return _pick(t, 256)


def _rmsnorm_fwd(x, g):
    t, d = x.shape
    tm = _token_tile(t)

    def body(x_ref, g_ref, h_ref, r_ref):
        y, r = rms_fwd(x_ref[...], g_ref[...])
        h_ref[...] = y.astype(BF16)
        r_ref[...] = r

    return _pcall(
        body, grid=(t // tm,),
        in_specs=[pl.BlockSpec((tm, d), lambda i: (i, 0)), pl.BlockSpec((1, d), lambda i: (0, 0))],
        out_specs=[pl.BlockSpec((tm, d), lambda i: (i, 0)), pl.BlockSpec((tm, 1), lambda i: (i, 0))],
        out_shape=[jax.ShapeDtypeStruct((t, d), BF16), jax.ShapeDtypeStruct((t, 1), F32)],
        compiler_params=_params("parallel"), name="rmsnorm_fwd")(x, g)


def _rmsnorm_bwd(x, r, g, dh, dres):
    t, d = x.shape
    tm = _token_tile(t)

    def body(x_ref, r_ref, g_ref, dh_ref, dres_ref, dx_ref, dxb_ref, dg_ref):
        dx, dg_rows = rms_bwd(x_ref[...], r_ref[...], g_ref[...], dh_ref[...])
        dx = dx + dres_ref[...]
        dx_ref[...] = dx
        dxb_ref[...] = dx.astype(BF16)

        @pl.when(pl.program_id(0) == 0)
        def _():
            dg_ref[...] = jnp.zeros_like(dg_ref)

        dg_ref[...] += jnp.sum(dg_rows, axis=0, keepdims=True)

    tile = pl.BlockSpec((tm, d), lambda i: (i, 0))
    row = pl.BlockSpec((1, d), lambda i: (0, 0))
    return _pcall(
        body, grid=(t // tm,),
        in_specs=[tile, pl.BlockSpec((tm, 1), lambda i: (i, 0)), row, tile, tile],
        out_specs=[tile, tile, row],
        out_shape=[jax.ShapeDtypeStruct((t, d), F32), jax.ShapeDtypeStruct((t, d), BF16), jax.ShapeDtypeStruct((1, d), F32)],
        compiler_params=_params("arbitrary"), name="rmsnorm_bwd")(x, r, g, dh, dres)


def _matmul(a, b, mode, name, tiles, add=None, out_dtype=F32):
    if mode == "nn":
        (m, k), n = a.shape, b.shape[1]
    elif mode == "nt":
        (m, k), n = a.shape, b.shape[0]
    else:
        (k, m), n = a.shape, b.shape[1]
    tm, tn, tk = min(tiles[0], m), min(tiles[1], n), min(tiles[2], k)
    assert m % tm == 0 and n % tn == 0 and k % tk == 0, (name, m, n, k, tiles)
    nk = k // tk
    dims = {"nn": (((1,), (0,)), ((), ())), "nt": (((1,), (1,)), ((), ())), "tn": (((0,), (0,)), ((), ()))}[mode]

    def finish(res, add_ref, o_ref):
        if add_ref is not None:
            res = res + add_ref[...]
        o_ref[...] = res.astype(o_ref.dtype)

    def body(*refs):
        a_ref, b_ref = refs[0], refs[1]
        add_ref = refs[2] if add is not None else None
        o_ref = refs[3] if add is not None else refs[2]
        part = lax.dot_general(a_ref[...], b_ref[...], dims, preferred_element_type=F32)
        if nk == 1:
            finish(part, add_ref, o_ref)
            return
        acc_ref = refs[-1]
        kk = pl.program_id(2)

        @pl.when(kk == 0)
        def _():
            acc_ref[...] = part

        @pl.when(kk > 0)
        def _():
            acc_ref[...] += part

        @pl.when(kk == nk - 1)
        def _():
            finish(acc_ref[...], add_ref, o_ref)

    a_spec = pl.BlockSpec((tk, tm), lambda j, i, kk: (kk, i)) if mode == "tn" else pl.BlockSpec((tm, tk), lambda j, i, kk: (i, kk))
    b_spec = pl.BlockSpec((tn, tk), lambda j, i, kk: (j, kk)) if mode == "nt" else pl.BlockSpec((tk, tn), lambda j, i, kk: (kk, j))
    o_spec = pl.BlockSpec((tm, tn), lambda j, i, kk: (i, j))
    in_specs = [a_spec, b_spec] + ([o_spec] if add is not None else [])
    args = (a, b) + ((add,) if add is not None else ())
    return _pcall(
        body, grid=(n // tn, m // tm, nk), in_specs=in_specs, out_specs=o_spec,
        out_shape=jax.ShapeDtypeStruct((m, n), out_dtype),
        scratch_shapes=[pltpu.VMEM((tm, tn), F32)] if nk > 1 else [],
        compiler_params=_params("parallel", "parallel", "arbitrary"), name=name)(*args)


def _prev_halo_spec(tm, width, col_block):
    return pl.BlockSpec((HALO, width), lambda i: (jnp.maximum(i * (tm // HALO) - 1, 0), col_block))


def _fill_with_prev(xp_ref, tile, halo, first):
    xp_ref[0:HALO, :] = jnp.where(first, 0.0, halo)
    xp_ref[HALO:, :] = tile


def _causal_conv(xp_ref, w, taps, tm):
    out = None
    for k in range(taps):
        term = xp_ref[pl.ds(HALO - (taps - 1 - k), tm), :] * w[k:k + 1, :]
        out = term if out is None else out + term
    return out


def _dn_prep(p, conv_w, a_log4, dt_bias4):
    t = p.shape[0]
    tm = _token_tile(t)
    w3 = 3 * DN_WIDTH

    def body(x_ref, halo_ref, pb_ref, pa_ref, w_ref, alog_ref, dtb_ref, q_ref, k_ref, v_ref, beta_ref, g_ref, xp_ref):
        _fill_with_prev(xp_ref, x_ref[...], halo_ref[...], pl.program_id(0) == 0)
        y = silu(_causal_conv(xp_ref, w_ref[...], CONV_K, tm))
        for h in range(N_HEADS):
            sl = slice(h * HEAD_DIM, (h + 1) * HEAD_DIM)
            qn, _ = l2_fwd(y[:, sl])
            q_ref[:, sl] = qn * (HEAD_DIM ** -0.5)
            kn, _ = l2_fwd(y[:, DN_WIDTH + h * HEAD_DIM:DN_WIDTH + (h + 1) * HEAD_DIM])
            k_ref[:, sl] = kn
        v_ref[...] = y[:, 2 * DN_WIDTH:]
        lane = lax.broadcasted_iota(jnp.int32, (tm, LANES), 1)
        head = lane < N_HEADS
        beta_ref[...] = jnp.where(head, sigmoid(pb_ref[...]), 0.0)
        g_ref[...] = jnp.where(head, -jnp.exp(alog_ref[...]) * softplus(pa_ref[...] + dtb_ref[...]), 0.0)

    tok = lambda w, cb: pl.BlockSpec((tm, w), lambda i: (i, cb))
    full = lambda a: pl.BlockSpec(a.shape, lambda i: (0, 0))
    return _pcall(
        body, grid=(t // tm,),
        in_specs=[tok(w3, 0), _prev_halo_spec(tm, w3, 0), tok(LANES, PROJ_MAIN // LANES), tok(LANES, PROJ_MAIN // LANES + 1),
                  full(conv_w), full(a_log4), full(dt_bias4)],
        out_specs=[tok(DN_WIDTH, 0)] * 3 + [tok(LANES, 0)] * 2,
        out_shape=[jax.ShapeDtypeStruct((t, DN_WIDTH), F32)] * 3 + [jax.ShapeDtypeStruct((t, LANES), F32)] * 2,
        scratch_shapes=[pltpu.VMEM((HALO + tm, w3), F32)],
        compiler_params=_params("parallel"), name="dn_prep")(p, p, p, p, conv_w, a_log4, dt_bias4)


def _dn_prep_bwd(p, conv_w, a_log4, dt_bias4, dq, dk, dv, dbeta4, dg4):
    t = p.shape[0]
    tm = _token_tile(t)
    w3 = 3 * DN_WIDTH

    def body(x_ref, halo_ref, pb_ref, pa_ref, w_ref, alog_ref, dtb_ref, dq_ref, dk_ref, dv_ref, dbeta_ref, dg_ref,
             dc_ref, dw_ref, dpb_ref, dpa_ref, dalog_ref, ddtb_ref, xp_ref):
        first = pl.program_id(0) == 0
        _fill_with_prev(xp_ref, x_ref[...], halo_ref[...], first)
        c = _causal_conv(xp_ref, w_ref[...], CONV_K, tm)
        y = silu(c)
        for h in range(N_HEADS):
            sl = slice(h * HEAD_DIM, (h + 1) * HEAD_DIM)
            slk = slice(DN_WIDTH + h * HEAD_DIM, DN_WIDTH + (h + 1) * HEAD_DIM)
            _, rq = l2_fwd(y[:, sl])
            dc_ref[:, sl] = l2_bwd(y[:, sl], rq, dq_ref[:, sl] * (HEAD_DIM ** -0.5)) * silu_grad(c[:, sl])
            _, rk = l2_fwd(y[:, slk])
            dc_ref[:, slk] = l2_bwd(y[:, slk], rk, dk_ref[:, sl]) * silu_grad(c[:, slk])
        dc_ref[:, 2 * DN_WIDTH:] = dv_ref[...] * silu_grad(c[:, 2 * DN_WIDTH:])

        @pl.when(first)
        def _():
            dw_ref[...] = jnp.zeros_like(dw_ref)
            dalog_ref[...] = jnp.zeros_like(dalog_ref)
            ddtb_ref[...] = jnp.zeros_like(ddtb_ref)

        dc = dc_ref[...]
        for k in range(CONV_K):
            shifted = xp_ref[pl.ds(HALO - (CONV_K - 1 - k), tm), :]
            dw_ref[k:k + 1, :] += jnp.sum(dc * shifted, axis=0, keepdims=True)

        lane = lax.broadcasted_iota(jnp.int32, (tm, LANES), 1)
        head = lane < N_HEADS
        beta = sigmoid(pb_ref[...])
        dpb_ref[...] = jnp.where(head, dbeta_ref[...] * beta * (1.0 - beta), 0.0).astype(BF16)
        z = pa_ref[...] + dtb_ref[...]
        neg_rate = -jnp.exp(alog_ref[...])
        dg = dg_ref[...]
        dpa = jnp.where(head, dg * neg_rate * sigmoid(z), 0.0)
        dpa_ref[...] = dpa.astype(BF16)
        g = jnp.where(head, neg_rate * softplus(z), 0.0)
        dalog_ref[...] += jnp.sum(dg * g, axis=0, keepdims=True)
        ddtb_ref[...] += jnp.sum(dpa, axis=0, keepdims=True)

    tok = lambda w, cb: pl.BlockSpec((tm, w), lambda i: (i, cb))
    full = lambda shape: pl.BlockSpec(shape, lambda i: (0, 0))
    return _pcall(
        body, grid=(t // tm,),
        in_specs=[tok(w3, 0), _prev_halo_spec(tm, w3, 0), tok(LANES, PROJ_MAIN // LANES), tok(LANES, PROJ_MAIN // LANES + 1),
                  full(conv_w.shape), full(a_log4.shape), full(dt_bias4.shape)] + [tok(DN_WIDTH, 0)] * 3 + [tok(LANES, 0)] * 2,
        out_specs=[tok(w3, 0), full((CONV_K, w3)), tok(LANES, 0), tok(LANES, 0), full((1, LANES)), full((1, LANES))],
        out_shape=[jax.ShapeDtypeStruct((t, w3), F32), jax.ShapeDtypeStruct((CONV_K, w3), F32),
                   jax.ShapeDtypeStruct((t, LANES), BF16), jax.ShapeDtypeStruct((t, LANES), BF16),
                   jax.ShapeDtypeStruct((1, LANES), F32), jax.ShapeDtypeStruct((1, LANES), F32)],
        scratch_shapes=[pltpu.VMEM((HALO + tm, w3), F32)],
        compiler_params=_params("arbitrary"), name="dn_prep_bwd")(p, p, p, p, conv_w, a_log4, dt_bias4, dq, dk, dv, dbeta4, dg4)


def _conv_bwd_input(dc, w, name):
    t, c = dc.shape
    taps = w.shape[0]
    tm = _token_tile(t)
    ct = _pick(c, 1536)
    n_tok = t // tm

    def body(dc_ref, next_ref, w_ref, dx_ref, buf_ref):
        buf_ref[0:tm, :] = dc_ref[...]
        buf_ref[tm:, :] = jnp.where(pl.program_id(0) == n_tok - 1, 0.0, next_ref[...])
        wv = w_ref[...]
        out = None
        for k in range(taps):
            term = buf_ref[pl.ds(taps - 1 - k, tm), :] * wv[k:k + 1, :]
            out = term if out is None else out + term
        dx_ref[...] = out.astype(BF16)

    return _pcall(
        body, grid=(n_tok, c // ct),
        in_specs=[pl.BlockSpec((tm, ct), lambda i, j: (i, j)),
                  pl.BlockSpec((HALO, ct), lambda i, j: (jnp.minimum((i + 1) * (tm // HALO), t // HALO - 1), j)),
                  pl.BlockSpec((taps, ct), lambda i, j: (0, j))],
        out_specs=pl.BlockSpec((tm, ct), lambda i, j: (i, j)),
        out_shape=jax.ShapeDtypeStruct((t, c), BF16),
        scratch_shapes=[pltpu.VMEM((tm + HALO, ct), F32)],
        compiler_params=_params("parallel", "parallel"), name=name)(dc, dc, w)


def _dn_forward(q, k, v, beta4, g4, p, norm_g):
    t = q.shape[0]
    n = t // CHUNK

    def body(q_ref, k_ref, v_ref, b_ref, g_ref, gate_ref, ng_ref, mix_ref, s_all_ref, s_ref):
        @pl.when(pl.program_id(0) == 0)
        def _():
            s_ref[...] = jnp.zeros_like(s_ref)

        s_all_ref[0] = s_ref[...]
        o, s_new, _ = dn_chunk_fwd(_stack_heads(q_ref[...]), _stack_heads(k_ref[...]), _stack_heads(v_ref[...]),
                                   _stack_lanes(b_ref[...]), chunk_cumsum(g_ref[...]), lambda h: s_ref[h])
        o_n, _ = rms_fwd(o, ng_ref[...])
        for h in range(N_HEADS):
            sl = slice(h * HEAD_DIM, (h + 1) * HEAD_DIM)
            s_ref[h] = s_new[h]
            mix_ref[:, sl] = (o_n[_head_rows(h)] * silu(gate_ref[:, sl])).astype(BF16)

    ch = lambda w, cb: pl.BlockSpec((CHUNK, w), lambda i: (i, cb))
    return _pcall(
        body, grid=(n,),
        in_specs=[ch(DN_WIDTH, 0)] * 3 + [ch(LANES, 0)] * 2 + [ch(DN_WIDTH, 3), pl.BlockSpec((1, HEAD_DIM), lambda i: (0, 0))],
        out_specs=[ch(DN_WIDTH, 0), pl.BlockSpec((1, N_HEADS, HEAD_DIM, HEAD_DIM), lambda i: (i, 0, 0, 0))],
        out_shape=[jax.ShapeDtypeStruct((t, DN_WIDTH), BF16), jax.ShapeDtypeStruct((n, N_HEADS, HEAD_DIM, HEAD_DIM), F32)],
        scratch_shapes=[pltpu.VMEM((N_HEADS, HEAD_DIM, HEAD_DIM), F32)],
        compiler_params=_params("arbitrary"), name="dn_forward")(q, k, v, beta4, g4, p, norm_g)


def _dn_backward(q, k, v, beta4, g4, p, norm_g, s_all, dmix):
    t = q.shape[0]
    n = t // CHUNK

    def body(q_ref, k_ref, v_ref, b_ref, g_ref, gate_ref, ng_ref, s_in_ref, dmix_ref,
             dq_ref, dk_ref, dv_ref, db_ref, dg_ref, dgate_ref, dng_ref, ds_ref):
        @pl.when(pl.program_id(0) == 0)
        def _():
            ds_ref[...] = jnp.zeros_like(ds_ref)
            dng_ref[...] = jnp.zeros_like(dng_ref)

        q, k, v, beta = _stack_heads(q_ref[...]), _stack_heads(k_ref[...]), _stack_heads(v_ref[...]), _stack_lanes(b_ref[...])
        s_of = lambda h: s_in_ref[0, h]
        o, _, loc = dn_chunk_fwd(q, k, v, beta, chunk_cumsum(g_ref[...]), s_of)
        o_n, r = rms_fwd(o, ng_ref[...])
        gate = _stack_heads(gate_ref[...])
        dmx = _stack_heads(dmix_ref[...])
        dgate = dmx * o_n * silu_grad(gate)
        do, dng_rows = rms_bwd(o, r, ng_ref[...], dmx * silu(gate))
        dng_ref[...] += jnp.sum(dng_rows, axis=0, keepdims=True)
        dq, dk, dv, dbeta, dgc, ds = dn_chunk_bwd(loc, q, k, v, beta, s_of, do, lambda h: ds_ref[h])
        lane = lax.broadcasted_iota(jnp.int32, (CHUNK, LANES), 1)
        db4 = jnp.zeros((CHUNK, LANES), F32)
        dgc4 = jnp.zeros((CHUNK, LANES), F32)
        for h in range(N_HEADS):
            sl = slice(h * HEAD_DIM, (h + 1) * HEAD_DIM)
            rows = _head_rows(h)
            dgate_ref[:, sl] = dgate[rows].astype(BF16)
            dq_ref[:, sl] = dq[rows]
            dk_ref[:, sl] = dk[rows]
            dv_ref[:, sl] = dv[rows]
            ds_ref[h] = ds[h]
            db4 = jnp.where(lane == h, dbeta[rows], db4)
            dgc4 = jnp.where(lane == h, dgc[rows], dgc4)
        _, strict = _tri_masks(CHUNK)
        db_ref[...] = db4
        dg_ref[...] = dot_nn(jnp.logical_not(strict).astype(F32), dgc4)

    rev = lambda w, cb: pl.BlockSpec((CHUNK, w), lambda i: (n - 1 - i, cb))
    return _pcall(
        body, grid=(n,),
        in_specs=[rev(DN_WIDTH, 0)] * 3 + [rev(LANES, 0)] * 2 + [rev(DN_WIDTH, 3), pl.BlockSpec((1, HEAD_DIM), lambda i: (0, 0)),
                  pl.BlockSpec((1, N_HEADS, HEAD_DIM, HEAD_DIM), lambda i: (n - 1 - i, 0, 0, 0)), rev(DN_WIDTH, 0)],
        out_specs=[rev(DN_WIDTH, 0)] * 3 + [rev(LANES, 0)] * 2 + [rev(DN_WIDTH, 0), pl.BlockSpec((1, HEAD_DIM), lambda i: (0, 0))],
        out_shape=[jax.ShapeDtypeStruct((t, DN_WIDTH), F32)] * 3 + [jax.ShapeDtypeStruct((t, LANES), F32)] * 2
        + [jax.ShapeDtypeStruct((t, DN_WIDTH), BF16), jax.ShapeDtypeStruct((1, HEAD_DIM), F32)],
        scratch_shapes=[pltpu.VMEM((N_HEADS, HEAD_DIM, HEAD_DIM), F32)],
        compiler_params=_params("arbitrary"), name="dn_backward")(q, k, v, beta4, g4, p, norm_g, s_all, dmix)


def _sg_mask():
    row = lax.broadcasted_iota(jnp.int32, (SG_BLOCK, SG_BLOCK), 0)
    col = lax.broadcasted_iota(jnp.int32, (SG_BLOCK, SG_BLOCK), 1)
    return (col // CHUNK) <= (row // CHUNK)


def _sg_forward(p, norm_g, w_s, b_t):
    t = p.shape[0]

    def body(u_ref, v_ref, ng_ref, w_ref, b_ref, o_ref):
        mask = _sg_mask()
        for g in range(SG_GROUPS):
            sl = slice(g * SG_DIM, (g + 1) * SG_DIM)
            vn, _ = rms_fwd(gelu(v_ref[:, sl]), ng_ref[:, sl])
            s = dot_nn(jnp.where(mask, w_ref[g], 0.0), vn, FAST) + b_ref[:, g:g + 1]
            o_ref[:, sl] = (gelu(u_ref[:, sl]) * s).astype(BF16)

    blk = lambda cb: pl.BlockSpec((SG_BLOCK, SG_WIDTH), lambda i: (i, cb))
    return _pcall(
        body, grid=(t // SG_BLOCK,),
        in_specs=[blk(4), blk(5), pl.BlockSpec((1, SG_WIDTH), lambda i: (0, 0)),
                  pl.BlockSpec((SG_GROUPS, SG_BLOCK, SG_BLOCK), lambda i: (0, 0, 0)), pl.BlockSpec((SG_BLOCK, SG_GROUPS), lambda i: (0, 0))],
        out_specs=blk(0), out_shape=jax.ShapeDtypeStruct((t, SG_WIDTH), BF16),
        compiler_params=_params("parallel"), name="sg_forward")(p, p, norm_g, w_s, b_t)


def _sg_backward(p, norm_g, w_s, b_t, dmix):
    t = p.shape[0]

    def body(u_ref, v_ref, ng_ref, w_ref, b_ref, do_ref, du_ref, dv_ref, dng_ref, dw_ref, db_ref):
        @pl.when(pl.program_id(0) == 0)
        def _():
            dng_ref[...] = jnp.zeros_like(dng_ref)
            dw_ref[...] = jnp.zeros_like(dw_ref)
            db_ref[...] = jnp.zeros_like(db_ref)

        mask = _sg_mask()
        lane = lax.broadcasted_iota(jnp.int32, (SG_BLOCK, LANES), 1)
        db = jnp.zeros((SG_BLOCK, LANES), F32)
        for g in range(SG_GROUPS):
            sl = slice(g * SG_DIM, (g + 1) * SG_DIM)
            u_raw, v_raw, do = u_ref[:, sl], v_ref[:, sl], do_ref[:, sl]
            vg = gelu(v_raw)
            vn, r = rms_fwd(vg, ng_ref[:, sl])
            w_m = jnp.where(mask, w_ref[g], 0.0)
            s = dot_nn(w_m, vn, FAST) + b_ref[:, g:g + 1]
            du_ref[:, sl] = (do * s * gelu_grad(u_raw)).astype(BF16)
            ds = do * gelu(u_raw)
            db = jnp.where(lane == g, jnp.sum(ds, axis=1, keepdims=True), db)
            dw_ref[g] += jnp.where(mask, dot_nt(ds, vn, FAST), 0.0)
            dvg, dng_rows = rms_bwd(vg, r, ng_ref[:, sl], dot_tn(w_m, ds, FAST))
            dng_ref[:, sl] += jnp.sum(dng_rows, axis=0, keepdims=True)
            dv_ref[:, sl] = (dvg * gelu_grad(v_raw)).astype(BF16)
        db_ref[...] += db

    blk = lambda cb: pl.BlockSpec((SG_BLOCK, SG_WIDTH), lambda i: (i, cb))
    const2 = lambda shape: pl.BlockSpec(shape, lambda i: (0, 0))
    w_spec = pl.BlockSpec((SG_GROUPS, SG_BLOCK, SG_BLOCK), lambda i: (0, 0, 0))
    return _pcall(
        body, grid=(t // SG_BLOCK,),
        in_specs=[blk(4), blk(5), const2((1, SG_WIDTH)), w_spec, const2((SG_BLOCK, SG_GROUPS)), blk(1)],
        out_specs=[blk(0), blk(0), const2((1, SG_WIDTH)), w_spec, const2((SG_BLOCK, LANES))],
        out_shape=[jax.ShapeDtypeStruct((t, SG_WIDTH), BF16)] * 2 + [jax.ShapeDtypeStruct((1, SG_WIDTH), F32),
                   jax.ShapeDtypeStruct((SG_GROUPS, SG_BLOCK, SG_BLOCK), F32), jax.ShapeDtypeStruct((SG_BLOCK, LANES), F32)],
        compiler_params=_params("arbitrary"), name="sg_backward")(p, p, norm_g, w_s, b_t, dmix)


FFN_CT = D_FF // 2


def _ffn_act(up, conv_w, conv_b):
    t = up.shape[0]
    tm = _token_tile(t)
    nj = D_FF // FFN_CT

    def body(ug_ref, uv_ref, hg_ref, hv_ref, wg_ref, wv_ref, bg_ref, bv_ref, act_ref, xg_ref, xv_ref):
        first = pl.program_id(0) == 0
        _fill_with_prev(xg_ref, ug_ref[...], hg_ref[...], first)
        _fill_with_prev(xv_ref, uv_ref[...], hv_ref[...], first)
        cg = _causal_conv(xg_ref, wg_ref[...], FFN_CONV, tm) + bg_ref[...]
        cv = _causal_conv(xv_ref, wv_ref[...], FFN_CONV, tm) + bv_ref[...]
        act_ref[...] = (silu(cg) * cv).astype(BF16)

    tok = lambda off: pl.BlockSpec((tm, FFN_CT), lambda i, j: (i, j + off))
    halo = lambda off: pl.BlockSpec((HALO, FFN_CT), lambda i, j: (jnp.maximum(i * (tm // HALO) - 1, 0), j + off))
    par = lambda rows, off: pl.BlockSpec((rows, FFN_CT), lambda i, j: (0, j + off))
    return _pcall(
        body, grid=(t // tm, nj),
        in_specs=[tok(0), tok(nj), halo(0), halo(nj), par(FFN_CONV, 0), par(FFN_CONV, nj), par(1, 0), par(1, nj)],
        out_specs=pl.BlockSpec((tm, FFN_CT), lambda i, j: (i, j)),
        out_shape=jax.ShapeDtypeStruct((t, D_FF), BF16),
        scratch_shapes=[pltpu.VMEM((HALO + tm, FFN_CT), F32)] * 2,
        compiler_params=_params("parallel", "parallel"), name="ffn_act")(up, up, up, up, conv_w, conv_w, conv_b, conv_b)


def _ffn_act_bwd(up, conv_w, conv_b, dact):
    t = up.shape[0]
    tm = _token_tile(t)
    nj = D_FF // FFN_CT

    def body(ug_ref, uv_ref, hg_ref, hv_ref, wg_ref, wv_ref, bg_ref, bv_ref, da_ref,
             dcg_ref, dcv_ref, dwg_ref, dwv_ref, dbg_ref, dbv_ref, xg_ref, xv_ref):
        first = pl.program_id(1) == 0
        _fill_with_prev(xg_ref, ug_ref[...], hg_ref[...], first)
        _fill_with_prev(xv_ref, uv_ref[...], hv_ref[...], first)
        cg = _causal_conv(xg_ref, wg_ref[...], FFN_CONV, tm) + bg_ref[...]
        cv = _causal_conv(xv_ref, wv_ref[...], FFN_CONV, tm) + bv_ref[...]
        da = da_ref[...]
        dcg = da * cv * silu_grad(cg)
        dcv = da * silu(cg)
        dcg_ref[...] = dcg
        dcv_ref[...] = dcv

        @pl.when(first)
        def _():
            dwg_ref[...] = jnp.zeros_like(dwg_ref)
            dwv_ref[...] = jnp.zeros_like(dwv_ref)
            dbg_ref[...] = jnp.zeros_like(dbg_ref)
            dbv_ref[...] = jnp.zeros_like(dbv_ref)

        dbg_ref[...] += jnp.sum(dcg, axis=0, keepdims=True)
        dbv_ref[...] += jnp.sum(dcv, axis=0, keepdims=True)
        for k in range(FFN_CONV):
            sh = pl.ds(HALO - (FFN_CONV - 1 - k), tm)
            dwg_ref[k:k + 1, :] += jnp.sum(dcg * xg_ref[sh, :], axis=0, keepdims=True)
            dwv_ref[k:k + 1, :] += jnp.sum(dcv * xv_ref[sh, :], axis=0, keepdims=True)

    tok = lambda off: pl.BlockSpec((tm, FFN_CT), lambda j, i: (i, j + off))
    halo = lambda off: pl.BlockSpec((HALO, FFN_CT), lambda j, i: (jnp.maximum(i * (tm // HALO) - 1, 0), j + off))
    par = lambda rows, off: pl.BlockSpec((rows, FFN_CT), lambda j, i: (0, j + off))
    return _pcall(
        body, grid=(nj, t // tm),
        in_specs=[tok(0), tok(nj), halo(0), halo(nj), par(FFN_CONV, 0), par(FFN_CONV, nj), par(1, 0), par(1, nj), tok(0)],
        out_specs=[tok(0), tok(0), par(FFN_CONV, 0), par(FFN_CONV, 0), par(1, 0), par(1, 0)],
        out_shape=[jax.ShapeDtypeStruct((t, D_FF), F32)] * 2 + [jax.ShapeDtypeStruct((FFN_CONV, D_FF), F32)] * 2
        + [jax.ShapeDtypeStruct((1, D_FF), F32)] * 2,
        scratch_shapes=[pltpu.VMEM((HALO + tm, FFN_CT), F32)] * 2,
        compiler_params=_params("parallel", "arbitrary"), name="ffn_act_bwd")(up, up, up, up, conv_w, conv_w, conv_b, conv_b, dact)


def _final_loss(x3, target, g):
    t, d = x3.shape
    tm = _token_tile(t)

    def body(x_ref, t_ref, g_ref, loss_ref, dx_ref, dxb_ref, dg_ref):
        @pl.when(pl.program_id(0) == 0)
        def _():
            loss_ref[...] = jnp.zeros_like(loss_ref)
            dg_ref[...] = jnp.zeros_like(dg_ref)

        x = x_ref[...]
        y, r = rms_fwd(x, g_ref[...])
        err = y - t_ref[...]
        per_tok = jnp.mean(err * err, axis=-1, keepdims=True)
        loss_ref[...] += 0.5 * jnp.sum(per_tok, axis=0, keepdims=True)
        dx, dg_rows = rms_bwd(x, r, g_ref[...], err * (1.0 / d))
        dx_ref[...] = dx
        dxb_ref[...] = dx.astype(BF16)
        dg_ref[...] += jnp.sum(dg_rows, axis=0, keepdims=True)

    tile = pl.BlockSpec((tm, d), lambda i: (i, 0))
    row = pl.BlockSpec((1, d), lambda i: (0, 0))
    return _pcall(
        body, grid=(t // tm,), in_specs=[tile, tile, row],
        out_specs=[pl.BlockSpec((1, LANES), lambda i: (0, 0)), tile, tile, row],
        out_shape=[jax.ShapeDtypeStruct((1, LANES), F32), jax.ShapeDtypeStruct((t, d), F32), jax.ShapeDtypeStruct((t, d), BF16),
                   jax.ShapeDtypeStruct((1, d), F32)],
        compiler_params=_params("arbitrary"), name="final_loss")(x3, target, g)


def _my_position():
    return lax.axis_index("x"), lax.axis_index("y"), lax.axis_index("c")


def _all_gather(slab):
    r, c = slab.shape

    def body(x_ref, out_ref, send_sems, recv_sems, local_sem):
        x, y, cc = _my_position()
        me, sibling = (x, y, cc), (x, y, 1 - cc)
        chips = [(1 - x, y), (x, 1 - y), (1 - x, 1 - y)]

        def block(px, py, pc):
            return out_ref.at[4 * px + 2 * py + pc]

        def copy(k, blk, to, src=None):
            return pltpu.make_async_remote_copy(
                src_ref=block(*blk) if src is None else src, dst_ref=block(*blk),
                send_sem=send_sems.at[k], recv_sem=recv_sems.at[k], device_id=to, device_id_type=MESH_ID)

        mine = pltpu.make_async_copy(x_ref, block(*me), local_sem)
        mine.start()
        first = [copy(0, me, sibling, src=x_ref)]
        first += [copy(1 + j, me, (*chip, cc), src=x_ref) for j, chip in enumerate(chips)]
        for cp in first:
            cp.start()
        passed = [copy(4 + j, (*chip, cc), sibling) for j, chip in enumerate(chips)]
        for j, chip in enumerate(chips):
            copy(1 + j, (*chip, cc), me).wait_recv()
            passed[j].start()
        copy(0, sibling, me).wait_recv()
        for j, chip in enumerate(chips):
            copy(4 + j, (*chip, 1 - cc), me).wait_recv()
        for cp in first + passed:
            cp.wait_send()
        mine.wait()

    return _pcall(
        body, out_shape=jax.ShapeDtypeStruct((N_DEV, r, c), slab.dtype),
        in_specs=[pl.BlockSpec(memory_space=pl.ANY)], out_specs=pl.BlockSpec(memory_space=pl.ANY),
        scratch_shapes=[pltpu.SemaphoreType.DMA((7,)), pltpu.SemaphoreType.DMA((7,)), pltpu.SemaphoreType.DMA],
        name="all_gather")(slab)


def _all_to_all(send):
    _, r, c = send.shape

    def body(send_ref, recv_ref, send_sems, recv_sems, local_sem):
        x, y, cc = _my_position()
        me = 4 * x + 2 * y + cc
        mine = pltpu.make_async_copy(send_ref.at[me], recv_ref.at[me], local_sem)
        mine.start()
        copies = []
        for rel in range(1, N_DEV):
            px, py, pc = x ^ (rel >> 2), y ^ ((rel >> 1) & 1), cc ^ (rel & 1)
            copies.append(pltpu.make_async_remote_copy(
                src_ref=send_ref.at[4 * px + 2 * py + pc], dst_ref=recv_ref.at[me],
                send_sem=send_sems.at[rel - 1], recv_sem=recv_sems.at[rel - 1],
                device_id=(px, py, pc), device_id_type=MESH_ID))
        for cp in copies:
            cp.start()
        for cp in copies:
            cp.wait()
        mine.wait()

    return _pcall(
        body, out_shape=jax.ShapeDtypeStruct(send.shape, send.dtype),
        in_specs=[pl.BlockSpec(memory_space=pl.ANY)], out_specs=pl.BlockSpec(memory_space=pl.ANY),
        scratch_shapes=[pltpu.SemaphoreType.DMA((7,)), pltpu.SemaphoreType.DMA((7,)), pltpu.SemaphoreType.DMA],
        name="all_to_all")(send)


def _sum_and_adamw(recv, w, m, v):
    _, r, c = recv.shape
    tr = SLAB_ROW_TILE
    bc1 = 1.0 - ADAM_B1 ** ADAM_STEP
    bc2 = 1.0 - ADAM_B2 ** ADAM_STEP

    def body(recv_ref, w_ref, m_ref, v_ref, g_ref, d_ref, nm_ref, nv_ref):
        g = recv_ref[0]
        for s in range(1, N_DEV):
            g = g + recv_ref[s]
        m_new = ADAM_B1 * m_ref[...] + (1.0 - ADAM_B1) * g
        v_new = ADAM_B2 * v_ref[...] + (1.0 - ADAM_B2) * (g * g)
        m_hat = m_new / bc1
        v_hat = v_new / bc2
        g_ref[...] = g
        d_ref[...] = -ADAM_LR * (m_hat / (jnp.sqrt(v_hat) + ADAM_EPS) + ADAM_WD * w_ref[...])
        nm_ref[...] = m_new
        nv_ref[...] = v_new

    tile = pl.BlockSpec((tr, c), lambda i: (i, 0))
    return _pcall(
        body, grid=(r // tr,),
        in_specs=[pl.BlockSpec((N_DEV, tr, c), lambda i: (0, i, 0)), tile, tile, tile],
        out_specs=[tile] * 4, out_shape=[jax.ShapeDtypeStruct((r, c), F32)] * 4,
        compiler_params=_params("parallel"), name="sum_and_adamw")(recv, w, m, v)


SHARDED = ("w_in", "w_up", "w_out", "w_down", "dn_conv_w", "ffn_conv_w")
REPLICATED = ("attn_norm_g", "dn_a_log", "dn_dt_bias", "dn_out_norm_g", "sg_norm_g", "sg_w", "sg_b", "ffn_norm_g",
              "ffn_conv_b", "final_norm_g")
WEIGHT_ORDER = ("attn_norm_g", "w_in", "dn_conv_w", "dn_a_log", "dn_dt_bias", "dn_out_norm_g", "sg_norm_g", "sg_w", "sg_b",
                "w_out", "ffn_norm_g", "w_up", "ffn_conv_w", "ffn_conv_b", "w_down", "final_norm_g")
SLAB_COLS = 1024
SLAB_ROW_TILE = 128


def _pad_to(flat, multiple):
    pad = (-flat.shape[-1]) % multiple
    if pad == 0:
        return flat
    return jnp.pad(flat, [(0, 0)] * (flat.ndim - 1) + [(0, pad)])


def _pack_local(named):
    flat = jnp.concatenate([named[n].reshape(-1) for n in SHARDED + REPLICATED])
    return _pad_to(flat, SLAB_ROW_TILE * SLAB_COLS).reshape(-1, SLAB_COLS)


def _unpack_local(slab, like):
    flat = slab.reshape(-1)
    out, off = {}, 0
    for n in SHARDED + REPLICATED:
        size = like[n].size
        out[n] = flat[off:off + size].reshape(like[n].shape)
        off += size
    return out


def _split_columns(full, n_local):
    r = full.shape[0]
    return full.reshape(r, N_DEV, n_local).transpose(1, 0, 2).reshape(N_DEV, r * n_local)


def _join_columns(blocks, r, n_local):
    return blocks.reshape(N_DEV, r, n_local).transpose(1, 0, 2).reshape(r, N_DEV * n_local)


def _lanes4(a):
    return jnp.pad(a.reshape(1, N_HEADS), ((0, 0), (0, LANES - N_HEADS)))


def kernel(x, attn_norm_g, w_in, dn_conv_w, dn_a_log, dn_dt_bias, dn_out_norm_g, sg_norm_g, sg_w, sg_b, w_out, ffn_norm_g, w_up, ffn_conv_w, ffn_conv_b, w_down, final_norm_g, loss_target, m_attn_norm_g, m_w_in, m_dn_conv_w, m_dn_a_log, m_dn_dt_bias, m_dn_out_norm_g, m_sg_norm_g, m_sg_w, m_sg_b, m_w_out, m_ffn_norm_g, m_w_up, m_ffn_conv_w, m_ffn_conv_b, m_w_down, m_final_norm_g, v_attn_norm_g, v_w_in, v_dn_conv_w, v_dn_a_log, v_dn_dt_bias, v_dn_out_norm_g, v_sg_norm_g, v_sg_w, v_sg_b, v_w_out, v_ffn_norm_g, v_w_up, v_ffn_conv_w, v_ffn_conv_b, v_w_down, v_final_norm_g):
    weights = dict(attn_norm_g=attn_norm_g, w_in=w_in, dn_conv_w=dn_conv_w, dn_a_log=dn_a_log, dn_dt_bias=dn_dt_bias,
                   dn_out_norm_g=dn_out_norm_g, sg_norm_g=sg_norm_g, sg_w=sg_w, sg_b=sg_b, w_out=w_out, ffn_norm_g=ffn_norm_g,
                   w_up=w_up, ffn_conv_w=ffn_conv_w, ffn_conv_b=ffn_conv_b, w_down=w_down, final_norm_g=final_norm_g)
    m_in = dict(attn_norm_g=m_attn_norm_g, w_in=m_w_in, dn_conv_w=m_dn_conv_w, dn_a_log=m_dn_a_log, dn_dt_bias=m_dn_dt_bias,
                dn_out_norm_g=m_dn_out_norm_g, sg_norm_g=m_sg_norm_g, sg_w=m_sg_w, sg_b=m_sg_b, w_out=m_w_out,
                ffn_norm_g=m_ffn_norm_g, w_up=m_w_up, ffn_conv_w=m_ffn_conv_w, ffn_conv_b=m_ffn_conv_b, w_down=m_w_down,
                final_norm_g=m_final_norm_g)
    v_in = dict(attn_norm_g=v_attn_norm_g, w_in=v_w_in, dn_conv_w=v_dn_conv_w, dn_a_log=v_dn_a_log, dn_dt_bias=v_dn_dt_bias,
                dn_out_norm_g=v_dn_out_norm_g, sg_norm_g=v_sg_norm_g, sg_w=v_sg_w, sg_b=v_sg_b, w_out=v_w_out,
                ffn_norm_g=v_ffn_norm_g, w_up=v_w_up, ffn_conv_w=v_ffn_conv_w, ffn_conv_b=v_ffn_conv_b, w_down=v_w_down,
                final_norm_g=v_final_norm_g)

    n_in, n_up = w_in.shape[2], w_up.shape[2]
    r_out, r_down = w_out.shape[1], w_down.shape[1]
    n_dnc, n_ffc = dn_conv_w.shape[2], ffn_conv_w.shape[2]
    taps = jnp.concatenate([dn_conv_w.reshape(-1), ffn_conv_w.reshape(-1)])
    flat = jnp.concatenate([w_in.reshape(-1).astype(BF16), w_up.reshape(-1).astype(BF16), w_out.reshape(-1).astype(BF16),
                            w_down.reshape(-1).astype(BF16), lax.bitcast_convert_type(taps, BF16).reshape(-1)])
    gathered = _all_gather(_pad_to(flat, 2 * SUBLANES * SLAB_COLS).reshape(-1, SLAB_COLS))
    gathered = gathered.reshape(N_DEV, -1)
    off = 0

    def take(size):
        nonlocal off
        part = gathered[:, off:off + size]
        off += size
        return part

    w_in_full = _join_columns(take(D_MODEL * n_in), D_MODEL, n_in)
    w_up_full = _join_columns(take(D_MODEL * n_up), D_MODEL, n_up)
    w_out_full = take(r_out * D_MODEL).reshape(N_DEV * r_out, D_MODEL)
    w_down_full = take(r_down * D_MODEL).reshape(N_DEV * r_down, D_MODEL)
    taps_all = lax.bitcast_convert_type(take(2 * (CONV_K * n_dnc + FFN_CONV * n_ffc)).reshape(N_DEV, -1, 2), F32)
    dn_conv_full = _join_columns(taps_all[:, :CONV_K * n_dnc], CONV_K, n_dnc)
    ffn_conv_full = _join_columns(taps_all[:, CONV_K * n_dnc:], FFN_CONV, n_ffc)
    pad_block = lambda cols: jnp.pad(cols, ((0, 0), (0, LANES - cols.shape[1])))
    w_in_p = jnp.concatenate([w_in_full[:, :PROJ_MAIN], pad_block(w_in_full[:, PROJ_MAIN:PROJ_MAIN + N_HEADS]),
                              pad_block(w_in_full[:, PROJ_MAIN + N_HEADS:])], axis=1)

    loss_lanes, grad_x, g = _local_step(
        x[0], loss_target[0], w_in_p, w_up_full, w_out_full, w_down_full, dn_conv_full, ffn_conv_full, attn_norm_g, dn_a_log,
        dn_dt_bias, dn_out_norm_g, sg_norm_g, sg_w, sg_b, ffn_norm_g, ffn_conv_b, final_norm_g)

    small = jnp.concatenate([g[n].reshape(-1) for n in REPLICATED])
    send = jnp.concatenate([
        _split_columns(g["w_in"], n_in), _split_columns(g["w_up"], n_up), g["w_out"].reshape(N_DEV, -1),
        g["w_down"].reshape(N_DEV, -1), _split_columns(g["dn_conv_w"], n_dnc), _split_columns(g["ffn_conv_w"], n_ffc),
        jnp.broadcast_to(small[None, :], (N_DEV, small.shape[0]))], axis=1)
    send = _pad_to(send, SLAB_ROW_TILE * SLAB_COLS).reshape(N_DEV, -1, SLAB_COLS)
    recv = _all_to_all(send)

    grad_s, delta_s, new_m_s, new_v_s = _sum_and_adamw(recv, _pack_local(weights), _pack_local(m_in), _pack_local(v_in))
    grads, deltas = _unpack_local(grad_s, weights), _unpack_local(delta_s, weights)
    new_m, new_v = _unpack_local(new_m_s, weights), _unpack_local(new_v_s, weights)

    loss = lax.psum(loss_lanes[0, 0], MESH_AXES)
    return (loss, grad_x[None], *[grads[n] for n in WEIGHT_ORDER], *[deltas[n] for n in WEIGHT_ORDER],
            *[new_m[n] for n in WEIGHT_ORDER], *[new_v[n] for n in WEIGHT_ORDER])


def _local_step(x2d, tgt, w_in_p, w_up_full, w_out_full, w_down_full, dn_conv_full, ffn_conv_full, attn_norm_g, dn_a_log,
                dn_dt_bias, dn_out_norm_g, sg_norm_g, sg_w, sg_b, ffn_norm_g, ffn_conv_b, final_norm_g):
    g1, g2, gf = attn_norm_g, ffn_norm_g, final_norm_g.reshape(1, D_MODEL)
    a_log4, dt_bias4 = _lanes4(dn_a_log), _lanes4(dn_dt_bias)
    sg_w3 = sg_w[0]
    sg_b_t = sg_b[0].T
    conv_b = ffn_conv_b

    h1, rstd1 = _rmsnorm_fwd(x2d, g1)
    p = _matmul(h1, w_in_p, "nn", "in_proj", (512, PROJ_PAD, D_MODEL))
    q, k, v, beta4, g4 = _dn_prep(p, dn_conv_full, a_log4, dt_bias4)
    mix_dn, s_all = _dn_forward(q, k, v, beta4, g4, p, dn_out_norm_g)
    o_sg = _sg_forward(p, sg_norm_g, sg_w3, sg_b_t)
    mix = jnp.concatenate([mix_dn, o_sg], axis=1)
    x2 = _matmul(mix, w_out_full, "nn", "out_proj", (1024, 1024, 1024), add=x2d)
    h2, rstd2 = _rmsnorm_fwd(x2, g2)
    up = _matmul(h2, w_up_full, "nn", "up_proj", (512, D_FF, D_MODEL))
    act = _ffn_act(up, ffn_conv_full, conv_b)
    x3 = _matmul(act, w_down_full, "nn", "down_proj", (512, 1024, D_FF), add=x2)
    loss_lanes, dx3, dx3b, d_gf = _final_loss(x3, tgt, gf)

    dact = _matmul(dx3b, w_down_full, "nt", "down_proj_dx", (512, D_FF, D_MODEL))
    d_w_down = _matmul(act, dx3b, "tn", "down_proj_dw", (D_FF // 2, 1024, 512))
    dc_g, dc_v, dcw_g, dcw_v, dcb_g, dcb_v = _ffn_act_bwd(up, ffn_conv_full, conv_b, dact)
    dup = jnp.concatenate([_conv_bwd_input(dc_g, ffn_conv_full[:, :D_FF], "ffn_conv_dx_gate"),
                           _conv_bwd_input(dc_v, ffn_conv_full[:, D_FF:], "ffn_conv_dx_value")], axis=1)
    dh2 = _matmul(dup, w_up_full, "nt", "up_proj_dx", (256, 1024, 2 * D_FF))
    d_w_up = _matmul(h2, dup, "tn", "up_proj_dw", (1024, D_FF // 2, 512))
    dx2, dx2b, d_g2 = _rmsnorm_bwd(x2, rstd2, g2, dh2, dx3)
    dmix = _matmul(dx2b, w_out_full, "nt", "out_proj_dx", (1024, 1024, 1024))
    d_w_out = _matmul(mix, dx2b, "tn", "out_proj_dw", (1024, 1024, 512))
    dp_u, dp_v, d_sg_norm, d_sg_w, d_sg_b_t = _sg_backward(p, sg_norm_g, sg_w3, sg_b_t, dmix)
    dq, dk, dv, dbeta4, dg4, dp_gate, d_dn_norm = _dn_backward(q, k, v, beta4, g4, p, dn_out_norm_g, s_all, dmix)
    dc_dn, d_dn_conv, dp_b, dp_a, d_a_log4, d_dt_bias4 = _dn_prep_bwd(p, dn_conv_full, a_log4, dt_bias4, dq, dk, dv, dbeta4, dg4)
    dp_qkv = _conv_bwd_input(dc_dn, dn_conv_full, "dn_conv_dx")
    dp = jnp.concatenate([dp_qkv, dp_gate, dp_u, dp_v, dp_b, dp_a], axis=1)
    dh1 = _matmul(dp, w_in_p, "nt", "in_proj_dx", (512, 1024, PROJ_PAD))
    d_w_in_p = _matmul(h1, dp, "tn", "in_proj_dw", (1024, PROJ_PAD // 2, 512))
    grad_x, _, d_g1 = _rmsnorm_bwd(x2d, rstd1, g1, dh1, dx2)
    d_w_in = jnp.concatenate([d_w_in_p[:, :PROJ_MAIN], d_w_in_p[:, PROJ_MAIN:PROJ_MAIN + N_HEADS],
                              d_w_in_p[:, PROJ_MAIN + LANES:PROJ_MAIN + LANES + N_HEADS]], axis=1)

    grads = dict(
        attn_norm_g=d_g1, w_in=d_w_in, dn_conv_w=d_dn_conv, dn_a_log=d_a_log4[:, :N_HEADS], dn_dt_bias=d_dt_bias4[:, :N_HEADS],
        dn_out_norm_g=d_dn_norm, sg_norm_g=d_sg_norm, sg_w=d_sg_w, sg_b=d_sg_b_t[:, :SG_GROUPS].T, w_out=d_w_out,
        ffn_norm_g=d_g2, w_up=d_w_up, ffn_conv_w=jnp.concatenate([dcw_g, dcw_v], axis=1),
        ffn_conv_b=jnp.concatenate([dcb_g, dcb_v], axis=1), w_down=d_w_down, final_norm_g=d_gf)
    return loss_lanes, grad_x, grads
```

```python
import functools
import math

import jax
import jax.numpy as jnp
from jax import lax
from jax.experimental import pallas as pl
from jax.experimental.pallas import tpu as pltpu

F32 = jnp.float32
BF16 = jnp.bfloat16
HI = lax.Precision.HIGHEST

D_MODEL = 1024
DN_WIDTH = 512
HEAD_DIM = 128
N_HEADS = 4
SG_WIDTH = 512
SG_GROUPS = 4
SG_DIM = 128
SG_BLOCK = 128
D_FF = 2816
CHUNK = 64
CONV_K = 4
FFN_CONV = 3
EPS = 1e-6
PROJ_COLS = 3080
PROJ_MAIN = 3072
PROJ_PAD = 3200
GELU_C = math.sqrt(2.0 / math.pi)
N_DEV = 8
LANES = 128
SUBLANES = 8
HALO = SUBLANES
VMEM_LIMIT = 48 * 1024 * 1024

ADAM_LR = 0.001
ADAM_B1 = 0.9
ADAM_B2 = 0.999
ADAM_EPS = 1e-08
ADAM_WD = 0.01
ADAM_STEP = 10

MESH_AXES = ("x", "y", "c")
MESH_ID = pl.DeviceIdType.MESH


def _pcall(body, **kw):
    return pl.pallas_call(body, **kw)


def _params(*sem):
    return pltpu.CompilerParams(dimension_semantics=sem, vmem_limit_bytes=VMEM_LIMIT)


def _pick(n, cap):
    best = None
    for t in range(LANES, cap + 1, LANES):
        if n % t == 0:
            best = t
    return best if best else n


FAST, MID, EXACT = "bf16 operands, one pass", "three bf16 passes", "six bf16 passes"


def dot_f32(a, b, dims, tier):
    if tier == FAST:
        return lax.dot_general(a.astype(BF16), b.astype(BF16), dims, preferred_element_type=F32)
    prec = lax.Precision.HIGH if tier == MID else HI
    return lax.dot_general(a, b, dims, precision=prec, preferred_element_type=F32)


def dot_nn(a, b, tier=EXACT):
    return dot_f32(a, b, (((1,), (0,)), ((), ())), tier)


def dot_nt(a, b, tier=EXACT):
    return dot_f32(a, b, (((1,), (1,)), ((), ())), tier)


def dot_tn(a, b, tier=EXACT):
    return dot_f32(a, b, (((0,), (0,)), ((), ())), tier)


def sigmoid(x):
    return 1.0 / (1.0 + jnp.exp(-x))


def silu(x):
    return x * sigmoid(x)


def silu_grad(x):
    s = sigmoid(x)
    return s * (1.0 + x * (1.0 - s))


def gelu(x):
    return 0.5 * x * (1.0 + jnp.tanh(GELU_C * (x + 0.044715 * x * x * x)))


def gelu_grad(x):
    t = jnp.tanh(GELU_C * (x + 0.044715 * x * x * x))
    return 0.5 * (1.0 + t) + 0.5 * x * (1.0 - t * t) * GELU_C * (1.0 + 3.0 * 0.044715 * x * x)


def softplus(z):
    return jnp.maximum(z, 0.0) + jnp.log(1.0 + jnp.exp(-jnp.abs(z)))


def rms_fwd(x, g):
    r = lax.rsqrt(jnp.mean(x * x, axis=-1, keepdims=True) + EPS)
    return x * r * g, r


def rms_bwd(x, r, g, dy):
    dyg = dy * g
    xr = x * r
    dx = r * (dyg - xr * jnp.mean(dyg * xr, axis=-1, keepdims=True))
    return dx, dy * xr


def l2_fwd(x):
    r = lax.rsqrt(jnp.sum(x * x, axis=-1, keepdims=True) + EPS)
    return x * r, r


def l2_bwd(x, r, dy):
    xr = x * r
    return r * (dy - xr * jnp.sum(dy * xr, axis=-1, keepdims=True))


def _tri_masks(n):
    row = lax.broadcasted_iota(jnp.int32, (n, n), 0)
    col = lax.broadcasted_iota(jnp.int32, (n, n), 1)
    return row >= col, row > col


def chunk_cumsum(g4):
    incl, _ = _tri_masks(g4.shape[0])
    return dot_nn(incl.astype(F32), g4)


STACK = N_HEADS * CHUNK


def _head_rows(h):
    return slice(h * CHUNK, (h + 1) * CHUNK)


def _stack_heads(x):
    return jnp.concatenate([x[:, h * HEAD_DIM:(h + 1) * HEAD_DIM] for h in range(N_HEADS)], axis=0)


def _stack_lanes(x4):
    return jnp.concatenate([x4[:, h:h + 1] for h in range(N_HEADS)], axis=0)


def _per_head(fn):
    return jnp.concatenate([fn(h) for h in range(N_HEADS)], axis=0)


def _unit_lower_inverse(l_strict, order):
    c = l_strict.shape[0]
    row = lax.broadcasted_iota(jnp.int32, (c, c), 0)
    col = lax.broadcasted_iota(jnp.int32, (c, c), 1)
    n = -l_strict
    a = (row == col).astype(F32) + n
    p = n
    for _ in range(int(math.log2(order)) - 1):
        p = dot_nn(p, p, FAST)
        a = a + dot_nn(a, p, FAST)
    return a


def dn_chunk_fwd(q, k, v, beta, gc4, s_of):
    row = lax.broadcasted_iota(jnp.int32, (STACK, STACK), 0)
    col = lax.broadcasted_iota(jnp.int32, (STACK, STACK), 1)
    same = (row // CHUNK) == (col // CHUNK)
    incl = jnp.logical_and(same, row >= col)
    strict = jnp.logical_and(same, row > col)
    gc_col = _stack_lanes(gc4)
    gc_row = jnp.sum(jnp.where(row == col, gc_col, 0.0), axis=0, keepdims=True)
    decay = jnp.where(incl, jnp.exp(jnp.minimum(gc_col - gc_row, 0.0)), 0.0)
    gamma = jnp.exp(gc_col)
    gc_last = jnp.concatenate([jnp.broadcast_to(gc4[CHUNK - 1:CHUNK, h:h + 1], (CHUNK, 1)) for h in range(N_HEADS)], axis=0)
    tau = jnp.exp(gc_last - gc_col)
    cd = jnp.exp(gc_last)
    kb = k * beta
    l_mat = jnp.where(strict, dot_nt(kb, k, FAST) * decay, 0.0)
    a_inv = _unit_lower_inverse(l_mat, CHUNK)
    sol = dot_nn(a_inv, jnp.concatenate([v * beta, kb * gamma], axis=1), MID)
    value, kcd = sol[:, :HEAD_DIM], sol[:, HEAD_DIM:]
    attn = jnp.where(incl, dot_nt(q, k, FAST) * decay, 0.0)
    qd = q * gamma
    kt = k * tau
    v_new = value - _per_head(lambda h: dot_nn(kcd[_head_rows(h)], s_of(h), FAST))
    o = _per_head(lambda h: dot_nn(qd[_head_rows(h)], s_of(h), FAST)) + dot_nn(attn, v_new, FAST)
    s_new = [s_of(h) * cd[h * CHUNK:h * CHUNK + 1, :] + dot_tn(kt[_head_rows(h)], v_new[_head_rows(h)], FAST)
             for h in range(N_HEADS)]
    loc = dict(decay=decay, gamma=gamma, tau=tau, cd=cd, kb=kb, l_mat=l_mat, a_inv=a_inv, sol=sol, kcd=kcd, attn=attn,
               v_new=v_new, qd=qd, kt=kt, incl=incl, strict=strict)
    return o, s_new, loc


def dn_chunk_bwd(loc, q, k, v, beta, s_of, do, ds_new_of):
    decay, gamma, tau, cd = loc["decay"], loc["gamma"], loc["tau"], loc["cd"]
    a_inv, attn, l_mat, sol = loc["a_inv"], loc["attn"], loc["l_mat"], loc["sol"]
    v_new, qd, kt, kcd, kb = loc["v_new"], loc["qd"], loc["kt"], loc["kcd"], loc["kb"]
    hr = _head_rows

    dv_new = dot_tn(attn, do, MID) + _per_head(lambda h: dot_nn(kt[hr(h)], ds_new_of(h), MID))
    dattn = jnp.where(loc["incl"], dot_nt(do, v_new, MID), 0.0)
    dqd = _per_head(lambda h: dot_nt(do[hr(h)], s_of(h), MID))
    ds = [dot_tn(qd[hr(h)], do[hr(h)], MID) + ds_new_of(h) * cd[h * CHUNK:h * CHUNK + 1, :]
          - dot_tn(kcd[hr(h)], dv_new[hr(h)], MID) for h in range(N_HEADS)]
    dkt = _per_head(lambda h: dot_nt(v_new[hr(h)], ds_new_of(h), MID))
    dkcd = -_per_head(lambda h: dot_nt(dv_new[hr(h)], s_of(h), MID))
    drhs = dot_tn(a_inv, jnp.concatenate([dv_new, dkcd], axis=1), MID)
    dvb, dkbg = drhs[:, :HEAD_DIM], drhs[:, HEAD_DIM:]
    dl = jnp.where(loc["strict"], -dot_nt(drhs, sol, MID), 0.0)
    dkk = dl * decay
    dqk = dattn * decay
    e = dl * l_mat + dattn * attn
    dgc = jnp.sum(e, axis=1, keepdims=True) - jnp.sum(e, axis=0, keepdims=True).T
    dkb = dot_nn(dkk, k, MID) + dkbg * gamma
    dk = dot_tn(dkk, kb, MID) + dot_tn(dqk, q, MID) + dkt * tau
    dq = dot_nn(dqk, k, MID) + dqd * gamma
    dgamma = jnp.sum(dkbg * kb, axis=1, keepdims=True) + jnp.sum(dqd * q, axis=1, keepdims=True)
    dtau_tau = jnp.sum(dkt * k, axis=1, keepdims=True) * tau
    dgc = dgc + dgamma * gamma - dtau_tau
    is_last = (lax.broadcasted_iota(jnp.int32, (STACK, 1), 0) % CHUNK) == CHUNK - 1

    def last_term(h):
        s, ds_new = s_of(h), ds_new_of(h)
        dcd = jnp.sum(jnp.sum(ds_new * s, axis=1, keepdims=True), axis=0, keepdims=True)
        total = jnp.sum(dtau_tau[hr(h)], axis=0, keepdims=True) + dcd * cd[h * CHUNK:h * CHUNK + 1, :]
        return jnp.broadcast_to(total, (CHUNK, 1))

    dgc = dgc + jnp.where(is_last, _per_head(last_term), 0.0)
    dk = dk + dkb * beta
    dbeta = jnp.sum(dkb * k, axis=1, keepdims=True) + jnp.sum(dvb * v, axis=1, keepdims=True)
    dv = dvb * beta
    return dq, dk, dv, dbeta, dgc, ds


def _token_tile(t):
    return _pick(t, 256)


def _rmsnorm_fwd(x, g):
    t, d = x.shape
    tm = _token_tile(t)

    def body(x_ref, g_ref, h_ref, r_ref):
        y, r = rms_fwd(x_ref[...], g_ref[...])
        h_ref[...] = y.astype(BF16)
        r_ref[...] = r

    return _pcall(
        body, grid=(t // tm,),
        in_specs=[pl.BlockSpec((tm, d), lambda i: (i, 0)), pl.BlockSpec((1, d), lambda i: (0, 0))],
        out_specs=[pl.BlockSpec((tm, d), lambda i: (i, 0)), pl.BlockSpec((tm, 1), lambda i: (i, 0))],
        out_shape=[jax.ShapeDtypeStruct((t, d), BF16), jax.ShapeDtypeStruct((t, 1), F32)],
        compiler_params=_params("parallel"), name="rmsnorm_fwd")(x, g)


def _rmsnorm_bwd(x, r, g, dh, dres):
    t, d = x.shape
    tm = _token_tile(t)

    def body(x_ref, r_ref, g_ref, dh_ref, dres_ref, dx_ref, dxb_ref, dg_ref):
        dx, dg_rows = rms_bwd(x_ref[...], r_ref[...], g_ref[...], dh_ref[...])
        dx = dx + dres_ref[...]
        dx_ref[...] = dx
        dxb_ref[...] = dx.astype(BF16)

        @pl.when(pl.program_id(0) == 0)
        def _():
            dg_ref[...] = jnp.zeros_like(dg_ref)

        dg_ref[...] += jnp.sum(dg_rows, axis=0, keepdims=True)

    tile = pl.BlockSpec((tm, d), lambda i: (i, 0))
    row = pl.BlockSpec((1, d), lambda i: (0, 0))
    return _pcall(
        body, grid=(t // tm,),
        in_specs=[tile, pl.BlockSpec((tm, 1), lambda i: (i, 0)), row, tile, tile],
        out_specs=[tile, tile, row],
        out_shape=[jax.ShapeDtypeStruct((t, d), F32), jax.ShapeDtypeStruct((t, d), BF16), jax.ShapeDtypeStruct((1, d), F32)],
        compiler_params=_params("arbitrary"), name="rmsnorm_bwd")(x, r, g, dh, dres)


def _matmul(a, b, mode, name, tiles, add=None, out_dtype=F32):
    if mode == "nn":
        (m, k), n = a.shape, b.shape[1]
    elif mode == "nt":
        (m, k), n = a.shape, b.shape[0]
    else:
        (k, m), n = a.shape, b.shape[1]
    tm, tn, tk = min(tiles[0], m), min(tiles[1], n), min(tiles[2], k)
    assert m % tm == 0 and n % tn == 0 and k % tk == 0, (name, m, n, k, tiles)
    nk = k // tk
    dims = {"nn": (((1,), (0,)), ((), ())), "nt": (((1,), (1,)), ((), ())), "tn": (((0,), (0,)), ((), ()))}[mode]

    def finish(res, add_ref, o_ref):
        if add_ref is not None:
            res = res + add_ref[...]
        o_ref[...] = res.astype(o_ref.dtype)

    def body(*refs):
        a_ref, b_ref = refs[0], refs[1]
        add_ref = refs[2] if add is not None else None
        o_ref = refs[3] if add is not None else refs[2]
        part = lax.dot_general(a_ref[...], b_ref[...], dims, preferred_element_type=F32)
        if nk == 1:
            finish(part, add_ref, o_ref)
            return
        acc_ref = refs[-1]
        kk = pl.program_id(2)

        @pl.when(kk == 0)
        def _():
            acc_ref[...] = part

        @pl.when(kk > 0)
        def _():
            acc_ref[...] += part

        @pl.when(kk == nk - 1)
        def _():
            finish(acc_ref[...], add_ref, o_ref)

    a_spec = pl.BlockSpec((tk, tm), lambda j, i, kk: (kk, i)) if mode == "tn" else pl.BlockSpec((tm, tk), lambda j, i, kk: (i, kk))
    b_spec = pl.BlockSpec((tn, tk), lambda j, i, kk: (j, kk)) if mode == "nt" else pl.BlockSpec((tk, tn), lambda j, i, kk: (kk, j))
    o_spec = pl.BlockSpec((tm, tn), lambda j, i, kk: (i, j))
    in_specs = [a_spec, b_spec] + ([o_spec] if add is not None else [])
    args = (a, b) + ((add,) if add is not None else ())
    return _pcall(
        body, grid=(n // tn, m // tm, nk), in_specs=in_specs, out_specs=o_spec,
        out_shape=jax.ShapeDtypeStruct((m, n), out_dtype),
        scratch_shapes=[pltpu.VMEM((tm, tn), F32)] if nk > 1 else [],
        compiler_params=_params("parallel", "parallel", "arbitrary"), name=name)(*args)


def _prev_halo_spec(tm, width, col_block):
    return pl.BlockSpec((HALO, width), lambda i: (jnp.maximum(i * (tm // HALO) - 1, 0), col_block))


def _fill_with_prev(xp_ref, tile, halo, first):
    xp_ref[0:HALO, :] = jnp.where(first, 0.0, halo)
    xp_ref[HALO:, :] = tile


def _causal_conv(xp_ref, w, taps, tm):
    out = None
    for k in range(taps):
        term = xp_ref[pl.ds(HALO - (taps - 1 - k), tm), :] * w[k:k + 1, :]
        out = term if out is None else out + term
    return out


def _dn_prep(p, conv_w, a_log4, dt_bias4):
    t = p.shape[0]
    tm = _token_tile(t)
    w3 = 3 * DN_WIDTH

    def body(x_ref, halo_ref, pbd_ref, w_ref, alog_ref, dtb_ref, q_ref, k_ref, v_ref, beta_ref, g_ref, xp_ref):
        _fill_with_prev(xp_ref, x_ref[...], halo_ref[...], pl.program_id(0) == 0)
        y = silu(_causal_conv(xp_ref, w_ref[...], CONV_K, tm))
        for h in range(N_HEADS):
            sl = slice(h * HEAD_DIM, (h + 1) * HEAD_DIM)
            qn, _ = l2_fwd(y[:, sl])
            q_ref[:, sl] = qn * (HEAD_DIM ** -0.5)
            kn, _ = l2_fwd(y[:, DN_WIDTH + h * HEAD_DIM:DN_WIDTH + (h + 1) * HEAD_DIM])
            k_ref[:, sl] = kn
        v_ref[...] = y[:, 2 * DN_WIDTH:]
        lane = lax.broadcasted_iota(jnp.int32, (tm, LANES), 1)
        head = lane < N_HEADS
        pbd = pbd_ref[...]
        beta_ref[...] = jnp.where(head, sigmoid(pbd), 0.0)
        a_raw = pltpu.roll(pbd, LANES - N_HEADS, 1)
        g_ref[...] = jnp.where(head, -jnp.exp(alog_ref[...]) * softplus(a_raw + dtb_ref[...]), 0.0)

    tok = lambda w, cb: pl.BlockSpec((tm, w), lambda i: (i, cb))
    full = lambda a: pl.BlockSpec(a.shape, lambda i: (0, 0))
    return _pcall(
        body, grid=(t // tm,),
        in_specs=[tok(w3, 0), _prev_halo_spec(tm, w3, 0), tok(LANES, PROJ_MAIN // LANES),
                  full(conv_w), full(a_log4), full(dt_bias4)],
        out_specs=[tok(DN_WIDTH, 0)] * 3 + [tok(LANES, 0)] * 2,
        out_shape=[jax.ShapeDtypeStruct((t, DN_WIDTH), F32)] * 3 + [jax.ShapeDtypeStruct((t, LANES), F32)] * 2,
        scratch_shapes=[pltpu.VMEM((HALO + tm, w3), F32)],
        compiler_params=_params("parallel"), name="dn_prep")(p, p, p, conv_w, a_log4, dt_bias4)


def _dn_prep_bwd(p, conv_w, a_log4, dt_bias4, dq, dk, dv, dbeta4, dg4, dp_buf):
    t = p.shape[0]
    tm = _token_tile(t)
    w3 = 3 * DN_WIDTH

    def body(x_ref, halo_ref, pbd_ref, w_ref, alog_ref, dtb_ref, dq_ref, dk_ref, dv_ref, dbeta_ref, dg_ref, _,
             dc_ref, dw_ref, dpbd_ref, dalog_ref, ddtb_ref, xp_ref):
        first = pl.program_id(0) == 0
        _fill_with_prev(xp_ref, x_ref[...], halo_ref[...], first)
        c = _causal_conv(xp_ref, w_ref[...], CONV_K, tm)
        y = silu(c)
        for h in range(N_HEADS):
            sl = slice(h * HEAD_DIM, (h + 1) * HEAD_DIM)
            slk = slice(DN_WIDTH + h * HEAD_DIM, DN_WIDTH + (h + 1) * HEAD_DIM)
            _, rq = l2_fwd(y[:, sl])
            dc_ref[:, sl] = l2_bwd(y[:, sl], rq, dq_ref[:, sl] * (HEAD_DIM ** -0.5)) * silu_grad(c[:, sl])
            _, rk = l2_fwd(y[:, slk])
            dc_ref[:, slk] = l2_bwd(y[:, slk], rk, dk_ref[:, sl]) * silu_grad(c[:, slk])
        dc_ref[:, 2 * DN_WIDTH:] = dv_ref[...] * silu_grad(c[:, 2 * DN_WIDTH:])

        @pl.when(first)
        def _():
            dw_ref[...] = jnp.zeros_like(dw_ref)
            dalog_ref[...] = jnp.zeros_like(dalog_ref)
            ddtb_ref[...] = jnp.zeros_like(ddtb_ref)

        dc = dc_ref[...]
        for k in range(CONV_K):
            shifted = xp_ref[pl.ds(HALO - (CONV_K - 1 - k), tm), :]
            dw_ref[k:k + 1, :] += jnp.sum(dc * shifted, axis=0, keepdims=True)

        lane = lax.broadcasted_iota(jnp.int32, (tm, LANES), 1)
        head = lane < N_HEADS
        pbd = pbd_ref[...]
        beta = sigmoid(pbd)
        dpb = jnp.where(head, dbeta_ref[...] * beta * (1.0 - beta), 0.0)
        z = pltpu.roll(pbd, LANES - N_HEADS, 1) + dtb_ref[...]
        neg_rate = -jnp.exp(alog_ref[...])
        dg = dg_ref[...]
        dpa = jnp.where(head, dg * neg_rate * sigmoid(z), 0.0)
        dpbd_ref[...] = (dpb + pltpu.roll(dpa, N_HEADS, 1)).astype(BF16)
        g = jnp.where(head, neg_rate * softplus(z), 0.0)
        dalog_ref[...] += jnp.sum(dg * g, axis=0, keepdims=True)
        ddtb_ref[...] += jnp.sum(dpa, axis=0, keepdims=True)

    tok = lambda w, cb: pl.BlockSpec((tm, w), lambda i: (i, cb))
    full = lambda shape: pl.BlockSpec(shape, lambda i: (0, 0))
    return _pcall(
        body, grid=(t // tm,),
        in_specs=[tok(w3, 0), _prev_halo_spec(tm, w3, 0), tok(LANES, PROJ_MAIN // LANES),
                  full(conv_w.shape), full(a_log4.shape), full(dt_bias4.shape)] + [tok(DN_WIDTH, 0)] * 3 + [tok(LANES, 0)] * 2
        + [pl.BlockSpec(memory_space=pl.ANY)],
        out_specs=[tok(w3, 0), full((CONV_K, w3)), tok(LANES, PROJ_MAIN // LANES), full((1, LANES)), full((1, LANES))],
        out_shape=[jax.ShapeDtypeStruct((t, w3), F32), jax.ShapeDtypeStruct((CONV_K, w3), F32),
                   jax.ShapeDtypeStruct(dp_buf.shape, dp_buf.dtype),
                   jax.ShapeDtypeStruct((1, LANES), F32), jax.ShapeDtypeStruct((1, LANES), F32)],
        input_output_aliases={11: 2},
        scratch_shapes=[pltpu.VMEM((HALO + tm, w3), F32)],
        compiler_params=_params("arbitrary"), name="dn_prep_bwd")(p, p, p, conv_w, a_log4, dt_bias4, dq, dk, dv, dbeta4, dg4, dp_buf)


def _conv_bwd_input(dc, w, name, out_cols=None, col_block=0, into=None):
    t, c = dc.shape
    taps = w.shape[0]
    tm = _token_tile(t)
    ct = _pick(c, 1536)
    n_tok = t // tm
    out_cols = c if out_cols is None else out_cols

    def body(dc_ref, next_ref, w_ref, *rest):
        dx_ref, buf_ref = rest[-2], rest[-1]
        buf_ref[0:tm, :] = dc_ref[...]
        buf_ref[tm:, :] = jnp.where(pl.program_id(0) == n_tok - 1, 0.0, next_ref[...])
        wv = w_ref[...]
        out = None
        for k in range(taps):
            term = buf_ref[pl.ds(taps - 1 - k, tm), :] * wv[k:k + 1, :]
            out = term if out is None else out + term
        dx_ref[...] = out.astype(BF16)

    in_specs = [pl.BlockSpec((tm, ct), lambda i, j: (i, j)),
                pl.BlockSpec((HALO, ct), lambda i, j: (jnp.minimum((i + 1) * (tm // HALO), t // HALO - 1), j)),
                pl.BlockSpec((taps, ct), lambda i, j: (0, j))]
    args = (dc, dc, w)
    aliases = {}
    if into is not None:
        in_specs.append(pl.BlockSpec(memory_space=pl.ANY))
        args += (into,)
        aliases = {3: 0}
    return _pcall(
        body, grid=(n_tok, c // ct), in_specs=in_specs,
        out_specs=pl.BlockSpec((tm, ct), lambda i, j: (i, j + col_block)),
        out_shape=jax.ShapeDtypeStruct((t, out_cols), BF16), input_output_aliases=aliases,
        scratch_shapes=[pltpu.VMEM((tm + HALO, ct), F32)],
        compiler_params=_params("parallel", "parallel"), name=name)(*args)


def _dn_forward(q, k, v, beta4, g4, p, norm_g):
    t = q.shape[0]
    n = t // CHUNK

    def body(q_ref, k_ref, v_ref, b_ref, g_ref, gate_ref, ng_ref, mix_ref, s_all_ref, s_ref):
        @pl.when(pl.program_id(0) == 0)
        def _():
            s_ref[...] = jnp.zeros_like(s_ref)

        s_all_ref[0] = s_ref[...]
        o, s_new, _ = dn_chunk_fwd(_stack_heads(q_ref[...]), _stack_heads(k_ref[...]), _stack_heads(v_ref[...]),
                                   _stack_lanes(b_ref[...]), chunk_cumsum(g_ref[...]), lambda h: s_ref[h])
        o_n, _ = rms_fwd(o, ng_ref[...])
        for h in range(N_HEADS):
            sl = slice(h * HEAD_DIM, (h + 1) * HEAD_DIM)
            s_ref[h] = s_new[h]
            mix_ref[:, sl] = (o_n[_head_rows(h)] * silu(gate_ref[:, sl])).astype(BF16)

    ch = lambda w, cb: pl.BlockSpec((CHUNK, w), lambda i: (i, cb))
    return _pcall(
        body, grid=(n,),
        in_specs=[ch(DN_WIDTH, 0)] * 3 + [ch(LANES, 0)] * 2 + [ch(DN_WIDTH, 3), pl.BlockSpec((1, HEAD_DIM), lambda i: (0, 0))],
        out_specs=[ch(DN_WIDTH, 0), pl.BlockSpec((1, N_HEADS, HEAD_DIM, HEAD_DIM), lambda i: (i, 0, 0, 0))],
        out_shape=[jax.ShapeDtypeStruct((t, DN_WIDTH + SG_WIDTH), BF16), jax.ShapeDtypeStruct((n, N_HEADS, HEAD_DIM, HEAD_DIM), F32)],
        scratch_shapes=[pltpu.VMEM((N_HEADS, HEAD_DIM, HEAD_DIM), F32)],
        compiler_params=_params("arbitrary"), name="dn_forward")(q, k, v, beta4, g4, p, norm_g)


def _dn_backward(q, k, v, beta4, g4, p, norm_g, s_all, dmix, dp_buf):
    t = q.shape[0]
    n = t // CHUNK

    def body(q_ref, k_ref, v_ref, b_ref, g_ref, gate_ref, ng_ref, s_in_ref, dmix_ref, _,
             dq_ref, dk_ref, dv_ref, db_ref, dg_ref, dgate_ref, dng_ref, ds_ref):
        @pl.when(pl.program_id(0) == 0)
        def _():
            ds_ref[...] = jnp.zeros_like(ds_ref)
            dng_ref[...] = jnp.zeros_like(dng_ref)

        q, k, v, beta = _stack_heads(q_ref[...]), _stack_heads(k_ref[...]), _stack_heads(v_ref[...]), _stack_lanes(b_ref[...])
        s_of = lambda h: s_in_ref[0, h]
        o, _, loc = dn_chunk_fwd(q, k, v, beta, chunk_cumsum(g_ref[...]), s_of)
        o_n, r = rms_fwd(o, ng_ref[...])
        gate = _stack_heads(gate_ref[...])
        dmx = _stack_heads(dmix_ref[...])
        dgate = dmx * o_n * silu_grad(gate)
        do, dng_rows = rms_bwd(o, r, ng_ref[...], dmx * silu(gate))
        dng_ref[...] += jnp.sum(dng_rows, axis=0, keepdims=True)
        dq, dk, dv, dbeta, dgc, ds = dn_chunk_bwd(loc, q, k, v, beta, s_of, do, lambda h: ds_ref[h])
        lane = lax.broadcasted_iota(jnp.int32, (CHUNK, LANES), 1)
        db4 = jnp.zeros((CHUNK, LANES), F32)
        dgc4 = jnp.zeros((CHUNK, LANES), F32)
        for h in range(N_HEADS):
            sl = slice(h * HEAD_DIM, (h + 1) * HEAD_DIM)
            rows = _head_rows(h)
            dgate_ref[:, sl] = dgate[rows].astype(BF16)
            dq_ref[:, sl] = dq[rows]
            dk_ref[:, sl] = dk[rows]
            dv_ref[:, sl] = dv[rows]
            ds_ref[h] = ds[h]
            db4 = jnp.where(lane == h, dbeta[rows], db4)
            dgc4 = jnp.where(lane == h, dgc[rows], dgc4)
        _, strict = _tri_masks(CHUNK)
        db_ref[...] = db4
        dg_ref[...] = dot_nn(jnp.logical_not(strict).astype(F32), dgc4)

    rev = lambda w, cb: pl.BlockSpec((CHUNK, w), lambda i: (n - 1 - i, cb))
    return _pcall(
        body, grid=(n,),
        in_specs=[rev(DN_WIDTH, 0)] * 3 + [rev(LANES, 0)] * 2 + [rev(DN_WIDTH, 3), pl.BlockSpec((1, HEAD_DIM), lambda i: (0, 0)),
                  pl.BlockSpec((1, N_HEADS, HEAD_DIM, HEAD_DIM), lambda i: (n - 1 - i, 0, 0, 0)), rev(DN_WIDTH, 0),
                  pl.BlockSpec(memory_space=pl.ANY)],
        out_specs=[rev(DN_WIDTH, 0)] * 3 + [rev(LANES, 0)] * 2 + [rev(DN_WIDTH, 3), pl.BlockSpec((1, HEAD_DIM), lambda i: (0, 0))],
        out_shape=[jax.ShapeDtypeStruct((t, DN_WIDTH), F32)] * 3 + [jax.ShapeDtypeStruct((t, LANES), F32)] * 2
        + [jax.ShapeDtypeStruct(dp_buf.shape, dp_buf.dtype), jax.ShapeDtypeStruct((1, HEAD_DIM), F32)],
        input_output_aliases={9: 5},
        scratch_shapes=[pltpu.VMEM((N_HEADS, HEAD_DIM, HEAD_DIM), F32)],
        compiler_params=_params("arbitrary"), name="dn_backward")(q, k, v, beta4, g4, p, norm_g, s_all, dmix, dp_buf)


def _sg_mask():
    row = lax.broadcasted_iota(jnp.int32, (SG_BLOCK, SG_BLOCK), 0)
    col = lax.broadcasted_iota(jnp.int32, (SG_BLOCK, SG_BLOCK), 1)
    return (col // CHUNK) <= (row // CHUNK)


def _sg_forward(p, norm_g, w_s, b_t, mix_buf):
    t = p.shape[0]

    def body(u_ref, v_ref, ng_ref, w_ref, b_ref, _, o_ref):
        mask = _sg_mask()
        for g in range(SG_GROUPS):
            sl = slice(g * SG_DIM, (g + 1) * SG_DIM)
            vn, _ = rms_fwd(gelu(v_ref[:, sl]), ng_ref[:, sl])
            s = dot_nn(jnp.where(mask, w_ref[g], 0.0), vn, FAST) + b_ref[:, g:g + 1]
            o_ref[:, sl] = (gelu(u_ref[:, sl]) * s).astype(BF16)

    blk = lambda cb: pl.BlockSpec((SG_BLOCK, SG_WIDTH), lambda i: (i, cb))
    return _pcall(
        body, grid=(t // SG_BLOCK,),
        in_specs=[blk(4), blk(5), pl.BlockSpec((1, SG_WIDTH), lambda i: (0, 0)),
                  pl.BlockSpec((SG_GROUPS, SG_BLOCK, SG_BLOCK), lambda i: (0, 0, 0)), pl.BlockSpec((SG_BLOCK, SG_GROUPS), lambda i: (0, 0)),
                  pl.BlockSpec(memory_space=pl.ANY)],
        out_specs=blk(1), out_shape=jax.ShapeDtypeStruct(mix_buf.shape, mix_buf.dtype), input_output_aliases={5: 0},
        compiler_params=_params("parallel"), name="sg_forward")(p, p, norm_g, w_s, b_t, mix_buf)


def _sg_backward(p, norm_g, w_s, b_t, dmix):
    t = p.shape[0]

    def body(u_ref, v_ref, ng_ref, w_ref, b_ref, do_ref, duv_ref, dng_ref, dw_ref, db_ref):
        @pl.when(pl.program_id(0) == 0)
        def _():
            dng_ref[...] = jnp.zeros_like(dng_ref)
            dw_ref[...] = jnp.zeros_like(dw_ref)
            db_ref[...] = jnp.zeros_like(db_ref)

        mask = _sg_mask()
        lane = lax.broadcasted_iota(jnp.int32, (SG_BLOCK, LANES), 1)
        db = jnp.zeros((SG_BLOCK, LANES), F32)
        for g in range(SG_GROUPS):
            sl = slice(g * SG_DIM, (g + 1) * SG_DIM)
            u_raw, v_raw, do = u_ref[:, sl], v_ref[:, sl], do_ref[:, sl]
            vg = gelu(v_raw)
            vn, r = rms_fwd(vg, ng_ref[:, sl])
            w_m = jnp.where(mask, w_ref[g], 0.0)
            s = dot_nn(w_m, vn, FAST) + b_ref[:, g:g + 1]
            duv_ref[:, sl] = (do * s * gelu_grad(u_raw)).astype(BF16)
            ds = do * gelu(u_raw)
            db = jnp.where(lane == g, jnp.sum(ds, axis=1, keepdims=True), db)
            dw_ref[g] += jnp.where(mask, dot_nt(ds, vn, FAST), 0.0)
            dvg, dng_rows = rms_bwd(vg, r, ng_ref[:, sl], dot_tn(w_m, ds, FAST))
            dng_ref[:, sl] += jnp.sum(dng_rows, axis=0, keepdims=True)
            duv_ref[:, SG_WIDTH + g * SG_DIM:SG_WIDTH + (g + 1) * SG_DIM] = (dvg * gelu_grad(v_raw)).astype(BF16)
        db_ref[...] += db

    blk = lambda cb: pl.BlockSpec((SG_BLOCK, SG_WIDTH), lambda i: (i, cb))
    const2 = lambda shape: pl.BlockSpec(shape, lambda i: (0, 0))
    w_spec = pl.BlockSpec((SG_GROUPS, SG_BLOCK, SG_BLOCK), lambda i: (0, 0, 0))
    return _pcall(
        body, grid=(t // SG_BLOCK,),
        in_specs=[blk(4), blk(5), const2((1, SG_WIDTH)), w_spec, const2((SG_BLOCK, SG_GROUPS)), blk(1)],
        out_specs=[pl.BlockSpec((SG_BLOCK, 2 * SG_WIDTH), lambda i: (i, 2)), const2((1, SG_WIDTH)), w_spec,
                   const2((SG_BLOCK, LANES))],
        out_shape=[jax.ShapeDtypeStruct((t, PROJ_PAD), BF16), jax.ShapeDtypeStruct((1, SG_WIDTH), F32),
                   jax.ShapeDtypeStruct((SG_GROUPS, SG_BLOCK, SG_BLOCK), F32), jax.ShapeDtypeStruct((SG_BLOCK, LANES), F32)],
        compiler_params=_params("arbitrary"), name="sg_backward")(p, p, norm_g, w_s, b_t, dmix)


FFN_CT = D_FF // 2


def _ffn_act(up, conv_w, conv_b):
    t = up.shape[0]
    tm = _token_tile(t)
    nj = D_FF // FFN_CT

    def body(ug_ref, uv_ref, hg_ref, hv_ref, wg_ref, wv_ref, bg_ref, bv_ref, act_ref, xg_ref, xv_ref):
        first = pl.program_id(0) == 0
        _fill_with_prev(xg_ref, ug_ref[...], hg_ref[...], first)
        _fill_with_prev(xv_ref, uv_ref[...], hv_ref[...], first)
        cg = _causal_conv(xg_ref, wg_ref[...], FFN_CONV, tm) + bg_ref[...]
        cv = _causal_conv(xv_ref, wv_ref[...], FFN_CONV, tm) + bv_ref[...]
        act_ref[...] = (silu(cg) * cv).astype(BF16)

    tok = lambda off: pl.BlockSpec((tm, FFN_CT), lambda i, j: (i, j + off))
    halo = lambda off: pl.BlockSpec((HALO, FFN_CT), lambda i, j: (jnp.maximum(i * (tm // HALO) - 1, 0), j + off))
    par = lambda rows, off: pl.BlockSpec((rows, FFN_CT), lambda i, j: (0, j + off))
    return _pcall(
        body, grid=(t // tm, nj),
        in_specs=[tok(0), tok(nj), halo(0), halo(nj), par(FFN_CONV, 0), par(FFN_CONV, nj), par(1, 0), par(1, nj)],
        out_specs=pl.BlockSpec((tm, FFN_CT), lambda i, j: (i, j)),
        out_shape=jax.ShapeDtypeStruct((t, D_FF), BF16),
        scratch_shapes=[pltpu.VMEM((HALO + tm, FFN_CT), F32)] * 2,
        compiler_params=_params("parallel", "parallel"), name="ffn_act")(up, up, up, up, conv_w, conv_w, conv_b, conv_b)


def _ffn_act_bwd(up, conv_w, conv_b, dact):
    t = up.shape[0]
    tm = _token_tile(t)
    nj = D_FF // FFN_CT

    def body(ug_ref, uv_ref, hg_ref, hv_ref, wg_ref, wv_ref, bg_ref, bv_ref, da_ref,
             dcg_ref, dcv_ref, dwg_ref, dwv_ref, dbg_ref, dbv_ref, xg_ref, xv_ref):
        first = pl.program_id(1) == 0
        _fill_with_prev(xg_ref, ug_ref[...], hg_ref[...], first)
        _fill_with_prev(xv_ref, uv_ref[...], hv_ref[...], first)
        cg = _causal_conv(xg_ref, wg_ref[...], FFN_CONV, tm) + bg_ref[...]
        cv = _causal_conv(xv_ref, wv_ref[...], FFN_CONV, tm) + bv_ref[...]
        da = da_ref[...]
        dcg = da * cv * silu_grad(cg)
        dcv = da * silu(cg)
        dcg_ref[...] = dcg
        dcv_ref[...] = dcv

        @pl.when(first)
        def _():
            dwg_ref[...] = jnp.zeros_like(dwg_ref)
            dwv_ref[...] = jnp.zeros_like(dwv_ref)
            dbg_ref[...] = jnp.zeros_like(dbg_ref)
            dbv_ref[...] = jnp.zeros_like(dbv_ref)

        dbg_ref[...] += jnp.sum(dcg, axis=0, keepdims=True)
        dbv_ref[...] += jnp.sum(dcv, axis=0, keepdims=True)
        for k in range(FFN_CONV):
            sh = pl.ds(HALO - (FFN_CONV - 1 - k), tm)
            dwg_ref[k:k + 1, :] += jnp.sum(dcg * xg_ref[sh, :], axis=0, keepdims=True)
            dwv_ref[k:k + 1, :] += jnp.sum(dcv * xv_ref[sh, :], axis=0, keepdims=True)

    tok = lambda off: pl.BlockSpec((tm, FFN_CT), lambda j, i: (i, j + off))
    halo = lambda off: pl.BlockSpec((HALO, FFN_CT), lambda j, i: (jnp.maximum(i * (tm // HALO) - 1, 0), j + off))
    par = lambda rows, off: pl.BlockSpec((rows, FFN_CT), lambda j, i: (0, j + off))
    return _pcall(
        body, grid=(nj, t // tm),
        in_specs=[tok(0), tok(nj), halo(0), halo(nj), par(FFN_CONV, 0), par(FFN_CONV, nj), par(1, 0), par(1, nj), tok(0)],
        out_specs=[tok(0), tok(0), par(FFN_CONV, 0), par(FFN_CONV, 0), par(1, 0), par(1, 0)],
        out_shape=[jax.ShapeDtypeStruct((t, D_FF), F32)] * 2 + [jax.ShapeDtypeStruct((FFN_CONV, D_FF), F32)] * 2
        + [jax.ShapeDtypeStruct((1, D_FF), F32)] * 2,
        scratch_shapes=[pltpu.VMEM((HALO + tm, FFN_CT), F32)] * 2,
        compiler_params=_params("parallel", "arbitrary"), name="ffn_act_bwd")(up, up, up, up, conv_w, conv_w, conv_b, conv_b, dact)


def _final_loss(x3, target, g):
    t, d = x3.shape
    tm = _token_tile(t)

    def body(x_ref, t_ref, g_ref, loss_ref, dx_ref, dxb_ref, dg_ref):
        @pl.when(pl.program_id(0) == 0)
        def _():
            loss_ref[...] = jnp.zeros_like(loss_ref)
            dg_ref[...] = jnp.zeros_like(dg_ref)

        x = x_ref[...]
        y, r = rms_fwd(x, g_ref[...])
        err = y - t_ref[...]
        per_tok = jnp.mean(err * err, axis=-1, keepdims=True)
        loss_ref[...] += 0.5 * jnp.sum(per_tok, axis=0, keepdims=True)
        dx, dg_rows = rms_bwd(x, r, g_ref[...], err * (1.0 / d))
        dx_ref[...] = dx
        dxb_ref[...] = dx.astype(BF16)
        dg_ref[...] += jnp.sum(dg_rows, axis=0, keepdims=True)

    tile = pl.BlockSpec((tm, d), lambda i: (i, 0))
    row = pl.BlockSpec((1, d), lambda i: (0, 0))
    return _pcall(
        body, grid=(t // tm,), in_specs=[tile, tile, row],
        out_specs=[pl.BlockSpec((1, LANES), lambda i: (0, 0)), tile, tile, row],
        out_shape=[jax.ShapeDtypeStruct((1, LANES), F32), jax.ShapeDtypeStruct((t, d), F32), jax.ShapeDtypeStruct((t, d), BF16),
                   jax.ShapeDtypeStruct((1, d), F32)],
        compiler_params=_params("arbitrary"), name="final_loss")(x3, target, g)


def _my_position():
    return lax.axis_index("x"), lax.axis_index("y"), lax.axis_index("c")


COPIES = N_DEV - 1


def _all_gather(arrays):
    n = len(arrays)

    def body(*refs):
        x_refs, out_refs = refs[:n], refs[n:2 * n]
        send_sems, recv_sems, local_sems = refs[2 * n:]
        x, y, cc = _my_position()
        me, sibling = (x, y, cc), (x, y, 1 - cc)
        chips = [(1 - x, y), (x, 1 - y), (1 - x, 1 - y)]

        def block(a, px, py, pc):
            return out_refs[a].at[4 * px + 2 * py + pc]

        def copy(a, k, blk, to, src=None):
            return pltpu.make_async_remote_copy(
                src_ref=block(a, *blk) if src is None else src, dst_ref=block(a, *blk),
                send_sem=send_sems.at[a * COPIES + k], recv_sem=recv_sems.at[a * COPIES + k],
                device_id=to, device_id_type=MESH_ID)

        mine = [pltpu.make_async_copy(x_refs[a], block(a, *me), local_sems.at[a]) for a in range(n)]
        for cp in mine:
            cp.start()
        first = []
        for a in range(n):
            first.append(copy(a, 0, me, sibling, src=x_refs[a]))
            first += [copy(a, 1 + j, me, (*chip, cc), src=x_refs[a]) for j, chip in enumerate(chips)]
        for cp in first:
            cp.start()
        passed = []
        for j, chip in enumerate(chips):
            for a in range(n):
                copy(a, 1 + j, (*chip, cc), me).wait_recv()
                passed.append(copy(a, 4 + j, (*chip, cc), sibling))
                passed[-1].start()
        for a in range(n):
            copy(a, 0, sibling, me).wait_recv()
        for j, chip in enumerate(chips):
            for a in range(n):
                copy(a, 4 + j, (*chip, 1 - cc), me).wait_recv()
        for cp in first + passed:
            cp.wait_send()
        for cp in mine:
            cp.wait()

    any_spec = pl.BlockSpec(memory_space=pl.ANY)
    return _pcall(
        body, out_shape=[jax.ShapeDtypeStruct((N_DEV,) + a.shape, a.dtype) for a in arrays],
        in_specs=[any_spec] * n, out_specs=[any_spec] * n,
        scratch_shapes=[pltpu.SemaphoreType.DMA((n * COPIES,)), pltpu.SemaphoreType.DMA((n * COPIES,)),
                        pltpu.SemaphoreType.DMA((n,))],
        name="all_gather")(*arrays)


def _all_to_all(sends):
    n = len(sends)

    def body(*refs):
        send_refs, recv_refs = refs[:n], refs[n:2 * n]
        send_sems, recv_sems, local_sems = refs[2 * n:]
        x, y, cc = _my_position()
        me = 4 * x + 2 * y + cc
        mine = [pltpu.make_async_copy(send_refs[a].at[me], recv_refs[a].at[me], local_sems.at[a]) for a in range(n)]
        for cp in mine:
            cp.start()
        copies = []
        for rel in range(1, N_DEV):
            px, py, pc = x ^ (rel >> 2), y ^ ((rel >> 1) & 1), cc ^ (rel & 1)
            for a in range(n):
                copies.append(pltpu.make_async_remote_copy(
                    src_ref=send_refs[a].at[4 * px + 2 * py + pc], dst_ref=recv_refs[a].at[me],
                    send_sem=send_sems.at[a * COPIES + rel - 1], recv_sem=recv_sems.at[a * COPIES + rel - 1],
                    device_id=(px, py, pc), device_id_type=MESH_ID))
        for cp in copies:
            cp.start()
        for cp in copies:
            cp.wait()
        for cp in mine:
            cp.wait()

    any_spec = pl.BlockSpec(memory_space=pl.ANY)
    return _pcall(
        body, out_shape=[jax.ShapeDtypeStruct(s.shape, s.dtype) for s in sends],
        in_specs=[any_spec] * n, out_specs=[any_spec] * n,
        scratch_shapes=[pltpu.SemaphoreType.DMA((n * COPIES,)), pltpu.SemaphoreType.DMA((n * COPIES,)),
                        pltpu.SemaphoreType.DMA((n,))],
        name="all_to_all")(*sends)


def _join_shards(g, n_local, out_cols, name):
    _, r, wp = g.shape
    tr = min(r, 256)

    def body(g_ref, o_ref, acc_ref):
        acc_ref[...] = jnp.zeros_like(acc_ref)
        for k in range(N_DEV):
            shift = (n_local * k) % LANES
            start = n_local * k - shift
            piece = g_ref[k].astype(F32)
            if shift:
                piece = pltpu.roll(piece, shift, 1)
            acc_ref[:, start:start + wp] += piece
        o_ref[...] = acc_ref[...].astype(BF16)

    return _pcall(
        body, grid=(r // tr,), in_specs=[pl.BlockSpec((N_DEV, tr, wp), lambda i: (0, i, 0))],
        out_specs=pl.BlockSpec((tr, out_cols), lambda i: (i, 0)), out_shape=jax.ShapeDtypeStruct((r, out_cols), BF16),
        scratch_shapes=[pltpu.VMEM((tr, out_cols), F32)], compiler_params=_params("parallel"), name=name)(g)


def _split_shards(full, n_local, wp, name):
    r, c = full.shape
    tr = min(r, 256)

    def body(x_ref, o_ref):
        lane = lax.broadcasted_iota(jnp.int32, (tr, wp), 1)
        for k in range(N_DEV):
            shift = (n_local * k) % LANES
            start = n_local * k - shift
            win = x_ref[:, start:start + wp]
            if shift:
                win = pltpu.roll(win, wp - shift, 1)
            o_ref[k] = jnp.where(lane < n_local, win, 0.0).astype(BF16)

    return _pcall(
        body, grid=(r // tr,), in_specs=[pl.BlockSpec((tr, c), lambda i: (i, 0))],
        out_specs=pl.BlockSpec((N_DEV, tr, wp), lambda i: (0, i, 0)), out_shape=jax.ShapeDtypeStruct((N_DEV, r, wp), BF16),
        compiler_params=_params("parallel"), name=name)(full)


def _sum_and_adamw(recv, w, m, v, name):
    _, r, wp = recv.shape
    c = w.shape[1]
    tr = SLAB_ROW_TILE if r % SLAB_ROW_TILE == 0 else (SLAB_ROW_TILE // 4 if r % (SLAB_ROW_TILE // 4) == 0 else r)
    bc1 = 1.0 - ADAM_B1 ** ADAM_STEP
    bc2 = 1.0 - ADAM_B2 ** ADAM_STEP

    def body(recv_ref, w_ref, m_ref, v_ref, g_ref, d_ref, nm_ref, nv_ref):
        g = recv_ref[0, :, 0:c].astype(F32)
        for s in range(1, N_DEV):
            g = g + recv_ref[s, :, 0:c].astype(F32)
        m_new = ADAM_B1 * m_ref[...] + (1.0 - ADAM_B1) * g
        v_new = ADAM_B2 * v_ref[...] + (1.0 - ADAM_B2) * (g * g)
        m_hat = m_new / bc1
        v_hat = v_new / bc2
        g_ref[...] = g
        d_ref[...] = -ADAM_LR * (m_hat / (jnp.sqrt(v_hat) + ADAM_EPS) + ADAM_WD * w_ref[...])
        nm_ref[...] = m_new
        nv_ref[...] = v_new

    tile = pl.BlockSpec((tr, c), lambda i: (i, 0))
    return _pcall(
        body, grid=(r // tr,),
        in_specs=[pl.BlockSpec((N_DEV, tr, wp), lambda i: (0, i, 0)), tile, tile, tile],
        out_specs=[tile] * 4, out_shape=[jax.ShapeDtypeStruct((r, c), F32)] * 4,
        compiler_params=_params("parallel"), name=name)(recv, w, m, v)


SHARDED_TAPS = ("dn_conv_w", "ffn_conv_w")
REPLICATED = ("attn_norm_g", "dn_a_log", "dn_dt_bias", "dn_out_norm_g", "sg_norm_g", "sg_w", "sg_b", "ffn_norm_g",
              "ffn_conv_b", "final_norm_g")
SMALL = SHARDED_TAPS + REPLICATED
WEIGHT_ORDER = ("attn_norm_g", "w_in", "dn_conv_w", "dn_a_log", "dn_dt_bias", "dn_out_norm_g", "sg_norm_g", "sg_w", "sg_b",
                "w_out", "ffn_norm_g", "w_up", "ffn_conv_w", "ffn_conv_b", "w_down", "final_norm_g")
SLAB_COLS = 1024
SLAB_ROW_TILE = 128


def _pad_to(flat, multiple):
    pad = (-flat.shape[-1]) % multiple
    if pad == 0:
        return flat
    return jnp.pad(flat, [(0, 0)] * (flat.ndim - 1) + [(0, pad)])


def _lane_padded(n):
    return -(-n // LANES) * LANES


def _pack_small(named):
    flat = jnp.concatenate([named[n].reshape(-1) for n in SMALL])
    return _pad_to(flat, SUBLANES * SLAB_COLS).reshape(-1, SLAB_COLS)


def _unpack_small(slab, like):
    flat = slab.reshape(-1)
    out, off = {}, 0
    for n in SMALL:
        size = like[n].size
        out[n] = flat[off:off + size].reshape(like[n].shape)
        off += size
    return out


def _split_columns(full, n_local):
    r = full.shape[0]
    return full.reshape(r, N_DEV, n_local).transpose(1, 0, 2).reshape(N_DEV, r * n_local)


def _join_columns(blocks, r, n_local):
    return blocks.reshape(N_DEV, r, n_local).transpose(1, 0, 2).reshape(r, N_DEV * n_local)


def _lanes4(a):
    return jnp.pad(a.reshape(1, N_HEADS), ((0, 0), (0, LANES - N_HEADS)))


def kernel(x, attn_norm_g, w_in, dn_conv_w, dn_a_log, dn_dt_bias, dn_out_norm_g, sg_norm_g, sg_w, sg_b, w_out, ffn_norm_g, w_up, ffn_conv_w, ffn_conv_b, w_down, final_norm_g, loss_target, m_attn_norm_g, m_w_in, m_dn_conv_w, m_dn_a_log, m_dn_dt_bias, m_dn_out_norm_g, m_sg_norm_g, m_sg_w, m_sg_b, m_w_out, m_ffn_norm_g, m_w_up, m_ffn_conv_w, m_ffn_conv_b, m_w_down, m_final_norm_g, v_attn_norm_g, v_w_in, v_dn_conv_w, v_dn_a_log, v_dn_dt_bias, v_dn_out_norm_g, v_sg_norm_g, v_sg_w, v_sg_b, v_w_out, v_ffn_norm_g, v_w_up, v_ffn_conv_w, v_ffn_conv_b, v_w_down, v_final_norm_g):
    weights = dict(attn_norm_g=attn_norm_g, w_in=w_in, dn_conv_w=dn_conv_w, dn_a_log=dn_a_log, dn_dt_bias=dn_dt_bias,
                   dn_out_norm_g=dn_out_norm_g, sg_norm_g=sg_norm_g, sg_w=sg_w, sg_b=sg_b, w_out=w_out, ffn_norm_g=ffn_norm_g,
                   w_up=w_up, ffn_conv_w=ffn_conv_w, ffn_conv_b=ffn_conv_b, w_down=w_down, final_norm_g=final_norm_g)
    m_in = dict(attn_norm_g=m_attn_norm_g, w_in=m_w_in, dn_conv_w=m_dn_conv_w, dn_a_log=m_dn_a_log, dn_dt_bias=m_dn_dt_bias,
                dn_out_norm_g=m_dn_out_norm_g, sg_norm_g=m_sg_norm_g, sg_w=m_sg_w, sg_b=m_sg_b, w_out=m_w_out,
                ffn_norm_g=m_ffn_norm_g, w_up=m_w_up, ffn_conv_w=m_ffn_conv_w, ffn_conv_b=m_ffn_conv_b, w_down=m_w_down,
                final_norm_g=m_final_norm_g)
    v_in = dict(attn_norm_g=v_attn_norm_g, w_in=v_w_in, dn_conv_w=v_dn_conv_w, dn_a_log=v_dn_a_log, dn_dt_bias=v_dn_dt_bias,
                dn_out_norm_g=v_dn_out_norm_g, sg_norm_g=v_sg_norm_g, sg_w=v_sg_w, sg_b=v_sg_b, w_out=v_w_out,
                ffn_norm_g=v_ffn_norm_g, w_up=v_w_up, ffn_conv_w=v_ffn_conv_w, ffn_conv_b=v_ffn_conv_b, w_down=v_w_down,
                final_norm_g=v_final_norm_g)

    n_in, n_up = w_in.shape[2], w_up.shape[2]
    r_out, r_down = w_out.shape[1], w_down.shape[1]
    n_dnc, n_ffc = dn_conv_w.shape[2], ffn_conv_w.shape[2]
    wp_in, wp_up = _lane_padded(n_in), _lane_padded(n_up)
    taps = _pad_to(jnp.concatenate([dn_conv_w.reshape(-1), ffn_conv_w.reshape(-1)]), SUBLANES * LANES).reshape(-1, LANES)
    g_in, g_up, g_out, g_down, g_taps = _all_gather([
        jnp.pad(w_in[0].astype(BF16), ((0, 0), (0, wp_in - n_in))), jnp.pad(w_up[0].astype(BF16), ((0, 0), (0, wp_up - n_up))),
        w_out[0].astype(BF16), w_down[0].astype(BF16), taps])
    w_in_p = _join_shards(g_in, n_in, PROJ_PAD, "join_w_in")
    w_up_full = _join_shards(g_up, n_up, N_DEV * n_up, "join_w_up")
    w_out_full = g_out.reshape(N_DEV * r_out, D_MODEL)
    w_down_full = g_down.reshape(N_DEV * r_down, D_MODEL)
    taps_all = g_taps.reshape(N_DEV, -1)
    dn_conv_full = _join_columns(taps_all[:, :CONV_K * n_dnc], CONV_K, n_dnc)
    ffn_conv_full = _join_columns(taps_all[:, CONV_K * n_dnc:CONV_K * n_dnc + FFN_CONV * n_ffc], FFN_CONV, n_ffc)

    loss_lanes, grad_x, g = _local_step(
        x[0], loss_target[0], w_in_p, w_up_full, w_out_full, w_down_full, dn_conv_full, ffn_conv_full, attn_norm_g, dn_a_log,
        dn_dt_bias, dn_out_norm_g, sg_norm_g, sg_w, sg_b, ffn_norm_g, ffn_conv_b, final_norm_g)

    small = jnp.concatenate([g[n].reshape(-1) for n in REPLICATED])
    small_send = jnp.concatenate([_split_columns(g["dn_conv_w"], n_dnc), _split_columns(g["ffn_conv_w"], n_ffc),
                                  jnp.broadcast_to(small[None, :], (N_DEV, small.shape[0]))], axis=1)
    small_send = _pad_to(small_send, SUBLANES * SLAB_COLS).reshape(N_DEV, -1, SLAB_COLS)
    r_in, r_up, r_o, r_dn, r_small = _all_to_all([
        _split_shards(g["w_in"], n_in, wp_in, "split_dw_in"), _split_shards(g["w_up"], n_up, wp_up, "split_dw_up"),
        g["w_out"].reshape(N_DEV, r_out, D_MODEL), g["w_down"].reshape(N_DEV, r_down, D_MODEL), small_send])

    upd = {
        "w_in": _sum_and_adamw(r_in, w_in[0], m_w_in[0], v_w_in[0], "adamw_w_in"),
        "w_up": _sum_and_adamw(r_up, w_up[0], m_w_up[0], v_w_up[0], "adamw_w_up"),
        "w_out": _sum_and_adamw(r_o, w_out[0], m_w_out[0], v_w_out[0], "adamw_w_out"),
        "w_down": _sum_and_adamw(r_dn, w_down[0], m_w_down[0], v_w_down[0], "adamw_w_down"),
    }
    small_upd = _sum_and_adamw(r_small, _pack_small(weights), _pack_small(m_in), _pack_small(v_in), "adamw_small")
    results = []
    for i in range(4):
        named = _unpack_small(small_upd[i], weights)
        named.update({n: upd[n][i][None] for n in upd})
        results.append(named)

    loss = lax.psum(loss_lanes[0, 0], MESH_AXES)
    return (loss, grad_x[None], *[r[n] for r in results for n in WEIGHT_ORDER])


def _local_step(x2d, tgt, w_in_p, w_up_full, w_out_full, w_down_full, dn_conv_full, ffn_conv_full, attn_norm_g, dn_a_log,
                dn_dt_bias, dn_out_norm_g, sg_norm_g, sg_w, sg_b, ffn_norm_g, ffn_conv_b, final_norm_g):
    g1, g2, gf = attn_norm_g, ffn_norm_g, final_norm_g.reshape(1, D_MODEL)
    a_log4, dt_bias4 = _lanes4(dn_a_log), _lanes4(dn_dt_bias)
    sg_w3 = sg_w[0]
    sg_b_t = sg_b[0].T
    conv_b = ffn_conv_b

    h1, rstd1 = _rmsnorm_fwd(x2d, g1)
    p = _matmul(h1, w_in_p, "nn", "in_proj", (512, PROJ_PAD, D_MODEL))
    q, k, v, beta4, g4 = _dn_prep(p, dn_conv_full, a_log4, dt_bias4)
    mix_half, s_all = _dn_forward(q, k, v, beta4, g4, p, dn_out_norm_g)
    mix = _sg_forward(p, sg_norm_g, sg_w3, sg_b_t, mix_half)
    x2 = _matmul(mix, w_out_full, "nn", "out_proj", (1024, 1024, 1024), add=x2d)
    h2, rstd2 = _rmsnorm_fwd(x2, g2)
    up = _matmul(h2, w_up_full, "nn", "up_proj", (512, D_FF, D_MODEL))
    act = _ffn_act(up, ffn_conv_full, conv_b)
    x3 = _matmul(act, w_down_full, "nn", "down_proj", (512, 1024, D_FF), add=x2)
    loss_lanes, dx3, dx3b, d_gf = _final_loss(x3, tgt, gf)

    dact = _matmul(dx3b, w_down_full, "nt", "down_proj_dx", (512, D_FF, D_MODEL))
    d_w_down = _matmul(act, dx3b, "tn", "down_proj_dw", (D_FF // 2, 1024, 512), out_dtype=BF16)
    dc_g, dc_v, dcw_g, dcw_v, dcb_g, dcb_v = _ffn_act_bwd(up, ffn_conv_full, conv_b, dact)
    n_ct = D_FF // _pick(D_FF, 1536)
    dup = _conv_bwd_input(dc_g, ffn_conv_full[:, :D_FF], "ffn_conv_dx_gate", out_cols=2 * D_FF)
    dup = _conv_bwd_input(dc_v, ffn_conv_full[:, D_FF:], "ffn_conv_dx_value", out_cols=2 * D_FF, col_block=n_ct, into=dup)
    dh2 = _matmul(dup, w_up_full, "nt", "up_proj_dx", (256, 1024, 2 * D_FF))
    d_w_up = _matmul(h2, dup, "tn", "up_proj_dw", (1024, D_FF // 2, 512))
    dx2, dx2b, d_g2 = _rmsnorm_bwd(x2, rstd2, g2, dh2, dx3)
    dmix = _matmul(dx2b, w_out_full, "nt", "out_proj_dx", (1024, 1024, 1024))
    d_w_out = _matmul(mix, dx2b, "tn", "out_proj_dw", (1024, 1024, 512), out_dtype=BF16)
    dp, d_sg_norm, d_sg_w, d_sg_b_t = _sg_backward(p, sg_norm_g, sg_w3, sg_b_t, dmix)
    dq, dk, dv, dbeta4, dg4, dp, d_dn_norm = _dn_backward(q, k, v, beta4, g4, p, dn_out_norm_g, s_all, dmix, dp)
    dc_dn, d_dn_conv, dp, d_a_log4, d_dt_bias4 = _dn_prep_bwd(p, dn_conv_full, a_log4, dt_bias4, dq, dk, dv, dbeta4, dg4, dp)
    dp = _conv_bwd_input(dc_dn, dn_conv_full, "dn_conv_dx", out_cols=PROJ_PAD, into=dp)
    dh1 = _matmul(dp, w_in_p, "nt", "in_proj_dx", (512, 1024, PROJ_PAD))
    d_w_in_p = _matmul(h1, dp, "tn", "in_proj_dw", (512, PROJ_PAD, 512))
    grad_x, _, d_g1 = _rmsnorm_bwd(x2d, rstd1, g1, dh1, dx2)

    grads = dict(
        attn_norm_g=d_g1, w_in=d_w_in_p, dn_conv_w=d_dn_conv, dn_a_log=d_a_log4[:, :N_HEADS], dn_dt_bias=d_dt_bias4[:, :N_HEADS],
        dn_out_norm_g=d_dn_norm, sg_norm_g=d_sg_norm, sg_w=d_sg_w, sg_b=d_sg_b_t[:, :SG_GROUPS].T, w_out=d_w_out,
        ffn_norm_g=d_g2, w_up=d_w_up, ffn_conv_w=jnp.concatenate([dcw_g, dcw_v], axis=1),
        ffn_conv_b=jnp.concatenate([dcb_g, dcb_v], axis=1), w_down=d_w_down, final_norm_g=d_gf)
    return loss_lanes, grad_x, grads
```

```python
import functools
import math

import jax
import jax.numpy as jnp
from jax import lax
from jax.experimental import pallas as pl
from jax.experimental.pallas import tpu as pltpu

F32 = jnp.float32
BF16 = jnp.bfloat16
HI = lax.Precision.HIGHEST

D_MODEL = 1024
DN_WIDTH = 512
HEAD_DIM = 128
N_HEADS = 4
SG_WIDTH = 512
SG_GROUPS = 4
SG_DIM = 128
SG_BLOCK = 128
D_FF = 2816
CHUNK = 64
CONV_K = 4
FFN_CONV = 3
EPS = 1e-6
PROJ_COLS = 3080
PROJ_MAIN = 3072
PROJ_PAD = 3200
GELU_C = math.sqrt(2.0 / math.pi)
N_DEV = 8
LANES = 128
SUBLANES = 8
HALO = SUBLANES
VMEM_LIMIT = 48 * 1024 * 1024

ADAM_LR = 0.001
ADAM_B1 = 0.9
ADAM_B2 = 0.999
ADAM_EPS = 1e-08
ADAM_WD = 0.01
ADAM_STEP = 10

MESH_AXES = ("x", "y", "c")
MESH_ID = pl.DeviceIdType.MESH


def _pcall(body, **kw):
    return pl.pallas_call(body, **kw)


def _params(*sem):
    return pltpu.CompilerParams(dimension_semantics=sem, vmem_limit_bytes=VMEM_LIMIT)


def _pick(n, cap):
    best = None
    for t in range(LANES, cap + 1, LANES):
        if n % t == 0:
            best = t
    return best if best else n


FAST, MID, EXACT = "bf16 operands, one pass", "three bf16 passes", "six bf16 passes"


def dot_f32(a, b, dims, tier):
    if tier == FAST:
        return lax.dot_general(a.astype(BF16), b.astype(BF16), dims, preferred_element_type=F32)
    prec = lax.Precision.HIGH if tier == MID else HI
    return lax.dot_general(a, b, dims, precision=prec, preferred_element_type=F32)


def dot_nn(a, b, tier=EXACT):
    return dot_f32(a, b, (((1,), (0,)), ((), ())), tier)


def dot_nt(a, b, tier=EXACT):
    return dot_f32(a, b, (((1,), (1,)), ((), ())), tier)


def dot_tn(a, b, tier=EXACT):
    return dot_f32(a, b, (((0,), (0,)), ((), ())), tier)


def sigmoid(x):
    return 1.0 / (1.0 + jnp.exp(-x))


def silu(x):
    return x * sigmoid(x)


def silu_grad(x):
    s = sigmoid(x)
    return s * (1.0 + x * (1.0 - s))


def gelu(x):
    return 0.5 * x * (1.0 + jnp.tanh(GELU_C * (x + 0.044715 * x * x * x)))


def gelu_grad(x):
    t = jnp.tanh(GELU_C * (x + 0.044715 * x * x * x))
    return 0.5 * (1.0 + t) + 0.5 * x * (1.0 - t * t) * GELU_C * (1.0 + 3.0 * 0.044715 * x * x)


def softplus(z):
    return jnp.maximum(z, 0.0) + jnp.log(1.0 + jnp.exp(-jnp.abs(z)))


def rms_fwd(x, g):
    r = lax.rsqrt(jnp.mean(x * x, axis=-1, keepdims=True) + EPS)
    return x * r * g, r


def rms_bwd(x, r, g, dy):
    dyg = dy * g
    xr = x * r
    dx = r * (dyg - xr * jnp.mean(dyg * xr, axis=-1, keepdims=True))
    return dx, dy * xr


def l2_fwd(x):
    r = lax.rsqrt(jnp.sum(x * x, axis=-1, keepdims=True) + EPS)
    return x * r, r


def l2_bwd(x, r, dy):
    xr = x * r
    return r * (dy - xr * jnp.sum(dy * xr, axis=-1, keepdims=True))


def _tri_masks(n):
    row = lax.broadcasted_iota(jnp.int32, (n, n), 0)
    col = lax.broadcasted_iota(jnp.int32, (n, n), 1)
    return row >= col, row > col


def chunk_cumsum(g4):
    incl, _ = _tri_masks(g4.shape[0])
    return dot_nn(incl.astype(F32), g4)


STACK = N_HEADS * CHUNK


def _head_rows(h):
    return slice(h * CHUNK, (h + 1) * CHUNK)


def _stack_heads(x):
    return jnp.concatenate([x[:, h * HEAD_DIM:(h + 1) * HEAD_DIM] for h in range(N_HEADS)], axis=0)


def _stack_lanes(x4):
    return jnp.concatenate([x4[:, h:h + 1] for h in range(N_HEADS)], axis=0)


def _per_head(fn):
    return jnp.concatenate([fn(h) for h in range(N_HEADS)], axis=0)


def _unit_lower_inverse(l_strict, order):
    c = l_strict.shape[0]
    row = lax.broadcasted_iota(jnp.int32, (c, c), 0)
    col = lax.broadcasted_iota(jnp.int32, (c, c), 1)
    n = -l_strict
    a = (row == col).astype(F32) + n
    p = n
    for _ in range(int(math.log2(order)) - 1):
        p = dot_nn(p, p, FAST)
        a = a + dot_nn(a, p, FAST)
    return a


def dn_chunk_fwd(q, k, v, beta, gc4, s_of):
    row = lax.broadcasted_iota(jnp.int32, (STACK, STACK), 0)
    col = lax.broadcasted_iota(jnp.int32, (STACK, STACK), 1)
    same = (row // CHUNK) == (col // CHUNK)
    incl = jnp.logical_and(same, row >= col)
    strict = jnp.logical_and(same, row > col)
    gc_col = _stack_lanes(gc4)
    gc_row = jnp.sum(jnp.where(row == col, gc_col, 0.0), axis=0, keepdims=True)
    decay = jnp.where(incl, jnp.exp(jnp.minimum(gc_col - gc_row, 0.0)), 0.0)
    gamma = jnp.exp(gc_col)
    gc_last = jnp.concatenate([jnp.broadcast_to(gc4[CHUNK - 1:CHUNK, h:h + 1], (CHUNK, 1)) for h in range(N_HEADS)], axis=0)
    tau = jnp.exp(gc_last - gc_col)
    cd = jnp.exp(gc_last)
    kb = k * beta
    l_mat = jnp.where(strict, dot_nt(kb, k, FAST) * decay, 0.0)
    a_inv = _unit_lower_inverse(l_mat, CHUNK)
    sol = dot_nn(a_inv, jnp.concatenate([v * beta, kb * gamma], axis=1), MID)
    value, kcd = sol[:, :HEAD_DIM], sol[:, HEAD_DIM:]
    attn = jnp.where(incl, dot_nt(q, k, FAST) * decay, 0.0)
    qd = q * gamma
    kt = k * tau
    v_new = value - _per_head(lambda h: dot_nn(kcd[_head_rows(h)], s_of(h), FAST))
    o = _per_head(lambda h: dot_nn(qd[_head_rows(h)], s_of(h), FAST)) + dot_nn(attn, v_new, FAST)
    s_new = [s_of(h) * cd[h * CHUNK:h * CHUNK + 1, :] + dot_tn(kt[_head_rows(h)], v_new[_head_rows(h)], FAST)
             for h in range(N_HEADS)]
    loc = dict(decay=decay, gamma=gamma, tau=tau, cd=cd, kb=kb, l_mat=l_mat, a_inv=a_inv, sol=sol, kcd=kcd, attn=attn,
               v_new=v_new, qd=qd, kt=kt, incl=incl, strict=strict)
    return o, s_new, loc


def dn_chunk_bwd(loc, q, k, v, beta, s_of, do, ds_new_of):
    decay, gamma, tau, cd = loc["decay"], loc["gamma"], loc["tau"], loc["cd"]
    a_inv, attn, l_mat, sol = loc["a_inv"], loc["attn"], loc["l_mat"], loc["sol"]
    v_new, qd, kt, kcd, kb = loc["v_new"], loc["qd"], loc["kt"], loc["kcd"], loc["kb"]
    hr = _head_rows

    dv_new = dot_tn(attn, do, MID) + _per_head(lambda h: dot_nn(kt[hr(h)], ds_new_of(h), MID))
    dattn = jnp.where(loc["incl"], dot_nt(do, v_new, MID), 0.0)
    dqd = _per_head(lambda h: dot_nt(do[hr(h)], s_of(h), MID))
    ds = [dot_tn(qd[hr(h)], do[hr(h)], MID) + ds_new_of(h) * cd[h * CHUNK:h * CHUNK + 1, :]
          - dot_tn(kcd[hr(h)], dv_new[hr(h)], MID) for h in range(N_HEADS)]
    dkt = _per_head(lambda h: dot_nt(v_new[hr(h)], ds_new_of(h), MID))
    dkcd = -_per_head(lambda h: dot_nt(dv_new[hr(h)], s_of(h), MID))
    drhs = dot_tn(a_inv, jnp.concatenate([dv_new, dkcd], axis=1), MID)
    dvb, dkbg = drhs[:, :HEAD_DIM], drhs[:, HEAD_DIM:]
    dl = jnp.where(loc["strict"], -dot_nt(drhs, sol, MID), 0.0)
    dkk = dl * decay
    dqk = dattn * decay
    e = dl * l_mat + dattn * attn
    dgc = jnp.sum(e, axis=1, keepdims=True) - jnp.sum(e, axis=0, keepdims=True).T
    dkb = dot_nn(dkk, k, MID) + dkbg * gamma
    dk = dot_tn(dkk, kb, MID) + dot_tn(dqk, q, MID) + dkt * tau
    dq = dot_nn(dqk, k, MID) + dqd * gamma
    dgamma = jnp.sum(dkbg * kb, axis=1, keepdims=True) + jnp.sum(dqd * q, axis=1, keepdims=True)
    dtau_tau = jnp.sum(dkt * k, axis=1, keepdims=True) * tau
    dgc = dgc + dgamma * gamma - dtau_tau
    is_last = (lax.broadcasted_iota(jnp.int32, (STACK, 1), 0) % CHUNK) == CHUNK - 1

    def last_term(h):
        s, ds_new = s_of(h), ds_new_of(h)
        dcd = jnp.sum(jnp.sum(ds_new * s, axis=1, keepdims=True), axis=0, keepdims=True)
        total = jnp.sum(dtau_tau[hr(h)], axis=0, keepdims=True) + dcd * cd[h * CHUNK:h * CHUNK + 1, :]
        return jnp.broadcast_to(total, (CHUNK, 1))

    dgc = dgc + jnp.where(is_last, _per_head(last_term), 0.0)
    dk = dk + dkb * beta
    dbeta = jnp.sum(dkb * k, axis=1, keepdims=True) + jnp.sum(dvb * v, axis=1, keepdims=True)
    dv = dvb * beta
    return dq, dk, dv, dbeta, dgc, ds


def _token_tile(t):
    return _pick(t, 256)


STRIP = 32
NORM_STRIP = 16


def _for_strips(n_rows, rows, fn):
    def step(r, carry):
        fn(pl.multiple_of(r * rows, rows))
        return carry

    lax.fori_loop(0, n_rows // rows, step, 0)


def _fold_rows(x):
    out = x[0:SUBLANES, :]
    for i in range(1, x.shape[0] // SUBLANES):
        out = out + x[i * SUBLANES:(i + 1) * SUBLANES, :]
    return out


def _rmsnorm_fwd(x, g):
    t, d = x.shape
    tm = _token_tile(t)

    def body(x_ref, g_ref, h_ref, r_ref):
        def strip(row0):
            rows = pl.ds(row0, NORM_STRIP)
            y, r = rms_fwd(x_ref[rows, :], g_ref[...])
            h_ref[rows, :] = y.astype(BF16)
            r_ref[rows, :] = r

        _for_strips(tm, NORM_STRIP, strip)

    return _pcall(
        body, grid=(t // tm,),
        in_specs=[pl.BlockSpec((tm, d), lambda i: (i, 0)), pl.BlockSpec((1, d), lambda i: (0, 0))],
        out_specs=[pl.BlockSpec((tm, d), lambda i: (i, 0)), pl.BlockSpec((tm, 1), lambda i: (i, 0))],
        out_shape=[jax.ShapeDtypeStruct((t, d), BF16), jax.ShapeDtypeStruct((t, 1), F32)],
        compiler_params=_params("parallel"), name="rmsnorm_fwd")(x, g)


def _rmsnorm_bwd(x, r, g, dh, dres):
    t, d = x.shape
    tm = _token_tile(t)

    def body(x_ref, r_ref, g_ref, dh_ref, dres_ref, dx_ref, dxb_ref, dg_ref, acc_ref):
        acc_ref[...] = jnp.zeros_like(acc_ref)

        def strip(row0):
            rows = pl.ds(row0, NORM_STRIP)
            dx, dg_rows = rms_bwd(x_ref[rows, :], r_ref[rows, :], g_ref[...], dh_ref[rows, :])
            dx = dx + dres_ref[rows, :]
            dx_ref[rows, :] = dx
            dxb_ref[rows, :] = dx.astype(BF16)
            acc_ref[...] += _fold_rows(dg_rows)

        _for_strips(tm, NORM_STRIP, strip)

        @pl.when(pl.program_id(0) == 0)
        def _():
            dg_ref[...] = jnp.zeros_like(dg_ref)

        dg_ref[...] += jnp.sum(acc_ref[...], axis=0, keepdims=True)

    tile = pl.BlockSpec((tm, d), lambda i: (i, 0))
    row = pl.BlockSpec((1, d), lambda i: (0, 0))
    return _pcall(
        body, grid=(t // tm,),
        in_specs=[tile, pl.BlockSpec((tm, 1), lambda i: (i, 0)), row, tile, tile],
        out_specs=[tile, tile, row],
        out_shape=[jax.ShapeDtypeStruct((t, d), F32), jax.ShapeDtypeStruct((t, d), BF16), jax.ShapeDtypeStruct((1, d), F32)],
        scratch_shapes=[pltpu.VMEM((SUBLANES, d), F32)],
        compiler_params=_params("arbitrary"), name="rmsnorm_bwd")(x, r, g, dh, dres)


def _matmul(a, b, mode, name, tiles, add=None, out_dtype=F32):
    if mode == "nn":
        (m, k), n = a.shape, b.shape[1]
    elif mode == "nt":
        (m, k), n = a.shape, b.shape[0]
    else:
        (k, m), n = a.shape, b.shape[1]
    tm, tn, tk = min(tiles[0], m), min(tiles[1], n), min(tiles[2], k)
    assert m % tm == 0 and n % tn == 0 and k % tk == 0, (name, m, n, k, tiles)
    nk = k // tk
    dims = {"nn": (((1,), (0,)), ((), ())), "nt": (((1,), (1,)), ((), ())), "tn": (((0,), (0,)), ((), ()))}[mode]

    def finish(res, add_ref, o_ref):
        if add_ref is not None:
            res = res + add_ref[...]
        o_ref[...] = res.astype(o_ref.dtype)

    def body(*refs):
        a_ref, b_ref = refs[0], refs[1]
        add_ref = refs[2] if add is not None else None
        o_ref = refs[3] if add is not None else refs[2]
        part = lax.dot_general(a_ref[...], b_ref[...], dims, preferred_element_type=F32)
        if nk == 1:
            finish(part, add_ref, o_ref)
            return
        acc_ref = refs[-1]
        kk = pl.program_id(2)

        @pl.when(kk == 0)
        def _():
            acc_ref[...] = part

        @pl.when(kk > 0)
        def _():
            acc_ref[...] += part

        @pl.when(kk == nk - 1)
        def _():
            finish(acc_ref[...], add_ref, o_ref)

    a_spec = pl.BlockSpec((tk, tm), lambda j, i, kk: (kk, i)) if mode == "tn" else pl.BlockSpec((tm, tk), lambda j, i, kk: (i, kk))
    b_spec = pl.BlockSpec((tn, tk), lambda j, i, kk: (j, kk)) if mode == "nt" else pl.BlockSpec((tk, tn), lambda j, i, kk: (kk, j))
    o_spec = pl.BlockSpec((tm, tn), lambda j, i, kk: (i, j))
    in_specs = [a_spec, b_spec] + ([o_spec] if add is not None else [])
    args = (a, b) + ((add,) if add is not None else ())
    return _pcall(
        body, grid=(n // tn, m // tm, nk), in_specs=in_specs, out_specs=o_spec,
        out_shape=jax.ShapeDtypeStruct((m, n), out_dtype),
        scratch_shapes=[pltpu.VMEM((tm, tn), F32)] if nk > 1 else [],
        compiler_params=_params("parallel", "parallel", "arbitrary"), name=name)(*args)


def _prev_halo_spec(tm, width, col_block):
    return pl.BlockSpec((HALO, width), lambda i: (jnp.maximum(i * (tm // HALO) - 1, 0), col_block))


def _fill_with_prev(xp_ref, tile, halo, first):
    xp_ref[0:HALO, :] = jnp.where(first, 0.0, halo)
    xp_ref[HALO:, :] = tile


def _delayed(xp_ref, row0, cols, taps):
    ext = xp_ref[pl.ds(row0, STRIP + HALO), cols]
    return [ext[HALO:, :]] + [pltpu.roll(ext, j, 0)[HALO:, :] for j in range(1, taps)]


def _causal_conv(delayed, w):
    taps = len(delayed)
    out = delayed[0] * w[taps - 1:taps, :]
    for j in range(1, taps):
        out = out + delayed[j] * w[taps - 1 - j:taps - j, :]
    return out


def _advanced_conv(buf_ref, row0, cols, w):
    taps = w.shape[0]
    ext = buf_ref[pl.ds(row0, STRIP + HALO), cols]
    out = ext[:STRIP, :] * w[taps - 1:taps, :]
    for j in range(1, taps):
        out = out + pltpu.roll(ext, STRIP + HALO - j, 0)[:STRIP, :] * w[taps - 1 - j:taps - j, :]
    return out


def _dn_prep(p, conv_w, a_log4, dt_bias4):
    t = p.shape[0]
    tm = _token_tile(t)
    w3 = 3 * DN_WIDTH

    def body(x_ref, halo_ref, pbd_ref, w_ref, alog_ref, dtb_ref, q_ref, k_ref, v_ref, beta_ref, g_ref, xp_ref):
        _fill_with_prev(xp_ref, x_ref[...], halo_ref[...], pl.program_id(0) == 0)

        def strip(row0):
            rows = pl.ds(row0, STRIP)
            for h in range(N_HEADS):
                sl = slice(h * HEAD_DIM, (h + 1) * HEAD_DIM)
                for part, out_ref in ((0, q_ref), (1, k_ref), (2, v_ref)):
                    cols = slice(part * DN_WIDTH + h * HEAD_DIM, part * DN_WIDTH + (h + 1) * HEAD_DIM)
                    y = silu(_causal_conv(_delayed(xp_ref, row0, cols, CONV_K), w_ref[:, cols]))
                    if part == 0:
                        y = l2_fwd(y)[0] * (HEAD_DIM ** -0.5)
                    elif part == 1:
                        y = l2_fwd(y)[0]
                    out_ref[rows, sl] = y
            head = lax.broadcasted_iota(jnp.int32, (STRIP, LANES), 1) < N_HEADS
            pbd = pbd_ref[rows, :]
            beta_ref[rows, :] = jnp.where(head, sigmoid(pbd), 0.0)
            a_raw = pltpu.roll(pbd, LANES - N_HEADS, 1)
            g_ref[rows, :] = jnp.where(head, -jnp.exp(alog_ref[...]) * softplus(a_raw + dtb_ref[...]), 0.0)

        _for_strips(tm, STRIP, strip)

    tok = lambda w, cb: pl.BlockSpec((tm, w), lambda i: (i, cb))
    full = lambda a: pl.BlockSpec(a.shape, lambda i: (0, 0))
    return _pcall(
        body, grid=(t // tm,),
        in_specs=[tok(w3, 0), _prev_halo_spec(tm, w3, 0), tok(LANES, PROJ_MAIN // LANES),
                  full(conv_w), full(a_log4), full(dt_bias4)],
        out_specs=[tok(DN_WIDTH, 0)] * 3 + [tok(LANES, 0)] * 2,
        out_shape=[jax.ShapeDtypeStruct((t, DN_WIDTH), F32)] * 3 + [jax.ShapeDtypeStruct((t, LANES), F32)] * 2,
        scratch_shapes=[pltpu.VMEM((HALO + tm, w3), F32)],
        compiler_params=_params("parallel"), name="dn_prep")(p, p, p, conv_w, a_log4, dt_bias4)


def _dn_prep_bwd(p, conv_w, a_log4, dt_bias4, dq, dk, dv, dbeta4, dg4, dp_buf):
    t = p.shape[0]
    tm = _token_tile(t)
    w3 = 3 * DN_WIDTH

    def body(x_ref, halo_ref, pbd_ref, w_ref, alog_ref, dtb_ref, dq_ref, dk_ref, dv_ref, dbeta_ref, dg_ref, _,
             dc_ref, dw_ref, dpbd_ref, dalog_ref, ddtb_ref, xp_ref, dw_acc, lane_acc):
        first = pl.program_id(0) == 0
        _fill_with_prev(xp_ref, x_ref[...], halo_ref[...], first)
        dw_acc[...] = jnp.zeros_like(dw_acc)
        lane_acc[...] = jnp.zeros_like(lane_acc)

        def strip(row0):
            rows = pl.ds(row0, STRIP)
            for h in range(N_HEADS):
                sl = slice(h * HEAD_DIM, (h + 1) * HEAD_DIM)
                for part, dy_ref in ((0, dq_ref), (1, dk_ref), (2, dv_ref)):
                    cols = slice(part * DN_WIDTH + h * HEAD_DIM, part * DN_WIDTH + (h + 1) * HEAD_DIM)
                    delayed = _delayed(xp_ref, row0, cols, CONV_K)
                    c = _causal_conv(delayed, w_ref[:, cols])
                    dy = dy_ref[rows, sl]
                    if part < 2:
                        y = silu(c)
                        _, r = l2_fwd(y)
                        dy = l2_bwd(y, r, dy * (HEAD_DIM ** -0.5) if part == 0 else dy)
                    dc = dy * silu_grad(c)
                    dc_ref[rows, cols] = dc
                    for j in range(CONV_K):
                        k = CONV_K - 1 - j
                        dw_acc[k * SUBLANES:(k + 1) * SUBLANES, cols] += _fold_rows(dc * delayed[j])
            head = lax.broadcasted_iota(jnp.int32, (STRIP, LANES), 1) < N_HEADS
            pbd = pbd_ref[rows, :]
            beta = sigmoid(pbd)
            dpb = jnp.where(head, dbeta_ref[rows, :] * beta * (1.0 - beta), 0.0)
            z = pltpu.roll(pbd, LANES - N_HEADS, 1) + dtb_ref[...]
            neg_rate = -jnp.exp(alog_ref[...])
            dg = dg_ref[rows, :]
            dpa = jnp.where(head, dg * neg_rate * sigmoid(z), 0.0)
            dpbd_ref[rows, :] = (dpb + pltpu.roll(dpa, N_HEADS, 1)).astype(BF16)
            g = jnp.where(head, neg_rate * softplus(z), 0.0)
            lane_acc[0:SUBLANES, :] += _fold_rows(dg * g)
            lane_acc[SUBLANES:, :] += _fold_rows(dpa)

        _for_strips(tm, STRIP, strip)

        @pl.when(first)
        def _():
            dw_ref[...] = jnp.zeros_like(dw_ref)
            dalog_ref[...] = jnp.zeros_like(dalog_ref)
            ddtb_ref[...] = jnp.zeros_like(ddtb_ref)

        for k in range(CONV_K):
            dw_ref[k:k + 1, :] += jnp.sum(dw_acc[k * SUBLANES:(k + 1) * SUBLANES, :], axis=0, keepdims=True)
        dalog_ref[...] += jnp.sum(lane_acc[0:SUBLANES, :], axis=0, keepdims=True)
        ddtb_ref[...] += jnp.sum(lane_acc[SUBLANES:, :], axis=0, keepdims=True)

    tok = lambda w, cb: pl.BlockSpec((tm, w), lambda i: (i, cb))
    full = lambda shape: pl.BlockSpec(shape, lambda i: (0, 0))
    return _pcall(
        body, grid=(t // tm,),
        in_specs=[tok(w3, 0), _prev_halo_spec(tm, w3, 0), tok(LANES, PROJ_MAIN // LANES),
                  full(conv_w.shape), full(a_log4.shape), full(dt_bias4.shape)] + [tok(DN_WIDTH, 0)] * 3 + [tok(LANES, 0)] * 2
        + [pl.BlockSpec(memory_space=pl.ANY)],
        out_specs=[tok(w3, 0), full((CONV_K, w3)), tok(LANES, PROJ_MAIN // LANES), full((1, LANES)), full((1, LANES))],
        out_shape=[jax.ShapeDtypeStruct((t, w3), F32), jax.ShapeDtypeStruct((CONV_K, w3), F32),
                   jax.ShapeDtypeStruct(dp_buf.shape, dp_buf.dtype),
                   jax.ShapeDtypeStruct((1, LANES), F32), jax.ShapeDtypeStruct((1, LANES), F32)],
        input_output_aliases={11: 2},
        scratch_shapes=[pltpu.VMEM((HALO + tm, w3), F32), pltpu.VMEM((CONV_K * SUBLANES, w3), F32),
                        pltpu.VMEM((2 * SUBLANES, LANES), F32)],
        compiler_params=_params("arbitrary"), name="dn_prep_bwd")(p, p, p, conv_w, a_log4, dt_bias4, dq, dk, dv, dbeta4, dg4, dp_buf)


def _conv_bwd_input(dc, w, name, out_cols=None, col_block=0, into=None):
    t, c = dc.shape
    taps = w.shape[0]
    tm = _token_tile(t)
    ct = _pick(c, 1536)
    n_tok = t // tm
    out_cols = c if out_cols is None else out_cols

    def body(dc_ref, next_ref, w_ref, *rest):
        dx_ref, buf_ref = rest[-2], rest[-1]
        buf_ref[0:tm, :] = dc_ref[...]
        buf_ref[tm:, :] = jnp.where(pl.program_id(0) == n_tok - 1, 0.0, next_ref[...])

        def strip(row0):
            for c0 in range(0, ct, LANES):
                cols = slice(c0, c0 + LANES)
                dx_ref[pl.ds(row0, STRIP), cols] = _advanced_conv(buf_ref, row0, cols, w_ref[:, cols]).astype(BF16)

        _for_strips(tm, STRIP, strip)

    in_specs = [pl.BlockSpec((tm, ct), lambda i, j: (i, j)),
                pl.BlockSpec((HALO, ct), lambda i, j: (jnp.minimum((i + 1) * (tm // HALO), t // HALO - 1), j)),
                pl.BlockSpec((taps, ct), lambda i, j: (0, j))]
    args = (dc, dc, w)
    aliases = {}
    if into is not None:
        in_specs.append(pl.BlockSpec(memory_space=pl.ANY))
        args += (into,)
        aliases = {3: 0}
    return _pcall(
        body, grid=(n_tok, c // ct), in_specs=in_specs,
        out_specs=pl.BlockSpec((tm, ct), lambda i, j: (i, j + col_block)),
        out_shape=jax.ShapeDtypeStruct((t, out_cols), BF16), input_output_aliases=aliases,
        scratch_shapes=[pltpu.VMEM((tm + HALO, ct), F32)],
        compiler_params=_params("parallel", "parallel"), name=name)(*args)


def _dn_forward(q, k, v, beta4, g4, p, norm_g):
    t = q.shape[0]
    n = t // CHUNK

    def body(q_ref, k_ref, v_ref, b_ref, g_ref, gate_ref, ng_ref, mix_ref, s_all_ref, s_ref):
        @pl.when(pl.program_id(0) == 0)
        def _():
            s_ref[...] = jnp.zeros_like(s_ref)

        s_all_ref[0] = s_ref[...]
        o, s_new, _ = dn_chunk_fwd(_stack_heads(q_ref[...]), _stack_heads(k_ref[...]), _stack_heads(v_ref[...]),
                                   _stack_lanes(b_ref[...]), chunk_cumsum(g_ref[...]), lambda h: s_ref[h])
        o_n, _ = rms_fwd(o, ng_ref[...])
        for h in range(N_HEADS):
            sl = slice(h * HEAD_DIM, (h + 1) * HEAD_DIM)
            s_ref[h] = s_new[h]
            mix_ref[:, sl] = (o_n[_head_rows(h)] * silu(gate_ref[:, sl])).astype(BF16)

    ch = lambda w, cb: pl.BlockSpec((CHUNK, w), lambda i: (i, cb))
    return _pcall(
        body, grid=(n,),
        in_specs=[ch(DN_WIDTH, 0)] * 3 + [ch(LANES, 0)] * 2 + [ch(DN_WIDTH, 3), pl.BlockSpec((1, HEAD_DIM), lambda i: (0, 0))],
        out_specs=[ch(DN_WIDTH, 0), pl.BlockSpec((1, N_HEADS, HEAD_DIM, HEAD_DIM), lambda i: (i, 0, 0, 0))],
        out_shape=[jax.ShapeDtypeStruct((t, DN_WIDTH + SG_WIDTH), BF16), jax.ShapeDtypeStruct((n, N_HEADS, HEAD_DIM, HEAD_DIM), F32)],
        scratch_shapes=[pltpu.VMEM((N_HEADS, HEAD_DIM, HEAD_DIM), F32)],
        compiler_params=_params("arbitrary"), name="dn_forward")(q, k, v, beta4, g4, p, norm_g)


def _dn_backward(q, k, v, beta4, g4, p, norm_g, s_all, dmix, dp_buf):
    t = q.shape[0]
    n = t // CHUNK

    def body(q_ref, k_ref, v_ref, b_ref, g_ref, gate_ref, ng_ref, s_in_ref, dmix_ref, _,
             dq_ref, dk_ref, dv_ref, db_ref, dg_ref, dgate_ref, dng_ref, ds_ref):
        @pl.when(pl.program_id(0) == 0)
        def _():
            ds_ref[...] = jnp.zeros_like(ds_ref)
            dng_ref[...] = jnp.zeros_like(dng_ref)

        q, k, v, beta = _stack_heads(q_ref[...]), _stack_heads(k_ref[...]), _stack_heads(v_ref[...]), _stack_lanes(b_ref[...])
        s_of = lambda h: s_in_ref[0, h]
        o, _, loc = dn_chunk_fwd(q, k, v, beta, chunk_cumsum(g_ref[...]), s_of)
        o_n, r = rms_fwd(o, ng_ref[...])
        gate = _stack_heads(gate_ref[...])
        dmx = _stack_heads(dmix_ref[...])
        dgate = dmx * o_n * silu_grad(gate)
        do, dng_rows = rms_bwd(o, r, ng_ref[...], dmx * silu(gate))
        dng_ref[...] += jnp.sum(dng_rows, axis=0, keepdims=True)
        dq, dk, dv, dbeta, dgc, ds = dn_chunk_bwd(loc, q, k, v, beta, s_of, do, lambda h: ds_ref[h])
        lane = lax.broadcasted_iota(jnp.int32, (CHUNK, LANES), 1)
        db4 = jnp.zeros((CHUNK, LANES), F32)
        dgc4 = jnp.zeros((CHUNK, LANES), F32)
        for h in range(N_HEADS):
            sl = slice(h * HEAD_DIM, (h + 1) * HEAD_DIM)
            rows = _head_rows(h)
            dgate_ref[:, sl] = dgate[rows].astype(BF16)
            dq_ref[:, sl] = dq[rows]
            dk_ref[:, sl] = dk[rows]
            dv_ref[:, sl] = dv[rows]
            ds_ref[h] = ds[h]
            db4 = jnp.where(lane == h, dbeta[rows], db4)
            dgc4 = jnp.where(lane == h, dgc[rows], dgc4)
        _, strict = _tri_masks(CHUNK)
        db_ref[...] = db4
        dg_ref[...] = dot_nn(jnp.logical_not(strict).astype(F32), dgc4)

    rev = lambda w, cb: pl.BlockSpec((CHUNK, w), lambda i: (n - 1 - i, cb))
    return _pcall(
        body, grid=(n,),
        in_specs=[rev(DN_WIDTH, 0)] * 3 + [rev(LANES, 0)] * 2 + [rev(DN_WIDTH, 3), pl.BlockSpec((1, HEAD_DIM), lambda i: (0, 0)),
                  pl.BlockSpec((1, N_HEADS, HEAD_DIM, HEAD_DIM), lambda i: (n - 1 - i, 0, 0, 0)), rev(DN_WIDTH, 0),
                  pl.BlockSpec(memory_space=pl.ANY)],
        out_specs=[rev(DN_WIDTH, 0)] * 3 + [rev(LANES, 0)] * 2 + [rev(DN_WIDTH, 3), pl.BlockSpec((1, HEAD_DIM), lambda i: (0, 0))],
        out_shape=[jax.ShapeDtypeStruct((t, DN_WIDTH), F32)] * 3 + [jax.ShapeDtypeStruct((t, LANES), F32)] * 2
        + [jax.ShapeDtypeStruct(dp_buf.shape, dp_buf.dtype), jax.ShapeDtypeStruct((1, HEAD_DIM), F32)],
        input_output_aliases={9: 5},
        scratch_shapes=[pltpu.VMEM((N_HEADS, HEAD_DIM, HEAD_DIM), F32)],
        compiler_params=_params("arbitrary"), name="dn_backward")(q, k, v, beta4, g4, p, norm_g, s_all, dmix, dp_buf)


def _sg_mask():
    row = lax.broadcasted_iota(jnp.int32, (SG_BLOCK, SG_BLOCK), 0)
    col = lax.broadcasted_iota(jnp.int32, (SG_BLOCK, SG_BLOCK), 1)
    return (col // CHUNK) <= (row // CHUNK)


def _sg_forward(p, norm_g, w_s, b_t, mix_buf):
    t = p.shape[0]

    def body(u_ref, v_ref, ng_ref, w_ref, b_ref, _, o_ref):
        mask = _sg_mask()
        for g in range(SG_GROUPS):
            sl = slice(g * SG_DIM, (g + 1) * SG_DIM)
            vn, _ = rms_fwd(gelu(v_ref[:, sl]), ng_ref[:, sl])
            s = dot_nn(jnp.where(mask, w_ref[g], 0.0), vn, FAST) + b_ref[:, g:g + 1]
            o_ref[:, sl] = (gelu(u_ref[:, sl]) * s).astype(BF16)

    blk = lambda cb: pl.BlockSpec((SG_BLOCK, SG_WIDTH), lambda i: (i, cb))
    return _pcall(
        body, grid=(t // SG_BLOCK,),
        in_specs=[blk(4), blk(5), pl.BlockSpec((1, SG_WIDTH), lambda i: (0, 0)),
                  pl.BlockSpec((SG_GROUPS, SG_BLOCK, SG_BLOCK), lambda i: (0, 0, 0)), pl.BlockSpec((SG_BLOCK, SG_GROUPS), lambda i: (0, 0)),
                  pl.BlockSpec(memory_space=pl.ANY)],
        out_specs=blk(1), out_shape=jax.ShapeDtypeStruct(mix_buf.shape, mix_buf.dtype), input_output_aliases={5: 0},
        compiler_params=_params("parallel"), name="sg_forward")(p, p, norm_g, w_s, b_t, mix_buf)


def _sg_backward(p, norm_g, w_s, b_t, dmix):
    t = p.shape[0]

    def body(u_ref, v_ref, ng_ref, w_ref, b_ref, do_ref, duv_ref, dng_ref, dw_ref, db_ref):
        @pl.when(pl.program_id(0) == 0)
        def _():
            dng_ref[...] = jnp.zeros_like(dng_ref)
            dw_ref[...] = jnp.zeros_like(dw_ref)
            db_ref[...] = jnp.zeros_like(db_ref)

        mask = _sg_mask()
        lane = lax.broadcasted_iota(jnp.int32, (SG_BLOCK, LANES), 1)
        db = jnp.zeros((SG_BLOCK, LANES), F32)
        for g in range(SG_GROUPS):
            sl = slice(g * SG_DIM, (g + 1) * SG_DIM)
            u_raw, v_raw, do = u_ref[:, sl], v_ref[:, sl], do_ref[:, sl]
            vg = gelu(v_raw)
            vn, r = rms_fwd(vg, ng_ref[:, sl])
            w_m = jnp.where(mask, w_ref[g], 0.0)
            s = dot_nn(w_m, vn, FAST) + b_ref[:, g:g + 1]
            duv_ref[:, sl] = (do * s * gelu_grad(u_raw)).astype(BF16)
            ds = do * gelu(u_raw)
            db = jnp.where(lane == g, jnp.sum(ds, axis=1, keepdims=True), db)
            dw_ref[g] += jnp.where(mask, dot_nt(ds, vn, FAST), 0.0)
            dvg, dng_rows = rms_bwd(vg, r, ng_ref[:, sl], dot_tn(w_m, ds, FAST))
            dng_ref[:, sl] += jnp.sum(dng_rows, axis=0, keepdims=True)
            duv_ref[:, SG_WIDTH + g * SG_DIM:SG_WIDTH + (g + 1) * SG_DIM] = (dvg * gelu_grad(v_raw)).astype(BF16)
        db_ref[...] += db

    blk = lambda cb: pl.BlockSpec((SG_BLOCK, SG_WIDTH), lambda i: (i, cb))
    const2 = lambda shape: pl.BlockSpec(shape, lambda i: (0, 0))
    w_spec = pl.BlockSpec((SG_GROUPS, SG_BLOCK, SG_BLOCK), lambda i: (0, 0, 0))
    return _pcall(
        body, grid=(t // SG_BLOCK,),
        in_specs=[blk(4), blk(5), const2((1, SG_WIDTH)), w_spec, const2((SG_BLOCK, SG_GROUPS)), blk(1)],
        out_specs=[pl.BlockSpec((SG_BLOCK, 2 * SG_WIDTH), lambda i: (i, 2)), const2((1, SG_WIDTH)), w_spec,
                   const2((SG_BLOCK, LANES))],
        out_shape=[jax.ShapeDtypeStruct((t, PROJ_PAD), BF16), jax.ShapeDtypeStruct((1, SG_WIDTH), F32),
                   jax.ShapeDtypeStruct((SG_GROUPS, SG_BLOCK, SG_BLOCK), F32), jax.ShapeDtypeStruct((SG_BLOCK, LANES), F32)],
        compiler_params=_params("arbitrary"), name="sg_backward")(p, p, norm_g, w_s, b_t, dmix)


FFN_CT = D_FF // 2


def _ffn_act(up, conv_w, conv_b):
    t = up.shape[0]
    tm = _token_tile(t)
    nj = D_FF // FFN_CT

    def body(ug_ref, uv_ref, hg_ref, hv_ref, wg_ref, wv_ref, bg_ref, bv_ref, act_ref, xg_ref, xv_ref):
        first = pl.program_id(0) == 0
        _fill_with_prev(xg_ref, ug_ref[...], hg_ref[...], first)
        _fill_with_prev(xv_ref, uv_ref[...], hv_ref[...], first)

        def strip(row0):
            for c0 in range(0, FFN_CT, LANES):
                cols = slice(c0, c0 + LANES)
                cg = _causal_conv(_delayed(xg_ref, row0, cols, FFN_CONV), wg_ref[:, cols]) + bg_ref[:, cols]
                cv = _causal_conv(_delayed(xv_ref, row0, cols, FFN_CONV), wv_ref[:, cols]) + bv_ref[:, cols]
                act_ref[pl.ds(row0, STRIP), cols] = (silu(cg) * cv).astype(BF16)

        _for_strips(tm, STRIP, strip)

    tok = lambda off: pl.BlockSpec((tm, FFN_CT), lambda i, j: (i, j + off))
    halo = lambda off: pl.BlockSpec((HALO, FFN_CT), lambda i, j: (jnp.maximum(i * (tm // HALO) - 1, 0), j + off))
    par = lambda rows, off: pl.BlockSpec((rows, FFN_CT), lambda i, j: (0, j + off))
    return _pcall(
        body, grid=(t // tm, nj),
        in_specs=[tok(0), tok(nj), halo(0), halo(nj), par(FFN_CONV, 0), par(FFN_CONV, nj), par(1, 0), par(1, nj)],
        out_specs=pl.BlockSpec((tm, FFN_CT), lambda i, j: (i, j)),
        out_shape=jax.ShapeDtypeStruct((t, D_FF), BF16),
        scratch_shapes=[pltpu.VMEM((HALO + tm, FFN_CT), F32)] * 2,
        compiler_params=_params("parallel", "parallel"), name="ffn_act")(up, up, up, up, conv_w, conv_w, conv_b, conv_b)


def _ffn_bwd(up, conv_w, conv_b, dact):
    t = up.shape[0]
    tm = _pick(t, 128)
    n_tok = t // tm
    width = 2 * D_FF

    def dconv(delayed_g, delayed_v, da, wg, wv, bg, bv):
        cg = _causal_conv(delayed_g, wg) + bg
        cv = _causal_conv(delayed_v, wv) + bv
        s = sigmoid(cg)
        return da * cv * (s * (1.0 + cg * (1.0 - s))), da * (cg * s)

    def body(up_ref, prev_ref, next_ref, da_ref, dan_ref, w_ref, b_ref, dup_ref, dw_ref, db_ref, xp_ref, dc_ref, dw_acc, db_acc):
        first = pl.program_id(0) == 0
        last = pl.program_id(0) == n_tok - 1
        xp_ref[0:HALO, :] = jnp.where(first, 0.0, prev_ref[...])
        xp_ref[HALO:HALO + tm, :] = up_ref[...]
        xp_ref[HALO + tm:, :] = next_ref[...]
        dw_acc[...] = jnp.zeros_like(dw_acc)
        db_acc[...] = jnp.zeros_like(db_acc)

        def strip(row0):
            rows = pl.ds(row0, STRIP)
            for c0 in range(0, D_FF, LANES):
                gc, vc = slice(c0, c0 + LANES), slice(D_FF + c0, D_FF + c0 + LANES)
                del_g, del_v = _delayed(xp_ref, row0, gc, FFN_CONV), _delayed(xp_ref, row0, vc, FFN_CONV)
                dcg, dcv = dconv(del_g, del_v, da_ref[rows, gc], w_ref[:, gc], w_ref[:, vc], b_ref[:, gc], b_ref[:, vc])
                dc_ref[rows, gc] = dcg
                dc_ref[rows, vc] = dcv
                db_acc[:, gc] += _fold_rows(dcg)
                db_acc[:, vc] += _fold_rows(dcv)
                for j in range(FFN_CONV):
                    k = FFN_CONV - 1 - j
                    dw_acc[k * SUBLANES:(k + 1) * SUBLANES, gc] += _fold_rows(dcg * del_g[j])
                    dw_acc[k * SUBLANES:(k + 1) * SUBLANES, vc] += _fold_rows(dcv * del_v[j])

        _for_strips(tm, STRIP, strip)

        for c0 in range(0, D_FF, LANES):
            gc, vc = slice(c0, c0 + LANES), slice(D_FF + c0, D_FF + c0 + LANES)

            def delayed(cols):
                ext = xp_ref[tm:tm + 2 * HALO, cols]
                return [ext[HALO:, :]] + [pltpu.roll(ext, j, 0)[HALO:, :] for j in range(1, FFN_CONV)]

            dcg, dcv = dconv(delayed(gc), delayed(vc), dan_ref[:, gc], w_ref[:, gc], w_ref[:, vc], b_ref[:, gc], b_ref[:, vc])
            dc_ref[tm:, gc] = jnp.where(last, 0.0, dcg)
            dc_ref[tm:, vc] = jnp.where(last, 0.0, dcv)

        def strip_dx(row0):
            for c0 in range(0, width, LANES):
                cols = slice(c0, c0 + LANES)
                dup_ref[pl.ds(row0, STRIP), cols] = _advanced_conv(dc_ref, row0, cols, w_ref[:, cols]).astype(BF16)

        _for_strips(tm, STRIP, strip_dx)

        @pl.when(first)
        def _():
            dw_ref[...] = jnp.zeros_like(dw_ref)
            db_ref[...] = jnp.zeros_like(db_ref)

        for k in range(FFN_CONV):
            dw_ref[k:k + 1, :] += jnp.sum(dw_acc[k * SUBLANES:(k + 1) * SUBLANES, :], axis=0, keepdims=True)
        db_ref[...] += jnp.sum(db_acc[...], axis=0, keepdims=True)

    next_rows = lambda i: jnp.minimum((i + 1) * (tm // HALO), t // HALO - 1)
    full = lambda rows: pl.BlockSpec((rows, width), lambda i: (0, 0))
    return _pcall(
        body, grid=(n_tok,),
        in_specs=[pl.BlockSpec((tm, width), lambda i: (i, 0)),
                  pl.BlockSpec((HALO, width), lambda i: (jnp.maximum(i * (tm // HALO) - 1, 0), 0)),
                  pl.BlockSpec((HALO, width), lambda i: (next_rows(i), 0)),
                  pl.BlockSpec((tm, D_FF), lambda i: (i, 0)), pl.BlockSpec((HALO, D_FF), lambda i: (next_rows(i), 0)),
                  full(FFN_CONV), full(1)],
        out_specs=[pl.BlockSpec((tm, width), lambda i: (i, 0)), full(FFN_CONV), full(1)],
        out_shape=[jax.ShapeDtypeStruct((t, width), BF16), jax.ShapeDtypeStruct((FFN_CONV, width), F32),
                   jax.ShapeDtypeStruct((1, width), F32)],
        scratch_shapes=[pltpu.VMEM((tm + 2 * HALO, width), F32), pltpu.VMEM((tm + HALO, width), F32),
                        pltpu.VMEM((FFN_CONV * SUBLANES, width), F32), pltpu.VMEM((SUBLANES, width), F32)],
        compiler_params=_params("arbitrary"), name="ffn_bwd")(up, up, up, dact, dact, conv_w, conv_b)


def _final_loss(x3, target, g):
    t, d = x3.shape
    tm = _token_tile(t)

    def body(x_ref, t_ref, g_ref, loss_ref, dx_ref, dxb_ref, dg_ref, dg_acc, loss_acc):
        dg_acc[...] = jnp.zeros_like(dg_acc)
        loss_acc[...] = jnp.zeros_like(loss_acc)

        def strip(row0):
            rows = pl.ds(row0, NORM_STRIP)
            x = x_ref[rows, :]
            y, r = rms_fwd(x, g_ref[...])
            err = y - t_ref[rows, :]
            loss_acc[...] += _fold_rows(jnp.mean(err * err, axis=-1, keepdims=True))
            dx, dg_rows = rms_bwd(x, r, g_ref[...], err * (1.0 / d))
            dx_ref[rows, :] = dx
            dxb_ref[rows, :] = dx.astype(BF16)
            dg_acc[...] += _fold_rows(dg_rows)

        _for_strips(tm, NORM_STRIP, strip)

        @pl.when(pl.program_id(0) == 0)
        def _():
            loss_ref[...] = jnp.zeros_like(loss_ref)
            dg_ref[...] = jnp.zeros_like(dg_ref)

        loss_ref[...] += 0.5 * jnp.sum(loss_acc[...], axis=0, keepdims=True)
        dg_ref[...] += jnp.sum(dg_acc[...], axis=0, keepdims=True)

    tile = pl.BlockSpec((tm, d), lambda i: (i, 0))
    row = pl.BlockSpec((1, d), lambda i: (0, 0))
    return _pcall(
        body, grid=(t // tm,), in_specs=[tile, tile, row],
        out_specs=[pl.BlockSpec((1, LANES), lambda i: (0, 0)), tile, tile, row],
        out_shape=[jax.ShapeDtypeStruct((1, LANES), F32), jax.ShapeDtypeStruct((t, d), F32), jax.ShapeDtypeStruct((t, d), BF16),
                   jax.ShapeDtypeStruct((1, d), F32)],
        scratch_shapes=[pltpu.VMEM((SUBLANES, d), F32), pltpu.VMEM((SUBLANES, 1), F32)],
        compiler_params=_params("arbitrary"), name="final_loss")(x3, target, g)


def _my_position():
    return lax.axis_index("x"), lax.axis_index("y"), lax.axis_index("c")


COPIES = N_DEV - 1


def _all_gather(arrays):
    n = len(arrays)

    def body(*refs):
        x_refs, out_refs = refs[:n], refs[n:2 * n]
        send_sems, recv_sems, local_sems = refs[2 * n:]
        x, y, cc = _my_position()
        me, sibling = (x, y, cc), (x, y, 1 - cc)
        chips = [(1 - x, y), (x, 1 - y), (1 - x, 1 - y)]

        def block(a, px, py, pc):
            return out_refs[a].at[4 * px + 2 * py + pc]

        def copy(a, k, blk, to, src=None):
            return pltpu.make_async_remote_copy(
                src_ref=block(a, *blk) if src is None else src, dst_ref=block(a, *blk),
                send_sem=send_sems.at[a * COPIES + k], recv_sem=recv_sems.at[a * COPIES + k],
                device_id=to, device_id_type=MESH_ID)

        mine = [pltpu.make_async_copy(x_refs[a], block(a, *me), local_sems.at[a]) for a in range(n)]
        for cp in mine:
            cp.start()
        first = []
        for a in range(n):
            first.append(copy(a, 0, me, sibling, src=x_refs[a]))
            first += [copy(a, 1 + j, me, (*chip, cc), src=x_refs[a]) for j, chip in enumerate(chips)]
        for cp in first:
            cp.start()
        passed = []
        for j, chip in enumerate(chips):
            for a in range(n):
                copy(a, 1 + j, (*chip, cc), me).wait_recv()
                passed.append(copy(a, 4 + j, (*chip, cc), sibling))
                passed[-1].start()
        for a in range(n):
            copy(a, 0, sibling, me).wait_recv()
        for j, chip in enumerate(chips):
            for a in range(n):
                copy(a, 4 + j, (*chip, 1 - cc), me).wait_recv()
        for cp in first + passed:
            cp.wait_send()
        for cp in mine:
            cp.wait()

    any_spec = pl.BlockSpec(memory_space=pl.ANY)
    return _pcall(
        body, out_shape=[jax.ShapeDtypeStruct((N_DEV,) + a.shape, a.dtype) for a in arrays],
        in_specs=[any_spec] * n, out_specs=[any_spec] * n,
        scratch_shapes=[pltpu.SemaphoreType.DMA((n * COPIES,)), pltpu.SemaphoreType.DMA((n * COPIES,)),
                        pltpu.SemaphoreType.DMA((n,))],
        name="all_gather")(*arrays)


def _all_to_all(sends):
    n = len(sends)

    def body(*refs):
        send_refs, recv_refs = refs[:n], refs[n:2 * n]
        send_sems, recv_sems, local_sems = refs[2 * n:]
        x, y, cc = _my_position()
        me = 4 * x + 2 * y + cc
        mine = [pltpu.make_async_copy(send_refs[a].at[me], recv_refs[a].at[me], local_sems.at[a]) for a in range(n)]
        for cp in mine:
            cp.start()
        copies = []
        for rel in range(1, N_DEV):
            px, py, pc = x ^ (rel >> 2), y ^ ((rel >> 1) & 1), cc ^ (rel & 1)
            for a in range(n):
                copies.append(pltpu.make_async_remote_copy(
                    src_ref=send_refs[a].at[4 * px + 2 * py + pc], dst_ref=recv_refs[a].at[me],
                    send_sem=send_sems.at[a * COPIES + rel - 1], recv_sem=recv_sems.at[a * COPIES + rel - 1],
                    device_id=(px, py, pc), device_id_type=MESH_ID))
        for cp in copies:
            cp.start()
        for cp in copies:
            cp.wait()
        for cp in mine:
            cp.wait()

    any_spec = pl.BlockSpec(memory_space=pl.ANY)
    return _pcall(
        body, out_shape=[jax.ShapeDtypeStruct(s.shape, s.dtype) for s in sends],
        in_specs=[any_spec] * n, out_specs=[any_spec] * n,
        scratch_shapes=[pltpu.SemaphoreType.DMA((n * COPIES,)), pltpu.SemaphoreType.DMA((n * COPIES,)),
                        pltpu.SemaphoreType.DMA((n,))],
        name="all_to_all")(*sends)


def _join_shards(g, n_local, out_cols, name):
    _, r, wp = g.shape
    tr = min(r, 256)

    def body(g_ref, o_ref, acc_ref):
        acc_ref[...] = jnp.zeros_like(acc_ref)
        for k in range(N_DEV):
            shift = (n_local * k) % LANES
            start = n_local * k - shift
            piece = g_ref[k].astype(F32)
            if shift:
                piece = pltpu.roll(piece, shift, 1)
            acc_ref[:, start:start + wp] += piece
        o_ref[...] = acc_ref[...].astype(BF16)

    return _pcall(
        body, grid=(r // tr,), in_specs=[pl.BlockSpec((N_DEV, tr, wp), lambda i: (0, i, 0))],
        out_specs=pl.BlockSpec((tr, out_cols), lambda i: (i, 0)), out_shape=jax.ShapeDtypeStruct((r, out_cols), BF16),
        scratch_shapes=[pltpu.VMEM((tr, out_cols), F32)], compiler_params=_params("parallel"), name=name)(g)


def _split_shards(full, n_local, wp, name):
    r, c = full.shape
    tr = min(r, 256)

    def body(x_ref, o_ref):
        lane = lax.broadcasted_iota(jnp.int32, (tr, wp), 1)
        for k in range(N_DEV):
            shift = (n_local * k) % LANES
            start = n_local * k - shift
            win = x_ref[:, start:start + wp]
            if shift:
                win = pltpu.roll(win, wp - shift, 1)
            o_ref[k] = jnp.where(lane < n_local, win, 0.0).astype(BF16)

    return _pcall(
        body, grid=(r // tr,), in_specs=[pl.BlockSpec((tr, c), lambda i: (i, 0))],
        out_specs=pl.BlockSpec((N_DEV, tr, wp), lambda i: (0, i, 0)), out_shape=jax.ShapeDtypeStruct((N_DEV, r, wp), BF16),
        compiler_params=_params("parallel"), name=name)(full)


def _sum_and_adamw(recv, w, m, v, name):
    _, r, wp = recv.shape
    c = w.shape[1]
    tr = SLAB_ROW_TILE if r % SLAB_ROW_TILE == 0 else (SLAB_ROW_TILE // 4 if r % (SLAB_ROW_TILE // 4) == 0 else r)
    bc1 = 1.0 - ADAM_B1 ** ADAM_STEP
    bc2 = 1.0 - ADAM_B2 ** ADAM_STEP

    def body(recv_ref, w_ref, m_ref, v_ref, g_ref, d_ref, nm_ref, nv_ref):
        g = recv_ref[0, :, 0:c].astype(F32)
        for s in range(1, N_DEV):
            g = g + recv_ref[s, :, 0:c].astype(F32)
        m_new = ADAM_B1 * m_ref[...] + (1.0 - ADAM_B1) * g
        v_new = ADAM_B2 * v_ref[...] + (1.0 - ADAM_B2) * (g * g)
        m_hat = m_new / bc1
        v_hat = v_new / bc2
        g_ref[...] = g
        d_ref[...] = -ADAM_LR * (m_hat / (jnp.sqrt(v_hat) + ADAM_EPS) + ADAM_WD * w_ref[...])
        nm_ref[...] = m_new
        nv_ref[...] = v_new

    tile = pl.BlockSpec((tr, c), lambda i: (i, 0))
    return _pcall(
        body, grid=(r // tr,),
        in_specs=[pl.BlockSpec((N_DEV, tr, wp), lambda i: (0, i, 0)), tile, tile, tile],
        out_specs=[tile] * 4, out_shape=[jax.ShapeDtypeStruct((r, c), F32)] * 4,
        compiler_params=_params("parallel"), name=name)(recv, w, m, v)


SHARDED_TAPS = ("dn_conv_w", "ffn_conv_w")
REPLICATED = ("attn_norm_g", "dn_a_log", "dn_dt_bias", "dn_out_norm_g", "sg_norm_g", "sg_w", "sg_b", "ffn_norm_g",
              "ffn_conv_b", "final_norm_g")
SMALL = SHARDED_TAPS + REPLICATED
WEIGHT_ORDER = ("attn_norm_g", "w_in", "dn_conv_w", "dn_a_log", "dn_dt_bias", "dn_out_norm_g", "sg_norm_g", "sg_w", "sg_b",
                "w_out", "ffn_norm_g", "w_up", "ffn_conv_w", "ffn_conv_b", "w_down", "final_norm_g")
SLAB_COLS = 1024
SLAB_ROW_TILE = 128


def _pad_to(flat, multiple):
    pad = (-flat.shape[-1]) % multiple
    if pad == 0:
        return flat
    return jnp.pad(flat, [(0, 0)] * (flat.ndim - 1) + [(0, pad)])


def _lane_padded(n):
    return -(-n // LANES) * LANES


def _pack_small(named):
    flat = jnp.concatenate([named[n].reshape(-1) for n in SMALL])
    return _pad_to(flat, SUBLANES * SLAB_COLS).reshape(-1, SLAB_COLS)


def _unpack_small(slab, like):
    flat = slab.reshape(-1)
    out, off = {}, 0
    for n in SMALL:
        size = like[n].size
        out[n] = flat[off:off + size].reshape(like[n].shape)
        off += size
    return out


def _split_columns(full, n_local):
    r = full.shape[0]
    return full.reshape(r, N_DEV, n_local).transpose(1, 0, 2).reshape(N_DEV, r * n_local)


def _join_columns(blocks, r, n_local):
    return blocks.reshape(N_DEV, r, n_local).transpose(1, 0, 2).reshape(r, N_DEV * n_local)


def _lanes4(a):
    return jnp.pad(a.reshape(1, N_HEADS), ((0, 0), (0, LANES - N_HEADS)))


def kernel(x, attn_norm_g, w_in, dn_conv_w, dn_a_log, dn_dt_bias, dn_out_norm_g, sg_norm_g, sg_w, sg_b, w_out, ffn_norm_g, w_up, ffn_conv_w, ffn_conv_b, w_down, final_norm_g, loss_target, m_attn_norm_g, m_w_in, m_dn_conv_w, m_dn_a_log, m_dn_dt_bias, m_dn_out_norm_g, m_sg_norm_g, m_sg_w, m_sg_b, m_w_out, m_ffn_norm_g, m_w_up, m_ffn_conv_w, m_ffn_conv_b, m_w_down, m_final_norm_g, v_attn_norm_g, v_w_in, v_dn_conv_w, v_dn_a_log, v_dn_dt_bias, v_dn_out_norm_g, v_sg_norm_g, v_sg_w, v_sg_b, v_w_out, v_ffn_norm_g, v_w_up, v_ffn_conv_w, v_ffn_conv_b, v_w_down, v_final_norm_g):
    weights = dict(attn_norm_g=attn_norm_g, w_in=w_in, dn_conv_w=dn_conv_w, dn_a_log=dn_a_log, dn_dt_bias=dn_dt_bias,
                   dn_out_norm_g=dn_out_norm_g, sg_norm_g=sg_norm_g, sg_w=sg_w, sg_b=sg_b, w_out=w_out, ffn_norm_g=ffn_norm_g,
                   w_up=w_up, ffn_conv_w=ffn_conv_w, ffn_conv_b=ffn_conv_b, w_down=w_down, final_norm_g=final_norm_g)
    m_in = dict(attn_norm_g=m_attn_norm_g, w_in=m_w_in, dn_conv_w=m_dn_conv_w, dn_a_log=m_dn_a_log, dn_dt_bias=m_dn_dt_bias,
                dn_out_norm_g=m_dn_out_norm_g, sg_norm_g=m_sg_norm_g, sg_w=m_sg_w, sg_b=m_sg_b, w_out=m_w_out,
                ffn_norm_g=m_ffn_norm_g, w_up=m_w_up, ffn_conv_w=m_ffn_conv_w, ffn_conv_b=m_ffn_conv_b, w_down=m_w_down,
                final_norm_g=m_final_norm_g)
    v_in = dict(attn_norm_g=v_attn_norm_g, w_in=v_w_in, dn_conv_w=v_dn_conv_w, dn_a_log=v_dn_a_log, dn_dt_bias=v_dn_dt_bias,
                dn_out_norm_g=v_dn_out_norm_g, sg_norm_g=v_sg_norm_g, sg_w=v_sg_w, sg_b=v_sg_b, w_out=v_w_out,
                ffn_norm_g=v_ffn_norm_g, w_up=v_w_up, ffn_conv_w=v_ffn_conv_w, ffn_conv_b=v_ffn_conv_b, w_down=v_w_down,
                final_norm_g=v_final_norm_g)

    n_in, n_up = w_in.shape[2], w_up.shape[2]
    r_out, r_down = w_out.shape[1], w_down.shape[1]
    n_dnc, n_ffc = dn_conv_w.shape[2], ffn_conv_w.shape[2]
    wp_in, wp_up = _lane_padded(n_in), _lane_padded(n_up)
    taps = _pad_to(jnp.concatenate([dn_conv_w.reshape(-1), ffn_conv_w.reshape(-1)]), SUBLANES * LANES).reshape(-1, LANES)
    g_in, g_up, g_out, g_down, g_taps = _all_gather([
        jnp.pad(w_in[0].astype(BF16), ((0, 0), (0, wp_in - n_in))), jnp.pad(w_up[0].astype(BF16), ((0, 0), (0, wp_up - n_up))),
        w_out[0].astype(BF16), w_down[0].astype(BF16), taps])
    w_in_p = _join_shards(g_in, n_in, PROJ_PAD, "join_w_in")
    w_up_full = _join_shards(g_up, n_up, N_DEV * n_up, "join_w_up")
    w_out_full = g_out.reshape(N_DEV * r_out, D_MODEL)
    w_down_full = g_down.reshape(N_DEV * r_down, D_MODEL)
    taps_all = g_taps.reshape(N_DEV, -1)
    dn_conv_full = _join_columns(taps_all[:, :CONV_K * n_dnc], CONV_K, n_dnc)
    ffn_conv_full = _join_columns(taps_all[:, CONV_K * n_dnc:CONV_K * n_dnc + FFN_CONV * n_ffc], FFN_CONV, n_ffc)

    loss_lanes, grad_x, g = _local_step(
        x[0], loss_target[0], w_in_p, w_up_full, w_out_full, w_down_full, dn_conv_full, ffn_conv_full, attn_norm_g, dn_a_log,
        dn_dt_bias, dn_out_norm_g, sg_norm_g, sg_w, sg_b, ffn_norm_g, ffn_conv_b, final_norm_g)

    small = jnp.concatenate([g[n].reshape(-1) for n in REPLICATED])
    small_send = jnp.concatenate([_split_columns(g["dn_conv_w"], n_dnc), _split_columns(g["ffn_conv_w"], n_ffc),
                                  jnp.broadcast_to(small[None, :], (N_DEV, small.shape[0]))], axis=1)
    small_send = _pad_to(small_send, SUBLANES * SLAB_COLS).reshape(N_DEV, -1, SLAB_COLS)
    r_in, r_up, r_o, r_dn, r_small = _all_to_all([
        _split_shards(g["w_in"], n_in, wp_in, "split_dw_in"), _split_shards(g["w_up"], n_up, wp_up, "split_dw_up"),
        g["w_out"].reshape(N_DEV, r_out, D_MODEL), g["w_down"].reshape(N_DEV, r_down, D_MODEL), small_send])

    upd = {
        "w_in": _sum_and_adamw(r_in, w_in[0], m_w_in[0], v_w_in[0], "adamw_w_in"),
        "w_up": _sum_and_adamw(r_up, w_up[0], m_w_up[0], v_w_up[0], "adamw_w_up"),
        "w_out": _sum_and_adamw(r_o, w_out[0], m_w_out[0], v_w_out[0], "adamw_w_out"),
        "w_down": _sum_and_adamw(r_dn, w_down[0], m_w_down[0], v_w_down[0], "adamw_w_down"),
    }
    small_upd = _sum_and_adamw(r_small, _pack_small(weights), _pack_small(m_in), _pack_small(v_in), "adamw_small")
    results = []
    for i in range(4):
        named = _unpack_small(small_upd[i], weights)
        named.update({n: upd[n][i][None] for n in upd})
        results.append(named)

    loss = lax.psum(loss_lanes[0, 0], MESH_AXES)
    return (loss, grad_x[None], *[r[n] for r in results for n in WEIGHT_ORDER])


def _local_step(x2d, tgt, w_in_p, w_up_full, w_out_full, w_down_full, dn_conv_full, ffn_conv_full, attn_norm_g, dn_a_log,
                dn_dt_bias, dn_out_norm_g, sg_norm_g, sg_w, sg_b, ffn_norm_g, ffn_conv_b, final_norm_g):
    g1, g2, gf = attn_norm_g, ffn_norm_g, final_norm_g.reshape(1, D_MODEL)
    a_log4, dt_bias4 = _lanes4(dn_a_log), _lanes4(dn_dt_bias)
    sg_w3 = sg_w[0]
    sg_b_t = sg_b[0].T
    conv_b = ffn_conv_b

    h1, rstd1 = _rmsnorm_fwd(x2d, g1)
    p = _matmul(h1, w_in_p, "nn", "in_proj", (512, PROJ_PAD, D_MODEL))
    q, k, v, beta4, g4 = _dn_prep(p, dn_conv_full, a_log4, dt_bias4)
    mix_half, s_all = _dn_forward(q, k, v, beta4, g4, p, dn_out_norm_g)
    mix = _sg_forward(p, sg_norm_g, sg_w3, sg_b_t, mix_half)
    x2 = _matmul(mix, w_out_full, "nn", "out_proj", (1024, 1024, 1024), add=x2d)
    h2, rstd2 = _rmsnorm_fwd(x2, g2)
    up = _matmul(h2, w_up_full, "nn", "up_proj", (512, D_FF, D_MODEL))
    act = _ffn_act(up, ffn_conv_full, conv_b)
    x3 = _matmul(act, w_down_full, "nn", "down_proj", (512, 1024, D_FF), add=x2)
    loss_lanes, dx3, dx3b, d_gf = _final_loss(x3, tgt, gf)

    dact = _matmul(dx3b, w_down_full, "nt", "down_proj_dx", (512, D_FF, D_MODEL))
    d_w_down = _matmul(act, dx3b, "tn", "down_proj_dw", (D_FF // 2, 1024, 512), out_dtype=BF16)
    dup, d_ffn_conv, d_ffn_conv_b = _ffn_bwd(up, ffn_conv_full, conv_b, dact)
    dh2 = _matmul(dup, w_up_full, "nt", "up_proj_dx", (256, 1024, 2 * D_FF))
    d_w_up = _matmul(h2, dup, "tn", "up_proj_dw", (1024, D_FF // 2, 512))
    dx2, dx2b, d_g2 = _rmsnorm_bwd(x2, rstd2, g2, dh2, dx3)
    dmix = _matmul(dx2b, w_out_full, "nt", "out_proj_dx", (1024, 1024, 1024))
    d_w_out = _matmul(mix, dx2b, "tn", "out_proj_dw", (1024, 1024, 512), out_dtype=BF16)
    dp, d_sg_norm, d_sg_w, d_sg_b_t = _sg_backward(p, sg_norm_g, sg_w3, sg_b_t, dmix)
    dq, dk, dv, dbeta4, dg4, dp, d_dn_norm = _dn_backward(q, k, v, beta4, g4, p, dn_out_norm_g, s_all, dmix, dp)
    dc_dn, d_dn_conv, dp, d_a_log4, d_dt_bias4 = _dn_prep_bwd(p, dn_conv_full, a_log4, dt_bias4, dq, dk, dv, dbeta4, dg4, dp)
    dp = _conv_bwd_input(dc_dn, dn_conv_full, "dn_conv_dx", out_cols=PROJ_PAD, into=dp)
    dh1 = _matmul(dp, w_in_p, "nt", "in_proj_dx", (512, 1024, PROJ_PAD))
    d_w_in_p = _matmul(h1, dp, "tn", "in_proj_dw", (512, PROJ_PAD, 512))
    grad_x, _, d_g1 = _rmsnorm_bwd(x2d, rstd1, g1, dh1, dx2)

    grads = dict(
        attn_norm_g=d_g1, w_in=d_w_in_p, dn_conv_w=d_dn_conv, dn_a_log=d_a_log4[:, :N_HEADS], dn_dt_bias=d_dt_bias4[:, :N_HEADS],
        dn_out_norm_g=d_dn_norm, sg_norm_g=d_sg_norm, sg_w=d_sg_w, sg_b=d_sg_b_t[:, :SG_GROUPS].T, w_out=d_w_out,
        ffn_norm_g=d_g2, w_up=d_w_up, ffn_conv_w=d_ffn_conv, ffn_conv_b=d_ffn_conv_b, w_down=d_w_down, final_norm_g=d_gf)
    return loss_lanes, grad_x, grads
```

```python
import functools
import math

import jax
import jax.numpy as jnp
from jax import lax
from jax.experimental import pallas as pl
from jax.experimental.pallas import tpu as pltpu

F32 = jnp.float32
BF16 = jnp.bfloat16
HI = lax.Precision.HIGHEST

D_MODEL = 1024
DN_WIDTH = 512
HEAD_DIM = 128
N_HEADS = 4
SG_WIDTH = 512
SG_GROUPS = 4
SG_DIM = 128
SG_BLOCK = 128
D_FF = 2816
CHUNK = 64
CONV_K = 4
FFN_CONV = 3
EPS = 1e-6
PROJ_COLS = 3080
PROJ_MAIN = 3072
PROJ_PAD = 3200
GELU_C = math.sqrt(2.0 / math.pi)
N_DEV = 8
LANES = 128
SUBLANES = 8
HALO = SUBLANES
VMEM_LIMIT = 48 * 1024 * 1024

ADAM_LR = 0.001
ADAM_B1 = 0.9
ADAM_B2 = 0.999
ADAM_EPS = 1e-08
ADAM_WD = 0.01
ADAM_STEP = 10

MESH_AXES = ("x", "y", "c")
MESH_ID = pl.DeviceIdType.MESH


def _pcall(body, **kw):
    return pl.pallas_call(body, **kw)


def _params(*sem):
    return pltpu.CompilerParams(dimension_semantics=sem, vmem_limit_bytes=VMEM_LIMIT)


def _pick(n, cap):
    best = None
    for t in range(LANES, cap + 1, LANES):
        if n % t == 0:
            best = t
    return best if best else n


FAST, MID, EXACT = "bf16 operands, one pass", "three bf16 passes", "six bf16 passes"


def dot_f32(a, b, dims, tier):
    if tier == FAST:
        return lax.dot_general(a.astype(BF16), b.astype(BF16), dims, preferred_element_type=F32)
    prec = lax.Precision.HIGH if tier == MID else HI
    return lax.dot_general(a, b, dims, precision=prec, preferred_element_type=F32)


def dot_nn(a, b, tier=EXACT):
    return dot_f32(a, b, (((1,), (0,)), ((), ())), tier)


def dot_nt(a, b, tier=EXACT):
    return dot_f32(a, b, (((1,), (1,)), ((), ())), tier)


def dot_tn(a, b, tier=EXACT):
    return dot_f32(a, b, (((0,), (0,)), ((), ())), tier)


def sigmoid(x):
    return 1.0 / (1.0 + jnp.exp(-x))


def silu(x):
    return x * sigmoid(x)


def silu_grad(x):
    s = sigmoid(x)
    return s * (1.0 + x * (1.0 - s))


def gelu(x):
    return 0.5 * x * (1.0 + jnp.tanh(GELU_C * (x + 0.044715 * x * x * x)))


def gelu_grad(x):
    t = jnp.tanh(GELU_C * (x + 0.044715 * x * x * x))
    return 0.5 * (1.0 + t) + 0.5 * x * (1.0 - t * t) * GELU_C * (1.0 + 3.0 * 0.044715 * x * x)


def softplus(z):
    return jnp.maximum(z, 0.0) + jnp.log(1.0 + jnp.exp(-jnp.abs(z)))


def rms_fwd(x, g):
    r = lax.rsqrt(jnp.mean(x * x, axis=-1, keepdims=True) + EPS)
    return x * r * g, r


def rms_bwd(x, r, g, dy):
    dyg = dy * g
    xr = x * r
    dx = r * (dyg - xr * jnp.mean(dyg * xr, axis=-1, keepdims=True))
    return dx, dy * xr


def l2_fwd(x):
    r = lax.rsqrt(jnp.sum(x * x, axis=-1, keepdims=True) + EPS)
    return x * r, r


def l2_bwd(x, r, dy):
    xr = x * r
    return r * (dy - xr * jnp.sum(dy * xr, axis=-1, keepdims=True))


def _tri_masks(n):
    row = lax.broadcasted_iota(jnp.int32, (n, n), 0)
    col = lax.broadcasted_iota(jnp.int32, (n, n), 1)
    return row >= col, row > col


def chunk_cumsum(g4):
    incl, _ = _tri_masks(g4.shape[0])
    return dot_nn(incl.astype(F32), g4)


STACK = N_HEADS * CHUNK


def _head_rows(h):
    return slice(h * CHUNK, (h + 1) * CHUNK)


def _stack_heads(x):
    return jnp.concatenate([x[:, h * HEAD_DIM:(h + 1) * HEAD_DIM] for h in range(N_HEADS)], axis=0)


def _stack_lanes(x4):
    return jnp.concatenate([x4[:, h:h + 1] for h in range(N_HEADS)], axis=0)


def _per_head(fn):
    return jnp.concatenate([fn(h) for h in range(N_HEADS)], axis=0)


def _unit_lower_inverse(l_strict, order):
    c = l_strict.shape[0]
    row = lax.broadcasted_iota(jnp.int32, (c, c), 0)
    col = lax.broadcasted_iota(jnp.int32, (c, c), 1)
    n = -l_strict
    a = (row == col).astype(F32) + n
    p = n
    for _ in range(int(math.log2(order)) - 1):
        p = dot_nn(p, p, FAST)
        a = a + dot_nn(a, p, FAST)
    return a


def dn_chunk_fwd(q, k, v, beta, gc4, s_of):
    row = lax.broadcasted_iota(jnp.int32, (STACK, STACK), 0)
    col = lax.broadcasted_iota(jnp.int32, (STACK, STACK), 1)
    same = (row // CHUNK) == (col // CHUNK)
    incl = jnp.logical_and(same, row >= col)
    strict = jnp.logical_and(same, row > col)
    gc_col = _stack_lanes(gc4)
    gc_row = jnp.sum(jnp.where(row == col, gc_col, 0.0), axis=0, keepdims=True)
    decay = jnp.where(incl, jnp.exp(jnp.minimum(gc_col - gc_row, 0.0)), 0.0)
    gamma = jnp.exp(gc_col)
    gc_last = jnp.concatenate([jnp.broadcast_to(gc4[CHUNK - 1:CHUNK, h:h + 1], (CHUNK, 1)) for h in range(N_HEADS)], axis=0)
    tau = jnp.exp(gc_last - gc_col)
    cd = jnp.exp(gc_last)
    kb = k * beta
    l_mat = jnp.where(strict, dot_nt(kb, k, FAST) * decay, 0.0)
    a_inv = _unit_lower_inverse(l_mat, CHUNK)
    sol = dot_nn(a_inv, jnp.concatenate([v * beta, kb * gamma], axis=1), MID)
    value, kcd = sol[:, :HEAD_DIM], sol[:, HEAD_DIM:]
    attn = jnp.where(incl, dot_nt(q, k, FAST) * decay, 0.0)
    qd = q * gamma
    kt = k * tau
    v_new = value - _per_head(lambda h: dot_nn(kcd[_head_rows(h)], s_of(h), FAST))
    o = _per_head(lambda h: dot_nn(qd[_head_rows(h)], s_of(h), FAST)) + dot_nn(attn, v_new, FAST)
    s_new = [s_of(h) * cd[h * CHUNK:h * CHUNK + 1, :] + dot_tn(kt[_head_rows(h)], v_new[_head_rows(h)], FAST)
             for h in range(N_HEADS)]
    loc = dict(decay=decay, gamma=gamma, tau=tau, cd=cd, kb=kb, l_mat=l_mat, a_inv=a_inv, sol=sol, kcd=kcd, attn=attn,
               v_new=v_new, qd=qd, kt=kt, incl=incl, strict=strict)
    return o, s_new, loc


def dn_chunk_bwd(loc, q, k, v, beta, s_of, do, ds_new_of):
    decay, gamma, tau, cd = loc["decay"], loc["gamma"], loc["tau"], loc["cd"]
    a_inv, attn, l_mat, sol = loc["a_inv"], loc["attn"], loc["l_mat"], loc["sol"]
    v_new, qd, kt, kcd, kb = loc["v_new"], loc["qd"], loc["kt"], loc["kcd"], loc["kb"]
    hr = _head_rows

    dv_new = dot_tn(attn, do, MID) + _per_head(lambda h: dot_nn(kt[hr(h)], ds_new_of(h), MID))
    dattn = jnp.where(loc["incl"], dot_nt(do, v_new, MID), 0.0)
    dqd = _per_head(lambda h: dot_nt(do[hr(h)], s_of(h), MID))
    ds = [dot_tn(qd[hr(h)], do[hr(h)], MID) + ds_new_of(h) * cd[h * CHUNK:h * CHUNK + 1, :]
          - dot_tn(kcd[hr(h)], dv_new[hr(h)], MID) for h in range(N_HEADS)]
    dkt = _per_head(lambda h: dot_nt(v_new[hr(h)], ds_new_of(h), MID))
    dkcd = -_per_head(lambda h: dot_nt(dv_new[hr(h)], s_of(h), MID))
    drhs = dot_tn(a_inv, jnp.concatenate([dv_new, dkcd], axis=1), MID)
    dvb, dkbg = drhs[:, :HEAD_DIM], drhs[:, HEAD_DIM:]
    dl = jnp.where(loc["strict"], -dot_nt(drhs, sol, MID), 0.0)
    dkk = dl * decay
    dqk = dattn * decay
    e = dl * l_mat + dattn * attn
    dgc = jnp.sum(e, axis=1, keepdims=True) - jnp.sum(e, axis=0, keepdims=True).T
    dkb = dot_nn(dkk, k, MID) + dkbg * gamma
    dk = dot_tn(dkk, kb, MID) + dot_tn(dqk, q, MID) + dkt * tau
    dq = dot_nn(dqk, k, MID) + dqd * gamma
    dgamma = jnp.sum(dkbg * kb, axis=1, keepdims=True) + jnp.sum(dqd * q, axis=1, keepdims=True)
    dtau_tau = jnp.sum(dkt * k, axis=1, keepdims=True) * tau
    dgc = dgc + dgamma * gamma - dtau_tau
    is_last = (lax.broadcasted_iota(jnp.int32, (STACK, 1), 0) % CHUNK) == CHUNK - 1

    def last_term(h):
        s, ds_new = s_of(h), ds_new_of(h)
        dcd = jnp.sum(jnp.sum(ds_new * s, axis=1, keepdims=True), axis=0, keepdims=True)
        total = jnp.sum(dtau_tau[hr(h)], axis=0, keepdims=True) + dcd * cd[h * CHUNK:h * CHUNK + 1, :]
        return jnp.broadcast_to(total, (CHUNK, 1))

    dgc = dgc + jnp.where(is_last, _per_head(last_term), 0.0)
    dk = dk + dkb * beta
    dbeta = jnp.sum(dkb * k, axis=1, keepdims=True) + jnp.sum(dvb * v, axis=1, keepdims=True)
    dv = dvb * beta
    return dq, dk, dv, dbeta, dgc, ds


def _token_tile(t):
    return _pick(t, 256)


STRIP = 32
NORM_STRIP = 16


def _for_strips(n_rows, rows, fn):
    def step(r, carry):
        fn(pl.multiple_of(r * rows, rows))
        return carry

    lax.fori_loop(0, n_rows // rows, step, 0)


def _fold_rows(x):
    out = x[0:SUBLANES, :]
    for i in range(1, x.shape[0] // SUBLANES):
        out = out + x[i * SUBLANES:(i + 1) * SUBLANES, :]
    return out


def _rmsnorm_fwd(x, g):
    t, d = x.shape
    tm = _token_tile(t)

    def body(x_ref, g_ref, h_ref, r_ref):
        y, r = rms_fwd(x_ref[...], g_ref[...])
        h_ref[...] = y.astype(BF16)
        r_ref[...] = r

    return _pcall(
        body, grid=(t // tm,),
        in_specs=[pl.BlockSpec((tm, d), lambda i: (i, 0)), pl.BlockSpec((1, d), lambda i: (0, 0))],
        out_specs=[pl.BlockSpec((tm, d), lambda i: (i, 0)), pl.BlockSpec((tm, 1), lambda i: (i, 0))],
        out_shape=[jax.ShapeDtypeStruct((t, d), BF16), jax.ShapeDtypeStruct((t, 1), F32)],
        compiler_params=_params("parallel"), name="rmsnorm_fwd")(x, g)


def _rmsnorm_bwd(x, r, g, dh, dres):
    t, d = x.shape
    tm = _token_tile(t)

    def body(x_ref, r_ref, g_ref, dh_ref, dres_ref, dx_ref, dxb_ref, dg_ref):
        dx, dg_rows = rms_bwd(x_ref[...], r_ref[...], g_ref[...], dh_ref[...])
        dx = dx + dres_ref[...]
        dx_ref[...] = dx
        dxb_ref[...] = dx.astype(BF16)

        @pl.when(pl.program_id(0) == 0)
        def _():
            dg_ref[...] = jnp.zeros_like(dg_ref)

        dg_ref[...] += jnp.sum(dg_rows, axis=0, keepdims=True)

    tile = pl.BlockSpec((tm, d), lambda i: (i, 0))
    row = pl.BlockSpec((1, d), lambda i: (0, 0))
    return _pcall(
        body, grid=(t // tm,),
        in_specs=[tile, pl.BlockSpec((tm, 1), lambda i: (i, 0)), row, tile, tile],
        out_specs=[tile, tile, row],
        out_shape=[jax.ShapeDtypeStruct((t, d), F32), jax.ShapeDtypeStruct((t, d), BF16), jax.ShapeDtypeStruct((1, d), F32)],
        compiler_params=_params("arbitrary"), name="rmsnorm_bwd")(x, r, g, dh, dres)


def _matmul(a, b, mode, name, tiles, add=None, out_dtype=F32):
    if mode == "nn":
        (m, k), n = a.shape, b.shape[1]
    elif mode == "nt":
        (m, k), n = a.shape, b.shape[0]
    else:
        (k, m), n = a.shape, b.shape[1]
    tm, tn, tk = min(tiles[0], m), min(tiles[1], n), min(tiles[2], k)
    assert m % tm == 0 and n % tn == 0 and k % tk == 0, (name, m, n, k, tiles)
    nk = k // tk
    dims = {"nn": (((1,), (0,)), ((), ())), "nt": (((1,), (1,)), ((), ())), "tn": (((0,), (0,)), ((), ()))}[mode]

    def finish(res, add_ref, o_ref):
        if add_ref is not None:
            res = res + add_ref[...]
        o_ref[...] = res.astype(o_ref.dtype)

    def body(*refs):
        a_ref, b_ref = refs[0], refs[1]
        add_ref = refs[2] if add is not None else None
        o_ref = refs[3] if add is not None else refs[2]
        part = lax.dot_general(a_ref[...], b_ref[...], dims, preferred_element_type=F32)
        if nk == 1:
            finish(part, add_ref, o_ref)
            return
        acc_ref = refs[-1]
        kk = pl.program_id(2)

        @pl.when(kk == 0)
        def _():
            acc_ref[...] = part

        @pl.when(kk > 0)
        def _():
            acc_ref[...] += part

        @pl.when(kk == nk - 1)
        def _():
            finish(acc_ref[...], add_ref, o_ref)

    a_spec = pl.BlockSpec((tk, tm), lambda j, i, kk: (kk, i)) if mode == "tn" else pl.BlockSpec((tm, tk), lambda j, i, kk: (i, kk))
    b_spec = pl.BlockSpec((tn, tk), lambda j, i, kk: (j, kk)) if mode == "nt" else pl.BlockSpec((tk, tn), lambda j, i, kk: (kk, j))
    o_spec = pl.BlockSpec((tm, tn), lambda j, i, kk: (i, j))
    in_specs = [a_spec, b_spec] + ([o_spec] if add is not None else [])
    args = (a, b) + ((add,) if add is not None else ())
    return _pcall(
        body, grid=(n // tn, m // tm, nk), in_specs=in_specs, out_specs=o_spec,
        out_shape=jax.ShapeDtypeStruct((m, n), out_dtype),
        scratch_shapes=[pltpu.VMEM((tm, tn), F32)] if nk > 1 else [],
        compiler_params=_params("parallel", "parallel", "arbitrary"), name=name)(*args)


def _prev_halo_spec(tm, width, col_block):
    return pl.BlockSpec((HALO, width), lambda i: (jnp.maximum(i * (tm // HALO) - 1, 0), col_block))


def _fill_with_prev(xp_ref, tile, halo, first):
    xp_ref[0:HALO, :] = jnp.where(first, 0.0, halo)
    xp_ref[HALO:, :] = tile


def _delayed(xp_ref, row0, cols, taps):
    ext = xp_ref[pl.ds(row0, STRIP + HALO), cols]
    return [ext[HALO:, :]] + [pltpu.roll(ext, j, 0)[HALO:, :] for j in range(1, taps)]


def _causal_conv(delayed, w):
    taps = len(delayed)
    out = delayed[0] * w[taps - 1:taps, :]
    for j in range(1, taps):
        out = out + delayed[j] * w[taps - 1 - j:taps - j, :]
    return out


def _advanced_conv(buf_ref, row0, cols, w):
    taps = w.shape[0]
    ext = buf_ref[pl.ds(row0, STRIP + HALO), cols]
    out = ext[:STRIP, :] * w[taps - 1:taps, :]
    for j in range(1, taps):
        out = out + pltpu.roll(ext, STRIP + HALO - j, 0)[:STRIP, :] * w[taps - 1 - j:taps - j, :]
    return out


def _dn_prep(p, conv_w, a_log4, dt_bias4):
    t = p.shape[0]
    tm = _token_tile(t)
    w3 = 3 * DN_WIDTH

    def body(x_ref, halo_ref, pbd_ref, w_ref, alog_ref, dtb_ref, q_ref, k_ref, v_ref, beta_ref, g_ref, xp_ref):
        _fill_with_prev(xp_ref, x_ref[...], halo_ref[...], pl.program_id(0) == 0)

        def strip(row0):
            rows = pl.ds(row0, STRIP)
            for h in range(N_HEADS):
                sl = slice(h * HEAD_DIM, (h + 1) * HEAD_DIM)
                for part, out_ref in ((0, q_ref), (1, k_ref), (2, v_ref)):
                    cols = slice(part * DN_WIDTH + h * HEAD_DIM, part * DN_WIDTH + (h + 1) * HEAD_DIM)
                    y = silu(_causal_conv(_delayed(xp_ref, row0, cols, CONV_K), w_ref[:, cols]))
                    if part == 0:
                        y = l2_fwd(y)[0] * (HEAD_DIM ** -0.5)
                    elif part == 1:
                        y = l2_fwd(y)[0]
                    out_ref[rows, sl] = y
            head = lax.broadcasted_iota(jnp.int32, (STRIP, LANES), 1) < N_HEADS
            pbd = pbd_ref[rows, :]
            beta_ref[rows, :] = jnp.where(head, sigmoid(pbd), 0.0)
            a_raw = pltpu.roll(pbd, LANES - N_HEADS, 1)
            g_ref[rows, :] = jnp.where(head, -jnp.exp(alog_ref[...]) * softplus(a_raw + dtb_ref[...]), 0.0)

        _for_strips(tm, STRIP, strip)

    tok = lambda w, cb: pl.BlockSpec((tm, w), lambda i: (i, cb))
    full = lambda a: pl.BlockSpec(a.shape, lambda i: (0, 0))
    return _pcall(
        body, grid=(t // tm,),
        in_specs=[tok(w3, 0), _prev_halo_spec(tm, w3, 0), tok(LANES, PROJ_MAIN // LANES),
                  full(conv_w), full(a_log4), full(dt_bias4)],
        out_specs=[tok(DN_WIDTH, 0)] * 3 + [tok(LANES, 0)] * 2,
        out_shape=[jax.ShapeDtypeStruct((t, DN_WIDTH), F32)] * 3 + [jax.ShapeDtypeStruct((t, LANES), F32)] * 2,
        scratch_shapes=[pltpu.VMEM((HALO + tm, w3), F32)],
        compiler_params=_params("parallel"), name="dn_prep")(p, p, p, conv_w, a_log4, dt_bias4)


def _dn_prep_bwd(p, conv_w, a_log4, dt_bias4, dq, dk, dv, dbeta4, dg4, dp_buf):
    t = p.shape[0]
    tm = _token_tile(t)
    w3 = 3 * DN_WIDTH

    def body(x_ref, halo_ref, pbd_ref, w_ref, alog_ref, dtb_ref, dq_ref, dk_ref, dv_ref, dbeta_ref, dg_ref, _,
             dc_ref, dw_ref, dpbd_ref, dalog_ref, ddtb_ref, xp_ref, dw_acc, lane_acc):
        first = pl.program_id(0) == 0
        _fill_with_prev(xp_ref, x_ref[...], halo_ref[...], first)
        dw_acc[...] = jnp.zeros_like(dw_acc)
        lane_acc[...] = jnp.zeros_like(lane_acc)

        def strip(row0):
            rows = pl.ds(row0, STRIP)
            for h in range(N_HEADS):
                sl = slice(h * HEAD_DIM, (h + 1) * HEAD_DIM)
                for part, dy_ref in ((0, dq_ref), (1, dk_ref), (2, dv_ref)):
                    cols = slice(part * DN_WIDTH + h * HEAD_DIM, part * DN_WIDTH + (h + 1) * HEAD_DIM)
                    delayed = _delayed(xp_ref, row0, cols, CONV_K)
                    c = _causal_conv(delayed, w_ref[:, cols])
                    dy = dy_ref[rows, sl]
                    if part < 2:
                        y = silu(c)
                        _, r = l2_fwd(y)
                        dy = l2_bwd(y, r, dy * (HEAD_DIM ** -0.5) if part == 0 else dy)
                    dc = dy * silu_grad(c)
                    dc_ref[rows, cols] = dc
                    for j in range(CONV_K):
                        k = CONV_K - 1 - j
                        dw_acc[k * SUBLANES:(k + 1) * SUBLANES, cols] += _fold_rows(dc * delayed[j])
            head = lax.broadcasted_iota(jnp.int32, (STRIP, LANES), 1) < N_HEADS
            pbd = pbd_ref[rows, :]
            beta = sigmoid(pbd)
            dpb = jnp.where(head, dbeta_ref[rows, :] * beta * (1.0 - beta), 0.0)
            z = pltpu.roll(pbd, LANES - N_HEADS, 1) + dtb_ref[...]
            neg_rate = -jnp.exp(alog_ref[...])
            dg = dg_ref[rows, :]
            dpa = jnp.where(head, dg * neg_rate * sigmoid(z), 0.0)
            dpbd_ref[rows, :] = (dpb + pltpu.roll(dpa, N_HEADS, 1)).astype(BF16)
            g = jnp.where(head, neg_rate * softplus(z), 0.0)
            lane_acc[0:SUBLANES, :] += _fold_rows(dg * g)
            lane_acc[SUBLANES:, :] += _fold_rows(dpa)

        _for_strips(tm, STRIP, strip)

        @pl.when(first)
        def _():
            dw_ref[...] = jnp.zeros_like(dw_ref)
            dalog_ref[...] = jnp.zeros_like(dalog_ref)
            ddtb_ref[...] = jnp.zeros_like(ddtb_ref)

        for k in range(CONV_K):
            dw_ref[k:k + 1, :] += jnp.sum(dw_acc[k * SUBLANES:(k + 1) * SUBLANES, :], axis=0, keepdims=True)
        dalog_ref[...] += jnp.sum(lane_acc[0:SUBLANES, :], axis=0, keepdims=True)
        ddtb_ref[...] += jnp.sum(lane_acc[SUBLANES:, :], axis=0, keepdims=True)

    tok = lambda w, cb: pl.BlockSpec((tm, w), lambda i: (i, cb))
    full = lambda shape: pl.BlockSpec(shape, lambda i: (0, 0))
    return _pcall(
        body, grid=(t // tm,),
        in_specs=[tok(w3, 0), _prev_halo_spec(tm, w3, 0), tok(LANES, PROJ_MAIN // LANES),
                  full(conv_w.shape), full(a_log4.shape), full(dt_bias4.shape)] + [tok(DN_WIDTH, 0)] * 3 + [tok(LANES, 0)] * 2
        + [pl.BlockSpec(memory_space=pl.ANY)],
        out_specs=[tok(w3, 0), full((CONV_K, w3)), tok(LANES, PROJ_MAIN // LANES), full((1, LANES)), full((1, LANES))],
        out_shape=[jax.ShapeDtypeStruct((t, w3), F32), jax.ShapeDtypeStruct((CONV_K, w3), F32),
                   jax.ShapeDtypeStruct(dp_buf.shape, dp_buf.dtype),
                   jax.ShapeDtypeStruct((1, LANES), F32), jax.ShapeDtypeStruct((1, LANES), F32)],
        input_output_aliases={11: 2},
        scratch_shapes=[pltpu.VMEM((HALO + tm, w3), F32), pltpu.VMEM((CONV_K * SUBLANES, w3), F32),
                        pltpu.VMEM((2 * SUBLANES, LANES), F32)],
        compiler_params=_params("arbitrary"), name="dn_prep_bwd")(p, p, p, conv_w, a_log4, dt_bias4, dq, dk, dv, dbeta4, dg4, dp_buf)


def _conv_bwd_input(dc, w, name, out_cols=None, col_block=0, into=None):
    t, c = dc.shape
    taps = w.shape[0]
    tm = _token_tile(t)
    ct = _pick(c, 1536)
    n_tok = t // tm
    out_cols = c if out_cols is None else out_cols

    def body(dc_ref, next_ref, w_ref, *rest):
        dx_ref, buf_ref = rest[-2], rest[-1]
        buf_ref[0:tm, :] = dc_ref[...]
        buf_ref[tm:, :] = jnp.where(pl.program_id(0) == n_tok - 1, 0.0, next_ref[...])

        def strip(row0):
            for c0 in range(0, ct, LANES):
                cols = slice(c0, c0 + LANES)
                dx_ref[pl.ds(row0, STRIP), cols] = _advanced_conv(buf_ref, row0, cols, w_ref[:, cols]).astype(BF16)

        _for_strips(tm, STRIP, strip)

    in_specs = [pl.BlockSpec((tm, ct), lambda i, j: (i, j)),
                pl.BlockSpec((HALO, ct), lambda i, j: (jnp.minimum((i + 1) * (tm // HALO), t // HALO - 1), j)),
                pl.BlockSpec((taps, ct), lambda i, j: (0, j))]
    args = (dc, dc, w)
    aliases = {}
    if into is not None:
        in_specs.append(pl.BlockSpec(memory_space=pl.ANY))
        args += (into,)
        aliases = {3: 0}
    return _pcall(
        body, grid=(n_tok, c // ct), in_specs=in_specs,
        out_specs=pl.BlockSpec((tm, ct), lambda i, j: (i, j + col_block)),
        out_shape=jax.ShapeDtypeStruct((t, out_cols), BF16), input_output_aliases=aliases,
        scratch_shapes=[pltpu.VMEM((tm + HALO, ct), F32)],
        compiler_params=_params("parallel", "parallel"), name=name)(*args)


def _dn_forward(q, k, v, beta4, g4, p, norm_g):
    t = q.shape[0]
    n = t // CHUNK

    def body(q_ref, k_ref, v_ref, b_ref, g_ref, gate_ref, ng_ref, mix_ref, s_all_ref, s_ref):
        @pl.when(pl.program_id(0) == 0)
        def _():
            s_ref[...] = jnp.zeros_like(s_ref)

        s_all_ref[0] = s_ref[...]
        o, s_new, _ = dn_chunk_fwd(_stack_heads(q_ref[...]), _stack_heads(k_ref[...]), _stack_heads(v_ref[...]),
                                   _stack_lanes(b_ref[...]), chunk_cumsum(g_ref[...]), lambda h: s_ref[h])
        o_n, _ = rms_fwd(o, ng_ref[...])
        for h in range(N_HEADS):
            sl = slice(h * HEAD_DIM, (h + 1) * HEAD_DIM)
            s_ref[h] = s_new[h]
            mix_ref[:, sl] = (o_n[_head_rows(h)] * silu(gate_ref[:, sl])).astype(BF16)

    ch = lambda w, cb: pl.BlockSpec((CHUNK, w), lambda i: (i, cb))
    return _pcall(
        body, grid=(n,),
        in_specs=[ch(DN_WIDTH, 0)] * 3 + [ch(LANES, 0)] * 2 + [ch(DN_WIDTH, 3), pl.BlockSpec((1, HEAD_DIM), lambda i: (0, 0))],
        out_specs=[ch(DN_WIDTH, 0), pl.BlockSpec((1, N_HEADS, HEAD_DIM, HEAD_DIM), lambda i: (i, 0, 0, 0))],
        out_shape=[jax.ShapeDtypeStruct((t, DN_WIDTH + SG_WIDTH), BF16), jax.ShapeDtypeStruct((n, N_HEADS, HEAD_DIM, HEAD_DIM), F32)],
        scratch_shapes=[pltpu.VMEM((N_HEADS, HEAD_DIM, HEAD_DIM), F32)],
        compiler_params=_params("arbitrary"), name="dn_forward")(q, k, v, beta4, g4, p, norm_g)


def _dn_backward(q, k, v, beta4, g4, p, norm_g, s_all, dmix, dp_buf):
    t = q.shape[0]
    n = t // CHUNK

    def body(q_ref, k_ref, v_ref, b_ref, g_ref, gate_ref, ng_ref, s_in_ref, dmix_ref, _,
             dq_ref, dk_ref, dv_ref, db_ref, dg_ref, dgate_ref, dng_ref, ds_ref):
        @pl.when(pl.program_id(0) == 0)
        def _():
            ds_ref[...] = jnp.zeros_like(ds_ref)
            dng_ref[...] = jnp.zeros_like(dng_ref)

        q, k, v, beta = _stack_heads(q_ref[...]), _stack_heads(k_ref[...]), _stack_heads(v_ref[...]), _stack_lanes(b_ref[...])
        s_of = lambda h: s_in_ref[0, h]
        o, _, loc = dn_chunk_fwd(q, k, v, beta, chunk_cumsum(g_ref[...]), s_of)
        o_n, r = rms_fwd(o, ng_ref[...])
        gate = _stack_heads(gate_ref[...])
        dmx = _stack_heads(dmix_ref[...])
        dgate = dmx * o_n * silu_grad(gate)
        do, dng_rows = rms_bwd(o, r, ng_ref[...], dmx * silu(gate))
        dng_ref[...] += jnp.sum(dng_rows, axis=0, keepdims=True)
        dq, dk, dv, dbeta, dgc, ds = dn_chunk_bwd(loc, q, k, v, beta, s_of, do, lambda h: ds_ref[h])
        lane = lax.broadcasted_iota(jnp.int32, (CHUNK, LANES), 1)
        db4 = jnp.zeros((CHUNK, LANES), F32)
        dgc4 = jnp.zeros((CHUNK, LANES), F32)
        for h in range(N_HEADS):
            sl = slice(h * HEAD_DIM, (h + 1) * HEAD_DIM)
            rows = _head_rows(h)
            dgate_ref[:, sl] = dgate[rows].astype(BF16)
            dq_ref[:, sl] = dq[rows]
            dk_ref[:, sl] = dk[rows]
            dv_ref[:, sl] = dv[rows]
            ds_ref[h] = ds[h]
            db4 = jnp.where(lane == h, dbeta[rows], db4)
            dgc4 = jnp.where(lane == h, dgc[rows], dgc4)
        _, strict = _tri_masks(CHUNK)
        db_ref[...] = db4
        dg_ref[...] = dot_nn(jnp.logical_not(strict).astype(F32), dgc4)

    rev = lambda w, cb: pl.BlockSpec((CHUNK, w), lambda i: (n - 1 - i, cb))
    return _pcall(
        body, grid=(n,),
        in_specs=[rev(DN_WIDTH, 0)] * 3 + [rev(LANES, 0)] * 2 + [rev(DN_WIDTH, 3), pl.BlockSpec((1, HEAD_DIM), lambda i: (0, 0)),
                  pl.BlockSpec((1, N_HEADS, HEAD_DIM, HEAD_DIM), lambda i: (n - 1 - i, 0, 0, 0)), rev(DN_WIDTH, 0),
                  pl.BlockSpec(memory_space=pl.ANY)],
        out_specs=[rev(DN_WIDTH, 0)] * 3 + [rev(LANES, 0)] * 2 + [rev(DN_WIDTH, 3), pl.BlockSpec((1, HEAD_DIM), lambda i: (0, 0))],
        out_shape=[jax.ShapeDtypeStruct((t, DN_WIDTH), F32)] * 3 + [jax.ShapeDtypeStruct((t, LANES), F32)] * 2
        + [jax.ShapeDtypeStruct(dp_buf.shape, dp_buf.dtype), jax.ShapeDtypeStruct((1, HEAD_DIM), F32)],
        input_output_aliases={9: 5},
        scratch_shapes=[pltpu.VMEM((N_HEADS, HEAD_DIM, HEAD_DIM), F32)],
        compiler_params=_params("arbitrary"), name="dn_backward")(q, k, v, beta4, g4, p, norm_g, s_all, dmix, dp_buf)


def _sg_mask():
    row = lax.broadcasted_iota(jnp.int32, (SG_BLOCK, SG_BLOCK), 0)
    col = lax.broadcasted_iota(jnp.int32, (SG_BLOCK, SG_BLOCK), 1)
    return (col // CHUNK) <= (row // CHUNK)


def _sg_forward(p, norm_g, w_s, b_t, mix_buf):
    t = p.shape[0]

    def body(u_ref, v_ref, ng_ref, w_ref, b_ref, _, o_ref):
        mask = _sg_mask()
        for g in range(SG_GROUPS):
            sl = slice(g * SG_DIM, (g + 1) * SG_DIM)
            vn, _ = rms_fwd(gelu(v_ref[:, sl]), ng_ref[:, sl])
            s = dot_nn(jnp.where(mask, w_ref[g], 0.0), vn, FAST) + b_ref[:, g:g + 1]
            o_ref[:, sl] = (gelu(u_ref[:, sl]) * s).astype(BF16)

    blk = lambda cb: pl.BlockSpec((SG_BLOCK, SG_WIDTH), lambda i: (i, cb))
    return _pcall(
        body, grid=(t // SG_BLOCK,),
        in_specs=[blk(4), blk(5), pl.BlockSpec((1, SG_WIDTH), lambda i: (0, 0)),
                  pl.BlockSpec((SG_GROUPS, SG_BLOCK, SG_BLOCK), lambda i: (0, 0, 0)), pl.BlockSpec((SG_BLOCK, SG_GROUPS), lambda i: (0, 0)),
                  pl.BlockSpec(memory_space=pl.ANY)],
        out_specs=blk(1), out_shape=jax.ShapeDtypeStruct(mix_buf.shape, mix_buf.dtype), input_output_aliases={5: 0},
        compiler_params=_params("parallel"), name="sg_forward")(p, p, norm_g, w_s, b_t, mix_buf)


def _sg_backward(p, norm_g, w_s, b_t, dmix):
    t = p.shape[0]

    def body(u_ref, v_ref, ng_ref, w_ref, b_ref, do_ref, duv_ref, dng_ref, dw_ref, db_ref):
        @pl.when(pl.program_id(0) == 0)
        def _():
            dng_ref[...] = jnp.zeros_like(dng_ref)
            dw_ref[...] = jnp.zeros_like(dw_ref)
            db_ref[...] = jnp.zeros_like(db_ref)

        mask = _sg_mask()
        lane = lax.broadcasted_iota(jnp.int32, (SG_BLOCK, LANES), 1)
        db = jnp.zeros((SG_BLOCK, LANES), F32)
        for g in range(SG_GROUPS):
            sl = slice(g * SG_DIM, (g + 1) * SG_DIM)
            u_raw, v_raw, do = u_ref[:, sl], v_ref[:, sl], do_ref[:, sl]
            vg = gelu(v_raw)
            vn, r = rms_fwd(vg, ng_ref[:, sl])
            w_m = jnp.where(mask, w_ref[g], 0.0)
            s = dot_nn(w_m, vn, FAST) + b_ref[:, g:g + 1]
            duv_ref[:, sl] = (do * s * gelu_grad(u_raw)).astype(BF16)
            ds = do * gelu(u_raw)
            db = jnp.where(lane == g, jnp.sum(ds, axis=1, keepdims=True), db)
            dw_ref[g] += jnp.where(mask, dot_nt(ds, vn, FAST), 0.0)
            dvg, dng_rows = rms_bwd(vg, r, ng_ref[:, sl], dot_tn(w_m, ds, FAST))
            dng_ref[:, sl] += jnp.sum(dng_rows, axis=0, keepdims=True)
            duv_ref[:, SG_WIDTH + g * SG_DIM:SG_WIDTH + (g + 1) * SG_DIM] = (dvg * gelu_grad(v_raw)).astype(BF16)
        db_ref[...] += db

    blk = lambda cb: pl.BlockSpec((SG_BLOCK, SG_WIDTH), lambda i: (i, cb))
    const2 = lambda shape: pl.BlockSpec(shape, lambda i: (0, 0))
    w_spec = pl.BlockSpec((SG_GROUPS, SG_BLOCK, SG_BLOCK), lambda i: (0, 0, 0))
    return _pcall(
        body, grid=(t // SG_BLOCK,),
        in_specs=[blk(4), blk(5), const2((1, SG_WIDTH)), w_spec, const2((SG_BLOCK, SG_GROUPS)), blk(1)],
        out_specs=[pl.BlockSpec((SG_BLOCK, 2 * SG_WIDTH), lambda i: (i, 2)), const2((1, SG_WIDTH)), w_spec,
                   const2((SG_BLOCK, LANES))],
        out_shape=[jax.ShapeDtypeStruct((t, PROJ_PAD), BF16), jax.ShapeDtypeStruct((1, SG_WIDTH), F32),
                   jax.ShapeDtypeStruct((SG_GROUPS, SG_BLOCK, SG_BLOCK), F32), jax.ShapeDtypeStruct((SG_BLOCK, LANES), F32)],
        compiler_params=_params("arbitrary"), name="sg_backward")(p, p, norm_g, w_s, b_t, dmix)


FFN_CT = D_FF // 2


def _ffn_act(up, conv_w, conv_b):
    t = up.shape[0]
    tm = _token_tile(t)
    nj = D_FF // FFN_CT

    def body(ug_ref, uv_ref, hg_ref, hv_ref, wg_ref, wv_ref, bg_ref, bv_ref, act_ref, xg_ref, xv_ref):
        first = pl.program_id(0) == 0
        _fill_with_prev(xg_ref, ug_ref[...], hg_ref[...], first)
        _fill_with_prev(xv_ref, uv_ref[...], hv_ref[...], first)

        def strip(row0):
            for c0 in range(0, FFN_CT, LANES):
                cols = slice(c0, c0 + LANES)
                cg = _causal_conv(_delayed(xg_ref, row0, cols, FFN_CONV), wg_ref[:, cols]) + bg_ref[:, cols]
                cv = _causal_conv(_delayed(xv_ref, row0, cols, FFN_CONV), wv_ref[:, cols]) + bv_ref[:, cols]
                act_ref[pl.ds(row0, STRIP), cols] = (silu(cg) * cv).astype(BF16)

        _for_strips(tm, STRIP, strip)

    tok = lambda off: pl.BlockSpec((tm, FFN_CT), lambda i, j: (i, j + off))
    halo = lambda off: pl.BlockSpec((HALO, FFN_CT), lambda i, j: (jnp.maximum(i * (tm // HALO) - 1, 0), j + off))
    par = lambda rows, off: pl.BlockSpec((rows, FFN_CT), lambda i, j: (0, j + off))
    return _pcall(
        body, grid=(t // tm, nj),
        in_specs=[tok(0), tok(nj), halo(0), halo(nj), par(FFN_CONV, 0), par(FFN_CONV, nj), par(1, 0), par(1, nj)],
        out_specs=pl.BlockSpec((tm, FFN_CT), lambda i, j: (i, j)),
        out_shape=jax.ShapeDtypeStruct((t, D_FF), BF16),
        scratch_shapes=[pltpu.VMEM((HALO + tm, FFN_CT), F32)] * 2,
        compiler_params=_params("parallel", "parallel"), name="ffn_act")(up, up, up, up, conv_w, conv_w, conv_b, conv_b)


def _ffn_bwd(up, conv_w, conv_b, dact):
    t = up.shape[0]
    tm = _pick(t, 128)
    n_tok = t // tm
    width = 2 * D_FF

    def dconv(delayed_g, delayed_v, da, wg, wv, bg, bv):
        cg = _causal_conv(delayed_g, wg) + bg
        cv = _causal_conv(delayed_v, wv) + bv
        s = sigmoid(cg)
        return da * cv * (s * (1.0 + cg * (1.0 - s))), da * (cg * s)

    def body(up_ref, prev_ref, next_ref, da_ref, dan_ref, w_ref, b_ref, dup_ref, dw_ref, db_ref, xp_ref, dc_ref, dw_acc, db_acc):
        first = pl.program_id(0) == 0
        last = pl.program_id(0) == n_tok - 1
        xp_ref[0:HALO, :] = jnp.where(first, 0.0, prev_ref[...])
        xp_ref[HALO:HALO + tm, :] = up_ref[...]
        xp_ref[HALO + tm:, :] = next_ref[...]
        dw_acc[...] = jnp.zeros_like(dw_acc)
        db_acc[...] = jnp.zeros_like(db_acc)

        def strip(row0):
            rows = pl.ds(row0, STRIP)
            for c0 in range(0, D_FF, LANES):
                gc, vc = slice(c0, c0 + LANES), slice(D_FF + c0, D_FF + c0 + LANES)
                del_g, del_v = _delayed(xp_ref, row0, gc, FFN_CONV), _delayed(xp_ref, row0, vc, FFN_CONV)
                dcg, dcv = dconv(del_g, del_v, da_ref[rows, gc], w_ref[:, gc], w_ref[:, vc], b_ref[:, gc], b_ref[:, vc])
                dc_ref[rows, gc] = dcg
                dc_ref[rows, vc] = dcv
                db_acc[:, gc] += _fold_rows(dcg)
                db_acc[:, vc] += _fold_rows(dcv)
                for j in range(FFN_CONV):
                    k = FFN_CONV - 1 - j
                    dw_acc[k * SUBLANES:(k + 1) * SUBLANES, gc] += _fold_rows(dcg * del_g[j])
                    dw_acc[k * SUBLANES:(k + 1) * SUBLANES, vc] += _fold_rows(dcv * del_v[j])

        _for_strips(tm, STRIP, strip)

        for c0 in range(0, D_FF, LANES):
            gc, vc = slice(c0, c0 + LANES), slice(D_FF + c0, D_FF + c0 + LANES)

            def delayed(cols):
                ext = xp_ref[tm:tm + 2 * HALO, cols]
                return [ext[HALO:, :]] + [pltpu.roll(ext, j, 0)[HALO:, :] for j in range(1, FFN_CONV)]

            dcg, dcv = dconv(delayed(gc), delayed(vc), dan_ref[:, gc], w_ref[:, gc], w_ref[:, vc], b_ref[:, gc], b_ref[:, vc])
            dc_ref[tm:, gc] = jnp.where(last, 0.0, dcg)
            dc_ref[tm:, vc] = jnp.where(last, 0.0, dcv)

        def strip_dx(row0):
            for c0 in range(0, width, LANES):
                cols = slice(c0, c0 + LANES)
                dup_ref[pl.ds(row0, STRIP), cols] = _advanced_conv(dc_ref, row0, cols, w_ref[:, cols]).astype(BF16)

        _for_strips(tm, STRIP, strip_dx)

        @pl.when(first)
        def _():
            dw_ref[...] = jnp.zeros_like(dw_ref)
            db_ref[...] = jnp.zeros_like(db_ref)

        for k in range(FFN_CONV):
            dw_ref[k:k + 1, :] += jnp.sum(dw_acc[k * SUBLANES:(k + 1) * SUBLANES, :], axis=0, keepdims=True)
        db_ref[...] += jnp.sum(db_acc[...], axis=0, keepdims=True)

    next_rows = lambda i: jnp.minimum((i + 1) * (tm // HALO), t // HALO - 1)
    full = lambda rows: pl.BlockSpec((rows, width), lambda i: (0, 0))
    return _pcall(
        body, grid=(n_tok,),
        in_specs=[pl.BlockSpec((tm, width), lambda i: (i, 0)),
                  pl.BlockSpec((HALO, width), lambda i: (jnp.maximum(i * (tm // HALO) - 1, 0), 0)),
                  pl.BlockSpec((HALO, width), lambda i: (next_rows(i), 0)),
                  pl.BlockSpec((tm, D_FF), lambda i: (i, 0)), pl.BlockSpec((HALO, D_FF), lambda i: (next_rows(i), 0)),
                  full(FFN_CONV), full(1)],
        out_specs=[pl.BlockSpec((tm, width), lambda i: (i, 0)), full(FFN_CONV), full(1)],
        out_shape=[jax.ShapeDtypeStruct((t, width), BF16), jax.ShapeDtypeStruct((FFN_CONV, width), F32),
                   jax.ShapeDtypeStruct((1, width), F32)],
        scratch_shapes=[pltpu.VMEM((tm + 2 * HALO, width), F32), pltpu.VMEM((tm + HALO, width), F32),
                        pltpu.VMEM((FFN_CONV * SUBLANES, width), F32), pltpu.VMEM((SUBLANES, width), F32)],
        compiler_params=_params("arbitrary"), name="ffn_bwd")(up, up, up, dact, dact, conv_w, conv_b)


def _final_loss(x3, target, g):
    t, d = x3.shape
    tm = _token_tile(t)

    def body(x_ref, t_ref, g_ref, loss_ref, dx_ref, dxb_ref, dg_ref):
        @pl.when(pl.program_id(0) == 0)
        def _():
            loss_ref[...] = jnp.zeros_like(loss_ref)
            dg_ref[...] = jnp.zeros_like(dg_ref)

        x = x_ref[...]
        y, r = rms_fwd(x, g_ref[...])
        err = y - t_ref[...]
        per_tok = jnp.mean(err * err, axis=-1, keepdims=True)
        loss_ref[...] += 0.5 * jnp.sum(per_tok, axis=0, keepdims=True)
        dx, dg_rows = rms_bwd(x, r, g_ref[...], err * (1.0 / d))
        dx_ref[...] = dx
        dxb_ref[...] = dx.astype(BF16)
        dg_ref[...] += jnp.sum(dg_rows, axis=0, keepdims=True)

    tile = pl.BlockSpec((tm, d), lambda i: (i, 0))
    row = pl.BlockSpec((1, d), lambda i: (0, 0))
    return _pcall(
        body, grid=(t // tm,), in_specs=[tile, tile, row],
        out_specs=[pl.BlockSpec((1, LANES), lambda i: (0, 0)), tile, tile, row],
        out_shape=[jax.ShapeDtypeStruct((1, LANES), F32), jax.ShapeDtypeStruct((t, d), F32), jax.ShapeDtypeStruct((t, d), BF16),
                   jax.ShapeDtypeStruct((1, d), F32)],
        compiler_params=_params("arbitrary"), name="final_loss")(x3, target, g)


def _my_position():
    return lax.axis_index("x"), lax.axis_index("y"), lax.axis_index("c")


COPIES = N_DEV - 1


def _all_gather(arrays):
    n = len(arrays)

    def body(*refs):
        x_refs, out_refs = refs[:n], refs[n:2 * n]
        send_sems, recv_sems, local_sems = refs[2 * n:]
        x, y, cc = _my_position()
        me, sibling = (x, y, cc), (x, y, 1 - cc)
        chips = [(1 - x, y), (x, 1 - y), (1 - x, 1 - y)]

        def block(a, px, py, pc):
            return out_refs[a].at[4 * px + 2 * py + pc]

        def copy(a, k, blk, to, src=None):
            return pltpu.make_async_remote_copy(
                src_ref=block(a, *blk) if src is None else src, dst_ref=block(a, *blk),
                send_sem=send_sems.at[a * COPIES + k], recv_sem=recv_sems.at[a * COPIES + k],
                device_id=to, device_id_type=MESH_ID)

        mine = [pltpu.make_async_copy(x_refs[a], block(a, *me), local_sems.at[a]) for a in range(n)]
        for cp in mine:
            cp.start()
        first = []
        for a in range(n):
            first.append(copy(a, 0, me, sibling, src=x_refs[a]))
            first += [copy(a, 1 + j, me, (*chip, cc), src=x_refs[a]) for j, chip in enumerate(chips)]
        for cp in first:
            cp.start()
        passed = []
        for j, chip in enumerate(chips):
            for a in range(n):
                copy(a, 1 + j, (*chip, cc), me).wait_recv()
                passed.append(copy(a, 4 + j, (*chip, cc), sibling))
                passed[-1].start()
        for a in range(n):
            copy(a, 0, sibling, me).wait_recv()
        for j, chip in enumerate(chips):
            for a in range(n):
                copy(a, 4 + j, (*chip, 1 - cc), me).wait_recv()
        for cp in first + passed:
            cp.wait_send()
        for cp in mine:
            cp.wait()

    any_spec = pl.BlockSpec(memory_space=pl.ANY)
    return _pcall(
        body, out_shape=[jax.ShapeDtypeStruct((N_DEV,) + a.shape, a.dtype) for a in arrays],
        in_specs=[any_spec] * n, out_specs=[any_spec] * n,
        scratch_shapes=[pltpu.SemaphoreType.DMA((n * COPIES,)), pltpu.SemaphoreType.DMA((n * COPIES,)),
                        pltpu.SemaphoreType.DMA((n,))],
        name="all_gather")(*arrays)


def _all_to_all(sends):
    n = len(sends)

    def body(*refs):
        send_refs, recv_refs = refs[:n], refs[n:2 * n]
        send_sems, recv_sems, local_sems = refs[2 * n:]
        x, y, cc = _my_position()
        me = 4 * x + 2 * y + cc
        mine = [pltpu.make_async_copy(send_refs[a].at[me], recv_refs[a].at[me], local_sems.at[a]) for a in range(n)]
        for cp in mine:
            cp.start()
        copies = []
        for rel in range(1, N_DEV):
            px, py, pc = x ^ (rel >> 2), y ^ ((rel >> 1) & 1), cc ^ (rel & 1)
            for a in range(n):
                copies.append(pltpu.make_async_remote_copy(
                    src_ref=send_refs[a].at[4 * px + 2 * py + pc], dst_ref=recv_refs[a].at[me],
                    send_sem=send_sems.at[a * COPIES + rel - 1], recv_sem=recv_sems.at[a * COPIES + rel - 1],
                    device_id=(px, py, pc), device_id_type=MESH_ID))
        for cp in copies:
            cp.start()
        for cp in copies:
            cp.wait()
        for cp in mine:
            cp.wait()

    any_spec = pl.BlockSpec(memory_space=pl.ANY)
    return _pcall(
        body, out_shape=[jax.ShapeDtypeStruct(s.shape, s.dtype) for s in sends],
        in_specs=[any_spec] * n, out_specs=[any_spec] * n,
        scratch_shapes=[pltpu.SemaphoreType.DMA((n * COPIES,)), pltpu.SemaphoreType.DMA((n * COPIES,)),
                        pltpu.SemaphoreType.DMA((n,))],
        name="all_to_all")(*sends)


def _hbm(a):
    return pltpu.with_memory_space_constraint(a, pltpu.HBM)


def _split_copies(send_refs, land_refs, send_sems, recv_sems, local_sems, gather):
    x, y, cc = _my_position()
    me = 4 * x + 2 * y + cc
    local, remote = [], []
    for a, (send, land) in enumerate(zip(send_refs, land_refs)):
        local.append(pltpu.make_async_copy(send if gather else send.at[me], land.at[me], local_sems.at[a]))
    for a, (send, land) in enumerate(zip(send_refs, land_refs)):
        for rel in range(1, N_DEV):
            px, py, pc = x ^ (rel >> 2), y ^ ((rel >> 1) & 1), cc ^ (rel & 1)
            remote.append(pltpu.make_async_remote_copy(
                src_ref=send if gather else send.at[4 * px + 2 * py + pc], dst_ref=land.at[me],
                send_sem=send_sems.at[a * COPIES + rel - 1], recv_sem=recv_sems.at[a * COPIES + rel - 1],
                device_id=(px, py, pc), device_id_type=MESH_ID))
    return local, remote


SPLIT_EFFECT = pltpu.SideEffectType.DATAFLOW_SIDE_EFFECTING


def _exchange_start(sends, after, gather, name):
    n = len(sends)
    lands = [_hbm(lax.empty((N_DEV,) + s.shape if gather else s.shape, s.dtype)) for s in sends]

    def body(*refs):
        send_refs, land_refs = refs[:n], refs[n:2 * n]
        send_sems, recv_sems, local_sems = refs[2 * n + 1:2 * n + 4]
        token = refs[-1]
        local, remote = _split_copies(send_refs, land_refs, send_sems, recv_sems, local_sems, gather)
        for cp in local + remote:
            cp.start()
        token[...] = jnp.zeros_like(token)

    hbm, sem = pl.BlockSpec(memory_space=pltpu.HBM), pl.BlockSpec(memory_space=pltpu.SEMAPHORE)
    out = _pcall(
        body, name=name,
        out_shape=[pltpu.SemaphoreType.DMA((n * COPIES,)), pltpu.SemaphoreType.DMA((n * COPIES,)), pltpu.SemaphoreType.DMA((n,))]
        + [pltpu.HBM(s.shape, s.dtype) for s in sends] + [pltpu.HBM(z.shape, z.dtype) for z in lands]
        + [jax.ShapeDtypeStruct((SUBLANES, LANES), F32)],
        in_specs=[hbm] * (2 * n) + [pl.BlockSpec(memory_space=pl.ANY)],
        out_specs=[sem] * 3 + [hbm] * (2 * n) + [pl.BlockSpec(memory_space=pltpu.VMEM)],
        input_output_aliases={i: 3 + i for i in range(2 * n)},
        compiler_params=pltpu.CompilerParams(has_side_effects=SPLIT_EFFECT),
    )(*[_hbm(s) for s in sends], *lands, after)
    return dict(sems=out[:3], sends=out[3:3 + n], lands=out[3 + n:3 + 2 * n], gather=gather), out[-1]


def _exchange_wait(handle, after, name):
    sends, lands, gather = handle["sends"], handle["lands"], handle["gather"]
    n = len(sends)

    def body(*refs):
        send_refs, land_refs = refs[:n], refs[n:2 * n]
        send_sems, recv_sems, local_sems = refs[2 * n:2 * n + 3]
        local, remote = _split_copies(send_refs, land_refs, send_sems, recv_sems, local_sems, gather)
        for cp in remote:
            cp.wait_send()
            cp.wait_recv()
        for cp in local:
            cp.wait()

    hbm, sem = pl.BlockSpec(memory_space=pltpu.HBM), pl.BlockSpec(memory_space=pltpu.SEMAPHORE)
    out = _pcall(
        body, name=name,
        out_shape=[pltpu.HBM(s.shape, s.dtype) for s in sends] + [pltpu.HBM(z.shape, z.dtype) for z in lands],
        in_specs=[hbm] * (2 * n) + [sem] * 3 + [pl.BlockSpec(memory_space=pl.ANY)],
        out_specs=[hbm] * (2 * n), input_output_aliases={i: i for i in range(2 * n)},
        compiler_params=pltpu.CompilerParams(has_side_effects=SPLIT_EFFECT),
    )(*sends, *lands, *handle["sems"], after)
    return out[n:]


def _join_shards(g, n_local, out_cols, name):
    _, r, wp = g.shape
    tr = min(r, 256)

    def body(g_ref, o_ref, acc_ref):
        acc_ref[...] = jnp.zeros_like(acc_ref)
        for k in range(N_DEV):
            shift = (n_local * k) % LANES
            start = n_local * k - shift
            piece = g_ref[k].astype(F32)
            if shift:
                piece = pltpu.roll(piece, shift, 1)
            acc_ref[:, start:start + wp] += piece
        o_ref[...] = acc_ref[...].astype(BF16)

    return _pcall(
        body, grid=(r // tr,), in_specs=[pl.BlockSpec((N_DEV, tr, wp), lambda i: (0, i, 0))],
        out_specs=pl.BlockSpec((tr, out_cols), lambda i: (i, 0)), out_shape=jax.ShapeDtypeStruct((r, out_cols), BF16),
        scratch_shapes=[pltpu.VMEM((tr, out_cols), F32)], compiler_params=_params("parallel"), name=name)(g)


def _split_shards(full, n_local, wp, name):
    r, c = full.shape
    tr = min(r, 256)

    def body(x_ref, o_ref):
        lane = lax.broadcasted_iota(jnp.int32, (tr, wp), 1)
        for k in range(N_DEV):
            shift = (n_local * k) % LANES
            start = n_local * k - shift
            win = x_ref[:, start:start + wp]
            if shift:
                win = pltpu.roll(win, wp - shift, 1)
            o_ref[k] = jnp.where(lane < n_local, win, 0.0).astype(BF16)

    return _pcall(
        body, grid=(r // tr,), in_specs=[pl.BlockSpec((tr, c), lambda i: (i, 0))],
        out_specs=pl.BlockSpec((N_DEV, tr, wp), lambda i: (0, i, 0)), out_shape=jax.ShapeDtypeStruct((N_DEV, r, wp), BF16),
        compiler_params=_params("parallel"), name=name)(full)


def _sum_and_adamw(recv, w, m, v, name):
    _, r, wp = recv.shape
    c = w.shape[1]
    tr = SLAB_ROW_TILE if r % SLAB_ROW_TILE == 0 else (SLAB_ROW_TILE // 4 if r % (SLAB_ROW_TILE // 4) == 0 else r)
    bc1 = 1.0 - ADAM_B1 ** ADAM_STEP
    bc2 = 1.0 - ADAM_B2 ** ADAM_STEP

    def body(recv_ref, w_ref, m_ref, v_ref, g_ref, d_ref, nm_ref, nv_ref):
        g = recv_ref[0, :, 0:c].astype(F32)
        for s in range(1, N_DEV):
            g = g + recv_ref[s, :, 0:c].astype(F32)
        m_new = ADAM_B1 * m_ref[...] + (1.0 - ADAM_B1) * g
        v_new = ADAM_B2 * v_ref[...] + (1.0 - ADAM_B2) * (g * g)
        m_hat = m_new / bc1
        v_hat = v_new / bc2
        g_ref[...] = g
        d_ref[...] = -ADAM_LR * (m_hat / (jnp.sqrt(v_hat) + ADAM_EPS) + ADAM_WD * w_ref[...])
        nm_ref[...] = m_new
        nv_ref[...] = v_new

    tile = pl.BlockSpec((tr, c), lambda i: (i, 0))
    return _pcall(
        body, grid=(r // tr,),
        in_specs=[pl.BlockSpec((N_DEV, tr, wp), lambda i: (0, i, 0)), tile, tile, tile],
        out_specs=[tile] * 4, out_shape=[jax.ShapeDtypeStruct((r, c), F32)] * 4,
        compiler_params=_params("parallel"), name=name)(recv, w, m, v)


SHARDED_TAPS = ("dn_conv_w", "ffn_conv_w")
REPLICATED = ("attn_norm_g", "dn_a_log", "dn_dt_bias", "dn_out_norm_g", "sg_norm_g", "sg_w", "sg_b", "ffn_norm_g",
              "ffn_conv_b", "final_norm_g")
SMALL = SHARDED_TAPS + REPLICATED
WEIGHT_ORDER = ("attn_norm_g", "w_in", "dn_conv_w", "dn_a_log", "dn_dt_bias", "dn_out_norm_g", "sg_norm_g", "sg_w", "sg_b",
                "w_out", "ffn_norm_g", "w_up", "ffn_conv_w", "ffn_conv_b", "w_down", "final_norm_g")
SLAB_COLS = 1024
SLAB_ROW_TILE = 128


def _pad_to(flat, multiple):
    pad = (-flat.shape[-1]) % multiple
    if pad == 0:
        return flat
    return jnp.pad(flat, [(0, 0)] * (flat.ndim - 1) + [(0, pad)])


def _lane_padded(n):
    return -(-n // LANES) * LANES


def _pack_small(named):
    flat = jnp.concatenate([named[n].reshape(-1) for n in SMALL])
    return _pad_to(flat, SUBLANES * SLAB_COLS).reshape(-1, SLAB_COLS)


def _unpack_small(slab, like):
    flat = slab.reshape(-1)
    out, off = {}, 0
    for n in SMALL:
        size = like[n].size
        out[n] = flat[off:off + size].reshape(like[n].shape)
        off += size
    return out


def _split_columns(full, n_local):
    r = full.shape[0]
    return full.reshape(r, N_DEV, n_local).transpose(1, 0, 2).reshape(N_DEV, r * n_local)


def _join_columns(blocks, r, n_local):
    return blocks.reshape(N_DEV, r, n_local).transpose(1, 0, 2).reshape(r, N_DEV * n_local)


def _lanes4(a):
    return jnp.pad(a.reshape(1, N_HEADS), ((0, 0), (0, LANES - N_HEADS)))


def kernel(x, attn_norm_g, w_in, dn_conv_w, dn_a_log, dn_dt_bias, dn_out_norm_g, sg_norm_g, sg_w, sg_b, w_out, ffn_norm_g, w_up, ffn_conv_w, ffn_conv_b, w_down, final_norm_g, loss_target, m_attn_norm_g, m_w_in, m_dn_conv_w, m_dn_a_log, m_dn_dt_bias, m_dn_out_norm_g, m_sg_norm_g, m_sg_w, m_sg_b, m_w_out, m_ffn_norm_g, m_w_up, m_ffn_conv_w, m_ffn_conv_b, m_w_down, m_final_norm_g, v_attn_norm_g, v_w_in, v_dn_conv_w, v_dn_a_log, v_dn_dt_bias, v_dn_out_norm_g, v_sg_norm_g, v_sg_w, v_sg_b, v_w_out, v_ffn_norm_g, v_w_up, v_ffn_conv_w, v_ffn_conv_b, v_w_down, v_final_norm_g):
    weights = dict(attn_norm_g=attn_norm_g, w_in=w_in, dn_conv_w=dn_conv_w, dn_a_log=dn_a_log, dn_dt_bias=dn_dt_bias,
                   dn_out_norm_g=dn_out_norm_g, sg_norm_g=sg_norm_g, sg_w=sg_w, sg_b=sg_b, w_out=w_out, ffn_norm_g=ffn_norm_g,
                   w_up=w_up, ffn_conv_w=ffn_conv_w, ffn_conv_b=ffn_conv_b, w_down=w_down, final_norm_g=final_norm_g)
    m_in = dict(attn_norm_g=m_attn_norm_g, w_in=m_w_in, dn_conv_w=m_dn_conv_w, dn_a_log=m_dn_a_log, dn_dt_bias=m_dn_dt_bias,
                dn_out_norm_g=m_dn_out_norm_g, sg_norm_g=m_sg_norm_g, sg_w=m_sg_w, sg_b=m_sg_b, w_out=m_w_out,
                ffn_norm_g=m_ffn_norm_g, w_up=m_w_up, ffn_conv_w=m_ffn_conv_w, ffn_conv_b=m_ffn_conv_b, w_down=m_w_down,
                final_norm_g=m_final_norm_g)
    v_in = dict(attn_norm_g=v_attn_norm_g, w_in=v_w_in, dn_conv_w=v_dn_conv_w, dn_a_log=v_dn_a_log, dn_dt_bias=v_dn_dt_bias,
                dn_out_norm_g=v_dn_out_norm_g, sg_norm_g=v_sg_norm_g, sg_w=v_sg_w, sg_b=v_sg_b, w_out=v_w_out,
                ffn_norm_g=v_ffn_norm_g, w_up=v_w_up, ffn_conv_w=v_ffn_conv_w, ffn_conv_b=v_ffn_conv_b, w_down=v_w_down,
                final_norm_g=v_final_norm_g)

    n_in, n_up = w_in.shape[2], w_up.shape[2]
    r_out, r_down = w_out.shape[1], w_down.shape[1]
    n_dnc, n_ffc = dn_conv_w.shape[2], ffn_conv_w.shape[2]
    wp_in, wp_up = _lane_padded(n_in), _lane_padded(n_up)
    taps = _pad_to(jnp.concatenate([dn_conv_w.reshape(-1), ffn_conv_w.reshape(-1)]), SUBLANES * LANES).reshape(-1, LANES)
    g_in, g_taps = _all_gather([jnp.pad(w_in[0].astype(BF16), ((0, 0), (0, wp_in - n_in))), taps])
    late_weights, token = _exchange_start(
        [w_out[0].astype(BF16), jnp.pad(w_up[0].astype(BF16), ((0, 0), (0, wp_up - n_up))), w_down[0].astype(BF16)],
        g_taps, True, "gather_late_start")
    w_in_p = _join_shards(g_in, n_in, PROJ_PAD, "join_w_in")
    taps_all = g_taps.reshape(N_DEV, -1)
    dn_conv_full = _join_columns(taps_all[:, :CONV_K * n_dnc], CONV_K, n_dnc)
    ffn_conv_full = _join_columns(taps_all[:, CONV_K * n_dnc:CONV_K * n_dnc + FFN_CONV * n_ffc], FFN_CONV, n_ffc)

    def late(after):
        g_out, g_up, g_down = _exchange_wait(late_weights, after, "gather_late_wait")
        return (_join_shards(g_up, n_up, N_DEV * n_up, "join_w_up"),
                g_out.reshape(N_DEV * r_out, D_MODEL), g_down.reshape(N_DEV * r_down, D_MODEL))

    def send_early(blocks, after, name):
        return _exchange_start(blocks, after, False, name)

    loss_lanes, grad_x, g, early = _local_step(
        x[0], loss_target[0], w_in_p, late, send_early, dn_conv_full, ffn_conv_full, attn_norm_g + token[0:1, 0:1],
        dn_a_log, dn_dt_bias, dn_out_norm_g, sg_norm_g, sg_w, sg_b, ffn_norm_g, ffn_conv_b, final_norm_g, (n_up, wp_up))

    small = jnp.concatenate([g[n].reshape(-1) for n in REPLICATED])
    small_send = jnp.concatenate([_split_columns(g["dn_conv_w"], n_dnc), _split_columns(g["ffn_conv_w"], n_ffc),
                                  jnp.broadcast_to(small[None, :], (N_DEV, small.shape[0]))], axis=1)
    small_send = _pad_to(small_send, SUBLANES * SLAB_COLS).reshape(N_DEV, -1, SLAB_COLS)
    r_in, r_small = _all_to_all([_split_shards(g["w_in"], n_in, wp_in, "split_dw_in"), small_send])
    r_dn, = _exchange_wait(early[0], r_small, "send_dw_down_wait")
    r_up, r_o = _exchange_wait(early[1], r_small, "send_dw_up_out_wait")

    upd = {
        "w_in": _sum_and_adamw(r_in, w_in[0], m_w_in[0], v_w_in[0], "adamw_w_in"),
        "w_up": _sum_and_adamw(r_up, w_up[0], m_w_up[0], v_w_up[0], "adamw_w_up"),
        "w_out": _sum_and_adamw(r_o, w_out[0], m_w_out[0], v_w_out[0], "adamw_w_out"),
        "w_down": _sum_and_adamw(r_dn, w_down[0], m_w_down[0], v_w_down[0], "adamw_w_down"),
    }
    small_upd = _sum_and_adamw(r_small, _pack_small(weights), _pack_small(m_in), _pack_small(v_in), "adamw_small")
    results = []
    for i in range(4):
        named = _unpack_small(small_upd[i], weights)
        named.update({n: upd[n][i][None] for n in upd})
        results.append(named)

    loss = lax.psum(loss_lanes[0, 0], MESH_AXES)
    return (loss, grad_x[None], *[r[n] for r in results for n in WEIGHT_ORDER])


def _local_step(x2d, tgt, w_in_p, late_weights, send_early, dn_conv_full, ffn_conv_full, attn_norm_g, dn_a_log,
                dn_dt_bias, dn_out_norm_g, sg_norm_g, sg_w, sg_b, ffn_norm_g, ffn_conv_b, final_norm_g, up_shard):
    g1, g2, gf = attn_norm_g, ffn_norm_g, final_norm_g.reshape(1, D_MODEL)
    a_log4, dt_bias4 = _lanes4(dn_a_log), _lanes4(dn_dt_bias)
    sg_w3 = sg_w[0]
    sg_b_t = sg_b[0].T
    conv_b = ffn_conv_b

    h1, rstd1 = _rmsnorm_fwd(x2d, g1)
    p = _matmul(h1, w_in_p, "nn", "in_proj", (512, PROJ_PAD, D_MODEL))
    q, k, v, beta4, g4 = _dn_prep(p, dn_conv_full, a_log4, dt_bias4)
    mix_half, s_all = _dn_forward(q, k, v, beta4, g4, p, dn_out_norm_g)
    mix = _sg_forward(p, sg_norm_g, sg_w3, sg_b_t, mix_half)
    w_up_full, w_out_full, w_down_full = late_weights(mix)
    x2 = _matmul(mix, w_out_full, "nn", "out_proj", (1024, 1024, 1024), add=x2d)
    h2, rstd2 = _rmsnorm_fwd(x2, g2)
    up = _matmul(h2, w_up_full, "nn", "up_proj", (512, D_FF, D_MODEL))
    act = _ffn_act(up, ffn_conv_full, conv_b)
    x3 = _matmul(act, w_down_full, "nn", "down_proj", (512, 1024, D_FF), add=x2)
    loss_lanes, dx3, dx3b, d_gf = _final_loss(x3, tgt, gf)

    dact = _matmul(dx3b, w_down_full, "nt", "down_proj_dx", (512, D_FF, D_MODEL))
    d_w_down = _matmul(act, dx3b, "tn", "down_proj_dw", (D_FF // 2, 1024, 512), out_dtype=BF16)
    sent_down, token = send_early([d_w_down.reshape(N_DEV, D_FF // N_DEV, D_MODEL)], d_w_down, "send_dw_down")
    dup, d_ffn_conv, d_ffn_conv_b = _ffn_bwd(up, ffn_conv_full, conv_b + token[0:1, 0:1], dact)
    dh2 = _matmul(dup, w_up_full, "nt", "up_proj_dx", (256, 1024, 2 * D_FF))
    d_w_up = _matmul(h2, dup, "tn", "up_proj_dw", (1024, D_FF // 2, 512))
    dx2, dx2b, d_g2 = _rmsnorm_bwd(x2, rstd2, g2, dh2, dx3)
    dmix = _matmul(dx2b, w_out_full, "nt", "out_proj_dx", (1024, 1024, 1024))
    d_w_out = _matmul(mix, dx2b, "tn", "out_proj_dw", (1024, 1024, 512), out_dtype=BF16)
    sent_up_out, token = send_early(
        [_split_shards(d_w_up, up_shard[0], up_shard[1], "split_dw_up"), d_w_out.reshape(N_DEV, D_MODEL // N_DEV, D_MODEL)],
        d_w_out, "send_dw_up_out")
    dp, d_sg_norm, d_sg_w, d_sg_b_t = _sg_backward(p, sg_norm_g + token[0:1, 0:1], sg_w3, sg_b_t, dmix)
    dq, dk, dv, dbeta4, dg4, dp, d_dn_norm = _dn_backward(q, k, v, beta4, g4, p, dn_out_norm_g, s_all, dmix, dp)
    dc_dn, d_dn_conv, dp, d_a_log4, d_dt_bias4 = _dn_prep_bwd(p, dn_conv_full, a_log4, dt_bias4, dq, dk, dv, dbeta4, dg4, dp)
    dp = _conv_bwd_input(dc_dn, dn_conv_full, "dn_conv_dx", out_cols=PROJ_PAD, into=dp)
    dh1 = _matmul(dp, w_in_p, "nt", "in_proj_dx", (512, 1024, PROJ_PAD))
    d_w_in_p = _matmul(h1, dp, "tn", "in_proj_dw", (512, PROJ_PAD, 512))
    grad_x, _, d_g1 = _rmsnorm_bwd(x2d, rstd1, g1, dh1, dx2)

    grads = dict(
        attn_norm_g=d_g1, w_in=d_w_in_p, dn_conv_w=d_dn_conv, dn_a_log=d_a_log4[:, :N_HEADS], dn_dt_bias=d_dt_bias4[:, :N_HEADS],
        dn_out_norm_g=d_dn_norm, sg_norm_g=d_sg_norm, sg_w=d_sg_w, sg_b=d_sg_b_t[:, :SG_GROUPS].T,
        ffn_norm_g=d_g2, ffn_conv_w=d_ffn_conv, ffn_conv_b=d_ffn_conv_b, final_norm_g=d_gf)
    return loss_lanes, grad_x, grads, (sent_down, sent_up_out)
```

```python
import functools
import math

import jax
import jax.numpy as jnp
from jax import lax
from jax.experimental import pallas as pl
from jax.experimental.pallas import tpu as pltpu

F32 = jnp.float32
BF16 = jnp.bfloat16
HI = lax.Precision.HIGHEST

D_MODEL = 1024
DN_WIDTH = 512
HEAD_DIM = 128
N_HEADS = 4
SG_WIDTH = 512
SG_GROUPS = 4
SG_DIM = 128
SG_BLOCK = 128
D_FF = 2816
CHUNK = 64
CONV_K = 4
FFN_CONV = 3
EPS = 1e-6
PROJ_COLS = 3080
PROJ_MAIN = 3072
PROJ_PAD = 3200
GELU_C = math.sqrt(2.0 / math.pi)
N_DEV = 8
LANES = 128
SUBLANES = 8
HALO = SUBLANES
VMEM_LIMIT = 48 * 1024 * 1024

ADAM_LR = 0.001
ADAM_B1 = 0.9
ADAM_B2 = 0.999
ADAM_EPS = 1e-08
ADAM_WD = 0.01
ADAM_STEP = 10

MESH_AXES = ("x", "y", "c")
MESH_ID = pl.DeviceIdType.MESH


def _pcall(body, **kw):
    return pl.pallas_call(body, **kw)


def _params(*sem):
    return pltpu.CompilerParams(dimension_semantics=sem, vmem_limit_bytes=VMEM_LIMIT)


def _pick(n, cap):
    best = None
    for t in range(LANES, cap + 1, LANES):
        if n % t == 0:
            best = t
    return best if best else n


FAST, MID, EXACT = "bf16 operands, one pass", "three bf16 passes", "six bf16 passes"


def dot_f32(a, b, dims, tier):
    if tier == FAST:
        return lax.dot_general(a.astype(BF16), b.astype(BF16), dims, preferred_element_type=F32)
    prec = lax.Precision.HIGH if tier == MID else HI
    return lax.dot_general(a, b, dims, precision=prec, preferred_element_type=F32)


def dot_nn(a, b, tier=EXACT):
    return dot_f32(a, b, (((1,), (0,)), ((), ())), tier)


def dot_nt(a, b, tier=EXACT):
    return dot_f32(a, b, (((1,), (1,)), ((), ())), tier)


def dot_tn(a, b, tier=EXACT):
    return dot_f32(a, b, (((0,), (0,)), ((), ())), tier)


def sigmoid(x):
    return 1.0 / (1.0 + jnp.exp(-x))


def silu(x):
    return x * sigmoid(x)


def silu_grad(x):
    s = sigmoid(x)
    return s * (1.0 + x * (1.0 - s))


def gelu(x):
    return 0.5 * x * (1.0 + jnp.tanh(GELU_C * (x + 0.044715 * x * x * x)))


def gelu_grad(x):
    t = jnp.tanh(GELU_C * (x + 0.044715 * x * x * x))
    return 0.5 * (1.0 + t) + 0.5 * x * (1.0 - t * t) * GELU_C * (1.0 + 3.0 * 0.044715 * x * x)


def softplus(z):
    return jnp.maximum(z, 0.0) + jnp.log(1.0 + jnp.exp(-jnp.abs(z)))


def rms_fwd(x, g):
    r = lax.rsqrt(jnp.mean(x * x, axis=-1, keepdims=True) + EPS)
    return x * r * g, r


def rms_bwd(x, r, g, dy):
    dyg = dy * g
    xr = x * r
    dx = r * (dyg - xr * jnp.mean(dyg * xr, axis=-1, keepdims=True))
    return dx, dy * xr


def l2_fwd(x):
    r = lax.rsqrt(jnp.sum(x * x, axis=-1, keepdims=True) + EPS)
    return x * r, r


def l2_bwd(x, r, dy):
    xr = x * r
    return r * (dy - xr * jnp.sum(dy * xr, axis=-1, keepdims=True))


def _tri_masks(n):
    row = lax.broadcasted_iota(jnp.int32, (n, n), 0)
    col = lax.broadcasted_iota(jnp.int32, (n, n), 1)
    return row >= col, row > col


def chunk_cumsum(g4):
    incl, _ = _tri_masks(g4.shape[0])
    return dot_nn(incl.astype(F32), g4)


STACK = N_HEADS * CHUNK


def _head_rows(h):
    return slice(h * CHUNK, (h + 1) * CHUNK)


def _stack_heads(x):
    return jnp.concatenate([x[:, h * HEAD_DIM:(h + 1) * HEAD_DIM] for h in range(N_HEADS)], axis=0)


def _stack_lanes(x4):
    return jnp.concatenate([x4[:, h:h + 1] for h in range(N_HEADS)], axis=0)


def _per_head(fn):
    return jnp.concatenate([fn(h) for h in range(N_HEADS)], axis=0)


def _unit_lower_inverse(l_strict, order):
    c = l_strict.shape[0]
    row = lax.broadcasted_iota(jnp.int32, (c, c), 0)
    col = lax.broadcasted_iota(jnp.int32, (c, c), 1)
    n = -l_strict
    a = (row == col).astype(F32) + n
    p = n
    for _ in range(int(math.log2(order)) - 1):
        p = dot_nn(p, p, FAST)
        a = a + dot_nn(a, p, FAST)
    return a


def dn_chunk_fwd(q, k, v, beta, gc4, s_of):
    row = lax.broadcasted_iota(jnp.int32, (STACK, STACK), 0)
    col = lax.broadcasted_iota(jnp.int32, (STACK, STACK), 1)
    same = (row // CHUNK) == (col // CHUNK)
    incl = jnp.logical_and(same, row >= col)
    strict = jnp.logical_and(same, row > col)
    gc_col = _stack_lanes(gc4)
    gc_row = jnp.sum(jnp.where(row == col, gc_col, 0.0), axis=0, keepdims=True)
    decay = jnp.where(incl, jnp.exp(jnp.minimum(gc_col - gc_row, 0.0)), 0.0)
    gamma = jnp.exp(gc_col)
    gc_last = jnp.concatenate([jnp.broadcast_to(gc4[CHUNK - 1:CHUNK, h:h + 1], (CHUNK, 1)) for h in range(N_HEADS)], axis=0)
    tau = jnp.exp(gc_last - gc_col)
    cd = jnp.exp(gc_last)
    kb = k * beta
    l_mat = jnp.where(strict, dot_nt(kb, k, FAST) * decay, 0.0)
    a_inv = _unit_lower_inverse(l_mat, CHUNK)
    sol = dot_nn(a_inv, jnp.concatenate([v * beta, kb * gamma], axis=1), FAST)
    value, kcd = sol[:, :HEAD_DIM], sol[:, HEAD_DIM:]
    attn = jnp.where(incl, dot_nt(q, k, FAST) * decay, 0.0)
    qd = q * gamma
    kt = k * tau
    v_new = value - _per_head(lambda h: dot_nn(kcd[_head_rows(h)], s_of(h), FAST))
    o = _per_head(lambda h: dot_nn(qd[_head_rows(h)], s_of(h), FAST)) + dot_nn(attn, v_new, FAST)
    s_new = [s_of(h) * cd[h * CHUNK:h * CHUNK + 1, :] + dot_tn(kt[_head_rows(h)], v_new[_head_rows(h)], FAST)
             for h in range(N_HEADS)]
    loc = dict(decay=decay, gamma=gamma, tau=tau, cd=cd, kb=kb, l_mat=l_mat, a_inv=a_inv, sol=sol, kcd=kcd, attn=attn,
               v_new=v_new, qd=qd, kt=kt, incl=incl, strict=strict)
    return o, s_new, loc


def dn_chunk_bwd(loc, q, k, v, beta, s_of, do, ds_new_of):
    decay, gamma, tau, cd = loc["decay"], loc["gamma"], loc["tau"], loc["cd"]
    a_inv, attn, l_mat, sol = loc["a_inv"], loc["attn"], loc["l_mat"], loc["sol"]
    v_new, qd, kt, kcd, kb = loc["v_new"], loc["qd"], loc["kt"], loc["kcd"], loc["kb"]
    hr = _head_rows

    dv_new = dot_tn(attn, do, FAST) + _per_head(lambda h: dot_nn(kt[hr(h)], ds_new_of(h), FAST))
    dattn = jnp.where(loc["incl"], dot_nt(do, v_new, FAST), 0.0)
    dqd = _per_head(lambda h: dot_nt(do[hr(h)], s_of(h), FAST))
    ds = [dot_tn(qd[hr(h)], do[hr(h)], FAST) + ds_new_of(h) * cd[h * CHUNK:h * CHUNK + 1, :]
          - dot_tn(kcd[hr(h)], dv_new[hr(h)], FAST) for h in range(N_HEADS)]
    dkt = _per_head(lambda h: dot_nt(v_new[hr(h)], ds_new_of(h), FAST))
    dkcd = -_per_head(lambda h: dot_nt(dv_new[hr(h)], s_of(h), FAST))
    drhs = dot_tn(a_inv, jnp.concatenate([dv_new, dkcd], axis=1), FAST)
    dvb, dkbg = drhs[:, :HEAD_DIM], drhs[:, HEAD_DIM:]
    dl = jnp.where(loc["strict"], -dot_nt(drhs, sol, FAST), 0.0)
    dkk = dl * decay
    dqk = dattn * decay
    e = dl * l_mat + dattn * attn
    dgc = jnp.sum(e, axis=1, keepdims=True) - jnp.sum(e, axis=0, keepdims=True).T
    dkb = dot_nn(dkk, k, FAST) + dkbg * gamma
    dk = dot_tn(dkk, kb, FAST) + dot_tn(dqk, q, FAST) + dkt * tau
    dq = dot_nn(dqk, k, FAST) + dqd * gamma
    dgamma = jnp.sum(dkbg * kb, axis=1, keepdims=True) + jnp.sum(dqd * q, axis=1, keepdims=True)
    dtau_tau = jnp.sum(dkt * k, axis=1, keepdims=True) * tau
    dgc = dgc + dgamma * gamma - dtau_tau
    is_last = (lax.broadcasted_iota(jnp.int32, (STACK, 1), 0) % CHUNK) == CHUNK - 1

    def last_term(h):
        s, ds_new = s_of(h), ds_new_of(h)
        dcd = jnp.sum(jnp.sum(ds_new * s, axis=1, keepdims=True), axis=0, keepdims=True)
        total = jnp.sum(dtau_tau[hr(h)], axis=0, keepdims=True) + dcd * cd[h * CHUNK:h * CHUNK + 1, :]
        return jnp.broadcast_to(total, (CHUNK, 1))

    dgc = dgc + jnp.where(is_last, _per_head(last_term), 0.0)
    dk = dk + dkb * beta
    dbeta = jnp.sum(dkb * k, axis=1, keepdims=True) + jnp.sum(dvb * v, axis=1, keepdims=True)
    dv = dvb * beta
    return dq, dk, dv, dbeta, dgc, ds


def _token_tile(t):
    return _pick(t, 256)


STRIP = 32
NORM_STRIP = 16


def _for_strips(n_rows, rows, fn):
    def step(r, carry):
        fn(pl.multiple_of(r * rows, rows))
        return carry

    lax.fori_loop(0, n_rows // rows, step, 0)


def _fold_rows(x):
    out = x[0:SUBLANES, :]
    for i in range(1, x.shape[0] // SUBLANES):
        out = out + x[i * SUBLANES:(i + 1) * SUBLANES, :]
    return out


def _rmsnorm_fwd(x, g):
    t, d = x.shape
    tm = _token_tile(t)

    def body(x_ref, g_ref, h_ref, r_ref):
        y, r = rms_fwd(x_ref[...], g_ref[...])
        h_ref[...] = y.astype(BF16)
        r_ref[...] = r

    return _pcall(
        body, grid=(t // tm,),
        in_specs=[pl.BlockSpec((tm, d), lambda i: (i, 0)), pl.BlockSpec((1, d), lambda i: (0, 0))],
        out_specs=[pl.BlockSpec((tm, d), lambda i: (i, 0)), pl.BlockSpec((tm, 1), lambda i: (i, 0))],
        out_shape=[jax.ShapeDtypeStruct((t, d), BF16), jax.ShapeDtypeStruct((t, 1), F32)],
        compiler_params=_params("parallel"), name="rmsnorm_fwd")(x, g)


def _rmsnorm_bwd(x, r, g, dh, dres):
    t, d = x.shape
    tm = _token_tile(t)

    def body(x_ref, r_ref, g_ref, dh_ref, dres_ref, dx_ref, dxb_ref, dg_ref):
        dx, dg_rows = rms_bwd(x_ref[...], r_ref[...], g_ref[...], dh_ref[...])
        dx = dx + dres_ref[...]
        dx_ref[...] = dx
        dxb_ref[...] = dx.astype(BF16)

        @pl.when(pl.program_id(0) == 0)
        def _():
            dg_ref[...] = jnp.zeros_like(dg_ref)

        dg_ref[...] += jnp.sum(dg_rows, axis=0, keepdims=True)

    tile = pl.BlockSpec((tm, d), lambda i: (i, 0))
    row = pl.BlockSpec((1, d), lambda i: (0, 0))
    return _pcall(
        body, grid=(t // tm,),
        in_specs=[tile, pl.BlockSpec((tm, 1), lambda i: (i, 0)), row, tile, tile],
        out_specs=[tile, tile, row],
        out_shape=[jax.ShapeDtypeStruct((t, d), F32), jax.ShapeDtypeStruct((t, d), BF16), jax.ShapeDtypeStruct((1, d), F32)],
        compiler_params=_params("arbitrary"), name="rmsnorm_bwd")(x, r, g, dh, dres)


def _matmul(a, b, mode, name, tiles, add=None, out_dtype=F32):
    if mode == "nn":
        (m, k), n = a.shape, b.shape[1]
    elif mode == "nt":
        (m, k), n = a.shape, b.shape[0]
    else:
        (k, m), n = a.shape, b.shape[1]
    tm, tn, tk = min(tiles[0], m), min(tiles[1], n), min(tiles[2], k)
    assert m % tm == 0 and n % tn == 0 and k % tk == 0, (name, m, n, k, tiles)
    nk = k // tk
    dims = {"nn": (((1,), (0,)), ((), ())), "nt": (((1,), (1,)), ((), ())), "tn": (((0,), (0,)), ((), ()))}[mode]

    def finish(res, add_ref, o_ref):
        if add_ref is not None:
            res = res + add_ref[...]
        o_ref[...] = res.astype(o_ref.dtype)

    def body(*refs):
        a_ref, b_ref = refs[0], refs[1]
        add_ref = refs[2] if add is not None else None
        o_ref = refs[3] if add is not None else refs[2]
        part = lax.dot_general(a_ref[...], b_ref[...], dims, preferred_element_type=F32)
        if nk == 1:
            finish(part, add_ref, o_ref)
            return
        acc_ref = refs[-1]
        kk = pl.program_id(2)

        @pl.when(kk == 0)
        def _():
            acc_ref[...] = part

        @pl.when(kk > 0)
        def _():
            acc_ref[...] += part

        @pl.when(kk == nk - 1)
        def _():
            finish(acc_ref[...], add_ref, o_ref)

    a_spec = pl.BlockSpec((tk, tm), lambda j, i, kk: (kk, i)) if mode == "tn" else pl.BlockSpec((tm, tk), lambda j, i, kk: (i, kk))
    b_spec = pl.BlockSpec((tn, tk), lambda j, i, kk: (j, kk)) if mode == "nt" else pl.BlockSpec((tk, tn), lambda j, i, kk: (kk, j))
    o_spec = pl.BlockSpec((tm, tn), lambda j, i, kk: (i, j))
    in_specs = [a_spec, b_spec] + ([o_spec] if add is not None else [])
    args = (a, b) + ((add,) if add is not None else ())
    return _pcall(
        body, grid=(n // tn, m // tm, nk), in_specs=in_specs, out_specs=o_spec,
        out_shape=jax.ShapeDtypeStruct((m, n), out_dtype),
        scratch_shapes=[pltpu.VMEM((tm, tn), F32)] if nk > 1 else [],
        compiler_params=_params("parallel", "parallel", "arbitrary"), name=name)(*args)


def _prev_halo_spec(tm, width, col_block):
    return pl.BlockSpec((HALO, width), lambda i: (jnp.maximum(i * (tm // HALO) - 1, 0), col_block))


def _fill_with_prev(xp_ref, tile, halo, first):
    xp_ref[0:HALO, :] = jnp.where(first, 0.0, halo)
    xp_ref[HALO:, :] = tile


def _delayed(xp_ref, row0, cols, taps):
    ext = xp_ref[pl.ds(row0, STRIP + HALO), cols]
    return [ext[HALO:, :]] + [pltpu.roll(ext, j, 0)[HALO:, :] for j in range(1, taps)]


def _causal_conv(delayed, w):
    taps = len(delayed)
    out = delayed[0] * w[taps - 1:taps, :]
    for j in range(1, taps):
        out = out + delayed[j] * w[taps - 1 - j:taps - j, :]
    return out


def _advanced_conv(buf_ref, row0, cols, w):
    taps = w.shape[0]
    ext = buf_ref[pl.ds(row0, STRIP + HALO), cols]
    out = ext[:STRIP, :] * w[taps - 1:taps, :]
    for j in range(1, taps):
        out = out + pltpu.roll(ext, STRIP + HALO - j, 0)[:STRIP, :] * w[taps - 1 - j:taps - j, :]
    return out


def _dn_prep(p, conv_w, a_log4, dt_bias4):
    t = p.shape[0]
    tm = _token_tile(t)
    w3 = 3 * DN_WIDTH

    def body(x_ref, halo_ref, pbd_ref, w_ref, alog_ref, dtb_ref, q_ref, k_ref, v_ref, beta_ref, g_ref, xp_ref):
        _fill_with_prev(xp_ref, x_ref[...], halo_ref[...], pl.program_id(0) == 0)

        def strip(row0):
            rows = pl.ds(row0, STRIP)
            for h in range(N_HEADS):
                sl = slice(h * HEAD_DIM, (h + 1) * HEAD_DIM)
                for part, out_ref in ((0, q_ref), (1, k_ref), (2, v_ref)):
                    cols = slice(part * DN_WIDTH + h * HEAD_DIM, part * DN_WIDTH + (h + 1) * HEAD_DIM)
                    y = silu(_causal_conv(_delayed(xp_ref, row0, cols, CONV_K), w_ref[:, cols]))
                    if part == 0:
                        y = l2_fwd(y)[0] * (HEAD_DIM ** -0.5)
                    elif part == 1:
                        y = l2_fwd(y)[0]
                    out_ref[rows, sl] = y
            head = lax.broadcasted_iota(jnp.int32, (STRIP, LANES), 1) < N_HEADS
            pbd = pbd_ref[rows, :]
            beta_ref[rows, :] = jnp.where(head, sigmoid(pbd), 0.0)
            a_raw = pltpu.roll(pbd, LANES - N_HEADS, 1)
            g_ref[rows, :] = jnp.where(head, -jnp.exp(alog_ref[...]) * softplus(a_raw + dtb_ref[...]), 0.0)

        _for_strips(tm, STRIP, strip)

    tok = lambda w, cb: pl.BlockSpec((tm, w), lambda i: (i, cb))
    full = lambda a: pl.BlockSpec(a.shape, lambda i: (0, 0))
    return _pcall(
        body, grid=(t // tm,),
        in_specs=[tok(w3, 0), _prev_halo_spec(tm, w3, 0), tok(LANES, PROJ_MAIN // LANES),
                  full(conv_w), full(a_log4), full(dt_bias4)],
        out_specs=[tok(DN_WIDTH, 0)] * 3 + [tok(LANES, 0)] * 2,
        out_shape=[jax.ShapeDtypeStruct((t, DN_WIDTH), F32)] * 3 + [jax.ShapeDtypeStruct((t, LANES), F32)] * 2,
        scratch_shapes=[pltpu.VMEM((HALO + tm, w3), F32)],
        compiler_params=_params("parallel"), name="dn_prep")(p, p, p, conv_w, a_log4, dt_bias4)


def _dn_prep_bwd(p, conv_w, a_log4, dt_bias4, dq, dk, dv, dbeta4, dg4, dp_buf):
    t = p.shape[0]
    tm = _token_tile(t)
    w3 = 3 * DN_WIDTH

    def body(x_ref, halo_ref, pbd_ref, w_ref, alog_ref, dtb_ref, dq_ref, dk_ref, dv_ref, dbeta_ref, dg_ref, _,
             dc_ref, dw_ref, dpbd_ref, dalog_ref, ddtb_ref, xp_ref, dw_acc, lane_acc):
        first = pl.program_id(0) == 0
        _fill_with_prev(xp_ref, x_ref[...], halo_ref[...], first)
        dw_acc[...] = jnp.zeros_like(dw_acc)
        lane_acc[...] = jnp.zeros_like(lane_acc)

        def strip(row0):
            rows = pl.ds(row0, STRIP)
            for h in range(N_HEADS):
                sl = slice(h * HEAD_DIM, (h + 1) * HEAD_DIM)
                for part, dy_ref in ((0, dq_ref), (1, dk_ref), (2, dv_ref)):
                    cols = slice(part * DN_WIDTH + h * HEAD_DIM, part * DN_WIDTH + (h + 1) * HEAD_DIM)
                    delayed = _delayed(xp_ref, row0, cols, CONV_K)
                    c = _causal_conv(delayed, w_ref[:, cols])
                    dy = dy_ref[rows, sl]
                    if part < 2:
                        y = silu(c)
                        _, r = l2_fwd(y)
                        dy = l2_bwd(y, r, dy * (HEAD_DIM ** -0.5) if part == 0 else dy)
                    dc = dy * silu_grad(c)
                    dc_ref[rows, cols] = dc
                    for j in range(CONV_K):
                        k = CONV_K - 1 - j
                        dw_acc[k * SUBLANES:(k + 1) * SUBLANES, cols] += _fold_rows(dc * delayed[j])
            head = lax.broadcasted_iota(jnp.int32, (STRIP, LANES), 1) < N_HEADS
            pbd = pbd_ref[rows, :]
            beta = sigmoid(pbd)
            dpb = jnp.where(head, dbeta_ref[rows, :] * beta * (1.0 - beta), 0.0)
            z = pltpu.roll(pbd, LANES - N_HEADS, 1) + dtb_ref[...]
            neg_rate = -jnp.exp(alog_ref[...])
            dg = dg_ref[rows, :]
            dpa = jnp.where(head, dg * neg_rate * sigmoid(z), 0.0)
            dpbd_ref[rows, :] = (dpb + pltpu.roll(dpa, N_HEADS, 1)).astype(BF16)
            g = jnp.where(head, neg_rate * softplus(z), 0.0)
            lane_acc[0:SUBLANES, :] += _fold_rows(dg * g)
            lane_acc[SUBLANES:, :] += _fold_rows(dpa)

        _for_strips(tm, STRIP, strip)

        @pl.when(first)
        def _():
            dw_ref[...] = jnp.zeros_like(dw_ref)
            dalog_ref[...] = jnp.zeros_like(dalog_ref)
            ddtb_ref[...] = jnp.zeros_like(ddtb_ref)

        for k in range(CONV_K):
            dw_ref[k:k + 1, :] += jnp.sum(dw_acc[k * SUBLANES:(k + 1) * SUBLANES, :], axis=0, keepdims=True)
        dalog_ref[...] += jnp.sum(lane_acc[0:SUBLANES, :], axis=0, keepdims=True)
        ddtb_ref[...] += jnp.sum(lane_acc[SUBLANES:, :], axis=0, keepdims=True)

    tok = lambda w, cb: pl.BlockSpec((tm, w), lambda i: (i, cb))
    full = lambda shape: pl.BlockSpec(shape, lambda i: (0, 0))
    return _pcall(
        body, grid=(t // tm,),
        in_specs=[tok(w3, 0), _prev_halo_spec(tm, w3, 0), tok(LANES, PROJ_MAIN // LANES),
                  full(conv_w.shape), full(a_log4.shape), full(dt_bias4.shape)] + [tok(DN_WIDTH, 0)] * 3 + [tok(LANES, 0)] * 2
        + [pl.BlockSpec(memory_space=pl.ANY)],
        out_specs=[tok(w3, 0), full((CONV_K, w3)), tok(LANES, PROJ_MAIN // LANES), full((1, LANES)), full((1, LANES))],
        out_shape=[jax.ShapeDtypeStruct((t, w3), F32), jax.ShapeDtypeStruct((CONV_K, w3), F32),
                   jax.ShapeDtypeStruct(dp_buf.shape, dp_buf.dtype),
                   jax.ShapeDtypeStruct((1, LANES), F32), jax.ShapeDtypeStruct((1, LANES), F32)],
        input_output_aliases={11: 2},
        scratch_shapes=[pltpu.VMEM((HALO + tm, w3), F32), pltpu.VMEM((CONV_K * SUBLANES, w3), F32),
                        pltpu.VMEM((2 * SUBLANES, LANES), F32)],
        compiler_params=_params("arbitrary"), name="dn_prep_bwd")(p, p, p, conv_w, a_log4, dt_bias4, dq, dk, dv, dbeta4, dg4, dp_buf)


def _conv_bwd_input(dc, w, name, out_cols=None, col_block=0, into=None):
    t, c = dc.shape
    taps = w.shape[0]
    tm = _token_tile(t)
    ct = _pick(c, 1536)
    n_tok = t // tm
    out_cols = c if out_cols is None else out_cols

    def body(dc_ref, next_ref, w_ref, *rest):
        dx_ref, buf_ref = rest[-2], rest[-1]
        buf_ref[0:tm, :] = dc_ref[...]
        buf_ref[tm:, :] = jnp.where(pl.program_id(0) == n_tok - 1, 0.0, next_ref[...])

        def strip(row0):
            for c0 in range(0, ct, LANES):
                cols = slice(c0, c0 + LANES)
                dx_ref[pl.ds(row0, STRIP), cols] = _advanced_conv(buf_ref, row0, cols, w_ref[:, cols]).astype(BF16)

        _for_strips(tm, STRIP, strip)

    in_specs = [pl.BlockSpec((tm, ct), lambda i, j: (i, j)),
                pl.BlockSpec((HALO, ct), lambda i, j: (jnp.minimum((i + 1) * (tm // HALO), t // HALO - 1), j)),
                pl.BlockSpec((taps, ct), lambda i, j: (0, j))]
    args = (dc, dc, w)
    aliases = {}
    if into is not None:
        in_specs.append(pl.BlockSpec(memory_space=pl.ANY))
        args += (into,)
        aliases = {3: 0}
    return _pcall(
        body, grid=(n_tok, c // ct), in_specs=in_specs,
        out_specs=pl.BlockSpec((tm, ct), lambda i, j: (i, j + col_block)),
        out_shape=jax.ShapeDtypeStruct((t, out_cols), BF16), input_output_aliases=aliases,
        scratch_shapes=[pltpu.VMEM((tm + HALO, ct), F32)],
        compiler_params=_params("parallel", "parallel"), name=name)(*args)


def _dn_forward(q, k, v, beta4, g4, p, norm_g):
    t = q.shape[0]
    n = t // CHUNK

    def body(q_ref, k_ref, v_ref, b_ref, g_ref, gate_ref, ng_ref, mix_ref, s_all_ref, s_ref):
        @pl.when(pl.program_id(0) == 0)
        def _():
            s_ref[...] = jnp.zeros_like(s_ref)

        s_all_ref[0] = s_ref[...]
        o, s_new, _ = dn_chunk_fwd(_stack_heads(q_ref[...]), _stack_heads(k_ref[...]), _stack_heads(v_ref[...]),
                                   _stack_lanes(b_ref[...]), chunk_cumsum(g_ref[...]), lambda h: s_ref[h])
        o_n, _ = rms_fwd(o, ng_ref[...])
        for h in range(N_HEADS):
            sl = slice(h * HEAD_DIM, (h + 1) * HEAD_DIM)
            s_ref[h] = s_new[h]
            mix_ref[:, sl] = (o_n[_head_rows(h)] * silu(gate_ref[:, sl])).astype(BF16)

    ch = lambda w, cb: pl.BlockSpec((CHUNK, w), lambda i: (i, cb))
    return _pcall(
        body, grid=(n,),
        in_specs=[ch(DN_WIDTH, 0)] * 3 + [ch(LANES, 0)] * 2 + [ch(DN_WIDTH, 3), pl.BlockSpec((1, HEAD_DIM), lambda i: (0, 0))],
        out_specs=[ch(DN_WIDTH, 0), pl.BlockSpec((1, N_HEADS, HEAD_DIM, HEAD_DIM), lambda i: (i, 0, 0, 0))],
        out_shape=[jax.ShapeDtypeStruct((t, DN_WIDTH + SG_WIDTH), BF16), jax.ShapeDtypeStruct((n, N_HEADS, HEAD_DIM, HEAD_DIM), F32)],
        scratch_shapes=[pltpu.VMEM((N_HEADS, HEAD_DIM, HEAD_DIM), F32)],
        compiler_params=_params("arbitrary"), name="dn_forward")(q, k, v, beta4, g4, p, norm_g)


def _dn_backward(q, k, v, beta4, g4, p, norm_g, s_all, dmix, dp_buf):
    t = q.shape[0]
    n = t // CHUNK

    def body(q_ref, k_ref, v_ref, b_ref, g_ref, gate_ref, ng_ref, s_in_ref, dmix_ref, _,
             dq_ref, dk_ref, dv_ref, db_ref, dg_ref, dgate_ref, dng_ref, ds_ref):
        @pl.when(pl.program_id(0) == 0)
        def _():
            ds_ref[...] = jnp.zeros_like(ds_ref)
            dng_ref[...] = jnp.zeros_like(dng_ref)

        q, k, v, beta = _stack_heads(q_ref[...]), _stack_heads(k_ref[...]), _stack_heads(v_ref[...]), _stack_lanes(b_ref[...])
        s_of = lambda h: s_in_ref[0, h]
        o, _, loc = dn_chunk_fwd(q, k, v, beta, chunk_cumsum(g_ref[...]), s_of)
        o_n, r = rms_fwd(o, ng_ref[...])
        gate = _stack_heads(gate_ref[...])
        dmx = _stack_heads(dmix_ref[...])
        dgate = dmx * o_n * silu_grad(gate)
        do, dng_rows = rms_bwd(o, r, ng_ref[...], dmx * silu(gate))
        dng_ref[...] += jnp.sum(dng_rows, axis=0, keepdims=True)
        dq, dk, dv, dbeta, dgc, ds = dn_chunk_bwd(loc, q, k, v, beta, s_of, do, lambda h: ds_ref[h])
        lane = lax.broadcasted_iota(jnp.int32, (CHUNK, LANES), 1)
        db4 = jnp.zeros((CHUNK, LANES), F32)
        dgc4 = jnp.zeros((CHUNK, LANES), F32)
        for h in range(N_HEADS):
            sl = slice(h * HEAD_DIM, (h + 1) * HEAD_DIM)
            rows = _head_rows(h)
            dgate_ref[:, sl] = dgate[rows].astype(BF16)
            dq_ref[:, sl] = dq[rows]
            dk_ref[:, sl] = dk[rows]
            dv_ref[:, sl] = dv[rows]
            ds_ref[h] = ds[h]
            db4 = jnp.where(lane == h, dbeta[rows], db4)
            dgc4 = jnp.where(lane == h, dgc[rows], dgc4)
        _, strict = _tri_masks(CHUNK)
        db_ref[...] = db4
        dg_ref[...] = dot_nn(jnp.logical_not(strict).astype(F32), dgc4)

    rev = lambda w, cb: pl.BlockSpec((CHUNK, w), lambda i: (n - 1 - i, cb))
    return _pcall(
        body, grid=(n,),
        in_specs=[rev(DN_WIDTH, 0)] * 3 + [rev(LANES, 0)] * 2 + [rev(DN_WIDTH, 3), pl.BlockSpec((1, HEAD_DIM), lambda i: (0, 0)),
                  pl.BlockSpec((1, N_HEADS, HEAD_DIM, HEAD_DIM), lambda i: (n - 1 - i, 0, 0, 0)), rev(DN_WIDTH, 0),
                  pl.BlockSpec(memory_space=pl.ANY)],
        out_specs=[rev(DN_WIDTH, 0)] * 3 + [rev(LANES, 0)] * 2 + [rev(DN_WIDTH, 3), pl.BlockSpec((1, HEAD_DIM), lambda i: (0, 0))],
        out_shape=[jax.ShapeDtypeStruct((t, DN_WIDTH), F32)] * 3 + [jax.ShapeDtypeStruct((t, LANES), F32)] * 2
        + [jax.ShapeDtypeStruct(dp_buf.shape, dp_buf.dtype), jax.ShapeDtypeStruct((1, HEAD_DIM), F32)],
        input_output_aliases={9: 5},
        scratch_shapes=[pltpu.VMEM((N_HEADS, HEAD_DIM, HEAD_DIM), F32)],
        compiler_params=_params("arbitrary"), name="dn_backward")(q, k, v, beta4, g4, p, norm_g, s_all, dmix, dp_buf)


def _sg_mask():
    row = lax.broadcasted_iota(jnp.int32, (SG_BLOCK, SG_BLOCK), 0)
    col = lax.broadcasted_iota(jnp.int32, (SG_BLOCK, SG_BLOCK), 1)
    return (col // CHUNK) <= (row // CHUNK)


def _sg_forward(p, norm_g, w_s, b_t, mix_buf):
    t = p.shape[0]

    def body(u_ref, v_ref, ng_ref, w_ref, b_ref, _, o_ref):
        mask = _sg_mask()
        for g in range(SG_GROUPS):
            sl = slice(g * SG_DIM, (g + 1) * SG_DIM)
            vn, _ = rms_fwd(gelu(v_ref[:, sl]), ng_ref[:, sl])
            s = dot_nn(jnp.where(mask, w_ref[g], 0.0), vn, FAST) + b_ref[:, g:g + 1]
            o_ref[:, sl] = (gelu(u_ref[:, sl]) * s).astype(BF16)

    blk = lambda cb: pl.BlockSpec((SG_BLOCK, SG_WIDTH), lambda i: (i, cb))
    return _pcall(
        body, grid=(t // SG_BLOCK,),
        in_specs=[blk(4), blk(5), pl.BlockSpec((1, SG_WIDTH), lambda i: (0, 0)),
                  pl.BlockSpec((SG_GROUPS, SG_BLOCK, SG_BLOCK), lambda i: (0, 0, 0)), pl.BlockSpec((SG_BLOCK, SG_GROUPS), lambda i: (0, 0)),
                  pl.BlockSpec(memory_space=pl.ANY)],
        out_specs=blk(1), out_shape=jax.ShapeDtypeStruct(mix_buf.shape, mix_buf.dtype), input_output_aliases={5: 0},
        compiler_params=_params("parallel"), name="sg_forward")(p, p, norm_g, w_s, b_t, mix_buf)


def _sg_backward(p, norm_g, w_s, b_t, dmix):
    t = p.shape[0]

    def body(u_ref, v_ref, ng_ref, w_ref, b_ref, do_ref, duv_ref, dng_ref, dw_ref, db_ref):
        @pl.when(pl.program_id(0) == 0)
        def _():
            dng_ref[...] = jnp.zeros_like(dng_ref)
            dw_ref[...] = jnp.zeros_like(dw_ref)
            db_ref[...] = jnp.zeros_like(db_ref)

        mask = _sg_mask()
        lane = lax.broadcasted_iota(jnp.int32, (SG_BLOCK, LANES), 1)
        db = jnp.zeros((SG_BLOCK, LANES), F32)
        for g in range(SG_GROUPS):
            sl = slice(g * SG_DIM, (g + 1) * SG_DIM)
            u_raw, v_raw, do = u_ref[:, sl], v_ref[:, sl], do_ref[:, sl]
            vg = gelu(v_raw)
            vn, r = rms_fwd(vg, ng_ref[:, sl])
            w_m = jnp.where(mask, w_ref[g], 0.0)
            s = dot_nn(w_m, vn, FAST) + b_ref[:, g:g + 1]
            duv_ref[:, sl] = (do * s * gelu_grad(u_raw)).astype(BF16)
            ds = do * gelu(u_raw)
            db = jnp.where(lane == g, jnp.sum(ds, axis=1, keepdims=True), db)
            dw_ref[g] += jnp.where(mask, dot_nt(ds, vn, FAST), 0.0)
            dvg, dng_rows = rms_bwd(vg, r, ng_ref[:, sl], dot_tn(w_m, ds, FAST))
            dng_ref[:, sl] += jnp.sum(dng_rows, axis=0, keepdims=True)
            duv_ref[:, SG_WIDTH + g * SG_DIM:SG_WIDTH + (g + 1) * SG_DIM] = (dvg * gelu_grad(v_raw)).astype(BF16)
        db_ref[...] += db

    blk = lambda cb: pl.BlockSpec((SG_BLOCK, SG_WIDTH), lambda i: (i, cb))
    const2 = lambda shape: pl.BlockSpec(shape, lambda i: (0, 0))
    w_spec = pl.BlockSpec((SG_GROUPS, SG_BLOCK, SG_BLOCK), lambda i: (0, 0, 0))
    return _pcall(
        body, grid=(t // SG_BLOCK,),
        in_specs=[blk(4), blk(5), const2((1, SG_WIDTH)), w_spec, const2((SG_BLOCK, SG_GROUPS)), blk(1)],
        out_specs=[pl.BlockSpec((SG_BLOCK, 2 * SG_WIDTH), lambda i: (i, 2)), const2((1, SG_WIDTH)), w_spec,
                   const2((SG_BLOCK, LANES))],
        out_shape=[jax.ShapeDtypeStruct((t, PROJ_PAD), BF16), jax.ShapeDtypeStruct((1, SG_WIDTH), F32),
                   jax.ShapeDtypeStruct((SG_GROUPS, SG_BLOCK, SG_BLOCK), F32), jax.ShapeDtypeStruct((SG_BLOCK, LANES), F32)],
        compiler_params=_params("arbitrary"), name="sg_backward")(p, p, norm_g, w_s, b_t, dmix)


FFN_CT = D_FF // 2


def _ffn_act(up, conv_w, conv_b):
    t = up.shape[0]
    tm = _token_tile(t)
    nj = D_FF // FFN_CT

    def body(ug_ref, uv_ref, hg_ref, hv_ref, wg_ref, wv_ref, bg_ref, bv_ref, act_ref, xg_ref, xv_ref):
        first = pl.program_id(0) == 0
        _fill_with_prev(xg_ref, ug_ref[...], hg_ref[...], first)
        _fill_with_prev(xv_ref, uv_ref[...], hv_ref[...], first)

        def strip(row0):
            for c0 in range(0, FFN_CT, LANES):
                cols = slice(c0, c0 + LANES)
                cg = _causal_conv(_delayed(xg_ref, row0, cols, FFN_CONV), wg_ref[:, cols]) + bg_ref[:, cols]
                cv = _causal_conv(_delayed(xv_ref, row0, cols, FFN_CONV), wv_ref[:, cols]) + bv_ref[:, cols]
                act_ref[pl.ds(row0, STRIP), cols] = (silu(cg) * cv).astype(BF16)

        _for_strips(tm, STRIP, strip)

    tok = lambda off: pl.BlockSpec((tm, FFN_CT), lambda i, j: (i, j + off))
    halo = lambda off: pl.BlockSpec((HALO, FFN_CT), lambda i, j: (jnp.maximum(i * (tm // HALO) - 1, 0), j + off))
    par = lambda rows, off: pl.BlockSpec((rows, FFN_CT), lambda i, j: (0, j + off))
    return _pcall(
        body, grid=(t // tm, nj),
        in_specs=[tok(0), tok(nj), halo(0), halo(nj), par(FFN_CONV, 0), par(FFN_CONV, nj), par(1, 0), par(1, nj)],
        out_specs=pl.BlockSpec((tm, FFN_CT), lambda i, j: (i, j)),
        out_shape=jax.ShapeDtypeStruct((t, D_FF), BF16),
        scratch_shapes=[pltpu.VMEM((HALO + tm, FFN_CT), F32)] * 2,
        compiler_params=_params("parallel", "parallel"), name="ffn_act")(up, up, up, up, conv_w, conv_w, conv_b, conv_b)


def _ffn_bwd(up, conv_w, conv_b, dact):
    t = up.shape[0]
    tm = _pick(t, 128)
    n_tok = t // tm
    width = 2 * D_FF

    def dconv(delayed_g, delayed_v, da, wg, wv, bg, bv):
        cg = _causal_conv(delayed_g, wg) + bg
        cv = _causal_conv(delayed_v, wv) + bv
        s = sigmoid(cg)
        return da * cv * (s * (1.0 + cg * (1.0 - s))), da * (cg * s)

    def body(up_ref, prev_ref, next_ref, da_ref, dan_ref, w_ref, b_ref, dup_ref, dw_ref, db_ref, xp_ref, dc_ref, dw_acc, db_acc):
        first = pl.program_id(0) == 0
        last = pl.program_id(0) == n_tok - 1
        xp_ref[0:HALO, :] = jnp.where(first, 0.0, prev_ref[...])
        xp_ref[HALO:HALO + tm, :] = up_ref[...]
        xp_ref[HALO + tm:, :] = next_ref[...]
        dw_acc[...] = jnp.zeros_like(dw_acc)
        db_acc[...] = jnp.zeros_like(db_acc)

        def strip(row0):
            rows = pl.ds(row0, STRIP)
            for c0 in range(0, D_FF, LANES):
                gc, vc = slice(c0, c0 + LANES), slice(D_FF + c0, D_FF + c0 + LANES)
                del_g, del_v = _delayed(xp_ref, row0, gc, FFN_CONV), _delayed(xp_ref, row0, vc, FFN_CONV)
                dcg, dcv = dconv(del_g, del_v, da_ref[rows, gc], w_ref[:, gc], w_ref[:, vc], b_ref[:, gc], b_ref[:, vc])
                dc_ref[rows, gc] = dcg
                dc_ref[rows, vc] = dcv
                db_acc[:, gc] += _fold_rows(dcg)
                db_acc[:, vc] += _fold_rows(dcv)
                for j in range(FFN_CONV):
                    k = FFN_CONV - 1 - j
                    dw_acc[k * SUBLANES:(k + 1) * SUBLANES, gc] += _fold_rows(dcg * del_g[j])
                    dw_acc[k * SUBLANES:(k + 1) * SUBLANES, vc] += _fold_rows(dcv * del_v[j])

        _for_strips(tm, STRIP, strip)

        for c0 in range(0, D_FF, LANES):
            gc, vc = slice(c0, c0 + LANES), slice(D_FF + c0, D_FF + c0 + LANES)

            def delayed(cols):
                ext = xp_ref[tm:tm + 2 * HALO, cols]
                return [ext[HALO:, :]] + [pltpu.roll(ext, j, 0)[HALO:, :] for j in range(1, FFN_CONV)]

            dcg, dcv = dconv(delayed(gc), delayed(vc), dan_ref[:, gc], w_ref[:, gc], w_ref[:, vc], b_ref[:, gc], b_ref[:, vc])
            dc_ref[tm:, gc] = jnp.where(last, 0.0, dcg)
            dc_ref[tm:, vc] = jnp.where(last, 0.0, dcv)

        def strip_dx(row0):
            for c0 in range(0, width, LANES):
                cols = slice(c0, c0 + LANES)
                dup_ref[pl.ds(row0, STRIP), cols] = _advanced_conv(dc_ref, row0, cols, w_ref[:, cols]).astype(BF16)

        _for_strips(tm, STRIP, strip_dx)

        @pl.when(first)
        def _():
            dw_ref[...] = jnp.zeros_like(dw_ref)
            db_ref[...] = jnp.zeros_like(db_ref)

        for k in range(FFN_CONV):
            dw_ref[k:k + 1, :] += jnp.sum(dw_acc[k * SUBLANES:(k + 1) * SUBLANES, :], axis=0, keepdims=True)
        db_ref[...] += jnp.sum(db_acc[...], axis=0, keepdims=True)

    next_rows = lambda i: jnp.minimum((i + 1) * (tm // HALO), t // HALO - 1)
    full = lambda rows: pl.BlockSpec((rows, width), lambda i: (0, 0))
    return _pcall(
        body, grid=(n_tok,),
        in_specs=[pl.BlockSpec((tm, width), lambda i: (i, 0)),
                  pl.BlockSpec((HALO, width), lambda i: (jnp.maximum(i * (tm // HALO) - 1, 0), 0)),
                  pl.BlockSpec((HALO, width), lambda i: (next_rows(i), 0)),
                  pl.BlockSpec((tm, D_FF), lambda i: (i, 0)), pl.BlockSpec((HALO, D_FF), lambda i: (next_rows(i), 0)),
                  full(FFN_CONV), full(1)],
        out_specs=[pl.BlockSpec((tm, width), lambda i: (i, 0)), full(FFN_CONV), full(1)],
        out_shape=[jax.ShapeDtypeStruct((t, width), BF16), jax.ShapeDtypeStruct((FFN_CONV, width), F32),
                   jax.ShapeDtypeStruct((1, width), F32)],
        scratch_shapes=[pltpu.VMEM((tm + 2 * HALO, width), F32), pltpu.VMEM((tm + HALO, width), F32),
                        pltpu.VMEM((FFN_CONV * SUBLANES, width), F32), pltpu.VMEM((SUBLANES, width), F32)],
        compiler_params=_params("arbitrary"), name="ffn_bwd")(up, up, up, dact, dact, conv_w, conv_b)


def _final_loss(x3, target, g):
    t, d = x3.shape
    tm = _token_tile(t)

    def body(x_ref, t_ref, g_ref, loss_ref, dx_ref, dxb_ref, dg_ref):
        @pl.when(pl.program_id(0) == 0)
        def _():
            loss_ref[...] = jnp.zeros_like(loss_ref)
            dg_ref[...] = jnp.zeros_like(dg_ref)

        x = x_ref[...]
        y, r = rms_fwd(x, g_ref[...])
        err = y - t_ref[...]
        per_tok = jnp.mean(err * err, axis=-1, keepdims=True)
        loss_ref[...] += 0.5 * jnp.sum(per_tok, axis=0, keepdims=True)
        dx, dg_rows = rms_bwd(x, r, g_ref[...], err * (1.0 / d))
        dx_ref[...] = dx
        dxb_ref[...] = dx.astype(BF16)
        dg_ref[...] += jnp.sum(dg_rows, axis=0, keepdims=True)

    tile = pl.BlockSpec((tm, d), lambda i: (i, 0))
    row = pl.BlockSpec((1, d), lambda i: (0, 0))
    return _pcall(
        body, grid=(t // tm,), in_specs=[tile, tile, row],
        out_specs=[pl.BlockSpec((1, LANES), lambda i: (0, 0)), tile, tile, row],
        out_shape=[jax.ShapeDtypeStruct((1, LANES), F32), jax.ShapeDtypeStruct((t, d), F32), jax.ShapeDtypeStruct((t, d), BF16),
                   jax.ShapeDtypeStruct((1, d), F32)],
        compiler_params=_params("arbitrary"), name="final_loss")(x3, target, g)


def _my_position():
    return lax.axis_index("x"), lax.axis_index("y"), lax.axis_index("c")


COPIES = N_DEV - 1


def _all_gather(arrays):
    n = len(arrays)

    def body(*refs):
        x_refs, out_refs = refs[:n], refs[n:2 * n]
        send_sems, recv_sems, local_sems = refs[2 * n:]
        x, y, cc = _my_position()
        me, sibling = (x, y, cc), (x, y, 1 - cc)
        chips = [(1 - x, y), (x, 1 - y), (1 - x, 1 - y)]

        def block(a, px, py, pc):
            return out_refs[a].at[4 * px + 2 * py + pc]

        def copy(a, k, blk, to, src=None):
            return pltpu.make_async_remote_copy(
                src_ref=block(a, *blk) if src is None else src, dst_ref=block(a, *blk),
                send_sem=send_sems.at[a * COPIES + k], recv_sem=recv_sems.at[a * COPIES + k],
                device_id=to, device_id_type=MESH_ID)

        mine = [pltpu.make_async_copy(x_refs[a], block(a, *me), local_sems.at[a]) for a in range(n)]
        for cp in mine:
            cp.start()
        first = []
        for a in range(n):
            first.append(copy(a, 0, me, sibling, src=x_refs[a]))
            first += [copy(a, 1 + j, me, (*chip, cc), src=x_refs[a]) for j, chip in enumerate(chips)]
        for cp in first:
            cp.start()
        passed = []
        for j, chip in enumerate(chips):
            for a in range(n):
                copy(a, 1 + j, (*chip, cc), me).wait_recv()
                passed.append(copy(a, 4 + j, (*chip, cc), sibling))
                passed[-1].start()
        for a in range(n):
            copy(a, 0, sibling, me).wait_recv()
        for j, chip in enumerate(chips):
            for a in range(n):
                copy(a, 4 + j, (*chip, 1 - cc), me).wait_recv()
        for cp in first + passed:
            cp.wait_send()
        for cp in mine:
            cp.wait()

    any_spec = pl.BlockSpec(memory_space=pl.ANY)
    return _pcall(
        body, out_shape=[jax.ShapeDtypeStruct((N_DEV,) + a.shape, a.dtype) for a in arrays],
        in_specs=[any_spec] * n, out_specs=[any_spec] * n,
        scratch_shapes=[pltpu.SemaphoreType.DMA((n * COPIES,)), pltpu.SemaphoreType.DMA((n * COPIES,)),
                        pltpu.SemaphoreType.DMA((n,))],
        name="all_gather")(*arrays)


def _all_to_all(sends):
    n = len(sends)

    def body(*refs):
        send_refs, recv_refs = refs[:n], refs[n:2 * n]
        send_sems, recv_sems, local_sems = refs[2 * n:]
        x, y, cc = _my_position()
        me = 4 * x + 2 * y + cc
        mine = [pltpu.make_async_copy(send_refs[a].at[me], recv_refs[a].at[me], local_sems.at[a]) for a in range(n)]
        for cp in mine:
            cp.start()
        copies = []
        for rel in range(1, N_DEV):
            px, py, pc = x ^ (rel >> 2), y ^ ((rel >> 1) & 1), cc ^ (rel & 1)
            for a in range(n):
                copies.append(pltpu.make_async_remote_copy(
                    src_ref=send_refs[a].at[4 * px + 2 * py + pc], dst_ref=recv_refs[a].at[me],
                    send_sem=send_sems.at[a * COPIES + rel - 1], recv_sem=recv_sems.at[a * COPIES + rel - 1],
                    device_id=(px, py, pc), device_id_type=MESH_ID))
        for cp in copies:
            cp.start()
        for cp in copies:
            cp.wait()
        for cp in mine:
            cp.wait()

    any_spec = pl.BlockSpec(memory_space=pl.ANY)
    return _pcall(
        body, out_shape=[jax.ShapeDtypeStruct(s.shape, s.dtype) for s in sends],
        in_specs=[any_spec] * n, out_specs=[any_spec] * n,
        scratch_shapes=[pltpu.SemaphoreType.DMA((n * COPIES,)), pltpu.SemaphoreType.DMA((n * COPIES,)),
                        pltpu.SemaphoreType.DMA((n,))],
        name="all_to_all")(*sends)


def _hbm(a):
    return pltpu.with_memory_space_constraint(a, pltpu.HBM)


def _split_copies(send_refs, land_refs, send_sems, recv_sems, local_sems, gather):
    x, y, cc = _my_position()
    me = 4 * x + 2 * y + cc
    local, remote = [], []
    for a, (send, land) in enumerate(zip(send_refs, land_refs)):
        local.append(pltpu.make_async_copy(send if gather else send.at[me], land.at[me], local_sems.at[a]))
    for a, (send, land) in enumerate(zip(send_refs, land_refs)):
        for rel in range(1, N_DEV):
            px, py, pc = x ^ (rel >> 2), y ^ ((rel >> 1) & 1), cc ^ (rel & 1)
            remote.append(pltpu.make_async_remote_copy(
                src_ref=send if gather else send.at[4 * px + 2 * py + pc], dst_ref=land.at[me],
                send_sem=send_sems.at[a * COPIES + rel - 1], recv_sem=recv_sems.at[a * COPIES + rel - 1],
                device_id=(px, py, pc), device_id_type=MESH_ID))
    return local, remote


SPLIT_EFFECT = pltpu.SideEffectType.DATAFLOW_SIDE_EFFECTING


def _exchange_start(sends, after, gather, name):
    n = len(sends)
    lands = [_hbm(lax.empty((N_DEV,) + s.shape if gather else s.shape, s.dtype)) for s in sends]

    def body(*refs):
        send_refs, land_refs = refs[:n], refs[n:2 * n]
        send_sems, recv_sems, local_sems = refs[2 * n + 1:2 * n + 4]
        token = refs[-1]
        local, remote = _split_copies(send_refs, land_refs, send_sems, recv_sems, local_sems, gather)
        for cp in local + remote:
            cp.start()
        token[...] = jnp.zeros_like(token)

    hbm, sem = pl.BlockSpec(memory_space=pltpu.HBM), pl.BlockSpec(memory_space=pltpu.SEMAPHORE)
    out = _pcall(
        body, name=name,
        out_shape=[pltpu.SemaphoreType.DMA((n * COPIES,)), pltpu.SemaphoreType.DMA((n * COPIES,)), pltpu.SemaphoreType.DMA((n,))]
        + [pltpu.HBM(s.shape, s.dtype) for s in sends] + [pltpu.HBM(z.shape, z.dtype) for z in lands]
        + [jax.ShapeDtypeStruct((SUBLANES, LANES), F32)],
        in_specs=[hbm] * (2 * n) + [pl.BlockSpec(memory_space=pl.ANY)],
        out_specs=[sem] * 3 + [hbm] * (2 * n) + [pl.BlockSpec(memory_space=pltpu.VMEM)],
        input_output_aliases={i: 3 + i for i in range(2 * n)},
        compiler_params=pltpu.CompilerParams(has_side_effects=SPLIT_EFFECT),
    )(*[_hbm(s) for s in sends], *lands, after)
    return dict(sems=out[:3], sends=out[3:3 + n], lands=out[3 + n:3 + 2 * n], gather=gather), out[-1]


def _exchange_wait(handle, after, name):
    sends, lands, gather = handle["sends"], handle["lands"], handle["gather"]
    n = len(sends)

    def body(*refs):
        send_refs, land_refs = refs[:n], refs[n:2 * n]
        send_sems, recv_sems, local_sems = refs[2 * n:2 * n + 3]
        local, remote = _split_copies(send_refs, land_refs, send_sems, recv_sems, local_sems, gather)
        for cp in remote:
            cp.wait_send()
            cp.wait_recv()
        for cp in local:
            cp.wait()

    hbm, sem = pl.BlockSpec(memory_space=pltpu.HBM), pl.BlockSpec(memory_space=pltpu.SEMAPHORE)
    out = _pcall(
        body, name=name,
        out_shape=[pltpu.HBM(s.shape, s.dtype) for s in sends] + [pltpu.HBM(z.shape, z.dtype) for z in lands],
        in_specs=[hbm] * (2 * n) + [sem] * 3 + [pl.BlockSpec(memory_space=pl.ANY)],
        out_specs=[hbm] * (2 * n), input_output_aliases={i: i for i in range(2 * n)},
        compiler_params=pltpu.CompilerParams(has_side_effects=SPLIT_EFFECT),
    )(*sends, *lands, *handle["sems"], after)
    return out[n:]


def _join_shards(g, n_local, out_cols, name):
    _, r, wp = g.shape
    tr = min(r, 256)

    def body(g_ref, o_ref, acc_ref):
        acc_ref[...] = jnp.zeros_like(acc_ref)
        for k in range(N_DEV):
            shift = (n_local * k) % LANES
            start = n_local * k - shift
            piece = g_ref[k].astype(F32)
            if shift:
                piece = pltpu.roll(piece, shift, 1)
            acc_ref[:, start:start + wp] += piece
        o_ref[...] = acc_ref[...].astype(BF16)

    return _pcall(
        body, grid=(r // tr,), in_specs=[pl.BlockSpec((N_DEV, tr, wp), lambda i: (0, i, 0))],
        out_specs=pl.BlockSpec((tr, out_cols), lambda i: (i, 0)), out_shape=jax.ShapeDtypeStruct((r, out_cols), BF16),
        scratch_shapes=[pltpu.VMEM((tr, out_cols), F32)], compiler_params=_params("parallel"), name=name)(g)


def _split_shards(full, n_local, wp, name):
    r, c = full.shape
    tr = min(r, 256)

    def body(x_ref, o_ref):
        lane = lax.broadcasted_iota(jnp.int32, (tr, wp), 1)
        for k in range(N_DEV):
            shift = (n_local * k) % LANES
            start = n_local * k - shift
            win = x_ref[:, start:start + wp]
            if shift:
                win = pltpu.roll(win, wp - shift, 1)
            o_ref[k] = jnp.where(lane < n_local, win, 0.0).astype(BF16)

    return _pcall(
        body, grid=(r // tr,), in_specs=[pl.BlockSpec((tr, c), lambda i: (i, 0))],
        out_specs=pl.BlockSpec((N_DEV, tr, wp), lambda i: (0, i, 0)), out_shape=jax.ShapeDtypeStruct((N_DEV, r, wp), BF16),
        compiler_params=_params("parallel"), name=name)(full)


def _sum_and_adamw(recv, w, m, v, name):
    _, r, wp = recv.shape
    c = w.shape[1]
    tr = SLAB_ROW_TILE if r % SLAB_ROW_TILE == 0 else (SLAB_ROW_TILE // 4 if r % (SLAB_ROW_TILE // 4) == 0 else r)
    bc1 = 1.0 - ADAM_B1 ** ADAM_STEP
    bc2 = 1.0 - ADAM_B2 ** ADAM_STEP

    def body(recv_ref, w_ref, m_ref, v_ref, g_ref, d_ref, nm_ref, nv_ref):
        g = recv_ref[0, :, 0:c].astype(F32)
        for s in range(1, N_DEV):
            g = g + recv_ref[s, :, 0:c].astype(F32)
        m_new = ADAM_B1 * m_ref[...] + (1.0 - ADAM_B1) * g
        v_new = ADAM_B2 * v_ref[...] + (1.0 - ADAM_B2) * (g * g)
        m_hat = m_new / bc1
        v_hat = v_new / bc2
        g_ref[...] = g
        d_ref[...] = -ADAM_LR * (m_hat / (jnp.sqrt(v_hat) + ADAM_EPS) + ADAM_WD * w_ref[...])
        nm_ref[...] = m_new
        nv_ref[...] = v_new

    tile = pl.BlockSpec((tr, c), lambda i: (i, 0))
    return _pcall(
        body, grid=(r // tr,),
        in_specs=[pl.BlockSpec((N_DEV, tr, wp), lambda i: (0, i, 0)), tile, tile, tile],
        out_specs=[tile] * 4, out_shape=[jax.ShapeDtypeStruct((r, c), F32)] * 4,
        compiler_params=_params("parallel"), name=name)(recv, w, m, v)


SHARDED_TAPS = ("dn_conv_w", "ffn_conv_w")
REPLICATED = ("attn_norm_g", "dn_a_log", "dn_dt_bias", "dn_out_norm_g", "sg_norm_g", "sg_w", "sg_b", "ffn_norm_g",
              "ffn_conv_b", "final_norm_g")
SMALL = SHARDED_TAPS + REPLICATED
WEIGHT_ORDER = ("attn_norm_g", "w_in", "dn_conv_w", "dn_a_log", "dn_dt_bias", "dn_out_norm_g", "sg_norm_g", "sg_w", "sg_b",
                "w_out", "ffn_norm_g", "w_up", "ffn_conv_w", "ffn_conv_b", "w_down", "final_norm_g")
SLAB_COLS = 1024
SLAB_ROW_TILE = 128


def _pad_to(flat, multiple):
    pad = (-flat.shape[-1]) % multiple
    if pad == 0:
        return flat
    return jnp.pad(flat, [(0, 0)] * (flat.ndim - 1) + [(0, pad)])


def _lane_padded(n):
    return -(-n // LANES) * LANES


def _pack_small(named):
    flat = jnp.concatenate([named[n].reshape(-1) for n in SMALL])
    return _pad_to(flat, SUBLANES * SLAB_COLS).reshape(-1, SLAB_COLS)


def _unpack_small(slab, like):
    flat = slab.reshape(-1)
    out, off = {}, 0
    for n in SMALL:
        size = like[n].size
        out[n] = flat[off:off + size].reshape(like[n].shape)
        off += size
    return out


def _split_columns(full, n_local):
    r = full.shape[0]
    return full.reshape(r, N_DEV, n_local).transpose(1, 0, 2).reshape(N_DEV, r * n_local)


def _join_columns(blocks, r, n_local):
    return blocks.reshape(N_DEV, r, n_local).transpose(1, 0, 2).reshape(r, N_DEV * n_local)


def _lanes4(a):
    return jnp.pad(a.reshape(1, N_HEADS), ((0, 0), (0, LANES - N_HEADS)))


def kernel(x, attn_norm_g, w_in, dn_conv_w, dn_a_log, dn_dt_bias, dn_out_norm_g, sg_norm_g, sg_w, sg_b, w_out, ffn_norm_g, w_up, ffn_conv_w, ffn_conv_b, w_down, final_norm_g, loss_target, m_attn_norm_g, m_w_in, m_dn_conv_w, m_dn_a_log, m_dn_dt_bias, m_dn_out_norm_g, m_sg_norm_g, m_sg_w, m_sg_b, m_w_out, m_ffn_norm_g, m_w_up, m_ffn_conv_w, m_ffn_conv_b, m_w_down, m_final_norm_g, v_attn_norm_g, v_w_in, v_dn_conv_w, v_dn_a_log, v_dn_dt_bias, v_dn_out_norm_g, v_sg_norm_g, v_sg_w, v_sg_b, v_w_out, v_ffn_norm_g, v_w_up, v_ffn_conv_w, v_ffn_conv_b, v_w_down, v_final_norm_g):
    weights = dict(attn_norm_g=attn_norm_g, w_in=w_in, dn_conv_w=dn_conv_w, dn_a_log=dn_a_log, dn_dt_bias=dn_dt_bias,
                   dn_out_norm_g=dn_out_norm_g, sg_norm_g=sg_norm_g, sg_w=sg_w, sg_b=sg_b, w_out=w_out, ffn_norm_g=ffn_norm_g,
                   w_up=w_up, ffn_conv_w=ffn_conv_w, ffn_conv_b=ffn_conv_b, w_down=w_down, final_norm_g=final_norm_g)
    m_in = dict(attn_norm_g=m_attn_norm_g, w_in=m_w_in, dn_conv_w=m_dn_conv_w, dn_a_log=m_dn_a_log, dn_dt_bias=m_dn_dt_bias,
                dn_out_norm_g=m_dn_out_norm_g, sg_norm_g=m_sg_norm_g, sg_w=m_sg_w, sg_b=m_sg_b, w_out=m_w_out,
                ffn_norm_g=m_ffn_norm_g, w_up=m_w_up, ffn_conv_w=m_ffn_conv_w, ffn_conv_b=m_ffn_conv_b, w_down=m_w_down,
                final_norm_g=m_final_norm_g)
    v_in = dict(attn_norm_g=v_attn_norm_g, w_in=v_w_in, dn_conv_w=v_dn_conv_w, dn_a_log=v_dn_a_log, dn_dt_bias=v_dn_dt_bias,
                dn_out_norm_g=v_dn_out_norm_g, sg_norm_g=v_sg_norm_g, sg_w=v_sg_w, sg_b=v_sg_b, w_out=v_w_out,
                ffn_norm_g=v_ffn_norm_g, w_up=v_w_up, ffn_conv_w=v_ffn_conv_w, ffn_conv_b=v_ffn_conv_b, w_down=v_w_down,
                final_norm_g=v_final_norm_g)

    n_in, n_up = w_in.shape[2], w_up.shape[2]
    r_out, r_down = w_out.shape[1], w_down.shape[1]
    n_dnc, n_ffc = dn_conv_w.shape[2], ffn_conv_w.shape[2]
    wp_in, wp_up = _lane_padded(n_in), _lane_padded(n_up)
    taps = _pad_to(jnp.concatenate([dn_conv_w.reshape(-1), ffn_conv_w.reshape(-1)]), SUBLANES * LANES).reshape(-1, LANES)
    g_in, g_taps = _all_gather([jnp.pad(w_in[0].astype(BF16), ((0, 0), (0, wp_in - n_in))), taps])
    late_weights, token = _exchange_start(
        [w_out[0].astype(BF16), jnp.pad(w_up[0].astype(BF16), ((0, 0), (0, wp_up - n_up))), w_down[0].astype(BF16)],
        g_taps, True, "gather_late_start")
    w_in_p = _join_shards(g_in, n_in, PROJ_PAD, "join_w_in")
    taps_all = g_taps.reshape(N_DEV, -1)
    dn_conv_full = _join_columns(taps_all[:, :CONV_K * n_dnc], CONV_K, n_dnc)
    ffn_conv_full = _join_columns(taps_all[:, CONV_K * n_dnc:CONV_K * n_dnc + FFN_CONV * n_ffc], FFN_CONV, n_ffc)

    def late(after):
        g_out, g_up, g_down = _exchange_wait(late_weights, after, "gather_late_wait")
        return (_join_shards(g_up, n_up, N_DEV * n_up, "join_w_up"),
                g_out.reshape(N_DEV * r_out, D_MODEL), g_down.reshape(N_DEV * r_down, D_MODEL))

    def send_early(blocks, after, name):
        return _exchange_start(blocks, after, False, name)

    loss_lanes, grad_x, g, early = _local_step(
        x[0], loss_target[0], w_in_p, late, send_early, dn_conv_full, ffn_conv_full, attn_norm_g + token[0:1, 0:1],
        dn_a_log, dn_dt_bias, dn_out_norm_g, sg_norm_g, sg_w, sg_b, ffn_norm_g, ffn_conv_b, final_norm_g, (n_up, wp_up))

    small = jnp.concatenate([g[n].reshape(-1) for n in REPLICATED])
    small_send = jnp.concatenate([_split_columns(g["dn_conv_w"], n_dnc), _split_columns(g["ffn_conv_w"], n_ffc),
                                  jnp.broadcast_to(small[None, :], (N_DEV, small.shape[0]))], axis=1)
    small_send = _pad_to(small_send, SUBLANES * SLAB_COLS).reshape(N_DEV, -1, SLAB_COLS)
    r_in, r_small = _all_to_all([_split_shards(g["w_in"], n_in, wp_in, "split_dw_in"), small_send])
    r_dn, = _exchange_wait(early[0], r_small, "send_dw_down_wait")
    r_up, r_o = _exchange_wait(early[1], r_small, "send_dw_up_out_wait")

    upd = {
        "w_in": _sum_and_adamw(r_in, w_in[0], m_w_in[0], v_w_in[0], "adamw_w_in"),
        "w_up": _sum_and_adamw(r_up, w_up[0], m_w_up[0], v_w_up[0], "adamw_w_up"),
        "w_out": _sum_and_adamw(r_o, w_out[0], m_w_out[0], v_w_out[0], "adamw_w_out"),
        "w_down": _sum_and_adamw(r_dn, w_down[0], m_w_down[0], v_w_down[0], "adamw_w_down"),
    }
    small_upd = _sum_and_adamw(r_small, _pack_small(weights), _pack_small(m_in), _pack_small(v_in), "adamw_small")
    results = []
    for i in range(4):
        named = _unpack_small(small_upd[i], weights)
        named.update({n: upd[n][i][None] for n in upd})
        results.append(named)

    loss = lax.psum(loss_lanes[0, 0], MESH_AXES)
    return (loss, grad_x[None], *[r[n] for r in results for n in WEIGHT_ORDER])


def _local_step(x2d, tgt, w_in_p, late_weights, send_early, dn_conv_full, ffn_conv_full, attn_norm_g, dn_a_log,
                dn_dt_bias, dn_out_norm_g, sg_norm_g, sg_w, sg_b, ffn_norm_g, ffn_conv_b, final_norm_g, up_shard):
    t = x2d.shape[0]
    g1, g2, gf = attn_norm_g, ffn_norm_g, final_norm_g.reshape(1, D_MODEL)
    a_log4, dt_bias4 = _lanes4(dn_a_log), _lanes4(dn_dt_bias)
    sg_w3 = sg_w[0]
    sg_b_t = sg_b[0].T
    conv_b = ffn_conv_b

    h1, rstd1 = _rmsnorm_fwd(x2d, g1)
    p = _matmul(h1, w_in_p, "nn", "in_proj", (512, PROJ_PAD, D_MODEL))
    q, k, v, beta4, g4 = _dn_prep(p, dn_conv_full, a_log4, dt_bias4)
    mix_half, s_all = _dn_forward(q, k, v, beta4, g4, p, dn_out_norm_g)
    mix = _sg_forward(p, sg_norm_g, sg_w3, sg_b_t, mix_half)
    w_up_full, w_out_full, w_down_full = late_weights(mix)
    x2 = _matmul(mix, w_out_full, "nn", "out_proj", (1024, 1024, 1024), add=x2d)
    h2, rstd2 = _rmsnorm_fwd(x2, g2)
    up = _matmul(h2, w_up_full, "nn", "up_proj", (512, D_FF, D_MODEL))
    act = _ffn_act(up, ffn_conv_full, conv_b)
    x3 = _matmul(act, w_down_full, "nn", "down_proj", (512, 1024, D_FF), add=x2)
    loss_lanes, dx3, dx3b, d_gf = _final_loss(x3, tgt, gf)

    dact = _matmul(dx3b, w_down_full, "nt", "down_proj_dx", (512, D_FF, D_MODEL))
    d_w_down = _matmul(act, dx3b, "tn", "down_proj_dw", (256, 1024, t), out_dtype=BF16)
    sent_down, token = send_early([d_w_down.reshape(N_DEV, D_FF // N_DEV, D_MODEL)], d_w_down, "send_dw_down")
    dup, d_ffn_conv, d_ffn_conv_b = _ffn_bwd(up, ffn_conv_full, conv_b + token[0:1, 0:1], dact)
    dh2 = _matmul(dup, w_up_full, "nt", "up_proj_dx", (256, 1024, 2 * D_FF))
    d_w_up = _matmul(h2, dup, "tn", "up_proj_dw", (1024, 512, t))
    dx2, dx2b, d_g2 = _rmsnorm_bwd(x2, rstd2, g2, dh2, dx3)
    dmix = _matmul(dx2b, w_out_full, "nt", "out_proj_dx", (1024, 1024, 1024))
    d_w_out = _matmul(mix, dx2b, "tn", "out_proj_dw", (512, 1024, t), out_dtype=BF16)
    sent_up_out, token = send_early(
        [_split_shards(d_w_up, up_shard[0], up_shard[1], "split_dw_up"), d_w_out.reshape(N_DEV, D_MODEL // N_DEV, D_MODEL)],
        d_w_out, "send_dw_up_out")
    dp, d_sg_norm, d_sg_w, d_sg_b_t = _sg_backward(p, sg_norm_g + token[0:1, 0:1], sg_w3, sg_b_t, dmix)
    dq, dk, dv, dbeta4, dg4, dp, d_dn_norm = _dn_backward(q, k, v, beta4, g4, p, dn_out_norm_g, s_all, dmix, dp)
    dc_dn, d_dn_conv, dp, d_a_log4, d_dt_bias4 = _dn_prep_bwd(p, dn_conv_full, a_log4, dt_bias4, dq, dk, dv, dbeta4, dg4, dp)
    dp = _conv_bwd_input(dc_dn, dn_conv_full, "dn_conv_dx", out_cols=PROJ_PAD, into=dp)
    dh1 = _matmul(dp, w_in_p, "nt", "in_proj_dx", (512, 1024, PROJ_PAD))
    d_w_in_p = _matmul(h1, dp, "tn", "in_proj_dw", (1024, PROJ_PAD // 5, t))
    grad_x, _, d_g1 = _rmsnorm_bwd(x2d, rstd1, g1, dh1, dx2)

    grads = dict(
        attn_norm_g=d_g1, w_in=d_w_in_p, dn_conv_w=d_dn_conv, dn_a_log=d_a_log4[:, :N_HEADS], dn_dt_bias=d_dt_bias4[:, :N_HEADS],
        dn_out_norm_g=d_dn_norm, sg_norm_g=d_sg_norm, sg_w=d_sg_w, sg_b=d_sg_b_t[:, :SG_GROUPS].T,
        ffn_norm_g=d_g2, ffn_conv_w=d_ffn_conv, ffn_conv_b=d_ffn_conv_b, final_norm_g=d_gf)
    return loss_lanes, grad_x, grads, (sent_down, sent_up_out)
```

```python
import functools
import math

import jax
import jax.numpy as jnp
from jax import lax
from jax.experimental import pallas as pl
from jax.experimental.pallas import tpu as pltpu

F32 = jnp.float32
BF16 = jnp.bfloat16
HI = lax.Precision.HIGHEST

D_MODEL = 1024
DN_WIDTH = 512
HEAD_DIM = 128
N_HEADS = 4
SG_WIDTH = 512
SG_GROUPS = 4
SG_DIM = 128
SG_BLOCK = 128
D_FF = 2816
CHUNK = 64
CONV_K = 4
FFN_CONV = 3
EPS = 1e-6
PROJ_COLS = 3080
PROJ_MAIN = 3072
PROJ_PAD = 3200
GELU_C = math.sqrt(2.0 / math.pi)
N_DEV = 8
LANES = 128
SUBLANES = 8
HALO = SUBLANES
VMEM_LIMIT = 48 * 1024 * 1024

ADAM_LR = 0.001
ADAM_B1 = 0.9
ADAM_B2 = 0.999
ADAM_EPS = 1e-08
ADAM_WD = 0.01
ADAM_STEP = 10

MESH_AXES = ("x", "y", "c")
MESH_ID = pl.DeviceIdType.MESH


def _pcall(body, **kw):
    return pl.pallas_call(body, **kw)


def _params(*sem):
    return pltpu.CompilerParams(dimension_semantics=sem, vmem_limit_bytes=VMEM_LIMIT)


def _pick(n, cap):
    best = None
    for t in range(LANES, cap + 1, LANES):
        if n % t == 0:
            best = t
    return best if best else n


FAST, MID, EXACT = "bf16 operands, one pass", "three bf16 passes", "six bf16 passes"


def dot_f32(a, b, dims, tier):
    if tier == FAST:
        return lax.dot_general(a.astype(BF16), b.astype(BF16), dims, preferred_element_type=F32)
    prec = lax.Precision.HIGH if tier == MID else HI
    return lax.dot_general(a, b, dims, precision=prec, preferred_element_type=F32)


def dot_nn(a, b, tier=EXACT):
    return dot_f32(a, b, (((1,), (0,)), ((), ())), tier)


def dot_nt(a, b, tier=EXACT):
    return dot_f32(a, b, (((1,), (1,)), ((), ())), tier)


def dot_tn(a, b, tier=EXACT):
    return dot_f32(a, b, (((0,), (0,)), ((), ())), tier)


def sigmoid(x):
    return 1.0 / (1.0 + jnp.exp(-x))


def silu(x):
    return x * sigmoid(x)


def silu_grad(x):
    s = sigmoid(x)
    return s * (1.0 + x * (1.0 - s))


def gelu(x):
    return 0.5 * x * (1.0 + jnp.tanh(GELU_C * (x + 0.044715 * x * x * x)))


def gelu_grad(x):
    t = jnp.tanh(GELU_C * (x + 0.044715 * x * x * x))
    return 0.5 * (1.0 + t) + 0.5 * x * (1.0 - t * t) * GELU_C * (1.0 + 3.0 * 0.044715 * x * x)


def softplus(z):
    return jnp.maximum(z, 0.0) + jnp.log(1.0 + jnp.exp(-jnp.abs(z)))


def rms_fwd(x, g):
    r = lax.rsqrt(jnp.mean(x * x, axis=-1, keepdims=True) + EPS)
    return x * r * g, r


def rms_bwd(x, r, g, dy):
    dyg = dy * g
    xr = x * r
    dx = r * (dyg - xr * jnp.mean(dyg * xr, axis=-1, keepdims=True))
    return dx, dy * xr


def l2_fwd(x):
    r = lax.rsqrt(jnp.sum(x * x, axis=-1, keepdims=True) + EPS)
    return x * r, r


def l2_bwd(x, r, dy):
    xr = x * r
    return r * (dy - xr * jnp.sum(dy * xr, axis=-1, keepdims=True))


def _tri_masks(n):
    row = lax.broadcasted_iota(jnp.int32, (n, n), 0)
    col = lax.broadcasted_iota(jnp.int32, (n, n), 1)
    return row >= col, row > col


def chunk_cumsum(g4):
    incl, _ = _tri_masks(g4.shape[0])
    return dot_nn(incl.astype(F32), g4)


STACK = N_HEADS * CHUNK
DN_CHUNKS = 2


def _head_rows(h):
    return slice(h * CHUNK, (h + 1) * CHUNK)


def _stack_heads(x):
    return jnp.concatenate([x[:, h * HEAD_DIM:(h + 1) * HEAD_DIM] for h in range(N_HEADS)], axis=0)


def _stack_lanes(x4):
    return jnp.concatenate([x4[:, h:h + 1] for h in range(N_HEADS)], axis=0)


def _per_head(fn):
    return jnp.concatenate([fn(h) for h in range(N_HEADS)], axis=0)


def _unit_lower_inverse(l_strict, order):
    c = l_strict.shape[0]
    row = lax.broadcasted_iota(jnp.int32, (c, c), 0)
    col = lax.broadcasted_iota(jnp.int32, (c, c), 1)
    n = -l_strict
    a = (row == col).astype(F32) + n
    p = n
    for _ in range(int(math.log2(order)) - 1):
        p = dot_nn(p, p, FAST)
        a = a + dot_nn(a, p, FAST)
    return a


def dn_chunk_local(q, k, v, beta, gc4):
    row = lax.broadcasted_iota(jnp.int32, (STACK, STACK), 0)
    col = lax.broadcasted_iota(jnp.int32, (STACK, STACK), 1)
    same = (row // CHUNK) == (col // CHUNK)
    incl = jnp.logical_and(same, row >= col)
    strict = jnp.logical_and(same, row > col)
    gc_col = _stack_lanes(gc4)
    gc_row = jnp.sum(jnp.where(row == col, gc_col, 0.0), axis=0, keepdims=True)
    decay = jnp.where(incl, jnp.exp(jnp.minimum(gc_col - gc_row, 0.0)), 0.0)
    gamma = jnp.exp(gc_col)
    gc_last = jnp.concatenate([jnp.broadcast_to(gc4[CHUNK - 1:CHUNK, h:h + 1], (CHUNK, 1)) for h in range(N_HEADS)], axis=0)
    tau = jnp.exp(gc_last - gc_col)
    cd = jnp.exp(gc_last)
    kb = k * beta
    l_mat = jnp.where(strict, dot_nt(kb, k, FAST) * decay, 0.0)
    a_inv = _unit_lower_inverse(l_mat, CHUNK)
    sol = dot_nn(a_inv, jnp.concatenate([v * beta, kb * gamma], axis=1), FAST)
    value, kcd = sol[:, :HEAD_DIM], sol[:, HEAD_DIM:]
    attn = jnp.where(incl, dot_nt(q, k, FAST) * decay, 0.0)
    return dict(decay=decay, gamma=gamma, tau=tau, cd=cd, kb=kb, l_mat=l_mat, a_inv=a_inv, sol=sol, value=value, kcd=kcd,
                attn=attn, qd=q * gamma, kt=k * tau, incl=incl, strict=strict)


def dn_chunk_state(loc, s):
    kcd, qd, kt, cd = loc["kcd"], loc["qd"], loc["kt"], loc["cd"]
    v_new = loc["value"] - _per_head(lambda h: dot_nn(kcd[_head_rows(h)], s[h], FAST))
    o = _per_head(lambda h: dot_nn(qd[_head_rows(h)], s[h], FAST)) + dot_nn(loc["attn"], v_new, FAST)
    s_new = [s[h] * cd[h * CHUNK:h * CHUNK + 1, :] + dot_tn(kt[_head_rows(h)], v_new[_head_rows(h)], FAST)
             for h in range(N_HEADS)]
    loc["v_new"] = v_new
    return o, s_new


def dn_chunk_bwd(loc, q, k, v, beta, s, do, ds_new):
    s_of, ds_new_of = (lambda h: s[h]), (lambda h: ds_new[h])
    decay, gamma, tau, cd = loc["decay"], loc["gamma"], loc["tau"], loc["cd"]
    a_inv, attn, l_mat, sol = loc["a_inv"], loc["attn"], loc["l_mat"], loc["sol"]
    v_new, qd, kt, kcd, kb = loc["v_new"], loc["qd"], loc["kt"], loc["kcd"], loc["kb"]
    hr = _head_rows

    dv_new = dot_tn(attn, do, FAST) + _per_head(lambda h: dot_nn(kt[hr(h)], ds_new_of(h), FAST))
    dattn = jnp.where(loc["incl"], dot_nt(do, v_new, FAST), 0.0)
    dqd = _per_head(lambda h: dot_nt(do[hr(h)], s_of(h), FAST))
    ds = [dot_tn(qd[hr(h)], do[hr(h)], FAST) + ds_new_of(h) * cd[h * CHUNK:h * CHUNK + 1, :]
          - dot_tn(kcd[hr(h)], dv_new[hr(h)], FAST) for h in range(N_HEADS)]
    dkt = _per_head(lambda h: dot_nt(v_new[hr(h)], ds_new_of(h), FAST))
    dkcd = -_per_head(lambda h: dot_nt(dv_new[hr(h)], s_of(h), FAST))
    drhs = dot_tn(a_inv, jnp.concatenate([dv_new, dkcd], axis=1), FAST)
    dvb, dkbg = drhs[:, :HEAD_DIM], drhs[:, HEAD_DIM:]
    dl = jnp.where(loc["strict"], -dot_nt(drhs, sol, FAST), 0.0)
    dkk = dl * decay
    dqk = dattn * decay
    e = dl * l_mat + dattn * attn
    dgc = jnp.sum(e, axis=1, keepdims=True) - jnp.sum(e, axis=0, keepdims=True).T
    dkb = dot_nn(dkk, k, FAST) + dkbg * gamma
    dk = dot_tn(dkk, kb, FAST) + dot_tn(dqk, q, FAST) + dkt * tau
    dq = dot_nn(dqk, k, FAST) + dqd * gamma
    dgamma = jnp.sum(dkbg * kb, axis=1, keepdims=True) + jnp.sum(dqd * q, axis=1, keepdims=True)
    dtau_tau = jnp.sum(dkt * k, axis=1, keepdims=True) * tau
    dgc = dgc + dgamma * gamma - dtau_tau
    is_last = (lax.broadcasted_iota(jnp.int32, (STACK, 1), 0) % CHUNK) == CHUNK - 1

    def last_term(h):
        s, ds_new = s_of(h), ds_new_of(h)
        dcd = jnp.sum(jnp.sum(ds_new * s, axis=1, keepdims=True), axis=0, keepdims=True)
        total = jnp.sum(dtau_tau[hr(h)], axis=0, keepdims=True) + dcd * cd[h * CHUNK:h * CHUNK + 1, :]
        return jnp.broadcast_to(total, (CHUNK, 1))

    dgc = dgc + jnp.where(is_last, _per_head(last_term), 0.0)
    dk = dk + dkb * beta
    dbeta = jnp.sum(dkb * k, axis=1, keepdims=True) + jnp.sum(dvb * v, axis=1, keepdims=True)
    dv = dvb * beta
    return dq, dk, dv, dbeta, dgc, ds


def _token_tile(t):
    return _pick(t, 256)


STRIP = 32
NORM_STRIP = 16


def _for_strips(n_rows, rows, fn):
    def step(r, carry):
        fn(pl.multiple_of(r * rows, rows))
        return carry

    lax.fori_loop(0, n_rows // rows, step, 0)


def _fold_rows(x):
    out = x[0:SUBLANES, :]
    for i in range(1, x.shape[0] // SUBLANES):
        out = out + x[i * SUBLANES:(i + 1) * SUBLANES, :]
    return out


def _rmsnorm_fwd(x, g):
    t, d = x.shape
    tm = _token_tile(t)

    def body(x_ref, g_ref, h_ref, r_ref):
        y, r = rms_fwd(x_ref[...], g_ref[...])
        h_ref[...] = y.astype(BF16)
        r_ref[...] = r

    return _pcall(
        body, grid=(t // tm,),
        in_specs=[pl.BlockSpec((tm, d), lambda i: (i, 0)), pl.BlockSpec((1, d), lambda i: (0, 0))],
        out_specs=[pl.BlockSpec((tm, d), lambda i: (i, 0)), pl.BlockSpec((tm, 1), lambda i: (i, 0))],
        out_shape=[jax.ShapeDtypeStruct((t, d), BF16), jax.ShapeDtypeStruct((t, 1), F32)],
        compiler_params=_params("parallel"), name="rmsnorm_fwd")(x, g)


def _rmsnorm_bwd(x, r, g, dh, dres):
    t, d = x.shape
    tm = _token_tile(t)

    def body(x_ref, r_ref, g_ref, dh_ref, dres_ref, dx_ref, dxb_ref, dg_ref):
        dx, dg_rows = rms_bwd(x_ref[...], r_ref[...], g_ref[...], dh_ref[...])
        dx = dx + dres_ref[...]
        dx_ref[...] = dx
        dxb_ref[...] = dx.astype(BF16)

        @pl.when(pl.program_id(0) == 0)
        def _():
            dg_ref[...] = jnp.zeros_like(dg_ref)

        dg_ref[...] += jnp.sum(dg_rows, axis=0, keepdims=True)

    tile = pl.BlockSpec((tm, d), lambda i: (i, 0))
    row = pl.BlockSpec((1, d), lambda i: (0, 0))
    return _pcall(
        body, grid=(t // tm,),
        in_specs=[tile, pl.BlockSpec((tm, 1), lambda i: (i, 0)), row, tile, tile],
        out_specs=[tile, tile, row],
        out_shape=[jax.ShapeDtypeStruct((t, d), F32), jax.ShapeDtypeStruct((t, d), BF16), jax.ShapeDtypeStruct((1, d), F32)],
        compiler_params=_params("arbitrary"), name="rmsnorm_bwd")(x, r, g, dh, dres)


def _matmul(a, b, mode, name, tiles, add=None, out_dtype=F32):
    if mode == "nn":
        (m, k), n = a.shape, b.shape[1]
    elif mode == "nt":
        (m, k), n = a.shape, b.shape[0]
    else:
        (k, m), n = a.shape, b.shape[1]
    tm, tn, tk = min(tiles[0], m), min(tiles[1], n), min(tiles[2], k)
    assert m % tm == 0 and n % tn == 0 and k % tk == 0, (name, m, n, k, tiles)
    nk = k // tk
    dims = {"nn": (((1,), (0,)), ((), ())), "nt": (((1,), (1,)), ((), ())), "tn": (((0,), (0,)), ((), ()))}[mode]

    def finish(res, add_ref, o_ref):
        if add_ref is not None:
            res = res + add_ref[...]
        o_ref[...] = res.astype(o_ref.dtype)

    def body(*refs):
        a_ref, b_ref = refs[0], refs[1]
        add_ref = refs[2] if add is not None else None
        o_ref = refs[3] if add is not None else refs[2]
        part = lax.dot_general(a_ref[...], b_ref[...], dims, preferred_element_type=F32)
        if nk == 1:
            finish(part, add_ref, o_ref)
            return
        acc_ref = refs[-1]
        kk = pl.program_id(2)

        @pl.when(kk == 0)
        def _():
            acc_ref[...] = part

        @pl.when(kk > 0)
        def _():
            acc_ref[...] += part

        @pl.when(kk == nk - 1)
        def _():
            finish(acc_ref[...], add_ref, o_ref)

    a_spec = pl.BlockSpec((tk, tm), lambda j, i, kk: (kk, i)) if mode == "tn" else pl.BlockSpec((tm, tk), lambda j, i, kk: (i, kk))
    b_spec = pl.BlockSpec((tn, tk), lambda j, i, kk: (j, kk)) if mode == "nt" else pl.BlockSpec((tk, tn), lambda j, i, kk: (kk, j))
    o_spec = pl.BlockSpec((tm, tn), lambda j, i, kk: (i, j))
    in_specs = [a_spec, b_spec] + ([o_spec] if add is not None else [])
    args = (a, b) + ((add,) if add is not None else ())
    return _pcall(
        body, grid=(n // tn, m // tm, nk), in_specs=in_specs, out_specs=o_spec,
        out_shape=jax.ShapeDtypeStruct((m, n), out_dtype),
        scratch_shapes=[pltpu.VMEM((tm, tn), F32)] if nk > 1 else [],
        compiler_params=_params("parallel", "parallel", "arbitrary"), name=name)(*args)


def _prev_halo_spec(tm, width, col_block):
    return pl.BlockSpec((HALO, width), lambda i: (jnp.maximum(i * (tm // HALO) - 1, 0), col_block))


def _fill_with_prev(xp_ref, tile, halo, first):
    xp_ref[0:HALO, :] = jnp.where(first, 0.0, halo)
    xp_ref[HALO:, :] = tile


def _delayed(xp_ref, row0, cols, taps):
    ext = xp_ref[pl.ds(row0, STRIP + HALO), cols]
    return [ext[HALO:, :]] + [pltpu.roll(ext, j, 0)[HALO:, :] for j in range(1, taps)]


def _causal_conv(delayed, w):
    taps = len(delayed)
    out = delayed[0] * w[taps - 1:taps, :]
    for j in range(1, taps):
        out = out + delayed[j] * w[taps - 1 - j:taps - j, :]
    return out


def _advanced_conv(buf_ref, row0, cols, w):
    taps = w.shape[0]
    ext = buf_ref[pl.ds(row0, STRIP + HALO), cols]
    out = ext[:STRIP, :] * w[taps - 1:taps, :]
    for j in range(1, taps):
        out = out + pltpu.roll(ext, STRIP + HALO - j, 0)[:STRIP, :] * w[taps - 1 - j:taps - j, :]
    return out


def _dn_prep(p, conv_w, a_log4, dt_bias4):
    t = p.shape[0]
    tm = _token_tile(t)
    w3 = 3 * DN_WIDTH

    def body(x_ref, halo_ref, pbd_ref, w_ref, alog_ref, dtb_ref, q_ref, k_ref, v_ref, beta_ref, g_ref, xp_ref):
        _fill_with_prev(xp_ref, x_ref[...], halo_ref[...], pl.program_id(0) == 0)

        def strip(row0):
            rows = pl.ds(row0, STRIP)
            for h in range(N_HEADS):
                sl = slice(h * HEAD_DIM, (h + 1) * HEAD_DIM)
                for part, out_ref in ((0, q_ref), (1, k_ref), (2, v_ref)):
                    cols = slice(part * DN_WIDTH + h * HEAD_DIM, part * DN_WIDTH + (h + 1) * HEAD_DIM)
                    y = silu(_causal_conv(_delayed(xp_ref, row0, cols, CONV_K), w_ref[:, cols]))
                    if part == 0:
                        y = l2_fwd(y)[0] * (HEAD_DIM ** -0.5)
                    elif part == 1:
                        y = l2_fwd(y)[0]
                    out_ref[rows, sl] = y
            head = lax.broadcasted_iota(jnp.int32, (STRIP, LANES), 1) < N_HEADS
            pbd = pbd_ref[rows, :]
            beta_ref[rows, :] = jnp.where(head, sigmoid(pbd), 0.0)
            a_raw = pltpu.roll(pbd, LANES - N_HEADS, 1)
            g_ref[rows, :] = jnp.where(head, -jnp.exp(alog_ref[...]) * softplus(a_raw + dtb_ref[...]), 0.0)

        _for_strips(tm, STRIP, strip)

    tok = lambda w, cb: pl.BlockSpec((tm, w), lambda i: (i, cb))
    full = lambda a: pl.BlockSpec(a.shape, lambda i: (0, 0))
    return _pcall(
        body, grid=(t // tm,),
        in_specs=[tok(w3, 0), _prev_halo_spec(tm, w3, 0), tok(LANES, PROJ_MAIN // LANES),
                  full(conv_w), full(a_log4), full(dt_bias4)],
        out_specs=[tok(DN_WIDTH, 0)] * 3 + [tok(LANES, 0)] * 2,
        out_shape=[jax.ShapeDtypeStruct((t, DN_WIDTH), F32)] * 3 + [jax.ShapeDtypeStruct((t, LANES), F32)] * 2,
        scratch_shapes=[pltpu.VMEM((HALO + tm, w3), F32)],
        compiler_params=_params("parallel"), name="dn_prep")(p, p, p, conv_w, a_log4, dt_bias4)


def _dn_prep_bwd(p, conv_w, a_log4, dt_bias4, dq, dk, dv, dbeta4, dg4, dp_buf):
    t = p.shape[0]
    tm = _token_tile(t)
    w3 = 3 * DN_WIDTH

    def body(x_ref, halo_ref, pbd_ref, w_ref, alog_ref, dtb_ref, dq_ref, dk_ref, dv_ref, dbeta_ref, dg_ref, _,
             dc_ref, dw_ref, dpbd_ref, dalog_ref, ddtb_ref, xp_ref, dw_acc, lane_acc):
        first = pl.program_id(0) == 0
        _fill_with_prev(xp_ref, x_ref[...], halo_ref[...], first)
        dw_acc[...] = jnp.zeros_like(dw_acc)
        lane_acc[...] = jnp.zeros_like(lane_acc)

        def strip(row0):
            rows = pl.ds(row0, STRIP)
            for h in range(N_HEADS):
                sl = slice(h * HEAD_DIM, (h + 1) * HEAD_DIM)
                for part, dy_ref in ((0, dq_ref), (1, dk_ref), (2, dv_ref)):
                    cols = slice(part * DN_WIDTH + h * HEAD_DIM, part * DN_WIDTH + (h + 1) * HEAD_DIM)
                    delayed = _delayed(xp_ref, row0, cols, CONV_K)
                    c = _causal_conv(delayed, w_ref[:, cols])
                    dy = dy_ref[rows, sl]
                    if part < 2:
                        y = silu(c)
                        _, r = l2_fwd(y)
                        dy = l2_bwd(y, r, dy * (HEAD_DIM ** -0.5) if part == 0 else dy)
                    dc = dy * silu_grad(c)
                    dc_ref[rows, cols] = dc
                    for j in range(CONV_K):
                        k = CONV_K - 1 - j
                        dw_acc[k * SUBLANES:(k + 1) * SUBLANES, cols] += _fold_rows(dc * delayed[j])
            head = lax.broadcasted_iota(jnp.int32, (STRIP, LANES), 1) < N_HEADS
            pbd = pbd_ref[rows, :]
            beta = sigmoid(pbd)
            dpb = jnp.where(head, dbeta_ref[rows, :] * beta * (1.0 - beta), 0.0)
            z = pltpu.roll(pbd, LANES - N_HEADS, 1) + dtb_ref[...]
            neg_rate = -jnp.exp(alog_ref[...])
            dg = dg_ref[rows, :]
            dpa = jnp.where(head, dg * neg_rate * sigmoid(z), 0.0)
            dpbd_ref[rows, :] = (dpb + pltpu.roll(dpa, N_HEADS, 1)).astype(BF16)
            g = jnp.where(head, neg_rate * softplus(z), 0.0)
            lane_acc[0:SUBLANES, :] += _fold_rows(dg * g)
            lane_acc[SUBLANES:, :] += _fold_rows(dpa)

        _for_strips(tm, STRIP, strip)

        @pl.when(first)
        def _():
            dw_ref[...] = jnp.zeros_like(dw_ref)
            dalog_ref[...] = jnp.zeros_like(dalog_ref)
            ddtb_ref[...] = jnp.zeros_like(ddtb_ref)

        for k in range(CONV_K):
            dw_ref[k:k + 1, :] += jnp.sum(dw_acc[k * SUBLANES:(k + 1) * SUBLANES, :], axis=0, keepdims=True)
        dalog_ref[...] += jnp.sum(lane_acc[0:SUBLANES, :], axis=0, keepdims=True)
        ddtb_ref[...] += jnp.sum(lane_acc[SUBLANES:, :], axis=0, keepdims=True)

    tok = lambda w, cb: pl.BlockSpec((tm, w), lambda i: (i, cb))
    full = lambda shape: pl.BlockSpec(shape, lambda i: (0, 0))
    return _pcall(
        body, grid=(t // tm,),
        in_specs=[tok(w3, 0), _prev_halo_spec(tm, w3, 0), tok(LANES, PROJ_MAIN // LANES),
                  full(conv_w.shape), full(a_log4.shape), full(dt_bias4.shape)] + [tok(DN_WIDTH, 0)] * 3 + [tok(LANES, 0)] * 2
        + [pl.BlockSpec(memory_space=pl.ANY)],
        out_specs=[tok(w3, 0), full((CONV_K, w3)), tok(LANES, PROJ_MAIN // LANES), full((1, LANES)), full((1, LANES))],
        out_shape=[jax.ShapeDtypeStruct((t, w3), F32), jax.ShapeDtypeStruct((CONV_K, w3), F32),
                   jax.ShapeDtypeStruct(dp_buf.shape, dp_buf.dtype),
                   jax.ShapeDtypeStruct((1, LANES), F32), jax.ShapeDtypeStruct((1, LANES), F32)],
        input_output_aliases={11: 2},
        scratch_shapes=[pltpu.VMEM((HALO + tm, w3), F32), pltpu.VMEM((CONV_K * SUBLANES, w3), F32),
                        pltpu.VMEM((2 * SUBLANES, LANES), F32)],
        compiler_params=_params("arbitrary"), name="dn_prep_bwd")(p, p, p, conv_w, a_log4, dt_bias4, dq, dk, dv, dbeta4, dg4, dp_buf)


def _conv_bwd_input(dc, w, name, out_cols=None, col_block=0, into=None):
    t, c = dc.shape
    taps = w.shape[0]
    tm = _token_tile(t)
    ct = _pick(c, 1536)
    n_tok = t // tm
    out_cols = c if out_cols is None else out_cols

    def body(dc_ref, next_ref, w_ref, *rest):
        dx_ref, buf_ref = rest[-2], rest[-1]
        buf_ref[0:tm, :] = dc_ref[...]
        buf_ref[tm:, :] = jnp.where(pl.program_id(0) == n_tok - 1, 0.0, next_ref[...])

        def strip(row0):
            for c0 in range(0, ct, LANES):
                cols = slice(c0, c0 + LANES)
                dx_ref[pl.ds(row0, STRIP), cols] = _advanced_conv(buf_ref, row0, cols, w_ref[:, cols]).astype(BF16)

        _for_strips(tm, STRIP, strip)

    in_specs = [pl.BlockSpec((tm, ct), lambda i, j: (i, j)),
                pl.BlockSpec((HALO, ct), lambda i, j: (jnp.minimum((i + 1) * (tm // HALO), t // HALO - 1), j)),
                pl.BlockSpec((taps, ct), lambda i, j: (0, j))]
    args = (dc, dc, w)
    aliases = {}
    if into is not None:
        in_specs.append(pl.BlockSpec(memory_space=pl.ANY))
        args += (into,)
        aliases = {3: 0}
    return _pcall(
        body, grid=(n_tok, c // ct), in_specs=in_specs,
        out_specs=pl.BlockSpec((tm, ct), lambda i, j: (i, j + col_block)),
        out_shape=jax.ShapeDtypeStruct((t, out_cols), BF16), input_output_aliases=aliases,
        scratch_shapes=[pltpu.VMEM((tm + HALO, ct), F32)],
        compiler_params=_params("parallel", "parallel"), name=name)(*args)


def _dn_forward(q, k, v, beta4, g4, p, norm_g):
    t = q.shape[0]
    n = t // CHUNK
    rows_per_step = DN_CHUNKS * CHUNK

    def body(q_ref, k_ref, v_ref, b_ref, g_ref, gate_ref, ng_ref, mix_ref, s_all_ref, s_ref):
        @pl.when(pl.program_id(0) == 0)
        def _():
            s_ref[...] = jnp.zeros_like(s_ref)

        locs = []
        for c in range(DN_CHUNKS):
            rows = slice(c * CHUNK, (c + 1) * CHUNK)
            locs.append(dn_chunk_local(_stack_heads(q_ref[rows, :]), _stack_heads(k_ref[rows, :]), _stack_heads(v_ref[rows, :]),
                                       _stack_lanes(b_ref[rows, :]), chunk_cumsum(g_ref[rows, :])))
        s = [s_ref[h] for h in range(N_HEADS)]
        for c in range(DN_CHUNKS):
            rows = slice(c * CHUNK, (c + 1) * CHUNK)
            for h in range(N_HEADS):
                s_all_ref[c, h] = s[h]
            o, s = dn_chunk_state(locs[c], s)
            o_n, _ = rms_fwd(o, ng_ref[...])
            for h in range(N_HEADS):
                sl = slice(h * HEAD_DIM, (h + 1) * HEAD_DIM)
                mix_ref[rows, sl] = (o_n[_head_rows(h)] * silu(gate_ref[rows, sl])).astype(BF16)
        for h in range(N_HEADS):
            s_ref[h] = s[h]

    ch = lambda w, cb: pl.BlockSpec((rows_per_step, w), lambda i: (i, cb))
    return _pcall(
        body, grid=(n // DN_CHUNKS,),
        in_specs=[ch(DN_WIDTH, 0)] * 3 + [ch(LANES, 0)] * 2 + [ch(DN_WIDTH, 3), pl.BlockSpec((1, HEAD_DIM), lambda i: (0, 0))],
        out_specs=[ch(DN_WIDTH, 0), pl.BlockSpec((DN_CHUNKS, N_HEADS, HEAD_DIM, HEAD_DIM), lambda i: (i, 0, 0, 0))],
        out_shape=[jax.ShapeDtypeStruct((t, DN_WIDTH + SG_WIDTH), BF16), jax.ShapeDtypeStruct((n, N_HEADS, HEAD_DIM, HEAD_DIM), F32)],
        scratch_shapes=[pltpu.VMEM((N_HEADS, HEAD_DIM, HEAD_DIM), F32)],
        compiler_params=_params("arbitrary"), name="dn_forward")(q, k, v, beta4, g4, p, norm_g)


def _dn_backward(q, k, v, beta4, g4, p, norm_g, s_all, dmix, dp_buf):
    t = q.shape[0]
    n = t // CHUNK
    steps = n // DN_CHUNKS
    rows_per_step = DN_CHUNKS * CHUNK

    def body(q_ref, k_ref, v_ref, b_ref, g_ref, gate_ref, ng_ref, s_in_ref, dmix_ref, _,
             dq_ref, dk_ref, dv_ref, db_ref, dg_ref, dgate_ref, dng_ref, ds_ref):
        @pl.when(pl.program_id(0) == 0)
        def _():
            ds_ref[...] = jnp.zeros_like(ds_ref)
            dng_ref[...] = jnp.zeros_like(dng_ref)

        fwd = []
        for c in range(DN_CHUNKS):
            rows = slice(c * CHUNK, (c + 1) * CHUNK)
            q, k, v = _stack_heads(q_ref[rows, :]), _stack_heads(k_ref[rows, :]), _stack_heads(v_ref[rows, :])
            beta = _stack_lanes(b_ref[rows, :])
            loc = dn_chunk_local(q, k, v, beta, chunk_cumsum(g_ref[rows, :]))
            s = [s_in_ref[c, h] for h in range(N_HEADS)]
            o, _ = dn_chunk_state(loc, s)
            fwd.append((q, k, v, beta, loc, s, o))
        ds = [ds_ref[h] for h in range(N_HEADS)]
        lane = lax.broadcasted_iota(jnp.int32, (CHUNK, LANES), 1)
        _, strict = _tri_masks(CHUNK)
        for c in reversed(range(DN_CHUNKS)):
            rows = slice(c * CHUNK, (c + 1) * CHUNK)
            q, k, v, beta, loc, s, o = fwd[c]
            o_n, r = rms_fwd(o, ng_ref[...])
            gate = _stack_heads(gate_ref[rows, :])
            dmx = _stack_heads(dmix_ref[rows, :])
            dgate = dmx * o_n * silu_grad(gate)
            do, dng_rows = rms_bwd(o, r, ng_ref[...], dmx * silu(gate))
            dng_ref[...] += jnp.sum(dng_rows, axis=0, keepdims=True)
            dq, dk, dv, dbeta, dgc, ds = dn_chunk_bwd(loc, q, k, v, beta, s, do, ds)
            db4 = jnp.zeros((CHUNK, LANES), F32)
            dgc4 = jnp.zeros((CHUNK, LANES), F32)
            for h in range(N_HEADS):
                sl = slice(h * HEAD_DIM, (h + 1) * HEAD_DIM)
                head_rows = _head_rows(h)
                dgate_ref[rows, sl] = dgate[head_rows].astype(BF16)
                dq_ref[rows, sl] = dq[head_rows]
                dk_ref[rows, sl] = dk[head_rows]
                dv_ref[rows, sl] = dv[head_rows]
                db4 = jnp.where(lane == h, dbeta[head_rows], db4)
                dgc4 = jnp.where(lane == h, dgc[head_rows], dgc4)
            db_ref[rows, :] = db4
            dg_ref[rows, :] = dot_nn(jnp.logical_not(strict).astype(F32), dgc4)
        for h in range(N_HEADS):
            ds_ref[h] = ds[h]

    rev = lambda w, cb: pl.BlockSpec((rows_per_step, w), lambda i: (steps - 1 - i, cb))
    return _pcall(
        body, grid=(steps,),
        in_specs=[rev(DN_WIDTH, 0)] * 3 + [rev(LANES, 0)] * 2 + [rev(DN_WIDTH, 3), pl.BlockSpec((1, HEAD_DIM), lambda i: (0, 0)),
                  pl.BlockSpec((DN_CHUNKS, N_HEADS, HEAD_DIM, HEAD_DIM), lambda i: (steps - 1 - i, 0, 0, 0)), rev(DN_WIDTH, 0),
                  pl.BlockSpec(memory_space=pl.ANY)],
        out_specs=[rev(DN_WIDTH, 0)] * 3 + [rev(LANES, 0)] * 2 + [rev(DN_WIDTH, 3), pl.BlockSpec((1, HEAD_DIM), lambda i: (0, 0))],
        out_shape=[jax.ShapeDtypeStruct((t, DN_WIDTH), F32)] * 3 + [jax.ShapeDtypeStruct((t, LANES), F32)] * 2
        + [jax.ShapeDtypeStruct(dp_buf.shape, dp_buf.dtype), jax.ShapeDtypeStruct((1, HEAD_DIM), F32)],
        input_output_aliases={9: 5},
        scratch_shapes=[pltpu.VMEM((N_HEADS, HEAD_DIM, HEAD_DIM), F32)],
        compiler_params=_params("arbitrary"), name="dn_backward")(q, k, v, beta4, g4, p, norm_g, s_all, dmix, dp_buf)


def _sg_mask():
    row = lax.broadcasted_iota(jnp.int32, (SG_BLOCK, SG_BLOCK), 0)
    col = lax.broadcasted_iota(jnp.int32, (SG_BLOCK, SG_BLOCK), 1)
    return (col // CHUNK) <= (row // CHUNK)


def _sg_forward(p, norm_g, w_s, b_t, mix_buf):
    t = p.shape[0]

    def body(u_ref, v_ref, ng_ref, w_ref, b_ref, _, o_ref):
        mask = _sg_mask()
        for g in range(SG_GROUPS):
            sl = slice(g * SG_DIM, (g + 1) * SG_DIM)
            vn, _ = rms_fwd(gelu(v_ref[:, sl]), ng_ref[:, sl])
            s = dot_nn(jnp.where(mask, w_ref[g], 0.0), vn, FAST) + b_ref[:, g:g + 1]
            o_ref[:, sl] = (gelu(u_ref[:, sl]) * s).astype(BF16)

    blk = lambda cb: pl.BlockSpec((SG_BLOCK, SG_WIDTH), lambda i: (i, cb))
    return _pcall(
        body, grid=(t // SG_BLOCK,),
        in_specs=[blk(4), blk(5), pl.BlockSpec((1, SG_WIDTH), lambda i: (0, 0)),
                  pl.BlockSpec((SG_GROUPS, SG_BLOCK, SG_BLOCK), lambda i: (0, 0, 0)), pl.BlockSpec((SG_BLOCK, SG_GROUPS), lambda i: (0, 0)),
                  pl.BlockSpec(memory_space=pl.ANY)],
        out_specs=blk(1), out_shape=jax.ShapeDtypeStruct(mix_buf.shape, mix_buf.dtype), input_output_aliases={5: 0},
        compiler_params=_params("parallel"), name="sg_forward")(p, p, norm_g, w_s, b_t, mix_buf)


def _sg_backward(p, norm_g, w_s, b_t, dmix):
    t = p.shape[0]

    def body(u_ref, v_ref, ng_ref, w_ref, b_ref, do_ref, duv_ref, dng_ref, dw_ref, db_ref):
        @pl.when(pl.program_id(0) == 0)
        def _():
            dng_ref[...] = jnp.zeros_like(dng_ref)
            dw_ref[...] = jnp.zeros_like(dw_ref)
            db_ref[...] = jnp.zeros_like(db_ref)

        mask = _sg_mask()
        lane = lax.broadcasted_iota(jnp.int32, (SG_BLOCK, LANES), 1)
        db = jnp.zeros((SG_BLOCK, LANES), F32)
        for g in range(SG_GROUPS):
            sl = slice(g * SG_DIM, (g + 1) * SG_DIM)
            u_raw, v_raw, do = u_ref[:, sl], v_ref[:, sl], do_ref[:, sl]
            vg = gelu(v_raw)
            vn, r = rms_fwd(vg, ng_ref[:, sl])
            w_m = jnp.where(mask, w_ref[g], 0.0)
            s = dot_nn(w_m, vn, FAST) + b_ref[:, g:g + 1]
            duv_ref[:, sl] = (do * s * gelu_grad(u_raw)).astype(BF16)
            ds = do * gelu(u_raw)
            db = jnp.where(lane == g, jnp.sum(ds, axis=1, keepdims=True), db)
            dw_ref[g] += jnp.where(mask, dot_nt(ds, vn, FAST), 0.0)
            dvg, dng_rows = rms_bwd(vg, r, ng_ref[:, sl], dot_tn(w_m, ds, FAST))
            dng_ref[:, sl] += jnp.sum(dng_rows, axis=0, keepdims=True)
            duv_ref[:, SG_WIDTH + g * SG_DIM:SG_WIDTH + (g + 1) * SG_DIM] = (dvg * gelu_grad(v_raw)).astype(BF16)
        db_ref[...] += db

    blk = lambda cb: pl.BlockSpec((SG_BLOCK, SG_WIDTH), lambda i: (i, cb))
    const2 = lambda shape: pl.BlockSpec(shape, lambda i: (0, 0))
    w_spec = pl.BlockSpec((SG_GROUPS, SG_BLOCK, SG_BLOCK), lambda i: (0, 0, 0))
    return _pcall(
        body, grid=(t // SG_BLOCK,),
        in_specs=[blk(4), blk(5), const2((1, SG_WIDTH)), w_spec, const2((SG_BLOCK, SG_GROUPS)), blk(1)],
        out_specs=[pl.BlockSpec((SG_BLOCK, 2 * SG_WIDTH), lambda i: (i, 2)), const2((1, SG_WIDTH)), w_spec,
                   const2((SG_BLOCK, LANES))],
        out_shape=[jax.ShapeDtypeStruct((t, PROJ_PAD), BF16), jax.ShapeDtypeStruct((1, SG_WIDTH), F32),
                   jax.ShapeDtypeStruct((SG_GROUPS, SG_BLOCK, SG_BLOCK), F32), jax.ShapeDtypeStruct((SG_BLOCK, LANES), F32)],
        compiler_params=_params("arbitrary"), name="sg_backward")(p, p, norm_g, w_s, b_t, dmix)


FFN_CT = D_FF // 2


def _ffn_act(up, conv_w, conv_b):
    t = up.shape[0]
    tm = _token_tile(t)
    nj = D_FF // FFN_CT

    def body(ug_ref, uv_ref, hg_ref, hv_ref, wg_ref, wv_ref, bg_ref, bv_ref, act_ref, xg_ref, xv_ref):
        first = pl.program_id(0) == 0
        _fill_with_prev(xg_ref, ug_ref[...], hg_ref[...], first)
        _fill_with_prev(xv_ref, uv_ref[...], hv_ref[...], first)

        def strip(row0):
            for c0 in range(0, FFN_CT, LANES):
                cols = slice(c0, c0 + LANES)
                cg = _causal_conv(_delayed(xg_ref, row0, cols, FFN_CONV), wg_ref[:, cols]) + bg_ref[:, cols]
                cv = _causal_conv(_delayed(xv_ref, row0, cols, FFN_CONV), wv_ref[:, cols]) + bv_ref[:, cols]
                act_ref[pl.ds(row0, STRIP), cols] = (silu(cg) * cv).astype(BF16)

        _for_strips(tm, STRIP, strip)

    tok = lambda off: pl.BlockSpec((tm, FFN_CT), lambda i, j: (i, j + off))
    halo = lambda off: pl.BlockSpec((HALO, FFN_CT), lambda i, j: (jnp.maximum(i * (tm // HALO) - 1, 0), j + off))
    par = lambda rows, off: pl.BlockSpec((rows, FFN_CT), lambda i, j: (0, j + off))
    return _pcall(
        body, grid=(t // tm, nj),
        in_specs=[tok(0), tok(nj), halo(0), halo(nj), par(FFN_CONV, 0), par(FFN_CONV, nj), par(1, 0), par(1, nj)],
        out_specs=pl.BlockSpec((tm, FFN_CT), lambda i, j: (i, j)),
        out_shape=jax.ShapeDtypeStruct((t, D_FF), BF16),
        scratch_shapes=[pltpu.VMEM((HALO + tm, FFN_CT), F32)] * 2,
        compiler_params=_params("parallel", "parallel"), name="ffn_act")(up, up, up, up, conv_w, conv_w, conv_b, conv_b)


def _ffn_bwd(up, conv_w, conv_b, dact):
    t = up.shape[0]
    tm = _pick(t, 128)
    n_tok = t // tm
    width = 2 * D_FF

    def dconv(delayed_g, delayed_v, da, wg, wv, bg, bv):
        cg = _causal_conv(delayed_g, wg) + bg
        cv = _causal_conv(delayed_v, wv) + bv
        s = sigmoid(cg)
        return da * cv * (s * (1.0 + cg * (1.0 - s))), da * (cg * s)

    def body(up_ref, prev_ref, next_ref, da_ref, dan_ref, w_ref, b_ref, dup_ref, dw_ref, db_ref, xp_ref, dc_ref, dw_acc, db_acc):
        first = pl.program_id(0) == 0
        last = pl.program_id(0) == n_tok - 1
        xp_ref[0:HALO, :] = jnp.where(first, 0.0, prev_ref[...])
        xp_ref[HALO:HALO + tm, :] = up_ref[...]
        xp_ref[HALO + tm:, :] = next_ref[...]
        dw_acc[...] = jnp.zeros_like(dw_acc)
        db_acc[...] = jnp.zeros_like(db_acc)

        def strip(row0):
            rows = pl.ds(row0, STRIP)
            for c0 in range(0, D_FF, LANES):
                gc, vc = slice(c0, c0 + LANES), slice(D_FF + c0, D_FF + c0 + LANES)
                del_g, del_v = _delayed(xp_ref, row0, gc, FFN_CONV), _delayed(xp_ref, row0, vc, FFN_CONV)
                dcg, dcv = dconv(del_g, del_v, da_ref[rows, gc], w_ref[:, gc], w_ref[:, vc], b_ref[:, gc], b_ref[:, vc])
                dc_ref[rows, gc] = dcg
                dc_ref[rows, vc] = dcv
                db_acc[:, gc] += _fold_rows(dcg)
                db_acc[:, vc] += _fold_rows(dcv)
                for j in range(FFN_CONV):
                    k = FFN_CONV - 1 - j
                    dw_acc[k * SUBLANES:(k + 1) * SUBLANES, gc] += _fold_rows(dcg * del_g[j])
                    dw_acc[k * SUBLANES:(k + 1) * SUBLANES, vc] += _fold_rows(dcv * del_v[j])

        _for_strips(tm, STRIP, strip)

        for c0 in range(0, D_FF, LANES):
            gc, vc = slice(c0, c0 + LANES), slice(D_FF + c0, D_FF + c0 + LANES)

            def delayed(cols):
                ext = xp_ref[tm:tm + 2 * HALO, cols]
                return [ext[HALO:, :]] + [pltpu.roll(ext, j, 0)[HALO:, :] for j in range(1, FFN_CONV)]

            dcg, dcv = dconv(delayed(gc), delayed(vc), dan_ref[:, gc], w_ref[:, gc], w_ref[:, vc], b_ref[:, gc], b_ref[:, vc])
            dc_ref[tm:, gc] = jnp.where(last, 0.0, dcg)
            dc_ref[tm:, vc] = jnp.where(last, 0.0, dcv)

        def strip_dx(row0):
            for c0 in range(0, width, LANES):
                cols = slice(c0, c0 + LANES)
                dup_ref[pl.ds(row0, STRIP), cols] = _advanced_conv(dc_ref, row0, cols, w_ref[:, cols]).astype(BF16)

        _for_strips(tm, STRIP, strip_dx)

        @pl.when(first)
        def _():
            dw_ref[...] = jnp.zeros_like(dw_ref)
            db_ref[...] = jnp.zeros_like(db_ref)

        for k in range(FFN_CONV):
            dw_ref[k:k + 1, :] += jnp.sum(dw_acc[k * SUBLANES:(k + 1) * SUBLANES, :], axis=0, keepdims=True)
        db_ref[...] += jnp.sum(db_acc[...], axis=0, keepdims=True)

    next_rows = lambda i: jnp.minimum((i + 1) * (tm // HALO), t // HALO - 1)
    full = lambda rows: pl.BlockSpec((rows, width), lambda i: (0, 0))
    return _pcall(
        body, grid=(n_tok,),
        in_specs=[pl.BlockSpec((tm, width), lambda i: (i, 0)),
                  pl.BlockSpec((HALO, width), lambda i: (jnp.maximum(i * (tm // HALO) - 1, 0), 0)),
                  pl.BlockSpec((HALO, width), lambda i: (next_rows(i), 0)),
                  pl.BlockSpec((tm, D_FF), lambda i: (i, 0)), pl.BlockSpec((HALO, D_FF), lambda i: (next_rows(i), 0)),
                  full(FFN_CONV), full(1)],
        out_specs=[pl.BlockSpec((tm, width), lambda i: (i, 0)), full(FFN_CONV), full(1)],
        out_shape=[jax.ShapeDtypeStruct((t, width), BF16), jax.ShapeDtypeStruct((FFN_CONV, width), F32),
                   jax.ShapeDtypeStruct((1, width), F32)],
        scratch_shapes=[pltpu.VMEM((tm + 2 * HALO, width), F32), pltpu.VMEM((tm + HALO, width), F32),
                        pltpu.VMEM((FFN_CONV * SUBLANES, width), F32), pltpu.VMEM((SUBLANES, width), F32)],
        compiler_params=_params("arbitrary"), name="ffn_bwd")(up, up, up, dact, dact, conv_w, conv_b)


def _final_loss(x3, target, g):
    t, d = x3.shape
    tm = _token_tile(t)

    def body(x_ref, t_ref, g_ref, loss_ref, dx_ref, dxb_ref, dg_ref):
        @pl.when(pl.program_id(0) == 0)
        def _():
            loss_ref[...] = jnp.zeros_like(loss_ref)
            dg_ref[...] = jnp.zeros_like(dg_ref)

        x = x_ref[...]
        y, r = rms_fwd(x, g_ref[...])
        err = y - t_ref[...]
        per_tok = jnp.mean(err * err, axis=-1, keepdims=True)
        loss_ref[...] += 0.5 * jnp.sum(per_tok, axis=0, keepdims=True)
        dx, dg_rows = rms_bwd(x, r, g_ref[...], err * (1.0 / d))
        dx_ref[...] = dx
        dxb_ref[...] = dx.astype(BF16)
        dg_ref[...] += jnp.sum(dg_rows, axis=0, keepdims=True)

    tile = pl.BlockSpec((tm, d), lambda i: (i, 0))
    row = pl.BlockSpec((1, d), lambda i: (0, 0))
    return _pcall(
        body, grid=(t // tm,), in_specs=[tile, tile, row],
        out_specs=[pl.BlockSpec((1, LANES), lambda i: (0, 0)), tile, tile, row],
        out_shape=[jax.ShapeDtypeStruct((1, LANES), F32), jax.ShapeDtypeStruct((t, d), F32), jax.ShapeDtypeStruct((t, d), BF16),
                   jax.ShapeDtypeStruct((1, d), F32)],
        compiler_params=_params("arbitrary"), name="final_loss")(x3, target, g)


def _my_position():
    return lax.axis_index("x"), lax.axis_index("y"), lax.axis_index("c")


COPIES = N_DEV - 1


def _all_gather(arrays):
    n = len(arrays)

    def body(*refs):
        x_refs, out_refs = refs[:n], refs[n:2 * n]
        send_sems, recv_sems, local_sems = refs[2 * n:]
        x, y, cc = _my_position()
        me, sibling = (x, y, cc), (x, y, 1 - cc)
        chips = [(1 - x, y), (x, 1 - y), (1 - x, 1 - y)]

        def block(a, px, py, pc):
            return out_refs[a].at[4 * px + 2 * py + pc]

        def copy(a, k, blk, to, src=None):
            return pltpu.make_async_remote_copy(
                src_ref=block(a, *blk) if src is None else src, dst_ref=block(a, *blk),
                send_sem=send_sems.at[a * COPIES + k], recv_sem=recv_sems.at[a * COPIES + k],
                device_id=to, device_id_type=MESH_ID)

        mine = [pltpu.make_async_copy(x_refs[a], block(a, *me), local_sems.at[a]) for a in range(n)]
        for cp in mine:
            cp.start()
        first = []
        for a in range(n):
            first.append(copy(a, 0, me, sibling, src=x_refs[a]))
            first += [copy(a, 1 + j, me, (*chip, cc), src=x_refs[a]) for j, chip in enumerate(chips)]
        for cp in first:
            cp.start()
        passed = []
        for j, chip in enumerate(chips):
            for a in range(n):
                copy(a, 1 + j, (*chip, cc), me).wait_recv()
                passed.append(copy(a, 4 + j, (*chip, cc), sibling))
                passed[-1].start()
        for a in range(n):
            copy(a, 0, sibling, me).wait_recv()
        for j, chip in enumerate(chips):
            for a in range(n):
                copy(a, 4 + j, (*chip, 1 - cc), me).wait_recv()
        for cp in first + passed:
            cp.wait_send()
        for cp in mine:
            cp.wait()

    any_spec = pl.BlockSpec(memory_space=pl.ANY)
    return _pcall(
        body, out_shape=[jax.ShapeDtypeStruct((N_DEV,) + a.shape, a.dtype) for a in arrays],
        in_specs=[any_spec] * n, out_specs=[any_spec] * n,
        scratch_shapes=[pltpu.SemaphoreType.DMA((n * COPIES,)), pltpu.SemaphoreType.DMA((n * COPIES,)),
                        pltpu.SemaphoreType.DMA((n,))],
        name="all_gather")(*arrays)


def _all_to_all(sends):
    n = len(sends)

    def body(*refs):
        send_refs, recv_refs = refs[:n], refs[n:2 * n]
        send_sems, recv_sems, local_sems = refs[2 * n:]
        x, y, cc = _my_position()
        me = 4 * x + 2 * y + cc
        mine = [pltpu.make_async_copy(send_refs[a].at[me], recv_refs[a].at[me], local_sems.at[a]) for a in range(n)]
        for cp in mine:
            cp.start()
        copies = []
        for rel in range(1, N_DEV):
            px, py, pc = x ^ (rel >> 2), y ^ ((rel >> 1) & 1), cc ^ (rel & 1)
            for a in range(n):
                copies.append(pltpu.make_async_remote_copy(
                    src_ref=send_refs[a].at[4 * px + 2 * py + pc], dst_ref=recv_refs[a].at[me],
                    send_sem=send_sems.at[a * COPIES + rel - 1], recv_sem=recv_sems.at[a * COPIES + rel - 1],
                    device_id=(px, py, pc), device_id_type=MESH_ID))
        for cp in copies:
            cp.start()
        for cp in copies:
            cp.wait()
        for cp in mine:
            cp.wait()

    any_spec = pl.BlockSpec(memory_space=pl.ANY)
    return _pcall(
        body, out_shape=[jax.ShapeDtypeStruct(s.shape, s.dtype) for s in sends],
        in_specs=[any_spec] * n, out_specs=[any_spec] * n,
        scratch_shapes=[pltpu.SemaphoreType.DMA((n * COPIES,)), pltpu.SemaphoreType.DMA((n * COPIES,)),
                        pltpu.SemaphoreType.DMA((n,))],
        name="all_to_all")(*sends)


def _hbm(a):
    return pltpu.with_memory_space_constraint(a, pltpu.HBM)


def _split_copies(send_refs, land_refs, send_sems, recv_sems, local_sems, gather):
    x, y, cc = _my_position()
    me = 4 * x + 2 * y + cc
    local, remote = [], []
    for a, (send, land) in enumerate(zip(send_refs, land_refs)):
        local.append(pltpu.make_async_copy(send if gather else send.at[me], land.at[me], local_sems.at[a]))
    for a, (send, land) in enumerate(zip(send_refs, land_refs)):
        for rel in range(1, N_DEV):
            px, py, pc = x ^ (rel >> 2), y ^ ((rel >> 1) & 1), cc ^ (rel & 1)
            remote.append(pltpu.make_async_remote_copy(
                src_ref=send if gather else send.at[4 * px + 2 * py + pc], dst_ref=land.at[me],
                send_sem=send_sems.at[a * COPIES + rel - 1], recv_sem=recv_sems.at[a * COPIES + rel - 1],
                device_id=(px, py, pc), device_id_type=MESH_ID))
    return local, remote


SPLIT_EFFECT = pltpu.SideEffectType.DATAFLOW_SIDE_EFFECTING


def _exchange_start(sends, after, gather, name):
    n = len(sends)
    lands = [_hbm(lax.empty((N_DEV,) + s.shape if gather else s.shape, s.dtype)) for s in sends]

    def body(*refs):
        send_refs, land_refs = refs[:n], refs[n:2 * n]
        send_sems, recv_sems, local_sems = refs[2 * n + 1:2 * n + 4]
        token = refs[-1]
        local, remote = _split_copies(send_refs, land_refs, send_sems, recv_sems, local_sems, gather)
        for cp in local + remote:
            cp.start()
        token[...] = jnp.zeros_like(token)

    hbm, sem = pl.BlockSpec(memory_space=pltpu.HBM), pl.BlockSpec(memory_space=pltpu.SEMAPHORE)
    out = _pcall(
        body, name=name,
        out_shape=[pltpu.SemaphoreType.DMA((n * COPIES,)), pltpu.SemaphoreType.DMA((n * COPIES,)), pltpu.SemaphoreType.DMA((n,))]
        + [pltpu.HBM(s.shape, s.dtype) for s in sends] + [pltpu.HBM(z.shape, z.dtype) for z in lands]
        + [jax.ShapeDtypeStruct((SUBLANES, LANES), F32)],
        in_specs=[hbm] * (2 * n) + [pl.BlockSpec(memory_space=pl.ANY)],
        out_specs=[sem] * 3 + [hbm] * (2 * n) + [pl.BlockSpec(memory_space=pltpu.VMEM)],
        input_output_aliases={i: 3 + i for i in range(2 * n)},
        compiler_params=pltpu.CompilerParams(has_side_effects=SPLIT_EFFECT),
    )(*[_hbm(s) for s in sends], *lands, after)
    return dict(sems=out[:3], sends=out[3:3 + n], lands=out[3 + n:3 + 2 * n], gather=gather), out[-1]


def _exchange_wait(handle, after, name):
    sends, lands, gather = handle["sends"], handle["lands"], handle["gather"]
    n = len(sends)

    def body(*refs):
        send_refs, land_refs = refs[:n], refs[n:2 * n]
        send_sems, recv_sems, local_sems = refs[2 * n:2 * n + 3]
        local, remote = _split_copies(send_refs, land_refs, send_sems, recv_sems, local_sems, gather)
        for cp in remote:
            cp.wait_send()
            cp.wait_recv()
        for cp in local:
            cp.wait()

    hbm, sem = pl.BlockSpec(memory_space=pltpu.HBM), pl.BlockSpec(memory_space=pltpu.SEMAPHORE)
    out = _pcall(
        body, name=name,
        out_shape=[pltpu.HBM(s.shape, s.dtype) for s in sends] + [pltpu.HBM(z.shape, z.dtype) for z in lands],
        in_specs=[hbm] * (2 * n) + [sem] * 3 + [pl.BlockSpec(memory_space=pl.ANY)],
        out_specs=[hbm] * (2 * n), input_output_aliases={i: i for i in range(2 * n)},
        compiler_params=pltpu.CompilerParams(has_side_effects=SPLIT_EFFECT),
    )(*sends, *lands, *handle["sems"], after)
    return out[n:]


def _join_shards(g, n_local, out_cols, name):
    _, r, wp = g.shape
    tr = min(r, 256)

    def body(g_ref, o_ref, acc_ref):
        acc_ref[...] = jnp.zeros_like(acc_ref)
        for k in range(N_DEV):
            shift = (n_local * k) % LANES
            start = n_local * k - shift
            piece = g_ref[k].astype(F32)
            if shift:
                piece = pltpu.roll(piece, shift, 1)
            acc_ref[:, start:start + wp] += piece
        o_ref[...] = acc_ref[...].astype(BF16)

    return _pcall(
        body, grid=(r // tr,), in_specs=[pl.BlockSpec((N_DEV, tr, wp), lambda i: (0, i, 0))],
        out_specs=pl.BlockSpec((tr, out_cols), lambda i: (i, 0)), out_shape=jax.ShapeDtypeStruct((r, out_cols), BF16),
        scratch_shapes=[pltpu.VMEM((tr, out_cols), F32)], compiler_params=_params("parallel"), name=name)(g)


def _split_shards(full, n_local, wp, name):
    r, c = full.shape
    tr = min(r, 256)

    def body(x_ref, o_ref):
        lane = lax.broadcasted_iota(jnp.int32, (tr, wp), 1)
        for k in range(N_DEV):
            shift = (n_local * k) % LANES
            start = n_local * k - shift
            win = x_ref[:, start:start + wp]
            if shift:
                win = pltpu.roll(win, wp - shift, 1)
            o_ref[k] = jnp.where(lane < n_local, win, 0.0).astype(BF16)

    return _pcall(
        body, grid=(r // tr,), in_specs=[pl.BlockSpec((tr, c), lambda i: (i, 0))],
        out_specs=pl.BlockSpec((N_DEV, tr, wp), lambda i: (0, i, 0)), out_shape=jax.ShapeDtypeStruct((N_DEV, r, wp), BF16),
        compiler_params=_params("parallel"), name=name)(full)


def _sum_and_adamw(recv, w, m, v, name):
    _, r, wp = recv.shape
    c = w.shape[-1]
    lead = w.ndim == 3
    tr = SLAB_ROW_TILE if r % SLAB_ROW_TILE == 0 else (SLAB_ROW_TILE // 4 if r % (SLAB_ROW_TILE // 4) == 0 else r)
    bc1 = 1.0 - ADAM_B1 ** ADAM_STEP
    bc2 = 1.0 - ADAM_B2 ** ADAM_STEP

    def body(recv_ref, w_ref, m_ref, v_ref, g_ref, d_ref, nm_ref, nv_ref):
        g = recv_ref[0, :, 0:c].astype(F32)
        for s in range(1, N_DEV):
            g = g + recv_ref[s, :, 0:c].astype(F32)
        m_new = ADAM_B1 * m_ref[...] + (1.0 - ADAM_B1) * g
        v_new = ADAM_B2 * v_ref[...] + (1.0 - ADAM_B2) * (g * g)
        m_hat = m_new / bc1
        v_hat = v_new / bc2
        g_ref[...] = g
        d_ref[...] = -ADAM_LR * (m_hat / (jnp.sqrt(v_hat) + ADAM_EPS) + ADAM_WD * w_ref[...])
        nm_ref[...] = m_new
        nv_ref[...] = v_new

    tile = pl.BlockSpec((None, tr, c), lambda i: (0, i, 0)) if lead else pl.BlockSpec((tr, c), lambda i: (i, 0))
    return _pcall(
        body, grid=(r // tr,),
        in_specs=[pl.BlockSpec((N_DEV, tr, wp), lambda i: (0, i, 0)), tile, tile, tile],
        out_specs=[tile] * 4, out_shape=[jax.ShapeDtypeStruct(w.shape, F32)] * 4,
        compiler_params=_params("parallel"), name=name)(recv, w, m, v)


SHARDED_TAPS = ("dn_conv_w", "ffn_conv_w")
REPLICATED = ("attn_norm_g", "dn_a_log", "dn_dt_bias", "dn_out_norm_g", "sg_norm_g", "sg_w", "sg_b", "ffn_norm_g",
              "ffn_conv_b", "final_norm_g")
SMALL = SHARDED_TAPS + REPLICATED
WEIGHT_ORDER = ("attn_norm_g", "w_in", "dn_conv_w", "dn_a_log", "dn_dt_bias", "dn_out_norm_g", "sg_norm_g", "sg_w", "sg_b",
                "w_out", "ffn_norm_g", "w_up", "ffn_conv_w", "ffn_conv_b", "w_down", "final_norm_g")
SLAB_COLS = 1024
SLAB_ROW_TILE = 128


def _pad_to(flat, multiple):
    pad = (-flat.shape[-1]) % multiple
    if pad == 0:
        return flat
    return jnp.pad(flat, [(0, 0)] * (flat.ndim - 1) + [(0, pad)])


def _lane_padded(n):
    return -(-n // LANES) * LANES


def _pack_small(named):
    flat = jnp.concatenate([named[n].reshape(-1) for n in SMALL])
    return _pad_to(flat, SUBLANES * SLAB_COLS).reshape(-1, SLAB_COLS)


def _unpack_small(slab, like):
    flat = slab.reshape(-1)
    out, off = {}, 0
    for n in SMALL:
        size = like[n].size
        out[n] = flat[off:off + size].reshape(like[n].shape)
        off += size
    return out


def _split_columns(full, n_local):
    r = full.shape[0]
    return full.reshape(r, N_DEV, n_local).transpose(1, 0, 2).reshape(N_DEV, r * n_local)


def _join_columns(blocks, r, n_local):
    return blocks.reshape(N_DEV, r, n_local).transpose(1, 0, 2).reshape(r, N_DEV * n_local)


def _lanes4(a):
    return jnp.pad(a.reshape(1, N_HEADS), ((0, 0), (0, LANES - N_HEADS)))


def kernel(x, attn_norm_g, w_in, dn_conv_w, dn_a_log, dn_dt_bias, dn_out_norm_g, sg_norm_g, sg_w, sg_b, w_out, ffn_norm_g, w_up, ffn_conv_w, ffn_conv_b, w_down, final_norm_g, loss_target, m_attn_norm_g, m_w_in, m_dn_conv_w, m_dn_a_log, m_dn_dt_bias, m_dn_out_norm_g, m_sg_norm_g, m_sg_w, m_sg_b, m_w_out, m_ffn_norm_g, m_w_up, m_ffn_conv_w, m_ffn_conv_b, m_w_down, m_final_norm_g, v_attn_norm_g, v_w_in, v_dn_conv_w, v_dn_a_log, v_dn_dt_bias, v_dn_out_norm_g, v_sg_norm_g, v_sg_w, v_sg_b, v_w_out, v_ffn_norm_g, v_w_up, v_ffn_conv_w, v_ffn_conv_b, v_w_down, v_final_norm_g):
    weights = dict(attn_norm_g=attn_norm_g, w_in=w_in, dn_conv_w=dn_conv_w, dn_a_log=dn_a_log, dn_dt_bias=dn_dt_bias,
                   dn_out_norm_g=dn_out_norm_g, sg_norm_g=sg_norm_g, sg_w=sg_w, sg_b=sg_b, w_out=w_out, ffn_norm_g=ffn_norm_g,
                   w_up=w_up, ffn_conv_w=ffn_conv_w, ffn_conv_b=ffn_conv_b, w_down=w_down, final_norm_g=final_norm_g)
    m_in = dict(attn_norm_g=m_attn_norm_g, w_in=m_w_in, dn_conv_w=m_dn_conv_w, dn_a_log=m_dn_a_log, dn_dt_bias=m_dn_dt_bias,
                dn_out_norm_g=m_dn_out_norm_g, sg_norm_g=m_sg_norm_g, sg_w=m_sg_w, sg_b=m_sg_b, w_out=m_w_out,
                ffn_norm_g=m_ffn_norm_g, w_up=m_w_up, ffn_conv_w=m_ffn_conv_w, ffn_conv_b=m_ffn_conv_b, w_down=m_w_down,
                final_norm_g=m_final_norm_g)
    v_in = dict(attn_norm_g=v_attn_norm_g, w_in=v_w_in, dn_conv_w=v_dn_conv_w, dn_a_log=v_dn_a_log, dn_dt_bias=v_dn_dt_bias,
                dn_out_norm_g=v_dn_out_norm_g, sg_norm_g=v_sg_norm_g, sg_w=v_sg_w, sg_b=v_sg_b, w_out=v_w_out,
                ffn_norm_g=v_ffn_norm_g, w_up=v_w_up, ffn_conv_w=v_ffn_conv_w, ffn_conv_b=v_ffn_conv_b, w_down=v_w_down,
                final_norm_g=v_final_norm_g)

    n_in, n_up = w_in.shape[2], w_up.shape[2]
    r_out, r_down = w_out.shape[1], w_down.shape[1]
    n_dnc, n_ffc = dn_conv_w.shape[2], ffn_conv_w.shape[2]
    wp_in, wp_up = _lane_padded(n_in), _lane_padded(n_up)
    taps = _pad_to(jnp.concatenate([dn_conv_w.reshape(-1), ffn_conv_w.reshape(-1)]), SUBLANES * LANES).reshape(-1, LANES)
    g_in, g_taps = _all_gather([jnp.pad(w_in[0].astype(BF16), ((0, 0), (0, wp_in - n_in))), taps])
    late_weights, token = _exchange_start(
        [w_out[0].astype(BF16), jnp.pad(w_up[0].astype(BF16), ((0, 0), (0, wp_up - n_up))), w_down[0].astype(BF16)],
        g_taps, True, "gather_late_start")
    w_in_p = _join_shards(g_in, n_in, PROJ_PAD, "join_w_in")
    taps_all = g_taps.reshape(N_DEV, -1)
    dn_conv_full = _join_columns(taps_all[:, :CONV_K * n_dnc], CONV_K, n_dnc)
    ffn_conv_full = _join_columns(taps_all[:, CONV_K * n_dnc:CONV_K * n_dnc + FFN_CONV * n_ffc], FFN_CONV, n_ffc)

    def late(after):
        g_out, g_up, g_down = _exchange_wait(late_weights, after, "gather_late_wait")
        return (_join_shards(g_up, n_up, N_DEV * n_up, "join_w_up"),
                g_out.reshape(N_DEV * r_out, D_MODEL), g_down.reshape(N_DEV * r_down, D_MODEL))

    def send_early(blocks, after, name):
        return _exchange_start(blocks, after, False, name)

    loss_lanes, grad_x, g, early = _local_step(
        x[0], loss_target[0], w_in_p, late, send_early, dn_conv_full, ffn_conv_full, attn_norm_g + token[0:1, 0:1],
        dn_a_log, dn_dt_bias, dn_out_norm_g, sg_norm_g, sg_w, sg_b, ffn_norm_g, ffn_conv_b, final_norm_g,
        (n_in, wp_in), (n_up, wp_up))

    small = jnp.concatenate([g[n].reshape(-1) for n in REPLICATED])
    small_send = jnp.concatenate([_split_columns(g["dn_conv_w"], n_dnc), _split_columns(g["ffn_conv_w"], n_ffc),
                                  jnp.broadcast_to(small[None, :], (N_DEV, small.shape[0]))], axis=1)
    small_send = _pad_to(small_send, SUBLANES * SLAB_COLS).reshape(N_DEV, -1, SLAB_COLS)
    r_small, = _all_to_all([small_send])
    r_dn, = _exchange_wait(early[0], r_small, "send_dw_down_wait")
    r_up, r_o = _exchange_wait(early[1], r_small, "send_dw_up_out_wait")
    r_in, = _exchange_wait(early[2], r_small, "send_dw_in_wait")

    upd = {
        "w_in": _sum_and_adamw(r_in, w_in, m_w_in, v_w_in, "adamw_w_in"),
        "w_up": _sum_and_adamw(r_up, w_up, m_w_up, v_w_up, "adamw_w_up"),
        "w_out": _sum_and_adamw(r_o, w_out, m_w_out, v_w_out, "adamw_w_out"),
        "w_down": _sum_and_adamw(r_dn, w_down, m_w_down, v_w_down, "adamw_w_down"),
    }
    small_upd = _sum_and_adamw(r_small, _pack_small(weights), _pack_small(m_in), _pack_small(v_in), "adamw_small")
    results = []
    for i in range(4):
        named = _unpack_small(small_upd[i], weights)
        named.update({n: upd[n][i] for n in upd})
        results.append(named)

    loss = lax.psum(loss_lanes[0, 0], MESH_AXES)
    return (loss, grad_x[None], *[r[n] for r in results for n in WEIGHT_ORDER])


def _local_step(x2d, tgt, w_in_p, late_weights, send_early, dn_conv_full, ffn_conv_full, attn_norm_g, dn_a_log,
                dn_dt_bias, dn_out_norm_g, sg_norm_g, sg_w, sg_b, ffn_norm_g, ffn_conv_b, final_norm_g, in_shard, up_shard):
    t = x2d.shape[0]
    g1, g2, gf = attn_norm_g, ffn_norm_g, final_norm_g.reshape(1, D_MODEL)
    a_log4, dt_bias4 = _lanes4(dn_a_log), _lanes4(dn_dt_bias)
    sg_w3 = sg_w[0]
    sg_b_t = sg_b[0].T
    conv_b = ffn_conv_b

    h1, rstd1 = _rmsnorm_fwd(x2d, g1)
    p = _matmul(h1, w_in_p, "nn", "in_proj", (512, PROJ_PAD, D_MODEL))
    q, k, v, beta4, g4 = _dn_prep(p, dn_conv_full, a_log4, dt_bias4)
    mix_half, s_all = _dn_forward(q, k, v, beta4, g4, p, dn_out_norm_g)
    mix = _sg_forward(p, sg_norm_g, sg_w3, sg_b_t, mix_half)
    w_up_full, w_out_full, w_down_full = late_weights(mix)
    x2 = _matmul(mix, w_out_full, "nn", "out_proj", (1024, 1024, 1024), add=x2d)
    h2, rstd2 = _rmsnorm_fwd(x2, g2)
    up = _matmul(h2, w_up_full, "nn", "up_proj", (512, D_FF, D_MODEL))
    act = _ffn_act(up, ffn_conv_full, conv_b)
    x3 = _matmul(act, w_down_full, "nn", "down_proj", (512, 1024, D_FF), add=x2)
    loss_lanes, dx3, dx3b, d_gf = _final_loss(x3, tgt, gf)

    dact = _matmul(dx3b, w_down_full, "nt", "down_proj_dx", (512, D_FF, D_MODEL))
    d_w_down = _matmul(act, dx3b, "tn", "down_proj_dw", (256, 1024, t), out_dtype=BF16)
    sent_down, token = send_early([d_w_down.reshape(N_DEV, D_FF // N_DEV, D_MODEL)], d_w_down, "send_dw_down")
    dup, d_ffn_conv, d_ffn_conv_b = _ffn_bwd(up, ffn_conv_full, conv_b + token[0:1, 0:1], dact)
    dh2 = _matmul(dup, w_up_full, "nt", "up_proj_dx", (256, 1024, 2 * D_FF))
    d_w_up = _matmul(h2, dup, "tn", "up_proj_dw", (1024, 512, t))
    dx2, dx2b, d_g2 = _rmsnorm_bwd(x2, rstd2, g2, dh2, dx3)
    dmix = _matmul(dx2b, w_out_full, "nt", "out_proj_dx", (1024, 1024, 1024))
    d_w_out = _matmul(mix, dx2b, "tn", "out_proj_dw", (512, 1024, t), out_dtype=BF16)
    sent_up_out, token = send_early(
        [_split_shards(d_w_up, up_shard[0], up_shard[1], "split_dw_up"), d_w_out.reshape(N_DEV, D_MODEL // N_DEV, D_MODEL)],
        d_w_out, "send_dw_up_out")
    dp, d_sg_norm, d_sg_w, d_sg_b_t = _sg_backward(p, sg_norm_g + token[0:1, 0:1], sg_w3, sg_b_t, dmix)
    dq, dk, dv, dbeta4, dg4, dp, d_dn_norm = _dn_backward(q, k, v, beta4, g4, p, dn_out_norm_g, s_all, dmix, dp)
    dc_dn, d_dn_conv, dp, d_a_log4, d_dt_bias4 = _dn_prep_bwd(p, dn_conv_full, a_log4, dt_bias4, dq, dk, dv, dbeta4, dg4, dp)
    dp = _conv_bwd_input(dc_dn, dn_conv_full, "dn_conv_dx", out_cols=PROJ_PAD, into=dp)
    d_w_in_p = _matmul(h1, dp, "tn", "in_proj_dw", (1024, PROJ_PAD // 5, t))
    sent_in, token = send_early([_split_shards(d_w_in_p, in_shard[0], in_shard[1], "split_dw_in")], d_w_in_p, "send_dw_in")
    dh1 = _matmul(dp, w_in_p, "nt", "in_proj_dx", (512, 1024, PROJ_PAD))
    grad_x, _, d_g1 = _rmsnorm_bwd(x2d, rstd1, g1 + token[0:1, 0:1], dh1, dx2)

    grads = dict(
        attn_norm_g=d_g1, dn_conv_w=d_dn_conv, dn_a_log=d_a_log4[:, :N_HEADS], dn_dt_bias=d_dt_bias4[:, :N_HEADS],
        dn_out_norm_g=d_dn_norm, sg_norm_g=d_sg_norm, sg_w=d_sg_w, sg_b=d_sg_b_t[:, :SG_GROUPS].T,
        ffn_norm_g=d_g2, ffn_conv_w=d_ffn_conv, ffn_conv_b=d_ffn_conv_b, final_norm_g=d_gf)
    return loss_lanes, grad_x, grads, (sent_down, sent_up_out, sent_in)
```

```python
import functools
import math

import jax
import jax.numpy as jnp
from jax import lax
from jax.experimental import pallas as pl
from jax.experimental.pallas import tpu as pltpu

F32 = jnp.float32
BF16 = jnp.bfloat16
HI = lax.Precision.HIGHEST

D_MODEL = 1024
DN_WIDTH = 512
HEAD_DIM = 128
N_HEADS = 4
SG_WIDTH = 512
SG_GROUPS = 4
SG_DIM = 128
SG_BLOCK = 128
D_FF = 2816
CHUNK = 64
CONV_K = 4
FFN_CONV = 3
EPS = 1e-6
PROJ_COLS = 3080
PROJ_MAIN = 3072
PROJ_PAD = 3200
GELU_C = math.sqrt(2.0 / math.pi)
N_DEV = 8
LANES = 128
SUBLANES = 8
HALO = SUBLANES
VMEM_LIMIT = 48 * 1024 * 1024

ADAM_LR = 0.001
ADAM_B1 = 0.9
ADAM_B2 = 0.999
ADAM_EPS = 1e-08
ADAM_WD = 0.01
ADAM_STEP = 10

MESH_AXES = ("x", "y", "c")
MESH_ID = pl.DeviceIdType.MESH


def _pcall(body, **kw):
    return pl.pallas_call(body, **kw)


def _params(*sem):
    return pltpu.CompilerParams(dimension_semantics=sem, vmem_limit_bytes=VMEM_LIMIT)


def _pick(n, cap):
    best = None
    for t in range(LANES, cap + 1, LANES):
        if n % t == 0:
            best = t
    return best if best else n


FAST, MID, EXACT = "bf16 operands, one pass", "three bf16 passes", "six bf16 passes"


def dot_f32(a, b, dims, tier):
    if tier == FAST:
        return lax.dot_general(a.astype(BF16), b.astype(BF16), dims, preferred_element_type=F32)
    prec = lax.Precision.HIGH if tier == MID else HI
    return lax.dot_general(a, b, dims, precision=prec, preferred_element_type=F32)


def dot_nn(a, b, tier=EXACT):
    return dot_f32(a, b, (((1,), (0,)), ((), ())), tier)


def dot_nt(a, b, tier=EXACT):
    return dot_f32(a, b, (((1,), (1,)), ((), ())), tier)


def dot_tn(a, b, tier=EXACT):
    return dot_f32(a, b, (((0,), (0,)), ((), ())), tier)


def sigmoid(x):
    return 1.0 / (1.0 + jnp.exp(-x))


def silu(x):
    return x * sigmoid(x)


def silu_grad(x):
    s = sigmoid(x)
    return s * (1.0 + x * (1.0 - s))


def gelu(x):
    return 0.5 * x * (1.0 + jnp.tanh(GELU_C * (x + 0.044715 * x * x * x)))


def gelu_grad(x):
    t = jnp.tanh(GELU_C * (x + 0.044715 * x * x * x))
    return 0.5 * (1.0 + t) + 0.5 * x * (1.0 - t * t) * GELU_C * (1.0 + 3.0 * 0.044715 * x * x)


def softplus(z):
    return jnp.maximum(z, 0.0) + jnp.log(1.0 + jnp.exp(-jnp.abs(z)))


def rms_fwd(x, g):
    r = lax.rsqrt(jnp.mean(x * x, axis=-1, keepdims=True) + EPS)
    return x * r * g, r


def rms_bwd(x, r, g, dy):
    dyg = dy * g
    xr = x * r
    dx = r * (dyg - xr * jnp.mean(dyg * xr, axis=-1, keepdims=True))
    return dx, dy * xr


def l2_fwd(x):
    r = lax.rsqrt(jnp.sum(x * x, axis=-1, keepdims=True) + EPS)
    return x * r, r


def l2_bwd(x, r, dy):
    xr = x * r
    return r * (dy - xr * jnp.sum(dy * xr, axis=-1, keepdims=True))


def _tri_masks(n):
    row = lax.broadcasted_iota(jnp.int32, (n, n), 0)
    col = lax.broadcasted_iota(jnp.int32, (n, n), 1)
    return row >= col, row > col


def chunk_cumsum(g4):
    incl, _ = _tri_masks(g4.shape[0])
    return dot_nn(incl.astype(F32), g4)


STACK = N_HEADS * CHUNK
DN_CHUNKS = 2


def _head_rows(h):
    return slice(h * CHUNK, (h + 1) * CHUNK)


def _stack_heads(x):
    return jnp.concatenate([x[:, h * HEAD_DIM:(h + 1) * HEAD_DIM] for h in range(N_HEADS)], axis=0)


def _stack_lanes(x4):
    return jnp.concatenate([x4[:, h:h + 1] for h in range(N_HEADS)], axis=0)


def _per_head(fn):
    return jnp.concatenate([fn(h) for h in range(N_HEADS)], axis=0)


def _unit_lower_inverse(l_strict, order):
    c = l_strict.shape[0]
    row = lax.broadcasted_iota(jnp.int32, (c, c), 0)
    col = lax.broadcasted_iota(jnp.int32, (c, c), 1)
    n = -l_strict
    a = (row == col).astype(F32) + n
    p = n
    for _ in range(int(math.log2(order)) - 1):
        p = dot_nn(p, p, FAST)
        a = a + dot_nn(a, p, FAST)
    return a


def dn_chunk_local(q, k, v, beta, gc4):
    row = lax.broadcasted_iota(jnp.int32, (STACK, STACK), 0)
    col = lax.broadcasted_iota(jnp.int32, (STACK, STACK), 1)
    same = (row // CHUNK) == (col // CHUNK)
    incl = jnp.logical_and(same, row >= col)
    strict = jnp.logical_and(same, row > col)
    gc_col = _stack_lanes(gc4)
    gc_row = jnp.sum(jnp.where(row == col, gc_col, 0.0), axis=0, keepdims=True)
    decay = jnp.where(incl, jnp.exp(jnp.minimum(gc_col - gc_row, 0.0)), 0.0)
    gamma = jnp.exp(gc_col)
    gc_last = jnp.concatenate([jnp.broadcast_to(gc4[CHUNK - 1:CHUNK, h:h + 1], (CHUNK, 1)) for h in range(N_HEADS)], axis=0)
    tau = jnp.exp(gc_last - gc_col)
    cd = jnp.exp(gc_last)
    kb = k * beta
    l_mat = jnp.where(strict, dot_nt(kb, k, FAST) * decay, 0.0)
    a_inv = _unit_lower_inverse(l_mat, CHUNK)
    sol = dot_nn(a_inv, jnp.concatenate([v * beta, kb * gamma], axis=1), FAST)
    value, kcd = sol[:, :HEAD_DIM], sol[:, HEAD_DIM:]
    attn = jnp.where(incl, dot_nt(q, k, FAST) * decay, 0.0)
    return dict(decay=decay, gamma=gamma, tau=tau, cd=cd, kb=kb, l_mat=l_mat, a_inv=a_inv, sol=sol, value=value, kcd=kcd,
                attn=attn, qd=q * gamma, kt=k * tau, incl=incl, strict=strict)


def dn_chunk_state(loc, s):
    kcd, qd, kt, cd = loc["kcd"], loc["qd"], loc["kt"], loc["cd"]
    v_new = loc["value"] - _per_head(lambda h: dot_nn(kcd[_head_rows(h)], s[h], FAST))
    o = _per_head(lambda h: dot_nn(qd[_head_rows(h)], s[h], FAST)) + dot_nn(loc["attn"], v_new, FAST)
    s_new = [s[h] * cd[h * CHUNK:h * CHUNK + 1, :] + dot_tn(kt[_head_rows(h)], v_new[_head_rows(h)], FAST)
             for h in range(N_HEADS)]
    loc["v_new"] = v_new
    return o, s_new


def dn_chunk_bwd(loc, q, k, v, beta, s, do, ds_new):
    s_of, ds_new_of = (lambda h: s[h]), (lambda h: ds_new[h])
    decay, gamma, tau, cd = loc["decay"], loc["gamma"], loc["tau"], loc["cd"]
    a_inv, attn, l_mat, sol = loc["a_inv"], loc["attn"], loc["l_mat"], loc["sol"]
    v_new, qd, kt, kcd, kb = loc["v_new"], loc["qd"], loc["kt"], loc["kcd"], loc["kb"]
    hr = _head_rows

    dv_new = dot_tn(attn, do, FAST) + _per_head(lambda h: dot_nn(kt[hr(h)], ds_new_of(h), FAST))
    dattn = jnp.where(loc["incl"], dot_nt(do, v_new, FAST), 0.0)
    dqd = _per_head(lambda h: dot_nt(do[hr(h)], s_of(h), FAST))
    ds = [dot_tn(qd[hr(h)], do[hr(h)], FAST) + ds_new_of(h) * cd[h * CHUNK:h * CHUNK + 1, :]
          - dot_tn(kcd[hr(h)], dv_new[hr(h)], FAST) for h in range(N_HEADS)]
    dkt = _per_head(lambda h: dot_nt(v_new[hr(h)], ds_new_of(h), FAST))
    dkcd = -_per_head(lambda h: dot_nt(dv_new[hr(h)], s_of(h), FAST))
    drhs = dot_tn(a_inv, jnp.concatenate([dv_new, dkcd], axis=1), FAST)
    dvb, dkbg = drhs[:, :HEAD_DIM], drhs[:, HEAD_DIM:]
    dl = jnp.where(loc["strict"], -dot_nt(drhs, sol, FAST), 0.0)
    dkk = dl * decay
    dqk = dattn * decay
    e = dl * l_mat + dattn * attn
    dgc = jnp.sum(e, axis=1, keepdims=True) - jnp.sum(e, axis=0, keepdims=True).T
    dkb = dot_nn(dkk, k, FAST) + dkbg * gamma
    dk = dot_tn(dkk, kb, FAST) + dot_tn(dqk, q, FAST) + dkt * tau
    dq = dot_nn(dqk, k, FAST) + dqd * gamma
    dgamma = jnp.sum(dkbg * kb, axis=1, keepdims=True) + jnp.sum(dqd * q, axis=1, keepdims=True)
    dtau_tau = jnp.sum(dkt * k, axis=1, keepdims=True) * tau
    dgc = dgc + dgamma * gamma - dtau_tau
    is_last = (lax.broadcasted_iota(jnp.int32, (STACK, 1), 0) % CHUNK) == CHUNK - 1

    def last_term(h):
        s, ds_new = s_of(h), ds_new_of(h)
        dcd = jnp.sum(jnp.sum(ds_new * s, axis=1, keepdims=True), axis=0, keepdims=True)
        total = jnp.sum(dtau_tau[hr(h)], axis=0, keepdims=True) + dcd * cd[h * CHUNK:h * CHUNK + 1, :]
        return jnp.broadcast_to(total, (CHUNK, 1))

    dgc = dgc + jnp.where(is_last, _per_head(last_term), 0.0)
    dk = dk + dkb * beta
    dbeta = jnp.sum(dkb * k, axis=1, keepdims=True) + jnp.sum(dvb * v, axis=1, keepdims=True)
    dv = dvb * beta
    return dq, dk, dv, dbeta, dgc, ds


def _token_tile(t):
    return _pick(t, 256)


STRIP = 32
NORM_STRIP = 16


def _for_strips(n_rows, rows, fn):
    def step(r, carry):
        fn(pl.multiple_of(r * rows, rows))
        return carry

    lax.fori_loop(0, n_rows // rows, step, 0)


def _fold_rows(x):
    out = x[0:SUBLANES, :]
    for i in range(1, x.shape[0] // SUBLANES):
        out = out + x[i * SUBLANES:(i + 1) * SUBLANES, :]
    return out


def _rmsnorm_fwd(x, g):
    t, d = x.shape
    tm = _token_tile(t)

    def body(x_ref, g_ref, h_ref, r_ref):
        y, r = rms_fwd(x_ref[...], g_ref[...])
        h_ref[...] = y.astype(BF16)
        r_ref[...] = r

    return _pcall(
        body, grid=(t // tm,),
        in_specs=[pl.BlockSpec((tm, d), lambda i: (i, 0)), pl.BlockSpec((1, d), lambda i: (0, 0))],
        out_specs=[pl.BlockSpec((tm, d), lambda i: (i, 0)), pl.BlockSpec((tm, 1), lambda i: (i, 0))],
        out_shape=[jax.ShapeDtypeStruct((t, d), BF16), jax.ShapeDtypeStruct((t, 1), F32)],
        compiler_params=_params("parallel"), name="rmsnorm_fwd")(x, g)


def _rmsnorm_bwd(x, r, g, dh, dres):
    t, d = x.shape
    tm = _token_tile(t)

    def body(x_ref, r_ref, g_ref, dh_ref, dres_ref, dx_ref, dxb_ref, dg_ref):
        dx, dg_rows = rms_bwd(x_ref[...], r_ref[...], g_ref[...], dh_ref[...])
        dx = dx + dres_ref[...]
        dx_ref[...] = dx
        dxb_ref[...] = dx.astype(BF16)

        @pl.when(pl.program_id(0) == 0)
        def _():
            dg_ref[...] = jnp.zeros_like(dg_ref)

        dg_ref[...] += jnp.sum(dg_rows, axis=0, keepdims=True)

    tile = pl.BlockSpec((tm, d), lambda i: (i, 0))
    row = pl.BlockSpec((1, d), lambda i: (0, 0))
    return _pcall(
        body, grid=(t // tm,),
        in_specs=[tile, pl.BlockSpec((tm, 1), lambda i: (i, 0)), row, tile, tile],
        out_specs=[tile, tile, row],
        out_shape=[jax.ShapeDtypeStruct((t, d), F32), jax.ShapeDtypeStruct((t, d), BF16), jax.ShapeDtypeStruct((1, d), F32)],
        compiler_params=_params("arbitrary"), name="rmsnorm_bwd")(x, r, g, dh, dres)


def _matmul(a, b, mode, name, tiles, add=None, out_dtype=F32):
    if mode == "nn":
        (m, k), n = a.shape, b.shape[1]
    elif mode == "nt":
        (m, k), n = a.shape, b.shape[0]
    else:
        (k, m), n = a.shape, b.shape[1]
    tm, tn, tk = min(tiles[0], m), min(tiles[1], n), min(tiles[2], k)
    assert m % tm == 0 and n % tn == 0 and k % tk == 0, (name, m, n, k, tiles)
    nk = k // tk
    dims = {"nn": (((1,), (0,)), ((), ())), "nt": (((1,), (1,)), ((), ())), "tn": (((0,), (0,)), ((), ()))}[mode]

    def finish(res, add_ref, o_ref):
        if add_ref is not None:
            res = res + add_ref[...]
        o_ref[...] = res.astype(o_ref.dtype)

    def body(*refs):
        a_ref, b_ref = refs[0], refs[1]
        add_ref = refs[2] if add is not None else None
        o_ref = refs[3] if add is not None else refs[2]
        part = lax.dot_general(a_ref[...], b_ref[...], dims, preferred_element_type=F32)
        if nk == 1:
            finish(part, add_ref, o_ref)
            return
        acc_ref = refs[-1]
        kk = pl.program_id(2)

        @pl.when(kk == 0)
        def _():
            acc_ref[...] = part

        @pl.when(kk > 0)
        def _():
            acc_ref[...] += part

        @pl.when(kk == nk - 1)
        def _():
            finish(acc_ref[...], add_ref, o_ref)

    a_spec = pl.BlockSpec((tk, tm), lambda j, i, kk: (kk, i)) if mode == "tn" else pl.BlockSpec((tm, tk), lambda j, i, kk: (i, kk))
    b_spec = pl.BlockSpec((tn, tk), lambda j, i, kk: (j, kk)) if mode == "nt" else pl.BlockSpec((tk, tn), lambda j, i, kk: (kk, j))
    o_spec = pl.BlockSpec((tm, tn), lambda j, i, kk: (i, j))
    in_specs = [a_spec, b_spec] + ([o_spec] if add is not None else [])
    args = (a, b) + ((add,) if add is not None else ())
    return _pcall(
        body, grid=(n // tn, m // tm, nk), in_specs=in_specs, out_specs=o_spec,
        out_shape=jax.ShapeDtypeStruct((m, n), out_dtype),
        scratch_shapes=[pltpu.VMEM((tm, tn), F32)] if nk > 1 else [],
        compiler_params=_params("parallel", "parallel", "arbitrary"), name=name)(*args)


def _prev_halo_spec(tm, width, col_block):
    return pl.BlockSpec((HALO, width), lambda i: (jnp.maximum(i * (tm // HALO) - 1, 0), col_block))


def _fill_with_prev(xp_ref, tile, halo, first):
    xp_ref[0:HALO, :] = jnp.where(first, 0.0, halo)
    xp_ref[HALO:, :] = tile


def _delayed(xp_ref, row0, cols, taps):
    ext = xp_ref[pl.ds(row0, STRIP + HALO), cols]
    return [ext[HALO:, :]] + [pltpu.roll(ext, j, 0)[HALO:, :] for j in range(1, taps)]


def _causal_conv(delayed, w):
    taps = len(delayed)
    out = delayed[0] * w[taps - 1:taps, :]
    for j in range(1, taps):
        out = out + delayed[j] * w[taps - 1 - j:taps - j, :]
    return out


def _advanced_conv(buf_ref, row0, cols, w):
    taps = w.shape[0]
    ext = buf_ref[pl.ds(row0, STRIP + HALO), cols]
    out = ext[:STRIP, :] * w[taps - 1:taps, :]
    for j in range(1, taps):
        out = out + pltpu.roll(ext, STRIP + HALO - j, 0)[:STRIP, :] * w[taps - 1 - j:taps - j, :]
    return out


def _dn_prep(p, conv_w, a_log4, dt_bias4):
    t = p.shape[0]
    tm = _token_tile(t)
    w3 = 3 * DN_WIDTH

    def body(x_ref, halo_ref, pbd_ref, w_ref, alog_ref, dtb_ref, q_ref, k_ref, v_ref, beta_ref, g_ref, xp_ref):
        _fill_with_prev(xp_ref, x_ref[...], halo_ref[...], pl.program_id(0) == 0)

        def strip(row0):
            rows = pl.ds(row0, STRIP)
            for h in range(N_HEADS):
                sl = slice(h * HEAD_DIM, (h + 1) * HEAD_DIM)
                for part, out_ref in ((0, q_ref), (1, k_ref), (2, v_ref)):
                    cols = slice(part * DN_WIDTH + h * HEAD_DIM, part * DN_WIDTH + (h + 1) * HEAD_DIM)
                    y = silu(_causal_conv(_delayed(xp_ref, row0, cols, CONV_K), w_ref[:, cols]))
                    if part == 0:
                        y = l2_fwd(y)[0] * (HEAD_DIM ** -0.5)
                    elif part == 1:
                        y = l2_fwd(y)[0]
                    out_ref[rows, sl] = y
            head = lax.broadcasted_iota(jnp.int32, (STRIP, LANES), 1) < N_HEADS
            pbd = pbd_ref[rows, :]
            beta_ref[rows, :] = jnp.where(head, sigmoid(pbd), 0.0)
            a_raw = pltpu.roll(pbd, LANES - N_HEADS, 1)
            g_ref[rows, :] = jnp.where(head, -jnp.exp(alog_ref[...]) * softplus(a_raw + dtb_ref[...]), 0.0)

        _for_strips(tm, STRIP, strip)

    tok = lambda w, cb: pl.BlockSpec((tm, w), lambda i: (i, cb))
    full = lambda a: pl.BlockSpec(a.shape, lambda i: (0, 0))
    return _pcall(
        body, grid=(t // tm,),
        in_specs=[tok(w3, 0), _prev_halo_spec(tm, w3, 0), tok(LANES, PROJ_MAIN // LANES),
                  full(conv_w), full(a_log4), full(dt_bias4)],
        out_specs=[tok(DN_WIDTH, 0)] * 3 + [tok(LANES, 0)] * 2,
        out_shape=[jax.ShapeDtypeStruct((t, DN_WIDTH), F32)] * 3 + [jax.ShapeDtypeStruct((t, LANES), F32)] * 2,
        scratch_shapes=[pltpu.VMEM((HALO + tm, w3), F32)],
        compiler_params=_params("parallel"), name="dn_prep")(p, p, p, conv_w, a_log4, dt_bias4)


def _dn_prep_bwd(p, conv_w, a_log4, dt_bias4, dq, dk, dv, dbeta4, dg4, dp_buf):
    t = p.shape[0]
    tm = _token_tile(t)
    w3 = 3 * DN_WIDTH

    def body(x_ref, halo_ref, pbd_ref, w_ref, alog_ref, dtb_ref, dq_ref, dk_ref, dv_ref, dbeta_ref, dg_ref, _,
             dc_ref, dw_ref, dpbd_ref, dalog_ref, ddtb_ref, xp_ref, dw_acc, lane_acc):
        first = pl.program_id(0) == 0
        _fill_with_prev(xp_ref, x_ref[...], halo_ref[...], first)
        dw_acc[...] = jnp.zeros_like(dw_acc)
        lane_acc[...] = jnp.zeros_like(lane_acc)

        def strip(row0):
            rows = pl.ds(row0, STRIP)
            for h in range(N_HEADS):
                sl = slice(h * HEAD_DIM, (h + 1) * HEAD_DIM)
                for part, dy_ref in ((0, dq_ref), (1, dk_ref), (2, dv_ref)):
                    cols = slice(part * DN_WIDTH + h * HEAD_DIM, part * DN_WIDTH + (h + 1) * HEAD_DIM)
                    delayed = _delayed(xp_ref, row0, cols, CONV_K)
                    c = _causal_conv(delayed, w_ref[:, cols])
                    dy = dy_ref[rows, sl]
                    if part < 2:
                        y = silu(c)
                        _, r = l2_fwd(y)
                        dy = l2_bwd(y, r, dy * (HEAD_DIM ** -0.5) if part == 0 else dy)
                    dc = dy * silu_grad(c)
                    dc_ref[rows, cols] = dc
                    for j in range(CONV_K):
                        k = CONV_K - 1 - j
                        dw_acc[k * SUBLANES:(k + 1) * SUBLANES, cols] += _fold_rows(dc * delayed[j])
            head = lax.broadcasted_iota(jnp.int32, (STRIP, LANES), 1) < N_HEADS
            pbd = pbd_ref[rows, :]
            beta = sigmoid(pbd)
            dpb = jnp.where(head, dbeta_ref[rows, :] * beta * (1.0 - beta), 0.0)
            z = pltpu.roll(pbd, LANES - N_HEADS, 1) + dtb_ref[...]
            neg_rate = -jnp.exp(alog_ref[...])
            dg = dg_ref[rows, :]
            dpa = jnp.where(head, dg * neg_rate * sigmoid(z), 0.0)
            dpbd_ref[rows, :] = (dpb + pltpu.roll(dpa, N_HEADS, 1)).astype(BF16)
            g = jnp.where(head, neg_rate * softplus(z), 0.0)
            lane_acc[0:SUBLANES, :] += _fold_rows(dg * g)
            lane_acc[SUBLANES:, :] += _fold_rows(dpa)

        _for_strips(tm, STRIP, strip)

        @pl.when(first)
        def _():
            dw_ref[...] = jnp.zeros_like(dw_ref)
            dalog_ref[...] = jnp.zeros_like(dalog_ref)
            ddtb_ref[...] = jnp.zeros_like(ddtb_ref)

        for k in range(CONV_K):
            dw_ref[k:k + 1, :] += jnp.sum(dw_acc[k * SUBLANES:(k + 1) * SUBLANES, :], axis=0, keepdims=True)
        dalog_ref[...] += jnp.sum(lane_acc[0:SUBLANES, :], axis=0, keepdims=True)
        ddtb_ref[...] += jnp.sum(lane_acc[SUBLANES:, :], axis=0, keepdims=True)

    tok = lambda w, cb: pl.BlockSpec((tm, w), lambda i: (i, cb))
    full = lambda shape: pl.BlockSpec(shape, lambda i: (0, 0))
    return _pcall(
        body, grid=(t // tm,),
        in_specs=[tok(w3, 0), _prev_halo_spec(tm, w3, 0), tok(LANES, PROJ_MAIN // LANES),
                  full(conv_w.shape), full(a_log4.shape), full(dt_bias4.shape)] + [tok(DN_WIDTH, 0)] * 3 + [tok(LANES, 0)] * 2
        + [pl.BlockSpec(memory_space=pl.ANY)],
        out_specs=[tok(w3, 0), full((CONV_K, w3)), tok(LANES, PROJ_MAIN // LANES), full((1, LANES)), full((1, LANES))],
        out_shape=[jax.ShapeDtypeStruct((t, w3), F32), jax.ShapeDtypeStruct((CONV_K, w3), F32),
                   jax.ShapeDtypeStruct(dp_buf.shape, dp_buf.dtype),
                   jax.ShapeDtypeStruct((1, LANES), F32), jax.ShapeDtypeStruct((1, LANES), F32)],
        input_output_aliases={11: 2},
        scratch_shapes=[pltpu.VMEM((HALO + tm, w3), F32), pltpu.VMEM((CONV_K * SUBLANES, w3), F32),
                        pltpu.VMEM((2 * SUBLANES, LANES), F32)],
        compiler_params=_params("arbitrary"), name="dn_prep_bwd")(p, p, p, conv_w, a_log4, dt_bias4, dq, dk, dv, dbeta4, dg4, dp_buf)


def _conv_bwd_input(dc, w, name, out_cols=None, col_block=0, into=None):
    t, c = dc.shape
    taps = w.shape[0]
    tm = _token_tile(t)
    ct = _pick(c, 1536)
    n_tok = t // tm
    out_cols = c if out_cols is None else out_cols

    def body(dc_ref, next_ref, w_ref, *rest):
        dx_ref, buf_ref = rest[-2], rest[-1]
        buf_ref[0:tm, :] = dc_ref[...]
        buf_ref[tm:, :] = jnp.where(pl.program_id(0) == n_tok - 1, 0.0, next_ref[...])

        def strip(row0):
            for c0 in range(0, ct, LANES):
                cols = slice(c0, c0 + LANES)
                dx_ref[pl.ds(row0, STRIP), cols] = _advanced_conv(buf_ref, row0, cols, w_ref[:, cols]).astype(BF16)

        _for_strips(tm, STRIP, strip)

    in_specs = [pl.BlockSpec((tm, ct), lambda i, j: (i, j)),
                pl.BlockSpec((HALO, ct), lambda i, j: (jnp.minimum((i + 1) * (tm // HALO), t // HALO - 1), j)),
                pl.BlockSpec((taps, ct), lambda i, j: (0, j))]
    args = (dc, dc, w)
    aliases = {}
    if into is not None:
        in_specs.append(pl.BlockSpec(memory_space=pl.ANY))
        args += (into,)
        aliases = {3: 0}
    return _pcall(
        body, grid=(n_tok, c // ct), in_specs=in_specs,
        out_specs=pl.BlockSpec((tm, ct), lambda i, j: (i, j + col_block)),
        out_shape=jax.ShapeDtypeStruct((t, out_cols), BF16), input_output_aliases=aliases,
        scratch_shapes=[pltpu.VMEM((tm + HALO, ct), F32)],
        compiler_params=_params("parallel", "parallel"), name=name)(*args)


def _dn_forward(q, k, v, beta4, g4, p, norm_g):
    t = q.shape[0]
    n = t // CHUNK
    rows_per_step = DN_CHUNKS * CHUNK

    def body(q_ref, k_ref, v_ref, b_ref, g_ref, gate_ref, ng_ref, mix_ref, s_all_ref, s_ref):
        @pl.when(pl.program_id(0) == 0)
        def _():
            s_ref[...] = jnp.zeros_like(s_ref)

        locs = []
        for c in range(DN_CHUNKS):
            rows = slice(c * CHUNK, (c + 1) * CHUNK)
            locs.append(dn_chunk_local(_stack_heads(q_ref[rows, :]), _stack_heads(k_ref[rows, :]), _stack_heads(v_ref[rows, :]),
                                       _stack_lanes(b_ref[rows, :]), chunk_cumsum(g_ref[rows, :])))
        s = [s_ref[h] for h in range(N_HEADS)]
        for c in range(DN_CHUNKS):
            rows = slice(c * CHUNK, (c + 1) * CHUNK)
            for h in range(N_HEADS):
                s_all_ref[c, h] = s[h]
            o, s = dn_chunk_state(locs[c], s)
            o_n, _ = rms_fwd(o, ng_ref[...])
            for h in range(N_HEADS):
                sl = slice(h * HEAD_DIM, (h + 1) * HEAD_DIM)
                mix_ref[rows, sl] = (o_n[_head_rows(h)] * silu(gate_ref[rows, sl])).astype(BF16)
        for h in range(N_HEADS):
            s_ref[h] = s[h]

    ch = lambda w, cb: pl.BlockSpec((rows_per_step, w), lambda i: (i, cb))
    return _pcall(
        body, grid=(n // DN_CHUNKS,),
        in_specs=[ch(DN_WIDTH, 0)] * 3 + [ch(LANES, 0)] * 2 + [ch(DN_WIDTH, 3), pl.BlockSpec((1, HEAD_DIM), lambda i: (0, 0))],
        out_specs=[ch(DN_WIDTH, 0), pl.BlockSpec((DN_CHUNKS, N_HEADS, HEAD_DIM, HEAD_DIM), lambda i: (i, 0, 0, 0))],
        out_shape=[jax.ShapeDtypeStruct((t, DN_WIDTH + SG_WIDTH), BF16), jax.ShapeDtypeStruct((n, N_HEADS, HEAD_DIM, HEAD_DIM), F32)],
        scratch_shapes=[pltpu.VMEM((N_HEADS, HEAD_DIM, HEAD_DIM), F32)],
        compiler_params=_params("arbitrary"), name="dn_forward")(q, k, v, beta4, g4, p, norm_g)


def _dn_backward(q, k, v, beta4, g4, p, norm_g, s_all, dmix, dp_buf):
    t = q.shape[0]
    n = t // CHUNK
    steps = n // DN_CHUNKS
    rows_per_step = DN_CHUNKS * CHUNK

    def body(q_ref, k_ref, v_ref, b_ref, g_ref, gate_ref, ng_ref, s_in_ref, dmix_ref, _,
             dq_ref, dk_ref, dv_ref, db_ref, dg_ref, dgate_ref, dng_ref, ds_ref):
        @pl.when(pl.program_id(0) == 0)
        def _():
            ds_ref[...] = jnp.zeros_like(ds_ref)
            dng_ref[...] = jnp.zeros_like(dng_ref)

        fwd = []
        for c in range(DN_CHUNKS):
            rows = slice(c * CHUNK, (c + 1) * CHUNK)
            q, k, v = _stack_heads(q_ref[rows, :]), _stack_heads(k_ref[rows, :]), _stack_heads(v_ref[rows, :])
            beta = _stack_lanes(b_ref[rows, :])
            loc = dn_chunk_local(q, k, v, beta, chunk_cumsum(g_ref[rows, :]))
            s = [s_in_ref[c, h] for h in range(N_HEADS)]
            o, _ = dn_chunk_state(loc, s)
            fwd.append((q, k, v, beta, loc, s, o))
        ds = [ds_ref[h] for h in range(N_HEADS)]
        lane = lax.broadcasted_iota(jnp.int32, (CHUNK, LANES), 1)
        _, strict = _tri_masks(CHUNK)
        for c in reversed(range(DN_CHUNKS)):
            rows = slice(c * CHUNK, (c + 1) * CHUNK)
            q, k, v, beta, loc, s, o = fwd[c]
            o_n, r = rms_fwd(o, ng_ref[...])
            gate = _stack_heads(gate_ref[rows, :])
            dmx = _stack_heads(dmix_ref[rows, :])
            dgate = dmx * o_n * silu_grad(gate)
            do, dng_rows = rms_bwd(o, r, ng_ref[...], dmx * silu(gate))
            dng_ref[...] += jnp.sum(dng_rows, axis=0, keepdims=True)
            dq, dk, dv, dbeta, dgc, ds = dn_chunk_bwd(loc, q, k, v, beta, s, do, ds)
            db4 = jnp.zeros((CHUNK, LANES), F32)
            dgc4 = jnp.zeros((CHUNK, LANES), F32)
            for h in range(N_HEADS):
                sl = slice(h * HEAD_DIM, (h + 1) * HEAD_DIM)
                head_rows = _head_rows(h)
                dgate_ref[rows, sl] = dgate[head_rows].astype(BF16)
                dq_ref[rows, sl] = dq[head_rows]
                dk_ref[rows, sl] = dk[head_rows]
                dv_ref[rows, sl] = dv[head_rows]
                db4 = jnp.where(lane == h, dbeta[head_rows], db4)
                dgc4 = jnp.where(lane == h, dgc[head_rows], dgc4)
            db_ref[rows, :] = db4
            dg_ref[rows, :] = dot_nn(jnp.logical_not(strict).astype(F32), dgc4)
        for h in range(N_HEADS):
            ds_ref[h] = ds[h]

    rev = lambda w, cb: pl.BlockSpec((rows_per_step, w), lambda i: (steps - 1 - i, cb))
    return _pcall(
        body, grid=(steps,),
        in_specs=[rev(DN_WIDTH, 0)] * 3 + [rev(LANES, 0)] * 2 + [rev(DN_WIDTH, 3), pl.BlockSpec((1, HEAD_DIM), lambda i: (0, 0)),
                  pl.BlockSpec((DN_CHUNKS, N_HEADS, HEAD_DIM, HEAD_DIM), lambda i: (steps - 1 - i, 0, 0, 0)), rev(DN_WIDTH, 0),
                  pl.BlockSpec(memory_space=pl.ANY)],
        out_specs=[rev(DN_WIDTH, 0)] * 3 + [rev(LANES, 0)] * 2 + [rev(DN_WIDTH, 3), pl.BlockSpec((1, HEAD_DIM), lambda i: (0, 0))],
        out_shape=[jax.ShapeDtypeStruct((t, DN_WIDTH), F32)] * 3 + [jax.ShapeDtypeStruct((t, LANES), F32)] * 2
        + [jax.ShapeDtypeStruct(dp_buf.shape, dp_buf.dtype), jax.ShapeDtypeStruct((1, HEAD_DIM), F32)],
        input_output_aliases={9: 5},
        scratch_shapes=[pltpu.VMEM((N_HEADS, HEAD_DIM, HEAD_DIM), F32)],
        compiler_params=_params("arbitrary"), name="dn_backward")(q, k, v, beta4, g4, p, norm_g, s_all, dmix, dp_buf)


def _sg_mask():
    row = lax.broadcasted_iota(jnp.int32, (SG_BLOCK, SG_BLOCK), 0)
    col = lax.broadcasted_iota(jnp.int32, (SG_BLOCK, SG_BLOCK), 1)
    return (col // CHUNK) <= (row // CHUNK)


def _sg_forward(p, norm_g, w_s, b_t, mix_buf):
    t = p.shape[0]

    def body(u_ref, v_ref, ng_ref, w_ref, b_ref, _, o_ref):
        mask = _sg_mask()
        for g in range(SG_GROUPS):
            sl = slice(g * SG_DIM, (g + 1) * SG_DIM)
            vn, _ = rms_fwd(gelu(v_ref[:, sl]), ng_ref[:, sl])
            s = dot_nn(jnp.where(mask, w_ref[g], 0.0), vn, FAST) + b_ref[:, g:g + 1]
            o_ref[:, sl] = (gelu(u_ref[:, sl]) * s).astype(BF16)

    blk = lambda cb: pl.BlockSpec((SG_BLOCK, SG_WIDTH), lambda i: (i, cb))
    return _pcall(
        body, grid=(t // SG_BLOCK,),
        in_specs=[blk(4), blk(5), pl.BlockSpec((1, SG_WIDTH), lambda i: (0, 0)),
                  pl.BlockSpec((SG_GROUPS, SG_BLOCK, SG_BLOCK), lambda i: (0, 0, 0)), pl.BlockSpec((SG_BLOCK, SG_GROUPS), lambda i: (0, 0)),
                  pl.BlockSpec(memory_space=pl.ANY)],
        out_specs=blk(1), out_shape=jax.ShapeDtypeStruct(mix_buf.shape, mix_buf.dtype), input_output_aliases={5: 0},
        compiler_params=_params("parallel"), name="sg_forward")(p, p, norm_g, w_s, b_t, mix_buf)


def _sg_backward(p, norm_g, w_s, b_t, dmix):
    t = p.shape[0]

    def body(u_ref, v_ref, ng_ref, w_ref, b_ref, do_ref, duv_ref, dng_ref, dw_ref, db_ref):
        @pl.when(pl.program_id(0) == 0)
        def _():
            dng_ref[...] = jnp.zeros_like(dng_ref)
            dw_ref[...] = jnp.zeros_like(dw_ref)
            db_ref[...] = jnp.zeros_like(db_ref)

        mask = _sg_mask()
        lane = lax.broadcasted_iota(jnp.int32, (SG_BLOCK, LANES), 1)
        db = jnp.zeros((SG_BLOCK, LANES), F32)
        for g in range(SG_GROUPS):
            sl = slice(g * SG_DIM, (g + 1) * SG_DIM)
            u_raw, v_raw, do = u_ref[:, sl], v_ref[:, sl], do_ref[:, sl]
            vg = gelu(v_raw)
            vn, r = rms_fwd(vg, ng_ref[:, sl])
            w_m = jnp.where(mask, w_ref[g], 0.0)
            s = dot_nn(w_m, vn, FAST) + b_ref[:, g:g + 1]
            duv_ref[:, sl] = (do * s * gelu_grad(u_raw)).astype(BF16)
            ds = do * gelu(u_raw)
            db = jnp.where(lane == g, jnp.sum(ds, axis=1, keepdims=True), db)
            dw_ref[g] += jnp.where(mask, dot_nt(ds, vn, FAST), 0.0)
            dvg, dng_rows = rms_bwd(vg, r, ng_ref[:, sl], dot_tn(w_m, ds, FAST))
            dng_ref[:, sl] += jnp.sum(dng_rows, axis=0, keepdims=True)
            duv_ref[:, SG_WIDTH + g * SG_DIM:SG_WIDTH + (g + 1) * SG_DIM] = (dvg * gelu_grad(v_raw)).astype(BF16)
        db_ref[...] += db

    blk = lambda cb: pl.BlockSpec((SG_BLOCK, SG_WIDTH), lambda i: (i, cb))
    const2 = lambda shape: pl.BlockSpec(shape, lambda i: (0, 0))
    w_spec = pl.BlockSpec((SG_GROUPS, SG_BLOCK, SG_BLOCK), lambda i: (0, 0, 0))
    return _pcall(
        body, grid=(t // SG_BLOCK,),
        in_specs=[blk(4), blk(5), const2((1, SG_WIDTH)), w_spec, const2((SG_BLOCK, SG_GROUPS)), blk(1)],
        out_specs=[pl.BlockSpec((SG_BLOCK, 2 * SG_WIDTH), lambda i: (i, 2)), const2((1, SG_WIDTH)), w_spec,
                   const2((SG_BLOCK, LANES))],
        out_shape=[jax.ShapeDtypeStruct((t, PROJ_PAD), BF16), jax.ShapeDtypeStruct((1, SG_WIDTH), F32),
                   jax.ShapeDtypeStruct((SG_GROUPS, SG_BLOCK, SG_BLOCK), F32), jax.ShapeDtypeStruct((SG_BLOCK, LANES), F32)],
        compiler_params=_params("arbitrary"), name="sg_backward")(p, p, norm_g, w_s, b_t, dmix)


FFN_CT = D_FF // 2


def _ffn_act(up, conv_w, conv_b):
    t = up.shape[0]
    tm = _token_tile(t)
    nj = D_FF // FFN_CT

    def body(ug_ref, uv_ref, hg_ref, hv_ref, wg_ref, wv_ref, bg_ref, bv_ref, act_ref, xg_ref, xv_ref):
        first = pl.program_id(0) == 0
        _fill_with_prev(xg_ref, ug_ref[...], hg_ref[...], first)
        _fill_with_prev(xv_ref, uv_ref[...], hv_ref[...], first)

        def strip(row0):
            for c0 in range(0, FFN_CT, LANES):
                cols = slice(c0, c0 + LANES)
                cg = _causal_conv(_delayed(xg_ref, row0, cols, FFN_CONV), wg_ref[:, cols]) + bg_ref[:, cols]
                cv = _causal_conv(_delayed(xv_ref, row0, cols, FFN_CONV), wv_ref[:, cols]) + bv_ref[:, cols]
                act_ref[pl.ds(row0, STRIP), cols] = (silu(cg) * cv).astype(BF16)

        _for_strips(tm, STRIP, strip)

    tok = lambda off: pl.BlockSpec((tm, FFN_CT), lambda i, j: (i, j + off))
    halo = lambda off: pl.BlockSpec((HALO, FFN_CT), lambda i, j: (jnp.maximum(i * (tm // HALO) - 1, 0), j + off))
    par = lambda rows, off: pl.BlockSpec((rows, FFN_CT), lambda i, j: (0, j + off))
    return _pcall(
        body, grid=(t // tm, nj),
        in_specs=[tok(0), tok(nj), halo(0), halo(nj), par(FFN_CONV, 0), par(FFN_CONV, nj), par(1, 0), par(1, nj)],
        out_specs=pl.BlockSpec((tm, FFN_CT), lambda i, j: (i, j)),
        out_shape=jax.ShapeDtypeStruct((t, D_FF), BF16),
        scratch_shapes=[pltpu.VMEM((HALO + tm, FFN_CT), F32)] * 2,
        compiler_params=_params("parallel", "parallel"), name="ffn_act")(up, up, up, up, conv_w, conv_w, conv_b, conv_b)


def _ffn_bwd(up, conv_w, conv_b, dact):
    t = up.shape[0]
    tm = _pick(t, 128)
    n_tok = t // tm
    width = 2 * D_FF

    def dconv(delayed_g, delayed_v, da, wg, wv, bg, bv):
        cg = _causal_conv(delayed_g, wg) + bg
        cv = _causal_conv(delayed_v, wv) + bv
        s = sigmoid(cg)
        return da * cv * (s * (1.0 + cg * (1.0 - s))), da * (cg * s)

    def body(up_ref, prev_ref, next_ref, da_ref, dan_ref, w_ref, b_ref, dup_ref, dw_ref, db_ref, xp_ref, dc_ref, dw_acc, db_acc):
        first = pl.program_id(0) == 0
        last = pl.program_id(0) == n_tok - 1
        xp_ref[0:HALO, :] = jnp.where(first, 0.0, prev_ref[...])
        xp_ref[HALO:HALO + tm, :] = up_ref[...]
        xp_ref[HALO + tm:, :] = next_ref[...]
        dw_acc[...] = jnp.zeros_like(dw_acc)
        db_acc[...] = jnp.zeros_like(db_acc)

        def strip(row0):
            rows = pl.ds(row0, STRIP)
            for c0 in range(0, D_FF, LANES):
                gc, vc = slice(c0, c0 + LANES), slice(D_FF + c0, D_FF + c0 + LANES)
                del_g, del_v = _delayed(xp_ref, row0, gc, FFN_CONV), _delayed(xp_ref, row0, vc, FFN_CONV)
                dcg, dcv = dconv(del_g, del_v, da_ref[rows, gc], w_ref[:, gc], w_ref[:, vc], b_ref[:, gc], b_ref[:, vc])
                dc_ref[rows, gc] = dcg
                dc_ref[rows, vc] = dcv
                db_acc[:, gc] += _fold_rows(dcg)
                db_acc[:, vc] += _fold_rows(dcv)
                for j in range(FFN_CONV):
                    k = FFN_CONV - 1 - j
                    dw_acc[k * SUBLANES:(k + 1) * SUBLANES, gc] += _fold_rows(dcg * del_g[j])
                    dw_acc[k * SUBLANES:(k + 1) * SUBLANES, vc] += _fold_rows(dcv * del_v[j])

        _for_strips(tm, STRIP, strip)

        for c0 in range(0, D_FF, LANES):
            gc, vc = slice(c0, c0 + LANES), slice(D_FF + c0, D_FF + c0 + LANES)

            def delayed(cols):
                ext = xp_ref[tm:tm + 2 * HALO, cols]
                return [ext[HALO:, :]] + [pltpu.roll(ext, j, 0)[HALO:, :] for j in range(1, FFN_CONV)]

            dcg, dcv = dconv(delayed(gc), delayed(vc), dan_ref[:, gc], w_ref[:, gc], w_ref[:, vc], b_ref[:, gc], b_ref[:, vc])
            dc_ref[tm:, gc] = jnp.where(last, 0.0, dcg)
            dc_ref[tm:, vc] = jnp.where(last, 0.0, dcv)

        def strip_dx(row0):
            for c0 in range(0, width, LANES):
                cols = slice(c0, c0 + LANES)
                dup_ref[pl.ds(row0, STRIP), cols] = _advanced_conv(dc_ref, row0, cols, w_ref[:, cols]).astype(BF16)

        _for_strips(tm, STRIP, strip_dx)

        @pl.when(first)
        def _():
            dw_ref[...] = jnp.zeros_like(dw_ref)
            db_ref[...] = jnp.zeros_like(db_ref)

        for k in range(FFN_CONV):
            dw_ref[k:k + 1, :] += jnp.sum(dw_acc[k * SUBLANES:(k + 1) * SUBLANES, :], axis=0, keepdims=True)
        db_ref[...] += jnp.sum(db_acc[...], axis=0, keepdims=True)

    next_rows = lambda i: jnp.minimum((i + 1) * (tm // HALO), t // HALO - 1)
    full = lambda rows: pl.BlockSpec((rows, width), lambda i: (0, 0))
    return _pcall(
        body, grid=(n_tok,),
        in_specs=[pl.BlockSpec((tm, width), lambda i: (i, 0)),
                  pl.BlockSpec((HALO, width), lambda i: (jnp.maximum(i * (tm // HALO) - 1, 0), 0)),
                  pl.BlockSpec((HALO, width), lambda i: (next_rows(i), 0)),
                  pl.BlockSpec((tm, D_FF), lambda i: (i, 0)), pl.BlockSpec((HALO, D_FF), lambda i: (next_rows(i), 0)),
                  full(FFN_CONV), full(1)],
        out_specs=[pl.BlockSpec((tm, width), lambda i: (i, 0)), full(FFN_CONV), full(1)],
        out_shape=[jax.ShapeDtypeStruct((t, width), BF16), jax.ShapeDtypeStruct((FFN_CONV, width), F32),
                   jax.ShapeDtypeStruct((1, width), F32)],
        scratch_shapes=[pltpu.VMEM((tm + 2 * HALO, width), F32), pltpu.VMEM((tm + HALO, width), F32),
                        pltpu.VMEM((FFN_CONV * SUBLANES, width), F32), pltpu.VMEM((SUBLANES, width), F32)],
        compiler_params=_params("arbitrary"), name="ffn_bwd")(up, up, up, dact, dact, conv_w, conv_b)


def _final_loss(x3, target, g):
    t, d = x3.shape
    tm = _token_tile(t)

    def body(x_ref, t_ref, g_ref, loss_ref, dx_ref, dxb_ref, dg_ref):
        @pl.when(pl.program_id(0) == 0)
        def _():
            loss_ref[...] = jnp.zeros_like(loss_ref)
            dg_ref[...] = jnp.zeros_like(dg_ref)

        x = x_ref[...]
        y, r = rms_fwd(x, g_ref[...])
        err = y - t_ref[...]
        per_tok = jnp.mean(err * err, axis=-1, keepdims=True)
        loss_ref[...] += 0.5 * jnp.sum(per_tok, axis=0, keepdims=True)
        dx, dg_rows = rms_bwd(x, r, g_ref[...], err * (1.0 / d))
        dx_ref[...] = dx
        dxb_ref[...] = dx.astype(BF16)
        dg_ref[...] += jnp.sum(dg_rows, axis=0, keepdims=True)

    tile = pl.BlockSpec((tm, d), lambda i: (i, 0))
    row = pl.BlockSpec((1, d), lambda i: (0, 0))
    return _pcall(
        body, grid=(t // tm,), in_specs=[tile, tile, row],
        out_specs=[pl.BlockSpec((1, LANES), lambda i: (0, 0)), tile, tile, row],
        out_shape=[jax.ShapeDtypeStruct((1, LANES), F32), jax.ShapeDtypeStruct((t, d), F32), jax.ShapeDtypeStruct((t, d), BF16),
                   jax.ShapeDtypeStruct((1, d), F32)],
        compiler_params=_params("arbitrary"), name="final_loss")(x3, target, g)


def _my_position():
    return lax.axis_index("x"), lax.axis_index("y"), lax.axis_index("c")


COPIES = N_DEV - 1


def _all_gather(arrays):
    n = len(arrays)

    def body(*refs):
        x_refs, out_refs = refs[:n], refs[n:2 * n]
        send_sems, recv_sems, local_sems = refs[2 * n:]
        x, y, cc = _my_position()
        me, sibling = (x, y, cc), (x, y, 1 - cc)
        chips = [(1 - x, y), (x, 1 - y), (1 - x, 1 - y)]

        def block(a, px, py, pc):
            return out_refs[a].at[4 * px + 2 * py + pc]

        def copy(a, k, blk, to, src=None):
            return pltpu.make_async_remote_copy(
                src_ref=block(a, *blk) if src is None else src, dst_ref=block(a, *blk),
                send_sem=send_sems.at[a * COPIES + k], recv_sem=recv_sems.at[a * COPIES + k],
                device_id=to, device_id_type=MESH_ID)

        mine = [pltpu.make_async_copy(x_refs[a], block(a, *me), local_sems.at[a]) for a in range(n)]
        for cp in mine:
            cp.start()
        first = []
        for a in range(n):
            first.append(copy(a, 0, me, sibling, src=x_refs[a]))
            first += [copy(a, 1 + j, me, (*chip, cc), src=x_refs[a]) for j, chip in enumerate(chips)]
        for cp in first:
            cp.start()
        passed = []
        for j, chip in enumerate(chips):
            for a in range(n):
                copy(a, 1 + j, (*chip, cc), me).wait_recv()
                passed.append(copy(a, 4 + j, (*chip, cc), sibling))
                passed[-1].start()
        for a in range(n):
            copy(a, 0, sibling, me).wait_recv()
        for j, chip in enumerate(chips):
            for a in range(n):
                copy(a, 4 + j, (*chip, 1 - cc), me).wait_recv()
        for cp in first + passed:
            cp.wait_send()
        for cp in mine:
            cp.wait()

    any_spec = pl.BlockSpec(memory_space=pl.ANY)
    return _pcall(
        body, out_shape=[jax.ShapeDtypeStruct((N_DEV,) + a.shape, a.dtype) for a in arrays],
        in_specs=[any_spec] * n, out_specs=[any_spec] * n,
        scratch_shapes=[pltpu.SemaphoreType.DMA((n * COPIES,)), pltpu.SemaphoreType.DMA((n * COPIES,)),
                        pltpu.SemaphoreType.DMA((n,))],
        name="all_gather")(*arrays)


def _all_to_all(sends):
    n = len(sends)

    def body(*refs):
        send_refs, recv_refs = refs[:n], refs[n:2 * n]
        send_sems, recv_sems, local_sems = refs[2 * n:]
        x, y, cc = _my_position()
        me = 4 * x + 2 * y + cc
        mine = [pltpu.make_async_copy(send_refs[a].at[me], recv_refs[a].at[me], local_sems.at[a]) for a in range(n)]
        for cp in mine:
            cp.start()
        copies = []
        for rel in range(1, N_DEV):
            px, py, pc = x ^ (rel >> 2), y ^ ((rel >> 1) & 1), cc ^ (rel & 1)
            for a in range(n):
                copies.append(pltpu.make_async_remote_copy(
                    src_ref=send_refs[a].at[4 * px + 2 * py + pc], dst_ref=recv_refs[a].at[me],
                    send_sem=send_sems.at[a * COPIES + rel - 1], recv_sem=recv_sems.at[a * COPIES + rel - 1],
                    device_id=(px, py, pc), device_id_type=MESH_ID))
        for cp in copies:
            cp.start()
        for cp in copies:
            cp.wait()
        for cp in mine:
            cp.wait()

    any_spec = pl.BlockSpec(memory_space=pl.ANY)
    return _pcall(
        body, out_shape=[jax.ShapeDtypeStruct(s.shape, s.dtype) for s in sends],
        in_specs=[any_spec] * n, out_specs=[any_spec] * n,
        scratch_shapes=[pltpu.SemaphoreType.DMA((n * COPIES,)), pltpu.SemaphoreType.DMA((n * COPIES,)),
                        pltpu.SemaphoreType.DMA((n,))],
        name="all_to_all")(*sends)


def _hbm(a):
    return pltpu.with_memory_space_constraint(a, pltpu.HBM)


def _split_copies(send_refs, land_refs, send_sems, recv_sems, local_sems, gather):
    x, y, cc = _my_position()
    me = 4 * x + 2 * y + cc
    local, remote = [], []
    for a, (send, land) in enumerate(zip(send_refs, land_refs)):
        local.append(pltpu.make_async_copy(send if gather else send.at[me], land.at[me], local_sems.at[a]))
    for a, (send, land) in enumerate(zip(send_refs, land_refs)):
        for rel in range(1, N_DEV):
            px, py, pc = x ^ (rel >> 2), y ^ ((rel >> 1) & 1), cc ^ (rel & 1)
            remote.append(pltpu.make_async_remote_copy(
                src_ref=send if gather else send.at[4 * px + 2 * py + pc], dst_ref=land.at[me],
                send_sem=send_sems.at[a * COPIES + rel - 1], recv_sem=recv_sems.at[a * COPIES + rel - 1],
                device_id=(px, py, pc), device_id_type=MESH_ID))
    return local, remote


SPLIT_EFFECT = pltpu.SideEffectType.DATAFLOW_SIDE_EFFECTING


def _exchange_start(sends, after, gather, name):
    n = len(sends)
    lands = [_hbm(lax.empty((N_DEV,) + s.shape if gather else s.shape, s.dtype)) for s in sends]

    def body(*refs):
        send_refs, land_refs = refs[:n], refs[n:2 * n]
        send_sems, recv_sems, local_sems = refs[2 * n + 1:2 * n + 4]
        token = refs[-1]
        local, remote = _split_copies(send_refs, land_refs, send_sems, recv_sems, local_sems, gather)
        for cp in local + remote:
            cp.start()
        token[...] = jnp.zeros_like(token)

    hbm, sem = pl.BlockSpec(memory_space=pltpu.HBM), pl.BlockSpec(memory_space=pltpu.SEMAPHORE)
    out = _pcall(
        body, name=name,
        out_shape=[pltpu.SemaphoreType.DMA((n * COPIES,)), pltpu.SemaphoreType.DMA((n * COPIES,)), pltpu.SemaphoreType.DMA((n,))]
        + [pltpu.HBM(s.shape, s.dtype) for s in sends] + [pltpu.HBM(z.shape, z.dtype) for z in lands]
        + [jax.ShapeDtypeStruct((SUBLANES, LANES), F32)],
        in_specs=[hbm] * (2 * n) + [pl.BlockSpec(memory_space=pl.ANY)],
        out_specs=[sem] * 3 + [hbm] * (2 * n) + [pl.BlockSpec(memory_space=pltpu.VMEM)],
        input_output_aliases={i: 3 + i for i in range(2 * n)},
        compiler_params=pltpu.CompilerParams(has_side_effects=SPLIT_EFFECT),
    )(*[_hbm(s) for s in sends], *lands, after)
    return dict(sems=out[:3], sends=out[3:3 + n], lands=out[3 + n:3 + 2 * n], gather=gather), out[-1]


def _exchange_wait(handle, after, name):
    sends, lands, gather = handle["sends"], handle["lands"], handle["gather"]
    n = len(sends)

    def body(*refs):
        send_refs, land_refs = refs[:n], refs[n:2 * n]
        send_sems, recv_sems, local_sems = refs[2 * n:2 * n + 3]
        local, remote = _split_copies(send_refs, land_refs, send_sems, recv_sems, local_sems, gather)
        for cp in remote:
            cp.wait_send()
            cp.wait_recv()
        for cp in local:
            cp.wait()

    hbm, sem = pl.BlockSpec(memory_space=pltpu.HBM), pl.BlockSpec(memory_space=pltpu.SEMAPHORE)
    out = _pcall(
        body, name=name,
        out_shape=[pltpu.HBM(s.shape, s.dtype) for s in sends] + [pltpu.HBM(z.shape, z.dtype) for z in lands],
        in_specs=[hbm] * (2 * n) + [sem] * 3 + [pl.BlockSpec(memory_space=pl.ANY)],
        out_specs=[hbm] * (2 * n), input_output_aliases={i: i for i in range(2 * n)},
        compiler_params=pltpu.CompilerParams(has_side_effects=SPLIT_EFFECT),
    )(*sends, *lands, *handle["sems"], after)
    return out[n:]


def _join_shards(g, n_local, out_cols, name):
    _, r, wp = g.shape
    tr = min(r, 256)

    def body(g_ref, o_ref, acc_ref):
        acc_ref[...] = jnp.zeros_like(acc_ref)
        for k in range(N_DEV):
            shift = (n_local * k) % LANES
            start = n_local * k - shift
            piece = g_ref[k].astype(F32)
            if shift:
                piece = pltpu.roll(piece, shift, 1)
            acc_ref[:, start:start + wp] += piece
        o_ref[...] = acc_ref[...].astype(BF16)

    return _pcall(
        body, grid=(r // tr,), in_specs=[pl.BlockSpec((N_DEV, tr, wp), lambda i: (0, i, 0))],
        out_specs=pl.BlockSpec((tr, out_cols), lambda i: (i, 0)), out_shape=jax.ShapeDtypeStruct((r, out_cols), BF16),
        scratch_shapes=[pltpu.VMEM((tr, out_cols), F32)], compiler_params=_params("parallel"), name=name)(g)


def _split_shards(full, n_local, wp, name):
    r, c = full.shape
    tr = min(r, 256)

    def body(x_ref, o_ref):
        lane = lax.broadcasted_iota(jnp.int32, (tr, wp), 1)
        for k in range(N_DEV):
            shift = (n_local * k) % LANES
            start = n_local * k - shift
            win = x_ref[:, start:start + wp]
            if shift:
                win = pltpu.roll(win, wp - shift, 1)
            o_ref[k] = jnp.where(lane < n_local, win, 0.0).astype(BF16)

    return _pcall(
        body, grid=(r // tr,), in_specs=[pl.BlockSpec((tr, c), lambda i: (i, 0))],
        out_specs=pl.BlockSpec((N_DEV, tr, wp), lambda i: (0, i, 0)), out_shape=jax.ShapeDtypeStruct((N_DEV, r, wp), BF16),
        compiler_params=_params("parallel"), name=name)(full)


def _sum_and_adamw(recv, w, m, v, name):
    _, r, wp = recv.shape
    c = w.shape[-1]
    lead = w.ndim == 3
    tr = SLAB_ROW_TILE if r % SLAB_ROW_TILE == 0 else (SLAB_ROW_TILE // 4 if r % (SLAB_ROW_TILE // 4) == 0 else r)
    bc1 = 1.0 - ADAM_B1 ** ADAM_STEP
    bc2 = 1.0 - ADAM_B2 ** ADAM_STEP

    def body(recv_ref, w_ref, m_ref, v_ref, g_ref, d_ref, nm_ref, nv_ref):
        g = recv_ref[0, :, 0:c].astype(F32)
        for s in range(1, N_DEV):
            g = g + recv_ref[s, :, 0:c].astype(F32)
        m_new = ADAM_B1 * m_ref[...] + (1.0 - ADAM_B1) * g
        v_new = ADAM_B2 * v_ref[...] + (1.0 - ADAM_B2) * (g * g)
        m_hat = m_new / bc1
        v_hat = v_new / bc2
        g_ref[...] = g
        d_ref[...] = -ADAM_LR * (m_hat / (jnp.sqrt(v_hat) + ADAM_EPS) + ADAM_WD * w_ref[...])
        nm_ref[...] = m_new
        nv_ref[...] = v_new

    tile = pl.BlockSpec((None, tr, c), lambda i: (0, i, 0)) if lead else pl.BlockSpec((tr, c), lambda i: (i, 0))
    return _pcall(
        body, grid=(r // tr,),
        in_specs=[pl.BlockSpec((N_DEV, tr, wp), lambda i: (0, i, 0)), tile, tile, tile],
        out_specs=[tile] * 4, out_shape=[jax.ShapeDtypeStruct(w.shape, F32)] * 4,
        compiler_params=_params("parallel"), name=name)(recv, w, m, v)


SHARDED_TAPS = ("dn_conv_w", "ffn_conv_w")
REPLICATED = ("attn_norm_g", "dn_a_log", "dn_dt_bias", "dn_out_norm_g", "sg_norm_g", "sg_w", "sg_b", "ffn_norm_g",
              "ffn_conv_b", "final_norm_g")
SMALL = SHARDED_TAPS + REPLICATED
WEIGHT_ORDER = ("attn_norm_g", "w_in", "dn_conv_w", "dn_a_log", "dn_dt_bias", "dn_out_norm_g", "sg_norm_g", "sg_w", "sg_b",
                "w_out", "ffn_norm_g", "w_up", "ffn_conv_w", "ffn_conv_b", "w_down", "final_norm_g")
SLAB_COLS = 1024
SLAB_ROW_TILE = 128


def _pad_to(flat, multiple):
    pad = (-flat.shape[-1]) % multiple
    if pad == 0:
        return flat
    return jnp.pad(flat, [(0, 0)] * (flat.ndim - 1) + [(0, pad)])


def _lane_padded(n):
    return -(-n // LANES) * LANES


def _pack_small(named):
    flat = jnp.concatenate([named[n].reshape(-1) for n in SMALL])
    return _pad_to(flat, SUBLANES * SLAB_COLS).reshape(-1, SLAB_COLS)


def _unpack_small(slab, like):
    flat = slab.reshape(-1)
    out, off = {}, 0
    for n in SMALL:
        size = like[n].size
        out[n] = flat[off:off + size].reshape(like[n].shape)
        off += size
    return out


def _split_columns(full, n_local):
    r = full.shape[0]
    return full.reshape(r, N_DEV, n_local).transpose(1, 0, 2).reshape(N_DEV, r * n_local)


def _join_columns(blocks, r, n_local):
    return blocks.reshape(N_DEV, r, n_local).transpose(1, 0, 2).reshape(r, N_DEV * n_local)


def _lanes4(a):
    return jnp.pad(a.reshape(1, N_HEADS), ((0, 0), (0, LANES - N_HEADS)))


def kernel(x, attn_norm_g, w_in, dn_conv_w, dn_a_log, dn_dt_bias, dn_out_norm_g, sg_norm_g, sg_w, sg_b, w_out, ffn_norm_g, w_up, ffn_conv_w, ffn_conv_b, w_down, final_norm_g, loss_target, m_attn_norm_g, m_w_in, m_dn_conv_w, m_dn_a_log, m_dn_dt_bias, m_dn_out_norm_g, m_sg_norm_g, m_sg_w, m_sg_b, m_w_out, m_ffn_norm_g, m_w_up, m_ffn_conv_w, m_ffn_conv_b, m_w_down, m_final_norm_g, v_attn_norm_g, v_w_in, v_dn_conv_w, v_dn_a_log, v_dn_dt_bias, v_dn_out_norm_g, v_sg_norm_g, v_sg_w, v_sg_b, v_w_out, v_ffn_norm_g, v_w_up, v_ffn_conv_w, v_ffn_conv_b, v_w_down, v_final_norm_g):
    weights = dict(attn_norm_g=attn_norm_g, w_in=w_in, dn_conv_w=dn_conv_w, dn_a_log=dn_a_log, dn_dt_bias=dn_dt_bias,
                   dn_out_norm_g=dn_out_norm_g, sg_norm_g=sg_norm_g, sg_w=sg_w, sg_b=sg_b, w_out=w_out, ffn_norm_g=ffn_norm_g,
                   w_up=w_up, ffn_conv_w=ffn_conv_w, ffn_conv_b=ffn_conv_b, w_down=w_down, final_norm_g=final_norm_g)
    m_in = dict(attn_norm_g=m_attn_norm_g, w_in=m_w_in, dn_conv_w=m_dn_conv_w, dn_a_log=m_dn_a_log, dn_dt_bias=m_dn_dt_bias,
                dn_out_norm_g=m_dn_out_norm_g, sg_norm_g=m_sg_norm_g, sg_w=m_sg_w, sg_b=m_sg_b, w_out=m_w_out,
                ffn_norm_g=m_ffn_norm_g, w_up=m_w_up, ffn_conv_w=m_ffn_conv_w, ffn_conv_b=m_ffn_conv_b, w_down=m_w_down,
                final_norm_g=m_final_norm_g)
    v_in = dict(attn_norm_g=v_attn_norm_g, w_in=v_w_in, dn_conv_w=v_dn_conv_w, dn_a_log=v_dn_a_log, dn_dt_bias=v_dn_dt_bias,
                dn_out_norm_g=v_dn_out_norm_g, sg_norm_g=v_sg_norm_g, sg_w=v_sg_w, sg_b=v_sg_b, w_out=v_w_out,
                ffn_norm_g=v_ffn_norm_g, w_up=v_w_up, ffn_conv_w=v_ffn_conv_w, ffn_conv_b=v_ffn_conv_b, w_down=v_w_down,
                final_norm_g=v_final_norm_g)

    n_in, n_up = w_in.shape[2], w_up.shape[2]
    r_out, r_down = w_out.shape[1], w_down.shape[1]
    n_dnc, n_ffc = dn_conv_w.shape[2], ffn_conv_w.shape[2]
    transposed = lambda a: jnp.transpose(a, (0, 2, 1))
    taps = _pad_to(jnp.concatenate([dn_conv_w.reshape(-1), ffn_conv_w.reshape(-1)]), SUBLANES * LANES).reshape(-1, LANES)
    g_in, g_taps = _all_gather([transposed(w_in)[0].astype(BF16), taps])
    late_weights, token = _exchange_start(
        [w_out[0].astype(BF16), transposed(w_up)[0].astype(BF16), w_down[0].astype(BF16)], g_taps, True, "gather_late_start")
    w_in_t = jnp.pad(g_in.reshape(N_DEV * n_in, D_MODEL), ((0, PROJ_PAD - N_DEV * n_in), (0, 0)))
    taps_all = g_taps.reshape(N_DEV, -1)
    dn_conv_full = _join_columns(taps_all[:, :CONV_K * n_dnc], CONV_K, n_dnc)
    ffn_conv_full = _join_columns(taps_all[:, CONV_K * n_dnc:CONV_K * n_dnc + FFN_CONV * n_ffc], FFN_CONV, n_ffc)

    def late(after):
        g_out, g_up, g_down = _exchange_wait(late_weights, after, "gather_late_wait")
        return (g_up.reshape(N_DEV * n_up, D_MODEL),
                g_out.reshape(N_DEV * r_out, D_MODEL), g_down.reshape(N_DEV * r_down, D_MODEL))

    def send_early(blocks, after, name):
        return _exchange_start(blocks, after, False, name)

    loss_lanes, grad_x, g, early = _local_step(
        x[0], loss_target[0], w_in_t, late, send_early, dn_conv_full, ffn_conv_full, attn_norm_g + token[0:1, 0:1],
        dn_a_log, dn_dt_bias, dn_out_norm_g, sg_norm_g, sg_w, sg_b, ffn_norm_g, ffn_conv_b, final_norm_g, n_in)

    small = jnp.concatenate([g[n].reshape(-1) for n in REPLICATED])
    small_send = jnp.concatenate([_split_columns(g["dn_conv_w"], n_dnc), _split_columns(g["ffn_conv_w"], n_ffc),
                                  jnp.broadcast_to(small[None, :], (N_DEV, small.shape[0]))], axis=1)
    small_send = _pad_to(small_send, SUBLANES * SLAB_COLS).reshape(N_DEV, -1, SLAB_COLS)
    r_small, = _all_to_all([small_send])
    r_dn, = _exchange_wait(early[0], r_small, "send_dw_down_wait")
    r_up, r_o = _exchange_wait(early[1], r_small, "send_dw_up_out_wait")
    r_in, = _exchange_wait(early[2], r_small, "send_dw_in_wait")

    upd = {
        "w_in": [transposed(o) for o in _sum_and_adamw(r_in, transposed(w_in), transposed(m_w_in), transposed(v_w_in),
                                                       "adamw_w_in")],
        "w_up": [transposed(o) for o in _sum_and_adamw(r_up, transposed(w_up), transposed(m_w_up), transposed(v_w_up),
                                                       "adamw_w_up")],
        "w_out": _sum_and_adamw(r_o, w_out, m_w_out, v_w_out, "adamw_w_out"),
        "w_down": _sum_and_adamw(r_dn, w_down, m_w_down, v_w_down, "adamw_w_down"),
    }
    small_upd = _sum_and_adamw(r_small, _pack_small(weights), _pack_small(m_in), _pack_small(v_in), "adamw_small")
    results = []
    for i in range(4):
        named = _unpack_small(small_upd[i], weights)
        named.update({n: upd[n][i] for n in upd})
        results.append(named)

    loss = lax.psum(loss_lanes[0, 0], MESH_AXES)
    return (loss, grad_x[None], *[r[n] for r in results for n in WEIGHT_ORDER])


def _local_step(x2d, tgt, w_in_t, late_weights, send_early, dn_conv_full, ffn_conv_full, attn_norm_g, dn_a_log,
                dn_dt_bias, dn_out_norm_g, sg_norm_g, sg_w, sg_b, ffn_norm_g, ffn_conv_b, final_norm_g, n_in):
    t = x2d.shape[0]
    g1, g2, gf = attn_norm_g, ffn_norm_g, final_norm_g.reshape(1, D_MODEL)
    a_log4, dt_bias4 = _lanes4(dn_a_log), _lanes4(dn_dt_bias)
    sg_w3 = sg_w[0]
    sg_b_t = sg_b[0].T
    conv_b = ffn_conv_b

    h1, rstd1 = _rmsnorm_fwd(x2d, g1)
    p = _matmul(h1, w_in_t, "nt", "in_proj", (512, PROJ_PAD, D_MODEL))
    q, k, v, beta4, g4 = _dn_prep(p, dn_conv_full, a_log4, dt_bias4)
    mix_half, s_all = _dn_forward(q, k, v, beta4, g4, p, dn_out_norm_g)
    mix = _sg_forward(p, sg_norm_g, sg_w3, sg_b_t, mix_half)
    w_up_t, w_out_full, w_down_full = late_weights(mix)
    x2 = _matmul(mix, w_out_full, "nn", "out_proj", (1024, 1024, 1024), add=x2d)
    h2, rstd2 = _rmsnorm_fwd(x2, g2)
    up = _matmul(h2, w_up_t, "nt", "up_proj", (512, D_FF, D_MODEL))
    act = _ffn_act(up, ffn_conv_full, conv_b)
    x3 = _matmul(act, w_down_full, "nn", "down_proj", (512, 1024, D_FF), add=x2)
    loss_lanes, dx3, dx3b, d_gf = _final_loss(x3, tgt, gf)

    dact = _matmul(dx3b, w_down_full, "nt", "down_proj_dx", (512, D_FF, D_MODEL))
    d_w_down = _matmul(act, dx3b, "tn", "down_proj_dw", (256, 1024, t), out_dtype=BF16)
    sent_down, token = send_early([d_w_down.reshape(N_DEV, D_FF // N_DEV, D_MODEL)], d_w_down, "send_dw_down")
    dup, d_ffn_conv, d_ffn_conv_b = _ffn_bwd(up, ffn_conv_full, conv_b + token[0:1, 0:1], dact)
    dh2 = _matmul(dup, w_up_t, "nn", "up_proj_dx", (256, 1024, 2 * D_FF))
    d_w_up_t = _matmul(dup, h2, "tn", "up_proj_dw", (512, 1024, t), out_dtype=BF16)
    dx2, dx2b, d_g2 = _rmsnorm_bwd(x2, rstd2, g2, dh2, dx3)
    dmix = _matmul(dx2b, w_out_full, "nt", "out_proj_dx", (1024, 1024, 1024))
    d_w_out = _matmul(mix, dx2b, "tn", "out_proj_dw", (512, 1024, t), out_dtype=BF16)
    sent_up_out, token = send_early(
        [d_w_up_t.reshape(N_DEV, 2 * D_FF // N_DEV, D_MODEL), d_w_out.reshape(N_DEV, D_MODEL // N_DEV, D_MODEL)],
        d_w_out, "send_dw_up_out")
    dp, d_sg_norm, d_sg_w, d_sg_b_t = _sg_backward(p, sg_norm_g + token[0:1, 0:1], sg_w3, sg_b_t, dmix)
    dq, dk, dv, dbeta4, dg4, dp, d_dn_norm = _dn_backward(q, k, v, beta4, g4, p, dn_out_norm_g, s_all, dmix, dp)
    dc_dn, d_dn_conv, dp, d_a_log4, d_dt_bias4 = _dn_prep_bwd(p, dn_conv_full, a_log4, dt_bias4, dq, dk, dv, dbeta4, dg4, dp)
    dp = _conv_bwd_input(dc_dn, dn_conv_full, "dn_conv_dx", out_cols=PROJ_PAD, into=dp)
    d_w_in_t = _matmul(dp, h1, "tn", "in_proj_dw", (PROJ_PAD // 5, 1024, t), out_dtype=BF16)
    sent_in, token = send_early([d_w_in_t[:N_DEV * n_in].reshape(N_DEV, n_in, D_MODEL)], d_w_in_t, "send_dw_in")
    dh1 = _matmul(dp, w_in_t, "nn", "in_proj_dx", (512, 1024, PROJ_PAD))
    grad_x, _, d_g1 = _rmsnorm_bwd(x2d, rstd1, g1 + token[0:1, 0:1], dh1, dx2)

    grads = dict(
        attn_norm_g=d_g1, dn_conv_w=d_dn_conv, dn_a_log=d_a_log4[:, :N_HEADS], dn_dt_bias=d_dt_bias4[:, :N_HEADS],
        dn_out_norm_g=d_dn_norm, sg_norm_g=d_sg_norm, sg_w=d_sg_w, sg_b=d_sg_b_t[:, :SG_GROUPS].T,
        ffn_norm_g=d_g2, ffn_conv_w=d_ffn_conv, ffn_conv_b=d_ffn_conv_b, final_norm_g=d_gf)
    return loss_lanes, grad_x, grads, (sent_down, sent_up_out, sent_in)
```

```python
import functools
import math

import jax
import jax.numpy as jnp
from jax import lax
from jax.experimental import pallas as pl
from jax.experimental.pallas import tpu as pltpu

F32 = jnp.float32
BF16 = jnp.bfloat16
HI = lax.Precision.HIGHEST

D_MODEL = 1024
DN_WIDTH = 512
HEAD_DIM = 128
N_HEADS = 4
SG_WIDTH = 512
SG_GROUPS = 4
SG_DIM = 128
SG_BLOCK = 128
D_FF = 2816
CHUNK = 64
CONV_K = 4
FFN_CONV = 3
EPS = 1e-6
PROJ_COLS = 3080
PROJ_MAIN = 3072
PROJ_PAD = 3200
GELU_C = math.sqrt(2.0 / math.pi)
N_DEV = 8
LANES = 128
SUBLANES = 8
HALO = SUBLANES
VMEM_LIMIT = 48 * 1024 * 1024

ADAM_LR = 0.001
ADAM_B1 = 0.9
ADAM_B2 = 0.999
ADAM_EPS = 1e-08
ADAM_WD = 0.01
ADAM_STEP = 10

MESH_AXES = ("x", "y", "c")
MESH_ID = pl.DeviceIdType.MESH


def _pcall(body, **kw):
    return pl.pallas_call(body, **kw)


def _params(*sem):
    return pltpu.CompilerParams(dimension_semantics=sem, vmem_limit_bytes=VMEM_LIMIT)


def _pick(n, cap):
    best = None
    for t in range(LANES, cap + 1, LANES):
        if n % t == 0:
            best = t
    return best if best else n


FAST, MID, EXACT = "bf16 operands, one pass", "three bf16 passes", "six bf16 passes"


def dot_f32(a, b, dims, tier):
    if tier == FAST:
        return lax.dot_general(a.astype(BF16), b.astype(BF16), dims, preferred_element_type=F32)
    prec = lax.Precision.HIGH if tier == MID else HI
    return lax.dot_general(a, b, dims, precision=prec, preferred_element_type=F32)


def dot_nn(a, b, tier=EXACT):
    return dot_f32(a, b, (((1,), (0,)), ((), ())), tier)


def dot_nt(a, b, tier=EXACT):
    return dot_f32(a, b, (((1,), (1,)), ((), ())), tier)


def dot_tn(a, b, tier=EXACT):
    return dot_f32(a, b, (((0,), (0,)), ((), ())), tier)


def sigmoid(x):
    return 1.0 / (1.0 + jnp.exp(-x))


def silu(x):
    return x * sigmoid(x)


def silu_grad(x):
    s = sigmoid(x)
    return s * (1.0 + x * (1.0 - s))


def gelu(x):
    return 0.5 * x * (1.0 + jnp.tanh(GELU_C * (x + 0.044715 * x * x * x)))


def gelu_grad(x):
    t = jnp.tanh(GELU_C * (x + 0.044715 * x * x * x))
    return 0.5 * (1.0 + t) + 0.5 * x * (1.0 - t * t) * GELU_C * (1.0 + 3.0 * 0.044715 * x * x)


def softplus(z):
    return jnp.maximum(z, 0.0) + jnp.log(1.0 + jnp.exp(-jnp.abs(z)))


def rms_fwd(x, g):
    r = lax.rsqrt(jnp.mean(x * x, axis=-1, keepdims=True) + EPS)
    return x * r * g, r


def rms_bwd(x, r, g, dy):
    dyg = dy * g
    xr = x * r
    dx = r * (dyg - xr * jnp.mean(dyg * xr, axis=-1, keepdims=True))
    return dx, dy * xr


def l2_fwd(x):
    r = lax.rsqrt(jnp.sum(x * x, axis=-1, keepdims=True) + EPS)
    return x * r, r


def l2_bwd(x, r, dy):
    xr = x * r
    return r * (dy - xr * jnp.sum(dy * xr, axis=-1, keepdims=True))


def _tri_masks(n):
    row = lax.broadcasted_iota(jnp.int32, (n, n), 0)
    col = lax.broadcasted_iota(jnp.int32, (n, n), 1)
    return row >= col, row > col


def chunk_cumsum(g4):
    incl, _ = _tri_masks(g4.shape[0])
    return dot_nn(incl.astype(F32), g4)


STACK = N_HEADS * CHUNK
DN_CHUNKS = 4


def _head_rows(h):
    return slice(h * CHUNK, (h + 1) * CHUNK)


def _stack_heads(x):
    return jnp.concatenate([x[:, h * HEAD_DIM:(h + 1) * HEAD_DIM] for h in range(N_HEADS)], axis=0)


def _stack_lanes(x4):
    return jnp.concatenate([x4[:, h:h + 1] for h in range(N_HEADS)], axis=0)


def _per_head(fn):
    return jnp.concatenate([fn(h) for h in range(N_HEADS)], axis=0)


def _unit_lower_inverses(l_strict, order):
    c = l_strict[0].shape[0]
    row = lax.broadcasted_iota(jnp.int32, (c, c), 0)
    col = lax.broadcasted_iota(jnp.int32, (c, c), 1)
    eye = (row == col).astype(F32)
    p = [-l for l in l_strict]
    a = [eye + n for n in p]
    for _ in range(int(math.log2(order)) - 1):
        p = [dot_nn(x, x, FAST) for x in p]
        a = [x + dot_nn(x, y, FAST) for x, y in zip(a, p)]
    return a


def dn_chunks_local(chunks):
    row = lax.broadcasted_iota(jnp.int32, (STACK, STACK), 0)
    col = lax.broadcasted_iota(jnp.int32, (STACK, STACK), 1)
    same = (row // CHUNK) == (col // CHUNK)
    incl = jnp.logical_and(same, row >= col)
    strict = jnp.logical_and(same, row > col)
    locs = []
    for q, k, v, beta, gc4 in chunks:
        gc_col = _stack_lanes(gc4)
        gc_row = jnp.sum(jnp.where(row == col, gc_col, 0.0), axis=0, keepdims=True)
        decay = jnp.where(incl, jnp.exp(jnp.minimum(gc_col - gc_row, 0.0)), 0.0)
        gamma = jnp.exp(gc_col)
        gc_last = jnp.concatenate([jnp.broadcast_to(gc4[CHUNK - 1:CHUNK, h:h + 1], (CHUNK, 1)) for h in range(N_HEADS)], axis=0)
        tau = jnp.exp(gc_last - gc_col)
        kb = k * beta
        locs.append(dict(decay=decay, gamma=gamma, tau=tau, cd=jnp.exp(gc_last), kb=kb, qd=q * gamma, kt=k * tau,
                         incl=incl, strict=strict))
    for loc, (q, k, v, beta, gc4) in zip(locs, chunks):
        loc["l_mat"] = jnp.where(strict, dot_nt(loc["kb"], k, FAST) * loc["decay"], 0.0)
    for loc, a_inv in zip(locs, _unit_lower_inverses([loc["l_mat"] for loc in locs], CHUNK)):
        loc["a_inv"] = a_inv
    for loc, (q, k, v, beta, gc4) in zip(locs, chunks):
        sol = dot_nn(loc["a_inv"], jnp.concatenate([v * beta, loc["kb"] * loc["gamma"]], axis=1), FAST)
        loc.update(sol=sol, value=sol[:, :HEAD_DIM], kcd=sol[:, HEAD_DIM:])
        loc["attn"] = jnp.where(incl, dot_nt(q, k, FAST) * loc["decay"], 0.0)
    return locs


def dn_chunk_state(loc, s):
    kcd, qd, kt, cd = loc["kcd"], loc["qd"], loc["kt"], loc["cd"]
    v_new = loc["value"] - _per_head(lambda h: dot_nn(kcd[_head_rows(h)], s[h], FAST))
    o = _per_head(lambda h: dot_nn(qd[_head_rows(h)], s[h], FAST)) + dot_nn(loc["attn"], v_new, FAST)
    s_new = [s[h] * cd[h * CHUNK:h * CHUNK + 1, :] + dot_tn(kt[_head_rows(h)], v_new[_head_rows(h)], FAST)
             for h in range(N_HEADS)]
    loc["v_new"] = v_new
    return o, s_new


def dn_chunks_bwd(items, ds_last):
    hr = _head_rows
    n = len(items)
    pre = []
    for q, k, v, beta, loc, s, do in items:
        pre.append(dict(
            dv_part=dot_tn(loc["attn"], do, FAST),
            dattn=jnp.where(loc["incl"], dot_nt(do, loc["v_new"], FAST), 0.0),
            dqd=_per_head(lambda h: dot_nt(do[hr(h)], s[h], FAST)),
            ds_part=[dot_tn(loc["qd"][hr(h)], do[hr(h)], FAST) for h in range(N_HEADS)]))
    ds_new_of, dv_new_of = [None] * n, [None] * n
    ds = ds_last
    for c in reversed(range(n)):
        loc = items[c][4]
        ds_new_of[c] = ds
        dv_new = pre[c]["dv_part"] + _per_head(lambda h: dot_nn(loc["kt"][hr(h)], ds[h], FAST))
        dv_new_of[c] = dv_new
        ds = [pre[c]["ds_part"][h] + ds[h] * loc["cd"][h * CHUNK:h * CHUNK + 1, :]
              - dot_tn(loc["kcd"][hr(h)], dv_new[hr(h)], FAST) for h in range(N_HEADS)]
    is_last = (lax.broadcasted_iota(jnp.int32, (STACK, 1), 0) % CHUNK) == CHUNK - 1
    out = []
    for c, (q, k, v, beta, loc, s, do) in enumerate(items):
        decay, gamma, tau, cd, kb = loc["decay"], loc["gamma"], loc["tau"], loc["cd"], loc["kb"]
        dv_new, ds_new, dattn, dqd = dv_new_of[c], ds_new_of[c], pre[c]["dattn"], pre[c]["dqd"]
        dkt = _per_head(lambda h: dot_nt(loc["v_new"][hr(h)], ds_new[h], FAST))
        dkcd = -_per_head(lambda h: dot_nt(dv_new[hr(h)], s[h], FAST))
        drhs = dot_tn(loc["a_inv"], jnp.concatenate([dv_new, dkcd], axis=1), FAST)
        dvb, dkbg = drhs[:, :HEAD_DIM], drhs[:, HEAD_DIM:]
        dl = jnp.where(loc["strict"], -dot_nt(drhs, loc["sol"], FAST), 0.0)
        dkk = dl * decay
        dqk = dattn * decay
        e = dl * loc["l_mat"] + dattn * loc["attn"]
        dgc = jnp.sum(e, axis=1, keepdims=True) - jnp.sum(e, axis=0, keepdims=True).T
        dkb = dot_nn(dkk, k, FAST) + dkbg * gamma
        dk = dot_tn(dkk, kb, FAST) + dot_tn(dqk, q, FAST) + dkt * tau
        dq = dot_nn(dqk, k, FAST) + dqd * gamma
        dgamma = jnp.sum(dkbg * kb, axis=1, keepdims=True) + jnp.sum(dqd * q, axis=1, keepdims=True)
        dtau_tau = jnp.sum(dkt * k, axis=1, keepdims=True) * tau
        dgc = dgc + dgamma * gamma - dtau_tau

        def last_term(h):
            dcd = jnp.sum(jnp.sum(ds_new[h] * s[h], axis=1, keepdims=True), axis=0, keepdims=True)
            total = jnp.sum(dtau_tau[hr(h)], axis=0, keepdims=True) + dcd * cd[h * CHUNK:h * CHUNK + 1, :]
            return jnp.broadcast_to(total, (CHUNK, 1))

        dgc = dgc + jnp.where(is_last, _per_head(last_term), 0.0)
        dk = dk + dkb * beta
        dbeta = jnp.sum(dkb * k, axis=1, keepdims=True) + jnp.sum(dvb * v, axis=1, keepdims=True)
        out.append((dq, dk, dvb * beta, dbeta, dgc))
    return out, ds


def _token_tile(t):
    return _pick(t, 256)


STRIP = 32
NORM_STRIP = 16


def _for_strips(n_rows, rows, fn):
    def step(r, carry):
        fn(pl.multiple_of(r * rows, rows))
        return carry

    lax.fori_loop(0, n_rows // rows, step, 0)


def _fold_rows(x):
    out = x[0:SUBLANES, :]
    for i in range(1, x.shape[0] // SUBLANES):
        out = out + x[i * SUBLANES:(i + 1) * SUBLANES, :]
    return out


def _rmsnorm_fwd(x, g):
    t, d = x.shape
    tm = _token_tile(t)

    def body(x_ref, g_ref, h_ref, r_ref):
        y, r = rms_fwd(x_ref[...], g_ref[...])
        h_ref[...] = y.astype(BF16)
        r_ref[...] = r

    return _pcall(
        body, grid=(t // tm,),
        in_specs=[pl.BlockSpec((tm, d), lambda i: (i, 0)), pl.BlockSpec((1, d), lambda i: (0, 0))],
        out_specs=[pl.BlockSpec((tm, d), lambda i: (i, 0)), pl.BlockSpec((tm, 1), lambda i: (i, 0))],
        out_shape=[jax.ShapeDtypeStruct((t, d), BF16), jax.ShapeDtypeStruct((t, 1), F32)],
        compiler_params=_params("parallel"), name="rmsnorm_fwd")(x, g)


def _rmsnorm_bwd(x, r, g, dh, dres):
    t, d = x.shape
    tm = _token_tile(t)

    def body(x_ref, r_ref, g_ref, dh_ref, dres_ref, dx_ref, dxb_ref, dg_ref):
        dx, dg_rows = rms_bwd(x_ref[...], r_ref[...], g_ref[...], dh_ref[...])
        dx = dx + dres_ref[...]
        dx_ref[...] = dx
        dxb_ref[...] = dx.astype(BF16)

        @pl.when(pl.program_id(0) == 0)
        def _():
            dg_ref[...] = jnp.zeros_like(dg_ref)

        dg_ref[...] += jnp.sum(dg_rows, axis=0, keepdims=True)

    tile = pl.BlockSpec((tm, d), lambda i: (i, 0))
    row = pl.BlockSpec((1, d), lambda i: (0, 0))
    return _pcall(
        body, grid=(t // tm,),
        in_specs=[tile, pl.BlockSpec((tm, 1), lambda i: (i, 0)), row, tile, tile],
        out_specs=[tile, tile, row],
        out_shape=[jax.ShapeDtypeStruct((t, d), F32), jax.ShapeDtypeStruct((t, d), BF16), jax.ShapeDtypeStruct((1, d), F32)],
        compiler_params=_params("arbitrary"), name="rmsnorm_bwd")(x, r, g, dh, dres)


def _matmul(a, b, mode, name, tiles, add=None, out_dtype=F32):
    if mode == "nn":
        (m, k), n = a.shape, b.shape[1]
    elif mode == "nt":
        (m, k), n = a.shape, b.shape[0]
    else:
        (k, m), n = a.shape, b.shape[1]
    tm, tn, tk = min(tiles[0], m), min(tiles[1], n), min(tiles[2], k)
    assert m % tm == 0 and n % tn == 0 and k % tk == 0, (name, m, n, k, tiles)
    nk = k // tk
    dims = {"nn": (((1,), (0,)), ((), ())), "nt": (((1,), (1,)), ((), ())), "tn": (((0,), (0,)), ((), ()))}[mode]

    def finish(res, add_ref, o_ref):
        if add_ref is not None:
            res = res + add_ref[...]
        o_ref[...] = res.astype(o_ref.dtype)

    def body(*refs):
        a_ref, b_ref = refs[0], refs[1]
        add_ref = refs[2] if add is not None else None
        o_ref = refs[3] if add is not None else refs[2]
        part = lax.dot_general(a_ref[...], b_ref[...], dims, preferred_element_type=F32)
        if nk == 1:
            finish(part, add_ref, o_ref)
            return
        acc_ref = refs[-1]
        kk = pl.program_id(2)

        @pl.when(kk == 0)
        def _():
            acc_ref[...] = part

        @pl.when(kk > 0)
        def _():
            acc_ref[...] += part

        @pl.when(kk == nk - 1)
        def _():
            finish(acc_ref[...], add_ref, o_ref)

    a_spec = pl.BlockSpec((tk, tm), lambda j, i, kk: (kk, i)) if mode == "tn" else pl.BlockSpec((tm, tk), lambda j, i, kk: (i, kk))
    b_spec = pl.BlockSpec((tn, tk), lambda j, i, kk: (j, kk)) if mode == "nt" else pl.BlockSpec((tk, tn), lambda j, i, kk: (kk, j))
    o_spec = pl.BlockSpec((tm, tn), lambda j, i, kk: (i, j))
    in_specs = [a_spec, b_spec] + ([o_spec] if add is not None else [])
    args = (a, b) + ((add,) if add is not None else ())
    return _pcall(
        body, grid=(n // tn, m // tm, nk), in_specs=in_specs, out_specs=o_spec,
        out_shape=jax.ShapeDtypeStruct((m, n), out_dtype),
        scratch_shapes=[pltpu.VMEM((tm, tn), F32)] if nk > 1 else [],
        compiler_params=_params("parallel", "parallel", "arbitrary"), name=name)(*args)


def _prev_halo_spec(tm, width, col_block):
    return pl.BlockSpec((HALO, width), lambda i: (jnp.maximum(i * (tm // HALO) - 1, 0), col_block))


def _fill_with_prev(xp_ref, tile, halo, first):
    xp_ref[0:HALO, :] = jnp.where(first, 0.0, halo)
    xp_ref[HALO:, :] = tile


def _delayed(xp_ref, row0, cols, taps):
    ext = xp_ref[pl.ds(row0, STRIP + HALO), cols]
    return [ext[HALO:, :]] + [pltpu.roll(ext, j, 0)[HALO:, :] for j in range(1, taps)]


def _causal_conv(delayed, w):
    taps = len(delayed)
    out = delayed[0] * w[taps - 1:taps, :]
    for j in range(1, taps):
        out = out + delayed[j] * w[taps - 1 - j:taps - j, :]
    return out


def _advanced_conv(buf_ref, row0, cols, w):
    taps = w.shape[0]
    ext = buf_ref[pl.ds(row0, STRIP + HALO), cols]
    out = ext[:STRIP, :] * w[taps - 1:taps, :]
    for j in range(1, taps):
        out = out + pltpu.roll(ext, STRIP + HALO - j, 0)[:STRIP, :] * w[taps - 1 - j:taps - j, :]
    return out


def _dn_prep(p, conv_w, a_log4, dt_bias4):
    t = p.shape[0]
    tm = _token_tile(t)
    w3 = 3 * DN_WIDTH

    def body(x_ref, halo_ref, pbd_ref, w_ref, alog_ref, dtb_ref, q_ref, k_ref, v_ref, beta_ref, g_ref, xp_ref):
        _fill_with_prev(xp_ref, x_ref[...], halo_ref[...], pl.program_id(0) == 0)

        def strip(row0):
            rows = pl.ds(row0, STRIP)
            for h in range(N_HEADS):
                sl = slice(h * HEAD_DIM, (h + 1) * HEAD_DIM)
                for part, out_ref in ((0, q_ref), (1, k_ref), (2, v_ref)):
                    cols = slice(part * DN_WIDTH + h * HEAD_DIM, part * DN_WIDTH + (h + 1) * HEAD_DIM)
                    y = silu(_causal_conv(_delayed(xp_ref, row0, cols, CONV_K), w_ref[:, cols]))
                    if part == 0:
                        y = l2_fwd(y)[0] * (HEAD_DIM ** -0.5)
                    elif part == 1:
                        y = l2_fwd(y)[0]
                    out_ref[rows, sl] = y
            head = lax.broadcasted_iota(jnp.int32, (STRIP, LANES), 1) < N_HEADS
            pbd = pbd_ref[rows, :]
            beta_ref[rows, :] = jnp.where(head, sigmoid(pbd), 0.0)
            a_raw = pltpu.roll(pbd, LANES - N_HEADS, 1)
            g_ref[rows, :] = jnp.where(head, -jnp.exp(alog_ref[...]) * softplus(a_raw + dtb_ref[...]), 0.0)

        _for_strips(tm, STRIP, strip)

    tok = lambda w, cb: pl.BlockSpec((tm, w), lambda i: (i, cb))
    full = lambda a: pl.BlockSpec(a.shape, lambda i: (0, 0))
    return _pcall(
        body, grid=(t // tm,),
        in_specs=[tok(w3, 0), _prev_halo_spec(tm, w3, 0), tok(LANES, PROJ_MAIN // LANES),
                  full(conv_w), full(a_log4), full(dt_bias4)],
        out_specs=[tok(DN_WIDTH, 0)] * 3 + [tok(LANES, 0)] * 2,
        out_shape=[jax.ShapeDtypeStruct((t, DN_WIDTH), F32)] * 3 + [jax.ShapeDtypeStruct((t, LANES), F32)] * 2,
        scratch_shapes=[pltpu.VMEM((HALO + tm, w3), F32)],
        compiler_params=_params("parallel"), name="dn_prep")(p, p, p, conv_w, a_log4, dt_bias4)


def _dn_prep_bwd(p, conv_w, a_log4, dt_bias4, dq, dk, dv, dbeta4, dg4, dp_buf):
    t = p.shape[0]
    tm = _token_tile(t)
    w3 = 3 * DN_WIDTH

    def body(x_ref, halo_ref, pbd_ref, w_ref, alog_ref, dtb_ref, dq_ref, dk_ref, dv_ref, dbeta_ref, dg_ref, _,
             dc_ref, dw_ref, dpbd_ref, dalog_ref, ddtb_ref, xp_ref, dw_acc, lane_acc):
        first = pl.program_id(0) == 0
        _fill_with_prev(xp_ref, x_ref[...], halo_ref[...], first)
        dw_acc[...] = jnp.zeros_like(dw_acc)
        lane_acc[...] = jnp.zeros_like(lane_acc)

        def strip(row0):
            rows = pl.ds(row0, STRIP)
            for h in range(N_HEADS):
                sl = slice(h * HEAD_DIM, (h + 1) * HEAD_DIM)
                for part, dy_ref in ((0, dq_ref), (1, dk_ref), (2, dv_ref)):
                    cols = slice(part * DN_WIDTH + h * HEAD_DIM, part * DN_WIDTH + (h + 1) * HEAD_DIM)
                    delayed = _delayed(xp_ref, row0, cols, CONV_K)
                    c = _causal_conv(delayed, w_ref[:, cols])
                    dy = dy_ref[rows, sl]
                    if part < 2:
                        y = silu(c)
                        _, r = l2_fwd(y)
                        dy = l2_bwd(y, r, dy * (HEAD_DIM ** -0.5) if part == 0 else dy)
                    dc = dy * silu_grad(c)
                    dc_ref[rows, cols] = dc
                    for j in range(CONV_K):
                        k = CONV_K - 1 - j
                        dw_acc[k * SUBLANES:(k + 1) * SUBLANES, cols] += _fold_rows(dc * delayed[j])
            head = lax.broadcasted_iota(jnp.int32, (STRIP, LANES), 1) < N_HEADS
            pbd = pbd_ref[rows, :]
            beta = sigmoid(pbd)
            dpb = jnp.where(head, dbeta_ref[rows, :] * beta * (1.0 - beta), 0.0)
            z = pltpu.roll(pbd, LANES - N_HEADS, 1) + dtb_ref[...]
            neg_rate = -jnp.exp(alog_ref[...])
            dg = dg_ref[rows, :]
            dpa = jnp.where(head, dg * neg_rate * sigmoid(z), 0.0)
            dpbd_ref[rows, :] = (dpb + pltpu.roll(dpa, N_HEADS, 1)).astype(BF16)
            g = jnp.where(head, neg_rate * softplus(z), 0.0)
            lane_acc[0:SUBLANES, :] += _fold_rows(dg * g)
            lane_acc[SUBLANES:, :] += _fold_rows(dpa)

        _for_strips(tm, STRIP, strip)

        @pl.when(first)
        def _():
            dw_ref[...] = jnp.zeros_like(dw_ref)
            dalog_ref[...] = jnp.zeros_like(dalog_ref)
            ddtb_ref[...] = jnp.zeros_like(ddtb_ref)

        for k in range(CONV_K):
            dw_ref[k:k + 1, :] += jnp.sum(dw_acc[k * SUBLANES:(k + 1) * SUBLANES, :], axis=0, keepdims=True)
        dalog_ref[...] += jnp.sum(lane_acc[0:SUBLANES, :], axis=0, keepdims=True)
        ddtb_ref[...] += jnp.sum(lane_acc[SUBLANES:, :], axis=0, keepdims=True)

    tok = lambda w, cb: pl.BlockSpec((tm, w), lambda i: (i, cb))
    full = lambda shape: pl.BlockSpec(shape, lambda i: (0, 0))
    return _pcall(
        body, grid=(t // tm,),
        in_specs=[tok(w3, 0), _prev_halo_spec(tm, w3, 0), tok(LANES, PROJ_MAIN // LANES),
                  full(conv_w.shape), full(a_log4.shape), full(dt_bias4.shape)] + [tok(DN_WIDTH, 0)] * 3 + [tok(LANES, 0)] * 2
        + [pl.BlockSpec(memory_space=pl.ANY)],
        out_specs=[tok(w3, 0), full((CONV_K, w3)), tok(LANES, PROJ_MAIN // LANES), full((1, LANES)), full((1, LANES))],
        out_shape=[jax.ShapeDtypeStruct((t, w3), F32), jax.ShapeDtypeStruct((CONV_K, w3), F32),
                   jax.ShapeDtypeStruct(dp_buf.shape, dp_buf.dtype),
                   jax.ShapeDtypeStruct((1, LANES), F32), jax.ShapeDtypeStruct((1, LANES), F32)],
        input_output_aliases={11: 2},
        scratch_shapes=[pltpu.VMEM((HALO + tm, w3), F32), pltpu.VMEM((CONV_K * SUBLANES, w3), F32),
                        pltpu.VMEM((2 * SUBLANES, LANES), F32)],
        compiler_params=_params("arbitrary"), name="dn_prep_bwd")(p, p, p, conv_w, a_log4, dt_bias4, dq, dk, dv, dbeta4, dg4, dp_buf)


def _conv_bwd_input(dc, w, name, out_cols=None, col_block=0, into=None):
    t, c = dc.shape
    taps = w.shape[0]
    tm = _token_tile(t)
    ct = _pick(c, 1536)
    n_tok = t // tm
    out_cols = c if out_cols is None else out_cols

    def body(dc_ref, next_ref, w_ref, *rest):
        dx_ref, buf_ref = rest[-2], rest[-1]
        buf_ref[0:tm, :] = dc_ref[...]
        buf_ref[tm:, :] = jnp.where(pl.program_id(0) == n_tok - 1, 0.0, next_ref[...])

        def strip(row0):
            for c0 in range(0, ct, LANES):
                cols = slice(c0, c0 + LANES)
                dx_ref[pl.ds(row0, STRIP), cols] = _advanced_conv(buf_ref, row0, cols, w_ref[:, cols]).astype(BF16)

        _for_strips(tm, STRIP, strip)

    in_specs = [pl.BlockSpec((tm, ct), lambda i, j: (i, j)),
                pl.BlockSpec((HALO, ct), lambda i, j: (jnp.minimum((i + 1) * (tm // HALO), t // HALO - 1), j)),
                pl.BlockSpec((taps, ct), lambda i, j: (0, j))]
    args = (dc, dc, w)
    aliases = {}
    if into is not None:
        in_specs.append(pl.BlockSpec(memory_space=pl.ANY))
        args += (into,)
        aliases = {3: 0}
    return _pcall(
        body, grid=(n_tok, c // ct), in_specs=in_specs,
        out_specs=pl.BlockSpec((tm, ct), lambda i, j: (i, j + col_block)),
        out_shape=jax.ShapeDtypeStruct((t, out_cols), BF16), input_output_aliases=aliases,
        scratch_shapes=[pltpu.VMEM((tm + HALO, ct), F32)],
        compiler_params=_params("parallel", "parallel"), name=name)(*args)


def _dn_forward(q, k, v, beta4, g4, p, norm_g):
    t = q.shape[0]
    n = t // CHUNK
    rows_per_step = DN_CHUNKS * CHUNK

    def body(q_ref, k_ref, v_ref, b_ref, g_ref, gate_ref, ng_ref, mix_ref, s_all_ref, s_ref):
        @pl.when(pl.program_id(0) == 0)
        def _():
            s_ref[...] = jnp.zeros_like(s_ref)

        chunks = []
        for c in range(DN_CHUNKS):
            rows = slice(c * CHUNK, (c + 1) * CHUNK)
            chunks.append((_stack_heads(q_ref[rows, :]), _stack_heads(k_ref[rows, :]), _stack_heads(v_ref[rows, :]),
                           _stack_lanes(b_ref[rows, :]), chunk_cumsum(g_ref[rows, :])))
        locs = dn_chunks_local(chunks)
        s = [s_ref[h] for h in range(N_HEADS)]
        for c in range(DN_CHUNKS):
            rows = slice(c * CHUNK, (c + 1) * CHUNK)
            for h in range(N_HEADS):
                s_all_ref[c, h] = s[h]
            o, s = dn_chunk_state(locs[c], s)
            o_n, _ = rms_fwd(o, ng_ref[...])
            for h in range(N_HEADS):
                sl = slice(h * HEAD_DIM, (h + 1) * HEAD_DIM)
                mix_ref[rows, sl] = (o_n[_head_rows(h)] * silu(gate_ref[rows, sl])).astype(BF16)
        for h in range(N_HEADS):
            s_ref[h] = s[h]

    ch = lambda w, cb: pl.BlockSpec((rows_per_step, w), lambda i: (i, cb))
    return _pcall(
        body, grid=(n // DN_CHUNKS,),
        in_specs=[ch(DN_WIDTH, 0)] * 3 + [ch(LANES, 0)] * 2 + [ch(DN_WIDTH, 3), pl.BlockSpec((1, HEAD_DIM), lambda i: (0, 0))],
        out_specs=[ch(DN_WIDTH, 0), pl.BlockSpec((DN_CHUNKS, N_HEADS, HEAD_DIM, HEAD_DIM), lambda i: (i, 0, 0, 0))],
        out_shape=[jax.ShapeDtypeStruct((t, DN_WIDTH + SG_WIDTH), BF16), jax.ShapeDtypeStruct((n, N_HEADS, HEAD_DIM, HEAD_DIM), F32)],
        scratch_shapes=[pltpu.VMEM((N_HEADS, HEAD_DIM, HEAD_DIM), F32)],
        compiler_params=_params("arbitrary"), name="dn_forward")(q, k, v, beta4, g4, p, norm_g)


def _dn_backward(q, k, v, beta4, g4, p, norm_g, s_all, dmix, dp_buf):
    t = q.shape[0]
    n = t // CHUNK
    steps = n // DN_CHUNKS
    rows_per_step = DN_CHUNKS * CHUNK

    def body(q_ref, k_ref, v_ref, b_ref, g_ref, gate_ref, ng_ref, s_in_ref, dmix_ref, _,
             dq_ref, dk_ref, dv_ref, db_ref, dg_ref, dgate_ref, dng_ref, ds_ref):
        @pl.when(pl.program_id(0) == 0)
        def _():
            ds_ref[...] = jnp.zeros_like(ds_ref)
            dng_ref[...] = jnp.zeros_like(dng_ref)

        chunks = []
        for c in range(DN_CHUNKS):
            rows = slice(c * CHUNK, (c + 1) * CHUNK)
            chunks.append((_stack_heads(q_ref[rows, :]), _stack_heads(k_ref[rows, :]), _stack_heads(v_ref[rows, :]),
                           _stack_lanes(b_ref[rows, :]), chunk_cumsum(g_ref[rows, :])))
        items = []
        for c, loc in enumerate(dn_chunks_local(chunks)):
            rows = slice(c * CHUNK, (c + 1) * CHUNK)
            s = [s_in_ref[c, h] for h in range(N_HEADS)]
            o, _ = dn_chunk_state(loc, s)
            o_n, r = rms_fwd(o, ng_ref[...])
            gate = _stack_heads(gate_ref[rows, :])
            dmx = _stack_heads(dmix_ref[rows, :])
            dgate = dmx * o_n * silu_grad(gate)
            do, dng_rows = rms_bwd(o, r, ng_ref[...], dmx * silu(gate))
            dng_ref[...] += jnp.sum(dng_rows, axis=0, keepdims=True)
            for h in range(N_HEADS):
                dgate_ref[rows, h * HEAD_DIM:(h + 1) * HEAD_DIM] = dgate[_head_rows(h)].astype(BF16)
            items.append((*chunks[c][:4], loc, s, do))
        grads, ds = dn_chunks_bwd(items, [ds_ref[h] for h in range(N_HEADS)])
        lane = lax.broadcasted_iota(jnp.int32, (CHUNK, LANES), 1)
        _, strict = _tri_masks(CHUNK)
        for c in range(DN_CHUNKS):
            rows = slice(c * CHUNK, (c + 1) * CHUNK)
            dq, dk, dv, dbeta, dgc = grads[c]
            db4 = jnp.zeros((CHUNK, LANES), F32)
            dgc4 = jnp.zeros((CHUNK, LANES), F32)
            for h in range(N_HEADS):
                sl = slice(h * HEAD_DIM, (h + 1) * HEAD_DIM)
                head_rows = _head_rows(h)
                dq_ref[rows, sl] = dq[head_rows]
                dk_ref[rows, sl] = dk[head_rows]
                dv_ref[rows, sl] = dv[head_rows]
                db4 = jnp.where(lane == h, dbeta[head_rows], db4)
                dgc4 = jnp.where(lane == h, dgc[head_rows], dgc4)
            db_ref[rows, :] = db4
            dg_ref[rows, :] = dot_nn(jnp.logical_not(strict).astype(F32), dgc4)
        for h in range(N_HEADS):
            ds_ref[h] = ds[h]

    rev = lambda w, cb: pl.BlockSpec((rows_per_step, w), lambda i: (steps - 1 - i, cb))
    return _pcall(
        body, grid=(steps,),
        in_specs=[rev(DN_WIDTH, 0)] * 3 + [rev(LANES, 0)] * 2 + [rev(DN_WIDTH, 3), pl.BlockSpec((1, HEAD_DIM), lambda i: (0, 0)),
                  pl.BlockSpec((DN_CHUNKS, N_HEADS, HEAD_DIM, HEAD_DIM), lambda i: (steps - 1 - i, 0, 0, 0)), rev(DN_WIDTH, 0),
                  pl.BlockSpec(memory_space=pl.ANY)],
        out_specs=[rev(DN_WIDTH, 0)] * 3 + [rev(LANES, 0)] * 2 + [rev(DN_WIDTH, 3), pl.BlockSpec((1, HEAD_DIM), lambda i: (0, 0))],
        out_shape=[jax.ShapeDtypeStruct((t, DN_WIDTH), F32)] * 3 + [jax.ShapeDtypeStruct((t, LANES), F32)] * 2
        + [jax.ShapeDtypeStruct(dp_buf.shape, dp_buf.dtype), jax.ShapeDtypeStruct((1, HEAD_DIM), F32)],
        input_output_aliases={9: 5},
        scratch_shapes=[pltpu.VMEM((N_HEADS, HEAD_DIM, HEAD_DIM), F32)],
        compiler_params=_params("arbitrary"), name="dn_backward")(q, k, v, beta4, g4, p, norm_g, s_all, dmix, dp_buf)


def _sg_mask():
    row = lax.broadcasted_iota(jnp.int32, (SG_BLOCK, SG_BLOCK), 0)
    col = lax.broadcasted_iota(jnp.int32, (SG_BLOCK, SG_BLOCK), 1)
    return (col // CHUNK) <= (row // CHUNK)


def _sg_forward(p, norm_g, w_s, b_t, mix_buf):
    t = p.shape[0]

    def body(u_ref, v_ref, ng_ref, w_ref, b_ref, _, o_ref):
        mask = _sg_mask()
        for g in range(SG_GROUPS):
            sl = slice(g * SG_DIM, (g + 1) * SG_DIM)
            vn, _ = rms_fwd(gelu(v_ref[:, sl]), ng_ref[:, sl])
            s = dot_nn(jnp.where(mask, w_ref[g], 0.0), vn, FAST) + b_ref[:, g:g + 1]
            o_ref[:, sl] = (gelu(u_ref[:, sl]) * s).astype(BF16)

    blk = lambda cb: pl.BlockSpec((SG_BLOCK, SG_WIDTH), lambda i: (i, cb))
    return _pcall(
        body, grid=(t // SG_BLOCK,),
        in_specs=[blk(4), blk(5), pl.BlockSpec((1, SG_WIDTH), lambda i: (0, 0)),
                  pl.BlockSpec((SG_GROUPS, SG_BLOCK, SG_BLOCK), lambda i: (0, 0, 0)), pl.BlockSpec((SG_BLOCK, SG_GROUPS), lambda i: (0, 0)),
                  pl.BlockSpec(memory_space=pl.ANY)],
        out_specs=blk(1), out_shape=jax.ShapeDtypeStruct(mix_buf.shape, mix_buf.dtype), input_output_aliases={5: 0},
        compiler_params=_params("parallel"), name="sg_forward")(p, p, norm_g, w_s, b_t, mix_buf)


def _sg_backward(p, norm_g, w_s, b_t, dmix):
    t = p.shape[0]

    def body(u_ref, v_ref, ng_ref, w_ref, b_ref, do_ref, duv_ref, dng_ref, dw_ref, db_ref):
        @pl.when(pl.program_id(0) == 0)
        def _():
            dng_ref[...] = jnp.zeros_like(dng_ref)
            dw_ref[...] = jnp.zeros_like(dw_ref)
            db_ref[...] = jnp.zeros_like(db_ref)

        mask = _sg_mask()
        lane = lax.broadcasted_iota(jnp.int32, (SG_BLOCK, LANES), 1)
        db = jnp.zeros((SG_BLOCK, LANES), F32)
        for g in range(SG_GROUPS):
            sl = slice(g * SG_DIM, (g + 1) * SG_DIM)
            u_raw, v_raw, do = u_ref[:, sl], v_ref[:, sl], do_ref[:, sl]
            vg = gelu(v_raw)
            vn, r = rms_fwd(vg, ng_ref[:, sl])
            w_m = jnp.where(mask, w_ref[g], 0.0)
            s = dot_nn(w_m, vn, FAST) + b_ref[:, g:g + 1]
            duv_ref[:, sl] = (do * s * gelu_grad(u_raw)).astype(BF16)
            ds = do * gelu(u_raw)
            db = jnp.where(lane == g, jnp.sum(ds, axis=1, keepdims=True), db)
            dw_ref[g] += jnp.where(mask, dot_nt(ds, vn, FAST), 0.0)
            dvg, dng_rows = rms_bwd(vg, r, ng_ref[:, sl], dot_tn(w_m, ds, FAST))
            dng_ref[:, sl] += jnp.sum(dng_rows, axis=0, keepdims=True)
            duv_ref[:, SG_WIDTH + g * SG_DIM:SG_WIDTH + (g + 1) * SG_DIM] = (dvg * gelu_grad(v_raw)).astype(BF16)
        db_ref[...] += db

    blk = lambda cb: pl.BlockSpec((SG_BLOCK, SG_WIDTH), lambda i: (i, cb))
    const2 = lambda shape: pl.BlockSpec(shape, lambda i: (0, 0))
    w_spec = pl.BlockSpec((SG_GROUPS, SG_BLOCK, SG_BLOCK), lambda i: (0, 0, 0))
    return _pcall(
        body, grid=(t // SG_BLOCK,),
        in_specs=[blk(4), blk(5), const2((1, SG_WIDTH)), w_spec, const2((SG_BLOCK, SG_GROUPS)), blk(1)],
        out_specs=[pl.BlockSpec((SG_BLOCK, 2 * SG_WIDTH), lambda i: (i, 2)), const2((1, SG_WIDTH)), w_spec,
                   const2((SG_BLOCK, LANES))],
        out_shape=[jax.ShapeDtypeStruct((t, PROJ_PAD), BF16), jax.ShapeDtypeStruct((1, SG_WIDTH), F32),
                   jax.ShapeDtypeStruct((SG_GROUPS, SG_BLOCK, SG_BLOCK), F32), jax.ShapeDtypeStruct((SG_BLOCK, LANES), F32)],
        compiler_params=_params("arbitrary"), name="sg_backward")(p, p, norm_g, w_s, b_t, dmix)


FFN_CT = D_FF // 2


def _ffn_act(up, conv_w, conv_b):
    t = up.shape[0]
    tm = _token_tile(t)
    nj = D_FF // FFN_CT

    def body(ug_ref, uv_ref, hg_ref, hv_ref, wg_ref, wv_ref, bg_ref, bv_ref, act_ref, xg_ref, xv_ref):
        first = pl.program_id(0) == 0
        _fill_with_prev(xg_ref, ug_ref[...], hg_ref[...], first)
        _fill_with_prev(xv_ref, uv_ref[...], hv_ref[...], first)

        def strip(row0):
            for c0 in range(0, FFN_CT, LANES):
                cols = slice(c0, c0 + LANES)
                cg = _causal_conv(_delayed(xg_ref, row0, cols, FFN_CONV), wg_ref[:, cols]) + bg_ref[:, cols]
                cv = _causal_conv(_delayed(xv_ref, row0, cols, FFN_CONV), wv_ref[:, cols]) + bv_ref[:, cols]
                act_ref[pl.ds(row0, STRIP), cols] = (silu(cg) * cv).astype(BF16)

        _for_strips(tm, STRIP, strip)

    tok = lambda off: pl.BlockSpec((tm, FFN_CT), lambda i, j: (i, j + off))
    halo = lambda off: pl.BlockSpec((HALO, FFN_CT), lambda i, j: (jnp.maximum(i * (tm // HALO) - 1, 0), j + off))
    par = lambda rows, off: pl.BlockSpec((rows, FFN_CT), lambda i, j: (0, j + off))
    return _pcall(
        body, grid=(t // tm, nj),
        in_specs=[tok(0), tok(nj), halo(0), halo(nj), par(FFN_CONV, 0), par(FFN_CONV, nj), par(1, 0), par(1, nj)],
        out_specs=pl.BlockSpec((tm, FFN_CT), lambda i, j: (i, j)),
        out_shape=jax.ShapeDtypeStruct((t, D_FF), BF16),
        scratch_shapes=[pltpu.VMEM((HALO + tm, FFN_CT), F32)] * 2,
        compiler_params=_params("parallel", "parallel"), name="ffn_act")(up, up, up, up, conv_w, conv_w, conv_b, conv_b)


def _ffn_bwd(up, conv_w, conv_b, dact):
    t = up.shape[0]
    tm = _pick(t, 128)
    n_tok = t // tm
    width = 2 * D_FF

    def dconv(delayed_g, delayed_v, da, wg, wv, bg, bv):
        cg = _causal_conv(delayed_g, wg) + bg
        cv = _causal_conv(delayed_v, wv) + bv
        s = sigmoid(cg)
        return da * cv * (s * (1.0 + cg * (1.0 - s))), da * (cg * s)

    def body(up_ref, prev_ref, next_ref, da_ref, dan_ref, w_ref, b_ref, dup_ref, dw_ref, db_ref, xp_ref, dc_ref, dw_acc, db_acc):
        first = pl.program_id(0) == 0
        last = pl.program_id(0) == n_tok - 1
        xp_ref[0:HALO, :] = jnp.where(first, 0.0, prev_ref[...])
        xp_ref[HALO:HALO + tm, :] = up_ref[...]
        xp_ref[HALO + tm:, :] = next_ref[...]
        dw_acc[...] = jnp.zeros_like(dw_acc)
        db_acc[...] = jnp.zeros_like(db_acc)

        def strip(row0):
            rows = pl.ds(row0, STRIP)
            for c0 in range(0, D_FF, LANES):
                gc, vc = slice(c0, c0 + LANES), slice(D_FF + c0, D_FF + c0 + LANES)
                del_g, del_v = _delayed(xp_ref, row0, gc, FFN_CONV), _delayed(xp_ref, row0, vc, FFN_CONV)
                dcg, dcv = dconv(del_g, del_v, da_ref[rows, gc], w_ref[:, gc], w_ref[:, vc], b_ref[:, gc], b_ref[:, vc])
                dc_ref[rows, gc] = dcg
                dc_ref[rows, vc] = dcv
                db_acc[:, gc] += _fold_rows(dcg)
                db_acc[:, vc] += _fold_rows(dcv)
                for j in range(FFN_CONV):
                    k = FFN_CONV - 1 - j
                    dw_acc[k * SUBLANES:(k + 1) * SUBLANES, gc] += _fold_rows(dcg * del_g[j])
                    dw_acc[k * SUBLANES:(k + 1) * SUBLANES, vc] += _fold_rows(dcv * del_v[j])

        _for_strips(tm, STRIP, strip)

        for c0 in range(0, D_FF, LANES):
            gc, vc = slice(c0, c0 + LANES), slice(D_FF + c0, D_FF + c0 + LANES)

            def delayed(cols):
                ext = xp_ref[tm:tm + 2 * HALO, cols]
                return [ext[HALO:, :]] + [pltpu.roll(ext, j, 0)[HALO:, :] for j in range(1, FFN_CONV)]

            dcg, dcv = dconv(delayed(gc), delayed(vc), dan_ref[:, gc], w_ref[:, gc], w_ref[:, vc], b_ref[:, gc], b_ref[:, vc])
            dc_ref[tm:, gc] = jnp.where(last, 0.0, dcg)
            dc_ref[tm:, vc] = jnp.where(last, 0.0, dcv)

        def strip_dx(row0):
            for c0 in range(0, width, LANES):
                cols = slice(c0, c0 + LANES)
                dup_ref[pl.ds(row0, STRIP), cols] = _advanced_conv(dc_ref, row0, cols, w_ref[:, cols]).astype(BF16)

        _for_strips(tm, STRIP, strip_dx)

        @pl.when(first)
        def _():
            dw_ref[...] = jnp.zeros_like(dw_ref)
            db_ref[...] = jnp.zeros_like(db_ref)

        for k in range(FFN_CONV):
            dw_ref[k:k + 1, :] += jnp.sum(dw_acc[k * SUBLANES:(k + 1) * SUBLANES, :], axis=0, keepdims=True)
        db_ref[...] += jnp.sum(db_acc[...], axis=0, keepdims=True)

    next_rows = lambda i: jnp.minimum((i + 1) * (tm // HALO), t // HALO - 1)
    full = lambda rows: pl.BlockSpec((rows, width), lambda i: (0, 0))
    return _pcall(
        body, grid=(n_tok,),
        in_specs=[pl.BlockSpec((tm, width), lambda i: (i, 0)),
                  pl.BlockSpec((HALO, width), lambda i: (jnp.maximum(i * (tm // HALO) - 1, 0), 0)),
                  pl.BlockSpec((HALO, width), lambda i: (next_rows(i), 0)),
                  pl.BlockSpec((tm, D_FF), lambda i: (i, 0)), pl.BlockSpec((HALO, D_FF), lambda i: (next_rows(i), 0)),
                  full(FFN_CONV), full(1)],
        out_specs=[pl.BlockSpec((tm, width), lambda i: (i, 0)), full(FFN_CONV), full(1)],
        out_shape=[jax.ShapeDtypeStruct((t, width), BF16), jax.ShapeDtypeStruct((FFN_CONV, width), F32),
                   jax.ShapeDtypeStruct((1, width), F32)],
        scratch_shapes=[pltpu.VMEM((tm + 2 * HALO, width), F32), pltpu.VMEM((tm + HALO, width), F32),
                        pltpu.VMEM((FFN_CONV * SUBLANES, width), F32), pltpu.VMEM((SUBLANES, width), F32)],
        compiler_params=_params("arbitrary"), name="ffn_bwd")(up, up, up, dact, dact, conv_w, conv_b)


def _final_loss(x3, target, g):
    t, d = x3.shape
    tm = _token_tile(t)

    def body(x_ref, t_ref, g_ref, loss_ref, dx_ref, dxb_ref, dg_ref):
        @pl.when(pl.program_id(0) == 0)
        def _():
            loss_ref[...] = jnp.zeros_like(loss_ref)
            dg_ref[...] = jnp.zeros_like(dg_ref)

        x = x_ref[...]
        y, r = rms_fwd(x, g_ref[...])
        err = y - t_ref[...]
        per_tok = jnp.mean(err * err, axis=-1, keepdims=True)
        loss_ref[...] += 0.5 * jnp.sum(per_tok, axis=0, keepdims=True)
        dx, dg_rows = rms_bwd(x, r, g_ref[...], err * (1.0 / d))
        dx_ref[...] = dx
        dxb_ref[...] = dx.astype(BF16)
        dg_ref[...] += jnp.sum(dg_rows, axis=0, keepdims=True)

    tile = pl.BlockSpec((tm, d), lambda i: (i, 0))
    row = pl.BlockSpec((1, d), lambda i: (0, 0))
    return _pcall(
        body, grid=(t // tm,), in_specs=[tile, tile, row],
        out_specs=[pl.BlockSpec((1, LANES), lambda i: (0, 0)), tile, tile, row],
        out_shape=[jax.ShapeDtypeStruct((1, LANES), F32), jax.ShapeDtypeStruct((t, d), F32), jax.ShapeDtypeStruct((t, d), BF16),
                   jax.ShapeDtypeStruct((1, d), F32)],
        compiler_params=_params("arbitrary"), name="final_loss")(x3, target, g)


def _my_position():
    return lax.axis_index("x"), lax.axis_index("y"), lax.axis_index("c")


COPIES = N_DEV - 1


def _all_gather(arrays):
    n = len(arrays)

    def body(*refs):
        x_refs, out_refs = refs[:n], refs[n:2 * n]
        send_sems, recv_sems, local_sems = refs[2 * n:]
        x, y, cc = _my_position()
        me, sibling = (x, y, cc), (x, y, 1 - cc)
        chips = [(1 - x, y), (x, 1 - y), (1 - x, 1 - y)]

        def block(a, px, py, pc):
            return out_refs[a].at[4 * px + 2 * py + pc]

        def copy(a, k, blk, to, src=None):
            return pltpu.make_async_remote_copy(
                src_ref=block(a, *blk) if src is None else src, dst_ref=block(a, *blk),
                send_sem=send_sems.at[a * COPIES + k], recv_sem=recv_sems.at[a * COPIES + k],
                device_id=to, device_id_type=MESH_ID)

        mine = [pltpu.make_async_copy(x_refs[a], block(a, *me), local_sems.at[a]) for a in range(n)]
        for cp in mine:
            cp.start()
        first = []
        for a in range(n):
            first.append(copy(a, 0, me, sibling, src=x_refs[a]))
            first += [copy(a, 1 + j, me, (*chip, cc), src=x_refs[a]) for j, chip in enumerate(chips)]
        for cp in first:
            cp.start()
        passed = []
        for j, chip in enumerate(chips):
            for a in range(n):
                copy(a, 1 + j, (*chip, cc), me).wait_recv()
                passed.append(copy(a, 4 + j, (*chip, cc), sibling))
                passed[-1].start()
        for a in range(n):
            copy(a, 0, sibling, me).wait_recv()
        for j, chip in enumerate(chips):
            for a in range(n):
                copy(a, 4 + j, (*chip, 1 - cc), me).wait_recv()
        for cp in first + passed:
            cp.wait_send()
        for cp in mine:
            cp.wait()

    any_spec = pl.BlockSpec(memory_space=pl.ANY)
    return _pcall(
        body, out_shape=[jax.ShapeDtypeStruct((N_DEV,) + a.shape, a.dtype) for a in arrays],
        in_specs=[any_spec] * n, out_specs=[any_spec] * n,
        scratch_shapes=[pltpu.SemaphoreType.DMA((n * COPIES,)), pltpu.SemaphoreType.DMA((n * COPIES,)),
                        pltpu.SemaphoreType.DMA((n,))],
        name="all_gather")(*arrays)


def _all_to_all(sends):
    n = len(sends)

    def body(*refs):
        send_refs, recv_refs = refs[:n], refs[n:2 * n]
        send_sems, recv_sems, local_sems = refs[2 * n:]
        x, y, cc = _my_position()
        me = 4 * x + 2 * y + cc
        mine = [pltpu.make_async_copy(send_refs[a].at[me], recv_refs[a].at[me], local_sems.at[a]) for a in range(n)]
        for cp in mine:
            cp.start()
        copies = []
        for rel in range(1, N_DEV):
            px, py, pc = x ^ (rel >> 2), y ^ ((rel >> 1) & 1), cc ^ (rel & 1)
            for a in range(n):
                copies.append(pltpu.make_async_remote_copy(
                    src_ref=send_refs[a].at[4 * px + 2 * py + pc], dst_ref=recv_refs[a].at[me],
                    send_sem=send_sems.at[a * COPIES + rel - 1], recv_sem=recv_sems.at[a * COPIES + rel - 1],
                    device_id=(px, py, pc), device_id_type=MESH_ID))
        for cp in copies:
            cp.start()
        for cp in copies:
            cp.wait()
        for cp in mine:
            cp.wait()

    any_spec = pl.BlockSpec(memory_space=pl.ANY)
    return _pcall(
        body, out_shape=[jax.ShapeDtypeStruct(s.shape, s.dtype) for s in sends],
        in_specs=[any_spec] * n, out_specs=[any_spec] * n,
        scratch_shapes=[pltpu.SemaphoreType.DMA((n * COPIES,)), pltpu.SemaphoreType.DMA((n * COPIES,)),
                        pltpu.SemaphoreType.DMA((n,))],
        name="all_to_all")(*sends)


def _hbm(a):
    return pltpu.with_memory_space_constraint(a, pltpu.HBM)


def _split_copies(send_refs, land_refs, send_sems, recv_sems, local_sems, gather):
    x, y, cc = _my_position()
    me = 4 * x + 2 * y + cc
    local, remote = [], []
    for a, (send, land) in enumerate(zip(send_refs, land_refs)):
        local.append(pltpu.make_async_copy(send if gather else send.at[me], land.at[me], local_sems.at[a]))
    for a, (send, land) in enumerate(zip(send_refs, land_refs)):
        for rel in range(1, N_DEV):
            px, py, pc = x ^ (rel >> 2), y ^ ((rel >> 1) & 1), cc ^ (rel & 1)
            remote.append(pltpu.make_async_remote_copy(
                src_ref=send if gather else send.at[4 * px + 2 * py + pc], dst_ref=land.at[me],
                send_sem=send_sems.at[a * COPIES + rel - 1], recv_sem=recv_sems.at[a * COPIES + rel - 1],
                device_id=(px, py, pc), device_id_type=MESH_ID))
    return local, remote


SPLIT_EFFECT = pltpu.SideEffectType.DATAFLOW_SIDE_EFFECTING


def _exchange_start(sends, after, gather, name):
    n = len(sends)
    lands = [_hbm(lax.empty((N_DEV,) + s.shape if gather else s.shape, s.dtype)) for s in sends]

    def body(*refs):
        send_refs, land_refs = refs[:n], refs[n:2 * n]
        send_sems, recv_sems, local_sems = refs[2 * n + 1:2 * n + 4]
        token = refs[-1]
        local, remote = _split_copies(send_refs, land_refs, send_sems, recv_sems, local_sems, gather)
        for cp in local + remote:
            cp.start()
        token[...] = jnp.zeros_like(token)

    hbm, sem = pl.BlockSpec(memory_space=pltpu.HBM), pl.BlockSpec(memory_space=pltpu.SEMAPHORE)
    out = _pcall(
        body, name=name,
        out_shape=[pltpu.SemaphoreType.DMA((n * COPIES,)), pltpu.SemaphoreType.DMA((n * COPIES,)), pltpu.SemaphoreType.DMA((n,))]
        + [pltpu.HBM(s.shape, s.dtype) for s in sends] + [pltpu.HBM(z.shape, z.dtype) for z in lands]
        + [jax.ShapeDtypeStruct((SUBLANES, LANES), F32)],
        in_specs=[hbm] * (2 * n) + [pl.BlockSpec(memory_space=pl.ANY)],
        out_specs=[sem] * 3 + [hbm] * (2 * n) + [pl.BlockSpec(memory_space=pltpu.VMEM)],
        input_output_aliases={i: 3 + i for i in range(2 * n)},
        compiler_params=pltpu.CompilerParams(has_side_effects=SPLIT_EFFECT),
    )(*[_hbm(s) for s in sends], *lands, after)
    return dict(sems=out[:3], sends=out[3:3 + n], lands=out[3 + n:3 + 2 * n], gather=gather), out[-1]


def _exchange_wait(handle, after, name):
    sends, lands, gather = handle["sends"], handle["lands"], handle["gather"]
    n = len(sends)

    def body(*refs):
        send_refs, land_refs = refs[:n], refs[n:2 * n]
        send_sems, recv_sems, local_sems = refs[2 * n:2 * n + 3]
        local, remote = _split_copies(send_refs, land_refs, send_sems, recv_sems, local_sems, gather)
        for cp in remote:
            cp.wait_send()
            cp.wait_recv()
        for cp in local:
            cp.wait()

    hbm, sem = pl.BlockSpec(memory_space=pltpu.HBM), pl.BlockSpec(memory_space=pltpu.SEMAPHORE)
    out = _pcall(
        body, name=name,
        out_shape=[pltpu.HBM(s.shape, s.dtype) for s in sends] + [pltpu.HBM(z.shape, z.dtype) for z in lands],
        in_specs=[hbm] * (2 * n) + [sem] * 3 + [pl.BlockSpec(memory_space=pl.ANY)],
        out_specs=[hbm] * (2 * n), input_output_aliases={i: i for i in range(2 * n)},
        compiler_params=pltpu.CompilerParams(has_side_effects=SPLIT_EFFECT),
    )(*sends, *lands, *handle["sems"], after)
    return out[n:]


def _join_shards(g, n_local, out_cols, name):
    _, r, wp = g.shape
    tr = min(r, 256)

    def body(g_ref, o_ref, acc_ref):
        acc_ref[...] = jnp.zeros_like(acc_ref)
        for k in range(N_DEV):
            shift = (n_local * k) % LANES
            start = n_local * k - shift
            piece = g_ref[k].astype(F32)
            if shift:
                piece = pltpu.roll(piece, shift, 1)
            acc_ref[:, start:start + wp] += piece
        o_ref[...] = acc_ref[...].astype(BF16)

    return _pcall(
        body, grid=(r // tr,), in_specs=[pl.BlockSpec((N_DEV, tr, wp), lambda i: (0, i, 0))],
        out_specs=pl.BlockSpec((tr, out_cols), lambda i: (i, 0)), out_shape=jax.ShapeDtypeStruct((r, out_cols), BF16),
        scratch_shapes=[pltpu.VMEM((tr, out_cols), F32)], compiler_params=_params("parallel"), name=name)(g)


def _split_shards(full, n_local, wp, name):
    r, c = full.shape
    tr = min(r, 256)

    def body(x_ref, o_ref):
        lane = lax.broadcasted_iota(jnp.int32, (tr, wp), 1)
        for k in range(N_DEV):
            shift = (n_local * k) % LANES
            start = n_local * k - shift
            win = x_ref[:, start:start + wp]
            if shift:
                win = pltpu.roll(win, wp - shift, 1)
            o_ref[k] = jnp.where(lane < n_local, win, 0.0).astype(BF16)

    return _pcall(
        body, grid=(r // tr,), in_specs=[pl.BlockSpec((tr, c), lambda i: (i, 0))],
        out_specs=pl.BlockSpec((N_DEV, tr, wp), lambda i: (0, i, 0)), out_shape=jax.ShapeDtypeStruct((N_DEV, r, wp), BF16),
        compiler_params=_params("parallel"), name=name)(full)


def _sum_and_adamw(recv, w, m, v, name):
    _, r, wp = recv.shape
    c = w.shape[-1]
    lead = w.ndim == 3
    tr = SLAB_ROW_TILE if r % SLAB_ROW_TILE == 0 else (SLAB_ROW_TILE // 4 if r % (SLAB_ROW_TILE // 4) == 0 else r)
    bc1 = 1.0 - ADAM_B1 ** ADAM_STEP
    bc2 = 1.0 - ADAM_B2 ** ADAM_STEP

    def body(recv_ref, w_ref, m_ref, v_ref, g_ref, d_ref, nm_ref, nv_ref):
        g = recv_ref[0, :, 0:c].astype(F32)
        for s in range(1, N_DEV):
            g = g + recv_ref[s, :, 0:c].astype(F32)
        m_new = ADAM_B1 * m_ref[...] + (1.0 - ADAM_B1) * g
        v_new = ADAM_B2 * v_ref[...] + (1.0 - ADAM_B2) * (g * g)
        m_hat = m_new / bc1
        v_hat = v_new / bc2
        g_ref[...] = g
        d_ref[...] = -ADAM_LR * (m_hat / (jnp.sqrt(v_hat) + ADAM_EPS) + ADAM_WD * w_ref[...])
        nm_ref[...] = m_new
        nv_ref[...] = v_new

    tile = pl.BlockSpec((None, tr, c), lambda i: (0, i, 0)) if lead else pl.BlockSpec((tr, c), lambda i: (i, 0))
    return _pcall(
        body, grid=(r // tr,),
        in_specs=[pl.BlockSpec((N_DEV, tr, wp), lambda i: (0, i, 0)), tile, tile, tile],
        out_specs=[tile] * 4, out_shape=[jax.ShapeDtypeStruct(w.shape, F32)] * 4,
        compiler_params=_params("parallel"), name=name)(recv, w, m, v)


SHARDED_TAPS = ("dn_conv_w", "ffn_conv_w")
REPLICATED = ("attn_norm_g", "dn_a_log", "dn_dt_bias", "dn_out_norm_g", "sg_norm_g", "sg_w", "sg_b", "ffn_norm_g",
              "ffn_conv_b", "final_norm_g")
SMALL = SHARDED_TAPS + REPLICATED
WEIGHT_ORDER = ("attn_norm_g", "w_in", "dn_conv_w", "dn_a_log", "dn_dt_bias", "dn_out_norm_g", "sg_norm_g", "sg_w", "sg_b",
                "w_out", "ffn_norm_g", "w_up", "ffn_conv_w", "ffn_conv_b", "w_down", "final_norm_g")
SLAB_COLS = 1024
SLAB_ROW_TILE = 128


def _pad_to(flat, multiple):
    pad = (-flat.shape[-1]) % multiple
    if pad == 0:
        return flat
    return jnp.pad(flat, [(0, 0)] * (flat.ndim - 1) + [(0, pad)])


def _lane_padded(n):
    return -(-n // LANES) * LANES


def _pack_small(named):
    flat = jnp.concatenate([named[n].reshape(-1) for n in SMALL])
    return _pad_to(flat, SUBLANES * SLAB_COLS).reshape(-1, SLAB_COLS)


def _unpack_small(slab, like):
    flat = slab.reshape(-1)
    out, off = {}, 0
    for n in SMALL:
        size = like[n].size
        out[n] = flat[off:off + size].reshape(like[n].shape)
        off += size
    return out


def _split_columns(full, n_local):
    r = full.shape[0]
    return full.reshape(r, N_DEV, n_local).transpose(1, 0, 2).reshape(N_DEV, r * n_local)


def _join_columns(blocks, r, n_local):
    return blocks.reshape(N_DEV, r, n_local).transpose(1, 0, 2).reshape(r, N_DEV * n_local)


def _lanes4(a):
    return jnp.pad(a.reshape(1, N_HEADS), ((0, 0), (0, LANES - N_HEADS)))


def kernel(x, attn_norm_g, w_in, dn_conv_w, dn_a_log, dn_dt_bias, dn_out_norm_g, sg_norm_g, sg_w, sg_b, w_out, ffn_norm_g, w_up, ffn_conv_w, ffn_conv_b, w_down, final_norm_g, loss_target, m_attn_norm_g, m_w_in, m_dn_conv_w, m_dn_a_log, m_dn_dt_bias, m_dn_out_norm_g, m_sg_norm_g, m_sg_w, m_sg_b, m_w_out, m_ffn_norm_g, m_w_up, m_ffn_conv_w, m_ffn_conv_b, m_w_down, m_final_norm_g, v_attn_norm_g, v_w_in, v_dn_conv_w, v_dn_a_log, v_dn_dt_bias, v_dn_out_norm_g, v_sg_norm_g, v_sg_w, v_sg_b, v_w_out, v_ffn_norm_g, v_w_up, v_ffn_conv_w, v_ffn_conv_b, v_w_down, v_final_norm_g):
    weights = dict(attn_norm_g=attn_norm_g, w_in=w_in, dn_conv_w=dn_conv_w, dn_a_log=dn_a_log, dn_dt_bias=dn_dt_bias,
                   dn_out_norm_g=dn_out_norm_g, sg_norm_g=sg_norm_g, sg_w=sg_w, sg_b=sg_b, w_out=w_out, ffn_norm_g=ffn_norm_g,
                   w_up=w_up, ffn_conv_w=ffn_conv_w, ffn_conv_b=ffn_conv_b, w_down=w_down, final_norm_g=final_norm_g)
    m_in = dict(attn_norm_g=m_attn_norm_g, w_in=m_w_in, dn_conv_w=m_dn_conv_w, dn_a_log=m_dn_a_log, dn_dt_bias=m_dn_dt_bias,
                dn_out_norm_g=m_dn_out_norm_g, sg_norm_g=m_sg_norm_g, sg_w=m_sg_w, sg_b=m_sg_b, w_out=m_w_out,
                ffn_norm_g=m_ffn_norm_g, w_up=m_w_up, ffn_conv_w=m_ffn_conv_w, ffn_conv_b=m_ffn_conv_b, w_down=m_w_down,
                final_norm_g=m_final_norm_g)
    v_in = dict(attn_norm_g=v_attn_norm_g, w_in=v_w_in, dn_conv_w=v_dn_conv_w, dn_a_log=v_dn_a_log, dn_dt_bias=v_dn_dt_bias,
                dn_out_norm_g=v_dn_out_norm_g, sg_norm_g=v_sg_norm_g, sg_w=v_sg_w, sg_b=v_sg_b, w_out=v_w_out,
                ffn_norm_g=v_ffn_norm_g, w_up=v_w_up, ffn_conv_w=v_ffn_conv_w, ffn_conv_b=v_ffn_conv_b, w_down=v_w_down,
                final_norm_g=v_final_norm_g)

    n_in, n_up = w_in.shape[2], w_up.shape[2]
    r_out, r_down = w_out.shape[1], w_down.shape[1]
    n_dnc, n_ffc = dn_conv_w.shape[2], ffn_conv_w.shape[2]
    transposed = lambda a: jnp.transpose(a, (0, 2, 1))
    taps = _pad_to(jnp.concatenate([dn_conv_w.reshape(-1), ffn_conv_w.reshape(-1)]), SUBLANES * LANES).reshape(-1, LANES)
    g_in, g_taps = _all_gather([transposed(w_in)[0].astype(BF16), taps])
    late_weights, token = _exchange_start(
        [w_out[0].astype(BF16), transposed(w_up)[0].astype(BF16), w_down[0].astype(BF16)], g_taps, True, "gather_late_start")
    w_in_t = jnp.pad(g_in.reshape(N_DEV * n_in, D_MODEL), ((0, PROJ_PAD - N_DEV * n_in), (0, 0)))
    taps_all = g_taps.reshape(N_DEV, -1)
    dn_conv_full = _join_columns(taps_all[:, :CONV_K * n_dnc], CONV_K, n_dnc)
    ffn_conv_full = _join_columns(taps_all[:, CONV_K * n_dnc:CONV_K * n_dnc + FFN_CONV * n_ffc], FFN_CONV, n_ffc)

    def late(after):
        g_out, g_up, g_down = _exchange_wait(late_weights, after, "gather_late_wait")
        return (g_up.reshape(N_DEV * n_up, D_MODEL),
                g_out.reshape(N_DEV * r_out, D_MODEL), g_down.reshape(N_DEV * r_down, D_MODEL))

    def send_early(blocks, after, name):
        return _exchange_start(blocks, after, False, name)

    loss_lanes, grad_x, g, early = _local_step(
        x[0], loss_target[0], w_in_t, late, send_early, dn_conv_full, ffn_conv_full, attn_norm_g + token[0:1, 0:1],
        dn_a_log, dn_dt_bias, dn_out_norm_g, sg_norm_g, sg_w, sg_b, ffn_norm_g, ffn_conv_b, final_norm_g, n_in)

    small = jnp.concatenate([g[n].reshape(-1) for n in REPLICATED])
    small_send = jnp.concatenate([_split_columns(g["dn_conv_w"], n_dnc), _split_columns(g["ffn_conv_w"], n_ffc),
                                  jnp.broadcast_to(small[None, :], (N_DEV, small.shape[0]))], axis=1)
    small_send = _pad_to(small_send, SUBLANES * SLAB_COLS).reshape(N_DEV, -1, SLAB_COLS)
    r_small, = _all_to_all([small_send])
    r_dn, = _exchange_wait(early[0], r_small, "send_dw_down_wait")
    r_up, r_o = _exchange_wait(early[1], r_small, "send_dw_up_out_wait")
    r_in, = _exchange_wait(early[2], r_small, "send_dw_in_wait")

    upd = {
        "w_in": [transposed(o) for o in _sum_and_adamw(r_in, transposed(w_in), transposed(m_w_in), transposed(v_w_in),
                                                       "adamw_w_in")],
        "w_up": [transposed(o) for o in _sum_and_adamw(r_up, transposed(w_up), transposed(m_w_up), transposed(v_w_up),
                                                       "adamw_w_up")],
        "w_out": _sum_and_adamw(r_o, w_out, m_w_out, v_w_out, "adamw_w_out"),
        "w_down": _sum_and_adamw(r_dn, w_down, m_w_down, v_w_down, "adamw_w_down"),
    }
    small_upd = _sum_and_adamw(r_small, _pack_small(weights), _pack_small(m_in), _pack_small(v_in), "adamw_small")
    results = []
    for i in range(4):
        named = _unpack_small(small_upd[i], weights)
        named.update({n: upd[n][i] for n in upd})
        results.append(named)

    loss = lax.psum(loss_lanes[0, 0], MESH_AXES)
    return (loss, grad_x[None], *[r[n] for r in results for n in WEIGHT_ORDER])


def _local_step(x2d, tgt, w_in_t, late_weights, send_early, dn_conv_full, ffn_conv_full, attn_norm_g, dn_a_log,
                dn_dt_bias, dn_out_norm_g, sg_norm_g, sg_w, sg_b, ffn_norm_g, ffn_conv_b, final_norm_g, n_in):
    t = x2d.shape[0]
    g1, g2, gf = attn_norm_g, ffn_norm_g, final_norm_g.reshape(1, D_MODEL)
    a_log4, dt_bias4 = _lanes4(dn_a_log), _lanes4(dn_dt_bias)
    sg_w3 = sg_w[0]
    sg_b_t = sg_b[0].T
    conv_b = ffn_conv_b

    h1, rstd1 = _rmsnorm_fwd(x2d, g1)
    p = _matmul(h1, w_in_t, "nt", "in_proj", (512, PROJ_PAD, D_MODEL))
    q, k, v, beta4, g4 = _dn_prep(p, dn_conv_full, a_log4, dt_bias4)
    mix_half, s_all = _dn_forward(q, k, v, beta4, g4, p, dn_out_norm_g)
    mix = _sg_forward(p, sg_norm_g, sg_w3, sg_b_t, mix_half)
    w_up_t, w_out_full, w_down_full = late_weights(mix)
    x2 = _matmul(mix, w_out_full, "nn", "out_proj", (1024, 1024, 1024), add=x2d)
    h2, rstd2 = _rmsnorm_fwd(x2, g2)
    up = _matmul(h2, w_up_t, "nt", "up_proj", (512, D_FF, D_MODEL))
    act = _ffn_act(up, ffn_conv_full, conv_b)
    x3 = _matmul(act, w_down_full, "nn", "down_proj", (512, 1024, D_FF), add=x2)
    loss_lanes, dx3, dx3b, d_gf = _final_loss(x3, tgt, gf)

    dact = _matmul(dx3b, w_down_full, "nt", "down_proj_dx", (512, D_FF, D_MODEL))
    d_w_down = _matmul(act, dx3b, "tn", "down_proj_dw", (256, 1024, t), out_dtype=BF16)
    sent_down, token = send_early([d_w_down.reshape(N_DEV, D_FF // N_DEV, D_MODEL)], d_w_down, "send_dw_down")
    dup, d_ffn_conv, d_ffn_conv_b = _ffn_bwd(up, ffn_conv_full, conv_b + token[0:1, 0:1], dact)
    dh2 = _matmul(dup, w_up_t, "nn", "up_proj_dx", (256, 1024, 2 * D_FF))
    d_w_up_t = _matmul(dup, h2, "tn", "up_proj_dw", (512, 1024, t), out_dtype=BF16)
    dx2, dx2b, d_g2 = _rmsnorm_bwd(x2, rstd2, g2, dh2, dx3)
    dmix = _matmul(dx2b, w_out_full, "nt", "out_proj_dx", (1024, 1024, 1024))
    d_w_out = _matmul(mix, dx2b, "tn", "out_proj_dw", (512, 1024, t), out_dtype=BF16)
    sent_up_out, token = send_early(
        [d_w_up_t.reshape(N_DEV, 2 * D_FF // N_DEV, D_MODEL), d_w_out.reshape(N_DEV, D_MODEL // N_DEV, D_MODEL)],
        d_w_out, "send_dw_up_out")
    dp, d_sg_norm, d_sg_w, d_sg_b_t = _sg_backward(p, sg_norm_g + token[0:1, 0:1], sg_w3, sg_b_t, dmix)
    dq, dk, dv, dbeta4, dg4, dp, d_dn_norm = _dn_backward(q, k, v, beta4, g4, p, dn_out_norm_g, s_all, dmix, dp)
    dc_dn, d_dn_conv, dp, d_a_log4, d_dt_bias4 = _dn_prep_bwd(p, dn_conv_full, a_log4, dt_bias4, dq, dk, dv, dbeta4, dg4, dp)
    dp = _conv_bwd_input(dc_dn, dn_conv_full, "dn_conv_dx", out_cols=PROJ_PAD, into=dp)
    d_w_in_t = _matmul(dp, h1, "tn", "in_proj_dw", (PROJ_PAD // 5, 1024, t), out_dtype=BF16)
    sent_in, token = send_early([d_w_in_t[:N_DEV * n_in].reshape(N_DEV, n_in, D_MODEL)], d_w_in_t, "send_dw_in")
    dh1 = _matmul(dp, w_in_t, "nn", "in_proj_dx", (512, 1024, PROJ_PAD))
    grad_x, _, d_g1 = _rmsnorm_bwd(x2d, rstd1, g1 + token[0:1, 0:1], dh1, dx2)

    grads = dict(
        attn_norm_g=d_g1, dn_conv_w=d_dn_conv, dn_a_log=d_a_log4[:, :N_HEADS], dn_dt_bias=d_dt_bias4[:, :N_HEADS],
        dn_out_norm_g=d_dn_norm, sg_norm_g=d_sg_norm, sg_w=d_sg_w, sg_b=d_sg_b_t[:, :SG_GROUPS].T,
        ffn_norm_g=d_g2, ffn_conv_w=d_ffn_conv, ffn_conv_b=d_ffn_conv_b, final_norm_g=d_gf)
    return loss_lanes, grad_x, grads, (sent_down, sent_up_out, sent_in)
```

```python
import math

import jax
import jax.numpy as jnp
from jax import lax
from jax.experimental import pallas as pl
from jax.experimental.pallas import tpu as pltpu

F32 = jnp.float32
BF16 = jnp.bfloat16
HI = lax.Precision.HIGHEST

D_MODEL = 1024
DN_WIDTH = 512
HEAD_DIM = 128
N_HEADS = 4
SG_WIDTH = 512
SG_GROUPS = 4
SG_DIM = 128
SG_BLOCK = 128
D_FF = 2816
CHUNK = 64
CONV_K = 4
FFN_CONV = 3
EPS = 1e-6
PROJ_MAIN = 3072
PROJ_PAD = 3200
GELU_C = math.sqrt(2.0 / math.pi)
N_DEV = 8
LANES = 128
SUBLANES = 8
HALO = SUBLANES
VMEM_LIMIT = 48 * 1024 * 1024

ADAM_LR = 0.001
ADAM_B1 = 0.9
ADAM_B2 = 0.999
ADAM_EPS = 1e-08
ADAM_WD = 0.01
ADAM_STEP = 10

MESH_ID = pl.DeviceIdType.MESH


def _pcall(body, **kw):
    return pl.pallas_call(body, **kw)


def _params(*sem):
    return pltpu.CompilerParams(dimension_semantics=sem, vmem_limit_bytes=VMEM_LIMIT)


def _pick(n, cap):
    best = None
    for t in range(LANES, cap + 1, LANES):
        if n % t == 0:
            best = t
    return best if best else n


FAST, EXACT = "bf16 operands, one pass", "f32 operands, six bf16 passes"


def dot_f32(a, b, dims, tier):
    if tier == FAST:
        return lax.dot_general(a.astype(BF16), b.astype(BF16), dims, preferred_element_type=F32)
    return lax.dot_general(a, b, dims, precision=HI, preferred_element_type=F32)


def dot_nn(a, b, tier=EXACT):
    return dot_f32(a, b, (((1,), (0,)), ((), ())), tier)


def dot_nt(a, b, tier=EXACT):
    return dot_f32(a, b, (((1,), (1,)), ((), ())), tier)


def dot_tn(a, b, tier=EXACT):
    return dot_f32(a, b, (((0,), (0,)), ((), ())), tier)


def sigmoid(x):
    return 1.0 / (1.0 + jnp.exp(-x))


def silu(x):
    return x * sigmoid(x)


def silu_grad(x):
    s = sigmoid(x)
    return s * (1.0 + x * (1.0 - s))


def gelu(x):
    return 0.5 * x * (1.0 + jnp.tanh(GELU_C * (x + 0.044715 * x * x * x)))


def gelu_grad(x):
    t = jnp.tanh(GELU_C * (x + 0.044715 * x * x * x))
    return 0.5 * (1.0 + t) + 0.5 * x * (1.0 - t * t) * GELU_C * (1.0 + 3.0 * 0.044715 * x * x)


def softplus(z):
    return jnp.maximum(z, 0.0) + jnp.log(1.0 + jnp.exp(-jnp.abs(z)))


def rms_fwd(x, g):
    r = lax.rsqrt(jnp.mean(x * x, axis=-1, keepdims=True) + EPS)
    return x * r * g, r


def rms_bwd(x, r, g, dy):
    dyg = dy * g
    xr = x * r
    dx = r * (dyg - xr * jnp.mean(dyg * xr, axis=-1, keepdims=True))
    return dx, dy * xr


def l2_fwd(x):
    r = lax.rsqrt(jnp.sum(x * x, axis=-1, keepdims=True) + EPS)
    return x * r, r


def l2_bwd(x, r, dy):
    xr = x * r
    return r * (dy - xr * jnp.sum(dy * xr, axis=-1, keepdims=True))


def _tri_masks(n):
    row = lax.broadcasted_iota(jnp.int32, (n, n), 0)
    col = lax.broadcasted_iota(jnp.int32, (n, n), 1)
    return row >= col, row > col


def chunk_cumsum(g4):
    incl, _ = _tri_masks(g4.shape[0])
    return dot_nn(incl.astype(F32), g4)


STACK = N_HEADS * CHUNK
DN_FWD_CHUNKS = 8
DN_CHUNKS = 4


def _head_rows(h):
    return slice(h * CHUNK, (h + 1) * CHUNK)


def _stack_heads(x):
    return jnp.concatenate([x[:, h * HEAD_DIM:(h + 1) * HEAD_DIM] for h in range(N_HEADS)], axis=0)


def _stack_lanes(x4):
    return jnp.concatenate([x4[:, h:h + 1] for h in range(N_HEADS)], axis=0)


def _per_head(fn):
    return jnp.concatenate([fn(h) for h in range(N_HEADS)], axis=0)


def _unit_lower_inverses(l_strict, order):
    c = l_strict[0].shape[0]
    row = lax.broadcasted_iota(jnp.int32, (c, c), 0)
    col = lax.broadcasted_iota(jnp.int32, (c, c), 1)
    eye = (row == col).astype(F32)
    p = [-l for l in l_strict]
    a = [eye + n for n in p]
    for _ in range(int(math.log2(order)) - 1):
        p = [dot_nn(x, x, FAST) for x in p]
        a = [x + dot_nn(x, y, FAST) for x, y in zip(a, p)]
    return a


def dn_chunks_local(chunks, inverses=None):
    row = lax.broadcasted_iota(jnp.int32, (STACK, STACK), 0)
    col = lax.broadcasted_iota(jnp.int32, (STACK, STACK), 1)
    same = (row // CHUNK) == (col // CHUNK)
    incl = jnp.logical_and(same, row >= col)
    strict = jnp.logical_and(same, row > col)
    locs = []
    for q, k, v, beta, gc4 in chunks:
        gc_col = _stack_lanes(gc4)
        gc_row = jnp.sum(jnp.where(row == col, gc_col, 0.0), axis=0, keepdims=True)
        decay = jnp.where(incl, jnp.exp(jnp.minimum(gc_col - gc_row, 0.0)), 0.0)
        gamma = jnp.exp(gc_col)
        gc_last = jnp.concatenate([jnp.broadcast_to(gc4[CHUNK - 1:CHUNK, h:h + 1], (CHUNK, 1)) for h in range(N_HEADS)], axis=0)
        tau = jnp.exp(gc_last - gc_col)
        kb = k * beta
        locs.append(dict(decay=decay, gamma=gamma, tau=tau, cd=jnp.exp(gc_last), kb=kb, qd=q * gamma, kt=k * tau,
                         incl=incl, strict=strict))
    for loc, (q, k, v, beta, gc4) in zip(locs, chunks):
        loc["l_mat"] = jnp.where(strict, dot_nt(loc["kb"], k, FAST) * loc["decay"], 0.0)
    if inverses is None:
        inverses = _unit_lower_inverses([loc["l_mat"] for loc in locs], CHUNK)
    for loc, a_inv in zip(locs, inverses):
        loc["a_inv"] = a_inv
    for loc, (q, k, v, beta, gc4) in zip(locs, chunks):
        sol = dot_nn(loc["a_inv"], jnp.concatenate([v * beta, loc["kb"] * loc["gamma"]], axis=1), FAST)
        loc.update(sol=sol, value=sol[:, :HEAD_DIM], kcd=sol[:, HEAD_DIM:])
        loc["attn"] = jnp.where(incl, dot_nt(q, k, FAST) * loc["decay"], 0.0)
    return locs


def dn_chunk_state(loc, s):
    kcd, qd, kt, cd = loc["kcd"], loc["qd"], loc["kt"], loc["cd"]
    v_new = loc["value"] - _per_head(lambda h: dot_nn(kcd[_head_rows(h)], s[h], FAST))
    o = _per_head(lambda h: dot_nn(qd[_head_rows(h)], s[h], FAST)) + dot_nn(loc["attn"], v_new, FAST)
    s_new = [s[h] * cd[h * CHUNK:h * CHUNK + 1, :] + dot_tn(kt[_head_rows(h)], v_new[_head_rows(h)], FAST)
             for h in range(N_HEADS)]
    loc["v_new"] = v_new
    return o, s_new


def dn_chunks_bwd(items, ds_last):
    hr = _head_rows
    n = len(items)
    pre = []
    for q, k, v, beta, loc, s, do in items:
        pre.append(dict(
            dv_part=dot_tn(loc["attn"], do, FAST),
            dattn=jnp.where(loc["incl"], dot_nt(do, loc["v_new"], FAST), 0.0),
            dqd=_per_head(lambda h: dot_nt(do[hr(h)], s[h], FAST)),
            ds_part=[dot_tn(loc["qd"][hr(h)], do[hr(h)], FAST) for h in range(N_HEADS)]))
    ds_new_of, dv_new_of = [None] * n, [None] * n
    ds = ds_last
    for c in reversed(range(n)):
        loc = items[c][4]
        ds_new_of[c] = ds
        dv_new = pre[c]["dv_part"] + _per_head(lambda h: dot_nn(loc["kt"][hr(h)], ds[h], FAST))
        dv_new_of[c] = dv_new
        ds = [pre[c]["ds_part"][h] + ds[h] * loc["cd"][h * CHUNK:h * CHUNK + 1, :]
              - dot_tn(loc["kcd"][hr(h)], dv_new[hr(h)], FAST) for h in range(N_HEADS)]
    is_last = (lax.broadcasted_iota(jnp.int32, (STACK, 1), 0) % CHUNK) == CHUNK - 1
    out = []
    for c, (q, k, v, beta, loc, s, do) in enumerate(items):
        decay, gamma, tau, cd, kb = loc["decay"], loc["gamma"], loc["tau"], loc["cd"], loc["kb"]
        dv_new, ds_new, dattn, dqd = dv_new_of[c], ds_new_of[c], pre[c]["dattn"], pre[c]["dqd"]
        dkt = _per_head(lambda h: dot_nt(loc["v_new"][hr(h)], ds_new[h], FAST))
        dkcd = -_per_head(lambda h: dot_nt(dv_new[hr(h)], s[h], FAST))
        drhs = dot_tn(loc["a_inv"], jnp.concatenate([dv_new, dkcd], axis=1), FAST)
        dvb, dkbg = drhs[:, :HEAD_DIM], drhs[:, HEAD_DIM:]
        dl = jnp.where(loc["strict"], -dot_nt(drhs, loc["sol"], FAST), 0.0)
        dkk = dl * decay
        dqk = dattn * decay
        e = dl * loc["l_mat"] + dattn * loc["attn"]
        dgc = jnp.sum(e, axis=1, keepdims=True) - jnp.sum(e, axis=0, keepdims=True).T
        dkb = dot_nn(dkk, k, FAST) + dkbg * gamma
        dk = dot_tn(dkk, kb, FAST) + dot_tn(dqk, q, FAST) + dkt * tau
        dq = dot_nn(dqk, k, FAST) + dqd * gamma
        dgamma = jnp.sum(dkbg * kb, axis=1, keepdims=True) + jnp.sum(dqd * q, axis=1, keepdims=True)
        dtau_tau = jnp.sum(dkt * k, axis=1, keepdims=True) * tau
        dgc = dgc + dgamma * gamma - dtau_tau

        def last_term(h):
            dcd = jnp.sum(jnp.sum(ds_new[h] * s[h], axis=1, keepdims=True), axis=0, keepdims=True)
            total = jnp.sum(dtau_tau[hr(h)], axis=0, keepdims=True) + dcd * cd[h * CHUNK:h * CHUNK + 1, :]
            return jnp.broadcast_to(total, (CHUNK, 1))

        dgc = dgc + jnp.where(is_last, _per_head(last_term), 0.0)
        dk = dk + dkb * beta
        dbeta = jnp.sum(dkb * k, axis=1, keepdims=True) + jnp.sum(dvb * v, axis=1, keepdims=True)
        out.append((dq, dk, dvb * beta, dbeta, dgc))
    return out, ds


def _token_tile(t):
    return _pick(t, 256)


STRIP = 32


def _for_strips(n_rows, rows, fn):
    def step(r, carry):
        fn(pl.multiple_of(r * rows, rows))
        return carry

    lax.fori_loop(0, n_rows // rows, step, 0)


def _fold_rows(x):
    out = x[0:SUBLANES, :]
    for i in range(1, x.shape[0] // SUBLANES):
        out = out + x[i * SUBLANES:(i + 1) * SUBLANES, :]
    return out


def _rmsnorm_fwd(x, g):
    t, d = x.shape
    tm = _token_tile(t)

    def body(x_ref, g_ref, h_ref, r_ref):
        y, r = rms_fwd(x_ref[...], g_ref[...])
        h_ref[...] = y.astype(BF16)
        r_ref[...] = r

    return _pcall(
        body, grid=(t // tm,),
        in_specs=[pl.BlockSpec((tm, d), lambda i: (i, 0)), pl.BlockSpec((1, d), lambda i: (0, 0))],
        out_specs=[pl.BlockSpec((tm, d), lambda i: (i, 0)), pl.BlockSpec((tm, 1), lambda i: (i, 0))],
        out_shape=[jax.ShapeDtypeStruct((t, d), BF16), jax.ShapeDtypeStruct((t, 1), F32)],
        compiler_params=_params("parallel"), name="rmsnorm_fwd")(x, g)


def _rmsnorm_bwd(x, r, g, dh, dres):
    t, d = x.shape
    tm = _token_tile(t)

    def body(x_ref, r_ref, g_ref, dh_ref, dres_ref, dx_ref, dxb_ref, dg_ref):
        dx, dg_rows = rms_bwd(x_ref[...], r_ref[...], g_ref[...], dh_ref[...])
        dx = dx + dres_ref[...]
        dx_ref[...] = dx
        dxb_ref[...] = dx.astype(BF16)

        @pl.when(pl.program_id(0) == 0)
        def _():
            dg_ref[...] = jnp.zeros_like(dg_ref)

        dg_ref[...] += jnp.sum(dg_rows, axis=0, keepdims=True)

    tile = pl.BlockSpec((tm, d), lambda i: (i, 0))
    row = pl.BlockSpec((1, d), lambda i: (0, 0))
    return _pcall(
        body, grid=(t // tm,),
        in_specs=[tile, pl.BlockSpec((tm, 1), lambda i: (i, 0)), row, tile, tile],
        out_specs=[tile, tile, row],
        out_shape=[jax.ShapeDtypeStruct((t, d), F32), jax.ShapeDtypeStruct((t, d), BF16), jax.ShapeDtypeStruct((1, d), F32)],
        compiler_params=_params("arbitrary"), name="rmsnorm_bwd")(x, r, g, dh, dres)


def _matmul(a, b, mode, name, tiles, add=None, out_dtype=F32):
    if mode == "nn":
        (m, k), n = a.shape, b.shape[1]
    elif mode == "nt":
        (m, k), n = a.shape, b.shape[0]
    else:
        (k, m), n = a.shape, b.shape[1]
    tm, tn, tk = min(tiles[0], m), min(tiles[1], n), min(tiles[2], k)
    assert m % tm == 0 and n % tn == 0 and k % tk == 0, (name, m, n, k, tiles)
    nk = k // tk
    dims = {"nn": (((1,), (0,)), ((), ())), "nt": (((1,), (1,)), ((), ())), "tn": (((0,), (0,)), ((), ()))}[mode]

    def finish(res, add_ref, o_ref):
        if add_ref is not None:
            res = res + add_ref[...]
        o_ref[...] = res.astype(o_ref.dtype)

    def body(*refs):
        a_ref, b_ref = refs[0], refs[1]
        add_ref = refs[2] if add is not None else None
        o_ref = refs[3] if add is not None else refs[2]
        part = lax.dot_general(a_ref[...], b_ref[...], dims, preferred_element_type=F32)
        if nk == 1:
            finish(part, add_ref, o_ref)
            return
        acc_ref = refs[-1]
        kk = pl.program_id(2)

        @pl.when(kk == 0)
        def _():
            acc_ref[...] = part

        @pl.when(kk > 0)
        def _():
            acc_ref[...] += part

        @pl.when(kk == nk - 1)
        def _():
            finish(acc_ref[...], add_ref, o_ref)

    a_spec = pl.BlockSpec((tk, tm), lambda j, i, kk: (kk, i)) if mode == "tn" else pl.BlockSpec((tm, tk), lambda j, i, kk: (i, kk))
    b_spec = pl.BlockSpec((tn, tk), lambda j, i, kk: (j, kk)) if mode == "nt" else pl.BlockSpec((tk, tn), lambda j, i, kk: (kk, j))
    o_spec = pl.BlockSpec((tm, tn), lambda j, i, kk: (i, j))
    in_specs = [a_spec, b_spec] + ([o_spec] if add is not None else [])
    args = (a, b) + ((add,) if add is not None else ())
    return _pcall(
        body, grid=(n // tn, m // tm, nk), in_specs=in_specs, out_specs=o_spec,
        out_shape=jax.ShapeDtypeStruct((m, n), out_dtype),
        scratch_shapes=[pltpu.VMEM((tm, tn), F32)] if nk > 1 else [],
        compiler_params=_params("parallel", "parallel", "arbitrary"), name=name)(*args)


def _prev_halo_spec(tm, width, col_block):
    return pl.BlockSpec((HALO, width), lambda i: (jnp.maximum(i * (tm // HALO) - 1, 0), col_block))


def _fill_with_prev(xp_ref, tile, halo, first):
    xp_ref[0:HALO, :] = jnp.where(first, 0.0, halo)
    xp_ref[HALO:, :] = tile


def _delayed(xp_ref, row0, cols, taps):
    ext = xp_ref[pl.ds(row0, STRIP + HALO), cols]
    return [ext[HALO:, :]] + [pltpu.roll(ext, j, 0)[HALO:, :] for j in range(1, taps)]


def _causal_conv(delayed, w):
    taps = len(delayed)
    out = delayed[0] * w[taps - 1:taps, :]
    for j in range(1, taps):
        out = out + delayed[j] * w[taps - 1 - j:taps - j, :]
    return out


def _advanced_conv(buf_ref, row0, cols, w):
    taps = w.shape[0]
    ext = buf_ref[pl.ds(row0, STRIP + HALO), cols]
    out = ext[:STRIP, :] * w[taps - 1:taps, :]
    for j in range(1, taps):
        out = out + pltpu.roll(ext, STRIP + HALO - j, 0)[:STRIP, :] * w[taps - 1 - j:taps - j, :]
    return out


def _dn_prep(p, conv_w, a_log4, dt_bias4):
    t = p.shape[0]
    tm = _token_tile(t)
    w3 = 3 * DN_WIDTH

    def body(x_ref, halo_ref, pbd_ref, w_ref, alog_ref, dtb_ref, q_ref, k_ref, v_ref, beta_ref, g_ref, xp_ref):
        _fill_with_prev(xp_ref, x_ref[...], halo_ref[...], pl.program_id(0) == 0)

        def strip(row0):
            rows = pl.ds(row0, STRIP)
            for h in range(N_HEADS):
                sl = slice(h * HEAD_DIM, (h + 1) * HEAD_DIM)
                for part, out_ref in ((0, q_ref), (1, k_ref), (2, v_ref)):
                    cols = slice(part * DN_WIDTH + h * HEAD_DIM, part * DN_WIDTH + (h + 1) * HEAD_DIM)
                    y = silu(_causal_conv(_delayed(xp_ref, row0, cols, CONV_K), w_ref[:, cols]))
                    if part == 0:
                        y = l2_fwd(y)[0] * (HEAD_DIM ** -0.5)
                    elif part == 1:
                        y = l2_fwd(y)[0]
                    out_ref[rows, sl] = y
            head = lax.broadcasted_iota(jnp.int32, (STRIP, LANES), 1) < N_HEADS
            pbd = pbd_ref[rows, :]
            beta_ref[rows, :] = jnp.where(head, sigmoid(pbd), 0.0)
            a_raw = pltpu.roll(pbd, LANES - N_HEADS, 1)
            g_ref[rows, :] = jnp.where(head, -jnp.exp(alog_ref[...]) * softplus(a_raw + dtb_ref[...]), 0.0)

        _for_strips(tm, STRIP, strip)

    tok = lambda w, cb: pl.BlockSpec((tm, w), lambda i: (i, cb))
    full = lambda a: pl.BlockSpec(a.shape, lambda i: (0, 0))
    return _pcall(
        body, grid=(t // tm,),
        in_specs=[tok(w3, 0), _prev_halo_spec(tm, w3, 0), tok(LANES, PROJ_MAIN // LANES),
                  full(conv_w), full(a_log4), full(dt_bias4)],
        out_specs=[tok(DN_WIDTH, 0)] * 3 + [tok(LANES, 0)] * 2,
        out_shape=[jax.ShapeDtypeStruct((t, DN_WIDTH), F32)] * 3 + [jax.ShapeDtypeStruct((t, LANES), F32)] * 2,
        scratch_shapes=[pltpu.VMEM((HALO + tm, w3), F32)],
        compiler_params=_params("parallel"), name="dn_prep")(p, p, p, conv_w, a_log4, dt_bias4)


def _dn_prep_bwd(p, conv_w, a_log4, dt_bias4, dq, dk, dv, dbeta4, dg4, dp_buf):
    t = p.shape[0]
    tm = _token_tile(t)
    w3 = 3 * DN_WIDTH

    def body(x_ref, halo_ref, pbd_ref, w_ref, alog_ref, dtb_ref, dq_ref, dk_ref, dv_ref, dbeta_ref, dg_ref, _,
             dc_ref, dw_ref, dpbd_ref, dalog_ref, ddtb_ref, xp_ref, dw_acc, lane_acc):
        first = pl.program_id(0) == 0
        _fill_with_prev(xp_ref, x_ref[...], halo_ref[...], first)
        dw_acc[...] = jnp.zeros_like(dw_acc)
        lane_acc[...] = jnp.zeros_like(lane_acc)

        def strip(row0):
            rows = pl.ds(row0, STRIP)
            for h in range(N_HEADS):
                sl = slice(h * HEAD_DIM, (h + 1) * HEAD_DIM)
                for part, dy_ref in ((0, dq_ref), (1, dk_ref), (2, dv_ref)):
                    cols = slice(part * DN_WIDTH + h * HEAD_DIM, part * DN_WIDTH + (h + 1) * HEAD_DIM)
                    delayed = _delayed(xp_ref, row0, cols, CONV_K)
                    c = _causal_conv(delayed, w_ref[:, cols])
                    dy = dy_ref[rows, sl]
                    if part < 2:
                        y = silu(c)
                        _, r = l2_fwd(y)
                        dy = l2_bwd(y, r, dy * (HEAD_DIM ** -0.5) if part == 0 else dy)
                    dc = dy * silu_grad(c)
                    dc_ref[rows, cols] = dc
                    for j in range(CONV_K):
                        k = CONV_K - 1 - j
                        dw_acc[k * SUBLANES:(k + 1) * SUBLANES, cols] += _fold_rows(dc * delayed[j])
            head = lax.broadcasted_iota(jnp.int32, (STRIP, LANES), 1) < N_HEADS
            pbd = pbd_ref[rows, :]
            beta = sigmoid(pbd)
            dpb = jnp.where(head, dbeta_ref[rows, :] * beta * (1.0 - beta), 0.0)
            z = pltpu.roll(pbd, LANES - N_HEADS, 1) + dtb_ref[...]
            neg_rate = -jnp.exp(alog_ref[...])
            dg = dg_ref[rows, :]
            dpa = jnp.where(head, dg * neg_rate * sigmoid(z), 0.0)
            dpbd_ref[rows, :] = (dpb + pltpu.roll(dpa, N_HEADS, 1)).astype(BF16)
            g = jnp.where(head, neg_rate * softplus(z), 0.0)
            lane_acc[0:SUBLANES, :] += _fold_rows(dg * g)
            lane_acc[SUBLANES:, :] += _fold_rows(dpa)

        _for_strips(tm, STRIP, strip)

        @pl.when(first)
        def _():
            dw_ref[...] = jnp.zeros_like(dw_ref)
            dalog_ref[...] = jnp.zeros_like(dalog_ref)
            ddtb_ref[...] = jnp.zeros_like(ddtb_ref)

        for k in range(CONV_K):
            dw_ref[k:k + 1, :] += jnp.sum(dw_acc[k * SUBLANES:(k + 1) * SUBLANES, :], axis=0, keepdims=True)
        dalog_ref[...] += jnp.sum(lane_acc[0:SUBLANES, :], axis=0, keepdims=True)
        ddtb_ref[...] += jnp.sum(lane_acc[SUBLANES:, :], axis=0, keepdims=True)

    tok = lambda w, cb: pl.BlockSpec((tm, w), lambda i: (i, cb))
    full = lambda shape: pl.BlockSpec(shape, lambda i: (0, 0))
    return _pcall(
        body, grid=(t // tm,),
        in_specs=[tok(w3, 0), _prev_halo_spec(tm, w3, 0), tok(LANES, PROJ_MAIN // LANES),
                  full(conv_w.shape), full(a_log4.shape), full(dt_bias4.shape)] + [tok(DN_WIDTH, 0)] * 3 + [tok(LANES, 0)] * 2
        + [pl.BlockSpec(memory_space=pl.ANY)],
        out_specs=[tok(w3, 0), full((CONV_K, w3)), tok(LANES, PROJ_MAIN // LANES), full((1, LANES)), full((1, LANES))],
        out_shape=[jax.ShapeDtypeStruct((t, w3), F32), jax.ShapeDtypeStruct((CONV_K, w3), F32),
                   jax.ShapeDtypeStruct(dp_buf.shape, dp_buf.dtype),
                   jax.ShapeDtypeStruct((1, LANES), F32), jax.ShapeDtypeStruct((1, LANES), F32)],
        input_output_aliases={11: 2},
        scratch_shapes=[pltpu.VMEM((HALO + tm, w3), F32), pltpu.VMEM((CONV_K * SUBLANES, w3), F32),
                        pltpu.VMEM((2 * SUBLANES, LANES), F32)],
        compiler_params=_params("arbitrary"), name="dn_prep_bwd")(p, p, p, conv_w, a_log4, dt_bias4, dq, dk, dv, dbeta4, dg4, dp_buf)


def _conv_bwd_input(dc, w, name, out_cols=None, col_block=0, into=None):
    t, c = dc.shape
    taps = w.shape[0]
    tm = _token_tile(t)
    ct = _pick(c, 1536)
    n_tok = t // tm
    out_cols = c if out_cols is None else out_cols

    def body(dc_ref, next_ref, w_ref, *rest):
        dx_ref, buf_ref = rest[-2], rest[-1]
        buf_ref[0:tm, :] = dc_ref[...]
        buf_ref[tm:, :] = jnp.where(pl.program_id(0) == n_tok - 1, 0.0, next_ref[...])

        def strip(row0):
            for c0 in range(0, ct, LANES):
                cols = slice(c0, c0 + LANES)
                dx_ref[pl.ds(row0, STRIP), cols] = _advanced_conv(buf_ref, row0, cols, w_ref[:, cols]).astype(BF16)

        _for_strips(tm, STRIP, strip)

    in_specs = [pl.BlockSpec((tm, ct), lambda i, j: (i, j)),
                pl.BlockSpec((HALO, ct), lambda i, j: (jnp.minimum((i + 1) * (tm // HALO), t // HALO - 1), j)),
                pl.BlockSpec((taps, ct), lambda i, j: (0, j))]
    args = (dc, dc, w)
    aliases = {}
    if into is not None:
        in_specs.append(pl.BlockSpec(memory_space=pl.ANY))
        args += (into,)
        aliases = {3: 0}
    return _pcall(
        body, grid=(n_tok, c // ct), in_specs=in_specs,
        out_specs=pl.BlockSpec((tm, ct), lambda i, j: (i, j + col_block)),
        out_shape=jax.ShapeDtypeStruct((t, out_cols), BF16), input_output_aliases=aliases,
        scratch_shapes=[pltpu.VMEM((tm + HALO, ct), F32)],
        compiler_params=_params("parallel", "parallel"), name=name)(*args)


def _dn_forward(q, k, v, beta4, g4, p, norm_g):
    t = q.shape[0]
    n = t // CHUNK
    nc = DN_FWD_CHUNKS
    rows_per_step = nc * CHUNK

    def body(q_ref, k_ref, v_ref, b_ref, g_ref, gate_ref, ng_ref, mix_ref, s_all_ref, ainv_ref, s_ref):
        @pl.when(pl.program_id(0) == 0)
        def _():
            s_ref[...] = jnp.zeros_like(s_ref)

        chunks = []
        for c in range(nc):
            rows = slice(c * CHUNK, (c + 1) * CHUNK)
            chunks.append((_stack_heads(q_ref[rows, :]), _stack_heads(k_ref[rows, :]), _stack_heads(v_ref[rows, :]),
                           _stack_lanes(b_ref[rows, :]), chunk_cumsum(g_ref[rows, :])))
        locs = dn_chunks_local(chunks)
        s = [s_ref[h] for h in range(N_HEADS)]
        for c in range(nc):
            rows = slice(c * CHUNK, (c + 1) * CHUNK)
            ainv_ref[c] = locs[c]["a_inv"].astype(BF16)
            for h in range(N_HEADS):
                s_all_ref[c, h] = s[h]
            o, s = dn_chunk_state(locs[c], s)
            o_n, _ = rms_fwd(o, ng_ref[...])
            for h in range(N_HEADS):
                sl = slice(h * HEAD_DIM, (h + 1) * HEAD_DIM)
                mix_ref[rows, sl] = (o_n[_head_rows(h)] * silu(gate_ref[rows, sl])).astype(BF16)
        for h in range(N_HEADS):
            s_ref[h] = s[h]

    ch = lambda w, cb: pl.BlockSpec((rows_per_step, w), lambda i: (i, cb))
    return _pcall(
        body, grid=(n // nc,),
        in_specs=[ch(DN_WIDTH, 0)] * 3 + [ch(LANES, 0)] * 2 + [ch(DN_WIDTH, 3), pl.BlockSpec((1, HEAD_DIM), lambda i: (0, 0))],
        out_specs=[ch(DN_WIDTH, 0), pl.BlockSpec((nc, N_HEADS, HEAD_DIM, HEAD_DIM), lambda i: (i, 0, 0, 0)),
                   pl.BlockSpec((nc, STACK, STACK), lambda i: (i, 0, 0))],
        out_shape=[jax.ShapeDtypeStruct((t, DN_WIDTH + SG_WIDTH), BF16), jax.ShapeDtypeStruct((n, N_HEADS, HEAD_DIM, HEAD_DIM), F32),
                   jax.ShapeDtypeStruct((n, STACK, STACK), BF16)],
        scratch_shapes=[pltpu.VMEM((N_HEADS, HEAD_DIM, HEAD_DIM), F32)],
        compiler_params=_params("arbitrary"), name="dn_forward")(q, k, v, beta4, g4, p, norm_g)


def _dn_backward(q, k, v, beta4, g4, p, norm_g, s_all, a_inv_all, dmix, dp_buf):
    t = q.shape[0]
    n = t // CHUNK
    steps = n // DN_CHUNKS
    rows_per_step = DN_CHUNKS * CHUNK

    def body(q_ref, k_ref, v_ref, b_ref, g_ref, gate_ref, ng_ref, s_in_ref, ainv_ref, dmix_ref, _,
             dq_ref, dk_ref, dv_ref, db_ref, dg_ref, dgate_ref, dng_ref, ds_ref):
        @pl.when(pl.program_id(0) == 0)
        def _():
            ds_ref[...] = jnp.zeros_like(ds_ref)
            dng_ref[...] = jnp.zeros_like(dng_ref)

        chunks = []
        for c in range(DN_CHUNKS):
            rows = slice(c * CHUNK, (c + 1) * CHUNK)
            chunks.append((_stack_heads(q_ref[rows, :]), _stack_heads(k_ref[rows, :]), _stack_heads(v_ref[rows, :]),
                           _stack_lanes(b_ref[rows, :]), chunk_cumsum(g_ref[rows, :])))
        items = []
        for c, loc in enumerate(dn_chunks_local(chunks, [ainv_ref[c] for c in range(DN_CHUNKS)])):
            rows = slice(c * CHUNK, (c + 1) * CHUNK)
            s = [s_in_ref[c, h] for h in range(N_HEADS)]
            o, _ = dn_chunk_state(loc, s)
            o_n, r = rms_fwd(o, ng_ref[...])
            gate = _stack_heads(gate_ref[rows, :])
            dmx = _stack_heads(dmix_ref[rows, :])
            dgate = dmx * o_n * silu_grad(gate)
            do, dng_rows = rms_bwd(o, r, ng_ref[...], dmx * silu(gate))
            dng_ref[...] += jnp.sum(dng_rows, axis=0, keepdims=True)
            for h in range(N_HEADS):
                dgate_ref[rows, h * HEAD_DIM:(h + 1) * HEAD_DIM] = dgate[_head_rows(h)].astype(BF16)
            items.append((*chunks[c][:4], loc, s, do))
        grads, ds = dn_chunks_bwd(items, [ds_ref[h] for h in range(N_HEADS)])
        lane = lax.broadcasted_iota(jnp.int32, (CHUNK, LANES), 1)
        _, strict = _tri_masks(CHUNK)
        for c in range(DN_CHUNKS):
            rows = slice(c * CHUNK, (c + 1) * CHUNK)
            dq, dk, dv, dbeta, dgc = grads[c]
            db4 = jnp.zeros((CHUNK, LANES), F32)
            dgc4 = jnp.zeros((CHUNK, LANES), F32)
            for h in range(N_HEADS):
                sl = slice(h * HEAD_DIM, (h + 1) * HEAD_DIM)
                head_rows = _head_rows(h)
                dq_ref[rows, sl] = dq[head_rows]
                dk_ref[rows, sl] = dk[head_rows]
                dv_ref[rows, sl] = dv[head_rows]
                db4 = jnp.where(lane == h, dbeta[head_rows], db4)
                dgc4 = jnp.where(lane == h, dgc[head_rows], dgc4)
            db_ref[rows, :] = db4
            dg_ref[rows, :] = dot_nn(jnp.logical_not(strict).astype(F32), dgc4)
        for h in range(N_HEADS):
            ds_ref[h] = ds[h]

    rev = lambda w, cb: pl.BlockSpec((rows_per_step, w), lambda i: (steps - 1 - i, cb))
    return _pcall(
        body, grid=(steps,),
        in_specs=[rev(DN_WIDTH, 0)] * 3 + [rev(LANES, 0)] * 2 + [rev(DN_WIDTH, 3), pl.BlockSpec((1, HEAD_DIM), lambda i: (0, 0)),
                  pl.BlockSpec((DN_CHUNKS, N_HEADS, HEAD_DIM, HEAD_DIM), lambda i: (steps - 1 - i, 0, 0, 0)),
                  pl.BlockSpec((DN_CHUNKS, STACK, STACK), lambda i: (steps - 1 - i, 0, 0)), rev(DN_WIDTH, 0),
                  pl.BlockSpec(memory_space=pl.ANY)],
        out_specs=[rev(DN_WIDTH, 0)] * 3 + [rev(LANES, 0)] * 2 + [rev(DN_WIDTH, 3), pl.BlockSpec((1, HEAD_DIM), lambda i: (0, 0))],
        out_shape=[jax.ShapeDtypeStruct((t, DN_WIDTH), F32)] * 3 + [jax.ShapeDtypeStruct((t, LANES), F32)] * 2
        + [jax.ShapeDtypeStruct(dp_buf.shape, dp_buf.dtype), jax.ShapeDtypeStruct((1, HEAD_DIM), F32)],
        input_output_aliases={10: 5},
        scratch_shapes=[pltpu.VMEM((N_HEADS, HEAD_DIM, HEAD_DIM), F32)],
        compiler_params=_params("arbitrary"), name="dn_backward")(q, k, v, beta4, g4, p, norm_g, s_all, a_inv_all, dmix, dp_buf)


def _sg_mask():
    row = lax.broadcasted_iota(jnp.int32, (SG_BLOCK, SG_BLOCK), 0)
    col = lax.broadcasted_iota(jnp.int32, (SG_BLOCK, SG_BLOCK), 1)
    return (col // CHUNK) <= (row // CHUNK)


def _sg_forward(p, norm_g, w_s, b_t, mix_buf):
    t = p.shape[0]

    def body(u_ref, v_ref, ng_ref, w_ref, b_ref, _, o_ref):
        mask = _sg_mask()
        for g in range(SG_GROUPS):
            sl = slice(g * SG_DIM, (g + 1) * SG_DIM)
            vn, _ = rms_fwd(gelu(v_ref[:, sl]), ng_ref[:, sl])
            s = dot_nn(jnp.where(mask, w_ref[g], 0.0), vn, FAST) + b_ref[:, g:g + 1]
            o_ref[:, sl] = (gelu(u_ref[:, sl]) * s).astype(BF16)

    blk = lambda cb: pl.BlockSpec((SG_BLOCK, SG_WIDTH), lambda i: (i, cb))
    return _pcall(
        body, grid=(t // SG_BLOCK,),
        in_specs=[blk(4), blk(5), pl.BlockSpec((1, SG_WIDTH), lambda i: (0, 0)),
                  pl.BlockSpec((SG_GROUPS, SG_BLOCK, SG_BLOCK), lambda i: (0, 0, 0)), pl.BlockSpec((SG_BLOCK, SG_GROUPS), lambda i: (0, 0)),
                  pl.BlockSpec(memory_space=pl.ANY)],
        out_specs=blk(1), out_shape=jax.ShapeDtypeStruct(mix_buf.shape, mix_buf.dtype), input_output_aliases={5: 0},
        compiler_params=_params("parallel"), name="sg_forward")(p, p, norm_g, w_s, b_t, mix_buf)


def _sg_backward(p, norm_g, w_s, b_t, dmix):
    t = p.shape[0]

    def body(u_ref, v_ref, ng_ref, w_ref, b_ref, do_ref, duv_ref, dng_ref, dw_ref, db_ref):
        @pl.when(pl.program_id(0) == 0)
        def _():
            dng_ref[...] = jnp.zeros_like(dng_ref)
            dw_ref[...] = jnp.zeros_like(dw_ref)
            db_ref[...] = jnp.zeros_like(db_ref)

        mask = _sg_mask()
        lane = lax.broadcasted_iota(jnp.int32, (SG_BLOCK, LANES), 1)
        db = jnp.zeros((SG_BLOCK, LANES), F32)
        for g in range(SG_GROUPS):
            sl = slice(g * SG_DIM, (g + 1) * SG_DIM)
            u_raw, v_raw, do = u_ref[:, sl], v_ref[:, sl], do_ref[:, sl]
            vg = gelu(v_raw)
            vn, r = rms_fwd(vg, ng_ref[:, sl])
            w_m = jnp.where(mask, w_ref[g], 0.0)
            s = dot_nn(w_m, vn, FAST) + b_ref[:, g:g + 1]
            duv_ref[:, sl] = (do * s * gelu_grad(u_raw)).astype(BF16)
            ds = do * gelu(u_raw)
            db = jnp.where(lane == g, jnp.sum(ds, axis=1, keepdims=True), db)
            dw_ref[g] += jnp.where(mask, dot_nt(ds, vn, FAST), 0.0)
            dvg, dng_rows = rms_bwd(vg, r, ng_ref[:, sl], dot_tn(w_m, ds, FAST))
            dng_ref[:, sl] += jnp.sum(dng_rows, axis=0, keepdims=True)
            duv_ref[:, SG_WIDTH + g * SG_DIM:SG_WIDTH + (g + 1) * SG_DIM] = (dvg * gelu_grad(v_raw)).astype(BF16)
        db_ref[...] += db

    blk = lambda cb: pl.BlockSpec((SG_BLOCK, SG_WIDTH), lambda i: (i, cb))
    const2 = lambda shape: pl.BlockSpec(shape, lambda i: (0, 0))
    w_spec = pl.BlockSpec((SG_GROUPS, SG_BLOCK, SG_BLOCK), lambda i: (0, 0, 0))
    return _pcall(
        body, grid=(t // SG_BLOCK,),
        in_specs=[blk(4), blk(5), const2((1, SG_WIDTH)), w_spec, const2((SG_BLOCK, SG_GROUPS)), blk(1)],
        out_specs=[pl.BlockSpec((SG_BLOCK, 2 * SG_WIDTH), lambda i: (i, 2)), const2((1, SG_WIDTH)), w_spec,
                   const2((SG_BLOCK, LANES))],
        out_shape=[jax.ShapeDtypeStruct((t, PROJ_PAD), BF16), jax.ShapeDtypeStruct((1, SG_WIDTH), F32),
                   jax.ShapeDtypeStruct((SG_GROUPS, SG_BLOCK, SG_BLOCK), F32), jax.ShapeDtypeStruct((SG_BLOCK, LANES), F32)],
        compiler_params=_params("arbitrary"), name="sg_backward")(p, p, norm_g, w_s, b_t, dmix)


FFN_CT = D_FF // 2


def _ffn_act(up, conv_w, conv_b):
    t = up.shape[0]
    tm = _token_tile(t)
    nj = D_FF // FFN_CT

    def body(ug_ref, uv_ref, hg_ref, hv_ref, wg_ref, wv_ref, bg_ref, bv_ref, act_ref, xg_ref, xv_ref):
        first = pl.program_id(0) == 0
        _fill_with_prev(xg_ref, ug_ref[...], hg_ref[...], first)
        _fill_with_prev(xv_ref, uv_ref[...], hv_ref[...], first)

        def strip(row0):
            for c0 in range(0, FFN_CT, LANES):
                cols = slice(c0, c0 + LANES)
                cg = _causal_conv(_delayed(xg_ref, row0, cols, FFN_CONV), wg_ref[:, cols]) + bg_ref[:, cols]
                cv = _causal_conv(_delayed(xv_ref, row0, cols, FFN_CONV), wv_ref[:, cols]) + bv_ref[:, cols]
                act_ref[pl.ds(row0, STRIP), cols] = (silu(cg) * cv).astype(BF16)

        _for_strips(tm, STRIP, strip)

    tok = lambda off: pl.BlockSpec((tm, FFN_CT), lambda i, j: (i, j + off))
    halo = lambda off: pl.BlockSpec((HALO, FFN_CT), lambda i, j: (jnp.maximum(i * (tm // HALO) - 1, 0), j + off))
    par = lambda rows, off: pl.BlockSpec((rows, FFN_CT), lambda i, j: (0, j + off))
    return _pcall(
        body, grid=(t // tm, nj),
        in_specs=[tok(0), tok(nj), halo(0), halo(nj), par(FFN_CONV, 0), par(FFN_CONV, nj), par(1, 0), par(1, nj)],
        out_specs=pl.BlockSpec((tm, FFN_CT), lambda i, j: (i, j)),
        out_shape=jax.ShapeDtypeStruct((t, D_FF), BF16),
        scratch_shapes=[pltpu.VMEM((HALO + tm, FFN_CT), F32)] * 2,
        compiler_params=_params("parallel", "parallel"), name="ffn_act")(up, up, up, up, conv_w, conv_w, conv_b, conv_b)


def _ffn_bwd(up, conv_w, conv_b, dact):
    t = up.shape[0]
    tm = _pick(t, 128)
    n_tok = t // tm
    width = 2 * D_FF

    def dconv(delayed_g, delayed_v, da, wg, wv, bg, bv):
        cg = _causal_conv(delayed_g, wg) + bg
        cv = _causal_conv(delayed_v, wv) + bv
        s = sigmoid(cg)
        return da * cv * (s * (1.0 + cg * (1.0 - s))), da * (cg * s)

    def body(up_ref, prev_ref, next_ref, da_ref, dan_ref, w_ref, b_ref, dup_ref, dw_ref, db_ref, xp_ref, dc_ref, dw_acc, db_acc):
        first = pl.program_id(0) == 0
        last = pl.program_id(0) == n_tok - 1
        xp_ref[0:HALO, :] = jnp.where(first, 0.0, prev_ref[...])
        xp_ref[HALO:HALO + tm, :] = up_ref[...]
        xp_ref[HALO + tm:, :] = next_ref[...]
        dw_acc[...] = jnp.zeros_like(dw_acc)
        db_acc[...] = jnp.zeros_like(db_acc)

        def strip(row0):
            rows = pl.ds(row0, STRIP)
            for c0 in range(0, D_FF, LANES):
                gc, vc = slice(c0, c0 + LANES), slice(D_FF + c0, D_FF + c0 + LANES)
                del_g, del_v = _delayed(xp_ref, row0, gc, FFN_CONV), _delayed(xp_ref, row0, vc, FFN_CONV)
                dcg, dcv = dconv(del_g, del_v, da_ref[rows, gc], w_ref[:, gc], w_ref[:, vc], b_ref[:, gc], b_ref[:, vc])
                dc_ref[rows, gc] = dcg
                dc_ref[rows, vc] = dcv
                db_acc[:, gc] += _fold_rows(dcg)
                db_acc[:, vc] += _fold_rows(dcv)
                for j in range(FFN_CONV):
                    k = FFN_CONV - 1 - j
                    dw_acc[k * SUBLANES:(k + 1) * SUBLANES, gc] += _fold_rows(dcg * del_g[j])
                    dw_acc[k * SUBLANES:(k + 1) * SUBLANES, vc] += _fold_rows(dcv * del_v[j])

        _for_strips(tm, STRIP, strip)

        for c0 in range(0, D_FF, LANES):
            gc, vc = slice(c0, c0 + LANES), slice(D_FF + c0, D_FF + c0 + LANES)

            def delayed(cols):
                ext = xp_ref[tm:tm + 2 * HALO, cols]
                return [ext[HALO:, :]] + [pltpu.roll(ext, j, 0)[HALO:, :] for j in range(1, FFN_CONV)]

            dcg, dcv = dconv(delayed(gc), delayed(vc), dan_ref[:, gc], w_ref[:, gc], w_ref[:, vc], b_ref[:, gc], b_ref[:, vc])
            dc_ref[tm:, gc] = jnp.where(last, 0.0, dcg)
            dc_ref[tm:, vc] = jnp.where(last, 0.0, dcv)

        def strip_dx(row0):
            for c0 in range(0, width, LANES):
                cols = slice(c0, c0 + LANES)
                dup_ref[pl.ds(row0, STRIP), cols] = _advanced_conv(dc_ref, row0, cols, w_ref[:, cols]).astype(BF16)

        _for_strips(tm, STRIP, strip_dx)

        @pl.when(first)
        def _():
            dw_ref[...] = jnp.zeros_like(dw_ref)
            db_ref[...] = jnp.zeros_like(db_ref)

        for k in range(FFN_CONV):
            dw_ref[k:k + 1, :] += jnp.sum(dw_acc[k * SUBLANES:(k + 1) * SUBLANES, :], axis=0, keepdims=True)
        db_ref[...] += jnp.sum(db_acc[...], axis=0, keepdims=True)

    next_rows = lambda i: jnp.minimum((i + 1) * (tm // HALO), t // HALO - 1)
    full = lambda rows: pl.BlockSpec((rows, width), lambda i: (0, 0))
    return _pcall(
        body, grid=(n_tok,),
        in_specs=[pl.BlockSpec((tm, width), lambda i: (i, 0)),
                  pl.BlockSpec((HALO, width), lambda i: (jnp.maximum(i * (tm // HALO) - 1, 0), 0)),
                  pl.BlockSpec((HALO, width), lambda i: (next_rows(i), 0)),
                  pl.BlockSpec((tm, D_FF), lambda i: (i, 0)), pl.BlockSpec((HALO, D_FF), lambda i: (next_rows(i), 0)),
                  full(FFN_CONV), full(1)],
        out_specs=[pl.BlockSpec((tm, width), lambda i: (i, 0)), full(FFN_CONV), full(1)],
        out_shape=[jax.ShapeDtypeStruct((t, width), BF16), jax.ShapeDtypeStruct((FFN_CONV, width), F32),
                   jax.ShapeDtypeStruct((1, width), F32)],
        scratch_shapes=[pltpu.VMEM((tm + 2 * HALO, width), F32), pltpu.VMEM((tm + HALO, width), F32),
                        pltpu.VMEM((FFN_CONV * SUBLANES, width), F32), pltpu.VMEM((SUBLANES, width), F32)],
        compiler_params=_params("arbitrary"), name="ffn_bwd")(up, up, up, dact, dact, conv_w, conv_b)


def _final_loss(x3, target, g):
    t, d = x3.shape
    tm = _token_tile(t)

    def body(x_ref, t_ref, g_ref, loss_ref, dx_ref, dxb_ref, dg_ref):
        @pl.when(pl.program_id(0) == 0)
        def _():
            loss_ref[...] = jnp.zeros_like(loss_ref)
            dg_ref[...] = jnp.zeros_like(dg_ref)

        x = x_ref[...]
        y, r = rms_fwd(x, g_ref[...])
        err = y - t_ref[...]
        per_tok = jnp.mean(err * err, axis=-1, keepdims=True)
        loss_ref[...] += 0.5 * jnp.sum(per_tok, axis=0, keepdims=True)
        dx, dg_rows = rms_bwd(x, r, g_ref[...], err * (1.0 / d))
        dx_ref[...] = dx
        dxb_ref[...] = dx.astype(BF16)
        dg_ref[...] += jnp.sum(dg_rows, axis=0, keepdims=True)

    tile = pl.BlockSpec((tm, d), lambda i: (i, 0))
    row = pl.BlockSpec((1, d), lambda i: (0, 0))
    return _pcall(
        body, grid=(t // tm,), in_specs=[tile, tile, row],
        out_specs=[pl.BlockSpec((1, LANES), lambda i: (0, 0)), tile, tile, row],
        out_shape=[jax.ShapeDtypeStruct((1, LANES), F32), jax.ShapeDtypeStruct((t, d), F32), jax.ShapeDtypeStruct((t, d), BF16),
                   jax.ShapeDtypeStruct((1, d), F32)],
        compiler_params=_params("arbitrary"), name="final_loss")(x3, target, g)


def _my_position():
    return lax.axis_index("x"), lax.axis_index("y"), lax.axis_index("c")


COPIES = N_DEV - 1


def _all_gather(arrays):
    n = len(arrays)

    def body(*refs):
        x_refs, out_refs = refs[:n], refs[n:2 * n]
        send_sems, recv_sems, local_sems = refs[2 * n:]
        x, y, cc = _my_position()
        me, sibling = (x, y, cc), (x, y, 1 - cc)
        chips = [(1 - x, y), (x, 1 - y), (1 - x, 1 - y)]

        def block(a, px, py, pc):
            return out_refs[a].at[4 * px + 2 * py + pc]

        def copy(a, k, blk, to, src=None):
            return pltpu.make_async_remote_copy(
                src_ref=block(a, *blk) if src is None else src, dst_ref=block(a, *blk),
                send_sem=send_sems.at[a * COPIES + k], recv_sem=recv_sems.at[a * COPIES + k],
                device_id=to, device_id_type=MESH_ID)

        mine = [pltpu.make_async_copy(x_refs[a], block(a, *me), local_sems.at[a]) for a in range(n)]
        for cp in mine:
            cp.start()
        first = []
        for a in range(n):
            first.append(copy(a, 0, me, sibling, src=x_refs[a]))
            first += [copy(a, 1 + j, me, (*chip, cc), src=x_refs[a]) for j, chip in enumerate(chips)]
        for cp in first:
            cp.start()
        passed = []
        for j, chip in enumerate(chips):
            for a in range(n):
                copy(a, 1 + j, (*chip, cc), me).wait_recv()
                passed.append(copy(a, 4 + j, (*chip, cc), sibling))
                passed[-1].start()
        for a in range(n):
            copy(a, 0, sibling, me).wait_recv()
        for j, chip in enumerate(chips):
            for a in range(n):
                copy(a, 4 + j, (*chip, 1 - cc), me).wait_recv()
        for cp in first + passed:
            cp.wait_send()
        for cp in mine:
            cp.wait()

    any_spec = pl.BlockSpec(memory_space=pl.ANY)
    return _pcall(
        body, out_shape=[jax.ShapeDtypeStruct((N_DEV,) + a.shape, a.dtype) for a in arrays],
        in_specs=[any_spec] * n, out_specs=[any_spec] * n,
        scratch_shapes=[pltpu.SemaphoreType.DMA((n * COPIES,)), pltpu.SemaphoreType.DMA((n * COPIES,)),
                        pltpu.SemaphoreType.DMA((n,))],
        name="all_gather")(*arrays)


def _all_to_all(sends):
    n = len(sends)

    def body(*refs):
        send_refs, recv_refs = refs[:n], refs[n:2 * n]
        send_sems, recv_sems, local_sems = refs[2 * n:]
        x, y, cc = _my_position()
        me = 4 * x + 2 * y + cc
        mine = [pltpu.make_async_copy(send_refs[a].at[me], recv_refs[a].at[me], local_sems.at[a]) for a in range(n)]
        for cp in mine:
            cp.start()
        copies = []
        for rel in range(1, N_DEV):
            px, py, pc = x ^ (rel >> 2), y ^ ((rel >> 1) & 1), cc ^ (rel & 1)
            for a in range(n):
                copies.append(pltpu.make_async_remote_copy(
                    src_ref=send_refs[a].at[4 * px + 2 * py + pc], dst_ref=recv_refs[a].at[me],
                    send_sem=send_sems.at[a * COPIES + rel - 1], recv_sem=recv_sems.at[a * COPIES + rel - 1],
                    device_id=(px, py, pc), device_id_type=MESH_ID))
        for cp in copies:
            cp.start()
        for cp in copies:
            cp.wait()
        for cp in mine:
            cp.wait()

    any_spec = pl.BlockSpec(memory_space=pl.ANY)
    return _pcall(
        body, out_shape=[jax.ShapeDtypeStruct(s.shape, s.dtype) for s in sends],
        in_specs=[any_spec] * n, out_specs=[any_spec] * n,
        scratch_shapes=[pltpu.SemaphoreType.DMA((n * COPIES,)), pltpu.SemaphoreType.DMA((n * COPIES,)),
                        pltpu.SemaphoreType.DMA((n,))],
        name="all_to_all")(*sends)


def _hbm(a):
    return pltpu.with_memory_space_constraint(a, pltpu.HBM)


def _split_copies(send_refs, land_refs, send_sems, recv_sems, local_sems, gather):
    x, y, cc = _my_position()
    me = 4 * x + 2 * y + cc
    local, remote = [], []
    for a, (send, land) in enumerate(zip(send_refs, land_refs)):
        local.append(pltpu.make_async_copy(send if gather else send.at[me], land.at[me], local_sems.at[a]))
    for a, (send, land) in enumerate(zip(send_refs, land_refs)):
        for rel in range(1, N_DEV):
            px, py, pc = x ^ (rel >> 2), y ^ ((rel >> 1) & 1), cc ^ (rel & 1)
            remote.append(pltpu.make_async_remote_copy(
                src_ref=send if gather else send.at[4 * px + 2 * py + pc], dst_ref=land.at[me],
                send_sem=send_sems.at[a * COPIES + rel - 1], recv_sem=recv_sems.at[a * COPIES + rel - 1],
                device_id=(px, py, pc), device_id_type=MESH_ID))
    return local, remote


SPLIT_EFFECT = pltpu.SideEffectType.DATAFLOW_SIDE_EFFECTING


def _exchange_start(sends, after, gather, name):
    n = len(sends)
    lands = [_hbm(lax.empty((N_DEV,) + s.shape if gather else s.shape, s.dtype)) for s in sends]

    def body(*refs):
        send_refs, land_refs = refs[:n], refs[n:2 * n]
        send_sems, recv_sems, local_sems = refs[2 * n + 1:2 * n + 4]
        token = refs[-1]
        local, remote = _split_copies(send_refs, land_refs, send_sems, recv_sems, local_sems, gather)
        for cp in local + remote:
            cp.start()
        token[...] = jnp.zeros_like(token)

    hbm, sem = pl.BlockSpec(memory_space=pltpu.HBM), pl.BlockSpec(memory_space=pltpu.SEMAPHORE)
    out = _pcall(
        body, name=name,
        out_shape=[pltpu.SemaphoreType.DMA((n * COPIES,)), pltpu.SemaphoreType.DMA((n * COPIES,)), pltpu.SemaphoreType.DMA((n,))]
        + [pltpu.HBM(s.shape, s.dtype) for s in sends] + [pltpu.HBM(z.shape, z.dtype) for z in lands]
        + [jax.ShapeDtypeStruct((SUBLANES, LANES), F32)],
        in_specs=[hbm] * (2 * n) + [pl.BlockSpec(memory_space=pl.ANY)],
        out_specs=[sem] * 3 + [hbm] * (2 * n) + [pl.BlockSpec(memory_space=pltpu.VMEM)],
        input_output_aliases={i: 3 + i for i in range(2 * n)},
        compiler_params=pltpu.CompilerParams(has_side_effects=SPLIT_EFFECT),
    )(*[_hbm(s) for s in sends], *lands, after)
    return dict(sems=out[:3], sends=out[3:3 + n], lands=out[3 + n:3 + 2 * n], gather=gather), out[-1]


def _exchange_wait(handle, after, name):
    sends, lands, gather = handle["sends"], handle["lands"], handle["gather"]
    n = len(sends)

    def body(*refs):
        send_refs, land_refs = refs[:n], refs[n:2 * n]
        send_sems, recv_sems, local_sems = refs[2 * n:2 * n + 3]
        local, remote = _split_copies(send_refs, land_refs, send_sems, recv_sems, local_sems, gather)
        for cp in remote:
            cp.wait_send()
            cp.wait_recv()
        for cp in local:
            cp.wait()

    hbm, sem = pl.BlockSpec(memory_space=pltpu.HBM), pl.BlockSpec(memory_space=pltpu.SEMAPHORE)
    out = _pcall(
        body, name=name,
        out_shape=[pltpu.HBM(s.shape, s.dtype) for s in sends] + [pltpu.HBM(z.shape, z.dtype) for z in lands],
        in_specs=[hbm] * (2 * n) + [sem] * 3 + [pl.BlockSpec(memory_space=pl.ANY)],
        out_specs=[hbm] * (2 * n), input_output_aliases={i: i for i in range(2 * n)},
        compiler_params=pltpu.CompilerParams(has_side_effects=SPLIT_EFFECT),
    )(*sends, *lands, *handle["sems"], after)
    return out[n:]


def _sum_and_adamw(recv, w, m, v, name):
    _, r, wp = recv.shape
    c = w.shape[-1]
    lead = w.ndim == 3
    tr = SLAB_ROW_TILE if r % SLAB_ROW_TILE == 0 else (SLAB_ROW_TILE // 4 if r % (SLAB_ROW_TILE // 4) == 0 else r)
    bc1 = 1.0 - ADAM_B1 ** ADAM_STEP
    bc2 = 1.0 - ADAM_B2 ** ADAM_STEP

    def body(recv_ref, w_ref, m_ref, v_ref, g_ref, d_ref, nm_ref, nv_ref):
        g = recv_ref[0, :, 0:c].astype(F32)
        for s in range(1, N_DEV):
            g = g + recv_ref[s, :, 0:c].astype(F32)
        m_new = ADAM_B1 * m_ref[...] + (1.0 - ADAM_B1) * g
        v_new = ADAM_B2 * v_ref[...] + (1.0 - ADAM_B2) * (g * g)
        m_hat = m_new / bc1
        v_hat = v_new / bc2
        g_ref[...] = g
        d_ref[...] = -ADAM_LR * (m_hat / (jnp.sqrt(v_hat) + ADAM_EPS) + ADAM_WD * w_ref[...])
        nm_ref[...] = m_new
        nv_ref[...] = v_new

    tile = pl.BlockSpec((None, tr, c), lambda i: (0, i, 0)) if lead else pl.BlockSpec((tr, c), lambda i: (i, 0))
    return _pcall(
        body, grid=(r // tr,),
        in_specs=[pl.BlockSpec((N_DEV, tr, wp), lambda i: (0, i, 0)), tile, tile, tile],
        out_specs=[tile] * 4, out_shape=[jax.ShapeDtypeStruct(w.shape, F32)] * 4,
        compiler_params=_params("parallel"), name=name)(recv, w, m, v)


SHARDED_TAPS = ("dn_conv_w", "ffn_conv_w")
REPLICATED = ("attn_norm_g", "dn_a_log", "dn_dt_bias", "dn_out_norm_g", "sg_norm_g", "sg_w", "sg_b", "ffn_norm_g",
              "ffn_conv_b", "final_norm_g")
SMALL = SHARDED_TAPS + REPLICATED
WEIGHT_ORDER = ("attn_norm_g", "w_in", "dn_conv_w", "dn_a_log", "dn_dt_bias", "dn_out_norm_g", "sg_norm_g", "sg_w", "sg_b",
                "w_out", "ffn_norm_g", "w_up", "ffn_conv_w", "ffn_conv_b", "w_down", "final_norm_g")
SLAB_COLS = 1024
SLAB_ROW_TILE = 128


def _pad_to(flat, multiple):
    pad = (-flat.shape[-1]) % multiple
    if pad == 0:
        return flat
    return jnp.pad(flat, [(0, 0)] * (flat.ndim - 1) + [(0, pad)])


def _pack_small(named):
    flat = jnp.concatenate([named[n].reshape(-1) for n in SMALL])
    return _pad_to(flat, SUBLANES * SLAB_COLS).reshape(-1, SLAB_COLS)


def _unpack_small(slab, like):
    flat = slab.reshape(-1)
    out, off = {}, 0
    for n in SMALL:
        size = like[n].size
        out[n] = flat[off:off + size].reshape(like[n].shape)
        off += size
    return out


def _split_columns(full, n_local):
    r = full.shape[0]
    return full.reshape(r, N_DEV, n_local).transpose(1, 0, 2).reshape(N_DEV, r * n_local)


def _join_columns(blocks, r, n_local):
    return blocks.reshape(N_DEV, r, n_local).transpose(1, 0, 2).reshape(r, N_DEV * n_local)


def _lanes4(a):
    return jnp.pad(a.reshape(1, N_HEADS), ((0, 0), (0, LANES - N_HEADS)))


def kernel(x, attn_norm_g, w_in, dn_conv_w, dn_a_log, dn_dt_bias, dn_out_norm_g, sg_norm_g, sg_w, sg_b, w_out, ffn_norm_g, w_up, ffn_conv_w, ffn_conv_b, w_down, final_norm_g, loss_target, m_attn_norm_g, m_w_in, m_dn_conv_w, m_dn_a_log, m_dn_dt_bias, m_dn_out_norm_g, m_sg_norm_g, m_sg_w, m_sg_b, m_w_out, m_ffn_norm_g, m_w_up, m_ffn_conv_w, m_ffn_conv_b, m_w_down, m_final_norm_g, v_attn_norm_g, v_w_in, v_dn_conv_w, v_dn_a_log, v_dn_dt_bias, v_dn_out_norm_g, v_sg_norm_g, v_sg_w, v_sg_b, v_w_out, v_ffn_norm_g, v_w_up, v_ffn_conv_w, v_ffn_conv_b, v_w_down, v_final_norm_g):
    weights = dict(attn_norm_g=attn_norm_g, w_in=w_in, dn_conv_w=dn_conv_w, dn_a_log=dn_a_log, dn_dt_bias=dn_dt_bias,
                   dn_out_norm_g=dn_out_norm_g, sg_norm_g=sg_norm_g, sg_w=sg_w, sg_b=sg_b, w_out=w_out, ffn_norm_g=ffn_norm_g,
                   w_up=w_up, ffn_conv_w=ffn_conv_w, ffn_conv_b=ffn_conv_b, w_down=w_down, final_norm_g=final_norm_g)
    m_in = dict(attn_norm_g=m_attn_norm_g, w_in=m_w_in, dn_conv_w=m_dn_conv_w, dn_a_log=m_dn_a_log, dn_dt_bias=m_dn_dt_bias,
                dn_out_norm_g=m_dn_out_norm_g, sg_norm_g=m_sg_norm_g, sg_w=m_sg_w, sg_b=m_sg_b, w_out=m_w_out,
                ffn_norm_g=m_ffn_norm_g, w_up=m_w_up, ffn_conv_w=m_ffn_conv_w, ffn_conv_b=m_ffn_conv_b, w_down=m_w_down,
                final_norm_g=m_final_norm_g)
    v_in = dict(attn_norm_g=v_attn_norm_g, w_in=v_w_in, dn_conv_w=v_dn_conv_w, dn_a_log=v_dn_a_log, dn_dt_bias=v_dn_dt_bias,
                dn_out_norm_g=v_dn_out_norm_g, sg_norm_g=v_sg_norm_g, sg_w=v_sg_w, sg_b=v_sg_b, w_out=v_w_out,
                ffn_norm_g=v_ffn_norm_g, w_up=v_w_up, ffn_conv_w=v_ffn_conv_w, ffn_conv_b=v_ffn_conv_b, w_down=v_w_down,
                final_norm_g=v_final_norm_g)

    n_in, n_up = w_in.shape[2], w_up.shape[2]
    r_out, r_down = w_out.shape[1], w_down.shape[1]
    n_dnc, n_ffc = dn_conv_w.shape[2], ffn_conv_w.shape[2]
    transposed = lambda a: jnp.transpose(a, (0, 2, 1))
    taps = _pad_to(jnp.concatenate([dn_conv_w.reshape(-1), ffn_conv_w.reshape(-1)]), SUBLANES * LANES).reshape(-1, LANES)
    g_in, g_taps = _all_gather([transposed(w_in)[0].astype(BF16), taps])
    late_weights, token = _exchange_start(
        [w_out[0].astype(BF16), transposed(w_up)[0].astype(BF16), w_down[0].astype(BF16)], g_taps, True, "gather_late_start")
    w_in_t = jnp.pad(g_in.reshape(N_DEV * n_in, D_MODEL), ((0, PROJ_PAD - N_DEV * n_in), (0, 0)))
    taps_all = g_taps.reshape(N_DEV, -1)
    dn_conv_full = _join_columns(taps_all[:, :CONV_K * n_dnc], CONV_K, n_dnc)
    ffn_conv_full = _join_columns(taps_all[:, CONV_K * n_dnc:CONV_K * n_dnc + FFN_CONV * n_ffc], FFN_CONV, n_ffc)

    def late(after):
        g_out, g_up, g_down = _exchange_wait(late_weights, after, "gather_late_wait")
        return (g_up.reshape(N_DEV * n_up, D_MODEL),
                g_out.reshape(N_DEV * r_out, D_MODEL), g_down.reshape(N_DEV * r_down, D_MODEL))

    def send_early(blocks, after, name):
        return _exchange_start(blocks, after, False, name)

    loss_lanes, grad_x, g, early = _local_step(
        x[0], loss_target[0], w_in_t, late, send_early, dn_conv_full, ffn_conv_full, attn_norm_g + token[0:1, 0:1],
        dn_a_log, dn_dt_bias, dn_out_norm_g, sg_norm_g, sg_w, sg_b, ffn_norm_g, ffn_conv_b, final_norm_g, n_in)

    small = jnp.concatenate([g[n].reshape(-1) for n in REPLICATED] + [loss_lanes[0, 0:1]])
    small_send = jnp.concatenate([_split_columns(g["dn_conv_w"], n_dnc), _split_columns(g["ffn_conv_w"], n_ffc),
                                  jnp.broadcast_to(small[None, :], (N_DEV, small.shape[0]))], axis=1)
    small_send = _pad_to(small_send, SUBLANES * SLAB_COLS).reshape(N_DEV, -1, SLAB_COLS)
    r_small, = _all_to_all([small_send])
    r_dn, = _exchange_wait(early[0], r_small, "send_dw_down_wait")
    r_up, r_o = _exchange_wait(early[1], r_small, "send_dw_up_out_wait")
    r_in, = _exchange_wait(early[2], r_small, "send_dw_in_wait")

    upd = {
        "w_in": [transposed(o) for o in _sum_and_adamw(r_in, transposed(w_in), transposed(m_w_in), transposed(v_w_in),
                                                       "adamw_w_in")],
        "w_up": [transposed(o) for o in _sum_and_adamw(r_up, transposed(w_up), transposed(m_w_up), transposed(v_w_up),
                                                       "adamw_w_up")],
        "w_out": _sum_and_adamw(r_o, w_out, m_w_out, v_w_out, "adamw_w_out"),
        "w_down": _sum_and_adamw(r_dn, w_down, m_w_down, v_w_down, "adamw_w_down"),
    }
    small_upd = _sum_and_adamw(r_small, _pack_small(weights), _pack_small(m_in), _pack_small(v_in), "adamw_small")
    results = []
    for i in range(4):
        named = _unpack_small(small_upd[i], weights)
        named.update({n: upd[n][i] for n in upd})
        results.append(named)

    loss = small_upd[0].reshape(-1)[sum(weights[n].size for n in SMALL)]
    return (loss, grad_x[None], *[r[n] for r in results for n in WEIGHT_ORDER])


def _local_step(x2d, tgt, w_in_t, late_weights, send_early, dn_conv_full, ffn_conv_full, attn_norm_g, dn_a_log,
                dn_dt_bias, dn_out_norm_g, sg_norm_g, sg_w, sg_b, ffn_norm_g, ffn_conv_b, final_norm_g, n_in):
    t = x2d.shape[0]
    g1, g2, gf = attn_norm_g, ffn_norm_g, final_norm_g.reshape(1, D_MODEL)
    a_log4, dt_bias4 = _lanes4(dn_a_log), _lanes4(dn_dt_bias)
    sg_w3 = sg_w[0]
    sg_b_t = sg_b[0].T
    conv_b = ffn_conv_b

    h1, rstd1 = _rmsnorm_fwd(x2d, g1)
    p = _matmul(h1, w_in_t, "nt", "in_proj", (512, PROJ_PAD, D_MODEL))
    q, k, v, beta4, g4 = _dn_prep(p, dn_conv_full, a_log4, dt_bias4)
    mix_half, s_all, a_inv_all = _dn_forward(q, k, v, beta4, g4, p, dn_out_norm_g)
    mix = _sg_forward(p, sg_norm_g, sg_w3, sg_b_t, mix_half)
    w_up_t, w_out_full, w_down_full = late_weights(mix)
    x2 = _matmul(mix, w_out_full, "nn", "out_proj", (1024, 1024, 1024), add=x2d)
    h2, rstd2 = _rmsnorm_fwd(x2, g2)
    up = _matmul(h2, w_up_t, "nt", "up_proj", (512, D_FF, D_MODEL))
    act = _ffn_act(up, ffn_conv_full, conv_b)
    x3 = _matmul(act, w_down_full, "nn", "down_proj", (512, 1024, D_FF), add=x2)
    loss_lanes, dx3, dx3b, d_gf = _final_loss(x3, tgt, gf)

    dact = _matmul(dx3b, w_down_full, "nt", "down_proj_dx", (512, D_FF, D_MODEL))
    d_w_down = _matmul(act, dx3b, "tn", "down_proj_dw", (256, 1024, t), out_dtype=BF16)
    sent_down, token = send_early([d_w_down.reshape(N_DEV, D_FF // N_DEV, D_MODEL)], d_w_down, "send_dw_down")
    dup, d_ffn_conv, d_ffn_conv_b = _ffn_bwd(up, ffn_conv_full, conv_b + token[0:1, 0:1], dact)
    dh2 = _matmul(dup, w_up_t, "nn", "up_proj_dx", (256, 1024, 2 * D_FF))
    d_w_up_t = _matmul(dup, h2, "tn", "up_proj_dw", (512, 1024, t), out_dtype=BF16)
    dx2, dx2b, d_g2 = _rmsnorm_bwd(x2, rstd2, g2, dh2, dx3)
    dmix = _matmul(dx2b, w_out_full, "nt", "out_proj_dx", (1024, 1024, 1024))
    d_w_out = _matmul(mix, dx2b, "tn", "out_proj_dw", (512, 1024, t), out_dtype=BF16)
    sent_up_out, token = send_early(
        [d_w_up_t.reshape(N_DEV, 2 * D_FF // N_DEV, D_MODEL), d_w_out.reshape(N_DEV, D_MODEL // N_DEV, D_MODEL)],
        d_w_out, "send_dw_up_out")
    dp, d_sg_norm, d_sg_w, d_sg_b_t = _sg_backward(p, sg_norm_g + token[0:1, 0:1], sg_w3, sg_b_t, dmix)
    dq, dk, dv, dbeta4, dg4, dp, d_dn_norm = _dn_backward(q, k, v, beta4, g4, p, dn_out_norm_g, s_all, a_inv_all, dmix, dp)
    dc_dn, d_dn_conv, dp, d_a_log4, d_dt_bias4 = _dn_prep_bwd(p, dn_conv_full, a_log4, dt_bias4, dq, dk, dv, dbeta4, dg4, dp)
    dp = _conv_bwd_input(dc_dn, dn_conv_full, "dn_conv_dx", out_cols=PROJ_PAD, into=dp)
    d_w_in_t = _matmul(dp, h1, "tn", "in_proj_dw", (PROJ_PAD // 5, 1024, t), out_dtype=BF16)
    sent_in, token = send_early([d_w_in_t[:N_DEV * n_in].reshape(N_DEV, n_in, D_MODEL)], d_w_in_t, "send_dw_in")
    dh1 = _matmul(dp, w_in_t, "nn", "in_proj_dx", (512, 1024, PROJ_PAD))
    grad_x, _, d_g1 = _rmsnorm_bwd(x2d, rstd1, g1 + token[0:1, 0:1], dh1, dx2)

    grads = dict(
        attn_norm_g=d_g1, dn_conv_w=d_dn_conv, dn_a_log=d_a_log4[:, :N_HEADS], dn_dt_bias=d_dt_bias4[:, :N_HEADS],
        dn_out_norm_g=d_dn_norm, sg_norm_g=d_sg_norm, sg_w=d_sg_w, sg_b=d_sg_b_t[:, :SG_GROUPS].T,
        ffn_norm_g=d_g2, ffn_conv_w=d_ffn_conv, ffn_conv_b=d_ffn_conv_b, final_norm_g=d_gf)
    return loss_lanes, grad_x, grads, (sent_down, sent_up_out, sent_in)
```

```python
import math

import jax
import jax.numpy as jnp
from jax import lax
from jax.experimental import pallas as pl
from jax.experimental.pallas import tpu as pltpu

F32 = jnp.float32
BF16 = jnp.bfloat16
HI = lax.Precision.HIGHEST

D_MODEL = 1024
DN_WIDTH = 512
HEAD_DIM = 128
N_HEADS = 4
SG_WIDTH = 512
SG_GROUPS = 4
SG_DIM = 128
SG_BLOCK = 128
D_FF = 2816
CHUNK = 64
CONV_K = 4
FFN_CONV = 3
EPS = 1e-6
PROJ_MAIN = 3072
PROJ_PAD = 3200
GELU_C = math.sqrt(2.0 / math.pi)
N_DEV = 8
LANES = 128
SUBLANES = 8
HALO = SUBLANES
VMEM_LIMIT = 48 * 1024 * 1024

ADAM_LR = 0.001
ADAM_B1 = 0.9
ADAM_B2 = 0.999
ADAM_EPS = 1e-08
ADAM_WD = 0.01
ADAM_STEP = 10

MESH_ID = pl.DeviceIdType.MESH


def _pcall(body, **kw):
    return pl.pallas_call(body, **kw)


def _params(*sem):
    return pltpu.CompilerParams(dimension_semantics=sem, vmem_limit_bytes=VMEM_LIMIT)


def _pick(n, cap):
    best = None
    for t in range(LANES, cap + 1, LANES):
        if n % t == 0:
            best = t
    return best if best else n


FAST, EXACT = "bf16 operands, one pass", "f32 operands, six bf16 passes"


def dot_f32(a, b, dims, tier):
    if tier == FAST:
        return lax.dot_general(a.astype(BF16), b.astype(BF16), dims, preferred_element_type=F32)
    return lax.dot_general(a, b, dims, precision=HI, preferred_element_type=F32)


def dot_nn(a, b, tier=EXACT):
    return dot_f32(a, b, (((1,), (0,)), ((), ())), tier)


def dot_nt(a, b, tier=EXACT):
    return dot_f32(a, b, (((1,), (1,)), ((), ())), tier)


def dot_tn(a, b, tier=EXACT):
    return dot_f32(a, b, (((0,), (0,)), ((), ())), tier)


def sigmoid(x):
    return 1.0 / (1.0 + jnp.exp(-x))


def silu(x):
    return x * sigmoid(x)


def silu_grad(x):
    s = sigmoid(x)
    return s * (1.0 + x * (1.0 - s))


def gelu(x):
    return 0.5 * x * (1.0 + jnp.tanh(GELU_C * (x + 0.044715 * x * x * x)))


def gelu_grad(x):
    t = jnp.tanh(GELU_C * (x + 0.044715 * x * x * x))
    return 0.5 * (1.0 + t) + 0.5 * x * (1.0 - t * t) * GELU_C * (1.0 + 3.0 * 0.044715 * x * x)


def softplus(z):
    return jnp.maximum(z, 0.0) + jnp.log(1.0 + jnp.exp(-jnp.abs(z)))


def rms_fwd(x, g):
    r = lax.rsqrt(jnp.mean(x * x, axis=-1, keepdims=True) + EPS)
    return x * r * g, r


def rms_bwd(x, r, g, dy):
    dyg = dy * g
    xr = x * r
    dx = r * (dyg - xr * jnp.mean(dyg * xr, axis=-1, keepdims=True))
    return dx, dy * xr


def l2_fwd(x):
    r = lax.rsqrt(jnp.sum(x * x, axis=-1, keepdims=True) + EPS)
    return x * r, r


def l2_bwd(x, r, dy):
    xr = x * r
    return r * (dy - xr * jnp.sum(dy * xr, axis=-1, keepdims=True))


def _tri_masks(n):
    row = lax.broadcasted_iota(jnp.int32, (n, n), 0)
    col = lax.broadcasted_iota(jnp.int32, (n, n), 1)
    return row >= col, row > col


def chunk_cumsum(g4):
    incl, _ = _tri_masks(g4.shape[0])
    return dot_nn(incl.astype(F32), g4)


STACK = N_HEADS * CHUNK
DN_FWD_CHUNKS = 8
DN_CHUNKS = 4


def _head_rows(h):
    return slice(h * CHUNK, (h + 1) * CHUNK)


def _stack_heads(x):
    return jnp.concatenate([x[:, h * HEAD_DIM:(h + 1) * HEAD_DIM] for h in range(N_HEADS)], axis=0)


def _stack_lanes(x4):
    return jnp.concatenate([x4[:, h:h + 1] for h in range(N_HEADS)], axis=0)


def _per_head(fn):
    return jnp.concatenate([fn(h) for h in range(N_HEADS)], axis=0)


def _unit_lower_inverses(l_strict, order):
    c = l_strict[0].shape[0]
    row = lax.broadcasted_iota(jnp.int32, (c, c), 0)
    col = lax.broadcasted_iota(jnp.int32, (c, c), 1)
    eye = (row == col).astype(F32)
    p = [-l for l in l_strict]
    a = [eye + n for n in p]
    for _ in range(int(math.log2(order)) - 1):
        p = [dot_nn(x, x, FAST) for x in p]
        a = [x + dot_nn(x, y, FAST) for x, y in zip(a, p)]
    return a


def dn_chunks_local(chunks, inverses=None):
    row = lax.broadcasted_iota(jnp.int32, (STACK, STACK), 0)
    col = lax.broadcasted_iota(jnp.int32, (STACK, STACK), 1)
    same = (row // CHUNK) == (col // CHUNK)
    incl = jnp.logical_and(same, row >= col)
    strict = jnp.logical_and(same, row > col)
    locs = []
    for q, k, v, beta, gc4 in chunks:
        gc_col = _stack_lanes(gc4)
        gc_row = jnp.sum(jnp.where(row == col, gc_col, 0.0), axis=0, keepdims=True)
        decay = jnp.where(incl, jnp.exp(jnp.minimum(gc_col - gc_row, 0.0)), 0.0)
        gamma = jnp.exp(gc_col)
        gc_last = jnp.concatenate([jnp.broadcast_to(gc4[CHUNK - 1:CHUNK, h:h + 1], (CHUNK, 1)) for h in range(N_HEADS)], axis=0)
        tau = jnp.exp(gc_last - gc_col)
        kb = k * beta
        locs.append(dict(decay=decay, gamma=gamma, tau=tau, cd=jnp.exp(gc_last), kb=kb, qd=q * gamma, kt=k * tau,
                         incl=incl, strict=strict))
    for loc, (q, k, v, beta, gc4) in zip(locs, chunks):
        loc["l_mat"] = jnp.where(strict, dot_nt(loc["kb"], k, FAST) * loc["decay"], 0.0)
    if inverses is None:
        inverses = _unit_lower_inverses([loc["l_mat"] for loc in locs], CHUNK)
    for loc, a_inv in zip(locs, inverses):
        loc["a_inv"] = a_inv
    for loc, (q, k, v, beta, gc4) in zip(locs, chunks):
        sol = dot_nn(loc["a_inv"], jnp.concatenate([v * beta, loc["kb"] * loc["gamma"]], axis=1), FAST)
        loc.update(sol=sol, value=sol[:, :HEAD_DIM], kcd=sol[:, HEAD_DIM:])
        loc["attn"] = jnp.where(incl, dot_nt(q, k, FAST) * loc["decay"], 0.0)
    return locs


def dn_chunk_state(loc, s):
    kcd, qd, kt, cd = loc["kcd"], loc["qd"], loc["kt"], loc["cd"]
    v_new = loc["value"] - _per_head(lambda h: dot_nn(kcd[_head_rows(h)], s[h], FAST))
    o = _per_head(lambda h: dot_nn(qd[_head_rows(h)], s[h], FAST)) + dot_nn(loc["attn"], v_new, FAST)
    s_new = [s[h] * cd[h * CHUNK:h * CHUNK + 1, :] + dot_tn(kt[_head_rows(h)], v_new[_head_rows(h)], FAST)
             for h in range(N_HEADS)]
    loc["v_new"] = v_new
    return o, s_new


def dn_chunks_bwd(items, ds_last):
    hr = _head_rows
    n = len(items)
    pre = []
    for q, k, v, beta, loc, s, do in items:
        pre.append(dict(
            dv_part=dot_tn(loc["attn"], do, FAST),
            dattn=jnp.where(loc["incl"], dot_nt(do, loc["v_new"], FAST), 0.0),
            dqd=_per_head(lambda h: dot_nt(do[hr(h)], s[h], FAST)),
            ds_part=[dot_tn(loc["qd"][hr(h)], do[hr(h)], FAST) for h in range(N_HEADS)]))
    ds_new_of, dv_new_of = [None] * n, [None] * n
    ds = ds_last
    for c in reversed(range(n)):
        loc = items[c][4]
        ds_new_of[c] = ds
        dv_new = pre[c]["dv_part"] + _per_head(lambda h: dot_nn(loc["kt"][hr(h)], ds[h], FAST))
        dv_new_of[c] = dv_new
        ds = [pre[c]["ds_part"][h] + ds[h] * loc["cd"][h * CHUNK:h * CHUNK + 1, :]
              - dot_tn(loc["kcd"][hr(h)], dv_new[hr(h)], FAST) for h in range(N_HEADS)]
    is_last = (lax.broadcasted_iota(jnp.int32, (STACK, 1), 0) % CHUNK) == CHUNK - 1
    out = []
    for c, (q, k, v, beta, loc, s, do) in enumerate(items):
        decay, gamma, tau, cd, kb = loc["decay"], loc["gamma"], loc["tau"], loc["cd"], loc["kb"]
        dv_new, ds_new, dattn, dqd = dv_new_of[c], ds_new_of[c], pre[c]["dattn"], pre[c]["dqd"]
        dkt = _per_head(lambda h: dot_nt(loc["v_new"][hr(h)], ds_new[h], FAST))
        dkcd = -_per_head(lambda h: dot_nt(dv_new[hr(h)], s[h], FAST))
        drhs = dot_tn(loc["a_inv"], jnp.concatenate([dv_new, dkcd], axis=1), FAST)
        dvb, dkbg = drhs[:, :HEAD_DIM], drhs[:, HEAD_DIM:]
        dl = jnp.where(loc["strict"], -dot_nt(drhs, loc["sol"], FAST), 0.0)
        dkk = dl * decay
        dqk = dattn * decay
        e = dl * loc["l_mat"] + dattn * loc["attn"]
        dgc = jnp.sum(e, axis=1, keepdims=True) - jnp.sum(e, axis=0, keepdims=True).T
        dkb = dot_nn(dkk, k, FAST) + dkbg * gamma
        dk = dot_tn(dkk, kb, FAST) + dot_tn(dqk, q, FAST) + dkt * tau
        dq = dot_nn(dqk, k, FAST) + dqd * gamma
        dgamma = jnp.sum(dkbg * kb, axis=1, keepdims=True) + jnp.sum(dqd * q, axis=1, keepdims=True)
        dtau_tau = jnp.sum(dkt * k, axis=1, keepdims=True) * tau
        dgc = dgc + dgamma * gamma - dtau_tau

        def last_term(h):
            dcd = jnp.sum(jnp.sum(ds_new[h] * s[h], axis=1, keepdims=True), axis=0, keepdims=True)
            total = jnp.sum(dtau_tau[hr(h)], axis=0, keepdims=True) + dcd * cd[h * CHUNK:h * CHUNK + 1, :]
            return jnp.broadcast_to(total, (CHUNK, 1))

        dgc = dgc + jnp.where(is_last, _per_head(last_term), 0.0)
        dk = dk + dkb * beta
        dbeta = jnp.sum(dkb * k, axis=1, keepdims=True) + jnp.sum(dvb * v, axis=1, keepdims=True)
        out.append((dq, dk, dvb * beta, dbeta, dgc))
    return out, ds


def _token_tile(t):
    return _pick(t, 256)


STRIP = 32


def _for_strips(n_rows, rows, fn):
    def step(r, carry):
        fn(pl.multiple_of(r * rows, rows))
        return carry

    lax.fori_loop(0, n_rows // rows, step, 0)


def _fold_rows(x):
    out = x[0:SUBLANES, :]
    for i in range(1, x.shape[0] // SUBLANES):
        out = out + x[i * SUBLANES:(i + 1) * SUBLANES, :]
    return out


def _matmul(a, b, mode, name, tiles, add=None, out_dtype=F32):
    if mode == "nn":
        (m, k), n = a.shape, b.shape[1]
    elif mode == "nt":
        (m, k), n = a.shape, b.shape[0]
    else:
        (k, m), n = a.shape, b.shape[1]
    tm, tn, tk = min(tiles[0], m), min(tiles[1], n), min(tiles[2], k)
    assert m % tm == 0 and n % tn == 0 and k % tk == 0, (name, m, n, k, tiles)
    nk = k // tk
    dims = {"nn": (((1,), (0,)), ((), ())), "nt": (((1,), (1,)), ((), ())), "tn": (((0,), (0,)), ((), ()))}[mode]

    def finish(res, add_ref, o_ref):
        if add_ref is not None:
            res = res + add_ref[...]
        o_ref[...] = res.astype(o_ref.dtype)

    def body(*refs):
        a_ref, b_ref = refs[0], refs[1]
        add_ref = refs[2] if add is not None else None
        o_ref = refs[3] if add is not None else refs[2]
        part = lax.dot_general(a_ref[...], b_ref[...], dims, preferred_element_type=F32)
        if nk == 1:
            finish(part, add_ref, o_ref)
            return
        acc_ref = refs[-1]
        kk = pl.program_id(2)

        @pl.when(kk == 0)
        def _():
            acc_ref[...] = part

        @pl.when(kk > 0)
        def _():
            acc_ref[...] += part

        @pl.when(kk == nk - 1)
        def _():
            finish(acc_ref[...], add_ref, o_ref)

    a_spec = pl.BlockSpec((tk, tm), lambda j, i, kk: (kk, i)) if mode == "tn" else pl.BlockSpec((tm, tk), lambda j, i, kk: (i, kk))
    b_spec = pl.BlockSpec((tn, tk), lambda j, i, kk: (j, kk)) if mode == "nt" else pl.BlockSpec((tk, tn), lambda j, i, kk: (kk, j))
    o_spec = pl.BlockSpec((tm, tn), lambda j, i, kk: (i, j))
    in_specs = [a_spec, b_spec] + ([o_spec] if add is not None else [])
    args = (a, b) + ((add,) if add is not None else ())
    return _pcall(
        body, grid=(n // tn, m // tm, nk), in_specs=in_specs, out_specs=o_spec,
        out_shape=jax.ShapeDtypeStruct((m, n), out_dtype),
        scratch_shapes=[pltpu.VMEM((tm, tn), F32)] if nk > 1 else [],
        compiler_params=_params("parallel", "parallel", "arbitrary"), name=name)(*args)


def _matmul_rows(a, b, mode, name, tm, extra, outs, fn):
    m, k = a.shape
    n = b.shape[1] if mode == "nn" else b.shape[0]
    tm = min(tm, m)
    dims = (((1,), (0,)), ((), ())) if mode == "nn" else (((1,), (1,)), ((), ()))

    def spec(shape, kind):
        if kind == "rows":
            return pl.BlockSpec((tm, shape[1]), lambda i: (i, 0))
        return pl.BlockSpec(shape, lambda i: (0,) * len(shape))

    def body(a_ref, b_ref, *refs):
        rows = lax.dot_general(a_ref[...], b_ref[...], dims, preferred_element_type=F32)
        fn(rows, pl.program_id(0) == 0, *refs)

    return _pcall(
        body, grid=(m // tm,),
        in_specs=[pl.BlockSpec((tm, k), lambda i: (i, 0)), pl.BlockSpec(b.shape, lambda i: (0, 0))]
        + [spec(x.shape, kind) for x, kind in extra],
        out_specs=[spec(shape, kind) for shape, _, kind in outs],
        out_shape=[jax.ShapeDtypeStruct(shape, dtype) for shape, dtype, _ in outs],
        compiler_params=_params("arbitrary"), name=name)(a, b, *[x for x, _ in extra])


def _rmsnorm_matmul(x, g, b_t, name, tm):
    t, d = x.shape
    n = b_t.shape[0]
    tm = min(tm, t)

    def body(x_ref, g_ref, b_ref, o_ref, h_ref, r_ref):
        y, r = rms_fwd(x_ref[...], g_ref[...])
        h = y.astype(BF16)
        h_ref[...] = h
        r_ref[...] = r
        o_ref[...] = lax.dot_general(h, b_ref[...], (((1,), (1,)), ((), ())), preferred_element_type=F32)

    rows = lambda w: pl.BlockSpec((tm, w), lambda i: (i, 0))
    return _pcall(
        body, grid=(t // tm,),
        in_specs=[rows(d), pl.BlockSpec((1, d), lambda i: (0, 0)), pl.BlockSpec((n, d), lambda i: (0, 0))],
        out_specs=[rows(n), rows(d), rows(1)],
        out_shape=[jax.ShapeDtypeStruct((t, n), F32), jax.ShapeDtypeStruct((t, d), BF16), jax.ShapeDtypeStruct((t, 1), F32)],
        compiler_params=_params("parallel"), name=name)(x, g, b_t)


def _rmsnorm_bwd_rows(t, d):
    def fn(dh, first, x_ref, r_ref, g_ref, dres_ref, dx_ref, dxb_ref, dg_ref):
        dx, dg_rows = rms_bwd(x_ref[...], r_ref[...], g_ref[...], dh)
        dx = dx + dres_ref[...]
        dx_ref[...] = dx
        dxb_ref[...] = dx.astype(BF16)

        @pl.when(first)
        def _():
            dg_ref[...] = jnp.zeros_like(dg_ref)

        dg_ref[...] += jnp.sum(dg_rows, axis=0, keepdims=True)

    return fn, [((t, d), F32, "rows"), ((t, d), BF16, "rows"), ((1, d), F32, "whole")]


def _final_loss_rows(t, d):
    def fn(rows, first, res_ref, t_ref, g_ref, loss_ref, dx_ref, dxb_ref, dg_ref):
        @pl.when(first)
        def _():
            loss_ref[...] = jnp.zeros_like(loss_ref)
            dg_ref[...] = jnp.zeros_like(dg_ref)

        x = rows + res_ref[...]
        y, r = rms_fwd(x, g_ref[...])
        err = y - t_ref[...]
        loss_ref[...] += 0.5 * jnp.sum(jnp.mean(err * err, axis=-1, keepdims=True), axis=0, keepdims=True)
        dx, dg_rows = rms_bwd(x, r, g_ref[...], err * (1.0 / d))
        dx_ref[...] = dx
        dxb_ref[...] = dx.astype(BF16)
        dg_ref[...] += jnp.sum(dg_rows, axis=0, keepdims=True)

    return fn, [((1, LANES), F32, "whole"), ((t, d), F32, "rows"), ((t, d), BF16, "rows"), ((1, d), F32, "whole")]


def _prev_halo_spec(tm, width, col_block):
    return pl.BlockSpec((HALO, width), lambda i: (jnp.maximum(i * (tm // HALO) - 1, 0), col_block))


def _fill_with_prev(xp_ref, tile, halo, first):
    xp_ref[0:HALO, :] = jnp.where(first, 0.0, halo)
    xp_ref[HALO:, :] = tile


def _delayed(xp_ref, row0, cols, taps):
    ext = xp_ref[pl.ds(row0, STRIP + HALO), cols]
    return [ext[HALO:, :]] + [pltpu.roll(ext, j, 0)[HALO:, :] for j in range(1, taps)]


def _causal_conv(delayed, w):
    taps = len(delayed)
    out = delayed[0] * w[taps - 1:taps, :]
    for j in range(1, taps):
        out = out + delayed[j] * w[taps - 1 - j:taps - j, :]
    return out


def _advanced_conv(buf_ref, row0, cols, w):
    taps = w.shape[0]
    ext = buf_ref[pl.ds(row0, STRIP + HALO), cols]
    out = ext[:STRIP, :] * w[taps - 1:taps, :]
    for j in range(1, taps):
        out = out + pltpu.roll(ext, STRIP + HALO - j, 0)[:STRIP, :] * w[taps - 1 - j:taps - j, :]
    return out


def _dn_prep(p, conv_w, a_log4, dt_bias4):
    t = p.shape[0]
    tm = _token_tile(t)
    w3 = 3 * DN_WIDTH

    def body(x_ref, halo_ref, pbd_ref, w_ref, alog_ref, dtb_ref, q_ref, k_ref, v_ref, beta_ref, g_ref, xp_ref):
        _fill_with_prev(xp_ref, x_ref[...], halo_ref[...], pl.program_id(0) == 0)

        def strip(row0):
            rows = pl.ds(row0, STRIP)
            for h in range(N_HEADS):
                sl = slice(h * HEAD_DIM, (h + 1) * HEAD_DIM)
                for part, out_ref in ((0, q_ref), (1, k_ref), (2, v_ref)):
                    cols = slice(part * DN_WIDTH + h * HEAD_DIM, part * DN_WIDTH + (h + 1) * HEAD_DIM)
                    y = silu(_causal_conv(_delayed(xp_ref, row0, cols, CONV_K), w_ref[:, cols]))
                    if part == 0:
                        y = l2_fwd(y)[0] * (HEAD_DIM ** -0.5)
                    elif part == 1:
                        y = l2_fwd(y)[0]
                    out_ref[rows, sl] = y
            head = lax.broadcasted_iota(jnp.int32, (STRIP, LANES), 1) < N_HEADS
            pbd = pbd_ref[rows, :]
            beta_ref[rows, :] = jnp.where(head, sigmoid(pbd), 0.0)
            a_raw = pltpu.roll(pbd, LANES - N_HEADS, 1)
            g_ref[rows, :] = jnp.where(head, -jnp.exp(alog_ref[...]) * softplus(a_raw + dtb_ref[...]), 0.0)

        _for_strips(tm, STRIP, strip)

    tok = lambda w, cb: pl.BlockSpec((tm, w), lambda i: (i, cb))
    full = lambda a: pl.BlockSpec(a.shape, lambda i: (0, 0))
    return _pcall(
        body, grid=(t // tm,),
        in_specs=[tok(w3, 0), _prev_halo_spec(tm, w3, 0), tok(LANES, PROJ_MAIN // LANES),
                  full(conv_w), full(a_log4), full(dt_bias4)],
        out_specs=[tok(DN_WIDTH, 0)] * 3 + [tok(LANES, 0)] * 2,
        out_shape=[jax.ShapeDtypeStruct((t, DN_WIDTH), F32)] * 3 + [jax.ShapeDtypeStruct((t, LANES), F32)] * 2,
        scratch_shapes=[pltpu.VMEM((HALO + tm, w3), F32)],
        compiler_params=_params("parallel"), name="dn_prep")(p, p, p, conv_w, a_log4, dt_bias4)


def _dn_prep_bwd(p, conv_w, a_log4, dt_bias4, dq, dk, dv, dbeta4, dg4, dp_buf):
    t = p.shape[0]
    tm = _token_tile(t)
    w3 = 3 * DN_WIDTH

    def body(x_ref, halo_ref, pbd_ref, w_ref, alog_ref, dtb_ref, dq_ref, dk_ref, dv_ref, dbeta_ref, dg_ref, _,
             dc_ref, dw_ref, dpbd_ref, dalog_ref, ddtb_ref, xp_ref, dw_acc, lane_acc):
        first = pl.program_id(0) == 0
        _fill_with_prev(xp_ref, x_ref[...], halo_ref[...], first)
        dw_acc[...] = jnp.zeros_like(dw_acc)
        lane_acc[...] = jnp.zeros_like(lane_acc)

        def strip(row0):
            rows = pl.ds(row0, STRIP)
            for h in range(N_HEADS):
                sl = slice(h * HEAD_DIM, (h + 1) * HEAD_DIM)
                for part, dy_ref in ((0, dq_ref), (1, dk_ref), (2, dv_ref)):
                    cols = slice(part * DN_WIDTH + h * HEAD_DIM, part * DN_WIDTH + (h + 1) * HEAD_DIM)
                    delayed = _delayed(xp_ref, row0, cols, CONV_K)
                    c = _causal_conv(delayed, w_ref[:, cols])
                    dy = dy_ref[rows, sl]
                    if part < 2:
                        y = silu(c)
                        _, r = l2_fwd(y)
                        dy = l2_bwd(y, r, dy * (HEAD_DIM ** -0.5) if part == 0 else dy)
                    dc = dy * silu_grad(c)
                    dc_ref[rows, cols] = dc
                    for j in range(CONV_K):
                        k = CONV_K - 1 - j
                        dw_acc[k * SUBLANES:(k + 1) * SUBLANES, cols] += _fold_rows(dc * delayed[j])
            head = lax.broadcasted_iota(jnp.int32, (STRIP, LANES), 1) < N_HEADS
            pbd = pbd_ref[rows, :]
            beta = sigmoid(pbd)
            dpb = jnp.where(head, dbeta_ref[rows, :] * beta * (1.0 - beta), 0.0)
            z = pltpu.roll(pbd, LANES - N_HEADS, 1) + dtb_ref[...]
            neg_rate = -jnp.exp(alog_ref[...])
            dg = dg_ref[rows, :]
            dpa = jnp.where(head, dg * neg_rate * sigmoid(z), 0.0)
            dpbd_ref[rows, :] = (dpb + pltpu.roll(dpa, N_HEADS, 1)).astype(BF16)
            g = jnp.where(head, neg_rate * softplus(z), 0.0)
            lane_acc[0:SUBLANES, :] += _fold_rows(dg * g)
            lane_acc[SUBLANES:, :] += _fold_rows(dpa)

        _for_strips(tm, STRIP, strip)

        @pl.when(first)
        def _():
            dw_ref[...] = jnp.zeros_like(dw_ref)
            dalog_ref[...] = jnp.zeros_like(dalog_ref)
            ddtb_ref[...] = jnp.zeros_like(ddtb_ref)

        for k in range(CONV_K):
            dw_ref[k:k + 1, :] += jnp.sum(dw_acc[k * SUBLANES:(k + 1) * SUBLANES, :], axis=0, keepdims=True)
        dalog_ref[...] += jnp.sum(lane_acc[0:SUBLANES, :], axis=0, keepdims=True)
        ddtb_ref[...] += jnp.sum(lane_acc[SUBLANES:, :], axis=0, keepdims=True)

    tok = lambda w, cb: pl.BlockSpec((tm, w), lambda i: (i, cb))
    full = lambda shape: pl.BlockSpec(shape, lambda i: (0, 0))
    return _pcall(
        body, grid=(t // tm,),
        in_specs=[tok(w3, 0), _prev_halo_spec(tm, w3, 0), tok(LANES, PROJ_MAIN // LANES),
                  full(conv_w.shape), full(a_log4.shape), full(dt_bias4.shape)] + [tok(DN_WIDTH, 0)] * 3 + [tok(LANES, 0)] * 2
        + [pl.BlockSpec(memory_space=pl.ANY)],
        out_specs=[tok(w3, 0), full((CONV_K, w3)), tok(LANES, PROJ_MAIN // LANES), full((1, LANES)), full((1, LANES))],
        out_shape=[jax.ShapeDtypeStruct((t, w3), F32), jax.ShapeDtypeStruct((CONV_K, w3), F32),
                   jax.ShapeDtypeStruct(dp_buf.shape, dp_buf.dtype),
                   jax.ShapeDtypeStruct((1, LANES), F32), jax.ShapeDtypeStruct((1, LANES), F32)],
        input_output_aliases={11: 2},
        scratch_shapes=[pltpu.VMEM((HALO + tm, w3), F32), pltpu.VMEM((CONV_K * SUBLANES, w3), F32),
                        pltpu.VMEM((2 * SUBLANES, LANES), F32)],
        compiler_params=_params("arbitrary"), name="dn_prep_bwd")(p, p, p, conv_w, a_log4, dt_bias4, dq, dk, dv, dbeta4, dg4, dp_buf)


def _conv_bwd_input(dc, w, name, out_cols=None, col_block=0, into=None):
    t, c = dc.shape
    taps = w.shape[0]
    tm = _token_tile(t)
    ct = _pick(c, 1536)
    n_tok = t // tm
    out_cols = c if out_cols is None else out_cols

    def body(dc_ref, next_ref, w_ref, *rest):
        dx_ref, buf_ref = rest[-2], rest[-1]
        buf_ref[0:tm, :] = dc_ref[...]
        buf_ref[tm:, :] = jnp.where(pl.program_id(0) == n_tok - 1, 0.0, next_ref[...])

        def strip(row0):
            for c0 in range(0, ct, LANES):
                cols = slice(c0, c0 + LANES)
                dx_ref[pl.ds(row0, STRIP), cols] = _advanced_conv(buf_ref, row0, cols, w_ref[:, cols]).astype(BF16)

        _for_strips(tm, STRIP, strip)

    in_specs = [pl.BlockSpec((tm, ct), lambda i, j: (i, j)),
                pl.BlockSpec((HALO, ct), lambda i, j: (jnp.minimum((i + 1) * (tm // HALO), t // HALO - 1), j)),
                pl.BlockSpec((taps, ct), lambda i, j: (0, j))]
    args = (dc, dc, w)
    aliases = {}
    if into is not None:
        in_specs.append(pl.BlockSpec(memory_space=pl.ANY))
        args += (into,)
        aliases = {3: 0}
    return _pcall(
        body, grid=(n_tok, c // ct), in_specs=in_specs,
        out_specs=pl.BlockSpec((tm, ct), lambda i, j: (i, j + col_block)),
        out_shape=jax.ShapeDtypeStruct((t, out_cols), BF16), input_output_aliases=aliases,
        scratch_shapes=[pltpu.VMEM((tm + HALO, ct), F32)],
        compiler_params=_params("parallel", "parallel"), name=name)(*args)


def _dn_forward(q, k, v, beta4, g4, p, norm_g):
    t = q.shape[0]
    n = t // CHUNK
    nc = DN_FWD_CHUNKS
    rows_per_step = nc * CHUNK

    def body(q_ref, k_ref, v_ref, b_ref, g_ref, gate_ref, ng_ref, mix_ref, s_all_ref, ainv_ref, s_ref):
        @pl.when(pl.program_id(0) == 0)
        def _():
            s_ref[...] = jnp.zeros_like(s_ref)

        chunks = []
        for c in range(nc):
            rows = slice(c * CHUNK, (c + 1) * CHUNK)
            chunks.append((_stack_heads(q_ref[rows, :]), _stack_heads(k_ref[rows, :]), _stack_heads(v_ref[rows, :]),
                           _stack_lanes(b_ref[rows, :]), chunk_cumsum(g_ref[rows, :])))
        locs = dn_chunks_local(chunks)
        s = [s_ref[h] for h in range(N_HEADS)]
        for c in range(nc):
            rows = slice(c * CHUNK, (c + 1) * CHUNK)
            ainv_ref[c] = locs[c]["a_inv"].astype(BF16)
            for h in range(N_HEADS):
                s_all_ref[c, h] = s[h]
            o, s = dn_chunk_state(locs[c], s)
            o_n, _ = rms_fwd(o, ng_ref[...])
            for h in range(N_HEADS):
                sl = slice(h * HEAD_DIM, (h + 1) * HEAD_DIM)
                mix_ref[rows, sl] = (o_n[_head_rows(h)] * silu(gate_ref[rows, sl])).astype(BF16)
        for h in range(N_HEADS):
            s_ref[h] = s[h]

    ch = lambda w, cb: pl.BlockSpec((rows_per_step, w), lambda i: (i, cb))
    return _pcall(
        body, grid=(n // nc,),
        in_specs=[ch(DN_WIDTH, 0)] * 3 + [ch(LANES, 0)] * 2 + [ch(DN_WIDTH, 3), pl.BlockSpec((1, HEAD_DIM), lambda i: (0, 0))],
        out_specs=[ch(DN_WIDTH, 0), pl.BlockSpec((nc, N_HEADS, HEAD_DIM, HEAD_DIM), lambda i: (i, 0, 0, 0)),
                   pl.BlockSpec((nc, STACK, STACK), lambda i: (i, 0, 0))],
        out_shape=[jax.ShapeDtypeStruct((t, DN_WIDTH + SG_WIDTH), BF16), jax.ShapeDtypeStruct((n, N_HEADS, HEAD_DIM, HEAD_DIM), F32),
                   jax.ShapeDtypeStruct((n, STACK, STACK), BF16)],
        scratch_shapes=[pltpu.VMEM((N_HEADS, HEAD_DIM, HEAD_DIM), F32)],
        compiler_params=_params("arbitrary"), name="dn_forward")(q, k, v, beta4, g4, p, norm_g)


def _dn_backward(q, k, v, beta4, g4, p, norm_g, s_all, a_inv_all, dmix, dp_buf):
    t = q.shape[0]
    n = t // CHUNK
    steps = n // DN_CHUNKS
    rows_per_step = DN_CHUNKS * CHUNK

    def body(q_ref, k_ref, v_ref, b_ref, g_ref, gate_ref, ng_ref, s_in_ref, ainv_ref, dmix_ref, _,
             dq_ref, dk_ref, dv_ref, db_ref, dg_ref, dgate_ref, dng_ref, ds_ref):
        @pl.when(pl.program_id(0) == 0)
        def _():
            ds_ref[...] = jnp.zeros_like(ds_ref)
            dng_ref[...] = jnp.zeros_like(dng_ref)

        chunks = []
        for c in range(DN_CHUNKS):
            rows = slice(c * CHUNK, (c + 1) * CHUNK)
            chunks.append((_stack_heads(q_ref[rows, :]), _stack_heads(k_ref[rows, :]), _stack_heads(v_ref[rows, :]),
                           _stack_lanes(b_ref[rows, :]), chunk_cumsum(g_ref[rows, :])))
        items = []
        for c, loc in enumerate(dn_chunks_local(chunks, [ainv_ref[c] for c in range(DN_CHUNKS)])):
            rows = slice(c * CHUNK, (c + 1) * CHUNK)
            s = [s_in_ref[c, h] for h in range(N_HEADS)]
            o, _ = dn_chunk_state(loc, s)
            o_n, r = rms_fwd(o, ng_ref[...])
            gate = _stack_heads(gate_ref[rows, :])
            dmx = _stack_heads(dmix_ref[rows, :])
            dgate = dmx * o_n * silu_grad(gate)
            do, dng_rows = rms_bwd(o, r, ng_ref[...], dmx * silu(gate))
            dng_ref[...] += jnp.sum(dng_rows, axis=0, keepdims=True)
            for h in range(N_HEADS):
                dgate_ref[rows, h * HEAD_DIM:(h + 1) * HEAD_DIM] = dgate[_head_rows(h)].astype(BF16)
            items.append((*chunks[c][:4], loc, s, do))
        grads, ds = dn_chunks_bwd(items, [ds_ref[h] for h in range(N_HEADS)])
        lane = lax.broadcasted_iota(jnp.int32, (CHUNK, LANES), 1)
        _, strict = _tri_masks(CHUNK)
        for c in range(DN_CHUNKS):
            rows = slice(c * CHUNK, (c + 1) * CHUNK)
            dq, dk, dv, dbeta, dgc = grads[c]
            db4 = jnp.zeros((CHUNK, LANES), F32)
            dgc4 = jnp.zeros((CHUNK, LANES), F32)
            for h in range(N_HEADS):
                sl = slice(h * HEAD_DIM, (h + 1) * HEAD_DIM)
                head_rows = _head_rows(h)
                dq_ref[rows, sl] = dq[head_rows]
                dk_ref[rows, sl] = dk[head_rows]
                dv_ref[rows, sl] = dv[head_rows]
                db4 = jnp.where(lane == h, dbeta[head_rows], db4)
                dgc4 = jnp.where(lane == h, dgc[head_rows], dgc4)
            db_ref[rows, :] = db4
            dg_ref[rows, :] = dot_nn(jnp.logical_not(strict).astype(F32), dgc4)
        for h in range(N_HEADS):
            ds_ref[h] = ds[h]

    rev = lambda w, cb: pl.BlockSpec((rows_per_step, w), lambda i: (steps - 1 - i, cb))
    return _pcall(
        body, grid=(steps,),
        in_specs=[rev(DN_WIDTH, 0)] * 3 + [rev(LANES, 0)] * 2 + [rev(DN_WIDTH, 3), pl.BlockSpec((1, HEAD_DIM), lambda i: (0, 0)),
                  pl.BlockSpec((DN_CHUNKS, N_HEADS, HEAD_DIM, HEAD_DIM), lambda i: (steps - 1 - i, 0, 0, 0)),
                  pl.BlockSpec((DN_CHUNKS, STACK, STACK), lambda i: (steps - 1 - i, 0, 0)), rev(DN_WIDTH, 0),
                  pl.BlockSpec(memory_space=pl.ANY)],
        out_specs=[rev(DN_WIDTH, 0)] * 3 + [rev(LANES, 0)] * 2 + [rev(DN_WIDTH, 3), pl.BlockSpec((1, HEAD_DIM), lambda i: (0, 0))],
        out_shape=[jax.ShapeDtypeStruct((t, DN_WIDTH), F32)] * 3 + [jax.ShapeDtypeStruct((t, LANES), F32)] * 2
        + [jax.ShapeDtypeStruct(dp_buf.shape, dp_buf.dtype), jax.ShapeDtypeStruct((1, HEAD_DIM), F32)],
        input_output_aliases={10: 5},
        scratch_shapes=[pltpu.VMEM((N_HEADS, HEAD_DIM, HEAD_DIM), F32)],
        compiler_params=_params("arbitrary"), name="dn_backward")(q, k, v, beta4, g4, p, norm_g, s_all, a_inv_all, dmix, dp_buf)


def _sg_mask():
    row = lax.broadcasted_iota(jnp.int32, (SG_BLOCK, SG_BLOCK), 0)
    col = lax.broadcasted_iota(jnp.int32, (SG_BLOCK, SG_BLOCK), 1)
    return (col // CHUNK) <= (row // CHUNK)


def _sg_forward(p, norm_g, w_s, b_t, mix_buf):
    t = p.shape[0]

    def body(u_ref, v_ref, ng_ref, w_ref, b_ref, _, o_ref):
        mask = _sg_mask()
        for g in range(SG_GROUPS):
            sl = slice(g * SG_DIM, (g + 1) * SG_DIM)
            vn, _ = rms_fwd(gelu(v_ref[:, sl]), ng_ref[:, sl])
            s = dot_nn(jnp.where(mask, w_ref[g], 0.0), vn, FAST) + b_ref[:, g:g + 1]
            o_ref[:, sl] = (gelu(u_ref[:, sl]) * s).astype(BF16)

    blk = lambda cb: pl.BlockSpec((SG_BLOCK, SG_WIDTH), lambda i: (i, cb))
    return _pcall(
        body, grid=(t // SG_BLOCK,),
        in_specs=[blk(4), blk(5), pl.BlockSpec((1, SG_WIDTH), lambda i: (0, 0)),
                  pl.BlockSpec((SG_GROUPS, SG_BLOCK, SG_BLOCK), lambda i: (0, 0, 0)), pl.BlockSpec((SG_BLOCK, SG_GROUPS), lambda i: (0, 0)),
                  pl.BlockSpec(memory_space=pl.ANY)],
        out_specs=blk(1), out_shape=jax.ShapeDtypeStruct(mix_buf.shape, mix_buf.dtype), input_output_aliases={5: 0},
        compiler_params=_params("parallel"), name="sg_forward")(p, p, norm_g, w_s, b_t, mix_buf)


def _sg_backward(p, norm_g, w_s, b_t, dmix):
    t = p.shape[0]

    def body(u_ref, v_ref, ng_ref, w_ref, b_ref, do_ref, duv_ref, dng_ref, dw_ref, db_ref):
        @pl.when(pl.program_id(0) == 0)
        def _():
            dng_ref[...] = jnp.zeros_like(dng_ref)
            dw_ref[...] = jnp.zeros_like(dw_ref)
            db_ref[...] = jnp.zeros_like(db_ref)

        mask = _sg_mask()
        lane = lax.broadcasted_iota(jnp.int32, (SG_BLOCK, LANES), 1)
        db = jnp.zeros((SG_BLOCK, LANES), F32)
        for g in range(SG_GROUPS):
            sl = slice(g * SG_DIM, (g + 1) * SG_DIM)
            u_raw, v_raw, do = u_ref[:, sl], v_ref[:, sl], do_ref[:, sl]
            vg = gelu(v_raw)
            vn, r = rms_fwd(vg, ng_ref[:, sl])
            w_m = jnp.where(mask, w_ref[g], 0.0)
            s = dot_nn(w_m, vn, FAST) + b_ref[:, g:g + 1]
            duv_ref[:, sl] = (do * s * gelu_grad(u_raw)).astype(BF16)
            ds = do * gelu(u_raw)
            db = jnp.where(lane == g, jnp.sum(ds, axis=1, keepdims=True), db)
            dw_ref[g] += jnp.where(mask, dot_nt(ds, vn, FAST), 0.0)
            dvg, dng_rows = rms_bwd(vg, r, ng_ref[:, sl], dot_tn(w_m, ds, FAST))
            dng_ref[:, sl] += jnp.sum(dng_rows, axis=0, keepdims=True)
            duv_ref[:, SG_WIDTH + g * SG_DIM:SG_WIDTH + (g + 1) * SG_DIM] = (dvg * gelu_grad(v_raw)).astype(BF16)
        db_ref[...] += db

    blk = lambda cb: pl.BlockSpec((SG_BLOCK, SG_WIDTH), lambda i: (i, cb))
    const2 = lambda shape: pl.BlockSpec(shape, lambda i: (0, 0))
    w_spec = pl.BlockSpec((SG_GROUPS, SG_BLOCK, SG_BLOCK), lambda i: (0, 0, 0))
    return _pcall(
        body, grid=(t // SG_BLOCK,),
        in_specs=[blk(4), blk(5), const2((1, SG_WIDTH)), w_spec, const2((SG_BLOCK, SG_GROUPS)), blk(1)],
        out_specs=[pl.BlockSpec((SG_BLOCK, 2 * SG_WIDTH), lambda i: (i, 2)), const2((1, SG_WIDTH)), w_spec,
                   const2((SG_BLOCK, LANES))],
        out_shape=[jax.ShapeDtypeStruct((t, PROJ_PAD), BF16), jax.ShapeDtypeStruct((1, SG_WIDTH), F32),
                   jax.ShapeDtypeStruct((SG_GROUPS, SG_BLOCK, SG_BLOCK), F32), jax.ShapeDtypeStruct((SG_BLOCK, LANES), F32)],
        compiler_params=_params("arbitrary"), name="sg_backward")(p, p, norm_g, w_s, b_t, dmix)


FFN_CT = D_FF // 2


def _ffn_act(up, conv_w, conv_b):
    t = up.shape[0]
    tm = _token_tile(t)
    nj = D_FF // FFN_CT

    def body(ug_ref, uv_ref, hg_ref, hv_ref, wg_ref, wv_ref, bg_ref, bv_ref, act_ref, xg_ref, xv_ref):
        first = pl.program_id(0) == 0
        _fill_with_prev(xg_ref, ug_ref[...], hg_ref[...], first)
        _fill_with_prev(xv_ref, uv_ref[...], hv_ref[...], first)

        def strip(row0):
            for c0 in range(0, FFN_CT, LANES):
                cols = slice(c0, c0 + LANES)
                cg = _causal_conv(_delayed(xg_ref, row0, cols, FFN_CONV), wg_ref[:, cols]) + bg_ref[:, cols]
                cv = _causal_conv(_delayed(xv_ref, row0, cols, FFN_CONV), wv_ref[:, cols]) + bv_ref[:, cols]
                act_ref[pl.ds(row0, STRIP), cols] = (silu(cg) * cv).astype(BF16)

        _for_strips(tm, STRIP, strip)

    tok = lambda off: pl.BlockSpec((tm, FFN_CT), lambda i, j: (i, j + off))
    halo = lambda off: pl.BlockSpec((HALO, FFN_CT), lambda i, j: (jnp.maximum(i * (tm // HALO) - 1, 0), j + off))
    par = lambda rows, off: pl.BlockSpec((rows, FFN_CT), lambda i, j: (0, j + off))
    return _pcall(
        body, grid=(t // tm, nj),
        in_specs=[tok(0), tok(nj), halo(0), halo(nj), par(FFN_CONV, 0), par(FFN_CONV, nj), par(1, 0), par(1, nj)],
        out_specs=pl.BlockSpec((tm, FFN_CT), lambda i, j: (i, j)),
        out_shape=jax.ShapeDtypeStruct((t, D_FF), BF16),
        scratch_shapes=[pltpu.VMEM((HALO + tm, FFN_CT), F32)] * 2,
        compiler_params=_params("parallel", "parallel"), name="ffn_act")(up, up, up, up, conv_w, conv_w, conv_b, conv_b)


def _ffn_bwd(up, conv_w, conv_b, dact):
    t = up.shape[0]
    tm = _pick(t, 128)
    n_tok = t // tm
    width = 2 * D_FF

    def dconv(delayed_g, delayed_v, da, wg, wv, bg, bv):
        cg = _causal_conv(delayed_g, wg) + bg
        cv = _causal_conv(delayed_v, wv) + bv
        s = sigmoid(cg)
        return da * cv * (s * (1.0 + cg * (1.0 - s))), da * (cg * s)

    def body(up_ref, prev_ref, next_ref, da_ref, dan_ref, w_ref, b_ref, dup_ref, dw_ref, db_ref, xp_ref, dc_ref, dw_acc, db_acc):
        first = pl.program_id(0) == 0
        last = pl.program_id(0) == n_tok - 1
        xp_ref[0:HALO, :] = jnp.where(first, 0.0, prev_ref[...])
        xp_ref[HALO:HALO + tm, :] = up_ref[...]
        xp_ref[HALO + tm:, :] = next_ref[...]
        dw_acc[...] = jnp.zeros_like(dw_acc)
        db_acc[...] = jnp.zeros_like(db_acc)

        def strip(row0):
            rows = pl.ds(row0, STRIP)
            for c0 in range(0, D_FF, LANES):
                gc, vc = slice(c0, c0 + LANES), slice(D_FF + c0, D_FF + c0 + LANES)
                del_g, del_v = _delayed(xp_ref, row0, gc, FFN_CONV), _delayed(xp_ref, row0, vc, FFN_CONV)
                dcg, dcv = dconv(del_g, del_v, da_ref[rows, gc], w_ref[:, gc], w_ref[:, vc], b_ref[:, gc], b_ref[:, vc])
                dc_ref[rows, gc] = dcg
                dc_ref[rows, vc] = dcv
                db_acc[:, gc] += _fold_rows(dcg)
                db_acc[:, vc] += _fold_rows(dcv)
                for j in range(FFN_CONV):
                    k = FFN_CONV - 1 - j
                    dw_acc[k * SUBLANES:(k + 1) * SUBLANES, gc] += _fold_rows(dcg * del_g[j])
                    dw_acc[k * SUBLANES:(k + 1) * SUBLANES, vc] += _fold_rows(dcv * del_v[j])

        _for_strips(tm, STRIP, strip)

        for c0 in range(0, D_FF, LANES):
            gc, vc = slice(c0, c0 + LANES), slice(D_FF + c0, D_FF + c0 + LANES)

            def delayed(cols):
                ext = xp_ref[tm:tm + 2 * HALO, cols]
                return [ext[HALO:, :]] + [pltpu.roll(ext, j, 0)[HALO:, :] for j in range(1, FFN_CONV)]

            dcg, dcv = dconv(delayed(gc), delayed(vc), dan_ref[:, gc], w_ref[:, gc], w_ref[:, vc], b_ref[:, gc], b_ref[:, vc])
            dc_ref[tm:, gc] = jnp.where(last, 0.0, dcg)
            dc_ref[tm:, vc] = jnp.where(last, 0.0, dcv)

        def strip_dx(row0):
            for c0 in range(0, width, LANES):
                cols = slice(c0, c0 + LANES)
                dup_ref[pl.ds(row0, STRIP), cols] = _advanced_conv(dc_ref, row0, cols, w_ref[:, cols]).astype(BF16)

        _for_strips(tm, STRIP, strip_dx)

        @pl.when(first)
        def _():
            dw_ref[...] = jnp.zeros_like(dw_ref)
            db_ref[...] = jnp.zeros_like(db_ref)

        for k in range(FFN_CONV):
            dw_ref[k:k + 1, :] += jnp.sum(dw_acc[k * SUBLANES:(k + 1) * SUBLANES, :], axis=0, keepdims=True)
        db_ref[...] += jnp.sum(db_acc[...], axis=0, keepdims=True)

    next_rows = lambda i: jnp.minimum((i + 1) * (tm // HALO), t // HALO - 1)
    full = lambda rows: pl.BlockSpec((rows, width), lambda i: (0, 0))
    return _pcall(
        body, grid=(n_tok,),
        in_specs=[pl.BlockSpec((tm, width), lambda i: (i, 0)),
                  pl.BlockSpec((HALO, width), lambda i: (jnp.maximum(i * (tm // HALO) - 1, 0), 0)),
                  pl.BlockSpec((HALO, width), lambda i: (next_rows(i), 0)),
                  pl.BlockSpec((tm, D_FF), lambda i: (i, 0)), pl.BlockSpec((HALO, D_FF), lambda i: (next_rows(i), 0)),
                  full(FFN_CONV), full(1)],
        out_specs=[pl.BlockSpec((tm, width), lambda i: (i, 0)), full(FFN_CONV), full(1)],
        out_shape=[jax.ShapeDtypeStruct((t, width), BF16), jax.ShapeDtypeStruct((FFN_CONV, width), F32),
                   jax.ShapeDtypeStruct((1, width), F32)],
        scratch_shapes=[pltpu.VMEM((tm + 2 * HALO, width), F32), pltpu.VMEM((tm + HALO, width), F32),
                        pltpu.VMEM((FFN_CONV * SUBLANES, width), F32), pltpu.VMEM((SUBLANES, width), F32)],
        compiler_params=_params("arbitrary"), name="ffn_bwd")(up, up, up, dact, dact, conv_w, conv_b)


def _my_position():
    return lax.axis_index("x"), lax.axis_index("y"), lax.axis_index("c")


COPIES = N_DEV - 1


def _all_gather(arrays):
    n = len(arrays)

    def body(*refs):
        x_refs, out_refs = refs[:n], refs[n:2 * n]
        send_sems, recv_sems, local_sems = refs[2 * n:]
        x, y, cc = _my_position()
        me, sibling = (x, y, cc), (x, y, 1 - cc)
        chips = [(1 - x, y), (x, 1 - y), (1 - x, 1 - y)]

        def block(a, px, py, pc):
            return out_refs[a].at[4 * px + 2 * py + pc]

        def copy(a, k, blk, to, src=None):
            return pltpu.make_async_remote_copy(
                src_ref=block(a, *blk) if src is None else src, dst_ref=block(a, *blk),
                send_sem=send_sems.at[a * COPIES + k], recv_sem=recv_sems.at[a * COPIES + k],
                device_id=to, device_id_type=MESH_ID)

        mine = [pltpu.make_async_copy(x_refs[a], block(a, *me), local_sems.at[a]) for a in range(n)]
        for cp in mine:
            cp.start()
        first = []
        for a in range(n):
            first.append(copy(a, 0, me, sibling, src=x_refs[a]))
            first += [copy(a, 1 + j, me, (*chip, cc), src=x_refs[a]) for j, chip in enumerate(chips)]
        for cp in first:
            cp.start()
        passed = []
        for j, chip in enumerate(chips):
            for a in range(n):
                copy(a, 1 + j, (*chip, cc), me).wait_recv()
                passed.append(copy(a, 4 + j, (*chip, cc), sibling))
                passed[-1].start()
        for a in range(n):
            copy(a, 0, sibling, me).wait_recv()
        for j, chip in enumerate(chips):
            for a in range(n):
                copy(a, 4 + j, (*chip, 1 - cc), me).wait_recv()
        for cp in first + passed:
            cp.wait_send()
        for cp in mine:
            cp.wait()

    any_spec = pl.BlockSpec(memory_space=pl.ANY)
    return _pcall(
        body, out_shape=[jax.ShapeDtypeStruct((N_DEV,) + a.shape, a.dtype) for a in arrays],
        in_specs=[any_spec] * n, out_specs=[any_spec] * n,
        scratch_shapes=[pltpu.SemaphoreType.DMA((n * COPIES,)), pltpu.SemaphoreType.DMA((n * COPIES,)),
                        pltpu.SemaphoreType.DMA((n,))],
        name="all_gather")(*arrays)


def _all_to_all(sends):
    n = len(sends)

    def body(*refs):
        send_refs, recv_refs = refs[:n], refs[n:2 * n]
        send_sems, recv_sems, local_sems = refs[2 * n:]
        x, y, cc = _my_position()
        me = 4 * x + 2 * y + cc
        mine = [pltpu.make_async_copy(send_refs[a].at[me], recv_refs[a].at[me], local_sems.at[a]) for a in range(n)]
        for cp in mine:
            cp.start()
        copies = []
        for rel in range(1, N_DEV):
            px, py, pc = x ^ (rel >> 2), y ^ ((rel >> 1) & 1), cc ^ (rel & 1)
            for a in range(n):
                copies.append(pltpu.make_async_remote_copy(
                    src_ref=send_refs[a].at[4 * px + 2 * py + pc], dst_ref=recv_refs[a].at[me],
                    send_sem=send_sems.at[a * COPIES + rel - 1], recv_sem=recv_sems.at[a * COPIES + rel - 1],
                    device_id=(px, py, pc), device_id_type=MESH_ID))
        for cp in copies:
            cp.start()
        for cp in copies:
            cp.wait()
        for cp in mine:
            cp.wait()

    any_spec = pl.BlockSpec(memory_space=pl.ANY)
    return _pcall(
        body, out_shape=[jax.ShapeDtypeStruct(s.shape, s.dtype) for s in sends],
        in_specs=[any_spec] * n, out_specs=[any_spec] * n,
        scratch_shapes=[pltpu.SemaphoreType.DMA((n * COPIES,)), pltpu.SemaphoreType.DMA((n * COPIES,)),
                        pltpu.SemaphoreType.DMA((n,))],
        name="all_to_all")(*sends)


def _hbm(a):
    return pltpu.with_memory_space_constraint(a, pltpu.HBM)


def _split_copies(send_refs, land_refs, send_sems, recv_sems, local_sems, gather):
    x, y, cc = _my_position()
    me = 4 * x + 2 * y + cc
    local, remote = [], []
    for a, (send, land) in enumerate(zip(send_refs, land_refs)):
        local.append(pltpu.make_async_copy(send if gather else send.at[me], land.at[me], local_sems.at[a]))
    for a, (send, land) in enumerate(zip(send_refs, land_refs)):
        for rel in range(1, N_DEV):
            px, py, pc = x ^ (rel >> 2), y ^ ((rel >> 1) & 1), cc ^ (rel & 1)
            remote.append(pltpu.make_async_remote_copy(
                src_ref=send if gather else send.at[4 * px + 2 * py + pc], dst_ref=land.at[me],
                send_sem=send_sems.at[a * COPIES + rel - 1], recv_sem=recv_sems.at[a * COPIES + rel - 1],
                device_id=(px, py, pc), device_id_type=MESH_ID))
    return local, remote


SPLIT_EFFECT = pltpu.SideEffectType.DATAFLOW_SIDE_EFFECTING


def _exchange_start(sends, after, gather, name):
    n = len(sends)
    lands = [_hbm(lax.empty((N_DEV,) + s.shape if gather else s.shape, s.dtype)) for s in sends]

    def body(*refs):
        send_refs, land_refs = refs[:n], refs[n:2 * n]
        send_sems, recv_sems, local_sems = refs[2 * n + 1:2 * n + 4]
        token = refs[-1]
        local, remote = _split_copies(send_refs, land_refs, send_sems, recv_sems, local_sems, gather)
        for cp in local + remote:
            cp.start()
        token[...] = jnp.zeros_like(token)

    hbm, sem = pl.BlockSpec(memory_space=pltpu.HBM), pl.BlockSpec(memory_space=pltpu.SEMAPHORE)
    out = _pcall(
        body, name=name,
        out_shape=[pltpu.SemaphoreType.DMA((n * COPIES,)), pltpu.SemaphoreType.DMA((n * COPIES,)), pltpu.SemaphoreType.DMA((n,))]
        + [pltpu.HBM(s.shape, s.dtype) for s in sends] + [pltpu.HBM(z.shape, z.dtype) for z in lands]
        + [jax.ShapeDtypeStruct((SUBLANES, LANES), F32)],
        in_specs=[hbm] * (2 * n) + [pl.BlockSpec(memory_space=pl.ANY)],
        out_specs=[sem] * 3 + [hbm] * (2 * n) + [pl.BlockSpec(memory_space=pltpu.VMEM)],
        input_output_aliases={i: 3 + i for i in range(2 * n)},
        compiler_params=pltpu.CompilerParams(has_side_effects=SPLIT_EFFECT),
    )(*[_hbm(s) for s in sends], *lands, after)
    return dict(sems=out[:3], sends=out[3:3 + n], lands=out[3 + n:3 + 2 * n], gather=gather), out[-1]


def _exchange_wait(handle, after, name):
    sends, lands, gather = handle["sends"], handle["lands"], handle["gather"]
    n = len(sends)

    def body(*refs):
        send_refs, land_refs = refs[:n], refs[n:2 * n]
        send_sems, recv_sems, local_sems = refs[2 * n:2 * n + 3]
        local, remote = _split_copies(send_refs, land_refs, send_sems, recv_sems, local_sems, gather)
        for cp in remote:
            cp.wait_send()
            cp.wait_recv()
        for cp in local:
            cp.wait()

    hbm, sem = pl.BlockSpec(memory_space=pltpu.HBM), pl.BlockSpec(memory_space=pltpu.SEMAPHORE)
    out = _pcall(
        body, name=name,
        out_shape=[pltpu.HBM(s.shape, s.dtype) for s in sends] + [pltpu.HBM(z.shape, z.dtype) for z in lands],
        in_specs=[hbm] * (2 * n) + [sem] * 3 + [pl.BlockSpec(memory_space=pl.ANY)],
        out_specs=[hbm] * (2 * n), input_output_aliases={i: i for i in range(2 * n)},
        compiler_params=pltpu.CompilerParams(has_side_effects=SPLIT_EFFECT),
    )(*sends, *lands, *handle["sems"], after)
    return out[n:]


def _sum_and_adamw(recv, w, m, v, name):
    _, r, wp = recv.shape
    c = w.shape[-1]
    lead = w.ndim == 3
    tr = SLAB_ROW_TILE if r % SLAB_ROW_TILE == 0 else (SLAB_ROW_TILE // 4 if r % (SLAB_ROW_TILE // 4) == 0 else r)
    bc1 = 1.0 - ADAM_B1 ** ADAM_STEP
    bc2 = 1.0 - ADAM_B2 ** ADAM_STEP

    def body(recv_ref, w_ref, m_ref, v_ref, g_ref, d_ref, nm_ref, nv_ref):
        g = recv_ref[0, :, 0:c].astype(F32)
        for s in range(1, N_DEV):
            g = g + recv_ref[s, :, 0:c].astype(F32)
        m_new = ADAM_B1 * m_ref[...] + (1.0 - ADAM_B1) * g
        v_new = ADAM_B2 * v_ref[...] + (1.0 - ADAM_B2) * (g * g)
        m_hat = m_new / bc1
        v_hat = v_new / bc2
        g_ref[...] = g
        d_ref[...] = -ADAM_LR * (m_hat / (jnp.sqrt(v_hat) + ADAM_EPS) + ADAM_WD * w_ref[...])
        nm_ref[...] = m_new
        nv_ref[...] = v_new

    tile = pl.BlockSpec((None, tr, c), lambda i: (0, i, 0)) if lead else pl.BlockSpec((tr, c), lambda i: (i, 0))
    return _pcall(
        body, grid=(r // tr,),
        in_specs=[pl.BlockSpec((N_DEV, tr, wp), lambda i: (0, i, 0)), tile, tile, tile],
        out_specs=[tile] * 4, out_shape=[jax.ShapeDtypeStruct(w.shape, F32)] * 4,
        compiler_params=_params("parallel"), name=name)(recv, w, m, v)


SHARDED_TAPS = ("dn_conv_w", "ffn_conv_w")
REPLICATED = ("attn_norm_g", "dn_a_log", "dn_dt_bias", "dn_out_norm_g", "sg_norm_g", "sg_w", "sg_b", "ffn_norm_g",
              "ffn_conv_b", "final_norm_g")
SMALL = SHARDED_TAPS + REPLICATED
WEIGHT_ORDER = ("attn_norm_g", "w_in", "dn_conv_w", "dn_a_log", "dn_dt_bias", "dn_out_norm_g", "sg_norm_g", "sg_w", "sg_b",
                "w_out", "ffn_norm_g", "w_up", "ffn_conv_w", "ffn_conv_b", "w_down", "final_norm_g")
SLAB_COLS = 1024
SLAB_ROW_TILE = 128


def _pad_to(flat, multiple):
    pad = (-flat.shape[-1]) % multiple
    if pad == 0:
        return flat
    return jnp.pad(flat, [(0, 0)] * (flat.ndim - 1) + [(0, pad)])


def _pack_small(named):
    flat = jnp.concatenate([named[n].reshape(-1) for n in SMALL])
    return _pad_to(flat, SUBLANES * SLAB_COLS).reshape(-1, SLAB_COLS)


def _unpack_small(slab, like):
    flat = slab.reshape(-1)
    out, off = {}, 0
    for n in SMALL:
        size = like[n].size
        out[n] = flat[off:off + size].reshape(like[n].shape)
        off += size
    return out


def _split_columns(full, n_local):
    r = full.shape[0]
    return full.reshape(r, N_DEV, n_local).transpose(1, 0, 2).reshape(N_DEV, r * n_local)


def _join_columns(blocks, r, n_local):
    return blocks.reshape(N_DEV, r, n_local).transpose(1, 0, 2).reshape(r, N_DEV * n_local)


def _lanes4(a):
    return jnp.pad(a.reshape(1, N_HEADS), ((0, 0), (0, LANES - N_HEADS)))


def kernel(x, attn_norm_g, w_in, dn_conv_w, dn_a_log, dn_dt_bias, dn_out_norm_g, sg_norm_g, sg_w, sg_b, w_out, ffn_norm_g, w_up, ffn_conv_w, ffn_conv_b, w_down, final_norm_g, loss_target, m_attn_norm_g, m_w_in, m_dn_conv_w, m_dn_a_log, m_dn_dt_bias, m_dn_out_norm_g, m_sg_norm_g, m_sg_w, m_sg_b, m_w_out, m_ffn_norm_g, m_w_up, m_ffn_conv_w, m_ffn_conv_b, m_w_down, m_final_norm_g, v_attn_norm_g, v_w_in, v_dn_conv_w, v_dn_a_log, v_dn_dt_bias, v_dn_out_norm_g, v_sg_norm_g, v_sg_w, v_sg_b, v_w_out, v_ffn_norm_g, v_w_up, v_ffn_conv_w, v_ffn_conv_b, v_w_down, v_final_norm_g):
    weights = dict(attn_norm_g=attn_norm_g, w_in=w_in, dn_conv_w=dn_conv_w, dn_a_log=dn_a_log, dn_dt_bias=dn_dt_bias,
                   dn_out_norm_g=dn_out_norm_g, sg_norm_g=sg_norm_g, sg_w=sg_w, sg_b=sg_b, w_out=w_out, ffn_norm_g=ffn_norm_g,
                   w_up=w_up, ffn_conv_w=ffn_conv_w, ffn_conv_b=ffn_conv_b, w_down=w_down, final_norm_g=final_norm_g)
    m_in = dict(attn_norm_g=m_attn_norm_g, w_in=m_w_in, dn_conv_w=m_dn_conv_w, dn_a_log=m_dn_a_log, dn_dt_bias=m_dn_dt_bias,
                dn_out_norm_g=m_dn_out_norm_g, sg_norm_g=m_sg_norm_g, sg_w=m_sg_w, sg_b=m_sg_b, w_out=m_w_out,
                ffn_norm_g=m_ffn_norm_g, w_up=m_w_up, ffn_conv_w=m_ffn_conv_w, ffn_conv_b=m_ffn_conv_b, w_down=m_w_down,
                final_norm_g=m_final_norm_g)
    v_in = dict(attn_norm_g=v_attn_norm_g, w_in=v_w_in, dn_conv_w=v_dn_conv_w, dn_a_log=v_dn_a_log, dn_dt_bias=v_dn_dt_bias,
                dn_out_norm_g=v_dn_out_norm_g, sg_norm_g=v_sg_norm_g, sg_w=v_sg_w, sg_b=v_sg_b, w_out=v_w_out,
                ffn_norm_g=v_ffn_norm_g, w_up=v_w_up, ffn_conv_w=v_ffn_conv_w, ffn_conv_b=v_ffn_conv_b, w_down=v_w_down,
                final_norm_g=v_final_norm_g)

    n_in, n_up = w_in.shape[2], w_up.shape[2]
    r_out, r_down = w_out.shape[1], w_down.shape[1]
    n_dnc, n_ffc = dn_conv_w.shape[2], ffn_conv_w.shape[2]
    transposed = lambda a: jnp.transpose(a, (0, 2, 1))
    taps = _pad_to(jnp.concatenate([dn_conv_w.reshape(-1), ffn_conv_w.reshape(-1)]), SUBLANES * LANES).reshape(-1, LANES)
    g_in, g_taps = _all_gather([transposed(w_in)[0].astype(BF16), taps])
    late_weights, token = _exchange_start(
        [w_out[0].astype(BF16), transposed(w_up)[0].astype(BF16), w_down[0].astype(BF16)], g_taps, True, "gather_late_start")
    w_in_t = jnp.pad(g_in.reshape(N_DEV * n_in, D_MODEL), ((0, PROJ_PAD - N_DEV * n_in), (0, 0)))
    taps_all = g_taps.reshape(N_DEV, -1)
    dn_conv_full = _join_columns(taps_all[:, :CONV_K * n_dnc], CONV_K, n_dnc)
    ffn_conv_full = _join_columns(taps_all[:, CONV_K * n_dnc:CONV_K * n_dnc + FFN_CONV * n_ffc], FFN_CONV, n_ffc)

    def late(after):
        g_out, g_up, g_down = _exchange_wait(late_weights, after, "gather_late_wait")
        return (g_up.reshape(N_DEV * n_up, D_MODEL),
                g_out.reshape(N_DEV * r_out, D_MODEL), g_down.reshape(N_DEV * r_down, D_MODEL))

    def send_early(blocks, after, name):
        return _exchange_start(blocks, after, False, name)

    def send_small(g, loss_lanes, after):
        small = jnp.concatenate([g[n].reshape(-1) for n in REPLICATED] + [loss_lanes[0, 0:1]])
        slab = jnp.concatenate([_split_columns(g["dn_conv_w"], n_dnc), _split_columns(g["ffn_conv_w"], n_ffc),
                                jnp.broadcast_to(small[None, :], (N_DEV, small.shape[0]))], axis=1)
        return send_early([_pad_to(slab, SUBLANES * SLAB_COLS).reshape(N_DEV, -1, SLAB_COLS)], after, "send_small")

    grad_x, d_g1, early = _local_step(
        x[0], loss_target[0], w_in_t, late, send_early, send_small, dn_conv_full, ffn_conv_full,
        attn_norm_g + token[0:1, 0:1], dn_a_log, dn_dt_bias, dn_out_norm_g, sg_norm_g, sg_w, sg_b, ffn_norm_g, ffn_conv_b,
        final_norm_g, n_in)

    norm_rows = D_MODEL // LANES
    r_g1, = _all_to_all([jnp.broadcast_to(d_g1.reshape(1, norm_rows, LANES), (N_DEV, norm_rows, LANES))])
    r_dn, = _exchange_wait(early[0], r_g1, "send_dw_down_wait")
    r_up, r_o = _exchange_wait(early[1], r_g1, "send_dw_up_out_wait")
    r_small, = _exchange_wait(early[2], r_g1, "send_small_wait")
    r_in, = _exchange_wait(early[3], r_g1, "send_dw_in_wait")

    upd = {
        "w_in": [transposed(o) for o in _sum_and_adamw(r_in, transposed(w_in), transposed(m_w_in), transposed(v_w_in),
                                                       "adamw_w_in")],
        "w_up": [transposed(o) for o in _sum_and_adamw(r_up, transposed(w_up), transposed(m_w_up), transposed(v_w_up),
                                                       "adamw_w_up")],
        "w_out": _sum_and_adamw(r_o, w_out, m_w_out, v_w_out, "adamw_w_out"),
        "w_down": _sum_and_adamw(r_dn, w_down, m_w_down, v_w_down, "adamw_w_down"),
    }
    small_upd = _sum_and_adamw(r_small, _pack_small(weights), _pack_small(m_in), _pack_small(v_in), "adamw_small")
    as_rows = lambda a: a.reshape(norm_rows, LANES)
    norm_upd = _sum_and_adamw(r_g1, as_rows(attn_norm_g), as_rows(m_attn_norm_g), as_rows(v_attn_norm_g), "adamw_attn_norm")
    results = []
    for i in range(4):
        named = _unpack_small(small_upd[i], weights)
        named.update({n: upd[n][i] for n in upd})
        named["attn_norm_g"] = norm_upd[i].reshape(attn_norm_g.shape)
        results.append(named)

    loss = small_upd[0].reshape(-1)[sum(weights[n].size for n in SMALL)]
    return (loss, grad_x[None], *[r[n] for r in results for n in WEIGHT_ORDER])


def _local_step(x2d, tgt, w_in_t, late_weights, send_early, send_small, dn_conv_full, ffn_conv_full, attn_norm_g, dn_a_log,
                dn_dt_bias, dn_out_norm_g, sg_norm_g, sg_w, sg_b, ffn_norm_g, ffn_conv_b, final_norm_g, n_in):
    t = x2d.shape[0]
    g1, g2, gf = attn_norm_g, ffn_norm_g, final_norm_g.reshape(1, D_MODEL)
    a_log4, dt_bias4 = _lanes4(dn_a_log), _lanes4(dn_dt_bias)
    sg_w3 = sg_w[0]
    sg_b_t = sg_b[0].T
    conv_b = ffn_conv_b

    p, h1, rstd1 = _rmsnorm_matmul(x2d, g1, w_in_t, "norm_in_proj", 512)
    q, k, v, beta4, g4 = _dn_prep(p, dn_conv_full, a_log4, dt_bias4)
    mix_half, s_all, a_inv_all = _dn_forward(q, k, v, beta4, g4, p, dn_out_norm_g)
    mix = _sg_forward(p, sg_norm_g, sg_w3, sg_b_t, mix_half)
    w_up_t, w_out_full, w_down_full = late_weights(mix)
    x2 = _matmul(mix, w_out_full, "nn", "out_proj", (1024, 1024, 1024), add=x2d)
    up, h2, rstd2 = _rmsnorm_matmul(x2, g2, w_up_t, "norm_up_proj", 256)
    act = _ffn_act(up, ffn_conv_full, conv_b)
    fn, outs = _final_loss_rows(t, D_MODEL)
    loss_lanes, dx3, dx3b, d_gf = _matmul_rows(act, w_down_full, "nn", "down_proj_loss", 512,
                                               [(x2, "rows"), (tgt, "rows"), (gf, "whole")], outs, fn)

    dact = _matmul(dx3b, w_down_full, "nt", "down_proj_dx", (512, D_FF, D_MODEL))
    d_w_down = _matmul(act, dx3b, "tn", "down_proj_dw", (256, 1024, t), out_dtype=BF16)
    sent_down, token = send_early([d_w_down.reshape(N_DEV, D_FF // N_DEV, D_MODEL)], d_w_down, "send_dw_down")
    dup, d_ffn_conv, d_ffn_conv_b = _ffn_bwd(up, ffn_conv_full, conv_b + token[0:1, 0:1], dact)
    fn, outs = _rmsnorm_bwd_rows(t, D_MODEL)
    dx2, dx2b, d_g2 = _matmul_rows(dup, w_up_t, "nn", "up_proj_dx_norm", 256,
                                   [(x2, "rows"), (rstd2, "rows"), (g2, "whole"), (dx3, "rows")], outs, fn)
    d_w_up_t = _matmul(dup, h2, "tn", "up_proj_dw", (512, 1024, t), out_dtype=BF16)
    dmix = _matmul(dx2b, w_out_full, "nt", "out_proj_dx", (1024, 1024, 1024))
    d_w_out = _matmul(mix, dx2b, "tn", "out_proj_dw", (512, 1024, t), out_dtype=BF16)
    sent_up_out, token = send_early(
        [d_w_up_t.reshape(N_DEV, 2 * D_FF // N_DEV, D_MODEL), d_w_out.reshape(N_DEV, D_MODEL // N_DEV, D_MODEL)],
        d_w_out, "send_dw_up_out")
    dp, d_sg_norm, d_sg_w, d_sg_b_t = _sg_backward(p, sg_norm_g + token[0:1, 0:1], sg_w3, sg_b_t, dmix)
    dq, dk, dv, dbeta4, dg4, dp, d_dn_norm = _dn_backward(q, k, v, beta4, g4, p, dn_out_norm_g, s_all, a_inv_all, dmix, dp)
    dc_dn, d_dn_conv, dp, d_a_log4, d_dt_bias4 = _dn_prep_bwd(p, dn_conv_full, a_log4, dt_bias4, dq, dk, dv, dbeta4, dg4, dp)
    small_grads = dict(
        attn_norm_g=jnp.zeros_like(attn_norm_g), dn_conv_w=d_dn_conv, dn_a_log=d_a_log4[:, :N_HEADS],
        dn_dt_bias=d_dt_bias4[:, :N_HEADS], dn_out_norm_g=d_dn_norm, sg_norm_g=d_sg_norm, sg_w=d_sg_w,
        sg_b=d_sg_b_t[:, :SG_GROUPS].T, ffn_norm_g=d_g2, ffn_conv_w=d_ffn_conv, ffn_conv_b=d_ffn_conv_b, final_norm_g=d_gf)
    sent_small, token = send_small(small_grads, loss_lanes, d_dn_conv)
    dp = _conv_bwd_input(dc_dn, dn_conv_full + token[0:1, 0:1], "dn_conv_dx", out_cols=PROJ_PAD, into=dp)
    d_w_in_t = _matmul(dp, h1, "tn", "in_proj_dw", (PROJ_PAD // 5, 1024, t), out_dtype=BF16)
    sent_in, token = send_early([d_w_in_t[:N_DEV * n_in].reshape(N_DEV, n_in, D_MODEL)], d_w_in_t, "send_dw_in")
    fn, outs = _rmsnorm_bwd_rows(t, D_MODEL)
    grad_x, _, d_g1 = _matmul_rows(dp, w_in_t, "nn", "in_proj_dx_norm", 512,
                                   [(x2d, "rows"), (rstd1, "rows"), (g1 + token[0:1, 0:1], "whole"), (dx2, "rows")], outs, fn)

    return grad_x, d_g1, (sent_down, sent_up_out, sent_small, sent_in)
```

```python
import math

import jax
import jax.numpy as jnp
from jax import lax
from jax.experimental import pallas as pl
from jax.experimental.pallas import tpu as pltpu

F32 = jnp.float32
BF16 = jnp.bfloat16
HI = lax.Precision.HIGHEST

D_MODEL = 1024
DN_WIDTH = 512
HEAD_DIM = 128
N_HEADS = 4
SG_WIDTH = 512
SG_GROUPS = 4
SG_DIM = 128
SG_BLOCK = 128
D_FF = 2816
CHUNK = 64
CONV_K = 4
FFN_CONV = 3
EPS = 1e-6
PROJ_MAIN = 3072
PROJ_PAD = 3200
GELU_C = math.sqrt(2.0 / math.pi)
N_DEV = 8
LANES = 128
SUBLANES = 8
HALO = SUBLANES
VMEM_LIMIT = 48 * 1024 * 1024

ADAM_LR = 0.001
ADAM_B1 = 0.9
ADAM_B2 = 0.999
ADAM_EPS = 1e-08
ADAM_WD = 0.01
ADAM_STEP = 10

MESH_ID = pl.DeviceIdType.MESH


def _pcall(body, **kw):
    return pl.pallas_call(body, **kw)


def _params(*sem):
    return pltpu.CompilerParams(dimension_semantics=sem, vmem_limit_bytes=VMEM_LIMIT)


def _pick(n, cap):
    best = None
    for t in range(LANES, cap + 1, LANES):
        if n % t == 0:
            best = t
    return best if best else n


FAST, EXACT = "bf16 operands, one pass", "f32 operands, six bf16 passes"


def dot_f32(a, b, dims, tier):
    if tier == FAST:
        return lax.dot_general(a.astype(BF16), b.astype(BF16), dims, preferred_element_type=F32)
    return lax.dot_general(a, b, dims, precision=HI, preferred_element_type=F32)


def dot_nn(a, b, tier=EXACT):
    return dot_f32(a, b, (((1,), (0,)), ((), ())), tier)


def dot_nt(a, b, tier=EXACT):
    return dot_f32(a, b, (((1,), (1,)), ((), ())), tier)


def dot_tn(a, b, tier=EXACT):
    return dot_f32(a, b, (((0,), (0,)), ((), ())), tier)


def sigmoid(x):
    return 1.0 / (1.0 + jnp.exp(-x))


def silu(x):
    return x * sigmoid(x)


def silu_grad(x):
    s = sigmoid(x)
    return s * (1.0 + x * (1.0 - s))


def gelu(x):
    return 0.5 * x * (1.0 + jnp.tanh(GELU_C * (x + 0.044715 * x * x * x)))


def gelu_grad(x):
    t = jnp.tanh(GELU_C * (x + 0.044715 * x * x * x))
    return 0.5 * (1.0 + t) + 0.5 * x * (1.0 - t * t) * GELU_C * (1.0 + 3.0 * 0.044715 * x * x)


def softplus(z):
    return jnp.maximum(z, 0.0) + jnp.log(1.0 + jnp.exp(-jnp.abs(z)))


def rms_fwd(x, g):
    r = lax.rsqrt(jnp.mean(x * x, axis=-1, keepdims=True) + EPS)
    return x * r * g, r


def rms_bwd(x, r, g, dy):
    dyg = dy * g
    xr = x * r
    dx = r * (dyg - xr * jnp.mean(dyg * xr, axis=-1, keepdims=True))
    return dx, dy * xr


def l2_fwd(x):
    r = lax.rsqrt(jnp.sum(x * x, axis=-1, keepdims=True) + EPS)
    return x * r, r


def l2_bwd(x, r, dy):
    xr = x * r
    return r * (dy - xr * jnp.sum(dy * xr, axis=-1, keepdims=True))


def _tri_masks(n):
    row = lax.broadcasted_iota(jnp.int32, (n, n), 0)
    col = lax.broadcasted_iota(jnp.int32, (n, n), 1)
    return row >= col, row > col


def chunk_cumsum(g4):
    incl, _ = _tri_masks(g4.shape[0])
    return dot_nn(incl.astype(F32), g4)


STACK = N_HEADS * CHUNK
DN_FWD_CHUNKS = 8
DN_CHUNKS = 4


def _head_rows(h):
    return slice(h * CHUNK, (h + 1) * CHUNK)


def _stack_heads(x):
    return jnp.concatenate([x[:, h * HEAD_DIM:(h + 1) * HEAD_DIM] for h in range(N_HEADS)], axis=0)


def _stack_lanes(x4):
    return jnp.concatenate([x4[:, h:h + 1] for h in range(N_HEADS)], axis=0)


def _per_head(fn):
    return jnp.concatenate([fn(h) for h in range(N_HEADS)], axis=0)


def _unit_lower_inverses(l_strict, order):
    c = l_strict[0].shape[0]
    row = lax.broadcasted_iota(jnp.int32, (c, c), 0)
    col = lax.broadcasted_iota(jnp.int32, (c, c), 1)
    eye = (row == col).astype(F32)
    p = [-l for l in l_strict]
    a = [eye + n for n in p]
    for _ in range(int(math.log2(order)) - 1):
        p = [dot_nn(x, x, FAST) for x in p]
        a = [x + dot_nn(x, y, FAST) for x, y in zip(a, p)]
    return a


def dn_chunks_local(chunks, inverses=None):
    row = lax.broadcasted_iota(jnp.int32, (STACK, STACK), 0)
    col = lax.broadcasted_iota(jnp.int32, (STACK, STACK), 1)
    same = (row // CHUNK) == (col // CHUNK)
    incl = jnp.logical_and(same, row >= col)
    strict = jnp.logical_and(same, row > col)
    locs = []
    for q, k, v, beta, gc4 in chunks:
        gc_col = _stack_lanes(gc4)
        gc_row = jnp.sum(jnp.where(row == col, gc_col, 0.0), axis=0, keepdims=True)
        decay = jnp.where(incl, jnp.exp(jnp.minimum(gc_col - gc_row, 0.0)), 0.0)
        gamma = jnp.exp(gc_col)
        gc_last = jnp.concatenate([jnp.broadcast_to(gc4[CHUNK - 1:CHUNK, h:h + 1], (CHUNK, 1)) for h in range(N_HEADS)], axis=0)
        tau = jnp.exp(gc_last - gc_col)
        kb = k * beta
        locs.append(dict(decay=decay, gamma=gamma, tau=tau, cd=jnp.exp(gc_last), kb=kb, qd=q * gamma, kt=k * tau,
                         incl=incl, strict=strict))
    for loc, (q, k, v, beta, gc4) in zip(locs, chunks):
        loc["l_mat"] = jnp.where(strict, dot_nt(loc["kb"], k, FAST) * loc["decay"], 0.0)
    if inverses is None:
        inverses = _unit_lower_inverses([loc["l_mat"] for loc in locs], CHUNK)
    for loc, a_inv in zip(locs, inverses):
        loc["a_inv"] = a_inv
    for loc, (q, k, v, beta, gc4) in zip(locs, chunks):
        sol = dot_nn(loc["a_inv"], jnp.concatenate([v * beta, loc["kb"] * loc["gamma"]], axis=1), FAST)
        loc.update(sol=sol, value=sol[:, :HEAD_DIM], kcd=sol[:, HEAD_DIM:])
        loc["attn"] = jnp.where(incl, dot_nt(q, k, FAST) * loc["decay"], 0.0)
    return locs


def dn_chunk_state(loc, s):
    kcd, qd, kt, cd = loc["kcd"], loc["qd"], loc["kt"], loc["cd"]
    v_new = loc["value"] - _per_head(lambda h: dot_nn(kcd[_head_rows(h)], s[h], FAST))
    o = _per_head(lambda h: dot_nn(qd[_head_rows(h)], s[h], FAST)) + dot_nn(loc["attn"], v_new, FAST)
    s_new = [s[h] * cd[h * CHUNK:h * CHUNK + 1, :] + dot_tn(kt[_head_rows(h)], v_new[_head_rows(h)], FAST)
             for h in range(N_HEADS)]
    loc["v_new"] = v_new
    return o, s_new


def dn_chunks_bwd(items, ds_last):
    hr = _head_rows
    n = len(items)
    pre = []
    for q, k, v, beta, loc, s, do in items:
        pre.append(dict(
            dv_part=dot_tn(loc["attn"], do, FAST),
            dattn=jnp.where(loc["incl"], dot_nt(do, loc["v_new"], FAST), 0.0),
            dqd=_per_head(lambda h: dot_nt(do[hr(h)], s[h], FAST)),
            ds_part=[dot_tn(loc["qd"][hr(h)], do[hr(h)], FAST) for h in range(N_HEADS)]))
    ds_new_of, dv_new_of = [None] * n, [None] * n
    ds = ds_last
    for c in reversed(range(n)):
        loc = items[c][4]
        ds_new_of[c] = ds
        dv_new = pre[c]["dv_part"] + _per_head(lambda h: dot_nn(loc["kt"][hr(h)], ds[h], FAST))
        dv_new_of[c] = dv_new
        ds = [pre[c]["ds_part"][h] + ds[h] * loc["cd"][h * CHUNK:h * CHUNK + 1, :]
              - dot_tn(loc["kcd"][hr(h)], dv_new[hr(h)], FAST) for h in range(N_HEADS)]
    is_last = (lax.broadcasted_iota(jnp.int32, (STACK, 1), 0) % CHUNK) == CHUNK - 1
    out = []
    for c, (q, k, v, beta, loc, s, do) in enumerate(items):
        decay, gamma, tau, cd, kb = loc["decay"], loc["gamma"], loc["tau"], loc["cd"], loc["kb"]
        dv_new, ds_new, dattn, dqd = dv_new_of[c], ds_new_of[c], pre[c]["dattn"], pre[c]["dqd"]
        dkt = _per_head(lambda h: dot_nt(loc["v_new"][hr(h)], ds_new[h], FAST))
        dkcd = -_per_head(lambda h: dot_nt(dv_new[hr(h)], s[h], FAST))
        drhs = dot_tn(loc["a_inv"], jnp.concatenate([dv_new, dkcd], axis=1), FAST)
        dvb, dkbg = drhs[:, :HEAD_DIM], drhs[:, HEAD_DIM:]
        dl = jnp.where(loc["strict"], -dot_nt(drhs, loc["sol"], FAST), 0.0)
        dkk = dl * decay
        dqk = dattn * decay
        e = dl * loc["l_mat"] + dattn * loc["attn"]
        dgc = jnp.sum(e, axis=1, keepdims=True) - jnp.sum(e, axis=0, keepdims=True).T
        dkb = dot_nn(dkk, k, FAST) + dkbg * gamma
        dk = dot_tn(dkk, kb, FAST) + dot_tn(dqk, q, FAST) + dkt * tau
        dq = dot_nn(dqk, k, FAST) + dqd * gamma
        dgamma = jnp.sum(dkbg * kb, axis=1, keepdims=True) + jnp.sum(dqd * q, axis=1, keepdims=True)
        dtau_tau = jnp.sum(dkt * k, axis=1, keepdims=True) * tau
        dgc = dgc + dgamma * gamma - dtau_tau

        def last_term(h):
            dcd = jnp.sum(jnp.sum(ds_new[h] * s[h], axis=1, keepdims=True), axis=0, keepdims=True)
            total = jnp.sum(dtau_tau[hr(h)], axis=0, keepdims=True) + dcd * cd[h * CHUNK:h * CHUNK + 1, :]
            return jnp.broadcast_to(total, (CHUNK, 1))

        dgc = dgc + jnp.where(is_last, _per_head(last_term), 0.0)
        dk = dk + dkb * beta
        dbeta = jnp.sum(dkb * k, axis=1, keepdims=True) + jnp.sum(dvb * v, axis=1, keepdims=True)
        out.append((dq, dk, dvb * beta, dbeta, dgc))
    return out, ds


def _token_tile(t):
    return _pick(t, 256)


STRIP = 32


def _for_strips(n_rows, rows, fn):
    def step(r, carry):
        fn(pl.multiple_of(r * rows, rows))
        return carry

    lax.fori_loop(0, n_rows // rows, step, 0)


def _fold_rows(x):
    out = x[0:SUBLANES, :]
    for i in range(1, x.shape[0] // SUBLANES):
        out = out + x[i * SUBLANES:(i + 1) * SUBLANES, :]
    return out


def _matmul(a, b, mode, name, tiles, add=None, out_dtype=F32):
    if mode == "nn":
        (m, k), n = a.shape, b.shape[1]
    elif mode == "nt":
        (m, k), n = a.shape, b.shape[0]
    else:
        (k, m), n = a.shape, b.shape[1]
    tm, tn, tk = min(tiles[0], m), min(tiles[1], n), min(tiles[2], k)
    assert m % tm == 0 and n % tn == 0 and k % tk == 0, (name, m, n, k, tiles)
    nk = k // tk
    dims = {"nn": (((1,), (0,)), ((), ())), "nt": (((1,), (1,)), ((), ())), "tn": (((0,), (0,)), ((), ()))}[mode]

    def finish(res, add_ref, o_ref):
        if add_ref is not None:
            res = res + add_ref[...]
        o_ref[...] = res.astype(o_ref.dtype)

    def body(*refs):
        a_ref, b_ref = refs[0], refs[1]
        add_ref = refs[2] if add is not None else None
        o_ref = refs[3] if add is not None else refs[2]
        part = lax.dot_general(a_ref[...], b_ref[...], dims, preferred_element_type=F32)
        if nk == 1:
            finish(part, add_ref, o_ref)
            return
        acc_ref = refs[-1]
        kk = pl.program_id(2)

        @pl.when(kk == 0)
        def _():
            acc_ref[...] = part

        @pl.when(kk > 0)
        def _():
            acc_ref[...] += part

        @pl.when(kk == nk - 1)
        def _():
            finish(acc_ref[...], add_ref, o_ref)

    a_spec = pl.BlockSpec((tk, tm), lambda j, i, kk: (kk, i)) if mode == "tn" else pl.BlockSpec((tm, tk), lambda j, i, kk: (i, kk))
    b_spec = pl.BlockSpec((tn, tk), lambda j, i, kk: (j, kk)) if mode == "nt" else pl.BlockSpec((tk, tn), lambda j, i, kk: (kk, j))
    o_spec = pl.BlockSpec((tm, tn), lambda j, i, kk: (i, j))
    in_specs = [a_spec, b_spec] + ([o_spec] if add is not None else [])
    args = (a, b) + ((add,) if add is not None else ())
    return _pcall(
        body, grid=(n // tn, m // tm, nk), in_specs=in_specs, out_specs=o_spec,
        out_shape=jax.ShapeDtypeStruct((m, n), out_dtype),
        scratch_shapes=[pltpu.VMEM((tm, tn), F32)] if nk > 1 else [],
        compiler_params=_params("parallel", "parallel", "arbitrary"), name=name)(*args)


def _matmul_rows(a, b, mode, name, tm, extra, outs, fn):
    m, k = a.shape
    n = b.shape[1] if mode == "nn" else b.shape[0]
    tm = min(tm, m)
    dims = (((1,), (0,)), ((), ())) if mode == "nn" else (((1,), (1,)), ((), ()))

    def spec(shape, kind):
        if kind == "rows":
            return pl.BlockSpec((tm, shape[1]), lambda i: (i, 0))
        return pl.BlockSpec(shape, lambda i: (0,) * len(shape))

    def body(a_ref, b_ref, *refs):
        rows = lax.dot_general(a_ref[...], b_ref[...], dims, preferred_element_type=F32)
        fn(rows, pl.program_id(0) == 0, *refs)

    return _pcall(
        body, grid=(m // tm,),
        in_specs=[pl.BlockSpec((tm, k), lambda i: (i, 0)), pl.BlockSpec(b.shape, lambda i: (0, 0))]
        + [spec(x.shape, kind) for x, kind in extra],
        out_specs=[spec(shape, kind) for shape, _, kind in outs],
        out_shape=[jax.ShapeDtypeStruct(shape, dtype) for shape, dtype, _ in outs],
        compiler_params=_params("arbitrary"), name=name)(a, b, *[x for x, _ in extra])


def _rmsnorm_matmul(x, g, b_t, name, tm):
    t, d = x.shape
    n = b_t.shape[0]
    tm = min(tm, t)

    def body(x_ref, g_ref, b_ref, o_ref, h_ref, r_ref):
        y, r = rms_fwd(x_ref[...], g_ref[...])
        h = y.astype(BF16)
        h_ref[...] = h
        r_ref[...] = r
        o_ref[...] = lax.dot_general(h, b_ref[...], (((1,), (1,)), ((), ())), preferred_element_type=F32)

    rows = lambda w: pl.BlockSpec((tm, w), lambda i: (i, 0))
    return _pcall(
        body, grid=(t // tm,),
        in_specs=[rows(d), pl.BlockSpec((1, d), lambda i: (0, 0)), pl.BlockSpec((n, d), lambda i: (0, 0))],
        out_specs=[rows(n), rows(d), rows(1)],
        out_shape=[jax.ShapeDtypeStruct((t, n), F32), jax.ShapeDtypeStruct((t, d), BF16), jax.ShapeDtypeStruct((t, 1), F32)],
        compiler_params=_params("parallel"), name=name)(x, g, b_t)


def _rmsnorm_bwd_rows(t, d):
    def fn(dh, first, x_ref, r_ref, g_ref, dres_ref, dx_ref, dxb_ref, dg_ref):
        dx, dg_rows = rms_bwd(x_ref[...], r_ref[...], g_ref[...], dh)
        dx = dx + dres_ref[...]
        dx_ref[...] = dx
        dxb_ref[...] = dx.astype(BF16)

        @pl.when(first)
        def _():
            dg_ref[...] = jnp.zeros_like(dg_ref)

        dg_ref[...] += jnp.sum(dg_rows, axis=0, keepdims=True)

    return fn, [((t, d), F32, "rows"), ((t, d), BF16, "rows"), ((1, d), F32, "whole")]


def _final_loss_rows(t, d):
    def fn(rows, first, res_ref, t_ref, g_ref, loss_ref, dx_ref, dxb_ref, dg_ref):
        @pl.when(first)
        def _():
            loss_ref[...] = jnp.zeros_like(loss_ref)
            dg_ref[...] = jnp.zeros_like(dg_ref)

        x = rows + res_ref[...]
        y, r = rms_fwd(x, g_ref[...])
        err = y - t_ref[...]
        loss_ref[...] += 0.5 * jnp.sum(jnp.mean(err * err, axis=-1, keepdims=True), axis=0, keepdims=True)
        dx, dg_rows = rms_bwd(x, r, g_ref[...], err * (1.0 / d))
        dx_ref[...] = dx
        dxb_ref[...] = dx.astype(BF16)
        dg_ref[...] += jnp.sum(dg_rows, axis=0, keepdims=True)

    return fn, [((1, LANES), F32, "whole"), ((t, d), F32, "rows"), ((t, d), BF16, "rows"), ((1, d), F32, "whole")]


def _prev_halo_spec(tm, width, col_block):
    return pl.BlockSpec((HALO, width), lambda i: (jnp.maximum(i * (tm // HALO) - 1, 0), col_block))


def _fill_with_prev(xp_ref, tile, halo, first):
    xp_ref[0:HALO, :] = jnp.where(first, 0.0, halo)
    xp_ref[HALO:, :] = tile


def _delayed(xp_ref, row0, cols, taps):
    ext = xp_ref[pl.ds(row0, STRIP + HALO), cols]
    return [ext[HALO:, :]] + [pltpu.roll(ext, j, 0)[HALO:, :] for j in range(1, taps)]


def _causal_conv(delayed, w):
    taps = len(delayed)
    out = delayed[0] * w[taps - 1:taps, :]
    for j in range(1, taps):
        out = out + delayed[j] * w[taps - 1 - j:taps - j, :]
    return out


def _advanced_conv(buf_ref, row0, cols, w):
    taps = w.shape[0]
    ext = buf_ref[pl.ds(row0, STRIP + HALO), cols]
    out = ext[:STRIP, :] * w[taps - 1:taps, :]
    for j in range(1, taps):
        out = out + pltpu.roll(ext, STRIP + HALO - j, 0)[:STRIP, :] * w[taps - 1 - j:taps - j, :]
    return out


def _dn_prep(p, conv_w, a_log4, dt_bias4):
    t = p.shape[0]
    tm = _token_tile(t)
    w3 = 3 * DN_WIDTH

    def body(x_ref, halo_ref, pbd_ref, w_ref, alog_ref, dtb_ref, q_ref, k_ref, v_ref, beta_ref, g_ref, xp_ref):
        _fill_with_prev(xp_ref, x_ref[...], halo_ref[...], pl.program_id(0) == 0)

        def strip(row0):
            rows = pl.ds(row0, STRIP)
            for h in range(N_HEADS):
                sl = slice(h * HEAD_DIM, (h + 1) * HEAD_DIM)
                for part, out_ref in ((0, q_ref), (1, k_ref), (2, v_ref)):
                    cols = slice(part * DN_WIDTH + h * HEAD_DIM, part * DN_WIDTH + (h + 1) * HEAD_DIM)
                    y = silu(_causal_conv(_delayed(xp_ref, row0, cols, CONV_K), w_ref[:, cols]))
                    if part == 0:
                        y = l2_fwd(y)[0] * (HEAD_DIM ** -0.5)
                    elif part == 1:
                        y = l2_fwd(y)[0]
                    out_ref[rows, sl] = y
            head = lax.broadcasted_iota(jnp.int32, (STRIP, LANES), 1) < N_HEADS
            pbd = pbd_ref[rows, :]
            beta_ref[rows, :] = jnp.where(head, sigmoid(pbd), 0.0)
            a_raw = pltpu.roll(pbd, LANES - N_HEADS, 1)
            g_ref[rows, :] = jnp.where(head, -jnp.exp(alog_ref[...]) * softplus(a_raw + dtb_ref[...]), 0.0)

        _for_strips(tm, STRIP, strip)

    tok = lambda w, cb: pl.BlockSpec((tm, w), lambda i: (i, cb))
    full = lambda a: pl.BlockSpec(a.shape, lambda i: (0, 0))
    return _pcall(
        body, grid=(t // tm,),
        in_specs=[tok(w3, 0), _prev_halo_spec(tm, w3, 0), tok(LANES, PROJ_MAIN // LANES),
                  full(conv_w), full(a_log4), full(dt_bias4)],
        out_specs=[tok(DN_WIDTH, 0)] * 3 + [tok(LANES, 0)] * 2,
        out_shape=[jax.ShapeDtypeStruct((t, DN_WIDTH), F32)] * 3 + [jax.ShapeDtypeStruct((t, LANES), F32)] * 2,
        scratch_shapes=[pltpu.VMEM((HALO + tm, w3), F32)],
        compiler_params=_params("parallel"), name="dn_prep")(p, p, p, conv_w, a_log4, dt_bias4)


def _dn_prep_bwd(p, conv_w, a_log4, dt_bias4, dq, dk, dv, dbeta4, dg4, dp_buf):
    t = p.shape[0]
    tm = _token_tile(t)
    w3 = 3 * DN_WIDTH

    def body(x_ref, halo_ref, pbd_ref, w_ref, alog_ref, dtb_ref, dq_ref, dk_ref, dv_ref, dbeta_ref, dg_ref, _,
             dc_ref, dw_ref, dpbd_ref, dalog_ref, ddtb_ref, xp_ref, dw_acc, lane_acc):
        first = pl.program_id(0) == 0
        _fill_with_prev(xp_ref, x_ref[...], halo_ref[...], first)
        dw_acc[...] = jnp.zeros_like(dw_acc)
        lane_acc[...] = jnp.zeros_like(lane_acc)

        def strip(row0):
            rows = pl.ds(row0, STRIP)
            for h in range(N_HEADS):
                sl = slice(h * HEAD_DIM, (h + 1) * HEAD_DIM)
                for part, dy_ref in ((0, dq_ref), (1, dk_ref), (2, dv_ref)):
                    cols = slice(part * DN_WIDTH + h * HEAD_DIM, part * DN_WIDTH + (h + 1) * HEAD_DIM)
                    delayed = _delayed(xp_ref, row0, cols, CONV_K)
                    c = _causal_conv(delayed, w_ref[:, cols])
                    dy = dy_ref[rows, sl]
                    if part < 2:
                        y = silu(c)
                        _, r = l2_fwd(y)
                        dy = l2_bwd(y, r, dy * (HEAD_DIM ** -0.5) if part == 0 else dy)
                    dc = dy * silu_grad(c)
                    dc_ref[rows, cols] = dc
                    for j in range(CONV_K):
                        k = CONV_K - 1 - j
                        dw_acc[k * SUBLANES:(k + 1) * SUBLANES, cols] += _fold_rows(dc * delayed[j])
            head = lax.broadcasted_iota(jnp.int32, (STRIP, LANES), 1) < N_HEADS
            pbd = pbd_ref[rows, :]
            beta = sigmoid(pbd)
            dpb = jnp.where(head, dbeta_ref[rows, :] * beta * (1.0 - beta), 0.0)
            z = pltpu.roll(pbd, LANES - N_HEADS, 1) + dtb_ref[...]
            neg_rate = -jnp.exp(alog_ref[...])
            dg = dg_ref[rows, :]
            dpa = jnp.where(head, dg * neg_rate * sigmoid(z), 0.0)
            dpbd_ref[rows, :] = (dpb + pltpu.roll(dpa, N_HEADS, 1)).astype(BF16)
            g = jnp.where(head, neg_rate * softplus(z), 0.0)
            lane_acc[0:SUBLANES, :] += _fold_rows(dg * g)
            lane_acc[SUBLANES:, :] += _fold_rows(dpa)

        _for_strips(tm, STRIP, strip)

        @pl.when(first)
        def _():
            dw_ref[...] = jnp.zeros_like(dw_ref)
            dalog_ref[...] = jnp.zeros_like(dalog_ref)
            ddtb_ref[...] = jnp.zeros_like(ddtb_ref)

        for k in range(CONV_K):
            dw_ref[k:k + 1, :] += jnp.sum(dw_acc[k * SUBLANES:(k + 1) * SUBLANES, :], axis=0, keepdims=True)
        dalog_ref[...] += jnp.sum(lane_acc[0:SUBLANES, :], axis=0, keepdims=True)
        ddtb_ref[...] += jnp.sum(lane_acc[SUBLANES:, :], axis=0, keepdims=True)

    tok = lambda w, cb: pl.BlockSpec((tm, w), lambda i: (i, cb))
    full = lambda shape: pl.BlockSpec(shape, lambda i: (0, 0))
    return _pcall(
        body, grid=(t // tm,),
        in_specs=[tok(w3, 0), _prev_halo_spec(tm, w3, 0), tok(LANES, PROJ_MAIN // LANES),
                  full(conv_w.shape), full(a_log4.shape), full(dt_bias4.shape)] + [tok(DN_WIDTH, 0)] * 3 + [tok(LANES, 0)] * 2
        + [pl.BlockSpec(memory_space=pl.ANY)],
        out_specs=[tok(w3, 0), full((CONV_K, w3)), tok(LANES, PROJ_MAIN // LANES), full((1, LANES)), full((1, LANES))],
        out_shape=[jax.ShapeDtypeStruct((t, w3), F32), jax.ShapeDtypeStruct((CONV_K, w3), F32),
                   jax.ShapeDtypeStruct(dp_buf.shape, dp_buf.dtype),
                   jax.ShapeDtypeStruct((1, LANES), F32), jax.ShapeDtypeStruct((1, LANES), F32)],
        input_output_aliases={11: 2},
        scratch_shapes=[pltpu.VMEM((HALO + tm, w3), F32), pltpu.VMEM((CONV_K * SUBLANES, w3), F32),
                        pltpu.VMEM((2 * SUBLANES, LANES), F32)],
        compiler_params=_params("arbitrary"), name="dn_prep_bwd")(p, p, p, conv_w, a_log4, dt_bias4, dq, dk, dv, dbeta4, dg4, dp_buf)


def _conv_bwd_input(dc, w, name, out_cols=None, col_block=0, into=None):
    t, c = dc.shape
    taps = w.shape[0]
    tm = _token_tile(t)
    ct = _pick(c, 1536)
    n_tok = t // tm
    out_cols = c if out_cols is None else out_cols

    def body(dc_ref, next_ref, w_ref, *rest):
        dx_ref, buf_ref = rest[-2], rest[-1]
        buf_ref[0:tm, :] = dc_ref[...]
        buf_ref[tm:, :] = jnp.where(pl.program_id(0) == n_tok - 1, 0.0, next_ref[...])

        def strip(row0):
            for c0 in range(0, ct, LANES):
                cols = slice(c0, c0 + LANES)
                dx_ref[pl.ds(row0, STRIP), cols] = _advanced_conv(buf_ref, row0, cols, w_ref[:, cols]).astype(BF16)

        _for_strips(tm, STRIP, strip)

    in_specs = [pl.BlockSpec((tm, ct), lambda i, j: (i, j)),
                pl.BlockSpec((HALO, ct), lambda i, j: (jnp.minimum((i + 1) * (tm // HALO), t // HALO - 1), j)),
                pl.BlockSpec((taps, ct), lambda i, j: (0, j))]
    args = (dc, dc, w)
    aliases = {}
    if into is not None:
        in_specs.append(pl.BlockSpec(memory_space=pl.ANY))
        args += (into,)
        aliases = {3: 0}
    return _pcall(
        body, grid=(n_tok, c // ct), in_specs=in_specs,
        out_specs=pl.BlockSpec((tm, ct), lambda i, j: (i, j + col_block)),
        out_shape=jax.ShapeDtypeStruct((t, out_cols), BF16), input_output_aliases=aliases,
        scratch_shapes=[pltpu.VMEM((tm + HALO, ct), F32)],
        compiler_params=_params("parallel", "parallel"), name=name)(*args)


def _dn_forward(q, k, v, beta4, g4, p, norm_g):
    t = q.shape[0]
    n = t // CHUNK
    nc = DN_FWD_CHUNKS
    rows_per_step = nc * CHUNK

    def body(q_ref, k_ref, v_ref, b_ref, g_ref, gate_ref, ng_ref, mix_ref, s_all_ref, ainv_ref, s_ref):
        @pl.when(pl.program_id(0) == 0)
        def _():
            s_ref[...] = jnp.zeros_like(s_ref)

        chunks = []
        for c in range(nc):
            rows = slice(c * CHUNK, (c + 1) * CHUNK)
            chunks.append((_stack_heads(q_ref[rows, :]), _stack_heads(k_ref[rows, :]), _stack_heads(v_ref[rows, :]),
                           _stack_lanes(b_ref[rows, :]), chunk_cumsum(g_ref[rows, :])))
        locs = dn_chunks_local(chunks)
        s = [s_ref[h] for h in range(N_HEADS)]
        for c in range(nc):
            rows = slice(c * CHUNK, (c + 1) * CHUNK)
            ainv_ref[c] = locs[c]["a_inv"].astype(BF16)
            for h in range(N_HEADS):
                s_all_ref[c, h] = s[h]
            o, s = dn_chunk_state(locs[c], s)
            o_n, _ = rms_fwd(o, ng_ref[...])
            for h in range(N_HEADS):
                sl = slice(h * HEAD_DIM, (h + 1) * HEAD_DIM)
                mix_ref[rows, sl] = (o_n[_head_rows(h)] * silu(gate_ref[rows, sl])).astype(BF16)
        for h in range(N_HEADS):
            s_ref[h] = s[h]

    ch = lambda w, cb: pl.BlockSpec((rows_per_step, w), lambda i: (i, cb))
    return _pcall(
        body, grid=(n // nc,),
        in_specs=[ch(DN_WIDTH, 0)] * 3 + [ch(LANES, 0)] * 2 + [ch(DN_WIDTH, 3), pl.BlockSpec((1, HEAD_DIM), lambda i: (0, 0))],
        out_specs=[ch(DN_WIDTH, 0), pl.BlockSpec((nc, N_HEADS, HEAD_DIM, HEAD_DIM), lambda i: (i, 0, 0, 0)),
                   pl.BlockSpec((nc, STACK, STACK), lambda i: (i, 0, 0))],
        out_shape=[jax.ShapeDtypeStruct((t, DN_WIDTH + SG_WIDTH), BF16), jax.ShapeDtypeStruct((n, N_HEADS, HEAD_DIM, HEAD_DIM), F32),
                   jax.ShapeDtypeStruct((n, STACK, STACK), BF16)],
        scratch_shapes=[pltpu.VMEM((N_HEADS, HEAD_DIM, HEAD_DIM), F32)],
        compiler_params=_params("arbitrary"), name="dn_forward")(q, k, v, beta4, g4, p, norm_g)


def _dn_backward(q, k, v, beta4, g4, p, norm_g, s_all, a_inv_all, dmix, dp_buf):
    t = q.shape[0]
    n = t // CHUNK
    steps = n // DN_CHUNKS
    rows_per_step = DN_CHUNKS * CHUNK

    def body(q_ref, k_ref, v_ref, b_ref, g_ref, gate_ref, ng_ref, s_in_ref, ainv_ref, dmix_ref, _,
             dq_ref, dk_ref, dv_ref, db_ref, dg_ref, dgate_ref, dng_ref, ds_ref):
        @pl.when(pl.program_id(0) == 0)
        def _():
            ds_ref[...] = jnp.zeros_like(ds_ref)
            dng_ref[...] = jnp.zeros_like(dng_ref)

        chunks = []
        for c in range(DN_CHUNKS):
            rows = slice(c * CHUNK, (c + 1) * CHUNK)
            chunks.append((_stack_heads(q_ref[rows, :]), _stack_heads(k_ref[rows, :]), _stack_heads(v_ref[rows, :]),
                           _stack_lanes(b_ref[rows, :]), chunk_cumsum(g_ref[rows, :])))
        items = []
        for c, loc in enumerate(dn_chunks_local(chunks, [ainv_ref[c] for c in range(DN_CHUNKS)])):
            rows = slice(c * CHUNK, (c + 1) * CHUNK)
            s = [s_in_ref[c, h] for h in range(N_HEADS)]
            o, _ = dn_chunk_state(loc, s)
            o_n, r = rms_fwd(o, ng_ref[...])
            gate = _stack_heads(gate_ref[rows, :])
            dmx = _stack_heads(dmix_ref[rows, :])
            dgate = dmx * o_n * silu_grad(gate)
            do, dng_rows = rms_bwd(o, r, ng_ref[...], dmx * silu(gate))
            dng_ref[...] += jnp.sum(dng_rows, axis=0, keepdims=True)
            for h in range(N_HEADS):
                dgate_ref[rows, h * HEAD_DIM:(h + 1) * HEAD_DIM] = dgate[_head_rows(h)].astype(BF16)
            items.append((*chunks[c][:4], loc, s, do))
        grads, ds = dn_chunks_bwd(items, [ds_ref[h] for h in range(N_HEADS)])
        lane = lax.broadcasted_iota(jnp.int32, (CHUNK, LANES), 1)
        _, strict = _tri_masks(CHUNK)
        for c in range(DN_CHUNKS):
            rows = slice(c * CHUNK, (c + 1) * CHUNK)
            dq, dk, dv, dbeta, dgc = grads[c]
            db4 = jnp.zeros((CHUNK, LANES), F32)
            dgc4 = jnp.zeros((CHUNK, LANES), F32)
            for h in range(N_HEADS):
                sl = slice(h * HEAD_DIM, (h + 1) * HEAD_DIM)
                head_rows = _head_rows(h)
                dq_ref[rows, sl] = dq[head_rows]
                dk_ref[rows, sl] = dk[head_rows]
                dv_ref[rows, sl] = dv[head_rows]
                db4 = jnp.where(lane == h, dbeta[head_rows], db4)
                dgc4 = jnp.where(lane == h, dgc[head_rows], dgc4)
            db_ref[rows, :] = db4
            dg_ref[rows, :] = dot_nn(jnp.logical_not(strict).astype(F32), dgc4)
        for h in range(N_HEADS):
            ds_ref[h] = ds[h]

    rev = lambda w, cb: pl.BlockSpec((rows_per_step, w), lambda i: (steps - 1 - i, cb))
    return _pcall(
        body, grid=(steps,),
        in_specs=[rev(DN_WIDTH, 0)] * 3 + [rev(LANES, 0)] * 2 + [rev(DN_WIDTH, 3), pl.BlockSpec((1, HEAD_DIM), lambda i: (0, 0)),
                  pl.BlockSpec((DN_CHUNKS, N_HEADS, HEAD_DIM, HEAD_DIM), lambda i: (steps - 1 - i, 0, 0, 0)),
                  pl.BlockSpec((DN_CHUNKS, STACK, STACK), lambda i: (steps - 1 - i, 0, 0)), rev(DN_WIDTH, 0),
                  pl.BlockSpec(memory_space=pl.ANY)],
        out_specs=[rev(DN_WIDTH, 0)] * 3 + [rev(LANES, 0)] * 2 + [rev(DN_WIDTH, 3), pl.BlockSpec((1, HEAD_DIM), lambda i: (0, 0))],
        out_shape=[jax.ShapeDtypeStruct((t, DN_WIDTH), F32)] * 3 + [jax.ShapeDtypeStruct((t, LANES), F32)] * 2
        + [jax.ShapeDtypeStruct(dp_buf.shape, dp_buf.dtype), jax.ShapeDtypeStruct((1, HEAD_DIM), F32)],
        input_output_aliases={10: 5},
        scratch_shapes=[pltpu.VMEM((N_HEADS, HEAD_DIM, HEAD_DIM), F32)],
        compiler_params=_params("arbitrary"), name="dn_backward")(q, k, v, beta4, g4, p, norm_g, s_all, a_inv_all, dmix, dp_buf)


def _sg_mask():
    row = lax.broadcasted_iota(jnp.int32, (SG_BLOCK, SG_BLOCK), 0)
    col = lax.broadcasted_iota(jnp.int32, (SG_BLOCK, SG_BLOCK), 1)
    return (col // CHUNK) <= (row // CHUNK)


def _sg_forward(p, norm_g, w_s, b_t, mix_buf):
    t = p.shape[0]

    def body(u_ref, v_ref, ng_ref, w_ref, b_ref, _, o_ref):
        mask = _sg_mask()
        for g in range(SG_GROUPS):
            sl = slice(g * SG_DIM, (g + 1) * SG_DIM)
            vn, _ = rms_fwd(gelu(v_ref[:, sl]), ng_ref[:, sl])
            s = dot_nn(jnp.where(mask, w_ref[g], 0.0), vn, FAST) + b_ref[:, g:g + 1]
            o_ref[:, sl] = (gelu(u_ref[:, sl]) * s).astype(BF16)

    blk = lambda cb: pl.BlockSpec((SG_BLOCK, SG_WIDTH), lambda i: (i, cb))
    return _pcall(
        body, grid=(t // SG_BLOCK,),
        in_specs=[blk(4), blk(5), pl.BlockSpec((1, SG_WIDTH), lambda i: (0, 0)),
                  pl.BlockSpec((SG_GROUPS, SG_BLOCK, SG_BLOCK), lambda i: (0, 0, 0)), pl.BlockSpec((SG_BLOCK, SG_GROUPS), lambda i: (0, 0)),
                  pl.BlockSpec(memory_space=pl.ANY)],
        out_specs=blk(1), out_shape=jax.ShapeDtypeStruct(mix_buf.shape, mix_buf.dtype), input_output_aliases={5: 0},
        compiler_params=_params("parallel"), name="sg_forward")(p, p, norm_g, w_s, b_t, mix_buf)


def _sg_backward(p, norm_g, w_s, b_t, dmix):
    t = p.shape[0]

    def body(u_ref, v_ref, ng_ref, w_ref, b_ref, do_ref, duv_ref, dng_ref, dw_ref, db_ref):
        @pl.when(pl.program_id(0) == 0)
        def _():
            dng_ref[...] = jnp.zeros_like(dng_ref)
            dw_ref[...] = jnp.zeros_like(dw_ref)
            db_ref[...] = jnp.zeros_like(db_ref)

        mask = _sg_mask()
        lane = lax.broadcasted_iota(jnp.int32, (SG_BLOCK, LANES), 1)
        db = jnp.zeros((SG_BLOCK, LANES), F32)
        for g in range(SG_GROUPS):
            sl = slice(g * SG_DIM, (g + 1) * SG_DIM)
            u_raw, v_raw, do = u_ref[:, sl], v_ref[:, sl], do_ref[:, sl]
            vg = gelu(v_raw)
            vn, r = rms_fwd(vg, ng_ref[:, sl])
            w_m = jnp.where(mask, w_ref[g], 0.0)
            s = dot_nn(w_m, vn, FAST) + b_ref[:, g:g + 1]
            duv_ref[:, sl] = (do * s * gelu_grad(u_raw)).astype(BF16)
            ds = do * gelu(u_raw)
            db = jnp.where(lane == g, jnp.sum(ds, axis=1, keepdims=True), db)
            dw_ref[g] += jnp.where(mask, dot_nt(ds, vn, FAST), 0.0)
            dvg, dng_rows = rms_bwd(vg, r, ng_ref[:, sl], dot_tn(w_m, ds, FAST))
            dng_ref[:, sl] += jnp.sum(dng_rows, axis=0, keepdims=True)
            duv_ref[:, SG_WIDTH + g * SG_DIM:SG_WIDTH + (g + 1) * SG_DIM] = (dvg * gelu_grad(v_raw)).astype(BF16)
        db_ref[...] += db

    blk = lambda cb: pl.BlockSpec((SG_BLOCK, SG_WIDTH), lambda i: (i, cb))
    const2 = lambda shape: pl.BlockSpec(shape, lambda i: (0, 0))
    w_spec = pl.BlockSpec((SG_GROUPS, SG_BLOCK, SG_BLOCK), lambda i: (0, 0, 0))
    return _pcall(
        body, grid=(t // SG_BLOCK,),
        in_specs=[blk(4), blk(5), const2((1, SG_WIDTH)), w_spec, const2((SG_BLOCK, SG_GROUPS)), blk(1)],
        out_specs=[pl.BlockSpec((SG_BLOCK, 2 * SG_WIDTH), lambda i: (i, 2)), const2((1, SG_WIDTH)), w_spec,
                   const2((SG_BLOCK, LANES))],
        out_shape=[jax.ShapeDtypeStruct((t, PROJ_PAD), BF16), jax.ShapeDtypeStruct((1, SG_WIDTH), F32),
                   jax.ShapeDtypeStruct((SG_GROUPS, SG_BLOCK, SG_BLOCK), F32), jax.ShapeDtypeStruct((SG_BLOCK, LANES), F32)],
        compiler_params=_params("arbitrary"), name="sg_backward")(p, p, norm_g, w_s, b_t, dmix)


FFN_CT = D_FF // 2


def _ffn_act(up, conv_w, conv_b):
    t = up.shape[0]
    tm = _token_tile(t)
    nj = D_FF // FFN_CT

    def body(ug_ref, uv_ref, hg_ref, hv_ref, wg_ref, wv_ref, bg_ref, bv_ref, act_ref, xg_ref, xv_ref):
        first = pl.program_id(0) == 0
        _fill_with_prev(xg_ref, ug_ref[...], hg_ref[...], first)
        _fill_with_prev(xv_ref, uv_ref[...], hv_ref[...], first)

        def strip(row0):
            for c0 in range(0, FFN_CT, LANES):
                cols = slice(c0, c0 + LANES)
                cg = _causal_conv(_delayed(xg_ref, row0, cols, FFN_CONV), wg_ref[:, cols]) + bg_ref[:, cols]
                cv = _causal_conv(_delayed(xv_ref, row0, cols, FFN_CONV), wv_ref[:, cols]) + bv_ref[:, cols]
                act_ref[pl.ds(row0, STRIP), cols] = (silu(cg) * cv).astype(BF16)

        _for_strips(tm, STRIP, strip)

    tok = lambda off: pl.BlockSpec((tm, FFN_CT), lambda i, j: (i, j + off))
    halo = lambda off: pl.BlockSpec((HALO, FFN_CT), lambda i, j: (jnp.maximum(i * (tm // HALO) - 1, 0), j + off))
    par = lambda rows, off: pl.BlockSpec((rows, FFN_CT), lambda i, j: (0, j + off))
    return _pcall(
        body, grid=(t // tm, nj),
        in_specs=[tok(0), tok(nj), halo(0), halo(nj), par(FFN_CONV, 0), par(FFN_CONV, nj), par(1, 0), par(1, nj)],
        out_specs=pl.BlockSpec((tm, FFN_CT), lambda i, j: (i, j)),
        out_shape=jax.ShapeDtypeStruct((t, D_FF), BF16),
        scratch_shapes=[pltpu.VMEM((HALO + tm, FFN_CT), F32)] * 2,
        compiler_params=_params("parallel", "parallel"), name="ffn_act")(up, up, up, up, conv_w, conv_w, conv_b, conv_b)


def _ffn_bwd(up, conv_w, conv_b, dact):
    t = up.shape[0]
    tm = _pick(t, 128)
    n_tok = t // tm
    width = 2 * D_FF

    def dconv(delayed_g, delayed_v, da, wg, wv, bg, bv):
        cg = _causal_conv(delayed_g, wg) + bg
        cv = _causal_conv(delayed_v, wv) + bv
        s = sigmoid(cg)
        return da * cv * (s * (1.0 + cg * (1.0 - s))), da * (cg * s)

    def body(up_ref, prev_ref, next_ref, da_ref, dan_ref, w_ref, b_ref, dup_ref, dw_ref, db_ref, xp_ref, dc_ref, dw_acc, db_acc):
        first = pl.program_id(0) == 0
        last = pl.program_id(0) == n_tok - 1
        xp_ref[0:HALO, :] = jnp.where(first, 0.0, prev_ref[...])
        xp_ref[HALO:HALO + tm, :] = up_ref[...]
        xp_ref[HALO + tm:, :] = next_ref[...]
        dw_acc[...] = jnp.zeros_like(dw_acc)
        db_acc[...] = jnp.zeros_like(db_acc)

        def strip(row0):
            rows = pl.ds(row0, STRIP)
            for c0 in range(0, D_FF, LANES):
                gc, vc = slice(c0, c0 + LANES), slice(D_FF + c0, D_FF + c0 + LANES)
                del_g, del_v = _delayed(xp_ref, row0, gc, FFN_CONV), _delayed(xp_ref, row0, vc, FFN_CONV)
                dcg, dcv = dconv(del_g, del_v, da_ref[rows, gc], w_ref[:, gc], w_ref[:, vc], b_ref[:, gc], b_ref[:, vc])
                dc_ref[rows, gc] = dcg
                dc_ref[rows, vc] = dcv
                db_acc[:, gc] += _fold_rows(dcg)
                db_acc[:, vc] += _fold_rows(dcv)
                for j in range(FFN_CONV):
                    k = FFN_CONV - 1 - j
                    dw_acc[k * SUBLANES:(k + 1) * SUBLANES, gc] += _fold_rows(dcg * del_g[j])
                    dw_acc[k * SUBLANES:(k + 1) * SUBLANES, vc] += _fold_rows(dcv * del_v[j])

        _for_strips(tm, STRIP, strip)

        for c0 in range(0, D_FF, LANES):
            gc, vc = slice(c0, c0 + LANES), slice(D_FF + c0, D_FF + c0 + LANES)

            def delayed(cols):
                ext = xp_ref[tm:tm + 2 * HALO, cols]
                return [ext[HALO:, :]] + [pltpu.roll(ext, j, 0)[HALO:, :] for j in range(1, FFN_CONV)]

            dcg, dcv = dconv(delayed(gc), delayed(vc), dan_ref[:, gc], w_ref[:, gc], w_ref[:, vc], b_ref[:, gc], b_ref[:, vc])
            dc_ref[tm:, gc] = jnp.where(last, 0.0, dcg)
            dc_ref[tm:, vc] = jnp.where(last, 0.0, dcv)

        def strip_dx(row0):
            for c0 in range(0, width, LANES):
                cols = slice(c0, c0 + LANES)
                dup_ref[pl.ds(row0, STRIP), cols] = _advanced_conv(dc_ref, row0, cols, w_ref[:, cols]).astype(BF16)

        _for_strips(tm, STRIP, strip_dx)

        @pl.when(first)
        def _():
            dw_ref[...] = jnp.zeros_like(dw_ref)
            db_ref[...] = jnp.zeros_like(db_ref)

        for k in range(FFN_CONV):
            dw_ref[k:k + 1, :] += jnp.sum(dw_acc[k * SUBLANES:(k + 1) * SUBLANES, :], axis=0, keepdims=True)
        db_ref[...] += jnp.sum(db_acc[...], axis=0, keepdims=True)

    next_rows = lambda i: jnp.minimum((i + 1) * (tm // HALO), t // HALO - 1)
    full = lambda rows: pl.BlockSpec((rows, width), lambda i: (0, 0))
    return _pcall(
        body, grid=(n_tok,),
        in_specs=[pl.BlockSpec((tm, width), lambda i: (i, 0)),
                  pl.BlockSpec((HALO, width), lambda i: (jnp.maximum(i * (tm // HALO) - 1, 0), 0)),
                  pl.BlockSpec((HALO, width), lambda i: (next_rows(i), 0)),
                  pl.BlockSpec((tm, D_FF), lambda i: (i, 0)), pl.BlockSpec((HALO, D_FF), lambda i: (next_rows(i), 0)),
                  full(FFN_CONV), full(1)],
        out_specs=[pl.BlockSpec((tm, width), lambda i: (i, 0)), full(FFN_CONV), full(1)],
        out_shape=[jax.ShapeDtypeStruct((t, width), BF16), jax.ShapeDtypeStruct((FFN_CONV, width), F32),
                   jax.ShapeDtypeStruct((1, width), F32)],
        scratch_shapes=[pltpu.VMEM((tm + 2 * HALO, width), F32), pltpu.VMEM((tm + HALO, width), F32),
                        pltpu.VMEM((FFN_CONV * SUBLANES, width), F32), pltpu.VMEM((SUBLANES, width), F32)],
        compiler_params=_params("arbitrary"), name="ffn_bwd")(up, up, up, dact, dact, conv_w, conv_b)


def _my_position():
    return lax.axis_index("x"), lax.axis_index("y"), lax.axis_index("c")


COPIES = N_DEV - 1


def _all_gather(arrays):
    n = len(arrays)

    def body(*refs):
        x_refs, out_refs = refs[:n], refs[n:2 * n]
        send_sems, recv_sems, local_sems = refs[2 * n:]
        x, y, cc = _my_position()
        me, sibling = (x, y, cc), (x, y, 1 - cc)
        chips = [(1 - x, y), (x, 1 - y), (1 - x, 1 - y)]

        def block(a, px, py, pc):
            return out_refs[a].at[4 * px + 2 * py + pc]

        def copy(a, k, blk, to, src=None):
            return pltpu.make_async_remote_copy(
                src_ref=block(a, *blk) if src is None else src, dst_ref=block(a, *blk),
                send_sem=send_sems.at[a * COPIES + k], recv_sem=recv_sems.at[a * COPIES + k],
                device_id=to, device_id_type=MESH_ID)

        mine = [pltpu.make_async_copy(x_refs[a], block(a, *me), local_sems.at[a]) for a in range(n)]
        for cp in mine:
            cp.start()
        first = []
        for a in range(n):
            first.append(copy(a, 0, me, sibling, src=x_refs[a]))
            first += [copy(a, 1 + j, me, (*chip, cc), src=x_refs[a]) for j, chip in enumerate(chips)]
        for cp in first:
            cp.start()
        passed = []
        for j, chip in enumerate(chips):
            for a in range(n):
                copy(a, 1 + j, (*chip, cc), me).wait_recv()
                passed.append(copy(a, 4 + j, (*chip, cc), sibling))
                passed[-1].start()
        for a in range(n):
            copy(a, 0, sibling, me).wait_recv()
        for j, chip in enumerate(chips):
            for a in range(n):
                copy(a, 4 + j, (*chip, 1 - cc), me).wait_recv()
        for cp in first + passed:
            cp.wait_send()
        for cp in mine:
            cp.wait()

    any_spec = pl.BlockSpec(memory_space=pl.ANY)
    return _pcall(
        body, out_shape=[jax.ShapeDtypeStruct((N_DEV,) + a.shape, a.dtype) for a in arrays],
        in_specs=[any_spec] * n, out_specs=[any_spec] * n,
        scratch_shapes=[pltpu.SemaphoreType.DMA((n * COPIES,)), pltpu.SemaphoreType.DMA((n * COPIES,)),
                        pltpu.SemaphoreType.DMA((n,))],
        name="all_gather")(*arrays)


def _all_to_all(sends):
    n = len(sends)

    def body(*refs):
        send_refs, recv_refs = refs[:n], refs[n:2 * n]
        send_sems, recv_sems, local_sems = refs[2 * n:]
        x, y, cc = _my_position()
        me = 4 * x + 2 * y + cc
        mine = [pltpu.make_async_copy(send_refs[a].at[me], recv_refs[a].at[me], local_sems.at[a]) for a in range(n)]
        for cp in mine:
            cp.start()
        copies = []
        for rel in range(1, N_DEV):
            px, py, pc = x ^ (rel >> 2), y ^ ((rel >> 1) & 1), cc ^ (rel & 1)
            for a in range(n):
                copies.append(pltpu.make_async_remote_copy(
                    src_ref=send_refs[a].at[4 * px + 2 * py + pc], dst_ref=recv_refs[a].at[me],
                    send_sem=send_sems.at[a * COPIES + rel - 1], recv_sem=recv_sems.at[a * COPIES + rel - 1],
                    device_id=(px, py, pc), device_id_type=MESH_ID))
        for cp in copies:
            cp.start()
        for cp in copies:
            cp.wait()
        for cp in mine:
            cp.wait()

    any_spec = pl.BlockSpec(memory_space=pl.ANY)
    return _pcall(
        body, out_shape=[jax.ShapeDtypeStruct(s.shape, s.dtype) for s in sends],
        in_specs=[any_spec] * n, out_specs=[any_spec] * n,
        scratch_shapes=[pltpu.SemaphoreType.DMA((n * COPIES,)), pltpu.SemaphoreType.DMA((n * COPIES,)),
                        pltpu.SemaphoreType.DMA((n,))],
        name="all_to_all")(*sends)


def _hbm(a):
    return pltpu.with_memory_space_constraint(a, pltpu.HBM)


def _split_copies(send_refs, land_refs, send_sems, recv_sems, local_sems, gather):
    x, y, cc = _my_position()
    me = 4 * x + 2 * y + cc
    local, remote = [], []
    for a, (send, land) in enumerate(zip(send_refs, land_refs)):
        local.append(pltpu.make_async_copy(send if gather else send.at[me], land.at[me], local_sems.at[a]))
    for a, (send, land) in enumerate(zip(send_refs, land_refs)):
        for rel in range(1, N_DEV):
            px, py, pc = x ^ (rel >> 2), y ^ ((rel >> 1) & 1), cc ^ (rel & 1)
            remote.append(pltpu.make_async_remote_copy(
                src_ref=send if gather else send.at[4 * px + 2 * py + pc], dst_ref=land.at[me],
                send_sem=send_sems.at[a * COPIES + rel - 1], recv_sem=recv_sems.at[a * COPIES + rel - 1],
                device_id=(px, py, pc), device_id_type=MESH_ID))
    return local, remote


SPLIT_EFFECT = pltpu.SideEffectType.DATAFLOW_SIDE_EFFECTING


def _exchange_start(sends, after, gather, name):
    n = len(sends)
    lands = [_hbm(lax.empty((N_DEV,) + s.shape if gather else s.shape, s.dtype)) for s in sends]

    def body(*refs):
        send_refs, land_refs = refs[:n], refs[n:2 * n]
        send_sems, recv_sems, local_sems = refs[2 * n + 1:2 * n + 4]
        token = refs[-1]
        local, remote = _split_copies(send_refs, land_refs, send_sems, recv_sems, local_sems, gather)
        for cp in local + remote:
            cp.start()
        token[...] = jnp.zeros_like(token)

    hbm, sem = pl.BlockSpec(memory_space=pltpu.HBM), pl.BlockSpec(memory_space=pltpu.SEMAPHORE)
    out = _pcall(
        body, name=name,
        out_shape=[pltpu.SemaphoreType.DMA((n * COPIES,)), pltpu.SemaphoreType.DMA((n * COPIES,)), pltpu.SemaphoreType.DMA((n,))]
        + [pltpu.HBM(s.shape, s.dtype) for s in sends] + [pltpu.HBM(z.shape, z.dtype) for z in lands]
        + [jax.ShapeDtypeStruct((SUBLANES, LANES), F32)],
        in_specs=[hbm] * (2 * n) + [pl.BlockSpec(memory_space=pl.ANY)],
        out_specs=[sem] * 3 + [hbm] * (2 * n) + [pl.BlockSpec(memory_space=pltpu.VMEM)],
        input_output_aliases={i: 3 + i for i in range(2 * n)},
        compiler_params=pltpu.CompilerParams(has_side_effects=SPLIT_EFFECT),
    )(*[_hbm(s) for s in sends], *lands, after)
    return dict(sems=out[:3], sends=out[3:3 + n], lands=out[3 + n:3 + 2 * n], gather=gather), out[-1]


def _exchange_wait(handle, after, name):
    sends, lands, gather = handle["sends"], handle["lands"], handle["gather"]
    n = len(sends)

    def body(*refs):
        send_refs, land_refs = refs[:n], refs[n:2 * n]
        send_sems, recv_sems, local_sems = refs[2 * n:2 * n + 3]
        local, remote = _split_copies(send_refs, land_refs, send_sems, recv_sems, local_sems, gather)
        for cp in remote:
            cp.wait_send()
            cp.wait_recv()
        for cp in local:
            cp.wait()

    hbm, sem = pl.BlockSpec(memory_space=pltpu.HBM), pl.BlockSpec(memory_space=pltpu.SEMAPHORE)
    out = _pcall(
        body, name=name,
        out_shape=[pltpu.HBM(s.shape, s.dtype) for s in sends] + [pltpu.HBM(z.shape, z.dtype) for z in lands],
        in_specs=[hbm] * (2 * n) + [sem] * 3 + [pl.BlockSpec(memory_space=pl.ANY)],
        out_specs=[hbm] * (2 * n), input_output_aliases={i: i for i in range(2 * n)},
        compiler_params=pltpu.CompilerParams(has_side_effects=SPLIT_EFFECT),
    )(*sends, *lands, *handle["sems"], after)
    return out[n:]


def _sum_and_adamw(recv, w, m, v, name):
    _, r, wp = recv.shape
    c = w.shape[-1]
    lead = w.ndim == 3
    tr = SLAB_ROW_TILE if r % SLAB_ROW_TILE == 0 else (SLAB_ROW_TILE // 4 if r % (SLAB_ROW_TILE // 4) == 0 else r)
    bc1 = 1.0 - ADAM_B1 ** ADAM_STEP
    bc2 = 1.0 - ADAM_B2 ** ADAM_STEP

    def body(recv_ref, w_ref, m_ref, v_ref, g_ref, d_ref, nm_ref, nv_ref):
        g = recv_ref[0, :, 0:c].astype(F32)
        for s in range(1, N_DEV):
            g = g + recv_ref[s, :, 0:c].astype(F32)
        m_new = ADAM_B1 * m_ref[...] + (1.0 - ADAM_B1) * g
        v_new = ADAM_B2 * v_ref[...] + (1.0 - ADAM_B2) * (g * g)
        m_hat = m_new / bc1
        v_hat = v_new / bc2
        g_ref[...] = g
        d_ref[...] = -ADAM_LR * (m_hat / (jnp.sqrt(v_hat) + ADAM_EPS) + ADAM_WD * w_ref[...])
        nm_ref[...] = m_new
        nv_ref[...] = v_new

    tile = pl.BlockSpec((None, tr, c), lambda i: (0, i, 0)) if lead else pl.BlockSpec((tr, c), lambda i: (i, 0))
    return _pcall(
        body, grid=(r // tr,),
        in_specs=[pl.BlockSpec((N_DEV, tr, wp), lambda i: (0, i, 0)), tile, tile, tile],
        out_specs=[tile] * 4, out_shape=[jax.ShapeDtypeStruct(w.shape, F32)] * 4,
        compiler_params=_params("parallel"), name=name)(recv, w, m, v)


SHARDED_TAPS = ("dn_conv_w", "ffn_conv_w")
REPLICATED = ("attn_norm_g", "dn_a_log", "dn_dt_bias", "dn_out_norm_g", "sg_norm_g", "sg_w", "sg_b", "ffn_norm_g",
              "ffn_conv_b", "final_norm_g")
SMALL = SHARDED_TAPS + REPLICATED
WEIGHT_ORDER = ("attn_norm_g", "w_in", "dn_conv_w", "dn_a_log", "dn_dt_bias", "dn_out_norm_g", "sg_norm_g", "sg_w", "sg_b",
                "w_out", "ffn_norm_g", "w_up", "ffn_conv_w", "ffn_conv_b", "w_down", "final_norm_g")
SLAB_COLS = 1024
SLAB_ROW_TILE = 128


def _pad_to(flat, multiple):
    pad = (-flat.shape[-1]) % multiple
    if pad == 0:
        return flat
    return jnp.pad(flat, [(0, 0)] * (flat.ndim - 1) + [(0, pad)])


def _pack_small(named):
    flat = jnp.concatenate([named[n].reshape(-1) for n in SMALL])
    return _pad_to(flat, SUBLANES * SLAB_COLS).reshape(-1, SLAB_COLS)


def _unpack_small(slab, like):
    flat = slab.reshape(-1)
    out, off = {}, 0
    for n in SMALL:
        size = like[n].size
        out[n] = flat[off:off + size].reshape(like[n].shape)
        off += size
    return out


def _split_columns(full, n_local):
    r = full.shape[0]
    return full.reshape(r, N_DEV, n_local).transpose(1, 0, 2).reshape(N_DEV, r * n_local)


def _join_columns(blocks, r, n_local):
    return blocks.reshape(N_DEV, r, n_local).transpose(1, 0, 2).reshape(r, N_DEV * n_local)


def _lanes4(a):
    return jnp.pad(a.reshape(1, N_HEADS), ((0, 0), (0, LANES - N_HEADS)))


def kernel(x, attn_norm_g, w_in, dn_conv_w, dn_a_log, dn_dt_bias, dn_out_norm_g, sg_norm_g, sg_w, sg_b, w_out, ffn_norm_g, w_up, ffn_conv_w, ffn_conv_b, w_down, final_norm_g, loss_target, m_attn_norm_g, m_w_in, m_dn_conv_w, m_dn_a_log, m_dn_dt_bias, m_dn_out_norm_g, m_sg_norm_g, m_sg_w, m_sg_b, m_w_out, m_ffn_norm_g, m_w_up, m_ffn_conv_w, m_ffn_conv_b, m_w_down, m_final_norm_g, v_attn_norm_g, v_w_in, v_dn_conv_w, v_dn_a_log, v_dn_dt_bias, v_dn_out_norm_g, v_sg_norm_g, v_sg_w, v_sg_b, v_w_out, v_ffn_norm_g, v_w_up, v_ffn_conv_w, v_ffn_conv_b, v_w_down, v_final_norm_g):
    weights = dict(attn_norm_g=attn_norm_g, w_in=w_in, dn_conv_w=dn_conv_w, dn_a_log=dn_a_log, dn_dt_bias=dn_dt_bias,
                   dn_out_norm_g=dn_out_norm_g, sg_norm_g=sg_norm_g, sg_w=sg_w, sg_b=sg_b, w_out=w_out, ffn_norm_g=ffn_norm_g,
                   w_up=w_up, ffn_conv_w=ffn_conv_w, ffn_conv_b=ffn_conv_b, w_down=w_down, final_norm_g=final_norm_g)
    m_in = dict(attn_norm_g=m_attn_norm_g, w_in=m_w_in, dn_conv_w=m_dn_conv_w, dn_a_log=m_dn_a_log, dn_dt_bias=m_dn_dt_bias,
                dn_out_norm_g=m_dn_out_norm_g, sg_norm_g=m_sg_norm_g, sg_w=m_sg_w, sg_b=m_sg_b, w_out=m_w_out,
                ffn_norm_g=m_ffn_norm_g, w_up=m_w_up, ffn_conv_w=m_ffn_conv_w, ffn_conv_b=m_ffn_conv_b, w_down=m_w_down,
                final_norm_g=m_final_norm_g)
    v_in = dict(attn_norm_g=v_attn_norm_g, w_in=v_w_in, dn_conv_w=v_dn_conv_w, dn_a_log=v_dn_a_log, dn_dt_bias=v_dn_dt_bias,
                dn_out_norm_g=v_dn_out_norm_g, sg_norm_g=v_sg_norm_g, sg_w=v_sg_w, sg_b=v_sg_b, w_out=v_w_out,
                ffn_norm_g=v_ffn_norm_g, w_up=v_w_up, ffn_conv_w=v_ffn_conv_w, ffn_conv_b=v_ffn_conv_b, w_down=v_w_down,
                final_norm_g=v_final_norm_g)

    n_in, n_up = w_in.shape[2], w_up.shape[2]
    r_out, r_down = w_out.shape[1], w_down.shape[1]
    n_dnc, n_ffc = dn_conv_w.shape[2], ffn_conv_w.shape[2]
    transposed = lambda a: jnp.transpose(a, (0, 2, 1))
    taps = _pad_to(jnp.concatenate([dn_conv_w.reshape(-1), ffn_conv_w.reshape(-1)]), SUBLANES * LANES).reshape(-1, LANES)
    g_in, g_taps = _all_gather([transposed(w_in)[0].astype(BF16), taps])
    gather_out, token = _exchange_start([w_out[0].astype(BF16)], g_taps, True, "gather_w_out")
    gather_up, token = _exchange_start([transposed(w_up)[0].astype(BF16)], token, True, "gather_w_up")
    gather_down, token = _exchange_start([w_down[0].astype(BF16)], token, True, "gather_w_down")
    w_in_t = jnp.pad(g_in.reshape(N_DEV * n_in, D_MODEL), ((0, PROJ_PAD - N_DEV * n_in), (0, 0)))
    taps_all = g_taps.reshape(N_DEV, -1)
    dn_conv_full = _join_columns(taps_all[:, :CONV_K * n_dnc], CONV_K, n_dnc)
    ffn_conv_full = _join_columns(taps_all[:, CONV_K * n_dnc:CONV_K * n_dnc + FFN_CONV * n_ffc], FFN_CONV, n_ffc)
    late = dict(
        w_out=lambda after: _exchange_wait(gather_out, after, "gather_w_out_wait")[0].reshape(N_DEV * r_out, D_MODEL),
        w_up_t=lambda after: _exchange_wait(gather_up, after, "gather_w_up_wait")[0].reshape(N_DEV * n_up, D_MODEL),
        w_down=lambda after: _exchange_wait(gather_down, after, "gather_w_down_wait")[0].reshape(N_DEV * r_down, D_MODEL))

    def send_early(blocks, after, name):
        return _exchange_start(blocks, after, False, name)

    def send_small(g, loss_lanes, after):
        small = jnp.concatenate([g[n].reshape(-1) for n in REPLICATED] + [loss_lanes[0, 0:1]])
        slab = jnp.concatenate([_split_columns(g["dn_conv_w"], n_dnc), _split_columns(g["ffn_conv_w"], n_ffc),
                                jnp.broadcast_to(small[None, :], (N_DEV, small.shape[0]))], axis=1)
        return send_early([_pad_to(slab, SUBLANES * SLAB_COLS).reshape(N_DEV, -1, SLAB_COLS)], after, "send_small")

    upd = {}

    def update_early(sent_down, sent_up_out, sent_small, after):
        r_dn, = _exchange_wait(sent_down, after, "send_dw_down_wait")
        r_up, r_o = _exchange_wait(sent_up_out, after, "send_dw_up_out_wait")
        r_small, = _exchange_wait(sent_small, after, "send_small_wait")
        upd["w_down"] = _sum_and_adamw(r_dn, w_down, m_w_down, v_w_down, "adamw_w_down")
        upd["w_up"] = [transposed(o) for o in _sum_and_adamw(r_up, transposed(w_up), transposed(m_w_up), transposed(v_w_up),
                                                             "adamw_w_up")]
        upd["w_out"] = _sum_and_adamw(r_o, w_out, m_w_out, v_w_out, "adamw_w_out")
        upd["small"] = _sum_and_adamw(r_small, _pack_small(weights), _pack_small(m_in), _pack_small(v_in), "adamw_small")

    grad_x, d_g1, sent_in = _local_step(
        x[0], loss_target[0], w_in_t, late, send_early, send_small, update_early, dn_conv_full, ffn_conv_full,
        attn_norm_g + token[0:1, 0:1], dn_a_log, dn_dt_bias, dn_out_norm_g, sg_norm_g, sg_w, sg_b, ffn_norm_g, ffn_conv_b,
        final_norm_g, n_in)

    norm_rows = D_MODEL // LANES
    r_g1, = _all_to_all([jnp.broadcast_to(d_g1.reshape(1, norm_rows, LANES), (N_DEV, norm_rows, LANES))])
    r_in, = _exchange_wait(sent_in, r_g1, "send_dw_in_wait")
    upd["w_in"] = [transposed(o) for o in _sum_and_adamw(r_in, transposed(w_in), transposed(m_w_in), transposed(v_w_in),
                                                         "adamw_w_in")]
    small_upd = upd.pop("small")
    as_rows = lambda a: a.reshape(norm_rows, LANES)
    norm_upd = _sum_and_adamw(r_g1, as_rows(attn_norm_g), as_rows(m_attn_norm_g), as_rows(v_attn_norm_g), "adamw_attn_norm")
    results = []
    for i in range(4):
        named = _unpack_small(small_upd[i], weights)
        named.update({n: upd[n][i] for n in upd})
        named["attn_norm_g"] = norm_upd[i].reshape(attn_norm_g.shape)
        results.append(named)

    loss = small_upd[0].reshape(-1)[sum(weights[n].size for n in SMALL)]
    return (loss, grad_x[None], *[r[n] for r in results for n in WEIGHT_ORDER])


def _local_step(x2d, tgt, w_in_t, late, send_early, send_small, update_early, dn_conv_full, ffn_conv_full, attn_norm_g,
                dn_a_log, dn_dt_bias, dn_out_norm_g, sg_norm_g, sg_w, sg_b, ffn_norm_g, ffn_conv_b, final_norm_g, n_in):
    t = x2d.shape[0]
    g1, g2, gf = attn_norm_g, ffn_norm_g, final_norm_g.reshape(1, D_MODEL)
    a_log4, dt_bias4 = _lanes4(dn_a_log), _lanes4(dn_dt_bias)
    sg_w3 = sg_w[0]
    sg_b_t = sg_b[0].T
    conv_b = ffn_conv_b

    p, h1, rstd1 = _rmsnorm_matmul(x2d, g1, w_in_t, "norm_in_proj", 512)
    q, k, v, beta4, g4 = _dn_prep(p, dn_conv_full, a_log4, dt_bias4)
    mix_half, s_all, a_inv_all = _dn_forward(q, k, v, beta4, g4, p, dn_out_norm_g)
    mix = _sg_forward(p, sg_norm_g, sg_w3, sg_b_t, mix_half)
    w_out_full = late["w_out"](mix)
    x2 = _matmul(mix, w_out_full, "nn", "out_proj", (1024, 1024, 1024), add=x2d)
    w_up_t = late["w_up_t"](x2)
    up, h2, rstd2 = _rmsnorm_matmul(x2, g2, w_up_t, "norm_up_proj", 256)
    act = _ffn_act(up, ffn_conv_full, conv_b)
    w_down_full = late["w_down"](act)
    fn, outs = _final_loss_rows(t, D_MODEL)
    loss_lanes, dx3, dx3b, d_gf = _matmul_rows(act, w_down_full, "nn", "down_proj_loss", 512,
                                               [(x2, "rows"), (tgt, "rows"), (gf, "whole")], outs, fn)

    dact = _matmul(dx3b, w_down_full, "nt", "down_proj_dx", (512, D_FF, D_MODEL))
    d_w_down = _matmul(act, dx3b, "tn", "down_proj_dw", (256, 1024, t), out_dtype=BF16)
    sent_down, token = send_early([d_w_down.reshape(N_DEV, D_FF // N_DEV, D_MODEL)], d_w_down, "send_dw_down")
    dup, d_ffn_conv, d_ffn_conv_b = _ffn_bwd(up, ffn_conv_full, conv_b + token[0:1, 0:1], dact)
    fn, outs = _rmsnorm_bwd_rows(t, D_MODEL)
    dx2, dx2b, d_g2 = _matmul_rows(dup, w_up_t, "nn", "up_proj_dx_norm", 256,
                                   [(x2, "rows"), (rstd2, "rows"), (g2, "whole"), (dx3, "rows")], outs, fn)
    d_w_up_t = _matmul(dup, h2, "tn", "up_proj_dw", (512, 1024, t), out_dtype=BF16)
    dmix = _matmul(dx2b, w_out_full, "nt", "out_proj_dx", (1024, 1024, 1024))
    d_w_out = _matmul(mix, dx2b, "tn", "out_proj_dw", (512, 1024, t), out_dtype=BF16)
    sent_up_out, token = send_early(
        [d_w_up_t.reshape(N_DEV, 2 * D_FF // N_DEV, D_MODEL), d_w_out.reshape(N_DEV, D_MODEL // N_DEV, D_MODEL)],
        d_w_out, "send_dw_up_out")
    dp, d_sg_norm, d_sg_w, d_sg_b_t = _sg_backward(p, sg_norm_g + token[0:1, 0:1], sg_w3, sg_b_t, dmix)
    dq, dk, dv, dbeta4, dg4, dp, d_dn_norm = _dn_backward(q, k, v, beta4, g4, p, dn_out_norm_g, s_all, a_inv_all, dmix, dp)
    dc_dn, d_dn_conv, dp, d_a_log4, d_dt_bias4 = _dn_prep_bwd(p, dn_conv_full, a_log4, dt_bias4, dq, dk, dv, dbeta4, dg4, dp)
    small_grads = dict(
        attn_norm_g=jnp.zeros_like(attn_norm_g), dn_conv_w=d_dn_conv, dn_a_log=d_a_log4[:, :N_HEADS],
        dn_dt_bias=d_dt_bias4[:, :N_HEADS], dn_out_norm_g=d_dn_norm, sg_norm_g=d_sg_norm, sg_w=d_sg_w,
        sg_b=d_sg_b_t[:, :SG_GROUPS].T, ffn_norm_g=d_g2, ffn_conv_w=d_ffn_conv, ffn_conv_b=d_ffn_conv_b, final_norm_g=d_gf)
    sent_small, token = send_small(small_grads, loss_lanes, d_dn_conv)
    dp = _conv_bwd_input(dc_dn, dn_conv_full + token[0:1, 0:1], "dn_conv_dx", out_cols=PROJ_PAD, into=dp)
    d_w_in_t = _matmul(dp, h1, "tn", "in_proj_dw", (PROJ_PAD // 5, 1024, t), out_dtype=BF16)
    sent_in, token = send_early([d_w_in_t[:N_DEV * n_in].reshape(N_DEV, n_in, D_MODEL)], d_w_in_t, "send_dw_in")
    update_early(sent_down, sent_up_out, sent_small, token)
    fn, outs = _rmsnorm_bwd_rows(t, D_MODEL)
    grad_x, _, d_g1 = _matmul_rows(dp, w_in_t, "nn", "in_proj_dx_norm", 512,
                                   [(x2d, "rows"), (rstd1, "rows"), (g1 + token[0:1, 0:1], "whole"), (dx2, "rows")], outs, fn)

    return grad_x, d_g1, sent_in
```

```python
import math

import jax
import jax.numpy as jnp
from jax import lax
from jax.experimental import pallas as pl
from jax.experimental.pallas import tpu as pltpu

F32 = jnp.float32
BF16 = jnp.bfloat16
HI = lax.Precision.HIGHEST

D_MODEL = 1024
DN_WIDTH = 512
HEAD_DIM = 128
N_HEADS = 4
SG_WIDTH = 512
SG_GROUPS = 4
SG_DIM = 128
SG_BLOCK = 128
D_FF = 2816
CHUNK = 64
CONV_K = 4
FFN_CONV = 3
EPS = 1e-6
PROJ_MAIN = 3072
PROJ_PAD = 3200
GELU_C = math.sqrt(2.0 / math.pi)
N_DEV = 8
LANES = 128
SUBLANES = 8
HALO = SUBLANES
VMEM_LIMIT = 48 * 1024 * 1024

ADAM_LR = 0.001
ADAM_B1 = 0.9
ADAM_B2 = 0.999
ADAM_EPS = 1e-08
ADAM_WD = 0.01
ADAM_STEP = 10

MESH_ID = pl.DeviceIdType.MESH


def _pcall(body, **kw):
    return pl.pallas_call(body, **kw)


def _params(*sem):
    return pltpu.CompilerParams(dimension_semantics=sem, vmem_limit_bytes=VMEM_LIMIT)


def _pick(n, cap):
    best = None
    for t in range(LANES, cap + 1, LANES):
        if n % t == 0:
            best = t
    return best if best else n


FAST, EXACT = "bf16 operands, one pass", "f32 operands, six bf16 passes"


def dot_f32(a, b, dims, tier):
    if tier == FAST:
        return lax.dot_general(a.astype(BF16), b.astype(BF16), dims, preferred_element_type=F32)
    return lax.dot_general(a, b, dims, precision=HI, preferred_element_type=F32)


def dot_nn(a, b, tier=EXACT):
    return dot_f32(a, b, (((1,), (0,)), ((), ())), tier)


def dot_nt(a, b, tier=EXACT):
    return dot_f32(a, b, (((1,), (1,)), ((), ())), tier)


def dot_tn(a, b, tier=EXACT):
    return dot_f32(a, b, (((0,), (0,)), ((), ())), tier)


def sigmoid(x):
    return 0.5 * jnp.tanh(0.5 * x) + 0.5


def silu(x):
    return x * sigmoid(x)


def silu_grad(x):
    s = sigmoid(x)
    return s * (1.0 + x * (1.0 - s))


def gelu(x):
    return 0.5 * x * (1.0 + jnp.tanh(GELU_C * (x + 0.044715 * x * x * x)))


def gelu_grad(x):
    t = jnp.tanh(GELU_C * (x + 0.044715 * x * x * x))
    return 0.5 * (1.0 + t) + 0.5 * x * (1.0 - t * t) * GELU_C * (1.0 + 3.0 * 0.044715 * x * x)


def softplus(z):
    return jnp.maximum(z, 0.0) + jnp.log(1.0 + jnp.exp(-jnp.abs(z)))


def rms_fwd(x, g):
    r = lax.rsqrt(jnp.mean(x * x, axis=-1, keepdims=True) + EPS)
    return x * r * g, r


def rms_bwd(x, r, g, dy):
    dyg = dy * g
    xr = x * r
    dx = r * (dyg - xr * jnp.mean(dyg * xr, axis=-1, keepdims=True))
    return dx, dy * xr


def l2_fwd(x):
    r = lax.rsqrt(jnp.sum(x * x, axis=-1, keepdims=True) + EPS)
    return x * r, r


def l2_bwd(x, r, dy):
    xr = x * r
    return r * (dy - xr * jnp.sum(dy * xr, axis=-1, keepdims=True))


def _tri_masks(n):
    row = lax.broadcasted_iota(jnp.int32, (n, n), 0)
    col = lax.broadcasted_iota(jnp.int32, (n, n), 1)
    return row >= col, row > col


def chunk_cumsum(g4):
    incl, _ = _tri_masks(g4.shape[0])
    return dot_nn(incl.astype(F32), g4)


STACK = N_HEADS * CHUNK
DN_FWD_CHUNKS = 8
DN_CHUNKS = 4


def _head_rows(h):
    return slice(h * CHUNK, (h + 1) * CHUNK)


def _stack_heads(x):
    return jnp.concatenate([x[:, h * HEAD_DIM:(h + 1) * HEAD_DIM] for h in range(N_HEADS)], axis=0)


def _stack_lanes(x4):
    return jnp.concatenate([x4[:, h:h + 1] for h in range(N_HEADS)], axis=0)


def _per_head(fn):
    return jnp.concatenate([fn(h) for h in range(N_HEADS)], axis=0)


def _unit_lower_inverses(l_strict, order):
    c = l_strict[0].shape[0]
    row = lax.broadcasted_iota(jnp.int32, (c, c), 0)
    col = lax.broadcasted_iota(jnp.int32, (c, c), 1)
    eye = (row == col).astype(F32)
    p = [-l for l in l_strict]
    a = [eye + n for n in p]
    for _ in range(int(math.log2(order)) - 1):
        p = [dot_nn(x, x, FAST) for x in p]
        a = [x + dot_nn(x, y, FAST) for x, y in zip(a, p)]
    return a


def dn_chunks_local(chunks, inverses=None):
    row = lax.broadcasted_iota(jnp.int32, (STACK, STACK), 0)
    col = lax.broadcasted_iota(jnp.int32, (STACK, STACK), 1)
    same = (row // CHUNK) == (col // CHUNK)
    incl = jnp.logical_and(same, row >= col)
    strict = jnp.logical_and(same, row > col)
    locs = []
    for q, k, v, beta, gc4 in chunks:
        gc_col = _stack_lanes(gc4)
        gc_row = jnp.sum(jnp.where(row == col, gc_col, 0.0), axis=0, keepdims=True)
        decay = jnp.where(incl, jnp.exp(jnp.minimum(gc_col - gc_row, 0.0)), 0.0)
        gamma = jnp.exp(gc_col)
        gc_last = jnp.concatenate([jnp.broadcast_to(gc4[CHUNK - 1:CHUNK, h:h + 1], (CHUNK, 1)) for h in range(N_HEADS)], axis=0)
        tau = jnp.exp(gc_last - gc_col)
        kb = k * beta
        locs.append(dict(decay=decay, gamma=gamma, tau=tau, cd=jnp.exp(gc_last), kb=kb, qd=q * gamma, kt=k * tau,
                         incl=incl, strict=strict))
    for loc, (q, k, v, beta, gc4) in zip(locs, chunks):
        loc["l_mat"] = jnp.where(strict, dot_nt(loc["kb"], k, FAST) * loc["decay"], 0.0)
    if inverses is None:
        inverses = _unit_lower_inverses([loc["l_mat"] for loc in locs], CHUNK)
    for loc, a_inv in zip(locs, inverses):
        loc["a_inv"] = a_inv
    for loc, (q, k, v, beta, gc4) in zip(locs, chunks):
        sol = dot_nn(loc["a_inv"], jnp.concatenate([v * beta, loc["kb"] * loc["gamma"]], axis=1), FAST)
        loc.update(sol=sol, value=sol[:, :HEAD_DIM], kcd=sol[:, HEAD_DIM:])
        loc["attn"] = jnp.where(incl, dot_nt(q, k, FAST) * loc["decay"], 0.0)
    return locs


def dn_chunk_state(loc, s):
    kcd, qd, kt, cd = loc["kcd"], loc["qd"], loc["kt"], loc["cd"]
    v_new = loc["value"] - _per_head(lambda h: dot_nn(kcd[_head_rows(h)], s[h], FAST))
    o = _per_head(lambda h: dot_nn(qd[_head_rows(h)], s[h], FAST)) + dot_nn(loc["attn"], v_new, FAST)
    s_new = [s[h] * cd[h * CHUNK:h * CHUNK + 1, :] + dot_tn(kt[_head_rows(h)], v_new[_head_rows(h)], FAST)
             for h in range(N_HEADS)]
    loc["v_new"] = v_new
    return o, s_new


def dn_chunks_bwd(items, ds_last):
    hr = _head_rows
    n = len(items)
    pre = []
    for q, k, v, beta, loc, s, do in items:
        pre.append(dict(
            dv_part=dot_tn(loc["attn"], do, FAST),
            dattn=jnp.where(loc["incl"], dot_nt(do, loc["v_new"], FAST), 0.0),
            dqd=_per_head(lambda h: dot_nt(do[hr(h)], s[h], FAST)),
            ds_part=[dot_tn(loc["qd"][hr(h)], do[hr(h)], FAST) for h in range(N_HEADS)]))
    ds_new_of, dv_new_of = [None] * n, [None] * n
    ds = ds_last
    for c in reversed(range(n)):
        loc = items[c][4]
        ds_new_of[c] = ds
        dv_new = pre[c]["dv_part"] + _per_head(lambda h: dot_nn(loc["kt"][hr(h)], ds[h], FAST))
        dv_new_of[c] = dv_new
        ds = [pre[c]["ds_part"][h] + ds[h] * loc["cd"][h * CHUNK:h * CHUNK + 1, :]
              - dot_tn(loc["kcd"][hr(h)], dv_new[hr(h)], FAST) for h in range(N_HEADS)]
    is_last = (lax.broadcasted_iota(jnp.int32, (STACK, 1), 0) % CHUNK) == CHUNK - 1
    out = []
    for c, (q, k, v, beta, loc, s, do) in enumerate(items):
        decay, gamma, tau, cd, kb = loc["decay"], loc["gamma"], loc["tau"], loc["cd"], loc["kb"]
        dv_new, ds_new, dattn, dqd = dv_new_of[c], ds_new_of[c], pre[c]["dattn"], pre[c]["dqd"]
        dkt = _per_head(lambda h: dot_nt(loc["v_new"][hr(h)], ds_new[h], FAST))
        dkcd = -_per_head(lambda h: dot_nt(dv_new[hr(h)], s[h], FAST))
        drhs = dot_tn(loc["a_inv"], jnp.concatenate([dv_new, dkcd], axis=1), FAST)
        dvb, dkbg = drhs[:, :HEAD_DIM], drhs[:, HEAD_DIM:]
        dl = jnp.where(loc["strict"], -dot_nt(drhs, loc["sol"], FAST), 0.0)
        dkk = dl * decay
        dqk = dattn * decay
        e = dl * loc["l_mat"] + dattn * loc["attn"]
        dgc = jnp.sum(e, axis=1, keepdims=True) - jnp.sum(e, axis=0, keepdims=True).T
        dkb = dot_nn(dkk, k, FAST) + dkbg * gamma
        dk = dot_tn(dkk, kb, FAST) + dot_tn(dqk, q, FAST) + dkt * tau
        dq = dot_nn(dqk, k, FAST) + dqd * gamma
        dgamma = jnp.sum(dkbg * kb, axis=1, keepdims=True) + jnp.sum(dqd * q, axis=1, keepdims=True)
        dtau_tau = jnp.sum(dkt * k, axis=1, keepdims=True) * tau
        dgc = dgc + dgamma * gamma - dtau_tau

        def last_term(h):
            dcd = jnp.sum(jnp.sum(ds_new[h] * s[h], axis=1, keepdims=True), axis=0, keepdims=True)
            total = jnp.sum(dtau_tau[hr(h)], axis=0, keepdims=True) + dcd * cd[h * CHUNK:h * CHUNK + 1, :]
            return jnp.broadcast_to(total, (CHUNK, 1))

        dgc = dgc + jnp.where(is_last, _per_head(last_term), 0.0)
        dk = dk + dkb * beta
        dbeta = jnp.sum(dkb * k, axis=1, keepdims=True) + jnp.sum(dvb * v, axis=1, keepdims=True)
        out.append((dq, dk, dvb * beta, dbeta, dgc))
    return out, ds


def _token_tile(t):
    return _pick(t, 256)


STRIP = 32


def _for_strips(n_rows, rows, fn):
    def step(r, carry):
        fn(pl.multiple_of(r * rows, rows))
        return carry

    lax.fori_loop(0, n_rows // rows, step, 0)


def _fold_rows(x):
    out = x[0:SUBLANES, :]
    for i in range(1, x.shape[0] // SUBLANES):
        out = out + x[i * SUBLANES:(i + 1) * SUBLANES, :]
    return out


def _matmul(a, b, mode, name, tiles, add=None, out_dtype=F32):
    if mode == "nn":
        (m, k), n = a.shape, b.shape[1]
    elif mode == "nt":
        (m, k), n = a.shape, b.shape[0]
    else:
        (k, m), n = a.shape, b.shape[1]
    tm, tn, tk = min(tiles[0], m), min(tiles[1], n), min(tiles[2], k)
    assert m % tm == 0 and n % tn == 0 and k % tk == 0, (name, m, n, k, tiles)
    nk = k // tk
    dims = {"nn": (((1,), (0,)), ((), ())), "nt": (((1,), (1,)), ((), ())), "tn": (((0,), (0,)), ((), ()))}[mode]

    def finish(res, add_ref, o_ref):
        if add_ref is not None:
            res = res + add_ref[...]
        o_ref[...] = res.astype(o_ref.dtype)

    def body(*refs):
        a_ref, b_ref = refs[0], refs[1]
        add_ref = refs[2] if add is not None else None
        o_ref = refs[3] if add is not None else refs[2]
        part = lax.dot_general(a_ref[...], b_ref[...], dims, preferred_element_type=F32)
        if nk == 1:
            finish(part, add_ref, o_ref)
            return
        acc_ref = refs[-1]
        kk = pl.program_id(2)

        @pl.when(kk == 0)
        def _():
            acc_ref[...] = part

        @pl.when(kk > 0)
        def _():
            acc_ref[...] += part

        @pl.when(kk == nk - 1)
        def _():
            finish(acc_ref[...], add_ref, o_ref)

    a_spec = pl.BlockSpec((tk, tm), lambda j, i, kk: (kk, i)) if mode == "tn" else pl.BlockSpec((tm, tk), lambda j, i, kk: (i, kk))
    b_spec = pl.BlockSpec((tn, tk), lambda j, i, kk: (j, kk)) if mode == "nt" else pl.BlockSpec((tk, tn), lambda j, i, kk: (kk, j))
    o_spec = pl.BlockSpec((tm, tn), lambda j, i, kk: (i, j))
    in_specs = [a_spec, b_spec] + ([o_spec] if add is not None else [])
    args = (a, b) + ((add,) if add is not None else ())
    return _pcall(
        body, grid=(n // tn, m // tm, nk), in_specs=in_specs, out_specs=o_spec,
        out_shape=jax.ShapeDtypeStruct((m, n), out_dtype),
        scratch_shapes=[pltpu.VMEM((tm, tn), F32)] if nk > 1 else [],
        compiler_params=_params("parallel", "parallel", "arbitrary"), name=name)(*args)


def _matmul_rows(a, b, mode, name, tm, extra, outs, fn):
    m, k = a.shape
    n = b.shape[1] if mode == "nn" else b.shape[0]
    tm = min(tm, m)
    dims = (((1,), (0,)), ((), ())) if mode == "nn" else (((1,), (1,)), ((), ()))

    def spec(shape, kind):
        if kind == "rows":
            return pl.BlockSpec((tm, shape[1]), lambda i: (i, 0))
        return pl.BlockSpec(shape, lambda i: (0,) * len(shape))

    def body(a_ref, b_ref, *refs):
        rows = lax.dot_general(a_ref[...], b_ref[...], dims, preferred_element_type=F32)
        fn(rows, pl.program_id(0) == 0, *refs)

    return _pcall(
        body, grid=(m // tm,),
        in_specs=[pl.BlockSpec((tm, k), lambda i: (i, 0)), pl.BlockSpec(b.shape, lambda i: (0, 0))]
        + [spec(x.shape, kind) for x, kind in extra],
        out_specs=[spec(shape, kind) for shape, _, kind in outs],
        out_shape=[jax.ShapeDtypeStruct(shape, dtype) for shape, dtype, _ in outs],
        compiler_params=_params("arbitrary"), name=name)(a, b, *[x for x, _ in extra])


def _rmsnorm_matmul(x, g, b_t, name, tm):
    t, d = x.shape
    n = b_t.shape[0]
    tm = min(tm, t)

    def body(x_ref, g_ref, b_ref, o_ref, h_ref, r_ref):
        y, r = rms_fwd(x_ref[...], g_ref[...])
        h = y.astype(BF16)
        h_ref[...] = h
        r_ref[...] = r
        o_ref[...] = lax.dot_general(h, b_ref[...], (((1,), (1,)), ((), ())), preferred_element_type=F32)

    rows = lambda w: pl.BlockSpec((tm, w), lambda i: (i, 0))
    return _pcall(
        body, grid=(t // tm,),
        in_specs=[rows(d), pl.BlockSpec((1, d), lambda i: (0, 0)), pl.BlockSpec((n, d), lambda i: (0, 0))],
        out_specs=[rows(n), rows(d), rows(1)],
        out_shape=[jax.ShapeDtypeStruct((t, n), F32), jax.ShapeDtypeStruct((t, d), BF16), jax.ShapeDtypeStruct((t, 1), F32)],
        compiler_params=_params("parallel"), name=name)(x, g, b_t)


def _rmsnorm_bwd_rows(t, d):
    def fn(dh, first, x_ref, r_ref, g_ref, dres_ref, dx_ref, dxb_ref, dg_ref):
        dx, dg_rows = rms_bwd(x_ref[...], r_ref[...], g_ref[...], dh)
        dx = dx + dres_ref[...]
        dx_ref[...] = dx
        dxb_ref[...] = dx.astype(BF16)

        @pl.when(first)
        def _():
            dg_ref[...] = jnp.zeros_like(dg_ref)

        dg_ref[...] += jnp.sum(dg_rows, axis=0, keepdims=True)

    return fn, [((t, d), F32, "rows"), ((t, d), BF16, "rows"), ((1, d), F32, "whole")]


def _final_loss_rows(t, d):
    def fn(rows, first, res_ref, t_ref, g_ref, loss_ref, dx_ref, dxb_ref, dg_ref):
        @pl.when(first)
        def _():
            loss_ref[...] = jnp.zeros_like(loss_ref)
            dg_ref[...] = jnp.zeros_like(dg_ref)

        x = rows + res_ref[...]
        y, r = rms_fwd(x, g_ref[...])
        err = y - t_ref[...]
        loss_ref[...] += 0.5 * jnp.sum(jnp.mean(err * err, axis=-1, keepdims=True), axis=0, keepdims=True)
        dx, dg_rows = rms_bwd(x, r, g_ref[...], err * (1.0 / d))
        dx_ref[...] = dx
        dxb_ref[...] = dx.astype(BF16)
        dg_ref[...] += jnp.sum(dg_rows, axis=0, keepdims=True)

    return fn, [((1, LANES), F32, "whole"), ((t, d), F32, "rows"), ((t, d), BF16, "rows"), ((1, d), F32, "whole")]


def _prev_halo_spec(tm, width, col_block):
    return pl.BlockSpec((HALO, width), lambda i: (jnp.maximum(i * (tm // HALO) - 1, 0), col_block))


def _fill_with_prev(xp_ref, tile, halo, first):
    xp_ref[0:HALO, :] = jnp.where(first, 0.0, halo)
    xp_ref[HALO:, :] = tile


def _delayed(xp_ref, row0, cols, taps):
    ext = xp_ref[pl.ds(row0, STRIP + HALO), cols]
    return [ext[HALO:, :]] + [pltpu.roll(ext, j, 0)[HALO:, :] for j in range(1, taps)]


def _causal_conv(delayed, w):
    taps = len(delayed)
    out = delayed[0] * w[taps - 1:taps, :]
    for j in range(1, taps):
        out = out + delayed[j] * w[taps - 1 - j:taps - j, :]
    return out


def _advanced_conv(buf_ref, row0, cols, w):
    taps = w.shape[0]
    ext = buf_ref[pl.ds(row0, STRIP + HALO), cols]
    out = ext[:STRIP, :] * w[taps - 1:taps, :]
    for j in range(1, taps):
        out = out + pltpu.roll(ext, STRIP + HALO - j, 0)[:STRIP, :] * w[taps - 1 - j:taps - j, :]
    return out


def _dn_prep(p, conv_w, a_log4, dt_bias4):
    t = p.shape[0]
    tm = _token_tile(t)
    w3 = 3 * DN_WIDTH

    def body(x_ref, halo_ref, pbd_ref, w_ref, alog_ref, dtb_ref, q_ref, k_ref, v_ref, beta_ref, g_ref, xp_ref):
        _fill_with_prev(xp_ref, x_ref[...], halo_ref[...], pl.program_id(0) == 0)

        def strip(row0):
            rows = pl.ds(row0, STRIP)
            for h in range(N_HEADS):
                sl = slice(h * HEAD_DIM, (h + 1) * HEAD_DIM)
                for part, out_ref in ((0, q_ref), (1, k_ref), (2, v_ref)):
                    cols = slice(part * DN_WIDTH + h * HEAD_DIM, part * DN_WIDTH + (h + 1) * HEAD_DIM)
                    y = silu(_causal_conv(_delayed(xp_ref, row0, cols, CONV_K), w_ref[:, cols]))
                    if part == 0:
                        y = l2_fwd(y)[0] * (HEAD_DIM ** -0.5)
                    elif part == 1:
                        y = l2_fwd(y)[0]
                    out_ref[rows, sl] = y
            head = lax.broadcasted_iota(jnp.int32, (STRIP, LANES), 1) < N_HEADS
            pbd = pbd_ref[rows, :]
            beta_ref[rows, :] = jnp.where(head, sigmoid(pbd), 0.0)
            a_raw = pltpu.roll(pbd, LANES - N_HEADS, 1)
            g_ref[rows, :] = jnp.where(head, -jnp.exp(alog_ref[...]) * softplus(a_raw + dtb_ref[...]), 0.0)

        _for_strips(tm, STRIP, strip)

    tok = lambda w, cb: pl.BlockSpec((tm, w), lambda i: (i, cb))
    full = lambda a: pl.BlockSpec(a.shape, lambda i: (0, 0))
    return _pcall(
        body, grid=(t // tm,),
        in_specs=[tok(w3, 0), _prev_halo_spec(tm, w3, 0), tok(LANES, PROJ_MAIN // LANES),
                  full(conv_w), full(a_log4), full(dt_bias4)],
        out_specs=[tok(DN_WIDTH, 0)] * 3 + [tok(LANES, 0)] * 2,
        out_shape=[jax.ShapeDtypeStruct((t, DN_WIDTH), F32)] * 3 + [jax.ShapeDtypeStruct((t, LANES), F32)] * 2,
        scratch_shapes=[pltpu.VMEM((HALO + tm, w3), F32)],
        compiler_params=_params("parallel"), name="dn_prep")(p, p, p, conv_w, a_log4, dt_bias4)


def _dn_prep_bwd(p, conv_w, a_log4, dt_bias4, dq, dk, dv, dbeta4, dg4, dp_buf):
    t = p.shape[0]
    tm = _token_tile(t)
    w3 = 3 * DN_WIDTH

    def body(x_ref, halo_ref, pbd_ref, w_ref, alog_ref, dtb_ref, dq_ref, dk_ref, dv_ref, dbeta_ref, dg_ref, _,
             dc_ref, dw_ref, dpbd_ref, dalog_ref, ddtb_ref, xp_ref, dw_acc, lane_acc):
        first = pl.program_id(0) == 0
        _fill_with_prev(xp_ref, x_ref[...], halo_ref[...], first)
        dw_acc[...] = jnp.zeros_like(dw_acc)
        lane_acc[...] = jnp.zeros_like(lane_acc)

        def strip(row0):
            rows = pl.ds(row0, STRIP)
            for h in range(N_HEADS):
                sl = slice(h * HEAD_DIM, (h + 1) * HEAD_DIM)
                for part, dy_ref in ((0, dq_ref), (1, dk_ref), (2, dv_ref)):
                    cols = slice(part * DN_WIDTH + h * HEAD_DIM, part * DN_WIDTH + (h + 1) * HEAD_DIM)
                    delayed = _delayed(xp_ref, row0, cols, CONV_K)
                    c = _causal_conv(delayed, w_ref[:, cols])
                    dy = dy_ref[rows, sl]
                    if part < 2:
                        y = silu(c)
                        _, r = l2_fwd(y)
                        dy = l2_bwd(y, r, dy * (HEAD_DIM ** -0.5) if part == 0 else dy)
                    dc = dy * silu_grad(c)
                    dc_ref[rows, cols] = dc
                    for j in range(CONV_K):
                        k = CONV_K - 1 - j
                        dw_acc[k * SUBLANES:(k + 1) * SUBLANES, cols] += _fold_rows(dc * delayed[j])
            head = lax.broadcasted_iota(jnp.int32, (STRIP, LANES), 1) < N_HEADS
            pbd = pbd_ref[rows, :]
            beta = sigmoid(pbd)
            dpb = jnp.where(head, dbeta_ref[rows, :] * beta * (1.0 - beta), 0.0)
            z = pltpu.roll(pbd, LANES - N_HEADS, 1) + dtb_ref[...]
            neg_rate = -jnp.exp(alog_ref[...])
            dg = dg_ref[rows, :]
            dpa = jnp.where(head, dg * neg_rate * sigmoid(z), 0.0)
            dpbd_ref[rows, :] = (dpb + pltpu.roll(dpa, N_HEADS, 1)).astype(BF16)
            g = jnp.where(head, neg_rate * softplus(z), 0.0)
            lane_acc[0:SUBLANES, :] += _fold_rows(dg * g)
            lane_acc[SUBLANES:, :] += _fold_rows(dpa)

        _for_strips(tm, STRIP, strip)

        @pl.when(first)
        def _():
            dw_ref[...] = jnp.zeros_like(dw_ref)
            dalog_ref[...] = jnp.zeros_like(dalog_ref)
            ddtb_ref[...] = jnp.zeros_like(ddtb_ref)

        for k in range(CONV_K):
            dw_ref[k:k + 1, :] += jnp.sum(dw_acc[k * SUBLANES:(k + 1) * SUBLANES, :], axis=0, keepdims=True)
        dalog_ref[...] += jnp.sum(lane_acc[0:SUBLANES, :], axis=0, keepdims=True)
        ddtb_ref[...] += jnp.sum(lane_acc[SUBLANES:, :], axis=0, keepdims=True)

    tok = lambda w, cb: pl.BlockSpec((tm, w), lambda i: (i, cb))
    full = lambda shape: pl.BlockSpec(shape, lambda i: (0, 0))
    return _pcall(
        body, grid=(t // tm,),
        in_specs=[tok(w3, 0), _prev_halo_spec(tm, w3, 0), tok(LANES, PROJ_MAIN // LANES),
                  full(conv_w.shape), full(a_log4.shape), full(dt_bias4.shape)] + [tok(DN_WIDTH, 0)] * 3 + [tok(LANES, 0)] * 2
        + [pl.BlockSpec(memory_space=pl.ANY)],
        out_specs=[tok(w3, 0), full((CONV_K, w3)), tok(LANES, PROJ_MAIN // LANES), full((1, LANES)), full((1, LANES))],
        out_shape=[jax.ShapeDtypeStruct((t, w3), F32), jax.ShapeDtypeStruct((CONV_K, w3), F32),
                   jax.ShapeDtypeStruct(dp_buf.shape, dp_buf.dtype),
                   jax.ShapeDtypeStruct((1, LANES), F32), jax.ShapeDtypeStruct((1, LANES), F32)],
        input_output_aliases={11: 2},
        scratch_shapes=[pltpu.VMEM((HALO + tm, w3), F32), pltpu.VMEM((CONV_K * SUBLANES, w3), F32),
                        pltpu.VMEM((2 * SUBLANES, LANES), F32)],
        compiler_params=_params("arbitrary"), name="dn_prep_bwd")(p, p, p, conv_w, a_log4, dt_bias4, dq, dk, dv, dbeta4, dg4, dp_buf)


def _conv_bwd_input(dc, w, name, out_cols=None, col_block=0, into=None):
    t, c = dc.shape
    taps = w.shape[0]
    tm = _token_tile(t)
    ct = _pick(c, 1536)
    n_tok = t // tm
    out_cols = c if out_cols is None else out_cols

    def body(dc_ref, next_ref, w_ref, *rest):
        dx_ref, buf_ref = rest[-2], rest[-1]
        buf_ref[0:tm, :] = dc_ref[...]
        buf_ref[tm:, :] = jnp.where(pl.program_id(0) == n_tok - 1, 0.0, next_ref[...])

        def strip(row0):
            for c0 in range(0, ct, LANES):
                cols = slice(c0, c0 + LANES)
                dx_ref[pl.ds(row0, STRIP), cols] = _advanced_conv(buf_ref, row0, cols, w_ref[:, cols]).astype(BF16)

        _for_strips(tm, STRIP, strip)

    in_specs = [pl.BlockSpec((tm, ct), lambda i, j: (i, j)),
                pl.BlockSpec((HALO, ct), lambda i, j: (jnp.minimum((i + 1) * (tm // HALO), t // HALO - 1), j)),
                pl.BlockSpec((taps, ct), lambda i, j: (0, j))]
    args = (dc, dc, w)
    aliases = {}
    if into is not None:
        in_specs.append(pl.BlockSpec(memory_space=pl.ANY))
        args += (into,)
        aliases = {3: 0}
    return _pcall(
        body, grid=(n_tok, c // ct), in_specs=in_specs,
        out_specs=pl.BlockSpec((tm, ct), lambda i, j: (i, j + col_block)),
        out_shape=jax.ShapeDtypeStruct((t, out_cols), BF16), input_output_aliases=aliases,
        scratch_shapes=[pltpu.VMEM((tm + HALO, ct), F32)],
        compiler_params=_params("parallel", "parallel"), name=name)(*args)


def _dn_forward(q, k, v, beta4, g4, p, norm_g):
    t = q.shape[0]
    n = t // CHUNK
    nc = DN_FWD_CHUNKS
    rows_per_step = nc * CHUNK

    def body(q_ref, k_ref, v_ref, b_ref, g_ref, gate_ref, ng_ref, mix_ref, s_all_ref, ainv_ref, s_ref):
        @pl.when(pl.program_id(0) == 0)
        def _():
            s_ref[...] = jnp.zeros_like(s_ref)

        chunks = []
        for c in range(nc):
            rows = slice(c * CHUNK, (c + 1) * CHUNK)
            chunks.append((_stack_heads(q_ref[rows, :]), _stack_heads(k_ref[rows, :]), _stack_heads(v_ref[rows, :]),
                           _stack_lanes(b_ref[rows, :]), chunk_cumsum(g_ref[rows, :])))
        locs = dn_chunks_local(chunks)
        s = [s_ref[h] for h in range(N_HEADS)]
        for c in range(nc):
            rows = slice(c * CHUNK, (c + 1) * CHUNK)
            ainv_ref[c] = locs[c]["a_inv"].astype(BF16)
            for h in range(N_HEADS):
                s_all_ref[c, h] = s[h]
            o, s = dn_chunk_state(locs[c], s)
            o_n, _ = rms_fwd(o, ng_ref[...])
            for h in range(N_HEADS):
                sl = slice(h * HEAD_DIM, (h + 1) * HEAD_DIM)
                mix_ref[rows, sl] = (o_n[_head_rows(h)] * silu(gate_ref[rows, sl])).astype(BF16)
        for h in range(N_HEADS):
            s_ref[h] = s[h]

    ch = lambda w, cb: pl.BlockSpec((rows_per_step, w), lambda i: (i, cb))
    return _pcall(
        body, grid=(n // nc,),
        in_specs=[ch(DN_WIDTH, 0)] * 3 + [ch(LANES, 0)] * 2 + [ch(DN_WIDTH, 3), pl.BlockSpec((1, HEAD_DIM), lambda i: (0, 0))],
        out_specs=[ch(DN_WIDTH, 0), pl.BlockSpec((nc, N_HEADS, HEAD_DIM, HEAD_DIM), lambda i: (i, 0, 0, 0)),
                   pl.BlockSpec((nc, STACK, STACK), lambda i: (i, 0, 0))],
        out_shape=[jax.ShapeDtypeStruct((t, DN_WIDTH + SG_WIDTH), BF16), jax.ShapeDtypeStruct((n, N_HEADS, HEAD_DIM, HEAD_DIM), F32),
                   jax.ShapeDtypeStruct((n, STACK, STACK), BF16)],
        scratch_shapes=[pltpu.VMEM((N_HEADS, HEAD_DIM, HEAD_DIM), F32)],
        compiler_params=_params("arbitrary"), name="dn_forward")(q, k, v, beta4, g4, p, norm_g)


def _dn_backward(q, k, v, beta4, g4, p, norm_g, s_all, a_inv_all, dmix, dp_buf):
    t = q.shape[0]
    n = t // CHUNK
    steps = n // DN_CHUNKS
    rows_per_step = DN_CHUNKS * CHUNK

    def body(q_ref, k_ref, v_ref, b_ref, g_ref, gate_ref, ng_ref, s_in_ref, ainv_ref, dmix_ref, _,
             dq_ref, dk_ref, dv_ref, db_ref, dg_ref, dgate_ref, dng_ref, ds_ref):
        @pl.when(pl.program_id(0) == 0)
        def _():
            ds_ref[...] = jnp.zeros_like(ds_ref)
            dng_ref[...] = jnp.zeros_like(dng_ref)

        chunks = []
        for c in range(DN_CHUNKS):
            rows = slice(c * CHUNK, (c + 1) * CHUNK)
            chunks.append((_stack_heads(q_ref[rows, :]), _stack_heads(k_ref[rows, :]), _stack_heads(v_ref[rows, :]),
                           _stack_lanes(b_ref[rows, :]), chunk_cumsum(g_ref[rows, :])))
        items = []
        for c, loc in enumerate(dn_chunks_local(chunks, [ainv_ref[c] for c in range(DN_CHUNKS)])):
            rows = slice(c * CHUNK, (c + 1) * CHUNK)
            s = [s_in_ref[c, h] for h in range(N_HEADS)]
            o, _ = dn_chunk_state(loc, s)
            o_n, r = rms_fwd(o, ng_ref[...])
            gate = _stack_heads(gate_ref[rows, :])
            dmx = _stack_heads(dmix_ref[rows, :])
            dgate = dmx * o_n * silu_grad(gate)
            do, dng_rows = rms_bwd(o, r, ng_ref[...], dmx * silu(gate))
            dng_ref[...] += jnp.sum(dng_rows, axis=0, keepdims=True)
            for h in range(N_HEADS):
                dgate_ref[rows, h * HEAD_DIM:(h + 1) * HEAD_DIM] = dgate[_head_rows(h)].astype(BF16)
            items.append((*chunks[c][:4], loc, s, do))
        grads, ds = dn_chunks_bwd(items, [ds_ref[h] for h in range(N_HEADS)])
        lane = lax.broadcasted_iota(jnp.int32, (CHUNK, LANES), 1)
        _, strict = _tri_masks(CHUNK)
        for c in range(DN_CHUNKS):
            rows = slice(c * CHUNK, (c + 1) * CHUNK)
            dq, dk, dv, dbeta, dgc = grads[c]
            db4 = jnp.zeros((CHUNK, LANES), F32)
            dgc4 = jnp.zeros((CHUNK, LANES), F32)
            for h in range(N_HEADS):
                sl = slice(h * HEAD_DIM, (h + 1) * HEAD_DIM)
                head_rows = _head_rows(h)
                dq_ref[rows, sl] = dq[head_rows]
                dk_ref[rows, sl] = dk[head_rows]
                dv_ref[rows, sl] = dv[head_rows]
                db4 = jnp.where(lane == h, dbeta[head_rows], db4)
                dgc4 = jnp.where(lane == h, dgc[head_rows], dgc4)
            db_ref[rows, :] = db4
            dg_ref[rows, :] = dot_nn(jnp.logical_not(strict).astype(F32), dgc4)
        for h in range(N_HEADS):
            ds_ref[h] = ds[h]

    rev = lambda w, cb: pl.BlockSpec((rows_per_step, w), lambda i: (steps - 1 - i, cb))
    return _pcall(
        body, grid=(steps,),
        in_specs=[rev(DN_WIDTH, 0)] * 3 + [rev(LANES, 0)] * 2 + [rev(DN_WIDTH, 3), pl.BlockSpec((1, HEAD_DIM), lambda i: (0, 0)),
                  pl.BlockSpec((DN_CHUNKS, N_HEADS, HEAD_DIM, HEAD_DIM), lambda i: (steps - 1 - i, 0, 0, 0)),
                  pl.BlockSpec((DN_CHUNKS, STACK, STACK), lambda i: (steps - 1 - i, 0, 0)), rev(DN_WIDTH, 0),
                  pl.BlockSpec(memory_space=pl.ANY)],
        out_specs=[rev(DN_WIDTH, 0)] * 3 + [rev(LANES, 0)] * 2 + [rev(DN_WIDTH, 3), pl.BlockSpec((1, HEAD_DIM), lambda i: (0, 0))],
        out_shape=[jax.ShapeDtypeStruct((t, DN_WIDTH), F32)] * 3 + [jax.ShapeDtypeStruct((t, LANES), F32)] * 2
        + [jax.ShapeDtypeStruct(dp_buf.shape, dp_buf.dtype), jax.ShapeDtypeStruct((1, HEAD_DIM), F32)],
        input_output_aliases={10: 5},
        scratch_shapes=[pltpu.VMEM((N_HEADS, HEAD_DIM, HEAD_DIM), F32)],
        compiler_params=_params("arbitrary"), name="dn_backward")(q, k, v, beta4, g4, p, norm_g, s_all, a_inv_all, dmix, dp_buf)


def _sg_mask():
    row = lax.broadcasted_iota(jnp.int32, (SG_BLOCK, SG_BLOCK), 0)
    col = lax.broadcasted_iota(jnp.int32, (SG_BLOCK, SG_BLOCK), 1)
    return (col // CHUNK) <= (row // CHUNK)


def _sg_forward(p, norm_g, w_s, b_t, mix_buf):
    t = p.shape[0]

    def body(u_ref, v_ref, ng_ref, w_ref, b_ref, _, o_ref):
        mask = _sg_mask()
        for g in range(SG_GROUPS):
            sl = slice(g * SG_DIM, (g + 1) * SG_DIM)
            vn, _ = rms_fwd(gelu(v_ref[:, sl]), ng_ref[:, sl])
            s = dot_nn(jnp.where(mask, w_ref[g], 0.0), vn, FAST) + b_ref[:, g:g + 1]
            o_ref[:, sl] = (gelu(u_ref[:, sl]) * s).astype(BF16)

    blk = lambda cb: pl.BlockSpec((SG_BLOCK, SG_WIDTH), lambda i: (i, cb))
    return _pcall(
        body, grid=(t // SG_BLOCK,),
        in_specs=[blk(4), blk(5), pl.BlockSpec((1, SG_WIDTH), lambda i: (0, 0)),
                  pl.BlockSpec((SG_GROUPS, SG_BLOCK, SG_BLOCK), lambda i: (0, 0, 0)), pl.BlockSpec((SG_BLOCK, SG_GROUPS), lambda i: (0, 0)),
                  pl.BlockSpec(memory_space=pl.ANY)],
        out_specs=blk(1), out_shape=jax.ShapeDtypeStruct(mix_buf.shape, mix_buf.dtype), input_output_aliases={5: 0},
        compiler_params=_params("parallel"), name="sg_forward")(p, p, norm_g, w_s, b_t, mix_buf)


def _sg_backward(p, norm_g, w_s, b_t, dmix):
    t = p.shape[0]

    def body(u_ref, v_ref, ng_ref, w_ref, b_ref, do_ref, duv_ref, dng_ref, dw_ref, db_ref):
        @pl.when(pl.program_id(0) == 0)
        def _():
            dng_ref[...] = jnp.zeros_like(dng_ref)
            dw_ref[...] = jnp.zeros_like(dw_ref)
            db_ref[...] = jnp.zeros_like(db_ref)

        mask = _sg_mask()
        lane = lax.broadcasted_iota(jnp.int32, (SG_BLOCK, LANES), 1)
        db = jnp.zeros((SG_BLOCK, LANES), F32)
        for g in range(SG_GROUPS):
            sl = slice(g * SG_DIM, (g + 1) * SG_DIM)
            u_raw, v_raw, do = u_ref[:, sl], v_ref[:, sl], do_ref[:, sl]
            vg = gelu(v_raw)
            vn, r = rms_fwd(vg, ng_ref[:, sl])
            w_m = jnp.where(mask, w_ref[g], 0.0)
            s = dot_nn(w_m, vn, FAST) + b_ref[:, g:g + 1]
            duv_ref[:, sl] = (do * s * gelu_grad(u_raw)).astype(BF16)
            ds = do * gelu(u_raw)
            db = jnp.where(lane == g, jnp.sum(ds, axis=1, keepdims=True), db)
            dw_ref[g] += jnp.where(mask, dot_nt(ds, vn, FAST), 0.0)
            dvg, dng_rows = rms_bwd(vg, r, ng_ref[:, sl], dot_tn(w_m, ds, FAST))
            dng_ref[:, sl] += jnp.sum(dng_rows, axis=0, keepdims=True)
            duv_ref[:, SG_WIDTH + g * SG_DIM:SG_WIDTH + (g + 1) * SG_DIM] = (dvg * gelu_grad(v_raw)).astype(BF16)
        db_ref[...] += db

    blk = lambda cb: pl.BlockSpec((SG_BLOCK, SG_WIDTH), lambda i: (i, cb))
    const2 = lambda shape: pl.BlockSpec(shape, lambda i: (0, 0))
    w_spec = pl.BlockSpec((SG_GROUPS, SG_BLOCK, SG_BLOCK), lambda i: (0, 0, 0))
    return _pcall(
        body, grid=(t // SG_BLOCK,),
        in_specs=[blk(4), blk(5), const2((1, SG_WIDTH)), w_spec, const2((SG_BLOCK, SG_GROUPS)), blk(1)],
        out_specs=[pl.BlockSpec((SG_BLOCK, 2 * SG_WIDTH), lambda i: (i, 2)), const2((1, SG_WIDTH)), w_spec,
                   const2((SG_BLOCK, LANES))],
        out_shape=[jax.ShapeDtypeStruct((t, PROJ_PAD), BF16), jax.ShapeDtypeStruct((1, SG_WIDTH), F32),
                   jax.ShapeDtypeStruct((SG_GROUPS, SG_BLOCK, SG_BLOCK), F32), jax.ShapeDtypeStruct((SG_BLOCK, LANES), F32)],
        compiler_params=_params("arbitrary"), name="sg_backward")(p, p, norm_g, w_s, b_t, dmix)


FFN_CT = D_FF // 2


def _ffn_act(up, conv_w, conv_b):
    t = up.shape[0]
    tm = _token_tile(t)
    nj = D_FF // FFN_CT

    def body(ug_ref, uv_ref, hg_ref, hv_ref, wg_ref, wv_ref, bg_ref, bv_ref, act_ref, xg_ref, xv_ref):
        first = pl.program_id(0) == 0
        _fill_with_prev(xg_ref, ug_ref[...], hg_ref[...], first)
        _fill_with_prev(xv_ref, uv_ref[...], hv_ref[...], first)

        def strip(row0):
            for c0 in range(0, FFN_CT, LANES):
                cols = slice(c0, c0 + LANES)
                cg = _causal_conv(_delayed(xg_ref, row0, cols, FFN_CONV), wg_ref[:, cols]) + bg_ref[:, cols]
                cv = _causal_conv(_delayed(xv_ref, row0, cols, FFN_CONV), wv_ref[:, cols]) + bv_ref[:, cols]
                act_ref[pl.ds(row0, STRIP), cols] = (silu(cg) * cv).astype(BF16)

        _for_strips(tm, STRIP, strip)

    tok = lambda off: pl.BlockSpec((tm, FFN_CT), lambda i, j: (i, j + off))
    halo = lambda off: pl.BlockSpec((HALO, FFN_CT), lambda i, j: (jnp.maximum(i * (tm // HALO) - 1, 0), j + off))
    par = lambda rows, off: pl.BlockSpec((rows, FFN_CT), lambda i, j: (0, j + off))
    return _pcall(
        body, grid=(t // tm, nj),
        in_specs=[tok(0), tok(nj), halo(0), halo(nj), par(FFN_CONV, 0), par(FFN_CONV, nj), par(1, 0), par(1, nj)],
        out_specs=pl.BlockSpec((tm, FFN_CT), lambda i, j: (i, j)),
        out_shape=jax.ShapeDtypeStruct((t, D_FF), BF16),
        scratch_shapes=[pltpu.VMEM((HALO + tm, FFN_CT), F32)] * 2,
        compiler_params=_params("parallel", "parallel"), name="ffn_act")(up, up, up, up, conv_w, conv_w, conv_b, conv_b)


FFN_COLS = 256


def _norm_up_ffn(x, g, w_up_t, conv_w, conv_b):
    t, d = x.shape
    tm = min(t, 256)
    blocks = D_FF // FFN_COLS
    nt = (((1,), (1,)), ((), ()))

    def body(x_ref, g_ref, w_ref, cw_ref, cb_ref, up_ref, act_ref, h_ref, r_ref, tail_ref, xg_ref, xv_ref):
        @pl.when(pl.program_id(0) == 0)
        def _():
            tail_ref[...] = jnp.zeros_like(tail_ref)

        y, r = rms_fwd(x_ref[...], g_ref[...])
        h = y.astype(BF16)
        h_ref[...] = h
        r_ref[...] = r

        def project(blk):
            out = []
            for half, x_buf in ((0, xg_ref), (1, xv_ref)):
                cols = slice(half * D_FF + blk * FFN_COLS, half * D_FF + (blk + 1) * FFN_COLS)
                u = lax.dot_general(h, w_ref[cols, :], nt, preferred_element_type=F32)
                up_ref[:, cols] = u
                x_buf[blk % 2, 0:HALO, :] = tail_ref[:, cols]
                x_buf[blk % 2, HALO:, :] = u
                tail_ref[:, cols] = u[tm - HALO:, :]
                out.append(cols)
            return out

        def activate(blk, g_cols, v_cols):
            xg, xv = xg_ref.at[blk % 2], xv_ref.at[blk % 2]
            for row0 in range(0, tm, STRIP):
                for c0 in range(0, FFN_COLS, LANES):
                    lanes = slice(c0, c0 + LANES)
                    gc = slice(g_cols.start + c0, g_cols.start + c0 + LANES)
                    vc = slice(v_cols.start + c0, v_cols.start + c0 + LANES)
                    cg = _causal_conv(_delayed(xg, row0, lanes, FFN_CONV), cw_ref[:, gc]) + cb_ref[:, gc]
                    cv = _causal_conv(_delayed(xv, row0, lanes, FFN_CONV), cw_ref[:, vc]) + cb_ref[:, vc]
                    act_ref[row0:row0 + STRIP, blk * FFN_COLS + c0:blk * FFN_COLS + c0 + LANES] = (silu(cg) * cv).astype(BF16)

        pending = None
        for blk in range(blocks):
            cols = project(blk)
            if pending is not None:
                activate(*pending)
            pending = (blk, *cols)
        activate(*pending)

    rows = lambda w: pl.BlockSpec((tm, w), lambda i: (i, 0))
    whole = lambda a: pl.BlockSpec(a.shape, lambda i: (0, 0))
    return _pcall(
        body, grid=(t // tm,),
        in_specs=[rows(d), whole(g), whole(w_up_t), whole(conv_w), whole(conv_b)],
        out_specs=[rows(2 * D_FF), rows(D_FF), rows(d), rows(1)],
        out_shape=[jax.ShapeDtypeStruct((t, 2 * D_FF), F32), jax.ShapeDtypeStruct((t, D_FF), BF16),
                   jax.ShapeDtypeStruct((t, d), BF16), jax.ShapeDtypeStruct((t, 1), F32)],
        scratch_shapes=[pltpu.VMEM((HALO, 2 * D_FF), F32), pltpu.VMEM((2, HALO + tm, FFN_COLS), F32),
                        pltpu.VMEM((2, HALO + tm, FFN_COLS), F32)],
        compiler_params=_params("arbitrary"), name="norm_up_ffn")(x, g, w_up_t, conv_w, conv_b)


def _ffn_bwd(up, conv_w, conv_b, dact):
    t = up.shape[0]
    tm = _pick(t, 128)
    n_tok = t // tm
    width = 2 * D_FF

    def dconv(delayed_g, delayed_v, da, wg, wv, bg, bv):
        cg = _causal_conv(delayed_g, wg) + bg
        cv = _causal_conv(delayed_v, wv) + bv
        s = sigmoid(cg)
        return da * cv * (s * (1.0 + cg * (1.0 - s))), da * (cg * s)

    def body(up_ref, prev_ref, next_ref, da_ref, dan_ref, w_ref, b_ref, dup_ref, dw_ref, db_ref, xp_ref, dc_ref, dw_acc, db_acc):
        first = pl.program_id(0) == 0
        last = pl.program_id(0) == n_tok - 1
        xp_ref[0:HALO, :] = jnp.where(first, 0.0, prev_ref[...])
        xp_ref[HALO:HALO + tm, :] = up_ref[...]
        xp_ref[HALO + tm:, :] = next_ref[...]
        dw_acc[...] = jnp.zeros_like(dw_acc)
        db_acc[...] = jnp.zeros_like(db_acc)

        def strip(row0):
            rows = pl.ds(row0, STRIP)
            for c0 in range(0, D_FF, LANES):
                gc, vc = slice(c0, c0 + LANES), slice(D_FF + c0, D_FF + c0 + LANES)
                del_g, del_v = _delayed(xp_ref, row0, gc, FFN_CONV), _delayed(xp_ref, row0, vc, FFN_CONV)
                dcg, dcv = dconv(del_g, del_v, da_ref[rows, gc], w_ref[:, gc], w_ref[:, vc], b_ref[:, gc], b_ref[:, vc])
                dc_ref[rows, gc] = dcg
                dc_ref[rows, vc] = dcv
                db_acc[:, gc] += _fold_rows(dcg)
                db_acc[:, vc] += _fold_rows(dcv)
                for j in range(FFN_CONV):
                    k = FFN_CONV - 1 - j
                    dw_acc[k * SUBLANES:(k + 1) * SUBLANES, gc] += _fold_rows(dcg * del_g[j])
                    dw_acc[k * SUBLANES:(k + 1) * SUBLANES, vc] += _fold_rows(dcv * del_v[j])

        _for_strips(tm, STRIP, strip)

        for c0 in range(0, D_FF, LANES):
            gc, vc = slice(c0, c0 + LANES), slice(D_FF + c0, D_FF + c0 + LANES)

            def delayed(cols):
                ext = xp_ref[tm:tm + 2 * HALO, cols]
                return [ext[HALO:, :]] + [pltpu.roll(ext, j, 0)[HALO:, :] for j in range(1, FFN_CONV)]

            dcg, dcv = dconv(delayed(gc), delayed(vc), dan_ref[:, gc], w_ref[:, gc], w_ref[:, vc], b_ref[:, gc], b_ref[:, vc])
            dc_ref[tm:, gc] = jnp.where(last, 0.0, dcg)
            dc_ref[tm:, vc] = jnp.where(last, 0.0, dcv)

        def strip_dx(row0):
            for c0 in range(0, width, LANES):
                cols = slice(c0, c0 + LANES)
                dup_ref[pl.ds(row0, STRIP), cols] = _advanced_conv(dc_ref, row0, cols, w_ref[:, cols]).astype(BF16)

        _for_strips(tm, STRIP, strip_dx)

        @pl.when(first)
        def _():
            dw_ref[...] = jnp.zeros_like(dw_ref)
            db_ref[...] = jnp.zeros_like(db_ref)

        for k in range(FFN_CONV):
            dw_ref[k:k + 1, :] += jnp.sum(dw_acc[k * SUBLANES:(k + 1) * SUBLANES, :], axis=0, keepdims=True)
        db_ref[...] += jnp.sum(db_acc[...], axis=0, keepdims=True)

    next_rows = lambda i: jnp.minimum((i + 1) * (tm // HALO), t // HALO - 1)
    full = lambda rows: pl.BlockSpec((rows, width), lambda i: (0, 0))
    return _pcall(
        body, grid=(n_tok,),
        in_specs=[pl.BlockSpec((tm, width), lambda i: (i, 0)),
                  pl.BlockSpec((HALO, width), lambda i: (jnp.maximum(i * (tm // HALO) - 1, 0), 0)),
                  pl.BlockSpec((HALO, width), lambda i: (next_rows(i), 0)),
                  pl.BlockSpec((tm, D_FF), lambda i: (i, 0)), pl.BlockSpec((HALO, D_FF), lambda i: (next_rows(i), 0)),
                  full(FFN_CONV), full(1)],
        out_specs=[pl.BlockSpec((tm, width), lambda i: (i, 0)), full(FFN_CONV), full(1)],
        out_shape=[jax.ShapeDtypeStruct((t, width), BF16), jax.ShapeDtypeStruct((FFN_CONV, width), F32),
                   jax.ShapeDtypeStruct((1, width), F32)],
        scratch_shapes=[pltpu.VMEM((tm + 2 * HALO, width), F32), pltpu.VMEM((tm + HALO, width), F32),
                        pltpu.VMEM((FFN_CONV * SUBLANES, width), F32), pltpu.VMEM((SUBLANES, width), F32)],
        compiler_params=_params("arbitrary"), name="ffn_bwd")(up, up, up, dact, dact, conv_w, conv_b)


def _my_position():
    return lax.axis_index("x"), lax.axis_index("y"), lax.axis_index("c")


COPIES = N_DEV - 1


def _all_gather(arrays):
    n = len(arrays)

    def body(*refs):
        x_refs, out_refs = refs[:n], refs[n:2 * n]
        send_sems, recv_sems, local_sems = refs[2 * n:]
        x, y, cc = _my_position()
        me, sibling = (x, y, cc), (x, y, 1 - cc)
        chips = [(1 - x, y), (x, 1 - y), (1 - x, 1 - y)]

        def block(a, px, py, pc):
            return out_refs[a].at[4 * px + 2 * py + pc]

        def copy(a, k, blk, to, src=None):
            return pltpu.make_async_remote_copy(
                src_ref=block(a, *blk) if src is None else src, dst_ref=block(a, *blk),
                send_sem=send_sems.at[a * COPIES + k], recv_sem=recv_sems.at[a * COPIES + k],
                device_id=to, device_id_type=MESH_ID)

        mine = [pltpu.make_async_copy(x_refs[a], block(a, *me), local_sems.at[a]) for a in range(n)]
        for cp in mine:
            cp.start()
        first = []
        for a in range(n):
            first.append(copy(a, 0, me, sibling, src=x_refs[a]))
            first += [copy(a, 1 + j, me, (*chip, cc), src=x_refs[a]) for j, chip in enumerate(chips)]
        for cp in first:
            cp.start()
        passed = []
        for j, chip in enumerate(chips):
            for a in range(n):
                copy(a, 1 + j, (*chip, cc), me).wait_recv()
                passed.append(copy(a, 4 + j, (*chip, cc), sibling))
                passed[-1].start()
        for a in range(n):
            copy(a, 0, sibling, me).wait_recv()
        for j, chip in enumerate(chips):
            for a in range(n):
                copy(a, 4 + j, (*chip, 1 - cc), me).wait_recv()
        for cp in first + passed:
            cp.wait_send()
        for cp in mine:
            cp.wait()

    any_spec = pl.BlockSpec(memory_space=pl.ANY)
    return _pcall(
        body, out_shape=[jax.ShapeDtypeStruct((N_DEV,) + a.shape, a.dtype) for a in arrays],
        in_specs=[any_spec] * n, out_specs=[any_spec] * n,
        scratch_shapes=[pltpu.SemaphoreType.DMA((n * COPIES,)), pltpu.SemaphoreType.DMA((n * COPIES,)),
                        pltpu.SemaphoreType.DMA((n,))],
        name="all_gather")(*arrays)


def _all_to_all(sends):
    n = len(sends)

    def body(*refs):
        send_refs, recv_refs = refs[:n], refs[n:2 * n]
        send_sems, recv_sems, local_sems = refs[2 * n:]
        x, y, cc = _my_position()
        me = 4 * x + 2 * y + cc
        mine = [pltpu.make_async_copy(send_refs[a].at[me], recv_refs[a].at[me], local_sems.at[a]) for a in range(n)]
        for cp in mine:
            cp.start()
        copies = []
        for rel in range(1, N_DEV):
            px, py, pc = x ^ (rel >> 2), y ^ ((rel >> 1) & 1), cc ^ (rel & 1)
            for a in range(n):
                copies.append(pltpu.make_async_remote_copy(
                    src_ref=send_refs[a].at[4 * px + 2 * py + pc], dst_ref=recv_refs[a].at[me],
                    send_sem=send_sems.at[a * COPIES + rel - 1], recv_sem=recv_sems.at[a * COPIES + rel - 1],
                    device_id=(px, py, pc), device_id_type=MESH_ID))
        for cp in copies:
            cp.start()
        for cp in copies:
            cp.wait()
        for cp in mine:
            cp.wait()

    any_spec = pl.BlockSpec(memory_space=pl.ANY)
    return _pcall(
        body, out_shape=[jax.ShapeDtypeStruct(s.shape, s.dtype) for s in sends],
        in_specs=[any_spec] * n, out_specs=[any_spec] * n,
        scratch_shapes=[pltpu.SemaphoreType.DMA((n * COPIES,)), pltpu.SemaphoreType.DMA((n * COPIES,)),
                        pltpu.SemaphoreType.DMA((n,))],
        name="all_to_all")(*sends)


def _hbm(a):
    return pltpu.with_memory_space_constraint(a, pltpu.HBM)


def _split_copies(send_refs, land_refs, send_sems, recv_sems, local_sems, gather):
    x, y, cc = _my_position()
    me = 4 * x + 2 * y + cc
    local, remote = [], []
    for a, (send, land) in enumerate(zip(send_refs, land_refs)):
        local.append(pltpu.make_async_copy(send if gather else send.at[me], land.at[me], local_sems.at[a]))
    for a, (send, land) in enumerate(zip(send_refs, land_refs)):
        for rel in range(1, N_DEV):
            px, py, pc = x ^ (rel >> 2), y ^ ((rel >> 1) & 1), cc ^ (rel & 1)
            remote.append(pltpu.make_async_remote_copy(
                src_ref=send if gather else send.at[4 * px + 2 * py + pc], dst_ref=land.at[me],
                send_sem=send_sems.at[a * COPIES + rel - 1], recv_sem=recv_sems.at[a * COPIES + rel - 1],
                device_id=(px, py, pc), device_id_type=MESH_ID))
    return local, remote


SPLIT_EFFECT = pltpu.SideEffectType.DATAFLOW_SIDE_EFFECTING


def _exchange_start(sends, after, gather, name):
    n = len(sends)
    lands = [_hbm(lax.empty((N_DEV,) + s.shape if gather else s.shape, s.dtype)) for s in sends]

    def body(*refs):
        send_refs, land_refs = refs[:n], refs[n:2 * n]
        send_sems, recv_sems, local_sems = refs[2 * n + 1:2 * n + 4]
        token = refs[-1]
        local, remote = _split_copies(send_refs, land_refs, send_sems, recv_sems, local_sems, gather)
        for cp in local + remote:
            cp.start()
        token[...] = jnp.zeros_like(token)

    hbm, sem = pl.BlockSpec(memory_space=pltpu.HBM), pl.BlockSpec(memory_space=pltpu.SEMAPHORE)
    out = _pcall(
        body, name=name,
        out_shape=[pltpu.SemaphoreType.DMA((n * COPIES,)), pltpu.SemaphoreType.DMA((n * COPIES,)), pltpu.SemaphoreType.DMA((n,))]
        + [pltpu.HBM(s.shape, s.dtype) for s in sends] + [pltpu.HBM(z.shape, z.dtype) for z in lands]
        + [jax.ShapeDtypeStruct((SUBLANES, LANES), F32)],
        in_specs=[hbm] * (2 * n) + [pl.BlockSpec(memory_space=pl.ANY)],
        out_specs=[sem] * 3 + [hbm] * (2 * n) + [pl.BlockSpec(memory_space=pltpu.VMEM)],
        input_output_aliases={i: 3 + i for i in range(2 * n)},
        compiler_params=pltpu.CompilerParams(has_side_effects=SPLIT_EFFECT),
    )(*[_hbm(s) for s in sends], *lands, after)
    return dict(sems=out[:3], sends=out[3:3 + n], lands=out[3 + n:3 + 2 * n], gather=gather), out[-1]


def _exchange_wait(handle, after, name):
    sends, lands, gather = handle["sends"], handle["lands"], handle["gather"]
    n = len(sends)

    def body(*refs):
        send_refs, land_refs = refs[:n], refs[n:2 * n]
        send_sems, recv_sems, local_sems = refs[2 * n:2 * n + 3]
        local, remote = _split_copies(send_refs, land_refs, send_sems, recv_sems, local_sems, gather)
        for cp in remote:
            cp.wait_send()
            cp.wait_recv()
        for cp in local:
            cp.wait()

    hbm, sem = pl.BlockSpec(memory_space=pltpu.HBM), pl.BlockSpec(memory_space=pltpu.SEMAPHORE)
    out = _pcall(
        body, name=name,
        out_shape=[pltpu.HBM(s.shape, s.dtype) for s in sends] + [pltpu.HBM(z.shape, z.dtype) for z in lands],
        in_specs=[hbm] * (2 * n) + [sem] * 3 + [pl.BlockSpec(memory_space=pl.ANY)],
        out_specs=[hbm] * (2 * n), input_output_aliases={i: i for i in range(2 * n)},
        compiler_params=pltpu.CompilerParams(has_side_effects=SPLIT_EFFECT),
    )(*sends, *lands, *handle["sems"], after)
    return out[n:]


def _sum_and_adamw(recv, w, m, v, name):
    _, r, wp = recv.shape
    c = w.shape[-1]
    lead = w.ndim == 3
    tr = SLAB_ROW_TILE if r % SLAB_ROW_TILE == 0 else (SLAB_ROW_TILE // 4 if r % (SLAB_ROW_TILE // 4) == 0 else r)
    bc1 = 1.0 - ADAM_B1 ** ADAM_STEP
    bc2 = 1.0 - ADAM_B2 ** ADAM_STEP

    def body(recv_ref, w_ref, m_ref, v_ref, g_ref, d_ref, nm_ref, nv_ref):
        g = recv_ref[0, :, 0:c].astype(F32)
        for s in range(1, N_DEV):
            g = g + recv_ref[s, :, 0:c].astype(F32)
        m_new = ADAM_B1 * m_ref[...] + (1.0 - ADAM_B1) * g
        v_new = ADAM_B2 * v_ref[...] + (1.0 - ADAM_B2) * (g * g)
        m_hat = m_new / bc1
        v_hat = v_new / bc2
        g_ref[...] = g
        d_ref[...] = -ADAM_LR * (m_hat / (jnp.sqrt(v_hat) + ADAM_EPS) + ADAM_WD * w_ref[...])
        nm_ref[...] = m_new
        nv_ref[...] = v_new

    tile = pl.BlockSpec((None, tr, c), lambda i: (0, i, 0)) if lead else pl.BlockSpec((tr, c), lambda i: (i, 0))
    return _pcall(
        body, grid=(r // tr,),
        in_specs=[pl.BlockSpec((N_DEV, tr, wp), lambda i: (0, i, 0)), tile, tile, tile],
        out_specs=[tile] * 4, out_shape=[jax.ShapeDtypeStruct(w.shape, F32)] * 4,
        compiler_params=_params("parallel"), name=name)(recv, w, m, v)


SHARDED_TAPS = ("dn_conv_w", "ffn_conv_w")
REPLICATED = ("attn_norm_g", "dn_a_log", "dn_dt_bias", "dn_out_norm_g", "sg_norm_g", "sg_w", "sg_b", "ffn_norm_g",
              "ffn_conv_b", "final_norm_g")
SMALL = SHARDED_TAPS + REPLICATED
WEIGHT_ORDER = ("attn_norm_g", "w_in", "dn_conv_w", "dn_a_log", "dn_dt_bias", "dn_out_norm_g", "sg_norm_g", "sg_w", "sg_b",
                "w_out", "ffn_norm_g", "w_up", "ffn_conv_w", "ffn_conv_b", "w_down", "final_norm_g")
SLAB_COLS = 1024
SLAB_ROW_TILE = 128


def _pad_to(flat, multiple):
    pad = (-flat.shape[-1]) % multiple
    if pad == 0:
        return flat
    return jnp.pad(flat, [(0, 0)] * (flat.ndim - 1) + [(0, pad)])


def _pack_small(named):
    flat = jnp.concatenate([named[n].reshape(-1) for n in SMALL])
    return _pad_to(flat, SUBLANES * SLAB_COLS).reshape(-1, SLAB_COLS)


def _unpack_small(slab, like):
    flat = slab.reshape(-1)
    out, off = {}, 0
    for n in SMALL:
        size = like[n].size
        out[n] = flat[off:off + size].reshape(like[n].shape)
        off += size
    return out


def _split_columns(full, n_local):
    r = full.shape[0]
    return full.reshape(r, N_DEV, n_local).transpose(1, 0, 2).reshape(N_DEV, r * n_local)


def _join_columns(blocks, r, n_local):
    return blocks.reshape(N_DEV, r, n_local).transpose(1, 0, 2).reshape(r, N_DEV * n_local)


def _lanes4(a):
    return jnp.pad(a.reshape(1, N_HEADS), ((0, 0), (0, LANES - N_HEADS)))


def kernel(x, attn_norm_g, w_in, dn_conv_w, dn_a_log, dn_dt_bias, dn_out_norm_g, sg_norm_g, sg_w, sg_b, w_out, ffn_norm_g, w_up, ffn_conv_w, ffn_conv_b, w_down, final_norm_g, loss_target, m_attn_norm_g, m_w_in, m_dn_conv_w, m_dn_a_log, m_dn_dt_bias, m_dn_out_norm_g, m_sg_norm_g, m_sg_w, m_sg_b, m_w_out, m_ffn_norm_g, m_w_up, m_ffn_conv_w, m_ffn_conv_b, m_w_down, m_final_norm_g, v_attn_norm_g, v_w_in, v_dn_conv_w, v_dn_a_log, v_dn_dt_bias, v_dn_out_norm_g, v_sg_norm_g, v_sg_w, v_sg_b, v_w_out, v_ffn_norm_g, v_w_up, v_ffn_conv_w, v_ffn_conv_b, v_w_down, v_final_norm_g):
    weights = dict(attn_norm_g=attn_norm_g, w_in=w_in, dn_conv_w=dn_conv_w, dn_a_log=dn_a_log, dn_dt_bias=dn_dt_bias,
                   dn_out_norm_g=dn_out_norm_g, sg_norm_g=sg_norm_g, sg_w=sg_w, sg_b=sg_b, w_out=w_out, ffn_norm_g=ffn_norm_g,
                   w_up=w_up, ffn_conv_w=ffn_conv_w, ffn_conv_b=ffn_conv_b, w_down=w_down, final_norm_g=final_norm_g)
    m_in = dict(attn_norm_g=m_attn_norm_g, w_in=m_w_in, dn_conv_w=m_dn_conv_w, dn_a_log=m_dn_a_log, dn_dt_bias=m_dn_dt_bias,
                dn_out_norm_g=m_dn_out_norm_g, sg_norm_g=m_sg_norm_g, sg_w=m_sg_w, sg_b=m_sg_b, w_out=m_w_out,
                ffn_norm_g=m_ffn_norm_g, w_up=m_w_up, ffn_conv_w=m_ffn_conv_w, ffn_conv_b=m_ffn_conv_b, w_down=m_w_down,
                final_norm_g=m_final_norm_g)
    v_in = dict(attn_norm_g=v_attn_norm_g, w_in=v_w_in, dn_conv_w=v_dn_conv_w, dn_a_log=v_dn_a_log, dn_dt_bias=v_dn_dt_bias,
                dn_out_norm_g=v_dn_out_norm_g, sg_norm_g=v_sg_norm_g, sg_w=v_sg_w, sg_b=v_sg_b, w_out=v_w_out,
                ffn_norm_g=v_ffn_norm_g, w_up=v_w_up, ffn_conv_w=v_ffn_conv_w, ffn_conv_b=v_ffn_conv_b, w_down=v_w_down,
                final_norm_g=v_final_norm_g)

    n_in, n_up = w_in.shape[2], w_up.shape[2]
    r_out, r_down = w_out.shape[1], w_down.shape[1]
    n_dnc, n_ffc = dn_conv_w.shape[2], ffn_conv_w.shape[2]
    transposed = lambda a: jnp.transpose(a, (0, 2, 1))
    taps = _pad_to(jnp.concatenate([dn_conv_w.reshape(-1), ffn_conv_w.reshape(-1)]), SUBLANES * LANES).reshape(-1, LANES)
    g_in, g_taps = _all_gather([transposed(w_in)[0].astype(BF16), taps])
    gather_out, token = _exchange_start([w_out[0].astype(BF16)], g_taps, True, "gather_w_out")
    gather_up, token = _exchange_start([transposed(w_up)[0].astype(BF16)], token, True, "gather_w_up")
    gather_down, token = _exchange_start([w_down[0].astype(BF16)], token, True, "gather_w_down")
    w_in_t = jnp.pad(g_in.reshape(N_DEV * n_in, D_MODEL), ((0, PROJ_PAD - N_DEV * n_in), (0, 0)))
    taps_all = g_taps.reshape(N_DEV, -1)
    dn_conv_full = _join_columns(taps_all[:, :CONV_K * n_dnc], CONV_K, n_dnc)
    ffn_conv_full = _join_columns(taps_all[:, CONV_K * n_dnc:CONV_K * n_dnc + FFN_CONV * n_ffc], FFN_CONV, n_ffc)
    late = dict(
        w_out=lambda after: _exchange_wait(gather_out, after, "gather_w_out_wait")[0].reshape(N_DEV * r_out, D_MODEL),
        w_up_t=lambda after: _exchange_wait(gather_up, after, "gather_w_up_wait")[0].reshape(N_DEV * n_up, D_MODEL),
        w_down=lambda after: _exchange_wait(gather_down, after, "gather_w_down_wait")[0].reshape(N_DEV * r_down, D_MODEL))

    def send_early(blocks, after, name):
        return _exchange_start(blocks, after, False, name)

    def send_small(g, loss_lanes, after):
        small = jnp.concatenate([g[n].reshape(-1) for n in REPLICATED] + [loss_lanes[0, 0:1]])
        slab = jnp.concatenate([_split_columns(g["dn_conv_w"], n_dnc), _split_columns(g["ffn_conv_w"], n_ffc),
                                jnp.broadcast_to(small[None, :], (N_DEV, small.shape[0]))], axis=1)
        return send_early([_pad_to(slab, SUBLANES * SLAB_COLS).reshape(N_DEV, -1, SLAB_COLS)], after, "send_small")

    upd = {}

    def update_early(sent_down, sent_up_out, sent_small, after):
        r_dn, = _exchange_wait(sent_down, after, "send_dw_down_wait")
        r_up, r_o = _exchange_wait(sent_up_out, after, "send_dw_up_out_wait")
        r_small, = _exchange_wait(sent_small, after, "send_small_wait")
        upd["w_down"] = _sum_and_adamw(r_dn, w_down, m_w_down, v_w_down, "adamw_w_down")
        upd["w_up"] = [transposed(o) for o in _sum_and_adamw(r_up, transposed(w_up), transposed(m_w_up), transposed(v_w_up),
                                                             "adamw_w_up")]
        upd["w_out"] = _sum_and_adamw(r_o, w_out, m_w_out, v_w_out, "adamw_w_out")
        upd["small"] = _sum_and_adamw(r_small, _pack_small(weights), _pack_small(m_in), _pack_small(v_in), "adamw_small")

    grad_x, d_g1, sent_in = _local_step(
        x[0], loss_target[0], w_in_t, late, send_early, send_small, update_early, dn_conv_full, ffn_conv_full,
        attn_norm_g + token[0:1, 0:1], dn_a_log, dn_dt_bias, dn_out_norm_g, sg_norm_g, sg_w, sg_b, ffn_norm_g, ffn_conv_b,
        final_norm_g, n_in)

    norm_rows = D_MODEL // LANES
    r_g1, = _all_to_all([jnp.broadcast_to(d_g1.reshape(1, norm_rows, LANES), (N_DEV, norm_rows, LANES))])
    r_in, = _exchange_wait(sent_in, r_g1, "send_dw_in_wait")
    upd["w_in"] = [transposed(o) for o in _sum_and_adamw(r_in, transposed(w_in), transposed(m_w_in), transposed(v_w_in),
                                                         "adamw_w_in")]
    small_upd = upd.pop("small")
    as_rows = lambda a: a.reshape(norm_rows, LANES)
    norm_upd = _sum_and_adamw(r_g1, as_rows(attn_norm_g), as_rows(m_attn_norm_g), as_rows(v_attn_norm_g), "adamw_attn_norm")
    results = []
    for i in range(4):
        named = _unpack_small(small_upd[i], weights)
        named.update({n: upd[n][i] for n in upd})
        named["attn_norm_g"] = norm_upd[i].reshape(attn_norm_g.shape)
        results.append(named)

    loss = small_upd[0].reshape(-1)[sum(weights[n].size for n in SMALL)]
    return (loss, grad_x[None], *[r[n] for r in results for n in WEIGHT_ORDER])


def _local_step(x2d, tgt, w_in_t, late, send_early, send_small, update_early, dn_conv_full, ffn_conv_full, attn_norm_g,
                dn_a_log, dn_dt_bias, dn_out_norm_g, sg_norm_g, sg_w, sg_b, ffn_norm_g, ffn_conv_b, final_norm_g, n_in):
    t = x2d.shape[0]
    g1, g2, gf = attn_norm_g, ffn_norm_g, final_norm_g.reshape(1, D_MODEL)
    a_log4, dt_bias4 = _lanes4(dn_a_log), _lanes4(dn_dt_bias)
    sg_w3 = sg_w[0]
    sg_b_t = sg_b[0].T
    conv_b = ffn_conv_b

    p, h1, rstd1 = _rmsnorm_matmul(x2d, g1, w_in_t, "norm_in_proj", 512)
    q, k, v, beta4, g4 = _dn_prep(p, dn_conv_full, a_log4, dt_bias4)
    mix_half, s_all, a_inv_all = _dn_forward(q, k, v, beta4, g4, p, dn_out_norm_g)
    mix = _sg_forward(p, sg_norm_g, sg_w3, sg_b_t, mix_half)
    w_out_full = late["w_out"](mix)
    x2 = _matmul(mix, w_out_full, "nn", "out_proj", (1024, 1024, 1024), add=x2d)
    w_up_t = late["w_up_t"](x2)
    up, act, h2, rstd2 = _norm_up_ffn(x2, g2, w_up_t, ffn_conv_full, conv_b)
    w_down_full = late["w_down"](act)
    fn, outs = _final_loss_rows(t, D_MODEL)
    loss_lanes, dx3, dx3b, d_gf = _matmul_rows(act, w_down_full, "nn", "down_proj_loss", 512,
                                               [(x2, "rows"), (tgt, "rows"), (gf, "whole")], outs, fn)

    dact = _matmul(dx3b, w_down_full, "nt", "down_proj_dx", (512, D_FF, D_MODEL))
    d_w_down = _matmul(act, dx3b, "tn", "down_proj_dw", (256, 1024, t), out_dtype=BF16)
    sent_down, token = send_early([d_w_down.reshape(N_DEV, D_FF // N_DEV, D_MODEL)], d_w_down, "send_dw_down")
    dup, d_ffn_conv, d_ffn_conv_b = _ffn_bwd(up, ffn_conv_full, conv_b + token[0:1, 0:1], dact)
    fn, outs = _rmsnorm_bwd_rows(t, D_MODEL)
    dx2, dx2b, d_g2 = _matmul_rows(dup, w_up_t, "nn", "up_proj_dx_norm", 256,
                                   [(x2, "rows"), (rstd2, "rows"), (g2, "whole"), (dx3, "rows")], outs, fn)
    d_w_up_t = _matmul(dup, h2, "tn", "up_proj_dw", (512, 1024, t), out_dtype=BF16)
    dmix = _matmul(dx2b, w_out_full, "nt", "out_proj_dx", (1024, 1024, 1024))
    d_w_out = _matmul(mix, dx2b, "tn", "out_proj_dw", (512, 1024, t), out_dtype=BF16)
    sent_up_out, token = send_early(
        [d_w_up_t.reshape(N_DEV, 2 * D_FF // N_DEV, D_MODEL), d_w_out.reshape(N_DEV, D_MODEL // N_DEV, D_MODEL)],
        d_w_out, "send_dw_up_out")
    dp, d_sg_norm, d_sg_w, d_sg_b_t = _sg_backward(p, sg_norm_g + token[0:1, 0:1], sg_w3, sg_b_t, dmix)
    dq, dk, dv, dbeta4, dg4, dp, d_dn_norm = _dn_backward(q, k, v, beta4, g4, p, dn_out_norm_g, s_all, a_inv_all, dmix, dp)
    dc_dn, d_dn_conv, dp, d_a_log4, d_dt_bias4 = _dn_prep_bwd(p, dn_conv_full, a_log4, dt_bias4, dq, dk, dv, dbeta4, dg4, dp)
    small_grads = dict(
        attn_norm_g=jnp.zeros_like(attn_norm_g), dn_conv_w=d_dn_conv, dn_a_log=d_a_log4[:, :N_HEADS],
        dn_dt_bias=d_dt_bias4[:, :N_HEADS], dn_out_norm_g=d_dn_norm, sg_norm_g=d_sg_norm, sg_w=d_sg_w,
        sg_b=d_sg_b_t[:, :SG_GROUPS].T, ffn_norm_g=d_g2, ffn_conv_w=d_ffn_conv, ffn_conv_b=d_ffn_conv_b, final_norm_g=d_gf)
    sent_small, token = send_small(small_grads, loss_lanes, d_dn_conv)
    dp = _conv_bwd_input(dc_dn, dn_conv_full + token[0:1, 0:1], "dn_conv_dx", out_cols=PROJ_PAD, into=dp)
    d_w_in_t = _matmul(dp, h1, "tn", "in_proj_dw", (PROJ_PAD // 5, 1024, t), out_dtype=BF16)
    sent_in, token = send_early([d_w_in_t[:N_DEV * n_in].reshape(N_DEV, n_in, D_MODEL)], d_w_in_t, "send_dw_in")
    update_early(sent_down, sent_up_out, sent_small, token)
    fn, outs = _rmsnorm_bwd_rows(t, D_MODEL)
    grad_x, _, d_g1 = _matmul_rows(dp, w_in_t, "nn", "in_proj_dx_norm", 512,
                                   [(x2d, "rows"), (rstd1, "rows"), (g1 + token[0:1, 0:1], "whole"), (dx2, "rows")], outs, fn)

    return grad_x, d_g1, sent_in
```

```python
import math

import jax
import jax.numpy as jnp
from jax import lax
from jax.experimental import pallas as pl
from jax.experimental.pallas import tpu as pltpu

F32 = jnp.float32
BF16 = jnp.bfloat16
HI = lax.Precision.HIGHEST

D_MODEL = 1024
DN_WIDTH = 512
HEAD_DIM = 128
N_HEADS = 4
SG_WIDTH = 512
SG_GROUPS = 4
SG_DIM = 128
SG_BLOCK = 128
D_FF = 2816
CHUNK = 64
CONV_K = 4
FFN_CONV = 3
EPS = 1e-6
PROJ_MAIN = 3072
PROJ_PAD = 3200
GELU_C = math.sqrt(2.0 / math.pi)
N_DEV = 8
LANES = 128
SUBLANES = 8
HALO = SUBLANES
VMEM_LIMIT = 48 * 1024 * 1024

ADAM_LR = 0.001
ADAM_B1 = 0.9
ADAM_B2 = 0.999
ADAM_EPS = 1e-08
ADAM_WD = 0.01
ADAM_STEP = 10

MESH_ID = pl.DeviceIdType.MESH


def _pcall(body, **kw):
    return pl.pallas_call(body, **kw)


def _params(*sem):
    return pltpu.CompilerParams(dimension_semantics=sem, vmem_limit_bytes=VMEM_LIMIT)


def _pick(n, cap):
    best = None
    for t in range(LANES, cap + 1, LANES):
        if n % t == 0:
            best = t
    return best if best else n


FAST, EXACT = "bf16 operands, one pass", "f32 operands, six bf16 passes"


def dot_f32(a, b, dims, tier):
    if tier == FAST:
        return lax.dot_general(a.astype(BF16), b.astype(BF16), dims, preferred_element_type=F32)
    return lax.dot_general(a, b, dims, precision=HI, preferred_element_type=F32)


def dot_nn(a, b, tier=EXACT):
    return dot_f32(a, b, (((1,), (0,)), ((), ())), tier)


def dot_nt(a, b, tier=EXACT):
    return dot_f32(a, b, (((1,), (1,)), ((), ())), tier)


def dot_tn(a, b, tier=EXACT):
    return dot_f32(a, b, (((0,), (0,)), ((), ())), tier)


def sigmoid(x):
    return 0.5 * jnp.tanh(0.5 * x) + 0.5


def silu(x):
    return x * sigmoid(x)


def silu_grad(x):
    s = sigmoid(x)
    return s * (1.0 + x * (1.0 - s))


def gelu(x):
    return 0.5 * x * (1.0 + jnp.tanh(GELU_C * (x + 0.044715 * x * x * x)))


def gelu_grad(x):
    t = jnp.tanh(GELU_C * (x + 0.044715 * x * x * x))
    return 0.5 * (1.0 + t) + 0.5 * x * (1.0 - t * t) * GELU_C * (1.0 + 3.0 * 0.044715 * x * x)


def softplus(z):
    return jnp.maximum(z, 0.0) + jnp.log(1.0 + jnp.exp(-jnp.abs(z)))


def rms_fwd(x, g):
    r = lax.rsqrt(jnp.mean(x * x, axis=-1, keepdims=True) + EPS)
    return x * r * g, r


def rms_bwd(x, r, g, dy):
    dyg = dy * g
    xr = x * r
    dx = r * (dyg - xr * jnp.mean(dyg * xr, axis=-1, keepdims=True))
    return dx, dy * xr


def l2_fwd(x):
    r = lax.rsqrt(jnp.sum(x * x, axis=-1, keepdims=True) + EPS)
    return x * r, r


def l2_bwd(x, r, dy):
    xr = x * r
    return r * (dy - xr * jnp.sum(dy * xr, axis=-1, keepdims=True))


def _tri_masks(n):
    row = lax.broadcasted_iota(jnp.int32, (n, n), 0)
    col = lax.broadcasted_iota(jnp.int32, (n, n), 1)
    return row >= col, row > col


def chunk_cumsum(g4):
    incl, _ = _tri_masks(g4.shape[0])
    return dot_nn(incl.astype(F32), g4)


STACK = N_HEADS * CHUNK
DN_FWD_CHUNKS = 8
DN_CHUNKS = 4


def _head_rows(h):
    return slice(h * CHUNK, (h + 1) * CHUNK)


def _stack_heads(x):
    return jnp.concatenate([x[:, h * HEAD_DIM:(h + 1) * HEAD_DIM] for h in range(N_HEADS)], axis=0)


def _stack_lanes(x4):
    return jnp.concatenate([x4[:, h:h + 1] for h in range(N_HEADS)], axis=0)


def _per_head(fn):
    return jnp.concatenate([fn(h) for h in range(N_HEADS)], axis=0)


def _unit_lower_inverses(l_strict, order):
    c = l_strict[0].shape[0]
    row = lax.broadcasted_iota(jnp.int32, (c, c), 0)
    col = lax.broadcasted_iota(jnp.int32, (c, c), 1)
    eye = (row == col).astype(F32)
    p = [-l for l in l_strict]
    a = [eye + n for n in p]
    for _ in range(int(math.log2(order)) - 1):
        p = [dot_nn(x, x, FAST) for x in p]
        a = [x + dot_nn(x, y, FAST) for x, y in zip(a, p)]
    return a


def dn_chunks_local(chunks, inverses=None):
    row = lax.broadcasted_iota(jnp.int32, (STACK, STACK), 0)
    col = lax.broadcasted_iota(jnp.int32, (STACK, STACK), 1)
    same = (row // CHUNK) == (col // CHUNK)
    incl = jnp.logical_and(same, row >= col)
    strict = jnp.logical_and(same, row > col)
    locs = []
    for q, k, v, beta, gc4 in chunks:
        gc_col = _stack_lanes(gc4)
        gc_row = jnp.sum(jnp.where(row == col, gc_col, 0.0), axis=0, keepdims=True)
        decay = jnp.where(incl, jnp.exp(jnp.minimum(gc_col - gc_row, 0.0)), 0.0)
        gamma = jnp.exp(gc_col)
        gc_last = jnp.concatenate([jnp.broadcast_to(gc4[CHUNK - 1:CHUNK, h:h + 1], (CHUNK, 1)) for h in range(N_HEADS)], axis=0)
        tau = jnp.exp(gc_last - gc_col)
        kb = k * beta
        locs.append(dict(decay=decay, gamma=gamma, tau=tau, cd=jnp.exp(gc_last), kb=kb, qd=q * gamma, kt=k * tau,
                         incl=incl, strict=strict))
    for loc, (q, k, v, beta, gc4) in zip(locs, chunks):
        loc["l_mat"] = jnp.where(strict, dot_nt(loc["kb"], k, FAST) * loc["decay"], 0.0)
    if inverses is None:
        inverses = _unit_lower_inverses([loc["l_mat"] for loc in locs], CHUNK)
    for loc, a_inv in zip(locs, inverses):
        loc["a_inv"] = a_inv
    for loc, (q, k, v, beta, gc4) in zip(locs, chunks):
        sol = dot_nn(loc["a_inv"], jnp.concatenate([v * beta, loc["kb"] * loc["gamma"]], axis=1), FAST)
        loc.update(sol=sol, value=sol[:, :HEAD_DIM], kcd=sol[:, HEAD_DIM:])
        loc["attn"] = jnp.where(incl, dot_nt(q, k, FAST) * loc["decay"], 0.0)
    return locs


def dn_chunk_state(loc, s):
    kcd, qd, kt, cd = loc["kcd"], loc["qd"], loc["kt"], loc["cd"]
    v_new = loc["value"] - _per_head(lambda h: dot_nn(kcd[_head_rows(h)], s[h], FAST))
    o = _per_head(lambda h: dot_nn(qd[_head_rows(h)], s[h], FAST)) + dot_nn(loc["attn"], v_new, FAST)
    s_new = [s[h] * cd[h * CHUNK:h * CHUNK + 1, :] + dot_tn(kt[_head_rows(h)], v_new[_head_rows(h)], FAST)
             for h in range(N_HEADS)]
    loc["v_new"] = v_new
    return o, s_new


def dn_chunks_bwd(items, ds_last):
    hr = _head_rows
    n = len(items)
    pre = []
    for q, k, v, beta, loc, s, do in items:
        pre.append(dict(
            dv_part=dot_tn(loc["attn"], do, FAST),
            dattn=jnp.where(loc["incl"], dot_nt(do, loc["v_new"], FAST), 0.0),
            dqd=_per_head(lambda h: dot_nt(do[hr(h)], s[h], FAST)),
            ds_part=[dot_tn(loc["qd"][hr(h)], do[hr(h)], FAST) for h in range(N_HEADS)]))
    ds_new_of, dv_new_of = [None] * n, [None] * n
    ds = ds_last
    for c in reversed(range(n)):
        loc = items[c][4]
        ds_new_of[c] = ds
        dv_new = pre[c]["dv_part"] + _per_head(lambda h: dot_nn(loc["kt"][hr(h)], ds[h], FAST))
        dv_new_of[c] = dv_new
        ds = [pre[c]["ds_part"][h] + ds[h] * loc["cd"][h * CHUNK:h * CHUNK + 1, :]
              - dot_tn(loc["kcd"][hr(h)], dv_new[hr(h)], FAST) for h in range(N_HEADS)]
    is_last = (lax.broadcasted_iota(jnp.int32, (STACK, 1), 0) % CHUNK) == CHUNK - 1
    out = []
    for c, (q, k, v, beta, loc, s, do) in enumerate(items):
        decay, gamma, tau, cd, kb = loc["decay"], loc["gamma"], loc["tau"], loc["cd"], loc["kb"]
        dv_new, ds_new, dattn, dqd = dv_new_of[c], ds_new_of[c], pre[c]["dattn"], pre[c]["dqd"]
        dkt = _per_head(lambda h: dot_nt(loc["v_new"][hr(h)], ds_new[h], FAST))
        dkcd = -_per_head(lambda h: dot_nt(dv_new[hr(h)], s[h], FAST))
        drhs = dot_tn(loc["a_inv"], jnp.concatenate([dv_new, dkcd], axis=1), FAST)
        dvb, dkbg = drhs[:, :HEAD_DIM], drhs[:, HEAD_DIM:]
        dl = jnp.where(loc["strict"], -dot_nt(drhs, loc["sol"], FAST), 0.0)
        dkk = dl * decay
        dqk = dattn * decay
        e = dl * loc["l_mat"] + dattn * loc["attn"]
        dgc = jnp.sum(e, axis=1, keepdims=True) - jnp.sum(e, axis=0, keepdims=True).T
        dkb = dot_nn(dkk, k, FAST) + dkbg * gamma
        dk = dot_tn(dkk, kb, FAST) + dot_tn(dqk, q, FAST) + dkt * tau
        dq = dot_nn(dqk, k, FAST) + dqd * gamma
        dgamma = jnp.sum(dkbg * kb, axis=1, keepdims=True) + jnp.sum(dqd * q, axis=1, keepdims=True)
        dtau_tau = jnp.sum(dkt * k, axis=1, keepdims=True) * tau
        dgc = dgc + dgamma * gamma - dtau_tau

        def last_term(h):
            dcd = jnp.sum(jnp.sum(ds_new[h] * s[h], axis=1, keepdims=True), axis=0, keepdims=True)
            total = jnp.sum(dtau_tau[hr(h)], axis=0, keepdims=True) + dcd * cd[h * CHUNK:h * CHUNK + 1, :]
            return jnp.broadcast_to(total, (CHUNK, 1))

        dgc = dgc + jnp.where(is_last, _per_head(last_term), 0.0)
        dk = dk + dkb * beta
        dbeta = jnp.sum(dkb * k, axis=1, keepdims=True) + jnp.sum(dvb * v, axis=1, keepdims=True)
        out.append((dq, dk, dvb * beta, dbeta, dgc))
    return out, ds


def _token_tile(t):
    return _pick(t, 256)


STRIP = 32


def _for_strips(n_rows, rows, fn, start=0):
    def step(r, carry):
        fn(pl.multiple_of(r * rows, rows))
        return carry

    lax.fori_loop(start, n_rows // rows, step, 0)


def _fold_rows(x):
    out = x[0:SUBLANES, :]
    for i in range(1, x.shape[0] // SUBLANES):
        out = out + x[i * SUBLANES:(i + 1) * SUBLANES, :]
    return out


def _matmul(a, b, mode, name, tiles, add=None, out_dtype=F32):
    if mode == "nn":
        (m, k), n = a.shape, b.shape[1]
    elif mode == "nt":
        (m, k), n = a.shape, b.shape[0]
    else:
        (k, m), n = a.shape, b.shape[1]
    tm, tn, tk = min(tiles[0], m), min(tiles[1], n), min(tiles[2], k)
    assert m % tm == 0 and n % tn == 0 and k % tk == 0, (name, m, n, k, tiles)
    nk = k // tk
    dims = {"nn": (((1,), (0,)), ((), ())), "nt": (((1,), (1,)), ((), ())), "tn": (((0,), (0,)), ((), ()))}[mode]

    def finish(res, add_ref, o_ref):
        if add_ref is not None:
            res = res + add_ref[...]
        o_ref[...] = res.astype(o_ref.dtype)

    def body(*refs):
        a_ref, b_ref = refs[0], refs[1]
        add_ref = refs[2] if add is not None else None
        o_ref = refs[3] if add is not None else refs[2]
        part = lax.dot_general(a_ref[...], b_ref[...], dims, preferred_element_type=F32)
        if nk == 1:
            finish(part, add_ref, o_ref)
            return
        acc_ref = refs[-1]
        kk = pl.program_id(2)

        @pl.when(kk == 0)
        def _():
            acc_ref[...] = part

        @pl.when(kk > 0)
        def _():
            acc_ref[...] += part

        @pl.when(kk == nk - 1)
        def _():
            finish(acc_ref[...], add_ref, o_ref)

    a_spec = pl.BlockSpec((tk, tm), lambda j, i, kk: (kk, i)) if mode == "tn" else pl.BlockSpec((tm, tk), lambda j, i, kk: (i, kk))
    b_spec = pl.BlockSpec((tn, tk), lambda j, i, kk: (j, kk)) if mode == "nt" else pl.BlockSpec((tk, tn), lambda j, i, kk: (kk, j))
    o_spec = pl.BlockSpec((tm, tn), lambda j, i, kk: (i, j))
    in_specs = [a_spec, b_spec] + ([o_spec] if add is not None else [])
    args = (a, b) + ((add,) if add is not None else ())
    return _pcall(
        body, grid=(n // tn, m // tm, nk), in_specs=in_specs, out_specs=o_spec,
        out_shape=jax.ShapeDtypeStruct((m, n), out_dtype),
        scratch_shapes=[pltpu.VMEM((tm, tn), F32)] if nk > 1 else [],
        compiler_params=_params("parallel", "parallel", "arbitrary"), name=name)(*args)


def _matmul_rows(a, b, mode, name, tm, extra, outs, fn):
    m, k = a.shape
    n = b.shape[1] if mode == "nn" else b.shape[0]
    tm = min(tm, m)
    dims = (((1,), (0,)), ((), ())) if mode == "nn" else (((1,), (1,)), ((), ()))

    def spec(shape, kind):
        if kind == "rows":
            return pl.BlockSpec((tm, shape[1]), lambda i: (i, 0))
        return pl.BlockSpec(shape, lambda i: (0,) * len(shape))

    def body(a_ref, b_ref, *refs):
        rows = lax.dot_general(a_ref[...], b_ref[...], dims, preferred_element_type=F32)
        fn(rows, pl.program_id(0) == 0, *refs)

    return _pcall(
        body, grid=(m // tm,),
        in_specs=[pl.BlockSpec((tm, k), lambda i: (i, 0)), pl.BlockSpec(b.shape, lambda i: (0, 0))]
        + [spec(x.shape, kind) for x, kind in extra],
        out_specs=[spec(shape, kind) for shape, _, kind in outs],
        out_shape=[jax.ShapeDtypeStruct(shape, dtype) for shape, dtype, _ in outs],
        compiler_params=_params("arbitrary"), name=name)(a, b, *[x for x, _ in extra])


def _rmsnorm_matmul(x, g, b_t, name, tm):
    t, d = x.shape
    n = b_t.shape[0]
    tm = min(tm, t)

    def body(x_ref, g_ref, b_ref, o_ref, h_ref, r_ref):
        y, r = rms_fwd(x_ref[...], g_ref[...])
        h = y.astype(BF16)
        h_ref[...] = h
        r_ref[...] = r
        o_ref[...] = lax.dot_general(h, b_ref[...], (((1,), (1,)), ((), ())), preferred_element_type=F32)

    rows = lambda w: pl.BlockSpec((tm, w), lambda i: (i, 0))
    return _pcall(
        body, grid=(t // tm,),
        in_specs=[rows(d), pl.BlockSpec((1, d), lambda i: (0, 0)), pl.BlockSpec((n, d), lambda i: (0, 0))],
        out_specs=[rows(n), rows(d), rows(1)],
        out_shape=[jax.ShapeDtypeStruct((t, n), F32), jax.ShapeDtypeStruct((t, d), BF16), jax.ShapeDtypeStruct((t, 1), F32)],
        compiler_params=_params("parallel"), name=name)(x, g, b_t)


def _rmsnorm_bwd_rows(t, d):
    def fn(dh, first, x_ref, r_ref, g_ref, dres_ref, dx_ref, dxb_ref, dg_ref):
        dx, dg_rows = rms_bwd(x_ref[...], r_ref[...], g_ref[...], dh)
        dx = dx + dres_ref[...]
        dx_ref[...] = dx
        dxb_ref[...] = dx.astype(BF16)

        @pl.when(first)
        def _():
            dg_ref[...] = jnp.zeros_like(dg_ref)

        dg_ref[...] += jnp.sum(dg_rows, axis=0, keepdims=True)

    return fn, [((t, d), F32, "rows"), ((t, d), BF16, "rows"), ((1, d), F32, "whole")]


def _final_loss_rows(t, d):
    def fn(rows, first, res_ref, t_ref, g_ref, loss_ref, dx_ref, dxb_ref, dg_ref):
        @pl.when(first)
        def _():
            loss_ref[...] = jnp.zeros_like(loss_ref)
            dg_ref[...] = jnp.zeros_like(dg_ref)

        x = rows + res_ref[...]
        y, r = rms_fwd(x, g_ref[...])
        err = y - t_ref[...]
        loss_ref[...] += 0.5 * jnp.sum(jnp.mean(err * err, axis=-1, keepdims=True), axis=0, keepdims=True)
        dx, dg_rows = rms_bwd(x, r, g_ref[...], err * (1.0 / d))
        dx_ref[...] = dx
        dxb_ref[...] = dx.astype(BF16)
        dg_ref[...] += jnp.sum(dg_rows, axis=0, keepdims=True)

    return fn, [((1, LANES), F32, "whole"), ((t, d), F32, "rows"), ((t, d), BF16, "rows"), ((1, d), F32, "whole")]


def _prev_halo_spec(tm, width, col_block):
    return pl.BlockSpec((HALO, width), lambda i: (jnp.maximum(i * (tm // HALO) - 1, 0), col_block))


def _history(tile_ref, halo_ref, first, row0, cols):
    if isinstance(row0, int) and row0 == 0:
        return jnp.concatenate([jnp.where(first, 0.0, halo_ref[:, cols]), tile_ref[0:STRIP, cols]], axis=0)
    return tile_ref[pl.ds(pl.multiple_of(row0 - HALO, HALO), STRIP + HALO), cols]


def _first_then_strips(n_rows, fn):
    fn(0)
    _for_strips(n_rows, STRIP, fn, start=1)


def _delays(ext, taps):
    return [ext[HALO:, :]] + [pltpu.roll(ext, j, 0)[HALO:, :] for j in range(1, taps)]


def _causal_conv(delayed, w):
    taps = len(delayed)
    out = delayed[0] * w[taps - 1:taps, :]
    for j in range(1, taps):
        out = out + delayed[j] * w[taps - 1 - j:taps - j, :]
    return out


def _advanced_conv(buf_ref, row0, cols, w):
    taps = w.shape[0]
    ext = buf_ref[pl.ds(row0, STRIP + HALO), cols]
    out = ext[:STRIP, :] * w[taps - 1:taps, :]
    for j in range(1, taps):
        out = out + pltpu.roll(ext, STRIP + HALO - j, 0)[:STRIP, :] * w[taps - 1 - j:taps - j, :]
    return out


def _dn_prep(p, conv_w, a_log4, dt_bias4):
    t = p.shape[0]
    tm = _token_tile(t)
    w3 = 3 * DN_WIDTH

    def body(x_ref, halo_ref, pbd_ref, w_ref, alog_ref, dtb_ref, q_ref, k_ref, v_ref, beta_ref, g_ref):
        first = pl.program_id(0) == 0

        def strip(row0):
            rows = pl.ds(row0, STRIP)
            for h in range(N_HEADS):
                sl = slice(h * HEAD_DIM, (h + 1) * HEAD_DIM)
                for part, out_ref in ((0, q_ref), (1, k_ref), (2, v_ref)):
                    cols = slice(part * DN_WIDTH + h * HEAD_DIM, part * DN_WIDTH + (h + 1) * HEAD_DIM)
                    y = silu(_causal_conv(_delays(_history(x_ref, halo_ref, first, row0, cols), CONV_K), w_ref[:, cols]))
                    if part == 0:
                        y = l2_fwd(y)[0] * (HEAD_DIM ** -0.5)
                    elif part == 1:
                        y = l2_fwd(y)[0]
                    out_ref[rows, sl] = y
            head = lax.broadcasted_iota(jnp.int32, (STRIP, LANES), 1) < N_HEADS
            pbd = pbd_ref[rows, :]
            beta_ref[rows, :] = jnp.where(head, sigmoid(pbd), 0.0)
            a_raw = pltpu.roll(pbd, LANES - N_HEADS, 1)
            g_ref[rows, :] = jnp.where(head, -jnp.exp(alog_ref[...]) * softplus(a_raw + dtb_ref[...]), 0.0)

        _first_then_strips(tm, strip)

    tok = lambda w, cb: pl.BlockSpec((tm, w), lambda i: (i, cb))
    full = lambda a: pl.BlockSpec(a.shape, lambda i: (0, 0))
    return _pcall(
        body, grid=(t // tm,),
        in_specs=[tok(w3, 0), _prev_halo_spec(tm, w3, 0), tok(LANES, PROJ_MAIN // LANES),
                  full(conv_w), full(a_log4), full(dt_bias4)],
        out_specs=[tok(DN_WIDTH, 0)] * 3 + [tok(LANES, 0)] * 2,
        out_shape=[jax.ShapeDtypeStruct((t, DN_WIDTH), F32)] * 3 + [jax.ShapeDtypeStruct((t, LANES), F32)] * 2,
        compiler_params=_params("parallel"), name="dn_prep")(p, p, p, conv_w, a_log4, dt_bias4)


def _dn_prep_bwd(p, conv_w, a_log4, dt_bias4, dq, dk, dv, dbeta4, dg4, dp_buf):
    t = p.shape[0]
    tm = _token_tile(t)
    w3 = 3 * DN_WIDTH

    def body(x_ref, halo_ref, pbd_ref, w_ref, alog_ref, dtb_ref, dq_ref, dk_ref, dv_ref, dbeta_ref, dg_ref, _,
             dc_ref, dw_ref, dpbd_ref, dalog_ref, ddtb_ref, dw_acc, lane_acc):
        first = pl.program_id(0) == 0
        dw_acc[...] = jnp.zeros_like(dw_acc)
        lane_acc[...] = jnp.zeros_like(lane_acc)

        def strip(row0):
            rows = pl.ds(row0, STRIP)
            for h in range(N_HEADS):
                sl = slice(h * HEAD_DIM, (h + 1) * HEAD_DIM)
                for part, dy_ref in ((0, dq_ref), (1, dk_ref), (2, dv_ref)):
                    cols = slice(part * DN_WIDTH + h * HEAD_DIM, part * DN_WIDTH + (h + 1) * HEAD_DIM)
                    delayed = _delays(_history(x_ref, halo_ref, first, row0, cols), CONV_K)
                    c = _causal_conv(delayed, w_ref[:, cols])
                    dy = dy_ref[rows, sl]
                    if part < 2:
                        y = silu(c)
                        _, r = l2_fwd(y)
                        dy = l2_bwd(y, r, dy * (HEAD_DIM ** -0.5) if part == 0 else dy)
                    dc = dy * silu_grad(c)
                    dc_ref[rows, cols] = dc
                    for j in range(CONV_K):
                        k = CONV_K - 1 - j
                        dw_acc[k * SUBLANES:(k + 1) * SUBLANES, cols] += _fold_rows(dc * delayed[j])
            head = lax.broadcasted_iota(jnp.int32, (STRIP, LANES), 1) < N_HEADS
            pbd = pbd_ref[rows, :]
            beta = sigmoid(pbd)
            dpb = jnp.where(head, dbeta_ref[rows, :] * beta * (1.0 - beta), 0.0)
            z = pltpu.roll(pbd, LANES - N_HEADS, 1) + dtb_ref[...]
            neg_rate = -jnp.exp(alog_ref[...])
            dg = dg_ref[rows, :]
            dpa = jnp.where(head, dg * neg_rate * sigmoid(z), 0.0)
            dpbd_ref[rows, :] = (dpb + pltpu.roll(dpa, N_HEADS, 1)).astype(BF16)
            g = jnp.where(head, neg_rate * softplus(z), 0.0)
            lane_acc[0:SUBLANES, :] += _fold_rows(dg * g)
            lane_acc[SUBLANES:, :] += _fold_rows(dpa)

        _first_then_strips(tm, strip)

        @pl.when(first)
        def _():
            dw_ref[...] = jnp.zeros_like(dw_ref)
            dalog_ref[...] = jnp.zeros_like(dalog_ref)
            ddtb_ref[...] = jnp.zeros_like(ddtb_ref)

        for k in range(CONV_K):
            dw_ref[k:k + 1, :] += jnp.sum(dw_acc[k * SUBLANES:(k + 1) * SUBLANES, :], axis=0, keepdims=True)
        dalog_ref[...] += jnp.sum(lane_acc[0:SUBLANES, :], axis=0, keepdims=True)
        ddtb_ref[...] += jnp.sum(lane_acc[SUBLANES:, :], axis=0, keepdims=True)

    tok = lambda w, cb: pl.BlockSpec((tm, w), lambda i: (i, cb))
    full = lambda shape: pl.BlockSpec(shape, lambda i: (0, 0))
    return _pcall(
        body, grid=(t // tm,),
        in_specs=[tok(w3, 0), _prev_halo_spec(tm, w3, 0), tok(LANES, PROJ_MAIN // LANES),
                  full(conv_w.shape), full(a_log4.shape), full(dt_bias4.shape)] + [tok(DN_WIDTH, 0)] * 3 + [tok(LANES, 0)] * 2
        + [pl.BlockSpec(memory_space=pl.ANY)],
        out_specs=[tok(w3, 0), full((CONV_K, w3)), tok(LANES, PROJ_MAIN // LANES), full((1, LANES)), full((1, LANES))],
        out_shape=[jax.ShapeDtypeStruct((t, w3), F32), jax.ShapeDtypeStruct((CONV_K, w3), F32),
                   jax.ShapeDtypeStruct(dp_buf.shape, dp_buf.dtype),
                   jax.ShapeDtypeStruct((1, LANES), F32), jax.ShapeDtypeStruct((1, LANES), F32)],
        input_output_aliases={11: 2},
        scratch_shapes=[pltpu.VMEM((CONV_K * SUBLANES, w3), F32), pltpu.VMEM((2 * SUBLANES, LANES), F32)],
        compiler_params=_params("arbitrary"), name="dn_prep_bwd")(p, p, p, conv_w, a_log4, dt_bias4, dq, dk, dv, dbeta4, dg4, dp_buf)


def _conv_bwd_input(dc, w, name, out_cols=None, col_block=0, into=None):
    t, c = dc.shape
    taps = w.shape[0]
    tm = _token_tile(t)
    ct = _pick(c, 1536)
    n_tok = t // tm
    out_cols = c if out_cols is None else out_cols

    def body(dc_ref, next_ref, w_ref, *rest):
        dx_ref, buf_ref = rest[-2], rest[-1]
        buf_ref[0:tm, :] = dc_ref[...]
        buf_ref[tm:, :] = jnp.where(pl.program_id(0) == n_tok - 1, 0.0, next_ref[...])

        def strip(row0):
            for c0 in range(0, ct, LANES):
                cols = slice(c0, c0 + LANES)
                dx_ref[pl.ds(row0, STRIP), cols] = _advanced_conv(buf_ref, row0, cols, w_ref[:, cols]).astype(BF16)

        _for_strips(tm, STRIP, strip)

    in_specs = [pl.BlockSpec((tm, ct), lambda i, j: (i, j)),
                pl.BlockSpec((HALO, ct), lambda i, j: (jnp.minimum((i + 1) * (tm // HALO), t // HALO - 1), j)),
                pl.BlockSpec((taps, ct), lambda i, j: (0, j))]
    args = (dc, dc, w)
    aliases = {}
    if into is not None:
        in_specs.append(pl.BlockSpec(memory_space=pl.ANY))
        args += (into,)
        aliases = {3: 0}
    return _pcall(
        body, grid=(n_tok, c // ct), in_specs=in_specs,
        out_specs=pl.BlockSpec((tm, ct), lambda i, j: (i, j + col_block)),
        out_shape=jax.ShapeDtypeStruct((t, out_cols), BF16), input_output_aliases=aliases,
        scratch_shapes=[pltpu.VMEM((tm + HALO, ct), F32)],
        compiler_params=_params("parallel", "parallel"), name=name)(*args)


def _dn_forward(q, k, v, beta4, g4, p, norm_g):
    t = q.shape[0]
    n = t // CHUNK
    nc = DN_FWD_CHUNKS
    rows_per_step = nc * CHUNK

    def body(q_ref, k_ref, v_ref, b_ref, g_ref, gate_ref, ng_ref, mix_ref, s_all_ref, ainv_ref, s_ref):
        @pl.when(pl.program_id(0) == 0)
        def _():
            s_ref[...] = jnp.zeros_like(s_ref)

        chunks = []
        for c in range(nc):
            rows = slice(c * CHUNK, (c + 1) * CHUNK)
            chunks.append((_stack_heads(q_ref[rows, :]), _stack_heads(k_ref[rows, :]), _stack_heads(v_ref[rows, :]),
                           _stack_lanes(b_ref[rows, :]), chunk_cumsum(g_ref[rows, :])))
        locs = dn_chunks_local(chunks)
        s = [s_ref[h] for h in range(N_HEADS)]
        for c in range(nc):
            rows = slice(c * CHUNK, (c + 1) * CHUNK)
            ainv_ref[c] = locs[c]["a_inv"].astype(BF16)
            for h in range(N_HEADS):
                s_all_ref[c, h] = s[h]
            o, s = dn_chunk_state(locs[c], s)
            o_n, _ = rms_fwd(o, ng_ref[...])
            for h in range(N_HEADS):
                sl = slice(h * HEAD_DIM, (h + 1) * HEAD_DIM)
                mix_ref[rows, sl] = (o_n[_head_rows(h)] * silu(gate_ref[rows, sl])).astype(BF16)
        for h in range(N_HEADS):
            s_ref[h] = s[h]

    ch = lambda w, cb: pl.BlockSpec((rows_per_step, w), lambda i: (i, cb))
    return _pcall(
        body, grid=(n // nc,),
        in_specs=[ch(DN_WIDTH, 0)] * 3 + [ch(LANES, 0)] * 2 + [ch(DN_WIDTH, 3), pl.BlockSpec((1, HEAD_DIM), lambda i: (0, 0))],
        out_specs=[ch(DN_WIDTH, 0), pl.BlockSpec((nc, N_HEADS, HEAD_DIM, HEAD_DIM), lambda i: (i, 0, 0, 0)),
                   pl.BlockSpec((nc, STACK, STACK), lambda i: (i, 0, 0))],
        out_shape=[jax.ShapeDtypeStruct((t, DN_WIDTH + SG_WIDTH), BF16), jax.ShapeDtypeStruct((n, N_HEADS, HEAD_DIM, HEAD_DIM), F32),
                   jax.ShapeDtypeStruct((n, STACK, STACK), BF16)],
        scratch_shapes=[pltpu.VMEM((N_HEADS, HEAD_DIM, HEAD_DIM), F32)],
        compiler_params=_params("arbitrary"), name="dn_forward")(q, k, v, beta4, g4, p, norm_g)


def _dn_backward(q, k, v, beta4, g4, p, norm_g, s_all, a_inv_all, dmix, dp_buf):
    t = q.shape[0]
    n = t // CHUNK
    steps = n // DN_CHUNKS
    rows_per_step = DN_CHUNKS * CHUNK

    def body(q_ref, k_ref, v_ref, b_ref, g_ref, gate_ref, ng_ref, s_in_ref, ainv_ref, dmix_ref, _,
             dq_ref, dk_ref, dv_ref, db_ref, dg_ref, dgate_ref, dng_ref, ds_ref):
        @pl.when(pl.program_id(0) == 0)
        def _():
            ds_ref[...] = jnp.zeros_like(ds_ref)
            dng_ref[...] = jnp.zeros_like(dng_ref)

        chunks = []
        for c in range(DN_CHUNKS):
            rows = slice(c * CHUNK, (c + 1) * CHUNK)
            chunks.append((_stack_heads(q_ref[rows, :]), _stack_heads(k_ref[rows, :]), _stack_heads(v_ref[rows, :]),
                           _stack_lanes(b_ref[rows, :]), chunk_cumsum(g_ref[rows, :])))
        items = []
        for c, loc in enumerate(dn_chunks_local(chunks, [ainv_ref[c] for c in range(DN_CHUNKS)])):
            rows = slice(c * CHUNK, (c + 1) * CHUNK)
            s = [s_in_ref[c, h] for h in range(N_HEADS)]
            o, _ = dn_chunk_state(loc, s)
            o_n, r = rms_fwd(o, ng_ref[...])
            gate = _stack_heads(gate_ref[rows, :])
            dmx = _stack_heads(dmix_ref[rows, :])
            dgate = dmx * o_n * silu_grad(gate)
            do, dng_rows = rms_bwd(o, r, ng_ref[...], dmx * silu(gate))
            dng_ref[...] += jnp.sum(dng_rows, axis=0, keepdims=True)
            for h in range(N_HEADS):
                dgate_ref[rows, h * HEAD_DIM:(h + 1) * HEAD_DIM] = dgate[_head_rows(h)].astype(BF16)
            items.append((*chunks[c][:4], loc, s, do))
        grads, ds = dn_chunks_bwd(items, [ds_ref[h] for h in range(N_HEADS)])
        lane = lax.broadcasted_iota(jnp.int32, (CHUNK, LANES), 1)
        _, strict = _tri_masks(CHUNK)
        for c in range(DN_CHUNKS):
            rows = slice(c * CHUNK, (c + 1) * CHUNK)
            dq, dk, dv, dbeta, dgc = grads[c]
            db4 = jnp.zeros((CHUNK, LANES), F32)
            dgc4 = jnp.zeros((CHUNK, LANES), F32)
            for h in range(N_HEADS):
                sl = slice(h * HEAD_DIM, (h + 1) * HEAD_DIM)
                head_rows = _head_rows(h)
                dq_ref[rows, sl] = dq[head_rows]
                dk_ref[rows, sl] = dk[head_rows]
                dv_ref[rows, sl] = dv[head_rows]
                db4 = jnp.where(lane == h, dbeta[head_rows], db4)
                dgc4 = jnp.where(lane == h, dgc[head_rows], dgc4)
            db_ref[rows, :] = db4
            dg_ref[rows, :] = dot_nn(jnp.logical_not(strict).astype(F32), dgc4)
        for h in range(N_HEADS):
            ds_ref[h] = ds[h]

    rev = lambda w, cb: pl.BlockSpec((rows_per_step, w), lambda i: (steps - 1 - i, cb))
    return _pcall(
        body, grid=(steps,),
        in_specs=[rev(DN_WIDTH, 0)] * 3 + [rev(LANES, 0)] * 2 + [rev(DN_WIDTH, 3), pl.BlockSpec((1, HEAD_DIM), lambda i: (0, 0)),
                  pl.BlockSpec((DN_CHUNKS, N_HEADS, HEAD_DIM, HEAD_DIM), lambda i: (steps - 1 - i, 0, 0, 0)),
                  pl.BlockSpec((DN_CHUNKS, STACK, STACK), lambda i: (steps - 1 - i, 0, 0)), rev(DN_WIDTH, 0),
                  pl.BlockSpec(memory_space=pl.ANY)],
        out_specs=[rev(DN_WIDTH, 0)] * 3 + [rev(LANES, 0)] * 2 + [rev(DN_WIDTH, 3), pl.BlockSpec((1, HEAD_DIM), lambda i: (0, 0))],
        out_shape=[jax.ShapeDtypeStruct((t, DN_WIDTH), F32)] * 3 + [jax.ShapeDtypeStruct((t, LANES), F32)] * 2
        + [jax.ShapeDtypeStruct(dp_buf.shape, dp_buf.dtype), jax.ShapeDtypeStruct((1, HEAD_DIM), F32)],
        input_output_aliases={10: 5},
        scratch_shapes=[pltpu.VMEM((N_HEADS, HEAD_DIM, HEAD_DIM), F32)],
        compiler_params=_params("arbitrary"), name="dn_backward")(q, k, v, beta4, g4, p, norm_g, s_all, a_inv_all, dmix, dp_buf)


def _sg_mask():
    row = lax.broadcasted_iota(jnp.int32, (SG_BLOCK, SG_BLOCK), 0)
    col = lax.broadcasted_iota(jnp.int32, (SG_BLOCK, SG_BLOCK), 1)
    return (col // CHUNK) <= (row // CHUNK)


def _sg_forward(p, norm_g, w_s, b_t, mix_buf):
    t = p.shape[0]

    def body(u_ref, v_ref, ng_ref, w_ref, b_ref, _, o_ref):
        mask = _sg_mask()
        for g in range(SG_GROUPS):
            sl = slice(g * SG_DIM, (g + 1) * SG_DIM)
            vn, _ = rms_fwd(gelu(v_ref[:, sl]), ng_ref[:, sl])
            s = dot_nn(jnp.where(mask, w_ref[g], 0.0), vn, FAST) + b_ref[:, g:g + 1]
            o_ref[:, sl] = (gelu(u_ref[:, sl]) * s).astype(BF16)

    blk = lambda cb: pl.BlockSpec((SG_BLOCK, SG_WIDTH), lambda i: (i, cb))
    return _pcall(
        body, grid=(t // SG_BLOCK,),
        in_specs=[blk(4), blk(5), pl.BlockSpec((1, SG_WIDTH), lambda i: (0, 0)),
                  pl.BlockSpec((SG_GROUPS, SG_BLOCK, SG_BLOCK), lambda i: (0, 0, 0)), pl.BlockSpec((SG_BLOCK, SG_GROUPS), lambda i: (0, 0)),
                  pl.BlockSpec(memory_space=pl.ANY)],
        out_specs=blk(1), out_shape=jax.ShapeDtypeStruct(mix_buf.shape, mix_buf.dtype), input_output_aliases={5: 0},
        compiler_params=_params("parallel"), name="sg_forward")(p, p, norm_g, w_s, b_t, mix_buf)


def _sg_backward(p, norm_g, w_s, b_t, dmix):
    t = p.shape[0]

    def body(u_ref, v_ref, ng_ref, w_ref, b_ref, do_ref, duv_ref, dng_ref, dw_ref, db_ref):
        @pl.when(pl.program_id(0) == 0)
        def _():
            dng_ref[...] = jnp.zeros_like(dng_ref)
            dw_ref[...] = jnp.zeros_like(dw_ref)
            db_ref[...] = jnp.zeros_like(db_ref)

        mask = _sg_mask()
        lane = lax.broadcasted_iota(jnp.int32, (SG_BLOCK, LANES), 1)
        db = jnp.zeros((SG_BLOCK, LANES), F32)
        for g in range(SG_GROUPS):
            sl = slice(g * SG_DIM, (g + 1) * SG_DIM)
            u_raw, v_raw, do = u_ref[:, sl], v_ref[:, sl], do_ref[:, sl]
            vg = gelu(v_raw)
            vn, r = rms_fwd(vg, ng_ref[:, sl])
            w_m = jnp.where(mask, w_ref[g], 0.0)
            s = dot_nn(w_m, vn, FAST) + b_ref[:, g:g + 1]
            duv_ref[:, sl] = (do * s * gelu_grad(u_raw)).astype(BF16)
            ds = do * gelu(u_raw)
            db = jnp.where(lane == g, jnp.sum(ds, axis=1, keepdims=True), db)
            dw_ref[g] += jnp.where(mask, dot_nt(ds, vn, FAST), 0.0)
            dvg, dng_rows = rms_bwd(vg, r, ng_ref[:, sl], dot_tn(w_m, ds, FAST))
            dng_ref[:, sl] += jnp.sum(dng_rows, axis=0, keepdims=True)
            duv_ref[:, SG_WIDTH + g * SG_DIM:SG_WIDTH + (g + 1) * SG_DIM] = (dvg * gelu_grad(v_raw)).astype(BF16)
        db_ref[...] += db

    blk = lambda cb: pl.BlockSpec((SG_BLOCK, SG_WIDTH), lambda i: (i, cb))
    const2 = lambda shape: pl.BlockSpec(shape, lambda i: (0, 0))
    w_spec = pl.BlockSpec((SG_GROUPS, SG_BLOCK, SG_BLOCK), lambda i: (0, 0, 0))
    return _pcall(
        body, grid=(t // SG_BLOCK,),
        in_specs=[blk(4), blk(5), const2((1, SG_WIDTH)), w_spec, const2((SG_BLOCK, SG_GROUPS)), blk(1)],
        out_specs=[pl.BlockSpec((SG_BLOCK, 2 * SG_WIDTH), lambda i: (i, 2)), const2((1, SG_WIDTH)), w_spec,
                   const2((SG_BLOCK, LANES))],
        out_shape=[jax.ShapeDtypeStruct((t, PROJ_PAD), BF16), jax.ShapeDtypeStruct((1, SG_WIDTH), F32),
                   jax.ShapeDtypeStruct((SG_GROUPS, SG_BLOCK, SG_BLOCK), F32), jax.ShapeDtypeStruct((SG_BLOCK, LANES), F32)],
        compiler_params=_params("arbitrary"), name="sg_backward")(p, p, norm_g, w_s, b_t, dmix)


FFN_COLS = 256


def _norm_up_ffn(x, g, w_up_t, conv_w, conv_b):
    t, d = x.shape
    tm = min(t, 256)
    blocks = D_FF // FFN_COLS
    nt = (((1,), (1,)), ((), ()))

    def body(x_ref, g_ref, w_ref, cw_ref, cb_ref, up_ref, act_ref, h_ref, r_ref, tail_ref, prev_ref):
        @pl.when(pl.program_id(0) == 0)
        def _():
            tail_ref[...] = jnp.zeros_like(tail_ref)

        y, r = rms_fwd(x_ref[...], g_ref[...])
        h = y.astype(BF16)
        h_ref[...] = h
        r_ref[...] = r

        def project(blk):
            out = []
            for half in range(2):
                cols = slice(half * D_FF + blk * FFN_COLS, half * D_FF + (blk + 1) * FFN_COLS)
                u = lax.dot_general(h, w_ref[cols, :], nt, preferred_element_type=F32)
                up_ref[:, cols] = u
                prev_ref[:, cols] = tail_ref[:, cols]
                tail_ref[:, cols] = u[tm - HALO:, :]
                out.append(cols)
            return out

        def history(row0, cols):
            if row0 == 0:
                return jnp.concatenate([prev_ref[:, cols], up_ref[0:STRIP, cols]], axis=0)
            return up_ref[row0 - HALO:row0 + STRIP, cols]

        def activate(blk, g_cols, v_cols):
            for row0 in range(0, tm, STRIP):
                for c0 in range(0, FFN_COLS, LANES):
                    gc = slice(g_cols.start + c0, g_cols.start + c0 + LANES)
                    vc = slice(v_cols.start + c0, v_cols.start + c0 + LANES)
                    cg = _causal_conv(_delays(history(row0, gc), FFN_CONV), cw_ref[:, gc]) + cb_ref[:, gc]
                    cv = _causal_conv(_delays(history(row0, vc), FFN_CONV), cw_ref[:, vc]) + cb_ref[:, vc]
                    act_ref[row0:row0 + STRIP, blk * FFN_COLS + c0:blk * FFN_COLS + c0 + LANES] = (silu(cg) * cv).astype(BF16)

        pending = None
        for blk in range(blocks):
            cols = project(blk)
            if pending is not None:
                activate(*pending)
            pending = (blk, *cols)
        activate(*pending)

    rows = lambda w: pl.BlockSpec((tm, w), lambda i: (i, 0))
    whole = lambda a: pl.BlockSpec(a.shape, lambda i: (0, 0))
    return _pcall(
        body, grid=(t // tm,),
        in_specs=[rows(d), whole(g), whole(w_up_t), whole(conv_w), whole(conv_b)],
        out_specs=[rows(2 * D_FF), rows(D_FF), rows(d), rows(1)],
        out_shape=[jax.ShapeDtypeStruct((t, 2 * D_FF), F32), jax.ShapeDtypeStruct((t, D_FF), BF16),
                   jax.ShapeDtypeStruct((t, d), BF16), jax.ShapeDtypeStruct((t, 1), F32)],
        scratch_shapes=[pltpu.VMEM((HALO, 2 * D_FF), F32), pltpu.VMEM((HALO, 2 * D_FF), F32)],
        compiler_params=_params("arbitrary"), name="norm_up_ffn")(x, g, w_up_t, conv_w, conv_b)


def _ffn_bwd(up, conv_w, conv_b, dact):
    t = up.shape[0]
    tm = _pick(t, 128)
    n_tok = t // tm
    width = 2 * D_FF

    def dconv(delayed_g, delayed_v, da, wg, wv, bg, bv):
        cg = _causal_conv(delayed_g, wg) + bg
        cv = _causal_conv(delayed_v, wv) + bv
        s = sigmoid(cg)
        return da * cv * (s * (1.0 + cg * (1.0 - s))), da * (cg * s)

    def body(up_ref, prev_ref, next_ref, da_ref, dan_ref, w_ref, b_ref, dup_ref, dw_ref, db_ref, dc_ref, dw_acc, db_acc):
        first = pl.program_id(0) == 0
        last = pl.program_id(0) == n_tok - 1
        dw_acc[...] = jnp.zeros_like(dw_acc)
        db_acc[...] = jnp.zeros_like(db_acc)

        def strip(row0):
            rows = pl.ds(row0, STRIP)
            for c0 in range(0, D_FF, LANES):
                gc, vc = slice(c0, c0 + LANES), slice(D_FF + c0, D_FF + c0 + LANES)
                del_g = _delays(_history(up_ref, prev_ref, first, row0, gc), FFN_CONV)
                del_v = _delays(_history(up_ref, prev_ref, first, row0, vc), FFN_CONV)
                dcg, dcv = dconv(del_g, del_v, da_ref[rows, gc], w_ref[:, gc], w_ref[:, vc], b_ref[:, gc], b_ref[:, vc])
                dc_ref[rows, gc] = dcg
                dc_ref[rows, vc] = dcv
                db_acc[:, gc] += _fold_rows(dcg)
                db_acc[:, vc] += _fold_rows(dcv)
                for j in range(FFN_CONV):
                    k = FFN_CONV - 1 - j
                    dw_acc[k * SUBLANES:(k + 1) * SUBLANES, gc] += _fold_rows(dcg * del_g[j])
                    dw_acc[k * SUBLANES:(k + 1) * SUBLANES, vc] += _fold_rows(dcv * del_v[j])

        _first_then_strips(tm, strip)

        for c0 in range(0, D_FF, LANES):
            gc, vc = slice(c0, c0 + LANES), slice(D_FF + c0, D_FF + c0 + LANES)

            def delayed(cols):
                return _delays(jnp.concatenate([up_ref[tm - HALO:tm, cols], next_ref[:, cols]], axis=0), FFN_CONV)

            dcg, dcv = dconv(delayed(gc), delayed(vc), dan_ref[:, gc], w_ref[:, gc], w_ref[:, vc], b_ref[:, gc], b_ref[:, vc])
            dc_ref[tm:, gc] = jnp.where(last, 0.0, dcg)
            dc_ref[tm:, vc] = jnp.where(last, 0.0, dcv)

        def strip_dx(row0):
            for c0 in range(0, width, LANES):
                cols = slice(c0, c0 + LANES)
                dup_ref[pl.ds(row0, STRIP), cols] = _advanced_conv(dc_ref, row0, cols, w_ref[:, cols]).astype(BF16)

        _for_strips(tm, STRIP, strip_dx)

        @pl.when(first)
        def _():
            dw_ref[...] = jnp.zeros_like(dw_ref)
            db_ref[...] = jnp.zeros_like(db_ref)

        for k in range(FFN_CONV):
            dw_ref[k:k + 1, :] += jnp.sum(dw_acc[k * SUBLANES:(k + 1) * SUBLANES, :], axis=0, keepdims=True)
        db_ref[...] += jnp.sum(db_acc[...], axis=0, keepdims=True)

    next_rows = lambda i: jnp.minimum((i + 1) * (tm // HALO), t // HALO - 1)
    full = lambda rows: pl.BlockSpec((rows, width), lambda i: (0, 0))
    return _pcall(
        body, grid=(n_tok,),
        in_specs=[pl.BlockSpec((tm, width), lambda i: (i, 0)),
                  pl.BlockSpec((HALO, width), lambda i: (jnp.maximum(i * (tm // HALO) - 1, 0), 0)),
                  pl.BlockSpec((HALO, width), lambda i: (next_rows(i), 0)),
                  pl.BlockSpec((tm, D_FF), lambda i: (i, 0)), pl.BlockSpec((HALO, D_FF), lambda i: (next_rows(i), 0)),
                  full(FFN_CONV), full(1)],
        out_specs=[pl.BlockSpec((tm, width), lambda i: (i, 0)), full(FFN_CONV), full(1)],
        out_shape=[jax.ShapeDtypeStruct((t, width), BF16), jax.ShapeDtypeStruct((FFN_CONV, width), F32),
                   jax.ShapeDtypeStruct((1, width), F32)],
        scratch_shapes=[pltpu.VMEM((tm + HALO, width), F32),
                        pltpu.VMEM((FFN_CONV * SUBLANES, width), F32), pltpu.VMEM((SUBLANES, width), F32)],
        compiler_params=_params("arbitrary"), name="ffn_bwd")(up, up, up, dact, dact, conv_w, conv_b)


def _my_position():
    return lax.axis_index("x"), lax.axis_index("y"), lax.axis_index("c")


COPIES = N_DEV - 1


def _all_gather(arrays):
    n = len(arrays)

    def body(*refs):
        x_refs, out_refs = refs[:n], refs[n:2 * n]
        send_sems, recv_sems, local_sems = refs[2 * n:]
        x, y, cc = _my_position()
        me, sibling = (x, y, cc), (x, y, 1 - cc)
        chips = [(1 - x, y), (x, 1 - y), (1 - x, 1 - y)]

        def block(a, px, py, pc):
            return out_refs[a].at[4 * px + 2 * py + pc]

        def copy(a, k, blk, to, src=None):
            return pltpu.make_async_remote_copy(
                src_ref=block(a, *blk) if src is None else src, dst_ref=block(a, *blk),
                send_sem=send_sems.at[a * COPIES + k], recv_sem=recv_sems.at[a * COPIES + k],
                device_id=to, device_id_type=MESH_ID)

        mine = [pltpu.make_async_copy(x_refs[a], block(a, *me), local_sems.at[a]) for a in range(n)]
        for cp in mine:
            cp.start()
        first = []
        for a in range(n):
            first.append(copy(a, 0, me, sibling, src=x_refs[a]))
            first += [copy(a, 1 + j, me, (*chip, cc), src=x_refs[a]) for j, chip in enumerate(chips)]
        for cp in first:
            cp.start()
        passed = []
        for j, chip in enumerate(chips):
            for a in range(n):
                copy(a, 1 + j, (*chip, cc), me).wait_recv()
                passed.append(copy(a, 4 + j, (*chip, cc), sibling))
                passed[-1].start()
        for a in range(n):
            copy(a, 0, sibling, me).wait_recv()
        for j, chip in enumerate(chips):
            for a in range(n):
                copy(a, 4 + j, (*chip, 1 - cc), me).wait_recv()
        for cp in first + passed:
            cp.wait_send()
        for cp in mine:
            cp.wait()

    any_spec = pl.BlockSpec(memory_space=pl.ANY)
    return _pcall(
        body, out_shape=[jax.ShapeDtypeStruct((N_DEV,) + a.shape, a.dtype) for a in arrays],
        in_specs=[any_spec] * n, out_specs=[any_spec] * n,
        scratch_shapes=[pltpu.SemaphoreType.DMA((n * COPIES,)), pltpu.SemaphoreType.DMA((n * COPIES,)),
                        pltpu.SemaphoreType.DMA((n,))],
        name="all_gather")(*arrays)


def _all_to_all(sends):
    n = len(sends)

    def body(*refs):
        send_refs, recv_refs = refs[:n], refs[n:2 * n]
        send_sems, recv_sems, local_sems = refs[2 * n:]
        x, y, cc = _my_position()
        me = 4 * x + 2 * y + cc
        mine = [pltpu.make_async_copy(send_refs[a].at[me], recv_refs[a].at[me], local_sems.at[a]) for a in range(n)]
        for cp in mine:
            cp.start()
        copies = []
        for rel in range(1, N_DEV):
            px, py, pc = x ^ (rel >> 2), y ^ ((rel >> 1) & 1), cc ^ (rel & 1)
            for a in range(n):
                copies.append(pltpu.make_async_remote_copy(
                    src_ref=send_refs[a].at[4 * px + 2 * py + pc], dst_ref=recv_refs[a].at[me],
                    send_sem=send_sems.at[a * COPIES + rel - 1], recv_sem=recv_sems.at[a * COPIES + rel - 1],
                    device_id=(px, py, pc), device_id_type=MESH_ID))
        for cp in copies:
            cp.start()
        for cp in copies:
            cp.wait()
        for cp in mine:
            cp.wait()

    any_spec = pl.BlockSpec(memory_space=pl.ANY)
    return _pcall(
        body, out_shape=[jax.ShapeDtypeStruct(s.shape, s.dtype) for s in sends],
        in_specs=[any_spec] * n, out_specs=[any_spec] * n,
        scratch_shapes=[pltpu.SemaphoreType.DMA((n * COPIES,)), pltpu.SemaphoreType.DMA((n * COPIES,)),
                        pltpu.SemaphoreType.DMA((n,))],
        name="all_to_all")(*sends)


def _hbm(a):
    return pltpu.with_memory_space_constraint(a, pltpu.HBM)


def _split_copies(send_refs, land_refs, send_sems, recv_sems, local_sems, gather):
    x, y, cc = _my_position()
    me = 4 * x + 2 * y + cc
    local, remote = [], []
    for a, (send, land) in enumerate(zip(send_refs, land_refs)):
        local.append(pltpu.make_async_copy(send if gather else send.at[me], land.at[me], local_sems.at[a]))
    for a, (send, land) in enumerate(zip(send_refs, land_refs)):
        for rel in range(1, N_DEV):
            px, py, pc = x ^ (rel >> 2), y ^ ((rel >> 1) & 1), cc ^ (rel & 1)
            remote.append(pltpu.make_async_remote_copy(
                src_ref=send if gather else send.at[4 * px + 2 * py + pc], dst_ref=land.at[me],
                send_sem=send_sems.at[a * COPIES + rel - 1], recv_sem=recv_sems.at[a * COPIES + rel - 1],
                device_id=(px, py, pc), device_id_type=MESH_ID))
    return local, remote


SPLIT_EFFECT = pltpu.SideEffectType.DATAFLOW_SIDE_EFFECTING


def _exchange_start(sends, after, gather, name):
    n = len(sends)
    lands = [_hbm(lax.empty((N_DEV,) + s.shape if gather else s.shape, s.dtype)) for s in sends]

    def body(*refs):
        send_refs, land_refs = refs[:n], refs[n:2 * n]
        send_sems, recv_sems, local_sems = refs[2 * n + 1:2 * n + 4]
        token = refs[-1]
        local, remote = _split_copies(send_refs, land_refs, send_sems, recv_sems, local_sems, gather)
        for cp in local + remote:
            cp.start()
        token[...] = jnp.zeros_like(token)

    hbm, sem = pl.BlockSpec(memory_space=pltpu.HBM), pl.BlockSpec(memory_space=pltpu.SEMAPHORE)
    out = _pcall(
        body, name=name,
        out_shape=[pltpu.SemaphoreType.DMA((n * COPIES,)), pltpu.SemaphoreType.DMA((n * COPIES,)), pltpu.SemaphoreType.DMA((n,))]
        + [pltpu.HBM(s.shape, s.dtype) for s in sends] + [pltpu.HBM(z.shape, z.dtype) for z in lands]
        + [jax.ShapeDtypeStruct((SUBLANES, LANES), F32)],
        in_specs=[hbm] * (2 * n) + [pl.BlockSpec(memory_space=pl.ANY)],
        out_specs=[sem] * 3 + [hbm] * (2 * n) + [pl.BlockSpec(memory_space=pltpu.VMEM)],
        input_output_aliases={i: 3 + i for i in range(2 * n)},
        compiler_params=pltpu.CompilerParams(has_side_effects=SPLIT_EFFECT),
    )(*[_hbm(s) for s in sends], *lands, after)
    return dict(sems=out[:3], sends=out[3:3 + n], lands=out[3 + n:3 + 2 * n], gather=gather), out[-1]


def _exchange_wait(handle, after, name):
    sends, lands, gather = handle["sends"], handle["lands"], handle["gather"]
    n = len(sends)

    def body(*refs):
        send_refs, land_refs = refs[:n], refs[n:2 * n]
        send_sems, recv_sems, local_sems = refs[2 * n:2 * n + 3]
        local, remote = _split_copies(send_refs, land_refs, send_sems, recv_sems, local_sems, gather)
        for cp in remote:
            cp.wait_send()
            cp.wait_recv()
        for cp in local:
            cp.wait()

    hbm, sem = pl.BlockSpec(memory_space=pltpu.HBM), pl.BlockSpec(memory_space=pltpu.SEMAPHORE)
    out = _pcall(
        body, name=name,
        out_shape=[pltpu.HBM(s.shape, s.dtype) for s in sends] + [pltpu.HBM(z.shape, z.dtype) for z in lands],
        in_specs=[hbm] * (2 * n) + [sem] * 3 + [pl.BlockSpec(memory_space=pl.ANY)],
        out_specs=[hbm] * (2 * n), input_output_aliases={i: i for i in range(2 * n)},
        compiler_params=pltpu.CompilerParams(has_side_effects=SPLIT_EFFECT),
    )(*sends, *lands, *handle["sems"], after)
    return out[n:]


def _sum_and_adamw(recv, w, m, v, name):
    _, r, wp = recv.shape
    c = w.shape[-1]
    lead = w.ndim == 3
    tr = SLAB_ROW_TILE if r % SLAB_ROW_TILE == 0 else (SLAB_ROW_TILE // 4 if r % (SLAB_ROW_TILE // 4) == 0 else r)
    bc1 = 1.0 - ADAM_B1 ** ADAM_STEP
    bc2 = 1.0 - ADAM_B2 ** ADAM_STEP

    def body(recv_ref, w_ref, m_ref, v_ref, g_ref, d_ref, nm_ref, nv_ref):
        g = recv_ref[0, :, 0:c].astype(F32)
        for s in range(1, N_DEV):
            g = g + recv_ref[s, :, 0:c].astype(F32)
        m_new = ADAM_B1 * m_ref[...] + (1.0 - ADAM_B1) * g
        v_new = ADAM_B2 * v_ref[...] + (1.0 - ADAM_B2) * (g * g)
        m_hat = m_new / bc1
        v_hat = v_new / bc2
        g_ref[...] = g
        d_ref[...] = -ADAM_LR * (m_hat / (jnp.sqrt(v_hat) + ADAM_EPS) + ADAM_WD * w_ref[...])
        nm_ref[...] = m_new
        nv_ref[...] = v_new

    tile = pl.BlockSpec((None, tr, c), lambda i: (0, i, 0)) if lead else pl.BlockSpec((tr, c), lambda i: (i, 0))
    return _pcall(
        body, grid=(r // tr,),
        in_specs=[pl.BlockSpec((N_DEV, tr, wp), lambda i: (0, i, 0)), tile, tile, tile],
        out_specs=[tile] * 4, out_shape=[jax.ShapeDtypeStruct(w.shape, F32)] * 4,
        compiler_params=_params("parallel"), name=name)(recv, w, m, v)


SHARDED_TAPS = ("dn_conv_w", "ffn_conv_w")
REPLICATED = ("attn_norm_g", "dn_a_log", "dn_dt_bias", "dn_out_norm_g", "sg_norm_g", "sg_w", "sg_b", "ffn_norm_g",
              "ffn_conv_b", "final_norm_g")
SMALL = SHARDED_TAPS + REPLICATED
WEIGHT_ORDER = ("attn_norm_g", "w_in", "dn_conv_w", "dn_a_log", "dn_dt_bias", "dn_out_norm_g", "sg_norm_g", "sg_w", "sg_b",
                "w_out", "ffn_norm_g", "w_up", "ffn_conv_w", "ffn_conv_b", "w_down", "final_norm_g")
SLAB_COLS = 1024
SLAB_ROW_TILE = 128


def _pad_to(flat, multiple):
    pad = (-flat.shape[-1]) % multiple
    if pad == 0:
        return flat
    return jnp.pad(flat, [(0, 0)] * (flat.ndim - 1) + [(0, pad)])


def _pack_small(named):
    flat = jnp.concatenate([named[n].reshape(-1) for n in SMALL])
    return _pad_to(flat, SUBLANES * SLAB_COLS).reshape(-1, SLAB_COLS)


def _unpack_small(slab, like):
    flat = slab.reshape(-1)
    out, off = {}, 0
    for n in SMALL:
        size = like[n].size
        out[n] = flat[off:off + size].reshape(like[n].shape)
        off += size
    return out


def _split_columns(full, n_local):
    r = full.shape[0]
    return full.reshape(r, N_DEV, n_local).transpose(1, 0, 2).reshape(N_DEV, r * n_local)


def _join_columns(blocks, r, n_local):
    return blocks.reshape(N_DEV, r, n_local).transpose(1, 0, 2).reshape(r, N_DEV * n_local)


def _lanes4(a):
    return jnp.pad(a.reshape(1, N_HEADS), ((0, 0), (0, LANES - N_HEADS)))


def kernel(x, attn_norm_g, w_in, dn_conv_w, dn_a_log, dn_dt_bias, dn_out_norm_g, sg_norm_g, sg_w, sg_b, w_out, ffn_norm_g, w_up, ffn_conv_w, ffn_conv_b, w_down, final_norm_g, loss_target, m_attn_norm_g, m_w_in, m_dn_conv_w, m_dn_a_log, m_dn_dt_bias, m_dn_out_norm_g, m_sg_norm_g, m_sg_w, m_sg_b, m_w_out, m_ffn_norm_g, m_w_up, m_ffn_conv_w, m_ffn_conv_b, m_w_down, m_final_norm_g, v_attn_norm_g, v_w_in, v_dn_conv_w, v_dn_a_log, v_dn_dt_bias, v_dn_out_norm_g, v_sg_norm_g, v_sg_w, v_sg_b, v_w_out, v_ffn_norm_g, v_w_up, v_ffn_conv_w, v_ffn_conv_b, v_w_down, v_final_norm_g):
    weights = dict(attn_norm_g=attn_norm_g, w_in=w_in, dn_conv_w=dn_conv_w, dn_a_log=dn_a_log, dn_dt_bias=dn_dt_bias,
                   dn_out_norm_g=dn_out_norm_g, sg_norm_g=sg_norm_g, sg_w=sg_w, sg_b=sg_b, w_out=w_out, ffn_norm_g=ffn_norm_g,
                   w_up=w_up, ffn_conv_w=ffn_conv_w, ffn_conv_b=ffn_conv_b, w_down=w_down, final_norm_g=final_norm_g)
    m_in = dict(attn_norm_g=m_attn_norm_g, w_in=m_w_in, dn_conv_w=m_dn_conv_w, dn_a_log=m_dn_a_log, dn_dt_bias=m_dn_dt_bias,
                dn_out_norm_g=m_dn_out_norm_g, sg_norm_g=m_sg_norm_g, sg_w=m_sg_w, sg_b=m_sg_b, w_out=m_w_out,
                ffn_norm_g=m_ffn_norm_g, w_up=m_w_up, ffn_conv_w=m_ffn_conv_w, ffn_conv_b=m_ffn_conv_b, w_down=m_w_down,
                final_norm_g=m_final_norm_g)
    v_in = dict(attn_norm_g=v_attn_norm_g, w_in=v_w_in, dn_conv_w=v_dn_conv_w, dn_a_log=v_dn_a_log, dn_dt_bias=v_dn_dt_bias,
                dn_out_norm_g=v_dn_out_norm_g, sg_norm_g=v_sg_norm_g, sg_w=v_sg_w, sg_b=v_sg_b, w_out=v_w_out,
                ffn_norm_g=v_ffn_norm_g, w_up=v_w_up, ffn_conv_w=v_ffn_conv_w, ffn_conv_b=v_ffn_conv_b, w_down=v_w_down,
                final_norm_g=v_final_norm_g)

    n_in, n_up = w_in.shape[2], w_up.shape[2]
    r_out, r_down = w_out.shape[1], w_down.shape[1]
    n_dnc, n_ffc = dn_conv_w.shape[2], ffn_conv_w.shape[2]
    transposed = lambda a: jnp.transpose(a, (0, 2, 1))
    taps = _pad_to(jnp.concatenate([dn_conv_w.reshape(-1), ffn_conv_w.reshape(-1)]), SUBLANES * LANES).reshape(-1, LANES)
    g_in, g_taps = _all_gather([transposed(w_in)[0].astype(BF16), taps])
    gather_out, token = _exchange_start([w_out[0].astype(BF16)], g_taps, True, "gather_w_out")
    gather_up, token = _exchange_start([transposed(w_up)[0].astype(BF16)], token, True, "gather_w_up")
    gather_down, token = _exchange_start([w_down[0].astype(BF16)], token, True, "gather_w_down")
    w_in_t = jnp.pad(g_in.reshape(N_DEV * n_in, D_MODEL), ((0, PROJ_PAD - N_DEV * n_in), (0, 0)))
    taps_all = g_taps.reshape(N_DEV, -1)
    dn_conv_full = _join_columns(taps_all[:, :CONV_K * n_dnc], CONV_K, n_dnc)
    ffn_conv_full = _join_columns(taps_all[:, CONV_K * n_dnc:CONV_K * n_dnc + FFN_CONV * n_ffc], FFN_CONV, n_ffc)
    late = dict(
        w_out=lambda after: _exchange_wait(gather_out, after, "gather_w_out_wait")[0].reshape(N_DEV * r_out, D_MODEL),
        w_up_t=lambda after: _exchange_wait(gather_up, after, "gather_w_up_wait")[0].reshape(N_DEV * n_up, D_MODEL),
        w_down=lambda after: _exchange_wait(gather_down, after, "gather_w_down_wait")[0].reshape(N_DEV * r_down, D_MODEL))

    def send_early(blocks, after, name):
        return _exchange_start(blocks, after, False, name)

    def send_small(g, loss_lanes, after):
        small = jnp.concatenate([g[n].reshape(-1) for n in REPLICATED] + [loss_lanes[0, 0:1]])
        slab = jnp.concatenate([_split_columns(g["dn_conv_w"], n_dnc), _split_columns(g["ffn_conv_w"], n_ffc),
                                jnp.broadcast_to(small[None, :], (N_DEV, small.shape[0]))], axis=1)
        return send_early([_pad_to(slab, SUBLANES * SLAB_COLS).reshape(N_DEV, -1, SLAB_COLS)], after, "send_small")

    upd = {}

    def update_early(sent_down, sent_up_out, sent_small, after):
        r_dn, = _exchange_wait(sent_down, after, "send_dw_down_wait")
        r_up, r_o = _exchange_wait(sent_up_out, after, "send_dw_up_out_wait")
        r_small, = _exchange_wait(sent_small, after, "send_small_wait")
        upd["w_down"] = _sum_and_adamw(r_dn, w_down, m_w_down, v_w_down, "adamw_w_down")
        upd["w_up"] = [transposed(o) for o in _sum_and_adamw(r_up, transposed(w_up), transposed(m_w_up), transposed(v_w_up),
                                                             "adamw_w_up")]
        upd["w_out"] = _sum_and_adamw(r_o, w_out, m_w_out, v_w_out, "adamw_w_out")
        upd["small"] = _sum_and_adamw(r_small, _pack_small(weights), _pack_small(m_in), _pack_small(v_in), "adamw_small")

    grad_x, d_g1, sent_in = _local_step(
        x[0], loss_target[0], w_in_t, late, send_early, send_small, update_early, dn_conv_full, ffn_conv_full,
        attn_norm_g + token[0:1, 0:1], dn_a_log, dn_dt_bias, dn_out_norm_g, sg_norm_g, sg_w, sg_b, ffn_norm_g, ffn_conv_b,
        final_norm_g, n_in)

    norm_rows = D_MODEL // LANES
    r_g1, = _all_to_all([jnp.broadcast_to(d_g1.reshape(1, norm_rows, LANES), (N_DEV, norm_rows, LANES))])
    r_in, = _exchange_wait(sent_in, r_g1, "send_dw_in_wait")
    upd["w_in"] = [transposed(o) for o in _sum_and_adamw(r_in, transposed(w_in), transposed(m_w_in), transposed(v_w_in),
                                                         "adamw_w_in")]
    small_upd = upd.pop("small")
    as_rows = lambda a: a.reshape(norm_rows, LANES)
    norm_upd = _sum_and_adamw(r_g1, as_rows(attn_norm_g), as_rows(m_attn_norm_g), as_rows(v_attn_norm_g), "adamw_attn_norm")
    results = []
    for i in range(4):
        named = _unpack_small(small_upd[i], weights)
        named.update({n: upd[n][i] for n in upd})
        named["attn_norm_g"] = norm_upd[i].reshape(attn_norm_g.shape)
        results.append(named)

    loss = small_upd[0].reshape(-1)[sum(weights[n].size for n in SMALL)]
    return (loss, grad_x[None], *[r[n] for r in results for n in WEIGHT_ORDER])


def _local_step(x2d, tgt, w_in_t, late, send_early, send_small, update_early, dn_conv_full, ffn_conv_full, attn_norm_g,
                dn_a_log, dn_dt_bias, dn_out_norm_g, sg_norm_g, sg_w, sg_b, ffn_norm_g, ffn_conv_b, final_norm_g, n_in):
    t = x2d.shape[0]
    g1, g2, gf = attn_norm_g, ffn_norm_g, final_norm_g.reshape(1, D_MODEL)
    a_log4, dt_bias4 = _lanes4(dn_a_log), _lanes4(dn_dt_bias)
    sg_w3 = sg_w[0]
    sg_b_t = sg_b[0].T
    conv_b = ffn_conv_b

    p, h1, rstd1 = _rmsnorm_matmul(x2d, g1, w_in_t, "norm_in_proj", 512)
    q, k, v, beta4, g4 = _dn_prep(p, dn_conv_full, a_log4, dt_bias4)
    mix_half, s_all, a_inv_all = _dn_forward(q, k, v, beta4, g4, p, dn_out_norm_g)
    mix = _sg_forward(p, sg_norm_g, sg_w3, sg_b_t, mix_half)
    w_out_full = late["w_out"](mix)
    x2 = _matmul(mix, w_out_full, "nn", "out_proj", (1024, 1024, 1024), add=x2d)
    w_up_t = late["w_up_t"](x2)
    up, act, h2, rstd2 = _norm_up_ffn(x2, g2, w_up_t, ffn_conv_full, conv_b)
    w_down_full = late["w_down"](act)
    fn, outs = _final_loss_rows(t, D_MODEL)
    loss_lanes, dx3, dx3b, d_gf = _matmul_rows(act, w_down_full, "nn", "down_proj_loss", 512,
                                               [(x2, "rows"), (tgt, "rows"), (gf, "whole")], outs, fn)

    dact = _matmul(dx3b, w_down_full, "nt", "down_proj_dx", (512, D_FF, D_MODEL))
    d_w_down = _matmul(act, dx3b, "tn", "down_proj_dw", (256, 1024, t), out_dtype=BF16)
    sent_down, token = send_early([d_w_down.reshape(N_DEV, D_FF // N_DEV, D_MODEL)], d_w_down, "send_dw_down")
    dup, d_ffn_conv, d_ffn_conv_b = _ffn_bwd(up, ffn_conv_full, conv_b + token[0:1, 0:1], dact)
    fn, outs = _rmsnorm_bwd_rows(t, D_MODEL)
    dx2, dx2b, d_g2 = _matmul_rows(dup, w_up_t, "nn", "up_proj_dx_norm", 256,
                                   [(x2, "rows"), (rstd2, "rows"), (g2, "whole"), (dx3, "rows")], outs, fn)
    d_w_up_t = _matmul(dup, h2, "tn", "up_proj_dw", (512, 1024, t), out_dtype=BF16)
    dmix = _matmul(dx2b, w_out_full, "nt", "out_proj_dx", (1024, 1024, 1024))
    d_w_out = _matmul(mix, dx2b, "tn", "out_proj_dw", (512, 1024, t), out_dtype=BF16)
    sent_up_out, token = send_early(
        [d_w_up_t.reshape(N_DEV, 2 * D_FF // N_DEV, D_MODEL), d_w_out.reshape(N_DEV, D_MODEL // N_DEV, D_MODEL)],
        d_w_out, "send_dw_up_out")
    dp, d_sg_norm, d_sg_w, d_sg_b_t = _sg_backward(p, sg_norm_g + token[0:1, 0:1], sg_w3, sg_b_t, dmix)
    dq, dk, dv, dbeta4, dg4, dp, d_dn_norm = _dn_backward(q, k, v, beta4, g4, p, dn_out_norm_g, s_all, a_inv_all, dmix, dp)
    dc_dn, d_dn_conv, dp, d_a_log4, d_dt_bias4 = _dn_prep_bwd(p, dn_conv_full, a_log4, dt_bias4, dq, dk, dv, dbeta4, dg4, dp)
    small_grads = dict(
        attn_norm_g=jnp.zeros_like(attn_norm_g), dn_conv_w=d_dn_conv, dn_a_log=d_a_log4[:, :N_HEADS],
        dn_dt_bias=d_dt_bias4[:, :N_HEADS], dn_out_norm_g=d_dn_norm, sg_norm_g=d_sg_norm, sg_w=d_sg_w,
        sg_b=d_sg_b_t[:, :SG_GROUPS].T, ffn_norm_g=d_g2, ffn_conv_w=d_ffn_conv, ffn_conv_b=d_ffn_conv_b, final_norm_g=d_gf)
    sent_small, token = send_small(small_grads, loss_lanes, d_dn_conv)
    dp = _conv_bwd_input(dc_dn, dn_conv_full + token[0:1, 0:1], "dn_conv_dx", out_cols=PROJ_PAD, into=dp)
    d_w_in_t = _matmul(dp, h1, "tn", "in_proj_dw", (PROJ_PAD // 5, 1024, t), out_dtype=BF16)
    sent_in, token = send_early([d_w_in_t[:N_DEV * n_in].reshape(N_DEV, n_in, D_MODEL)], d_w_in_t, "send_dw_in")
    update_early(sent_down, sent_up_out, sent_small, token)
    fn, outs = _rmsnorm_bwd_rows(t, D_MODEL)
    grad_x, _, d_g1 = _matmul_rows(dp, w_in_t, "nn", "in_proj_dx_norm", 512,
                                   [(x2d, "rows"), (rstd1, "rows"), (g1 + token[0:1, 0:1], "whole"), (dx2, "rows")], outs, fn)

    return grad_x, d_g1, sent_in
```

```python
import math

import jax
import jax.numpy as jnp
from jax import lax
from jax.experimental import pallas as pl
from jax.experimental.pallas import tpu as pltpu

F32 = jnp.float32
BF16 = jnp.bfloat16
HI = lax.Precision.HIGHEST

D_MODEL = 1024
DN_WIDTH = 512
HEAD_DIM = 128
N_HEADS = 4
SG_WIDTH = 512
SG_GROUPS = 4
SG_DIM = 128
SG_BLOCK = 128
D_FF = 2816
CHUNK = 64
CONV_K = 4
FFN_CONV = 3
EPS = 1e-6
PROJ_MAIN = 3072
PROJ_PAD = 3200
GELU_C = math.sqrt(2.0 / math.pi)
N_DEV = 8
LANES = 128
SUBLANES = 8
HALO = SUBLANES
VMEM_LIMIT = 48 * 1024 * 1024

ADAM_LR = 0.001
ADAM_B1 = 0.9
ADAM_B2 = 0.999
ADAM_EPS = 1e-08
ADAM_WD = 0.01
ADAM_STEP = 10

MESH_ID = pl.DeviceIdType.MESH


def _pcall(body, **kw):
    return pl.pallas_call(body, **kw)


def _params(*sem):
    return pltpu.CompilerParams(dimension_semantics=sem, vmem_limit_bytes=VMEM_LIMIT)


def _pick(n, cap):
    best = None
    for t in range(LANES, cap + 1, LANES):
        if n % t == 0:
            best = t
    return best if best else n


FAST, EXACT = "bf16 operands, one pass", "f32 operands, six bf16 passes"


def dot_f32(a, b, dims, tier):
    if tier == FAST:
        return lax.dot_general(a.astype(BF16), b.astype(BF16), dims, preferred_element_type=F32)
    return lax.dot_general(a, b, dims, precision=HI, preferred_element_type=F32)


def dot_nn(a, b, tier=EXACT):
    return dot_f32(a, b, (((1,), (0,)), ((), ())), tier)


def dot_nt(a, b, tier=EXACT):
    return dot_f32(a, b, (((1,), (1,)), ((), ())), tier)


def dot_tn(a, b, tier=EXACT):
    return dot_f32(a, b, (((0,), (0,)), ((), ())), tier)


def sigmoid(x):
    return 0.5 * jnp.tanh(0.5 * x) + 0.5


def silu(x):
    return x * sigmoid(x)


def silu_grad(x):
    s = sigmoid(x)
    return s * (1.0 + x * (1.0 - s))


def gelu(x):
    return 0.5 * x * (1.0 + jnp.tanh(GELU_C * (x + 0.044715 * x * x * x)))


def gelu_grad(x):
    t = jnp.tanh(GELU_C * (x + 0.044715 * x * x * x))
    return 0.5 * (1.0 + t) + 0.5 * x * (1.0 - t * t) * GELU_C * (1.0 + 3.0 * 0.044715 * x * x)


def softplus(z):
    return jnp.maximum(z, 0.0) + jnp.log(1.0 + jnp.exp(-jnp.abs(z)))


def rms_fwd(x, g):
    r = lax.rsqrt(jnp.mean(x * x, axis=-1, keepdims=True) + EPS)
    return x * r * g, r


def rms_bwd(x, r, g, dy):
    dyg = dy * g
    xr = x * r
    dx = r * (dyg - xr * jnp.mean(dyg * xr, axis=-1, keepdims=True))
    return dx, dy * xr


def l2_fwd(x):
    r = lax.rsqrt(jnp.sum(x * x, axis=-1, keepdims=True) + EPS)
    return x * r, r


def l2_bwd(x, r, dy):
    xr = x * r
    return r * (dy - xr * jnp.sum(dy * xr, axis=-1, keepdims=True))


def _tri_masks(n):
    row = lax.broadcasted_iota(jnp.int32, (n, n), 0)
    col = lax.broadcasted_iota(jnp.int32, (n, n), 1)
    return row >= col, row > col


def chunk_cumsum(g4):
    incl, _ = _tri_masks(g4.shape[0])
    return dot_nn(incl.astype(F32), g4)


STACK = N_HEADS * CHUNK
DN_FWD_CHUNKS = 8
DN_CHUNKS = 4


def _head_rows(h):
    return slice(h * CHUNK, (h + 1) * CHUNK)


def _stack_heads(x):
    return jnp.concatenate([x[:, h * HEAD_DIM:(h + 1) * HEAD_DIM] for h in range(N_HEADS)], axis=0)


def _stack_lanes(x4):
    return jnp.concatenate([x4[:, h:h + 1] for h in range(N_HEADS)], axis=0)


def _per_head(fn):
    return jnp.concatenate([fn(h) for h in range(N_HEADS)], axis=0)


def _unit_lower_inverses(l_strict, order):
    c = l_strict[0].shape[0]
    row = lax.broadcasted_iota(jnp.int32, (c, c), 0)
    col = lax.broadcasted_iota(jnp.int32, (c, c), 1)
    eye = (row == col).astype(F32)
    p = [-l for l in l_strict]
    a = [eye + n for n in p]
    for _ in range(int(math.log2(order)) - 1):
        p = [dot_nn(x, x, FAST) for x in p]
        a = [x + dot_nn(x, y, FAST) for x, y in zip(a, p)]
    return a


def dn_chunks_local(chunks, inverses=None):
    row = lax.broadcasted_iota(jnp.int32, (STACK, STACK), 0)
    col = lax.broadcasted_iota(jnp.int32, (STACK, STACK), 1)
    same = (row // CHUNK) == (col // CHUNK)
    incl = jnp.logical_and(same, row >= col)
    strict = jnp.logical_and(same, row > col)
    locs = []
    for q, k, v, beta, gc4 in chunks:
        gc_col = _stack_lanes(gc4)
        gc_row = jnp.sum(jnp.where(row == col, gc_col, 0.0), axis=0, keepdims=True)
        decay = jnp.where(incl, jnp.exp(jnp.minimum(gc_col - gc_row, 0.0)), 0.0)
        gamma = jnp.exp(gc_col)
        gc_last = jnp.concatenate([jnp.broadcast_to(gc4[CHUNK - 1:CHUNK, h:h + 1], (CHUNK, 1)) for h in range(N_HEADS)], axis=0)
        tau = jnp.exp(gc_last - gc_col)
        kb = k * beta
        locs.append(dict(decay=decay, gamma=gamma, tau=tau, cd=jnp.exp(gc_last), kb=kb, qd=q * gamma, kt=k * tau,
                         incl=incl, strict=strict))
    for loc, (q, k, v, beta, gc4) in zip(locs, chunks):
        loc["l_mat"] = jnp.where(strict, dot_nt(loc["kb"], k, FAST) * loc["decay"], 0.0)
    if inverses is None:
        inverses = _unit_lower_inverses([loc["l_mat"] for loc in locs], CHUNK)
    for loc, a_inv in zip(locs, inverses):
        loc["a_inv"] = a_inv
    for loc, (q, k, v, beta, gc4) in zip(locs, chunks):
        sol = dot_nn(loc["a_inv"], jnp.concatenate([v * beta, loc["kb"] * loc["gamma"]], axis=1), FAST)
        loc.update(sol=sol, value=sol[:, :HEAD_DIM], kcd=sol[:, HEAD_DIM:])
        loc["attn"] = jnp.where(incl, dot_nt(q, k, FAST) * loc["decay"], 0.0)
    return locs


def dn_chunk_state(loc, s):
    kcd, qd, kt, cd = loc["kcd"], loc["qd"], loc["kt"], loc["cd"]
    v_new = loc["value"] - _per_head(lambda h: dot_nn(kcd[_head_rows(h)], s[h], FAST))
    o = _per_head(lambda h: dot_nn(qd[_head_rows(h)], s[h], FAST)) + dot_nn(loc["attn"], v_new, FAST)
    s_new = [s[h] * cd[h * CHUNK:h * CHUNK + 1, :] + dot_tn(kt[_head_rows(h)], v_new[_head_rows(h)], FAST)
             for h in range(N_HEADS)]
    loc["v_new"] = v_new
    return o, s_new


def dn_chunks_bwd(items, ds_last):
    hr = _head_rows
    n = len(items)
    pre = []
    for q, k, v, beta, loc, s, do in items:
        pre.append(dict(
            dv_part=dot_tn(loc["attn"], do, FAST),
            dattn=jnp.where(loc["incl"], dot_nt(do, loc["v_new"], FAST), 0.0),
            dqd=_per_head(lambda h: dot_nt(do[hr(h)], s[h], FAST)),
            ds_part=[dot_tn(loc["qd"][hr(h)], do[hr(h)], FAST) for h in range(N_HEADS)]))
    ds_new_of, dv_new_of = [None] * n, [None] * n
    ds = ds_last
    for c in reversed(range(n)):
        loc = items[c][4]
        ds_new_of[c] = ds
        dv_new = pre[c]["dv_part"] + _per_head(lambda h: dot_nn(loc["kt"][hr(h)], ds[h], FAST))
        dv_new_of[c] = dv_new
        ds = [pre[c]["ds_part"][h] + ds[h] * loc["cd"][h * CHUNK:h * CHUNK + 1, :]
              - dot_tn(loc["kcd"][hr(h)], dv_new[hr(h)], FAST) for h in range(N_HEADS)]
    is_last = (lax.broadcasted_iota(jnp.int32, (STACK, 1), 0) % CHUNK) == CHUNK - 1
    out = []
    for c, (q, k, v, beta, loc, s, do) in enumerate(items):
        decay, gamma, tau, cd, kb = loc["decay"], loc["gamma"], loc["tau"], loc["cd"], loc["kb"]
        dv_new, ds_new, dattn, dqd = dv_new_of[c], ds_new_of[c], pre[c]["dattn"], pre[c]["dqd"]
        dkt = _per_head(lambda h: dot_nt(loc["v_new"][hr(h)], ds_new[h], FAST))
        dkcd = -_per_head(lambda h: dot_nt(dv_new[hr(h)], s[h], FAST))
        drhs = dot_tn(loc["a_inv"], jnp.concatenate([dv_new, dkcd], axis=1), FAST)
        dvb, dkbg = drhs[:, :HEAD_DIM], drhs[:, HEAD_DIM:]
        dl = jnp.where(loc["strict"], -dot_nt(drhs, loc["sol"], FAST), 0.0)
        dkk = dl * decay
        dqk = dattn * decay
        e = dl * loc["l_mat"] + dattn * loc["attn"]
        dgc = jnp.sum(e, axis=1, keepdims=True) - jnp.sum(e, axis=0, keepdims=True).T
        dkb = dot_nn(dkk, k, FAST) + dkbg * gamma
        dk = dot_tn(dkk, kb, FAST) + dot_tn(dqk, q, FAST) + dkt * tau
        dq = dot_nn(dqk, k, FAST) + dqd * gamma
        dgamma = jnp.sum(dkbg * kb, axis=1, keepdims=True) + jnp.sum(dqd * q, axis=1, keepdims=True)
        dtau_tau = jnp.sum(dkt * k, axis=1, keepdims=True) * tau
        dgc = dgc + dgamma * gamma - dtau_tau

        def last_term(h):
            dcd = jnp.sum(jnp.sum(ds_new[h] * s[h], axis=1, keepdims=True), axis=0, keepdims=True)
            total = jnp.sum(dtau_tau[hr(h)], axis=0, keepdims=True) + dcd * cd[h * CHUNK:h * CHUNK + 1, :]
            return jnp.broadcast_to(total, (CHUNK, 1))

        dgc = dgc + jnp.where(is_last, _per_head(last_term), 0.0)
        dk = dk + dkb * beta
        dbeta = jnp.sum(dkb * k, axis=1, keepdims=True) + jnp.sum(dvb * v, axis=1, keepdims=True)
        out.append((dq, dk, dvb * beta, dbeta, dgc))
    return out, ds


def _token_tile(t):
    return _pick(t, 256)


STRIP = 32


def _for_strips(n_rows, rows, fn, start=0):
    def step(r, carry):
        fn(pl.multiple_of(r * rows, rows))
        return carry

    lax.fori_loop(start, n_rows // rows, step, 0)


def _fold_rows(x):
    out = x[0:SUBLANES, :]
    for i in range(1, x.shape[0] // SUBLANES):
        out = out + x[i * SUBLANES:(i + 1) * SUBLANES, :]
    return out


def _matmul(a, b, mode, name, tiles, add=None, out_dtype=F32):
    if mode == "nn":
        (m, k), n = a.shape, b.shape[1]
    elif mode == "nt":
        (m, k), n = a.shape, b.shape[0]
    else:
        (k, m), n = a.shape, b.shape[1]
    tm, tn, tk = min(tiles[0], m), min(tiles[1], n), min(tiles[2], k)
    assert m % tm == 0 and n % tn == 0 and k % tk == 0, (name, m, n, k, tiles)
    nk = k // tk
    dims = {"nn": (((1,), (0,)), ((), ())), "nt": (((1,), (1,)), ((), ())), "tn": (((0,), (0,)), ((), ()))}[mode]

    def finish(res, add_ref, o_ref):
        if add_ref is not None:
            res = res + add_ref[...]
        o_ref[...] = res.astype(o_ref.dtype)

    def body(*refs):
        a_ref, b_ref = refs[0], refs[1]
        add_ref = refs[2] if add is not None else None
        o_ref = refs[3] if add is not None else refs[2]
        part = lax.dot_general(a_ref[...], b_ref[...], dims, preferred_element_type=F32)
        if nk == 1:
            finish(part, add_ref, o_ref)
            return
        acc_ref = refs[-1]
        kk = pl.program_id(2)

        @pl.when(kk == 0)
        def _():
            acc_ref[...] = part

        @pl.when(kk > 0)
        def _():
            acc_ref[...] += part

        @pl.when(kk == nk - 1)
        def _():
            finish(acc_ref[...], add_ref, o_ref)

    a_spec = pl.BlockSpec((tk, tm), lambda j, i, kk: (kk, i)) if mode == "tn" else pl.BlockSpec((tm, tk), lambda j, i, kk: (i, kk))
    b_spec = pl.BlockSpec((tn, tk), lambda j, i, kk: (j, kk)) if mode == "nt" else pl.BlockSpec((tk, tn), lambda j, i, kk: (kk, j))
    o_spec = pl.BlockSpec((tm, tn), lambda j, i, kk: (i, j))
    in_specs = [a_spec, b_spec] + ([o_spec] if add is not None else [])
    args = (a, b) + ((add,) if add is not None else ())
    return _pcall(
        body, grid=(n // tn, m // tm, nk), in_specs=in_specs, out_specs=o_spec,
        out_shape=jax.ShapeDtypeStruct((m, n), out_dtype),
        scratch_shapes=[pltpu.VMEM((tm, tn), F32)] if nk > 1 else [],
        compiler_params=_params("parallel", "parallel", "arbitrary"), name=name)(*args)


ROW_PARTS = 2


def _matmul_rows(a, b, mode, name, tm, extra, outs, fn):
    m, k = a.shape
    n = b.shape[1] if mode == "nn" else b.shape[0]
    tm = min(tm, m)
    part = tm // ROW_PARTS
    dims = (((1,), (0,)), ((), ())) if mode == "nn" else (((1,), (1,)), ((), ()))

    def spec(shape, kind):
        if kind == "rows":
            return pl.BlockSpec((tm, shape[1]), lambda i: (i, 0))
        return pl.BlockSpec(shape, lambda i: (0,) * len(shape))

    def body(a_ref, b_ref, *refs):
        first_step = pl.program_id(0) == 0
        pending = None
        for j in range(ROW_PARTS):
            part_rows = slice(j * part, (j + 1) * part)
            rows = lax.dot_general(a_ref[part_rows, :], b_ref[...], dims, preferred_element_type=F32)
            if pending is not None:
                fn(*pending, *refs)
            pending = (rows, jnp.logical_and(first_step, j == 0) if j else first_step, part_rows)
        fn(*pending, *refs)

    return _pcall(
        body, grid=(m // tm,),
        in_specs=[pl.BlockSpec((tm, k), lambda i: (i, 0)), pl.BlockSpec(b.shape, lambda i: (0, 0))]
        + [spec(x.shape, kind) for x, kind in extra],
        out_specs=[spec(shape, kind) for shape, _, kind in outs],
        out_shape=[jax.ShapeDtypeStruct(shape, dtype) for shape, dtype, _ in outs],
        compiler_params=_params("arbitrary"), name=name)(a, b, *[x for x, _ in extra])


def _rmsnorm_matmul(x, g, b_t, name, tm):
    t, d = x.shape
    n = b_t.shape[0]
    tm = min(tm, t)

    def body(x_ref, g_ref, b_ref, o_ref, h_ref, r_ref):
        y, r = rms_fwd(x_ref[...], g_ref[...])
        h = y.astype(BF16)
        h_ref[...] = h
        r_ref[...] = r
        o_ref[...] = lax.dot_general(h, b_ref[...], (((1,), (1,)), ((), ())), preferred_element_type=F32)

    rows = lambda w: pl.BlockSpec((tm, w), lambda i: (i, 0))
    return _pcall(
        body, grid=(t // tm,),
        in_specs=[rows(d), pl.BlockSpec((1, d), lambda i: (0, 0)), pl.BlockSpec((n, d), lambda i: (0, 0))],
        out_specs=[rows(n), rows(d), rows(1)],
        out_shape=[jax.ShapeDtypeStruct((t, n), F32), jax.ShapeDtypeStruct((t, d), BF16), jax.ShapeDtypeStruct((t, 1), F32)],
        compiler_params=_params("parallel"), name=name)(x, g, b_t)


def _rmsnorm_bwd_rows(t, d):
    def fn(dh, first, rows, x_ref, r_ref, g_ref, dres_ref, dx_ref, dxb_ref, dg_ref):
        dx, dg_rows = rms_bwd(x_ref[rows, :], r_ref[rows, :], g_ref[...], dh)
        dx = dx + dres_ref[rows, :]
        dx_ref[rows, :] = dx
        dxb_ref[rows, :] = dx.astype(BF16)

        @pl.when(first)
        def _():
            dg_ref[...] = jnp.zeros_like(dg_ref)

        dg_ref[...] += jnp.sum(dg_rows, axis=0, keepdims=True)

    return fn, [((t, d), F32, "rows"), ((t, d), BF16, "rows"), ((1, d), F32, "whole")]


def _final_loss_rows(t, d):
    def fn(product, first, rows, res_ref, t_ref, g_ref, loss_ref, dx_ref, dxb_ref, dg_ref):
        @pl.when(first)
        def _():
            loss_ref[...] = jnp.zeros_like(loss_ref)
            dg_ref[...] = jnp.zeros_like(dg_ref)

        x = product + res_ref[rows, :]
        y, r = rms_fwd(x, g_ref[...])
        err = y - t_ref[rows, :]
        loss_ref[...] += 0.5 * jnp.sum(jnp.mean(err * err, axis=-1, keepdims=True), axis=0, keepdims=True)
        dx, dg_rows = rms_bwd(x, r, g_ref[...], err * (1.0 / d))
        dx_ref[rows, :] = dx
        dxb_ref[rows, :] = dx.astype(BF16)
        dg_ref[...] += jnp.sum(dg_rows, axis=0, keepdims=True)

    return fn, [((1, LANES), F32, "whole"), ((t, d), F32, "rows"), ((t, d), BF16, "rows"), ((1, d), F32, "whole")]


def _prev_halo_spec(tm, width, col_block):
    return pl.BlockSpec((HALO, width), lambda i: (jnp.maximum(i * (tm // HALO) - 1, 0), col_block))


def _history(tile_ref, halo_ref, first, row0, cols):
    if isinstance(row0, int) and row0 == 0:
        return jnp.concatenate([jnp.where(first, 0.0, halo_ref[:, cols]), tile_ref[0:STRIP, cols]], axis=0)
    return tile_ref[pl.ds(pl.multiple_of(row0 - HALO, HALO), STRIP + HALO), cols]


def _first_then_strips(n_rows, fn):
    fn(0)
    _for_strips(n_rows, STRIP, fn, start=1)


def _delays(ext, taps):
    return [ext[HALO:, :]] + [pltpu.roll(ext, j, 0)[HALO:, :] for j in range(1, taps)]


def _causal_conv(delayed, w):
    taps = len(delayed)
    out = delayed[0] * w[taps - 1:taps, :]
    for j in range(1, taps):
        out = out + delayed[j] * w[taps - 1 - j:taps - j, :]
    return out


def _advanced_conv(buf_ref, row0, cols, w):
    taps = w.shape[0]
    ext = buf_ref[pl.ds(row0, STRIP + HALO), cols]
    out = ext[:STRIP, :] * w[taps - 1:taps, :]
    for j in range(1, taps):
        out = out + pltpu.roll(ext, STRIP + HALO - j, 0)[:STRIP, :] * w[taps - 1 - j:taps - j, :]
    return out


def _dn_prep(p, conv_w, a_log4, dt_bias4):
    t = p.shape[0]
    tm = _token_tile(t)
    w3 = 3 * DN_WIDTH

    def body(x_ref, halo_ref, pbd_ref, w_ref, alog_ref, dtb_ref, q_ref, k_ref, v_ref, beta_ref, g_ref):
        first = pl.program_id(0) == 0

        def strip(row0):
            rows = pl.ds(row0, STRIP)
            for h in range(N_HEADS):
                sl = slice(h * HEAD_DIM, (h + 1) * HEAD_DIM)
                for part, out_ref in ((0, q_ref), (1, k_ref), (2, v_ref)):
                    cols = slice(part * DN_WIDTH + h * HEAD_DIM, part * DN_WIDTH + (h + 1) * HEAD_DIM)
                    y = silu(_causal_conv(_delays(_history(x_ref, halo_ref, first, row0, cols), CONV_K), w_ref[:, cols]))
                    if part == 0:
                        y = l2_fwd(y)[0] * (HEAD_DIM ** -0.5)
                    elif part == 1:
                        y = l2_fwd(y)[0]
                    out_ref[rows, sl] = y
            head = lax.broadcasted_iota(jnp.int32, (STRIP, LANES), 1) < N_HEADS
            pbd = pbd_ref[rows, :]
            beta_ref[rows, :] = jnp.where(head, sigmoid(pbd), 0.0)
            a_raw = pltpu.roll(pbd, LANES - N_HEADS, 1)
            g_ref[rows, :] = jnp.where(head, -jnp.exp(alog_ref[...]) * softplus(a_raw + dtb_ref[...]), 0.0)

        _first_then_strips(tm, strip)

    tok = lambda w, cb: pl.BlockSpec((tm, w), lambda i: (i, cb))
    full = lambda a: pl.BlockSpec(a.shape, lambda i: (0, 0))
    return _pcall(
        body, grid=(t // tm,),
        in_specs=[tok(w3, 0), _prev_halo_spec(tm, w3, 0), tok(LANES, PROJ_MAIN // LANES),
                  full(conv_w), full(a_log4), full(dt_bias4)],
        out_specs=[tok(DN_WIDTH, 0)] * 3 + [tok(LANES, 0)] * 2,
        out_shape=[jax.ShapeDtypeStruct((t, DN_WIDTH), F32)] * 3 + [jax.ShapeDtypeStruct((t, LANES), F32)] * 2,
        compiler_params=_params("parallel"), name="dn_prep")(p, p, p, conv_w, a_log4, dt_bias4)


def _dn_prep_bwd(p, conv_w, a_log4, dt_bias4, dq, dk, dv, dbeta4, dg4, dp_buf):
    t = p.shape[0]
    tm = _token_tile(t)
    w3 = 3 * DN_WIDTH

    def body(x_ref, halo_ref, pbd_ref, w_ref, alog_ref, dtb_ref, dq_ref, dk_ref, dv_ref, dbeta_ref, dg_ref, _,
             dc_ref, dw_ref, dpbd_ref, dalog_ref, ddtb_ref, dw_acc, lane_acc):
        first = pl.program_id(0) == 0
        dw_acc[...] = jnp.zeros_like(dw_acc)
        lane_acc[...] = jnp.zeros_like(lane_acc)

        def strip(row0):
            rows = pl.ds(row0, STRIP)
            for h in range(N_HEADS):
                sl = slice(h * HEAD_DIM, (h + 1) * HEAD_DIM)
                for part, dy_ref in ((0, dq_ref), (1, dk_ref), (2, dv_ref)):
                    cols = slice(part * DN_WIDTH + h * HEAD_DIM, part * DN_WIDTH + (h + 1) * HEAD_DIM)
                    delayed = _delays(_history(x_ref, halo_ref, first, row0, cols), CONV_K)
                    c = _causal_conv(delayed, w_ref[:, cols])
                    dy = dy_ref[rows, sl]
                    if part < 2:
                        y = silu(c)
                        _, r = l2_fwd(y)
                        dy = l2_bwd(y, r, dy * (HEAD_DIM ** -0.5) if part == 0 else dy)
                    dc = dy * silu_grad(c)
                    dc_ref[rows, cols] = dc
                    for j in range(CONV_K):
                        k = CONV_K - 1 - j
                        dw_acc[k * SUBLANES:(k + 1) * SUBLANES, cols] += _fold_rows(dc * delayed[j])
            head = lax.broadcasted_iota(jnp.int32, (STRIP, LANES), 1) < N_HEADS
            pbd = pbd_ref[rows, :]
            beta = sigmoid(pbd)
            dpb = jnp.where(head, dbeta_ref[rows, :] * beta * (1.0 - beta), 0.0)
            z = pltpu.roll(pbd, LANES - N_HEADS, 1) + dtb_ref[...]
            neg_rate = -jnp.exp(alog_ref[...])
            dg = dg_ref[rows, :]
            dpa = jnp.where(head, dg * neg_rate * sigmoid(z), 0.0)
            dpbd_ref[rows, :] = (dpb + pltpu.roll(dpa, N_HEADS, 1)).astype(BF16)
            g = jnp.where(head, neg_rate * softplus(z), 0.0)
            lane_acc[0:SUBLANES, :] += _fold_rows(dg * g)
            lane_acc[SUBLANES:, :] += _fold_rows(dpa)

        _first_then_strips(tm, strip)

        @pl.when(first)
        def _():
            dw_ref[...] = jnp.zeros_like(dw_ref)
            dalog_ref[...] = jnp.zeros_like(dalog_ref)
            ddtb_ref[...] = jnp.zeros_like(ddtb_ref)

        for k in range(CONV_K):
            dw_ref[k:k + 1, :] += jnp.sum(dw_acc[k * SUBLANES:(k + 1) * SUBLANES, :], axis=0, keepdims=True)
        dalog_ref[...] += jnp.sum(lane_acc[0:SUBLANES, :], axis=0, keepdims=True)
        ddtb_ref[...] += jnp.sum(lane_acc[SUBLANES:, :], axis=0, keepdims=True)

    tok = lambda w, cb: pl.BlockSpec((tm, w), lambda i: (i, cb))
    full = lambda shape: pl.BlockSpec(shape, lambda i: (0, 0))
    return _pcall(
        body, grid=(t // tm,),
        in_specs=[tok(w3, 0), _prev_halo_spec(tm, w3, 0), tok(LANES, PROJ_MAIN // LANES),
                  full(conv_w.shape), full(a_log4.shape), full(dt_bias4.shape)] + [tok(DN_WIDTH, 0)] * 3 + [tok(LANES, 0)] * 2
        + [pl.BlockSpec(memory_space=pl.ANY)],
        out_specs=[tok(w3, 0), full((CONV_K, w3)), tok(LANES, PROJ_MAIN // LANES), full((1, LANES)), full((1, LANES))],
        out_shape=[jax.ShapeDtypeStruct((t, w3), F32), jax.ShapeDtypeStruct((CONV_K, w3), F32),
                   jax.ShapeDtypeStruct(dp_buf.shape, dp_buf.dtype),
                   jax.ShapeDtypeStruct((1, LANES), F32), jax.ShapeDtypeStruct((1, LANES), F32)],
        input_output_aliases={11: 2},
        scratch_shapes=[pltpu.VMEM((CONV_K * SUBLANES, w3), F32), pltpu.VMEM((2 * SUBLANES, LANES), F32)],
        compiler_params=_params("arbitrary"), name="dn_prep_bwd")(p, p, p, conv_w, a_log4, dt_bias4, dq, dk, dv, dbeta4, dg4, dp_buf)


def _conv_bwd_input(dc, w, name, out_cols=None, col_block=0, into=None):
    t, c = dc.shape
    taps = w.shape[0]
    tm = _token_tile(t)
    ct = _pick(c, 1536)
    n_tok = t // tm
    out_cols = c if out_cols is None else out_cols

    def body(dc_ref, next_ref, w_ref, *rest):
        dx_ref, buf_ref = rest[-2], rest[-1]
        buf_ref[0:tm, :] = dc_ref[...]
        buf_ref[tm:, :] = jnp.where(pl.program_id(0) == n_tok - 1, 0.0, next_ref[...])

        def strip(row0):
            for c0 in range(0, ct, LANES):
                cols = slice(c0, c0 + LANES)
                dx_ref[pl.ds(row0, STRIP), cols] = _advanced_conv(buf_ref, row0, cols, w_ref[:, cols]).astype(BF16)

        _for_strips(tm, STRIP, strip)

    in_specs = [pl.BlockSpec((tm, ct), lambda i, j: (i, j)),
                pl.BlockSpec((HALO, ct), lambda i, j: (jnp.minimum((i + 1) * (tm // HALO), t // HALO - 1), j)),
                pl.BlockSpec((taps, ct), lambda i, j: (0, j))]
    args = (dc, dc, w)
    aliases = {}
    if into is not None:
        in_specs.append(pl.BlockSpec(memory_space=pl.ANY))
        args += (into,)
        aliases = {3: 0}
    return _pcall(
        body, grid=(n_tok, c // ct), in_specs=in_specs,
        out_specs=pl.BlockSpec((tm, ct), lambda i, j: (i, j + col_block)),
        out_shape=jax.ShapeDtypeStruct((t, out_cols), BF16), input_output_aliases=aliases,
        scratch_shapes=[pltpu.VMEM((tm + HALO, ct), F32)],
        compiler_params=_params("parallel", "parallel"), name=name)(*args)


def _dn_forward(q, k, v, beta4, g4, p, norm_g):
    t = q.shape[0]
    n = t // CHUNK
    nc = DN_FWD_CHUNKS
    rows_per_step = nc * CHUNK

    def body(q_ref, k_ref, v_ref, b_ref, g_ref, gate_ref, ng_ref, mix_ref, s_all_ref, ainv_ref, vnew_ref, o_ref, s_ref):
        @pl.when(pl.program_id(0) == 0)
        def _():
            s_ref[...] = jnp.zeros_like(s_ref)

        chunks = []
        for c in range(nc):
            rows = slice(c * CHUNK, (c + 1) * CHUNK)
            chunks.append((_stack_heads(q_ref[rows, :]), _stack_heads(k_ref[rows, :]), _stack_heads(v_ref[rows, :]),
                           _stack_lanes(b_ref[rows, :]), chunk_cumsum(g_ref[rows, :])))
        locs = dn_chunks_local(chunks)
        s = [s_ref[h] for h in range(N_HEADS)]
        for c in range(nc):
            rows = slice(c * CHUNK, (c + 1) * CHUNK)
            ainv_ref[c] = locs[c]["a_inv"].astype(BF16)
            for h in range(N_HEADS):
                s_all_ref[c, h] = s[h]
            o, s = dn_chunk_state(locs[c], s)
            vnew_ref[c] = locs[c]["v_new"].astype(BF16)
            o_ref[c] = o
            o_n, _ = rms_fwd(o, ng_ref[...])
            for h in range(N_HEADS):
                sl = slice(h * HEAD_DIM, (h + 1) * HEAD_DIM)
                mix_ref[rows, sl] = (o_n[_head_rows(h)] * silu(gate_ref[rows, sl])).astype(BF16)
        for h in range(N_HEADS):
            s_ref[h] = s[h]

    ch = lambda w, cb: pl.BlockSpec((rows_per_step, w), lambda i: (i, cb))
    per_chunk = lambda *shape: pl.BlockSpec((nc,) + shape, lambda i: (i,) + (0,) * len(shape))
    return _pcall(
        body, grid=(n // nc,),
        in_specs=[ch(DN_WIDTH, 0)] * 3 + [ch(LANES, 0)] * 2 + [ch(DN_WIDTH, 3), pl.BlockSpec((1, HEAD_DIM), lambda i: (0, 0))],
        out_specs=[ch(DN_WIDTH, 0), per_chunk(N_HEADS, HEAD_DIM, HEAD_DIM), per_chunk(STACK, STACK),
                   per_chunk(STACK, HEAD_DIM), per_chunk(STACK, HEAD_DIM)],
        out_shape=[jax.ShapeDtypeStruct((t, DN_WIDTH + SG_WIDTH), BF16), jax.ShapeDtypeStruct((n, N_HEADS, HEAD_DIM, HEAD_DIM), F32),
                   jax.ShapeDtypeStruct((n, STACK, STACK), BF16), jax.ShapeDtypeStruct((n, STACK, HEAD_DIM), BF16),
                   jax.ShapeDtypeStruct((n, STACK, HEAD_DIM), F32)],
        scratch_shapes=[pltpu.VMEM((N_HEADS, HEAD_DIM, HEAD_DIM), F32)],
        compiler_params=_params("arbitrary"), name="dn_forward")(q, k, v, beta4, g4, p, norm_g)


def _dn_backward(q, k, v, beta4, g4, p, norm_g, saved, dmix, dp_buf):
    t = q.shape[0]
    n = t // CHUNK
    steps = n // DN_CHUNKS
    rows_per_step = DN_CHUNKS * CHUNK

    def body(q_ref, k_ref, v_ref, b_ref, g_ref, gate_ref, ng_ref, s_in_ref, ainv_ref, vnew_ref, o_ref, dmix_ref, _,
             dq_ref, dk_ref, dv_ref, db_ref, dg_ref, dgate_ref, dng_ref, ds_ref):
        @pl.when(pl.program_id(0) == 0)
        def _():
            ds_ref[...] = jnp.zeros_like(ds_ref)
            dng_ref[...] = jnp.zeros_like(dng_ref)

        chunks = []
        for c in range(DN_CHUNKS):
            rows = slice(c * CHUNK, (c + 1) * CHUNK)
            chunks.append((_stack_heads(q_ref[rows, :]), _stack_heads(k_ref[rows, :]), _stack_heads(v_ref[rows, :]),
                           _stack_lanes(b_ref[rows, :]), chunk_cumsum(g_ref[rows, :])))
        items = []
        for c, loc in enumerate(dn_chunks_local(chunks, [ainv_ref[c] for c in range(DN_CHUNKS)])):
            rows = slice(c * CHUNK, (c + 1) * CHUNK)
            s = [s_in_ref[c, h] for h in range(N_HEADS)]
            loc["v_new"] = vnew_ref[c]
            o = o_ref[c]
            o_n, r = rms_fwd(o, ng_ref[...])
            gate = _stack_heads(gate_ref[rows, :])
            dmx = _stack_heads(dmix_ref[rows, :])
            dgate = dmx * o_n * silu_grad(gate)
            do, dng_rows = rms_bwd(o, r, ng_ref[...], dmx * silu(gate))
            dng_ref[...] += jnp.sum(dng_rows, axis=0, keepdims=True)
            for h in range(N_HEADS):
                dgate_ref[rows, h * HEAD_DIM:(h + 1) * HEAD_DIM] = dgate[_head_rows(h)].astype(BF16)
            items.append((*chunks[c][:4], loc, s, do))
        grads, ds = dn_chunks_bwd(items, [ds_ref[h] for h in range(N_HEADS)])
        lane = lax.broadcasted_iota(jnp.int32, (CHUNK, LANES), 1)
        _, strict = _tri_masks(CHUNK)
        for c in range(DN_CHUNKS):
            rows = slice(c * CHUNK, (c + 1) * CHUNK)
            dq, dk, dv, dbeta, dgc = grads[c]
            db4 = jnp.zeros((CHUNK, LANES), F32)
            dgc4 = jnp.zeros((CHUNK, LANES), F32)
            for h in range(N_HEADS):
                sl = slice(h * HEAD_DIM, (h + 1) * HEAD_DIM)
                head_rows = _head_rows(h)
                dq_ref[rows, sl] = dq[head_rows]
                dk_ref[rows, sl] = dk[head_rows]
                dv_ref[rows, sl] = dv[head_rows]
                db4 = jnp.where(lane == h, dbeta[head_rows], db4)
                dgc4 = jnp.where(lane == h, dgc[head_rows], dgc4)
            db_ref[rows, :] = db4
            dg_ref[rows, :] = dot_nn(jnp.logical_not(strict).astype(F32), dgc4)
        for h in range(N_HEADS):
            ds_ref[h] = ds[h]

    rev = lambda w, cb: pl.BlockSpec((rows_per_step, w), lambda i: (steps - 1 - i, cb))
    per_chunk = lambda a: pl.BlockSpec((DN_CHUNKS,) + a.shape[1:], lambda i: (steps - 1 - i,) + (0,) * (a.ndim - 1))
    return _pcall(
        body, grid=(steps,),
        in_specs=[rev(DN_WIDTH, 0)] * 3 + [rev(LANES, 0)] * 2 + [rev(DN_WIDTH, 3), pl.BlockSpec((1, HEAD_DIM), lambda i: (0, 0))]
        + [per_chunk(a) for a in saved] + [rev(DN_WIDTH, 0), pl.BlockSpec(memory_space=pl.ANY)],
        out_specs=[rev(DN_WIDTH, 0)] * 3 + [rev(LANES, 0)] * 2 + [rev(DN_WIDTH, 3), pl.BlockSpec((1, HEAD_DIM), lambda i: (0, 0))],
        out_shape=[jax.ShapeDtypeStruct((t, DN_WIDTH), F32)] * 3 + [jax.ShapeDtypeStruct((t, LANES), F32)] * 2
        + [jax.ShapeDtypeStruct(dp_buf.shape, dp_buf.dtype), jax.ShapeDtypeStruct((1, HEAD_DIM), F32)],
        input_output_aliases={12: 5},
        scratch_shapes=[pltpu.VMEM((N_HEADS, HEAD_DIM, HEAD_DIM), F32)],
        compiler_params=_params("arbitrary"), name="dn_backward")(q, k, v, beta4, g4, p, norm_g, *saved, dmix, dp_buf)


def _sg_mask():
    row = lax.broadcasted_iota(jnp.int32, (SG_BLOCK, SG_BLOCK), 0)
    col = lax.broadcasted_iota(jnp.int32, (SG_BLOCK, SG_BLOCK), 1)
    return (col // CHUNK) <= (row // CHUNK)


def _sg_forward(p, norm_g, w_s, b_t, mix_buf):
    t = p.shape[0]

    def body(u_ref, v_ref, ng_ref, w_ref, b_ref, _, o_ref):
        mask = _sg_mask()
        groups = range(SG_GROUPS)
        sl = [slice(g * SG_DIM, (g + 1) * SG_DIM) for g in groups]
        vn = [rms_fwd(gelu(v_ref[:, sl[g]]), ng_ref[:, sl[g]])[0] for g in groups]
        s = [dot_nn(jnp.where(mask, w_ref[g], 0.0), vn[g], FAST) + b_ref[:, g:g + 1] for g in groups]
        for g in groups:
            o_ref[:, sl[g]] = (gelu(u_ref[:, sl[g]]) * s[g]).astype(BF16)

    blk = lambda cb: pl.BlockSpec((SG_BLOCK, SG_WIDTH), lambda i: (i, cb))
    return _pcall(
        body, grid=(t // SG_BLOCK,),
        in_specs=[blk(4), blk(5), pl.BlockSpec((1, SG_WIDTH), lambda i: (0, 0)),
                  pl.BlockSpec((SG_GROUPS, SG_BLOCK, SG_BLOCK), lambda i: (0, 0, 0)), pl.BlockSpec((SG_BLOCK, SG_GROUPS), lambda i: (0, 0)),
                  pl.BlockSpec(memory_space=pl.ANY)],
        out_specs=blk(1), out_shape=jax.ShapeDtypeStruct(mix_buf.shape, mix_buf.dtype), input_output_aliases={5: 0},
        compiler_params=_params("parallel"), name="sg_forward")(p, p, norm_g, w_s, b_t, mix_buf)


def _sg_backward(p, norm_g, w_s, b_t, dmix):
    t = p.shape[0]

    def body(u_ref, v_ref, ng_ref, w_ref, b_ref, do_ref, duv_ref, dng_ref, dw_ref, db_ref):
        @pl.when(pl.program_id(0) == 0)
        def _():
            dng_ref[...] = jnp.zeros_like(dng_ref)
            dw_ref[...] = jnp.zeros_like(dw_ref)
            db_ref[...] = jnp.zeros_like(db_ref)

        mask = _sg_mask()
        lane = lax.broadcasted_iota(jnp.int32, (SG_BLOCK, LANES), 1)
        groups = range(SG_GROUPS)
        sl = [slice(g * SG_DIM, (g + 1) * SG_DIM) for g in groups]
        w_m = [jnp.where(mask, w_ref[g], 0.0) for g in groups]
        vg = [gelu(v_ref[:, sl[g]]) for g in groups]
        normed = [rms_fwd(vg[g], ng_ref[:, sl[g]]) for g in groups]
        s = [dot_nn(w_m[g], normed[g][0], FAST) + b_ref[:, g:g + 1] for g in groups]
        ds = []
        db = jnp.zeros((SG_BLOCK, LANES), F32)
        for g in groups:
            u_raw, do = u_ref[:, sl[g]], do_ref[:, sl[g]]
            duv_ref[:, sl[g]] = (do * s[g] * gelu_grad(u_raw)).astype(BF16)
            ds.append(do * gelu(u_raw))
            db = jnp.where(lane == g, jnp.sum(ds[g], axis=1, keepdims=True), db)
        dw = [jnp.where(mask, dot_nt(ds[g], normed[g][0], FAST), 0.0) for g in groups]
        dvn = [dot_tn(w_m[g], ds[g], FAST) for g in groups]
        for g in groups:
            dw_ref[g] += dw[g]
            dvg, dng_rows = rms_bwd(vg[g], normed[g][1], ng_ref[:, sl[g]], dvn[g])
            dng_ref[:, sl[g]] += jnp.sum(dng_rows, axis=0, keepdims=True)
            duv_ref[:, SG_WIDTH + g * SG_DIM:SG_WIDTH + (g + 1) * SG_DIM] = (dvg * gelu_grad(v_ref[:, sl[g]])).astype(BF16)
        db_ref[...] += db

    blk = lambda cb: pl.BlockSpec((SG_BLOCK, SG_WIDTH), lambda i: (i, cb))
    const2 = lambda shape: pl.BlockSpec(shape, lambda i: (0, 0))
    w_spec = pl.BlockSpec((SG_GROUPS, SG_BLOCK, SG_BLOCK), lambda i: (0, 0, 0))
    return _pcall(
        body, grid=(t // SG_BLOCK,),
        in_specs=[blk(4), blk(5), const2((1, SG_WIDTH)), w_spec, const2((SG_BLOCK, SG_GROUPS)), blk(1)],
        out_specs=[pl.BlockSpec((SG_BLOCK, 2 * SG_WIDTH), lambda i: (i, 2)), const2((1, SG_WIDTH)), w_spec,
                   const2((SG_BLOCK, LANES))],
        out_shape=[jax.ShapeDtypeStruct((t, PROJ_PAD), BF16), jax.ShapeDtypeStruct((1, SG_WIDTH), F32),
                   jax.ShapeDtypeStruct((SG_GROUPS, SG_BLOCK, SG_BLOCK), F32), jax.ShapeDtypeStruct((SG_BLOCK, LANES), F32)],
        compiler_params=_params("arbitrary"), name="sg_backward")(p, p, norm_g, w_s, b_t, dmix)


FFN_COLS = 256


def _norm_up_ffn(x, g, w_up_t, conv_w, conv_b):
    t, d = x.shape
    tm = min(t, 256)
    blocks = D_FF // FFN_COLS
    nt = (((1,), (1,)), ((), ()))

    def body(x_ref, g_ref, w_ref, cw_ref, cb_ref, up_ref, act_ref, h_ref, r_ref, tail_ref, prev_ref):
        @pl.when(pl.program_id(0) == 0)
        def _():
            tail_ref[...] = jnp.zeros_like(tail_ref)

        y, r = rms_fwd(x_ref[...], g_ref[...])
        h = y.astype(BF16)
        h_ref[...] = h
        r_ref[...] = r

        def project(blk):
            out = []
            for half in range(2):
                cols = slice(half * D_FF + blk * FFN_COLS, half * D_FF + (blk + 1) * FFN_COLS)
                u = lax.dot_general(h, w_ref[cols, :], nt, preferred_element_type=F32)
                up_ref[:, cols] = u
                prev_ref[:, cols] = tail_ref[:, cols]
                tail_ref[:, cols] = u[tm - HALO:, :]
                out.append(cols)
            return out

        def history(row0, cols):
            if row0 == 0:
                return jnp.concatenate([prev_ref[:, cols], up_ref[0:STRIP, cols]], axis=0)
            return up_ref[row0 - HALO:row0 + STRIP, cols]

        def activate(blk, g_cols, v_cols):
            for row0 in range(0, tm, STRIP):
                for c0 in range(0, FFN_COLS, LANES):
                    gc = slice(g_cols.start + c0, g_cols.start + c0 + LANES)
                    vc = slice(v_cols.start + c0, v_cols.start + c0 + LANES)
                    cg = _causal_conv(_delays(history(row0, gc), FFN_CONV), cw_ref[:, gc]) + cb_ref[:, gc]
                    cv = _causal_conv(_delays(history(row0, vc), FFN_CONV), cw_ref[:, vc]) + cb_ref[:, vc]
                    act_ref[row0:row0 + STRIP, blk * FFN_COLS + c0:blk * FFN_COLS + c0 + LANES] = (silu(cg) * cv).astype(BF16)

        pending = None
        for blk in range(blocks):
            cols = project(blk)
            if pending is not None:
                activate(*pending)
            pending = (blk, *cols)
        activate(*pending)

    rows = lambda w: pl.BlockSpec((tm, w), lambda i: (i, 0))
    whole = lambda a: pl.BlockSpec(a.shape, lambda i: (0, 0))
    return _pcall(
        body, grid=(t // tm,),
        in_specs=[rows(d), whole(g), whole(w_up_t), whole(conv_w), whole(conv_b)],
        out_specs=[rows(2 * D_FF), rows(D_FF), rows(d), rows(1)],
        out_shape=[jax.ShapeDtypeStruct((t, 2 * D_FF), F32), jax.ShapeDtypeStruct((t, D_FF), BF16),
                   jax.ShapeDtypeStruct((t, d), BF16), jax.ShapeDtypeStruct((t, 1), F32)],
        scratch_shapes=[pltpu.VMEM((HALO, 2 * D_FF), F32), pltpu.VMEM((HALO, 2 * D_FF), F32)],
        compiler_params=_params("arbitrary"), name="norm_up_ffn")(x, g, w_up_t, conv_w, conv_b)


def _ffn_bwd(up, conv_w, conv_b, dact):
    t = up.shape[0]
    tm = _pick(t, 128)
    n_tok = t // tm
    width = 2 * D_FF

    def dconv(delayed_g, delayed_v, da, wg, wv, bg, bv):
        cg = _causal_conv(delayed_g, wg) + bg
        cv = _causal_conv(delayed_v, wv) + bv
        s = sigmoid(cg)
        return da * cv * (s * (1.0 + cg * (1.0 - s))), da * (cg * s)

    def body(up_ref, prev_ref, next_ref, da_ref, dan_ref, w_ref, b_ref, dup_ref, dw_ref, db_ref, dc_ref, dw_acc, db_acc):
        first = pl.program_id(0) == 0
        last = pl.program_id(0) == n_tok - 1
        dw_acc[...] = jnp.zeros_like(dw_acc)
        db_acc[...] = jnp.zeros_like(db_acc)

        def strip(row0):
            rows = pl.ds(row0, STRIP)
            for c0 in range(0, D_FF, LANES):
                gc, vc = slice(c0, c0 + LANES), slice(D_FF + c0, D_FF + c0 + LANES)
                del_g = _delays(_history(up_ref, prev_ref, first, row0, gc), FFN_CONV)
                del_v = _delays(_history(up_ref, prev_ref, first, row0, vc), FFN_CONV)
                dcg, dcv = dconv(del_g, del_v, da_ref[rows, gc], w_ref[:, gc], w_ref[:, vc], b_ref[:, gc], b_ref[:, vc])
                dc_ref[rows, gc] = dcg
                dc_ref[rows, vc] = dcv
                db_acc[:, gc] += _fold_rows(dcg)
                db_acc[:, vc] += _fold_rows(dcv)
                for j in range(FFN_CONV):
                    k = FFN_CONV - 1 - j
                    dw_acc[k * SUBLANES:(k + 1) * SUBLANES, gc] += _fold_rows(dcg * del_g[j])
                    dw_acc[k * SUBLANES:(k + 1) * SUBLANES, vc] += _fold_rows(dcv * del_v[j])

        _first_then_strips(tm, strip)

        for c0 in range(0, D_FF, LANES):
            gc, vc = slice(c0, c0 + LANES), slice(D_FF + c0, D_FF + c0 + LANES)

            def delayed(cols):
                return _delays(jnp.concatenate([up_ref[tm - HALO:tm, cols], next_ref[:, cols]], axis=0), FFN_CONV)

            dcg, dcv = dconv(delayed(gc), delayed(vc), dan_ref[:, gc], w_ref[:, gc], w_ref[:, vc], b_ref[:, gc], b_ref[:, vc])
            dc_ref[tm:, gc] = jnp.where(last, 0.0, dcg)
            dc_ref[tm:, vc] = jnp.where(last, 0.0, dcv)

        def strip_dx(row0):
            for c0 in range(0, width, LANES):
                cols = slice(c0, c0 + LANES)
                dup_ref[pl.ds(row0, STRIP), cols] = _advanced_conv(dc_ref, row0, cols, w_ref[:, cols]).astype(BF16)

        _for_strips(tm, STRIP, strip_dx)

        @pl.when(first)
        def _():
            dw_ref[...] = jnp.zeros_like(dw_ref)
            db_ref[...] = jnp.zeros_like(db_ref)

        for k in range(FFN_CONV):
            dw_ref[k:k + 1, :] += jnp.sum(dw_acc[k * SUBLANES:(k + 1) * SUBLANES, :], axis=0, keepdims=True)
        db_ref[...] += jnp.sum(db_acc[...], axis=0, keepdims=True)

    next_rows = lambda i: jnp.minimum((i + 1) * (tm // HALO), t // HALO - 1)
    full = lambda rows: pl.BlockSpec((rows, width), lambda i: (0, 0))
    return _pcall(
        body, grid=(n_tok,),
        in_specs=[pl.BlockSpec((tm, width), lambda i: (i, 0)),
                  pl.BlockSpec((HALO, width), lambda i: (jnp.maximum(i * (tm // HALO) - 1, 0), 0)),
                  pl.BlockSpec((HALO, width), lambda i: (next_rows(i), 0)),
                  pl.BlockSpec((tm, D_FF), lambda i: (i, 0)), pl.BlockSpec((HALO, D_FF), lambda i: (next_rows(i), 0)),
                  full(FFN_CONV), full(1)],
        out_specs=[pl.BlockSpec((tm, width), lambda i: (i, 0)), full(FFN_CONV), full(1)],
        out_shape=[jax.ShapeDtypeStruct((t, width), BF16), jax.ShapeDtypeStruct((FFN_CONV, width), F32),
                   jax.ShapeDtypeStruct((1, width), F32)],
        scratch_shapes=[pltpu.VMEM((tm + HALO, width), F32),
                        pltpu.VMEM((FFN_CONV * SUBLANES, width), F32), pltpu.VMEM((SUBLANES, width), F32)],
        compiler_params=_params("arbitrary"), name="ffn_bwd")(up, up, up, dact, dact, conv_w, conv_b)


def _my_position():
    return lax.axis_index("x"), lax.axis_index("y"), lax.axis_index("c")


COPIES = N_DEV - 1


def _all_gather(arrays):
    n = len(arrays)

    def body(*refs):
        x_refs, out_refs = refs[:n], refs[n:2 * n]
        send_sems, recv_sems, local_sems = refs[2 * n:]
        x, y, cc = _my_position()
        me, sibling = (x, y, cc), (x, y, 1 - cc)
        chips = [(1 - x, y), (x, 1 - y), (1 - x, 1 - y)]

        def block(a, px, py, pc):
            return out_refs[a].at[4 * px + 2 * py + pc]

        def copy(a, k, blk, to, src=None):
            return pltpu.make_async_remote_copy(
                src_ref=block(a, *blk) if src is None else src, dst_ref=block(a, *blk),
                send_sem=send_sems.at[a * COPIES + k], recv_sem=recv_sems.at[a * COPIES + k],
                device_id=to, device_id_type=MESH_ID)

        mine = [pltpu.make_async_copy(x_refs[a], block(a, *me), local_sems.at[a]) for a in range(n)]
        for cp in mine:
            cp.start()
        first = []
        for a in range(n):
            first.append(copy(a, 0, me, sibling, src=x_refs[a]))
            first += [copy(a, 1 + j, me, (*chip, cc), src=x_refs[a]) for j, chip in enumerate(chips)]
        for cp in first:
            cp.start()
        passed = []
        for j, chip in enumerate(chips):
            for a in range(n):
                copy(a, 1 + j, (*chip, cc), me).wait_recv()
                passed.append(copy(a, 4 + j, (*chip, cc), sibling))
                passed[-1].start()
        for a in range(n):
            copy(a, 0, sibling, me).wait_recv()
        for j, chip in enumerate(chips):
            for a in range(n):
                copy(a, 4 + j, (*chip, 1 - cc), me).wait_recv()
        for cp in first + passed:
            cp.wait_send()
        for cp in mine:
            cp.wait()

    any_spec = pl.BlockSpec(memory_space=pl.ANY)
    return _pcall(
        body, out_shape=[jax.ShapeDtypeStruct((N_DEV,) + a.shape, a.dtype) for a in arrays],
        in_specs=[any_spec] * n, out_specs=[any_spec] * n,
        scratch_shapes=[pltpu.SemaphoreType.DMA((n * COPIES,)), pltpu.SemaphoreType.DMA((n * COPIES,)),
                        pltpu.SemaphoreType.DMA((n,))],
        name="all_gather")(*arrays)


def _all_to_all(sends):
    n = len(sends)

    def body(*refs):
        send_refs, recv_refs = refs[:n], refs[n:2 * n]
        send_sems, recv_sems, local_sems = refs[2 * n:]
        x, y, cc = _my_position()
        me = 4 * x + 2 * y + cc
        mine = [pltpu.make_async_copy(send_refs[a].at[me], recv_refs[a].at[me], local_sems.at[a]) for a in range(n)]
        for cp in mine:
            cp.start()
        copies = []
        for rel in range(1, N_DEV):
            px, py, pc = x ^ (rel >> 2), y ^ ((rel >> 1) & 1), cc ^ (rel & 1)
            for a in range(n):
                copies.append(pltpu.make_async_remote_copy(
                    src_ref=send_refs[a].at[4 * px + 2 * py + pc], dst_ref=recv_refs[a].at[me],
                    send_sem=send_sems.at[a * COPIES + rel - 1], recv_sem=recv_sems.at[a * COPIES + rel - 1],
                    device_id=(px, py, pc), device_id_type=MESH_ID))
        for cp in copies:
            cp.start()
        for cp in copies:
            cp.wait()
        for cp in mine:
            cp.wait()

    any_spec = pl.BlockSpec(memory_space=pl.ANY)
    return _pcall(
        body, out_shape=[jax.ShapeDtypeStruct(s.shape, s.dtype) for s in sends],
        in_specs=[any_spec] * n, out_specs=[any_spec] * n,
        scratch_shapes=[pltpu.SemaphoreType.DMA((n * COPIES,)), pltpu.SemaphoreType.DMA((n * COPIES,)),
                        pltpu.SemaphoreType.DMA((n,))],
        name="all_to_all")(*sends)


def _hbm(a):
    return pltpu.with_memory_space_constraint(a, pltpu.HBM)


def _split_copies(send_refs, land_refs, send_sems, recv_sems, local_sems, gather):
    x, y, cc = _my_position()
    me = 4 * x + 2 * y + cc
    local, remote = [], []
    for a, (send, land) in enumerate(zip(send_refs, land_refs)):
        local.append(pltpu.make_async_copy(send if gather else send.at[me], land.at[me], local_sems.at[a]))
    for a, (send, land) in enumerate(zip(send_refs, land_refs)):
        for rel in range(1, N_DEV):
            px, py, pc = x ^ (rel >> 2), y ^ ((rel >> 1) & 1), cc ^ (rel & 1)
            remote.append(pltpu.make_async_remote_copy(
                src_ref=send if gather else send.at[4 * px + 2 * py + pc], dst_ref=land.at[me],
                send_sem=send_sems.at[a * COPIES + rel - 1], recv_sem=recv_sems.at[a * COPIES + rel - 1],
                device_id=(px, py, pc), device_id_type=MESH_ID))
    return local, remote


SPLIT_EFFECT = pltpu.SideEffectType.DATAFLOW_SIDE_EFFECTING


def _exchange_start(sends, after, gather, name):
    n = len(sends)
    lands = [_hbm(lax.empty((N_DEV,) + s.shape if gather else s.shape, s.dtype)) for s in sends]

    def body(*refs):
        send_refs, land_refs = refs[:n], refs[n:2 * n]
        send_sems, recv_sems, local_sems = refs[2 * n + 1:2 * n + 4]
        token = refs[-1]
        local, remote = _split_copies(send_refs, land_refs, send_sems, recv_sems, local_sems, gather)
        for cp in local + remote:
            cp.start()
        token[...] = jnp.zeros_like(token)

    hbm, sem = pl.BlockSpec(memory_space=pltpu.HBM), pl.BlockSpec(memory_space=pltpu.SEMAPHORE)
    out = _pcall(
        body, name=name,
        out_shape=[pltpu.SemaphoreType.DMA((n * COPIES,)), pltpu.SemaphoreType.DMA((n * COPIES,)), pltpu.SemaphoreType.DMA((n,))]
        + [pltpu.HBM(s.shape, s.dtype) for s in sends] + [pltpu.HBM(z.shape, z.dtype) for z in lands]
        + [jax.ShapeDtypeStruct((SUBLANES, LANES), F32)],
        in_specs=[hbm] * (2 * n) + [pl.BlockSpec(memory_space=pl.ANY)],
        out_specs=[sem] * 3 + [hbm] * (2 * n) + [pl.BlockSpec(memory_space=pltpu.VMEM)],
        input_output_aliases={i: 3 + i for i in range(2 * n)},
        compiler_params=pltpu.CompilerParams(has_side_effects=SPLIT_EFFECT),
    )(*[_hbm(s) for s in sends], *lands, after)
    return dict(sems=out[:3], sends=out[3:3 + n], lands=out[3 + n:3 + 2 * n], gather=gather), out[-1]


def _exchange_wait(handle, after, name):
    sends, lands, gather = handle["sends"], handle["lands"], handle["gather"]
    n = len(sends)

    def body(*refs):
        send_refs, land_refs = refs[:n], refs[n:2 * n]
        send_sems, recv_sems, local_sems = refs[2 * n:2 * n + 3]
        local, remote = _split_copies(send_refs, land_refs, send_sems, recv_sems, local_sems, gather)
        for cp in remote:
            cp.wait_send()
            cp.wait_recv()
        for cp in local:
            cp.wait()

    hbm, sem = pl.BlockSpec(memory_space=pltpu.HBM), pl.BlockSpec(memory_space=pltpu.SEMAPHORE)
    out = _pcall(
        body, name=name,
        out_shape=[pltpu.HBM(s.shape, s.dtype) for s in sends] + [pltpu.HBM(z.shape, z.dtype) for z in lands],
        in_specs=[hbm] * (2 * n) + [sem] * 3 + [pl.BlockSpec(memory_space=pl.ANY)],
        out_specs=[hbm] * (2 * n), input_output_aliases={i: i for i in range(2 * n)},
        compiler_params=pltpu.CompilerParams(has_side_effects=SPLIT_EFFECT),
    )(*sends, *lands, *handle["sems"], after)
    return out[n:]


def _sum_and_adamw(recv, w, m, v, name):
    _, r, wp = recv.shape
    c = w.shape[-1]
    lead = w.ndim == 3
    tr = SLAB_ROW_TILE if r % SLAB_ROW_TILE == 0 else (SLAB_ROW_TILE // 4 if r % (SLAB_ROW_TILE // 4) == 0 else r)
    bc1 = 1.0 - ADAM_B1 ** ADAM_STEP
    bc2 = 1.0 - ADAM_B2 ** ADAM_STEP

    def body(recv_ref, w_ref, m_ref, v_ref, g_ref, d_ref, nm_ref, nv_ref):
        g = recv_ref[0, :, 0:c].astype(F32)
        for s in range(1, N_DEV):
            g = g + recv_ref[s, :, 0:c].astype(F32)
        m_new = ADAM_B1 * m_ref[...] + (1.0 - ADAM_B1) * g
        v_new = ADAM_B2 * v_ref[...] + (1.0 - ADAM_B2) * (g * g)
        m_hat = m_new / bc1
        v_hat = v_new / bc2
        g_ref[...] = g
        d_ref[...] = -ADAM_LR * (m_hat / (jnp.sqrt(v_hat) + ADAM_EPS) + ADAM_WD * w_ref[...])
        nm_ref[...] = m_new
        nv_ref[...] = v_new

    tile = pl.BlockSpec((None, tr, c), lambda i: (0, i, 0)) if lead else pl.BlockSpec((tr, c), lambda i: (i, 0))
    return _pcall(
        body, grid=(r // tr,),
        in_specs=[pl.BlockSpec((N_DEV, tr, wp), lambda i: (0, i, 0)), tile, tile, tile],
        out_specs=[tile] * 4, out_shape=[jax.ShapeDtypeStruct(w.shape, F32)] * 4,
        compiler_params=_params("parallel"), name=name)(recv, w, m, v)


SHARDED_TAPS = ("dn_conv_w", "ffn_conv_w")
REPLICATED = ("attn_norm_g", "dn_a_log", "dn_dt_bias", "dn_out_norm_g", "sg_norm_g", "sg_w", "sg_b", "ffn_norm_g",
              "ffn_conv_b", "final_norm_g")
SMALL = SHARDED_TAPS + REPLICATED
WEIGHT_ORDER = ("attn_norm_g", "w_in", "dn_conv_w", "dn_a_log", "dn_dt_bias", "dn_out_norm_g", "sg_norm_g", "sg_w", "sg_b",
                "w_out", "ffn_norm_g", "w_up", "ffn_conv_w", "ffn_conv_b", "w_down", "final_norm_g")
SLAB_COLS = 1024
SLAB_ROW_TILE = 128


def _pad_to(flat, multiple):
    pad = (-flat.shape[-1]) % multiple
    if pad == 0:
        return flat
    return jnp.pad(flat, [(0, 0)] * (flat.ndim - 1) + [(0, pad)])


def _pack_small(named):
    flat = jnp.concatenate([named[n].reshape(-1) for n in SMALL])
    return _pad_to(flat, SUBLANES * SLAB_COLS).reshape(-1, SLAB_COLS)


def _unpack_small(slab, like):
    flat = slab.reshape(-1)
    out, off = {}, 0
    for n in SMALL:
        size = like[n].size
        out[n] = flat[off:off + size].reshape(like[n].shape)
        off += size
    return out


def _split_columns(full, n_local):
    r = full.shape[0]
    return full.reshape(r, N_DEV, n_local).transpose(1, 0, 2).reshape(N_DEV, r * n_local)


def _join_columns(blocks, r, n_local):
    return blocks.reshape(N_DEV, r, n_local).transpose(1, 0, 2).reshape(r, N_DEV * n_local)


def _lanes4(a):
    return jnp.pad(a.reshape(1, N_HEADS), ((0, 0), (0, LANES - N_HEADS)))


def kernel(x, attn_norm_g, w_in, dn_conv_w, dn_a_log, dn_dt_bias, dn_out_norm_g, sg_norm_g, sg_w, sg_b, w_out, ffn_norm_g, w_up, ffn_conv_w, ffn_conv_b, w_down, final_norm_g, loss_target, m_attn_norm_g, m_w_in, m_dn_conv_w, m_dn_a_log, m_dn_dt_bias, m_dn_out_norm_g, m_sg_norm_g, m_sg_w, m_sg_b, m_w_out, m_ffn_norm_g, m_w_up, m_ffn_conv_w, m_ffn_conv_b, m_w_down, m_final_norm_g, v_attn_norm_g, v_w_in, v_dn_conv_w, v_dn_a_log, v_dn_dt_bias, v_dn_out_norm_g, v_sg_norm_g, v_sg_w, v_sg_b, v_w_out, v_ffn_norm_g, v_w_up, v_ffn_conv_w, v_ffn_conv_b, v_w_down, v_final_norm_g):
    weights = dict(attn_norm_g=attn_norm_g, w_in=w_in, dn_conv_w=dn_conv_w, dn_a_log=dn_a_log, dn_dt_bias=dn_dt_bias,
                   dn_out_norm_g=dn_out_norm_g, sg_norm_g=sg_norm_g, sg_w=sg_w, sg_b=sg_b, w_out=w_out, ffn_norm_g=ffn_norm_g,
                   w_up=w_up, ffn_conv_w=ffn_conv_w, ffn_conv_b=ffn_conv_b, w_down=w_down, final_norm_g=final_norm_g)
    m_in = dict(attn_norm_g=m_attn_norm_g, w_in=m_w_in, dn_conv_w=m_dn_conv_w, dn_a_log=m_dn_a_log, dn_dt_bias=m_dn_dt_bias,
                dn_out_norm_g=m_dn_out_norm_g, sg_norm_g=m_sg_norm_g, sg_w=m_sg_w, sg_b=m_sg_b, w_out=m_w_out,
                ffn_norm_g=m_ffn_norm_g, w_up=m_w_up, ffn_conv_w=m_ffn_conv_w, ffn_conv_b=m_ffn_conv_b, w_down=m_w_down,
                final_norm_g=m_final_norm_g)
    v_in = dict(attn_norm_g=v_attn_norm_g, w_in=v_w_in, dn_conv_w=v_dn_conv_w, dn_a_log=v_dn_a_log, dn_dt_bias=v_dn_dt_bias,
                dn_out_norm_g=v_dn_out_norm_g, sg_norm_g=v_sg_norm_g, sg_w=v_sg_w, sg_b=v_sg_b, w_out=v_w_out,
                ffn_norm_g=v_ffn_norm_g, w_up=v_w_up, ffn_conv_w=v_ffn_conv_w, ffn_conv_b=v_ffn_conv_b, w_down=v_w_down,
                final_norm_g=v_final_norm_g)

    n_in, n_up = w_in.shape[2], w_up.shape[2]
    r_out, r_down = w_out.shape[1], w_down.shape[1]
    n_dnc, n_ffc = dn_conv_w.shape[2], ffn_conv_w.shape[2]
    transposed = lambda a: jnp.transpose(a, (0, 2, 1))
    taps = _pad_to(jnp.concatenate([dn_conv_w.reshape(-1), ffn_conv_w.reshape(-1)]), SUBLANES * LANES).reshape(-1, LANES)
    g_in, g_taps = _all_gather([transposed(w_in)[0].astype(BF16), taps])
    gather_out, token = _exchange_start([w_out[0].astype(BF16)], g_taps, True, "gather_w_out")
    gather_up, token = _exchange_start([transposed(w_up)[0].astype(BF16)], token, True, "gather_w_up")
    gather_down, token = _exchange_start([w_down[0].astype(BF16)], token, True, "gather_w_down")
    w_in_t = jnp.pad(g_in.reshape(N_DEV * n_in, D_MODEL), ((0, PROJ_PAD - N_DEV * n_in), (0, 0)))
    taps_all = g_taps.reshape(N_DEV, -1)
    dn_conv_full = _join_columns(taps_all[:, :CONV_K * n_dnc], CONV_K, n_dnc)
    ffn_conv_full = _join_columns(taps_all[:, CONV_K * n_dnc:CONV_K * n_dnc + FFN_CONV * n_ffc], FFN_CONV, n_ffc)
    late = dict(
        w_out=lambda after: _exchange_wait(gather_out, after, "gather_w_out_wait")[0].reshape(N_DEV * r_out, D_MODEL),
        w_up_t=lambda after: _exchange_wait(gather_up, after, "gather_w_up_wait")[0].reshape(N_DEV * n_up, D_MODEL),
        w_down=lambda after: _exchange_wait(gather_down, after, "gather_w_down_wait")[0].reshape(N_DEV * r_down, D_MODEL))

    def send_early(blocks, after, name):
        return _exchange_start(blocks, after, False, name)

    def send_small(g, loss_lanes, after):
        small = jnp.concatenate([g[n].reshape(-1) for n in REPLICATED] + [loss_lanes[0, 0:1]])
        slab = jnp.concatenate([_split_columns(g["dn_conv_w"], n_dnc), _split_columns(g["ffn_conv_w"], n_ffc),
                                jnp.broadcast_to(small[None, :], (N_DEV, small.shape[0]))], axis=1)
        return send_early([_pad_to(slab, SUBLANES * SLAB_COLS).reshape(N_DEV, -1, SLAB_COLS)], after, "send_small")

    upd = {}

    def update_early(sent_down, sent_up_out, sent_small, after):
        r_dn, = _exchange_wait(sent_down, after, "send_dw_down_wait")
        r_up, r_o = _exchange_wait(sent_up_out, after, "send_dw_up_out_wait")
        r_small, = _exchange_wait(sent_small, after, "send_small_wait")
        upd["w_down"] = _sum_and_adamw(r_dn, w_down, m_w_down, v_w_down, "adamw_w_down")
        upd["w_up"] = [transposed(o) for o in _sum_and_adamw(r_up, transposed(w_up), transposed(m_w_up), transposed(v_w_up),
                                                             "adamw_w_up")]
        upd["w_out"] = _sum_and_adamw(r_o, w_out, m_w_out, v_w_out, "adamw_w_out")
        upd["small"] = _sum_and_adamw(r_small, _pack_small(weights), _pack_small(m_in), _pack_small(v_in), "adamw_small")

    grad_x, d_g1, sent_in = _local_step(
        x[0], loss_target[0], w_in_t, late, send_early, send_small, update_early, dn_conv_full, ffn_conv_full,
        attn_norm_g + token[0:1, 0:1], dn_a_log, dn_dt_bias, dn_out_norm_g, sg_norm_g, sg_w, sg_b, ffn_norm_g, ffn_conv_b,
        final_norm_g, n_in)

    norm_rows = D_MODEL // LANES
    r_g1, = _all_to_all([jnp.broadcast_to(d_g1.reshape(1, norm_rows, LANES), (N_DEV, norm_rows, LANES))])
    r_in, = _exchange_wait(sent_in, r_g1, "send_dw_in_wait")
    upd["w_in"] = [transposed(o) for o in _sum_and_adamw(r_in, transposed(w_in), transposed(m_w_in), transposed(v_w_in),
                                                         "adamw_w_in")]
    small_upd = upd.pop("small")
    as_rows = lambda a: a.reshape(norm_rows, LANES)
    norm_upd = _sum_and_adamw(r_g1, as_rows(attn_norm_g), as_rows(m_attn_norm_g), as_rows(v_attn_norm_g), "adamw_attn_norm")
    results = []
    for i in range(4):
        named = _unpack_small(small_upd[i], weights)
        named.update({n: upd[n][i] for n in upd})
        named["attn_norm_g"] = norm_upd[i].reshape(attn_norm_g.shape)
        results.append(named)

    loss = small_upd[0].reshape(-1)[sum(weights[n].size for n in SMALL)]
    return (loss, grad_x[None], *[r[n] for r in results for n in WEIGHT_ORDER])


def _local_step(x2d, tgt, w_in_t, late, send_early, send_small, update_early, dn_conv_full, ffn_conv_full, attn_norm_g,
                dn_a_log, dn_dt_bias, dn_out_norm_g, sg_norm_g, sg_w, sg_b, ffn_norm_g, ffn_conv_b, final_norm_g, n_in):
    t = x2d.shape[0]
    g1, g2, gf = attn_norm_g, ffn_norm_g, final_norm_g.reshape(1, D_MODEL)
    a_log4, dt_bias4 = _lanes4(dn_a_log), _lanes4(dn_dt_bias)
    sg_w3 = sg_w[0]
    sg_b_t = sg_b[0].T
    conv_b = ffn_conv_b

    p, h1, rstd1 = _rmsnorm_matmul(x2d, g1, w_in_t, "norm_in_proj", 512)
    q, k, v, beta4, g4 = _dn_prep(p, dn_conv_full, a_log4, dt_bias4)
    mix_half, *dn_saved = _dn_forward(q, k, v, beta4, g4, p, dn_out_norm_g)
    mix = _sg_forward(p, sg_norm_g, sg_w3, sg_b_t, mix_half)
    w_out_full = late["w_out"](mix)
    x2 = _matmul(mix, w_out_full, "nn", "out_proj", (1024, 1024, 1024), add=x2d)
    w_up_t = late["w_up_t"](x2)
    up, act, h2, rstd2 = _norm_up_ffn(x2, g2, w_up_t, ffn_conv_full, conv_b)
    w_down_full = late["w_down"](act)
    fn, outs = _final_loss_rows(t, D_MODEL)
    loss_lanes, dx3, dx3b, d_gf = _matmul_rows(act, w_down_full, "nn", "down_proj_loss", 512,
                                               [(x2, "rows"), (tgt, "rows"), (gf, "whole")], outs, fn)

    dact = _matmul(dx3b, w_down_full, "nt", "down_proj_dx", (512, D_FF, D_MODEL))
    d_w_down = _matmul(act, dx3b, "tn", "down_proj_dw", (256, 1024, t), out_dtype=BF16)
    sent_down, token = send_early([d_w_down.reshape(N_DEV, D_FF // N_DEV, D_MODEL)], d_w_down, "send_dw_down")
    dup, d_ffn_conv, d_ffn_conv_b = _ffn_bwd(up, ffn_conv_full, conv_b + token[0:1, 0:1], dact)
    fn, outs = _rmsnorm_bwd_rows(t, D_MODEL)
    dx2, dx2b, d_g2 = _matmul_rows(dup, w_up_t, "nn", "up_proj_dx_norm", 256,
                                   [(x2, "rows"), (rstd2, "rows"), (g2, "whole"), (dx3, "rows")], outs, fn)
    d_w_up_t = _matmul(dup, h2, "tn", "up_proj_dw", (512, 1024, t), out_dtype=BF16)
    dmix = _matmul(dx2b, w_out_full, "nt", "out_proj_dx", (1024, 1024, 1024))
    d_w_out = _matmul(mix, dx2b, "tn", "out_proj_dw", (512, 1024, t), out_dtype=BF16)
    sent_up_out, token = send_early(
        [d_w_up_t.reshape(N_DEV, 2 * D_FF // N_DEV, D_MODEL), d_w_out.reshape(N_DEV, D_MODEL // N_DEV, D_MODEL)],
        d_w_out, "send_dw_up_out")
    dp, d_sg_norm, d_sg_w, d_sg_b_t = _sg_backward(p, sg_norm_g + token[0:1, 0:1], sg_w3, sg_b_t, dmix)
    dq, dk, dv, dbeta4, dg4, dp, d_dn_norm = _dn_backward(q, k, v, beta4, g4, p, dn_out_norm_g, dn_saved, dmix, dp)
    dc_dn, d_dn_conv, dp, d_a_log4, d_dt_bias4 = _dn_prep_bwd(p, dn_conv_full, a_log4, dt_bias4, dq, dk, dv, dbeta4, dg4, dp)
    small_grads = dict(
        attn_norm_g=jnp.zeros_like(attn_norm_g), dn_conv_w=d_dn_conv, dn_a_log=d_a_log4[:, :N_HEADS],
        dn_dt_bias=d_dt_bias4[:, :N_HEADS], dn_out_norm_g=d_dn_norm, sg_norm_g=d_sg_norm, sg_w=d_sg_w,
        sg_b=d_sg_b_t[:, :SG_GROUPS].T, ffn_norm_g=d_g2, ffn_conv_w=d_ffn_conv, ffn_conv_b=d_ffn_conv_b, final_norm_g=d_gf)
    sent_small, token = send_small(small_grads, loss_lanes, d_dn_conv)
    dp = _conv_bwd_input(dc_dn, dn_conv_full + token[0:1, 0:1], "dn_conv_dx", out_cols=PROJ_PAD, into=dp)
    d_w_in_t = _matmul(dp, h1, "tn", "in_proj_dw", (PROJ_PAD // 5, 1024, t), out_dtype=BF16)
    sent_in, token = send_early([d_w_in_t[:N_DEV * n_in].reshape(N_DEV, n_in, D_MODEL)], d_w_in_t, "send_dw_in")
    update_early(sent_down, sent_up_out, sent_small, token)
    fn, outs = _rmsnorm_bwd_rows(t, D_MODEL)
    grad_x, _, d_g1 = _matmul_rows(dp, w_in_t, "nn", "in_proj_dx_norm", 512,
                                   [(x2d, "rows"), (rstd1, "rows"), (g1 + token[0:1, 0:1], "whole"), (dx2, "rows")], outs, fn)

    return grad_x, d_g1, sent_in
```

```python
import math

import jax
import jax.numpy as jnp
from jax import lax
from jax.experimental import pallas as pl
from jax.experimental.pallas import tpu as pltpu

F32 = jnp.float32
BF16 = jnp.bfloat16
HI = lax.Precision.HIGHEST

D_MODEL = 1024
DN_WIDTH = 512
HEAD_DIM = 128
N_HEADS = 4
SG_WIDTH = 512
SG_GROUPS = 4
SG_DIM = 128
SG_BLOCK = 128
D_FF = 2816
CHUNK = 64
CONV_K = 4
FFN_CONV = 3
EPS = 1e-6
PROJ_MAIN = 3072
PROJ_PAD = 3200
GELU_C = math.sqrt(2.0 / math.pi)
N_DEV = 8
LANES = 128
SUBLANES = 8
HALO = SUBLANES
VMEM_LIMIT = 48 * 1024 * 1024

ADAM_LR = 0.001
ADAM_B1 = 0.9
ADAM_B2 = 0.999
ADAM_EPS = 1e-08
ADAM_WD = 0.01
ADAM_STEP = 10

MESH_ID = pl.DeviceIdType.MESH


def _pcall(body, **kw):
    return pl.pallas_call(body, **kw)


def _params(*sem):
    return pltpu.CompilerParams(dimension_semantics=sem, vmem_limit_bytes=VMEM_LIMIT)


def _pick(n, cap):
    best = None
    for t in range(LANES, cap + 1, LANES):
        if n % t == 0:
            best = t
    return best if best else n


FAST, EXACT = "bf16 operands, one pass", "f32 operands, six bf16 passes"


def dot_f32(a, b, dims, tier):
    if tier == FAST:
        return lax.dot_general(a.astype(BF16), b.astype(BF16), dims, preferred_element_type=F32)
    return lax.dot_general(a, b, dims, precision=HI, preferred_element_type=F32)


def dot_nn(a, b, tier=EXACT):
    return dot_f32(a, b, (((1,), (0,)), ((), ())), tier)


def dot_nt(a, b, tier=EXACT):
    return dot_f32(a, b, (((1,), (1,)), ((), ())), tier)


def dot_tn(a, b, tier=EXACT):
    return dot_f32(a, b, (((0,), (0,)), ((), ())), tier)


def sigmoid(x):
    return 0.5 * jnp.tanh(0.5 * x) + 0.5


def silu(x):
    return x * sigmoid(x)


def silu_grad(x):
    s = sigmoid(x)
    return s * (1.0 + x * (1.0 - s))


def gelu(x):
    return 0.5 * x * (1.0 + jnp.tanh(GELU_C * (x + 0.044715 * x * x * x)))


def gelu_grad(x):
    t = jnp.tanh(GELU_C * (x + 0.044715 * x * x * x))
    return 0.5 * (1.0 + t) + 0.5 * x * (1.0 - t * t) * GELU_C * (1.0 + 3.0 * 0.044715 * x * x)


def softplus(z):
    return jnp.maximum(z, 0.0) + jnp.log(1.0 + jnp.exp(-jnp.abs(z)))


def rms_fwd(x, g):
    r = lax.rsqrt(jnp.mean(x * x, axis=-1, keepdims=True) + EPS)
    return x * r * g, r


def rms_bwd(x, r, g, dy):
    dyg = dy * g
    xr = x * r
    dx = r * (dyg - xr * jnp.mean(dyg * xr, axis=-1, keepdims=True))
    return dx, dy * xr


def l2_fwd(x):
    r = lax.rsqrt(jnp.sum(x * x, axis=-1, keepdims=True) + EPS)
    return x * r, r


def l2_bwd(x, r, dy):
    xr = x * r
    return r * (dy - xr * jnp.sum(dy * xr, axis=-1, keepdims=True))


def _tri_masks(n):
    row = lax.broadcasted_iota(jnp.int32, (n, n), 0)
    col = lax.broadcasted_iota(jnp.int32, (n, n), 1)
    return row >= col, row > col


def chunk_cumsum(g4):
    incl, _ = _tri_masks(g4.shape[0])
    return dot_nn(incl.astype(F32), g4)


STACK = N_HEADS * CHUNK
DN_FWD_CHUNKS = 8
DN_CHUNKS = 4


def _head_rows(h):
    return slice(h * CHUNK, (h + 1) * CHUNK)


def _stack_heads(x):
    return jnp.concatenate([x[:, h * HEAD_DIM:(h + 1) * HEAD_DIM] for h in range(N_HEADS)], axis=0)


def _stack_lanes(x4):
    return jnp.concatenate([x4[:, h:h + 1] for h in range(N_HEADS)], axis=0)


def _per_head(fn):
    return jnp.concatenate([fn(h) for h in range(N_HEADS)], axis=0)


def _unit_lower_inverses(l_strict, order):
    c = l_strict[0].shape[0]
    row = lax.broadcasted_iota(jnp.int32, (c, c), 0)
    col = lax.broadcasted_iota(jnp.int32, (c, c), 1)
    eye = (row == col).astype(F32)
    p = [-l for l in l_strict]
    a = [eye + n for n in p]
    for _ in range(int(math.log2(order)) - 1):
        p = [dot_nn(x, x, FAST) for x in p]
        a = [x + dot_nn(x, y, FAST) for x, y in zip(a, p)]
    return a


def dn_chunks_local(chunks, inverses=None):
    row = lax.broadcasted_iota(jnp.int32, (STACK, STACK), 0)
    col = lax.broadcasted_iota(jnp.int32, (STACK, STACK), 1)
    same = (row // CHUNK) == (col // CHUNK)
    incl = jnp.logical_and(same, row >= col)
    strict = jnp.logical_and(same, row > col)
    locs = []
    for q, k, v, beta, gc4 in chunks:
        gc_col = _stack_lanes(gc4)
        gc_row = jnp.sum(jnp.where(row == col, gc_col, 0.0), axis=0, keepdims=True)
        decay = jnp.where(incl, jnp.exp(jnp.minimum(gc_col - gc_row, 0.0)), 0.0)
        gamma = jnp.exp(gc_col)
        gc_last = jnp.concatenate([jnp.broadcast_to(gc4[CHUNK - 1:CHUNK, h:h + 1], (CHUNK, 1)) for h in range(N_HEADS)], axis=0)
        tau = jnp.exp(gc_last - gc_col)
        kb = k * beta
        locs.append(dict(decay=decay, gamma=gamma, tau=tau, cd=jnp.exp(gc_last), kb=kb, qd=q * gamma, kt=k * tau,
                         incl=incl, strict=strict))
    for loc, (q, k, v, beta, gc4) in zip(locs, chunks):
        loc["l_mat"] = jnp.where(strict, dot_nt(loc["kb"], k, FAST) * loc["decay"], 0.0)
    if inverses is None:
        inverses = _unit_lower_inverses([loc["l_mat"] for loc in locs], CHUNK)
    for loc, a_inv in zip(locs, inverses):
        loc["a_inv"] = a_inv
    for loc, (q, k, v, beta, gc4) in zip(locs, chunks):
        sol = dot_nn(loc["a_inv"], jnp.concatenate([v * beta, loc["kb"] * loc["gamma"]], axis=1), FAST)
        loc.update(sol=sol, value=sol[:, :HEAD_DIM], kcd=sol[:, HEAD_DIM:])
        loc["attn"] = jnp.where(incl, dot_nt(q, k, FAST) * loc["decay"], 0.0)
    return locs


def dn_chunk_state(loc, s):
    kcd, qd, kt, cd = loc["kcd"], loc["qd"], loc["kt"], loc["cd"]
    v_new = loc["value"] - _per_head(lambda h: dot_nn(kcd[_head_rows(h)], s[h], FAST))
    o = _per_head(lambda h: dot_nn(qd[_head_rows(h)], s[h], FAST)) + dot_nn(loc["attn"], v_new, FAST)
    s_new = [s[h] * cd[h * CHUNK:h * CHUNK + 1, :] + dot_tn(kt[_head_rows(h)], v_new[_head_rows(h)], FAST)
             for h in range(N_HEADS)]
    loc["v_new"] = v_new
    return o, s_new


def dn_chunks_bwd(items, ds_last):
    hr = _head_rows
    n = len(items)
    pre = []
    for q, k, v, beta, loc, s, do in items:
        pre.append(dict(
            dv_part=dot_tn(loc["attn"], do, FAST),
            dattn=jnp.where(loc["incl"], dot_nt(do, loc["v_new"], FAST), 0.0),
            dqd=_per_head(lambda h: dot_nt(do[hr(h)], s[h], FAST)),
            ds_part=[dot_tn(loc["qd"][hr(h)], do[hr(h)], FAST) for h in range(N_HEADS)]))
    ds_new_of, dv_new_of = [None] * n, [None] * n
    ds = ds_last
    for c in reversed(range(n)):
        loc = items[c][4]
        ds_new_of[c] = ds
        dv_new = pre[c]["dv_part"] + _per_head(lambda h: dot_nn(loc["kt"][hr(h)], ds[h], FAST))
        dv_new_of[c] = dv_new
        ds = [pre[c]["ds_part"][h] + ds[h] * loc["cd"][h * CHUNK:h * CHUNK + 1, :]
              - dot_tn(loc["kcd"][hr(h)], dv_new[hr(h)], FAST) for h in range(N_HEADS)]
    is_last = (lax.broadcasted_iota(jnp.int32, (STACK, 1), 0) % CHUNK) == CHUNK - 1
    out = []
    for c, (q, k, v, beta, loc, s, do) in enumerate(items):
        decay, gamma, tau, cd, kb = loc["decay"], loc["gamma"], loc["tau"], loc["cd"], loc["kb"]
        dv_new, ds_new, dattn, dqd = dv_new_of[c], ds_new_of[c], pre[c]["dattn"], pre[c]["dqd"]
        dkt = _per_head(lambda h: dot_nt(loc["v_new"][hr(h)], ds_new[h], FAST))
        dkcd = -_per_head(lambda h: dot_nt(dv_new[hr(h)], s[h], FAST))
        drhs = dot_tn(loc["a_inv"], jnp.concatenate([dv_new, dkcd], axis=1), FAST)
        dvb, dkbg = drhs[:, :HEAD_DIM], drhs[:, HEAD_DIM:]
        dl = jnp.where(loc["strict"], -dot_nt(drhs, loc["sol"], FAST), 0.0)
        dkk = dl * decay
        dqk = dattn * decay
        e = dl * loc["l_mat"] + dattn * loc["attn"]
        dgc = jnp.sum(e, axis=1, keepdims=True) - jnp.sum(e, axis=0, keepdims=True).T
        dkb = dot_nn(dkk, k, FAST) + dkbg * gamma
        dk = dot_tn(dkk, kb, FAST) + dot_tn(dqk, q, FAST) + dkt * tau
        dq = dot_nn(dqk, k, FAST) + dqd * gamma
        dgamma = jnp.sum(dkbg * kb, axis=1, keepdims=True) + jnp.sum(dqd * q, axis=1, keepdims=True)
        dtau_tau = jnp.sum(dkt * k, axis=1, keepdims=True) * tau
        dgc = dgc + dgamma * gamma - dtau_tau

        def last_term(h):
            dcd = jnp.sum(jnp.sum(ds_new[h] * s[h], axis=1, keepdims=True), axis=0, keepdims=True)
            total = jnp.sum(dtau_tau[hr(h)], axis=0, keepdims=True) + dcd * cd[h * CHUNK:h * CHUNK + 1, :]
            return jnp.broadcast_to(total, (CHUNK, 1))

        dgc = dgc + jnp.where(is_last, _per_head(last_term), 0.0)
        dk = dk + dkb * beta
        dbeta = jnp.sum(dkb * k, axis=1, keepdims=True) + jnp.sum(dvb * v, axis=1, keepdims=True)
        out.append((dq, dk, dvb * beta, dbeta, dgc))
    return out, ds


def _token_tile(t):
    return _pick(t, 256)


STRIP = 32


def _for_strips(n_rows, rows, fn, start=0):
    def step(r, carry):
        fn(pl.multiple_of(r * rows, rows))
        return carry

    lax.fori_loop(start, n_rows // rows, step, 0)


def _fold_rows(x):
    out = x[0:SUBLANES, :]
    for i in range(1, x.shape[0] // SUBLANES):
        out = out + x[i * SUBLANES:(i + 1) * SUBLANES, :]
    return out


def _matmul(a, b, mode, name, tiles, add=None, out_dtype=F32):
    if mode == "nn":
        (m, k), n = a.shape, b.shape[1]
    elif mode == "nt":
        (m, k), n = a.shape, b.shape[0]
    else:
        (k, m), n = a.shape, b.shape[1]
    tm, tn, tk = min(tiles[0], m), min(tiles[1], n), min(tiles[2], k)
    assert m % tm == 0 and n % tn == 0 and k % tk == 0, (name, m, n, k, tiles)
    nk = k // tk
    dims = {"nn": (((1,), (0,)), ((), ())), "nt": (((1,), (1,)), ((), ())), "tn": (((0,), (0,)), ((), ()))}[mode]

    def finish(res, add_ref, o_ref):
        if add_ref is not None:
            res = res + add_ref[...]
        o_ref[...] = res.astype(o_ref.dtype)

    def body(*refs):
        a_ref, b_ref = refs[0], refs[1]
        add_ref = refs[2] if add is not None else None
        o_ref = refs[3] if add is not None else refs[2]
        part = lax.dot_general(a_ref[...], b_ref[...], dims, preferred_element_type=F32)
        if nk == 1:
            finish(part, add_ref, o_ref)
            return
        acc_ref = refs[-1]
        kk = pl.program_id(2)

        @pl.when(kk == 0)
        def _():
            acc_ref[...] = part

        @pl.when(kk > 0)
        def _():
            acc_ref[...] += part

        @pl.when(kk == nk - 1)
        def _():
            finish(acc_ref[...], add_ref, o_ref)

    a_spec = pl.BlockSpec((tk, tm), lambda j, i, kk: (kk, i)) if mode == "tn" else pl.BlockSpec((tm, tk), lambda j, i, kk: (i, kk))
    b_spec = pl.BlockSpec((tn, tk), lambda j, i, kk: (j, kk)) if mode == "nt" else pl.BlockSpec((tk, tn), lambda j, i, kk: (kk, j))
    o_spec = pl.BlockSpec((tm, tn), lambda j, i, kk: (i, j))
    in_specs = [a_spec, b_spec] + ([o_spec] if add is not None else [])
    args = (a, b) + ((add,) if add is not None else ())
    return _pcall(
        body, grid=(n // tn, m // tm, nk), in_specs=in_specs, out_specs=o_spec,
        out_shape=jax.ShapeDtypeStruct((m, n), out_dtype),
        scratch_shapes=[pltpu.VMEM((tm, tn), F32)] if nk > 1 else [],
        compiler_params=_params("parallel", "parallel", "arbitrary"), name=name)(*args)


def _matmul_rows(a, b, mode, name, tm, extra, outs, fn):
    m, k = a.shape
    n = b.shape[1] if mode == "nn" else b.shape[0]
    tm = min(tm, m)
    dims = (((1,), (0,)), ((), ())) if mode == "nn" else (((1,), (1,)), ((), ()))

    def spec(shape, kind):
        if kind == "rows":
            return pl.BlockSpec((tm, shape[1]), lambda i: (i, 0))
        return pl.BlockSpec(shape, lambda i: (0,) * len(shape))

    def body(a_ref, b_ref, *refs):
        rows = lax.dot_general(a_ref[...], b_ref[...], dims, preferred_element_type=F32)
        fn(rows, pl.program_id(0) == 0, *refs)

    return _pcall(
        body, grid=(m // tm,),
        in_specs=[pl.BlockSpec((tm, k), lambda i: (i, 0)), pl.BlockSpec(b.shape, lambda i: (0, 0))]
        + [spec(x.shape, kind) for x, kind in extra],
        out_specs=[spec(shape, kind) for shape, _, kind in outs],
        out_shape=[jax.ShapeDtypeStruct(shape, dtype) for shape, dtype, _ in outs],
        compiler_params=_params("arbitrary"), name=name)(a, b, *[x for x, _ in extra])


def _rmsnorm_matmul(x, g, b_t, name, tm):
    t, d = x.shape
    n = b_t.shape[0]
    tm = min(tm, t)

    def body(x_ref, g_ref, b_ref, o_ref, h_ref, r_ref):
        y, r = rms_fwd(x_ref[...], g_ref[...])
        h = y.astype(BF16)
        h_ref[...] = h
        r_ref[...] = r
        o_ref[...] = lax.dot_general(h, b_ref[...], (((1,), (1,)), ((), ())), preferred_element_type=F32)

    rows = lambda w: pl.BlockSpec((tm, w), lambda i: (i, 0))
    return _pcall(
        body, grid=(t // tm,),
        in_specs=[rows(d), pl.BlockSpec((1, d), lambda i: (0, 0)), pl.BlockSpec((n, d), lambda i: (0, 0))],
        out_specs=[rows(n), rows(d), rows(1)],
        out_shape=[jax.ShapeDtypeStruct((t, n), F32), jax.ShapeDtypeStruct((t, d), BF16), jax.ShapeDtypeStruct((t, 1), F32)],
        compiler_params=_params("parallel"), name=name)(x, g, b_t)


def _rmsnorm_bwd_rows(t, d):
    def fn(dh, first, x_ref, r_ref, g_ref, dres_ref, dx_ref, dxb_ref, dg_ref):
        dx, dg_rows = rms_bwd(x_ref[...], r_ref[...], g_ref[...], dh)
        dx = dx + dres_ref[...]
        dx_ref[...] = dx
        dxb_ref[...] = dx.astype(BF16)

        @pl.when(first)
        def _():
            dg_ref[...] = jnp.zeros_like(dg_ref)

        dg_ref[...] += jnp.sum(dg_rows, axis=0, keepdims=True)

    return fn, [((t, d), F32, "rows"), ((t, d), BF16, "rows"), ((1, d), F32, "whole")]


def _final_loss_rows(t, d):
    def fn(rows, first, res_ref, t_ref, g_ref, loss_ref, dx_ref, dxb_ref, dg_ref):
        @pl.when(first)
        def _():
            loss_ref[...] = jnp.zeros_like(loss_ref)
            dg_ref[...] = jnp.zeros_like(dg_ref)

        x = rows + res_ref[...]
        y, r = rms_fwd(x, g_ref[...])
        err = y - t_ref[...]
        loss_ref[...] += 0.5 * jnp.sum(jnp.mean(err * err, axis=-1, keepdims=True), axis=0, keepdims=True)
        dx, dg_rows = rms_bwd(x, r, g_ref[...], err * (1.0 / d))
        dx_ref[...] = dx
        dxb_ref[...] = dx.astype(BF16)
        dg_ref[...] += jnp.sum(dg_rows, axis=0, keepdims=True)

    return fn, [((1, LANES), F32, "whole"), ((t, d), F32, "rows"), ((t, d), BF16, "rows"), ((1, d), F32, "whole")]


def _prev_halo_spec(tm, width, col_block):
    return pl.BlockSpec((HALO, width), lambda i: (jnp.maximum(i * (tm // HALO) - 1, 0), col_block))


def _history(tile_ref, halo_ref, first, row0, cols):
    if isinstance(row0, int) and row0 == 0:
        return jnp.concatenate([jnp.where(first, 0.0, halo_ref[:, cols]), tile_ref[0:STRIP, cols]], axis=0)
    return tile_ref[pl.ds(pl.multiple_of(row0 - HALO, HALO), STRIP + HALO), cols]


def _first_then_strips(n_rows, fn):
    fn(0)
    _for_strips(n_rows, STRIP, fn, start=1)


def _delays(ext, taps):
    return [ext[HALO:, :]] + [pltpu.roll(ext, j, 0)[HALO:, :] for j in range(1, taps)]


def _causal_conv(delayed, w):
    taps = len(delayed)
    out = delayed[0] * w[taps - 1:taps, :]
    for j in range(1, taps):
        out = out + delayed[j] * w[taps - 1 - j:taps - j, :]
    return out


def _advanced_conv(buf_ref, row0, cols, w):
    taps = w.shape[0]
    ext = buf_ref[pl.ds(row0, STRIP + HALO), cols]
    out = ext[:STRIP, :] * w[taps - 1:taps, :]
    for j in range(1, taps):
        out = out + pltpu.roll(ext, STRIP + HALO - j, 0)[:STRIP, :] * w[taps - 1 - j:taps - j, :]
    return out


def _dn_prep(p, conv_w, a_log4, dt_bias4):
    t = p.shape[0]
    tm = _token_tile(t)
    w3 = 3 * DN_WIDTH

    def body(x_ref, halo_ref, pbd_ref, w_ref, alog_ref, dtb_ref, q_ref, k_ref, v_ref, beta_ref, g_ref):
        first = pl.program_id(0) == 0

        def strip(row0):
            rows = pl.ds(row0, STRIP)
            for h in range(N_HEADS):
                sl = slice(h * HEAD_DIM, (h + 1) * HEAD_DIM)
                for part, out_ref in ((0, q_ref), (1, k_ref), (2, v_ref)):
                    cols = slice(part * DN_WIDTH + h * HEAD_DIM, part * DN_WIDTH + (h + 1) * HEAD_DIM)
                    y = silu(_causal_conv(_delays(_history(x_ref, halo_ref, first, row0, cols), CONV_K), w_ref[:, cols]))
                    if part == 0:
                        y = l2_fwd(y)[0] * (HEAD_DIM ** -0.5)
                    elif part == 1:
                        y = l2_fwd(y)[0]
                    out_ref[rows, sl] = y
            head = lax.broadcasted_iota(jnp.int32, (STRIP, LANES), 1) < N_HEADS
            pbd = pbd_ref[rows, :]
            beta_ref[rows, :] = jnp.where(head, sigmoid(pbd), 0.0)
            a_raw = pltpu.roll(pbd, LANES - N_HEADS, 1)
            g_ref[rows, :] = jnp.where(head, -jnp.exp(alog_ref[...]) * softplus(a_raw + dtb_ref[...]), 0.0)

        _first_then_strips(tm, strip)

    tok = lambda w, cb: pl.BlockSpec((tm, w), lambda i: (i, cb))
    full = lambda a: pl.BlockSpec(a.shape, lambda i: (0, 0))
    return _pcall(
        body, grid=(t // tm,),
        in_specs=[tok(w3, 0), _prev_halo_spec(tm, w3, 0), tok(LANES, PROJ_MAIN // LANES),
                  full(conv_w), full(a_log4), full(dt_bias4)],
        out_specs=[tok(DN_WIDTH, 0)] * 3 + [tok(LANES, 0)] * 2,
        out_shape=[jax.ShapeDtypeStruct((t, DN_WIDTH), F32)] * 3 + [jax.ShapeDtypeStruct((t, LANES), F32)] * 2,
        compiler_params=_params("parallel"), name="dn_prep")(p, p, p, conv_w, a_log4, dt_bias4)


def _dn_prep_bwd(p, conv_w, a_log4, dt_bias4, dq, dk, dv, dbeta4, dg4, dp_buf):
    t = p.shape[0]
    tm = _token_tile(t)
    w3 = 3 * DN_WIDTH

    def body(x_ref, halo_ref, pbd_ref, w_ref, alog_ref, dtb_ref, dq_ref, dk_ref, dv_ref, dbeta_ref, dg_ref, _,
             dc_ref, dw_ref, dpbd_ref, dalog_ref, ddtb_ref, dw_acc, lane_acc):
        first = pl.program_id(0) == 0
        dw_acc[...] = jnp.zeros_like(dw_acc)
        lane_acc[...] = jnp.zeros_like(lane_acc)

        def strip(row0):
            rows = pl.ds(row0, STRIP)
            for h in range(N_HEADS):
                sl = slice(h * HEAD_DIM, (h + 1) * HEAD_DIM)
                for part, dy_ref in ((0, dq_ref), (1, dk_ref), (2, dv_ref)):
                    cols = slice(part * DN_WIDTH + h * HEAD_DIM, part * DN_WIDTH + (h + 1) * HEAD_DIM)
                    delayed = _delays(_history(x_ref, halo_ref, first, row0, cols), CONV_K)
                    c = _causal_conv(delayed, w_ref[:, cols])
                    dy = dy_ref[rows, sl]
                    if part < 2:
                        y = silu(c)
                        _, r = l2_fwd(y)
                        dy = l2_bwd(y, r, dy * (HEAD_DIM ** -0.5) if part == 0 else dy)
                    dc = dy * silu_grad(c)
                    dc_ref[rows, cols] = dc
                    for j in range(CONV_K):
                        k = CONV_K - 1 - j
                        dw_acc[k * SUBLANES:(k + 1) * SUBLANES, cols] += _fold_rows(dc * delayed[j])
            head = lax.broadcasted_iota(jnp.int32, (STRIP, LANES), 1) < N_HEADS
            pbd = pbd_ref[rows, :]
            beta = sigmoid(pbd)
            dpb = jnp.where(head, dbeta_ref[rows, :] * beta * (1.0 - beta), 0.0)
            z = pltpu.roll(pbd, LANES - N_HEADS, 1) + dtb_ref[...]
            neg_rate = -jnp.exp(alog_ref[...])
            dg = dg_ref[rows, :]
            dpa = jnp.where(head, dg * neg_rate * sigmoid(z), 0.0)
            dpbd_ref[rows, :] = (dpb + pltpu.roll(dpa, N_HEADS, 1)).astype(BF16)
            g = jnp.where(head, neg_rate * softplus(z), 0.0)
            lane_acc[0:SUBLANES, :] += _fold_rows(dg * g)
            lane_acc[SUBLANES:, :] += _fold_rows(dpa)

        _first_then_strips(tm, strip)

        @pl.when(first)
        def _():
            dw_ref[...] = jnp.zeros_like(dw_ref)
            dalog_ref[...] = jnp.zeros_like(dalog_ref)
            ddtb_ref[...] = jnp.zeros_like(ddtb_ref)

        for k in range(CONV_K):
            dw_ref[k:k + 1, :] += jnp.sum(dw_acc[k * SUBLANES:(k + 1) * SUBLANES, :], axis=0, keepdims=True)
        dalog_ref[...] += jnp.sum(lane_acc[0:SUBLANES, :], axis=0, keepdims=True)
        ddtb_ref[...] += jnp.sum(lane_acc[SUBLANES:, :], axis=0, keepdims=True)

    tok = lambda w, cb: pl.BlockSpec((tm, w), lambda i: (i, cb))
    full = lambda shape: pl.BlockSpec(shape, lambda i: (0, 0))
    return _pcall(
        body, grid=(t // tm,),
        in_specs=[tok(w3, 0), _prev_halo_spec(tm, w3, 0), tok(LANES, PROJ_MAIN // LANES),
                  full(conv_w.shape), full(a_log4.shape), full(dt_bias4.shape)] + [tok(DN_WIDTH, 0)] * 3 + [tok(LANES, 0)] * 2
        + [pl.BlockSpec(memory_space=pl.ANY)],
        out_specs=[tok(w3, 0), full((CONV_K, w3)), tok(LANES, PROJ_MAIN // LANES), full((1, LANES)), full((1, LANES))],
        out_shape=[jax.ShapeDtypeStruct((t, w3), F32), jax.ShapeDtypeStruct((CONV_K, w3), F32),
                   jax.ShapeDtypeStruct(dp_buf.shape, dp_buf.dtype),
                   jax.ShapeDtypeStruct((1, LANES), F32), jax.ShapeDtypeStruct((1, LANES), F32)],
        input_output_aliases={11: 2},
        scratch_shapes=[pltpu.VMEM((CONV_K * SUBLANES, w3), F32), pltpu.VMEM((2 * SUBLANES, LANES), F32)],
        compiler_params=_params("arbitrary"), name="dn_prep_bwd")(p, p, p, conv_w, a_log4, dt_bias4, dq, dk, dv, dbeta4, dg4, dp_buf)


def _conv_bwd_input(dc, w, name, out_cols=None, col_block=0, into=None):
    t, c = dc.shape
    taps = w.shape[0]
    tm = _token_tile(t)
    ct = _pick(c, 1536)
    n_tok = t // tm
    out_cols = c if out_cols is None else out_cols

    def body(dc_ref, next_ref, w_ref, *rest):
        dx_ref, buf_ref = rest[-2], rest[-1]
        buf_ref[0:tm, :] = dc_ref[...]
        buf_ref[tm:, :] = jnp.where(pl.program_id(0) == n_tok - 1, 0.0, next_ref[...])

        def strip(row0):
            for c0 in range(0, ct, LANES):
                cols = slice(c0, c0 + LANES)
                dx_ref[pl.ds(row0, STRIP), cols] = _advanced_conv(buf_ref, row0, cols, w_ref[:, cols]).astype(BF16)

        _for_strips(tm, STRIP, strip)

    in_specs = [pl.BlockSpec((tm, ct), lambda i, j: (i, j)),
                pl.BlockSpec((HALO, ct), lambda i, j: (jnp.minimum((i + 1) * (tm // HALO), t // HALO - 1), j)),
                pl.BlockSpec((taps, ct), lambda i, j: (0, j))]
    args = (dc, dc, w)
    aliases = {}
    if into is not None:
        in_specs.append(pl.BlockSpec(memory_space=pl.ANY))
        args += (into,)
        aliases = {3: 0}
    return _pcall(
        body, grid=(n_tok, c // ct), in_specs=in_specs,
        out_specs=pl.BlockSpec((tm, ct), lambda i, j: (i, j + col_block)),
        out_shape=jax.ShapeDtypeStruct((t, out_cols), BF16), input_output_aliases=aliases,
        scratch_shapes=[pltpu.VMEM((tm + HALO, ct), F32)],
        compiler_params=_params("parallel", "parallel"), name=name)(*args)


def _dn_forward(q, k, v, beta4, g4, p, norm_g):
    t = q.shape[0]
    n = t // CHUNK
    nc = DN_FWD_CHUNKS
    rows_per_step = nc * CHUNK

    def body(q_ref, k_ref, v_ref, b_ref, g_ref, gate_ref, ng_ref, mix_ref, s_all_ref, ainv_ref, s_ref):
        @pl.when(pl.program_id(0) == 0)
        def _():
            s_ref[...] = jnp.zeros_like(s_ref)

        chunks = []
        for c in range(nc):
            rows = slice(c * CHUNK, (c + 1) * CHUNK)
            chunks.append((_stack_heads(q_ref[rows, :]), _stack_heads(k_ref[rows, :]), _stack_heads(v_ref[rows, :]),
                           _stack_lanes(b_ref[rows, :]), chunk_cumsum(g_ref[rows, :])))
        locs = dn_chunks_local(chunks)
        s = [s_ref[h] for h in range(N_HEADS)]
        for c in range(nc):
            rows = slice(c * CHUNK, (c + 1) * CHUNK)
            ainv_ref[c] = locs[c]["a_inv"].astype(BF16)
            for h in range(N_HEADS):
                s_all_ref[c, h] = s[h]
            o, s = dn_chunk_state(locs[c], s)
            o_n, _ = rms_fwd(o, ng_ref[...])
            for h in range(N_HEADS):
                sl = slice(h * HEAD_DIM, (h + 1) * HEAD_DIM)
                mix_ref[rows, sl] = (o_n[_head_rows(h)] * silu(gate_ref[rows, sl])).astype(BF16)
        for h in range(N_HEADS):
            s_ref[h] = s[h]

    ch = lambda w, cb: pl.BlockSpec((rows_per_step, w), lambda i: (i, cb))
    per_chunk = lambda *shape: pl.BlockSpec((nc,) + shape, lambda i: (i,) + (0,) * len(shape))
    return _pcall(
        body, grid=(n // nc,),
        in_specs=[ch(DN_WIDTH, 0)] * 3 + [ch(LANES, 0)] * 2 + [ch(DN_WIDTH, 3), pl.BlockSpec((1, HEAD_DIM), lambda i: (0, 0))],
        out_specs=[ch(DN_WIDTH, 0), per_chunk(N_HEADS, HEAD_DIM, HEAD_DIM), per_chunk(STACK, STACK)],
        out_shape=[jax.ShapeDtypeStruct((t, DN_WIDTH + SG_WIDTH), BF16), jax.ShapeDtypeStruct((n, N_HEADS, HEAD_DIM, HEAD_DIM), F32),
                   jax.ShapeDtypeStruct((n, STACK, STACK), BF16)],
        scratch_shapes=[pltpu.VMEM((N_HEADS, HEAD_DIM, HEAD_DIM), F32)],
        compiler_params=_params("arbitrary"), name="dn_forward")(q, k, v, beta4, g4, p, norm_g)


def _dn_backward(q, k, v, beta4, g4, p, norm_g, saved, dmix, dp_buf):
    t = q.shape[0]
    n = t // CHUNK
    steps = n // DN_CHUNKS
    rows_per_step = DN_CHUNKS * CHUNK

    def body(q_ref, k_ref, v_ref, b_ref, g_ref, gate_ref, ng_ref, s_in_ref, ainv_ref, dmix_ref, _,
             dq_ref, dk_ref, dv_ref, db_ref, dg_ref, dgate_ref, dng_ref, ds_ref):
        @pl.when(pl.program_id(0) == 0)
        def _():
            ds_ref[...] = jnp.zeros_like(ds_ref)
            dng_ref[...] = jnp.zeros_like(dng_ref)

        chunks = []
        for c in range(DN_CHUNKS):
            rows = slice(c * CHUNK, (c + 1) * CHUNK)
            chunks.append((_stack_heads(q_ref[rows, :]), _stack_heads(k_ref[rows, :]), _stack_heads(v_ref[rows, :]),
                           _stack_lanes(b_ref[rows, :]), chunk_cumsum(g_ref[rows, :])))
        items = []
        for c, loc in enumerate(dn_chunks_local(chunks, [ainv_ref[c] for c in range(DN_CHUNKS)])):
            rows = slice(c * CHUNK, (c + 1) * CHUNK)
            s = [s_in_ref[c, h] for h in range(N_HEADS)]
            o, _ = dn_chunk_state(loc, s)
            o_n, r = rms_fwd(o, ng_ref[...])
            gate = _stack_heads(gate_ref[rows, :])
            dmx = _stack_heads(dmix_ref[rows, :])
            dgate = dmx * o_n * silu_grad(gate)
            do, dng_rows = rms_bwd(o, r, ng_ref[...], dmx * silu(gate))
            dng_ref[...] += jnp.sum(dng_rows, axis=0, keepdims=True)
            for h in range(N_HEADS):
                dgate_ref[rows, h * HEAD_DIM:(h + 1) * HEAD_DIM] = dgate[_head_rows(h)].astype(BF16)
            items.append((*chunks[c][:4], loc, s, do))
        grads, ds = dn_chunks_bwd(items, [ds_ref[h] for h in range(N_HEADS)])
        lane = lax.broadcasted_iota(jnp.int32, (CHUNK, LANES), 1)
        _, strict = _tri_masks(CHUNK)
        for c in range(DN_CHUNKS):
            rows = slice(c * CHUNK, (c + 1) * CHUNK)
            dq, dk, dv, dbeta, dgc = grads[c]
            db4 = jnp.zeros((CHUNK, LANES), F32)
            dgc4 = jnp.zeros((CHUNK, LANES), F32)
            for h in range(N_HEADS):
                sl = slice(h * HEAD_DIM, (h + 1) * HEAD_DIM)
                head_rows = _head_rows(h)
                dq_ref[rows, sl] = dq[head_rows]
                dk_ref[rows, sl] = dk[head_rows]
                dv_ref[rows, sl] = dv[head_rows]
                db4 = jnp.where(lane == h, dbeta[head_rows], db4)
                dgc4 = jnp.where(lane == h, dgc[head_rows], dgc4)
            db_ref[rows, :] = db4
            dg_ref[rows, :] = dot_nn(jnp.logical_not(strict).astype(F32), dgc4)
        for h in range(N_HEADS):
            ds_ref[h] = ds[h]

    rev = lambda w, cb: pl.BlockSpec((rows_per_step, w), lambda i: (steps - 1 - i, cb))
    per_chunk = lambda a: pl.BlockSpec((DN_CHUNKS,) + a.shape[1:], lambda i: (steps - 1 - i,) + (0,) * (a.ndim - 1))
    return _pcall(
        body, grid=(steps,),
        in_specs=[rev(DN_WIDTH, 0)] * 3 + [rev(LANES, 0)] * 2 + [rev(DN_WIDTH, 3), pl.BlockSpec((1, HEAD_DIM), lambda i: (0, 0))]
        + [per_chunk(a) for a in saved] + [rev(DN_WIDTH, 0), pl.BlockSpec(memory_space=pl.ANY)],
        out_specs=[rev(DN_WIDTH, 0)] * 3 + [rev(LANES, 0)] * 2 + [rev(DN_WIDTH, 3), pl.BlockSpec((1, HEAD_DIM), lambda i: (0, 0))],
        out_shape=[jax.ShapeDtypeStruct((t, DN_WIDTH), F32)] * 3 + [jax.ShapeDtypeStruct((t, LANES), F32)] * 2
        + [jax.ShapeDtypeStruct(dp_buf.shape, dp_buf.dtype), jax.ShapeDtypeStruct((1, HEAD_DIM), F32)],
        input_output_aliases={10: 5},
        scratch_shapes=[pltpu.VMEM((N_HEADS, HEAD_DIM, HEAD_DIM), F32)],
        compiler_params=_params("arbitrary"), name="dn_backward")(q, k, v, beta4, g4, p, norm_g, *saved, dmix, dp_buf)


def _sg_mask():
    row = lax.broadcasted_iota(jnp.int32, (SG_BLOCK, SG_BLOCK), 0)
    col = lax.broadcasted_iota(jnp.int32, (SG_BLOCK, SG_BLOCK), 1)
    return (col // CHUNK) <= (row // CHUNK)


def _sg_forward(p, norm_g, w_s, b_t, mix_buf):
    t = p.shape[0]

    def body(u_ref, v_ref, ng_ref, w_ref, b_ref, _, o_ref):
        mask = _sg_mask()
        groups = range(SG_GROUPS)
        sl = [slice(g * SG_DIM, (g + 1) * SG_DIM) for g in groups]
        vn = [rms_fwd(gelu(v_ref[:, sl[g]]), ng_ref[:, sl[g]])[0] for g in groups]
        s = [dot_nn(jnp.where(mask, w_ref[g], 0.0), vn[g], FAST) + b_ref[:, g:g + 1] for g in groups]
        for g in groups:
            o_ref[:, sl[g]] = (gelu(u_ref[:, sl[g]]) * s[g]).astype(BF16)

    blk = lambda cb: pl.BlockSpec((SG_BLOCK, SG_WIDTH), lambda i: (i, cb))
    return _pcall(
        body, grid=(t // SG_BLOCK,),
        in_specs=[blk(4), blk(5), pl.BlockSpec((1, SG_WIDTH), lambda i: (0, 0)),
                  pl.BlockSpec((SG_GROUPS, SG_BLOCK, SG_BLOCK), lambda i: (0, 0, 0)), pl.BlockSpec((SG_BLOCK, SG_GROUPS), lambda i: (0, 0)),
                  pl.BlockSpec(memory_space=pl.ANY)],
        out_specs=blk(1), out_shape=jax.ShapeDtypeStruct(mix_buf.shape, mix_buf.dtype), input_output_aliases={5: 0},
        compiler_params=_params("parallel"), name="sg_forward")(p, p, norm_g, w_s, b_t, mix_buf)


def _sg_backward(p, norm_g, w_s, b_t, dmix):
    t = p.shape[0]

    def body(u_ref, v_ref, ng_ref, w_ref, b_ref, do_ref, duv_ref, dng_ref, dw_ref, db_ref):
        @pl.when(pl.program_id(0) == 0)
        def _():
            dng_ref[...] = jnp.zeros_like(dng_ref)
            dw_ref[...] = jnp.zeros_like(dw_ref)
            db_ref[...] = jnp.zeros_like(db_ref)

        mask = _sg_mask()
        lane = lax.broadcasted_iota(jnp.int32, (SG_BLOCK, LANES), 1)
        groups = range(SG_GROUPS)
        sl = [slice(g * SG_DIM, (g + 1) * SG_DIM) for g in groups]
        w_m = [jnp.where(mask, w_ref[g], 0.0) for g in groups]
        vg = [gelu(v_ref[:, sl[g]]) for g in groups]
        normed = [rms_fwd(vg[g], ng_ref[:, sl[g]]) for g in groups]
        s = [dot_nn(w_m[g], normed[g][0], FAST) + b_ref[:, g:g + 1] for g in groups]
        ds = []
        db = jnp.zeros((SG_BLOCK, LANES), F32)
        for g in groups:
            u_raw, do = u_ref[:, sl[g]], do_ref[:, sl[g]]
            duv_ref[:, sl[g]] = (do * s[g] * gelu_grad(u_raw)).astype(BF16)
            ds.append(do * gelu(u_raw))
            db = jnp.where(lane == g, jnp.sum(ds[g], axis=1, keepdims=True), db)
        dw = [jnp.where(mask, dot_nt(ds[g], normed[g][0], FAST), 0.0) for g in groups]
        dvn = [dot_tn(w_m[g], ds[g], FAST) for g in groups]
        for g in groups:
            dw_ref[g] += dw[g]
            dvg, dng_rows = rms_bwd(vg[g], normed[g][1], ng_ref[:, sl[g]], dvn[g])
            dng_ref[:, sl[g]] += jnp.sum(dng_rows, axis=0, keepdims=True)
            duv_ref[:, SG_WIDTH + g * SG_DIM:SG_WIDTH + (g + 1) * SG_DIM] = (dvg * gelu_grad(v_ref[:, sl[g]])).astype(BF16)
        db_ref[...] += db

    blk = lambda cb: pl.BlockSpec((SG_BLOCK, SG_WIDTH), lambda i: (i, cb))
    const2 = lambda shape: pl.BlockSpec(shape, lambda i: (0, 0))
    w_spec = pl.BlockSpec((SG_GROUPS, SG_BLOCK, SG_BLOCK), lambda i: (0, 0, 0))
    return _pcall(
        body, grid=(t // SG_BLOCK,),
        in_specs=[blk(4), blk(5), const2((1, SG_WIDTH)), w_spec, const2((SG_BLOCK, SG_GROUPS)), blk(1)],
        out_specs=[pl.BlockSpec((SG_BLOCK, 2 * SG_WIDTH), lambda i: (i, 2)), const2((1, SG_WIDTH)), w_spec,
                   const2((SG_BLOCK, LANES))],
        out_shape=[jax.ShapeDtypeStruct((t, PROJ_PAD), BF16), jax.ShapeDtypeStruct((1, SG_WIDTH), F32),
                   jax.ShapeDtypeStruct((SG_GROUPS, SG_BLOCK, SG_BLOCK), F32), jax.ShapeDtypeStruct((SG_BLOCK, LANES), F32)],
        compiler_params=_params("arbitrary"), name="sg_backward")(p, p, norm_g, w_s, b_t, dmix)


FFN_COLS = 256


def _norm_up_ffn(x, g, w_up_t, conv_w, conv_b):
    t, d = x.shape
    tm = min(t, 256)
    blocks = D_FF // FFN_COLS
    nt = (((1,), (1,)), ((), ()))

    def body(x_ref, g_ref, w_ref, cw_ref, cb_ref, up_ref, act_ref, h_ref, r_ref, tail_ref, prev_ref):
        @pl.when(pl.program_id(0) == 0)
        def _():
            tail_ref[...] = jnp.zeros_like(tail_ref)

        y, r = rms_fwd(x_ref[...], g_ref[...])
        h = y.astype(BF16)
        h_ref[...] = h
        r_ref[...] = r

        def project(blk):
            out = []
            for half in range(2):
                cols = slice(half * D_FF + blk * FFN_COLS, half * D_FF + (blk + 1) * FFN_COLS)
                u = lax.dot_general(h, w_ref[cols, :], nt, preferred_element_type=F32)
                up_ref[:, cols] = u
                prev_ref[:, cols] = tail_ref[:, cols]
                tail_ref[:, cols] = u[tm - HALO:, :]
                out.append(cols)
            return out

        def history(row0, cols):
            if row0 == 0:
                return jnp.concatenate([prev_ref[:, cols], up_ref[0:STRIP, cols]], axis=0)
            return up_ref[row0 - HALO:row0 + STRIP, cols]

        def activate(blk, g_cols, v_cols):
            for row0 in range(0, tm, STRIP):
                for c0 in range(0, FFN_COLS, LANES):
                    gc = slice(g_cols.start + c0, g_cols.start + c0 + LANES)
                    vc = slice(v_cols.start + c0, v_cols.start + c0 + LANES)
                    cg = _causal_conv(_delays(history(row0, gc), FFN_CONV), cw_ref[:, gc]) + cb_ref[:, gc]
                    cv = _causal_conv(_delays(history(row0, vc), FFN_CONV), cw_ref[:, vc]) + cb_ref[:, vc]
                    act_ref[row0:row0 + STRIP, blk * FFN_COLS + c0:blk * FFN_COLS + c0 + LANES] = (silu(cg) * cv).astype(BF16)

        pending = None
        for blk in range(blocks):
            cols = project(blk)
            if pending is not None:
                activate(*pending)
            pending = (blk, *cols)
        activate(*pending)

    rows = lambda w: pl.BlockSpec((tm, w), lambda i: (i, 0))
    whole = lambda a: pl.BlockSpec(a.shape, lambda i: (0, 0))
    return _pcall(
        body, grid=(t // tm,),
        in_specs=[rows(d), whole(g), whole(w_up_t), whole(conv_w), whole(conv_b)],
        out_specs=[rows(2 * D_FF), rows(D_FF), rows(d), rows(1)],
        out_shape=[jax.ShapeDtypeStruct((t, 2 * D_FF), F32), jax.ShapeDtypeStruct((t, D_FF), BF16),
                   jax.ShapeDtypeStruct((t, d), BF16), jax.ShapeDtypeStruct((t, 1), F32)],
        scratch_shapes=[pltpu.VMEM((HALO, 2 * D_FF), F32), pltpu.VMEM((HALO, 2 * D_FF), F32)],
        compiler_params=_params("arbitrary"), name="norm_up_ffn")(x, g, w_up_t, conv_w, conv_b)


def _ffn_bwd(up, conv_w, conv_b, dact):
    t = up.shape[0]
    tm = _pick(t, 128)
    n_tok = t // tm
    width = 2 * D_FF

    def dconv(delayed_g, delayed_v, da, wg, wv, bg, bv):
        cg = _causal_conv(delayed_g, wg) + bg
        cv = _causal_conv(delayed_v, wv) + bv
        s = sigmoid(cg)
        return da * cv * (s * (1.0 + cg * (1.0 - s))), da * (cg * s)

    def body(up_ref, prev_ref, next_ref, da_ref, dan_ref, w_ref, b_ref, dup_ref, dw_ref, db_ref, dc_ref, dw_acc, db_acc):
        first = pl.program_id(0) == 0
        last = pl.program_id(0) == n_tok - 1
        dw_acc[...] = jnp.zeros_like(dw_acc)
        db_acc[...] = jnp.zeros_like(db_acc)

        def strip(row0):
            rows = pl.ds(row0, STRIP)
            for c0 in range(0, D_FF, LANES):
                gc, vc = slice(c0, c0 + LANES), slice(D_FF + c0, D_FF + c0 + LANES)
                del_g = _delays(_history(up_ref, prev_ref, first, row0, gc), FFN_CONV)
                del_v = _delays(_history(up_ref, prev_ref, first, row0, vc), FFN_CONV)
                dcg, dcv = dconv(del_g, del_v, da_ref[rows, gc], w_ref[:, gc], w_ref[:, vc], b_ref[:, gc], b_ref[:, vc])
                dc_ref[rows, gc] = dcg
                dc_ref[rows, vc] = dcv
                db_acc[:, gc] += _fold_rows(dcg)
                db_acc[:, vc] += _fold_rows(dcv)
                for j in range(FFN_CONV):
                    k = FFN_CONV - 1 - j
                    dw_acc[k * SUBLANES:(k + 1) * SUBLANES, gc] += _fold_rows(dcg * del_g[j])
                    dw_acc[k * SUBLANES:(k + 1) * SUBLANES, vc] += _fold_rows(dcv * del_v[j])

        _first_then_strips(tm, strip)

        for c0 in range(0, D_FF, LANES):
            gc, vc = slice(c0, c0 + LANES), slice(D_FF + c0, D_FF + c0 + LANES)

            def delayed(cols):
                return _delays(jnp.concatenate([up_ref[tm - HALO:tm, cols], next_ref[:, cols]], axis=0), FFN_CONV)

            dcg, dcv = dconv(delayed(gc), delayed(vc), dan_ref[:, gc], w_ref[:, gc], w_ref[:, vc], b_ref[:, gc], b_ref[:, vc])
            dc_ref[tm:, gc] = jnp.where(last, 0.0, dcg)
            dc_ref[tm:, vc] = jnp.where(last, 0.0, dcv)

        def strip_dx(row0):
            for c0 in range(0, width, LANES):
                cols = slice(c0, c0 + LANES)
                dup_ref[pl.ds(row0, STRIP), cols] = _advanced_conv(dc_ref, row0, cols, w_ref[:, cols]).astype(BF16)

        _for_strips(tm, STRIP, strip_dx)

        @pl.when(first)
        def _():
            dw_ref[...] = jnp.zeros_like(dw_ref)
            db_ref[...] = jnp.zeros_like(db_ref)

        for k in range(FFN_CONV):
            dw_ref[k:k + 1, :] += jnp.sum(dw_acc[k * SUBLANES:(k + 1) * SUBLANES, :], axis=0, keepdims=True)
        db_ref[...] += jnp.sum(db_acc[...], axis=0, keepdims=True)

    next_rows = lambda i: jnp.minimum((i + 1) * (tm // HALO), t // HALO - 1)
    full = lambda rows: pl.BlockSpec((rows, width), lambda i: (0, 0))
    return _pcall(
        body, grid=(n_tok,),
        in_specs=[pl.BlockSpec((tm, width), lambda i: (i, 0)),
                  pl.BlockSpec((HALO, width), lambda i: (jnp.maximum(i * (tm // HALO) - 1, 0), 0)),
                  pl.BlockSpec((HALO, width), lambda i: (next_rows(i), 0)),
                  pl.BlockSpec((tm, D_FF), lambda i: (i, 0)), pl.BlockSpec((HALO, D_FF), lambda i: (next_rows(i), 0)),
                  full(FFN_CONV), full(1)],
        out_specs=[pl.BlockSpec((tm, width), lambda i: (i, 0)), full(FFN_CONV), full(1)],
        out_shape=[jax.ShapeDtypeStruct((t, width), BF16), jax.ShapeDtypeStruct((FFN_CONV, width), F32),
                   jax.ShapeDtypeStruct((1, width), F32)],
        scratch_shapes=[pltpu.VMEM((tm + HALO, width), F32),
                        pltpu.VMEM((FFN_CONV * SUBLANES, width), F32), pltpu.VMEM((SUBLANES, width), F32)],
        compiler_params=_params("arbitrary"), name="ffn_bwd")(up, up, up, dact, dact, conv_w, conv_b)


def _my_position():
    return lax.axis_index("x"), lax.axis_index("y"), lax.axis_index("c")


COPIES = N_DEV - 1


def _all_gather(arrays):
    n = len(arrays)

    def body(*refs):
        x_refs, out_refs = refs[:n], refs[n:2 * n]
        send_sems, recv_sems, local_sems = refs[2 * n:]
        x, y, cc = _my_position()
        me, sibling = (x, y, cc), (x, y, 1 - cc)
        chips = [(1 - x, y), (x, 1 - y), (1 - x, 1 - y)]

        def block(a, px, py, pc):
            return out_refs[a].at[4 * px + 2 * py + pc]

        def copy(a, k, blk, to, src=None):
            return pltpu.make_async_remote_copy(
                src_ref=block(a, *blk) if src is None else src, dst_ref=block(a, *blk),
                send_sem=send_sems.at[a * COPIES + k], recv_sem=recv_sems.at[a * COPIES + k],
                device_id=to, device_id_type=MESH_ID)

        mine = [pltpu.make_async_copy(x_refs[a], block(a, *me), local_sems.at[a]) for a in range(n)]
        for cp in mine:
            cp.start()
        first = []
        for a in range(n):
            first.append(copy(a, 0, me, sibling, src=x_refs[a]))
            first += [copy(a, 1 + j, me, (*chip, cc), src=x_refs[a]) for j, chip in enumerate(chips)]
        for cp in first:
            cp.start()
        passed = []
        for j, chip in enumerate(chips):
            for a in range(n):
                copy(a, 1 + j, (*chip, cc), me).wait_recv()
                passed.append(copy(a, 4 + j, (*chip, cc), sibling))
                passed[-1].start()
        for a in range(n):
            copy(a, 0, sibling, me).wait_recv()
        for j, chip in enumerate(chips):
            for a in range(n):
                copy(a, 4 + j, (*chip, 1 - cc), me).wait_recv()
        for cp in first + passed:
            cp.wait_send()
        for cp in mine:
            cp.wait()

    any_spec = pl.BlockSpec(memory_space=pl.ANY)
    return _pcall(
        body, out_shape=[jax.ShapeDtypeStruct((N_DEV,) + a.shape, a.dtype) for a in arrays],
        in_specs=[any_spec] * n, out_specs=[any_spec] * n,
        scratch_shapes=[pltpu.SemaphoreType.DMA((n * COPIES,)), pltpu.SemaphoreType.DMA((n * COPIES,)),
                        pltpu.SemaphoreType.DMA((n,))],
        name="all_gather")(*arrays)


def _all_to_all(sends):
    n = len(sends)

    def body(*refs):
        send_refs, recv_refs = refs[:n], refs[n:2 * n]
        send_sems, recv_sems, local_sems = refs[2 * n:]
        x, y, cc = _my_position()
        me = 4 * x + 2 * y + cc
        mine = [pltpu.make_async_copy(send_refs[a].at[me], recv_refs[a].at[me], local_sems.at[a]) for a in range(n)]
        for cp in mine:
            cp.start()
        copies = []
        for rel in range(1, N_DEV):
            px, py, pc = x ^ (rel >> 2), y ^ ((rel >> 1) & 1), cc ^ (rel & 1)
            for a in range(n):
                copies.append(pltpu.make_async_remote_copy(
                    src_ref=send_refs[a].at[4 * px + 2 * py + pc], dst_ref=recv_refs[a].at[me],
                    send_sem=send_sems.at[a * COPIES + rel - 1], recv_sem=recv_sems.at[a * COPIES + rel - 1],
                    device_id=(px, py, pc), device_id_type=MESH_ID))
        for cp in copies:
            cp.start()
        for cp in copies:
            cp.wait()
        for cp in mine:
            cp.wait()

    any_spec = pl.BlockSpec(memory_space=pl.ANY)
    return _pcall(
        body, out_shape=[jax.ShapeDtypeStruct(s.shape, s.dtype) for s in sends],
        in_specs=[any_spec] * n, out_specs=[any_spec] * n,
        scratch_shapes=[pltpu.SemaphoreType.DMA((n * COPIES,)), pltpu.SemaphoreType.DMA((n * COPIES,)),
                        pltpu.SemaphoreType.DMA((n,))],
        name="all_to_all")(*sends)


def _hbm(a):
    return pltpu.with_memory_space_constraint(a, pltpu.HBM)


def _split_copies(send_refs, land_refs, send_sems, recv_sems, local_sems, gather):
    x, y, cc = _my_position()
    me = 4 * x + 2 * y + cc
    local, remote = [], []
    for a, (send, land) in enumerate(zip(send_refs, land_refs)):
        local.append(pltpu.make_async_copy(send if gather else send.at[me], land.at[me], local_sems.at[a]))
    for a, (send, land) in enumerate(zip(send_refs, land_refs)):
        for rel in range(1, N_DEV):
            px, py, pc = x ^ (rel >> 2), y ^ ((rel >> 1) & 1), cc ^ (rel & 1)
            remote.append(pltpu.make_async_remote_copy(
                src_ref=send if gather else send.at[4 * px + 2 * py + pc], dst_ref=land.at[me],
                send_sem=send_sems.at[a * COPIES + rel - 1], recv_sem=recv_sems.at[a * COPIES + rel - 1],
                device_id=(px, py, pc), device_id_type=MESH_ID))
    return local, remote


SPLIT_EFFECT = pltpu.SideEffectType.DATAFLOW_SIDE_EFFECTING


def _exchange_start(sends, after, gather, name):
    n = len(sends)
    lands = [_hbm(lax.empty((N_DEV,) + s.shape if gather else s.shape, s.dtype)) for s in sends]

    def body(*refs):
        send_refs, land_refs = refs[:n], refs[n:2 * n]
        send_sems, recv_sems, local_sems = refs[2 * n + 1:2 * n + 4]
        token = refs[-1]
        local, remote = _split_copies(send_refs, land_refs, send_sems, recv_sems, local_sems, gather)
        for cp in local + remote:
            cp.start()
        token[...] = jnp.zeros_like(token)

    hbm, sem = pl.BlockSpec(memory_space=pltpu.HBM), pl.BlockSpec(memory_space=pltpu.SEMAPHORE)
    out = _pcall(
        body, name=name,
        out_shape=[pltpu.SemaphoreType.DMA((n * COPIES,)), pltpu.SemaphoreType.DMA((n * COPIES,)), pltpu.SemaphoreType.DMA((n,))]
        + [pltpu.HBM(s.shape, s.dtype) for s in sends] + [pltpu.HBM(z.shape, z.dtype) for z in lands]
        + [jax.ShapeDtypeStruct((SUBLANES, LANES), F32)],
        in_specs=[hbm] * (2 * n) + [pl.BlockSpec(memory_space=pl.ANY)],
        out_specs=[sem] * 3 + [hbm] * (2 * n) + [pl.BlockSpec(memory_space=pltpu.VMEM)],
        input_output_aliases={i: 3 + i for i in range(2 * n)},
        compiler_params=pltpu.CompilerParams(has_side_effects=SPLIT_EFFECT),
    )(*[_hbm(s) for s in sends], *lands, after)
    return dict(sems=out[:3], sends=out[3:3 + n], lands=out[3 + n:3 + 2 * n], gather=gather), out[-1]


def _exchange_wait(handle, after, name):
    sends, lands, gather = handle["sends"], handle["lands"], handle["gather"]
    n = len(sends)

    def body(*refs):
        send_refs, land_refs = refs[:n], refs[n:2 * n]
        send_sems, recv_sems, local_sems = refs[2 * n:2 * n + 3]
        local, remote = _split_copies(send_refs, land_refs, send_sems, recv_sems, local_sems, gather)
        for cp in remote:
            cp.wait_send()
            cp.wait_recv()
        for cp in local:
            cp.wait()

    hbm, sem = pl.BlockSpec(memory_space=pltpu.HBM), pl.BlockSpec(memory_space=pltpu.SEMAPHORE)
    out = _pcall(
        body, name=name,
        out_shape=[pltpu.HBM(s.shape, s.dtype) for s in sends] + [pltpu.HBM(z.shape, z.dtype) for z in lands],
        in_specs=[hbm] * (2 * n) + [sem] * 3 + [pl.BlockSpec(memory_space=pl.ANY)],
        out_specs=[hbm] * (2 * n), input_output_aliases={i: i for i in range(2 * n)},
        compiler_params=pltpu.CompilerParams(has_side_effects=SPLIT_EFFECT),
    )(*sends, *lands, *handle["sems"], after)
    return out[n:]


def _sum_and_adamw(recv, w, m, v, name):
    _, r, wp = recv.shape
    c = w.shape[-1]
    lead = w.ndim == 3
    tr = max([d for d in range(2 * SUBLANES, 257, 2 * SUBLANES) if r % d == 0], default=r)
    bc1 = 1.0 - ADAM_B1 ** ADAM_STEP
    bc2 = 1.0 - ADAM_B2 ** ADAM_STEP

    def body(recv_ref, w_ref, m_ref, v_ref, g_ref, d_ref, nm_ref, nv_ref):
        g = recv_ref[0, :, 0:c].astype(F32)
        for s in range(1, N_DEV):
            g = g + recv_ref[s, :, 0:c].astype(F32)
        m_new = ADAM_B1 * m_ref[...] + (1.0 - ADAM_B1) * g
        v_new = ADAM_B2 * v_ref[...] + (1.0 - ADAM_B2) * (g * g)
        m_hat = m_new / bc1
        v_hat = v_new / bc2
        g_ref[...] = g
        d_ref[...] = -ADAM_LR * (m_hat / (jnp.sqrt(v_hat) + ADAM_EPS) + ADAM_WD * w_ref[...])
        nm_ref[...] = m_new
        nv_ref[...] = v_new

    tile = pl.BlockSpec((None, tr, c), lambda i: (0, i, 0)) if lead else pl.BlockSpec((tr, c), lambda i: (i, 0))
    return _pcall(
        body, grid=(r // tr,),
        in_specs=[pl.BlockSpec((N_DEV, tr, wp), lambda i: (0, i, 0)), tile, tile, tile],
        out_specs=[tile] * 4, out_shape=[jax.ShapeDtypeStruct(w.shape, F32)] * 4,
        compiler_params=_params("parallel"), name=name)(recv, w, m, v)


SHARDED_TAPS = ("dn_conv_w", "ffn_conv_w")
REPLICATED = ("attn_norm_g", "dn_a_log", "dn_dt_bias", "dn_out_norm_g", "sg_norm_g", "sg_w", "sg_b", "ffn_norm_g",
              "ffn_conv_b", "final_norm_g")
SMALL = SHARDED_TAPS + REPLICATED
WEIGHT_ORDER = ("attn_norm_g", "w_in", "dn_conv_w", "dn_a_log", "dn_dt_bias", "dn_out_norm_g", "sg_norm_g", "sg_w", "sg_b",
                "w_out", "ffn_norm_g", "w_up", "ffn_conv_w", "ffn_conv_b", "w_down", "final_norm_g")
SLAB_COLS = 1024


def _pad_to(flat, multiple):
    pad = (-flat.shape[-1]) % multiple
    if pad == 0:
        return flat
    return jnp.pad(flat, [(0, 0)] * (flat.ndim - 1) + [(0, pad)])


def _pack_small(named):
    flat = jnp.concatenate([named[n].reshape(-1) for n in SMALL])
    return _pad_to(flat, SUBLANES * SLAB_COLS).reshape(-1, SLAB_COLS)


def _unpack_small(slab, like):
    flat = slab.reshape(-1)
    out, off = {}, 0
    for n in SMALL:
        size = like[n].size
        out[n] = flat[off:off + size].reshape(like[n].shape)
        off += size
    return out


def _split_columns(full, n_local):
    r = full.shape[0]
    return full.reshape(r, N_DEV, n_local).transpose(1, 0, 2).reshape(N_DEV, r * n_local)


def _join_columns(blocks, r, n_local):
    return blocks.reshape(N_DEV, r, n_local).transpose(1, 0, 2).reshape(r, N_DEV * n_local)


def _lanes4(a):
    return jnp.pad(a.reshape(1, N_HEADS), ((0, 0), (0, LANES - N_HEADS)))


def kernel(x, attn_norm_g, w_in, dn_conv_w, dn_a_log, dn_dt_bias, dn_out_norm_g, sg_norm_g, sg_w, sg_b, w_out, ffn_norm_g, w_up, ffn_conv_w, ffn_conv_b, w_down, final_norm_g, loss_target, m_attn_norm_g, m_w_in, m_dn_conv_w, m_dn_a_log, m_dn_dt_bias, m_dn_out_norm_g, m_sg_norm_g, m_sg_w, m_sg_b, m_w_out, m_ffn_norm_g, m_w_up, m_ffn_conv_w, m_ffn_conv_b, m_w_down, m_final_norm_g, v_attn_norm_g, v_w_in, v_dn_conv_w, v_dn_a_log, v_dn_dt_bias, v_dn_out_norm_g, v_sg_norm_g, v_sg_w, v_sg_b, v_w_out, v_ffn_norm_g, v_w_up, v_ffn_conv_w, v_ffn_conv_b, v_w_down, v_final_norm_g):
    weights = dict(attn_norm_g=attn_norm_g, w_in=w_in, dn_conv_w=dn_conv_w, dn_a_log=dn_a_log, dn_dt_bias=dn_dt_bias,
                   dn_out_norm_g=dn_out_norm_g, sg_norm_g=sg_norm_g, sg_w=sg_w, sg_b=sg_b, w_out=w_out, ffn_norm_g=ffn_norm_g,
                   w_up=w_up, ffn_conv_w=ffn_conv_w, ffn_conv_b=ffn_conv_b, w_down=w_down, final_norm_g=final_norm_g)
    m_in = dict(attn_norm_g=m_attn_norm_g, w_in=m_w_in, dn_conv_w=m_dn_conv_w, dn_a_log=m_dn_a_log, dn_dt_bias=m_dn_dt_bias,
                dn_out_norm_g=m_dn_out_norm_g, sg_norm_g=m_sg_norm_g, sg_w=m_sg_w, sg_b=m_sg_b, w_out=m_w_out,
                ffn_norm_g=m_ffn_norm_g, w_up=m_w_up, ffn_conv_w=m_ffn_conv_w, ffn_conv_b=m_ffn_conv_b, w_down=m_w_down,
                final_norm_g=m_final_norm_g)
    v_in = dict(attn_norm_g=v_attn_norm_g, w_in=v_w_in, dn_conv_w=v_dn_conv_w, dn_a_log=v_dn_a_log, dn_dt_bias=v_dn_dt_bias,
                dn_out_norm_g=v_dn_out_norm_g, sg_norm_g=v_sg_norm_g, sg_w=v_sg_w, sg_b=v_sg_b, w_out=v_w_out,
                ffn_norm_g=v_ffn_norm_g, w_up=v_w_up, ffn_conv_w=v_ffn_conv_w, ffn_conv_b=v_ffn_conv_b, w_down=v_w_down,
                final_norm_g=v_final_norm_g)

    n_in, n_up = w_in.shape[2], w_up.shape[2]
    r_out, r_down = w_out.shape[1], w_down.shape[1]
    n_dnc, n_ffc = dn_conv_w.shape[2], ffn_conv_w.shape[2]
    transposed = lambda a: jnp.transpose(a, (0, 2, 1))
    taps = _pad_to(jnp.concatenate([dn_conv_w.reshape(-1), ffn_conv_w.reshape(-1)]), SUBLANES * LANES).reshape(-1, LANES)
    g_in, g_taps = _all_gather([transposed(w_in)[0].astype(BF16), taps])
    gather_out, token = _exchange_start([w_out[0].astype(BF16)], g_taps, True, "gather_w_out")
    gather_up, token = _exchange_start([transposed(w_up)[0].astype(BF16)], token, True, "gather_w_up")
    gather_down, token = _exchange_start([w_down[0].astype(BF16)], token, True, "gather_w_down")
    w_in_t = jnp.pad(g_in.reshape(N_DEV * n_in, D_MODEL), ((0, PROJ_PAD - N_DEV * n_in), (0, 0)))
    taps_all = g_taps.reshape(N_DEV, -1)
    dn_conv_full = _join_columns(taps_all[:, :CONV_K * n_dnc], CONV_K, n_dnc)
    ffn_conv_full = _join_columns(taps_all[:, CONV_K * n_dnc:CONV_K * n_dnc + FFN_CONV * n_ffc], FFN_CONV, n_ffc)
    late = dict(
        w_out=lambda after: _exchange_wait(gather_out, after, "gather_w_out_wait")[0].reshape(N_DEV * r_out, D_MODEL),
        w_up_t=lambda after: _exchange_wait(gather_up, after, "gather_w_up_wait")[0].reshape(N_DEV * n_up, D_MODEL),
        w_down=lambda after: _exchange_wait(gather_down, after, "gather_w_down_wait")[0].reshape(N_DEV * r_down, D_MODEL))

    def send_early(blocks, after, name):
        return _exchange_start(blocks, after, False, name)

    def send_small(g, loss_lanes, after):
        small = jnp.concatenate([g[n].reshape(-1) for n in REPLICATED] + [loss_lanes[0, 0:1]])
        slab = jnp.concatenate([_split_columns(g["dn_conv_w"], n_dnc), _split_columns(g["ffn_conv_w"], n_ffc),
                                jnp.broadcast_to(small[None, :], (N_DEV, small.shape[0]))], axis=1)
        return send_early([_pad_to(slab, SUBLANES * SLAB_COLS).reshape(N_DEV, -1, SLAB_COLS)], after, "send_small")

    upd = {}

    def update_early(sent_down, sent_up_out, sent_small, after):
        r_dn, = _exchange_wait(sent_down, after, "send_dw_down_wait")
        r_up, r_o = _exchange_wait(sent_up_out, after, "send_dw_up_out_wait")
        r_small, = _exchange_wait(sent_small, after, "send_small_wait")
        upd["w_down"] = _sum_and_adamw(r_dn, w_down, m_w_down, v_w_down, "adamw_w_down")
        upd["w_up"] = [transposed(o) for o in _sum_and_adamw(r_up, transposed(w_up), transposed(m_w_up), transposed(v_w_up),
                                                             "adamw_w_up")]
        upd["w_out"] = _sum_and_adamw(r_o, w_out, m_w_out, v_w_out, "adamw_w_out")
        upd["small"] = _sum_and_adamw(r_small, _pack_small(weights), _pack_small(m_in), _pack_small(v_in), "adamw_small")

    grad_x, d_g1, sent_in = _local_step(
        x[0], loss_target[0], w_in_t, late, send_early, send_small, update_early, dn_conv_full, ffn_conv_full,
        attn_norm_g + token[0:1, 0:1], dn_a_log, dn_dt_bias, dn_out_norm_g, sg_norm_g, sg_w, sg_b, ffn_norm_g, ffn_conv_b,
        final_norm_g, n_in)

    norm_rows = D_MODEL // LANES
    r_g1, = _all_to_all([jnp.broadcast_to(d_g1.reshape(1, norm_rows, LANES), (N_DEV, norm_rows, LANES))])
    r_in, = _exchange_wait(sent_in, r_g1, "send_dw_in_wait")
    upd["w_in"] = [transposed(o) for o in _sum_and_adamw(r_in, transposed(w_in), transposed(m_w_in), transposed(v_w_in),
                                                         "adamw_w_in")]
    small_upd = upd.pop("small")
    as_rows = lambda a: a.reshape(norm_rows, LANES)
    norm_upd = _sum_and_adamw(r_g1, as_rows(attn_norm_g), as_rows(m_attn_norm_g), as_rows(v_attn_norm_g), "adamw_attn_norm")
    results = []
    for i in range(4):
        named = _unpack_small(small_upd[i], weights)
        named.update({n: upd[n][i] for n in upd})
        named["attn_norm_g"] = norm_upd[i].reshape(attn_norm_g.shape)
        results.append(named)

    loss = small_upd[0].reshape(-1)[sum(weights[n].size for n in SMALL)]
    return (loss, grad_x[None], *[r[n] for r in results for n in WEIGHT_ORDER])


def _local_step(x2d, tgt, w_in_t, late, send_early, send_small, update_early, dn_conv_full, ffn_conv_full, attn_norm_g,
                dn_a_log, dn_dt_bias, dn_out_norm_g, sg_norm_g, sg_w, sg_b, ffn_norm_g, ffn_conv_b, final_norm_g, n_in):
    t = x2d.shape[0]
    g1, g2, gf = attn_norm_g, ffn_norm_g, final_norm_g.reshape(1, D_MODEL)
    a_log4, dt_bias4 = _lanes4(dn_a_log), _lanes4(dn_dt_bias)
    sg_w3 = sg_w[0]
    sg_b_t = sg_b[0].T
    conv_b = ffn_conv_b

    p, h1, rstd1 = _rmsnorm_matmul(x2d, g1, w_in_t, "norm_in_proj", 512)
    q, k, v, beta4, g4 = _dn_prep(p, dn_conv_full, a_log4, dt_bias4)
    mix_half, *dn_saved = _dn_forward(q, k, v, beta4, g4, p, dn_out_norm_g)
    mix = _sg_forward(p, sg_norm_g, sg_w3, sg_b_t, mix_half)
    w_out_full = late["w_out"](mix)
    x2 = _matmul(mix, w_out_full, "nn", "out_proj", (1024, 1024, 1024), add=x2d)
    w_up_t = late["w_up_t"](x2)
    up, act, h2, rstd2 = _norm_up_ffn(x2, g2, w_up_t, ffn_conv_full, conv_b)
    w_down_full = late["w_down"](act)
    fn, outs = _final_loss_rows(t, D_MODEL)
    loss_lanes, dx3, dx3b, d_gf = _matmul_rows(act, w_down_full, "nn", "down_proj_loss", 512,
                                               [(x2, "rows"), (tgt, "rows"), (gf, "whole")], outs, fn)

    dact = _matmul(dx3b, w_down_full, "nt", "down_proj_dx", (512, D_FF, D_MODEL))
    d_w_down = _matmul(act, dx3b, "tn", "down_proj_dw", (256, 1024, t), out_dtype=BF16)
    sent_down, token = send_early([d_w_down.reshape(N_DEV, D_FF // N_DEV, D_MODEL)], d_w_down, "send_dw_down")
    dup, d_ffn_conv, d_ffn_conv_b = _ffn_bwd(up, ffn_conv_full, conv_b + token[0:1, 0:1], dact)
    fn, outs = _rmsnorm_bwd_rows(t, D_MODEL)
    dx2, dx2b, d_g2 = _matmul_rows(dup, w_up_t, "nn", "up_proj_dx_norm", 256,
                                   [(x2, "rows"), (rstd2, "rows"), (g2, "whole"), (dx3, "rows")], outs, fn)
    d_w_up_t = _matmul(dup, h2, "tn", "up_proj_dw", (512, 1024, t), out_dtype=BF16)
    dmix = _matmul(dx2b, w_out_full, "nt", "out_proj_dx", (1024, 1024, 1024))
    d_w_out = _matmul(mix, dx2b, "tn", "out_proj_dw", (512, 1024, t), out_dtype=BF16)
    sent_up_out, token = send_early(
        [d_w_up_t.reshape(N_DEV, 2 * D_FF // N_DEV, D_MODEL), d_w_out.reshape(N_DEV, D_MODEL // N_DEV, D_MODEL)],
        d_w_out, "send_dw_up_out")
    dp, d_sg_norm, d_sg_w, d_sg_b_t = _sg_backward(p, sg_norm_g + token[0:1, 0:1], sg_w3, sg_b_t, dmix)
    dq, dk, dv, dbeta4, dg4, dp, d_dn_norm = _dn_backward(q, k, v, beta4, g4, p, dn_out_norm_g, dn_saved, dmix, dp)
    dc_dn, d_dn_conv, dp, d_a_log4, d_dt_bias4 = _dn_prep_bwd(p, dn_conv_full, a_log4, dt_bias4, dq, dk, dv, dbeta4, dg4, dp)
    small_grads = dict(
        attn_norm_g=jnp.zeros_like(attn_norm_g), dn_conv_w=d_dn_conv, dn_a_log=d_a_log4[:, :N_HEADS],
        dn_dt_bias=d_dt_bias4[:, :N_HEADS], dn_out_norm_g=d_dn_norm, sg_norm_g=d_sg_norm, sg_w=d_sg_w,
        sg_b=d_sg_b_t[:, :SG_GROUPS].T, ffn_norm_g=d_g2, ffn_conv_w=d_ffn_conv, ffn_conv_b=d_ffn_conv_b, final_norm_g=d_gf)
    sent_small, token = send_small(small_grads, loss_lanes, d_dn_conv)
    dp = _conv_bwd_input(dc_dn, dn_conv_full + token[0:1, 0:1], "dn_conv_dx", out_cols=PROJ_PAD, into=dp)
    d_w_in_t = _matmul(dp, h1, "tn", "in_proj_dw", (PROJ_PAD // 5, 1024, t), out_dtype=BF16)
    sent_in, token = send_early([d_w_in_t[:N_DEV * n_in].reshape(N_DEV, n_in, D_MODEL)], d_w_in_t, "send_dw_in")
    update_early(sent_down, sent_up_out, sent_small, token)
    fn, outs = _rmsnorm_bwd_rows(t, D_MODEL)
    grad_x, _, d_g1 = _matmul_rows(dp, w_in_t, "nn", "in_proj_dx_norm", 512,
                                   [(x2d, "rows"), (rstd1, "rows"), (g1 + token[0:1, 0:1], "whole"), (dx2, "rows")], outs, fn)

    return grad_x, d_g1, sent_in
```

```python
import math

import jax
import jax.numpy as jnp
from jax import lax
from jax.experimental import pallas as pl
from jax.experimental.pallas import tpu as pltpu

F32 = jnp.float32
BF16 = jnp.bfloat16
HI = lax.Precision.HIGHEST

D_MODEL = 1024
DN_WIDTH = 512
HEAD_DIM = 128
N_HEADS = 4
SG_WIDTH = 512
SG_GROUPS = 4
SG_DIM = 128
SG_BLOCK = 128
D_FF = 2816
CHUNK = 64
CONV_K = 4
FFN_CONV = 3
EPS = 1e-6
PROJ_MAIN = 3072
PROJ_PAD = 3200
GELU_C = math.sqrt(2.0 / math.pi)
N_DEV = 8
LANES = 128
SUBLANES = 8
HALO = SUBLANES
VMEM_LIMIT = 48 * 1024 * 1024

ADAM_LR = 0.001
ADAM_B1 = 0.9
ADAM_B2 = 0.999
ADAM_EPS = 1e-08
ADAM_WD = 0.01
ADAM_STEP = 10

MESH_ID = pl.DeviceIdType.MESH


def _pcall(body, **kw):
    return pl.pallas_call(body, **kw)


def _params(*sem):
    return pltpu.CompilerParams(dimension_semantics=sem, vmem_limit_bytes=VMEM_LIMIT)


def _pick(n, cap):
    best = None
    for t in range(LANES, cap + 1, LANES):
        if n % t == 0:
            best = t
    return best if best else n


FAST, EXACT = "bf16 operands, one pass", "f32 operands, six bf16 passes"


def dot_f32(a, b, dims, tier):
    if tier == FAST:
        return lax.dot_general(a.astype(BF16), b.astype(BF16), dims, preferred_element_type=F32)
    return lax.dot_general(a, b, dims, precision=HI, preferred_element_type=F32)


def dot_nn(a, b, tier=EXACT):
    return dot_f32(a, b, (((1,), (0,)), ((), ())), tier)


def dot_nt(a, b, tier=EXACT):
    return dot_f32(a, b, (((1,), (1,)), ((), ())), tier)


def dot_tn(a, b, tier=EXACT):
    return dot_f32(a, b, (((0,), (0,)), ((), ())), tier)


def sigmoid(x):
    return 0.5 * jnp.tanh(0.5 * x) + 0.5


def silu(x):
    return x * sigmoid(x)


def silu_grad(x):
    s = sigmoid(x)
    return s * (1.0 + x * (1.0 - s))


def gelu(x):
    return 0.5 * x * (1.0 + jnp.tanh(GELU_C * (x + 0.044715 * x * x * x)))


def gelu_grad(x):
    t = jnp.tanh(GELU_C * (x + 0.044715 * x * x * x))
    return 0.5 * (1.0 + t) + 0.5 * x * (1.0 - t * t) * GELU_C * (1.0 + 3.0 * 0.044715 * x * x)


def softplus(z):
    return jnp.maximum(z, 0.0) + jnp.log(1.0 + jnp.exp(-jnp.abs(z)))


def rms_fwd(x, g):
    r = lax.rsqrt(jnp.mean(x * x, axis=-1, keepdims=True) + EPS)
    return x * r * g, r


def rms_bwd(x, r, g, dy):
    dyg = dy * g
    xr = x * r
    dx = r * (dyg - xr * jnp.mean(dyg * xr, axis=-1, keepdims=True))
    return dx, dy * xr


def l2_fwd(x):
    r = lax.rsqrt(jnp.sum(x * x, axis=-1, keepdims=True) + EPS)
    return x * r, r


def l2_bwd(x, r, dy):
    xr = x * r
    return r * (dy - xr * jnp.sum(dy * xr, axis=-1, keepdims=True))


def _tri_masks(n):
    row = lax.broadcasted_iota(jnp.int32, (n, n), 0)
    col = lax.broadcasted_iota(jnp.int32, (n, n), 1)
    return row >= col, row > col


def chunk_cumsum(g4):
    incl, _ = _tri_masks(g4.shape[0])
    return dot_nn(incl.astype(F32), g4)


STACK = N_HEADS * CHUNK
DN_FWD_CHUNKS = 8
DN_CHUNKS = 4


def _head_rows(h):
    return slice(h * CHUNK, (h + 1) * CHUNK)


def _stack_heads(x):
    return jnp.concatenate([x[:, h * HEAD_DIM:(h + 1) * HEAD_DIM] for h in range(N_HEADS)], axis=0)


def _stack_lanes(x4):
    return jnp.concatenate([x4[:, h:h + 1] for h in range(N_HEADS)], axis=0)


def _per_head(fn):
    return jnp.concatenate([fn(h) for h in range(N_HEADS)], axis=0)


def _unit_lower_inverses(l_strict, order):
    c = l_strict[0].shape[0]
    row = lax.broadcasted_iota(jnp.int32, (c, c), 0)
    col = lax.broadcasted_iota(jnp.int32, (c, c), 1)
    eye = (row == col).astype(F32)
    p = [-l for l in l_strict]
    a = [eye + n for n in p]
    for _ in range(int(math.log2(order)) - 1):
        p = [dot_nn(x, x, FAST) for x in p]
        a = [x + dot_nn(x, y, FAST) for x, y in zip(a, p)]
    return a


def dn_chunks_local(chunks, inverses=None):
    row = lax.broadcasted_iota(jnp.int32, (STACK, STACK), 0)
    col = lax.broadcasted_iota(jnp.int32, (STACK, STACK), 1)
    same = (row // CHUNK) == (col // CHUNK)
    incl = jnp.logical_and(same, row >= col)
    strict = jnp.logical_and(same, row > col)
    locs = []
    for q, k, v, beta, gc4 in chunks:
        gc_col = _stack_lanes(gc4)
        gc_row = jnp.sum(jnp.where(row == col, gc_col, 0.0), axis=0, keepdims=True)
        decay = jnp.where(incl, jnp.exp(jnp.minimum(gc_col - gc_row, 0.0)), 0.0)
        gamma = jnp.exp(gc_col)
        gc_last = jnp.concatenate([jnp.broadcast_to(gc4[CHUNK - 1:CHUNK, h:h + 1], (CHUNK, 1)) for h in range(N_HEADS)], axis=0)
        tau = jnp.exp(gc_last - gc_col)
        kb = k * beta
        locs.append(dict(decay=decay, gamma=gamma, tau=tau, cd=jnp.exp(gc_last), kb=kb, qd=q * gamma, kt=k * tau,
                         incl=incl, strict=strict))
    for loc, (q, k, v, beta, gc4) in zip(locs, chunks):
        loc["l_mat"] = jnp.where(strict, dot_nt(loc["kb"], k, FAST) * loc["decay"], 0.0)
    if inverses is None:
        inverses = _unit_lower_inverses([loc["l_mat"] for loc in locs], CHUNK)
    for loc, a_inv in zip(locs, inverses):
        loc["a_inv"] = a_inv
    for loc, (q, k, v, beta, gc4) in zip(locs, chunks):
        sol = dot_nn(loc["a_inv"], jnp.concatenate([v * beta, loc["kb"] * loc["gamma"]], axis=1), FAST)
        loc.update(sol=sol, value=sol[:, :HEAD_DIM], kcd=sol[:, HEAD_DIM:])
        loc["attn"] = jnp.where(incl, dot_nt(q, k, FAST) * loc["decay"], 0.0)
    return locs


def dn_chunk_state(loc, s):
    kcd, qd, kt, cd = loc["kcd"], loc["qd"], loc["kt"], loc["cd"]
    v_new = loc["value"] - _per_head(lambda h: dot_nn(kcd[_head_rows(h)], s[h], FAST))
    o = _per_head(lambda h: dot_nn(qd[_head_rows(h)], s[h], FAST)) + dot_nn(loc["attn"], v_new, FAST)
    s_new = [s[h] * cd[h * CHUNK:h * CHUNK + 1, :] + dot_tn(kt[_head_rows(h)], v_new[_head_rows(h)], FAST)
             for h in range(N_HEADS)]
    loc["v_new"] = v_new
    return o, s_new


def dn_chunks_bwd(items, ds_last):
    hr = _head_rows
    n = len(items)
    pre = []
    for q, k, v, beta, loc, s, do in items:
        pre.append(dict(
            dv_part=dot_tn(loc["attn"], do, FAST),
            dattn=jnp.where(loc["incl"], dot_nt(do, loc["v_new"], FAST), 0.0),
            dqd=_per_head(lambda h: dot_nt(do[hr(h)], s[h], FAST)),
            ds_part=[dot_tn(loc["qd"][hr(h)], do[hr(h)], FAST) for h in range(N_HEADS)]))
    ds_new_of, dv_new_of = [None] * n, [None] * n
    ds = ds_last
    for c in reversed(range(n)):
        loc = items[c][4]
        ds_new_of[c] = ds
        dv_new = pre[c]["dv_part"] + _per_head(lambda h: dot_nn(loc["kt"][hr(h)], ds[h], FAST))
        dv_new_of[c] = dv_new
        ds = [pre[c]["ds_part"][h] + ds[h] * loc["cd"][h * CHUNK:h * CHUNK + 1, :]
              - dot_tn(loc["kcd"][hr(h)], dv_new[hr(h)], FAST) for h in range(N_HEADS)]
    is_last = (lax.broadcasted_iota(jnp.int32, (STACK, 1), 0) % CHUNK) == CHUNK - 1
    out = []
    for c, (q, k, v, beta, loc, s, do) in enumerate(items):
        decay, gamma, tau, cd, kb = loc["decay"], loc["gamma"], loc["tau"], loc["cd"], loc["kb"]
        dv_new, ds_new, dattn, dqd = dv_new_of[c], ds_new_of[c], pre[c]["dattn"], pre[c]["dqd"]
        dkt = _per_head(lambda h: dot_nt(loc["v_new"][hr(h)], ds_new[h], FAST))
        dkcd = -_per_head(lambda h: dot_nt(dv_new[hr(h)], s[h], FAST))
        drhs = dot_tn(loc["a_inv"], jnp.concatenate([dv_new, dkcd], axis=1), FAST)
        dvb, dkbg = drhs[:, :HEAD_DIM], drhs[:, HEAD_DIM:]
        dl = jnp.where(loc["strict"], -dot_nt(drhs, loc["sol"], FAST), 0.0)
        dkk = dl * decay
        dqk = dattn * decay
        e = dl * loc["l_mat"] + dattn * loc["attn"]
        dgc = jnp.sum(e, axis=1, keepdims=True) - jnp.sum(e, axis=0, keepdims=True).T
        dkb = dot_nn(dkk, k, FAST) + dkbg * gamma
        dk = dot_tn(dkk, kb, FAST) + dot_tn(dqk, q, FAST) + dkt * tau
        dq = dot_nn(dqk, k, FAST) + dqd * gamma
        dgamma = jnp.sum(dkbg * kb, axis=1, keepdims=True) + jnp.sum(dqd * q, axis=1, keepdims=True)
        dtau_tau = jnp.sum(dkt * k, axis=1, keepdims=True) * tau
        dgc = dgc + dgamma * gamma - dtau_tau

        def last_term(h):
            dcd = jnp.sum(jnp.sum(ds_new[h] * s[h], axis=1, keepdims=True), axis=0, keepdims=True)
            total = jnp.sum(dtau_tau[hr(h)], axis=0, keepdims=True) + dcd * cd[h * CHUNK:h * CHUNK + 1, :]
            return jnp.broadcast_to(total, (CHUNK, 1))

        dgc = dgc + jnp.where(is_last, _per_head(last_term), 0.0)
        dk = dk + dkb * beta
        dbeta = jnp.sum(dkb * k, axis=1, keepdims=True) + jnp.sum(dvb * v, axis=1, keepdims=True)
        out.append((dq, dk, dvb * beta, dbeta, dgc))
    return out, ds


def _token_tile(t):
    return _pick(t, 256)


STRIP = 32


def _for_strips(n_rows, rows, fn, start=0):
    def step(r, carry):
        fn(pl.multiple_of(r * rows, rows))
        return carry

    lax.fori_loop(start, n_rows // rows, step, 0)


def _fold_rows(x):
    out = x[0:SUBLANES, :]
    for i in range(1, x.shape[0] // SUBLANES):
        out = out + x[i * SUBLANES:(i + 1) * SUBLANES, :]
    return out


def _matmul(a, b, mode, name, tiles, add=None, out_dtype=F32):
    if mode == "nn":
        (m, k), n = a.shape, b.shape[1]
    elif mode == "nt":
        (m, k), n = a.shape, b.shape[0]
    else:
        (k, m), n = a.shape, b.shape[1]
    tm, tn, tk = min(tiles[0], m), min(tiles[1], n), min(tiles[2], k)
    assert m % tm == 0 and n % tn == 0 and k % tk == 0, (name, m, n, k, tiles)
    nk = k // tk
    dims = {"nn": (((1,), (0,)), ((), ())), "nt": (((1,), (1,)), ((), ())), "tn": (((0,), (0,)), ((), ()))}[mode]

    def finish(res, add_ref, o_ref):
        if add_ref is not None:
            res = res + add_ref[...]
        o_ref[...] = res.astype(o_ref.dtype)

    def body(*refs):
        a_ref, b_ref = refs[0], refs[1]
        add_ref = refs[2] if add is not None else None
        o_ref = refs[3] if add is not None else refs[2]
        part = lax.dot_general(a_ref[...], b_ref[...], dims, preferred_element_type=F32)
        if nk == 1:
            finish(part, add_ref, o_ref)
            return
        acc_ref = refs[-1]
        kk = pl.program_id(2)

        @pl.when(kk == 0)
        def _():
            acc_ref[...] = part

        @pl.when(kk > 0)
        def _():
            acc_ref[...] += part

        @pl.when(kk == nk - 1)
        def _():
            finish(acc_ref[...], add_ref, o_ref)

    a_spec = pl.BlockSpec((tk, tm), lambda j, i, kk: (kk, i)) if mode == "tn" else pl.BlockSpec((tm, tk), lambda j, i, kk: (i, kk))
    b_spec = pl.BlockSpec((tn, tk), lambda j, i, kk: (j, kk)) if mode == "nt" else pl.BlockSpec((tk, tn), lambda j, i, kk: (kk, j))
    o_spec = pl.BlockSpec((tm, tn), lambda j, i, kk: (i, j))
    in_specs = [a_spec, b_spec] + ([o_spec] if add is not None else [])
    args = (a, b) + ((add,) if add is not None else ())
    return _pcall(
        body, grid=(n // tn, m // tm, nk), in_specs=in_specs, out_specs=o_spec,
        out_shape=jax.ShapeDtypeStruct((m, n), out_dtype),
        scratch_shapes=[pltpu.VMEM((tm, tn), F32)] if nk > 1 else [],
        compiler_params=_params("parallel", "parallel", "arbitrary"), name=name)(*args)


def _matmul_rows(a, b, mode, name, tm, extra, outs, fn):
    m, k = a.shape
    n = b.shape[1] if mode == "nn" else b.shape[0]
    tm = min(tm, m)
    dims = (((1,), (0,)), ((), ())) if mode == "nn" else (((1,), (1,)), ((), ()))

    def spec(shape, kind):
        if kind == "rows":
            return pl.BlockSpec((tm, shape[1]), lambda i: (i, 0))
        return pl.BlockSpec(shape, lambda i: (0,) * len(shape))

    def body(a_ref, b_ref, *refs):
        rows = lax.dot_general(a_ref[...], b_ref[...], dims, preferred_element_type=F32)
        fn(rows, pl.program_id(0) == 0, *refs)

    return _pcall(
        body, grid=(m // tm,),
        in_specs=[pl.BlockSpec((tm, k), lambda i: (i, 0)), pl.BlockSpec(b.shape, lambda i: (0, 0))]
        + [spec(x.shape, kind) for x, kind in extra],
        out_specs=[spec(shape, kind) for shape, _, kind in outs],
        out_shape=[jax.ShapeDtypeStruct(shape, dtype) for shape, dtype, _ in outs],
        compiler_params=_params("arbitrary"), name=name)(a, b, *[x for x, _ in extra])


def _rmsnorm_matmul(x, g, b_t, name, tm):
    t, d = x.shape
    n = b_t.shape[0]
    tm = min(tm, t)

    def body(x_ref, g_ref, b_ref, o_ref, h_ref, r_ref):
        y, r = rms_fwd(x_ref[...], g_ref[...])
        h = y.astype(BF16)
        h_ref[...] = h
        r_ref[...] = r
        o_ref[...] = lax.dot_general(h, b_ref[...], (((1,), (1,)), ((), ())), preferred_element_type=F32)

    rows = lambda w: pl.BlockSpec((tm, w), lambda i: (i, 0))
    return _pcall(
        body, grid=(t // tm,),
        in_specs=[rows(d), pl.BlockSpec((1, d), lambda i: (0, 0)), pl.BlockSpec((n, d), lambda i: (0, 0))],
        out_specs=[rows(n), rows(d), rows(1)],
        out_shape=[jax.ShapeDtypeStruct((t, n), F32), jax.ShapeDtypeStruct((t, d), BF16), jax.ShapeDtypeStruct((t, 1), F32)],
        compiler_params=_params("parallel"), name=name)(x, g, b_t)


def _rmsnorm_bwd_rows(t, d):
    def fn(dh, first, x_ref, r_ref, g_ref, dres_ref, dx_ref, dxb_ref, dg_ref):
        dx, dg_rows = rms_bwd(x_ref[...], r_ref[...], g_ref[...], dh)
        dx = dx + dres_ref[...]
        dx_ref[...] = dx
        dxb_ref[...] = dx.astype(BF16)

        @pl.when(first)
        def _():
            dg_ref[...] = jnp.zeros_like(dg_ref)

        dg_ref[...] += jnp.sum(dg_rows, axis=0, keepdims=True)

    return fn, [((t, d), F32, "rows"), ((t, d), BF16, "rows"), ((1, d), F32, "whole")]


def _final_loss_rows(t, d):
    def fn(rows, first, res_ref, t_ref, g_ref, loss_ref, dx_ref, dxb_ref, dg_ref):
        @pl.when(first)
        def _():
            loss_ref[...] = jnp.zeros_like(loss_ref)
            dg_ref[...] = jnp.zeros_like(dg_ref)

        x = rows + res_ref[...]
        y, r = rms_fwd(x, g_ref[...])
        err = y - t_ref[...]
        loss_ref[...] += 0.5 * jnp.sum(jnp.mean(err * err, axis=-1, keepdims=True), axis=0, keepdims=True)
        dx, dg_rows = rms_bwd(x, r, g_ref[...], err * (1.0 / d))
        dx_ref[...] = dx
        dxb_ref[...] = dx.astype(BF16)
        dg_ref[...] += jnp.sum(dg_rows, axis=0, keepdims=True)

    return fn, [((1, LANES), F32, "whole"), ((t, d), F32, "rows"), ((t, d), BF16, "rows"), ((1, d), F32, "whole")]


def _prev_halo_spec(tm, width, col_block):
    return pl.BlockSpec((HALO, width), lambda i: (jnp.maximum(i * (tm // HALO) - 1, 0), col_block))


def _history(tile_ref, halo_ref, first, row0, cols):
    if isinstance(row0, int) and row0 == 0:
        return jnp.concatenate([jnp.where(first, 0.0, halo_ref[:, cols]), tile_ref[0:STRIP, cols]], axis=0)
    return tile_ref[pl.ds(pl.multiple_of(row0 - HALO, HALO), STRIP + HALO), cols]


def _first_then_strips(n_rows, fn):
    fn(0)
    _for_strips(n_rows, STRIP, fn, start=1)


def _delays(ext, taps):
    return [ext[HALO:, :]] + [pltpu.roll(ext, j, 0)[HALO:, :] for j in range(1, taps)]


def _causal_conv(delayed, w):
    taps = len(delayed)
    out = delayed[0] * w[taps - 1:taps, :]
    for j in range(1, taps):
        out = out + delayed[j] * w[taps - 1 - j:taps - j, :]
    return out


def _advanced_conv(buf_ref, row0, cols, w):
    return _advanced(buf_ref[pl.ds(row0, STRIP + HALO), cols], w)


def _advanced(ext, w):
    taps = w.shape[0]
    out = ext[:STRIP, :] * w[taps - 1:taps, :]
    for j in range(1, taps):
        out = out + pltpu.roll(ext, STRIP + HALO - j, 0)[:STRIP, :] * w[taps - 1 - j:taps - j, :]
    return out


def _dn_prep(p, conv_w, a_log4, dt_bias4):
    t = p.shape[0]
    tm = _token_tile(t)
    w3 = 3 * DN_WIDTH

    def body(x_ref, halo_ref, pbd_ref, w_ref, alog_ref, dtb_ref, q_ref, k_ref, v_ref, beta_ref, g_ref):
        first = pl.program_id(0) == 0

        def strip(row0):
            rows = pl.ds(row0, STRIP)
            for h in range(N_HEADS):
                sl = slice(h * HEAD_DIM, (h + 1) * HEAD_DIM)
                for part, out_ref in ((0, q_ref), (1, k_ref), (2, v_ref)):
                    cols = slice(part * DN_WIDTH + h * HEAD_DIM, part * DN_WIDTH + (h + 1) * HEAD_DIM)
                    y = silu(_causal_conv(_delays(_history(x_ref, halo_ref, first, row0, cols), CONV_K), w_ref[:, cols]))
                    if part == 0:
                        y = l2_fwd(y)[0] * (HEAD_DIM ** -0.5)
                    elif part == 1:
                        y = l2_fwd(y)[0]
                    out_ref[rows, sl] = y
            head = lax.broadcasted_iota(jnp.int32, (STRIP, LANES), 1) < N_HEADS
            pbd = pbd_ref[rows, :]
            beta_ref[rows, :] = jnp.where(head, sigmoid(pbd), 0.0)
            a_raw = pltpu.roll(pbd, LANES - N_HEADS, 1)
            g_ref[rows, :] = jnp.where(head, -jnp.exp(alog_ref[...]) * softplus(a_raw + dtb_ref[...]), 0.0)

        _first_then_strips(tm, strip)

    tok = lambda w, cb: pl.BlockSpec((tm, w), lambda i: (i, cb))
    full = lambda a: pl.BlockSpec(a.shape, lambda i: (0, 0))
    return _pcall(
        body, grid=(t // tm,),
        in_specs=[tok(w3, 0), _prev_halo_spec(tm, w3, 0), tok(LANES, PROJ_MAIN // LANES),
                  full(conv_w), full(a_log4), full(dt_bias4)],
        out_specs=[tok(DN_WIDTH, 0)] * 3 + [tok(LANES, 0)] * 2,
        out_shape=[jax.ShapeDtypeStruct((t, DN_WIDTH), F32)] * 3 + [jax.ShapeDtypeStruct((t, LANES), F32)] * 2,
        compiler_params=_params("parallel"), name="dn_prep")(p, p, p, conv_w, a_log4, dt_bias4)


def _dn_prep_bwd(p, conv_w, a_log4, dt_bias4, dq, dk, dv, dbeta4, dg4, dp_buf):
    t = p.shape[0]
    tm = _token_tile(t)
    w3 = 3 * DN_WIDTH

    def body(x_ref, halo_ref, pbd_ref, w_ref, alog_ref, dtb_ref, dq_ref, dk_ref, dv_ref, dbeta_ref, dg_ref, _,
             dc_ref, dw_ref, dpbd_ref, dalog_ref, ddtb_ref, dw_acc, lane_acc):
        first = pl.program_id(0) == 0
        dw_acc[...] = jnp.zeros_like(dw_acc)
        lane_acc[...] = jnp.zeros_like(lane_acc)

        def strip(row0):
            rows = pl.ds(row0, STRIP)
            for h in range(N_HEADS):
                sl = slice(h * HEAD_DIM, (h + 1) * HEAD_DIM)
                for part, dy_ref in ((0, dq_ref), (1, dk_ref), (2, dv_ref)):
                    cols = slice(part * DN_WIDTH + h * HEAD_DIM, part * DN_WIDTH + (h + 1) * HEAD_DIM)
                    delayed = _delays(_history(x_ref, halo_ref, first, row0, cols), CONV_K)
                    c = _causal_conv(delayed, w_ref[:, cols])
                    dy = dy_ref[rows, sl]
                    if part < 2:
                        y = silu(c)
                        _, r = l2_fwd(y)
                        dy = l2_bwd(y, r, dy * (HEAD_DIM ** -0.5) if part == 0 else dy)
                    dc = dy * silu_grad(c)
                    dc_ref[rows, cols] = dc
                    for j in range(CONV_K):
                        k = CONV_K - 1 - j
                        dw_acc[k * SUBLANES:(k + 1) * SUBLANES, cols] += _fold_rows(dc * delayed[j])
            head = lax.broadcasted_iota(jnp.int32, (STRIP, LANES), 1) < N_HEADS
            pbd = pbd_ref[rows, :]
            beta = sigmoid(pbd)
            dpb = jnp.where(head, dbeta_ref[rows, :] * beta * (1.0 - beta), 0.0)
            z = pltpu.roll(pbd, LANES - N_HEADS, 1) + dtb_ref[...]
            neg_rate = -jnp.exp(alog_ref[...])
            dg = dg_ref[rows, :]
            dpa = jnp.where(head, dg * neg_rate * sigmoid(z), 0.0)
            dpbd_ref[rows, :] = (dpb + pltpu.roll(dpa, N_HEADS, 1)).astype(BF16)
            g = jnp.where(head, neg_rate * softplus(z), 0.0)
            lane_acc[0:SUBLANES, :] += _fold_rows(dg * g)
            lane_acc[SUBLANES:, :] += _fold_rows(dpa)

        _first_then_strips(tm, strip)

        @pl.when(first)
        def _():
            dw_ref[...] = jnp.zeros_like(dw_ref)
            dalog_ref[...] = jnp.zeros_like(dalog_ref)
            ddtb_ref[...] = jnp.zeros_like(ddtb_ref)

        for k in range(CONV_K):
            dw_ref[k:k + 1, :] += jnp.sum(dw_acc[k * SUBLANES:(k + 1) * SUBLANES, :], axis=0, keepdims=True)
        dalog_ref[...] += jnp.sum(lane_acc[0:SUBLANES, :], axis=0, keepdims=True)
        ddtb_ref[...] += jnp.sum(lane_acc[SUBLANES:, :], axis=0, keepdims=True)

    tok = lambda w, cb: pl.BlockSpec((tm, w), lambda i: (i, cb))
    full = lambda shape: pl.BlockSpec(shape, lambda i: (0, 0))
    return _pcall(
        body, grid=(t // tm,),
        in_specs=[tok(w3, 0), _prev_halo_spec(tm, w3, 0), tok(LANES, PROJ_MAIN // LANES),
                  full(conv_w.shape), full(a_log4.shape), full(dt_bias4.shape)] + [tok(DN_WIDTH, 0)] * 3 + [tok(LANES, 0)] * 2
        + [pl.BlockSpec(memory_space=pl.ANY)],
        out_specs=[tok(w3, 0), full((CONV_K, w3)), tok(LANES, PROJ_MAIN // LANES), full((1, LANES)), full((1, LANES))],
        out_shape=[jax.ShapeDtypeStruct((t, w3), F32), jax.ShapeDtypeStruct((CONV_K, w3), F32),
                   jax.ShapeDtypeStruct(dp_buf.shape, dp_buf.dtype),
                   jax.ShapeDtypeStruct((1, LANES), F32), jax.ShapeDtypeStruct((1, LANES), F32)],
        input_output_aliases={11: 2},
        scratch_shapes=[pltpu.VMEM((CONV_K * SUBLANES, w3), F32), pltpu.VMEM((2 * SUBLANES, LANES), F32)],
        compiler_params=_params("arbitrary"), name="dn_prep_bwd")(p, p, p, conv_w, a_log4, dt_bias4, dq, dk, dv, dbeta4, dg4, dp_buf)


def _conv_bwd_input(dc, w, name, out_cols=None, col_block=0, into=None):
    t, c = dc.shape
    taps = w.shape[0]
    tm = _token_tile(t)
    ct = _pick(c, 1536)
    n_tok = t // tm
    out_cols = c if out_cols is None else out_cols

    def body(dc_ref, next_ref, w_ref, *rest):
        dx_ref = rest[-1]
        last = pl.program_id(0) == n_tok - 1

        def strip(row0):
            for c0 in range(0, ct, LANES):
                cols = slice(c0, c0 + LANES)
                dx_ref[pl.ds(row0, STRIP), cols] = _advanced_conv(dc_ref, row0, cols, w_ref[:, cols]).astype(BF16)

        _for_strips(tm - STRIP, STRIP, strip)
        for c0 in range(0, ct, LANES):
            cols = slice(c0, c0 + LANES)
            ext = jnp.concatenate([dc_ref[tm - STRIP:tm, cols], jnp.where(last, 0.0, next_ref[:, cols])], axis=0)
            dx_ref[tm - STRIP:tm, cols] = _advanced(ext, w_ref[:, cols]).astype(BF16)

    in_specs = [pl.BlockSpec((tm, ct), lambda i, j: (i, j)),
                pl.BlockSpec((HALO, ct), lambda i, j: (jnp.minimum((i + 1) * (tm // HALO), t // HALO - 1), j)),
                pl.BlockSpec((taps, ct), lambda i, j: (0, j))]
    args = (dc, dc, w)
    aliases = {}
    if into is not None:
        in_specs.append(pl.BlockSpec(memory_space=pl.ANY))
        args += (into,)
        aliases = {3: 0}
    return _pcall(
        body, grid=(n_tok, c // ct), in_specs=in_specs,
        out_specs=pl.BlockSpec((tm, ct), lambda i, j: (i, j + col_block)),
        out_shape=jax.ShapeDtypeStruct((t, out_cols), BF16), input_output_aliases=aliases,
        compiler_params=_params("parallel", "parallel"), name=name)(*args)


def _dn_forward(q, k, v, beta4, g4, p, norm_g):
    t = q.shape[0]
    n = t // CHUNK
    nc = DN_FWD_CHUNKS
    rows_per_step = nc * CHUNK

    def body(q_ref, k_ref, v_ref, b_ref, g_ref, gate_ref, ng_ref, mix_ref, s_all_ref, ainv_ref, s_ref):
        @pl.when(pl.program_id(0) == 0)
        def _():
            s_ref[...] = jnp.zeros_like(s_ref)

        chunks = []
        for c in range(nc):
            rows = slice(c * CHUNK, (c + 1) * CHUNK)
            chunks.append((_stack_heads(q_ref[rows, :]), _stack_heads(k_ref[rows, :]), _stack_heads(v_ref[rows, :]),
                           _stack_lanes(b_ref[rows, :]), chunk_cumsum(g_ref[rows, :])))
        locs = dn_chunks_local(chunks)
        s = [s_ref[h] for h in range(N_HEADS)]
        for c in range(nc):
            rows = slice(c * CHUNK, (c + 1) * CHUNK)
            ainv_ref[c] = locs[c]["a_inv"].astype(BF16)
            for h in range(N_HEADS):
                s_all_ref[c, h] = s[h]
            o, s = dn_chunk_state(locs[c], s)
            o_n, _ = rms_fwd(o, ng_ref[...])
            for h in range(N_HEADS):
                sl = slice(h * HEAD_DIM, (h + 1) * HEAD_DIM)
                mix_ref[rows, sl] = (o_n[_head_rows(h)] * silu(gate_ref[rows, sl])).astype(BF16)
        for h in range(N_HEADS):
            s_ref[h] = s[h]

    ch = lambda w, cb: pl.BlockSpec((rows_per_step, w), lambda i: (i, cb))
    per_chunk = lambda *shape: pl.BlockSpec((nc,) + shape, lambda i: (i,) + (0,) * len(shape))
    return _pcall(
        body, grid=(n // nc,),
        in_specs=[ch(DN_WIDTH, 0)] * 3 + [ch(LANES, 0)] * 2 + [ch(DN_WIDTH, 3), pl.BlockSpec((1, HEAD_DIM), lambda i: (0, 0))],
        out_specs=[ch(DN_WIDTH, 0), per_chunk(N_HEADS, HEAD_DIM, HEAD_DIM), per_chunk(STACK, STACK)],
        out_shape=[jax.ShapeDtypeStruct((t, DN_WIDTH + SG_WIDTH), BF16), jax.ShapeDtypeStruct((n, N_HEADS, HEAD_DIM, HEAD_DIM), F32),
                   jax.ShapeDtypeStruct((n, STACK, STACK), BF16)],
        scratch_shapes=[pltpu.VMEM((N_HEADS, HEAD_DIM, HEAD_DIM), F32)],
        compiler_params=_params("arbitrary"), name="dn_forward")(q, k, v, beta4, g4, p, norm_g)


def _dn_backward(q, k, v, beta4, g4, p, norm_g, saved, dmix, dp_buf):
    t = q.shape[0]
    n = t // CHUNK
    steps = n // DN_CHUNKS
    rows_per_step = DN_CHUNKS * CHUNK

    def body(q_ref, k_ref, v_ref, b_ref, g_ref, gate_ref, ng_ref, s_in_ref, ainv_ref, dmix_ref, _,
             dq_ref, dk_ref, dv_ref, db_ref, dg_ref, dgate_ref, dng_ref, ds_ref):
        @pl.when(pl.program_id(0) == 0)
        def _():
            ds_ref[...] = jnp.zeros_like(ds_ref)
            dng_ref[...] = jnp.zeros_like(dng_ref)

        chunks = []
        for c in range(DN_CHUNKS):
            rows = slice(c * CHUNK, (c + 1) * CHUNK)
            chunks.append((_stack_heads(q_ref[rows, :]), _stack_heads(k_ref[rows, :]), _stack_heads(v_ref[rows, :]),
                           _stack_lanes(b_ref[rows, :]), chunk_cumsum(g_ref[rows, :])))
        items = []
        for c, loc in enumerate(dn_chunks_local(chunks, [ainv_ref[c] for c in range(DN_CHUNKS)])):
            rows = slice(c * CHUNK, (c + 1) * CHUNK)
            s = [s_in_ref[c, h] for h in range(N_HEADS)]
            o, _ = dn_chunk_state(loc, s)
            o_n, r = rms_fwd(o, ng_ref[...])
            gate = _stack_heads(gate_ref[rows, :])
            dmx = _stack_heads(dmix_ref[rows, :])
            dgate = dmx * o_n * silu_grad(gate)
            do, dng_rows = rms_bwd(o, r, ng_ref[...], dmx * silu(gate))
            dng_ref[...] += jnp.sum(dng_rows, axis=0, keepdims=True)
            for h in range(N_HEADS):
                dgate_ref[rows, h * HEAD_DIM:(h + 1) * HEAD_DIM] = dgate[_head_rows(h)].astype(BF16)
            items.append((*chunks[c][:4], loc, s, do))
        grads, ds = dn_chunks_bwd(items, [ds_ref[h] for h in range(N_HEADS)])
        lane = lax.broadcasted_iota(jnp.int32, (CHUNK, LANES), 1)
        _, strict = _tri_masks(CHUNK)
        for c in range(DN_CHUNKS):
            rows = slice(c * CHUNK, (c + 1) * CHUNK)
            dq, dk, dv, dbeta, dgc = grads[c]
            db4 = jnp.zeros((CHUNK, LANES), F32)
            dgc4 = jnp.zeros((CHUNK, LANES), F32)
            for h in range(N_HEADS):
                sl = slice(h * HEAD_DIM, (h + 1) * HEAD_DIM)
                head_rows = _head_rows(h)
                dq_ref[rows, sl] = dq[head_rows]
                dk_ref[rows, sl] = dk[head_rows]
                dv_ref[rows, sl] = dv[head_rows]
                db4 = jnp.where(lane == h, dbeta[head_rows], db4)
                dgc4 = jnp.where(lane == h, dgc[head_rows], dgc4)
            db_ref[rows, :] = db4
            dg_ref[rows, :] = dot_nn(jnp.logical_not(strict).astype(F32), dgc4)
        for h in range(N_HEADS):
            ds_ref[h] = ds[h]

    rev = lambda w, cb: pl.BlockSpec((rows_per_step, w), lambda i: (steps - 1 - i, cb))
    per_chunk = lambda a: pl.BlockSpec((DN_CHUNKS,) + a.shape[1:], lambda i: (steps - 1 - i,) + (0,) * (a.ndim - 1))
    return _pcall(
        body, grid=(steps,),
        in_specs=[rev(DN_WIDTH, 0)] * 3 + [rev(LANES, 0)] * 2 + [rev(DN_WIDTH, 3), pl.BlockSpec((1, HEAD_DIM), lambda i: (0, 0))]
        + [per_chunk(a) for a in saved] + [rev(DN_WIDTH, 0), pl.BlockSpec(memory_space=pl.ANY)],
        out_specs=[rev(DN_WIDTH, 0)] * 3 + [rev(LANES, 0)] * 2 + [rev(DN_WIDTH, 3), pl.BlockSpec((1, HEAD_DIM), lambda i: (0, 0))],
        out_shape=[jax.ShapeDtypeStruct((t, DN_WIDTH), F32)] * 3 + [jax.ShapeDtypeStruct((t, LANES), F32)] * 2
        + [jax.ShapeDtypeStruct(dp_buf.shape, dp_buf.dtype), jax.ShapeDtypeStruct((1, HEAD_DIM), F32)],
        input_output_aliases={10: 5},
        scratch_shapes=[pltpu.VMEM((N_HEADS, HEAD_DIM, HEAD_DIM), F32)],
        compiler_params=_params("arbitrary"), name="dn_backward")(q, k, v, beta4, g4, p, norm_g, *saved, dmix, dp_buf)


def _sg_mask():
    row = lax.broadcasted_iota(jnp.int32, (SG_BLOCK, SG_BLOCK), 0)
    col = lax.broadcasted_iota(jnp.int32, (SG_BLOCK, SG_BLOCK), 1)
    return (col // CHUNK) <= (row // CHUNK)


def _sg_forward(p, norm_g, w_s, b_t, mix_buf):
    t = p.shape[0]

    def body(u_ref, v_ref, ng_ref, w_ref, b_ref, _, o_ref):
        mask = _sg_mask()
        groups = range(SG_GROUPS)
        sl = [slice(g * SG_DIM, (g + 1) * SG_DIM) for g in groups]
        vn = [rms_fwd(gelu(v_ref[:, sl[g]]), ng_ref[:, sl[g]])[0] for g in groups]
        s = [dot_nn(jnp.where(mask, w_ref[g], 0.0), vn[g], FAST) + b_ref[:, g:g + 1] for g in groups]
        for g in groups:
            o_ref[:, sl[g]] = (gelu(u_ref[:, sl[g]]) * s[g]).astype(BF16)

    blk = lambda cb: pl.BlockSpec((SG_BLOCK, SG_WIDTH), lambda i: (i, cb))
    return _pcall(
        body, grid=(t // SG_BLOCK,),
        in_specs=[blk(4), blk(5), pl.BlockSpec((1, SG_WIDTH), lambda i: (0, 0)),
                  pl.BlockSpec((SG_GROUPS, SG_BLOCK, SG_BLOCK), lambda i: (0, 0, 0)), pl.BlockSpec((SG_BLOCK, SG_GROUPS), lambda i: (0, 0)),
                  pl.BlockSpec(memory_space=pl.ANY)],
        out_specs=blk(1), out_shape=jax.ShapeDtypeStruct(mix_buf.shape, mix_buf.dtype), input_output_aliases={5: 0},
        compiler_params=_params("parallel"), name="sg_forward")(p, p, norm_g, w_s, b_t, mix_buf)


def _sg_backward(p, norm_g, w_s, b_t, dmix):
    t = p.shape[0]

    def body(u_ref, v_ref, ng_ref, w_ref, b_ref, do_ref, duv_ref, dng_ref, dw_ref, db_ref):
        @pl.when(pl.program_id(0) == 0)
        def _():
            dng_ref[...] = jnp.zeros_like(dng_ref)
            dw_ref[...] = jnp.zeros_like(dw_ref)
            db_ref[...] = jnp.zeros_like(db_ref)

        mask = _sg_mask()
        lane = lax.broadcasted_iota(jnp.int32, (SG_BLOCK, LANES), 1)
        groups = range(SG_GROUPS)
        sl = [slice(g * SG_DIM, (g + 1) * SG_DIM) for g in groups]
        w_m = [jnp.where(mask, w_ref[g], 0.0) for g in groups]
        vg = [gelu(v_ref[:, sl[g]]) for g in groups]
        normed = [rms_fwd(vg[g], ng_ref[:, sl[g]]) for g in groups]
        s = [dot_nn(w_m[g], normed[g][0], FAST) + b_ref[:, g:g + 1] for g in groups]
        ds = []
        db = jnp.zeros((SG_BLOCK, LANES), F32)
        for g in groups:
            u_raw, do = u_ref[:, sl[g]], do_ref[:, sl[g]]
            duv_ref[:, sl[g]] = (do * s[g] * gelu_grad(u_raw)).astype(BF16)
            ds.append(do * gelu(u_raw))
            db = jnp.where(lane == g, jnp.sum(ds[g], axis=1, keepdims=True), db)
        dw = [jnp.where(mask, dot_nt(ds[g], normed[g][0], FAST), 0.0) for g in groups]
        dvn = [dot_tn(w_m[g], ds[g], FAST) for g in groups]
        for g in groups:
            dw_ref[g] += dw[g]
            dvg, dng_rows = rms_bwd(vg[g], normed[g][1], ng_ref[:, sl[g]], dvn[g])
            dng_ref[:, sl[g]] += jnp.sum(dng_rows, axis=0, keepdims=True)
            duv_ref[:, SG_WIDTH + g * SG_DIM:SG_WIDTH + (g + 1) * SG_DIM] = (dvg * gelu_grad(v_ref[:, sl[g]])).astype(BF16)
        db_ref[...] += db

    blk = lambda cb: pl.BlockSpec((SG_BLOCK, SG_WIDTH), lambda i: (i, cb))
    const2 = lambda shape: pl.BlockSpec(shape, lambda i: (0, 0))
    w_spec = pl.BlockSpec((SG_GROUPS, SG_BLOCK, SG_BLOCK), lambda i: (0, 0, 0))
    return _pcall(
        body, grid=(t // SG_BLOCK,),
        in_specs=[blk(4), blk(5), const2((1, SG_WIDTH)), w_spec, const2((SG_BLOCK, SG_GROUPS)), blk(1)],
        out_specs=[pl.BlockSpec((SG_BLOCK, 2 * SG_WIDTH), lambda i: (i, 2)), const2((1, SG_WIDTH)), w_spec,
                   const2((SG_BLOCK, LANES))],
        out_shape=[jax.ShapeDtypeStruct((t, PROJ_PAD), BF16), jax.ShapeDtypeStruct((1, SG_WIDTH), F32),
                   jax.ShapeDtypeStruct((SG_GROUPS, SG_BLOCK, SG_BLOCK), F32), jax.ShapeDtypeStruct((SG_BLOCK, LANES), F32)],
        compiler_params=_params("arbitrary"), name="sg_backward")(p, p, norm_g, w_s, b_t, dmix)


FFN_COLS = 256


def _norm_up_ffn(x, g, w_up_t, conv_w, conv_b):
    t, d = x.shape
    tm = min(t, 256)
    blocks = D_FF // FFN_COLS
    nt = (((1,), (1,)), ((), ()))

    def body(x_ref, g_ref, w_ref, cw_ref, cb_ref, up_ref, act_ref, h_ref, r_ref, tail_ref, prev_ref):
        @pl.when(pl.program_id(0) == 0)
        def _():
            tail_ref[...] = jnp.zeros_like(tail_ref)

        y, r = rms_fwd(x_ref[...], g_ref[...])
        h = y.astype(BF16)
        h_ref[...] = h
        r_ref[...] = r

        def project(blk):
            out = []
            for half in range(2):
                cols = slice(half * D_FF + blk * FFN_COLS, half * D_FF + (blk + 1) * FFN_COLS)
                u = lax.dot_general(h, w_ref[cols, :], nt, preferred_element_type=F32)
                up_ref[:, cols] = u
                prev_ref[:, cols] = tail_ref[:, cols]
                tail_ref[:, cols] = u[tm - HALO:, :]
                out.append(cols)
            return out

        def history(row0, cols):
            if row0 == 0:
                return jnp.concatenate([prev_ref[:, cols], up_ref[0:STRIP, cols]], axis=0)
            return up_ref[row0 - HALO:row0 + STRIP, cols]

        def activate(blk, g_cols, v_cols):
            for row0 in range(0, tm, STRIP):
                for c0 in range(0, FFN_COLS, LANES):
                    gc = slice(g_cols.start + c0, g_cols.start + c0 + LANES)
                    vc = slice(v_cols.start + c0, v_cols.start + c0 + LANES)
                    cg = _causal_conv(_delays(history(row0, gc), FFN_CONV), cw_ref[:, gc]) + cb_ref[:, gc]
                    cv = _causal_conv(_delays(history(row0, vc), FFN_CONV), cw_ref[:, vc]) + cb_ref[:, vc]
                    act_ref[row0:row0 + STRIP, blk * FFN_COLS + c0:blk * FFN_COLS + c0 + LANES] = (silu(cg) * cv).astype(BF16)

        pending = None
        for blk in range(blocks):
            cols = project(blk)
            if pending is not None:
                activate(*pending)
            pending = (blk, *cols)
        activate(*pending)

    rows = lambda w: pl.BlockSpec((tm, w), lambda i: (i, 0))
    whole = lambda a: pl.BlockSpec(a.shape, lambda i: (0, 0))
    return _pcall(
        body, grid=(t // tm,),
        in_specs=[rows(d), whole(g), whole(w_up_t), whole(conv_w), whole(conv_b)],
        out_specs=[rows(2 * D_FF), rows(D_FF), rows(d), rows(1)],
        out_shape=[jax.ShapeDtypeStruct((t, 2 * D_FF), F32), jax.ShapeDtypeStruct((t, D_FF), BF16),
                   jax.ShapeDtypeStruct((t, d), BF16), jax.ShapeDtypeStruct((t, 1), F32)],
        scratch_shapes=[pltpu.VMEM((HALO, 2 * D_FF), F32), pltpu.VMEM((HALO, 2 * D_FF), F32)],
        compiler_params=_params("arbitrary"), name="norm_up_ffn")(x, g, w_up_t, conv_w, conv_b)


def _ffn_bwd(up, conv_w, conv_b, dact):
    t = up.shape[0]
    tm = _pick(t, 256)
    n_tok = t // tm
    width = 2 * D_FF

    def dconv(delayed_g, delayed_v, da, wg, wv, bg, bv):
        cg = _causal_conv(delayed_g, wg) + bg
        cv = _causal_conv(delayed_v, wv) + bv
        s = sigmoid(cg)
        return da * cv * (s * (1.0 + cg * (1.0 - s))), da * (cg * s)

    def body(up_ref, prev_ref, next_ref, da_ref, dan_ref, w_ref, b_ref, dup_ref, dw_ref, db_ref, dc_ref, dw_acc, db_acc):
        first = pl.program_id(0) == 0
        last = pl.program_id(0) == n_tok - 1
        dw_acc[...] = jnp.zeros_like(dw_acc)
        db_acc[...] = jnp.zeros_like(db_acc)

        def strip(row0):
            rows = pl.ds(row0, STRIP)
            for c0 in range(0, D_FF, LANES):
                gc, vc = slice(c0, c0 + LANES), slice(D_FF + c0, D_FF + c0 + LANES)
                del_g = _delays(_history(up_ref, prev_ref, first, row0, gc), FFN_CONV)
                del_v = _delays(_history(up_ref, prev_ref, first, row0, vc), FFN_CONV)
                dcg, dcv = dconv(del_g, del_v, da_ref[rows, gc], w_ref[:, gc], w_ref[:, vc], b_ref[:, gc], b_ref[:, vc])
                dc_ref[rows, gc] = dcg
                dc_ref[rows, vc] = dcv
                db_acc[:, gc] += _fold_rows(dcg)
                db_acc[:, vc] += _fold_rows(dcv)
                for j in range(FFN_CONV):
                    k = FFN_CONV - 1 - j
                    dw_acc[k * SUBLANES:(k + 1) * SUBLANES, gc] += _fold_rows(dcg * del_g[j])
                    dw_acc[k * SUBLANES:(k + 1) * SUBLANES, vc] += _fold_rows(dcv * del_v[j])

        _first_then_strips(tm, strip)

        for c0 in range(0, D_FF, LANES):
            gc, vc = slice(c0, c0 + LANES), slice(D_FF + c0, D_FF + c0 + LANES)

            def delayed(cols):
                return _delays(jnp.concatenate([up_ref[tm - HALO:tm, cols], next_ref[:, cols]], axis=0), FFN_CONV)

            dcg, dcv = dconv(delayed(gc), delayed(vc), dan_ref[:, gc], w_ref[:, gc], w_ref[:, vc], b_ref[:, gc], b_ref[:, vc])
            dc_ref[tm:, gc] = jnp.where(last, 0.0, dcg)
            dc_ref[tm:, vc] = jnp.where(last, 0.0, dcv)

        def strip_dx(row0):
            for c0 in range(0, width, LANES):
                cols = slice(c0, c0 + LANES)
                dup_ref[pl.ds(row0, STRIP), cols] = _advanced_conv(dc_ref, row0, cols, w_ref[:, cols]).astype(BF16)

        _for_strips(tm, STRIP, strip_dx)

        @pl.when(first)
        def _():
            dw_ref[...] = jnp.zeros_like(dw_ref)
            db_ref[...] = jnp.zeros_like(db_ref)

        for k in range(FFN_CONV):
            dw_ref[k:k + 1, :] += jnp.sum(dw_acc[k * SUBLANES:(k + 1) * SUBLANES, :], axis=0, keepdims=True)
        db_ref[...] += jnp.sum(db_acc[...], axis=0, keepdims=True)

    next_rows = lambda i: jnp.minimum((i + 1) * (tm // HALO), t // HALO - 1)
    full = lambda rows: pl.BlockSpec((rows, width), lambda i: (0, 0))
    return _pcall(
        body, grid=(n_tok,),
        in_specs=[pl.BlockSpec((tm, width), lambda i: (i, 0)),
                  pl.BlockSpec((HALO, width), lambda i: (jnp.maximum(i * (tm // HALO) - 1, 0), 0)),
                  pl.BlockSpec((HALO, width), lambda i: (next_rows(i), 0)),
                  pl.BlockSpec((tm, D_FF), lambda i: (i, 0)), pl.BlockSpec((HALO, D_FF), lambda i: (next_rows(i), 0)),
                  full(FFN_CONV), full(1)],
        out_specs=[pl.BlockSpec((tm, width), lambda i: (i, 0)), full(FFN_CONV), full(1)],
        out_shape=[jax.ShapeDtypeStruct((t, width), BF16), jax.ShapeDtypeStruct((FFN_CONV, width), F32),
                   jax.ShapeDtypeStruct((1, width), F32)],
        scratch_shapes=[pltpu.VMEM((tm + HALO, width), F32),
                        pltpu.VMEM((FFN_CONV * SUBLANES, width), F32), pltpu.VMEM((SUBLANES, width), F32)],
        compiler_params=_params("arbitrary"), name="ffn_bwd")(up, up, up, dact, dact, conv_w, conv_b)


def _my_position():
    return lax.axis_index("x"), lax.axis_index("y"), lax.axis_index("c")


COPIES = N_DEV - 1


def _all_gather(arrays):
    n = len(arrays)

    def body(*refs):
        x_refs, out_refs = refs[:n], refs[n:2 * n]
        send_sems, recv_sems, local_sems = refs[2 * n:]
        x, y, cc = _my_position()
        me, sibling = (x, y, cc), (x, y, 1 - cc)
        chips = [(1 - x, y), (x, 1 - y), (1 - x, 1 - y)]

        def block(a, px, py, pc):
            return out_refs[a].at[4 * px + 2 * py + pc]

        def copy(a, k, blk, to, src=None):
            return pltpu.make_async_remote_copy(
                src_ref=block(a, *blk) if src is None else src, dst_ref=block(a, *blk),
                send_sem=send_sems.at[a * COPIES + k], recv_sem=recv_sems.at[a * COPIES + k],
                device_id=to, device_id_type=MESH_ID)

        mine = [pltpu.make_async_copy(x_refs[a], block(a, *me), local_sems.at[a]) for a in range(n)]
        for cp in mine:
            cp.start()
        first = []
        for a in range(n):
            first.append(copy(a, 0, me, sibling, src=x_refs[a]))
            first += [copy(a, 1 + j, me, (*chip, cc), src=x_refs[a]) for j, chip in enumerate(chips)]
        for cp in first:
            cp.start()
        passed = []
        for j, chip in enumerate(chips):
            for a in range(n):
                copy(a, 1 + j, (*chip, cc), me).wait_recv()
                passed.append(copy(a, 4 + j, (*chip, cc), sibling))
                passed[-1].start()
        for a in range(n):
            copy(a, 0, sibling, me).wait_recv()
        for j, chip in enumerate(chips):
            for a in range(n):
                copy(a, 4 + j, (*chip, 1 - cc), me).wait_recv()
        for cp in first + passed:
            cp.wait_send()
        for cp in mine:
            cp.wait()

    any_spec = pl.BlockSpec(memory_space=pl.ANY)
    return _pcall(
        body, out_shape=[jax.ShapeDtypeStruct((N_DEV,) + a.shape, a.dtype) for a in arrays],
        in_specs=[any_spec] * n, out_specs=[any_spec] * n,
        scratch_shapes=[pltpu.SemaphoreType.DMA((n * COPIES,)), pltpu.SemaphoreType.DMA((n * COPIES,)),
                        pltpu.SemaphoreType.DMA((n,))],
        name="all_gather")(*arrays)


def _all_to_all(sends):
    n = len(sends)

    def body(*refs):
        send_refs, recv_refs = refs[:n], refs[n:2 * n]
        send_sems, recv_sems, local_sems = refs[2 * n:]
        x, y, cc = _my_position()
        me = 4 * x + 2 * y + cc
        mine = [pltpu.make_async_copy(send_refs[a].at[me], recv_refs[a].at[me], local_sems.at[a]) for a in range(n)]
        for cp in mine:
            cp.start()
        copies = []
        for rel in range(1, N_DEV):
            px, py, pc = x ^ (rel >> 2), y ^ ((rel >> 1) & 1), cc ^ (rel & 1)
            for a in range(n):
                copies.append(pltpu.make_async_remote_copy(
                    src_ref=send_refs[a].at[4 * px + 2 * py + pc], dst_ref=recv_refs[a].at[me],
                    send_sem=send_sems.at[a * COPIES + rel - 1], recv_sem=recv_sems.at[a * COPIES + rel - 1],
                    device_id=(px, py, pc), device_id_type=MESH_ID))
        for cp in copies:
            cp.start()
        for cp in copies:
            cp.wait()
        for cp in mine:
            cp.wait()

    any_spec = pl.BlockSpec(memory_space=pl.ANY)
    return _pcall(
        body, out_shape=[jax.ShapeDtypeStruct(s.shape, s.dtype) for s in sends],
        in_specs=[any_spec] * n, out_specs=[any_spec] * n,
        scratch_shapes=[pltpu.SemaphoreType.DMA((n * COPIES,)), pltpu.SemaphoreType.DMA((n * COPIES,)),
                        pltpu.SemaphoreType.DMA((n,))],
        name="all_to_all")(*sends)


def _hbm(a):
    return pltpu.with_memory_space_constraint(a, pltpu.HBM)


def _split_copies(send_refs, land_refs, send_sems, recv_sems, local_sems, gather):
    x, y, cc = _my_position()
    me = 4 * x + 2 * y + cc
    local, remote = [], []
    for a, (send, land) in enumerate(zip(send_refs, land_refs)):
        local.append(pltpu.make_async_copy(send if gather else send.at[me], land.at[me], local_sems.at[a]))
    for a, (send, land) in enumerate(zip(send_refs, land_refs)):
        for rel in range(1, N_DEV):
            px, py, pc = x ^ (rel >> 2), y ^ ((rel >> 1) & 1), cc ^ (rel & 1)
            remote.append(pltpu.make_async_remote_copy(
                src_ref=send if gather else send.at[4 * px + 2 * py + pc], dst_ref=land.at[me],
                send_sem=send_sems.at[a * COPIES + rel - 1], recv_sem=recv_sems.at[a * COPIES + rel - 1],
                device_id=(px, py, pc), device_id_type=MESH_ID))
    return local, remote


SPLIT_EFFECT = pltpu.SideEffectType.DATAFLOW_SIDE_EFFECTING


def _exchange_start(sends, after, gather, name):
    n = len(sends)
    lands = [_hbm(lax.empty((N_DEV,) + s.shape if gather else s.shape, s.dtype)) for s in sends]

    def body(*refs):
        send_refs, land_refs = refs[:n], refs[n:2 * n]
        send_sems, recv_sems, local_sems = refs[2 * n + 1:2 * n + 4]
        token = refs[-1]
        local, remote = _split_copies(send_refs, land_refs, send_sems, recv_sems, local_sems, gather)
        for cp in local + remote:
            cp.start()
        token[...] = jnp.zeros_like(token)

    hbm, sem = pl.BlockSpec(memory_space=pltpu.HBM), pl.BlockSpec(memory_space=pltpu.SEMAPHORE)
    out = _pcall(
        body, name=name,
        out_shape=[pltpu.SemaphoreType.DMA((n * COPIES,)), pltpu.SemaphoreType.DMA((n * COPIES,)), pltpu.SemaphoreType.DMA((n,))]
        + [pltpu.HBM(s.shape, s.dtype) for s in sends] + [pltpu.HBM(z.shape, z.dtype) for z in lands]
        + [jax.ShapeDtypeStruct((SUBLANES, LANES), F32)],
        in_specs=[hbm] * (2 * n) + [pl.BlockSpec(memory_space=pl.ANY)],
        out_specs=[sem] * 3 + [hbm] * (2 * n) + [pl.BlockSpec(memory_space=pltpu.VMEM)],
        input_output_aliases={i: 3 + i for i in range(2 * n)},
        compiler_params=pltpu.CompilerParams(has_side_effects=SPLIT_EFFECT),
    )(*[_hbm(s) for s in sends], *lands, after)
    return dict(sems=out[:3], sends=out[3:3 + n], lands=out[3 + n:3 + 2 * n], gather=gather), out[-1]


def _exchange_wait(handle, after, name):
    sends, lands, gather = handle["sends"], handle["lands"], handle["gather"]
    n = len(sends)

    def body(*refs):
        send_refs, land_refs = refs[:n], refs[n:2 * n]
        send_sems, recv_sems, local_sems = refs[2 * n:2 * n + 3]
        local, remote = _split_copies(send_refs, land_refs, send_sems, recv_sems, local_sems, gather)
        for cp in remote:
            cp.wait_send()
            cp.wait_recv()
        for cp in local:
            cp.wait()

    hbm, sem = pl.BlockSpec(memory_space=pltpu.HBM), pl.BlockSpec(memory_space=pltpu.SEMAPHORE)
    out = _pcall(
        body, name=name,
        out_shape=[pltpu.HBM(s.shape, s.dtype) for s in sends] + [pltpu.HBM(z.shape, z.dtype) for z in lands],
        in_specs=[hbm] * (2 * n) + [sem] * 3 + [pl.BlockSpec(memory_space=pl.ANY)],
        out_specs=[hbm] * (2 * n), input_output_aliases={i: i for i in range(2 * n)},
        compiler_params=pltpu.CompilerParams(has_side_effects=SPLIT_EFFECT),
    )(*sends, *lands, *handle["sems"], after)
    return out[n:]


def _sum_and_adamw(recv, w, m, v, name):
    _, r, wp = recv.shape
    c = w.shape[-1]
    lead = w.ndim == 3
    tr = max([d for d in range(2 * SUBLANES, 257, 2 * SUBLANES) if r % d == 0], default=r)
    bc1 = 1.0 - ADAM_B1 ** ADAM_STEP
    bc2 = 1.0 - ADAM_B2 ** ADAM_STEP

    def body(recv_ref, w_ref, m_ref, v_ref, g_ref, d_ref, nm_ref, nv_ref):
        g = recv_ref[0, :, 0:c].astype(F32)
        for s in range(1, N_DEV):
            g = g + recv_ref[s, :, 0:c].astype(F32)
        m_new = ADAM_B1 * m_ref[...] + (1.0 - ADAM_B1) * g
        v_new = ADAM_B2 * v_ref[...] + (1.0 - ADAM_B2) * (g * g)
        m_hat = m_new / bc1
        v_hat = v_new / bc2
        g_ref[...] = g
        d_ref[...] = -ADAM_LR * (m_hat / (jnp.sqrt(v_hat) + ADAM_EPS) + ADAM_WD * w_ref[...])
        nm_ref[...] = m_new
        nv_ref[...] = v_new

    tile = pl.BlockSpec((None, tr, c), lambda i: (0, i, 0)) if lead else pl.BlockSpec((tr, c), lambda i: (i, 0))
    return _pcall(
        body, grid=(r // tr,),
        in_specs=[pl.BlockSpec((N_DEV, tr, wp), lambda i: (0, i, 0)), tile, tile, tile],
        out_specs=[tile] * 4, out_shape=[jax.ShapeDtypeStruct(w.shape, F32)] * 4,
        compiler_params=_params("parallel"), name=name)(recv, w, m, v)


SHARDED_TAPS = ("dn_conv_w", "ffn_conv_w")
REPLICATED = ("attn_norm_g", "dn_a_log", "dn_dt_bias", "dn_out_norm_g", "sg_norm_g", "sg_w", "sg_b", "ffn_norm_g",
              "ffn_conv_b", "final_norm_g")
SMALL = SHARDED_TAPS + REPLICATED
WEIGHT_ORDER = ("attn_norm_g", "w_in", "dn_conv_w", "dn_a_log", "dn_dt_bias", "dn_out_norm_g", "sg_norm_g", "sg_w", "sg_b",
                "w_out", "ffn_norm_g", "w_up", "ffn_conv_w", "ffn_conv_b", "w_down", "final_norm_g")
SLAB_COLS = 1024


def _pad_to(flat, multiple):
    pad = (-flat.shape[-1]) % multiple
    if pad == 0:
        return flat
    return jnp.pad(flat, [(0, 0)] * (flat.ndim - 1) + [(0, pad)])


def _pack_small(named):
    flat = jnp.concatenate([named[n].reshape(-1) for n in SMALL])
    return _pad_to(flat, SUBLANES * SLAB_COLS).reshape(-1, SLAB_COLS)


def _unpack_small(slab, like):
    flat = slab.reshape(-1)
    out, off = {}, 0
    for n in SMALL:
        size = like[n].size
        out[n] = flat[off:off + size].reshape(like[n].shape)
        off += size
    return out


def _split_columns(full, n_local):
    r = full.shape[0]
    return full.reshape(r, N_DEV, n_local).transpose(1, 0, 2).reshape(N_DEV, r * n_local)


def _join_columns(blocks, r, n_local):
    return blocks.reshape(N_DEV, r, n_local).transpose(1, 0, 2).reshape(r, N_DEV * n_local)


def _lanes4(a):
    return jnp.pad(a.reshape(1, N_HEADS), ((0, 0), (0, LANES - N_HEADS)))


def kernel(x, attn_norm_g, w_in, dn_conv_w, dn_a_log, dn_dt_bias, dn_out_norm_g, sg_norm_g, sg_w, sg_b, w_out, ffn_norm_g, w_up, ffn_conv_w, ffn_conv_b, w_down, final_norm_g, loss_target, m_attn_norm_g, m_w_in, m_dn_conv_w, m_dn_a_log, m_dn_dt_bias, m_dn_out_norm_g, m_sg_norm_g, m_sg_w, m_sg_b, m_w_out, m_ffn_norm_g, m_w_up, m_ffn_conv_w, m_ffn_conv_b, m_w_down, m_final_norm_g, v_attn_norm_g, v_w_in, v_dn_conv_w, v_dn_a_log, v_dn_dt_bias, v_dn_out_norm_g, v_sg_norm_g, v_sg_w, v_sg_b, v_w_out, v_ffn_norm_g, v_w_up, v_ffn_conv_w, v_ffn_conv_b, v_w_down, v_final_norm_g):
    weights = dict(attn_norm_g=attn_norm_g, w_in=w_in, dn_conv_w=dn_conv_w, dn_a_log=dn_a_log, dn_dt_bias=dn_dt_bias,
                   dn_out_norm_g=dn_out_norm_g, sg_norm_g=sg_norm_g, sg_w=sg_w, sg_b=sg_b, w_out=w_out, ffn_norm_g=ffn_norm_g,
                   w_up=w_up, ffn_conv_w=ffn_conv_w, ffn_conv_b=ffn_conv_b, w_down=w_down, final_norm_g=final_norm_g)
    m_in = dict(attn_norm_g=m_attn_norm_g, w_in=m_w_in, dn_conv_w=m_dn_conv_w, dn_a_log=m_dn_a_log, dn_dt_bias=m_dn_dt_bias,
                dn_out_norm_g=m_dn_out_norm_g, sg_norm_g=m_sg_norm_g, sg_w=m_sg_w, sg_b=m_sg_b, w_out=m_w_out,
                ffn_norm_g=m_ffn_norm_g, w_up=m_w_up, ffn_conv_w=m_ffn_conv_w, ffn_conv_b=m_ffn_conv_b, w_down=m_w_down,
                final_norm_g=m_final_norm_g)
    v_in = dict(attn_norm_g=v_attn_norm_g, w_in=v_w_in, dn_conv_w=v_dn_conv_w, dn_a_log=v_dn_a_log, dn_dt_bias=v_dn_dt_bias,
                dn_out_norm_g=v_dn_out_norm_g, sg_norm_g=v_sg_norm_g, sg_w=v_sg_w, sg_b=v_sg_b, w_out=v_w_out,
                ffn_norm_g=v_ffn_norm_g, w_up=v_w_up, ffn_conv_w=v_ffn_conv_w, ffn_conv_b=v_ffn_conv_b, w_down=v_w_down,
                final_norm_g=v_final_norm_g)

    n_in, n_up = w_in.shape[2], w_up.shape[2]
    r_out, r_down = w_out.shape[1], w_down.shape[1]
    n_dnc, n_ffc = dn_conv_w.shape[2], ffn_conv_w.shape[2]
    transposed = lambda a: jnp.transpose(a, (0, 2, 1))
    taps = _pad_to(jnp.concatenate([dn_conv_w.reshape(-1), ffn_conv_w.reshape(-1)]), SUBLANES * LANES).reshape(-1, LANES)
    g_in, g_taps = _all_gather([transposed(w_in)[0].astype(BF16), taps])
    gather_out, token = _exchange_start([w_out[0].astype(BF16)], g_taps, True, "gather_w_out")
    gather_up, token = _exchange_start([transposed(w_up)[0].astype(BF16)], token, True, "gather_w_up")
    gather_down, token = _exchange_start([w_down[0].astype(BF16)], token, True, "gather_w_down")
    w_in_t = jnp.pad(g_in.reshape(N_DEV * n_in, D_MODEL), ((0, PROJ_PAD - N_DEV * n_in), (0, 0)))
    taps_all = g_taps.reshape(N_DEV, -1)
    dn_conv_full = _join_columns(taps_all[:, :CONV_K * n_dnc], CONV_K, n_dnc)
    ffn_conv_full = _join_columns(taps_all[:, CONV_K * n_dnc:CONV_K * n_dnc + FFN_CONV * n_ffc], FFN_CONV, n_ffc)
    late = dict(
        w_out=lambda after: _exchange_wait(gather_out, after, "gather_w_out_wait")[0].reshape(N_DEV * r_out, D_MODEL),
        w_up_t=lambda after: _exchange_wait(gather_up, after, "gather_w_up_wait")[0].reshape(N_DEV * n_up, D_MODEL),
        w_down=lambda after: _exchange_wait(gather_down, after, "gather_w_down_wait")[0].reshape(N_DEV * r_down, D_MODEL))

    def send_early(blocks, after, name):
        return _exchange_start(blocks, after, False, name)

    def send_small(g, loss_lanes, after):
        small = jnp.concatenate([g[n].reshape(-1) for n in REPLICATED] + [loss_lanes[0, 0:1]])
        slab = jnp.concatenate([_split_columns(g["dn_conv_w"], n_dnc), _split_columns(g["ffn_conv_w"], n_ffc),
                                jnp.broadcast_to(small[None, :], (N_DEV, small.shape[0]))], axis=1)
        return send_early([_pad_to(slab, SUBLANES * SLAB_COLS).reshape(N_DEV, -1, SLAB_COLS)], after, "send_small")

    upd = {}

    def update_early(sent_down, sent_up_out, sent_small, after):
        r_dn, = _exchange_wait(sent_down, after, "send_dw_down_wait")
        r_up, r_o = _exchange_wait(sent_up_out, after, "send_dw_up_out_wait")
        r_small, = _exchange_wait(sent_small, after, "send_small_wait")
        upd["w_down"] = _sum_and_adamw(r_dn, w_down, m_w_down, v_w_down, "adamw_w_down")
        upd["w_up"] = [transposed(o) for o in _sum_and_adamw(r_up, transposed(w_up), transposed(m_w_up), transposed(v_w_up),
                                                             "adamw_w_up")]
        upd["w_out"] = _sum_and_adamw(r_o, w_out, m_w_out, v_w_out, "adamw_w_out")
        upd["small"] = _sum_and_adamw(r_small, _pack_small(weights), _pack_small(m_in), _pack_small(v_in), "adamw_small")

    grad_x, d_g1, sent_in = _local_step(
        x[0], loss_target[0], w_in_t, late, send_early, send_small, update_early, dn_conv_full, ffn_conv_full,
        attn_norm_g + token[0:1, 0:1], dn_a_log, dn_dt_bias, dn_out_norm_g, sg_norm_g, sg_w, sg_b, ffn_norm_g, ffn_conv_b,
        final_norm_g, n_in)

    norm_rows = D_MODEL // LANES
    r_g1, = _all_to_all([jnp.broadcast_to(d_g1.reshape(1, norm_rows, LANES), (N_DEV, norm_rows, LANES))])
    r_in, = _exchange_wait(sent_in, r_g1, "send_dw_in_wait")
    upd["w_in"] = [transposed(o) for o in _sum_and_adamw(r_in, transposed(w_in), transposed(m_w_in), transposed(v_w_in),
                                                         "adamw_w_in")]
    small_upd = upd.pop("small")
    as_rows = lambda a: a.reshape(norm_rows, LANES)
    norm_upd = _sum_and_adamw(r_g1, as_rows(attn_norm_g), as_rows(m_attn_norm_g), as_rows(v_attn_norm_g), "adamw_attn_norm")
    results = []
    for i in range(4):
        named = _unpack_small(small_upd[i], weights)
        named.update({n: upd[n][i] for n in upd})
        named["attn_norm_g"] = norm_upd[i].reshape(attn_norm_g.shape)
        results.append(named)

    loss = small_upd[0].reshape(-1)[sum(weights[n].size for n in SMALL)]
    return (loss, grad_x[None], *[r[n] for r in results for n in WEIGHT_ORDER])


def _local_step(x2d, tgt, w_in_t, late, send_early, send_small, update_early, dn_conv_full, ffn_conv_full, attn_norm_g,
                dn_a_log, dn_dt_bias, dn_out_norm_g, sg_norm_g, sg_w, sg_b, ffn_norm_g, ffn_conv_b, final_norm_g, n_in):
    t = x2d.shape[0]
    g1, g2, gf = attn_norm_g, ffn_norm_g, final_norm_g.reshape(1, D_MODEL)
    a_log4, dt_bias4 = _lanes4(dn_a_log), _lanes4(dn_dt_bias)
    sg_w3 = sg_w[0]
    sg_b_t = sg_b[0].T
    conv_b = ffn_conv_b

    p, h1, rstd1 = _rmsnorm_matmul(x2d, g1, w_in_t, "norm_in_proj", 512)
    q, k, v, beta4, g4 = _dn_prep(p, dn_conv_full, a_log4, dt_bias4)
    mix_half, *dn_saved = _dn_forward(q, k, v, beta4, g4, p, dn_out_norm_g)
    mix = _sg_forward(p, sg_norm_g, sg_w3, sg_b_t, mix_half)
    w_out_full = late["w_out"](mix)
    x2 = _matmul(mix, w_out_full, "nn", "out_proj", (1024, 1024, 1024), add=x2d)
    w_up_t = late["w_up_t"](x2)
    up, act, h2, rstd2 = _norm_up_ffn(x2, g2, w_up_t, ffn_conv_full, conv_b)
    w_down_full = late["w_down"](act)
    fn, outs = _final_loss_rows(t, D_MODEL)
    loss_lanes, dx3, dx3b, d_gf = _matmul_rows(act, w_down_full, "nn", "down_proj_loss", 512,
                                               [(x2, "rows"), (tgt, "rows"), (gf, "whole")], outs, fn)

    dact = _matmul(dx3b, w_down_full, "nt", "down_proj_dx", (512, D_FF, D_MODEL))
    d_w_down = _matmul(act, dx3b, "tn", "down_proj_dw", (256, 1024, t), out_dtype=BF16)
    sent_down, token = send_early([d_w_down.reshape(N_DEV, D_FF // N_DEV, D_MODEL)], d_w_down, "send_dw_down")
    dup, d_ffn_conv, d_ffn_conv_b = _ffn_bwd(up, ffn_conv_full, conv_b + token[0:1, 0:1], dact)
    fn, outs = _rmsnorm_bwd_rows(t, D_MODEL)
    dx2, dx2b, d_g2 = _matmul_rows(dup, w_up_t, "nn", "up_proj_dx_norm", 256,
                                   [(x2, "rows"), (rstd2, "rows"), (g2, "whole"), (dx3, "rows")], outs, fn)
    d_w_up_t = _matmul(dup, h2, "tn", "up_proj_dw", (512, 1024, t), out_dtype=BF16)
    dmix = _matmul(dx2b, w_out_full, "nt", "out_proj_dx", (1024, 1024, 1024))
    d_w_out = _matmul(mix, dx2b, "tn", "out_proj_dw", (512, 1024, t), out_dtype=BF16)
    sent_up_out, token = send_early(
        [d_w_up_t.reshape(N_DEV, 2 * D_FF // N_DEV, D_MODEL), d_w_out.reshape(N_DEV, D_MODEL // N_DEV, D_MODEL)],
        d_w_out, "send_dw_up_out")
    dp, d_sg_norm, d_sg_w, d_sg_b_t = _sg_backward(p, sg_norm_g + token[0:1, 0:1], sg_w3, sg_b_t, dmix)
    dq, dk, dv, dbeta4, dg4, dp, d_dn_norm = _dn_backward(q, k, v, beta4, g4, p, dn_out_norm_g, dn_saved, dmix, dp)
    dc_dn, d_dn_conv, dp, d_a_log4, d_dt_bias4 = _dn_prep_bwd(p, dn_conv_full, a_log4, dt_bias4, dq, dk, dv, dbeta4, dg4, dp)
    small_grads = dict(
        attn_norm_g=jnp.zeros_like(attn_norm_g), dn_conv_w=d_dn_conv, dn_a_log=d_a_log4[:, :N_HEADS],
        dn_dt_bias=d_dt_bias4[:, :N_HEADS], dn_out_norm_g=d_dn_norm, sg_norm_g=d_sg_norm, sg_w=d_sg_w,
        sg_b=d_sg_b_t[:, :SG_GROUPS].T, ffn_norm_g=d_g2, ffn_conv_w=d_ffn_conv, ffn_conv_b=d_ffn_conv_b, final_norm_g=d_gf)
    sent_small, token = send_small(small_grads, loss_lanes, d_dn_conv)
    dp = _conv_bwd_input(dc_dn, dn_conv_full + token[0:1, 0:1], "dn_conv_dx", out_cols=PROJ_PAD, into=dp)
    d_w_in_t = _matmul(dp, h1, "tn", "in_proj_dw", (PROJ_PAD // 5, 1024, t), out_dtype=BF16)
    sent_in, token = send_early([d_w_in_t[:N_DEV * n_in].reshape(N_DEV, n_in, D_MODEL)], d_w_in_t, "send_dw_in")
    update_early(sent_down, sent_up_out, sent_small, token)
    fn, outs = _rmsnorm_bwd_rows(t, D_MODEL)
    grad_x, _, d_g1 = _matmul_rows(dp, w_in_t, "nn", "in_proj_dx_norm", 512,
                                   [(x2d, "rows"), (rstd1, "rows"), (g1 + token[0:1, 0:1], "whole"), (dx2, "rows")], outs, fn)

    return grad_x, d_g1, sent_in
```

```python
import math

import jax
import jax.numpy as jnp
from jax import lax
from jax.experimental import pallas as pl
from jax.experimental.pallas import tpu as pltpu

F32 = jnp.float32
BF16 = jnp.bfloat16
HI = lax.Precision.HIGHEST

D_MODEL = 1024
DN_WIDTH = 512
HEAD_DIM = 128
N_HEADS = 4
SG_WIDTH = 512
SG_GROUPS = 4
SG_DIM = 128
SG_BLOCK = 128
D_FF = 2816
CHUNK = 64
CONV_K = 4
FFN_CONV = 3
EPS = 1e-6
PROJ_MAIN = 3072
PROJ_PAD = 3200
GELU_C = math.sqrt(2.0 / math.pi)
N_DEV = 8
LANES = 128
SUBLANES = 8
HALO = SUBLANES
VMEM_LIMIT = 48 * 1024 * 1024

ADAM_LR = 0.001
ADAM_B1 = 0.9
ADAM_B2 = 0.999
ADAM_EPS = 1e-08
ADAM_WD = 0.01
ADAM_STEP = 10

MESH_ID = pl.DeviceIdType.MESH


def _pcall(body, **kw):
    return pl.pallas_call(body, **kw)


def _params(*sem):
    return pltpu.CompilerParams(dimension_semantics=sem, vmem_limit_bytes=VMEM_LIMIT)


def _pick(n, cap):
    best = None
    for t in range(LANES, cap + 1, LANES):
        if n % t == 0:
            best = t
    return best if best else n


FAST, EXACT = "bf16 operands, one pass", "f32 operands, six bf16 passes"


def dot_f32(a, b, dims, tier):
    if tier == FAST:
        return lax.dot_general(a.astype(BF16), b.astype(BF16), dims, preferred_element_type=F32)
    return lax.dot_general(a, b, dims, precision=HI, preferred_element_type=F32)


def dot_nn(a, b, tier=EXACT):
    return dot_f32(a, b, (((1,), (0,)), ((), ())), tier)


def dot_nt(a, b, tier=EXACT):
    return dot_f32(a, b, (((1,), (1,)), ((), ())), tier)


def dot_tn(a, b, tier=EXACT):
    return dot_f32(a, b, (((0,), (0,)), ((), ())), tier)


def sigmoid(x):
    return 0.5 * jnp.tanh(0.5 * x) + 0.5


def silu(x):
    return x * sigmoid(x)


def silu_grad(x):
    s = sigmoid(x)
    return s * (1.0 + x * (1.0 - s))


def gelu(x):
    return 0.5 * x * (1.0 + jnp.tanh(GELU_C * (x + 0.044715 * x * x * x)))


def gelu_grad(x):
    t = jnp.tanh(GELU_C * (x + 0.044715 * x * x * x))
    return 0.5 * (1.0 + t) + 0.5 * x * (1.0 - t * t) * GELU_C * (1.0 + 3.0 * 0.044715 * x * x)


def softplus(z):
    return jnp.maximum(z, 0.0) + jnp.log(1.0 + jnp.exp(-jnp.abs(z)))


def rms_fwd(x, g):
    r = lax.rsqrt(jnp.mean(x * x, axis=-1, keepdims=True) + EPS)
    return x * r * g, r


def rms_bwd(x, r, g, dy):
    dyg = dy * g
    xr = x * r
    dx = r * (dyg - xr * jnp.mean(dyg * xr, axis=-1, keepdims=True))
    return dx, dy * xr


def l2_fwd(x):
    r = lax.rsqrt(jnp.sum(x * x, axis=-1, keepdims=True) + EPS)
    return x * r, r


def l2_bwd(x, r, dy):
    xr = x * r
    return r * (dy - xr * jnp.sum(dy * xr, axis=-1, keepdims=True))


def _tri_masks(n):
    row = lax.broadcasted_iota(jnp.int32, (n, n), 0)
    col = lax.broadcasted_iota(jnp.int32, (n, n), 1)
    return row >= col, row > col


def chunk_cumsum(g4):
    incl, _ = _tri_masks(g4.shape[0])
    return dot_nn(incl.astype(F32), g4)


STACK = N_HEADS * CHUNK
DN_FWD_CHUNKS = 8
DN_CHUNKS = 4


def _head_rows(h):
    return slice(h * CHUNK, (h + 1) * CHUNK)


def _stack_heads(x):
    return jnp.concatenate([x[:, h * HEAD_DIM:(h + 1) * HEAD_DIM] for h in range(N_HEADS)], axis=0)


def _stack_lanes(x4):
    return jnp.concatenate([x4[:, h:h + 1] for h in range(N_HEADS)], axis=0)


def _per_head(fn):
    return jnp.concatenate([fn(h) for h in range(N_HEADS)], axis=0)


def _unit_lower_inverses(l_strict, order):
    c = l_strict[0].shape[0]
    row = lax.broadcasted_iota(jnp.int32, (c, c), 0)
    col = lax.broadcasted_iota(jnp.int32, (c, c), 1)
    eye = (row == col).astype(F32)
    p = [-l for l in l_strict]
    a = [eye + n for n in p]
    for _ in range(int(math.log2(order)) - 1):
        p = [dot_nn(x, x, FAST) for x in p]
        a = [x + dot_nn(x, y, FAST) for x, y in zip(a, p)]
    return a


def dn_chunks_local(chunks, inverses=None):
    row = lax.broadcasted_iota(jnp.int32, (STACK, STACK), 0)
    col = lax.broadcasted_iota(jnp.int32, (STACK, STACK), 1)
    same = (row // CHUNK) == (col // CHUNK)
    incl = jnp.logical_and(same, row >= col)
    strict = jnp.logical_and(same, row > col)
    locs = []
    for q, k, v, beta, gc4 in chunks:
        gc_col = _stack_lanes(gc4)
        gc_row = jnp.sum(jnp.where(row == col, gc_col, 0.0), axis=0, keepdims=True)
        decay = jnp.where(incl, jnp.exp(jnp.minimum(gc_col - gc_row, 0.0)), 0.0)
        gamma = jnp.exp(gc_col)
        gc_last = jnp.concatenate([jnp.broadcast_to(gc4[CHUNK - 1:CHUNK, h:h + 1], (CHUNK, 1)) for h in range(N_HEADS)], axis=0)
        tau = jnp.exp(gc_last - gc_col)
        kb = k * beta
        locs.append(dict(decay=decay, gamma=gamma, tau=tau, cd=jnp.exp(gc_last), kb=kb, qd=q * gamma, kt=k * tau,
                         incl=incl, strict=strict))
    for loc, (q, k, v, beta, gc4) in zip(locs, chunks):
        loc["l_mat"] = jnp.where(strict, dot_nt(loc["kb"], k, FAST) * loc["decay"], 0.0)
    if inverses is None:
        inverses = _unit_lower_inverses([loc["l_mat"] for loc in locs], CHUNK)
    for loc, a_inv in zip(locs, inverses):
        loc["a_inv"] = a_inv
    for loc, (q, k, v, beta, gc4) in zip(locs, chunks):
        sol = dot_nn(loc["a_inv"], jnp.concatenate([v * beta, loc["kb"] * loc["gamma"]], axis=1), FAST)
        loc.update(sol=sol, value=sol[:, :HEAD_DIM], kcd=sol[:, HEAD_DIM:])
        loc["attn"] = jnp.where(incl, dot_nt(q, k, FAST) * loc["decay"], 0.0)
    return locs


def dn_chunk_state(loc, s):
    kcd, qd, kt, cd = loc["kcd"], loc["qd"], loc["kt"], loc["cd"]
    v_new = loc["value"] - _per_head(lambda h: dot_nn(kcd[_head_rows(h)], s[h], FAST))
    o = _per_head(lambda h: dot_nn(qd[_head_rows(h)], s[h], FAST)) + dot_nn(loc["attn"], v_new, FAST)
    s_new = [s[h] * cd[h * CHUNK:h * CHUNK + 1, :] + dot_tn(kt[_head_rows(h)], v_new[_head_rows(h)], FAST)
             for h in range(N_HEADS)]
    loc["v_new"] = v_new
    return o, s_new


def dn_chunks_bwd(items, ds_last):
    hr = _head_rows
    n = len(items)
    pre = []
    for q, k, v, beta, loc, s, do in items:
        pre.append(dict(
            dv_part=dot_tn(loc["attn"], do, FAST),
            dattn=jnp.where(loc["incl"], dot_nt(do, loc["v_new"], FAST), 0.0),
            dqd=_per_head(lambda h: dot_nt(do[hr(h)], s[h], FAST)),
            ds_part=[dot_tn(loc["qd"][hr(h)], do[hr(h)], FAST) for h in range(N_HEADS)]))
    ds_new_of, dv_new_of = [None] * n, [None] * n
    ds = ds_last
    for c in reversed(range(n)):
        loc = items[c][4]
        ds_new_of[c] = ds
        dv_new = pre[c]["dv_part"] + _per_head(lambda h: dot_nn(loc["kt"][hr(h)], ds[h], FAST))
        dv_new_of[c] = dv_new
        ds = [pre[c]["ds_part"][h] + ds[h] * loc["cd"][h * CHUNK:h * CHUNK + 1, :]
              - dot_tn(loc["kcd"][hr(h)], dv_new[hr(h)], FAST) for h in range(N_HEADS)]
    is_last = (lax.broadcasted_iota(jnp.int32, (STACK, 1), 0) % CHUNK) == CHUNK - 1
    out = []
    for c, (q, k, v, beta, loc, s, do) in enumerate(items):
        decay, gamma, tau, cd, kb = loc["decay"], loc["gamma"], loc["tau"], loc["cd"], loc["kb"]
        dv_new, ds_new, dattn, dqd = dv_new_of[c], ds_new_of[c], pre[c]["dattn"], pre[c]["dqd"]
        dkt = _per_head(lambda h: dot_nt(loc["v_new"][hr(h)], ds_new[h], FAST))
        dkcd = -_per_head(lambda h: dot_nt(dv_new[hr(h)], s[h], FAST))
        drhs = dot_tn(loc["a_inv"], jnp.concatenate([dv_new, dkcd], axis=1), FAST)
        dvb, dkbg = drhs[:, :HEAD_DIM], drhs[:, HEAD_DIM:]
        dl = jnp.where(loc["strict"], -dot_nt(drhs, loc["sol"], FAST), 0.0)
        dkk = dl * decay
        dqk = dattn * decay
        e = dl * loc["l_mat"] + dattn * loc["attn"]
        dgc = jnp.sum(e, axis=1, keepdims=True) - jnp.sum(e, axis=0, keepdims=True).T
        dkb = dot_nn(dkk, k, FAST) + dkbg * gamma
        dk = dot_tn(dkk, kb, FAST) + dot_tn(dqk, q, FAST) + dkt * tau
        dq = dot_nn(dqk, k, FAST) + dqd * gamma
        dgamma = jnp.sum(dkbg * kb, axis=1, keepdims=True) + jnp.sum(dqd * q, axis=1, keepdims=True)
        dtau_tau = jnp.sum(dkt * k, axis=1, keepdims=True) * tau
        dgc = dgc + dgamma * gamma - dtau_tau

        def last_term(h):
            dcd = jnp.sum(jnp.sum(ds_new[h] * s[h], axis=1, keepdims=True), axis=0, keepdims=True)
            total = jnp.sum(dtau_tau[hr(h)], axis=0, keepdims=True) + dcd * cd[h * CHUNK:h * CHUNK + 1, :]
            return jnp.broadcast_to(total, (CHUNK, 1))

        dgc = dgc + jnp.where(is_last, _per_head(last_term), 0.0)
        dk = dk + dkb * beta
        dbeta = jnp.sum(dkb * k, axis=1, keepdims=True) + jnp.sum(dvb * v, axis=1, keepdims=True)
        out.append((dq, dk, dvb * beta, dbeta, dgc))
    return out, ds


def _token_tile(t):
    return _pick(t, 256)


STRIP = 32


def _for_strips(n_rows, rows, fn, start=0):
    def step(r, carry):
        fn(pl.multiple_of(r * rows, rows))
        return carry

    lax.fori_loop(start, n_rows // rows, step, 0)


def _fold_rows(x):
    out = x[0:SUBLANES, :]
    for i in range(1, x.shape[0] // SUBLANES):
        out = out + x[i * SUBLANES:(i + 1) * SUBLANES, :]
    return out


def _matmul(a, b, mode, name, tiles, add=None, out_dtype=F32):
    if mode == "nn":
        (m, k), n = a.shape, b.shape[1]
    elif mode == "nt":
        (m, k), n = a.shape, b.shape[0]
    else:
        (k, m), n = a.shape, b.shape[1]
    tm, tn, tk = min(tiles[0], m), min(tiles[1], n), min(tiles[2], k)
    assert m % tm == 0 and n % tn == 0 and k % tk == 0, (name, m, n, k, tiles)
    nk = k // tk
    dims = {"nn": (((1,), (0,)), ((), ())), "nt": (((1,), (1,)), ((), ())), "tn": (((0,), (0,)), ((), ()))}[mode]

    def finish(res, add_ref, o_ref):
        if add_ref is not None:
            res = res + add_ref[...]
        o_ref[...] = res.astype(o_ref.dtype)

    def body(*refs):
        a_ref, b_ref = refs[0], refs[1]
        add_ref = refs[2] if add is not None else None
        o_ref = refs[3] if add is not None else refs[2]
        part = lax.dot_general(a_ref[...], b_ref[...], dims, preferred_element_type=F32)
        if nk == 1:
            finish(part, add_ref, o_ref)
            return
        acc_ref = refs[-1]
        kk = pl.program_id(2)

        @pl.when(kk == 0)
        def _():
            acc_ref[...] = part

        @pl.when(kk > 0)
        def _():
            acc_ref[...] += part

        @pl.when(kk == nk - 1)
        def _():
            finish(acc_ref[...], add_ref, o_ref)

    a_spec = pl.BlockSpec((tk, tm), lambda j, i, kk: (kk, i)) if mode == "tn" else pl.BlockSpec((tm, tk), lambda j, i, kk: (i, kk))
    b_spec = pl.BlockSpec((tn, tk), lambda j, i, kk: (j, kk)) if mode == "nt" else pl.BlockSpec((tk, tn), lambda j, i, kk: (kk, j))
    o_spec = pl.BlockSpec((tm, tn), lambda j, i, kk: (i, j))
    in_specs = [a_spec, b_spec] + ([o_spec] if add is not None else [])
    args = (a, b) + ((add,) if add is not None else ())
    return _pcall(
        body, grid=(n // tn, m // tm, nk), in_specs=in_specs, out_specs=o_spec,
        out_shape=jax.ShapeDtypeStruct((m, n), out_dtype),
        scratch_shapes=[pltpu.VMEM((tm, tn), F32)] if nk > 1 else [],
        compiler_params=_params("parallel", "parallel", "arbitrary"), name=name)(*args)


def _matmul_rows(a, b, mode, name, tm, extra, outs, fn):
    m, k = a.shape
    n = b.shape[1] if mode == "nn" else b.shape[0]
    tm = min(tm, m)
    dims = (((1,), (0,)), ((), ())) if mode == "nn" else (((1,), (1,)), ((), ()))

    def spec(shape, kind):
        if kind == "rows":
            return pl.BlockSpec((tm, shape[1]), lambda i: (i, 0))
        return pl.BlockSpec(shape, lambda i: (0,) * len(shape))

    def body(a_ref, b_ref, *refs):
        rows = lax.dot_general(a_ref[...], b_ref[...], dims, preferred_element_type=F32)
        fn(rows, pl.program_id(0) == 0, *refs)

    return _pcall(
        body, grid=(m // tm,),
        in_specs=[pl.BlockSpec((tm, k), lambda i: (i, 0)), pl.BlockSpec(b.shape, lambda i: (0, 0))]
        + [spec(x.shape, kind) for x, kind in extra],
        out_specs=[spec(shape, kind) for shape, _, kind in outs],
        out_shape=[jax.ShapeDtypeStruct(shape, dtype) for shape, dtype, _ in outs],
        compiler_params=_params("arbitrary"), name=name)(a, b, *[x for x, _ in extra])


def _rmsnorm_matmul(x, g, b_t, name, tm):
    t, d = x.shape
    n = b_t.shape[0]
    tm = min(tm, t)

    def body(x_ref, g_ref, b_ref, o_ref, h_ref, r_ref):
        y, r = rms_fwd(x_ref[...], g_ref[...])
        h = y.astype(BF16)
        h_ref[...] = h
        r_ref[...] = r
        o_ref[...] = lax.dot_general(h, b_ref[...], (((1,), (1,)), ((), ())), preferred_element_type=F32)

    rows = lambda w: pl.BlockSpec((tm, w), lambda i: (i, 0))
    return _pcall(
        body, grid=(t // tm,),
        in_specs=[rows(d), pl.BlockSpec((1, d), lambda i: (0, 0)), pl.BlockSpec((n, d), lambda i: (0, 0))],
        out_specs=[rows(n), rows(d), rows(1)],
        out_shape=[jax.ShapeDtypeStruct((t, n), F32), jax.ShapeDtypeStruct((t, d), BF16), jax.ShapeDtypeStruct((t, 1), F32)],
        compiler_params=_params("parallel"), name=name)(x, g, b_t)


def _rmsnorm_bwd_rows(t, d):
    def fn(dh, first, x_ref, r_ref, g_ref, dres_ref, dx_ref, dxb_ref, dg_ref):
        dx, dg_rows = rms_bwd(x_ref[...], r_ref[...], g_ref[...], dh)
        dx = dx + dres_ref[...]
        dx_ref[...] = dx
        dxb_ref[...] = dx.astype(BF16)

        @pl.when(first)
        def _():
            dg_ref[...] = jnp.zeros_like(dg_ref)

        dg_ref[...] += jnp.sum(dg_rows, axis=0, keepdims=True)

    return fn, [((t, d), F32, "rows"), ((t, d), BF16, "rows"), ((1, d), F32, "whole")]


def _final_loss_rows(t, d):
    def fn(rows, first, res_ref, t_ref, g_ref, loss_ref, dx_ref, dxb_ref, dg_ref):
        @pl.when(first)
        def _():
            loss_ref[...] = jnp.zeros_like(loss_ref)
            dg_ref[...] = jnp.zeros_like(dg_ref)

        x = rows + res_ref[...]
        y, r = rms_fwd(x, g_ref[...])
        err = y - t_ref[...]
        loss_ref[...] += 0.5 * jnp.sum(jnp.mean(err * err, axis=-1, keepdims=True), axis=0, keepdims=True)
        dx, dg_rows = rms_bwd(x, r, g_ref[...], err * (1.0 / d))
        dx_ref[...] = dx
        dxb_ref[...] = dx.astype(BF16)
        dg_ref[...] += jnp.sum(dg_rows, axis=0, keepdims=True)

    return fn, [((1, LANES), F32, "whole"), ((t, d), F32, "rows"), ((t, d), BF16, "rows"), ((1, d), F32, "whole")]


def _prev_halo_spec(tm, width, col_block):
    return pl.BlockSpec((HALO, width), lambda i: (jnp.maximum(i * (tm // HALO) - 1, 0), col_block))


def _history(tile_ref, halo_ref, first, row0, cols):
    if isinstance(row0, int) and row0 == 0:
        return jnp.concatenate([jnp.where(first, 0.0, halo_ref[:, cols]), tile_ref[0:STRIP, cols]], axis=0)
    return tile_ref[pl.ds(pl.multiple_of(row0 - HALO, HALO), STRIP + HALO), cols]


def _first_then_strips(n_rows, fn):
    fn(0)
    _for_strips(n_rows, STRIP, fn, start=1)


def _delays(ext, taps):
    return [ext[HALO:, :]] + [pltpu.roll(ext, j, 0)[HALO:, :] for j in range(1, taps)]


def _causal_conv(delayed, w):
    taps = len(delayed)
    out = delayed[0] * w[taps - 1:taps, :]
    for j in range(1, taps):
        out = out + delayed[j] * w[taps - 1 - j:taps - j, :]
    return out


def _advanced_conv(buf_ref, row0, cols, w):
    return _advanced(buf_ref[pl.ds(row0, STRIP + HALO), cols], w)


def _advanced(ext, w):
    taps = w.shape[0]
    out = ext[:STRIP, :] * w[taps - 1:taps, :]
    for j in range(1, taps):
        out = out + pltpu.roll(ext, STRIP + HALO - j, 0)[:STRIP, :] * w[taps - 1 - j:taps - j, :]
    return out


def _dn_prep(p, conv_w, a_log4, dt_bias4):
    t = p.shape[0]
    tm = _token_tile(t)
    w3 = 3 * DN_WIDTH

    def body(x_ref, halo_ref, pbd_ref, w_ref, alog_ref, dtb_ref, q_ref, k_ref, v_ref, beta_ref, g_ref):
        first = pl.program_id(0) == 0

        def strip(row0):
            rows = pl.ds(row0, STRIP)
            for h in range(N_HEADS):
                sl = slice(h * HEAD_DIM, (h + 1) * HEAD_DIM)
                for part, out_ref in ((0, q_ref), (1, k_ref), (2, v_ref)):
                    cols = slice(part * DN_WIDTH + h * HEAD_DIM, part * DN_WIDTH + (h + 1) * HEAD_DIM)
                    y = silu(_causal_conv(_delays(_history(x_ref, halo_ref, first, row0, cols), CONV_K), w_ref[:, cols]))
                    if part == 0:
                        y = l2_fwd(y)[0] * (HEAD_DIM ** -0.5)
                    elif part == 1:
                        y = l2_fwd(y)[0]
                    out_ref[rows, sl] = y
            head = lax.broadcasted_iota(jnp.int32, (STRIP, LANES), 1) < N_HEADS
            pbd = pbd_ref[rows, :]
            beta_ref[rows, :] = jnp.where(head, sigmoid(pbd), 0.0)
            a_raw = pltpu.roll(pbd, LANES - N_HEADS, 1)
            g_ref[rows, :] = jnp.where(head, -jnp.exp(alog_ref[...]) * softplus(a_raw + dtb_ref[...]), 0.0)

        _first_then_strips(tm, strip)

    tok = lambda w, cb: pl.BlockSpec((tm, w), lambda i: (i, cb))
    full = lambda a: pl.BlockSpec(a.shape, lambda i: (0, 0))
    return _pcall(
        body, grid=(t // tm,),
        in_specs=[tok(w3, 0), _prev_halo_spec(tm, w3, 0), tok(LANES, PROJ_MAIN // LANES),
                  full(conv_w), full(a_log4), full(dt_bias4)],
        out_specs=[tok(DN_WIDTH, 0)] * 3 + [tok(LANES, 0)] * 2,
        out_shape=[jax.ShapeDtypeStruct((t, DN_WIDTH), F32)] * 3 + [jax.ShapeDtypeStruct((t, LANES), F32)] * 2,
        compiler_params=_params("parallel"), name="dn_prep")(p, p, p, conv_w, a_log4, dt_bias4)


def _dn_prep_bwd(p, conv_w, a_log4, dt_bias4, dq, dk, dv, dbeta4, dg4, dp_buf):
    t = p.shape[0]
    tm = _token_tile(t)
    w3 = 3 * DN_WIDTH

    def body(x_ref, halo_ref, pbd_ref, w_ref, alog_ref, dtb_ref, dq_ref, dk_ref, dv_ref, dbeta_ref, dg_ref, _,
             dc_ref, dw_ref, dpbd_ref, dalog_ref, ddtb_ref, dw_acc, lane_acc):
        first = pl.program_id(0) == 0
        dw_acc[...] = jnp.zeros_like(dw_acc)
        lane_acc[...] = jnp.zeros_like(lane_acc)

        def strip(row0):
            rows = pl.ds(row0, STRIP)
            for h in range(N_HEADS):
                sl = slice(h * HEAD_DIM, (h + 1) * HEAD_DIM)
                for part, dy_ref in ((0, dq_ref), (1, dk_ref), (2, dv_ref)):
                    cols = slice(part * DN_WIDTH + h * HEAD_DIM, part * DN_WIDTH + (h + 1) * HEAD_DIM)
                    delayed = _delays(_history(x_ref, halo_ref, first, row0, cols), CONV_K)
                    c = _causal_conv(delayed, w_ref[:, cols])
                    dy = dy_ref[rows, sl]
                    if part < 2:
                        y = silu(c)
                        _, r = l2_fwd(y)
                        dy = l2_bwd(y, r, dy * (HEAD_DIM ** -0.5) if part == 0 else dy)
                    dc = dy * silu_grad(c)
                    dc_ref[rows, cols] = dc
                    for j in range(CONV_K):
                        k = CONV_K - 1 - j
                        dw_acc[k * SUBLANES:(k + 1) * SUBLANES, cols] += _fold_rows(dc * delayed[j])
            head = lax.broadcasted_iota(jnp.int32, (STRIP, LANES), 1) < N_HEADS
            pbd = pbd_ref[rows, :]
            beta = sigmoid(pbd)
            dpb = jnp.where(head, dbeta_ref[rows, :] * beta * (1.0 - beta), 0.0)
            z = pltpu.roll(pbd, LANES - N_HEADS, 1) + dtb_ref[...]
            neg_rate = -jnp.exp(alog_ref[...])
            dg = dg_ref[rows, :]
            dpa = jnp.where(head, dg * neg_rate * sigmoid(z), 0.0)
            dpbd_ref[rows, :] = (dpb + pltpu.roll(dpa, N_HEADS, 1)).astype(BF16)
            g = jnp.where(head, neg_rate * softplus(z), 0.0)
            lane_acc[0:SUBLANES, :] += _fold_rows(dg * g)
            lane_acc[SUBLANES:, :] += _fold_rows(dpa)

        _first_then_strips(tm, strip)

        @pl.when(first)
        def _():
            dw_ref[...] = jnp.zeros_like(dw_ref)
            dalog_ref[...] = jnp.zeros_like(dalog_ref)
            ddtb_ref[...] = jnp.zeros_like(ddtb_ref)

        for k in range(CONV_K):
            dw_ref[k:k + 1, :] += jnp.sum(dw_acc[k * SUBLANES:(k + 1) * SUBLANES, :], axis=0, keepdims=True)
        dalog_ref[...] += jnp.sum(lane_acc[0:SUBLANES, :], axis=0, keepdims=True)
        ddtb_ref[...] += jnp.sum(lane_acc[SUBLANES:, :], axis=0, keepdims=True)

    tok = lambda w, cb: pl.BlockSpec((tm, w), lambda i: (i, cb))
    full = lambda shape: pl.BlockSpec(shape, lambda i: (0, 0))
    return _pcall(
        body, grid=(t // tm,),
        in_specs=[tok(w3, 0), _prev_halo_spec(tm, w3, 0), tok(LANES, PROJ_MAIN // LANES),
                  full(conv_w.shape), full(a_log4.shape), full(dt_bias4.shape)] + [tok(DN_WIDTH, 0)] * 3 + [tok(LANES, 0)] * 2
        + [pl.BlockSpec(memory_space=pl.ANY)],
        out_specs=[tok(w3, 0), full((CONV_K, w3)), tok(LANES, PROJ_MAIN // LANES), full((1, LANES)), full((1, LANES))],
        out_shape=[jax.ShapeDtypeStruct((t, w3), F32), jax.ShapeDtypeStruct((CONV_K, w3), F32),
                   jax.ShapeDtypeStruct(dp_buf.shape, dp_buf.dtype),
                   jax.ShapeDtypeStruct((1, LANES), F32), jax.ShapeDtypeStruct((1, LANES), F32)],
        input_output_aliases={11: 2},
        scratch_shapes=[pltpu.VMEM((CONV_K * SUBLANES, w3), F32), pltpu.VMEM((2 * SUBLANES, LANES), F32)],
        compiler_params=_params("arbitrary"), name="dn_prep_bwd")(p, p, p, conv_w, a_log4, dt_bias4, dq, dk, dv, dbeta4, dg4, dp_buf)


def _conv_bwd_input(dc, w, name, out_cols=None, col_block=0, into=None):
    t, c = dc.shape
    taps = w.shape[0]
    tm = _token_tile(t)
    ct = _pick(c, 1536)
    n_tok = t // tm
    out_cols = c if out_cols is None else out_cols

    def body(dc_ref, next_ref, w_ref, *rest):
        dx_ref = rest[-1]
        last = pl.program_id(0) == n_tok - 1

        def strip(row0):
            for c0 in range(0, ct, LANES):
                cols = slice(c0, c0 + LANES)
                dx_ref[pl.ds(row0, STRIP), cols] = _advanced_conv(dc_ref, row0, cols, w_ref[:, cols]).astype(BF16)

        _for_strips(tm - STRIP, STRIP, strip)
        for c0 in range(0, ct, LANES):
            cols = slice(c0, c0 + LANES)
            ext = jnp.concatenate([dc_ref[tm - STRIP:tm, cols], jnp.where(last, 0.0, next_ref[:, cols])], axis=0)
            dx_ref[tm - STRIP:tm, cols] = _advanced(ext, w_ref[:, cols]).astype(BF16)

    in_specs = [pl.BlockSpec((tm, ct), lambda i, j: (i, j)),
                pl.BlockSpec((HALO, ct), lambda i, j: (jnp.minimum((i + 1) * (tm // HALO), t // HALO - 1), j)),
                pl.BlockSpec((taps, ct), lambda i, j: (0, j))]
    args = (dc, dc, w)
    aliases = {}
    if into is not None:
        in_specs.append(pl.BlockSpec(memory_space=pl.ANY))
        args += (into,)
        aliases = {3: 0}
    return _pcall(
        body, grid=(n_tok, c // ct), in_specs=in_specs,
        out_specs=pl.BlockSpec((tm, ct), lambda i, j: (i, j + col_block)),
        out_shape=jax.ShapeDtypeStruct((t, out_cols), BF16), input_output_aliases=aliases,
        compiler_params=_params("parallel", "parallel"), name=name)(*args)


def _dn_forward(q, k, v, beta4, g4, p, norm_g):
    t = q.shape[0]
    n = t // CHUNK
    nc = DN_FWD_CHUNKS
    rows_per_step = nc * CHUNK

    def body(q_ref, k_ref, v_ref, b_ref, g_ref, gate_ref, ng_ref, mix_ref, s_all_ref, ainv_ref, s_ref):
        @pl.when(pl.program_id(0) == 0)
        def _():
            s_ref[...] = jnp.zeros_like(s_ref)

        chunks = []
        for c in range(nc):
            rows = slice(c * CHUNK, (c + 1) * CHUNK)
            chunks.append((_stack_heads(q_ref[rows, :]), _stack_heads(k_ref[rows, :]), _stack_heads(v_ref[rows, :]),
                           _stack_lanes(b_ref[rows, :]), chunk_cumsum(g_ref[rows, :])))
        locs = dn_chunks_local(chunks)
        s = [s_ref[h] for h in range(N_HEADS)]
        for c in range(nc):
            rows = slice(c * CHUNK, (c + 1) * CHUNK)
            ainv_ref[c] = locs[c]["a_inv"].astype(BF16)
            for h in range(N_HEADS):
                s_all_ref[c, h] = s[h]
            o, s = dn_chunk_state(locs[c], s)
            o_n, _ = rms_fwd(o, ng_ref[...])
            for h in range(N_HEADS):
                sl = slice(h * HEAD_DIM, (h + 1) * HEAD_DIM)
                mix_ref[rows, sl] = (o_n[_head_rows(h)] * silu(gate_ref[rows, sl])).astype(BF16)
        for h in range(N_HEADS):
            s_ref[h] = s[h]

    ch = lambda w, cb: pl.BlockSpec((rows_per_step, w), lambda i: (i, cb))
    per_chunk = lambda *shape: pl.BlockSpec((nc,) + shape, lambda i: (i,) + (0,) * len(shape))
    return _pcall(
        body, grid=(n // nc,),
        in_specs=[ch(DN_WIDTH, 0)] * 3 + [ch(LANES, 0)] * 2 + [ch(DN_WIDTH, 3), pl.BlockSpec((1, HEAD_DIM), lambda i: (0, 0))],
        out_specs=[ch(DN_WIDTH, 0), per_chunk(N_HEADS, HEAD_DIM, HEAD_DIM), per_chunk(STACK, STACK)],
        out_shape=[jax.ShapeDtypeStruct((t, DN_WIDTH + SG_WIDTH), BF16), jax.ShapeDtypeStruct((n, N_HEADS, HEAD_DIM, HEAD_DIM), F32),
                   jax.ShapeDtypeStruct((n, STACK, STACK), BF16)],
        scratch_shapes=[pltpu.VMEM((N_HEADS, HEAD_DIM, HEAD_DIM), F32)],
        compiler_params=_params("arbitrary"), name="dn_forward")(q, k, v, beta4, g4, p, norm_g)


def _dn_backward(q, k, v, beta4, g4, p, norm_g, saved, dmix, dp_buf):
    t = q.shape[0]
    n = t // CHUNK
    steps = n // DN_CHUNKS
    rows_per_step = DN_CHUNKS * CHUNK

    def body(q_ref, k_ref, v_ref, b_ref, g_ref, gate_ref, ng_ref, s_in_ref, ainv_ref, dmix_ref, _,
             dq_ref, dk_ref, dv_ref, db_ref, dg_ref, dgate_ref, dng_ref, ds_ref):
        @pl.when(pl.program_id(0) == 0)
        def _():
            ds_ref[...] = jnp.zeros_like(ds_ref)
            dng_ref[...] = jnp.zeros_like(dng_ref)

        chunks = []
        for c in range(DN_CHUNKS):
            rows = slice(c * CHUNK, (c + 1) * CHUNK)
            chunks.append((_stack_heads(q_ref[rows, :]), _stack_heads(k_ref[rows, :]), _stack_heads(v_ref[rows, :]),
                           _stack_lanes(b_ref[rows, :]), chunk_cumsum(g_ref[rows, :])))
        items = []
        for c, loc in enumerate(dn_chunks_local(chunks, [ainv_ref[c] for c in range(DN_CHUNKS)])):
            rows = slice(c * CHUNK, (c + 1) * CHUNK)
            s = [s_in_ref[c, h] for h in range(N_HEADS)]
            o, _ = dn_chunk_state(loc, s)
            o_n, r = rms_fwd(o, ng_ref[...])
            gate = _stack_heads(gate_ref[rows, :])
            dmx = _stack_heads(dmix_ref[rows, :])
            dgate = dmx * o_n * silu_grad(gate)
            do, dng_rows = rms_bwd(o, r, ng_ref[...], dmx * silu(gate))
            dng_ref[...] += jnp.sum(dng_rows, axis=0, keepdims=True)
            for h in range(N_HEADS):
                dgate_ref[rows, h * HEAD_DIM:(h + 1) * HEAD_DIM] = dgate[_head_rows(h)].astype(BF16)
            items.append((*chunks[c][:4], loc, s, do))
        grads, ds = dn_chunks_bwd(items, [ds_ref[h] for h in range(N_HEADS)])
        lane = lax.broadcasted_iota(jnp.int32, (CHUNK, LANES), 1)
        _, strict = _tri_masks(CHUNK)
        for c in range(DN_CHUNKS):
            rows = slice(c * CHUNK, (c + 1) * CHUNK)
            dq, dk, dv, dbeta, dgc = grads[c]
            db4 = jnp.zeros((CHUNK, LANES), F32)
            dgc4 = jnp.zeros((CHUNK, LANES), F32)
            for h in range(N_HEADS):
                sl = slice(h * HEAD_DIM, (h + 1) * HEAD_DIM)
                head_rows = _head_rows(h)
                dq_ref[rows, sl] = dq[head_rows]
                dk_ref[rows, sl] = dk[head_rows]
                dv_ref[rows, sl] = dv[head_rows]
                db4 = jnp.where(lane == h, dbeta[head_rows], db4)
                dgc4 = jnp.where(lane == h, dgc[head_rows], dgc4)
            db_ref[rows, :] = db4
            dg_ref[rows, :] = dot_nn(jnp.logical_not(strict).astype(F32), dgc4)
        for h in range(N_HEADS):
            ds_ref[h] = ds[h]

    rev = lambda w, cb: pl.BlockSpec((rows_per_step, w), lambda i: (steps - 1 - i, cb))
    per_chunk = lambda a: pl.BlockSpec((DN_CHUNKS,) + a.shape[1:], lambda i: (steps - 1 - i,) + (0,) * (a.ndim - 1))
    return _pcall(
        body, grid=(steps,),
        in_specs=[rev(DN_WIDTH, 0)] * 3 + [rev(LANES, 0)] * 2 + [rev(DN_WIDTH, 3), pl.BlockSpec((1, HEAD_DIM), lambda i: (0, 0))]
        + [per_chunk(a) for a in saved] + [rev(DN_WIDTH, 0), pl.BlockSpec(memory_space=pl.ANY)],
        out_specs=[rev(DN_WIDTH, 0)] * 3 + [rev(LANES, 0)] * 2 + [rev(DN_WIDTH, 3), pl.BlockSpec((1, HEAD_DIM), lambda i: (0, 0))],
        out_shape=[jax.ShapeDtypeStruct((t, DN_WIDTH), F32)] * 3 + [jax.ShapeDtypeStruct((t, LANES), F32)] * 2
        + [jax.ShapeDtypeStruct(dp_buf.shape, dp_buf.dtype), jax.ShapeDtypeStruct((1, HEAD_DIM), F32)],
        input_output_aliases={10: 5},
        scratch_shapes=[pltpu.VMEM((N_HEADS, HEAD_DIM, HEAD_DIM), F32)],
        compiler_params=_params("arbitrary"), name="dn_backward")(q, k, v, beta4, g4, p, norm_g, *saved, dmix, dp_buf)


SG_STEP_BLOCKS = 2


def _sg_mask():
    row = lax.broadcasted_iota(jnp.int32, (SG_BLOCK, SG_BLOCK), 0)
    col = lax.broadcasted_iota(jnp.int32, (SG_BLOCK, SG_BLOCK), 1)
    return (col // CHUNK) <= (row // CHUNK)


def _sg_forward(p, norm_g, w_s, b_t, mix_buf):
    t = p.shape[0]
    step_rows = SG_STEP_BLOCKS * SG_BLOCK

    def body(u_ref, v_ref, ng_ref, w_ref, b_ref, _, o_ref):
        mask = _sg_mask()
        pairs = [(slice(b * SG_BLOCK, (b + 1) * SG_BLOCK), g, slice(g * SG_DIM, (g + 1) * SG_DIM))
                 for b in range(SG_STEP_BLOCKS) for g in range(SG_GROUPS)]
        w_m = [jnp.where(mask, w_ref[g], 0.0) for g in range(SG_GROUPS)]
        vn = [rms_fwd(gelu(v_ref[rows, sl]), ng_ref[:, sl])[0] for rows, g, sl in pairs]
        s = [dot_nn(w_m[g], vn[i], FAST) + b_ref[:, g:g + 1] for i, (rows, g, sl) in enumerate(pairs)]
        for i, (rows, g, sl) in enumerate(pairs):
            o_ref[rows, sl] = (gelu(u_ref[rows, sl]) * s[i]).astype(BF16)

    blk = lambda cb: pl.BlockSpec((step_rows, SG_WIDTH), lambda i: (i, cb))
    return _pcall(
        body, grid=(t // step_rows,),
        in_specs=[blk(4), blk(5), pl.BlockSpec((1, SG_WIDTH), lambda i: (0, 0)),
                  pl.BlockSpec((SG_GROUPS, SG_BLOCK, SG_BLOCK), lambda i: (0, 0, 0)), pl.BlockSpec((SG_BLOCK, SG_GROUPS), lambda i: (0, 0)),
                  pl.BlockSpec(memory_space=pl.ANY)],
        out_specs=blk(1), out_shape=jax.ShapeDtypeStruct(mix_buf.shape, mix_buf.dtype), input_output_aliases={5: 0},
        compiler_params=_params("parallel"), name="sg_forward")(p, p, norm_g, w_s, b_t, mix_buf)


def _sg_backward(p, norm_g, w_s, b_t, dmix):
    t = p.shape[0]

    def body(u_ref, v_ref, ng_ref, w_ref, b_ref, do_ref, duv_ref, dng_ref, dw_ref, db_ref):
        @pl.when(pl.program_id(0) == 0)
        def _():
            dng_ref[...] = jnp.zeros_like(dng_ref)
            dw_ref[...] = jnp.zeros_like(dw_ref)
            db_ref[...] = jnp.zeros_like(db_ref)

        mask = _sg_mask()
        lane = lax.broadcasted_iota(jnp.int32, (SG_BLOCK, LANES), 1)
        pairs = [(slice(b * SG_BLOCK, (b + 1) * SG_BLOCK), g, slice(g * SG_DIM, (g + 1) * SG_DIM))
                 for b in range(SG_STEP_BLOCKS) for g in range(SG_GROUPS)]
        w_m = [jnp.where(mask, w_ref[g], 0.0) for g in range(SG_GROUPS)]
        vg = [gelu(v_ref[rows, sl]) for rows, g, sl in pairs]
        normed = [rms_fwd(vg[i], ng_ref[:, sl]) for i, (rows, g, sl) in enumerate(pairs)]
        s = [dot_nn(w_m[g], normed[i][0], FAST) + b_ref[:, g:g + 1] for i, (rows, g, sl) in enumerate(pairs)]
        ds = []
        db = jnp.zeros((SG_BLOCK, LANES), F32)
        for i, (rows, g, sl) in enumerate(pairs):
            u_raw, do = u_ref[rows, sl], do_ref[rows, sl]
            duv_ref[rows, sl] = (do * s[i] * gelu_grad(u_raw)).astype(BF16)
            ds.append(do * gelu(u_raw))
            db = db + jnp.where(lane == g, jnp.sum(ds[i], axis=1, keepdims=True), 0.0)
        dw = [jnp.where(mask, dot_nt(ds[i], normed[i][0], FAST), 0.0) for i in range(len(pairs))]
        dvn = [dot_tn(w_m[g], ds[i], FAST) for i, (rows, g, sl) in enumerate(pairs)]
        for i, (rows, g, sl) in enumerate(pairs):
            dw_ref[g] += dw[i]
            dvg, dng_rows = rms_bwd(vg[i], normed[i][1], ng_ref[:, sl], dvn[i])
            dng_ref[:, sl] += jnp.sum(dng_rows, axis=0, keepdims=True)
            duv_ref[rows, SG_WIDTH + g * SG_DIM:SG_WIDTH + (g + 1) * SG_DIM] = (dvg * gelu_grad(v_ref[rows, sl])).astype(BF16)
        db_ref[...] += db

    step_rows = SG_STEP_BLOCKS * SG_BLOCK
    blk = lambda cb: pl.BlockSpec((step_rows, SG_WIDTH), lambda i: (i, cb))
    const2 = lambda shape: pl.BlockSpec(shape, lambda i: (0, 0))
    w_spec = pl.BlockSpec((SG_GROUPS, SG_BLOCK, SG_BLOCK), lambda i: (0, 0, 0))
    return _pcall(
        body, grid=(t // step_rows,),
        in_specs=[blk(4), blk(5), const2((1, SG_WIDTH)), w_spec, const2((SG_BLOCK, SG_GROUPS)), blk(1)],
        out_specs=[pl.BlockSpec((step_rows, 2 * SG_WIDTH), lambda i: (i, 2)), const2((1, SG_WIDTH)), w_spec,
                   const2((SG_BLOCK, LANES))],
        out_shape=[jax.ShapeDtypeStruct((t, PROJ_PAD), BF16), jax.ShapeDtypeStruct((1, SG_WIDTH), F32),
                   jax.ShapeDtypeStruct((SG_GROUPS, SG_BLOCK, SG_BLOCK), F32), jax.ShapeDtypeStruct((SG_BLOCK, LANES), F32)],
        compiler_params=_params("arbitrary"), name="sg_backward")(p, p, norm_g, w_s, b_t, dmix)


FFN_COLS = 256


def _norm_up_ffn(x, g, w_up_t, conv_w, conv_b):
    t, d = x.shape
    tm = min(t, 256)
    blocks = D_FF // FFN_COLS
    nt = (((1,), (1,)), ((), ()))

    def body(x_ref, g_ref, w_ref, cw_ref, cb_ref, up_ref, act_ref, h_ref, r_ref, tail_ref, prev_ref):
        @pl.when(pl.program_id(0) == 0)
        def _():
            tail_ref[...] = jnp.zeros_like(tail_ref)

        y, r = rms_fwd(x_ref[...], g_ref[...])
        h = y.astype(BF16)
        h_ref[...] = h
        r_ref[...] = r

        def project(blk):
            out = []
            for half in range(2):
                cols = slice(half * D_FF + blk * FFN_COLS, half * D_FF + (blk + 1) * FFN_COLS)
                u = lax.dot_general(h, w_ref[cols, :], nt, preferred_element_type=F32)
                up_ref[:, cols] = u
                prev_ref[:, cols] = tail_ref[:, cols]
                tail_ref[:, cols] = u[tm - HALO:, :]
                out.append(cols)
            return out

        def history(row0, cols):
            if row0 == 0:
                return jnp.concatenate([prev_ref[:, cols], up_ref[0:STRIP, cols]], axis=0)
            return up_ref[row0 - HALO:row0 + STRIP, cols]

        def activate(blk, g_cols, v_cols):
            for row0 in range(0, tm, STRIP):
                for c0 in range(0, FFN_COLS, LANES):
                    gc = slice(g_cols.start + c0, g_cols.start + c0 + LANES)
                    vc = slice(v_cols.start + c0, v_cols.start + c0 + LANES)
                    cg = _causal_conv(_delays(history(row0, gc), FFN_CONV), cw_ref[:, gc]) + cb_ref[:, gc]
                    cv = _causal_conv(_delays(history(row0, vc), FFN_CONV), cw_ref[:, vc]) + cb_ref[:, vc]
                    act_ref[row0:row0 + STRIP, blk * FFN_COLS + c0:blk * FFN_COLS + c0 + LANES] = (silu(cg) * cv).astype(BF16)

        pending = None
        for blk in range(blocks):
            cols = project(blk)
            if pending is not None:
                activate(*pending)
            pending = (blk, *cols)
        activate(*pending)

    rows = lambda w: pl.BlockSpec((tm, w), lambda i: (i, 0))
    whole = lambda a: pl.BlockSpec(a.shape, lambda i: (0, 0))
    return _pcall(
        body, grid=(t // tm,),
        in_specs=[rows(d), whole(g), whole(w_up_t), whole(conv_w), whole(conv_b)],
        out_specs=[rows(2 * D_FF), rows(D_FF), rows(d), rows(1)],
        out_shape=[jax.ShapeDtypeStruct((t, 2 * D_FF), F32), jax.ShapeDtypeStruct((t, D_FF), BF16),
                   jax.ShapeDtypeStruct((t, d), BF16), jax.ShapeDtypeStruct((t, 1), F32)],
        scratch_shapes=[pltpu.VMEM((HALO, 2 * D_FF), F32), pltpu.VMEM((HALO, 2 * D_FF), F32)],
        compiler_params=_params("arbitrary"), name="norm_up_ffn")(x, g, w_up_t, conv_w, conv_b)


def _ffn_bwd(up, conv_w, conv_b, dact):
    t = up.shape[0]
    tm = _pick(t, 256)
    n_tok = t // tm
    width = 2 * D_FF

    def dconv(delayed_g, delayed_v, da, wg, wv, bg, bv):
        cg = _causal_conv(delayed_g, wg) + bg
        cv = _causal_conv(delayed_v, wv) + bv
        s = sigmoid(cg)
        return da * cv * (s * (1.0 + cg * (1.0 - s))), da * (cg * s)

    def body(up_ref, prev_ref, next_ref, da_ref, dan_ref, w_ref, b_ref, dup_ref, dw_ref, db_ref, dc_ref, dw_acc, db_acc):
        first = pl.program_id(0) == 0
        last = pl.program_id(0) == n_tok - 1
        dw_acc[...] = jnp.zeros_like(dw_acc)
        db_acc[...] = jnp.zeros_like(db_acc)

        def strip(row0):
            rows = pl.ds(row0, STRIP)
            for c0 in range(0, D_FF, LANES):
                gc, vc = slice(c0, c0 + LANES), slice(D_FF + c0, D_FF + c0 + LANES)
                del_g = _delays(_history(up_ref, prev_ref, first, row0, gc), FFN_CONV)
                del_v = _delays(_history(up_ref, prev_ref, first, row0, vc), FFN_CONV)
                dcg, dcv = dconv(del_g, del_v, da_ref[rows, gc], w_ref[:, gc], w_ref[:, vc], b_ref[:, gc], b_ref[:, vc])
                dc_ref[rows, gc] = dcg
                dc_ref[rows, vc] = dcv
                db_acc[:, gc] += _fold_rows(dcg)
                db_acc[:, vc] += _fold_rows(dcv)
                for j in range(FFN_CONV):
                    k = FFN_CONV - 1 - j
                    dw_acc[k * SUBLANES:(k + 1) * SUBLANES, gc] += _fold_rows(dcg * del_g[j])
                    dw_acc[k * SUBLANES:(k + 1) * SUBLANES, vc] += _fold_rows(dcv * del_v[j])

        _first_then_strips(tm, strip)

        for c0 in range(0, D_FF, LANES):
            gc, vc = slice(c0, c0 + LANES), slice(D_FF + c0, D_FF + c0 + LANES)

            def delayed(cols):
                return _delays(jnp.concatenate([up_ref[tm - HALO:tm, cols], next_ref[:, cols]], axis=0), FFN_CONV)

            dcg, dcv = dconv(delayed(gc), delayed(vc), dan_ref[:, gc], w_ref[:, gc], w_ref[:, vc], b_ref[:, gc], b_ref[:, vc])
            dc_ref[tm:, gc] = jnp.where(last, 0.0, dcg)
            dc_ref[tm:, vc] = jnp.where(last, 0.0, dcv)

        def strip_dx(row0):
            for c0 in range(0, width, LANES):
                cols = slice(c0, c0 + LANES)
                dup_ref[pl.ds(row0, STRIP), cols] = _advanced_conv(dc_ref, row0, cols, w_ref[:, cols]).astype(BF16)

        _for_strips(tm, STRIP, strip_dx)

        @pl.when(first)
        def _():
            dw_ref[...] = jnp.zeros_like(dw_ref)
            db_ref[...] = jnp.zeros_like(db_ref)

        for k in range(FFN_CONV):
            dw_ref[k:k + 1, :] += jnp.sum(dw_acc[k * SUBLANES:(k + 1) * SUBLANES, :], axis=0, keepdims=True)
        db_ref[...] += jnp.sum(db_acc[...], axis=0, keepdims=True)

    next_rows = lambda i: jnp.minimum((i + 1) * (tm // HALO), t // HALO - 1)
    full = lambda rows: pl.BlockSpec((rows, width), lambda i: (0, 0))
    return _pcall(
        body, grid=(n_tok,),
        in_specs=[pl.BlockSpec((tm, width), lambda i: (i, 0)),
                  pl.BlockSpec((HALO, width), lambda i: (jnp.maximum(i * (tm // HALO) - 1, 0), 0)),
                  pl.BlockSpec((HALO, width), lambda i: (next_rows(i), 0)),
                  pl.BlockSpec((tm, D_FF), lambda i: (i, 0)), pl.BlockSpec((HALO, D_FF), lambda i: (next_rows(i), 0)),
                  full(FFN_CONV), full(1)],
        out_specs=[pl.BlockSpec((tm, width), lambda i: (i, 0)), full(FFN_CONV), full(1)],
        out_shape=[jax.ShapeDtypeStruct((t, width), BF16), jax.ShapeDtypeStruct((FFN_CONV, width), F32),
                   jax.ShapeDtypeStruct((1, width), F32)],
        scratch_shapes=[pltpu.VMEM((tm + HALO, width), F32),
                        pltpu.VMEM((FFN_CONV * SUBLANES, width), F32), pltpu.VMEM((SUBLANES, width), F32)],
        compiler_params=_params("arbitrary"), name="ffn_bwd")(up, up, up, dact, dact, conv_w, conv_b)


def _my_position():
    return lax.axis_index("x"), lax.axis_index("y"), lax.axis_index("c")


COPIES = N_DEV - 1


def _all_gather(arrays):
    n = len(arrays)

    def body(*refs):
        x_refs, out_refs = refs[:n], refs[n:2 * n]
        send_sems, recv_sems, local_sems = refs[2 * n:]
        x, y, cc = _my_position()
        me, sibling = (x, y, cc), (x, y, 1 - cc)
        chips = [(1 - x, y), (x, 1 - y), (1 - x, 1 - y)]

        def block(a, px, py, pc):
            return out_refs[a].at[4 * px + 2 * py + pc]

        def copy(a, k, blk, to, src=None):
            return pltpu.make_async_remote_copy(
                src_ref=block(a, *blk) if src is None else src, dst_ref=block(a, *blk),
                send_sem=send_sems.at[a * COPIES + k], recv_sem=recv_sems.at[a * COPIES + k],
                device_id=to, device_id_type=MESH_ID)

        mine = [pltpu.make_async_copy(x_refs[a], block(a, *me), local_sems.at[a]) for a in range(n)]
        for cp in mine:
            cp.start()
        first = []
        for a in range(n):
            first.append(copy(a, 0, me, sibling, src=x_refs[a]))
            first += [copy(a, 1 + j, me, (*chip, cc), src=x_refs[a]) for j, chip in enumerate(chips)]
        for cp in first:
            cp.start()
        passed = []
        for j, chip in enumerate(chips):
            for a in range(n):
                copy(a, 1 + j, (*chip, cc), me).wait_recv()
                passed.append(copy(a, 4 + j, (*chip, cc), sibling))
                passed[-1].start()
        for a in range(n):
            copy(a, 0, sibling, me).wait_recv()
        for j, chip in enumerate(chips):
            for a in range(n):
                copy(a, 4 + j, (*chip, 1 - cc), me).wait_recv()
        for cp in first + passed:
            cp.wait_send()
        for cp in mine:
            cp.wait()

    any_spec = pl.BlockSpec(memory_space=pl.ANY)
    return _pcall(
        body, out_shape=[jax.ShapeDtypeStruct((N_DEV,) + a.shape, a.dtype) for a in arrays],
        in_specs=[any_spec] * n, out_specs=[any_spec] * n,
        scratch_shapes=[pltpu.SemaphoreType.DMA((n * COPIES,)), pltpu.SemaphoreType.DMA((n * COPIES,)),
                        pltpu.SemaphoreType.DMA((n,))],
        name="all_gather")(*arrays)


def _all_to_all(sends):
    n = len(sends)

    def body(*refs):
        send_refs, recv_refs = refs[:n], refs[n:2 * n]
        send_sems, recv_sems, local_sems = refs[2 * n:]
        x, y, cc = _my_position()
        me = 4 * x + 2 * y + cc
        mine = [pltpu.make_async_copy(send_refs[a].at[me], recv_refs[a].at[me], local_sems.at[a]) for a in range(n)]
        for cp in mine:
            cp.start()
        copies = []
        for rel in range(1, N_DEV):
            px, py, pc = x ^ (rel >> 2), y ^ ((rel >> 1) & 1), cc ^ (rel & 1)
            for a in range(n):
                copies.append(pltpu.make_async_remote_copy(
                    src_ref=send_refs[a].at[4 * px + 2 * py + pc], dst_ref=recv_refs[a].at[me],
                    send_sem=send_sems.at[a * COPIES + rel - 1], recv_sem=recv_sems.at[a * COPIES + rel - 1],
                    device_id=(px, py, pc), device_id_type=MESH_ID))
        for cp in copies:
            cp.start()
        for cp in copies:
            cp.wait()
        for cp in mine:
            cp.wait()

    any_spec = pl.BlockSpec(memory_space=pl.ANY)
    return _pcall(
        body, out_shape=[jax.ShapeDtypeStruct(s.shape, s.dtype) for s in sends],
        in_specs=[any_spec] * n, out_specs=[any_spec] * n,
        scratch_shapes=[pltpu.SemaphoreType.DMA((n * COPIES,)), pltpu.SemaphoreType.DMA((n * COPIES,)),
                        pltpu.SemaphoreType.DMA((n,))],
        name="all_to_all")(*sends)


def _hbm(a):
    return pltpu.with_memory_space_constraint(a, pltpu.HBM)


def _split_copies(send_refs, land_refs, send_sems, recv_sems, local_sems, gather):
    x, y, cc = _my_position()
    me = 4 * x + 2 * y + cc
    local, remote = [], []
    for a, (send, land) in enumerate(zip(send_refs, land_refs)):
        local.append(pltpu.make_async_copy(send if gather else send.at[me], land.at[me], local_sems.at[a]))
    for a, (send, land) in enumerate(zip(send_refs, land_refs)):
        for rel in range(1, N_DEV):
            px, py, pc = x ^ (rel >> 2), y ^ ((rel >> 1) & 1), cc ^ (rel & 1)
            remote.append(pltpu.make_async_remote_copy(
                src_ref=send if gather else send.at[4 * px + 2 * py + pc], dst_ref=land.at[me],
                send_sem=send_sems.at[a * COPIES + rel - 1], recv_sem=recv_sems.at[a * COPIES + rel - 1],
                device_id=(px, py, pc), device_id_type=MESH_ID))
    return local, remote


SPLIT_EFFECT = pltpu.SideEffectType.DATAFLOW_SIDE_EFFECTING


def _exchange_start(sends, after, gather, name):
    n = len(sends)
    lands = [_hbm(lax.empty((N_DEV,) + s.shape if gather else s.shape, s.dtype)) for s in sends]

    def body(*refs):
        send_refs, land_refs = refs[:n], refs[n:2 * n]
        send_sems, recv_sems, local_sems = refs[2 * n + 1:2 * n + 4]
        token = refs[-1]
        local, remote = _split_copies(send_refs, land_refs, send_sems, recv_sems, local_sems, gather)
        for cp in local + remote:
            cp.start()
        token[...] = jnp.zeros_like(token)

    hbm, sem = pl.BlockSpec(memory_space=pltpu.HBM), pl.BlockSpec(memory_space=pltpu.SEMAPHORE)
    out = _pcall(
        body, name=name,
        out_shape=[pltpu.SemaphoreType.DMA((n * COPIES,)), pltpu.SemaphoreType.DMA((n * COPIES,)), pltpu.SemaphoreType.DMA((n,))]
        + [pltpu.HBM(s.shape, s.dtype) for s in sends] + [pltpu.HBM(z.shape, z.dtype) for z in lands]
        + [jax.ShapeDtypeStruct((SUBLANES, LANES), F32)],
        in_specs=[hbm] * (2 * n) + [pl.BlockSpec(memory_space=pl.ANY)],
        out_specs=[sem] * 3 + [hbm] * (2 * n) + [pl.BlockSpec(memory_space=pltpu.VMEM)],
        input_output_aliases={i: 3 + i for i in range(2 * n)},
        compiler_params=pltpu.CompilerParams(has_side_effects=SPLIT_EFFECT),
    )(*[_hbm(s) for s in sends], *lands, after)
    return dict(sems=out[:3], sends=out[3:3 + n], lands=out[3 + n:3 + 2 * n], gather=gather), out[-1]


def _exchange_wait(handle, after, name):
    sends, lands, gather = handle["sends"], handle["lands"], handle["gather"]
    n = len(sends)

    def body(*refs):
        send_refs, land_refs = refs[:n], refs[n:2 * n]
        send_sems, recv_sems, local_sems = refs[2 * n:2 * n + 3]
        local, remote = _split_copies(send_refs, land_refs, send_sems, recv_sems, local_sems, gather)
        for cp in remote:
            cp.wait_send()
            cp.wait_recv()
        for cp in local:
            cp.wait()

    hbm, sem = pl.BlockSpec(memory_space=pltpu.HBM), pl.BlockSpec(memory_space=pltpu.SEMAPHORE)
    out = _pcall(
        body, name=name,
        out_shape=[pltpu.HBM(s.shape, s.dtype) for s in sends] + [pltpu.HBM(z.shape, z.dtype) for z in lands],
        in_specs=[hbm] * (2 * n) + [sem] * 3 + [pl.BlockSpec(memory_space=pl.ANY)],
        out_specs=[hbm] * (2 * n), input_output_aliases={i: i for i in range(2 * n)},
        compiler_params=pltpu.CompilerParams(has_side_effects=SPLIT_EFFECT),
    )(*sends, *lands, *handle["sems"], after)
    return out[n:]


def _sum_and_adamw(recv, w, m, v, name):
    _, r, wp = recv.shape
    c = w.shape[-1]
    lead = w.ndim == 3
    tr = max([d for d in range(2 * SUBLANES, 257, 2 * SUBLANES) if r % d == 0], default=r)
    bc1 = 1.0 - ADAM_B1 ** ADAM_STEP
    bc2 = 1.0 - ADAM_B2 ** ADAM_STEP

    def body(recv_ref, w_ref, m_ref, v_ref, g_ref, d_ref, nm_ref, nv_ref):
        g = recv_ref[0, :, 0:c].astype(F32)
        for s in range(1, N_DEV):
            g = g + recv_ref[s, :, 0:c].astype(F32)
        m_new = ADAM_B1 * m_ref[...] + (1.0 - ADAM_B1) * g
        v_new = ADAM_B2 * v_ref[...] + (1.0 - ADAM_B2) * (g * g)
        m_hat = m_new / bc1
        v_hat = v_new / bc2
        g_ref[...] = g
        d_ref[...] = -ADAM_LR * (m_hat / (jnp.sqrt(v_hat) + ADAM_EPS) + ADAM_WD * w_ref[...])
        nm_ref[...] = m_new
        nv_ref[...] = v_new

    tile = pl.BlockSpec((None, tr, c), lambda i: (0, i, 0)) if lead else pl.BlockSpec((tr, c), lambda i: (i, 0))
    return _pcall(
        body, grid=(r // tr,),
        in_specs=[pl.BlockSpec((N_DEV, tr, wp), lambda i: (0, i, 0)), tile, tile, tile],
        out_specs=[tile] * 4, out_shape=[jax.ShapeDtypeStruct(w.shape, F32)] * 4,
        compiler_params=_params("parallel"), name=name)(recv, w, m, v)


SHARDED_TAPS = ("dn_conv_w", "ffn_conv_w")
REPLICATED = ("attn_norm_g", "dn_a_log", "dn_dt_bias", "dn_out_norm_g", "sg_norm_g", "sg_w", "sg_b", "ffn_norm_g",
              "ffn_conv_b", "final_norm_g")
SMALL = SHARDED_TAPS + REPLICATED
WEIGHT_ORDER = ("attn_norm_g", "w_in", "dn_conv_w", "dn_a_log", "dn_dt_bias", "dn_out_norm_g", "sg_norm_g", "sg_w", "sg_b",
                "w_out", "ffn_norm_g", "w_up", "ffn_conv_w", "ffn_conv_b", "w_down", "final_norm_g")
SLAB_COLS = 1024


def _pad_to(flat, multiple):
    pad = (-flat.shape[-1]) % multiple
    if pad == 0:
        return flat
    return jnp.pad(flat, [(0, 0)] * (flat.ndim - 1) + [(0, pad)])


def _pack_small(named):
    flat = jnp.concatenate([named[n].reshape(-1) for n in SMALL])
    return _pad_to(flat, SUBLANES * SLAB_COLS).reshape(-1, SLAB_COLS)


def _unpack_small(slab, like):
    flat = slab.reshape(-1)
    out, off = {}, 0
    for n in SMALL:
        size = like[n].size
        out[n] = flat[off:off + size].reshape(like[n].shape)
        off += size
    return out


def _split_columns(full, n_local):
    r = full.shape[0]
    return full.reshape(r, N_DEV, n_local).transpose(1, 0, 2).reshape(N_DEV, r * n_local)


def _join_columns(blocks, r, n_local):
    return blocks.reshape(N_DEV, r, n_local).transpose(1, 0, 2).reshape(r, N_DEV * n_local)


def _lanes4(a):
    return jnp.pad(a.reshape(1, N_HEADS), ((0, 0), (0, LANES - N_HEADS)))


def kernel(x, attn_norm_g, w_in, dn_conv_w, dn_a_log, dn_dt_bias, dn_out_norm_g, sg_norm_g, sg_w, sg_b, w_out, ffn_norm_g, w_up, ffn_conv_w, ffn_conv_b, w_down, final_norm_g, loss_target, m_attn_norm_g, m_w_in, m_dn_conv_w, m_dn_a_log, m_dn_dt_bias, m_dn_out_norm_g, m_sg_norm_g, m_sg_w, m_sg_b, m_w_out, m_ffn_norm_g, m_w_up, m_ffn_conv_w, m_ffn_conv_b, m_w_down, m_final_norm_g, v_attn_norm_g, v_w_in, v_dn_conv_w, v_dn_a_log, v_dn_dt_bias, v_dn_out_norm_g, v_sg_norm_g, v_sg_w, v_sg_b, v_w_out, v_ffn_norm_g, v_w_up, v_ffn_conv_w, v_ffn_conv_b, v_w_down, v_final_norm_g):
    weights = dict(attn_norm_g=attn_norm_g, w_in=w_in, dn_conv_w=dn_conv_w, dn_a_log=dn_a_log, dn_dt_bias=dn_dt_bias,
                   dn_out_norm_g=dn_out_norm_g, sg_norm_g=sg_norm_g, sg_w=sg_w, sg_b=sg_b, w_out=w_out, ffn_norm_g=ffn_norm_g,
                   w_up=w_up, ffn_conv_w=ffn_conv_w, ffn_conv_b=ffn_conv_b, w_down=w_down, final_norm_g=final_norm_g)
    m_in = dict(attn_norm_g=m_attn_norm_g, w_in=m_w_in, dn_conv_w=m_dn_conv_w, dn_a_log=m_dn_a_log, dn_dt_bias=m_dn_dt_bias,
                dn_out_norm_g=m_dn_out_norm_g, sg_norm_g=m_sg_norm_g, sg_w=m_sg_w, sg_b=m_sg_b, w_out=m_w_out,
                ffn_norm_g=m_ffn_norm_g, w_up=m_w_up, ffn_conv_w=m_ffn_conv_w, ffn_conv_b=m_ffn_conv_b, w_down=m_w_down,
                final_norm_g=m_final_norm_g)
    v_in = dict(attn_norm_g=v_attn_norm_g, w_in=v_w_in, dn_conv_w=v_dn_conv_w, dn_a_log=v_dn_a_log, dn_dt_bias=v_dn_dt_bias,
                dn_out_norm_g=v_dn_out_norm_g, sg_norm_g=v_sg_norm_g, sg_w=v_sg_w, sg_b=v_sg_b, w_out=v_w_out,
                ffn_norm_g=v_ffn_norm_g, w_up=v_w_up, ffn_conv_w=v_ffn_conv_w, ffn_conv_b=v_ffn_conv_b, w_down=v_w_down,
                final_norm_g=v_final_norm_g)

    n_in, n_up = w_in.shape[2], w_up.shape[2]
    r_out, r_down = w_out.shape[1], w_down.shape[1]
    n_dnc, n_ffc = dn_conv_w.shape[2], ffn_conv_w.shape[2]
    transposed = lambda a: jnp.transpose(a, (0, 2, 1))
    taps = _pad_to(jnp.concatenate([dn_conv_w.reshape(-1), ffn_conv_w.reshape(-1)]), SUBLANES * LANES).reshape(-1, LANES)
    g_in, g_taps = _all_gather([transposed(w_in)[0].astype(BF16), taps])
    gather_out, token = _exchange_start([w_out[0].astype(BF16)], g_taps, True, "gather_w_out")
    gather_up, token = _exchange_start([transposed(w_up)[0].astype(BF16)], token, True, "gather_w_up")
    gather_down, token = _exchange_start([w_down[0].astype(BF16)], token, True, "gather_w_down")
    w_in_t = jnp.pad(g_in.reshape(N_DEV * n_in, D_MODEL), ((0, PROJ_PAD - N_DEV * n_in), (0, 0)))
    taps_all = g_taps.reshape(N_DEV, -1)
    dn_conv_full = _join_columns(taps_all[:, :CONV_K * n_dnc], CONV_K, n_dnc)
    ffn_conv_full = _join_columns(taps_all[:, CONV_K * n_dnc:CONV_K * n_dnc + FFN_CONV * n_ffc], FFN_CONV, n_ffc)
    late = dict(
        w_out=lambda after: _exchange_wait(gather_out, after, "gather_w_out_wait")[0].reshape(N_DEV * r_out, D_MODEL),
        w_up_t=lambda after: _exchange_wait(gather_up, after, "gather_w_up_wait")[0].reshape(N_DEV * n_up, D_MODEL),
        w_down=lambda after: _exchange_wait(gather_down, after, "gather_w_down_wait")[0].reshape(N_DEV * r_down, D_MODEL))

    def send_early(blocks, after, name):
        return _exchange_start(blocks, after, False, name)

    def send_small(g, loss_lanes, after):
        small = jnp.concatenate([g[n].reshape(-1) for n in REPLICATED] + [loss_lanes[0, 0:1]])
        slab = jnp.concatenate([_split_columns(g["dn_conv_w"], n_dnc), _split_columns(g["ffn_conv_w"], n_ffc),
                                jnp.broadcast_to(small[None, :], (N_DEV, small.shape[0]))], axis=1)
        return send_early([_pad_to(slab, SUBLANES * SLAB_COLS).reshape(N_DEV, -1, SLAB_COLS)], after, "send_small")

    upd = {}

    def update_early(sent_down, sent_up_out, sent_small, after):
        r_dn, = _exchange_wait(sent_down, after, "send_dw_down_wait")
        r_up, r_o = _exchange_wait(sent_up_out, after, "send_dw_up_out_wait")
        r_small, = _exchange_wait(sent_small, after, "send_small_wait")
        upd["w_down"] = _sum_and_adamw(r_dn, w_down, m_w_down, v_w_down, "adamw_w_down")
        upd["w_up"] = [transposed(o) for o in _sum_and_adamw(r_up, transposed(w_up), transposed(m_w_up), transposed(v_w_up),
                                                             "adamw_w_up")]
        upd["w_out"] = _sum_and_adamw(r_o, w_out, m_w_out, v_w_out, "adamw_w_out")
        upd["small"] = _sum_and_adamw(r_small, _pack_small(weights), _pack_small(m_in), _pack_small(v_in), "adamw_small")

    grad_x, d_g1, sent_in = _local_step(
        x[0], loss_target[0], w_in_t, late, send_early, send_small, update_early, dn_conv_full, ffn_conv_full,
        attn_norm_g + token[0:1, 0:1], dn_a_log, dn_dt_bias, dn_out_norm_g, sg_norm_g, sg_w, sg_b, ffn_norm_g, ffn_conv_b,
        final_norm_g, n_in)

    norm_rows = D_MODEL // LANES
    r_g1, = _all_to_all([jnp.broadcast_to(d_g1.reshape(1, norm_rows, LANES), (N_DEV, norm_rows, LANES))])
    r_in, = _exchange_wait(sent_in, r_g1, "send_dw_in_wait")
    upd["w_in"] = [transposed(o) for o in _sum_and_adamw(r_in, transposed(w_in), transposed(m_w_in), transposed(v_w_in),
                                                         "adamw_w_in")]
    small_upd = upd.pop("small")
    as_rows = lambda a: a.reshape(norm_rows, LANES)
    norm_upd = _sum_and_adamw(r_g1, as_rows(attn_norm_g), as_rows(m_attn_norm_g), as_rows(v_attn_norm_g), "adamw_attn_norm")
    results = []
    for i in range(4):
        named = _unpack_small(small_upd[i], weights)
        named.update({n: upd[n][i] for n in upd})
        named["attn_norm_g"] = norm_upd[i].reshape(attn_norm_g.shape)
        results.append(named)

    loss = small_upd[0].reshape(-1)[sum(weights[n].size for n in SMALL)]
    return (loss, grad_x[None], *[r[n] for r in results for n in WEIGHT_ORDER])


def _local_step(x2d, tgt, w_in_t, late, send_early, send_small, update_early, dn_conv_full, ffn_conv_full, attn_norm_g,
                dn_a_log, dn_dt_bias, dn_out_norm_g, sg_norm_g, sg_w, sg_b, ffn_norm_g, ffn_conv_b, final_norm_g, n_in):
    t = x2d.shape[0]
    g1, g2, gf = attn_norm_g, ffn_norm_g, final_norm_g.reshape(1, D_MODEL)
    a_log4, dt_bias4 = _lanes4(dn_a_log), _lanes4(dn_dt_bias)
    sg_w3 = sg_w[0]
    sg_b_t = sg_b[0].T
    conv_b = ffn_conv_b

    p, h1, rstd1 = _rmsnorm_matmul(x2d, g1, w_in_t, "norm_in_proj", 512)
    q, k, v, beta4, g4 = _dn_prep(p, dn_conv_full, a_log4, dt_bias4)
    mix_half, *dn_saved = _dn_forward(q, k, v, beta4, g4, p, dn_out_norm_g)
    mix = _sg_forward(p, sg_norm_g, sg_w3, sg_b_t, mix_half)
    w_out_full = late["w_out"](mix)
    x2 = _matmul(mix, w_out_full, "nn", "out_proj", (1024, 1024, 1024), add=x2d)
    w_up_t = late["w_up_t"](x2)
    up, act, h2, rstd2 = _norm_up_ffn(x2, g2, w_up_t, ffn_conv_full, conv_b)
    w_down_full = late["w_down"](act)
    fn, outs = _final_loss_rows(t, D_MODEL)
    loss_lanes, dx3, dx3b, d_gf = _matmul_rows(act, w_down_full, "nn", "down_proj_loss", 512,
                                               [(x2, "rows"), (tgt, "rows"), (gf, "whole")], outs, fn)

    dact = _matmul(dx3b, w_down_full, "nt", "down_proj_dx", (512, D_FF, D_MODEL))
    d_w_down = _matmul(act, dx3b, "tn", "down_proj_dw", (256, 1024, t), out_dtype=BF16)
    sent_down, token = send_early([d_w_down.reshape(N_DEV, D_FF // N_DEV, D_MODEL)], d_w_down, "send_dw_down")
    dup, d_ffn_conv, d_ffn_conv_b = _ffn_bwd(up, ffn_conv_full, conv_b + token[0:1, 0:1], dact)
    fn, outs = _rmsnorm_bwd_rows(t, D_MODEL)
    dx2, dx2b, d_g2 = _matmul_rows(dup, w_up_t, "nn", "up_proj_dx_norm", 256,
                                   [(x2, "rows"), (rstd2, "rows"), (g2, "whole"), (dx3, "rows")], outs, fn)
    d_w_up_t = _matmul(dup, h2, "tn", "up_proj_dw", (512, 1024, t), out_dtype=BF16)
    dmix = _matmul(dx2b, w_out_full, "nt", "out_proj_dx", (1024, 1024, 1024))
    d_w_out = _matmul(mix, dx2b, "tn", "out_proj_dw", (512, 1024, t), out_dtype=BF16)
    sent_up_out, token = send_early(
        [d_w_up_t.reshape(N_DEV, 2 * D_FF // N_DEV, D_MODEL), d_w_out.reshape(N_DEV, D_MODEL // N_DEV, D_MODEL)],
        d_w_out, "send_dw_up_out")
    dp, d_sg_norm, d_sg_w, d_sg_b_t = _sg_backward(p, sg_norm_g + token[0:1, 0:1], sg_w3, sg_b_t, dmix)
    dq, dk, dv, dbeta4, dg4, dp, d_dn_norm = _dn_backward(q, k, v, beta4, g4, p, dn_out_norm_g, dn_saved, dmix, dp)
    dc_dn, d_dn_conv, dp, d_a_log4, d_dt_bias4 = _dn_prep_bwd(p, dn_conv_full, a_log4, dt_bias4, dq, dk, dv, dbeta4, dg4, dp)
    small_grads = dict(
        attn_norm_g=jnp.zeros_like(attn_norm_g), dn_conv_w=d_dn_conv, dn_a_log=d_a_log4[:, :N_HEADS],
        dn_dt_bias=d_dt_bias4[:, :N_HEADS], dn_out_norm_g=d_dn_norm, sg_norm_g=d_sg_norm, sg_w=d_sg_w,
        sg_b=d_sg_b_t[:, :SG_GROUPS].T, ffn_norm_g=d_g2, ffn_conv_w=d_ffn_conv, ffn_conv_b=d_ffn_conv_b, final_norm_g=d_gf)
    sent_small, token = send_small(small_grads, loss_lanes, d_dn_conv)
    dp = _conv_bwd_input(dc_dn, dn_conv_full + token[0:1, 0:1], "dn_conv_dx", out_cols=PROJ_PAD, into=dp)
    d_w_in_t = _matmul(dp, h1, "tn", "in_proj_dw", (PROJ_PAD // 5, 1024, t), out_dtype=BF16)
    sent_in, token = send_early([d_w_in_t[:N_DEV * n_in].reshape(N_DEV, n_in, D_MODEL)], d_w_in_t, "send_dw_in")
    update_early(sent_down, sent_up_out, sent_small, token)
    fn, outs = _rmsnorm_bwd_rows(t, D_MODEL)
    grad_x, _, d_g1 = _matmul_rows(dp, w_in_t, "nn", "in_proj_dx_norm", 512,
                                   [(x2d, "rows"), (rstd1, "rows"), (g1 + token[0:1, 0:1], "whole"), (dx2, "rows")], outs, fn)

    return grad_x, d_g1, sent_in
```

```python
import math

import jax
import jax.numpy as jnp
from jax import lax
from jax.experimental import pallas as pl
from jax.experimental.pallas import tpu as pltpu

F32 = jnp.float32
BF16 = jnp.bfloat16
HI = lax.Precision.HIGHEST

D_MODEL = 1024
DN_WIDTH = 512
HEAD_DIM = 128
N_HEADS = 4
SG_WIDTH = 512
SG_GROUPS = 4
SG_DIM = 128
SG_BLOCK = 128
D_FF = 2816
CHUNK = 64
CONV_K = 4
FFN_CONV = 3
EPS = 1e-6
PROJ_MAIN = 3072
PROJ_PAD = 3200
GELU_C = math.sqrt(2.0 / math.pi)
N_DEV = 8
LANES = 128
SUBLANES = 8
HALO = SUBLANES
VMEM_LIMIT = 48 * 1024 * 1024

ADAM_LR = 0.001
ADAM_B1 = 0.9
ADAM_B2 = 0.999
ADAM_EPS = 1e-08
ADAM_WD = 0.01
ADAM_STEP = 10

MESH_ID = pl.DeviceIdType.MESH


def _pcall(body, **kw):
    return pl.pallas_call(body, **kw)


def _params(*sem):
    return pltpu.CompilerParams(dimension_semantics=sem, vmem_limit_bytes=VMEM_LIMIT)


def _pick(n, cap):
    best = None
    for t in range(LANES, cap + 1, LANES):
        if n % t == 0:
            best = t
    return best if best else n


FAST, EXACT = "bf16 operands, one pass", "f32 operands, six bf16 passes"


def dot_f32(a, b, dims, tier):
    if tier == FAST:
        return lax.dot_general(a.astype(BF16), b.astype(BF16), dims, preferred_element_type=F32)
    return lax.dot_general(a, b, dims, precision=HI, preferred_element_type=F32)


def dot_nn(a, b, tier=EXACT):
    return dot_f32(a, b, (((1,), (0,)), ((), ())), tier)


def dot_nt(a, b, tier=EXACT):
    return dot_f32(a, b, (((1,), (1,)), ((), ())), tier)


def dot_tn(a, b, tier=EXACT):
    return dot_f32(a, b, (((0,), (0,)), ((), ())), tier)


def sigmoid(x):
    return 0.5 * jnp.tanh(0.5 * x) + 0.5


def silu(x):
    return x * sigmoid(x)


def silu_grad(x):
    s = sigmoid(x)
    return s * (1.0 + x * (1.0 - s))


def gelu(x):
    return 0.5 * x * (1.0 + jnp.tanh(GELU_C * (x + 0.044715 * x * x * x)))


def gelu_grad(x):
    t = jnp.tanh(GELU_C * (x + 0.044715 * x * x * x))
    return 0.5 * (1.0 + t) + 0.5 * x * (1.0 - t * t) * GELU_C * (1.0 + 3.0 * 0.044715 * x * x)


def softplus(z):
    return jnp.maximum(z, 0.0) + jnp.log(1.0 + jnp.exp(-jnp.abs(z)))


def rms_fwd(x, g):
    r = lax.rsqrt(jnp.mean(x * x, axis=-1, keepdims=True) + EPS)
    return x * r * g, r


def rms_bwd(x, r, g, dy):
    dyg = dy * g
    xr = x * r
    dx = r * (dyg - xr * jnp.mean(dyg * xr, axis=-1, keepdims=True))
    return dx, dy * xr


def l2_fwd(x):
    r = lax.rsqrt(jnp.sum(x * x, axis=-1, keepdims=True) + EPS)
    return x * r, r


def l2_bwd(x, r, dy):
    xr = x * r
    return r * (dy - xr * jnp.sum(dy * xr, axis=-1, keepdims=True))


def _tri_masks(n):
    row = lax.broadcasted_iota(jnp.int32, (n, n), 0)
    col = lax.broadcasted_iota(jnp.int32, (n, n), 1)
    return row >= col, row > col


def chunk_cumsum(g4):
    incl, _ = _tri_masks(g4.shape[0])
    return dot_nn(incl.astype(F32), g4)


STACK = N_HEADS * CHUNK
DN_FWD_CHUNKS = 8
DN_CHUNKS = 4


def _head_rows(h):
    return slice(h * CHUNK, (h + 1) * CHUNK)


def _stack_heads(x):
    return jnp.concatenate([x[:, h * HEAD_DIM:(h + 1) * HEAD_DIM] for h in range(N_HEADS)], axis=0)


def _stack_lanes(x4):
    return jnp.concatenate([x4[:, h:h + 1] for h in range(N_HEADS)], axis=0)


def _per_head(fn):
    return jnp.concatenate([fn(h) for h in range(N_HEADS)], axis=0)


def _unit_lower_inverses(l_strict, order):
    c = l_strict[0].shape[0]
    row = lax.broadcasted_iota(jnp.int32, (c, c), 0)
    col = lax.broadcasted_iota(jnp.int32, (c, c), 1)
    eye = (row == col).astype(F32)
    p = [-l for l in l_strict]
    a = [eye + n for n in p]
    for _ in range(int(math.log2(order)) - 1):
        p = [dot_nn(x, x, FAST) for x in p]
        a = [x + dot_nn(x, y, FAST) for x, y in zip(a, p)]
    return a


def dn_chunks_local(chunks, inverses=None):
    row = lax.broadcasted_iota(jnp.int32, (STACK, STACK), 0)
    col = lax.broadcasted_iota(jnp.int32, (STACK, STACK), 1)
    same = (row // CHUNK) == (col // CHUNK)
    incl = jnp.logical_and(same, row >= col)
    strict = jnp.logical_and(same, row > col)
    locs = []
    for q, k, v, beta, gc4 in chunks:
        gc_col = _stack_lanes(gc4)
        gc_row = jnp.sum(jnp.where(row == col, gc_col, 0.0), axis=0, keepdims=True)
        decay = jnp.where(incl, jnp.exp(jnp.minimum(gc_col - gc_row, 0.0)), 0.0)
        gamma = jnp.exp(gc_col)
        gc_last = jnp.concatenate([jnp.broadcast_to(gc4[CHUNK - 1:CHUNK, h:h + 1], (CHUNK, 1)) for h in range(N_HEADS)], axis=0)
        tau = jnp.exp(gc_last - gc_col)
        kb = k * beta
        locs.append(dict(decay=decay, gamma=gamma, tau=tau, cd=jnp.exp(gc_last), kb=kb, qd=q * gamma, kt=k * tau,
                         incl=incl, strict=strict))
    for loc, (q, k, v, beta, gc4) in zip(locs, chunks):
        loc["l_mat"] = jnp.where(strict, dot_nt(loc["kb"], k, FAST) * loc["decay"], 0.0)
    if inverses is None:
        inverses = _unit_lower_inverses([loc["l_mat"] for loc in locs], CHUNK)
    for loc, a_inv in zip(locs, inverses):
        loc["a_inv"] = a_inv
    for loc, (q, k, v, beta, gc4) in zip(locs, chunks):
        sol = dot_nn(loc["a_inv"], jnp.concatenate([v * beta, loc["kb"] * loc["gamma"]], axis=1), FAST)
        loc.update(sol=sol, value=sol[:, :HEAD_DIM], kcd=sol[:, HEAD_DIM:])
        loc["attn"] = jnp.where(incl, dot_nt(q, k, FAST) * loc["decay"], 0.0)
    return locs


def dn_chunk_state(loc, s):
    kcd, qd, kt, cd = loc["kcd"], loc["qd"], loc["kt"], loc["cd"]
    v_new = loc["value"] - _per_head(lambda h: dot_nn(kcd[_head_rows(h)], s[h], FAST))
    o = _per_head(lambda h: dot_nn(qd[_head_rows(h)], s[h], FAST)) + dot_nn(loc["attn"], v_new, FAST)
    s_new = [s[h] * cd[h * CHUNK:h * CHUNK + 1, :] + dot_tn(kt[_head_rows(h)], v_new[_head_rows(h)], FAST)
             for h in range(N_HEADS)]
    loc["v_new"] = v_new
    return o, s_new


def dn_chunks_bwd(items, ds_last):
    hr = _head_rows
    n = len(items)
    pre = []
    for q, k, v, beta, loc, s, do in items:
        pre.append(dict(
            dv_part=dot_tn(loc["attn"], do, FAST),
            dattn=jnp.where(loc["incl"], dot_nt(do, loc["v_new"], FAST), 0.0),
            dqd=_per_head(lambda h: dot_nt(do[hr(h)], s[h], FAST)),
            ds_part=[dot_tn(loc["qd"][hr(h)], do[hr(h)], FAST) for h in range(N_HEADS)]))
    ds_new_of, dv_new_of = [None] * n, [None] * n
    ds = ds_last
    for c in reversed(range(n)):
        loc = items[c][4]
        ds_new_of[c] = ds
        dv_new = pre[c]["dv_part"] + _per_head(lambda h: dot_nn(loc["kt"][hr(h)], ds[h], FAST))
        dv_new_of[c] = dv_new
        ds = [pre[c]["ds_part"][h] + ds[h] * loc["cd"][h * CHUNK:h * CHUNK + 1, :]
              - dot_tn(loc["kcd"][hr(h)], dv_new[hr(h)], FAST) for h in range(N_HEADS)]
    is_last = (lax.broadcasted_iota(jnp.int32, (STACK, 1), 0) % CHUNK) == CHUNK - 1
    out = []
    for c, (q, k, v, beta, loc, s, do) in enumerate(items):
        decay, gamma, tau, cd, kb = loc["decay"], loc["gamma"], loc["tau"], loc["cd"], loc["kb"]
        dv_new, ds_new, dattn, dqd = dv_new_of[c], ds_new_of[c], pre[c]["dattn"], pre[c]["dqd"]
        dkt = _per_head(lambda h: dot_nt(loc["v_new"][hr(h)], ds_new[h], FAST))
        dkcd = -_per_head(lambda h: dot_nt(dv_new[hr(h)], s[h], FAST))
        drhs = dot_tn(loc["a_inv"], jnp.concatenate([dv_new, dkcd], axis=1), FAST)
        dvb, dkbg = drhs[:, :HEAD_DIM], drhs[:, HEAD_DIM:]
        dl = jnp.where(loc["strict"], -dot_nt(drhs, loc["sol"], FAST), 0.0)
        dkk = dl * decay
        dqk = dattn * decay
        e = dl * loc["l_mat"] + dattn * loc["attn"]
        dgc = jnp.sum(e, axis=1, keepdims=True) - jnp.sum(e, axis=0, keepdims=True).T
        dkb = dot_nn(dkk, k, FAST) + dkbg * gamma
        dk = dot_tn(dkk, kb, FAST) + dot_tn(dqk, q, FAST) + dkt * tau
        dq = dot_nn(dqk, k, FAST) + dqd * gamma
        dgamma = jnp.sum(dkbg * kb, axis=1, keepdims=True) + jnp.sum(dqd * q, axis=1, keepdims=True)
        dtau_tau = jnp.sum(dkt * k, axis=1, keepdims=True) * tau
        dgc = dgc + dgamma * gamma - dtau_tau

        def last_term(h):
            dcd = jnp.sum(jnp.sum(ds_new[h] * s[h], axis=1, keepdims=True), axis=0, keepdims=True)
            total = jnp.sum(dtau_tau[hr(h)], axis=0, keepdims=True) + dcd * cd[h * CHUNK:h * CHUNK + 1, :]
            return jnp.broadcast_to(total, (CHUNK, 1))

        dgc = dgc + jnp.where(is_last, _per_head(last_term), 0.0)
        dk = dk + dkb * beta
        dbeta = jnp.sum(dkb * k, axis=1, keepdims=True) + jnp.sum(dvb * v, axis=1, keepdims=True)
        out.append((dq, dk, dvb * beta, dbeta, dgc))
    return out, ds


def _token_tile(t):
    return _pick(t, 256)


STRIP = 32


def _for_strips(n_rows, rows, fn, start=0):
    def step(r, carry):
        fn(pl.multiple_of(r * rows, rows))
        return carry

    lax.fori_loop(start, n_rows // rows, step, 0)


def _fold_rows(x):
    out = x[0:SUBLANES, :]
    for i in range(1, x.shape[0] // SUBLANES):
        out = out + x[i * SUBLANES:(i + 1) * SUBLANES, :]
    return out


def _matmul(a, b, mode, name, tiles, add=None, out_dtype=F32):
    if mode == "nn":
        (m, k), n = a.shape, b.shape[1]
    elif mode == "nt":
        (m, k), n = a.shape, b.shape[0]
    else:
        (k, m), n = a.shape, b.shape[1]
    tm, tn, tk = min(tiles[0], m), min(tiles[1], n), min(tiles[2], k)
    assert m % tm == 0 and n % tn == 0 and k % tk == 0, (name, m, n, k, tiles)
    nk = k // tk
    dims = {"nn": (((1,), (0,)), ((), ())), "nt": (((1,), (1,)), ((), ())), "tn": (((0,), (0,)), ((), ()))}[mode]

    def finish(res, add_ref, o_ref):
        if add_ref is not None:
            res = res + add_ref[...]
        o_ref[...] = res.astype(o_ref.dtype)

    def body(*refs):
        a_ref, b_ref = refs[0], refs[1]
        add_ref = refs[2] if add is not None else None
        o_ref = refs[3] if add is not None else refs[2]
        part = lax.dot_general(a_ref[...], b_ref[...], dims, preferred_element_type=F32)
        if nk == 1:
            finish(part, add_ref, o_ref)
            return
        acc_ref = refs[-1]
        kk = pl.program_id(2)

        @pl.when(kk == 0)
        def _():
            acc_ref[...] = part

        @pl.when(kk > 0)
        def _():
            acc_ref[...] += part

        @pl.when(kk == nk - 1)
        def _():
            finish(acc_ref[...], add_ref, o_ref)

    a_spec = pl.BlockSpec((tk, tm), lambda j, i, kk: (kk, i)) if mode == "tn" else pl.BlockSpec((tm, tk), lambda j, i, kk: (i, kk))
    b_spec = pl.BlockSpec((tn, tk), lambda j, i, kk: (j, kk)) if mode == "nt" else pl.BlockSpec((tk, tn), lambda j, i, kk: (kk, j))
    o_spec = pl.BlockSpec((tm, tn), lambda j, i, kk: (i, j))
    in_specs = [a_spec, b_spec] + ([o_spec] if add is not None else [])
    args = (a, b) + ((add,) if add is not None else ())
    return _pcall(
        body, grid=(n // tn, m // tm, nk), in_specs=in_specs, out_specs=o_spec,
        out_shape=jax.ShapeDtypeStruct((m, n), out_dtype),
        scratch_shapes=[pltpu.VMEM((tm, tn), F32)] if nk > 1 else [],
        compiler_params=_params("parallel", "parallel", "arbitrary"), name=name)(*args)


def _matmul_rows(a, b, mode, name, tm, extra, outs, fn):
    m, k = a.shape
    n = b.shape[1] if mode == "nn" else b.shape[0]
    tm = min(tm, m)
    dims = (((1,), (0,)), ((), ())) if mode == "nn" else (((1,), (1,)), ((), ()))

    def spec(shape, kind):
        if kind == "rows":
            return pl.BlockSpec((tm, shape[1]), lambda i: (i, 0))
        return pl.BlockSpec(shape, lambda i: (0,) * len(shape))

    def body(a_ref, b_ref, *refs):
        rows = lax.dot_general(a_ref[...], b_ref[...], dims, preferred_element_type=F32)
        fn(rows, pl.program_id(0) == 0, *refs)

    return _pcall(
        body, grid=(m // tm,),
        in_specs=[pl.BlockSpec((tm, k), lambda i: (i, 0)), pl.BlockSpec(b.shape, lambda i: (0, 0))]
        + [spec(x.shape, kind) for x, kind in extra],
        out_specs=[spec(shape, kind) for shape, _, kind in outs],
        out_shape=[jax.ShapeDtypeStruct(shape, dtype) for shape, dtype, _ in outs],
        compiler_params=_params("arbitrary"), name=name)(a, b, *[x for x, _ in extra])


def _rmsnorm_matmul(x, g, b_t, name, tm):
    t, d = x.shape
    n = b_t.shape[0]
    tm = min(tm, t)

    def body(x_ref, g_ref, b_ref, o_ref, h_ref, r_ref):
        y, r = rms_fwd(x_ref[...], g_ref[...])
        h = y.astype(BF16)
        h_ref[...] = h
        r_ref[...] = r
        o_ref[...] = lax.dot_general(h, b_ref[...], (((1,), (1,)), ((), ())), preferred_element_type=F32)

    rows = lambda w: pl.BlockSpec((tm, w), lambda i: (i, 0))
    return _pcall(
        body, grid=(t // tm,),
        in_specs=[rows(d), pl.BlockSpec((1, d), lambda i: (0, 0)), pl.BlockSpec((n, d), lambda i: (0, 0))],
        out_specs=[rows(n), rows(d), rows(1)],
        out_shape=[jax.ShapeDtypeStruct((t, n), F32), jax.ShapeDtypeStruct((t, d), BF16), jax.ShapeDtypeStruct((t, 1), F32)],
        compiler_params=_params("parallel"), name=name)(x, g, b_t)


def _rmsnorm_bwd_rows(t, d):
    def fn(dh, first, x_ref, r_ref, g_ref, dres_ref, dx_ref, dxb_ref, dg_ref):
        dx, dg_rows = rms_bwd(x_ref[...], r_ref[...], g_ref[...], dh)
        dx = dx + dres_ref[...]
        dx_ref[...] = dx
        dxb_ref[...] = dx.astype(BF16)

        @pl.when(first)
        def _():
            dg_ref[...] = jnp.zeros_like(dg_ref)

        dg_ref[...] += jnp.sum(dg_rows, axis=0, keepdims=True)

    return fn, [((t, d), F32, "rows"), ((t, d), BF16, "rows"), ((1, d), F32, "whole")]


def _final_loss_rows(t, d):
    def fn(rows, first, res_ref, t_ref, g_ref, loss_ref, dx_ref, dxb_ref, dg_ref):
        @pl.when(first)
        def _():
            loss_ref[...] = jnp.zeros_like(loss_ref)
            dg_ref[...] = jnp.zeros_like(dg_ref)

        x = rows + res_ref[...]
        y, r = rms_fwd(x, g_ref[...])
        err = y - t_ref[...]
        loss_ref[...] += 0.5 * jnp.sum(jnp.mean(err * err, axis=-1, keepdims=True), axis=0, keepdims=True)
        dx, dg_rows = rms_bwd(x, r, g_ref[...], err * (1.0 / d))
        dx_ref[...] = dx
        dxb_ref[...] = dx.astype(BF16)
        dg_ref[...] += jnp.sum(dg_rows, axis=0, keepdims=True)

    return fn, [((1, LANES), F32, "whole"), ((t, d), F32, "rows"), ((t, d), BF16, "rows"), ((1, d), F32, "whole")]


def _prev_halo_spec(tm, width, col_block):
    return pl.BlockSpec((HALO, width), lambda i: (jnp.maximum(i * (tm // HALO) - 1, 0), col_block))


def _history(tile_ref, halo_ref, first, row0, cols):
    if isinstance(row0, int) and row0 == 0:
        return jnp.concatenate([jnp.where(first, 0.0, halo_ref[:, cols]), tile_ref[0:STRIP, cols]], axis=0)
    return tile_ref[pl.ds(pl.multiple_of(row0 - HALO, HALO), STRIP + HALO), cols]


def _first_then_strips(n_rows, fn):
    fn(0)
    _for_strips(n_rows, STRIP, fn, start=1)


def _delays(ext, taps):
    return [ext[HALO:, :]] + [pltpu.roll(ext, j, 0)[HALO:, :] for j in range(1, taps)]


def _causal_conv(delayed, w):
    taps = len(delayed)
    out = delayed[0] * w[taps - 1:taps, :]
    for j in range(1, taps):
        out = out + delayed[j] * w[taps - 1 - j:taps - j, :]
    return out


def _advanced_conv(buf_ref, row0, cols, w):
    return _advanced(buf_ref[pl.ds(row0, STRIP + HALO), cols], w)


def _advanced(ext, w):
    taps = w.shape[0]
    out = ext[:STRIP, :] * w[taps - 1:taps, :]
    for j in range(1, taps):
        out = out + pltpu.roll(ext, STRIP + HALO - j, 0)[:STRIP, :] * w[taps - 1 - j:taps - j, :]
    return out


def _dn_prep(p, conv_w, a_log4, dt_bias4):
    t = p.shape[0]
    tm = _token_tile(t)
    w3 = 3 * DN_WIDTH

    def body(x_ref, halo_ref, pbd_ref, w_ref, alog_ref, dtb_ref, q_ref, k_ref, v_ref, beta_ref, g_ref):
        first = pl.program_id(0) == 0

        def strip(row0):
            rows = pl.ds(row0, STRIP)
            for h in range(N_HEADS):
                sl = slice(h * HEAD_DIM, (h + 1) * HEAD_DIM)
                for part, out_ref in ((0, q_ref), (1, k_ref), (2, v_ref)):
                    cols = slice(part * DN_WIDTH + h * HEAD_DIM, part * DN_WIDTH + (h + 1) * HEAD_DIM)
                    y = silu(_causal_conv(_delays(_history(x_ref, halo_ref, first, row0, cols), CONV_K), w_ref[:, cols]))
                    if part == 0:
                        y = l2_fwd(y)[0] * (HEAD_DIM ** -0.5)
                    elif part == 1:
                        y = l2_fwd(y)[0]
                    out_ref[rows, sl] = y
            head = lax.broadcasted_iota(jnp.int32, (STRIP, LANES), 1) < N_HEADS
            pbd = pbd_ref[rows, :]
            beta_ref[rows, :] = jnp.where(head, sigmoid(pbd), 0.0)
            a_raw = pltpu.roll(pbd, LANES - N_HEADS, 1)
            g_ref[rows, :] = jnp.where(head, -jnp.exp(alog_ref[...]) * softplus(a_raw + dtb_ref[...]), 0.0)

        _first_then_strips(tm, strip)

    tok = lambda w, cb: pl.BlockSpec((tm, w), lambda i: (i, cb))
    full = lambda a: pl.BlockSpec(a.shape, lambda i: (0, 0))
    return _pcall(
        body, grid=(t // tm,),
        in_specs=[tok(w3, 0), _prev_halo_spec(tm, w3, 0), tok(LANES, PROJ_MAIN // LANES),
                  full(conv_w), full(a_log4), full(dt_bias4)],
        out_specs=[tok(DN_WIDTH, 0)] * 3 + [tok(LANES, 0)] * 2,
        out_shape=[jax.ShapeDtypeStruct((t, DN_WIDTH), F32)] * 3 + [jax.ShapeDtypeStruct((t, LANES), F32)] * 2,
        compiler_params=_params("parallel"), name="dn_prep")(p, p, p, conv_w, a_log4, dt_bias4)


def _dn_prep_bwd(p, conv_w, a_log4, dt_bias4, dq, dk, dv, dbeta4, dg4, dp_buf):
    t = p.shape[0]
    tm = _token_tile(t)
    w3 = 3 * DN_WIDTH

    def body(x_ref, halo_ref, pbd_ref, w_ref, alog_ref, dtb_ref, dq_ref, dk_ref, dv_ref, dbeta_ref, dg_ref, _,
             dc_ref, dw_ref, dpbd_ref, dalog_ref, ddtb_ref, dw_acc, lane_acc):
        first = pl.program_id(0) == 0
        dw_acc[...] = jnp.zeros_like(dw_acc)
        lane_acc[...] = jnp.zeros_like(lane_acc)

        def strip(row0):
            rows = pl.ds(row0, STRIP)
            for h in range(N_HEADS):
                sl = slice(h * HEAD_DIM, (h + 1) * HEAD_DIM)
                for part, dy_ref in ((0, dq_ref), (1, dk_ref), (2, dv_ref)):
                    cols = slice(part * DN_WIDTH + h * HEAD_DIM, part * DN_WIDTH + (h + 1) * HEAD_DIM)
                    delayed = _delays(_history(x_ref, halo_ref, first, row0, cols), CONV_K)
                    c = _causal_conv(delayed, w_ref[:, cols])
                    dy = dy_ref[rows, sl]
                    if part < 2:
                        y = silu(c)
                        _, r = l2_fwd(y)
                        dy = l2_bwd(y, r, dy * (HEAD_DIM ** -0.5) if part == 0 else dy)
                    dc = dy * silu_grad(c)
                    dc_ref[rows, cols] = dc
                    for j in range(CONV_K):
                        k = CONV_K - 1 - j
                        dw_acc[k * SUBLANES:(k + 1) * SUBLANES, cols] += _fold_rows(dc * delayed[j])
            head = lax.broadcasted_iota(jnp.int32, (STRIP, LANES), 1) < N_HEADS
            pbd = pbd_ref[rows, :]
            beta = sigmoid(pbd)
            dpb = jnp.where(head, dbeta_ref[rows, :] * beta * (1.0 - beta), 0.0)
            z = pltpu.roll(pbd, LANES - N_HEADS, 1) + dtb_ref[...]
            neg_rate = -jnp.exp(alog_ref[...])
            dg = dg_ref[rows, :]
            dpa = jnp.where(head, dg * neg_rate * sigmoid(z), 0.0)
            dpbd_ref[rows, :] = (dpb + pltpu.roll(dpa, N_HEADS, 1)).astype(BF16)
            g = jnp.where(head, neg_rate * softplus(z), 0.0)
            lane_acc[0:SUBLANES, :] += _fold_rows(dg * g)
            lane_acc[SUBLANES:, :] += _fold_rows(dpa)

        _first_then_strips(tm, strip)

        @pl.when(first)
        def _():
            dw_ref[...] = jnp.zeros_like(dw_ref)
            dalog_ref[...] = jnp.zeros_like(dalog_ref)
            ddtb_ref[...] = jnp.zeros_like(ddtb_ref)

        for k in range(CONV_K):
            dw_ref[k:k + 1, :] += jnp.sum(dw_acc[k * SUBLANES:(k + 1) * SUBLANES, :], axis=0, keepdims=True)
        dalog_ref[...] += jnp.sum(lane_acc[0:SUBLANES, :], axis=0, keepdims=True)
        ddtb_ref[...] += jnp.sum(lane_acc[SUBLANES:, :], axis=0, keepdims=True)

    tok = lambda w, cb: pl.BlockSpec((tm, w), lambda i: (i, cb))
    full = lambda shape: pl.BlockSpec(shape, lambda i: (0, 0))
    return _pcall(
        body, grid=(t // tm,),
        in_specs=[tok(w3, 0), _prev_halo_spec(tm, w3, 0), tok(LANES, PROJ_MAIN // LANES),
                  full(conv_w.shape), full(a_log4.shape), full(dt_bias4.shape)] + [tok(DN_WIDTH, 0)] * 3 + [tok(LANES, 0)] * 2
        + [pl.BlockSpec(memory_space=pl.ANY)],
        out_specs=[tok(w3, 0), full((CONV_K, w3)), tok(LANES, PROJ_MAIN // LANES), full((1, LANES)), full((1, LANES))],
        out_shape=[jax.ShapeDtypeStruct((t, w3), F32), jax.ShapeDtypeStruct((CONV_K, w3), F32),
                   jax.ShapeDtypeStruct(dp_buf.shape, dp_buf.dtype),
                   jax.ShapeDtypeStruct((1, LANES), F32), jax.ShapeDtypeStruct((1, LANES), F32)],
        input_output_aliases={11: 2},
        scratch_shapes=[pltpu.VMEM((CONV_K * SUBLANES, w3), F32), pltpu.VMEM((2 * SUBLANES, LANES), F32)],
        compiler_params=_params("arbitrary"), name="dn_prep_bwd")(p, p, p, conv_w, a_log4, dt_bias4, dq, dk, dv, dbeta4, dg4, dp_buf)


def _conv_bwd_input(dc, w, name, out_cols=None, col_block=0, into=None):
    t, c = dc.shape
    taps = w.shape[0]
    tm = _token_tile(t)
    ct = _pick(c, 1536)
    n_tok = t // tm
    out_cols = c if out_cols is None else out_cols

    def body(dc_ref, next_ref, w_ref, *rest):
        dx_ref = rest[-1]
        last = pl.program_id(0) == n_tok - 1

        def strip(row0):
            for c0 in range(0, ct, LANES):
                cols = slice(c0, c0 + LANES)
                dx_ref[pl.ds(row0, STRIP), cols] = _advanced_conv(dc_ref, row0, cols, w_ref[:, cols]).astype(BF16)

        _for_strips(tm - STRIP, STRIP, strip)
        for c0 in range(0, ct, LANES):
            cols = slice(c0, c0 + LANES)
            ext = jnp.concatenate([dc_ref[tm - STRIP:tm, cols], jnp.where(last, 0.0, next_ref[:, cols])], axis=0)
            dx_ref[tm - STRIP:tm, cols] = _advanced(ext, w_ref[:, cols]).astype(BF16)

    in_specs = [pl.BlockSpec((tm, ct), lambda i, j: (i, j)),
                pl.BlockSpec((HALO, ct), lambda i, j: (jnp.minimum((i + 1) * (tm // HALO), t // HALO - 1), j)),
                pl.BlockSpec((taps, ct), lambda i, j: (0, j))]
    args = (dc, dc, w)
    aliases = {}
    if into is not None:
        in_specs.append(pl.BlockSpec(memory_space=pl.ANY))
        args += (into,)
        aliases = {3: 0}
    return _pcall(
        body, grid=(n_tok, c // ct), in_specs=in_specs,
        out_specs=pl.BlockSpec((tm, ct), lambda i, j: (i, j + col_block)),
        out_shape=jax.ShapeDtypeStruct((t, out_cols), BF16), input_output_aliases=aliases,
        compiler_params=_params("parallel", "parallel"), name=name)(*args)


def _dn_forward(q, k, v, beta4, g4, p, norm_g):
    t = q.shape[0]
    n = t // CHUNK
    nc = DN_FWD_CHUNKS
    rows_per_step = nc * CHUNK

    def body(q_ref, k_ref, v_ref, b_ref, g_ref, gate_ref, ng_ref, mix_ref, s_all_ref, ainv_ref, s_ref):
        @pl.when(pl.program_id(0) == 0)
        def _():
            s_ref[...] = jnp.zeros_like(s_ref)

        chunks = []
        for c in range(nc):
            rows = slice(c * CHUNK, (c + 1) * CHUNK)
            chunks.append((_stack_heads(q_ref[rows, :]), _stack_heads(k_ref[rows, :]), _stack_heads(v_ref[rows, :]),
                           _stack_lanes(b_ref[rows, :]), chunk_cumsum(g_ref[rows, :])))
        locs = dn_chunks_local(chunks)
        s = [s_ref[h] for h in range(N_HEADS)]
        for c in range(nc):
            rows = slice(c * CHUNK, (c + 1) * CHUNK)
            ainv_ref[c] = locs[c]["a_inv"].astype(BF16)
            for h in range(N_HEADS):
                s_all_ref[c, h] = s[h]
            o, s = dn_chunk_state(locs[c], s)
            o_n, _ = rms_fwd(o, ng_ref[...])
            for h in range(N_HEADS):
                sl = slice(h * HEAD_DIM, (h + 1) * HEAD_DIM)
                mix_ref[rows, sl] = (o_n[_head_rows(h)] * silu(gate_ref[rows, sl])).astype(BF16)
        for h in range(N_HEADS):
            s_ref[h] = s[h]

    ch = lambda w, cb: pl.BlockSpec((rows_per_step, w), lambda i: (i, cb))
    per_chunk = lambda *shape: pl.BlockSpec((nc,) + shape, lambda i: (i,) + (0,) * len(shape))
    return _pcall(
        body, grid=(n // nc,),
        in_specs=[ch(DN_WIDTH, 0)] * 3 + [ch(LANES, 0)] * 2 + [ch(DN_WIDTH, 3), pl.BlockSpec((1, HEAD_DIM), lambda i: (0, 0))],
        out_specs=[ch(DN_WIDTH, 0), per_chunk(N_HEADS, HEAD_DIM, HEAD_DIM), per_chunk(STACK, STACK)],
        out_shape=[jax.ShapeDtypeStruct((t, DN_WIDTH + SG_WIDTH), BF16), jax.ShapeDtypeStruct((n, N_HEADS, HEAD_DIM, HEAD_DIM), F32),
                   jax.ShapeDtypeStruct((n, STACK, STACK), BF16)],
        scratch_shapes=[pltpu.VMEM((N_HEADS, HEAD_DIM, HEAD_DIM), F32)],
        compiler_params=_params("arbitrary"), name="dn_forward")(q, k, v, beta4, g4, p, norm_g)


def _dn_backward(q, k, v, beta4, g4, p, norm_g, saved, dmix, dp_buf):
    t = q.shape[0]
    n = t // CHUNK
    steps = n // DN_CHUNKS
    rows_per_step = DN_CHUNKS * CHUNK

    def body(q_ref, k_ref, v_ref, b_ref, g_ref, gate_ref, ng_ref, s_in_ref, ainv_ref, dmix_ref, _,
             dq_ref, dk_ref, dv_ref, db_ref, dg_ref, dgate_ref, dng_ref, ds_ref):
        @pl.when(pl.program_id(0) == 0)
        def _():
            ds_ref[...] = jnp.zeros_like(ds_ref)
            dng_ref[...] = jnp.zeros_like(dng_ref)

        chunks = []
        for c in range(DN_CHUNKS):
            rows = slice(c * CHUNK, (c + 1) * CHUNK)
            chunks.append((_stack_heads(q_ref[rows, :]), _stack_heads(k_ref[rows, :]), _stack_heads(v_ref[rows, :]),
                           _stack_lanes(b_ref[rows, :]), chunk_cumsum(g_ref[rows, :])))
        items = []
        for c, loc in enumerate(dn_chunks_local(chunks, [ainv_ref[c] for c in range(DN_CHUNKS)])):
            rows = slice(c * CHUNK, (c + 1) * CHUNK)
            s = [s_in_ref[c, h] for h in range(N_HEADS)]
            o, _ = dn_chunk_state(loc, s)
            o_n, r = rms_fwd(o, ng_ref[...])
            gate = _stack_heads(gate_ref[rows, :])
            dmx = _stack_heads(dmix_ref[rows, :])
            dgate = dmx * o_n * silu_grad(gate)
            do, dng_rows = rms_bwd(o, r, ng_ref[...], dmx * silu(gate))
            dng_ref[...] += jnp.sum(dng_rows, axis=0, keepdims=True)
            for h in range(N_HEADS):
                dgate_ref[rows, h * HEAD_DIM:(h + 1) * HEAD_DIM] = dgate[_head_rows(h)].astype(BF16)
            items.append((*chunks[c][:4], loc, s, do))
        grads, ds = dn_chunks_bwd(items, [ds_ref[h] for h in range(N_HEADS)])
        lane = lax.broadcasted_iota(jnp.int32, (CHUNK, LANES), 1)
        _, strict = _tri_masks(CHUNK)
        for c in range(DN_CHUNKS):
            rows = slice(c * CHUNK, (c + 1) * CHUNK)
            dq, dk, dv, dbeta, dgc = grads[c]
            db4 = jnp.zeros((CHUNK, LANES), F32)
            dgc4 = jnp.zeros((CHUNK, LANES), F32)
            for h in range(N_HEADS):
                sl = slice(h * HEAD_DIM, (h + 1) * HEAD_DIM)
                head_rows = _head_rows(h)
                dq_ref[rows, sl] = dq[head_rows]
                dk_ref[rows, sl] = dk[head_rows]
                dv_ref[rows, sl] = dv[head_rows]
                db4 = jnp.where(lane == h, dbeta[head_rows], db4)
                dgc4 = jnp.where(lane == h, dgc[head_rows], dgc4)
            db_ref[rows, :] = db4
            dg_ref[rows, :] = dot_nn(jnp.logical_not(strict).astype(F32), dgc4)
        for h in range(N_HEADS):
            ds_ref[h] = ds[h]

    rev = lambda w, cb: pl.BlockSpec((rows_per_step, w), lambda i: (steps - 1 - i, cb))
    per_chunk = lambda a: pl.BlockSpec((DN_CHUNKS,) + a.shape[1:], lambda i: (steps - 1 - i,) + (0,) * (a.ndim - 1))
    return _pcall(
        body, grid=(steps,),
        in_specs=[rev(DN_WIDTH, 0)] * 3 + [rev(LANES, 0)] * 2 + [rev(DN_WIDTH, 3), pl.BlockSpec((1, HEAD_DIM), lambda i: (0, 0))]
        + [per_chunk(a) for a in saved] + [rev(DN_WIDTH, 0), pl.BlockSpec(memory_space=pl.ANY)],
        out_specs=[rev(DN_WIDTH, 0)] * 3 + [rev(LANES, 0)] * 2 + [rev(DN_WIDTH, 3), pl.BlockSpec((1, HEAD_DIM), lambda i: (0, 0))],
        out_shape=[jax.ShapeDtypeStruct((t, DN_WIDTH), F32)] * 3 + [jax.ShapeDtypeStruct((t, LANES), F32)] * 2
        + [jax.ShapeDtypeStruct(dp_buf.shape, dp_buf.dtype), jax.ShapeDtypeStruct((1, HEAD_DIM), F32)],
        input_output_aliases={10: 5},
        scratch_shapes=[pltpu.VMEM((N_HEADS, HEAD_DIM, HEAD_DIM), F32)],
        compiler_params=_params("arbitrary"), name="dn_backward")(q, k, v, beta4, g4, p, norm_g, *saved, dmix, dp_buf)


SG_STEP_BLOCKS = 4


def _sg_mask():
    row = lax.broadcasted_iota(jnp.int32, (SG_BLOCK, SG_BLOCK), 0)
    col = lax.broadcasted_iota(jnp.int32, (SG_BLOCK, SG_BLOCK), 1)
    return (col // CHUNK) <= (row // CHUNK)


def _sg_forward(p, norm_g, w_s, b_t, mix_buf):
    t = p.shape[0]
    step_rows = SG_STEP_BLOCKS * SG_BLOCK

    def body(u_ref, v_ref, ng_ref, w_ref, b_ref, _, o_ref):
        mask = _sg_mask()
        pairs = [(slice(b * SG_BLOCK, (b + 1) * SG_BLOCK), g, slice(g * SG_DIM, (g + 1) * SG_DIM))
                 for b in range(SG_STEP_BLOCKS) for g in range(SG_GROUPS)]
        w_m = [jnp.where(mask, w_ref[g], 0.0) for g in range(SG_GROUPS)]
        vn = [rms_fwd(gelu(v_ref[rows, sl]), ng_ref[:, sl])[0] for rows, g, sl in pairs]
        s = [dot_nn(w_m[g], vn[i], FAST) + b_ref[:, g:g + 1] for i, (rows, g, sl) in enumerate(pairs)]
        for i, (rows, g, sl) in enumerate(pairs):
            o_ref[rows, sl] = (gelu(u_ref[rows, sl]) * s[i]).astype(BF16)

    blk = lambda cb: pl.BlockSpec((step_rows, SG_WIDTH), lambda i: (i, cb))
    return _pcall(
        body, grid=(t // step_rows,),
        in_specs=[blk(4), blk(5), pl.BlockSpec((1, SG_WIDTH), lambda i: (0, 0)),
                  pl.BlockSpec((SG_GROUPS, SG_BLOCK, SG_BLOCK), lambda i: (0, 0, 0)), pl.BlockSpec((SG_BLOCK, SG_GROUPS), lambda i: (0, 0)),
                  pl.BlockSpec(memory_space=pl.ANY)],
        out_specs=blk(1), out_shape=jax.ShapeDtypeStruct(mix_buf.shape, mix_buf.dtype), input_output_aliases={5: 0},
        compiler_params=_params("parallel"), name="sg_forward")(p, p, norm_g, w_s, b_t, mix_buf)


def _sg_backward(p, norm_g, w_s, b_t, dmix):
    t = p.shape[0]

    def body(u_ref, v_ref, ng_ref, w_ref, b_ref, do_ref, duv_ref, dng_ref, dw_ref, db_ref):
        @pl.when(pl.program_id(0) == 0)
        def _():
            dng_ref[...] = jnp.zeros_like(dng_ref)
            dw_ref[...] = jnp.zeros_like(dw_ref)
            db_ref[...] = jnp.zeros_like(db_ref)

        mask = _sg_mask()
        lane = lax.broadcasted_iota(jnp.int32, (SG_BLOCK, LANES), 1)
        pairs = [(slice(b * SG_BLOCK, (b + 1) * SG_BLOCK), g, slice(g * SG_DIM, (g + 1) * SG_DIM))
                 for b in range(SG_STEP_BLOCKS) for g in range(SG_GROUPS)]
        w_m = [jnp.where(mask, w_ref[g], 0.0) for g in range(SG_GROUPS)]
        vg = [gelu(v_ref[rows, sl]) for rows, g, sl in pairs]
        normed = [rms_fwd(vg[i], ng_ref[:, sl]) for i, (rows, g, sl) in enumerate(pairs)]
        s = [dot_nn(w_m[g], normed[i][0], FAST) + b_ref[:, g:g + 1] for i, (rows, g, sl) in enumerate(pairs)]
        ds = []
        db = jnp.zeros((SG_BLOCK, LANES), F32)
        for i, (rows, g, sl) in enumerate(pairs):
            u_raw, do = u_ref[rows, sl], do_ref[rows, sl]
            duv_ref[rows, sl] = (do * s[i] * gelu_grad(u_raw)).astype(BF16)
            ds.append(do * gelu(u_raw))
            db = db + jnp.where(lane == g, jnp.sum(ds[i], axis=1, keepdims=True), 0.0)
        dw = [jnp.where(mask, dot_nt(ds[i], normed[i][0], FAST), 0.0) for i in range(len(pairs))]
        dvn = [dot_tn(w_m[g], ds[i], FAST) for i, (rows, g, sl) in enumerate(pairs)]
        for i, (rows, g, sl) in enumerate(pairs):
            dw_ref[g] += dw[i]
            dvg, dng_rows = rms_bwd(vg[i], normed[i][1], ng_ref[:, sl], dvn[i])
            dng_ref[:, sl] += jnp.sum(dng_rows, axis=0, keepdims=True)
            duv_ref[rows, SG_WIDTH + g * SG_DIM:SG_WIDTH + (g + 1) * SG_DIM] = (dvg * gelu_grad(v_ref[rows, sl])).astype(BF16)
        db_ref[...] += db

    step_rows = SG_STEP_BLOCKS * SG_BLOCK
    blk = lambda cb: pl.BlockSpec((step_rows, SG_WIDTH), lambda i: (i, cb))
    const2 = lambda shape: pl.BlockSpec(shape, lambda i: (0, 0))
    w_spec = pl.BlockSpec((SG_GROUPS, SG_BLOCK, SG_BLOCK), lambda i: (0, 0, 0))
    return _pcall(
        body, grid=(t // step_rows,),
        in_specs=[blk(4), blk(5), const2((1, SG_WIDTH)), w_spec, const2((SG_BLOCK, SG_GROUPS)), blk(1)],
        out_specs=[pl.BlockSpec((step_rows, 2 * SG_WIDTH), lambda i: (i, 2)), const2((1, SG_WIDTH)), w_spec,
                   const2((SG_BLOCK, LANES))],
        out_shape=[jax.ShapeDtypeStruct((t, PROJ_PAD), BF16), jax.ShapeDtypeStruct((1, SG_WIDTH), F32),
                   jax.ShapeDtypeStruct((SG_GROUPS, SG_BLOCK, SG_BLOCK), F32), jax.ShapeDtypeStruct((SG_BLOCK, LANES), F32)],
        compiler_params=_params("arbitrary"), name="sg_backward")(p, p, norm_g, w_s, b_t, dmix)


FFN_COLS = 256


def _norm_up_ffn(x, g, w_up_t, conv_w, conv_b):
    t, d = x.shape
    tm = min(t, 256)
    blocks = D_FF // FFN_COLS
    nt = (((1,), (1,)), ((), ()))

    def body(x_ref, g_ref, w_ref, cw_ref, cb_ref, up_ref, act_ref, h_ref, r_ref, tail_ref, prev_ref):
        @pl.when(pl.program_id(0) == 0)
        def _():
            tail_ref[...] = jnp.zeros_like(tail_ref)

        y, r = rms_fwd(x_ref[...], g_ref[...])
        h = y.astype(BF16)
        h_ref[...] = h
        r_ref[...] = r

        def project(blk):
            out = []
            for half in range(2):
                cols = slice(half * D_FF + blk * FFN_COLS, half * D_FF + (blk + 1) * FFN_COLS)
                u = lax.dot_general(h, w_ref[cols, :], nt, preferred_element_type=F32)
                up_ref[:, cols] = u
                prev_ref[:, cols] = tail_ref[:, cols]
                tail_ref[:, cols] = u[tm - HALO:, :]
                out.append(cols)
            return out

        def history(row0, cols):
            if row0 == 0:
                return jnp.concatenate([prev_ref[:, cols], up_ref[0:STRIP, cols]], axis=0)
            return up_ref[row0 - HALO:row0 + STRIP, cols]

        def activate(blk, g_cols, v_cols):
            for row0 in range(0, tm, STRIP):
                for c0 in range(0, FFN_COLS, LANES):
                    gc = slice(g_cols.start + c0, g_cols.start + c0 + LANES)
                    vc = slice(v_cols.start + c0, v_cols.start + c0 + LANES)
                    cg = _causal_conv(_delays(history(row0, gc), FFN_CONV), cw_ref[:, gc]) + cb_ref[:, gc]
                    cv = _causal_conv(_delays(history(row0, vc), FFN_CONV), cw_ref[:, vc]) + cb_ref[:, vc]
                    act_ref[row0:row0 + STRIP, blk * FFN_COLS + c0:blk * FFN_COLS + c0 + LANES] = (silu(cg) * cv).astype(BF16)

        pending = None
        for blk in range(blocks):
            cols = project(blk)
            if pending is not None:
                activate(*pending)
            pending = (blk, *cols)
        activate(*pending)

    rows = lambda w: pl.BlockSpec((tm, w), lambda i: (i, 0))
    whole = lambda a: pl.BlockSpec(a.shape, lambda i: (0, 0))
    return _pcall(
        body, grid=(t // tm,),
        in_specs=[rows(d), whole(g), whole(w_up_t), whole(conv_w), whole(conv_b)],
        out_specs=[rows(2 * D_FF), rows(D_FF), rows(d), rows(1)],
        out_shape=[jax.ShapeDtypeStruct((t, 2 * D_FF), F32), jax.ShapeDtypeStruct((t, D_FF), BF16),
                   jax.ShapeDtypeStruct((t, d), BF16), jax.ShapeDtypeStruct((t, 1), F32)],
        scratch_shapes=[pltpu.VMEM((HALO, 2 * D_FF), F32), pltpu.VMEM((HALO, 2 * D_FF), F32)],
        compiler_params=_params("arbitrary"), name="norm_up_ffn")(x, g, w_up_t, conv_w, conv_b)


def _ffn_bwd(up, conv_w, conv_b, dact):
    t = up.shape[0]
    tm = _pick(t, 256)
    n_tok = t // tm
    width = 2 * D_FF

    def dconv(delayed_g, delayed_v, da, wg, wv, bg, bv):
        cg = _causal_conv(delayed_g, wg) + bg
        cv = _causal_conv(delayed_v, wv) + bv
        s = sigmoid(cg)
        return da * cv * (s * (1.0 + cg * (1.0 - s))), da * (cg * s)

    def body(up_ref, prev_ref, next_ref, da_ref, dan_ref, w_ref, b_ref, dup_ref, dw_ref, db_ref, dc_ref, dw_acc, db_acc):
        first = pl.program_id(0) == 0
        last = pl.program_id(0) == n_tok - 1
        dw_acc[...] = jnp.zeros_like(dw_acc)
        db_acc[...] = jnp.zeros_like(db_acc)

        def strip(row0):
            rows = pl.ds(row0, STRIP)
            for c0 in range(0, D_FF, LANES):
                gc, vc = slice(c0, c0 + LANES), slice(D_FF + c0, D_FF + c0 + LANES)
                del_g = _delays(_history(up_ref, prev_ref, first, row0, gc), FFN_CONV)
                del_v = _delays(_history(up_ref, prev_ref, first, row0, vc), FFN_CONV)
                dcg, dcv = dconv(del_g, del_v, da_ref[rows, gc], w_ref[:, gc], w_ref[:, vc], b_ref[:, gc], b_ref[:, vc])
                dc_ref[rows, gc] = dcg
                dc_ref[rows, vc] = dcv
                db_acc[:, gc] += _fold_rows(dcg)
                db_acc[:, vc] += _fold_rows(dcv)
                for j in range(FFN_CONV):
                    k = FFN_CONV - 1 - j
                    dw_acc[k * SUBLANES:(k + 1) * SUBLANES, gc] += _fold_rows(dcg * del_g[j])
                    dw_acc[k * SUBLANES:(k + 1) * SUBLANES, vc] += _fold_rows(dcv * del_v[j])

        _first_then_strips(tm, strip)

        for c0 in range(0, D_FF, LANES):
            gc, vc = slice(c0, c0 + LANES), slice(D_FF + c0, D_FF + c0 + LANES)

            def delayed(cols):
                return _delays(jnp.concatenate([up_ref[tm - HALO:tm, cols], next_ref[:, cols]], axis=0), FFN_CONV)

            dcg, dcv = dconv(delayed(gc), delayed(vc), dan_ref[:, gc], w_ref[:, gc], w_ref[:, vc], b_ref[:, gc], b_ref[:, vc])
            dc_ref[tm:, gc] = jnp.where(last, 0.0, dcg)
            dc_ref[tm:, vc] = jnp.where(last, 0.0, dcv)

        def strip_dx(row0):
            for c0 in range(0, width, LANES):
                cols = slice(c0, c0 + LANES)
                dup_ref[pl.ds(row0, STRIP), cols] = _advanced_conv(dc_ref, row0, cols, w_ref[:, cols]).astype(BF16)

        _for_strips(tm, STRIP, strip_dx)

        @pl.when(first)
        def _():
            dw_ref[...] = jnp.zeros_like(dw_ref)
            db_ref[...] = jnp.zeros_like(db_ref)

        for k in range(FFN_CONV):
            dw_ref[k:k + 1, :] += jnp.sum(dw_acc[k * SUBLANES:(k + 1) * SUBLANES, :], axis=0, keepdims=True)
        db_ref[...] += jnp.sum(db_acc[...], axis=0, keepdims=True)

    next_rows = lambda i: jnp.minimum((i + 1) * (tm // HALO), t // HALO - 1)
    full = lambda rows: pl.BlockSpec((rows, width), lambda i: (0, 0))
    return _pcall(
        body, grid=(n_tok,),
        in_specs=[pl.BlockSpec((tm, width), lambda i: (i, 0)),
                  pl.BlockSpec((HALO, width), lambda i: (jnp.maximum(i * (tm // HALO) - 1, 0), 0)),
                  pl.BlockSpec((HALO, width), lambda i: (next_rows(i), 0)),
                  pl.BlockSpec((tm, D_FF), lambda i: (i, 0)), pl.BlockSpec((HALO, D_FF), lambda i: (next_rows(i), 0)),
                  full(FFN_CONV), full(1)],
        out_specs=[pl.BlockSpec((tm, width), lambda i: (i, 0)), full(FFN_CONV), full(1)],
        out_shape=[jax.ShapeDtypeStruct((t, width), BF16), jax.ShapeDtypeStruct((FFN_CONV, width), F32),
                   jax.ShapeDtypeStruct((1, width), F32)],
        scratch_shapes=[pltpu.VMEM((tm + HALO, width), F32),
                        pltpu.VMEM((FFN_CONV * SUBLANES, width), F32), pltpu.VMEM((SUBLANES, width), F32)],
        compiler_params=_params("arbitrary"), name="ffn_bwd")(up, up, up, dact, dact, conv_w, conv_b)


def _my_position():
    return lax.axis_index("x"), lax.axis_index("y"), lax.axis_index("c")


COPIES = N_DEV - 1


def _all_gather(arrays):
    n = len(arrays)

    def body(*refs):
        x_refs, out_refs = refs[:n], refs[n:2 * n]
        send_sems, recv_sems, local_sems = refs[2 * n:]
        x, y, cc = _my_position()
        me, sibling = (x, y, cc), (x, y, 1 - cc)
        chips = [(1 - x, y), (x, 1 - y), (1 - x, 1 - y)]

        def block(a, px, py, pc):
            return out_refs[a].at[4 * px + 2 * py + pc]

        def copy(a, k, blk, to, src=None):
            return pltpu.make_async_remote_copy(
                src_ref=block(a, *blk) if src is None else src, dst_ref=block(a, *blk),
                send_sem=send_sems.at[a * COPIES + k], recv_sem=recv_sems.at[a * COPIES + k],
                device_id=to, device_id_type=MESH_ID)

        mine = [pltpu.make_async_copy(x_refs[a], block(a, *me), local_sems.at[a]) for a in range(n)]
        for cp in mine:
            cp.start()
        first = []
        for a in range(n):
            first.append(copy(a, 0, me, sibling, src=x_refs[a]))
            first += [copy(a, 1 + j, me, (*chip, cc), src=x_refs[a]) for j, chip in enumerate(chips)]
        for cp in first:
            cp.start()
        passed = []
        for j, chip in enumerate(chips):
            for a in range(n):
                copy(a, 1 + j, (*chip, cc), me).wait_recv()
                passed.append(copy(a, 4 + j, (*chip, cc), sibling))
                passed[-1].start()
        for a in range(n):
            copy(a, 0, sibling, me).wait_recv()
        for j, chip in enumerate(chips):
            for a in range(n):
                copy(a, 4 + j, (*chip, 1 - cc), me).wait_recv()
        for cp in first + passed:
            cp.wait_send()
        for cp in mine:
            cp.wait()

    any_spec = pl.BlockSpec(memory_space=pl.ANY)
    return _pcall(
        body, out_shape=[jax.ShapeDtypeStruct((N_DEV,) + a.shape, a.dtype) for a in arrays],
        in_specs=[any_spec] * n, out_specs=[any_spec] * n,
        scratch_shapes=[pltpu.SemaphoreType.DMA((n * COPIES,)), pltpu.SemaphoreType.DMA((n * COPIES,)),
                        pltpu.SemaphoreType.DMA((n,))],
        name="all_gather")(*arrays)


def _all_to_all(sends):
    n = len(sends)

    def body(*refs):
        send_refs, recv_refs = refs[:n], refs[n:2 * n]
        send_sems, recv_sems, local_sems = refs[2 * n:]
        x, y, cc = _my_position()
        me = 4 * x + 2 * y + cc
        mine = [pltpu.make_async_copy(send_refs[a].at[me], recv_refs[a].at[me], local_sems.at[a]) for a in range(n)]
        for cp in mine:
            cp.start()
        copies = []
        for rel in range(1, N_DEV):
            px, py, pc = x ^ (rel >> 2), y ^ ((rel >> 1) & 1), cc ^ (rel & 1)
            for a in range(n):
                copies.append(pltpu.make_async_remote_copy(
                    src_ref=send_refs[a].at[4 * px + 2 * py + pc], dst_ref=recv_refs[a].at[me],
                    send_sem=send_sems.at[a * COPIES + rel - 1], recv_sem=recv_sems.at[a * COPIES + rel - 1],
                    device_id=(px, py, pc), device_id_type=MESH_ID))
        for cp in copies:
            cp.start()
        for cp in copies:
            cp.wait()
        for cp in mine:
            cp.wait()

    any_spec = pl.BlockSpec(memory_space=pl.ANY)
    return _pcall(
        body, out_shape=[jax.ShapeDtypeStruct(s.shape, s.dtype) for s in sends],
        in_specs=[any_spec] * n, out_specs=[any_spec] * n,
        scratch_shapes=[pltpu.SemaphoreType.DMA((n * COPIES,)), pltpu.SemaphoreType.DMA((n * COPIES,)),
                        pltpu.SemaphoreType.DMA((n,))],
        name="all_to_all")(*sends)


def _hbm(a):
    return pltpu.with_memory_space_constraint(a, pltpu.HBM)


def _split_copies(send_refs, land_refs, send_sems, recv_sems, local_sems, gather):
    x, y, cc = _my_position()
    me = 4 * x + 2 * y + cc
    local, remote = [], []
    for a, (send, land) in enumerate(zip(send_refs, land_refs)):
        local.append(pltpu.make_async_copy(send if gather else send.at[me], land.at[me], local_sems.at[a]))
    for a, (send, land) in enumerate(zip(send_refs, land_refs)):
        for rel in range(1, N_DEV):
            px, py, pc = x ^ (rel >> 2), y ^ ((rel >> 1) & 1), cc ^ (rel & 1)
            remote.append(pltpu.make_async_remote_copy(
                src_ref=send if gather else send.at[4 * px + 2 * py + pc], dst_ref=land.at[me],
                send_sem=send_sems.at[a * COPIES + rel - 1], recv_sem=recv_sems.at[a * COPIES + rel - 1],
                device_id=(px, py, pc), device_id_type=MESH_ID))
    return local, remote


SPLIT_EFFECT = pltpu.SideEffectType.DATAFLOW_SIDE_EFFECTING


def _exchange_start(sends, after, gather, name):
    n = len(sends)
    lands = [_hbm(lax.empty((N_DEV,) + s.shape if gather else s.shape, s.dtype)) for s in sends]

    def body(*refs):
        send_refs, land_refs = refs[:n], refs[n:2 * n]
        send_sems, recv_sems, local_sems = refs[2 * n + 1:2 * n + 4]
        token = refs[-1]
        local, remote = _split_copies(send_refs, land_refs, send_sems, recv_sems, local_sems, gather)
        for cp in local + remote:
            cp.start()
        token[...] = jnp.zeros_like(token)

    hbm, sem = pl.BlockSpec(memory_space=pltpu.HBM), pl.BlockSpec(memory_space=pltpu.SEMAPHORE)
    out = _pcall(
        body, name=name,
        out_shape=[pltpu.SemaphoreType.DMA((n * COPIES,)), pltpu.SemaphoreType.DMA((n * COPIES,)), pltpu.SemaphoreType.DMA((n,))]
        + [pltpu.HBM(s.shape, s.dtype) for s in sends] + [pltpu.HBM(z.shape, z.dtype) for z in lands]
        + [jax.ShapeDtypeStruct((SUBLANES, LANES), F32)],
        in_specs=[hbm] * (2 * n) + [pl.BlockSpec(memory_space=pl.ANY)],
        out_specs=[sem] * 3 + [hbm] * (2 * n) + [pl.BlockSpec(memory_space=pltpu.VMEM)],
        input_output_aliases={i: 3 + i for i in range(2 * n)},
        compiler_params=pltpu.CompilerParams(has_side_effects=SPLIT_EFFECT),
    )(*[_hbm(s) for s in sends], *lands, after)
    return dict(sems=out[:3], sends=out[3:3 + n], lands=out[3 + n:3 + 2 * n], gather=gather), out[-1]


def _exchange_wait(handle, after, name):
    sends, lands, gather = handle["sends"], handle["lands"], handle["gather"]
    n = len(sends)

    def body(*refs):
        send_refs, land_refs = refs[:n], refs[n:2 * n]
        send_sems, recv_sems, local_sems = refs[2 * n:2 * n + 3]
        local, remote = _split_copies(send_refs, land_refs, send_sems, recv_sems, local_sems, gather)
        for cp in remote:
            cp.wait_send()
            cp.wait_recv()
        for cp in local:
            cp.wait()

    hbm, sem = pl.BlockSpec(memory_space=pltpu.HBM), pl.BlockSpec(memory_space=pltpu.SEMAPHORE)
    out = _pcall(
        body, name=name,
        out_shape=[pltpu.HBM(s.shape, s.dtype) for s in sends] + [pltpu.HBM(z.shape, z.dtype) for z in lands],
        in_specs=[hbm] * (2 * n) + [sem] * 3 + [pl.BlockSpec(memory_space=pl.ANY)],
        out_specs=[hbm] * (2 * n), input_output_aliases={i: i for i in range(2 * n)},
        compiler_params=pltpu.CompilerParams(has_side_effects=SPLIT_EFFECT),
    )(*sends, *lands, *handle["sems"], after)
    return out[n:]


def _sum_and_adamw(recv, w, m, v, name):
    _, r, wp = recv.shape
    c = w.shape[-1]
    lead = w.ndim == 3
    tr = max([d for d in range(2 * SUBLANES, 257, 2 * SUBLANES) if r % d == 0], default=r)
    bc1 = 1.0 - ADAM_B1 ** ADAM_STEP
    bc2 = 1.0 - ADAM_B2 ** ADAM_STEP

    def body(recv_ref, w_ref, m_ref, v_ref, g_ref, d_ref, nm_ref, nv_ref):
        g = recv_ref[0, :, 0:c].astype(F32)
        for s in range(1, N_DEV):
            g = g + recv_ref[s, :, 0:c].astype(F32)
        m_new = ADAM_B1 * m_ref[...] + (1.0 - ADAM_B1) * g
        v_new = ADAM_B2 * v_ref[...] + (1.0 - ADAM_B2) * (g * g)
        m_hat = m_new / bc1
        v_hat = v_new / bc2
        g_ref[...] = g
        d_ref[...] = -ADAM_LR * (m_hat / (jnp.sqrt(v_hat) + ADAM_EPS) + ADAM_WD * w_ref[...])
        nm_ref[...] = m_new
        nv_ref[...] = v_new

    tile = pl.BlockSpec((None, tr, c), lambda i: (0, i, 0)) if lead else pl.BlockSpec((tr, c), lambda i: (i, 0))
    return _pcall(
        body, grid=(r // tr,),
        in_specs=[pl.BlockSpec((N_DEV, tr, wp), lambda i: (0, i, 0)), tile, tile, tile],
        out_specs=[tile] * 4, out_shape=[jax.ShapeDtypeStruct(w.shape, F32)] * 4,
        compiler_params=_params("parallel"), name=name)(recv, w, m, v)


SHARDED_TAPS = ("dn_conv_w", "ffn_conv_w")
REPLICATED = ("attn_norm_g", "dn_a_log", "dn_dt_bias", "dn_out_norm_g", "sg_norm_g", "sg_w", "sg_b", "ffn_norm_g",
              "ffn_conv_b", "final_norm_g")
SMALL = SHARDED_TAPS + REPLICATED
WEIGHT_ORDER = ("attn_norm_g", "w_in", "dn_conv_w", "dn_a_log", "dn_dt_bias", "dn_out_norm_g", "sg_norm_g", "sg_w", "sg_b",
                "w_out", "ffn_norm_g", "w_up", "ffn_conv_w", "ffn_conv_b", "w_down", "final_norm_g")
SLAB_COLS = 1024


def _pad_to(flat, multiple):
    pad = (-flat.shape[-1]) % multiple
    if pad == 0:
        return flat
    return jnp.pad(flat, [(0, 0)] * (flat.ndim - 1) + [(0, pad)])


def _pack_small(named):
    flat = jnp.concatenate([named[n].reshape(-1) for n in SMALL])
    return _pad_to(flat, SUBLANES * SLAB_COLS).reshape(-1, SLAB_COLS)


def _unpack_small(slab, like):
    flat = slab.reshape(-1)
    out, off = {}, 0
    for n in SMALL:
        size = like[n].size
        out[n] = flat[off:off + size].reshape(like[n].shape)
        off += size
    return out


def _split_columns(full, n_local):
    r = full.shape[0]
    return full.reshape(r, N_DEV, n_local).transpose(1, 0, 2).reshape(N_DEV, r * n_local)


def _join_columns(blocks, r, n_local):
    return blocks.reshape(N_DEV, r, n_local).transpose(1, 0, 2).reshape(r, N_DEV * n_local)


def _lanes4(a):
    return jnp.pad(a.reshape(1, N_HEADS), ((0, 0), (0, LANES - N_HEADS)))


def kernel(x, attn_norm_g, w_in, dn_conv_w, dn_a_log, dn_dt_bias, dn_out_norm_g, sg_norm_g, sg_w, sg_b, w_out, ffn_norm_g, w_up, ffn_conv_w, ffn_conv_b, w_down, final_norm_g, loss_target, m_attn_norm_g, m_w_in, m_dn_conv_w, m_dn_a_log, m_dn_dt_bias, m_dn_out_norm_g, m_sg_norm_g, m_sg_w, m_sg_b, m_w_out, m_ffn_norm_g, m_w_up, m_ffn_conv_w, m_ffn_conv_b, m_w_down, m_final_norm_g, v_attn_norm_g, v_w_in, v_dn_conv_w, v_dn_a_log, v_dn_dt_bias, v_dn_out_norm_g, v_sg_norm_g, v_sg_w, v_sg_b, v_w_out, v_ffn_norm_g, v_w_up, v_ffn_conv_w, v_ffn_conv_b, v_w_down, v_final_norm_g):
    weights = dict(attn_norm_g=attn_norm_g, w_in=w_in, dn_conv_w=dn_conv_w, dn_a_log=dn_a_log, dn_dt_bias=dn_dt_bias,
                   dn_out_norm_g=dn_out_norm_g, sg_norm_g=sg_norm_g, sg_w=sg_w, sg_b=sg_b, w_out=w_out, ffn_norm_g=ffn_norm_g,
                   w_up=w_up, ffn_conv_w=ffn_conv_w, ffn_conv_b=ffn_conv_b, w_down=w_down, final_norm_g=final_norm_g)
    m_in = dict(attn_norm_g=m_attn_norm_g, w_in=m_w_in, dn_conv_w=m_dn_conv_w, dn_a_log=m_dn_a_log, dn_dt_bias=m_dn_dt_bias,
                dn_out_norm_g=m_dn_out_norm_g, sg_norm_g=m_sg_norm_g, sg_w=m_sg_w, sg_b=m_sg_b, w_out=m_w_out,
                ffn_norm_g=m_ffn_norm_g, w_up=m_w_up, ffn_conv_w=m_ffn_conv_w, ffn_conv_b=m_ffn_conv_b, w_down=m_w_down,
                final_norm_g=m_final_norm_g)
    v_in = dict(attn_norm_g=v_attn_norm_g, w_in=v_w_in, dn_conv_w=v_dn_conv_w, dn_a_log=v_dn_a_log, dn_dt_bias=v_dn_dt_bias,
                dn_out_norm_g=v_dn_out_norm_g, sg_norm_g=v_sg_norm_g, sg_w=v_sg_w, sg_b=v_sg_b, w_out=v_w_out,
                ffn_norm_g=v_ffn_norm_g, w_up=v_w_up, ffn_conv_w=v_ffn_conv_w, ffn_conv_b=v_ffn_conv_b, w_down=v_w_down,
                final_norm_g=v_final_norm_g)

    n_in, n_up = w_in.shape[2], w_up.shape[2]
    r_out, r_down = w_out.shape[1], w_down.shape[1]
    n_dnc, n_ffc = dn_conv_w.shape[2], ffn_conv_w.shape[2]
    transposed = lambda a: jnp.transpose(a, (0, 2, 1))
    taps = _pad_to(jnp.concatenate([dn_conv_w.reshape(-1), ffn_conv_w.reshape(-1)]), SUBLANES * LANES).reshape(-1, LANES)
    g_in, g_taps = _all_gather([transposed(w_in)[0].astype(BF16), taps])
    gather_out, token = _exchange_start([w_out[0].astype(BF16)], g_taps, True, "gather_w_out")
    gather_up, token = _exchange_start([transposed(w_up)[0].astype(BF16)], token, True, "gather_w_up")
    gather_down, token = _exchange_start([w_down[0].astype(BF16)], token, True, "gather_w_down")
    w_in_t = jnp.pad(g_in.reshape(N_DEV * n_in, D_MODEL), ((0, PROJ_PAD - N_DEV * n_in), (0, 0)))
    taps_all = g_taps.reshape(N_DEV, -1)
    dn_conv_full = _join_columns(taps_all[:, :CONV_K * n_dnc], CONV_K, n_dnc)
    ffn_conv_full = _join_columns(taps_all[:, CONV_K * n_dnc:CONV_K * n_dnc + FFN_CONV * n_ffc], FFN_CONV, n_ffc)
    late = dict(
        w_out=lambda after: _exchange_wait(gather_out, after, "gather_w_out_wait")[0].reshape(N_DEV * r_out, D_MODEL),
        w_up_t=lambda after: _exchange_wait(gather_up, after, "gather_w_up_wait")[0].reshape(N_DEV * n_up, D_MODEL),
        w_down=lambda after: _exchange_wait(gather_down, after, "gather_w_down_wait")[0].reshape(N_DEV * r_down, D_MODEL))

    def send_early(blocks, after, name):
        return _exchange_start(blocks, after, False, name)

    def send_small(g, loss_lanes, after):
        small = jnp.concatenate([g[n].reshape(-1) for n in REPLICATED] + [loss_lanes[0, 0:1]])
        slab = jnp.concatenate([_split_columns(g["dn_conv_w"], n_dnc), _split_columns(g["ffn_conv_w"], n_ffc),
                                jnp.broadcast_to(small[None, :], (N_DEV, small.shape[0]))], axis=1)
        return send_early([_pad_to(slab, SUBLANES * SLAB_COLS).reshape(N_DEV, -1, SLAB_COLS)], after, "send_small")

    upd = {}

    def update_early(sent_down, sent_up_out, sent_small, after):
        r_dn, = _exchange_wait(sent_down, after, "send_dw_down_wait")
        r_up, r_o = _exchange_wait(sent_up_out, after, "send_dw_up_out_wait")
        r_small, = _exchange_wait(sent_small, after, "send_small_wait")
        upd["w_down"] = _sum_and_adamw(r_dn, w_down, m_w_down, v_w_down, "adamw_w_down")
        upd["w_up"] = [transposed(o) for o in _sum_and_adamw(r_up, transposed(w_up), transposed(m_w_up), transposed(v_w_up),
                                                             "adamw_w_up")]
        upd["w_out"] = _sum_and_adamw(r_o, w_out, m_w_out, v_w_out, "adamw_w_out")
        upd["small"] = _sum_and_adamw(r_small, _pack_small(weights), _pack_small(m_in), _pack_small(v_in), "adamw_small")

    grad_x, d_g1, sent_in = _local_step(
        x[0], loss_target[0], w_in_t, late, send_early, send_small, update_early, dn_conv_full, ffn_conv_full,
        attn_norm_g + token[0:1, 0:1], dn_a_log, dn_dt_bias, dn_out_norm_g, sg_norm_g, sg_w, sg_b, ffn_norm_g, ffn_conv_b,
        final_norm_g, n_in)

    norm_rows = D_MODEL // LANES
    r_g1, = _all_to_all([jnp.broadcast_to(d_g1.reshape(1, norm_rows, LANES), (N_DEV, norm_rows, LANES))])
    r_in, = _exchange_wait(sent_in, r_g1, "send_dw_in_wait")
    upd["w_in"] = [transposed(o) for o in _sum_and_adamw(r_in, transposed(w_in), transposed(m_w_in), transposed(v_w_in),
                                                         "adamw_w_in")]
    small_upd = upd.pop("small")
    as_rows = lambda a: a.reshape(norm_rows, LANES)
    norm_upd = _sum_and_adamw(r_g1, as_rows(attn_norm_g), as_rows(m_attn_norm_g), as_rows(v_attn_norm_g), "adamw_attn_norm")
    results = []
    for i in range(4):
        named = _unpack_small(small_upd[i], weights)
        named.update({n: upd[n][i] for n in upd})
        named["attn_norm_g"] = norm_upd[i].reshape(attn_norm_g.shape)
        results.append(named)

    loss = small_upd[0].reshape(-1)[sum(weights[n].size for n in SMALL)]
    return (loss, grad_x[None], *[r[n] for r in results for n in WEIGHT_ORDER])


def _local_step(x2d, tgt, w_in_t, late, send_early, send_small, update_early, dn_conv_full, ffn_conv_full, attn_norm_g,
                dn_a_log, dn_dt_bias, dn_out_norm_g, sg_norm_g, sg_w, sg_b, ffn_norm_g, ffn_conv_b, final_norm_g, n_in):
    t = x2d.shape[0]
    g1, g2, gf = attn_norm_g, ffn_norm_g, final_norm_g.reshape(1, D_MODEL)
    a_log4, dt_bias4 = _lanes4(dn_a_log), _lanes4(dn_dt_bias)
    sg_w3 = sg_w[0]
    sg_b_t = sg_b[0].T
    conv_b = ffn_conv_b

    p, h1, rstd1 = _rmsnorm_matmul(x2d, g1, w_in_t, "norm_in_proj", 512)
    q, k, v, beta4, g4 = _dn_prep(p, dn_conv_full, a_log4, dt_bias4)
    mix_half, *dn_saved = _dn_forward(q, k, v, beta4, g4, p, dn_out_norm_g)
    mix = _sg_forward(p, sg_norm_g, sg_w3, sg_b_t, mix_half)
    w_out_full = late["w_out"](mix)
    x2 = _matmul(mix, w_out_full, "nn", "out_proj", (1024, 1024, 1024), add=x2d)
    w_up_t = late["w_up_t"](x2)
    up, act, h2, rstd2 = _norm_up_ffn(x2, g2, w_up_t, ffn_conv_full, conv_b)
    w_down_full = late["w_down"](act)
    fn, outs = _final_loss_rows(t, D_MODEL)
    loss_lanes, dx3, dx3b, d_gf = _matmul_rows(act, w_down_full, "nn", "down_proj_loss", 512,
                                               [(x2, "rows"), (tgt, "rows"), (gf, "whole")], outs, fn)

    dact = _matmul(dx3b, w_down_full, "nt", "down_proj_dx", (512, D_FF, D_MODEL))
    d_w_down = _matmul(act, dx3b, "tn", "down_proj_dw", (256, 1024, t), out_dtype=BF16)
    sent_down, token = send_early([d_w_down.reshape(N_DEV, D_FF // N_DEV, D_MODEL)], d_w_down, "send_dw_down")
    dup, d_ffn_conv, d_ffn_conv_b = _ffn_bwd(up, ffn_conv_full, conv_b + token[0:1, 0:1], dact)
    fn, outs = _rmsnorm_bwd_rows(t, D_MODEL)
    dx2, dx2b, d_g2 = _matmul_rows(dup, w_up_t, "nn", "up_proj_dx_norm", 256,
                                   [(x2, "rows"), (rstd2, "rows"), (g2, "whole"), (dx3, "rows")], outs, fn)
    d_w_up_t = _matmul(dup, h2, "tn", "up_proj_dw", (512, 1024, t), out_dtype=BF16)
    dmix = _matmul(dx2b, w_out_full, "nt", "out_proj_dx", (1024, 1024, 1024))
    d_w_out = _matmul(mix, dx2b, "tn", "out_proj_dw", (512, 1024, t), out_dtype=BF16)
    sent_up_out, token = send_early(
        [d_w_up_t.reshape(N_DEV, 2 * D_FF // N_DEV, D_MODEL), d_w_out.reshape(N_DEV, D_MODEL // N_DEV, D_MODEL)],
        d_w_out, "send_dw_up_out")
    dp, d_sg_norm, d_sg_w, d_sg_b_t = _sg_backward(p, sg_norm_g + token[0:1, 0:1], sg_w3, sg_b_t, dmix)
    dq, dk, dv, dbeta4, dg4, dp, d_dn_norm = _dn_backward(q, k, v, beta4, g4, p, dn_out_norm_g, dn_saved, dmix, dp)
    dc_dn, d_dn_conv, dp, d_a_log4, d_dt_bias4 = _dn_prep_bwd(p, dn_conv_full, a_log4, dt_bias4, dq, dk, dv, dbeta4, dg4, dp)
    small_grads = dict(
        attn_norm_g=jnp.zeros_like(attn_norm_g), dn_conv_w=d_dn_conv, dn_a_log=d_a_log4[:, :N_HEADS],
        dn_dt_bias=d_dt_bias4[:, :N_HEADS], dn_out_norm_g=d_dn_norm, sg_norm_g=d_sg_norm, sg_w=d_sg_w,
        sg_b=d_sg_b_t[:, :SG_GROUPS].T, ffn_norm_g=d_g2, ffn_conv_w=d_ffn_conv, ffn_conv_b=d_ffn_conv_b, final_norm_g=d_gf)
    sent_small, token = send_small(small_grads, loss_lanes, d_dn_conv)
    dp = _conv_bwd_input(dc_dn, dn_conv_full + token[0:1, 0:1], "dn_conv_dx", out_cols=PROJ_PAD, into=dp)
    d_w_in_t = _matmul(dp, h1, "tn", "in_proj_dw", (PROJ_PAD // 5, 1024, t), out_dtype=BF16)
    sent_in, token = send_early([d_w_in_t[:N_DEV * n_in].reshape(N_DEV, n_in, D_MODEL)], d_w_in_t, "send_dw_in")
    update_early(sent_down, sent_up_out, sent_small, token)
    fn, outs = _rmsnorm_bwd_rows(t, D_MODEL)
    grad_x, _, d_g1 = _matmul_rows(dp, w_in_t, "nn", "in_proj_dx_norm", 512,
                                   [(x2d, "rows"), (rstd1, "rows"), (g1 + token[0:1, 0:1], "whole"), (dx2, "rows")], outs, fn)

    return grad_x, d_g1, sent_in
```

```python
import math

import jax
import jax.numpy as jnp
from jax import lax
from jax.experimental import pallas as pl
from jax.experimental.pallas import tpu as pltpu

F32 = jnp.float32
BF16 = jnp.bfloat16
HI = lax.Precision.HIGHEST

D_MODEL = 1024
DN_WIDTH = 512
HEAD_DIM = 128
N_HEADS = 4
SG_WIDTH = 512
SG_GROUPS = 4
SG_DIM = 128
SG_BLOCK = 128
D_FF = 2816
CHUNK = 64
CONV_K = 4
FFN_CONV = 3
EPS = 1e-6
PROJ_MAIN = 3072
PROJ_PAD = 3200
GELU_C = math.sqrt(2.0 / math.pi)
N_DEV = 8
LANES = 128
SUBLANES = 8
HALO = SUBLANES
VMEM_LIMIT = 48 * 1024 * 1024

ADAM_LR = 0.001
ADAM_B1 = 0.9
ADAM_B2 = 0.999
ADAM_EPS = 1e-08
ADAM_WD = 0.01
ADAM_STEP = 10

MESH_ID = pl.DeviceIdType.MESH


def _pcall(body, **kw):
    return pl.pallas_call(body, **kw)


def _params(*sem):
    return pltpu.CompilerParams(dimension_semantics=sem, vmem_limit_bytes=VMEM_LIMIT)


def _pick(n, cap):
    best = None
    for t in range(LANES, cap + 1, LANES):
        if n % t == 0:
            best = t
    return best if best else n


FAST, EXACT = "bf16 operands, one pass", "f32 operands, six bf16 passes"


def dot_f32(a, b, dims, tier):
    if tier == FAST:
        return lax.dot_general(a.astype(BF16), b.astype(BF16), dims, preferred_element_type=F32)
    return lax.dot_general(a, b, dims, precision=HI, preferred_element_type=F32)


def dot_nn(a, b, tier=EXACT):
    return dot_f32(a, b, (((1,), (0,)), ((), ())), tier)


def dot_nt(a, b, tier=EXACT):
    return dot_f32(a, b, (((1,), (1,)), ((), ())), tier)


def dot_tn(a, b, tier=EXACT):
    return dot_f32(a, b, (((0,), (0,)), ((), ())), tier)


def sigmoid(x):
    return 0.5 * jnp.tanh(0.5 * x) + 0.5


def silu(x):
    return x * sigmoid(x)


def silu_grad(x):
    s = sigmoid(x)
    return s * (1.0 + x * (1.0 - s))


def gelu(x):
    return 0.5 * x * (1.0 + jnp.tanh(GELU_C * (x + 0.044715 * x * x * x)))


def gelu_grad(x):
    t = jnp.tanh(GELU_C * (x + 0.044715 * x * x * x))
    return 0.5 * (1.0 + t) + 0.5 * x * (1.0 - t * t) * GELU_C * (1.0 + 3.0 * 0.044715 * x * x)


def softplus(z):
    return jnp.maximum(z, 0.0) + jnp.log(1.0 + jnp.exp(-jnp.abs(z)))


def rms_fwd(x, g):
    r = lax.rsqrt(jnp.mean(x * x, axis=-1, keepdims=True) + EPS)
    return x * r * g, r


def rms_bwd(x, r, g, dy):
    dyg = dy * g
    xr = x * r
    dx = r * (dyg - xr * jnp.mean(dyg * xr, axis=-1, keepdims=True))
    return dx, dy * xr


def l2_fwd(x):
    r = lax.rsqrt(jnp.sum(x * x, axis=-1, keepdims=True) + EPS)
    return x * r, r


def l2_bwd(x, r, dy):
    xr = x * r
    return r * (dy - xr * jnp.sum(dy * xr, axis=-1, keepdims=True))


def _tri_masks(n):
    row = lax.broadcasted_iota(jnp.int32, (n, n), 0)
    col = lax.broadcasted_iota(jnp.int32, (n, n), 1)
    return row >= col, row > col


def chunk_cumsum(g4):
    incl, _ = _tri_masks(g4.shape[0])
    return dot_nn(incl.astype(F32), g4)


STACK = N_HEADS * CHUNK
DN_FWD_CHUNKS = 8
DN_CHUNKS = 4


def _head_rows(h):
    return slice(h * CHUNK, (h + 1) * CHUNK)


def _stack_heads(x):
    return jnp.concatenate([x[:, h * HEAD_DIM:(h + 1) * HEAD_DIM] for h in range(N_HEADS)], axis=0)


def _stack_lanes(x4):
    return jnp.concatenate([x4[:, h:h + 1] for h in range(N_HEADS)], axis=0)


def _per_head(fn):
    return jnp.concatenate([fn(h) for h in range(N_HEADS)], axis=0)


def _unit_lower_inverses(l_strict, order):
    c = l_strict[0].shape[0]
    row = lax.broadcasted_iota(jnp.int32, (c, c), 0)
    col = lax.broadcasted_iota(jnp.int32, (c, c), 1)
    eye = (row == col).astype(F32)
    p = [-l for l in l_strict]
    a = [eye + n for n in p]
    for _ in range(int(math.log2(order)) - 1):
        p = [dot_nn(x, x, FAST) for x in p]
        a = [x + dot_nn(x, y, FAST) for x, y in zip(a, p)]
    return a


def dn_chunks_local(chunks, inverses=None):
    row = lax.broadcasted_iota(jnp.int32, (STACK, STACK), 0)
    col = lax.broadcasted_iota(jnp.int32, (STACK, STACK), 1)
    same = (row // CHUNK) == (col // CHUNK)
    incl = jnp.logical_and(same, row >= col)
    strict = jnp.logical_and(same, row > col)
    locs = []
    for q, k, v, beta, gc4 in chunks:
        gc_col = _stack_lanes(gc4)
        gc_row = jnp.sum(jnp.where(row == col, gc_col, 0.0), axis=0, keepdims=True)
        decay = jnp.where(incl, jnp.exp(jnp.minimum(gc_col - gc_row, 0.0)), 0.0)
        gamma = jnp.exp(gc_col)
        gc_last = jnp.concatenate([jnp.broadcast_to(gc4[CHUNK - 1:CHUNK, h:h + 1], (CHUNK, 1)) for h in range(N_HEADS)], axis=0)
        tau = jnp.exp(gc_last - gc_col)
        kb = k * beta
        locs.append(dict(decay=decay, gamma=gamma, tau=tau, cd=jnp.exp(gc_last), kb=kb, qd=q * gamma, kt=k * tau,
                         incl=incl, strict=strict))
    for loc, (q, k, v, beta, gc4) in zip(locs, chunks):
        loc["l_mat"] = jnp.where(strict, dot_nt(loc["kb"], k, FAST) * loc["decay"], 0.0)
    if inverses is None:
        inverses = _unit_lower_inverses([loc["l_mat"] for loc in locs], CHUNK)
    for loc, a_inv in zip(locs, inverses):
        loc["a_inv"] = a_inv
    for loc, (q, k, v, beta, gc4) in zip(locs, chunks):
        sol = dot_nn(loc["a_inv"], jnp.concatenate([v * beta, loc["kb"] * loc["gamma"]], axis=1), FAST)
        loc.update(sol=sol, value=sol[:, :HEAD_DIM], kcd=sol[:, HEAD_DIM:])
        loc["attn"] = jnp.where(incl, dot_nt(q, k, FAST) * loc["decay"], 0.0)
    return locs


def dn_chunk_state(loc, s):
    kcd, qd, kt, cd = loc["kcd"], loc["qd"], loc["kt"], loc["cd"]
    v_new = loc["value"] - _per_head(lambda h: dot_nn(kcd[_head_rows(h)], s[h], FAST))
    o = _per_head(lambda h: dot_nn(qd[_head_rows(h)], s[h], FAST)) + dot_nn(loc["attn"], v_new, FAST)
    s_new = [s[h] * cd[h * CHUNK:h * CHUNK + 1, :] + dot_tn(kt[_head_rows(h)], v_new[_head_rows(h)], FAST)
             for h in range(N_HEADS)]
    loc["v_new"] = v_new
    return o, s_new


def dn_chunks_bwd(items, ds_last):
    hr = _head_rows
    n = len(items)
    pre = []
    for q, k, v, beta, loc, s, do in items:
        pre.append(dict(
            dv_part=dot_tn(loc["attn"], do, FAST),
            dattn=jnp.where(loc["incl"], dot_nt(do, loc["v_new"], FAST), 0.0),
            dqd=_per_head(lambda h: dot_nt(do[hr(h)], s[h], FAST)),
            ds_part=[dot_tn(loc["qd"][hr(h)], do[hr(h)], FAST) for h in range(N_HEADS)]))
    ds_new_of, dv_new_of = [None] * n, [None] * n
    ds = ds_last
    for c in reversed(range(n)):
        loc = items[c][4]
        ds_new_of[c] = ds
        dv_new = pre[c]["dv_part"] + _per_head(lambda h: dot_nn(loc["kt"][hr(h)], ds[h], FAST))
        dv_new_of[c] = dv_new
        ds = [pre[c]["ds_part"][h] + ds[h] * loc["cd"][h * CHUNK:h * CHUNK + 1, :]
              - dot_tn(loc["kcd"][hr(h)], dv_new[hr(h)], FAST) for h in range(N_HEADS)]
    is_last = (lax.broadcasted_iota(jnp.int32, (STACK, 1), 0) % CHUNK) == CHUNK - 1
    out = []
    for c, (q, k, v, beta, loc, s, do) in enumerate(items):
        decay, gamma, tau, cd, kb = loc["decay"], loc["gamma"], loc["tau"], loc["cd"], loc["kb"]
        dv_new, ds_new, dattn, dqd = dv_new_of[c], ds_new_of[c], pre[c]["dattn"], pre[c]["dqd"]
        dkt = _per_head(lambda h: dot_nt(loc["v_new"][hr(h)], ds_new[h], FAST))
        dkcd = -_per_head(lambda h: dot_nt(dv_new[hr(h)], s[h], FAST))
        drhs = dot_tn(loc["a_inv"], jnp.concatenate([dv_new, dkcd], axis=1), FAST)
        dvb, dkbg = drhs[:, :HEAD_DIM], drhs[:, HEAD_DIM:]
        dl = jnp.where(loc["strict"], -dot_nt(drhs, loc["sol"], FAST), 0.0)
        dkk = dl * decay
        dqk = dattn * decay
        e = dl * loc["l_mat"] + dattn * loc["attn"]
        dgc = jnp.sum(e, axis=1, keepdims=True) - jnp.sum(e, axis=0, keepdims=True).T
        dkb = dot_nn(dkk, k, FAST) + dkbg * gamma
        dk = dot_tn(dkk, kb, FAST) + dot_tn(dqk, q, FAST) + dkt * tau
        dq = dot_nn(dqk, k, FAST) + dqd * gamma
        dgamma = jnp.sum(dkbg * kb, axis=1, keepdims=True) + jnp.sum(dqd * q, axis=1, keepdims=True)
        dtau_tau = jnp.sum(dkt * k, axis=1, keepdims=True) * tau
        dgc = dgc + dgamma * gamma - dtau_tau

        def last_term(h):
            dcd = jnp.sum(jnp.sum(ds_new[h] * s[h], axis=1, keepdims=True), axis=0, keepdims=True)
            total = jnp.sum(dtau_tau[hr(h)], axis=0, keepdims=True) + dcd * cd[h * CHUNK:h * CHUNK + 1, :]
            return jnp.broadcast_to(total, (CHUNK, 1))

        dgc = dgc + jnp.where(is_last, _per_head(last_term), 0.0)
        dk = dk + dkb * beta
        dbeta = jnp.sum(dkb * k, axis=1, keepdims=True) + jnp.sum(dvb * v, axis=1, keepdims=True)
        out.append((dq, dk, dvb * beta, dbeta, dgc))
    return out, ds


def _token_tile(t):
    return _pick(t, 256)


STRIP = 32


def _for_strips(n_rows, rows, fn, start=0):
    def step(r, carry):
        fn(pl.multiple_of(r * rows, rows))
        return carry

    lax.fori_loop(start, n_rows // rows, step, 0)


def _fold_rows(x):
    out = x[0:SUBLANES, :]
    for i in range(1, x.shape[0] // SUBLANES):
        out = out + x[i * SUBLANES:(i + 1) * SUBLANES, :]
    return out


def _matmul(a, b, mode, name, tiles, add=None, out_dtype=F32):
    if mode == "nn":
        (m, k), n = a.shape, b.shape[1]
    elif mode == "nt":
        (m, k), n = a.shape, b.shape[0]
    else:
        (k, m), n = a.shape, b.shape[1]
    tm, tn, tk = min(tiles[0], m), min(tiles[1], n), min(tiles[2], k)
    assert m % tm == 0 and n % tn == 0 and k % tk == 0, (name, m, n, k, tiles)
    nk = k // tk
    dims = {"nn": (((1,), (0,)), ((), ())), "nt": (((1,), (1,)), ((), ())), "tn": (((0,), (0,)), ((), ()))}[mode]

    def finish(res, add_ref, o_ref):
        if add_ref is not None:
            res = res + add_ref[...]
        o_ref[...] = res.astype(o_ref.dtype)

    def body(*refs):
        a_ref, b_ref = refs[0], refs[1]
        add_ref = refs[2] if add is not None else None
        o_ref = refs[3] if add is not None else refs[2]
        part = lax.dot_general(a_ref[...], b_ref[...], dims, preferred_element_type=F32)
        if nk == 1:
            finish(part, add_ref, o_ref)
            return
        acc_ref = refs[-1]
        kk = pl.program_id(2)

        @pl.when(kk == 0)
        def _():
            acc_ref[...] = part

        @pl.when(kk > 0)
        def _():
            acc_ref[...] += part

        @pl.when(kk == nk - 1)
        def _():
            finish(acc_ref[...], add_ref, o_ref)

    a_spec = pl.BlockSpec((tk, tm), lambda j, i, kk: (kk, i)) if mode == "tn" else pl.BlockSpec((tm, tk), lambda j, i, kk: (i, kk))
    b_spec = pl.BlockSpec((tn, tk), lambda j, i, kk: (j, kk)) if mode == "nt" else pl.BlockSpec((tk, tn), lambda j, i, kk: (kk, j))
    o_spec = pl.BlockSpec((tm, tn), lambda j, i, kk: (i, j))
    in_specs = [a_spec, b_spec] + ([o_spec] if add is not None else [])
    args = (a, b) + ((add,) if add is not None else ())
    return _pcall(
        body, grid=(n // tn, m // tm, nk), in_specs=in_specs, out_specs=o_spec,
        out_shape=jax.ShapeDtypeStruct((m, n), out_dtype),
        scratch_shapes=[pltpu.VMEM((tm, tn), F32)] if nk > 1 else [],
        compiler_params=_params("parallel", "parallel", "arbitrary"), name=name)(*args)


def _matmul_rows(a, b, mode, name, tm, extra, outs, fn):
    m, k = a.shape
    n = b.shape[1] if mode == "nn" else b.shape[0]
    tm = min(tm, m)
    dims = (((1,), (0,)), ((), ())) if mode == "nn" else (((1,), (1,)), ((), ()))

    def spec(shape, kind):
        if kind == "rows":
            return pl.BlockSpec((tm, shape[1]), lambda i: (i, 0))
        return pl.BlockSpec(shape, lambda i: (0,) * len(shape))

    def body(a_ref, b_ref, *refs):
        rows = lax.dot_general(a_ref[...], b_ref[...], dims, preferred_element_type=F32)
        fn(rows, pl.program_id(0) == 0, *refs)

    return _pcall(
        body, grid=(m // tm,),
        in_specs=[pl.BlockSpec((tm, k), lambda i: (i, 0)), pl.BlockSpec(b.shape, lambda i: (0, 0))]
        + [spec(x.shape, kind) for x, kind in extra],
        out_specs=[spec(shape, kind) for shape, _, kind in outs],
        out_shape=[jax.ShapeDtypeStruct(shape, dtype) for shape, dtype, _ in outs],
        compiler_params=_params("arbitrary"), name=name)(a, b, *[x for x, _ in extra])


def _rmsnorm_matmul(x, g, b_t, name, tm):
    t, d = x.shape
    n = b_t.shape[0]
    tm = min(tm, t)

    def body(x_ref, g_ref, b_ref, o_ref, h_ref, r_ref):
        y, r = rms_fwd(x_ref[...], g_ref[...])
        h = y.astype(BF16)
        h_ref[...] = h
        r_ref[...] = r
        o_ref[...] = lax.dot_general(h, b_ref[...], (((1,), (1,)), ((), ())), preferred_element_type=F32)

    rows = lambda w: pl.BlockSpec((tm, w), lambda i: (i, 0))
    return _pcall(
        body, grid=(t // tm,),
        in_specs=[rows(d), pl.BlockSpec((1, d), lambda i: (0, 0)), pl.BlockSpec((n, d), lambda i: (0, 0))],
        out_specs=[rows(n), rows(d), rows(1)],
        out_shape=[jax.ShapeDtypeStruct((t, n), F32), jax.ShapeDtypeStruct((t, d), BF16), jax.ShapeDtypeStruct((t, 1), F32)],
        compiler_params=_params("parallel"), name=name)(x, g, b_t)


def _rmsnorm_bwd_rows(t, d):
    def fn(dh, first, x_ref, r_ref, g_ref, dres_ref, dx_ref, dxb_ref, dg_ref):
        dx, dg_rows = rms_bwd(x_ref[...], r_ref[...], g_ref[...], dh)
        dx = dx + dres_ref[...]
        dx_ref[...] = dx
        dxb_ref[...] = dx.astype(BF16)

        @pl.when(first)
        def _():
            dg_ref[...] = jnp.zeros_like(dg_ref)

        dg_ref[...] += jnp.sum(dg_rows, axis=0, keepdims=True)

    return fn, [((t, d), F32, "rows"), ((t, d), BF16, "rows"), ((1, d), F32, "whole")]


def _final_loss_rows(t, d):
    def fn(rows, first, res_ref, t_ref, g_ref, loss_ref, dx_ref, dxb_ref, dg_ref):
        @pl.when(first)
        def _():
            loss_ref[...] = jnp.zeros_like(loss_ref)
            dg_ref[...] = jnp.zeros_like(dg_ref)

        x = rows + res_ref[...]
        y, r = rms_fwd(x, g_ref[...])
        err = y - t_ref[...]
        loss_ref[...] += 0.5 * jnp.sum(jnp.mean(err * err, axis=-1, keepdims=True), axis=0, keepdims=True)
        dx, dg_rows = rms_bwd(x, r, g_ref[...], err * (1.0 / d))
        dx_ref[...] = dx
        dxb_ref[...] = dx.astype(BF16)
        dg_ref[...] += jnp.sum(dg_rows, axis=0, keepdims=True)

    return fn, [((1, LANES), F32, "whole"), ((t, d), F32, "rows"), ((t, d), BF16, "rows"), ((1, d), F32, "whole")]


def _prev_halo_spec(tm, width, col_block):
    return pl.BlockSpec((HALO, width), lambda i: (jnp.maximum(i * (tm // HALO) - 1, 0), col_block))


def _history(tile_ref, halo_ref, first, row0, cols):
    if isinstance(row0, int) and row0 == 0:
        return jnp.concatenate([jnp.where(first, 0.0, halo_ref[:, cols]), tile_ref[0:STRIP, cols]], axis=0)
    return tile_ref[pl.ds(pl.multiple_of(row0 - HALO, HALO), STRIP + HALO), cols]


def _first_then_strips(n_rows, fn):
    fn(0)
    _for_strips(n_rows, STRIP, fn, start=1)


def _delays(ext, taps):
    return [ext[HALO:, :]] + [pltpu.roll(ext, j, 0)[HALO:, :] for j in range(1, taps)]


def _causal_conv(delayed, w):
    taps = len(delayed)
    out = delayed[0] * w[taps - 1:taps, :]
    for j in range(1, taps):
        out = out + delayed[j] * w[taps - 1 - j:taps - j, :]
    return out


def _advanced_conv(buf_ref, row0, cols, w):
    return _advanced(buf_ref[pl.ds(row0, STRIP + HALO), cols], w)


def _advanced(ext, w):
    taps = w.shape[0]
    out = ext[:STRIP, :] * w[taps - 1:taps, :]
    for j in range(1, taps):
        out = out + pltpu.roll(ext, STRIP + HALO - j, 0)[:STRIP, :] * w[taps - 1 - j:taps - j, :]
    return out


def _dn_prep(p, conv_w, a_log4, dt_bias4):
    t = p.shape[0]
    tm = _token_tile(t)
    w3 = 3 * DN_WIDTH

    def body(x_ref, halo_ref, pbd_ref, w_ref, alog_ref, dtb_ref, q_ref, k_ref, v_ref, beta_ref, g_ref):
        first = pl.program_id(0) == 0

        def strip(row0):
            rows = pl.ds(row0, STRIP)
            for h in range(N_HEADS):
                sl = slice(h * HEAD_DIM, (h + 1) * HEAD_DIM)
                for part, out_ref in ((0, q_ref), (1, k_ref), (2, v_ref)):
                    cols = slice(part * DN_WIDTH + h * HEAD_DIM, part * DN_WIDTH + (h + 1) * HEAD_DIM)
                    y = silu(_causal_conv(_delays(_history(x_ref, halo_ref, first, row0, cols), CONV_K), w_ref[:, cols]))
                    if part == 0:
                        y = l2_fwd(y)[0] * (HEAD_DIM ** -0.5)
                    elif part == 1:
                        y = l2_fwd(y)[0]
                    out_ref[rows, sl] = y
            head = lax.broadcasted_iota(jnp.int32, (STRIP, LANES), 1) < N_HEADS
            pbd = pbd_ref[rows, :]
            beta_ref[rows, :] = jnp.where(head, sigmoid(pbd), 0.0)
            a_raw = pltpu.roll(pbd, LANES - N_HEADS, 1)
            g_ref[rows, :] = jnp.where(head, -jnp.exp(alog_ref[...]) * softplus(a_raw + dtb_ref[...]), 0.0)

        _first_then_strips(tm, strip)

    tok = lambda w, cb: pl.BlockSpec((tm, w), lambda i: (i, cb))
    full = lambda a: pl.BlockSpec(a.shape, lambda i: (0, 0))
    return _pcall(
        body, grid=(t // tm,),
        in_specs=[tok(w3, 0), _prev_halo_spec(tm, w3, 0), tok(LANES, PROJ_MAIN // LANES),
                  full(conv_w), full(a_log4), full(dt_bias4)],
        out_specs=[tok(DN_WIDTH, 0)] * 3 + [tok(LANES, 0)] * 2,
        out_shape=[jax.ShapeDtypeStruct((t, DN_WIDTH), F32)] * 3 + [jax.ShapeDtypeStruct((t, LANES), F32)] * 2,
        compiler_params=_params("parallel"), name="dn_prep")(p, p, p, conv_w, a_log4, dt_bias4)


def _dn_prep_bwd(p, conv_w, a_log4, dt_bias4, dq, dk, dv, dbeta4, dg4, dp_buf):
    t = p.shape[0]
    tm = _token_tile(t)
    w3 = 3 * DN_WIDTH

    def body(x_ref, halo_ref, pbd_ref, w_ref, alog_ref, dtb_ref, dq_ref, dk_ref, dv_ref, dbeta_ref, dg_ref, _,
             dc_ref, dw_ref, dpbd_ref, dalog_ref, ddtb_ref, dw_acc, lane_acc):
        first = pl.program_id(0) == 0
        dw_acc[...] = jnp.zeros_like(dw_acc)
        lane_acc[...] = jnp.zeros_like(lane_acc)

        def strip(row0):
            rows = pl.ds(row0, STRIP)
            for h in range(N_HEADS):
                sl = slice(h * HEAD_DIM, (h + 1) * HEAD_DIM)
                for part, dy_ref in ((0, dq_ref), (1, dk_ref), (2, dv_ref)):
                    cols = slice(part * DN_WIDTH + h * HEAD_DIM, part * DN_WIDTH + (h + 1) * HEAD_DIM)
                    delayed = _delays(_history(x_ref, halo_ref, first, row0, cols), CONV_K)
                    c = _causal_conv(delayed, w_ref[:, cols])
                    dy = dy_ref[rows, sl]
                    if part < 2:
                        y = silu(c)
                        _, r = l2_fwd(y)
                        dy = l2_bwd(y, r, dy * (HEAD_DIM ** -0.5) if part == 0 else dy)
                    dc = dy * silu_grad(c)
                    dc_ref[rows, cols] = dc
                    for j in range(CONV_K):
                        k = CONV_K - 1 - j
                        dw_acc[k * SUBLANES:(k + 1) * SUBLANES, cols] += _fold_rows(dc * delayed[j])
            head = lax.broadcasted_iota(jnp.int32, (STRIP, LANES), 1) < N_HEADS
            pbd = pbd_ref[rows, :]
            beta = sigmoid(pbd)
            dpb = jnp.where(head, dbeta_ref[rows, :] * beta * (1.0 - beta), 0.0)
            z = pltpu.roll(pbd, LANES - N_HEADS, 1) + dtb_ref[...]
            neg_rate = -jnp.exp(alog_ref[...])
            dg = dg_ref[rows, :]
            dpa = jnp.where(head, dg * neg_rate * sigmoid(z), 0.0)
            dpbd_ref[rows, :] = (dpb + pltpu.roll(dpa, N_HEADS, 1)).astype(BF16)
            g = jnp.where(head, neg_rate * softplus(z), 0.0)
            lane_acc[0:SUBLANES, :] += _fold_rows(dg * g)
            lane_acc[SUBLANES:, :] += _fold_rows(dpa)

        _first_then_strips(tm, strip)

        @pl.when(first)
        def _():
            dw_ref[...] = jnp.zeros_like(dw_ref)
            dalog_ref[...] = jnp.zeros_like(dalog_ref)
            ddtb_ref[...] = jnp.zeros_like(ddtb_ref)

        for k in range(CONV_K):
            dw_ref[k:k + 1, :] += jnp.sum(dw_acc[k * SUBLANES:(k + 1) * SUBLANES, :], axis=0, keepdims=True)
        dalog_ref[...] += jnp.sum(lane_acc[0:SUBLANES, :], axis=0, keepdims=True)
        ddtb_ref[...] += jnp.sum(lane_acc[SUBLANES:, :], axis=0, keepdims=True)

    tok = lambda w, cb: pl.BlockSpec((tm, w), lambda i: (i, cb))
    full = lambda shape: pl.BlockSpec(shape, lambda i: (0, 0))
    return _pcall(
        body, grid=(t // tm,),
        in_specs=[tok(w3, 0), _prev_halo_spec(tm, w3, 0), tok(LANES, PROJ_MAIN // LANES),
                  full(conv_w.shape), full(a_log4.shape), full(dt_bias4.shape)] + [tok(DN_WIDTH, 0)] * 3 + [tok(LANES, 0)] * 2
        + [pl.BlockSpec(memory_space=pl.ANY)],
        out_specs=[tok(w3, 0), full((CONV_K, w3)), tok(LANES, PROJ_MAIN // LANES), full((1, LANES)), full((1, LANES))],
        out_shape=[jax.ShapeDtypeStruct((t, w3), F32), jax.ShapeDtypeStruct((CONV_K, w3), F32),
                   jax.ShapeDtypeStruct(dp_buf.shape, dp_buf.dtype),
                   jax.ShapeDtypeStruct((1, LANES), F32), jax.ShapeDtypeStruct((1, LANES), F32)],
        input_output_aliases={11: 2},
        scratch_shapes=[pltpu.VMEM((CONV_K * SUBLANES, w3), F32), pltpu.VMEM((2 * SUBLANES, LANES), F32)],
        compiler_params=_params("arbitrary"), name="dn_prep_bwd")(p, p, p, conv_w, a_log4, dt_bias4, dq, dk, dv, dbeta4, dg4, dp_buf)


def _conv_bwd_input(dc, w, name, out_cols=None, col_block=0, into=None):
    t, c = dc.shape
    taps = w.shape[0]
    tm = _token_tile(t)
    ct = _pick(c, 1536)
    n_tok = t // tm
    out_cols = c if out_cols is None else out_cols

    def body(dc_ref, next_ref, w_ref, *rest):
        dx_ref = rest[-1]
        last = pl.program_id(0) == n_tok - 1

        def strip(row0):
            for c0 in range(0, ct, LANES):
                cols = slice(c0, c0 + LANES)
                dx_ref[pl.ds(row0, STRIP), cols] = _advanced_conv(dc_ref, row0, cols, w_ref[:, cols]).astype(BF16)

        _for_strips(tm - STRIP, STRIP, strip)
        for c0 in range(0, ct, LANES):
            cols = slice(c0, c0 + LANES)
            ext = jnp.concatenate([dc_ref[tm - STRIP:tm, cols], jnp.where(last, 0.0, next_ref[:, cols])], axis=0)
            dx_ref[tm - STRIP:tm, cols] = _advanced(ext, w_ref[:, cols]).astype(BF16)

    in_specs = [pl.BlockSpec((tm, ct), lambda i, j: (i, j)),
                pl.BlockSpec((HALO, ct), lambda i, j: (jnp.minimum((i + 1) * (tm // HALO), t // HALO - 1), j)),
                pl.BlockSpec((taps, ct), lambda i, j: (0, j))]
    args = (dc, dc, w)
    aliases = {}
    if into is not None:
        in_specs.append(pl.BlockSpec(memory_space=pl.ANY))
        args += (into,)
        aliases = {3: 0}
    return _pcall(
        body, grid=(n_tok, c // ct), in_specs=in_specs,
        out_specs=pl.BlockSpec((tm, ct), lambda i, j: (i, j + col_block)),
        out_shape=jax.ShapeDtypeStruct((t, out_cols), BF16), input_output_aliases=aliases,
        compiler_params=_params("parallel", "parallel"), name=name)(*args)


def _dn_forward(q, k, v, beta4, g4, p, norm_g):
    t = q.shape[0]
    n = t // CHUNK
    nc = DN_FWD_CHUNKS
    rows_per_step = nc * CHUNK

    def body(q_ref, k_ref, v_ref, b_ref, g_ref, gate_ref, ng_ref, mix_ref, s_all_ref, ainv_ref, s_ref):
        @pl.when(pl.program_id(0) == 0)
        def _():
            s_ref[...] = jnp.zeros_like(s_ref)

        chunks = []
        for c in range(nc):
            rows = slice(c * CHUNK, (c + 1) * CHUNK)
            chunks.append((_stack_heads(q_ref[rows, :]), _stack_heads(k_ref[rows, :]), _stack_heads(v_ref[rows, :]),
                           _stack_lanes(b_ref[rows, :]), chunk_cumsum(g_ref[rows, :])))
        locs = dn_chunks_local(chunks)
        s = [s_ref[h] for h in range(N_HEADS)]
        for c in range(nc):
            rows = slice(c * CHUNK, (c + 1) * CHUNK)
            ainv_ref[c] = locs[c]["a_inv"].astype(BF16)
            for h in range(N_HEADS):
                s_all_ref[c, h] = s[h]
            o, s = dn_chunk_state(locs[c], s)
            o_n, _ = rms_fwd(o, ng_ref[...])
            for h in range(N_HEADS):
                sl = slice(h * HEAD_DIM, (h + 1) * HEAD_DIM)
                mix_ref[rows, sl] = (o_n[_head_rows(h)] * silu(gate_ref[rows, sl])).astype(BF16)
        for h in range(N_HEADS):
            s_ref[h] = s[h]

    ch = lambda w, cb: pl.BlockSpec((rows_per_step, w), lambda i: (i, cb))
    per_chunk = lambda *shape: pl.BlockSpec((nc,) + shape, lambda i: (i,) + (0,) * len(shape))
    return _pcall(
        body, grid=(n // nc,),
        in_specs=[ch(DN_WIDTH, 0)] * 3 + [ch(LANES, 0)] * 2 + [ch(DN_WIDTH, 3), pl.BlockSpec((1, HEAD_DIM), lambda i: (0, 0))],
        out_specs=[ch(DN_WIDTH, 0), per_chunk(N_HEADS, HEAD_DIM, HEAD_DIM), per_chunk(STACK, STACK)],
        out_shape=[jax.ShapeDtypeStruct((t, DN_WIDTH + SG_WIDTH), BF16), jax.ShapeDtypeStruct((n, N_HEADS, HEAD_DIM, HEAD_DIM), F32),
                   jax.ShapeDtypeStruct((n, STACK, STACK), BF16)],
        scratch_shapes=[pltpu.VMEM((N_HEADS, HEAD_DIM, HEAD_DIM), F32)],
        compiler_params=_params("arbitrary"), name="dn_forward")(q, k, v, beta4, g4, p, norm_g)


def _dn_backward(q, k, v, beta4, g4, p, norm_g, saved, dmix, dp_buf):
    t = q.shape[0]
    n = t // CHUNK
    steps = n // DN_CHUNKS
    rows_per_step = DN_CHUNKS * CHUNK

    def body(q_ref, k_ref, v_ref, b_ref, g_ref, gate_ref, ng_ref, s_in_ref, ainv_ref, dmix_ref, _,
             dq_ref, dk_ref, dv_ref, db_ref, dg_ref, dgate_ref, dng_ref, ds_ref):
        @pl.when(pl.program_id(0) == 0)
        def _():
            ds_ref[...] = jnp.zeros_like(ds_ref)
            dng_ref[...] = jnp.zeros_like(dng_ref)

        chunks = []
        for c in range(DN_CHUNKS):
            rows = slice(c * CHUNK, (c + 1) * CHUNK)
            chunks.append((_stack_heads(q_ref[rows, :]), _stack_heads(k_ref[rows, :]), _stack_heads(v_ref[rows, :]),
                           _stack_lanes(b_ref[rows, :]), chunk_cumsum(g_ref[rows, :])))
        items = []
        for c, loc in enumerate(dn_chunks_local(chunks, [ainv_ref[c] for c in range(DN_CHUNKS)])):
            rows = slice(c * CHUNK, (c + 1) * CHUNK)
            s = [s_in_ref[c, h] for h in range(N_HEADS)]
            o, _ = dn_chunk_state(loc, s)
            o_n, r = rms_fwd(o, ng_ref[...])
            gate = _stack_heads(gate_ref[rows, :])
            dmx = _stack_heads(dmix_ref[rows, :])
            dgate = dmx * o_n * silu_grad(gate)
            do, dng_rows = rms_bwd(o, r, ng_ref[...], dmx * silu(gate))
            dng_ref[...] += jnp.sum(dng_rows, axis=0, keepdims=True)
            for h in range(N_HEADS):
                dgate_ref[rows, h * HEAD_DIM:(h + 1) * HEAD_DIM] = dgate[_head_rows(h)].astype(BF16)
            items.append((*chunks[c][:4], loc, s, do))
        grads, ds = dn_chunks_bwd(items, [ds_ref[h] for h in range(N_HEADS)])
        lane = lax.broadcasted_iota(jnp.int32, (CHUNK, LANES), 1)
        _, strict = _tri_masks(CHUNK)
        for c in range(DN_CHUNKS):
            rows = slice(c * CHUNK, (c + 1) * CHUNK)
            dq, dk, dv, dbeta, dgc = grads[c]
            db4 = jnp.zeros((CHUNK, LANES), F32)
            dgc4 = jnp.zeros((CHUNK, LANES), F32)
            for h in range(N_HEADS):
                sl = slice(h * HEAD_DIM, (h + 1) * HEAD_DIM)
                head_rows = _head_rows(h)
                dq_ref[rows, sl] = dq[head_rows]
                dk_ref[rows, sl] = dk[head_rows]
                dv_ref[rows, sl] = dv[head_rows]
                db4 = jnp.where(lane == h, dbeta[head_rows], db4)
                dgc4 = jnp.where(lane == h, dgc[head_rows], dgc4)
            db_ref[rows, :] = db4
            dg_ref[rows, :] = dot_nn(jnp.logical_not(strict).astype(F32), dgc4)
        for h in range(N_HEADS):
            ds_ref[h] = ds[h]

    rev = lambda w, cb: pl.BlockSpec((rows_per_step, w), lambda i: (steps - 1 - i, cb))
    per_chunk = lambda a: pl.BlockSpec((DN_CHUNKS,) + a.shape[1:], lambda i: (steps - 1 - i,) + (0,) * (a.ndim - 1))
    return _pcall(
        body, grid=(steps,),
        in_specs=[rev(DN_WIDTH, 0)] * 3 + [rev(LANES, 0)] * 2 + [rev(DN_WIDTH, 3), pl.BlockSpec((1, HEAD_DIM), lambda i: (0, 0))]
        + [per_chunk(a) for a in saved] + [rev(DN_WIDTH, 0), pl.BlockSpec(memory_space=pl.ANY)],
        out_specs=[rev(DN_WIDTH, 0)] * 3 + [rev(LANES, 0)] * 2 + [rev(DN_WIDTH, 3), pl.BlockSpec((1, HEAD_DIM), lambda i: (0, 0))],
        out_shape=[jax.ShapeDtypeStruct((t, DN_WIDTH), F32)] * 3 + [jax.ShapeDtypeStruct((t, LANES), F32)] * 2
        + [jax.ShapeDtypeStruct(dp_buf.shape, dp_buf.dtype), jax.ShapeDtypeStruct((1, HEAD_DIM), F32)],
        input_output_aliases={10: 5},
        scratch_shapes=[pltpu.VMEM((N_HEADS, HEAD_DIM, HEAD_DIM), F32)],
        compiler_params=_params("arbitrary"), name="dn_backward")(q, k, v, beta4, g4, p, norm_g, *saved, dmix, dp_buf)


SG_STEP_BLOCKS = 4


def _sg_mask():
    row = lax.broadcasted_iota(jnp.int32, (SG_BLOCK, SG_BLOCK), 0)
    col = lax.broadcasted_iota(jnp.int32, (SG_BLOCK, SG_BLOCK), 1)
    return (col // CHUNK) <= (row // CHUNK)


def _sg_forward(p, norm_g, w_s, b_t, mix_buf):
    t = p.shape[0]
    step_rows = SG_STEP_BLOCKS * SG_BLOCK

    def body(u_ref, v_ref, ng_ref, w_ref, b_ref, _, o_ref):
        mask = _sg_mask()
        pairs = [(slice(b * SG_BLOCK, (b + 1) * SG_BLOCK), g, slice(g * SG_DIM, (g + 1) * SG_DIM))
                 for b in range(SG_STEP_BLOCKS) for g in range(SG_GROUPS)]
        w_m = [jnp.where(mask, w_ref[g], 0.0) for g in range(SG_GROUPS)]
        vn = [rms_fwd(gelu(v_ref[rows, sl]), ng_ref[:, sl])[0] for rows, g, sl in pairs]
        s = [dot_nn(w_m[g], vn[i], FAST) + b_ref[:, g:g + 1] for i, (rows, g, sl) in enumerate(pairs)]
        for i, (rows, g, sl) in enumerate(pairs):
            o_ref[rows, sl] = (gelu(u_ref[rows, sl]) * s[i]).astype(BF16)

    blk = lambda cb: pl.BlockSpec((step_rows, SG_WIDTH), lambda i: (i, cb))
    return _pcall(
        body, grid=(t // step_rows,),
        in_specs=[blk(4), blk(5), pl.BlockSpec((1, SG_WIDTH), lambda i: (0, 0)),
                  pl.BlockSpec((SG_GROUPS, SG_BLOCK, SG_BLOCK), lambda i: (0, 0, 0)), pl.BlockSpec((SG_BLOCK, SG_GROUPS), lambda i: (0, 0)),
                  pl.BlockSpec(memory_space=pl.ANY)],
        out_specs=blk(1), out_shape=jax.ShapeDtypeStruct(mix_buf.shape, mix_buf.dtype), input_output_aliases={5: 0},
        compiler_params=_params("parallel"), name="sg_forward")(p, p, norm_g, w_s, b_t, mix_buf)


def _sg_backward(p, norm_g, w_s, b_t, dmix):
    t = p.shape[0]

    def body(u_ref, v_ref, ng_ref, w_ref, b_ref, do_ref, duv_ref, dng_ref, dw_ref, db_ref):
        @pl.when(pl.program_id(0) == 0)
        def _():
            dng_ref[...] = jnp.zeros_like(dng_ref)
            dw_ref[...] = jnp.zeros_like(dw_ref)
            db_ref[...] = jnp.zeros_like(db_ref)

        mask = _sg_mask()
        lane = lax.broadcasted_iota(jnp.int32, (SG_BLOCK, LANES), 1)
        pairs = [(slice(b * SG_BLOCK, (b + 1) * SG_BLOCK), g, slice(g * SG_DIM, (g + 1) * SG_DIM))
                 for b in range(SG_STEP_BLOCKS) for g in range(SG_GROUPS)]
        w_m = [jnp.where(mask, w_ref[g], 0.0) for g in range(SG_GROUPS)]
        vg = [gelu(v_ref[rows, sl]) for rows, g, sl in pairs]
        normed = [rms_fwd(vg[i], ng_ref[:, sl]) for i, (rows, g, sl) in enumerate(pairs)]
        s = [dot_nn(w_m[g], normed[i][0], FAST) + b_ref[:, g:g + 1] for i, (rows, g, sl) in enumerate(pairs)]
        ds = []
        db = jnp.zeros((SG_BLOCK, LANES), F32)
        for i, (rows, g, sl) in enumerate(pairs):
            u_raw, do = u_ref[rows, sl], do_ref[rows, sl]
            duv_ref[rows, sl] = (do * s[i] * gelu_grad(u_raw)).astype(BF16)
            ds.append(do * gelu(u_raw))
            db = db + jnp.where(lane == g, jnp.sum(ds[i], axis=1, keepdims=True), 0.0)
        dw = [jnp.where(mask, dot_nt(ds[i], normed[i][0], FAST), 0.0) for i in range(len(pairs))]
        dvn = [dot_tn(w_m[g], ds[i], FAST) for i, (rows, g, sl) in enumerate(pairs)]
        for i, (rows, g, sl) in enumerate(pairs):
            dw_ref[g] += dw[i]
            dvg, dng_rows = rms_bwd(vg[i], normed[i][1], ng_ref[:, sl], dvn[i])
            dng_ref[:, sl] += jnp.sum(dng_rows, axis=0, keepdims=True)
            duv_ref[rows, SG_WIDTH + g * SG_DIM:SG_WIDTH + (g + 1) * SG_DIM] = (dvg * gelu_grad(v_ref[rows, sl])).astype(BF16)
        db_ref[...] += db

    step_rows = SG_STEP_BLOCKS * SG_BLOCK
    blk = lambda cb: pl.BlockSpec((step_rows, SG_WIDTH), lambda i: (i, cb))
    const2 = lambda shape: pl.BlockSpec(shape, lambda i: (0, 0))
    w_spec = pl.BlockSpec((SG_GROUPS, SG_BLOCK, SG_BLOCK), lambda i: (0, 0, 0))
    return _pcall(
        body, grid=(t // step_rows,),
        in_specs=[blk(4), blk(5), const2((1, SG_WIDTH)), w_spec, const2((SG_BLOCK, SG_GROUPS)), blk(1)],
        out_specs=[pl.BlockSpec((step_rows, 2 * SG_WIDTH), lambda i: (i, 2)), const2((1, SG_WIDTH)), w_spec,
                   const2((SG_BLOCK, LANES))],
        out_shape=[jax.ShapeDtypeStruct((t, PROJ_PAD), BF16), jax.ShapeDtypeStruct((1, SG_WIDTH), F32),
                   jax.ShapeDtypeStruct((SG_GROUPS, SG_BLOCK, SG_BLOCK), F32), jax.ShapeDtypeStruct((SG_BLOCK, LANES), F32)],
        compiler_params=_params("arbitrary"), name="sg_backward")(p, p, norm_g, w_s, b_t, dmix)


FFN_COLS = 256


def _norm_up_ffn(x, g, w_up_t, conv_w, conv_b):
    t, d = x.shape
    tm = min(t, 256)
    blocks = D_FF // FFN_COLS
    nt = (((1,), (1,)), ((), ()))

    def body(x_ref, g_ref, w_ref, cw_ref, cb_ref, up_ref, act_ref, h_ref, r_ref, tail_ref, prev_ref):
        @pl.when(pl.program_id(0) == 0)
        def _():
            tail_ref[...] = jnp.zeros_like(tail_ref)

        y, r = rms_fwd(x_ref[...], g_ref[...])
        h = y.astype(BF16)
        h_ref[...] = h
        r_ref[...] = r

        def project(blk):
            out = []
            for half in range(2):
                cols = slice(half * D_FF + blk * FFN_COLS, half * D_FF + (blk + 1) * FFN_COLS)
                u = lax.dot_general(h, w_ref[cols, :], nt, preferred_element_type=F32)
                up_ref[:, cols] = u
                prev_ref[:, cols] = tail_ref[:, cols]
                tail_ref[:, cols] = u[tm - HALO:, :]
                out.append(cols)
            return out

        def history(row0, cols):
            if row0 == 0:
                return jnp.concatenate([prev_ref[:, cols], up_ref[0:STRIP, cols]], axis=0)
            return up_ref[row0 - HALO:row0 + STRIP, cols]

        def activate(blk, g_cols, v_cols):
            for row0 in range(0, tm, STRIP):
                for c0 in range(0, FFN_COLS, LANES):
                    gc = slice(g_cols.start + c0, g_cols.start + c0 + LANES)
                    vc = slice(v_cols.start + c0, v_cols.start + c0 + LANES)
                    cg = _causal_conv(_delays(history(row0, gc), FFN_CONV), cw_ref[:, gc]) + cb_ref[:, gc]
                    cv = _causal_conv(_delays(history(row0, vc), FFN_CONV), cw_ref[:, vc]) + cb_ref[:, vc]
                    act_ref[row0:row0 + STRIP, blk * FFN_COLS + c0:blk * FFN_COLS + c0 + LANES] = (silu(cg) * cv).astype(BF16)

        pending = None
        for blk in range(blocks):
            cols = project(blk)
            if pending is not None:
                activate(*pending)
            pending = (blk, *cols)
        activate(*pending)

    rows = lambda w: pl.BlockSpec((tm, w), lambda i: (i, 0))
    whole = lambda a: pl.BlockSpec(a.shape, lambda i: (0, 0))
    return _pcall(
        body, grid=(t // tm,),
        in_specs=[rows(d), whole(g), whole(w_up_t), whole(conv_w), whole(conv_b)],
        out_specs=[rows(2 * D_FF), rows(D_FF), rows(d), rows(1)],
        out_shape=[jax.ShapeDtypeStruct((t, 2 * D_FF), F32), jax.ShapeDtypeStruct((t, D_FF), BF16),
                   jax.ShapeDtypeStruct((t, d), BF16), jax.ShapeDtypeStruct((t, 1), F32)],
        scratch_shapes=[pltpu.VMEM((HALO, 2 * D_FF), F32), pltpu.VMEM((HALO, 2 * D_FF), F32)],
        compiler_params=_params("arbitrary"), name="norm_up_ffn")(x, g, w_up_t, conv_w, conv_b)


def _ffn_bwd(up, conv_w, conv_b, dact):
    t = up.shape[0]
    tm = _pick(t, 256)
    n_tok = t // tm
    width = 2 * D_FF

    def dconv(delayed_g, delayed_v, da, wg, wv, bg, bv):
        cg = _causal_conv(delayed_g, wg) + bg
        cv = _causal_conv(delayed_v, wv) + bv
        s = sigmoid(cg)
        return da * cv * (s * (1.0 + cg * (1.0 - s))), da * (cg * s)

    def body(up_ref, prev_ref, next_ref, da_ref, dan_ref, w_ref, b_ref, dup_ref, dw_ref, db_ref, dc_ref, dw_acc, db_acc):
        first = pl.program_id(0) == 0
        last = pl.program_id(0) == n_tok - 1
        dw_acc[...] = jnp.zeros_like(dw_acc)
        db_acc[...] = jnp.zeros_like(db_acc)

        def strip(row0):
            rows = pl.ds(row0, STRIP)
            for c0 in range(0, D_FF, LANES):
                gc, vc = slice(c0, c0 + LANES), slice(D_FF + c0, D_FF + c0 + LANES)
                del_g = _delays(_history(up_ref, prev_ref, first, row0, gc), FFN_CONV)
                del_v = _delays(_history(up_ref, prev_ref, first, row0, vc), FFN_CONV)
                dcg, dcv = dconv(del_g, del_v, da_ref[rows, gc], w_ref[:, gc], w_ref[:, vc], b_ref[:, gc], b_ref[:, vc])
                dc_ref[rows, gc] = dcg
                dc_ref[rows, vc] = dcv
                db_acc[:, gc] += _fold_rows(dcg)
                db_acc[:, vc] += _fold_rows(dcv)
                for j in range(FFN_CONV):
                    k = FFN_CONV - 1 - j
                    dw_acc[k * SUBLANES:(k + 1) * SUBLANES, gc] += _fold_rows(dcg * del_g[j])
                    dw_acc[k * SUBLANES:(k + 1) * SUBLANES, vc] += _fold_rows(dcv * del_v[j])

        _first_then_strips(tm, strip)

        for c0 in range(0, D_FF, LANES):
            gc, vc = slice(c0, c0 + LANES), slice(D_FF + c0, D_FF + c0 + LANES)

            def delayed(cols):
                return _delays(jnp.concatenate([up_ref[tm - HALO:tm, cols], next_ref[:, cols]], axis=0), FFN_CONV)

            dcg, dcv = dconv(delayed(gc), delayed(vc), dan_ref[:, gc], w_ref[:, gc], w_ref[:, vc], b_ref[:, gc], b_ref[:, vc])
            dc_ref[tm:, gc] = jnp.where(last, 0.0, dcg)
            dc_ref[tm:, vc] = jnp.where(last, 0.0, dcv)

        def strip_dx(row0):
            for c0 in range(0, width, LANES):
                cols = slice(c0, c0 + LANES)
                dup_ref[pl.ds(row0, STRIP), cols] = _advanced_conv(dc_ref, row0, cols, w_ref[:, cols]).astype(BF16)

        _for_strips(tm, STRIP, strip_dx)

        @pl.when(first)
        def _():
            dw_ref[...] = jnp.zeros_like(dw_ref)
            db_ref[...] = jnp.zeros_like(db_ref)

        for k in range(FFN_CONV):
            dw_ref[k:k + 1, :] += jnp.sum(dw_acc[k * SUBLANES:(k + 1) * SUBLANES, :], axis=0, keepdims=True)
        db_ref[...] += jnp.sum(db_acc[...], axis=0, keepdims=True)

    next_rows = lambda i: jnp.minimum((i + 1) * (tm // HALO), t // HALO - 1)
    full = lambda rows: pl.BlockSpec((rows, width), lambda i: (0, 0))
    return _pcall(
        body, grid=(n_tok,),
        in_specs=[pl.BlockSpec((tm, width), lambda i: (i, 0)),
                  pl.BlockSpec((HALO, width), lambda i: (jnp.maximum(i * (tm // HALO) - 1, 0), 0)),
                  pl.BlockSpec((HALO, width), lambda i: (next_rows(i), 0)),
                  pl.BlockSpec((tm, D_FF), lambda i: (i, 0)), pl.BlockSpec((HALO, D_FF), lambda i: (next_rows(i), 0)),
                  full(FFN_CONV), full(1)],
        out_specs=[pl.BlockSpec((tm, width), lambda i: (i, 0)), full(FFN_CONV), full(1)],
        out_shape=[jax.ShapeDtypeStruct((t, width), BF16), jax.ShapeDtypeStruct((FFN_CONV, width), F32),
                   jax.ShapeDtypeStruct((1, width), F32)],
        scratch_shapes=[pltpu.VMEM((tm + HALO, width), F32),
                        pltpu.VMEM((FFN_CONV * SUBLANES, width), F32), pltpu.VMEM((SUBLANES, width), F32)],
        compiler_params=_params("arbitrary"), name="ffn_bwd")(up, up, up, dact, dact, conv_w, conv_b)


def _my_position():
    return lax.axis_index("x"), lax.axis_index("y"), lax.axis_index("c")


COPIES = N_DEV - 1


def _all_gather(arrays):
    n = len(arrays)

    def body(*refs):
        x_refs, out_refs = refs[:n], refs[n:2 * n]
        send_sems, recv_sems, local_sems = refs[2 * n:]
        x, y, cc = _my_position()
        me, sibling = (x, y, cc), (x, y, 1 - cc)
        chips = [(1 - x, y), (x, 1 - y), (1 - x, 1 - y)]

        def block(a, px, py, pc):
            return out_refs[a].at[4 * px + 2 * py + pc]

        def copy(a, k, blk, to, src=None):
            return pltpu.make_async_remote_copy(
                src_ref=block(a, *blk) if src is None else src, dst_ref=block(a, *blk),
                send_sem=send_sems.at[a * COPIES + k], recv_sem=recv_sems.at[a * COPIES + k],
                device_id=to, device_id_type=MESH_ID)

        mine = [pltpu.make_async_copy(x_refs[a], block(a, *me), local_sems.at[a]) for a in range(n)]
        for cp in mine:
            cp.start()
        first = []
        for a in range(n):
            first.append(copy(a, 0, me, sibling, src=x_refs[a]))
            first += [copy(a, 1 + j, me, (*chip, cc), src=x_refs[a]) for j, chip in enumerate(chips)]
        for cp in first:
            cp.start()
        passed = []
        for j, chip in enumerate(chips):
            for a in range(n):
                copy(a, 1 + j, (*chip, cc), me).wait_recv()
                passed.append(copy(a, 4 + j, (*chip, cc), sibling))
                passed[-1].start()
        for a in range(n):
            copy(a, 0, sibling, me).wait_recv()
        for j, chip in enumerate(chips):
            for a in range(n):
                copy(a, 4 + j, (*chip, 1 - cc), me).wait_recv()
        for cp in first + passed:
            cp.wait_send()
        for cp in mine:
            cp.wait()

    any_spec = pl.BlockSpec(memory_space=pl.ANY)
    return _pcall(
        body, out_shape=[jax.ShapeDtypeStruct((N_DEV,) + a.shape, a.dtype) for a in arrays],
        in_specs=[any_spec] * n, out_specs=[any_spec] * n,
        scratch_shapes=[pltpu.SemaphoreType.DMA((n * COPIES,)), pltpu.SemaphoreType.DMA((n * COPIES,)),
                        pltpu.SemaphoreType.DMA((n,))],
        name="all_gather")(*arrays)


def _all_to_all(sends):
    n = len(sends)

    def body(*refs):
        send_refs, recv_refs = refs[:n], refs[n:2 * n]
        send_sems, recv_sems, local_sems = refs[2 * n:]
        x, y, cc = _my_position()
        me = 4 * x + 2 * y + cc
        mine = [pltpu.make_async_copy(send_refs[a].at[me], recv_refs[a].at[me], local_sems.at[a]) for a in range(n)]
        for cp in mine:
            cp.start()
        copies = []
        for rel in range(1, N_DEV):
            px, py, pc = x ^ (rel >> 2), y ^ ((rel >> 1) & 1), cc ^ (rel & 1)
            for a in range(n):
                copies.append(pltpu.make_async_remote_copy(
                    src_ref=send_refs[a].at[4 * px + 2 * py + pc], dst_ref=recv_refs[a].at[me],
                    send_sem=send_sems.at[a * COPIES + rel - 1], recv_sem=recv_sems.at[a * COPIES + rel - 1],
                    device_id=(px, py, pc), device_id_type=MESH_ID))
        for cp in copies:
            cp.start()
        for cp in copies:
            cp.wait()
        for cp in mine:
            cp.wait()

    any_spec = pl.BlockSpec(memory_space=pl.ANY)
    return _pcall(
        body, out_shape=[jax.ShapeDtypeStruct(s.shape, s.dtype) for s in sends],
        in_specs=[any_spec] * n, out_specs=[any_spec] * n,
        scratch_shapes=[pltpu.SemaphoreType.DMA((n * COPIES,)), pltpu.SemaphoreType.DMA((n * COPIES,)),
                        pltpu.SemaphoreType.DMA((n,))],
        name="all_to_all")(*sends)


def _hbm(a):
    return pltpu.with_memory_space_constraint(a, pltpu.HBM)


def _split_copies(send_refs, land_refs, send_sems, recv_sems, local_sems, gather):
    x, y, cc = _my_position()
    me = 4 * x + 2 * y + cc
    local, remote = [], []
    for a, (send, land) in enumerate(zip(send_refs, land_refs)):
        local.append(pltpu.make_async_copy(send if gather else send.at[me], land.at[me], local_sems.at[a]))
    for a, (send, land) in enumerate(zip(send_refs, land_refs)):
        for rel in range(1, N_DEV):
            px, py, pc = x ^ (rel >> 2), y ^ ((rel >> 1) & 1), cc ^ (rel & 1)
            remote.append(pltpu.make_async_remote_copy(
                src_ref=send if gather else send.at[4 * px + 2 * py + pc], dst_ref=land.at[me],
                send_sem=send_sems.at[a * COPIES + rel - 1], recv_sem=recv_sems.at[a * COPIES + rel - 1],
                device_id=(px, py, pc), device_id_type=MESH_ID))
    return local, remote


SPLIT_EFFECT = pltpu.SideEffectType.DATAFLOW_SIDE_EFFECTING


def _exchange_start(sends, after, gather, name):
    n = len(sends)
    lands = [_hbm(lax.empty((N_DEV,) + s.shape if gather else s.shape, s.dtype)) for s in sends]

    def body(*refs):
        send_refs, land_refs = refs[:n], refs[n:2 * n]
        send_sems, recv_sems, local_sems = refs[2 * n + 1:2 * n + 4]
        token = refs[-1]
        local, remote = _split_copies(send_refs, land_refs, send_sems, recv_sems, local_sems, gather)
        for cp in local + remote:
            cp.start()
        token[...] = jnp.zeros_like(token)

    hbm, sem = pl.BlockSpec(memory_space=pltpu.HBM), pl.BlockSpec(memory_space=pltpu.SEMAPHORE)
    out = _pcall(
        body, name=name,
        out_shape=[pltpu.SemaphoreType.DMA((n * COPIES,)), pltpu.SemaphoreType.DMA((n * COPIES,)), pltpu.SemaphoreType.DMA((n,))]
        + [pltpu.HBM(s.shape, s.dtype) for s in sends] + [pltpu.HBM(z.shape, z.dtype) for z in lands]
        + [jax.ShapeDtypeStruct((SUBLANES, LANES), F32)],
        in_specs=[hbm] * (2 * n) + [pl.BlockSpec(memory_space=pl.ANY)],
        out_specs=[sem] * 3 + [hbm] * (2 * n) + [pl.BlockSpec(memory_space=pltpu.VMEM)],
        input_output_aliases={i: 3 + i for i in range(2 * n)},
        compiler_params=pltpu.CompilerParams(has_side_effects=SPLIT_EFFECT),
    )(*[_hbm(s) for s in sends], *lands, after)
    return dict(sems=out[:3], sends=out[3:3 + n], lands=out[3 + n:3 + 2 * n], gather=gather), out[-1]


def _exchange_wait(handle, after, name):
    sends, lands, gather = handle["sends"], handle["lands"], handle["gather"]
    n = len(sends)

    def body(*refs):
        send_refs, land_refs = refs[:n], refs[n:2 * n]
        send_sems, recv_sems, local_sems = refs[2 * n:2 * n + 3]
        local, remote = _split_copies(send_refs, land_refs, send_sems, recv_sems, local_sems, gather)
        for cp in remote:
            cp.wait_send()
            cp.wait_recv()
        for cp in local:
            cp.wait()

    hbm, sem = pl.BlockSpec(memory_space=pltpu.HBM), pl.BlockSpec(memory_space=pltpu.SEMAPHORE)
    out = _pcall(
        body, name=name,
        out_shape=[pltpu.HBM(s.shape, s.dtype) for s in sends] + [pltpu.HBM(z.shape, z.dtype) for z in lands],
        in_specs=[hbm] * (2 * n) + [sem] * 3 + [pl.BlockSpec(memory_space=pl.ANY)],
        out_specs=[hbm] * (2 * n), input_output_aliases={i: i for i in range(2 * n)},
        compiler_params=pltpu.CompilerParams(has_side_effects=SPLIT_EFFECT),
    )(*sends, *lands, *handle["sems"], after)
    return out[n:]


def _sum_and_adamw(recv, w, m, v, name):
    _, r, wp = recv.shape
    c = w.shape[-1]
    rowwise = w.ndim == 3 and w.shape[1] == 1
    lead = w.ndim == 3 and not rowwise
    get = (lambda ref: ref[:, 0, :]) if rowwise else (lambda ref: ref[...])
    tr = max([d for d in range(2 * SUBLANES, 257, 2 * SUBLANES) if r % d == 0], default=r)
    bc1 = 1.0 - ADAM_B1 ** ADAM_STEP
    bc2 = 1.0 - ADAM_B2 ** ADAM_STEP

    def body(recv_ref, w_ref, m_ref, v_ref, g_ref, d_ref, nm_ref, nv_ref):
        g = recv_ref[0, :, 0:c].astype(F32)
        for s in range(1, N_DEV):
            g = g + recv_ref[s, :, 0:c].astype(F32)
        m_new = ADAM_B1 * get(m_ref) + (1.0 - ADAM_B1) * g
        v_new = ADAM_B2 * get(v_ref) + (1.0 - ADAM_B2) * (g * g)
        m_hat = m_new / bc1
        v_hat = v_new / bc2
        delta = -ADAM_LR * (m_hat / (jnp.sqrt(v_hat) + ADAM_EPS) + ADAM_WD * get(w_ref))
        for ref, val in ((g_ref, g), (d_ref, delta), (nm_ref, m_new), (nv_ref, v_new)):
            if rowwise:
                ref[:, 0, :] = val
            else:
                ref[...] = val

    if rowwise:
        tile = pl.BlockSpec((tr, 1, c), lambda i: (i, 0, 0))
    elif lead:
        tile = pl.BlockSpec((None, tr, c), lambda i: (0, i, 0))
    else:
        tile = pl.BlockSpec((tr, c), lambda i: (i, 0))
    return _pcall(
        body, grid=(r // tr,),
        in_specs=[pl.BlockSpec((N_DEV, tr, wp), lambda i: (0, i, 0)), tile, tile, tile],
        out_specs=[tile] * 4, out_shape=[jax.ShapeDtypeStruct(w.shape, F32)] * 4,
        compiler_params=_params("parallel"), name=name)(recv, w, m, v)


SHARDED_TAPS = ("dn_conv_w", "ffn_conv_w")
REPLICATED = ("attn_norm_g", "dn_a_log", "dn_dt_bias", "dn_out_norm_g", "sg_norm_g", "sg_w", "sg_b", "ffn_norm_g",
              "ffn_conv_b", "final_norm_g")
SMALL = SHARDED_TAPS + REPLICATED
WEIGHT_ORDER = ("attn_norm_g", "w_in", "dn_conv_w", "dn_a_log", "dn_dt_bias", "dn_out_norm_g", "sg_norm_g", "sg_w", "sg_b",
                "w_out", "ffn_norm_g", "w_up", "ffn_conv_w", "ffn_conv_b", "w_down", "final_norm_g")
SLAB_COLS = 1024


def _pad_to(flat, multiple):
    pad = (-flat.shape[-1]) % multiple
    if pad == 0:
        return flat
    return jnp.pad(flat, [(0, 0)] * (flat.ndim - 1) + [(0, pad)])


def _pack_small(named):
    flat = jnp.concatenate([named[n].reshape(-1) for n in SMALL])
    return _pad_to(flat, SUBLANES * SLAB_COLS).reshape(-1, SLAB_COLS)


def _unpack_small(slab, like):
    flat = slab.reshape(-1)
    out, off = {}, 0
    for n in SMALL:
        size = like[n].size
        out[n] = flat[off:off + size].reshape(like[n].shape)
        off += size
    return out


def _split_columns(full, n_local):
    r = full.shape[0]
    return full.reshape(r, N_DEV, n_local).transpose(1, 0, 2).reshape(N_DEV, r * n_local)


def _join_columns(blocks, r, n_local):
    return blocks.reshape(N_DEV, r, n_local).transpose(1, 0, 2).reshape(r, N_DEV * n_local)


def _lanes4(a):
    return jnp.pad(a.reshape(1, N_HEADS), ((0, 0), (0, LANES - N_HEADS)))


def kernel(x, attn_norm_g, w_in, dn_conv_w, dn_a_log, dn_dt_bias, dn_out_norm_g, sg_norm_g, sg_w, sg_b, w_out, ffn_norm_g, w_up, ffn_conv_w, ffn_conv_b, w_down, final_norm_g, loss_target, m_attn_norm_g, m_w_in, m_dn_conv_w, m_dn_a_log, m_dn_dt_bias, m_dn_out_norm_g, m_sg_norm_g, m_sg_w, m_sg_b, m_w_out, m_ffn_norm_g, m_w_up, m_ffn_conv_w, m_ffn_conv_b, m_w_down, m_final_norm_g, v_attn_norm_g, v_w_in, v_dn_conv_w, v_dn_a_log, v_dn_dt_bias, v_dn_out_norm_g, v_sg_norm_g, v_sg_w, v_sg_b, v_w_out, v_ffn_norm_g, v_w_up, v_ffn_conv_w, v_ffn_conv_b, v_w_down, v_final_norm_g):
    weights = dict(attn_norm_g=attn_norm_g, w_in=w_in, dn_conv_w=dn_conv_w, dn_a_log=dn_a_log, dn_dt_bias=dn_dt_bias,
                   dn_out_norm_g=dn_out_norm_g, sg_norm_g=sg_norm_g, sg_w=sg_w, sg_b=sg_b, w_out=w_out, ffn_norm_g=ffn_norm_g,
                   w_up=w_up, ffn_conv_w=ffn_conv_w, ffn_conv_b=ffn_conv_b, w_down=w_down, final_norm_g=final_norm_g)
    m_in = dict(attn_norm_g=m_attn_norm_g, w_in=m_w_in, dn_conv_w=m_dn_conv_w, dn_a_log=m_dn_a_log, dn_dt_bias=m_dn_dt_bias,
                dn_out_norm_g=m_dn_out_norm_g, sg_norm_g=m_sg_norm_g, sg_w=m_sg_w, sg_b=m_sg_b, w_out=m_w_out,
                ffn_norm_g=m_ffn_norm_g, w_up=m_w_up, ffn_conv_w=m_ffn_conv_w, ffn_conv_b=m_ffn_conv_b, w_down=m_w_down,
                final_norm_g=m_final_norm_g)
    v_in = dict(attn_norm_g=v_attn_norm_g, w_in=v_w_in, dn_conv_w=v_dn_conv_w, dn_a_log=v_dn_a_log, dn_dt_bias=v_dn_dt_bias,
                dn_out_norm_g=v_dn_out_norm_g, sg_norm_g=v_sg_norm_g, sg_w=v_sg_w, sg_b=v_sg_b, w_out=v_w_out,
                ffn_norm_g=v_ffn_norm_g, w_up=v_w_up, ffn_conv_w=v_ffn_conv_w, ffn_conv_b=v_ffn_conv_b, w_down=v_w_down,
                final_norm_g=v_final_norm_g)

    n_in, n_up = w_in.shape[2], w_up.shape[2]
    r_out, r_down = w_out.shape[1], w_down.shape[1]
    n_dnc, n_ffc = dn_conv_w.shape[2], ffn_conv_w.shape[2]
    transposed = lambda a: jnp.transpose(a, (0, 2, 1))
    taps = _pad_to(jnp.concatenate([dn_conv_w.reshape(-1), ffn_conv_w.reshape(-1)]), SUBLANES * LANES).reshape(-1, LANES)
    g_in, g_taps = _all_gather([transposed(w_in)[0].astype(BF16), taps])
    gather_out, token = _exchange_start([w_out[0].astype(BF16)], g_taps, True, "gather_w_out")
    gather_up, token = _exchange_start([transposed(w_up)[0].astype(BF16)], token, True, "gather_w_up")
    gather_down, token = _exchange_start([w_down[0].astype(BF16)], token, True, "gather_w_down")
    w_in_t = jnp.pad(g_in.reshape(N_DEV * n_in, D_MODEL), ((0, PROJ_PAD - N_DEV * n_in), (0, 0)))
    taps_all = g_taps.reshape(N_DEV, -1)
    dn_conv_full = _join_columns(taps_all[:, :CONV_K * n_dnc], CONV_K, n_dnc)
    ffn_conv_full = _join_columns(taps_all[:, CONV_K * n_dnc:CONV_K * n_dnc + FFN_CONV * n_ffc], FFN_CONV, n_ffc)
    late = dict(
        w_out=lambda after: _exchange_wait(gather_out, after, "gather_w_out_wait")[0].reshape(N_DEV * r_out, D_MODEL),
        w_up_t=lambda after: _exchange_wait(gather_up, after, "gather_w_up_wait")[0].reshape(N_DEV * n_up, D_MODEL),
        w_down=lambda after: _exchange_wait(gather_down, after, "gather_w_down_wait")[0].reshape(N_DEV * r_down, D_MODEL))

    def send_early(blocks, after, name):
        return _exchange_start(blocks, after, False, name)

    def send_small(g, loss_lanes, after):
        small = jnp.concatenate([g[n].reshape(-1) for n in REPLICATED] + [loss_lanes[0, 0:1]])
        slab = jnp.concatenate([_split_columns(g["dn_conv_w"], n_dnc), _split_columns(g["ffn_conv_w"], n_ffc),
                                jnp.broadcast_to(small[None, :], (N_DEV, small.shape[0]))], axis=1)
        return send_early([_pad_to(slab, SUBLANES * SLAB_COLS).reshape(N_DEV, -1, SLAB_COLS)], after, "send_small")

    upd = {}

    def update_early(sent_down, sent_up_out, sent_small, after):
        r_dn, = _exchange_wait(sent_down, after, "send_dw_down_wait")
        r_up, r_o = _exchange_wait(sent_up_out, after, "send_dw_up_out_wait")
        r_small, = _exchange_wait(sent_small, after, "send_small_wait")
        upd["w_down"] = _sum_and_adamw(r_dn, w_down, m_w_down, v_w_down, "adamw_w_down")
        upd["w_up"] = [transposed(o) for o in _sum_and_adamw(r_up, transposed(w_up), transposed(m_w_up), transposed(v_w_up),
                                                             "adamw_w_up")]
        upd["w_out"] = _sum_and_adamw(r_o, w_out, m_w_out, v_w_out, "adamw_w_out")
        upd["small"] = _sum_and_adamw(r_small, _pack_small(weights), _pack_small(m_in), _pack_small(v_in), "adamw_small")

    grad_x, d_g1, sent_in = _local_step(
        x[0], loss_target[0], w_in_t, late, send_early, send_small, update_early, dn_conv_full, ffn_conv_full,
        attn_norm_g + token[0:1, 0:1], dn_a_log, dn_dt_bias, dn_out_norm_g, sg_norm_g, sg_w, sg_b, ffn_norm_g, ffn_conv_b,
        final_norm_g, n_in)

    norm_rows = D_MODEL // LANES
    r_g1, = _all_to_all([jnp.broadcast_to(d_g1.reshape(1, norm_rows, LANES), (N_DEV, norm_rows, LANES))])
    r_in, = _exchange_wait(sent_in, r_g1, "send_dw_in_wait")
    rowwise = lambda a: jnp.transpose(a, (2, 0, 1))
    upd["w_in"] = [jnp.transpose(o, (1, 2, 0)) for o in
                   _sum_and_adamw(r_in, rowwise(w_in), rowwise(m_w_in), rowwise(v_w_in), "adamw_w_in")]
    small_upd = upd.pop("small")
    as_rows = lambda a: a.reshape(norm_rows, LANES)
    norm_upd = _sum_and_adamw(r_g1, as_rows(attn_norm_g), as_rows(m_attn_norm_g), as_rows(v_attn_norm_g), "adamw_attn_norm")
    results = []
    for i in range(4):
        named = _unpack_small(small_upd[i], weights)
        named.update({n: upd[n][i] for n in upd})
        named["attn_norm_g"] = norm_upd[i].reshape(attn_norm_g.shape)
        results.append(named)

    loss = small_upd[0].reshape(-1)[sum(weights[n].size for n in SMALL)]
    return (loss, grad_x[None], *[r[n] for r in results for n in WEIGHT_ORDER])


def _local_step(x2d, tgt, w_in_t, late, send_early, send_small, update_early, dn_conv_full, ffn_conv_full, attn_norm_g,
                dn_a_log, dn_dt_bias, dn_out_norm_g, sg_norm_g, sg_w, sg_b, ffn_norm_g, ffn_conv_b, final_norm_g, n_in):
    t = x2d.shape[0]
    g1, g2, gf = attn_norm_g, ffn_norm_g, final_norm_g.reshape(1, D_MODEL)
    a_log4, dt_bias4 = _lanes4(dn_a_log), _lanes4(dn_dt_bias)
    sg_w3 = sg_w[0]
    sg_b_t = sg_b[0].T
    conv_b = ffn_conv_b

    p, h1, rstd1 = _rmsnorm_matmul(x2d, g1, w_in_t, "norm_in_proj", 512)
    q, k, v, beta4, g4 = _dn_prep(p, dn_conv_full, a_log4, dt_bias4)
    mix_half, *dn_saved = _dn_forward(q, k, v, beta4, g4, p, dn_out_norm_g)
    mix = _sg_forward(p, sg_norm_g, sg_w3, sg_b_t, mix_half)
    w_out_full = late["w_out"](mix)
    x2 = _matmul(mix, w_out_full, "nn", "out_proj", (1024, 1024, 1024), add=x2d)
    w_up_t = late["w_up_t"](x2)
    up, act, h2, rstd2 = _norm_up_ffn(x2, g2, w_up_t, ffn_conv_full, conv_b)
    w_down_full = late["w_down"](act)
    fn, outs = _final_loss_rows(t, D_MODEL)
    loss_lanes, dx3, dx3b, d_gf = _matmul_rows(act, w_down_full, "nn", "down_proj_loss", 512,
                                               [(x2, "rows"), (tgt, "rows"), (gf, "whole")], outs, fn)

    dact = _matmul(dx3b, w_down_full, "nt", "down_proj_dx", (512, D_FF, D_MODEL))
    d_w_down = _matmul(act, dx3b, "tn", "down_proj_dw", (256, 1024, t), out_dtype=BF16)
    sent_down, token = send_early([d_w_down.reshape(N_DEV, D_FF // N_DEV, D_MODEL)], d_w_down, "send_dw_down")
    dup, d_ffn_conv, d_ffn_conv_b = _ffn_bwd(up, ffn_conv_full, conv_b + token[0:1, 0:1], dact)
    fn, outs = _rmsnorm_bwd_rows(t, D_MODEL)
    dx2, dx2b, d_g2 = _matmul_rows(dup, w_up_t, "nn", "up_proj_dx_norm", 256,
                                   [(x2, "rows"), (rstd2, "rows"), (g2, "whole"), (dx3, "rows")], outs, fn)
    d_w_up_t = _matmul(dup, h2, "tn", "up_proj_dw", (512, 1024, t), out_dtype=BF16)
    dmix = _matmul(dx2b, w_out_full, "nt", "out_proj_dx", (1024, 1024, 1024))
    d_w_out = _matmul(mix, dx2b, "tn", "out_proj_dw", (512, 1024, t), out_dtype=BF16)
    sent_up_out, token = send_early(
        [d_w_up_t.reshape(N_DEV, 2 * D_FF // N_DEV, D_MODEL), d_w_out.reshape(N_DEV, D_MODEL // N_DEV, D_MODEL)],
        d_w_out, "send_dw_up_out")
    dp, d_sg_norm, d_sg_w, d_sg_b_t = _sg_backward(p, sg_norm_g + token[0:1, 0:1], sg_w3, sg_b_t, dmix)
    dq, dk, dv, dbeta4, dg4, dp, d_dn_norm = _dn_backward(q, k, v, beta4, g4, p, dn_out_norm_g, dn_saved, dmix, dp)
    dc_dn, d_dn_conv, dp, d_a_log4, d_dt_bias4 = _dn_prep_bwd(p, dn_conv_full, a_log4, dt_bias4, dq, dk, dv, dbeta4, dg4, dp)
    small_grads = dict(
        attn_norm_g=jnp.zeros_like(attn_norm_g), dn_conv_w=d_dn_conv, dn_a_log=d_a_log4[:, :N_HEADS],
        dn_dt_bias=d_dt_bias4[:, :N_HEADS], dn_out_norm_g=d_dn_norm, sg_norm_g=d_sg_norm, sg_w=d_sg_w,
        sg_b=d_sg_b_t[:, :SG_GROUPS].T, ffn_norm_g=d_g2, ffn_conv_w=d_ffn_conv, ffn_conv_b=d_ffn_conv_b, final_norm_g=d_gf)
    sent_small, token = send_small(small_grads, loss_lanes, d_dn_conv)
    dp = _conv_bwd_input(dc_dn, dn_conv_full + token[0:1, 0:1], "dn_conv_dx", out_cols=PROJ_PAD, into=dp)
    d_w_in_t = _matmul(dp, h1, "tn", "in_proj_dw", (PROJ_PAD // 5, 1024, t), out_dtype=BF16)
    sent_in, token = send_early([d_w_in_t[:N_DEV * n_in].reshape(N_DEV, n_in, D_MODEL)], d_w_in_t, "send_dw_in")
    update_early(sent_down, sent_up_out, sent_small, token)
    fn, outs = _rmsnorm_bwd_rows(t, D_MODEL)
    grad_x, _, d_g1 = _matmul_rows(dp, w_in_t, "nn", "in_proj_dx_norm", 512,
                                   [(x2d, "rows"), (rstd1, "rows"), (g1 + token[0:1, 0:1], "whole"), (dx2, "rows")], outs, fn)

    return grad_x, d_g1, sent_in
```

```python
import math

import jax
import jax.numpy as jnp
from jax import lax
from jax.experimental import pallas as pl
from jax.experimental.pallas import tpu as pltpu

F32 = jnp.float32
BF16 = jnp.bfloat16
HI = lax.Precision.HIGHEST

D_MODEL = 1024
DN_WIDTH = 512
HEAD_DIM = 128
N_HEADS = 4
SG_WIDTH = 512
SG_GROUPS = 4
SG_DIM = 128
SG_BLOCK = 128
D_FF = 2816
CHUNK = 64
CONV_K = 4
FFN_CONV = 3
EPS = 1e-6
PROJ_MAIN = 3072
PROJ_PAD = 3200
GELU_C = math.sqrt(2.0 / math.pi)
N_DEV = 8
LANES = 128
SUBLANES = 8
HALO = SUBLANES
VMEM_LIMIT = 48 * 1024 * 1024

ADAM_LR = 0.001
ADAM_B1 = 0.9
ADAM_B2 = 0.999
ADAM_EPS = 1e-08
ADAM_WD = 0.01
ADAM_STEP = 10

MESH_ID = pl.DeviceIdType.MESH


def _pcall(body, **kw):
    return pl.pallas_call(body, **kw)


def _params(*sem):
    return pltpu.CompilerParams(dimension_semantics=sem, vmem_limit_bytes=VMEM_LIMIT)


def _pick(n, cap):
    best = None
    for t in range(LANES, cap + 1, LANES):
        if n % t == 0:
            best = t
    return best if best else n


FAST, EXACT = "bf16 operands, one pass", "f32 operands, six bf16 passes"


def dot_f32(a, b, dims, tier):
    if tier == FAST:
        return lax.dot_general(a.astype(BF16), b.astype(BF16), dims, preferred_element_type=F32)
    return lax.dot_general(a, b, dims, precision=HI, preferred_element_type=F32)


def dot_nn(a, b, tier=EXACT):
    return dot_f32(a, b, (((1,), (0,)), ((), ())), tier)


def dot_nt(a, b, tier=EXACT):
    return dot_f32(a, b, (((1,), (1,)), ((), ())), tier)


def dot_tn(a, b, tier=EXACT):
    return dot_f32(a, b, (((0,), (0,)), ((), ())), tier)


def sigmoid(x):
    return 0.5 * jnp.tanh(0.5 * x) + 0.5


def silu(x):
    return x * sigmoid(x)


def silu_grad(x):
    s = sigmoid(x)
    return s * (1.0 + x * (1.0 - s))


def gelu(x):
    return 0.5 * x * (1.0 + jnp.tanh(GELU_C * (x + 0.044715 * x * x * x)))


def gelu_grad(x):
    t = jnp.tanh(GELU_C * (x + 0.044715 * x * x * x))
    return 0.5 * (1.0 + t) + 0.5 * x * (1.0 - t * t) * GELU_C * (1.0 + 3.0 * 0.044715 * x * x)


def softplus(z):
    return jnp.maximum(z, 0.0) + jnp.log(1.0 + jnp.exp(-jnp.abs(z)))


def rms_fwd(x, g):
    r = lax.rsqrt(jnp.mean(x * x, axis=-1, keepdims=True) + EPS)
    return x * r * g, r


def rms_bwd(x, r, g, dy):
    dyg = dy * g
    xr = x * r
    dx = r * (dyg - xr * jnp.mean(dyg * xr, axis=-1, keepdims=True))
    return dx, dy * xr


def l2_fwd(x):
    r = lax.rsqrt(jnp.sum(x * x, axis=-1, keepdims=True) + EPS)
    return x * r, r


def l2_bwd(x, r, dy):
    xr = x * r
    return r * (dy - xr * jnp.sum(dy * xr, axis=-1, keepdims=True))


def _tri_masks(n):
    row = lax.broadcasted_iota(jnp.int32, (n, n), 0)
    col = lax.broadcasted_iota(jnp.int32, (n, n), 1)
    return row >= col, row > col


def chunk_cumsum(g4):
    incl, _ = _tri_masks(g4.shape[0])
    return dot_nn(incl.astype(F32), g4)


STACK = N_HEADS * CHUNK
DN_FWD_CHUNKS = 8
DN_CHUNKS = 4


def _head_rows(h):
    return slice(h * CHUNK, (h + 1) * CHUNK)


def _stack_heads(x):
    return jnp.concatenate([x[:, h * HEAD_DIM:(h + 1) * HEAD_DIM] for h in range(N_HEADS)], axis=0)


def _stack_lanes(x4):
    return jnp.concatenate([x4[:, h:h + 1] for h in range(N_HEADS)], axis=0)


def _per_head(fn):
    return jnp.concatenate([fn(h) for h in range(N_HEADS)], axis=0)


def _unit_lower_inverses(l_strict, order):
    c = l_strict[0].shape[0]
    row = lax.broadcasted_iota(jnp.int32, (c, c), 0)
    col = lax.broadcasted_iota(jnp.int32, (c, c), 1)
    eye = (row == col).astype(F32)
    p = [-l for l in l_strict]
    a = [eye + n for n in p]
    for _ in range(int(math.log2(order)) - 1):
        p = [dot_nn(x, x, FAST) for x in p]
        a = [x + dot_nn(x, y, FAST) for x, y in zip(a, p)]
    return a


def dn_chunks_local(chunks, inverses=None):
    row = lax.broadcasted_iota(jnp.int32, (STACK, STACK), 0)
    col = lax.broadcasted_iota(jnp.int32, (STACK, STACK), 1)
    same = (row // CHUNK) == (col // CHUNK)
    incl = jnp.logical_and(same, row >= col)
    strict = jnp.logical_and(same, row > col)
    locs = []
    for q, k, v, beta, gc4 in chunks:
        gc_col = _stack_lanes(gc4)
        gc_row = jnp.sum(jnp.where(row == col, gc_col, 0.0), axis=0, keepdims=True)
        decay = jnp.where(incl, jnp.exp(jnp.minimum(gc_col - gc_row, 0.0)), 0.0)
        gamma = jnp.exp(gc_col)
        gc_last = jnp.concatenate([jnp.broadcast_to(gc4[CHUNK - 1:CHUNK, h:h + 1], (CHUNK, 1)) for h in range(N_HEADS)], axis=0)
        tau = jnp.exp(gc_last - gc_col)
        kb = k * beta
        locs.append(dict(decay=decay, gamma=gamma, tau=tau, cd=jnp.exp(gc_last), kb=kb, qd=q * gamma, kt=k * tau,
                         incl=incl, strict=strict))
    for loc, (q, k, v, beta, gc4) in zip(locs, chunks):
        loc["l_mat"] = jnp.where(strict, dot_nt(loc["kb"], k, FAST) * loc["decay"], 0.0)
    if inverses is None:
        inverses = _unit_lower_inverses([loc["l_mat"] for loc in locs], CHUNK)
    for loc, a_inv in zip(locs, inverses):
        loc["a_inv"] = a_inv
    for loc, (q, k, v, beta, gc4) in zip(locs, chunks):
        sol = dot_nn(loc["a_inv"], jnp.concatenate([v * beta, loc["kb"] * loc["gamma"]], axis=1), FAST)
        loc.update(sol=sol, value=sol[:, :HEAD_DIM], kcd=sol[:, HEAD_DIM:])
        loc["attn"] = jnp.where(incl, dot_nt(q, k, FAST) * loc["decay"], 0.0)
    return locs


def dn_chunk_state(loc, s):
    kcd, qd, kt, cd = loc["kcd"], loc["qd"], loc["kt"], loc["cd"]
    v_new = loc["value"] - _per_head(lambda h: dot_nn(kcd[_head_rows(h)], s[h], FAST))
    o = _per_head(lambda h: dot_nn(qd[_head_rows(h)], s[h], FAST)) + dot_nn(loc["attn"], v_new, FAST)
    s_new = [s[h] * cd[h * CHUNK:h * CHUNK + 1, :] + dot_tn(kt[_head_rows(h)], v_new[_head_rows(h)], FAST)
             for h in range(N_HEADS)]
    loc["v_new"] = v_new
    return o, s_new


def dn_chunks_bwd(items, ds_last):
    hr = _head_rows
    n = len(items)
    pre = []
    for q, k, v, beta, loc, s, do in items:
        pre.append(dict(
            dv_part=dot_tn(loc["attn"], do, FAST),
            dattn=jnp.where(loc["incl"], dot_nt(do, loc["v_new"], FAST), 0.0),
            dqd=_per_head(lambda h: dot_nt(do[hr(h)], s[h], FAST)),
            ds_part=[dot_tn(loc["qd"][hr(h)], do[hr(h)], FAST) for h in range(N_HEADS)]))
    ds_new_of, dv_new_of = [None] * n, [None] * n
    ds = ds_last
    for c in reversed(range(n)):
        loc = items[c][4]
        ds_new_of[c] = ds
        dv_new = pre[c]["dv_part"] + _per_head(lambda h: dot_nn(loc["kt"][hr(h)], ds[h], FAST))
        dv_new_of[c] = dv_new
        ds = [pre[c]["ds_part"][h] + ds[h] * loc["cd"][h * CHUNK:h * CHUNK + 1, :]
              - dot_tn(loc["kcd"][hr(h)], dv_new[hr(h)], FAST) for h in range(N_HEADS)]
    is_last = (lax.broadcasted_iota(jnp.int32, (STACK, 1), 0) % CHUNK) == CHUNK - 1
    out = []
    for c, (q, k, v, beta, loc, s, do) in enumerate(items):
        decay, gamma, tau, cd, kb = loc["decay"], loc["gamma"], loc["tau"], loc["cd"], loc["kb"]
        dv_new, ds_new, dattn, dqd = dv_new_of[c], ds_new_of[c], pre[c]["dattn"], pre[c]["dqd"]
        dkt = _per_head(lambda h: dot_nt(loc["v_new"][hr(h)], ds_new[h], FAST))
        dkcd = -_per_head(lambda h: dot_nt(dv_new[hr(h)], s[h], FAST))
        drhs = dot_tn(loc["a_inv"], jnp.concatenate([dv_new, dkcd], axis=1), FAST)
        dvb, dkbg = drhs[:, :HEAD_DIM], drhs[:, HEAD_DIM:]
        dl = jnp.where(loc["strict"], -dot_nt(drhs, loc["sol"], FAST), 0.0)
        dkk = dl * decay
        dqk = dattn * decay
        e = dl * loc["l_mat"] + dattn * loc["attn"]
        dgc = jnp.sum(e, axis=1, keepdims=True) - jnp.sum(e, axis=0, keepdims=True).T
        dkb = dot_nn(dkk, k, FAST) + dkbg * gamma
        dk = dot_tn(dkk, kb, FAST) + dot_tn(dqk, q, FAST) + dkt * tau
        dq = dot_nn(dqk, k, FAST) + dqd * gamma
        dgamma = jnp.sum(dkbg * kb, axis=1, keepdims=True) + jnp.sum(dqd * q, axis=1, keepdims=True)
        dtau_tau = jnp.sum(dkt * k, axis=1, keepdims=True) * tau
        dgc = dgc + dgamma * gamma - dtau_tau

        def last_term(h):
            dcd = jnp.sum(jnp.sum(ds_new[h] * s[h], axis=1, keepdims=True), axis=0, keepdims=True)
            total = jnp.sum(dtau_tau[hr(h)], axis=0, keepdims=True) + dcd * cd[h * CHUNK:h * CHUNK + 1, :]
            return jnp.broadcast_to(total, (CHUNK, 1))

        dgc = dgc + jnp.where(is_last, _per_head(last_term), 0.0)
        dk = dk + dkb * beta
        dbeta = jnp.sum(dkb * k, axis=1, keepdims=True) + jnp.sum(dvb * v, axis=1, keepdims=True)
        out.append((dq, dk, dvb * beta, dbeta, dgc))
    return out, ds


def _token_tile(t):
    return _pick(t, 256)


STRIP = 32


def _for_strips(n_rows, rows, fn, start=0):
    def step(r, carry):
        fn(pl.multiple_of(r * rows, rows))
        return carry

    lax.fori_loop(start, n_rows // rows, step, 0)


def _fold_rows(x):
    out = x[0:SUBLANES, :]
    for i in range(1, x.shape[0] // SUBLANES):
        out = out + x[i * SUBLANES:(i + 1) * SUBLANES, :]
    return out


def _matmul(a, b, mode, name, tiles, add=None, out_dtype=F32):
    if mode == "nn":
        (m, k), n = a.shape, b.shape[1]
    elif mode == "nt":
        (m, k), n = a.shape, b.shape[0]
    else:
        (k, m), n = a.shape, b.shape[1]
    tm, tn, tk = min(tiles[0], m), min(tiles[1], n), min(tiles[2], k)
    assert m % tm == 0 and n % tn == 0 and k % tk == 0, (name, m, n, k, tiles)
    nk = k // tk
    dims = {"nn": (((1,), (0,)), ((), ())), "nt": (((1,), (1,)), ((), ())), "tn": (((0,), (0,)), ((), ()))}[mode]

    def finish(res, add_ref, o_ref):
        if add_ref is not None:
            res = res + add_ref[...]
        o_ref[...] = res.astype(o_ref.dtype)

    def body(*refs):
        a_ref, b_ref = refs[0], refs[1]
        add_ref = refs[2] if add is not None else None
        o_ref = refs[3] if add is not None else refs[2]
        part = lax.dot_general(a_ref[...], b_ref[...], dims, preferred_element_type=F32)
        if nk == 1:
            finish(part, add_ref, o_ref)
            return
        acc_ref = refs[-1]
        kk = pl.program_id(2)

        @pl.when(kk == 0)
        def _():
            acc_ref[...] = part

        @pl.when(kk > 0)
        def _():
            acc_ref[...] += part

        @pl.when(kk == nk - 1)
        def _():
            finish(acc_ref[...], add_ref, o_ref)

    a_spec = pl.BlockSpec((tk, tm), lambda j, i, kk: (kk, i)) if mode == "tn" else pl.BlockSpec((tm, tk), lambda j, i, kk: (i, kk))
    b_spec = pl.BlockSpec((tn, tk), lambda j, i, kk: (j, kk)) if mode == "nt" else pl.BlockSpec((tk, tn), lambda j, i, kk: (kk, j))
    o_spec = pl.BlockSpec((tm, tn), lambda j, i, kk: (i, j))
    in_specs = [a_spec, b_spec] + ([o_spec] if add is not None else [])
    args = (a, b) + ((add,) if add is not None else ())
    return _pcall(
        body, grid=(n // tn, m // tm, nk), in_specs=in_specs, out_specs=o_spec,
        out_shape=jax.ShapeDtypeStruct((m, n), out_dtype),
        scratch_shapes=[pltpu.VMEM((tm, tn), F32)] if nk > 1 else [],
        compiler_params=_params("parallel", "parallel", "arbitrary"), name=name)(*args)


def _matmul_rows(a, b, mode, name, tm, extra, outs, fn):
    m, k = a.shape
    n = b.shape[1] if mode == "nn" else b.shape[0]
    tm = min(tm, m)
    steps = m // tm
    dims = (((1,), (0,)), ((), ())) if mode == "nn" else (((1,), (1,)), ((), ()))

    def spec(shape, kind):
        if kind == "rows":
            return pl.BlockSpec((tm, shape[1]), lambda i: (jnp.maximum(i - 1, 0), 0))
        return pl.BlockSpec(shape, lambda i: (0,) * len(shape))

    def body(a_ref, b_ref, *refs):
        *io_refs, held_ref = refs
        i = pl.program_id(0)

        @pl.when(i == 0)
        def _():
            held_ref[...] = jnp.zeros_like(held_ref)

        held = held_ref[...]
        rows = lax.dot_general(a_ref[...], b_ref[...], dims, preferred_element_type=F32)
        fn(held, i == 0, (i > 0).astype(F32), *io_refs)
        held_ref[...] = rows

    return _pcall(
        body, grid=(steps + 1,),
        in_specs=[pl.BlockSpec((tm, k), lambda i: (jnp.minimum(i, steps - 1), 0)), pl.BlockSpec(b.shape, lambda i: (0, 0))]
        + [spec(x.shape, kind) for x, kind in extra],
        out_specs=[spec(shape, kind) for shape, _, kind in outs],
        out_shape=[jax.ShapeDtypeStruct(shape, dtype) for shape, dtype, _ in outs],
        scratch_shapes=[pltpu.VMEM((tm, n), F32)],
        compiler_params=_params("arbitrary"), name=name)(a, b, *[x for x, _ in extra])


def _rmsnorm_matmul(x, g, b_t, name, tm):
    t, d = x.shape
    n = b_t.shape[0]
    tm = min(tm, t)

    def body(x_ref, g_ref, b_ref, o_ref, h_ref, r_ref):
        y, r = rms_fwd(x_ref[...], g_ref[...])
        h = y.astype(BF16)
        h_ref[...] = h
        r_ref[...] = r
        o_ref[...] = lax.dot_general(h, b_ref[...], (((1,), (1,)), ((), ())), preferred_element_type=F32)

    rows = lambda w: pl.BlockSpec((tm, w), lambda i: (i, 0))
    return _pcall(
        body, grid=(t // tm,),
        in_specs=[rows(d), pl.BlockSpec((1, d), lambda i: (0, 0)), pl.BlockSpec((n, d), lambda i: (0, 0))],
        out_specs=[rows(n), rows(d), rows(1)],
        out_shape=[jax.ShapeDtypeStruct((t, n), F32), jax.ShapeDtypeStruct((t, d), BF16), jax.ShapeDtypeStruct((t, 1), F32)],
        compiler_params=_params("parallel"), name=name)(x, g, b_t)


def _rmsnorm_bwd_rows(t, d):
    def fn(dh, first, valid, x_ref, r_ref, g_ref, dres_ref, dx_ref, dxb_ref, dg_ref):
        del valid
        dx, dg_rows = rms_bwd(x_ref[...], r_ref[...], g_ref[...], dh)
        dx = dx + dres_ref[...]
        dx_ref[...] = dx
        dxb_ref[...] = dx.astype(BF16)

        @pl.when(first)
        def _():
            dg_ref[...] = jnp.zeros_like(dg_ref)

        dg_ref[...] += jnp.sum(dg_rows, axis=0, keepdims=True)

    return fn, [((t, d), F32, "rows"), ((t, d), BF16, "rows"), ((1, d), F32, "whole")]


def _final_loss_rows(t, d):
    def fn(rows, first, valid, res_ref, t_ref, g_ref, loss_ref, dx_ref, dxb_ref, dg_ref):
        @pl.when(first)
        def _():
            loss_ref[...] = jnp.zeros_like(loss_ref)
            dg_ref[...] = jnp.zeros_like(dg_ref)

        x = rows + res_ref[...]
        y, r = rms_fwd(x, g_ref[...])
        err = y - t_ref[...]
        loss_ref[...] += valid * 0.5 * jnp.sum(jnp.mean(err * err, axis=-1, keepdims=True), axis=0, keepdims=True)
        dx, dg_rows = rms_bwd(x, r, g_ref[...], err * (1.0 / d))
        dx_ref[...] = dx
        dxb_ref[...] = dx.astype(BF16)
        dg_ref[...] += valid * jnp.sum(dg_rows, axis=0, keepdims=True)

    return fn, [((1, LANES), F32, "whole"), ((t, d), F32, "rows"), ((t, d), BF16, "rows"), ((1, d), F32, "whole")]


def _prev_halo_spec(tm, width, col_block):
    return pl.BlockSpec((HALO, width), lambda i: (jnp.maximum(i * (tm // HALO) - 1, 0), col_block))


def _history(tile_ref, halo_ref, first, row0, cols):
    if isinstance(row0, int) and row0 == 0:
        return jnp.concatenate([jnp.where(first, 0.0, halo_ref[:, cols]), tile_ref[0:STRIP, cols]], axis=0)
    return tile_ref[pl.ds(pl.multiple_of(row0 - HALO, HALO), STRIP + HALO), cols]


def _first_then_strips(n_rows, fn):
    fn(0)
    _for_strips(n_rows, STRIP, fn, start=1)


def _delays(ext, taps):
    return [ext[HALO:, :]] + [pltpu.roll(ext, j, 0)[HALO:, :] for j in range(1, taps)]


def _causal_conv(delayed, w):
    taps = len(delayed)
    out = delayed[0] * w[taps - 1:taps, :]
    for j in range(1, taps):
        out = out + delayed[j] * w[taps - 1 - j:taps - j, :]
    return out


def _advanced_conv(buf_ref, row0, cols, w):
    return _advanced(buf_ref[pl.ds(row0, STRIP + HALO), cols], w)


def _advanced(ext, w):
    taps = w.shape[0]
    out = ext[:STRIP, :] * w[taps - 1:taps, :]
    for j in range(1, taps):
        out = out + pltpu.roll(ext, STRIP + HALO - j, 0)[:STRIP, :] * w[taps - 1 - j:taps - j, :]
    return out


def _dn_prep(p, conv_w, a_log4, dt_bias4):
    t = p.shape[0]
    tm = _token_tile(t)
    w3 = 3 * DN_WIDTH

    def body(x_ref, halo_ref, pbd_ref, w_ref, alog_ref, dtb_ref, q_ref, k_ref, v_ref, beta_ref, g_ref):
        first = pl.program_id(0) == 0

        def strip(row0):
            rows = pl.ds(row0, STRIP)
            for h in range(N_HEADS):
                sl = slice(h * HEAD_DIM, (h + 1) * HEAD_DIM)
                for part, out_ref in ((0, q_ref), (1, k_ref), (2, v_ref)):
                    cols = slice(part * DN_WIDTH + h * HEAD_DIM, part * DN_WIDTH + (h + 1) * HEAD_DIM)
                    y = silu(_causal_conv(_delays(_history(x_ref, halo_ref, first, row0, cols), CONV_K), w_ref[:, cols]))
                    if part == 0:
                        y = l2_fwd(y)[0] * (HEAD_DIM ** -0.5)
                    elif part == 1:
                        y = l2_fwd(y)[0]
                    out_ref[rows, sl] = y
            head = lax.broadcasted_iota(jnp.int32, (STRIP, LANES), 1) < N_HEADS
            pbd = pbd_ref[rows, :]
            beta_ref[rows, :] = jnp.where(head, sigmoid(pbd), 0.0)
            a_raw = pltpu.roll(pbd, LANES - N_HEADS, 1)
            g_ref[rows, :] = jnp.where(head, -jnp.exp(alog_ref[...]) * softplus(a_raw + dtb_ref[...]), 0.0)

        _first_then_strips(tm, strip)

    tok = lambda w, cb: pl.BlockSpec((tm, w), lambda i: (i, cb))
    full = lambda a: pl.BlockSpec(a.shape, lambda i: (0, 0))
    return _pcall(
        body, grid=(t // tm,),
        in_specs=[tok(w3, 0), _prev_halo_spec(tm, w3, 0), tok(LANES, PROJ_MAIN // LANES),
                  full(conv_w), full(a_log4), full(dt_bias4)],
        out_specs=[tok(DN_WIDTH, 0)] * 3 + [tok(LANES, 0)] * 2,
        out_shape=[jax.ShapeDtypeStruct((t, DN_WIDTH), F32)] * 3 + [jax.ShapeDtypeStruct((t, LANES), F32)] * 2,
        compiler_params=_params("parallel"), name="dn_prep")(p, p, p, conv_w, a_log4, dt_bias4)


def _dn_prep_bwd(p, conv_w, a_log4, dt_bias4, dq, dk, dv, dbeta4, dg4, dp_buf):
    t = p.shape[0]
    tm = _token_tile(t)
    w3 = 3 * DN_WIDTH

    def body(x_ref, halo_ref, pbd_ref, w_ref, alog_ref, dtb_ref, dq_ref, dk_ref, dv_ref, dbeta_ref, dg_ref, _,
             dc_ref, dw_ref, dpbd_ref, dalog_ref, ddtb_ref, dw_acc, lane_acc):
        first = pl.program_id(0) == 0
        dw_acc[...] = jnp.zeros_like(dw_acc)
        lane_acc[...] = jnp.zeros_like(lane_acc)

        def strip(row0):
            rows = pl.ds(row0, STRIP)
            for h in range(N_HEADS):
                sl = slice(h * HEAD_DIM, (h + 1) * HEAD_DIM)
                for part, dy_ref in ((0, dq_ref), (1, dk_ref), (2, dv_ref)):
                    cols = slice(part * DN_WIDTH + h * HEAD_DIM, part * DN_WIDTH + (h + 1) * HEAD_DIM)
                    delayed = _delays(_history(x_ref, halo_ref, first, row0, cols), CONV_K)
                    c = _causal_conv(delayed, w_ref[:, cols])
                    dy = dy_ref[rows, sl]
                    if part < 2:
                        y = silu(c)
                        _, r = l2_fwd(y)
                        dy = l2_bwd(y, r, dy * (HEAD_DIM ** -0.5) if part == 0 else dy)
                    dc = dy * silu_grad(c)
                    dc_ref[rows, cols] = dc
                    for j in range(CONV_K):
                        k = CONV_K - 1 - j
                        dw_acc[k * SUBLANES:(k + 1) * SUBLANES, cols] += _fold_rows(dc * delayed[j])
            head = lax.broadcasted_iota(jnp.int32, (STRIP, LANES), 1) < N_HEADS
            pbd = pbd_ref[rows, :]
            beta = sigmoid(pbd)
            dpb = jnp.where(head, dbeta_ref[rows, :] * beta * (1.0 - beta), 0.0)
            z = pltpu.roll(pbd, LANES - N_HEADS, 1) + dtb_ref[...]
            neg_rate = -jnp.exp(alog_ref[...])
            dg = dg_ref[rows, :]
            dpa = jnp.where(head, dg * neg_rate * sigmoid(z), 0.0)
            dpbd_ref[rows, :] = (dpb + pltpu.roll(dpa, N_HEADS, 1)).astype(BF16)
            g = jnp.where(head, neg_rate * softplus(z), 0.0)
            lane_acc[0:SUBLANES, :] += _fold_rows(dg * g)
            lane_acc[SUBLANES:, :] += _fold_rows(dpa)

        _first_then_strips(tm, strip)

        @pl.when(first)
        def _():
            dw_ref[...] = jnp.zeros_like(dw_ref)
            dalog_ref[...] = jnp.zeros_like(dalog_ref)
            ddtb_ref[...] = jnp.zeros_like(ddtb_ref)

        for k in range(CONV_K):
            dw_ref[k:k + 1, :] += jnp.sum(dw_acc[k * SUBLANES:(k + 1) * SUBLANES, :], axis=0, keepdims=True)
        dalog_ref[...] += jnp.sum(lane_acc[0:SUBLANES, :], axis=0, keepdims=True)
        ddtb_ref[...] += jnp.sum(lane_acc[SUBLANES:, :], axis=0, keepdims=True)

    tok = lambda w, cb: pl.BlockSpec((tm, w), lambda i: (i, cb))
    full = lambda shape: pl.BlockSpec(shape, lambda i: (0, 0))
    return _pcall(
        body, grid=(t // tm,),
        in_specs=[tok(w3, 0), _prev_halo_spec(tm, w3, 0), tok(LANES, PROJ_MAIN // LANES),
                  full(conv_w.shape), full(a_log4.shape), full(dt_bias4.shape)] + [tok(DN_WIDTH, 0)] * 3 + [tok(LANES, 0)] * 2
        + [pl.BlockSpec(memory_space=pl.ANY)],
        out_specs=[tok(w3, 0), full((CONV_K, w3)), tok(LANES, PROJ_MAIN // LANES), full((1, LANES)), full((1, LANES))],
        out_shape=[jax.ShapeDtypeStruct((t, w3), F32), jax.ShapeDtypeStruct((CONV_K, w3), F32),
                   jax.ShapeDtypeStruct(dp_buf.shape, dp_buf.dtype),
                   jax.ShapeDtypeStruct((1, LANES), F32), jax.ShapeDtypeStruct((1, LANES), F32)],
        input_output_aliases={11: 2},
        scratch_shapes=[pltpu.VMEM((CONV_K * SUBLANES, w3), F32), pltpu.VMEM((2 * SUBLANES, LANES), F32)],
        compiler_params=_params("arbitrary"), name="dn_prep_bwd")(p, p, p, conv_w, a_log4, dt_bias4, dq, dk, dv, dbeta4, dg4, dp_buf)


def _conv_bwd_input(dc, w, name, out_cols=None, col_block=0, into=None):
    t, c = dc.shape
    taps = w.shape[0]
    tm = _token_tile(t)
    ct = _pick(c, 1536)
    n_tok = t // tm
    out_cols = c if out_cols is None else out_cols

    def body(dc_ref, next_ref, w_ref, *rest):
        dx_ref = rest[-1]
        last = pl.program_id(0) == n_tok - 1

        def strip(row0):
            for c0 in range(0, ct, LANES):
                cols = slice(c0, c0 + LANES)
                dx_ref[pl.ds(row0, STRIP), cols] = _advanced_conv(dc_ref, row0, cols, w_ref[:, cols]).astype(BF16)

        _for_strips(tm - STRIP, STRIP, strip)
        for c0 in range(0, ct, LANES):
            cols = slice(c0, c0 + LANES)
            ext = jnp.concatenate([dc_ref[tm - STRIP:tm, cols], jnp.where(last, 0.0, next_ref[:, cols])], axis=0)
            dx_ref[tm - STRIP:tm, cols] = _advanced(ext, w_ref[:, cols]).astype(BF16)

    in_specs = [pl.BlockSpec((tm, ct), lambda i, j: (i, j)),
                pl.BlockSpec((HALO, ct), lambda i, j: (jnp.minimum((i + 1) * (tm // HALO), t // HALO - 1), j)),
                pl.BlockSpec((taps, ct), lambda i, j: (0, j))]
    args = (dc, dc, w)
    aliases = {}
    if into is not None:
        in_specs.append(pl.BlockSpec(memory_space=pl.ANY))
        args += (into,)
        aliases = {3: 0}
    return _pcall(
        body, grid=(n_tok, c // ct), in_specs=in_specs,
        out_specs=pl.BlockSpec((tm, ct), lambda i, j: (i, j + col_block)),
        out_shape=jax.ShapeDtypeStruct((t, out_cols), BF16), input_output_aliases=aliases,
        compiler_params=_params("parallel", "parallel"), name=name)(*args)


def _dn_forward(q, k, v, beta4, g4, p, norm_g):
    t = q.shape[0]
    n = t // CHUNK
    nc = DN_FWD_CHUNKS
    rows_per_step = nc * CHUNK

    def body(q_ref, k_ref, v_ref, b_ref, g_ref, gate_ref, ng_ref, mix_ref, s_all_ref, ainv_ref, s_ref):
        @pl.when(pl.program_id(0) == 0)
        def _():
            s_ref[...] = jnp.zeros_like(s_ref)

        chunks = []
        for c in range(nc):
            rows = slice(c * CHUNK, (c + 1) * CHUNK)
            chunks.append((_stack_heads(q_ref[rows, :]), _stack_heads(k_ref[rows, :]), _stack_heads(v_ref[rows, :]),
                           _stack_lanes(b_ref[rows, :]), chunk_cumsum(g_ref[rows, :])))
        locs = dn_chunks_local(chunks)
        s = [s_ref[h] for h in range(N_HEADS)]
        for c in range(nc):
            rows = slice(c * CHUNK, (c + 1) * CHUNK)
            ainv_ref[c] = locs[c]["a_inv"].astype(BF16)
            for h in range(N_HEADS):
                s_all_ref[c, h] = s[h]
            o, s = dn_chunk_state(locs[c], s)
            o_n, _ = rms_fwd(o, ng_ref[...])
            for h in range(N_HEADS):
                sl = slice(h * HEAD_DIM, (h + 1) * HEAD_DIM)
                mix_ref[rows, sl] = (o_n[_head_rows(h)] * silu(gate_ref[rows, sl])).astype(BF16)
        for h in range(N_HEADS):
            s_ref[h] = s[h]

    ch = lambda w, cb: pl.BlockSpec((rows_per_step, w), lambda i: (i, cb))
    per_chunk = lambda *shape: pl.BlockSpec((nc,) + shape, lambda i: (i,) + (0,) * len(shape))
    return _pcall(
        body, grid=(n // nc,),
        in_specs=[ch(DN_WIDTH, 0)] * 3 + [ch(LANES, 0)] * 2 + [ch(DN_WIDTH, 3), pl.BlockSpec((1, HEAD_DIM), lambda i: (0, 0))],
        out_specs=[ch(DN_WIDTH, 0), per_chunk(N_HEADS, HEAD_DIM, HEAD_DIM), per_chunk(STACK, STACK)],
        out_shape=[jax.ShapeDtypeStruct((t, DN_WIDTH + SG_WIDTH), BF16), jax.ShapeDtypeStruct((n, N_HEADS, HEAD_DIM, HEAD_DIM), F32),
                   jax.ShapeDtypeStruct((n, STACK, STACK), BF16)],
        scratch_shapes=[pltpu.VMEM((N_HEADS, HEAD_DIM, HEAD_DIM), F32)],
        compiler_params=_params("arbitrary"), name="dn_forward")(q, k, v, beta4, g4, p, norm_g)


def _dn_backward(q, k, v, beta4, g4, p, norm_g, saved, dmix, dp_buf):
    t = q.shape[0]
    n = t // CHUNK
    steps = n // DN_CHUNKS
    rows_per_step = DN_CHUNKS * CHUNK

    def body(q_ref, k_ref, v_ref, b_ref, g_ref, gate_ref, ng_ref, s_in_ref, ainv_ref, dmix_ref, _,
             dq_ref, dk_ref, dv_ref, db_ref, dg_ref, dgate_ref, dng_ref, ds_ref):
        @pl.when(pl.program_id(0) == 0)
        def _():
            ds_ref[...] = jnp.zeros_like(ds_ref)
            dng_ref[...] = jnp.zeros_like(dng_ref)

        chunks = []
        for c in range(DN_CHUNKS):
            rows = slice(c * CHUNK, (c + 1) * CHUNK)
            chunks.append((_stack_heads(q_ref[rows, :]), _stack_heads(k_ref[rows, :]), _stack_heads(v_ref[rows, :]),
                           _stack_lanes(b_ref[rows, :]), chunk_cumsum(g_ref[rows, :])))
        items = []
        for c, loc in enumerate(dn_chunks_local(chunks, [ainv_ref[c] for c in range(DN_CHUNKS)])):
            rows = slice(c * CHUNK, (c + 1) * CHUNK)
            s = [s_in_ref[c, h] for h in range(N_HEADS)]
            o, _ = dn_chunk_state(loc, s)
            o_n, r = rms_fwd(o, ng_ref[...])
            gate = _stack_heads(gate_ref[rows, :])
            dmx = _stack_heads(dmix_ref[rows, :])
            dgate = dmx * o_n * silu_grad(gate)
            do, dng_rows = rms_bwd(o, r, ng_ref[...], dmx * silu(gate))
            dng_ref[...] += jnp.sum(dng_rows, axis=0, keepdims=True)
            for h in range(N_HEADS):
                dgate_ref[rows, h * HEAD_DIM:(h + 1) * HEAD_DIM] = dgate[_head_rows(h)].astype(BF16)
            items.append((*chunks[c][:4], loc, s, do))
        grads, ds = dn_chunks_bwd(items, [ds_ref[h] for h in range(N_HEADS)])
        lane = lax.broadcasted_iota(jnp.int32, (CHUNK, LANES), 1)
        _, strict = _tri_masks(CHUNK)
        for c in range(DN_CHUNKS):
            rows = slice(c * CHUNK, (c + 1) * CHUNK)
            dq, dk, dv, dbeta, dgc = grads[c]
            db4 = jnp.zeros((CHUNK, LANES), F32)
            dgc4 = jnp.zeros((CHUNK, LANES), F32)
            for h in range(N_HEADS):
                sl = slice(h * HEAD_DIM, (h + 1) * HEAD_DIM)
                head_rows = _head_rows(h)
                dq_ref[rows, sl] = dq[head_rows]
                dk_ref[rows, sl] = dk[head_rows]
                dv_ref[rows, sl] = dv[head_rows]
                db4 = jnp.where(lane == h, dbeta[head_rows], db4)
                dgc4 = jnp.where(lane == h, dgc[head_rows], dgc4)
            db_ref[rows, :] = db4
            dg_ref[rows, :] = dot_nn(jnp.logical_not(strict).astype(F32), dgc4)
        for h in range(N_HEADS):
            ds_ref[h] = ds[h]

    rev = lambda w, cb: pl.BlockSpec((rows_per_step, w), lambda i: (steps - 1 - i, cb))
    per_chunk = lambda a: pl.BlockSpec((DN_CHUNKS,) + a.shape[1:], lambda i: (steps - 1 - i,) + (0,) * (a.ndim - 1))
    return _pcall(
        body, grid=(steps,),
        in_specs=[rev(DN_WIDTH, 0)] * 3 + [rev(LANES, 0)] * 2 + [rev(DN_WIDTH, 3), pl.BlockSpec((1, HEAD_DIM), lambda i: (0, 0))]
        + [per_chunk(a) for a in saved] + [rev(DN_WIDTH, 0), pl.BlockSpec(memory_space=pl.ANY)],
        out_specs=[rev(DN_WIDTH, 0)] * 3 + [rev(LANES, 0)] * 2 + [rev(DN_WIDTH, 3), pl.BlockSpec((1, HEAD_DIM), lambda i: (0, 0))],
        out_shape=[jax.ShapeDtypeStruct((t, DN_WIDTH), F32)] * 3 + [jax.ShapeDtypeStruct((t, LANES), F32)] * 2
        + [jax.ShapeDtypeStruct(dp_buf.shape, dp_buf.dtype), jax.ShapeDtypeStruct((1, HEAD_DIM), F32)],
        input_output_aliases={10: 5},
        scratch_shapes=[pltpu.VMEM((N_HEADS, HEAD_DIM, HEAD_DIM), F32)],
        compiler_params=_params("arbitrary"), name="dn_backward")(q, k, v, beta4, g4, p, norm_g, *saved, dmix, dp_buf)


SG_STEP_BLOCKS = 4


def _sg_mask():
    row = lax.broadcasted_iota(jnp.int32, (SG_BLOCK, SG_BLOCK), 0)
    col = lax.broadcasted_iota(jnp.int32, (SG_BLOCK, SG_BLOCK), 1)
    return (col // CHUNK) <= (row // CHUNK)


def _sg_forward(p, norm_g, w_s, b_t, mix_buf):
    t = p.shape[0]
    step_rows = SG_STEP_BLOCKS * SG_BLOCK

    def body(u_ref, v_ref, ng_ref, w_ref, b_ref, _, o_ref):
        mask = _sg_mask()
        pairs = [(slice(b * SG_BLOCK, (b + 1) * SG_BLOCK), g, slice(g * SG_DIM, (g + 1) * SG_DIM))
                 for b in range(SG_STEP_BLOCKS) for g in range(SG_GROUPS)]
        w_m = [jnp.where(mask, w_ref[g], 0.0) for g in range(SG_GROUPS)]
        vn = [rms_fwd(gelu(v_ref[rows, sl]), ng_ref[:, sl])[0] for rows, g, sl in pairs]
        s = [dot_nn(w_m[g], vn[i], FAST) + b_ref[:, g:g + 1] for i, (rows, g, sl) in enumerate(pairs)]
        for i, (rows, g, sl) in enumerate(pairs):
            o_ref[rows, sl] = (gelu(u_ref[rows, sl]) * s[i]).astype(BF16)

    blk = lambda cb: pl.BlockSpec((step_rows, SG_WIDTH), lambda i: (i, cb))
    return _pcall(
        body, grid=(t // step_rows,),
        in_specs=[blk(4), blk(5), pl.BlockSpec((1, SG_WIDTH), lambda i: (0, 0)),
                  pl.BlockSpec((SG_GROUPS, SG_BLOCK, SG_BLOCK), lambda i: (0, 0, 0)), pl.BlockSpec((SG_BLOCK, SG_GROUPS), lambda i: (0, 0)),
                  pl.BlockSpec(memory_space=pl.ANY)],
        out_specs=blk(1), out_shape=jax.ShapeDtypeStruct(mix_buf.shape, mix_buf.dtype), input_output_aliases={5: 0},
        compiler_params=_params("parallel"), name="sg_forward")(p, p, norm_g, w_s, b_t, mix_buf)


def _sg_backward(p, norm_g, w_s, b_t, dmix):
    t = p.shape[0]

    def body(u_ref, v_ref, ng_ref, w_ref, b_ref, do_ref, duv_ref, dng_ref, dw_ref, db_ref):
        @pl.when(pl.program_id(0) == 0)
        def _():
            dng_ref[...] = jnp.zeros_like(dng_ref)
            dw_ref[...] = jnp.zeros_like(dw_ref)
            db_ref[...] = jnp.zeros_like(db_ref)

        mask = _sg_mask()
        lane = lax.broadcasted_iota(jnp.int32, (SG_BLOCK, LANES), 1)
        pairs = [(slice(b * SG_BLOCK, (b + 1) * SG_BLOCK), g, slice(g * SG_DIM, (g + 1) * SG_DIM))
                 for b in range(SG_STEP_BLOCKS) for g in range(SG_GROUPS)]
        w_m = [jnp.where(mask, w_ref[g], 0.0) for g in range(SG_GROUPS)]
        vg = [gelu(v_ref[rows, sl]) for rows, g, sl in pairs]
        normed = [rms_fwd(vg[i], ng_ref[:, sl]) for i, (rows, g, sl) in enumerate(pairs)]
        s = [dot_nn(w_m[g], normed[i][0], FAST) + b_ref[:, g:g + 1] for i, (rows, g, sl) in enumerate(pairs)]
        ds = []
        db = jnp.zeros((SG_BLOCK, LANES), F32)
        for i, (rows, g, sl) in enumerate(pairs):
            u_raw, do = u_ref[rows, sl], do_ref[rows, sl]
            duv_ref[rows, sl] = (do * s[i] * gelu_grad(u_raw)).astype(BF16)
            ds.append(do * gelu(u_raw))
            db = db + jnp.where(lane == g, jnp.sum(ds[i], axis=1, keepdims=True), 0.0)
        dw = [jnp.where(mask, dot_nt(ds[i], normed[i][0], FAST), 0.0) for i in range(len(pairs))]
        dvn = [dot_tn(w_m[g], ds[i], FAST) for i, (rows, g, sl) in enumerate(pairs)]
        for i, (rows, g, sl) in enumerate(pairs):
            dw_ref[g] += dw[i]
            dvg, dng_rows = rms_bwd(vg[i], normed[i][1], ng_ref[:, sl], dvn[i])
            dng_ref[:, sl] += jnp.sum(dng_rows, axis=0, keepdims=True)
            duv_ref[rows, SG_WIDTH + g * SG_DIM:SG_WIDTH + (g + 1) * SG_DIM] = (dvg * gelu_grad(v_ref[rows, sl])).astype(BF16)
        db_ref[...] += db

    step_rows = SG_STEP_BLOCKS * SG_BLOCK
    blk = lambda cb: pl.BlockSpec((step_rows, SG_WIDTH), lambda i: (i, cb))
    const2 = lambda shape: pl.BlockSpec(shape, lambda i: (0, 0))
    w_spec = pl.BlockSpec((SG_GROUPS, SG_BLOCK, SG_BLOCK), lambda i: (0, 0, 0))
    return _pcall(
        body, grid=(t // step_rows,),
        in_specs=[blk(4), blk(5), const2((1, SG_WIDTH)), w_spec, const2((SG_BLOCK, SG_GROUPS)), blk(1)],
        out_specs=[pl.BlockSpec((step_rows, 2 * SG_WIDTH), lambda i: (i, 2)), const2((1, SG_WIDTH)), w_spec,
                   const2((SG_BLOCK, LANES))],
        out_shape=[jax.ShapeDtypeStruct((t, PROJ_PAD), BF16), jax.ShapeDtypeStruct((1, SG_WIDTH), F32),
                   jax.ShapeDtypeStruct((SG_GROUPS, SG_BLOCK, SG_BLOCK), F32), jax.ShapeDtypeStruct((SG_BLOCK, LANES), F32)],
        compiler_params=_params("arbitrary"), name="sg_backward")(p, p, norm_g, w_s, b_t, dmix)


FFN_COLS = 256


def _norm_up_ffn(x, g, w_up_t, conv_w, conv_b):
    t, d = x.shape
    tm = min(t, 256)
    blocks = D_FF // FFN_COLS
    nt = (((1,), (1,)), ((), ()))

    def body(x_ref, g_ref, w_ref, cw_ref, cb_ref, up_ref, act_ref, h_ref, r_ref, tail_ref, prev_ref):
        @pl.when(pl.program_id(0) == 0)
        def _():
            tail_ref[...] = jnp.zeros_like(tail_ref)

        y, r = rms_fwd(x_ref[...], g_ref[...])
        h = y.astype(BF16)
        h_ref[...] = h
        r_ref[...] = r

        def project(blk):
            out = []
            for half in range(2):
                cols = slice(half * D_FF + blk * FFN_COLS, half * D_FF + (blk + 1) * FFN_COLS)
                u = lax.dot_general(h, w_ref[cols, :], nt, preferred_element_type=F32)
                up_ref[:, cols] = u
                prev_ref[:, cols] = tail_ref[:, cols]
                tail_ref[:, cols] = u[tm - HALO:, :]
                out.append(cols)
            return out

        def history(row0, cols):
            if row0 == 0:
                return jnp.concatenate([prev_ref[:, cols], up_ref[0:STRIP, cols]], axis=0)
            return up_ref[row0 - HALO:row0 + STRIP, cols]

        def activate(blk, g_cols, v_cols):
            for row0 in range(0, tm, STRIP):
                for c0 in range(0, FFN_COLS, LANES):
                    gc = slice(g_cols.start + c0, g_cols.start + c0 + LANES)
                    vc = slice(v_cols.start + c0, v_cols.start + c0 + LANES)
                    cg = _causal_conv(_delays(history(row0, gc), FFN_CONV), cw_ref[:, gc]) + cb_ref[:, gc]
                    cv = _causal_conv(_delays(history(row0, vc), FFN_CONV), cw_ref[:, vc]) + cb_ref[:, vc]
                    act_ref[row0:row0 + STRIP, blk * FFN_COLS + c0:blk * FFN_COLS + c0 + LANES] = (silu(cg) * cv).astype(BF16)

        pending = None
        for blk in range(blocks):
            cols = project(blk)
            if pending is not None:
                activate(*pending)
            pending = (blk, *cols)
        activate(*pending)

    rows = lambda w: pl.BlockSpec((tm, w), lambda i: (i, 0))
    whole = lambda a: pl.BlockSpec(a.shape, lambda i: (0, 0))
    return _pcall(
        body, grid=(t // tm,),
        in_specs=[rows(d), whole(g), whole(w_up_t), whole(conv_w), whole(conv_b)],
        out_specs=[rows(2 * D_FF), rows(D_FF), rows(d), rows(1)],
        out_shape=[jax.ShapeDtypeStruct((t, 2 * D_FF), F32), jax.ShapeDtypeStruct((t, D_FF), BF16),
                   jax.ShapeDtypeStruct((t, d), BF16), jax.ShapeDtypeStruct((t, 1), F32)],
        scratch_shapes=[pltpu.VMEM((HALO, 2 * D_FF), F32), pltpu.VMEM((HALO, 2 * D_FF), F32)],
        compiler_params=_params("arbitrary"), name="norm_up_ffn")(x, g, w_up_t, conv_w, conv_b)


def _ffn_bwd(up, conv_w, conv_b, dact):
    t = up.shape[0]
    tm = _pick(t, 256)
    n_tok = t // tm
    width = 2 * D_FF

    def dconv(delayed_g, delayed_v, da, wg, wv, bg, bv):
        cg = _causal_conv(delayed_g, wg) + bg
        cv = _causal_conv(delayed_v, wv) + bv
        s = sigmoid(cg)
        return da * cv * (s * (1.0 + cg * (1.0 - s))), da * (cg * s)

    def body(up_ref, prev_ref, next_ref, da_ref, dan_ref, w_ref, b_ref, dup_ref, dw_ref, db_ref, dc_ref, dw_acc, db_acc):
        first = pl.program_id(0) == 0
        last = pl.program_id(0) == n_tok - 1
        dw_acc[...] = jnp.zeros_like(dw_acc)
        db_acc[...] = jnp.zeros_like(db_acc)

        def strip(row0):
            rows = pl.ds(row0, STRIP)
            for c0 in range(0, D_FF, LANES):
                gc, vc = slice(c0, c0 + LANES), slice(D_FF + c0, D_FF + c0 + LANES)
                del_g = _delays(_history(up_ref, prev_ref, first, row0, gc), FFN_CONV)
                del_v = _delays(_history(up_ref, prev_ref, first, row0, vc), FFN_CONV)
                dcg, dcv = dconv(del_g, del_v, da_ref[rows, gc], w_ref[:, gc], w_ref[:, vc], b_ref[:, gc], b_ref[:, vc])
                dc_ref[rows, gc] = dcg
                dc_ref[rows, vc] = dcv
                db_acc[:, gc] += _fold_rows(dcg)
                db_acc[:, vc] += _fold_rows(dcv)
                for j in range(FFN_CONV):
                    k = FFN_CONV - 1 - j
                    dw_acc[k * SUBLANES:(k + 1) * SUBLANES, gc] += _fold_rows(dcg * del_g[j])
                    dw_acc[k * SUBLANES:(k + 1) * SUBLANES, vc] += _fold_rows(dcv * del_v[j])

        _first_then_strips(tm, strip)

        for c0 in range(0, D_FF, LANES):
            gc, vc = slice(c0, c0 + LANES), slice(D_FF + c0, D_FF + c0 + LANES)

            def delayed(cols):
                return _delays(jnp.concatenate([up_ref[tm - HALO:tm, cols], next_ref[:, cols]], axis=0), FFN_CONV)

            dcg, dcv = dconv(delayed(gc), delayed(vc), dan_ref[:, gc], w_ref[:, gc], w_ref[:, vc], b_ref[:, gc], b_ref[:, vc])
            dc_ref[tm:, gc] = jnp.where(last, 0.0, dcg)
            dc_ref[tm:, vc] = jnp.where(last, 0.0, dcv)

        def strip_dx(row0):
            for c0 in range(0, width, LANES):
                cols = slice(c0, c0 + LANES)
                dup_ref[pl.ds(row0, STRIP), cols] = _advanced_conv(dc_ref, row0, cols, w_ref[:, cols]).astype(BF16)

        _for_strips(tm, STRIP, strip_dx)

        @pl.when(first)
        def _():
            dw_ref[...] = jnp.zeros_like(dw_ref)
            db_ref[...] = jnp.zeros_like(db_ref)

        for k in range(FFN_CONV):
            dw_ref[k:k + 1, :] += jnp.sum(dw_acc[k * SUBLANES:(k + 1) * SUBLANES, :], axis=0, keepdims=True)
        db_ref[...] += jnp.sum(db_acc[...], axis=0, keepdims=True)

    next_rows = lambda i: jnp.minimum((i + 1) * (tm // HALO), t // HALO - 1)
    full = lambda rows: pl.BlockSpec((rows, width), lambda i: (0, 0))
    return _pcall(
        body, grid=(n_tok,),
        in_specs=[pl.BlockSpec((tm, width), lambda i: (i, 0)),
                  pl.BlockSpec((HALO, width), lambda i: (jnp.maximum(i * (tm // HALO) - 1, 0), 0)),
                  pl.BlockSpec((HALO, width), lambda i: (next_rows(i), 0)),
                  pl.BlockSpec((tm, D_FF), lambda i: (i, 0)), pl.BlockSpec((HALO, D_FF), lambda i: (next_rows(i), 0)),
                  full(FFN_CONV), full(1)],
        out_specs=[pl.BlockSpec((tm, width), lambda i: (i, 0)), full(FFN_CONV), full(1)],
        out_shape=[jax.ShapeDtypeStruct((t, width), BF16), jax.ShapeDtypeStruct((FFN_CONV, width), F32),
                   jax.ShapeDtypeStruct((1, width), F32)],
        scratch_shapes=[pltpu.VMEM((tm + HALO, width), F32),
                        pltpu.VMEM((FFN_CONV * SUBLANES, width), F32), pltpu.VMEM((SUBLANES, width), F32)],
        compiler_params=_params("arbitrary"), name="ffn_bwd")(up, up, up, dact, dact, conv_w, conv_b)


def _my_position():
    return lax.axis_index("x"), lax.axis_index("y"), lax.axis_index("c")


COPIES = N_DEV - 1


def _all_gather(arrays):
    n = len(arrays)

    def body(*refs):
        x_refs, out_refs = refs[:n], refs[n:2 * n]
        send_sems, recv_sems, local_sems = refs[2 * n:]
        x, y, cc = _my_position()
        me, sibling = (x, y, cc), (x, y, 1 - cc)
        chips = [(1 - x, y), (x, 1 - y), (1 - x, 1 - y)]

        def block(a, px, py, pc):
            return out_refs[a].at[4 * px + 2 * py + pc]

        def copy(a, k, blk, to, src=None):
            return pltpu.make_async_remote_copy(
                src_ref=block(a, *blk) if src is None else src, dst_ref=block(a, *blk),
                send_sem=send_sems.at[a * COPIES + k], recv_sem=recv_sems.at[a * COPIES + k],
                device_id=to, device_id_type=MESH_ID)

        mine = [pltpu.make_async_copy(x_refs[a], block(a, *me), local_sems.at[a]) for a in range(n)]
        for cp in mine:
            cp.start()
        first = []
        for a in range(n):
            first.append(copy(a, 0, me, sibling, src=x_refs[a]))
            first += [copy(a, 1 + j, me, (*chip, cc), src=x_refs[a]) for j, chip in enumerate(chips)]
        for cp in first:
            cp.start()
        passed = []
        for j, chip in enumerate(chips):
            for a in range(n):
                copy(a, 1 + j, (*chip, cc), me).wait_recv()
                passed.append(copy(a, 4 + j, (*chip, cc), sibling))
                passed[-1].start()
        for a in range(n):
            copy(a, 0, sibling, me).wait_recv()
        for j, chip in enumerate(chips):
            for a in range(n):
                copy(a, 4 + j, (*chip, 1 - cc), me).wait_recv()
        for cp in first + passed:
            cp.wait_send()
        for cp in mine:
            cp.wait()

    any_spec = pl.BlockSpec(memory_space=pl.ANY)
    return _pcall(
        body, out_shape=[jax.ShapeDtypeStruct((N_DEV,) + a.shape, a.dtype) for a in arrays],
        in_specs=[any_spec] * n, out_specs=[any_spec] * n,
        scratch_shapes=[pltpu.SemaphoreType.DMA((n * COPIES,)), pltpu.SemaphoreType.DMA((n * COPIES,)),
                        pltpu.SemaphoreType.DMA((n,))],
        name="all_gather")(*arrays)


def _all_to_all(sends):
    n = len(sends)

    def body(*refs):
        send_refs, recv_refs = refs[:n], refs[n:2 * n]
        send_sems, recv_sems, local_sems = refs[2 * n:]
        x, y, cc = _my_position()
        me = 4 * x + 2 * y + cc
        mine = [pltpu.make_async_copy(send_refs[a].at[me], recv_refs[a].at[me], local_sems.at[a]) for a in range(n)]
        for cp in mine:
            cp.start()
        copies = []
        for rel in range(1, N_DEV):
            px, py, pc = x ^ (rel >> 2), y ^ ((rel >> 1) & 1), cc ^ (rel & 1)
            for a in range(n):
                copies.append(pltpu.make_async_remote_copy(
                    src_ref=send_refs[a].at[4 * px + 2 * py + pc], dst_ref=recv_refs[a].at[me],
                    send_sem=send_sems.at[a * COPIES + rel - 1], recv_sem=recv_sems.at[a * COPIES + rel - 1],
                    device_id=(px, py, pc), device_id_type=MESH_ID))
        for cp in copies:
            cp.start()
        for cp in copies:
            cp.wait()
        for cp in mine:
            cp.wait()

    any_spec = pl.BlockSpec(memory_space=pl.ANY)
    return _pcall(
        body, out_shape=[jax.ShapeDtypeStruct(s.shape, s.dtype) for s in sends],
        in_specs=[any_spec] * n, out_specs=[any_spec] * n,
        scratch_shapes=[pltpu.SemaphoreType.DMA((n * COPIES,)), pltpu.SemaphoreType.DMA((n * COPIES,)),
                        pltpu.SemaphoreType.DMA((n,))],
        name="all_to_all")(*sends)


def _hbm(a):
    return pltpu.with_memory_space_constraint(a, pltpu.HBM)


def _split_copies(send_refs, land_refs, send_sems, recv_sems, local_sems, gather):
    x, y, cc = _my_position()
    me = 4 * x + 2 * y + cc
    local, remote = [], []
    for a, (send, land) in enumerate(zip(send_refs, land_refs)):
        local.append(pltpu.make_async_copy(send if gather else send.at[me], land.at[me], local_sems.at[a]))
    for a, (send, land) in enumerate(zip(send_refs, land_refs)):
        for rel in range(1, N_DEV):
            px, py, pc = x ^ (rel >> 2), y ^ ((rel >> 1) & 1), cc ^ (rel & 1)
            remote.append(pltpu.make_async_remote_copy(
                src_ref=send if gather else send.at[4 * px + 2 * py + pc], dst_ref=land.at[me],
                send_sem=send_sems.at[a * COPIES + rel - 1], recv_sem=recv_sems.at[a * COPIES + rel - 1],
                device_id=(px, py, pc), device_id_type=MESH_ID))
    return local, remote


SPLIT_EFFECT = pltpu.SideEffectType.DATAFLOW_SIDE_EFFECTING


def _exchange_start(sends, after, gather, name):
    n = len(sends)
    lands = [_hbm(lax.empty((N_DEV,) + s.shape if gather else s.shape, s.dtype)) for s in sends]

    def body(*refs):
        send_refs, land_refs = refs[:n], refs[n:2 * n]
        send_sems, recv_sems, local_sems = refs[2 * n + 1:2 * n + 4]
        token = refs[-1]
        local, remote = _split_copies(send_refs, land_refs, send_sems, recv_sems, local_sems, gather)
        for cp in local + remote:
            cp.start()
        token[...] = jnp.zeros_like(token)

    hbm, sem = pl.BlockSpec(memory_space=pltpu.HBM), pl.BlockSpec(memory_space=pltpu.SEMAPHORE)
    out = _pcall(
        body, name=name,
        out_shape=[pltpu.SemaphoreType.DMA((n * COPIES,)), pltpu.SemaphoreType.DMA((n * COPIES,)), pltpu.SemaphoreType.DMA((n,))]
        + [pltpu.HBM(s.shape, s.dtype) for s in sends] + [pltpu.HBM(z.shape, z.dtype) for z in lands]
        + [jax.ShapeDtypeStruct((SUBLANES, LANES), F32)],
        in_specs=[hbm] * (2 * n) + [pl.BlockSpec(memory_space=pl.ANY)],
        out_specs=[sem] * 3 + [hbm] * (2 * n) + [pl.BlockSpec(memory_space=pltpu.VMEM)],
        input_output_aliases={i: 3 + i for i in range(2 * n)},
        compiler_params=pltpu.CompilerParams(has_side_effects=SPLIT_EFFECT),
    )(*[_hbm(s) for s in sends], *lands, after)
    return dict(sems=out[:3], sends=out[3:3 + n], lands=out[3 + n:3 + 2 * n], gather=gather), out[-1]


def _exchange_wait(handle, after, name):
    sends, lands, gather = handle["sends"], handle["lands"], handle["gather"]
    n = len(sends)

    def body(*refs):
        send_refs, land_refs = refs[:n], refs[n:2 * n]
        send_sems, recv_sems, local_sems = refs[2 * n:2 * n + 3]
        local, remote = _split_copies(send_refs, land_refs, send_sems, recv_sems, local_sems, gather)
        for cp in remote:
            cp.wait_send()
            cp.wait_recv()
        for cp in local:
            cp.wait()

    hbm, sem = pl.BlockSpec(memory_space=pltpu.HBM), pl.BlockSpec(memory_space=pltpu.SEMAPHORE)
    out = _pcall(
        body, name=name,
        out_shape=[pltpu.HBM(s.shape, s.dtype) for s in sends] + [pltpu.HBM(z.shape, z.dtype) for z in lands],
        in_specs=[hbm] * (2 * n) + [sem] * 3 + [pl.BlockSpec(memory_space=pl.ANY)],
        out_specs=[hbm] * (2 * n), input_output_aliases={i: i for i in range(2 * n)},
        compiler_params=pltpu.CompilerParams(has_side_effects=SPLIT_EFFECT),
    )(*sends, *lands, *handle["sems"], after)
    return out[n:]


def _sum_and_adamw(recv, w, m, v, name):
    _, r, wp = recv.shape
    c = w.shape[-1]
    lead = w.ndim == 3
    tr = max([d for d in range(2 * SUBLANES, 257, 2 * SUBLANES) if r % d == 0], default=r)
    bc1 = 1.0 - ADAM_B1 ** ADAM_STEP
    bc2 = 1.0 - ADAM_B2 ** ADAM_STEP

    def body(recv_ref, w_ref, m_ref, v_ref, g_ref, d_ref, nm_ref, nv_ref):
        g = recv_ref[0, :, 0:c].astype(F32)
        for s in range(1, N_DEV):
            g = g + recv_ref[s, :, 0:c].astype(F32)
        m_new = ADAM_B1 * m_ref[...] + (1.0 - ADAM_B1) * g
        v_new = ADAM_B2 * v_ref[...] + (1.0 - ADAM_B2) * (g * g)
        m_hat = m_new / bc1
        v_hat = v_new / bc2
        g_ref[...] = g
        d_ref[...] = -ADAM_LR * (m_hat / (jnp.sqrt(v_hat) + ADAM_EPS) + ADAM_WD * w_ref[...])
        nm_ref[...] = m_new
        nv_ref[...] = v_new

    tile = pl.BlockSpec((None, tr, c), lambda i: (0, i, 0)) if lead else pl.BlockSpec((tr, c), lambda i: (i, 0))
    return _pcall(
        body, grid=(r // tr,),
        in_specs=[pl.BlockSpec((N_DEV, tr, wp), lambda i: (0, i, 0)), tile, tile, tile],
        out_specs=[tile] * 4, out_shape=[jax.ShapeDtypeStruct(w.shape, F32)] * 4,
        compiler_params=_params("parallel"), name=name)(recv, w, m, v)


SHARDED_TAPS = ("dn_conv_w", "ffn_conv_w")
REPLICATED = ("attn_norm_g", "dn_a_log", "dn_dt_bias", "dn_out_norm_g", "sg_norm_g", "sg_w", "sg_b", "ffn_norm_g",
              "ffn_conv_b", "final_norm_g")
SMALL = SHARDED_TAPS + REPLICATED
WEIGHT_ORDER = ("attn_norm_g", "w_in", "dn_conv_w", "dn_a_log", "dn_dt_bias", "dn_out_norm_g", "sg_norm_g", "sg_w", "sg_b",
                "w_out", "ffn_norm_g", "w_up", "ffn_conv_w", "ffn_conv_b", "w_down", "final_norm_g")
SLAB_COLS = 1024


def _pad_to(flat, multiple):
    pad = (-flat.shape[-1]) % multiple
    if pad == 0:
        return flat
    return jnp.pad(flat, [(0, 0)] * (flat.ndim - 1) + [(0, pad)])


def _pack_small(named):
    flat = jnp.concatenate([named[n].reshape(-1) for n in SMALL])
    return _pad_to(flat, SUBLANES * SLAB_COLS).reshape(-1, SLAB_COLS)


def _unpack_small(slab, like):
    flat = slab.reshape(-1)
    out, off = {}, 0
    for n in SMALL:
        size = like[n].size
        out[n] = flat[off:off + size].reshape(like[n].shape)
        off += size
    return out


def _split_columns(full, n_local):
    r = full.shape[0]
    return full.reshape(r, N_DEV, n_local).transpose(1, 0, 2).reshape(N_DEV, r * n_local)


def _join_columns(blocks, r, n_local):
    return blocks.reshape(N_DEV, r, n_local).transpose(1, 0, 2).reshape(r, N_DEV * n_local)


def _lanes4(a):
    return jnp.pad(a.reshape(1, N_HEADS), ((0, 0), (0, LANES - N_HEADS)))


def kernel(x, attn_norm_g, w_in, dn_conv_w, dn_a_log, dn_dt_bias, dn_out_norm_g, sg_norm_g, sg_w, sg_b, w_out, ffn_norm_g, w_up, ffn_conv_w, ffn_conv_b, w_down, final_norm_g, loss_target, m_attn_norm_g, m_w_in, m_dn_conv_w, m_dn_a_log, m_dn_dt_bias, m_dn_out_norm_g, m_sg_norm_g, m_sg_w, m_sg_b, m_w_out, m_ffn_norm_g, m_w_up, m_ffn_conv_w, m_ffn_conv_b, m_w_down, m_final_norm_g, v_attn_norm_g, v_w_in, v_dn_conv_w, v_dn_a_log, v_dn_dt_bias, v_dn_out_norm_g, v_sg_norm_g, v_sg_w, v_sg_b, v_w_out, v_ffn_norm_g, v_w_up, v_ffn_conv_w, v_ffn_conv_b, v_w_down, v_final_norm_g):
    weights = dict(attn_norm_g=attn_norm_g, w_in=w_in, dn_conv_w=dn_conv_w, dn_a_log=dn_a_log, dn_dt_bias=dn_dt_bias,
                   dn_out_norm_g=dn_out_norm_g, sg_norm_g=sg_norm_g, sg_w=sg_w, sg_b=sg_b, w_out=w_out, ffn_norm_g=ffn_norm_g,
                   w_up=w_up, ffn_conv_w=ffn_conv_w, ffn_conv_b=ffn_conv_b, w_down=w_down, final_norm_g=final_norm_g)
    m_in = dict(attn_norm_g=m_attn_norm_g, w_in=m_w_in, dn_conv_w=m_dn_conv_w, dn_a_log=m_dn_a_log, dn_dt_bias=m_dn_dt_bias,
                dn_out_norm_g=m_dn_out_norm_g, sg_norm_g=m_sg_norm_g, sg_w=m_sg_w, sg_b=m_sg_b, w_out=m_w_out,
                ffn_norm_g=m_ffn_norm_g, w_up=m_w_up, ffn_conv_w=m_ffn_conv_w, ffn_conv_b=m_ffn_conv_b, w_down=m_w_down,
                final_norm_g=m_final_norm_g)
    v_in = dict(attn_norm_g=v_attn_norm_g, w_in=v_w_in, dn_conv_w=v_dn_conv_w, dn_a_log=v_dn_a_log, dn_dt_bias=v_dn_dt_bias,
                dn_out_norm_g=v_dn_out_norm_g, sg_norm_g=v_sg_norm_g, sg_w=v_sg_w, sg_b=v_sg_b, w_out=v_w_out,
                ffn_norm_g=v_ffn_norm_g, w_up=v_w_up, ffn_conv_w=v_ffn_conv_w, ffn_conv_b=v_ffn_conv_b, w_down=v_w_down,
                final_norm_g=v_final_norm_g)

    n_in, n_up = w_in.shape[2], w_up.shape[2]
    r_out, r_down = w_out.shape[1], w_down.shape[1]
    n_dnc, n_ffc = dn_conv_w.shape[2], ffn_conv_w.shape[2]
    transposed = lambda a: jnp.transpose(a, (0, 2, 1))
    taps = _pad_to(jnp.concatenate([dn_conv_w.reshape(-1), ffn_conv_w.reshape(-1)]), SUBLANES * LANES).reshape(-1, LANES)
    g_in, g_taps = _all_gather([transposed(w_in)[0].astype(BF16), taps])
    gather_out, token = _exchange_start([w_out[0].astype(BF16)], g_taps, True, "gather_w_out")
    gather_up, token = _exchange_start([transposed(w_up)[0].astype(BF16)], token, True, "gather_w_up")
    gather_down, token = _exchange_start([w_down[0].astype(BF16)], token, True, "gather_w_down")
    w_in_t = jnp.pad(g_in.reshape(N_DEV * n_in, D_MODEL), ((0, PROJ_PAD - N_DEV * n_in), (0, 0)))
    taps_all = g_taps.reshape(N_DEV, -1)
    dn_conv_full = _join_columns(taps_all[:, :CONV_K * n_dnc], CONV_K, n_dnc)
    ffn_conv_full = _join_columns(taps_all[:, CONV_K * n_dnc:CONV_K * n_dnc + FFN_CONV * n_ffc], FFN_CONV, n_ffc)
    late = dict(
        w_out=lambda after: _exchange_wait(gather_out, after, "gather_w_out_wait")[0].reshape(N_DEV * r_out, D_MODEL),
        w_up_t=lambda after: _exchange_wait(gather_up, after, "gather_w_up_wait")[0].reshape(N_DEV * n_up, D_MODEL),
        w_down=lambda after: _exchange_wait(gather_down, after, "gather_w_down_wait")[0].reshape(N_DEV * r_down, D_MODEL))

    def send_early(blocks, after, name):
        return _exchange_start(blocks, after, False, name)

    def send_small(g, loss_lanes, after):
        small = jnp.concatenate([g[n].reshape(-1) for n in REPLICATED] + [loss_lanes[0, 0:1]])
        slab = jnp.concatenate([_split_columns(g["dn_conv_w"], n_dnc), _split_columns(g["ffn_conv_w"], n_ffc),
                                jnp.broadcast_to(small[None, :], (N_DEV, small.shape[0]))], axis=1)
        return send_early([_pad_to(slab, SUBLANES * SLAB_COLS).reshape(N_DEV, -1, SLAB_COLS)], after, "send_small")

    upd = {}

    def update_early(sent_down, sent_up_out, sent_small, after):
        r_dn, = _exchange_wait(sent_down, after, "send_dw_down_wait")
        r_up, r_o = _exchange_wait(sent_up_out, after, "send_dw_up_out_wait")
        r_small, = _exchange_wait(sent_small, after, "send_small_wait")
        upd["w_down"] = _sum_and_adamw(r_dn, w_down, m_w_down, v_w_down, "adamw_w_down")
        upd["w_up"] = [transposed(o) for o in _sum_and_adamw(r_up, transposed(w_up), transposed(m_w_up), transposed(v_w_up),
                                                             "adamw_w_up")]
        upd["w_out"] = _sum_and_adamw(r_o, w_out, m_w_out, v_w_out, "adamw_w_out")
        upd["small"] = _sum_and_adamw(r_small, _pack_small(weights), _pack_small(m_in), _pack_small(v_in), "adamw_small")

    grad_x, d_g1, sent_in = _local_step(
        x[0], loss_target[0], w_in_t, late, send_early, send_small, update_early, dn_conv_full, ffn_conv_full,
        attn_norm_g + token[0:1, 0:1], dn_a_log, dn_dt_bias, dn_out_norm_g, sg_norm_g, sg_w, sg_b, ffn_norm_g, ffn_conv_b,
        final_norm_g, n_in)

    norm_rows = D_MODEL // LANES
    r_g1, = _all_to_all([jnp.broadcast_to(d_g1.reshape(1, norm_rows, LANES), (N_DEV, norm_rows, LANES))])
    r_in, = _exchange_wait(sent_in, r_g1, "send_dw_in_wait")
    upd["w_in"] = [transposed(o) for o in _sum_and_adamw(r_in, transposed(w_in), transposed(m_w_in), transposed(v_w_in),
                                                         "adamw_w_in")]
    small_upd = upd.pop("small")
    as_rows = lambda a: a.reshape(norm_rows, LANES)
    norm_upd = _sum_and_adamw(r_g1, as_rows(attn_norm_g), as_rows(m_attn_norm_g), as_rows(v_attn_norm_g), "adamw_attn_norm")
    results = []
    for i in range(4):
        named = _unpack_small(small_upd[i], weights)
        named.update({n: upd[n][i] for n in upd})
        named["attn_norm_g"] = norm_upd[i].reshape(attn_norm_g.shape)
        results.append(named)

    loss = small_upd[0].reshape(-1)[sum(weights[n].size for n in SMALL)]
    return (loss, grad_x[None], *[r[n] for r in results for n in WEIGHT_ORDER])


def _local_step(x2d, tgt, w_in_t, late, send_early, send_small, update_early, dn_conv_full, ffn_conv_full, attn_norm_g,
                dn_a_log, dn_dt_bias, dn_out_norm_g, sg_norm_g, sg_w, sg_b, ffn_norm_g, ffn_conv_b, final_norm_g, n_in):
    t = x2d.shape[0]
    g1, g2, gf = attn_norm_g, ffn_norm_g, final_norm_g.reshape(1, D_MODEL)
    a_log4, dt_bias4 = _lanes4(dn_a_log), _lanes4(dn_dt_bias)
    sg_w3 = sg_w[0]
    sg_b_t = sg_b[0].T
    conv_b = ffn_conv_b

    p, h1, rstd1 = _rmsnorm_matmul(x2d, g1, w_in_t, "norm_in_proj", 512)
    q, k, v, beta4, g4 = _dn_prep(p, dn_conv_full, a_log4, dt_bias4)
    mix_half, *dn_saved = _dn_forward(q, k, v, beta4, g4, p, dn_out_norm_g)
    mix = _sg_forward(p, sg_norm_g, sg_w3, sg_b_t, mix_half)
    w_out_full = late["w_out"](mix)
    x2 = _matmul(mix, w_out_full, "nn", "out_proj", (1024, 1024, 1024), add=x2d)
    w_up_t = late["w_up_t"](x2)
    up, act, h2, rstd2 = _norm_up_ffn(x2, g2, w_up_t, ffn_conv_full, conv_b)
    w_down_full = late["w_down"](act)
    fn, outs = _final_loss_rows(t, D_MODEL)
    loss_lanes, dx3, dx3b, d_gf = _matmul_rows(act, w_down_full, "nn", "down_proj_loss", 512,
                                               [(x2, "rows"), (tgt, "rows"), (gf, "whole")], outs, fn)

    dact = _matmul(dx3b, w_down_full, "nt", "down_proj_dx", (512, D_FF, D_MODEL))
    d_w_down = _matmul(act, dx3b, "tn", "down_proj_dw", (256, 1024, t), out_dtype=BF16)
    sent_down, token = send_early([d_w_down.reshape(N_DEV, D_FF // N_DEV, D_MODEL)], d_w_down, "send_dw_down")
    dup, d_ffn_conv, d_ffn_conv_b = _ffn_bwd(up, ffn_conv_full, conv_b + token[0:1, 0:1], dact)
    fn, outs = _rmsnorm_bwd_rows(t, D_MODEL)
    dx2, dx2b, d_g2 = _matmul_rows(dup, w_up_t, "nn", "up_proj_dx_norm", 256,
                                   [(x2, "rows"), (rstd2, "rows"), (g2, "whole"), (dx3, "rows")], outs, fn)
    d_w_up_t = _matmul(dup, h2, "tn", "up_proj_dw", (512, 1024, t), out_dtype=BF16)
    dmix = _matmul(dx2b, w_out_full, "nt", "out_proj_dx", (1024, 1024, 1024))
    d_w_out = _matmul(mix, dx2b, "tn", "out_proj_dw", (512, 1024, t), out_dtype=BF16)
    sent_up_out, token = send_early(
        [d_w_up_t.reshape(N_DEV, 2 * D_FF // N_DEV, D_MODEL), d_w_out.reshape(N_DEV, D_MODEL // N_DEV, D_MODEL)],
        d_w_out, "send_dw_up_out")
    dp, d_sg_norm, d_sg_w, d_sg_b_t = _sg_backward(p, sg_norm_g + token[0:1, 0:1], sg_w3, sg_b_t, dmix)
    dq, dk, dv, dbeta4, dg4, dp, d_dn_norm = _dn_backward(q, k, v, beta4, g4, p, dn_out_norm_g, dn_saved, dmix, dp)
    dc_dn, d_dn_conv, dp, d_a_log4, d_dt_bias4 = _dn_prep_bwd(p, dn_conv_full, a_log4, dt_bias4, dq, dk, dv, dbeta4, dg4, dp)
    small_grads = dict(
        attn_norm_g=jnp.zeros_like(attn_norm_g), dn_conv_w=d_dn_conv, dn_a_log=d_a_log4[:, :N_HEADS],
        dn_dt_bias=d_dt_bias4[:, :N_HEADS], dn_out_norm_g=d_dn_norm, sg_norm_g=d_sg_norm, sg_w=d_sg_w,
        sg_b=d_sg_b_t[:, :SG_GROUPS].T, ffn_norm_g=d_g2, ffn_conv_w=d_ffn_conv, ffn_conv_b=d_ffn_conv_b, final_norm_g=d_gf)
    sent_small, token = send_small(small_grads, loss_lanes, d_dn_conv)
    dp = _conv_bwd_input(dc_dn, dn_conv_full + token[0:1, 0:1], "dn_conv_dx", out_cols=PROJ_PAD, into=dp)
    d_w_in_t = _matmul(dp, h1, "tn", "in_proj_dw", (PROJ_PAD // 5, 1024, t), out_dtype=BF16)
    sent_in, token = send_early([d_w_in_t[:N_DEV * n_in].reshape(N_DEV, n_in, D_MODEL)], d_w_in_t, "send_dw_in")
    update_early(sent_down, sent_up_out, sent_small, token)
    fn, outs = _rmsnorm_bwd_rows(t, D_MODEL)
    grad_x, _, d_g1 = _matmul_rows(dp, w_in_t, "nn", "in_proj_dx_norm", 512,
                                   [(x2d, "rows"), (rstd1, "rows"), (g1 + token[0:1, 0:1], "whole"), (dx2, "rows")], outs, fn)

    return grad_x, d_g1, sent_in
```

```python
import math

import jax
import jax.numpy as jnp
from jax import lax
from jax.experimental import pallas as pl
from jax.experimental.pallas import tpu as pltpu

F32 = jnp.float32
BF16 = jnp.bfloat16
HI = lax.Precision.HIGHEST

D_MODEL = 1024
DN_WIDTH = 512
HEAD_DIM = 128
N_HEADS = 4
SG_WIDTH = 512
SG_GROUPS = 4
SG_DIM = 128
SG_BLOCK = 128
D_FF = 2816
CHUNK = 64
CONV_K = 4
FFN_CONV = 3
EPS = 1e-6
PROJ_MAIN = 3072
PROJ_PAD = 3200
GELU_C = math.sqrt(2.0 / math.pi)
N_DEV = 8
LANES = 128
SUBLANES = 8
HALO = SUBLANES
VMEM_LIMIT = 48 * 1024 * 1024

ADAM_LR = 0.001
ADAM_B1 = 0.9
ADAM_B2 = 0.999
ADAM_EPS = 1e-08
ADAM_WD = 0.01
ADAM_STEP = 10

MESH_ID = pl.DeviceIdType.MESH


def _pcall(body, **kw):
    return pl.pallas_call(body, **kw)


def _params(*sem):
    return pltpu.CompilerParams(dimension_semantics=sem, vmem_limit_bytes=VMEM_LIMIT)


def _pick(n, cap):
    best = None
    for t in range(LANES, cap + 1, LANES):
        if n % t == 0:
            best = t
    return best if best else n


FAST, EXACT = "bf16 operands, one pass", "f32 operands, six bf16 passes"


def dot_f32(a, b, dims, tier):
    if tier == FAST:
        return lax.dot_general(a.astype(BF16), b.astype(BF16), dims, preferred_element_type=F32)
    return lax.dot_general(a, b, dims, precision=HI, preferred_element_type=F32)


def dot_nn(a, b, tier=EXACT):
    return dot_f32(a, b, (((1,), (0,)), ((), ())), tier)


def dot_nt(a, b, tier=EXACT):
    return dot_f32(a, b, (((1,), (1,)), ((), ())), tier)


def dot_tn(a, b, tier=EXACT):
    return dot_f32(a, b, (((0,), (0,)), ((), ())), tier)


def sigmoid(x):
    return 0.5 * jnp.tanh(0.5 * x) + 0.5


def silu(x):
    return x * sigmoid(x)


def silu_grad(x):
    s = sigmoid(x)
    return s * (1.0 + x * (1.0 - s))


def gelu(x):
    return 0.5 * x * (1.0 + jnp.tanh(GELU_C * (x + 0.044715 * x * x * x)))


def gelu_grad(x):
    t = jnp.tanh(GELU_C * (x + 0.044715 * x * x * x))
    return 0.5 * (1.0 + t) + 0.5 * x * (1.0 - t * t) * GELU_C * (1.0 + 3.0 * 0.044715 * x * x)


def softplus(z):
    return jnp.maximum(z, 0.0) + jnp.log(1.0 + jnp.exp(-jnp.abs(z)))


def rms_fwd(x, g):
    r = lax.rsqrt(jnp.mean(x * x, axis=-1, keepdims=True) + EPS)
    return x * r * g, r


def rms_bwd(x, r, g, dy):
    dyg = dy * g
    xr = x * r
    dx = r * (dyg - xr * jnp.mean(dyg * xr, axis=-1, keepdims=True))
    return dx, dy * xr


def l2_fwd(x):
    r = lax.rsqrt(jnp.sum(x * x, axis=-1, keepdims=True) + EPS)
    return x * r, r


def l2_bwd(x, r, dy):
    xr = x * r
    return r * (dy - xr * jnp.sum(dy * xr, axis=-1, keepdims=True))


def _tri_masks(n):
    row = lax.broadcasted_iota(jnp.int32, (n, n), 0)
    col = lax.broadcasted_iota(jnp.int32, (n, n), 1)
    return row >= col, row > col


def chunk_cumsum(g4):
    incl, _ = _tri_masks(g4.shape[0])
    return dot_nn(incl.astype(F32), g4)


STACK = N_HEADS * CHUNK
DN_FWD_CHUNKS = 8
DN_CHUNKS = 4


def _head_rows(h):
    return slice(h * CHUNK, (h + 1) * CHUNK)


def _stack_heads(x):
    return jnp.concatenate([x[:, h * HEAD_DIM:(h + 1) * HEAD_DIM] for h in range(N_HEADS)], axis=0)


def _stack_lanes(x4):
    return jnp.concatenate([x4[:, h:h + 1] for h in range(N_HEADS)], axis=0)


def _per_head(fn):
    return jnp.concatenate([fn(h) for h in range(N_HEADS)], axis=0)


def _unit_lower_inverses(l_strict, order):
    c = l_strict[0].shape[0]
    row = lax.broadcasted_iota(jnp.int32, (c, c), 0)
    col = lax.broadcasted_iota(jnp.int32, (c, c), 1)
    eye = (row == col).astype(F32)
    p = [-l for l in l_strict]
    a = [eye + n for n in p]
    for _ in range(int(math.log2(order)) - 1):
        p = [dot_nn(x, x, FAST) for x in p]
        a = [x + dot_nn(x, y, FAST) for x, y in zip(a, p)]
    return a


def dn_chunks_local(chunks, inverses=None):
    row = lax.broadcasted_iota(jnp.int32, (STACK, STACK), 0)
    col = lax.broadcasted_iota(jnp.int32, (STACK, STACK), 1)
    same = (row // CHUNK) == (col // CHUNK)
    incl = jnp.logical_and(same, row >= col)
    strict = jnp.logical_and(same, row > col)
    locs = []
    for q, k, v, beta, gc4 in chunks:
        gc_col = _stack_lanes(gc4)
        gc_row = jnp.sum(jnp.where(row == col, gc_col, 0.0), axis=0, keepdims=True)
        decay = jnp.where(incl, jnp.exp(jnp.minimum(gc_col - gc_row, 0.0)), 0.0)
        gamma = jnp.exp(gc_col)
        gc_last = jnp.concatenate([jnp.broadcast_to(gc4[CHUNK - 1:CHUNK, h:h + 1], (CHUNK, 1)) for h in range(N_HEADS)], axis=0)
        tau = jnp.exp(gc_last - gc_col)
        kb = k * beta
        locs.append(dict(decay=decay, gamma=gamma, tau=tau, cd=jnp.exp(gc_last), kb=kb, qd=q * gamma, kt=k * tau,
                         incl=incl, strict=strict))
    for loc, (q, k, v, beta, gc4) in zip(locs, chunks):
        loc["l_mat"] = jnp.where(strict, dot_nt(loc["kb"], k, FAST) * loc["decay"], 0.0)
    if inverses is None:
        inverses = _unit_lower_inverses([loc["l_mat"] for loc in locs], CHUNK)
    for loc, a_inv in zip(locs, inverses):
        loc["a_inv"] = a_inv
    for loc, (q, k, v, beta, gc4) in zip(locs, chunks):
        sol = dot_nn(loc["a_inv"], jnp.concatenate([v * beta, loc["kb"] * loc["gamma"]], axis=1), FAST)
        loc.update(sol=sol, value=sol[:, :HEAD_DIM], kcd=sol[:, HEAD_DIM:])
        loc["attn"] = jnp.where(incl, dot_nt(q, k, FAST) * loc["decay"], 0.0)
    return locs


def dn_chunk_state(loc, s):
    kcd, qd, kt, cd = loc["kcd"], loc["qd"], loc["kt"], loc["cd"]
    v_new = loc["value"] - _per_head(lambda h: dot_nn(kcd[_head_rows(h)], s[h], FAST))
    o = _per_head(lambda h: dot_nn(qd[_head_rows(h)], s[h], FAST)) + dot_nn(loc["attn"], v_new, FAST)
    s_new = [s[h] * cd[h * CHUNK:h * CHUNK + 1, :] + dot_tn(kt[_head_rows(h)], v_new[_head_rows(h)], FAST)
             for h in range(N_HEADS)]
    loc["v_new"] = v_new
    return o, s_new


def dn_chunks_bwd(items, ds_last):
    hr = _head_rows
    n = len(items)
    pre = []
    for q, k, v, beta, loc, s, do in items:
        pre.append(dict(
            dv_part=dot_tn(loc["attn"], do, FAST),
            dattn=jnp.where(loc["incl"], dot_nt(do, loc["v_new"], FAST), 0.0),
            dqd=_per_head(lambda h: dot_nt(do[hr(h)], s[h], FAST)),
            ds_part=[dot_tn(loc["qd"][hr(h)], do[hr(h)], FAST) for h in range(N_HEADS)]))
    ds_new_of, dv_new_of = [None] * n, [None] * n
    ds = ds_last
    for c in reversed(range(n)):
        loc = items[c][4]
        ds_new_of[c] = ds
        dv_new = pre[c]["dv_part"] + _per_head(lambda h: dot_nn(loc["kt"][hr(h)], ds[h], FAST))
        dv_new_of[c] = dv_new
        ds = [pre[c]["ds_part"][h] + ds[h] * loc["cd"][h * CHUNK:h * CHUNK + 1, :]
              - dot_tn(loc["kcd"][hr(h)], dv_new[hr(h)], FAST) for h in range(N_HEADS)]
    is_last = (lax.broadcasted_iota(jnp.int32, (STACK, 1), 0) % CHUNK) == CHUNK - 1
    out = []
    for c, (q, k, v, beta, loc, s, do) in enumerate(items):
        decay, gamma, tau, cd, kb = loc["decay"], loc["gamma"], loc["tau"], loc["cd"], loc["kb"]
        dv_new, ds_new, dattn, dqd = dv_new_of[c], ds_new_of[c], pre[c]["dattn"], pre[c]["dqd"]
        dkt = _per_head(lambda h: dot_nt(loc["v_new"][hr(h)], ds_new[h], FAST))
        dkcd = -_per_head(lambda h: dot_nt(dv_new[hr(h)], s[h], FAST))
        drhs = dot_tn(loc["a_inv"], jnp.concatenate([dv_new, dkcd], axis=1), FAST)
        dvb, dkbg = drhs[:, :HEAD_DIM], drhs[:, HEAD_DIM:]
        dl = jnp.where(loc["strict"], -dot_nt(drhs, loc["sol"], FAST), 0.0)
        dkk = dl * decay
        dqk = dattn * decay
        e = dl * loc["l_mat"] + dattn * loc["attn"]
        dgc = jnp.sum(e, axis=1, keepdims=True) - jnp.sum(e, axis=0, keepdims=True).T
        dkb = dot_nn(dkk, k, FAST) + dkbg * gamma
        dk = dot_tn(dkk, kb, FAST) + dot_tn(dqk, q, FAST) + dkt * tau
        dq = dot_nn(dqk, k, FAST) + dqd * gamma
        dgamma = jnp.sum(dkbg * kb, axis=1, keepdims=True) + jnp.sum(dqd * q, axis=1, keepdims=True)
        dtau_tau = jnp.sum(dkt * k, axis=1, keepdims=True) * tau
        dgc = dgc + dgamma * gamma - dtau_tau

        def last_term(h):
            dcd = jnp.sum(jnp.sum(ds_new[h] * s[h], axis=1, keepdims=True), axis=0, keepdims=True)
            total = jnp.sum(dtau_tau[hr(h)], axis=0, keepdims=True) + dcd * cd[h * CHUNK:h * CHUNK + 1, :]
            return jnp.broadcast_to(total, (CHUNK, 1))

        dgc = dgc + jnp.where(is_last, _per_head(last_term), 0.0)
        dk = dk + dkb * beta
        dbeta = jnp.sum(dkb * k, axis=1, keepdims=True) + jnp.sum(dvb * v, axis=1, keepdims=True)
        out.append((dq, dk, dvb * beta, dbeta, dgc))
    return out, ds


def _token_tile(t):
    return _pick(t, 256)


STRIP = 32


def _for_strips(n_rows, rows, fn, start=0):
    def step(r, carry):
        fn(pl.multiple_of(r * rows, rows))
        return carry

    lax.fori_loop(start, n_rows // rows, step, 0)


def _fold_rows(x):
    out = x[0:SUBLANES, :]
    for i in range(1, x.shape[0] // SUBLANES):
        out = out + x[i * SUBLANES:(i + 1) * SUBLANES, :]
    return out


def _matmul(a, b, mode, name, tiles, add=None, out_dtype=F32):
    if mode == "nn":
        (m, k), n = a.shape, b.shape[1]
    elif mode == "nt":
        (m, k), n = a.shape, b.shape[0]
    else:
        (k, m), n = a.shape, b.shape[1]
    tm, tn, tk = min(tiles[0], m), min(tiles[1], n), min(tiles[2], k)
    assert m % tm == 0 and n % tn == 0 and k % tk == 0, (name, m, n, k, tiles)
    nk = k // tk
    dims = {"nn": (((1,), (0,)), ((), ())), "nt": (((1,), (1,)), ((), ())), "tn": (((0,), (0,)), ((), ()))}[mode]

    def finish(res, add_ref, o_ref):
        if add_ref is not None:
            res = res + add_ref[...]
        o_ref[...] = res.astype(o_ref.dtype)

    def body(*refs):
        a_ref, b_ref = refs[0], refs[1]
        add_ref = refs[2] if add is not None else None
        o_ref = refs[3] if add is not None else refs[2]
        part = lax.dot_general(a_ref[...], b_ref[...], dims, preferred_element_type=F32)
        if nk == 1:
            finish(part, add_ref, o_ref)
            return
        acc_ref = refs[-1]
        kk = pl.program_id(2)

        @pl.when(kk == 0)
        def _():
            acc_ref[...] = part

        @pl.when(kk > 0)
        def _():
            acc_ref[...] += part

        @pl.when(kk == nk - 1)
        def _():
            finish(acc_ref[...], add_ref, o_ref)

    a_spec = pl.BlockSpec((tk, tm), lambda j, i, kk: (kk, i)) if mode == "tn" else pl.BlockSpec((tm, tk), lambda j, i, kk: (i, kk))
    b_spec = pl.BlockSpec((tn, tk), lambda j, i, kk: (j, kk)) if mode == "nt" else pl.BlockSpec((tk, tn), lambda j, i, kk: (kk, j))
    o_spec = pl.BlockSpec((tm, tn), lambda j, i, kk: (i, j))
    in_specs = [a_spec, b_spec] + ([o_spec] if add is not None else [])
    args = (a, b) + ((add,) if add is not None else ())
    return _pcall(
        body, grid=(n // tn, m // tm, nk), in_specs=in_specs, out_specs=o_spec,
        out_shape=jax.ShapeDtypeStruct((m, n), out_dtype),
        scratch_shapes=[pltpu.VMEM((tm, tn), F32)] if nk > 1 else [],
        compiler_params=_params("parallel", "parallel", "arbitrary"), name=name)(*args)


def _matmul_rows(a, b, mode, name, tm, extra, outs, fn):
    m, k = a.shape
    n = b.shape[1] if mode == "nn" else b.shape[0]
    tm = min(tm, m)
    dims = (((1,), (0,)), ((), ())) if mode == "nn" else (((1,), (1,)), ((), ()))

    def spec(shape, kind):
        if kind == "rows":
            return pl.BlockSpec((tm, shape[1]), lambda i: (i, 0))
        return pl.BlockSpec(shape, lambda i: (0,) * len(shape))

    def body(a_ref, b_ref, *refs):
        rows = lax.dot_general(a_ref[...], b_ref[...], dims, preferred_element_type=F32)
        fn(rows, pl.program_id(0) == 0, *refs)

    return _pcall(
        body, grid=(m // tm,),
        in_specs=[pl.BlockSpec((tm, k), lambda i: (i, 0)), pl.BlockSpec(b.shape, lambda i: (0, 0), pipeline_mode=pl.Buffered(1))]
        + [spec(x.shape, kind) for x, kind in extra],
        out_specs=[spec(shape, kind) for shape, _, kind in outs],
        out_shape=[jax.ShapeDtypeStruct(shape, dtype) for shape, dtype, _ in outs],
        compiler_params=_params("arbitrary"), name=name)(a, b, *[x for x, _ in extra])


def _rmsnorm_matmul(x, g, b_t, name, tm):
    t, d = x.shape
    n = b_t.shape[0]
    tm = min(tm, t)

    def body(x_ref, g_ref, b_ref, o_ref, h_ref, r_ref):
        y, r = rms_fwd(x_ref[...], g_ref[...])
        h = y.astype(BF16)
        h_ref[...] = h
        r_ref[...] = r
        o_ref[...] = lax.dot_general(h, b_ref[...], (((1,), (1,)), ((), ())), preferred_element_type=F32)

    rows = lambda w: pl.BlockSpec((tm, w), lambda i: (i, 0))
    return _pcall(
        body, grid=(t // tm,),
        in_specs=[rows(d), pl.BlockSpec((1, d), lambda i: (0, 0)), pl.BlockSpec((n, d), lambda i: (0, 0))],
        out_specs=[rows(n), rows(d), rows(1)],
        out_shape=[jax.ShapeDtypeStruct((t, n), F32), jax.ShapeDtypeStruct((t, d), BF16), jax.ShapeDtypeStruct((t, 1), F32)],
        compiler_params=_params("parallel"), name=name)(x, g, b_t)


def _rmsnorm_bwd_rows(t, d):
    def fn(dh, first, x_ref, r_ref, g_ref, dres_ref, dx_ref, dxb_ref, dg_ref):
        dx, dg_rows = rms_bwd(x_ref[...], r_ref[...], g_ref[...], dh)
        dx = dx + dres_ref[...]
        dx_ref[...] = dx
        dxb_ref[...] = dx.astype(BF16)

        @pl.when(first)
        def _():
            dg_ref[...] = jnp.zeros_like(dg_ref)

        dg_ref[...] += jnp.sum(dg_rows, axis=0, keepdims=True)

    return fn, [((t, d), F32, "rows"), ((t, d), BF16, "rows"), ((1, d), F32, "whole")]


def _final_loss_rows(t, d):
    def fn(rows, first, res_ref, t_ref, g_ref, loss_ref, dx_ref, dxb_ref, dg_ref):
        @pl.when(first)
        def _():
            loss_ref[...] = jnp.zeros_like(loss_ref)
            dg_ref[...] = jnp.zeros_like(dg_ref)

        x = rows + res_ref[...]
        y, r = rms_fwd(x, g_ref[...])
        err = y - t_ref[...]
        loss_ref[...] += 0.5 * jnp.sum(jnp.mean(err * err, axis=-1, keepdims=True), axis=0, keepdims=True)
        dx, dg_rows = rms_bwd(x, r, g_ref[...], err * (1.0 / d))
        dx_ref[...] = dx
        dxb_ref[...] = dx.astype(BF16)
        dg_ref[...] += jnp.sum(dg_rows, axis=0, keepdims=True)

    return fn, [((1, LANES), F32, "whole"), ((t, d), F32, "rows"), ((t, d), BF16, "rows"), ((1, d), F32, "whole")]


def _prev_halo_spec(tm, width, col_block):
    return pl.BlockSpec((HALO, width), lambda i: (jnp.maximum(i * (tm // HALO) - 1, 0), col_block))


def _history(tile_ref, halo_ref, first, row0, cols):
    if isinstance(row0, int) and row0 == 0:
        return jnp.concatenate([jnp.where(first, 0.0, halo_ref[:, cols]), tile_ref[0:STRIP, cols]], axis=0)
    return tile_ref[pl.ds(pl.multiple_of(row0 - HALO, HALO), STRIP + HALO), cols]


def _first_then_strips(n_rows, fn):
    fn(0)
    _for_strips(n_rows, STRIP, fn, start=1)


def _delays(ext, taps):
    return [ext[HALO:, :]] + [pltpu.roll(ext, j, 0)[HALO:, :] for j in range(1, taps)]


def _causal_conv(delayed, w):
    taps = len(delayed)
    out = delayed[0] * w[taps - 1:taps, :]
    for j in range(1, taps):
        out = out + delayed[j] * w[taps - 1 - j:taps - j, :]
    return out


def _advanced_conv(buf_ref, row0, cols, w):
    return _advanced(buf_ref[pl.ds(row0, STRIP + HALO), cols], w)


def _advanced(ext, w):
    taps = w.shape[0]
    out = ext[:STRIP, :] * w[taps - 1:taps, :]
    for j in range(1, taps):
        out = out + pltpu.roll(ext, STRIP + HALO - j, 0)[:STRIP, :] * w[taps - 1 - j:taps - j, :]
    return out


def _dn_prep(p, conv_w, a_log4, dt_bias4):
    t = p.shape[0]
    tm = _token_tile(t)
    w3 = 3 * DN_WIDTH

    def body(x_ref, halo_ref, pbd_ref, w_ref, alog_ref, dtb_ref, q_ref, k_ref, v_ref, beta_ref, g_ref):
        first = pl.program_id(0) == 0

        def strip(row0):
            rows = pl.ds(row0, STRIP)
            for h in range(N_HEADS):
                sl = slice(h * HEAD_DIM, (h + 1) * HEAD_DIM)
                for part, out_ref in ((0, q_ref), (1, k_ref), (2, v_ref)):
                    cols = slice(part * DN_WIDTH + h * HEAD_DIM, part * DN_WIDTH + (h + 1) * HEAD_DIM)
                    y = silu(_causal_conv(_delays(_history(x_ref, halo_ref, first, row0, cols), CONV_K), w_ref[:, cols]))
                    if part == 0:
                        y = l2_fwd(y)[0] * (HEAD_DIM ** -0.5)
                    elif part == 1:
                        y = l2_fwd(y)[0]
                    out_ref[rows, sl] = y
            head = lax.broadcasted_iota(jnp.int32, (STRIP, LANES), 1) < N_HEADS
            pbd = pbd_ref[rows, :]
            beta_ref[rows, :] = jnp.where(head, sigmoid(pbd), 0.0)
            a_raw = pltpu.roll(pbd, LANES - N_HEADS, 1)
            g_ref[rows, :] = jnp.where(head, -jnp.exp(alog_ref[...]) * softplus(a_raw + dtb_ref[...]), 0.0)

        _first_then_strips(tm, strip)

    tok = lambda w, cb: pl.BlockSpec((tm, w), lambda i: (i, cb))
    full = lambda a: pl.BlockSpec(a.shape, lambda i: (0, 0))
    return _pcall(
        body, grid=(t // tm,),
        in_specs=[tok(w3, 0), _prev_halo_spec(tm, w3, 0), tok(LANES, PROJ_MAIN // LANES),
                  full(conv_w), full(a_log4), full(dt_bias4)],
        out_specs=[tok(DN_WIDTH, 0)] * 3 + [tok(LANES, 0)] * 2,
        out_shape=[jax.ShapeDtypeStruct((t, DN_WIDTH), F32)] * 3 + [jax.ShapeDtypeStruct((t, LANES), F32)] * 2,
        compiler_params=_params("parallel"), name="dn_prep")(p, p, p, conv_w, a_log4, dt_bias4)


def _dn_prep_bwd(p, conv_w, a_log4, dt_bias4, dq, dk, dv, dbeta4, dg4, dp_buf):
    t = p.shape[0]
    tm = _token_tile(t)
    w3 = 3 * DN_WIDTH

    def body(x_ref, halo_ref, pbd_ref, w_ref, alog_ref, dtb_ref, dq_ref, dk_ref, dv_ref, dbeta_ref, dg_ref, _,
             dc_ref, dw_ref, dpbd_ref, dalog_ref, ddtb_ref, dw_acc, lane_acc):
        first = pl.program_id(0) == 0
        dw_acc[...] = jnp.zeros_like(dw_acc)
        lane_acc[...] = jnp.zeros_like(lane_acc)

        def strip(row0):
            rows = pl.ds(row0, STRIP)
            for h in range(N_HEADS):
                sl = slice(h * HEAD_DIM, (h + 1) * HEAD_DIM)
                for part, dy_ref in ((0, dq_ref), (1, dk_ref), (2, dv_ref)):
                    cols = slice(part * DN_WIDTH + h * HEAD_DIM, part * DN_WIDTH + (h + 1) * HEAD_DIM)
                    delayed = _delays(_history(x_ref, halo_ref, first, row0, cols), CONV_K)
                    c = _causal_conv(delayed, w_ref[:, cols])
                    dy = dy_ref[rows, sl]
                    if part < 2:
                        y = silu(c)
                        _, r = l2_fwd(y)
                        dy = l2_bwd(y, r, dy * (HEAD_DIM ** -0.5) if part == 0 else dy)
                    dc = dy * silu_grad(c)
                    dc_ref[rows, cols] = dc
                    for j in range(CONV_K):
                        k = CONV_K - 1 - j
                        dw_acc[k * SUBLANES:(k + 1) * SUBLANES, cols] += _fold_rows(dc * delayed[j])
            head = lax.broadcasted_iota(jnp.int32, (STRIP, LANES), 1) < N_HEADS
            pbd = pbd_ref[rows, :]
            beta = sigmoid(pbd)
            dpb = jnp.where(head, dbeta_ref[rows, :] * beta * (1.0 - beta), 0.0)
            z = pltpu.roll(pbd, LANES - N_HEADS, 1) + dtb_ref[...]
            neg_rate = -jnp.exp(alog_ref[...])
            dg = dg_ref[rows, :]
            dpa = jnp.where(head, dg * neg_rate * sigmoid(z), 0.0)
            dpbd_ref[rows, :] = (dpb + pltpu.roll(dpa, N_HEADS, 1)).astype(BF16)
            g = jnp.where(head, neg_rate * softplus(z), 0.0)
            lane_acc[0:SUBLANES, :] += _fold_rows(dg * g)
            lane_acc[SUBLANES:, :] += _fold_rows(dpa)

        _first_then_strips(tm, strip)

        @pl.when(first)
        def _():
            dw_ref[...] = jnp.zeros_like(dw_ref)
            dalog_ref[...] = jnp.zeros_like(dalog_ref)
            ddtb_ref[...] = jnp.zeros_like(ddtb_ref)

        for k in range(CONV_K):
            dw_ref[k:k + 1, :] += jnp.sum(dw_acc[k * SUBLANES:(k + 1) * SUBLANES, :], axis=0, keepdims=True)
        dalog_ref[...] += jnp.sum(lane_acc[0:SUBLANES, :], axis=0, keepdims=True)
        ddtb_ref[...] += jnp.sum(lane_acc[SUBLANES:, :], axis=0, keepdims=True)

    tok = lambda w, cb: pl.BlockSpec((tm, w), lambda i: (i, cb))
    full = lambda shape: pl.BlockSpec(shape, lambda i: (0, 0))
    return _pcall(
        body, grid=(t // tm,),
        in_specs=[tok(w3, 0), _prev_halo_spec(tm, w3, 0), tok(LANES, PROJ_MAIN // LANES),
                  full(conv_w.shape), full(a_log4.shape), full(dt_bias4.shape)] + [tok(DN_WIDTH, 0)] * 3 + [tok(LANES, 0)] * 2
        + [pl.BlockSpec(memory_space=pl.ANY)],
        out_specs=[tok(w3, 0), full((CONV_K, w3)), tok(LANES, PROJ_MAIN // LANES), full((1, LANES)), full((1, LANES))],
        out_shape=[jax.ShapeDtypeStruct((t, w3), F32), jax.ShapeDtypeStruct((CONV_K, w3), F32),
                   jax.ShapeDtypeStruct(dp_buf.shape, dp_buf.dtype),
                   jax.ShapeDtypeStruct((1, LANES), F32), jax.ShapeDtypeStruct((1, LANES), F32)],
        input_output_aliases={11: 2},
        scratch_shapes=[pltpu.VMEM((CONV_K * SUBLANES, w3), F32), pltpu.VMEM((2 * SUBLANES, LANES), F32)],
        compiler_params=_params("arbitrary"), name="dn_prep_bwd")(p, p, p, conv_w, a_log4, dt_bias4, dq, dk, dv, dbeta4, dg4, dp_buf)


def _conv_bwd_input(dc, w, name, out_cols=None, col_block=0, into=None):
    t, c = dc.shape
    taps = w.shape[0]
    tm = _token_tile(t)
    ct = _pick(c, 1536)
    n_tok = t // tm
    out_cols = c if out_cols is None else out_cols

    def body(dc_ref, next_ref, w_ref, *rest):
        dx_ref = rest[-1]
        last = pl.program_id(0) == n_tok - 1

        def strip(row0):
            for c0 in range(0, ct, LANES):
                cols = slice(c0, c0 + LANES)
                dx_ref[pl.ds(row0, STRIP), cols] = _advanced_conv(dc_ref, row0, cols, w_ref[:, cols]).astype(BF16)

        _for_strips(tm - STRIP, STRIP, strip)
        for c0 in range(0, ct, LANES):
            cols = slice(c0, c0 + LANES)
            ext = jnp.concatenate([dc_ref[tm - STRIP:tm, cols], jnp.where(last, 0.0, next_ref[:, cols])], axis=0)
            dx_ref[tm - STRIP:tm, cols] = _advanced(ext, w_ref[:, cols]).astype(BF16)

    in_specs = [pl.BlockSpec((tm, ct), lambda i, j: (i, j)),
                pl.BlockSpec((HALO, ct), lambda i, j: (jnp.minimum((i + 1) * (tm // HALO), t // HALO - 1), j)),
                pl.BlockSpec((taps, ct), lambda i, j: (0, j))]
    args = (dc, dc, w)
    aliases = {}
    if into is not None:
        in_specs.append(pl.BlockSpec(memory_space=pl.ANY))
        args += (into,)
        aliases = {3: 0}
    return _pcall(
        body, grid=(n_tok, c // ct), in_specs=in_specs,
        out_specs=pl.BlockSpec((tm, ct), lambda i, j: (i, j + col_block)),
        out_shape=jax.ShapeDtypeStruct((t, out_cols), BF16), input_output_aliases=aliases,
        compiler_params=_params("parallel", "parallel"), name=name)(*args)


def _dn_forward(q, k, v, beta4, g4, p, norm_g):
    t = q.shape[0]
    n = t // CHUNK
    nc = DN_FWD_CHUNKS
    rows_per_step = nc * CHUNK

    def body(q_ref, k_ref, v_ref, b_ref, g_ref, gate_ref, ng_ref, mix_ref, s_all_ref, ainv_ref, s_ref):
        @pl.when(pl.program_id(0) == 0)
        def _():
            s_ref[...] = jnp.zeros_like(s_ref)

        chunks = []
        for c in range(nc):
            rows = slice(c * CHUNK, (c + 1) * CHUNK)
            chunks.append((_stack_heads(q_ref[rows, :]), _stack_heads(k_ref[rows, :]), _stack_heads(v_ref[rows, :]),
                           _stack_lanes(b_ref[rows, :]), chunk_cumsum(g_ref[rows, :])))
        locs = dn_chunks_local(chunks)
        s = [s_ref[h] for h in range(N_HEADS)]
        for c in range(nc):
            rows = slice(c * CHUNK, (c + 1) * CHUNK)
            ainv_ref[c] = locs[c]["a_inv"].astype(BF16)
            for h in range(N_HEADS):
                s_all_ref[c, h] = s[h]
            o, s = dn_chunk_state(locs[c], s)
            o_n, _ = rms_fwd(o, ng_ref[...])
            for h in range(N_HEADS):
                sl = slice(h * HEAD_DIM, (h + 1) * HEAD_DIM)
                mix_ref[rows, sl] = (o_n[_head_rows(h)] * silu(gate_ref[rows, sl])).astype(BF16)
        for h in range(N_HEADS):
            s_ref[h] = s[h]

    ch = lambda w, cb: pl.BlockSpec((rows_per_step, w), lambda i: (i, cb))
    per_chunk = lambda *shape: pl.BlockSpec((nc,) + shape, lambda i: (i,) + (0,) * len(shape))
    return _pcall(
        body, grid=(n // nc,),
        in_specs=[ch(DN_WIDTH, 0)] * 3 + [ch(LANES, 0)] * 2 + [ch(DN_WIDTH, 3), pl.BlockSpec((1, HEAD_DIM), lambda i: (0, 0))],
        out_specs=[ch(DN_WIDTH, 0), per_chunk(N_HEADS, HEAD_DIM, HEAD_DIM), per_chunk(STACK, STACK)],
        out_shape=[jax.ShapeDtypeStruct((t, DN_WIDTH + SG_WIDTH), BF16), jax.ShapeDtypeStruct((n, N_HEADS, HEAD_DIM, HEAD_DIM), F32),
                   jax.ShapeDtypeStruct((n, STACK, STACK), BF16)],
        scratch_shapes=[pltpu.VMEM((N_HEADS, HEAD_DIM, HEAD_DIM), F32)],
        compiler_params=_params("arbitrary"), name="dn_forward")(q, k, v, beta4, g4, p, norm_g)


def _dn_backward(q, k, v, beta4, g4, p, norm_g, saved, dmix, dp_buf):
    t = q.shape[0]
    n = t // CHUNK
    steps = n // DN_CHUNKS
    rows_per_step = DN_CHUNKS * CHUNK

    def body(q_ref, k_ref, v_ref, b_ref, g_ref, gate_ref, ng_ref, s_in_ref, ainv_ref, dmix_ref, _,
             dq_ref, dk_ref, dv_ref, db_ref, dg_ref, dgate_ref, dng_ref, ds_ref):
        @pl.when(pl.program_id(0) == 0)
        def _():
            ds_ref[...] = jnp.zeros_like(ds_ref)
            dng_ref[...] = jnp.zeros_like(dng_ref)

        chunks = []
        for c in range(DN_CHUNKS):
            rows = slice(c * CHUNK, (c + 1) * CHUNK)
            chunks.append((_stack_heads(q_ref[rows, :]), _stack_heads(k_ref[rows, :]), _stack_heads(v_ref[rows, :]),
                           _stack_lanes(b_ref[rows, :]), chunk_cumsum(g_ref[rows, :])))
        items = []
        for c, loc in enumerate(dn_chunks_local(chunks, [ainv_ref[c] for c in range(DN_CHUNKS)])):
            rows = slice(c * CHUNK, (c + 1) * CHUNK)
            s = [s_in_ref[c, h] for h in range(N_HEADS)]
            o, _ = dn_chunk_state(loc, s)
            o_n, r = rms_fwd(o, ng_ref[...])
            gate = _stack_heads(gate_ref[rows, :])
            dmx = _stack_heads(dmix_ref[rows, :])
            dgate = dmx * o_n * silu_grad(gate)
            do, dng_rows = rms_bwd(o, r, ng_ref[...], dmx * silu(gate))
            dng_ref[...] += jnp.sum(dng_rows, axis=0, keepdims=True)
            for h in range(N_HEADS):
                dgate_ref[rows, h * HEAD_DIM:(h + 1) * HEAD_DIM] = dgate[_head_rows(h)].astype(BF16)
            items.append((*chunks[c][:4], loc, s, do))
        grads, ds = dn_chunks_bwd(items, [ds_ref[h] for h in range(N_HEADS)])
        lane = lax.broadcasted_iota(jnp.int32, (CHUNK, LANES), 1)
        _, strict = _tri_masks(CHUNK)
        for c in range(DN_CHUNKS):
            rows = slice(c * CHUNK, (c + 1) * CHUNK)
            dq, dk, dv, dbeta, dgc = grads[c]
            db4 = jnp.zeros((CHUNK, LANES), F32)
            dgc4 = jnp.zeros((CHUNK, LANES), F32)
            for h in range(N_HEADS):
                sl = slice(h * HEAD_DIM, (h + 1) * HEAD_DIM)
                head_rows = _head_rows(h)
                dq_ref[rows, sl] = dq[head_rows]
                dk_ref[rows, sl] = dk[head_rows]
                dv_ref[rows, sl] = dv[head_rows]
                db4 = jnp.where(lane == h, dbeta[head_rows], db4)
                dgc4 = jnp.where(lane == h, dgc[head_rows], dgc4)
            db_ref[rows, :] = db4
            dg_ref[rows, :] = dot_nn(jnp.logical_not(strict).astype(F32), dgc4)
        for h in range(N_HEADS):
            ds_ref[h] = ds[h]

    rev = lambda w, cb: pl.BlockSpec((rows_per_step, w), lambda i: (steps - 1 - i, cb))
    per_chunk = lambda a: pl.BlockSpec((DN_CHUNKS,) + a.shape[1:], lambda i: (steps - 1 - i,) + (0,) * (a.ndim - 1))
    return _pcall(
        body, grid=(steps,),
        in_specs=[rev(DN_WIDTH, 0)] * 3 + [rev(LANES, 0)] * 2 + [rev(DN_WIDTH, 3), pl.BlockSpec((1, HEAD_DIM), lambda i: (0, 0))]
        + [per_chunk(a) for a in saved] + [rev(DN_WIDTH, 0), pl.BlockSpec(memory_space=pl.ANY)],
        out_specs=[rev(DN_WIDTH, 0)] * 3 + [rev(LANES, 0)] * 2 + [rev(DN_WIDTH, 3), pl.BlockSpec((1, HEAD_DIM), lambda i: (0, 0))],
        out_shape=[jax.ShapeDtypeStruct((t, DN_WIDTH), F32)] * 3 + [jax.ShapeDtypeStruct((t, LANES), F32)] * 2
        + [jax.ShapeDtypeStruct(dp_buf.shape, dp_buf.dtype), jax.ShapeDtypeStruct((1, HEAD_DIM), F32)],
        input_output_aliases={10: 5},
        scratch_shapes=[pltpu.VMEM((N_HEADS, HEAD_DIM, HEAD_DIM), F32)],
        compiler_params=_params("arbitrary"), name="dn_backward")(q, k, v, beta4, g4, p, norm_g, *saved, dmix, dp_buf)


SG_STEP_BLOCKS = 4


def _sg_mask():
    row = lax.broadcasted_iota(jnp.int32, (SG_BLOCK, SG_BLOCK), 0)
    col = lax.broadcasted_iota(jnp.int32, (SG_BLOCK, SG_BLOCK), 1)
    return (col // CHUNK) <= (row // CHUNK)


def _sg_forward(p, norm_g, w_s, b_t, mix_buf):
    t = p.shape[0]
    step_rows = SG_STEP_BLOCKS * SG_BLOCK

    def body(u_ref, v_ref, ng_ref, w_ref, b_ref, _, o_ref):
        mask = _sg_mask()
        pairs = [(slice(b * SG_BLOCK, (b + 1) * SG_BLOCK), g, slice(g * SG_DIM, (g + 1) * SG_DIM))
                 for b in range(SG_STEP_BLOCKS) for g in range(SG_GROUPS)]
        w_m = [jnp.where(mask, w_ref[g], 0.0) for g in range(SG_GROUPS)]
        vn = [rms_fwd(gelu(v_ref[rows, sl]), ng_ref[:, sl])[0] for rows, g, sl in pairs]
        s = [dot_nn(w_m[g], vn[i], FAST) + b_ref[:, g:g + 1] for i, (rows, g, sl) in enumerate(pairs)]
        for i, (rows, g, sl) in enumerate(pairs):
            o_ref[rows, sl] = (gelu(u_ref[rows, sl]) * s[i]).astype(BF16)

    blk = lambda cb: pl.BlockSpec((step_rows, SG_WIDTH), lambda i: (i, cb))
    return _pcall(
        body, grid=(t // step_rows,),
        in_specs=[blk(4), blk(5), pl.BlockSpec((1, SG_WIDTH), lambda i: (0, 0)),
                  pl.BlockSpec((SG_GROUPS, SG_BLOCK, SG_BLOCK), lambda i: (0, 0, 0)), pl.BlockSpec((SG_BLOCK, SG_GROUPS), lambda i: (0, 0)),
                  pl.BlockSpec(memory_space=pl.ANY)],
        out_specs=blk(1), out_shape=jax.ShapeDtypeStruct(mix_buf.shape, mix_buf.dtype), input_output_aliases={5: 0},
        compiler_params=_params("parallel"), name="sg_forward")(p, p, norm_g, w_s, b_t, mix_buf)


def _sg_backward(p, norm_g, w_s, b_t, dmix):
    t = p.shape[0]

    def body(u_ref, v_ref, ng_ref, w_ref, b_ref, do_ref, duv_ref, dng_ref, dw_ref, db_ref):
        @pl.when(pl.program_id(0) == 0)
        def _():
            dng_ref[...] = jnp.zeros_like(dng_ref)
            dw_ref[...] = jnp.zeros_like(dw_ref)
            db_ref[...] = jnp.zeros_like(db_ref)

        mask = _sg_mask()
        lane = lax.broadcasted_iota(jnp.int32, (SG_BLOCK, LANES), 1)
        pairs = [(slice(b * SG_BLOCK, (b + 1) * SG_BLOCK), g, slice(g * SG_DIM, (g + 1) * SG_DIM))
                 for b in range(SG_STEP_BLOCKS) for g in range(SG_GROUPS)]
        w_m = [jnp.where(mask, w_ref[g], 0.0) for g in range(SG_GROUPS)]
        vg = [gelu(v_ref[rows, sl]) for rows, g, sl in pairs]
        normed = [rms_fwd(vg[i], ng_ref[:, sl]) for i, (rows, g, sl) in enumerate(pairs)]
        s = [dot_nn(w_m[g], normed[i][0], FAST) + b_ref[:, g:g + 1] for i, (rows, g, sl) in enumerate(pairs)]
        ds = []
        db = jnp.zeros((SG_BLOCK, LANES), F32)
        for i, (rows, g, sl) in enumerate(pairs):
            u_raw, do = u_ref[rows, sl], do_ref[rows, sl]
            duv_ref[rows, sl] = (do * s[i] * gelu_grad(u_raw)).astype(BF16)
            ds.append(do * gelu(u_raw))
            db = db + jnp.where(lane == g, jnp.sum(ds[i], axis=1, keepdims=True), 0.0)
        dw = [jnp.where(mask, dot_nt(ds[i], normed[i][0], FAST), 0.0) for i in range(len(pairs))]
        dvn = [dot_tn(w_m[g], ds[i], FAST) for i, (rows, g, sl) in enumerate(pairs)]
        for i, (rows, g, sl) in enumerate(pairs):
            dw_ref[g] += dw[i]
            dvg, dng_rows = rms_bwd(vg[i], normed[i][1], ng_ref[:, sl], dvn[i])
            dng_ref[:, sl] += jnp.sum(dng_rows, axis=0, keepdims=True)
            duv_ref[rows, SG_WIDTH + g * SG_DIM:SG_WIDTH + (g + 1) * SG_DIM] = (dvg * gelu_grad(v_ref[rows, sl])).astype(BF16)
        db_ref[...] += db

    step_rows = SG_STEP_BLOCKS * SG_BLOCK
    blk = lambda cb: pl.BlockSpec((step_rows, SG_WIDTH), lambda i: (i, cb))
    const2 = lambda shape: pl.BlockSpec(shape, lambda i: (0, 0))
    w_spec = pl.BlockSpec((SG_GROUPS, SG_BLOCK, SG_BLOCK), lambda i: (0, 0, 0))
    return _pcall(
        body, grid=(t // step_rows,),
        in_specs=[blk(4), blk(5), const2((1, SG_WIDTH)), w_spec, const2((SG_BLOCK, SG_GROUPS)), blk(1)],
        out_specs=[pl.BlockSpec((step_rows, 2 * SG_WIDTH), lambda i: (i, 2)), const2((1, SG_WIDTH)), w_spec,
                   const2((SG_BLOCK, LANES))],
        out_shape=[jax.ShapeDtypeStruct((t, PROJ_PAD), BF16), jax.ShapeDtypeStruct((1, SG_WIDTH), F32),
                   jax.ShapeDtypeStruct((SG_GROUPS, SG_BLOCK, SG_BLOCK), F32), jax.ShapeDtypeStruct((SG_BLOCK, LANES), F32)],
        compiler_params=_params("arbitrary"), name="sg_backward")(p, p, norm_g, w_s, b_t, dmix)


FFN_COLS = 256


def _norm_up_ffn(x, g, w_up_t, conv_w, conv_b):
    t, d = x.shape
    tm = min(t, 256)
    blocks = D_FF // FFN_COLS
    nt = (((1,), (1,)), ((), ()))

    def body(x_ref, g_ref, w_ref, cw_ref, cb_ref, up_ref, act_ref, h_ref, r_ref, tail_ref, prev_ref):
        @pl.when(pl.program_id(0) == 0)
        def _():
            tail_ref[...] = jnp.zeros_like(tail_ref)

        y, r = rms_fwd(x_ref[...], g_ref[...])
        h = y.astype(BF16)
        h_ref[...] = h
        r_ref[...] = r

        def project(blk):
            out = []
            for half in range(2):
                cols = slice(half * D_FF + blk * FFN_COLS, half * D_FF + (blk + 1) * FFN_COLS)
                u = lax.dot_general(h, w_ref[cols, :], nt, preferred_element_type=F32)
                up_ref[:, cols] = u
                prev_ref[:, cols] = tail_ref[:, cols]
                tail_ref[:, cols] = u[tm - HALO:, :]
                out.append(cols)
            return out

        def history(row0, cols):
            if row0 == 0:
                return jnp.concatenate([prev_ref[:, cols], up_ref[0:STRIP, cols]], axis=0)
            return up_ref[row0 - HALO:row0 + STRIP, cols]

        def activate(blk, g_cols, v_cols):
            for row0 in range(0, tm, STRIP):
                for c0 in range(0, FFN_COLS, LANES):
                    gc = slice(g_cols.start + c0, g_cols.start + c0 + LANES)
                    vc = slice(v_cols.start + c0, v_cols.start + c0 + LANES)
                    cg = _causal_conv(_delays(history(row0, gc), FFN_CONV), cw_ref[:, gc]) + cb_ref[:, gc]
                    cv = _causal_conv(_delays(history(row0, vc), FFN_CONV), cw_ref[:, vc]) + cb_ref[:, vc]
                    act_ref[row0:row0 + STRIP, blk * FFN_COLS + c0:blk * FFN_COLS + c0 + LANES] = (silu(cg) * cv).astype(BF16)

        pending = None
        for blk in range(blocks):
            cols = project(blk)
            if pending is not None:
                activate(*pending)
            pending = (blk, *cols)
        activate(*pending)

    rows = lambda w: pl.BlockSpec((tm, w), lambda i: (i, 0))
    whole = lambda a: pl.BlockSpec(a.shape, lambda i: (0, 0))
    return _pcall(
        body, grid=(t // tm,),
        in_specs=[rows(d), whole(g), whole(w_up_t), whole(conv_w), whole(conv_b)],
        out_specs=[rows(2 * D_FF), rows(D_FF), rows(d), rows(1)],
        out_shape=[jax.ShapeDtypeStruct((t, 2 * D_FF), F32), jax.ShapeDtypeStruct((t, D_FF), BF16),
                   jax.ShapeDtypeStruct((t, d), BF16), jax.ShapeDtypeStruct((t, 1), F32)],
        scratch_shapes=[pltpu.VMEM((HALO, 2 * D_FF), F32), pltpu.VMEM((HALO, 2 * D_FF), F32)],
        compiler_params=_params("arbitrary"), name="norm_up_ffn")(x, g, w_up_t, conv_w, conv_b)


def _ffn_bwd(up, conv_w, conv_b, dact):
    t = up.shape[0]
    tm = _pick(t, 256)
    n_tok = t // tm
    width = 2 * D_FF

    def dconv(delayed_g, delayed_v, da, wg, wv, bg, bv):
        cg = _causal_conv(delayed_g, wg) + bg
        cv = _causal_conv(delayed_v, wv) + bv
        s = sigmoid(cg)
        return da * cv * (s * (1.0 + cg * (1.0 - s))), da * (cg * s)

    def body(up_ref, prev_ref, next_ref, da_ref, dan_ref, w_ref, b_ref, dup_ref, dw_ref, db_ref, dc_ref, dw_acc, db_acc):
        first = pl.program_id(0) == 0
        last = pl.program_id(0) == n_tok - 1
        dw_acc[...] = jnp.zeros_like(dw_acc)
        db_acc[...] = jnp.zeros_like(db_acc)

        def strip(row0):
            rows = pl.ds(row0, STRIP)
            for c0 in range(0, D_FF, LANES):
                gc, vc = slice(c0, c0 + LANES), slice(D_FF + c0, D_FF + c0 + LANES)
                del_g = _delays(_history(up_ref, prev_ref, first, row0, gc), FFN_CONV)
                del_v = _delays(_history(up_ref, prev_ref, first, row0, vc), FFN_CONV)
                dcg, dcv = dconv(del_g, del_v, da_ref[rows, gc], w_ref[:, gc], w_ref[:, vc], b_ref[:, gc], b_ref[:, vc])
                dc_ref[rows, gc] = dcg
                dc_ref[rows, vc] = dcv
                db_acc[:, gc] += _fold_rows(dcg)
                db_acc[:, vc] += _fold_rows(dcv)
                for j in range(FFN_CONV):
                    k = FFN_CONV - 1 - j
                    dw_acc[k * SUBLANES:(k + 1) * SUBLANES, gc] += _fold_rows(dcg * del_g[j])
                    dw_acc[k * SUBLANES:(k + 1) * SUBLANES, vc] += _fold_rows(dcv * del_v[j])

        _first_then_strips(tm, strip)

        for c0 in range(0, D_FF, LANES):
            gc, vc = slice(c0, c0 + LANES), slice(D_FF + c0, D_FF + c0 + LANES)

            def delayed(cols):
                return _delays(jnp.concatenate([up_ref[tm - HALO:tm, cols], next_ref[:, cols]], axis=0), FFN_CONV)

            dcg, dcv = dconv(delayed(gc), delayed(vc), dan_ref[:, gc], w_ref[:, gc], w_ref[:, vc], b_ref[:, gc], b_ref[:, vc])
            dc_ref[tm:, gc] = jnp.where(last, 0.0, dcg)
            dc_ref[tm:, vc] = jnp.where(last, 0.0, dcv)

        def strip_dx(row0):
            for c0 in range(0, width, LANES):
                cols = slice(c0, c0 + LANES)
                dup_ref[pl.ds(row0, STRIP), cols] = _advanced_conv(dc_ref, row0, cols, w_ref[:, cols]).astype(BF16)

        _for_strips(tm, STRIP, strip_dx)

        @pl.when(first)
        def _():
            dw_ref[...] = jnp.zeros_like(dw_ref)
            db_ref[...] = jnp.zeros_like(db_ref)

        for k in range(FFN_CONV):
            dw_ref[k:k + 1, :] += jnp.sum(dw_acc[k * SUBLANES:(k + 1) * SUBLANES, :], axis=0, keepdims=True)
        db_ref[...] += jnp.sum(db_acc[...], axis=0, keepdims=True)

    next_rows = lambda i: jnp.minimum((i + 1) * (tm // HALO), t // HALO - 1)
    full = lambda rows: pl.BlockSpec((rows, width), lambda i: (0, 0))
    return _pcall(
        body, grid=(n_tok,),
        in_specs=[pl.BlockSpec((tm, width), lambda i: (i, 0)),
                  pl.BlockSpec((HALO, width), lambda i: (jnp.maximum(i * (tm // HALO) - 1, 0), 0)),
                  pl.BlockSpec((HALO, width), lambda i: (next_rows(i), 0)),
                  pl.BlockSpec((tm, D_FF), lambda i: (i, 0)), pl.BlockSpec((HALO, D_FF), lambda i: (next_rows(i), 0)),
                  full(FFN_CONV), full(1)],
        out_specs=[pl.BlockSpec((tm, width), lambda i: (i, 0)), full(FFN_CONV), full(1)],
        out_shape=[jax.ShapeDtypeStruct((t, width), BF16), jax.ShapeDtypeStruct((FFN_CONV, width), F32),
                   jax.ShapeDtypeStruct((1, width), F32)],
        scratch_shapes=[pltpu.VMEM((tm + HALO, width), F32),
                        pltpu.VMEM((FFN_CONV * SUBLANES, width), F32), pltpu.VMEM((SUBLANES, width), F32)],
        compiler_params=_params("arbitrary"), name="ffn_bwd")(up, up, up, dact, dact, conv_w, conv_b)


def _my_position():
    return lax.axis_index("x"), lax.axis_index("y"), lax.axis_index("c")


COPIES = N_DEV - 1


def _all_gather(arrays):
    n = len(arrays)

    def body(*refs):
        x_refs, out_refs = refs[:n], refs[n:2 * n]
        send_sems, recv_sems, local_sems = refs[2 * n:]
        x, y, cc = _my_position()
        me, sibling = (x, y, cc), (x, y, 1 - cc)
        chips = [(1 - x, y), (x, 1 - y), (1 - x, 1 - y)]

        def block(a, px, py, pc):
            return out_refs[a].at[4 * px + 2 * py + pc]

        def copy(a, k, blk, to, src=None):
            return pltpu.make_async_remote_copy(
                src_ref=block(a, *blk) if src is None else src, dst_ref=block(a, *blk),
                send_sem=send_sems.at[a * COPIES + k], recv_sem=recv_sems.at[a * COPIES + k],
                device_id=to, device_id_type=MESH_ID)

        mine = [pltpu.make_async_copy(x_refs[a], block(a, *me), local_sems.at[a]) for a in range(n)]
        for cp in mine:
            cp.start()
        first = []
        for a in range(n):
            first.append(copy(a, 0, me, sibling, src=x_refs[a]))
            first += [copy(a, 1 + j, me, (*chip, cc), src=x_refs[a]) for j, chip in enumerate(chips)]
        for cp in first:
            cp.start()
        passed = []
        for j, chip in enumerate(chips):
            for a in range(n):
                copy(a, 1 + j, (*chip, cc), me).wait_recv()
                passed.append(copy(a, 4 + j, (*chip, cc), sibling))
                passed[-1].start()
        for a in range(n):
            copy(a, 0, sibling, me).wait_recv()
        for j, chip in enumerate(chips):
            for a in range(n):
                copy(a, 4 + j, (*chip, 1 - cc), me).wait_recv()
        for cp in first + passed:
            cp.wait_send()
        for cp in mine:
            cp.wait()

    any_spec = pl.BlockSpec(memory_space=pl.ANY)
    return _pcall(
        body, out_shape=[jax.ShapeDtypeStruct((N_DEV,) + a.shape, a.dtype) for a in arrays],
        in_specs=[any_spec] * n, out_specs=[any_spec] * n,
        scratch_shapes=[pltpu.SemaphoreType.DMA((n * COPIES,)), pltpu.SemaphoreType.DMA((n * COPIES,)),
                        pltpu.SemaphoreType.DMA((n,))],
        name="all_gather")(*arrays)


def _all_to_all(sends):
    n = len(sends)

    def body(*refs):
        send_refs, recv_refs = refs[:n], refs[n:2 * n]
        send_sems, recv_sems, local_sems = refs[2 * n:]
        x, y, cc = _my_position()
        me = 4 * x + 2 * y + cc
        mine = [pltpu.make_async_copy(send_refs[a].at[me], recv_refs[a].at[me], local_sems.at[a]) for a in range(n)]
        for cp in mine:
            cp.start()
        copies = []
        for rel in range(1, N_DEV):
            px, py, pc = x ^ (rel >> 2), y ^ ((rel >> 1) & 1), cc ^ (rel & 1)
            for a in range(n):
                copies.append(pltpu.make_async_remote_copy(
                    src_ref=send_refs[a].at[4 * px + 2 * py + pc], dst_ref=recv_refs[a].at[me],
                    send_sem=send_sems.at[a * COPIES + rel - 1], recv_sem=recv_sems.at[a * COPIES + rel - 1],
                    device_id=(px, py, pc), device_id_type=MESH_ID))
        for cp in copies:
            cp.start()
        for cp in copies:
            cp.wait()
        for cp in mine:
            cp.wait()

    any_spec = pl.BlockSpec(memory_space=pl.ANY)
    return _pcall(
        body, out_shape=[jax.ShapeDtypeStruct(s.shape, s.dtype) for s in sends],
        in_specs=[any_spec] * n, out_specs=[any_spec] * n,
        scratch_shapes=[pltpu.SemaphoreType.DMA((n * COPIES,)), pltpu.SemaphoreType.DMA((n * COPIES,)),
                        pltpu.SemaphoreType.DMA((n,))],
        name="all_to_all")(*sends)


def _hbm(a):
    return pltpu.with_memory_space_constraint(a, pltpu.HBM)


def _split_copies(send_refs, land_refs, send_sems, recv_sems, local_sems, gather):
    x, y, cc = _my_position()
    me = 4 * x + 2 * y + cc
    local, remote = [], []
    for a, (send, land) in enumerate(zip(send_refs, land_refs)):
        local.append(pltpu.make_async_copy(send if gather else send.at[me], land.at[me], local_sems.at[a]))
    for a, (send, land) in enumerate(zip(send_refs, land_refs)):
        for rel in range(1, N_DEV):
            px, py, pc = x ^ (rel >> 2), y ^ ((rel >> 1) & 1), cc ^ (rel & 1)
            remote.append(pltpu.make_async_remote_copy(
                src_ref=send if gather else send.at[4 * px + 2 * py + pc], dst_ref=land.at[me],
                send_sem=send_sems.at[a * COPIES + rel - 1], recv_sem=recv_sems.at[a * COPIES + rel - 1],
                device_id=(px, py, pc), device_id_type=MESH_ID))
    return local, remote


SPLIT_EFFECT = pltpu.SideEffectType.DATAFLOW_SIDE_EFFECTING


def _exchange_start(sends, after, gather, name):
    n = len(sends)
    lands = [_hbm(lax.empty((N_DEV,) + s.shape if gather else s.shape, s.dtype)) for s in sends]

    def body(*refs):
        send_refs, land_refs = refs[:n], refs[n:2 * n]
        send_sems, recv_sems, local_sems = refs[2 * n + 1:2 * n + 4]
        token = refs[-1]
        local, remote = _split_copies(send_refs, land_refs, send_sems, recv_sems, local_sems, gather)
        for cp in local + remote:
            cp.start()
        token[...] = jnp.zeros_like(token)

    hbm, sem = pl.BlockSpec(memory_space=pltpu.HBM), pl.BlockSpec(memory_space=pltpu.SEMAPHORE)
    out = _pcall(
        body, name=name,
        out_shape=[pltpu.SemaphoreType.DMA((n * COPIES,)), pltpu.SemaphoreType.DMA((n * COPIES,)), pltpu.SemaphoreType.DMA((n,))]
        + [pltpu.HBM(s.shape, s.dtype) for s in sends] + [pltpu.HBM(z.shape, z.dtype) for z in lands]
        + [jax.ShapeDtypeStruct((SUBLANES, LANES), F32)],
        in_specs=[hbm] * (2 * n) + [pl.BlockSpec(memory_space=pl.ANY)],
        out_specs=[sem] * 3 + [hbm] * (2 * n) + [pl.BlockSpec(memory_space=pltpu.VMEM)],
        input_output_aliases={i: 3 + i for i in range(2 * n)},
        compiler_params=pltpu.CompilerParams(has_side_effects=SPLIT_EFFECT),
    )(*[_hbm(s) for s in sends], *lands, after)
    return dict(sems=out[:3], sends=out[3:3 + n], lands=out[3 + n:3 + 2 * n], gather=gather), out[-1]


def _exchange_wait(handle, after, name):
    sends, lands, gather = handle["sends"], handle["lands"], handle["gather"]
    n = len(sends)

    def body(*refs):
        send_refs, land_refs = refs[:n], refs[n:2 * n]
        send_sems, recv_sems, local_sems = refs[2 * n:2 * n + 3]
        local, remote = _split_copies(send_refs, land_refs, send_sems, recv_sems, local_sems, gather)
        for cp in remote:
            cp.wait_send()
            cp.wait_recv()
        for cp in local:
            cp.wait()

    hbm, sem = pl.BlockSpec(memory_space=pltpu.HBM), pl.BlockSpec(memory_space=pltpu.SEMAPHORE)
    out = _pcall(
        body, name=name,
        out_shape=[pltpu.HBM(s.shape, s.dtype) for s in sends] + [pltpu.HBM(z.shape, z.dtype) for z in lands],
        in_specs=[hbm] * (2 * n) + [sem] * 3 + [pl.BlockSpec(memory_space=pl.ANY)],
        out_specs=[hbm] * (2 * n), input_output_aliases={i: i for i in range(2 * n)},
        compiler_params=pltpu.CompilerParams(has_side_effects=SPLIT_EFFECT),
    )(*sends, *lands, *handle["sems"], after)
    return out[n:]


def _sum_and_adamw(recv, w, m, v, name):
    _, r, wp = recv.shape
    c = w.shape[-1]
    lead = w.ndim == 3
    tr = max([d for d in range(2 * SUBLANES, 257, 2 * SUBLANES) if r % d == 0], default=r)
    bc1 = 1.0 - ADAM_B1 ** ADAM_STEP
    bc2 = 1.0 - ADAM_B2 ** ADAM_STEP

    def body(recv_ref, w_ref, m_ref, v_ref, g_ref, d_ref, nm_ref, nv_ref):
        g = recv_ref[0, :, 0:c].astype(F32)
        for s in range(1, N_DEV):
            g = g + recv_ref[s, :, 0:c].astype(F32)
        m_new = ADAM_B1 * m_ref[...] + (1.0 - ADAM_B1) * g
        v_new = ADAM_B2 * v_ref[...] + (1.0 - ADAM_B2) * (g * g)
        m_hat = m_new / bc1
        v_hat = v_new / bc2
        g_ref[...] = g
        d_ref[...] = -ADAM_LR * (m_hat / (jnp.sqrt(v_hat) + ADAM_EPS) + ADAM_WD * w_ref[...])
        nm_ref[...] = m_new
        nv_ref[...] = v_new

    tile = pl.BlockSpec((None, tr, c), lambda i: (0, i, 0)) if lead else pl.BlockSpec((tr, c), lambda i: (i, 0))
    return _pcall(
        body, grid=(r // tr,),
        in_specs=[pl.BlockSpec((N_DEV, tr, wp), lambda i: (0, i, 0)), tile, tile, tile],
        out_specs=[tile] * 4, out_shape=[jax.ShapeDtypeStruct(w.shape, F32)] * 4,
        compiler_params=_params("parallel"), name=name)(recv, w, m, v)


SHARDED_TAPS = ("dn_conv_w", "ffn_conv_w")
REPLICATED = ("attn_norm_g", "dn_a_log", "dn_dt_bias", "dn_out_norm_g", "sg_norm_g", "sg_w", "sg_b", "ffn_norm_g",
              "ffn_conv_b", "final_norm_g")
SMALL = SHARDED_TAPS + REPLICATED
WEIGHT_ORDER = ("attn_norm_g", "w_in", "dn_conv_w", "dn_a_log", "dn_dt_bias", "dn_out_norm_g", "sg_norm_g", "sg_w", "sg_b",
                "w_out", "ffn_norm_g", "w_up", "ffn_conv_w", "ffn_conv_b", "w_down", "final_norm_g")
SLAB_COLS = 1024


def _pad_to(flat, multiple):
    pad = (-flat.shape[-1]) % multiple
    if pad == 0:
        return flat
    return jnp.pad(flat, [(0, 0)] * (flat.ndim - 1) + [(0, pad)])


def _pack_small(named):
    flat = jnp.concatenate([named[n].reshape(-1) for n in SMALL])
    return _pad_to(flat, SUBLANES * SLAB_COLS).reshape(-1, SLAB_COLS)


def _unpack_small(slab, like):
    flat = slab.reshape(-1)
    out, off = {}, 0
    for n in SMALL:
        size = like[n].size
        out[n] = flat[off:off + size].reshape(like[n].shape)
        off += size
    return out


def _split_columns(full, n_local):
    r = full.shape[0]
    return full.reshape(r, N_DEV, n_local).transpose(1, 0, 2).reshape(N_DEV, r * n_local)


def _join_columns(blocks, r, n_local):
    return blocks.reshape(N_DEV, r, n_local).transpose(1, 0, 2).reshape(r, N_DEV * n_local)


def _lanes4(a):
    return jnp.pad(a.reshape(1, N_HEADS), ((0, 0), (0, LANES - N_HEADS)))


def kernel(x, attn_norm_g, w_in, dn_conv_w, dn_a_log, dn_dt_bias, dn_out_norm_g, sg_norm_g, sg_w, sg_b, w_out, ffn_norm_g, w_up, ffn_conv_w, ffn_conv_b, w_down, final_norm_g, loss_target, m_attn_norm_g, m_w_in, m_dn_conv_w, m_dn_a_log, m_dn_dt_bias, m_dn_out_norm_g, m_sg_norm_g, m_sg_w, m_sg_b, m_w_out, m_ffn_norm_g, m_w_up, m_ffn_conv_w, m_ffn_conv_b, m_w_down, m_final_norm_g, v_attn_norm_g, v_w_in, v_dn_conv_w, v_dn_a_log, v_dn_dt_bias, v_dn_out_norm_g, v_sg_norm_g, v_sg_w, v_sg_b, v_w_out, v_ffn_norm_g, v_w_up, v_ffn_conv_w, v_ffn_conv_b, v_w_down, v_final_norm_g):
    weights = dict(attn_norm_g=attn_norm_g, w_in=w_in, dn_conv_w=dn_conv_w, dn_a_log=dn_a_log, dn_dt_bias=dn_dt_bias,
                   dn_out_norm_g=dn_out_norm_g, sg_norm_g=sg_norm_g, sg_w=sg_w, sg_b=sg_b, w_out=w_out, ffn_norm_g=ffn_norm_g,
                   w_up=w_up, ffn_conv_w=ffn_conv_w, ffn_conv_b=ffn_conv_b, w_down=w_down, final_norm_g=final_norm_g)
    m_in = dict(attn_norm_g=m_attn_norm_g, w_in=m_w_in, dn_conv_w=m_dn_conv_w, dn_a_log=m_dn_a_log, dn_dt_bias=m_dn_dt_bias,
                dn_out_norm_g=m_dn_out_norm_g, sg_norm_g=m_sg_norm_g, sg_w=m_sg_w, sg_b=m_sg_b, w_out=m_w_out,
                ffn_norm_g=m_ffn_norm_g, w_up=m_w_up, ffn_conv_w=m_ffn_conv_w, ffn_conv_b=m_ffn_conv_b, w_down=m_w_down,
                final_norm_g=m_final_norm_g)
    v_in = dict(attn_norm_g=v_attn_norm_g, w_in=v_w_in, dn_conv_w=v_dn_conv_w, dn_a_log=v_dn_a_log, dn_dt_bias=v_dn_dt_bias,
                dn_out_norm_g=v_dn_out_norm_g, sg_norm_g=v_sg_norm_g, sg_w=v_sg_w, sg_b=v_sg_b, w_out=v_w_out,
                ffn_norm_g=v_ffn_norm_g, w_up=v_w_up, ffn_conv_w=v_ffn_conv_w, ffn_conv_b=v_ffn_conv_b, w_down=v_w_down,
                final_norm_g=v_final_norm_g)

    n_in, n_up = w_in.shape[2], w_up.shape[2]
    r_out, r_down = w_out.shape[1], w_down.shape[1]
    n_dnc, n_ffc = dn_conv_w.shape[2], ffn_conv_w.shape[2]
    transposed = lambda a: jnp.transpose(a, (0, 2, 1))
    taps = _pad_to(jnp.concatenate([dn_conv_w.reshape(-1), ffn_conv_w.reshape(-1)]), SUBLANES * LANES).reshape(-1, LANES)
    g_in, g_taps = _all_gather([transposed(w_in)[0].astype(BF16), taps])
    gather_out, token = _exchange_start([w_out[0].astype(BF16)], g_taps, True, "gather_w_out")
    gather_up, token = _exchange_start([transposed(w_up)[0].astype(BF16)], token, True, "gather_w_up")
    gather_down, token = _exchange_start([w_down[0].astype(BF16)], token, True, "gather_w_down")
    w_in_t = jnp.pad(g_in.reshape(N_DEV * n_in, D_MODEL), ((0, PROJ_PAD - N_DEV * n_in), (0, 0)))
    taps_all = g_taps.reshape(N_DEV, -1)
    dn_conv_full = _join_columns(taps_all[:, :CONV_K * n_dnc], CONV_K, n_dnc)
    ffn_conv_full = _join_columns(taps_all[:, CONV_K * n_dnc:CONV_K * n_dnc + FFN_CONV * n_ffc], FFN_CONV, n_ffc)
    late = dict(
        w_out=lambda after: _exchange_wait(gather_out, after, "gather_w_out_wait")[0].reshape(N_DEV * r_out, D_MODEL),
        w_up_t=lambda after: _exchange_wait(gather_up, after, "gather_w_up_wait")[0].reshape(N_DEV * n_up, D_MODEL),
        w_down=lambda after: _exchange_wait(gather_down, after, "gather_w_down_wait")[0].reshape(N_DEV * r_down, D_MODEL))

    def send_early(blocks, after, name):
        return _exchange_start(blocks, after, False, name)

    def send_small(g, loss_lanes, after):
        small = jnp.concatenate([g[n].reshape(-1) for n in REPLICATED] + [loss_lanes[0, 0:1]])
        slab = jnp.concatenate([_split_columns(g["dn_conv_w"], n_dnc), _split_columns(g["ffn_conv_w"], n_ffc),
                                jnp.broadcast_to(small[None, :], (N_DEV, small.shape[0]))], axis=1)
        return send_early([_pad_to(slab, SUBLANES * SLAB_COLS).reshape(N_DEV, -1, SLAB_COLS)], after, "send_small")

    upd = {}

    def update_early(sent_down, sent_up_out, sent_small, after):
        r_dn, = _exchange_wait(sent_down, after, "send_dw_down_wait")
        r_up, r_o = _exchange_wait(sent_up_out, after, "send_dw_up_out_wait")
        r_small, = _exchange_wait(sent_small, after, "send_small_wait")
        upd["w_down"] = _sum_and_adamw(r_dn, w_down, m_w_down, v_w_down, "adamw_w_down")
        upd["w_up"] = [transposed(o) for o in _sum_and_adamw(r_up, transposed(w_up), transposed(m_w_up), transposed(v_w_up),
                                                             "adamw_w_up")]
        upd["w_out"] = _sum_and_adamw(r_o, w_out, m_w_out, v_w_out, "adamw_w_out")
        upd["small"] = _sum_and_adamw(r_small, _pack_small(weights), _pack_small(m_in), _pack_small(v_in), "adamw_small")

    grad_x, d_g1, sent_in = _local_step(
        x[0], loss_target[0], w_in_t, late, send_early, send_small, update_early, dn_conv_full, ffn_conv_full,
        attn_norm_g + token[0:1, 0:1], dn_a_log, dn_dt_bias, dn_out_norm_g, sg_norm_g, sg_w, sg_b, ffn_norm_g, ffn_conv_b,
        final_norm_g, n_in)

    norm_rows = D_MODEL // LANES
    r_g1, = _all_to_all([jnp.broadcast_to(d_g1.reshape(1, norm_rows, LANES), (N_DEV, norm_rows, LANES))])
    r_in, = _exchange_wait(sent_in, r_g1, "send_dw_in_wait")
    upd["w_in"] = [transposed(o) for o in _sum_and_adamw(r_in, transposed(w_in), transposed(m_w_in), transposed(v_w_in),
                                                         "adamw_w_in")]
    small_upd = upd.pop("small")
    as_rows = lambda a: a.reshape(norm_rows, LANES)
    norm_upd = _sum_and_adamw(r_g1, as_rows(attn_norm_g), as_rows(m_attn_norm_g), as_rows(v_attn_norm_g), "adamw_attn_norm")
    results = []
    for i in range(4):
        named = _unpack_small(small_upd[i], weights)
        named.update({n: upd[n][i] for n in upd})
        named["attn_norm_g"] = norm_upd[i].reshape(attn_norm_g.shape)
        results.append(named)

    loss = small_upd[0].reshape(-1)[sum(weights[n].size for n in SMALL)]
    return (loss, grad_x[None], *[r[n] for r in results for n in WEIGHT_ORDER])


def _local_step(x2d, tgt, w_in_t, late, send_early, send_small, update_early, dn_conv_full, ffn_conv_full, attn_norm_g,
                dn_a_log, dn_dt_bias, dn_out_norm_g, sg_norm_g, sg_w, sg_b, ffn_norm_g, ffn_conv_b, final_norm_g, n_in):
    t = x2d.shape[0]
    g1, g2, gf = attn_norm_g, ffn_norm_g, final_norm_g.reshape(1, D_MODEL)
    a_log4, dt_bias4 = _lanes4(dn_a_log), _lanes4(dn_dt_bias)
    sg_w3 = sg_w[0]
    sg_b_t = sg_b[0].T
    conv_b = ffn_conv_b

    p, h1, rstd1 = _rmsnorm_matmul(x2d, g1, w_in_t, "norm_in_proj", 512)
    q, k, v, beta4, g4 = _dn_prep(p, dn_conv_full, a_log4, dt_bias4)
    mix_half, *dn_saved = _dn_forward(q, k, v, beta4, g4, p, dn_out_norm_g)
    mix = _sg_forward(p, sg_norm_g, sg_w3, sg_b_t, mix_half)
    w_out_full = late["w_out"](mix)
    x2 = _matmul(mix, w_out_full, "nn", "out_proj", (1024, 1024, 1024), add=x2d)
    w_up_t = late["w_up_t"](x2)
    up, act, h2, rstd2 = _norm_up_ffn(x2, g2, w_up_t, ffn_conv_full, conv_b)
    w_down_full = late["w_down"](act)
    fn, outs = _final_loss_rows(t, D_MODEL)
    loss_lanes, dx3, dx3b, d_gf = _matmul_rows(act, w_down_full, "nn", "down_proj_loss", 512,
                                               [(x2, "rows"), (tgt, "rows"), (gf, "whole")], outs, fn)

    dact = _matmul(dx3b, w_down_full, "nt", "down_proj_dx", (512, D_FF, D_MODEL))
    d_w_down = _matmul(act, dx3b, "tn", "down_proj_dw", (256, 1024, t), out_dtype=BF16)
    sent_down, token = send_early([d_w_down.reshape(N_DEV, D_FF // N_DEV, D_MODEL)], d_w_down, "send_dw_down")
    dup, d_ffn_conv, d_ffn_conv_b = _ffn_bwd(up, ffn_conv_full, conv_b + token[0:1, 0:1], dact)
    fn, outs = _rmsnorm_bwd_rows(t, D_MODEL)
    dx2, dx2b, d_g2 = _matmul_rows(dup, w_up_t, "nn", "up_proj_dx_norm", 512,
                                   [(x2, "rows"), (rstd2, "rows"), (g2, "whole"), (dx3, "rows")], outs, fn)
    d_w_up_t = _matmul(dup, h2, "tn", "up_proj_dw", (512, 1024, t), out_dtype=BF16)
    dmix = _matmul(dx2b, w_out_full, "nt", "out_proj_dx", (1024, 1024, 1024))
    d_w_out = _matmul(mix, dx2b, "tn", "out_proj_dw", (512, 1024, t), out_dtype=BF16)
    sent_up_out, token = send_early(
        [d_w_up_t.reshape(N_DEV, 2 * D_FF // N_DEV, D_MODEL), d_w_out.reshape(N_DEV, D_MODEL // N_DEV, D_MODEL)],
        d_w_out, "send_dw_up_out")
    dp, d_sg_norm, d_sg_w, d_sg_b_t = _sg_backward(p, sg_norm_g + token[0:1, 0:1], sg_w3, sg_b_t, dmix)
    dq, dk, dv, dbeta4, dg4, dp, d_dn_norm = _dn_backward(q, k, v, beta4, g4, p, dn_out_norm_g, dn_saved, dmix, dp)
    dc_dn, d_dn_conv, dp, d_a_log4, d_dt_bias4 = _dn_prep_bwd(p, dn_conv_full, a_log4, dt_bias4, dq, dk, dv, dbeta4, dg4, dp)
    small_grads = dict(
        attn_norm_g=jnp.zeros_like(attn_norm_g), dn_conv_w=d_dn_conv, dn_a_log=d_a_log4[:, :N_HEADS],
        dn_dt_bias=d_dt_bias4[:, :N_HEADS], dn_out_norm_g=d_dn_norm, sg_norm_g=d_sg_norm, sg_w=d_sg_w,
        sg_b=d_sg_b_t[:, :SG_GROUPS].T, ffn_norm_g=d_g2, ffn_conv_w=d_ffn_conv, ffn_conv_b=d_ffn_conv_b, final_norm_g=d_gf)
    sent_small, token = send_small(small_grads, loss_lanes, d_dn_conv)
    dp = _conv_bwd_input(dc_dn, dn_conv_full + token[0:1, 0:1], "dn_conv_dx", out_cols=PROJ_PAD, into=dp)
    d_w_in_t = _matmul(dp, h1, "tn", "in_proj_dw", (PROJ_PAD // 5, 1024, t), out_dtype=BF16)
    sent_in, token = send_early([d_w_in_t[:N_DEV * n_in].reshape(N_DEV, n_in, D_MODEL)], d_w_in_t, "send_dw_in")
    update_early(sent_down, sent_up_out, sent_small, token)
    fn, outs = _rmsnorm_bwd_rows(t, D_MODEL)
    grad_x, _, d_g1 = _matmul_rows(dp, w_in_t, "nn", "in_proj_dx_norm", 512,
                                   [(x2d, "rows"), (rstd1, "rows"), (g1 + token[0:1, 0:1], "whole"), (dx2, "rows")], outs, fn)

    return grad_x, d_g1, sent_in
```

```python
import math

import jax
import jax.numpy as jnp
from jax import lax
from jax.experimental import pallas as pl
from jax.experimental.pallas import tpu as pltpu

F32 = jnp.float32
BF16 = jnp.bfloat16
HI = lax.Precision.HIGHEST

D_MODEL = 1024
DN_WIDTH = 512
HEAD_DIM = 128
N_HEADS = 4
SG_WIDTH = 512
SG_GROUPS = 4
SG_DIM = 128
SG_BLOCK = 128
D_FF = 2816
CHUNK = 64
CONV_K = 4
FFN_CONV = 3
EPS = 1e-6
PROJ_MAIN = 3072
PROJ_PAD = 3200
GELU_C = math.sqrt(2.0 / math.pi)
N_DEV = 8
LANES = 128
SUBLANES = 8
HALO = SUBLANES
VMEM_LIMIT = 48 * 1024 * 1024

ADAM_LR = 0.001
ADAM_B1 = 0.9
ADAM_B2 = 0.999
ADAM_EPS = 1e-08
ADAM_WD = 0.01
ADAM_STEP = 10

MESH_ID = pl.DeviceIdType.MESH


def _pcall(body, **kw):
    return pl.pallas_call(body, **kw)


def _params(*sem, fusible=None):
    return pltpu.CompilerParams(dimension_semantics=sem, vmem_limit_bytes=VMEM_LIMIT, allow_input_fusion=fusible)


def _pick(n, cap):
    best = None
    for t in range(LANES, cap + 1, LANES):
        if n % t == 0:
            best = t
    return best if best else n


FAST, EXACT = "bf16 operands, one pass", "f32 operands, six bf16 passes"


def dot_f32(a, b, dims, tier):
    if tier == FAST:
        return lax.dot_general(a.astype(BF16), b.astype(BF16), dims, preferred_element_type=F32)
    return lax.dot_general(a, b, dims, precision=HI, preferred_element_type=F32)


def dot_nn(a, b, tier=EXACT):
    return dot_f32(a, b, (((1,), (0,)), ((), ())), tier)


def dot_nt(a, b, tier=EXACT):
    return dot_f32(a, b, (((1,), (1,)), ((), ())), tier)


def dot_tn(a, b, tier=EXACT):
    return dot_f32(a, b, (((0,), (0,)), ((), ())), tier)


def sigmoid(x):
    return 0.5 * jnp.tanh(0.5 * x) + 0.5


def silu(x):
    return x * sigmoid(x)


def silu_grad(x):
    s = sigmoid(x)
    return s * (1.0 + x * (1.0 - s))


def gelu(x):
    return 0.5 * x * (1.0 + jnp.tanh(GELU_C * (x + 0.044715 * x * x * x)))


def gelu_grad(x):
    t = jnp.tanh(GELU_C * (x + 0.044715 * x * x * x))
    return 0.5 * (1.0 + t) + 0.5 * x * (1.0 - t * t) * GELU_C * (1.0 + 3.0 * 0.044715 * x * x)


def softplus(z):
    return jnp.maximum(z, 0.0) + jnp.log(1.0 + jnp.exp(-jnp.abs(z)))


def rms_fwd(x, g):
    r = lax.rsqrt(jnp.mean(x * x, axis=-1, keepdims=True) + EPS)
    return x * r * g, r


def rms_bwd(x, r, g, dy):
    dyg = dy * g
    xr = x * r
    dx = r * (dyg - xr * jnp.mean(dyg * xr, axis=-1, keepdims=True))
    return dx, dy * xr


def l2_fwd(x):
    r = lax.rsqrt(jnp.sum(x * x, axis=-1, keepdims=True) + EPS)
    return x * r, r


def l2_bwd(x, r, dy):
    xr = x * r
    return r * (dy - xr * jnp.sum(dy * xr, axis=-1, keepdims=True))


def _tri_masks(n):
    row = lax.broadcasted_iota(jnp.int32, (n, n), 0)
    col = lax.broadcasted_iota(jnp.int32, (n, n), 1)
    return row >= col, row > col


def chunk_cumsum(g4):
    incl, _ = _tri_masks(g4.shape[0])
    return dot_nn(incl.astype(F32), g4)


STACK = N_HEADS * CHUNK
DN_FWD_CHUNKS = 8
DN_CHUNKS = 4


def _head_rows(h):
    return slice(h * CHUNK, (h + 1) * CHUNK)


def _stack_heads(x):
    return jnp.concatenate([x[:, h * HEAD_DIM:(h + 1) * HEAD_DIM] for h in range(N_HEADS)], axis=0)


def _stack_lanes(x4):
    return jnp.concatenate([x4[:, h:h + 1] for h in range(N_HEADS)], axis=0)


def _per_head(fn):
    return jnp.concatenate([fn(h) for h in range(N_HEADS)], axis=0)


def _unit_lower_inverses(l_strict, order):
    c = l_strict[0].shape[0]
    row = lax.broadcasted_iota(jnp.int32, (c, c), 0)
    col = lax.broadcasted_iota(jnp.int32, (c, c), 1)
    eye = (row == col).astype(F32)
    p = [-l for l in l_strict]
    a = [eye + n for n in p]
    for _ in range(int(math.log2(order)) - 1):
        p = [dot_nn(x, x, FAST) for x in p]
        a = [x + dot_nn(x, y, FAST) for x, y in zip(a, p)]
    return a


def dn_chunks_local(chunks, inverses=None):
    row = lax.broadcasted_iota(jnp.int32, (STACK, STACK), 0)
    col = lax.broadcasted_iota(jnp.int32, (STACK, STACK), 1)
    same = (row // CHUNK) == (col // CHUNK)
    incl = jnp.logical_and(same, row >= col)
    strict = jnp.logical_and(same, row > col)
    locs = []
    for q, k, v, beta, gc4 in chunks:
        gc_col = _stack_lanes(gc4)
        gc_row = jnp.sum(jnp.where(row == col, gc_col, 0.0), axis=0, keepdims=True)
        decay = jnp.where(incl, jnp.exp(jnp.minimum(gc_col - gc_row, 0.0)), 0.0)
        gamma = jnp.exp(gc_col)
        gc_last = jnp.concatenate([jnp.broadcast_to(gc4[CHUNK - 1:CHUNK, h:h + 1], (CHUNK, 1)) for h in range(N_HEADS)], axis=0)
        tau = jnp.exp(gc_last - gc_col)
        kb = k * beta
        locs.append(dict(decay=decay, gamma=gamma, tau=tau, cd=jnp.exp(gc_last), kb=kb, qd=q * gamma, kt=k * tau,
                         incl=incl, strict=strict))
    for loc, (q, k, v, beta, gc4) in zip(locs, chunks):
        loc["l_mat"] = jnp.where(strict, dot_nt(loc["kb"], k, FAST) * loc["decay"], 0.0)
    if inverses is None:
        inverses = _unit_lower_inverses([loc["l_mat"] for loc in locs], CHUNK)
    for loc, a_inv in zip(locs, inverses):
        loc["a_inv"] = a_inv
    for loc, (q, k, v, beta, gc4) in zip(locs, chunks):
        sol = dot_nn(loc["a_inv"], jnp.concatenate([v * beta, loc["kb"] * loc["gamma"]], axis=1), FAST)
        loc.update(sol=sol, value=sol[:, :HEAD_DIM], kcd=sol[:, HEAD_DIM:])
        loc["attn"] = jnp.where(incl, dot_nt(q, k, FAST) * loc["decay"], 0.0)
    return locs


def dn_chunk_state(loc, s):
    kcd, qd, kt, cd = loc["kcd"], loc["qd"], loc["kt"], loc["cd"]
    v_new = loc["value"] - _per_head(lambda h: dot_nn(kcd[_head_rows(h)], s[h], FAST))
    o = _per_head(lambda h: dot_nn(qd[_head_rows(h)], s[h], FAST)) + dot_nn(loc["attn"], v_new, FAST)
    s_new = [s[h] * cd[h * CHUNK:h * CHUNK + 1, :] + dot_tn(kt[_head_rows(h)], v_new[_head_rows(h)], FAST)
             for h in range(N_HEADS)]
    loc["v_new"] = v_new
    return o, s_new


def dn_chunks_bwd(items, ds_last):
    hr = _head_rows
    n = len(items)
    pre = []
    for q, k, v, beta, loc, s, do in items:
        pre.append(dict(
            dv_part=dot_tn(loc["attn"], do, FAST),
            dattn=jnp.where(loc["incl"], dot_nt(do, loc["v_new"], FAST), 0.0),
            dqd=_per_head(lambda h: dot_nt(do[hr(h)], s[h], FAST)),
            ds_part=[dot_tn(loc["qd"][hr(h)], do[hr(h)], FAST) for h in range(N_HEADS)]))
    ds_new_of, dv_new_of = [None] * n, [None] * n
    ds = ds_last
    for c in reversed(range(n)):
        loc = items[c][4]
        ds_new_of[c] = ds
        dv_new = pre[c]["dv_part"] + _per_head(lambda h: dot_nn(loc["kt"][hr(h)], ds[h], FAST))
        dv_new_of[c] = dv_new
        ds = [pre[c]["ds_part"][h] + ds[h] * loc["cd"][h * CHUNK:h * CHUNK + 1, :]
              - dot_tn(loc["kcd"][hr(h)], dv_new[hr(h)], FAST) for h in range(N_HEADS)]
    is_last = (lax.broadcasted_iota(jnp.int32, (STACK, 1), 0) % CHUNK) == CHUNK - 1
    out = []
    for c, (q, k, v, beta, loc, s, do) in enumerate(items):
        decay, gamma, tau, cd, kb = loc["decay"], loc["gamma"], loc["tau"], loc["cd"], loc["kb"]
        dv_new, ds_new, dattn, dqd = dv_new_of[c], ds_new_of[c], pre[c]["dattn"], pre[c]["dqd"]
        dkt = _per_head(lambda h: dot_nt(loc["v_new"][hr(h)], ds_new[h], FAST))
        dkcd = -_per_head(lambda h: dot_nt(dv_new[hr(h)], s[h], FAST))
        drhs = dot_tn(loc["a_inv"], jnp.concatenate([dv_new, dkcd], axis=1), FAST)
        dvb, dkbg = drhs[:, :HEAD_DIM], drhs[:, HEAD_DIM:]
        dl = jnp.where(loc["strict"], -dot_nt(drhs, loc["sol"], FAST), 0.0)
        dkk = dl * decay
        dqk = dattn * decay
        e = dl * loc["l_mat"] + dattn * loc["attn"]
        dgc = jnp.sum(e, axis=1, keepdims=True) - jnp.sum(e, axis=0, keepdims=True).T
        dkb = dot_nn(dkk, k, FAST) + dkbg * gamma
        dk = dot_tn(dkk, kb, FAST) + dot_tn(dqk, q, FAST) + dkt * tau
        dq = dot_nn(dqk, k, FAST) + dqd * gamma
        dgamma = jnp.sum(dkbg * kb, axis=1, keepdims=True) + jnp.sum(dqd * q, axis=1, keepdims=True)
        dtau_tau = jnp.sum(dkt * k, axis=1, keepdims=True) * tau
        dgc = dgc + dgamma * gamma - dtau_tau

        def last_term(h):
            dcd = jnp.sum(jnp.sum(ds_new[h] * s[h], axis=1, keepdims=True), axis=0, keepdims=True)
            total = jnp.sum(dtau_tau[hr(h)], axis=0, keepdims=True) + dcd * cd[h * CHUNK:h * CHUNK + 1, :]
            return jnp.broadcast_to(total, (CHUNK, 1))

        dgc = dgc + jnp.where(is_last, _per_head(last_term), 0.0)
        dk = dk + dkb * beta
        dbeta = jnp.sum(dkb * k, axis=1, keepdims=True) + jnp.sum(dvb * v, axis=1, keepdims=True)
        out.append((dq, dk, dvb * beta, dbeta, dgc))
    return out, ds


def _token_tile(t):
    return _pick(t, 256)


STRIP = 32


def _for_strips(n_rows, rows, fn, start=0):
    def step(r, carry):
        fn(pl.multiple_of(r * rows, rows))
        return carry

    lax.fori_loop(start, n_rows // rows, step, 0)


def _fold_rows(x):
    out = x[0:SUBLANES, :]
    for i in range(1, x.shape[0] // SUBLANES):
        out = out + x[i * SUBLANES:(i + 1) * SUBLANES, :]
    return out


def _matmul(a, b, mode, name, tiles, add=None, out_dtype=F32):
    if mode == "nn":
        (m, k), n = a.shape, b.shape[1]
    elif mode == "nt":
        (m, k), n = a.shape, b.shape[0]
    else:
        (k, m), n = a.shape, b.shape[1]
    tm, tn, tk = min(tiles[0], m), min(tiles[1], n), min(tiles[2], k)
    assert m % tm == 0 and n % tn == 0 and k % tk == 0, (name, m, n, k, tiles)
    nk = k // tk
    dims = {"nn": (((1,), (0,)), ((), ())), "nt": (((1,), (1,)), ((), ())), "tn": (((0,), (0,)), ((), ()))}[mode]

    def finish(res, add_ref, o_ref):
        if add_ref is not None:
            res = res + add_ref[...]
        o_ref[...] = res.astype(o_ref.dtype)

    def body(*refs):
        a_ref, b_ref = refs[0], refs[1]
        add_ref = refs[2] if add is not None else None
        o_ref = refs[3] if add is not None else refs[2]
        part = lax.dot_general(a_ref[...], b_ref[...], dims, preferred_element_type=F32)
        if nk == 1:
            finish(part, add_ref, o_ref)
            return
        acc_ref = refs[-1]
        kk = pl.program_id(2)

        @pl.when(kk == 0)
        def _():
            acc_ref[...] = part

        @pl.when(kk > 0)
        def _():
            acc_ref[...] += part

        @pl.when(kk == nk - 1)
        def _():
            finish(acc_ref[...], add_ref, o_ref)

    a_spec = pl.BlockSpec((tk, tm), lambda j, i, kk: (kk, i)) if mode == "tn" else pl.BlockSpec((tm, tk), lambda j, i, kk: (i, kk))
    b_spec = pl.BlockSpec((tn, tk), lambda j, i, kk: (j, kk)) if mode == "nt" else pl.BlockSpec((tk, tn), lambda j, i, kk: (kk, j))
    o_spec = pl.BlockSpec((tm, tn), lambda j, i, kk: (i, j))
    in_specs = [a_spec, b_spec] + ([o_spec] if add is not None else [])
    args = (a, b) + ((add,) if add is not None else ())
    return _pcall(
        body, grid=(n // tn, m // tm, nk), in_specs=in_specs, out_specs=o_spec,
        out_shape=jax.ShapeDtypeStruct((m, n), out_dtype),
        scratch_shapes=[pltpu.VMEM((tm, tn), F32)] if nk > 1 else [],
        compiler_params=_params("parallel", "parallel", "arbitrary"), name=name)(*args)


def _matmul_rows(a, b, mode, name, tm, extra, outs, fn):
    m, k = a.shape
    n = b.shape[1] if mode == "nn" else b.shape[0]
    tm = min(tm, m)
    dims = (((1,), (0,)), ((), ())) if mode == "nn" else (((1,), (1,)), ((), ()))

    def spec(shape, kind):
        if kind == "rows":
            return pl.BlockSpec((tm, shape[1]), lambda i: (i, 0))
        return pl.BlockSpec(shape, lambda i: (0,) * len(shape))

    def body(a_ref, b_ref, *refs):
        rows = lax.dot_general(a_ref[...], b_ref[...], dims, preferred_element_type=F32)
        fn(rows, pl.program_id(0) == 0, *refs)

    return _pcall(
        body, grid=(m // tm,),
        in_specs=[pl.BlockSpec((tm, k), lambda i: (i, 0)), pl.BlockSpec(b.shape, lambda i: (0, 0))]
        + [spec(x.shape, kind) for x, kind in extra],
        out_specs=[spec(shape, kind) for shape, _, kind in outs],
        out_shape=[jax.ShapeDtypeStruct(shape, dtype) for shape, dtype, _ in outs],
        compiler_params=_params("arbitrary", fusible=[False, True] + [False] * len(extra)),
        name=name)(a, b, *[x for x, _ in extra])


def _rmsnorm_matmul(x, g, b_t, name, tm):
    t, d = x.shape
    n = b_t.shape[0]
    tm = min(tm, t)

    def body(x_ref, g_ref, b_ref, o_ref, h_ref, r_ref):
        y, r = rms_fwd(x_ref[...], g_ref[...])
        h = y.astype(BF16)
        h_ref[...] = h
        r_ref[...] = r
        o_ref[...] = lax.dot_general(h, b_ref[...], (((1,), (1,)), ((), ())), preferred_element_type=F32)

    rows = lambda w: pl.BlockSpec((tm, w), lambda i: (i, 0))
    return _pcall(
        body, grid=(t // tm,),
        in_specs=[rows(d), pl.BlockSpec((1, d), lambda i: (0, 0)), pl.BlockSpec((n, d), lambda i: (0, 0))],
        out_specs=[rows(n), rows(d), rows(1)],
        out_shape=[jax.ShapeDtypeStruct((t, n), F32), jax.ShapeDtypeStruct((t, d), BF16), jax.ShapeDtypeStruct((t, 1), F32)],
        compiler_params=_params("parallel", fusible=[False, False, True]), name=name)(x, g, b_t)


def _rmsnorm_bwd_rows(t, d):
    def fn(dh, first, x_ref, r_ref, g_ref, dres_ref, dx_ref, dxb_ref, dg_ref):
        dx, dg_rows = rms_bwd(x_ref[...], r_ref[...], g_ref[...], dh)
        dx = dx + dres_ref[...]
        dx_ref[...] = dx
        dxb_ref[...] = dx.astype(BF16)

        @pl.when(first)
        def _():
            dg_ref[...] = jnp.zeros_like(dg_ref)

        dg_ref[...] += jnp.sum(dg_rows, axis=0, keepdims=True)

    return fn, [((t, d), F32, "rows"), ((t, d), BF16, "rows"), ((1, d), F32, "whole")]


def _final_loss_rows(t, d):
    def fn(rows, first, res_ref, t_ref, g_ref, loss_ref, dx_ref, dxb_ref, dg_ref):
        @pl.when(first)
        def _():
            loss_ref[...] = jnp.zeros_like(loss_ref)
            dg_ref[...] = jnp.zeros_like(dg_ref)

        x = rows + res_ref[...]
        y, r = rms_fwd(x, g_ref[...])
        err = y - t_ref[...]
        loss_ref[...] += 0.5 * jnp.sum(jnp.mean(err * err, axis=-1, keepdims=True), axis=0, keepdims=True)
        dx, dg_rows = rms_bwd(x, r, g_ref[...], err * (1.0 / d))
        dx_ref[...] = dx
        dxb_ref[...] = dx.astype(BF16)
        dg_ref[...] += jnp.sum(dg_rows, axis=0, keepdims=True)

    return fn, [((1, LANES), F32, "whole"), ((t, d), F32, "rows"), ((t, d), BF16, "rows"), ((1, d), F32, "whole")]


def _prev_halo_spec(tm, width, col_block):
    return pl.BlockSpec((HALO, width), lambda i: (jnp.maximum(i * (tm // HALO) - 1, 0), col_block))


def _history(tile_ref, halo_ref, first, row0, cols):
    if isinstance(row0, int) and row0 == 0:
        return jnp.concatenate([jnp.where(first, 0.0, halo_ref[:, cols]), tile_ref[0:STRIP, cols]], axis=0)
    return tile_ref[pl.ds(pl.multiple_of(row0 - HALO, HALO), STRIP + HALO), cols]


def _first_then_strips(n_rows, fn):
    fn(0)
    _for_strips(n_rows, STRIP, fn, start=1)


def _delays(ext, taps):
    return [ext[HALO:, :]] + [pltpu.roll(ext, j, 0)[HALO:, :] for j in range(1, taps)]


def _causal_conv(delayed, w):
    taps = len(delayed)
    out = delayed[0] * w[taps - 1:taps, :]
    for j in range(1, taps):
        out = out + delayed[j] * w[taps - 1 - j:taps - j, :]
    return out


def _advanced_conv(buf_ref, row0, cols, w):
    return _advanced(buf_ref[pl.ds(row0, STRIP + HALO), cols], w)


def _advanced(ext, w):
    taps = w.shape[0]
    out = ext[:STRIP, :] * w[taps - 1:taps, :]
    for j in range(1, taps):
        out = out + pltpu.roll(ext, STRIP + HALO - j, 0)[:STRIP, :] * w[taps - 1 - j:taps - j, :]
    return out


def _dn_prep(p, conv_w, a_log4, dt_bias4):
    t = p.shape[0]
    tm = _token_tile(t)
    w3 = 3 * DN_WIDTH

    def body(x_ref, halo_ref, pbd_ref, w_ref, alog_ref, dtb_ref, q_ref, k_ref, v_ref, beta_ref, g_ref):
        first = pl.program_id(0) == 0

        def strip(row0):
            rows = pl.ds(row0, STRIP)
            for h in range(N_HEADS):
                sl = slice(h * HEAD_DIM, (h + 1) * HEAD_DIM)
                for part, out_ref in ((0, q_ref), (1, k_ref), (2, v_ref)):
                    cols = slice(part * DN_WIDTH + h * HEAD_DIM, part * DN_WIDTH + (h + 1) * HEAD_DIM)
                    y = silu(_causal_conv(_delays(_history(x_ref, halo_ref, first, row0, cols), CONV_K), w_ref[:, cols]))
                    if part == 0:
                        y = l2_fwd(y)[0] * (HEAD_DIM ** -0.5)
                    elif part == 1:
                        y = l2_fwd(y)[0]
                    out_ref[rows, sl] = y
            head = lax.broadcasted_iota(jnp.int32, (STRIP, LANES), 1) < N_HEADS
            pbd = pbd_ref[rows, :]
            beta_ref[rows, :] = jnp.where(head, sigmoid(pbd), 0.0)
            a_raw = pltpu.roll(pbd, LANES - N_HEADS, 1)
            g_ref[rows, :] = jnp.where(head, -jnp.exp(alog_ref[...]) * softplus(a_raw + dtb_ref[...]), 0.0)

        _first_then_strips(tm, strip)

    tok = lambda w, cb: pl.BlockSpec((tm, w), lambda i: (i, cb))
    full = lambda a: pl.BlockSpec(a.shape, lambda i: (0, 0))
    return _pcall(
        body, grid=(t // tm,),
        in_specs=[tok(w3, 0), _prev_halo_spec(tm, w3, 0), tok(LANES, PROJ_MAIN // LANES),
                  full(conv_w), full(a_log4), full(dt_bias4)],
        out_specs=[tok(DN_WIDTH, 0)] * 3 + [tok(LANES, 0)] * 2,
        out_shape=[jax.ShapeDtypeStruct((t, DN_WIDTH), F32)] * 3 + [jax.ShapeDtypeStruct((t, LANES), F32)] * 2,
        compiler_params=_params("parallel"), name="dn_prep")(p, p, p, conv_w, a_log4, dt_bias4)


def _dn_prep_bwd(p, conv_w, a_log4, dt_bias4, dq, dk, dv, dbeta4, dg4, dp_buf):
    t = p.shape[0]
    tm = _token_tile(t)
    w3 = 3 * DN_WIDTH

    def body(x_ref, halo_ref, pbd_ref, w_ref, alog_ref, dtb_ref, dq_ref, dk_ref, dv_ref, dbeta_ref, dg_ref, _,
             dc_ref, dw_ref, dpbd_ref, dalog_ref, ddtb_ref, dw_acc, lane_acc):
        first = pl.program_id(0) == 0
        dw_acc[...] = jnp.zeros_like(dw_acc)
        lane_acc[...] = jnp.zeros_like(lane_acc)

        def strip(row0):
            rows = pl.ds(row0, STRIP)
            for h in range(N_HEADS):
                sl = slice(h * HEAD_DIM, (h + 1) * HEAD_DIM)
                for part, dy_ref in ((0, dq_ref), (1, dk_ref), (2, dv_ref)):
                    cols = slice(part * DN_WIDTH + h * HEAD_DIM, part * DN_WIDTH + (h + 1) * HEAD_DIM)
                    delayed = _delays(_history(x_ref, halo_ref, first, row0, cols), CONV_K)
                    c = _causal_conv(delayed, w_ref[:, cols])
                    dy = dy_ref[rows, sl]
                    if part < 2:
                        y = silu(c)
                        _, r = l2_fwd(y)
                        dy = l2_bwd(y, r, dy * (HEAD_DIM ** -0.5) if part == 0 else dy)
                    dc = dy * silu_grad(c)
                    dc_ref[rows, cols] = dc
                    for j in range(CONV_K):
                        k = CONV_K - 1 - j
                        dw_acc[k * SUBLANES:(k + 1) * SUBLANES, cols] += _fold_rows(dc * delayed[j])
            head = lax.broadcasted_iota(jnp.int32, (STRIP, LANES), 1) < N_HEADS
            pbd = pbd_ref[rows, :]
            beta = sigmoid(pbd)
            dpb = jnp.where(head, dbeta_ref[rows, :] * beta * (1.0 - beta), 0.0)
            z = pltpu.roll(pbd, LANES - N_HEADS, 1) + dtb_ref[...]
            neg_rate = -jnp.exp(alog_ref[...])
            dg = dg_ref[rows, :]
            dpa = jnp.where(head, dg * neg_rate * sigmoid(z), 0.0)
            dpbd_ref[rows, :] = (dpb + pltpu.roll(dpa, N_HEADS, 1)).astype(BF16)
            g = jnp.where(head, neg_rate * softplus(z), 0.0)
            lane_acc[0:SUBLANES, :] += _fold_rows(dg * g)
            lane_acc[SUBLANES:, :] += _fold_rows(dpa)

        _first_then_strips(tm, strip)

        @pl.when(first)
        def _():
            dw_ref[...] = jnp.zeros_like(dw_ref)
            dalog_ref[...] = jnp.zeros_like(dalog_ref)
            ddtb_ref[...] = jnp.zeros_like(ddtb_ref)

        for k in range(CONV_K):
            dw_ref[k:k + 1, :] += jnp.sum(dw_acc[k * SUBLANES:(k + 1) * SUBLANES, :], axis=0, keepdims=True)
        dalog_ref[...] += jnp.sum(lane_acc[0:SUBLANES, :], axis=0, keepdims=True)
        ddtb_ref[...] += jnp.sum(lane_acc[SUBLANES:, :], axis=0, keepdims=True)

    tok = lambda w, cb: pl.BlockSpec((tm, w), lambda i: (i, cb))
    full = lambda shape: pl.BlockSpec(shape, lambda i: (0, 0))
    return _pcall(
        body, grid=(t // tm,),
        in_specs=[tok(w3, 0), _prev_halo_spec(tm, w3, 0), tok(LANES, PROJ_MAIN // LANES),
                  full(conv_w.shape), full(a_log4.shape), full(dt_bias4.shape)] + [tok(DN_WIDTH, 0)] * 3 + [tok(LANES, 0)] * 2
        + [pl.BlockSpec(memory_space=pl.ANY)],
        out_specs=[tok(w3, 0), full((CONV_K, w3)), tok(LANES, PROJ_MAIN // LANES), full((1, LANES)), full((1, LANES))],
        out_shape=[jax.ShapeDtypeStruct((t, w3), F32), jax.ShapeDtypeStruct((CONV_K, w3), F32),
                   jax.ShapeDtypeStruct(dp_buf.shape, dp_buf.dtype),
                   jax.ShapeDtypeStruct((1, LANES), F32), jax.ShapeDtypeStruct((1, LANES), F32)],
        input_output_aliases={11: 2},
        scratch_shapes=[pltpu.VMEM((CONV_K * SUBLANES, w3), F32), pltpu.VMEM((2 * SUBLANES, LANES), F32)],
        compiler_params=_params("arbitrary"), name="dn_prep_bwd")(p, p, p, conv_w, a_log4, dt_bias4, dq, dk, dv, dbeta4, dg4, dp_buf)


def _conv_bwd_input(dc, w, name, out_cols=None, col_block=0, into=None):
    t, c = dc.shape
    taps = w.shape[0]
    tm = _token_tile(t)
    ct = _pick(c, 1536)
    n_tok = t // tm
    out_cols = c if out_cols is None else out_cols

    def body(dc_ref, next_ref, w_ref, *rest):
        dx_ref = rest[-1]
        last = pl.program_id(0) == n_tok - 1

        def strip(row0):
            for c0 in range(0, ct, LANES):
                cols = slice(c0, c0 + LANES)
                dx_ref[pl.ds(row0, STRIP), cols] = _advanced_conv(dc_ref, row0, cols, w_ref[:, cols]).astype(BF16)

        _for_strips(tm - STRIP, STRIP, strip)
        for c0 in range(0, ct, LANES):
            cols = slice(c0, c0 + LANES)
            ext = jnp.concatenate([dc_ref[tm - STRIP:tm, cols], jnp.where(last, 0.0, next_ref[:, cols])], axis=0)
            dx_ref[tm - STRIP:tm, cols] = _advanced(ext, w_ref[:, cols]).astype(BF16)

    in_specs = [pl.BlockSpec((tm, ct), lambda i, j: (i, j)),
                pl.BlockSpec((HALO, ct), lambda i, j: (jnp.minimum((i + 1) * (tm // HALO), t // HALO - 1), j)),
                pl.BlockSpec((taps, ct), lambda i, j: (0, j))]
    args = (dc, dc, w)
    aliases = {}
    if into is not None:
        in_specs.append(pl.BlockSpec(memory_space=pl.ANY))
        args += (into,)
        aliases = {3: 0}
    return _pcall(
        body, grid=(n_tok, c // ct), in_specs=in_specs,
        out_specs=pl.BlockSpec((tm, ct), lambda i, j: (i, j + col_block)),
        out_shape=jax.ShapeDtypeStruct((t, out_cols), BF16), input_output_aliases=aliases,
        compiler_params=_params("parallel", "parallel"), name=name)(*args)


def _dn_forward(q, k, v, beta4, g4, p, norm_g):
    t = q.shape[0]
    n = t // CHUNK
    nc = DN_FWD_CHUNKS
    rows_per_step = nc * CHUNK

    def body(q_ref, k_ref, v_ref, b_ref, g_ref, gate_ref, ng_ref, mix_ref, s_all_ref, ainv_ref, s_ref):
        @pl.when(pl.program_id(0) == 0)
        def _():
            s_ref[...] = jnp.zeros_like(s_ref)

        chunks = []
        for c in range(nc):
            rows = slice(c * CHUNK, (c + 1) * CHUNK)
            chunks.append((_stack_heads(q_ref[rows, :]), _stack_heads(k_ref[rows, :]), _stack_heads(v_ref[rows, :]),
                           _stack_lanes(b_ref[rows, :]), chunk_cumsum(g_ref[rows, :])))
        locs = dn_chunks_local(chunks)
        s = [s_ref[h] for h in range(N_HEADS)]
        for c in range(nc):
            rows = slice(c * CHUNK, (c + 1) * CHUNK)
            ainv_ref[c] = locs[c]["a_inv"].astype(BF16)
            for h in range(N_HEADS):
                s_all_ref[c, h] = s[h]
            o, s = dn_chunk_state(locs[c], s)
            o_n, _ = rms_fwd(o, ng_ref[...])
            for h in range(N_HEADS):
                sl = slice(h * HEAD_DIM, (h + 1) * HEAD_DIM)
                mix_ref[rows, sl] = (o_n[_head_rows(h)] * silu(gate_ref[rows, sl])).astype(BF16)
        for h in range(N_HEADS):
            s_ref[h] = s[h]

    ch = lambda w, cb: pl.BlockSpec((rows_per_step, w), lambda i: (i, cb))
    per_chunk = lambda *shape: pl.BlockSpec((nc,) + shape, lambda i: (i,) + (0,) * len(shape))
    return _pcall(
        body, grid=(n // nc,),
        in_specs=[ch(DN_WIDTH, 0)] * 3 + [ch(LANES, 0)] * 2 + [ch(DN_WIDTH, 3), pl.BlockSpec((1, HEAD_DIM), lambda i: (0, 0))],
        out_specs=[ch(DN_WIDTH, 0), per_chunk(N_HEADS, HEAD_DIM, HEAD_DIM), per_chunk(STACK, STACK)],
        out_shape=[jax.ShapeDtypeStruct((t, DN_WIDTH + SG_WIDTH), BF16), jax.ShapeDtypeStruct((n, N_HEADS, HEAD_DIM, HEAD_DIM), F32),
                   jax.ShapeDtypeStruct((n, STACK, STACK), BF16)],
        scratch_shapes=[pltpu.VMEM((N_HEADS, HEAD_DIM, HEAD_DIM), F32)],
        compiler_params=_params("arbitrary"), name="dn_forward")(q, k, v, beta4, g4, p, norm_g)


def _dn_backward(q, k, v, beta4, g4, p, norm_g, saved, dmix, dp_buf):
    t = q.shape[0]
    n = t // CHUNK
    steps = n // DN_CHUNKS
    rows_per_step = DN_CHUNKS * CHUNK

    def body(q_ref, k_ref, v_ref, b_ref, g_ref, gate_ref, ng_ref, s_in_ref, ainv_ref, dmix_ref, _,
             dq_ref, dk_ref, dv_ref, db_ref, dg_ref, dgate_ref, dng_ref, ds_ref):
        @pl.when(pl.program_id(0) == 0)
        def _():
            ds_ref[...] = jnp.zeros_like(ds_ref)
            dng_ref[...] = jnp.zeros_like(dng_ref)

        chunks = []
        for c in range(DN_CHUNKS):
            rows = slice(c * CHUNK, (c + 1) * CHUNK)
            chunks.append((_stack_heads(q_ref[rows, :]), _stack_heads(k_ref[rows, :]), _stack_heads(v_ref[rows, :]),
                           _stack_lanes(b_ref[rows, :]), chunk_cumsum(g_ref[rows, :])))
        items = []
        for c, loc in enumerate(dn_chunks_local(chunks, [ainv_ref[c] for c in range(DN_CHUNKS)])):
            rows = slice(c * CHUNK, (c + 1) * CHUNK)
            s = [s_in_ref[c, h] for h in range(N_HEADS)]
            o, _ = dn_chunk_state(loc, s)
            o_n, r = rms_fwd(o, ng_ref[...])
            gate = _stack_heads(gate_ref[rows, :])
            dmx = _stack_heads(dmix_ref[rows, :])
            dgate = dmx * o_n * silu_grad(gate)
            do, dng_rows = rms_bwd(o, r, ng_ref[...], dmx * silu(gate))
            dng_ref[...] += jnp.sum(dng_rows, axis=0, keepdims=True)
            for h in range(N_HEADS):
                dgate_ref[rows, h * HEAD_DIM:(h + 1) * HEAD_DIM] = dgate[_head_rows(h)].astype(BF16)
            items.append((*chunks[c][:4], loc, s, do))
        grads, ds = dn_chunks_bwd(items, [ds_ref[h] for h in range(N_HEADS)])
        lane = lax.broadcasted_iota(jnp.int32, (CHUNK, LANES), 1)
        _, strict = _tri_masks(CHUNK)
        for c in range(DN_CHUNKS):
            rows = slice(c * CHUNK, (c + 1) * CHUNK)
            dq, dk, dv, dbeta, dgc = grads[c]
            db4 = jnp.zeros((CHUNK, LANES), F32)
            dgc4 = jnp.zeros((CHUNK, LANES), F32)
            for h in range(N_HEADS):
                sl = slice(h * HEAD_DIM, (h + 1) * HEAD_DIM)
                head_rows = _head_rows(h)
                dq_ref[rows, sl] = dq[head_rows]
                dk_ref[rows, sl] = dk[head_rows]
                dv_ref[rows, sl] = dv[head_rows]
                db4 = jnp.where(lane == h, dbeta[head_rows], db4)
                dgc4 = jnp.where(lane == h, dgc[head_rows], dgc4)
            db_ref[rows, :] = db4
            dg_ref[rows, :] = dot_nn(jnp.logical_not(strict).astype(F32), dgc4)
        for h in range(N_HEADS):
            ds_ref[h] = ds[h]

    rev = lambda w, cb: pl.BlockSpec((rows_per_step, w), lambda i: (steps - 1 - i, cb))
    per_chunk = lambda a: pl.BlockSpec((DN_CHUNKS,) + a.shape[1:], lambda i: (steps - 1 - i,) + (0,) * (a.ndim - 1))
    return _pcall(
        body, grid=(steps,),
        in_specs=[rev(DN_WIDTH, 0)] * 3 + [rev(LANES, 0)] * 2 + [rev(DN_WIDTH, 3), pl.BlockSpec((1, HEAD_DIM), lambda i: (0, 0))]
        + [per_chunk(a) for a in saved] + [rev(DN_WIDTH, 0), pl.BlockSpec(memory_space=pl.ANY)],
        out_specs=[rev(DN_WIDTH, 0)] * 3 + [rev(LANES, 0)] * 2 + [rev(DN_WIDTH, 3), pl.BlockSpec((1, HEAD_DIM), lambda i: (0, 0))],
        out_shape=[jax.ShapeDtypeStruct((t, DN_WIDTH), F32)] * 3 + [jax.ShapeDtypeStruct((t, LANES), F32)] * 2
        + [jax.ShapeDtypeStruct(dp_buf.shape, dp_buf.dtype), jax.ShapeDtypeStruct((1, HEAD_DIM), F32)],
        input_output_aliases={10: 5},
        scratch_shapes=[pltpu.VMEM((N_HEADS, HEAD_DIM, HEAD_DIM), F32)],
        compiler_params=_params("arbitrary"), name="dn_backward")(q, k, v, beta4, g4, p, norm_g, *saved, dmix, dp_buf)


SG_STEP_BLOCKS = 4


def _sg_mask():
    row = lax.broadcasted_iota(jnp.int32, (SG_BLOCK, SG_BLOCK), 0)
    col = lax.broadcasted_iota(jnp.int32, (SG_BLOCK, SG_BLOCK), 1)
    return (col // CHUNK) <= (row // CHUNK)


def _sg_forward(p, norm_g, w_s, b_t, mix_buf):
    t = p.shape[0]
    step_rows = SG_STEP_BLOCKS * SG_BLOCK

    def body(u_ref, v_ref, ng_ref, w_ref, b_ref, _, o_ref):
        mask = _sg_mask()
        pairs = [(slice(b * SG_BLOCK, (b + 1) * SG_BLOCK), g, slice(g * SG_DIM, (g + 1) * SG_DIM))
                 for b in range(SG_STEP_BLOCKS) for g in range(SG_GROUPS)]
        w_m = [jnp.where(mask, w_ref[g], 0.0) for g in range(SG_GROUPS)]
        vn = [rms_fwd(gelu(v_ref[rows, sl]), ng_ref[:, sl])[0] for rows, g, sl in pairs]
        s = [dot_nn(w_m[g], vn[i], FAST) + b_ref[:, g:g + 1] for i, (rows, g, sl) in enumerate(pairs)]
        for i, (rows, g, sl) in enumerate(pairs):
            o_ref[rows, sl] = (gelu(u_ref[rows, sl]) * s[i]).astype(BF16)

    blk = lambda cb: pl.BlockSpec((step_rows, SG_WIDTH), lambda i: (i, cb))
    return _pcall(
        body, grid=(t // step_rows,),
        in_specs=[blk(4), blk(5), pl.BlockSpec((1, SG_WIDTH), lambda i: (0, 0)),
                  pl.BlockSpec((SG_GROUPS, SG_BLOCK, SG_BLOCK), lambda i: (0, 0, 0)), pl.BlockSpec((SG_BLOCK, SG_GROUPS), lambda i: (0, 0)),
                  pl.BlockSpec(memory_space=pl.ANY)],
        out_specs=blk(1), out_shape=jax.ShapeDtypeStruct(mix_buf.shape, mix_buf.dtype), input_output_aliases={5: 0},
        compiler_params=_params("parallel"), name="sg_forward")(p, p, norm_g, w_s, b_t, mix_buf)


def _sg_backward(p, norm_g, w_s, b_t, dmix):
    t = p.shape[0]

    def body(u_ref, v_ref, ng_ref, w_ref, b_ref, do_ref, duv_ref, dng_ref, dw_ref, db_ref):
        @pl.when(pl.program_id(0) == 0)
        def _():
            dng_ref[...] = jnp.zeros_like(dng_ref)
            dw_ref[...] = jnp.zeros_like(dw_ref)
            db_ref[...] = jnp.zeros_like(db_ref)

        mask = _sg_mask()
        lane = lax.broadcasted_iota(jnp.int32, (SG_BLOCK, LANES), 1)
        pairs = [(slice(b * SG_BLOCK, (b + 1) * SG_BLOCK), g, slice(g * SG_DIM, (g + 1) * SG_DIM))
                 for b in range(SG_STEP_BLOCKS) for g in range(SG_GROUPS)]
        w_m = [jnp.where(mask, w_ref[g], 0.0) for g in range(SG_GROUPS)]
        vg = [gelu(v_ref[rows, sl]) for rows, g, sl in pairs]
        normed = [rms_fwd(vg[i], ng_ref[:, sl]) for i, (rows, g, sl) in enumerate(pairs)]
        s = [dot_nn(w_m[g], normed[i][0], FAST) + b_ref[:, g:g + 1] for i, (rows, g, sl) in enumerate(pairs)]
        ds = []
        db = jnp.zeros((SG_BLOCK, LANES), F32)
        for i, (rows, g, sl) in enumerate(pairs):
            u_raw, do = u_ref[rows, sl], do_ref[rows, sl]
            duv_ref[rows, sl] = (do * s[i] * gelu_grad(u_raw)).astype(BF16)
            ds.append(do * gelu(u_raw))
            db = db + jnp.where(lane == g, jnp.sum(ds[i], axis=1, keepdims=True), 0.0)
        dw = [jnp.where(mask, dot_nt(ds[i], normed[i][0], FAST), 0.0) for i in range(len(pairs))]
        dvn = [dot_tn(w_m[g], ds[i], FAST) for i, (rows, g, sl) in enumerate(pairs)]
        for i, (rows, g, sl) in enumerate(pairs):
            dw_ref[g] += dw[i]
            dvg, dng_rows = rms_bwd(vg[i], normed[i][1], ng_ref[:, sl], dvn[i])
            dng_ref[:, sl] += jnp.sum(dng_rows, axis=0, keepdims=True)
            duv_ref[rows, SG_WIDTH + g * SG_DIM:SG_WIDTH + (g + 1) * SG_DIM] = (dvg * gelu_grad(v_ref[rows, sl])).astype(BF16)
        db_ref[...] += db

    step_rows = SG_STEP_BLOCKS * SG_BLOCK
    blk = lambda cb: pl.BlockSpec((step_rows, SG_WIDTH), lambda i: (i, cb))
    const2 = lambda shape: pl.BlockSpec(shape, lambda i: (0, 0))
    w_spec = pl.BlockSpec((SG_GROUPS, SG_BLOCK, SG_BLOCK), lambda i: (0, 0, 0))
    return _pcall(
        body, grid=(t // step_rows,),
        in_specs=[blk(4), blk(5), const2((1, SG_WIDTH)), w_spec, const2((SG_BLOCK, SG_GROUPS)), blk(1)],
        out_specs=[pl.BlockSpec((step_rows, 2 * SG_WIDTH), lambda i: (i, 2)), const2((1, SG_WIDTH)), w_spec,
                   const2((SG_BLOCK, LANES))],
        out_shape=[jax.ShapeDtypeStruct((t, PROJ_PAD), BF16), jax.ShapeDtypeStruct((1, SG_WIDTH), F32),
                   jax.ShapeDtypeStruct((SG_GROUPS, SG_BLOCK, SG_BLOCK), F32), jax.ShapeDtypeStruct((SG_BLOCK, LANES), F32)],
        compiler_params=_params("arbitrary"), name="sg_backward")(p, p, norm_g, w_s, b_t, dmix)


FFN_COLS = 256


def _norm_up_ffn(x, g, w_up_t, conv_w, conv_b):
    t, d = x.shape
    tm = min(t, 256)
    blocks = D_FF // FFN_COLS
    nt = (((1,), (1,)), ((), ()))

    def body(x_ref, g_ref, w_ref, cw_ref, cb_ref, up_ref, act_ref, h_ref, r_ref, tail_ref, prev_ref):
        @pl.when(pl.program_id(0) == 0)
        def _():
            tail_ref[...] = jnp.zeros_like(tail_ref)

        y, r = rms_fwd(x_ref[...], g_ref[...])
        h = y.astype(BF16)
        h_ref[...] = h
        r_ref[...] = r

        def project(blk):
            out = []
            for half in range(2):
                cols = slice(half * D_FF + blk * FFN_COLS, half * D_FF + (blk + 1) * FFN_COLS)
                u = lax.dot_general(h, w_ref[cols, :], nt, preferred_element_type=F32)
                up_ref[:, cols] = u
                prev_ref[:, cols] = tail_ref[:, cols]
                tail_ref[:, cols] = u[tm - HALO:, :]
                out.append(cols)
            return out

        def history(row0, cols):
            if row0 == 0:
                return jnp.concatenate([prev_ref[:, cols], up_ref[0:STRIP, cols]], axis=0)
            return up_ref[row0 - HALO:row0 + STRIP, cols]

        def activate(blk, g_cols, v_cols):
            for row0 in range(0, tm, STRIP):
                for c0 in range(0, FFN_COLS, LANES):
                    gc = slice(g_cols.start + c0, g_cols.start + c0 + LANES)
                    vc = slice(v_cols.start + c0, v_cols.start + c0 + LANES)
                    cg = _causal_conv(_delays(history(row0, gc), FFN_CONV), cw_ref[:, gc]) + cb_ref[:, gc]
                    cv = _causal_conv(_delays(history(row0, vc), FFN_CONV), cw_ref[:, vc]) + cb_ref[:, vc]
                    act_ref[row0:row0 + STRIP, blk * FFN_COLS + c0:blk * FFN_COLS + c0 + LANES] = (silu(cg) * cv).astype(BF16)

        pending = None
        for blk in range(blocks):
            cols = project(blk)
            if pending is not None:
                activate(*pending)
            pending = (blk, *cols)
        activate(*pending)

    rows = lambda w: pl.BlockSpec((tm, w), lambda i: (i, 0))
    whole = lambda a: pl.BlockSpec(a.shape, lambda i: (0, 0))
    return _pcall(
        body, grid=(t // tm,),
        in_specs=[rows(d), whole(g), whole(w_up_t), whole(conv_w), whole(conv_b)],
        out_specs=[rows(2 * D_FF), rows(D_FF), rows(d), rows(1)],
        out_shape=[jax.ShapeDtypeStruct((t, 2 * D_FF), F32), jax.ShapeDtypeStruct((t, D_FF), BF16),
                   jax.ShapeDtypeStruct((t, d), BF16), jax.ShapeDtypeStruct((t, 1), F32)],
        scratch_shapes=[pltpu.VMEM((HALO, 2 * D_FF), F32), pltpu.VMEM((HALO, 2 * D_FF), F32)],
        compiler_params=_params("arbitrary"), name="norm_up_ffn")(x, g, w_up_t, conv_w, conv_b)


def _ffn_bwd(up, conv_w, conv_b, dact):
    t = up.shape[0]
    tm = _pick(t, 256)
    n_tok = t // tm
    width = 2 * D_FF

    def dconv(delayed_g, delayed_v, da, wg, wv, bg, bv):
        cg = _causal_conv(delayed_g, wg) + bg
        cv = _causal_conv(delayed_v, wv) + bv
        s = sigmoid(cg)
        return da * cv * (s * (1.0 + cg * (1.0 - s))), da * (cg * s)

    def body(up_ref, prev_ref, next_ref, da_ref, dan_ref, w_ref, b_ref, dup_ref, dw_ref, db_ref, dc_ref, dw_acc, db_acc):
        first = pl.program_id(0) == 0
        last = pl.program_id(0) == n_tok - 1
        dw_acc[...] = jnp.zeros_like(dw_acc)
        db_acc[...] = jnp.zeros_like(db_acc)

        def strip(row0):
            rows = pl.ds(row0, STRIP)
            for c0 in range(0, D_FF, LANES):
                gc, vc = slice(c0, c0 + LANES), slice(D_FF + c0, D_FF + c0 + LANES)
                del_g = _delays(_history(up_ref, prev_ref, first, row0, gc), FFN_CONV)
                del_v = _delays(_history(up_ref, prev_ref, first, row0, vc), FFN_CONV)
                dcg, dcv = dconv(del_g, del_v, da_ref[rows, gc], w_ref[:, gc], w_ref[:, vc], b_ref[:, gc], b_ref[:, vc])
                dc_ref[rows, gc] = dcg
                dc_ref[rows, vc] = dcv
                db_acc[:, gc] += _fold_rows(dcg)
                db_acc[:, vc] += _fold_rows(dcv)
                for j in range(FFN_CONV):
                    k = FFN_CONV - 1 - j
                    dw_acc[k * SUBLANES:(k + 1) * SUBLANES, gc] += _fold_rows(dcg * del_g[j])
                    dw_acc[k * SUBLANES:(k + 1) * SUBLANES, vc] += _fold_rows(dcv * del_v[j])

        _first_then_strips(tm, strip)

        for c0 in range(0, D_FF, LANES):
            gc, vc = slice(c0, c0 + LANES), slice(D_FF + c0, D_FF + c0 + LANES)

            def delayed(cols):
                return _delays(jnp.concatenate([up_ref[tm - HALO:tm, cols], next_ref[:, cols]], axis=0), FFN_CONV)

            dcg, dcv = dconv(delayed(gc), delayed(vc), dan_ref[:, gc], w_ref[:, gc], w_ref[:, vc], b_ref[:, gc], b_ref[:, vc])
            dc_ref[tm:, gc] = jnp.where(last, 0.0, dcg)
            dc_ref[tm:, vc] = jnp.where(last, 0.0, dcv)

        def strip_dx(row0):
            for c0 in range(0, width, LANES):
                cols = slice(c0, c0 + LANES)
                dup_ref[pl.ds(row0, STRIP), cols] = _advanced_conv(dc_ref, row0, cols, w_ref[:, cols]).astype(BF16)

        _for_strips(tm, STRIP, strip_dx)

        @pl.when(first)
        def _():
            dw_ref[...] = jnp.zeros_like(dw_ref)
            db_ref[...] = jnp.zeros_like(db_ref)

        for k in range(FFN_CONV):
            dw_ref[k:k + 1, :] += jnp.sum(dw_acc[k * SUBLANES:(k + 1) * SUBLANES, :], axis=0, keepdims=True)
        db_ref[...] += jnp.sum(db_acc[...], axis=0, keepdims=True)

    next_rows = lambda i: jnp.minimum((i + 1) * (tm // HALO), t // HALO - 1)
    full = lambda rows: pl.BlockSpec((rows, width), lambda i: (0, 0))
    return _pcall(
        body, grid=(n_tok,),
        in_specs=[pl.BlockSpec((tm, width), lambda i: (i, 0)),
                  pl.BlockSpec((HALO, width), lambda i: (jnp.maximum(i * (tm // HALO) - 1, 0), 0)),
                  pl.BlockSpec((HALO, width), lambda i: (next_rows(i), 0)),
                  pl.BlockSpec((tm, D_FF), lambda i: (i, 0)), pl.BlockSpec((HALO, D_FF), lambda i: (next_rows(i), 0)),
                  full(FFN_CONV), full(1)],
        out_specs=[pl.BlockSpec((tm, width), lambda i: (i, 0)), full(FFN_CONV), full(1)],
        out_shape=[jax.ShapeDtypeStruct((t, width), BF16), jax.ShapeDtypeStruct((FFN_CONV, width), F32),
                   jax.ShapeDtypeStruct((1, width), F32)],
        scratch_shapes=[pltpu.VMEM((tm + HALO, width), F32),
                        pltpu.VMEM((FFN_CONV * SUBLANES, width), F32), pltpu.VMEM((SUBLANES, width), F32)],
        compiler_params=_params("arbitrary"), name="ffn_bwd")(up, up, up, dact, dact, conv_w, conv_b)


def _my_position():
    return lax.axis_index("x"), lax.axis_index("y"), lax.axis_index("c")


COPIES = N_DEV - 1


def _all_gather(arrays):
    n = len(arrays)

    def body(*refs):
        x_refs, out_refs = refs[:n], refs[n:2 * n]
        send_sems, recv_sems, local_sems = refs[2 * n:]
        x, y, cc = _my_position()
        me, sibling = (x, y, cc), (x, y, 1 - cc)
        chips = [(1 - x, y), (x, 1 - y), (1 - x, 1 - y)]

        def block(a, px, py, pc):
            return out_refs[a].at[4 * px + 2 * py + pc]

        def copy(a, k, blk, to, src=None):
            return pltpu.make_async_remote_copy(
                src_ref=block(a, *blk) if src is None else src, dst_ref=block(a, *blk),
                send_sem=send_sems.at[a * COPIES + k], recv_sem=recv_sems.at[a * COPIES + k],
                device_id=to, device_id_type=MESH_ID)

        mine = [pltpu.make_async_copy(x_refs[a], block(a, *me), local_sems.at[a]) for a in range(n)]
        for cp in mine:
            cp.start()
        first = []
        for a in range(n):
            first.append(copy(a, 0, me, sibling, src=x_refs[a]))
            first += [copy(a, 1 + j, me, (*chip, cc), src=x_refs[a]) for j, chip in enumerate(chips)]
        for cp in first:
            cp.start()
        passed = []
        for j, chip in enumerate(chips):
            for a in range(n):
                copy(a, 1 + j, (*chip, cc), me).wait_recv()
                passed.append(copy(a, 4 + j, (*chip, cc), sibling))
                passed[-1].start()
        for a in range(n):
            copy(a, 0, sibling, me).wait_recv()
        for j, chip in enumerate(chips):
            for a in range(n):
                copy(a, 4 + j, (*chip, 1 - cc), me).wait_recv()
        for cp in first + passed:
            cp.wait_send()
        for cp in mine:
            cp.wait()

    any_spec = pl.BlockSpec(memory_space=pl.ANY)
    return _pcall(
        body, out_shape=[jax.ShapeDtypeStruct((N_DEV,) + a.shape, a.dtype) for a in arrays],
        in_specs=[any_spec] * n, out_specs=[any_spec] * n,
        scratch_shapes=[pltpu.SemaphoreType.DMA((n * COPIES,)), pltpu.SemaphoreType.DMA((n * COPIES,)),
                        pltpu.SemaphoreType.DMA((n,))],
        name="all_gather")(*arrays)


def _all_to_all(sends):
    n = len(sends)

    def body(*refs):
        send_refs, recv_refs = refs[:n], refs[n:2 * n]
        send_sems, recv_sems, local_sems = refs[2 * n:]
        x, y, cc = _my_position()
        me = 4 * x + 2 * y + cc
        mine = [pltpu.make_async_copy(send_refs[a].at[me], recv_refs[a].at[me], local_sems.at[a]) for a in range(n)]
        for cp in mine:
            cp.start()
        copies = []
        for rel in range(1, N_DEV):
            px, py, pc = x ^ (rel >> 2), y ^ ((rel >> 1) & 1), cc ^ (rel & 1)
            for a in range(n):
                copies.append(pltpu.make_async_remote_copy(
                    src_ref=send_refs[a].at[4 * px + 2 * py + pc], dst_ref=recv_refs[a].at[me],
                    send_sem=send_sems.at[a * COPIES + rel - 1], recv_sem=recv_sems.at[a * COPIES + rel - 1],
                    device_id=(px, py, pc), device_id_type=MESH_ID))
        for cp in copies:
            cp.start()
        for cp in copies:
            cp.wait()
        for cp in mine:
            cp.wait()

    any_spec = pl.BlockSpec(memory_space=pl.ANY)
    return _pcall(
        body, out_shape=[jax.ShapeDtypeStruct(s.shape, s.dtype) for s in sends],
        in_specs=[any_spec] * n, out_specs=[any_spec] * n,
        scratch_shapes=[pltpu.SemaphoreType.DMA((n * COPIES,)), pltpu.SemaphoreType.DMA((n * COPIES,)),
                        pltpu.SemaphoreType.DMA((n,))],
        name="all_to_all")(*sends)


def _hbm(a):
    return pltpu.with_memory_space_constraint(a, pltpu.HBM)


def _split_copies(send_refs, land_refs, send_sems, recv_sems, local_sems, gather):
    x, y, cc = _my_position()
    me = 4 * x + 2 * y + cc
    local, remote = [], []
    for a, (send, land) in enumerate(zip(send_refs, land_refs)):
        local.append(pltpu.make_async_copy(send if gather else send.at[me], land.at[me], local_sems.at[a]))
    for a, (send, land) in enumerate(zip(send_refs, land_refs)):
        for rel in range(1, N_DEV):
            px, py, pc = x ^ (rel >> 2), y ^ ((rel >> 1) & 1), cc ^ (rel & 1)
            remote.append(pltpu.make_async_remote_copy(
                src_ref=send if gather else send.at[4 * px + 2 * py + pc], dst_ref=land.at[me],
                send_sem=send_sems.at[a * COPIES + rel - 1], recv_sem=recv_sems.at[a * COPIES + rel - 1],
                device_id=(px, py, pc), device_id_type=MESH_ID))
    return local, remote


SPLIT_EFFECT = pltpu.SideEffectType.DATAFLOW_SIDE_EFFECTING


def _exchange_start(sends, after, gather, name):
    n = len(sends)
    lands = [_hbm(lax.empty((N_DEV,) + s.shape if gather else s.shape, s.dtype)) for s in sends]

    def body(*refs):
        send_refs, land_refs = refs[:n], refs[n:2 * n]
        send_sems, recv_sems, local_sems = refs[2 * n + 1:2 * n + 4]
        token = refs[-1]
        local, remote = _split_copies(send_refs, land_refs, send_sems, recv_sems, local_sems, gather)
        for cp in local + remote:
            cp.start()
        token[...] = jnp.zeros_like(token)

    hbm, sem = pl.BlockSpec(memory_space=pltpu.HBM), pl.BlockSpec(memory_space=pltpu.SEMAPHORE)
    out = _pcall(
        body, name=name,
        out_shape=[pltpu.SemaphoreType.DMA((n * COPIES,)), pltpu.SemaphoreType.DMA((n * COPIES,)), pltpu.SemaphoreType.DMA((n,))]
        + [pltpu.HBM(s.shape, s.dtype) for s in sends] + [pltpu.HBM(z.shape, z.dtype) for z in lands]
        + [jax.ShapeDtypeStruct((SUBLANES, LANES), F32)],
        in_specs=[hbm] * (2 * n) + [pl.BlockSpec(memory_space=pl.ANY)],
        out_specs=[sem] * 3 + [hbm] * (2 * n) + [pl.BlockSpec(memory_space=pltpu.VMEM)],
        input_output_aliases={i: 3 + i for i in range(2 * n)},
        compiler_params=pltpu.CompilerParams(has_side_effects=SPLIT_EFFECT),
    )(*[_hbm(s) for s in sends], *lands, after)
    return dict(sems=out[:3], sends=out[3:3 + n], lands=out[3 + n:3 + 2 * n], gather=gather), out[-1]


def _exchange_wait(handle, after, name):
    sends, lands, gather = handle["sends"], handle["lands"], handle["gather"]
    n = len(sends)

    def body(*refs):
        send_refs, land_refs = refs[:n], refs[n:2 * n]
        send_sems, recv_sems, local_sems = refs[2 * n:2 * n + 3]
        local, remote = _split_copies(send_refs, land_refs, send_sems, recv_sems, local_sems, gather)
        for cp in remote:
            cp.wait_send()
            cp.wait_recv()
        for cp in local:
            cp.wait()

    hbm, sem = pl.BlockSpec(memory_space=pltpu.HBM), pl.BlockSpec(memory_space=pltpu.SEMAPHORE)
    out = _pcall(
        body, name=name,
        out_shape=[pltpu.HBM(s.shape, s.dtype) for s in sends] + [pltpu.HBM(z.shape, z.dtype) for z in lands],
        in_specs=[hbm] * (2 * n) + [sem] * 3 + [pl.BlockSpec(memory_space=pl.ANY)],
        out_specs=[hbm] * (2 * n), input_output_aliases={i: i for i in range(2 * n)},
        compiler_params=pltpu.CompilerParams(has_side_effects=SPLIT_EFFECT),
    )(*sends, *lands, *handle["sems"], after)
    return out[n:]


def _sum_and_adamw(recv, w, m, v, name):
    _, r, wp = recv.shape
    c = w.shape[-1]
    lead = w.ndim == 3
    tr = max([d for d in range(2 * SUBLANES, 257, 2 * SUBLANES) if r % d == 0], default=r)
    bc1 = 1.0 - ADAM_B1 ** ADAM_STEP
    bc2 = 1.0 - ADAM_B2 ** ADAM_STEP

    def body(recv_ref, w_ref, m_ref, v_ref, g_ref, d_ref, nm_ref, nv_ref):
        g = recv_ref[0, :, 0:c].astype(F32)
        for s in range(1, N_DEV):
            g = g + recv_ref[s, :, 0:c].astype(F32)
        m_new = ADAM_B1 * m_ref[...] + (1.0 - ADAM_B1) * g
        v_new = ADAM_B2 * v_ref[...] + (1.0 - ADAM_B2) * (g * g)
        m_hat = m_new / bc1
        v_hat = v_new / bc2
        g_ref[...] = g
        d_ref[...] = -ADAM_LR * (m_hat / (jnp.sqrt(v_hat) + ADAM_EPS) + ADAM_WD * w_ref[...])
        nm_ref[...] = m_new
        nv_ref[...] = v_new

    tile = pl.BlockSpec((None, tr, c), lambda i: (0, i, 0)) if lead else pl.BlockSpec((tr, c), lambda i: (i, 0))
    return _pcall(
        body, grid=(r // tr,),
        in_specs=[pl.BlockSpec((N_DEV, tr, wp), lambda i: (0, i, 0)), tile, tile, tile],
        out_specs=[tile] * 4, out_shape=[jax.ShapeDtypeStruct(w.shape, F32)] * 4,
        compiler_params=_params("parallel"), name=name)(recv, w, m, v)


SHARDED_TAPS = ("dn_conv_w", "ffn_conv_w")
REPLICATED = ("attn_norm_g", "dn_a_log", "dn_dt_bias", "dn_out_norm_g", "sg_norm_g", "sg_w", "sg_b", "ffn_norm_g",
              "ffn_conv_b", "final_norm_g")
SMALL = SHARDED_TAPS + REPLICATED
WEIGHT_ORDER = ("attn_norm_g", "w_in", "dn_conv_w", "dn_a_log", "dn_dt_bias", "dn_out_norm_g", "sg_norm_g", "sg_w", "sg_b",
                "w_out", "ffn_norm_g", "w_up", "ffn_conv_w", "ffn_conv_b", "w_down", "final_norm_g")
SLAB_COLS = 1024


def _pad_to(flat, multiple):
    pad = (-flat.shape[-1]) % multiple
    if pad == 0:
        return flat
    return jnp.pad(flat, [(0, 0)] * (flat.ndim - 1) + [(0, pad)])


def _pack_small(named):
    flat = jnp.concatenate([named[n].reshape(-1) for n in SMALL])
    return _pad_to(flat, SUBLANES * SLAB_COLS).reshape(-1, SLAB_COLS)


def _unpack_small(slab, like):
    flat = slab.reshape(-1)
    out, off = {}, 0
    for n in SMALL:
        size = like[n].size
        out[n] = flat[off:off + size].reshape(like[n].shape)
        off += size
    return out


def _split_columns(full, n_local):
    r = full.shape[0]
    return full.reshape(r, N_DEV, n_local).transpose(1, 0, 2).reshape(N_DEV, r * n_local)


def _join_columns(blocks, r, n_local):
    return blocks.reshape(N_DEV, r, n_local).transpose(1, 0, 2).reshape(r, N_DEV * n_local)


def _lanes4(a):
    return jnp.pad(a.reshape(1, N_HEADS), ((0, 0), (0, LANES - N_HEADS)))


def kernel(x, attn_norm_g, w_in, dn_conv_w, dn_a_log, dn_dt_bias, dn_out_norm_g, sg_norm_g, sg_w, sg_b, w_out, ffn_norm_g, w_up, ffn_conv_w, ffn_conv_b, w_down, final_norm_g, loss_target, m_attn_norm_g, m_w_in, m_dn_conv_w, m_dn_a_log, m_dn_dt_bias, m_dn_out_norm_g, m_sg_norm_g, m_sg_w, m_sg_b, m_w_out, m_ffn_norm_g, m_w_up, m_ffn_conv_w, m_ffn_conv_b, m_w_down, m_final_norm_g, v_attn_norm_g, v_w_in, v_dn_conv_w, v_dn_a_log, v_dn_dt_bias, v_dn_out_norm_g, v_sg_norm_g, v_sg_w, v_sg_b, v_w_out, v_ffn_norm_g, v_w_up, v_ffn_conv_w, v_ffn_conv_b, v_w_down, v_final_norm_g):
    weights = dict(attn_norm_g=attn_norm_g, w_in=w_in, dn_conv_w=dn_conv_w, dn_a_log=dn_a_log, dn_dt_bias=dn_dt_bias,
                   dn_out_norm_g=dn_out_norm_g, sg_norm_g=sg_norm_g, sg_w=sg_w, sg_b=sg_b, w_out=w_out, ffn_norm_g=ffn_norm_g,
                   w_up=w_up, ffn_conv_w=ffn_conv_w, ffn_conv_b=ffn_conv_b, w_down=w_down, final_norm_g=final_norm_g)
    m_in = dict(attn_norm_g=m_attn_norm_g, w_in=m_w_in, dn_conv_w=m_dn_conv_w, dn_a_log=m_dn_a_log, dn_dt_bias=m_dn_dt_bias,
                dn_out_norm_g=m_dn_out_norm_g, sg_norm_g=m_sg_norm_g, sg_w=m_sg_w, sg_b=m_sg_b, w_out=m_w_out,
                ffn_norm_g=m_ffn_norm_g, w_up=m_w_up, ffn_conv_w=m_ffn_conv_w, ffn_conv_b=m_ffn_conv_b, w_down=m_w_down,
                final_norm_g=m_final_norm_g)
    v_in = dict(attn_norm_g=v_attn_norm_g, w_in=v_w_in, dn_conv_w=v_dn_conv_w, dn_a_log=v_dn_a_log, dn_dt_bias=v_dn_dt_bias,
                dn_out_norm_g=v_dn_out_norm_g, sg_norm_g=v_sg_norm_g, sg_w=v_sg_w, sg_b=v_sg_b, w_out=v_w_out,
                ffn_norm_g=v_ffn_norm_g, w_up=v_w_up, ffn_conv_w=v_ffn_conv_w, ffn_conv_b=v_ffn_conv_b, w_down=v_w_down,
                final_norm_g=v_final_norm_g)

    n_in, n_up = w_in.shape[2], w_up.shape[2]
    r_out, r_down = w_out.shape[1], w_down.shape[1]
    n_dnc, n_ffc = dn_conv_w.shape[2], ffn_conv_w.shape[2]
    transposed = lambda a: jnp.transpose(a, (0, 2, 1))
    taps = _pad_to(jnp.concatenate([dn_conv_w.reshape(-1), ffn_conv_w.reshape(-1)]), SUBLANES * LANES).reshape(-1, LANES)
    g_in, g_taps = _all_gather([transposed(w_in)[0].astype(BF16), taps])
    gather_out, token = _exchange_start([w_out[0].astype(BF16)], g_taps, True, "gather_w_out")
    gather_up, token = _exchange_start([transposed(w_up)[0].astype(BF16)], token, True, "gather_w_up")
    gather_down, token = _exchange_start([w_down[0].astype(BF16)], token, True, "gather_w_down")
    w_in_t = jnp.pad(g_in.reshape(N_DEV * n_in, D_MODEL), ((0, PROJ_PAD - N_DEV * n_in), (0, 0)))
    taps_all = g_taps.reshape(N_DEV, -1)
    dn_conv_full = _join_columns(taps_all[:, :CONV_K * n_dnc], CONV_K, n_dnc)
    ffn_conv_full = _join_columns(taps_all[:, CONV_K * n_dnc:CONV_K * n_dnc + FFN_CONV * n_ffc], FFN_CONV, n_ffc)
    late = dict(
        w_out=lambda after: _exchange_wait(gather_out, after, "gather_w_out_wait")[0].reshape(N_DEV * r_out, D_MODEL),
        w_up_t=lambda after: _exchange_wait(gather_up, after, "gather_w_up_wait")[0].reshape(N_DEV * n_up, D_MODEL),
        w_down=lambda after: _exchange_wait(gather_down, after, "gather_w_down_wait")[0].reshape(N_DEV * r_down, D_MODEL))

    def send_early(blocks, after, name):
        return _exchange_start(blocks, after, False, name)

    def send_small(g, loss_lanes, after):
        small = jnp.concatenate([g[n].reshape(-1) for n in REPLICATED] + [loss_lanes[0, 0:1]])
        slab = jnp.concatenate([_split_columns(g["dn_conv_w"], n_dnc), _split_columns(g["ffn_conv_w"], n_ffc),
                                jnp.broadcast_to(small[None, :], (N_DEV, small.shape[0]))], axis=1)
        return send_early([_pad_to(slab, SUBLANES * SLAB_COLS).reshape(N_DEV, -1, SLAB_COLS)], after, "send_small")

    upd = {}

    def update_early(sent_down, sent_up_out, sent_small, after):
        r_dn, = _exchange_wait(sent_down, after, "send_dw_down_wait")
        r_up, r_o = _exchange_wait(sent_up_out, after, "send_dw_up_out_wait")
        r_small, = _exchange_wait(sent_small, after, "send_small_wait")
        upd["w_down"] = _sum_and_adamw(r_dn, w_down, m_w_down, v_w_down, "adamw_w_down")
        upd["w_up"] = [transposed(o) for o in _sum_and_adamw(r_up, transposed(w_up), transposed(m_w_up), transposed(v_w_up),
                                                             "adamw_w_up")]
        upd["w_out"] = _sum_and_adamw(r_o, w_out, m_w_out, v_w_out, "adamw_w_out")
        upd["small"] = _sum_and_adamw(r_small, _pack_small(weights), _pack_small(m_in), _pack_small(v_in), "adamw_small")

    grad_x, d_g1, sent_in = _local_step(
        x[0], loss_target[0], w_in_t, late, send_early, send_small, update_early, dn_conv_full, ffn_conv_full,
        attn_norm_g + token[0:1, 0:1], dn_a_log, dn_dt_bias, dn_out_norm_g, sg_norm_g, sg_w, sg_b, ffn_norm_g, ffn_conv_b,
        final_norm_g, n_in)

    norm_rows = D_MODEL // LANES
    r_g1, = _all_to_all([jnp.broadcast_to(d_g1.reshape(1, norm_rows, LANES), (N_DEV, norm_rows, LANES))])
    r_in, = _exchange_wait(sent_in, r_g1, "send_dw_in_wait")
    upd["w_in"] = [transposed(o) for o in _sum_and_adamw(r_in, transposed(w_in), transposed(m_w_in), transposed(v_w_in),
                                                         "adamw_w_in")]
    small_upd = upd.pop("small")
    as_rows = lambda a: a.reshape(norm_rows, LANES)
    norm_upd = _sum_and_adamw(r_g1, as_rows(attn_norm_g), as_rows(m_attn_norm_g), as_rows(v_attn_norm_g), "adamw_attn_norm")
    results = []
    for i in range(4):
        named = _unpack_small(small_upd[i], weights)
        named.update({n: upd[n][i] for n in upd})
        named["attn_norm_g"] = norm_upd[i].reshape(attn_norm_g.shape)
        results.append(named)

    loss = small_upd[0].reshape(-1)[sum(weights[n].size for n in SMALL)]
    return (loss, grad_x[None], *[r[n] for r in results for n in WEIGHT_ORDER])


def _local_step(x2d, tgt, w_in_t, late, send_early, send_small, update_early, dn_conv_full, ffn_conv_full, attn_norm_g,
                dn_a_log, dn_dt_bias, dn_out_norm_g, sg_norm_g, sg_w, sg_b, ffn_norm_g, ffn_conv_b, final_norm_g, n_in):
    t = x2d.shape[0]
    g1, g2, gf = attn_norm_g, ffn_norm_g, final_norm_g.reshape(1, D_MODEL)
    a_log4, dt_bias4 = _lanes4(dn_a_log), _lanes4(dn_dt_bias)
    sg_w3 = sg_w[0]
    sg_b_t = sg_b[0].T
    conv_b = ffn_conv_b

    p, h1, rstd1 = _rmsnorm_matmul(x2d, g1, w_in_t, "norm_in_proj", 512)
    q, k, v, beta4, g4 = _dn_prep(p, dn_conv_full, a_log4, dt_bias4)
    mix_half, *dn_saved = _dn_forward(q, k, v, beta4, g4, p, dn_out_norm_g)
    mix = _sg_forward(p, sg_norm_g, sg_w3, sg_b_t, mix_half)
    w_out_full = late["w_out"](mix)
    x2 = _matmul(mix, w_out_full, "nn", "out_proj", (1024, 1024, 1024), add=x2d)
    w_up_t = late["w_up_t"](x2)
    up, act, h2, rstd2 = _norm_up_ffn(x2, g2, w_up_t, ffn_conv_full, conv_b)
    w_down_full = late["w_down"](act)
    fn, outs = _final_loss_rows(t, D_MODEL)
    loss_lanes, dx3, dx3b, d_gf = _matmul_rows(act, w_down_full, "nn", "down_proj_loss", 512,
                                               [(x2, "rows"), (tgt, "rows"), (gf, "whole")], outs, fn)

    dact = _matmul(dx3b, w_down_full, "nt", "down_proj_dx", (512, D_FF, D_MODEL))
    d_w_down = _matmul(act, dx3b, "tn", "down_proj_dw", (256, 1024, t), out_dtype=BF16)
    sent_down, token = send_early([d_w_down.reshape(N_DEV, D_FF // N_DEV, D_MODEL)], d_w_down, "send_dw_down")
    dup, d_ffn_conv, d_ffn_conv_b = _ffn_bwd(up, ffn_conv_full, conv_b + token[0:1, 0:1], dact)
    fn, outs = _rmsnorm_bwd_rows(t, D_MODEL)
    dx2, dx2b, d_g2 = _matmul_rows(dup, w_up_t, "nn", "up_proj_dx_norm", 256,
                                   [(x2, "rows"), (rstd2, "rows"), (g2, "whole"), (dx3, "rows")], outs, fn)
    d_w_up_t = _matmul(dup, h2, "tn", "up_proj_dw", (512, 1024, t), out_dtype=BF16)
    dmix = _matmul(dx2b, w_out_full, "nt", "out_proj_dx", (1024, 1024, 1024))
    d_w_out = _matmul(mix, dx2b, "tn", "out_proj_dw", (512, 1024, t), out_dtype=BF16)
    sent_up_out, token = send_early(
        [d_w_up_t.reshape(N_DEV, 2 * D_FF // N_DEV, D_MODEL), d_w_out.reshape(N_DEV, D_MODEL // N_DEV, D_MODEL)],
        d_w_out, "send_dw_up_out")
    dp, d_sg_norm, d_sg_w, d_sg_b_t = _sg_backward(p, sg_norm_g + token[0:1, 0:1], sg_w3, sg_b_t, dmix)
    dq, dk, dv, dbeta4, dg4, dp, d_dn_norm = _dn_backward(q, k, v, beta4, g4, p, dn_out_norm_g, dn_saved, dmix, dp)
    dc_dn, d_dn_conv, dp, d_a_log4, d_dt_bias4 = _dn_prep_bwd(p, dn_conv_full, a_log4, dt_bias4, dq, dk, dv, dbeta4, dg4, dp)
    small_grads = dict(
        attn_norm_g=jnp.zeros_like(attn_norm_g), dn_conv_w=d_dn_conv, dn_a_log=d_a_log4[:, :N_HEADS],
        dn_dt_bias=d_dt_bias4[:, :N_HEADS], dn_out_norm_g=d_dn_norm, sg_norm_g=d_sg_norm, sg_w=d_sg_w,
        sg_b=d_sg_b_t[:, :SG_GROUPS].T, ffn_norm_g=d_g2, ffn_conv_w=d_ffn_conv, ffn_conv_b=d_ffn_conv_b, final_norm_g=d_gf)
    sent_small, token = send_small(small_grads, loss_lanes, d_dn_conv)
    dp = _conv_bwd_input(dc_dn, dn_conv_full + token[0:1, 0:1], "dn_conv_dx", out_cols=PROJ_PAD, into=dp)
    d_w_in_t = _matmul(dp, h1, "tn", "in_proj_dw", (PROJ_PAD // 5, 1024, t), out_dtype=BF16)
    sent_in, token = send_early([d_w_in_t[:N_DEV * n_in].reshape(N_DEV, n_in, D_MODEL)], d_w_in_t, "send_dw_in")
    update_early(sent_down, sent_up_out, sent_small, token)
    fn, outs = _rmsnorm_bwd_rows(t, D_MODEL)
    grad_x, _, d_g1 = _matmul_rows(dp, w_in_t, "nn", "in_proj_dx_norm", 512,
                                   [(x2d, "rows"), (rstd1, "rows"), (g1 + token[0:1, 0:1], "whole"), (dx2, "rows")], outs, fn)

    return grad_x, d_g1, sent_in
```

```python
import math

import jax
import jax.numpy as jnp
from jax import lax
from jax.experimental import pallas as pl
from jax.experimental.pallas import tpu as pltpu

F32 = jnp.float32
BF16 = jnp.bfloat16
HI = lax.Precision.HIGHEST

D_MODEL = 1024
DN_WIDTH = 512
HEAD_DIM = 128
N_HEADS = 4
SG_WIDTH = 512
SG_GROUPS = 4
SG_DIM = 128
SG_BLOCK = 128
D_FF = 2816
CHUNK = 64
CONV_K = 4
FFN_CONV = 3
EPS = 1e-6
PROJ_MAIN = 3072
PROJ_PAD = 3200
GELU_C = math.sqrt(2.0 / math.pi)
N_DEV = 8
LANES = 128
SUBLANES = 8
HALO = SUBLANES
VMEM_LIMIT = 48 * 1024 * 1024

ADAM_LR = 0.001
ADAM_B1 = 0.9
ADAM_B2 = 0.999
ADAM_EPS = 1e-08
ADAM_WD = 0.01
ADAM_STEP = 10

MESH_ID = pl.DeviceIdType.MESH


def _pcall(body, **kw):
    return pl.pallas_call(body, **kw)


def _params(*sem):
    return pltpu.CompilerParams(dimension_semantics=sem, vmem_limit_bytes=VMEM_LIMIT)


def _pick(n, cap):
    best = None
    for t in range(LANES, cap + 1, LANES):
        if n % t == 0:
            best = t
    return best if best else n


FAST, EXACT = "bf16 operands, one pass", "f32 operands, six bf16 passes"


def dot_f32(a, b, dims, tier):
    if tier == FAST:
        return lax.dot_general(a.astype(BF16), b.astype(BF16), dims, preferred_element_type=F32)
    return lax.dot_general(a, b, dims, precision=HI, preferred_element_type=F32)


def dot_nn(a, b, tier=EXACT):
    return dot_f32(a, b, (((1,), (0,)), ((), ())), tier)


def dot_nt(a, b, tier=EXACT):
    return dot_f32(a, b, (((1,), (1,)), ((), ())), tier)


def dot_tn(a, b, tier=EXACT):
    return dot_f32(a, b, (((0,), (0,)), ((), ())), tier)


def sigmoid(x):
    return 0.5 * jnp.tanh(0.5 * x) + 0.5


def silu(x):
    return x * sigmoid(x)


def silu_grad(x):
    s = sigmoid(x)
    return s * (1.0 + x * (1.0 - s))


def gelu(x):
    return 0.5 * x * (1.0 + jnp.tanh(GELU_C * (x + 0.044715 * x * x * x)))


def gelu_grad(x):
    t = jnp.tanh(GELU_C * (x + 0.044715 * x * x * x))
    return 0.5 * (1.0 + t) + 0.5 * x * (1.0 - t * t) * GELU_C * (1.0 + 3.0 * 0.044715 * x * x)


def softplus(z):
    return jnp.maximum(z, 0.0) + jnp.log(1.0 + jnp.exp(-jnp.abs(z)))


def rms_fwd(x, g):
    r = lax.rsqrt(jnp.mean(x * x, axis=-1, keepdims=True) + EPS)
    return x * r * g, r


def rms_bwd(x, r, g, dy):
    dyg = dy * g
    xr = x * r
    dx = r * (dyg - xr * jnp.mean(dyg * xr, axis=-1, keepdims=True))
    return dx, dy * xr


def l2_fwd(x):
    r = lax.rsqrt(jnp.sum(x * x, axis=-1, keepdims=True) + EPS)
    return x * r, r


def l2_bwd(x, r, dy):
    xr = x * r
    return r * (dy - xr * jnp.sum(dy * xr, axis=-1, keepdims=True))


def _tri_masks(n):
    row = lax.broadcasted_iota(jnp.int32, (n, n), 0)
    col = lax.broadcasted_iota(jnp.int32, (n, n), 1)
    return row >= col, row > col


def chunk_cumsum(g4):
    incl, _ = _tri_masks(g4.shape[0])
    return dot_nn(incl.astype(F32), g4)


STACK = N_HEADS * CHUNK
DN_FWD_CHUNKS = 8
DN_CHUNKS = 4


def _head_rows(h):
    return slice(h * CHUNK, (h + 1) * CHUNK)


def _stack_heads(x):
    return jnp.concatenate([x[:, h * HEAD_DIM:(h + 1) * HEAD_DIM] for h in range(N_HEADS)], axis=0)


def _stack_lanes(x4):
    return jnp.concatenate([x4[:, h:h + 1] for h in range(N_HEADS)], axis=0)


def _per_head(fn):
    return jnp.concatenate([fn(h) for h in range(N_HEADS)], axis=0)


def _unit_lower_inverses(l_strict, order):
    c = l_strict[0].shape[0]
    row = lax.broadcasted_iota(jnp.int32, (c, c), 0)
    col = lax.broadcasted_iota(jnp.int32, (c, c), 1)
    eye = (row == col).astype(F32)
    p = [-l for l in l_strict]
    a = [eye + n for n in p]
    for _ in range(int(math.log2(order)) - 1):
        p = [dot_nn(x, x, FAST) for x in p]
        a = [x + dot_nn(x, y, FAST) for x, y in zip(a, p)]
    return a


def dn_chunks_local(chunks, inverses=None):
    row = lax.broadcasted_iota(jnp.int32, (STACK, STACK), 0)
    col = lax.broadcasted_iota(jnp.int32, (STACK, STACK), 1)
    same = (row // CHUNK) == (col // CHUNK)
    incl = jnp.logical_and(same, row >= col)
    strict = jnp.logical_and(same, row > col)
    locs = []
    for q, k, v, beta, gc4 in chunks:
        gc_col = _stack_lanes(gc4)
        gc_row = jnp.sum(jnp.where(row == col, gc_col, 0.0), axis=0, keepdims=True)
        decay = jnp.where(incl, jnp.exp(jnp.minimum(gc_col - gc_row, 0.0)), 0.0)
        gamma = jnp.exp(gc_col)
        gc_last = jnp.concatenate([jnp.broadcast_to(gc4[CHUNK - 1:CHUNK, h:h + 1], (CHUNK, 1)) for h in range(N_HEADS)], axis=0)
        tau = jnp.exp(gc_last - gc_col)
        kb = k * beta
        locs.append(dict(decay=decay, gamma=gamma, tau=tau, cd=jnp.exp(gc_last), kb=kb, qd=q * gamma, kt=k * tau,
                         incl=incl, strict=strict))
    for loc, (q, k, v, beta, gc4) in zip(locs, chunks):
        loc["l_mat"] = jnp.where(strict, dot_nt(loc["kb"], k, FAST) * loc["decay"], 0.0)
    if inverses is None:
        inverses = _unit_lower_inverses([loc["l_mat"] for loc in locs], CHUNK)
    for loc, a_inv in zip(locs, inverses):
        loc["a_inv"] = a_inv
    for loc, (q, k, v, beta, gc4) in zip(locs, chunks):
        sol = dot_nn(loc["a_inv"], jnp.concatenate([v * beta, loc["kb"] * loc["gamma"]], axis=1), FAST)
        loc.update(sol=sol, value=sol[:, :HEAD_DIM], kcd=sol[:, HEAD_DIM:])
        loc["attn"] = jnp.where(incl, dot_nt(q, k, FAST) * loc["decay"], 0.0)
    return locs


def dn_chunk_state(loc, s):
    kcd, qd, kt, cd = loc["kcd"], loc["qd"], loc["kt"], loc["cd"]
    v_new = loc["value"] - _per_head(lambda h: dot_nn(kcd[_head_rows(h)], s[h], FAST))
    o = _per_head(lambda h: dot_nn(qd[_head_rows(h)], s[h], FAST)) + dot_nn(loc["attn"], v_new, FAST)
    s_new = [s[h] * cd[h * CHUNK:h * CHUNK + 1, :] + dot_tn(kt[_head_rows(h)], v_new[_head_rows(h)], FAST)
             for h in range(N_HEADS)]
    loc["v_new"] = v_new
    return o, s_new


def dn_chunks_bwd(items, ds_last):
    hr = _head_rows
    n = len(items)
    pre = []
    for q, k, v, beta, loc, s, do in items:
        pre.append(dict(
            dv_part=dot_tn(loc["attn"], do, FAST),
            dattn=jnp.where(loc["incl"], dot_nt(do, loc["v_new"], FAST), 0.0),
            dqd=_per_head(lambda h: dot_nt(do[hr(h)], s[h], FAST)),
            ds_part=[dot_tn(loc["qd"][hr(h)], do[hr(h)], FAST) for h in range(N_HEADS)]))
    ds_new_of, dv_new_of = [None] * n, [None] * n
    ds = ds_last
    for c in reversed(range(n)):
        loc = items[c][4]
        ds_new_of[c] = ds
        dv_new = pre[c]["dv_part"] + _per_head(lambda h: dot_nn(loc["kt"][hr(h)], ds[h], FAST))
        dv_new_of[c] = dv_new
        ds = [pre[c]["ds_part"][h] + ds[h] * loc["cd"][h * CHUNK:h * CHUNK + 1, :]
              - dot_tn(loc["kcd"][hr(h)], dv_new[hr(h)], FAST) for h in range(N_HEADS)]
    is_last = (lax.broadcasted_iota(jnp.int32, (STACK, 1), 0) % CHUNK) == CHUNK - 1
    out = []
    for c, (q, k, v, beta, loc, s, do) in enumerate(items):
        decay, gamma, tau, cd, kb = loc["decay"], loc["gamma"], loc["tau"], loc["cd"], loc["kb"]
        dv_new, ds_new, dattn, dqd = dv_new_of[c], ds_new_of[c], pre[c]["dattn"], pre[c]["dqd"]
        dkt = _per_head(lambda h: dot_nt(loc["v_new"][hr(h)], ds_new[h], FAST))
        dkcd = -_per_head(lambda h: dot_nt(dv_new[hr(h)], s[h], FAST))
        drhs = dot_tn(loc["a_inv"], jnp.concatenate([dv_new, dkcd], axis=1), FAST)
        dvb, dkbg = drhs[:, :HEAD_DIM], drhs[:, HEAD_DIM:]
        dl = jnp.where(loc["strict"], -dot_nt(drhs, loc["sol"], FAST), 0.0)
        dkk = dl * decay
        dqk = dattn * decay
        e = dl * loc["l_mat"] + dattn * loc["attn"]
        dgc = jnp.sum(e, axis=1, keepdims=True) - jnp.sum(e, axis=0, keepdims=True).T
        dkb = dot_nn(dkk, k, FAST) + dkbg * gamma
        dk = dot_tn(dkk, kb, FAST) + dot_tn(dqk, q, FAST) + dkt * tau
        dq = dot_nn(dqk, k, FAST) + dqd * gamma
        dgamma = jnp.sum(dkbg * kb, axis=1, keepdims=True) + jnp.sum(dqd * q, axis=1, keepdims=True)
        dtau_tau = jnp.sum(dkt * k, axis=1, keepdims=True) * tau
        dgc = dgc + dgamma * gamma - dtau_tau

        def last_term(h):
            dcd = jnp.sum(jnp.sum(ds_new[h] * s[h], axis=1, keepdims=True), axis=0, keepdims=True)
            total = jnp.sum(dtau_tau[hr(h)], axis=0, keepdims=True) + dcd * cd[h * CHUNK:h * CHUNK + 1, :]
            return jnp.broadcast_to(total, (CHUNK, 1))

        dgc = dgc + jnp.where(is_last, _per_head(last_term), 0.0)
        dk = dk + dkb * beta
        dbeta = jnp.sum(dkb * k, axis=1, keepdims=True) + jnp.sum(dvb * v, axis=1, keepdims=True)
        out.append((dq, dk, dvb * beta, dbeta, dgc))
    return out, ds


def _token_tile(t):
    return _pick(t, 256)


STRIP = 32


def _for_strips(n_rows, rows, fn, start=0):
    def step(r, carry):
        fn(pl.multiple_of(r * rows, rows))
        return carry

    lax.fori_loop(start, n_rows // rows, step, 0)


def _fold_rows(x):
    out = x[0:SUBLANES, :]
    for i in range(1, x.shape[0] // SUBLANES):
        out = out + x[i * SUBLANES:(i + 1) * SUBLANES, :]
    return out


def _matmul(a, b, mode, name, tiles, add=None, out_dtype=F32):
    if mode == "nn":
        (m, k), n = a.shape, b.shape[1]
    elif mode == "nt":
        (m, k), n = a.shape, b.shape[0]
    else:
        (k, m), n = a.shape, b.shape[1]
    tm, tn, tk = min(tiles[0], m), min(tiles[1], n), min(tiles[2], k)
    assert m % tm == 0 and n % tn == 0 and k % tk == 0, (name, m, n, k, tiles)
    nk = k // tk
    dims = {"nn": (((1,), (0,)), ((), ())), "nt": (((1,), (1,)), ((), ())), "tn": (((0,), (0,)), ((), ()))}[mode]

    def finish(res, add_ref, o_ref):
        if add_ref is not None:
            res = res + add_ref[...]
        o_ref[...] = res.astype(o_ref.dtype)

    def body(*refs):
        a_ref, b_ref = refs[0], refs[1]
        add_ref = refs[2] if add is not None else None
        o_ref = refs[3] if add is not None else refs[2]
        part = lax.dot_general(a_ref[...], b_ref[...], dims, preferred_element_type=F32)
        if nk == 1:
            finish(part, add_ref, o_ref)
            return
        acc_ref = refs[-1]
        kk = pl.program_id(2)

        @pl.when(kk == 0)
        def _():
            acc_ref[...] = part

        @pl.when(kk > 0)
        def _():
            acc_ref[...] += part

        @pl.when(kk == nk - 1)
        def _():
            finish(acc_ref[...], add_ref, o_ref)

    a_spec = pl.BlockSpec((tk, tm), lambda j, i, kk: (kk, i)) if mode == "tn" else pl.BlockSpec((tm, tk), lambda j, i, kk: (i, kk))
    b_spec = pl.BlockSpec((tn, tk), lambda j, i, kk: (j, kk)) if mode == "nt" else pl.BlockSpec((tk, tn), lambda j, i, kk: (kk, j))
    o_spec = pl.BlockSpec((tm, tn), lambda j, i, kk: (i, j))
    in_specs = [a_spec, b_spec] + ([o_spec] if add is not None else [])
    args = (a, b) + ((add,) if add is not None else ())
    return _pcall(
        body, grid=(n // tn, m // tm, nk), in_specs=in_specs, out_specs=o_spec,
        out_shape=jax.ShapeDtypeStruct((m, n), out_dtype),
        scratch_shapes=[pltpu.VMEM((tm, tn), F32)] if nk > 1 else [],
        compiler_params=_params("parallel", "parallel", "arbitrary"), name=name)(*args)


def _matmul_rows(a, b, mode, name, tm, extra, outs, fn, b_tail=None):
    m, k = a.shape
    n = b.shape[1] if mode == "nn" else b.shape[0]
    tm = min(tm, m)
    dims = (((1,), (0,)), ((), ())) if mode == "nn" else (((1,), (1,)), ((), ()))
    main = b.shape[0] if b_tail is None else b.shape[0] // LANES * LANES
    tails = [] if b_tail is None else [b_tail]

    def spec(shape, kind):
        if kind == "rows":
            return pl.BlockSpec((tm, shape[1]), lambda i: (i, 0))
        return pl.BlockSpec(shape, lambda i: (0,) * len(shape))

    def body(a_ref, b_ref, *refs):
        if b_tail is None:
            rows = lax.dot_general(a_ref[...], b_ref[...], dims, preferred_element_type=F32)
        else:
            tail_ref, *refs = refs
            rows = (lax.dot_general(a_ref[:, :main], b_ref[...], dims, preferred_element_type=F32)
                    + lax.dot_general(a_ref[:, main:], tail_ref[...], dims, preferred_element_type=F32))
        fn(rows, pl.program_id(0) == 0, *refs)

    return _pcall(
        body, grid=(m // tm,),
        in_specs=[pl.BlockSpec((tm, k), lambda i: (i, 0)), pl.BlockSpec((main, b.shape[1]), lambda i: (0, 0))]
        + [pl.BlockSpec(x.shape, lambda i: (0, 0)) for x in tails] + [spec(x.shape, kind) for x, kind in extra],
        out_specs=[spec(shape, kind) for shape, _, kind in outs],
        out_shape=[jax.ShapeDtypeStruct(shape, dtype) for shape, dtype, _ in outs],
        compiler_params=_params("arbitrary"), name=name)(a, b, *tails, *[x for x, _ in extra])


def _rmsnorm_matmul(x, g, b_t, b_tail, name, tm):
    t, d = x.shape
    main = b_t.shape[0] // LANES * LANES
    n = main + b_tail.shape[0]
    tm = min(tm, t)
    nt = (((1,), (1,)), ((), ()))

    def body(x_ref, g_ref, b_ref, tail_ref, o_ref, h_ref, r_ref):
        y, r = rms_fwd(x_ref[...], g_ref[...])
        h = y.astype(BF16)
        h_ref[...] = h
        r_ref[...] = r
        o_ref[:, :main] = lax.dot_general(h, b_ref[...], nt, preferred_element_type=F32)
        o_ref[:, main:] = lax.dot_general(h, tail_ref[...], nt, preferred_element_type=F32)

    rows = lambda w: pl.BlockSpec((tm, w), lambda i: (i, 0))
    return _pcall(
        body, grid=(t // tm,),
        in_specs=[rows(d), pl.BlockSpec((1, d), lambda i: (0, 0)), pl.BlockSpec((main, d), lambda i: (0, 0)),
                  pl.BlockSpec(b_tail.shape, lambda i: (0, 0))],
        out_specs=[rows(n), rows(d), rows(1)],
        out_shape=[jax.ShapeDtypeStruct((t, n), F32), jax.ShapeDtypeStruct((t, d), BF16), jax.ShapeDtypeStruct((t, 1), F32)],
        compiler_params=_params("parallel"), name=name)(x, g, b_t, b_tail)


def _rmsnorm_bwd_rows(t, d):
    def fn(dh, first, x_ref, r_ref, g_ref, dres_ref, dx_ref, dxb_ref, dg_ref):
        dx, dg_rows = rms_bwd(x_ref[...], r_ref[...], g_ref[...], dh)
        dx = dx + dres_ref[...]
        dx_ref[...] = dx
        dxb_ref[...] = dx.astype(BF16)

        @pl.when(first)
        def _():
            dg_ref[...] = jnp.zeros_like(dg_ref)

        dg_ref[...] += jnp.sum(dg_rows, axis=0, keepdims=True)

    return fn, [((t, d), F32, "rows"), ((t, d), BF16, "rows"), ((1, d), F32, "whole")]


def _final_loss_rows(t, d):
    def fn(rows, first, res_ref, t_ref, g_ref, loss_ref, dx_ref, dxb_ref, dg_ref):
        @pl.when(first)
        def _():
            loss_ref[...] = jnp.zeros_like(loss_ref)
            dg_ref[...] = jnp.zeros_like(dg_ref)

        x = rows + res_ref[...]
        y, r = rms_fwd(x, g_ref[...])
        err = y - t_ref[...]
        loss_ref[...] += 0.5 * jnp.sum(jnp.mean(err * err, axis=-1, keepdims=True), axis=0, keepdims=True)
        dx, dg_rows = rms_bwd(x, r, g_ref[...], err * (1.0 / d))
        dx_ref[...] = dx
        dxb_ref[...] = dx.astype(BF16)
        dg_ref[...] += jnp.sum(dg_rows, axis=0, keepdims=True)

    return fn, [((1, LANES), F32, "whole"), ((t, d), F32, "rows"), ((t, d), BF16, "rows"), ((1, d), F32, "whole")]


def _prev_halo_spec(tm, width, col_block):
    return pl.BlockSpec((HALO, width), lambda i: (jnp.maximum(i * (tm // HALO) - 1, 0), col_block))


def _history(tile_ref, halo_ref, first, row0, cols):
    if isinstance(row0, int) and row0 == 0:
        return jnp.concatenate([jnp.where(first, 0.0, halo_ref[:, cols]), tile_ref[0:STRIP, cols]], axis=0)
    return tile_ref[pl.ds(pl.multiple_of(row0 - HALO, HALO), STRIP + HALO), cols]


def _first_then_strips(n_rows, fn):
    fn(0)
    _for_strips(n_rows, STRIP, fn, start=1)


def _delays(ext, taps):
    return [ext[HALO:, :]] + [pltpu.roll(ext, j, 0)[HALO:, :] for j in range(1, taps)]


def _causal_conv(delayed, w):
    taps = len(delayed)
    out = delayed[0] * w[taps - 1:taps, :]
    for j in range(1, taps):
        out = out + delayed[j] * w[taps - 1 - j:taps - j, :]
    return out


def _advanced_conv(buf_ref, row0, cols, w):
    return _advanced(buf_ref[pl.ds(row0, STRIP + HALO), cols], w)


def _advanced(ext, w):
    taps = w.shape[0]
    out = ext[:STRIP, :] * w[taps - 1:taps, :]
    for j in range(1, taps):
        out = out + pltpu.roll(ext, STRIP + HALO - j, 0)[:STRIP, :] * w[taps - 1 - j:taps - j, :]
    return out


def _dn_prep(p, conv_w, a_log4, dt_bias4):
    t = p.shape[0]
    tm = _token_tile(t)
    w3 = 3 * DN_WIDTH

    def body(x_ref, halo_ref, pbd_ref, w_ref, alog_ref, dtb_ref, q_ref, k_ref, v_ref, beta_ref, g_ref):
        first = pl.program_id(0) == 0

        def strip(row0):
            rows = pl.ds(row0, STRIP)
            for h in range(N_HEADS):
                sl = slice(h * HEAD_DIM, (h + 1) * HEAD_DIM)
                for part, out_ref in ((0, q_ref), (1, k_ref), (2, v_ref)):
                    cols = slice(part * DN_WIDTH + h * HEAD_DIM, part * DN_WIDTH + (h + 1) * HEAD_DIM)
                    y = silu(_causal_conv(_delays(_history(x_ref, halo_ref, first, row0, cols), CONV_K), w_ref[:, cols]))
                    if part == 0:
                        y = l2_fwd(y)[0] * (HEAD_DIM ** -0.5)
                    elif part == 1:
                        y = l2_fwd(y)[0]
                    out_ref[rows, sl] = y
            head = lax.broadcasted_iota(jnp.int32, (STRIP, LANES), 1) < N_HEADS
            pbd = pbd_ref[rows, :]
            beta_ref[rows, :] = jnp.where(head, sigmoid(pbd), 0.0)
            a_raw = pltpu.roll(pbd, LANES - N_HEADS, 1)
            g_ref[rows, :] = jnp.where(head, -jnp.exp(alog_ref[...]) * softplus(a_raw + dtb_ref[...]), 0.0)

        _first_then_strips(tm, strip)

    tok = lambda w, cb: pl.BlockSpec((tm, w), lambda i: (i, cb))
    full = lambda a: pl.BlockSpec(a.shape, lambda i: (0, 0))
    return _pcall(
        body, grid=(t // tm,),
        in_specs=[tok(w3, 0), _prev_halo_spec(tm, w3, 0), tok(LANES, PROJ_MAIN // LANES),
                  full(conv_w), full(a_log4), full(dt_bias4)],
        out_specs=[tok(DN_WIDTH, 0)] * 3 + [tok(LANES, 0)] * 2,
        out_shape=[jax.ShapeDtypeStruct((t, DN_WIDTH), F32)] * 3 + [jax.ShapeDtypeStruct((t, LANES), F32)] * 2,
        compiler_params=_params("parallel"), name="dn_prep")(p, p, p, conv_w, a_log4, dt_bias4)


def _dn_prep_bwd(p, conv_w, a_log4, dt_bias4, dq, dk, dv, dbeta4, dg4, dp_buf):
    t = p.shape[0]
    tm = _token_tile(t)
    w3 = 3 * DN_WIDTH

    def body(x_ref, halo_ref, pbd_ref, w_ref, alog_ref, dtb_ref, dq_ref, dk_ref, dv_ref, dbeta_ref, dg_ref, _,
             dc_ref, dw_ref, dpbd_ref, dalog_ref, ddtb_ref, dw_acc, lane_acc):
        first = pl.program_id(0) == 0
        dw_acc[...] = jnp.zeros_like(dw_acc)
        lane_acc[...] = jnp.zeros_like(lane_acc)

        def strip(row0):
            rows = pl.ds(row0, STRIP)
            for h in range(N_HEADS):
                sl = slice(h * HEAD_DIM, (h + 1) * HEAD_DIM)
                for part, dy_ref in ((0, dq_ref), (1, dk_ref), (2, dv_ref)):
                    cols = slice(part * DN_WIDTH + h * HEAD_DIM, part * DN_WIDTH + (h + 1) * HEAD_DIM)
                    delayed = _delays(_history(x_ref, halo_ref, first, row0, cols), CONV_K)
                    c = _causal_conv(delayed, w_ref[:, cols])
                    dy = dy_ref[rows, sl]
                    if part < 2:
                        y = silu(c)
                        _, r = l2_fwd(y)
                        dy = l2_bwd(y, r, dy * (HEAD_DIM ** -0.5) if part == 0 else dy)
                    dc = dy * silu_grad(c)
                    dc_ref[rows, cols] = dc
                    for j in range(CONV_K):
                        k = CONV_K - 1 - j
                        dw_acc[k * SUBLANES:(k + 1) * SUBLANES, cols] += _fold_rows(dc * delayed[j])
            head = lax.broadcasted_iota(jnp.int32, (STRIP, LANES), 1) < N_HEADS
            pbd = pbd_ref[rows, :]
            beta = sigmoid(pbd)
            dpb = jnp.where(head, dbeta_ref[rows, :] * beta * (1.0 - beta), 0.0)
            z = pltpu.roll(pbd, LANES - N_HEADS, 1) + dtb_ref[...]
            neg_rate = -jnp.exp(alog_ref[...])
            dg = dg_ref[rows, :]
            dpa = jnp.where(head, dg * neg_rate * sigmoid(z), 0.0)
            dpbd_ref[rows, :] = (dpb + pltpu.roll(dpa, N_HEADS, 1)).astype(BF16)
            g = jnp.where(head, neg_rate * softplus(z), 0.0)
            lane_acc[0:SUBLANES, :] += _fold_rows(dg * g)
            lane_acc[SUBLANES:, :] += _fold_rows(dpa)

        _first_then_strips(tm, strip)

        @pl.when(first)
        def _():
            dw_ref[...] = jnp.zeros_like(dw_ref)
            dalog_ref[...] = jnp.zeros_like(dalog_ref)
            ddtb_ref[...] = jnp.zeros_like(ddtb_ref)

        for k in range(CONV_K):
            dw_ref[k:k + 1, :] += jnp.sum(dw_acc[k * SUBLANES:(k + 1) * SUBLANES, :], axis=0, keepdims=True)
        dalog_ref[...] += jnp.sum(lane_acc[0:SUBLANES, :], axis=0, keepdims=True)
        ddtb_ref[...] += jnp.sum(lane_acc[SUBLANES:, :], axis=0, keepdims=True)

    tok = lambda w, cb: pl.BlockSpec((tm, w), lambda i: (i, cb))
    full = lambda shape: pl.BlockSpec(shape, lambda i: (0, 0))
    return _pcall(
        body, grid=(t // tm,),
        in_specs=[tok(w3, 0), _prev_halo_spec(tm, w3, 0), tok(LANES, PROJ_MAIN // LANES),
                  full(conv_w.shape), full(a_log4.shape), full(dt_bias4.shape)] + [tok(DN_WIDTH, 0)] * 3 + [tok(LANES, 0)] * 2
        + [pl.BlockSpec(memory_space=pl.ANY)],
        out_specs=[tok(w3, 0), full((CONV_K, w3)), tok(LANES, PROJ_MAIN // LANES), full((1, LANES)), full((1, LANES))],
        out_shape=[jax.ShapeDtypeStruct((t, w3), F32), jax.ShapeDtypeStruct((CONV_K, w3), F32),
                   jax.ShapeDtypeStruct(dp_buf.shape, dp_buf.dtype),
                   jax.ShapeDtypeStruct((1, LANES), F32), jax.ShapeDtypeStruct((1, LANES), F32)],
        input_output_aliases={11: 2},
        scratch_shapes=[pltpu.VMEM((CONV_K * SUBLANES, w3), F32), pltpu.VMEM((2 * SUBLANES, LANES), F32)],
        compiler_params=_params("arbitrary"), name="dn_prep_bwd")(p, p, p, conv_w, a_log4, dt_bias4, dq, dk, dv, dbeta4, dg4, dp_buf)


def _conv_bwd_input(dc, w, name, out_cols=None, col_block=0, into=None):
    t, c = dc.shape
    taps = w.shape[0]
    tm = _token_tile(t)
    ct = _pick(c, 1536)
    n_tok = t // tm
    out_cols = c if out_cols is None else out_cols

    def body(dc_ref, next_ref, w_ref, *rest):
        dx_ref = rest[-1]
        last = pl.program_id(0) == n_tok - 1

        def strip(row0):
            for c0 in range(0, ct, LANES):
                cols = slice(c0, c0 + LANES)
                dx_ref[pl.ds(row0, STRIP), cols] = _advanced_conv(dc_ref, row0, cols, w_ref[:, cols]).astype(BF16)

        _for_strips(tm - STRIP, STRIP, strip)
        for c0 in range(0, ct, LANES):
            cols = slice(c0, c0 + LANES)
            ext = jnp.concatenate([dc_ref[tm - STRIP:tm, cols], jnp.where(last, 0.0, next_ref[:, cols])], axis=0)
            dx_ref[tm - STRIP:tm, cols] = _advanced(ext, w_ref[:, cols]).astype(BF16)

    in_specs = [pl.BlockSpec((tm, ct), lambda i, j: (i, j)),
                pl.BlockSpec((HALO, ct), lambda i, j: (jnp.minimum((i + 1) * (tm // HALO), t // HALO - 1), j)),
                pl.BlockSpec((taps, ct), lambda i, j: (0, j))]
    args = (dc, dc, w)
    aliases = {}
    if into is not None:
        in_specs.append(pl.BlockSpec(memory_space=pl.ANY))
        args += (into,)
        aliases = {3: 0}
    return _pcall(
        body, grid=(n_tok, c // ct), in_specs=in_specs,
        out_specs=pl.BlockSpec((tm, ct), lambda i, j: (i, j + col_block)),
        out_shape=jax.ShapeDtypeStruct((t, out_cols), BF16), input_output_aliases=aliases,
        compiler_params=_params("parallel", "parallel"), name=name)(*args)


def _dn_forward(q, k, v, beta4, g4, p, norm_g):
    t = q.shape[0]
    n = t // CHUNK
    nc = DN_FWD_CHUNKS
    rows_per_step = nc * CHUNK

    def body(q_ref, k_ref, v_ref, b_ref, g_ref, gate_ref, ng_ref, mix_ref, s_all_ref, ainv_ref, s_ref):
        @pl.when(pl.program_id(0) == 0)
        def _():
            s_ref[...] = jnp.zeros_like(s_ref)

        chunks = []
        for c in range(nc):
            rows = slice(c * CHUNK, (c + 1) * CHUNK)
            chunks.append((_stack_heads(q_ref[rows, :]), _stack_heads(k_ref[rows, :]), _stack_heads(v_ref[rows, :]),
                           _stack_lanes(b_ref[rows, :]), chunk_cumsum(g_ref[rows, :])))
        locs = dn_chunks_local(chunks)
        s = [s_ref[h] for h in range(N_HEADS)]
        for c in range(nc):
            rows = slice(c * CHUNK, (c + 1) * CHUNK)
            ainv_ref[c] = locs[c]["a_inv"].astype(BF16)
            for h in range(N_HEADS):
                s_all_ref[c, h] = s[h]
            o, s = dn_chunk_state(locs[c], s)
            o_n, _ = rms_fwd(o, ng_ref[...])
            for h in range(N_HEADS):
                sl = slice(h * HEAD_DIM, (h + 1) * HEAD_DIM)
                mix_ref[rows, sl] = (o_n[_head_rows(h)] * silu(gate_ref[rows, sl])).astype(BF16)
        for h in range(N_HEADS):
            s_ref[h] = s[h]

    ch = lambda w, cb: pl.BlockSpec((rows_per_step, w), lambda i: (i, cb))
    per_chunk = lambda *shape: pl.BlockSpec((nc,) + shape, lambda i: (i,) + (0,) * len(shape))
    return _pcall(
        body, grid=(n // nc,),
        in_specs=[ch(DN_WIDTH, 0)] * 3 + [ch(LANES, 0)] * 2 + [ch(DN_WIDTH, 3), pl.BlockSpec((1, HEAD_DIM), lambda i: (0, 0))],
        out_specs=[ch(DN_WIDTH, 0), per_chunk(N_HEADS, HEAD_DIM, HEAD_DIM), per_chunk(STACK, STACK)],
        out_shape=[jax.ShapeDtypeStruct((t, DN_WIDTH + SG_WIDTH), BF16), jax.ShapeDtypeStruct((n, N_HEADS, HEAD_DIM, HEAD_DIM), F32),
                   jax.ShapeDtypeStruct((n, STACK, STACK), BF16)],
        scratch_shapes=[pltpu.VMEM((N_HEADS, HEAD_DIM, HEAD_DIM), F32)],
        compiler_params=_params("arbitrary"), name="dn_forward")(q, k, v, beta4, g4, p, norm_g)


def _dn_backward(q, k, v, beta4, g4, p, norm_g, saved, dmix, dp_buf):
    t = q.shape[0]
    n = t // CHUNK
    steps = n // DN_CHUNKS
    rows_per_step = DN_CHUNKS * CHUNK

    def body(q_ref, k_ref, v_ref, b_ref, g_ref, gate_ref, ng_ref, s_in_ref, ainv_ref, dmix_ref, _,
             dq_ref, dk_ref, dv_ref, db_ref, dg_ref, dgate_ref, dng_ref, ds_ref):
        @pl.when(pl.program_id(0) == 0)
        def _():
            ds_ref[...] = jnp.zeros_like(ds_ref)
            dng_ref[...] = jnp.zeros_like(dng_ref)

        chunks = []
        for c in range(DN_CHUNKS):
            rows = slice(c * CHUNK, (c + 1) * CHUNK)
            chunks.append((_stack_heads(q_ref[rows, :]), _stack_heads(k_ref[rows, :]), _stack_heads(v_ref[rows, :]),
                           _stack_lanes(b_ref[rows, :]), chunk_cumsum(g_ref[rows, :])))
        items = []
        for c, loc in enumerate(dn_chunks_local(chunks, [ainv_ref[c] for c in range(DN_CHUNKS)])):
            rows = slice(c * CHUNK, (c + 1) * CHUNK)
            s = [s_in_ref[c, h] for h in range(N_HEADS)]
            o, _ = dn_chunk_state(loc, s)
            o_n, r = rms_fwd(o, ng_ref[...])
            gate = _stack_heads(gate_ref[rows, :])
            dmx = _stack_heads(dmix_ref[rows, :])
            dgate = dmx * o_n * silu_grad(gate)
            do, dng_rows = rms_bwd(o, r, ng_ref[...], dmx * silu(gate))
            dng_ref[...] += jnp.sum(dng_rows, axis=0, keepdims=True)
            for h in range(N_HEADS):
                dgate_ref[rows, h * HEAD_DIM:(h + 1) * HEAD_DIM] = dgate[_head_rows(h)].astype(BF16)
            items.append((*chunks[c][:4], loc, s, do))
        grads, ds = dn_chunks_bwd(items, [ds_ref[h] for h in range(N_HEADS)])
        lane = lax.broadcasted_iota(jnp.int32, (CHUNK, LANES), 1)
        _, strict = _tri_masks(CHUNK)
        for c in range(DN_CHUNKS):
            rows = slice(c * CHUNK, (c + 1) * CHUNK)
            dq, dk, dv, dbeta, dgc = grads[c]
            db4 = jnp.zeros((CHUNK, LANES), F32)
            dgc4 = jnp.zeros((CHUNK, LANES), F32)
            for h in range(N_HEADS):
                sl = slice(h * HEAD_DIM, (h + 1) * HEAD_DIM)
                head_rows = _head_rows(h)
                dq_ref[rows, sl] = dq[head_rows]
                dk_ref[rows, sl] = dk[head_rows]
                dv_ref[rows, sl] = dv[head_rows]
                db4 = jnp.where(lane == h, dbeta[head_rows], db4)
                dgc4 = jnp.where(lane == h, dgc[head_rows], dgc4)
            db_ref[rows, :] = db4
            dg_ref[rows, :] = dot_nn(jnp.logical_not(strict).astype(F32), dgc4)
        for h in range(N_HEADS):
            ds_ref[h] = ds[h]

    rev = lambda w, cb: pl.BlockSpec((rows_per_step, w), lambda i: (steps - 1 - i, cb))
    per_chunk = lambda a: pl.BlockSpec((DN_CHUNKS,) + a.shape[1:], lambda i: (steps - 1 - i,) + (0,) * (a.ndim - 1))
    return _pcall(
        body, grid=(steps,),
        in_specs=[rev(DN_WIDTH, 0)] * 3 + [rev(LANES, 0)] * 2 + [rev(DN_WIDTH, 3), pl.BlockSpec((1, HEAD_DIM), lambda i: (0, 0))]
        + [per_chunk(a) for a in saved] + [rev(DN_WIDTH, 0), pl.BlockSpec(memory_space=pl.ANY)],
        out_specs=[rev(DN_WIDTH, 0)] * 3 + [rev(LANES, 0)] * 2 + [rev(DN_WIDTH, 3), pl.BlockSpec((1, HEAD_DIM), lambda i: (0, 0))],
        out_shape=[jax.ShapeDtypeStruct((t, DN_WIDTH), F32)] * 3 + [jax.ShapeDtypeStruct((t, LANES), F32)] * 2
        + [jax.ShapeDtypeStruct(dp_buf.shape, dp_buf.dtype), jax.ShapeDtypeStruct((1, HEAD_DIM), F32)],
        input_output_aliases={10: 5},
        scratch_shapes=[pltpu.VMEM((N_HEADS, HEAD_DIM, HEAD_DIM), F32)],
        compiler_params=_params("arbitrary"), name="dn_backward")(q, k, v, beta4, g4, p, norm_g, *saved, dmix, dp_buf)


SG_STEP_BLOCKS = 4


def _sg_mask():
    row = lax.broadcasted_iota(jnp.int32, (SG_BLOCK, SG_BLOCK), 0)
    col = lax.broadcasted_iota(jnp.int32, (SG_BLOCK, SG_BLOCK), 1)
    return (col // CHUNK) <= (row // CHUNK)


def _sg_forward(p, norm_g, w_s, b_t, mix_buf):
    t = p.shape[0]
    step_rows = SG_STEP_BLOCKS * SG_BLOCK

    def body(u_ref, v_ref, ng_ref, w_ref, b_ref, _, o_ref):
        mask = _sg_mask()
        pairs = [(slice(b * SG_BLOCK, (b + 1) * SG_BLOCK), g, slice(g * SG_DIM, (g + 1) * SG_DIM))
                 for b in range(SG_STEP_BLOCKS) for g in range(SG_GROUPS)]
        w_m = [jnp.where(mask, w_ref[g], 0.0) for g in range(SG_GROUPS)]
        vn = [rms_fwd(gelu(v_ref[rows, sl]), ng_ref[:, sl])[0] for rows, g, sl in pairs]
        s = [dot_nn(w_m[g], vn[i], FAST) + b_ref[:, g:g + 1] for i, (rows, g, sl) in enumerate(pairs)]
        for i, (rows, g, sl) in enumerate(pairs):
            o_ref[rows, sl] = (gelu(u_ref[rows, sl]) * s[i]).astype(BF16)

    blk = lambda cb: pl.BlockSpec((step_rows, SG_WIDTH), lambda i: (i, cb))
    return _pcall(
        body, grid=(t // step_rows,),
        in_specs=[blk(4), blk(5), pl.BlockSpec((1, SG_WIDTH), lambda i: (0, 0)),
                  pl.BlockSpec((SG_GROUPS, SG_BLOCK, SG_BLOCK), lambda i: (0, 0, 0)), pl.BlockSpec((SG_BLOCK, SG_GROUPS), lambda i: (0, 0)),
                  pl.BlockSpec(memory_space=pl.ANY)],
        out_specs=blk(1), out_shape=jax.ShapeDtypeStruct(mix_buf.shape, mix_buf.dtype), input_output_aliases={5: 0},
        compiler_params=_params("parallel"), name="sg_forward")(p, p, norm_g, w_s, b_t, mix_buf)


def _sg_backward(p, norm_g, w_s, b_t, dmix):
    t = p.shape[0]

    def body(u_ref, v_ref, ng_ref, w_ref, b_ref, do_ref, duv_ref, dng_ref, dw_ref, db_ref):
        @pl.when(pl.program_id(0) == 0)
        def _():
            dng_ref[...] = jnp.zeros_like(dng_ref)
            dw_ref[...] = jnp.zeros_like(dw_ref)
            db_ref[...] = jnp.zeros_like(db_ref)

        mask = _sg_mask()
        lane = lax.broadcasted_iota(jnp.int32, (SG_BLOCK, LANES), 1)
        pairs = [(slice(b * SG_BLOCK, (b + 1) * SG_BLOCK), g, slice(g * SG_DIM, (g + 1) * SG_DIM))
                 for b in range(SG_STEP_BLOCKS) for g in range(SG_GROUPS)]
        w_m = [jnp.where(mask, w_ref[g], 0.0) for g in range(SG_GROUPS)]
        vg = [gelu(v_ref[rows, sl]) for rows, g, sl in pairs]
        normed = [rms_fwd(vg[i], ng_ref[:, sl]) for i, (rows, g, sl) in enumerate(pairs)]
        s = [dot_nn(w_m[g], normed[i][0], FAST) + b_ref[:, g:g + 1] for i, (rows, g, sl) in enumerate(pairs)]
        ds = []
        db = jnp.zeros((SG_BLOCK, LANES), F32)
        for i, (rows, g, sl) in enumerate(pairs):
            u_raw, do = u_ref[rows, sl], do_ref[rows, sl]
            duv_ref[rows, sl] = (do * s[i] * gelu_grad(u_raw)).astype(BF16)
            ds.append(do * gelu(u_raw))
            db = db + jnp.where(lane == g, jnp.sum(ds[i], axis=1, keepdims=True), 0.0)
        dw = [jnp.where(mask, dot_nt(ds[i], normed[i][0], FAST), 0.0) for i in range(len(pairs))]
        dvn = [dot_tn(w_m[g], ds[i], FAST) for i, (rows, g, sl) in enumerate(pairs)]
        for i, (rows, g, sl) in enumerate(pairs):
            dw_ref[g] += dw[i]
            dvg, dng_rows = rms_bwd(vg[i], normed[i][1], ng_ref[:, sl], dvn[i])
            dng_ref[:, sl] += jnp.sum(dng_rows, axis=0, keepdims=True)
            duv_ref[rows, SG_WIDTH + g * SG_DIM:SG_WIDTH + (g + 1) * SG_DIM] = (dvg * gelu_grad(v_ref[rows, sl])).astype(BF16)
        db_ref[...] += db

    step_rows = SG_STEP_BLOCKS * SG_BLOCK
    blk = lambda cb: pl.BlockSpec((step_rows, SG_WIDTH), lambda i: (i, cb))
    const2 = lambda shape: pl.BlockSpec(shape, lambda i: (0, 0))
    w_spec = pl.BlockSpec((SG_GROUPS, SG_BLOCK, SG_BLOCK), lambda i: (0, 0, 0))
    return _pcall(
        body, grid=(t // step_rows,),
        in_specs=[blk(4), blk(5), const2((1, SG_WIDTH)), w_spec, const2((SG_BLOCK, SG_GROUPS)), blk(1)],
        out_specs=[pl.BlockSpec((step_rows, 2 * SG_WIDTH), lambda i: (i, 2)), const2((1, SG_WIDTH)), w_spec,
                   const2((SG_BLOCK, LANES))],
        out_shape=[jax.ShapeDtypeStruct((t, PROJ_PAD), BF16), jax.ShapeDtypeStruct((1, SG_WIDTH), F32),
                   jax.ShapeDtypeStruct((SG_GROUPS, SG_BLOCK, SG_BLOCK), F32), jax.ShapeDtypeStruct((SG_BLOCK, LANES), F32)],
        compiler_params=_params("arbitrary"), name="sg_backward")(p, p, norm_g, w_s, b_t, dmix)


FFN_COLS = 256


def _norm_up_ffn(x, g, w_up_t, conv_w, conv_b):
    t, d = x.shape
    tm = min(t, 256)
    blocks = D_FF // FFN_COLS
    nt = (((1,), (1,)), ((), ()))

    def body(x_ref, g_ref, w_ref, cw_ref, cb_ref, up_ref, act_ref, h_ref, r_ref, tail_ref, prev_ref):
        @pl.when(pl.program_id(0) == 0)
        def _():
            tail_ref[...] = jnp.zeros_like(tail_ref)

        y, r = rms_fwd(x_ref[...], g_ref[...])
        h = y.astype(BF16)
        h_ref[...] = h
        r_ref[...] = r

        def project(blk):
            out = []
            for half in range(2):
                cols = slice(half * D_FF + blk * FFN_COLS, half * D_FF + (blk + 1) * FFN_COLS)
                u = lax.dot_general(h, w_ref[cols, :], nt, preferred_element_type=F32)
                up_ref[:, cols] = u
                prev_ref[:, cols] = tail_ref[:, cols]
                tail_ref[:, cols] = u[tm - HALO:, :]
                out.append(cols)
            return out

        def history(row0, cols):
            if row0 == 0:
                return jnp.concatenate([prev_ref[:, cols], up_ref[0:STRIP, cols]], axis=0)
            return up_ref[row0 - HALO:row0 + STRIP, cols]

        def activate(blk, g_cols, v_cols):
            for row0 in range(0, tm, STRIP):
                for c0 in range(0, FFN_COLS, LANES):
                    gc = slice(g_cols.start + c0, g_cols.start + c0 + LANES)
                    vc = slice(v_cols.start + c0, v_cols.start + c0 + LANES)
                    cg = _causal_conv(_delays(history(row0, gc), FFN_CONV), cw_ref[:, gc]) + cb_ref[:, gc]
                    cv = _causal_conv(_delays(history(row0, vc), FFN_CONV), cw_ref[:, vc]) + cb_ref[:, vc]
                    act_ref[row0:row0 + STRIP, blk * FFN_COLS + c0:blk * FFN_COLS + c0 + LANES] = (silu(cg) * cv).astype(BF16)

        pending = None
        for blk in range(blocks):
            cols = project(blk)
            if pending is not None:
                activate(*pending)
            pending = (blk, *cols)
        activate(*pending)

    rows = lambda w: pl.BlockSpec((tm, w), lambda i: (i, 0))
    whole = lambda a: pl.BlockSpec(a.shape, lambda i: (0, 0))
    return _pcall(
        body, grid=(t // tm,),
        in_specs=[rows(d), whole(g), whole(w_up_t), whole(conv_w), whole(conv_b)],
        out_specs=[rows(2 * D_FF), rows(D_FF), rows(d), rows(1)],
        out_shape=[jax.ShapeDtypeStruct((t, 2 * D_FF), F32), jax.ShapeDtypeStruct((t, D_FF), BF16),
                   jax.ShapeDtypeStruct((t, d), BF16), jax.ShapeDtypeStruct((t, 1), F32)],
        scratch_shapes=[pltpu.VMEM((HALO, 2 * D_FF), F32), pltpu.VMEM((HALO, 2 * D_FF), F32)],
        compiler_params=_params("arbitrary"), name="norm_up_ffn")(x, g, w_up_t, conv_w, conv_b)


def _ffn_bwd(up, conv_w, conv_b, dact):
    t = up.shape[0]
    tm = _pick(t, 256)
    n_tok = t // tm
    width = 2 * D_FF

    def dconv(delayed_g, delayed_v, da, wg, wv, bg, bv):
        cg = _causal_conv(delayed_g, wg) + bg
        cv = _causal_conv(delayed_v, wv) + bv
        s = sigmoid(cg)
        return da * cv * (s * (1.0 + cg * (1.0 - s))), da * (cg * s)

    def body(up_ref, prev_ref, next_ref, da_ref, dan_ref, w_ref, b_ref, dup_ref, dw_ref, db_ref, dc_ref, dw_acc, db_acc):
        first = pl.program_id(0) == 0
        last = pl.program_id(0) == n_tok - 1
        dw_acc[...] = jnp.zeros_like(dw_acc)
        db_acc[...] = jnp.zeros_like(db_acc)

        def strip(row0):
            rows = pl.ds(row0, STRIP)
            for c0 in range(0, D_FF, LANES):
                gc, vc = slice(c0, c0 + LANES), slice(D_FF + c0, D_FF + c0 + LANES)
                del_g = _delays(_history(up_ref, prev_ref, first, row0, gc), FFN_CONV)
                del_v = _delays(_history(up_ref, prev_ref, first, row0, vc), FFN_CONV)
                dcg, dcv = dconv(del_g, del_v, da_ref[rows, gc], w_ref[:, gc], w_ref[:, vc], b_ref[:, gc], b_ref[:, vc])
                dc_ref[rows, gc] = dcg
                dc_ref[rows, vc] = dcv
                db_acc[:, gc] += _fold_rows(dcg)
                db_acc[:, vc] += _fold_rows(dcv)
                for j in range(FFN_CONV):
                    k = FFN_CONV - 1 - j
                    dw_acc[k * SUBLANES:(k + 1) * SUBLANES, gc] += _fold_rows(dcg * del_g[j])
                    dw_acc[k * SUBLANES:(k + 1) * SUBLANES, vc] += _fold_rows(dcv * del_v[j])

        _first_then_strips(tm, strip)

        for c0 in range(0, D_FF, LANES):
            gc, vc = slice(c0, c0 + LANES), slice(D_FF + c0, D_FF + c0 + LANES)

            def delayed(cols):
                return _delays(jnp.concatenate([up_ref[tm - HALO:tm, cols], next_ref[:, cols]], axis=0), FFN_CONV)

            dcg, dcv = dconv(delayed(gc), delayed(vc), dan_ref[:, gc], w_ref[:, gc], w_ref[:, vc], b_ref[:, gc], b_ref[:, vc])
            dc_ref[tm:, gc] = jnp.where(last, 0.0, dcg)
            dc_ref[tm:, vc] = jnp.where(last, 0.0, dcv)

        def strip_dx(row0):
            for c0 in range(0, width, LANES):
                cols = slice(c0, c0 + LANES)
                dup_ref[pl.ds(row0, STRIP), cols] = _advanced_conv(dc_ref, row0, cols, w_ref[:, cols]).astype(BF16)

        _for_strips(tm, STRIP, strip_dx)

        @pl.when(first)
        def _():
            dw_ref[...] = jnp.zeros_like(dw_ref)
            db_ref[...] = jnp.zeros_like(db_ref)

        for k in range(FFN_CONV):
            dw_ref[k:k + 1, :] += jnp.sum(dw_acc[k * SUBLANES:(k + 1) * SUBLANES, :], axis=0, keepdims=True)
        db_ref[...] += jnp.sum(db_acc[...], axis=0, keepdims=True)

    next_rows = lambda i: jnp.minimum((i + 1) * (tm // HALO), t // HALO - 1)
    full = lambda rows: pl.BlockSpec((rows, width), lambda i: (0, 0))
    return _pcall(
        body, grid=(n_tok,),
        in_specs=[pl.BlockSpec((tm, width), lambda i: (i, 0)),
                  pl.BlockSpec((HALO, width), lambda i: (jnp.maximum(i * (tm // HALO) - 1, 0), 0)),
                  pl.BlockSpec((HALO, width), lambda i: (next_rows(i), 0)),
                  pl.BlockSpec((tm, D_FF), lambda i: (i, 0)), pl.BlockSpec((HALO, D_FF), lambda i: (next_rows(i), 0)),
                  full(FFN_CONV), full(1)],
        out_specs=[pl.BlockSpec((tm, width), lambda i: (i, 0)), full(FFN_CONV), full(1)],
        out_shape=[jax.ShapeDtypeStruct((t, width), BF16), jax.ShapeDtypeStruct((FFN_CONV, width), F32),
                   jax.ShapeDtypeStruct((1, width), F32)],
        scratch_shapes=[pltpu.VMEM((tm + HALO, width), F32),
                        pltpu.VMEM((FFN_CONV * SUBLANES, width), F32), pltpu.VMEM((SUBLANES, width), F32)],
        compiler_params=_params("arbitrary"), name="ffn_bwd")(up, up, up, dact, dact, conv_w, conv_b)


def _my_position():
    return lax.axis_index("x"), lax.axis_index("y"), lax.axis_index("c")


COPIES = N_DEV - 1


def _all_gather(arrays):
    n = len(arrays)

    def body(*refs):
        x_refs, out_refs = refs[:n], refs[n:2 * n]
        send_sems, recv_sems, local_sems = refs[2 * n:]
        x, y, cc = _my_position()
        me, sibling = (x, y, cc), (x, y, 1 - cc)
        chips = [(1 - x, y), (x, 1 - y), (1 - x, 1 - y)]

        def block(a, px, py, pc):
            return out_refs[a].at[4 * px + 2 * py + pc]

        def copy(a, k, blk, to, src=None):
            return pltpu.make_async_remote_copy(
                src_ref=block(a, *blk) if src is None else src, dst_ref=block(a, *blk),
                send_sem=send_sems.at[a * COPIES + k], recv_sem=recv_sems.at[a * COPIES + k],
                device_id=to, device_id_type=MESH_ID)

        mine = [pltpu.make_async_copy(x_refs[a], block(a, *me), local_sems.at[a]) for a in range(n)]
        for cp in mine:
            cp.start()
        first = []
        for a in range(n):
            first.append(copy(a, 0, me, sibling, src=x_refs[a]))
            first += [copy(a, 1 + j, me, (*chip, cc), src=x_refs[a]) for j, chip in enumerate(chips)]
        for cp in first:
            cp.start()
        passed = []
        for j, chip in enumerate(chips):
            for a in range(n):
                copy(a, 1 + j, (*chip, cc), me).wait_recv()
                passed.append(copy(a, 4 + j, (*chip, cc), sibling))
                passed[-1].start()
        for a in range(n):
            copy(a, 0, sibling, me).wait_recv()
        for j, chip in enumerate(chips):
            for a in range(n):
                copy(a, 4 + j, (*chip, 1 - cc), me).wait_recv()
        for cp in first + passed:
            cp.wait_send()
        for cp in mine:
            cp.wait()

    any_spec = pl.BlockSpec(memory_space=pl.ANY)
    return _pcall(
        body, out_shape=[jax.ShapeDtypeStruct((N_DEV,) + a.shape, a.dtype) for a in arrays],
        in_specs=[any_spec] * n, out_specs=[any_spec] * n,
        scratch_shapes=[pltpu.SemaphoreType.DMA((n * COPIES,)), pltpu.SemaphoreType.DMA((n * COPIES,)),
                        pltpu.SemaphoreType.DMA((n,))],
        name="all_gather")(*arrays)


def _all_to_all(sends):
    n = len(sends)

    def body(*refs):
        send_refs, recv_refs = refs[:n], refs[n:2 * n]
        send_sems, recv_sems, local_sems = refs[2 * n:]
        x, y, cc = _my_position()
        me = 4 * x + 2 * y + cc
        mine = [pltpu.make_async_copy(send_refs[a].at[me], recv_refs[a].at[me], local_sems.at[a]) for a in range(n)]
        for cp in mine:
            cp.start()
        copies = []
        for rel in range(1, N_DEV):
            px, py, pc = x ^ (rel >> 2), y ^ ((rel >> 1) & 1), cc ^ (rel & 1)
            for a in range(n):
                copies.append(pltpu.make_async_remote_copy(
                    src_ref=send_refs[a].at[4 * px + 2 * py + pc], dst_ref=recv_refs[a].at[me],
                    send_sem=send_sems.at[a * COPIES + rel - 1], recv_sem=recv_sems.at[a * COPIES + rel - 1],
                    device_id=(px, py, pc), device_id_type=MESH_ID))
        for cp in copies:
            cp.start()
        for cp in copies:
            cp.wait()
        for cp in mine:
            cp.wait()

    any_spec = pl.BlockSpec(memory_space=pl.ANY)
    return _pcall(
        body, out_shape=[jax.ShapeDtypeStruct(s.shape, s.dtype) for s in sends],
        in_specs=[any_spec] * n, out_specs=[any_spec] * n,
        scratch_shapes=[pltpu.SemaphoreType.DMA((n * COPIES,)), pltpu.SemaphoreType.DMA((n * COPIES,)),
                        pltpu.SemaphoreType.DMA((n,))],
        name="all_to_all")(*sends)


def _hbm(a):
    return pltpu.with_memory_space_constraint(a, pltpu.HBM)


def _split_copies(send_refs, land_refs, send_sems, recv_sems, local_sems, gather):
    x, y, cc = _my_position()
    me = 4 * x + 2 * y + cc
    local, remote = [], []
    for a, (send, land) in enumerate(zip(send_refs, land_refs)):
        local.append(pltpu.make_async_copy(send if gather else send.at[me], land.at[me], local_sems.at[a]))
    for a, (send, land) in enumerate(zip(send_refs, land_refs)):
        for rel in range(1, N_DEV):
            px, py, pc = x ^ (rel >> 2), y ^ ((rel >> 1) & 1), cc ^ (rel & 1)
            remote.append(pltpu.make_async_remote_copy(
                src_ref=send if gather else send.at[4 * px + 2 * py + pc], dst_ref=land.at[me],
                send_sem=send_sems.at[a * COPIES + rel - 1], recv_sem=recv_sems.at[a * COPIES + rel - 1],
                device_id=(px, py, pc), device_id_type=MESH_ID))
    return local, remote


SPLIT_EFFECT = pltpu.SideEffectType.DATAFLOW_SIDE_EFFECTING


def _exchange_start(sends, after, gather, name):
    n = len(sends)
    lands = [_hbm(lax.empty((N_DEV,) + s.shape if gather else s.shape, s.dtype)) for s in sends]

    def body(*refs):
        send_refs, land_refs = refs[:n], refs[n:2 * n]
        send_sems, recv_sems, local_sems = refs[2 * n + 1:2 * n + 4]
        token = refs[-1]
        local, remote = _split_copies(send_refs, land_refs, send_sems, recv_sems, local_sems, gather)
        for cp in local + remote:
            cp.start()
        token[...] = jnp.zeros_like(token)

    hbm, sem = pl.BlockSpec(memory_space=pltpu.HBM), pl.BlockSpec(memory_space=pltpu.SEMAPHORE)
    out = _pcall(
        body, name=name,
        out_shape=[pltpu.SemaphoreType.DMA((n * COPIES,)), pltpu.SemaphoreType.DMA((n * COPIES,)), pltpu.SemaphoreType.DMA((n,))]
        + [pltpu.HBM(s.shape, s.dtype) for s in sends] + [pltpu.HBM(z.shape, z.dtype) for z in lands]
        + [jax.ShapeDtypeStruct((SUBLANES, LANES), F32)],
        in_specs=[hbm] * (2 * n) + [pl.BlockSpec(memory_space=pl.ANY)],
        out_specs=[sem] * 3 + [hbm] * (2 * n) + [pl.BlockSpec(memory_space=pltpu.VMEM)],
        input_output_aliases={i: 3 + i for i in range(2 * n)},
        compiler_params=pltpu.CompilerParams(has_side_effects=SPLIT_EFFECT),
    )(*[_hbm(s) for s in sends], *lands, after)
    return dict(sems=out[:3], sends=out[3:3 + n], lands=out[3 + n:3 + 2 * n], gather=gather), out[-1]


def _exchange_wait(handle, after, name):
    sends, lands, gather = handle["sends"], handle["lands"], handle["gather"]
    n = len(sends)

    def body(*refs):
        send_refs, land_refs = refs[:n], refs[n:2 * n]
        send_sems, recv_sems, local_sems = refs[2 * n:2 * n + 3]
        local, remote = _split_copies(send_refs, land_refs, send_sems, recv_sems, local_sems, gather)
        for cp in remote:
            cp.wait_send()
            cp.wait_recv()
        for cp in local:
            cp.wait()

    hbm, sem = pl.BlockSpec(memory_space=pltpu.HBM), pl.BlockSpec(memory_space=pltpu.SEMAPHORE)
    out = _pcall(
        body, name=name,
        out_shape=[pltpu.HBM(s.shape, s.dtype) for s in sends] + [pltpu.HBM(z.shape, z.dtype) for z in lands],
        in_specs=[hbm] * (2 * n) + [sem] * 3 + [pl.BlockSpec(memory_space=pl.ANY)],
        out_specs=[hbm] * (2 * n), input_output_aliases={i: i for i in range(2 * n)},
        compiler_params=pltpu.CompilerParams(has_side_effects=SPLIT_EFFECT),
    )(*sends, *lands, *handle["sems"], after)
    return out[n:]


def _sum_and_adamw(recv, w, m, v, name):
    _, r, wp = recv.shape
    c = w.shape[-1]
    lead = w.ndim == 3
    tr = max([d for d in range(2 * SUBLANES, 257, 2 * SUBLANES) if r % d == 0], default=r)
    bc1 = 1.0 - ADAM_B1 ** ADAM_STEP
    bc2 = 1.0 - ADAM_B2 ** ADAM_STEP

    def body(recv_ref, w_ref, m_ref, v_ref, g_ref, d_ref, nm_ref, nv_ref):
        g = recv_ref[0, :, 0:c].astype(F32)
        for s in range(1, N_DEV):
            g = g + recv_ref[s, :, 0:c].astype(F32)
        m_new = ADAM_B1 * m_ref[...] + (1.0 - ADAM_B1) * g
        v_new = ADAM_B2 * v_ref[...] + (1.0 - ADAM_B2) * (g * g)
        m_hat = m_new / bc1
        v_hat = v_new / bc2
        g_ref[...] = g
        d_ref[...] = -ADAM_LR * (m_hat / (jnp.sqrt(v_hat) + ADAM_EPS) + ADAM_WD * w_ref[...])
        nm_ref[...] = m_new
        nv_ref[...] = v_new

    tile = pl.BlockSpec((None, tr, c), lambda i: (0, i, 0)) if lead else pl.BlockSpec((tr, c), lambda i: (i, 0))
    return _pcall(
        body, grid=(r // tr,),
        in_specs=[pl.BlockSpec((N_DEV, tr, wp), lambda i: (0, i, 0)), tile, tile, tile],
        out_specs=[tile] * 4, out_shape=[jax.ShapeDtypeStruct(w.shape, F32)] * 4,
        compiler_params=_params("parallel"), name=name)(recv, w, m, v)


SHARDED_TAPS = ("dn_conv_w", "ffn_conv_w")
REPLICATED = ("attn_norm_g", "dn_a_log", "dn_dt_bias", "dn_out_norm_g", "sg_norm_g", "sg_w", "sg_b", "ffn_norm_g",
              "ffn_conv_b", "final_norm_g")
SMALL = SHARDED_TAPS + REPLICATED
WEIGHT_ORDER = ("attn_norm_g", "w_in", "dn_conv_w", "dn_a_log", "dn_dt_bias", "dn_out_norm_g", "sg_norm_g", "sg_w", "sg_b",
                "w_out", "ffn_norm_g", "w_up", "ffn_conv_w", "ffn_conv_b", "w_down", "final_norm_g")
SLAB_COLS = 1024


def _pad_to(flat, multiple):
    pad = (-flat.shape[-1]) % multiple
    if pad == 0:
        return flat
    return jnp.pad(flat, [(0, 0)] * (flat.ndim - 1) + [(0, pad)])


def _pack_small(named):
    flat = jnp.concatenate([named[n].reshape(-1) for n in SMALL])
    return _pad_to(flat, SUBLANES * SLAB_COLS).reshape(-1, SLAB_COLS)


def _unpack_small(slab, like):
    flat = slab.reshape(-1)
    out, off = {}, 0
    for n in SMALL:
        size = like[n].size
        out[n] = flat[off:off + size].reshape(like[n].shape)
        off += size
    return out


def _split_columns(full, n_local):
    r = full.shape[0]
    return full.reshape(r, N_DEV, n_local).transpose(1, 0, 2).reshape(N_DEV, r * n_local)


def _join_columns(blocks, r, n_local):
    return blocks.reshape(N_DEV, r, n_local).transpose(1, 0, 2).reshape(r, N_DEV * n_local)


def _lanes4(a):
    return jnp.pad(a.reshape(1, N_HEADS), ((0, 0), (0, LANES - N_HEADS)))


def kernel(x, attn_norm_g, w_in, dn_conv_w, dn_a_log, dn_dt_bias, dn_out_norm_g, sg_norm_g, sg_w, sg_b, w_out, ffn_norm_g, w_up, ffn_conv_w, ffn_conv_b, w_down, final_norm_g, loss_target, m_attn_norm_g, m_w_in, m_dn_conv_w, m_dn_a_log, m_dn_dt_bias, m_dn_out_norm_g, m_sg_norm_g, m_sg_w, m_sg_b, m_w_out, m_ffn_norm_g, m_w_up, m_ffn_conv_w, m_ffn_conv_b, m_w_down, m_final_norm_g, v_attn_norm_g, v_w_in, v_dn_conv_w, v_dn_a_log, v_dn_dt_bias, v_dn_out_norm_g, v_sg_norm_g, v_sg_w, v_sg_b, v_w_out, v_ffn_norm_g, v_w_up, v_ffn_conv_w, v_ffn_conv_b, v_w_down, v_final_norm_g):
    weights = dict(attn_norm_g=attn_norm_g, w_in=w_in, dn_conv_w=dn_conv_w, dn_a_log=dn_a_log, dn_dt_bias=dn_dt_bias,
                   dn_out_norm_g=dn_out_norm_g, sg_norm_g=sg_norm_g, sg_w=sg_w, sg_b=sg_b, w_out=w_out, ffn_norm_g=ffn_norm_g,
                   w_up=w_up, ffn_conv_w=ffn_conv_w, ffn_conv_b=ffn_conv_b, w_down=w_down, final_norm_g=final_norm_g)
    m_in = dict(attn_norm_g=m_attn_norm_g, w_in=m_w_in, dn_conv_w=m_dn_conv_w, dn_a_log=m_dn_a_log, dn_dt_bias=m_dn_dt_bias,
                dn_out_norm_g=m_dn_out_norm_g, sg_norm_g=m_sg_norm_g, sg_w=m_sg_w, sg_b=m_sg_b, w_out=m_w_out,
                ffn_norm_g=m_ffn_norm_g, w_up=m_w_up, ffn_conv_w=m_ffn_conv_w, ffn_conv_b=m_ffn_conv_b, w_down=m_w_down,
                final_norm_g=m_final_norm_g)
    v_in = dict(attn_norm_g=v_attn_norm_g, w_in=v_w_in, dn_conv_w=v_dn_conv_w, dn_a_log=v_dn_a_log, dn_dt_bias=v_dn_dt_bias,
                dn_out_norm_g=v_dn_out_norm_g, sg_norm_g=v_sg_norm_g, sg_w=v_sg_w, sg_b=v_sg_b, w_out=v_w_out,
                ffn_norm_g=v_ffn_norm_g, w_up=v_w_up, ffn_conv_w=v_ffn_conv_w, ffn_conv_b=v_ffn_conv_b, w_down=v_w_down,
                final_norm_g=v_final_norm_g)

    n_in, n_up = w_in.shape[2], w_up.shape[2]
    r_out, r_down = w_out.shape[1], w_down.shape[1]
    n_dnc, n_ffc = dn_conv_w.shape[2], ffn_conv_w.shape[2]
    transposed = lambda a: jnp.transpose(a, (0, 2, 1))
    taps = _pad_to(jnp.concatenate([dn_conv_w.reshape(-1), ffn_conv_w.reshape(-1)]), SUBLANES * LANES).reshape(-1, LANES)
    g_in, g_taps = _all_gather([transposed(w_in)[0].astype(BF16), taps])
    gather_out, token = _exchange_start([w_out[0].astype(BF16)], g_taps, True, "gather_w_out")
    gather_up, token = _exchange_start([transposed(w_up)[0].astype(BF16)], token, True, "gather_w_up")
    gather_down, token = _exchange_start([w_down[0].astype(BF16)], token, True, "gather_w_down")
    w_in_rows = g_in.reshape(N_DEV * n_in, D_MODEL)
    w_in_main = N_DEV * n_in // LANES * LANES
    w_in_t = (w_in_rows, jnp.pad(w_in_rows[w_in_main:], ((0, PROJ_PAD - N_DEV * n_in), (0, 0))))
    taps_all = g_taps.reshape(N_DEV, -1)
    dn_conv_full = _join_columns(taps_all[:, :CONV_K * n_dnc], CONV_K, n_dnc)
    ffn_conv_full = _join_columns(taps_all[:, CONV_K * n_dnc:CONV_K * n_dnc + FFN_CONV * n_ffc], FFN_CONV, n_ffc)
    late = dict(
        w_out=lambda after: _exchange_wait(gather_out, after, "gather_w_out_wait")[0].reshape(N_DEV * r_out, D_MODEL),
        w_up_t=lambda after: _exchange_wait(gather_up, after, "gather_w_up_wait")[0].reshape(N_DEV * n_up, D_MODEL),
        w_down=lambda after: _exchange_wait(gather_down, after, "gather_w_down_wait")[0].reshape(N_DEV * r_down, D_MODEL))

    def send_early(blocks, after, name):
        return _exchange_start(blocks, after, False, name)

    def send_small(g, loss_lanes, after):
        small = jnp.concatenate([g[n].reshape(-1) for n in REPLICATED] + [loss_lanes[0, 0:1]])
        slab = jnp.concatenate([_split_columns(g["dn_conv_w"], n_dnc), _split_columns(g["ffn_conv_w"], n_ffc),
                                jnp.broadcast_to(small[None, :], (N_DEV, small.shape[0]))], axis=1)
        return send_early([_pad_to(slab, SUBLANES * SLAB_COLS).reshape(N_DEV, -1, SLAB_COLS)], after, "send_small")

    upd = {}

    def update_early(sent_down, sent_up_out, sent_small, after):
        r_dn, = _exchange_wait(sent_down, after, "send_dw_down_wait")
        r_up, r_o = _exchange_wait(sent_up_out, after, "send_dw_up_out_wait")
        r_small, = _exchange_wait(sent_small, after, "send_small_wait")
        upd["w_down"] = _sum_and_adamw(r_dn, w_down, m_w_down, v_w_down, "adamw_w_down")
        upd["w_up"] = [transposed(o) for o in _sum_and_adamw(r_up, transposed(w_up), transposed(m_w_up), transposed(v_w_up),
                                                             "adamw_w_up")]
        upd["w_out"] = _sum_and_adamw(r_o, w_out, m_w_out, v_w_out, "adamw_w_out")
        upd["small"] = _sum_and_adamw(r_small, _pack_small(weights), _pack_small(m_in), _pack_small(v_in), "adamw_small")

    grad_x, d_g1, sent_in = _local_step(
        x[0], loss_target[0], w_in_t, late, send_early, send_small, update_early, dn_conv_full, ffn_conv_full,
        attn_norm_g + token[0:1, 0:1], dn_a_log, dn_dt_bias, dn_out_norm_g, sg_norm_g, sg_w, sg_b, ffn_norm_g, ffn_conv_b,
        final_norm_g, n_in)

    norm_rows = D_MODEL // LANES
    r_g1, = _all_to_all([jnp.broadcast_to(d_g1.reshape(1, norm_rows, LANES), (N_DEV, norm_rows, LANES))])
    r_in, = _exchange_wait(sent_in, r_g1, "send_dw_in_wait")
    upd["w_in"] = [transposed(o) for o in _sum_and_adamw(r_in, transposed(w_in), transposed(m_w_in), transposed(v_w_in),
                                                         "adamw_w_in")]
    small_upd = upd.pop("small")
    as_rows = lambda a: a.reshape(norm_rows, LANES)
    norm_upd = _sum_and_adamw(r_g1, as_rows(attn_norm_g), as_rows(m_attn_norm_g), as_rows(v_attn_norm_g), "adamw_attn_norm")
    results = []
    for i in range(4):
        named = _unpack_small(small_upd[i], weights)
        named.update({n: upd[n][i] for n in upd})
        named["attn_norm_g"] = norm_upd[i].reshape(attn_norm_g.shape)
        results.append(named)

    loss = small_upd[0].reshape(-1)[sum(weights[n].size for n in SMALL)]
    return (loss, grad_x[None], *[r[n] for r in results for n in WEIGHT_ORDER])


def _local_step(x2d, tgt, w_in_t, late, send_early, send_small, update_early, dn_conv_full, ffn_conv_full, attn_norm_g,
                dn_a_log, dn_dt_bias, dn_out_norm_g, sg_norm_g, sg_w, sg_b, ffn_norm_g, ffn_conv_b, final_norm_g, n_in):
    t = x2d.shape[0]
    g1, g2, gf = attn_norm_g, ffn_norm_g, final_norm_g.reshape(1, D_MODEL)
    a_log4, dt_bias4 = _lanes4(dn_a_log), _lanes4(dn_dt_bias)
    sg_w3 = sg_w[0]
    sg_b_t = sg_b[0].T
    conv_b = ffn_conv_b

    p, h1, rstd1 = _rmsnorm_matmul(x2d, g1, *w_in_t, "norm_in_proj", 512)
    q, k, v, beta4, g4 = _dn_prep(p, dn_conv_full, a_log4, dt_bias4)
    mix_half, *dn_saved = _dn_forward(q, k, v, beta4, g4, p, dn_out_norm_g)
    mix = _sg_forward(p, sg_norm_g, sg_w3, sg_b_t, mix_half)
    w_out_full = late["w_out"](mix)
    x2 = _matmul(mix, w_out_full, "nn", "out_proj", (1024, 1024, 1024), add=x2d)
    w_up_t = late["w_up_t"](x2)
    up, act, h2, rstd2 = _norm_up_ffn(x2, g2, w_up_t, ffn_conv_full, conv_b)
    w_down_full = late["w_down"](act)
    fn, outs = _final_loss_rows(t, D_MODEL)
    loss_lanes, dx3, dx3b, d_gf = _matmul_rows(act, w_down_full, "nn", "down_proj_loss", 512,
                                               [(x2, "rows"), (tgt, "rows"), (gf, "whole")], outs, fn)

    dact = _matmul(dx3b, w_down_full, "nt", "down_proj_dx", (512, D_FF, D_MODEL))
    d_w_down = _matmul(act, dx3b, "tn", "down_proj_dw", (256, 1024, t), out_dtype=BF16)
    sent_down, token = send_early([d_w_down.reshape(N_DEV, D_FF // N_DEV, D_MODEL)], d_w_down, "send_dw_down")
    dup, d_ffn_conv, d_ffn_conv_b = _ffn_bwd(up, ffn_conv_full, conv_b + token[0:1, 0:1], dact)
    fn, outs = _rmsnorm_bwd_rows(t, D_MODEL)
    dx2, dx2b, d_g2 = _matmul_rows(dup, w_up_t, "nn", "up_proj_dx_norm", 256,
                                   [(x2, "rows"), (rstd2, "rows"), (g2, "whole"), (dx3, "rows")], outs, fn)
    d_w_up_t = _matmul(dup, h2, "tn", "up_proj_dw", (512, 1024, t), out_dtype=BF16)
    dmix = _matmul(dx2b, w_out_full, "nt", "out_proj_dx", (1024, 1024, 1024))
    d_w_out = _matmul(mix, dx2b, "tn", "out_proj_dw", (512, 1024, t), out_dtype=BF16)
    sent_up_out, token = send_early(
        [d_w_up_t.reshape(N_DEV, 2 * D_FF // N_DEV, D_MODEL), d_w_out.reshape(N_DEV, D_MODEL // N_DEV, D_MODEL)],
        d_w_out, "send_dw_up_out")
    dp, d_sg_norm, d_sg_w, d_sg_b_t = _sg_backward(p, sg_norm_g + token[0:1, 0:1], sg_w3, sg_b_t, dmix)
    dq, dk, dv, dbeta4, dg4, dp, d_dn_norm = _dn_backward(q, k, v, beta4, g4, p, dn_out_norm_g, dn_saved, dmix, dp)
    dc_dn, d_dn_conv, dp, d_a_log4, d_dt_bias4 = _dn_prep_bwd(p, dn_conv_full, a_log4, dt_bias4, dq, dk, dv, dbeta4, dg4, dp)
    small_grads = dict(
        attn_norm_g=jnp.zeros_like(attn_norm_g), dn_conv_w=d_dn_conv, dn_a_log=d_a_log4[:, :N_HEADS],
        dn_dt_bias=d_dt_bias4[:, :N_HEADS], dn_out_norm_g=d_dn_norm, sg_norm_g=d_sg_norm, sg_w=d_sg_w,
        sg_b=d_sg_b_t[:, :SG_GROUPS].T, ffn_norm_g=d_g2, ffn_conv_w=d_ffn_conv, ffn_conv_b=d_ffn_conv_b, final_norm_g=d_gf)
    sent_small, token = send_small(small_grads, loss_lanes, d_dn_conv)
    dp = _conv_bwd_input(dc_dn, dn_conv_full + token[0:1, 0:1], "dn_conv_dx", out_cols=PROJ_PAD, into=dp)
    d_w_in_t = _matmul(dp, h1, "tn", "in_proj_dw", (PROJ_PAD // 5, 1024, t), out_dtype=BF16)
    sent_in, token = send_early([d_w_in_t[:N_DEV * n_in].reshape(N_DEV, n_in, D_MODEL)], d_w_in_t, "send_dw_in")
    update_early(sent_down, sent_up_out, sent_small, token)
    fn, outs = _rmsnorm_bwd_rows(t, D_MODEL)
    grad_x, _, d_g1 = _matmul_rows(dp, w_in_t[0], "nn", "in_proj_dx_norm", 512,
                                   [(x2d, "rows"), (rstd1, "rows"), (g1 + token[0:1, 0:1], "whole"), (dx2, "rows")], outs, fn,
                                   b_tail=w_in_t[1])

    return grad_x, d_g1, sent_in
```
